```python
import jax, jax.numpy as jnp
from jax import lax
import numpy as np

D_MODEL = 1024
BATCH = 8
SEQ = 2048
DEPTH = 1

HEAD_DIM = 64
N_Q_HEADS = 8
N_KV_HEADS = 2
ATTN_WIDTH = N_Q_HEADS * HEAD_DIM
KV_WIDTH = N_KV_HEADS * HEAD_DIM
CONV_WIDTH = D_MODEL - ATTN_WIDTH
MIX_WIDTH = ATTN_WIDTH + CONV_WIDTH
CONV_KERNEL = 31
WINDOW = 128
BLOCK = 128
ROPE_THETA = 10000.0
EPS = 1e-6
IN_WIDTH = ATTN_WIDTH + 2 * KV_WIDTH + ATTN_WIDTH + 2 * CONV_WIDTH + CONV_WIDTH

kernel_name = "hybrid_swa_sink_conformer_adaln"


def _rms_norm(t, w):
    tf = t.astype(jnp.float32)
    y = tf * lax.rsqrt(jnp.mean(tf * tf, axis=-1, keepdims=True) + EPS)
    return (y * w.astype(jnp.float32)).astype(t.dtype)


def _layer_norm(t, w, b):
    tf = t.astype(jnp.float32)
    mu = jnp.mean(tf, axis=-1, keepdims=True)
    var = jnp.mean(jnp.square(tf - mu), axis=-1, keepdims=True)
    y = (tf - mu) * lax.rsqrt(var + EPS)
    return (y * w.astype(jnp.float32) + b.astype(jnp.float32)).astype(t.dtype)


def _rope_tables(seq_len):
    inv = ROPE_THETA ** (-jnp.arange(0, HEAD_DIM, 2, dtype=jnp.float32) / HEAD_DIM)
    ang = jnp.arange(seq_len, dtype=jnp.float32)[:, None] * inv[None, :]
    return jnp.cos(ang), jnp.sin(ang)


def _apply_rope(t, cos, sin):
    tf = t.astype(jnp.float32)
    t1, t2 = jnp.split(tf, 2, axis=-1)
    c_, s_ = cos[None, :, None, :], sin[None, :, None, :]
    return jnp.concatenate([t1 * c_ - t2 * s_, t2 * c_ + t1 * s_], axis=-1).astype(t.dtype)


def _sliding_window_sink_attention(q, k, v, sinks):
    B, S, _, dh = q.shape
    nb = S // BLOCK
    G = N_Q_HEADS // N_KV_HEADS
    qb = q.reshape(B, nb, BLOCK, N_KV_HEADS, G, dh)

    def band(t):
        tb = t.reshape(B, nb, BLOCK, N_KV_HEADS, dh)
        prev = jnp.concatenate([jnp.zeros_like(tb[:, :1]), tb[:, :-1]], axis=1)
        return jnp.concatenate([prev, tb], axis=2)

    kb, vb = band(k), band(v)
    s = jnp.einsum('bnqhgd,bnkhd->bnhgqk', qb, kb,
                   preferred_element_type=jnp.float32) * (dh ** -0.5)
    qi = jnp.arange(BLOCK)[:, None]
    kj = jnp.arange(2 * BLOCK)[None, :]
    dist = qi + BLOCK - kj
    local = (dist >= 0) & (dist < WINDOW)
    exists = (jnp.arange(nb)[:, None, None] > 0) | (kj[None] >= BLOCK)
    mask = local[None] & exists
    s = jnp.where(mask[None, :, None, None], s, -jnp.inf)
    sink = sinks.astype(jnp.float32).reshape(1, 1, N_KV_HEADS, G, 1, 1)
    m = jnp.maximum(jnp.max(s, axis=-1, keepdims=True), sink)
    p = jnp.exp(s - m)
    p = p / (jnp.sum(p, axis=-1, keepdims=True) + jnp.exp(sink - m))
    o = jnp.einsum('bnhgqk,bnkhd->bnqhgd', p.astype(v.dtype), vb)
    return o.reshape(B, S, N_Q_HEADS * dh)


def _conformer_conv(u, conv_w, conv_b, ln_w, ln_b):
    a, g = jnp.split(u, 2, axis=-1)
    z = a * jax.nn.sigmoid(g)
    z = lax.conv_general_dilated(
        z, conv_w[:, None, :], window_strides=(1,),
        padding=[(CONV_KERNEL - 1, 0)],
        dimension_numbers=('NWC', 'WIO', 'NWC'),
        feature_group_count=CONV_WIDTH) + conv_b
    z = _layer_norm(z, ln_w, ln_b)
    return jax.nn.silu(z)


def _fwd_setup_inputs(seed: int = 0) -> dict:
    key = jax.random.key(seed)
    ks = jax.random.split(key, 16)
    f32 = jnp.float32
    n = lambda k, shape, s: jax.random.normal(k, shape, f32) * s
    return {
        "x": n(ks[0], (BATCH, SEQ, D_MODEL), 1.0),
        "c": n(ks[1], (BATCH, D_MODEL), 1.0),
        "w_ada": n(ks[2], (DEPTH, D_MODEL, 3 * D_MODEL), 0.5 * D_MODEL ** -0.5),
        "b_ada": n(ks[3], (DEPTH, 3 * D_MODEL), 0.02),
        "norm_w": 1.0 + n(ks[4], (DEPTH, D_MODEL), 0.02),
        "w_in": n(ks[5], (DEPTH, D_MODEL, IN_WIDTH), D_MODEL ** -0.5),
        "q_norm_w": 1.0 + n(ks[6], (DEPTH, HEAD_DIM), 0.02),
        "k_norm_w": 1.0 + n(ks[7], (DEPTH, HEAD_DIM), 0.02),
        "sinks": n(ks[8], (DEPTH, N_Q_HEADS), 0.5),
        "conv_w": n(ks[9], (DEPTH, CONV_KERNEL, CONV_WIDTH), CONV_KERNEL ** -0.5),
        "conv_b": n(ks[10], (DEPTH, CONV_WIDTH), 0.02),
        "ln_w": 1.0 + n(ks[11], (DEPTH, CONV_WIDTH), 0.02),
        "ln_b": n(ks[12], (DEPTH, CONV_WIDTH), 0.02),
        "w_out": n(ks[13], (DEPTH, MIX_WIDTH, D_MODEL), MIX_WIDTH ** -0.5),
    }


def _fwd_reference(x, c, w_ada, b_ada, norm_w, w_in, q_norm_w, k_norm_w, sinks,
              conv_w, conv_b, ln_w, ln_b, w_out):
    B, S, _ = x.shape
    cos, sin = _rope_tables(S)
    c_act = jax.nn.silu(c)
    splits = (ATTN_WIDTH,
              ATTN_WIDTH + KV_WIDTH,
              ATTN_WIDTH + 2 * KV_WIDTH,
              2 * ATTN_WIDTH + 2 * KV_WIDTH,
              2 * ATTN_WIDTH + 2 * KV_WIDTH + 2 * CONV_WIDTH)
    for l in range(DEPTH):
        shift, scale, gate = jnp.split(c_act @ w_ada[l] + b_ada[l], 3, axis=-1)
        h = _rms_norm(x, norm_w[l]) * (1.0 + scale[:, None, :]) + shift[:, None, :]
        proj = h @ w_in[l]
        q, k, v, gate_a, u, gate_b = jnp.split(proj, splits, axis=-1)
        q = q.reshape(B, S, N_Q_HEADS, HEAD_DIM)
        k = k.reshape(B, S, N_KV_HEADS, HEAD_DIM)
        v = v.reshape(B, S, N_KV_HEADS, HEAD_DIM)
        q = _apply_rope(_rms_norm(q, q_norm_w[l]), cos, sin)
        k = _apply_rope(_rms_norm(k, k_norm_w[l]), cos, sin)
        y_a = _sliding_window_sink_attention(q, k, v, sinks[l]) * jax.nn.silu(gate_a)
        y_b = _conformer_conv(u, conv_w[l], conv_b[l], ln_w[l], ln_b[l]) * jax.nn.silu(gate_b)
        y = jnp.concatenate([y_a, y_b], axis=-1) @ w_out[l]
        x = x + gate[:, None, :] * y
    return x


import jax as _jax
import jax.numpy as _jnp

TWIN_FORMAT = 'train_step'
FWD_PARAMS = ['x', 'c', 'w_ada', 'b_ada', 'norm_w', 'w_in', 'q_norm_w', 'k_norm_w', 'sinks', 'conv_w', 'conv_b', 'ln_w', 'ln_b', 'w_out']
TWIN_WEIGHTS = ['w_ada', 'b_ada', 'norm_w', 'w_in', 'q_norm_w', 'k_norm_w', 'sinks', 'conv_w', 'conv_b', 'ln_w', 'ln_b', 'w_out']
TWIN_DIFF_INPUT = 'x'
TWIN_INPUTS = ['x', 'c', 'w_ada', 'b_ada', 'norm_w', 'w_in', 'q_norm_w', 'k_norm_w', 'sinks', 'conv_w', 'conv_b', 'ln_w', 'ln_b', 'w_out', 'loss_target', 'm_w_ada', 'm_b_ada', 'm_norm_w', 'm_w_in', 'm_q_norm_w', 'm_k_norm_w', 'm_sinks', 'm_conv_w', 'm_conv_b', 'm_ln_w', 'm_ln_b', 'm_w_out', 'v_w_ada', 'v_b_ada', 'v_norm_w', 'v_w_in', 'v_q_norm_w', 'v_k_norm_w', 'v_sinks', 'v_conv_w', 'v_conv_b', 'v_ln_w', 'v_ln_b', 'v_w_out']
TWIN_OUTPUTS = ['loss', 'grad_x', 'grad_w_ada', 'grad_b_ada', 'grad_norm_w', 'grad_w_in', 'grad_q_norm_w', 'grad_k_norm_w', 'grad_sinks', 'grad_conv_w', 'grad_conv_b', 'grad_ln_w', 'grad_ln_b', 'grad_w_out', 'delta_w_ada', 'delta_b_ada', 'delta_norm_w', 'delta_w_in', 'delta_q_norm_w', 'delta_k_norm_w', 'delta_sinks', 'delta_conv_w', 'delta_conv_b', 'delta_ln_w', 'delta_ln_b', 'delta_w_out', 'new_m_w_ada', 'new_m_b_ada', 'new_m_norm_w', 'new_m_w_in', 'new_m_q_norm_w', 'new_m_k_norm_w', 'new_m_sinks', 'new_m_conv_w', 'new_m_conv_b', 'new_m_ln_w', 'new_m_ln_b', 'new_m_w_out', 'new_v_w_ada', 'new_v_b_ada', 'new_v_norm_w', 'new_v_w_in', 'new_v_q_norm_w', 'new_v_k_norm_w', 'new_v_sinks', 'new_v_conv_w', 'new_v_conv_b', 'new_v_ln_w', 'new_v_ln_b', 'new_v_w_out']
TWIN_LEAF_KINDS = {'loss': 'loss', 'grad_x': 'grad_x', 'grad_w_ada': 'grad_w', 'grad_b_ada': 'grad_w', 'grad_norm_w': 'grad_w', 'grad_w_in': 'grad_w', 'grad_q_norm_w': 'grad_w', 'grad_k_norm_w': 'grad_w', 'grad_sinks': 'grad_w', 'grad_conv_w': 'grad_w', 'grad_conv_b': 'grad_w', 'grad_ln_w': 'grad_w', 'grad_ln_b': 'grad_w', 'grad_w_out': 'grad_w', 'delta_w_ada': 'delta_w', 'delta_b_ada': 'delta_w', 'delta_norm_w': 'delta_w', 'delta_w_in': 'delta_w', 'delta_q_norm_w': 'delta_w', 'delta_k_norm_w': 'delta_w', 'delta_sinks': 'delta_w', 'delta_conv_w': 'delta_w', 'delta_conv_b': 'delta_w', 'delta_ln_w': 'delta_w', 'delta_ln_b': 'delta_w', 'delta_w_out': 'delta_w', 'new_m_w_ada': 'new_m', 'new_m_b_ada': 'new_m', 'new_m_norm_w': 'new_m', 'new_m_w_in': 'new_m', 'new_m_q_norm_w': 'new_m', 'new_m_k_norm_w': 'new_m', 'new_m_sinks': 'new_m', 'new_m_conv_w': 'new_m', 'new_m_conv_b': 'new_m', 'new_m_ln_w': 'new_m', 'new_m_ln_b': 'new_m', 'new_m_w_out': 'new_m', 'new_v_w_ada': 'new_v', 'new_v_b_ada': 'new_v', 'new_v_norm_w': 'new_v', 'new_v_w_in': 'new_v', 'new_v_q_norm_w': 'new_v', 'new_v_k_norm_w': 'new_v', 'new_v_sinks': 'new_v', 'new_v_conv_w': 'new_v', 'new_v_conv_b': 'new_v', 'new_v_ln_w': 'new_v', 'new_v_ln_b': 'new_v', 'new_v_w_out': 'new_v'}


def _forward(args):
    return _fwd_reference(*[args[k] for k in FWD_PARAMS])


def _output_shape():
    out = _jax.eval_shape(lambda: _forward(_fwd_setup_inputs(0)))
    return out.shape, out.dtype

N_MICROBATCH = 1
ADAM_LR = 0.001
ADAM_B1 = 0.9
ADAM_B2 = 0.999
ADAM_EPS = 1e-08
ADAM_WD = 0.01
ADAM_STEP = 10
PER_EXAMPLE_BATCH_AXIS = {'x': 0, 'c': 0, 'loss_target': 0}
SHARED_INPUTS = []
_WEIGHT_DTYPES = {'w_ada': _jnp.float32, 'b_ada': _jnp.float32, 'norm_w': _jnp.float32, 'w_in': _jnp.float32, 'q_norm_w': _jnp.float32, 'k_norm_w': _jnp.float32, 'sinks': _jnp.float32, 'conv_w': _jnp.float32, 'conv_b': _jnp.float32, 'ln_w': _jnp.float32, 'ln_b': _jnp.float32, 'w_out': _jnp.float32}
MOMENT_SCALE = {'w_ada': 1.179445e-01, 'b_ada': 2.371860e-01, 'norm_w': 1.725853e-01, 'w_in': 3.463971e-02, 'q_norm_w': 6.211507e-02, 'k_norm_w': 6.089331e-02, 'sinks': 3.499647e-02, 'conv_w': 3.023461e-02, 'conv_b': 9.670767e-02, 'ln_w': 3.047978e-01, 'ln_b': 1.888638e-01, 'w_out': 2.162508e-02}


def _to_microbatches(a, axis):
    t = _jnp.moveaxis(a, axis, 0)
    t = t.reshape((N_MICROBATCH, t.shape[0] // N_MICROBATCH) + t.shape[1:])
    return _jnp.moveaxis(t, 1, axis + 1)


def setup_inputs(seed: int = 0) -> dict:
    inp = _fwd_setup_inputs(seed)
    key = _jax.random.fold_in(_jax.random.key(seed), 7919)
    shape, _ = _output_shape()
    out = dict(inp)
    out["loss_target"] = _jax.random.normal(_jax.random.fold_in(key, 0), shape, _jnp.float32)
    for i, name in enumerate(TWIN_WEIGHTS):
        w = inp[name].astype(_jnp.float32)
        if MOMENT_SCALE is None:
            s = _jnp.sqrt(_jnp.mean(_jnp.square(w)) + 1e-30)
        else:
            s = MOMENT_SCALE[name]
        km, kv = _jax.random.split(_jax.random.fold_in(key, i + 1))
        out[name] = w
        out["m_" + name] = s * _jax.random.normal(km, w.shape, _jnp.float32)
        out["v_" + name] = (s * s) * _jax.random.uniform(kv, w.shape, _jnp.float32, 0.5, 1.5)
    if N_MICROBATCH > 1:
        for name, axis in PER_EXAMPLE_BATCH_AXIS.items():
            out[name] = _to_microbatches(out[name], axis)
    return {'x': out['x'], 'c': out['c'], 'w_ada': out['w_ada'], 'b_ada': out['b_ada'], 'norm_w': out['norm_w'], 'w_in': out['w_in'], 'q_norm_w': out['q_norm_w'], 'k_norm_w': out['k_norm_w'], 'sinks': out['sinks'], 'conv_w': out['conv_w'], 'conv_b': out['conv_b'], 'ln_w': out['ln_w'], 'ln_b': out['ln_b'], 'w_out': out['w_out'], 'loss_target': out['loss_target'], 'm_w_ada': out['m_w_ada'], 'm_b_ada': out['m_b_ada'], 'm_norm_w': out['m_norm_w'], 'm_w_in': out['m_w_in'], 'm_q_norm_w': out['m_q_norm_w'], 'm_k_norm_w': out['m_k_norm_w'], 'm_sinks': out['m_sinks'], 'm_conv_w': out['m_conv_w'], 'm_conv_b': out['m_conv_b'], 'm_ln_w': out['m_ln_w'], 'm_ln_b': out['m_ln_b'], 'm_w_out': out['m_w_out'], 'v_w_ada': out['v_w_ada'], 'v_b_ada': out['v_b_ada'], 'v_norm_w': out['v_norm_w'], 'v_w_in': out['v_w_in'], 'v_q_norm_w': out['v_q_norm_w'], 'v_k_norm_w': out['v_k_norm_w'], 'v_sinks': out['v_sinks'], 'v_conv_w': out['v_conv_w'], 'v_conv_b': out['v_conv_b'], 'v_ln_w': out['v_ln_w'], 'v_ln_b': out['v_ln_b'], 'v_w_out': out['v_w_out']}


def _loss(weights, diff, rest, loss_target):
    with _jax.named_scope("forward"):
        args = {**rest, TWIN_DIFF_INPUT: diff, **{k: w.astype(_WEIGHT_DTYPES[k]) for k, w in weights.items()}}
        y = _forward(args)
    with _jax.named_scope("loss_head"):
        err = _jnp.square(y.astype(_jnp.float32) - loss_target)
        return 0.5 * _jnp.sum(_jnp.mean(err, axis=-1)) if err.ndim else 0.5 * err


def _adamw(w, g, m, v):
    m = ADAM_B1 * m + (1.0 - ADAM_B1) * g
    v = ADAM_B2 * v + (1.0 - ADAM_B2) * _jnp.square(g)
    m_hat = m / (1.0 - ADAM_B1 ** ADAM_STEP)
    v_hat = v / (1.0 - ADAM_B2 ** ADAM_STEP)
    delta = -ADAM_LR * (m_hat / (_jnp.sqrt(v_hat) + ADAM_EPS) + ADAM_WD * w)
    return delta, m, v


def reference(x, c, w_ada, b_ada, norm_w, w_in, q_norm_w, k_norm_w, sinks, conv_w, conv_b, ln_w, ln_b, w_out, loss_target, m_w_ada, m_b_ada, m_norm_w, m_w_in, m_q_norm_w, m_k_norm_w, m_sinks, m_conv_w, m_conv_b, m_ln_w, m_ln_b, m_w_out, v_w_ada, v_b_ada, v_norm_w, v_w_in, v_q_norm_w, v_k_norm_w, v_sinks, v_conv_w, v_conv_b, v_ln_w, v_ln_b, v_w_out):
    given = dict(x=x, c=c, w_ada=w_ada, b_ada=b_ada, norm_w=norm_w, w_in=w_in, q_norm_w=q_norm_w, k_norm_w=k_norm_w, sinks=sinks, conv_w=conv_w, conv_b=conv_b, ln_w=ln_w, ln_b=ln_b, w_out=w_out, loss_target=loss_target, m_w_ada=m_w_ada, m_b_ada=m_b_ada, m_norm_w=m_norm_w, m_w_in=m_w_in, m_q_norm_w=m_q_norm_w, m_k_norm_w=m_k_norm_w, m_sinks=m_sinks, m_conv_w=m_conv_w, m_conv_b=m_conv_b, m_ln_w=m_ln_w, m_ln_b=m_ln_b, m_w_out=m_w_out, v_w_ada=v_w_ada, v_b_ada=v_b_ada, v_norm_w=v_norm_w, v_w_in=v_w_in, v_q_norm_w=v_q_norm_w, v_k_norm_w=v_k_norm_w, v_sinks=v_sinks, v_conv_w=v_conv_w, v_conv_b=v_conv_b, v_ln_w=v_ln_w, v_ln_b=v_ln_b, v_w_out=v_w_out)
    weights = {n: given[n] for n in TWIN_WEIGHTS}
    shared = {n: given[n] for n in SHARED_INPUTS}
    per_example = {n: given[n] for n in ['x', 'c']}
    grad_fn = _jax.value_and_grad(_loss, argnums=(0, 1))

    def one_microbatch(ex, loss_target):
        ex = dict(ex)
        diff = ex.pop(TWIN_DIFF_INPUT)
        return grad_fn(weights, diff, {**shared, **ex}, loss_target)

    if N_MICROBATCH == 1:
        loss, (grad_w, grad_x) = one_microbatch(per_example, given["loss_target"])
    else:
        def body(carry, xs):
            loss_sum, grad_sum = carry
            l_k, (gw_k, gx_k) = one_microbatch(xs[0], xs[1])
            with _jax.named_scope("update"):
                return (loss_sum + l_k, _jax.tree.map(_jnp.add, grad_sum, gw_k)), gx_k

        init = (_jnp.zeros((), _jnp.float32), _jax.tree.map(_jnp.zeros_like, weights))
        (loss, grad_w), grad_x = _jax.lax.scan(body, init, (per_example, given["loss_target"]))
    with _jax.named_scope("update"):
        delta_w, new_m, new_v = {}, {}, {}
        for n in TWIN_WEIGHTS:
            delta_w[n], new_m[n], new_v[n] = _adamw(weights[n], grad_w[n], given["m_" + n], given["v_" + n])
    return (loss, grad_x, *[grad_w[n] for n in TWIN_WEIGHTS], *[delta_w[n] for n in TWIN_WEIGHTS],
            *[new_m[n] for n in TWIN_WEIGHTS], *[new_v[n] for n in TWIN_WEIGHTS])
```

```python
import functools

import jax
import jax.numpy as jnp
from jax import lax
from jax.experimental import pallas as pl
from jax.experimental.pallas import tpu as pltpu

F32 = jnp.float32
BF16 = jnp.bfloat16

D_MODEL = 1024
ATTN_W = 512
KV_W = 128
CONV_W = 512
IN_W = 2816
HEAD_DIM = 64
CONV_TAPS = 31
QBLK = 128
EPS = 1e-6
ROPE_THETA = 10000.0

ADAM_LR = 0.001
ADAM_B1 = 0.9
ADAM_B2 = 0.999
ADAM_EPS = 1e-08
ADAM_WD = 0.01
ADAM_STEP = 10

N_CHIPS = 4
N_DEV = 8
MAIN_W = 640
GRP_W = 128
WIRE_W = MAIN_W + GRP_W
SHARD_W = IN_W // N_CHIPS
MAIN_START = (0, 768, 1408, 2176)
GRP_START = (640, 640, 2048, 2048)
ADA_SHARD = 3 * D_MODEL // N_CHIPS

VMEM_LIMIT = 56 * 1024 * 1024
CONV_PAD = 32
SMALL_ROWS = 56


def _cparams(**kw):
    return pltpu.CompilerParams(vmem_limit_bytes=VMEM_LIMIT, **kw)


def _sigmoid(v):
    return 1.0 / (1.0 + jnp.exp(-v))


def _silu(v):
    return v * _sigmoid(v)


def _dsilu(v):
    s = _sigmoid(v)
    return s * (1.0 + v * (1.0 - s))


def _lane(shape):
    return lax.broadcasted_iota(jnp.int32, shape, len(shape) - 1)


def _in_proj(x, s1, shift, nw, w_full):
    t = x.shape[0]
    tm = 256

    def body(x_ref, s1_ref, sh_ref, nw_ref, w_ref, q_ref, kv_ref, ga_ref, ua_ref, ug_ref, gb_ref):
        xv = x_ref[...]
        r = lax.rsqrt(jnp.mean(xv * xv, axis=-1, keepdims=True) + EPS)
        h = (xv * r) * nw_ref[...] * s1_ref[...] + sh_ref[...]
        p = jnp.dot(h.astype(BF16), w_ref[...], preferred_element_type=F32)
        q_ref[...] = p[:, 0:512]
        kv_ref[...] = p[:, 512:768]
        ga_ref[...] = p[:, 768:1280]
        ua_ref[...] = p[:, 1280:1792]
        ug_ref[...] = p[:, 1792:2304]
        gb_ref[...] = p[:, 2304:2816]

    row = lambda w: pl.BlockSpec((tm, w), lambda i: (i, 0))
    vec = pl.BlockSpec((1, D_MODEL), lambda i: (0, 0))
    return pl.pallas_call(
        body,
        name="in_proj",
        grid=(t // tm,),
        in_specs=[row(D_MODEL), vec, vec, vec,
                  pl.BlockSpec((D_MODEL, IN_W), lambda i: (0, 0), pipeline_mode=pl.Buffered(1))],
        out_specs=[row(512), row(256), row(512), row(512), row(512), row(512)],
        out_shape=[jax.ShapeDtypeStruct((t, w), F32) for w in (512, 256, 512, 512, 512, 512)],
        compiler_params=_cparams(dimension_semantics=("arbitrary",)),
    )(x, s1, shift, nw, w_full)


def _head_mean(s, left):
    sl = jnp.sum(jnp.where(left, s, 0.0), axis=-1, keepdims=True)
    sr = jnp.sum(jnp.where(left, 0.0, s), axis=-1, keepdims=True)
    return jnp.where(left, sl, sr) * (1.0 / HEAD_DIM)


def _rot(v, first):
    return jnp.where(first, pltpu.roll(v, 96, 1), pltpu.roll(v, 32, 1))


def _norm_rope(v, w, cos, sin_s, left, first):
    r = lax.rsqrt(_head_mean(v * v, left) + EPS)
    xh = v * r
    n = xh * w
    return n * cos + _rot(n, first) * sin_s, xh, r


def _norm_rope_bwd(d, xh, r, w, cos, sin_s, left, first):
    dn = d * cos - _rot(d, first) * sin_s
    dw = jnp.sum(dn * xh, axis=0, keepdims=True)
    dxh = dn * w
    return r * (dxh - xh * _head_mean(dxh * xh, left)), dw


def _dup_heads(v, left):
    sw = pltpu.roll(v, 64, 1)
    return jnp.where(left, v, sw), jnp.where(left, sw, v)


def _prep_kv(kv_ref, kw_ref, cos_ref, sin_ref, ka_ref, va_ref, t):
    ch = 256
    for g in range(2):
        ka_ref[g, 0:QBLK, :] = jnp.zeros((QBLK, 128), BF16)
        va_ref[g, 0:QBLK, :] = jnp.zeros((QBLK, 128), BF16)

    def chunk(i, carry):
        r0 = pl.multiple_of(i * ch, ch)
        left = _lane((ch, 128)) < 64
        first = (_lane((ch, 128)) % 64) < 32
        k = kv_ref[pl.ds(r0, ch), 0:128]
        v = kv_ref[pl.ds(r0, ch), 128:256]
        kr, _, _ = _norm_rope(k, kw_ref[...], cos_ref[pl.ds(r0, ch), :], sin_ref[pl.ds(r0, ch), :], left, first)
        k0, k1 = _dup_heads(kr, left)
        v0, v1 = _dup_heads(v, left)
        ka_ref[0, pl.ds(QBLK + r0, ch), :] = k0.astype(BF16)
        ka_ref[1, pl.ds(QBLK + r0, ch), :] = k1.astype(BF16)
        va_ref[0, pl.ds(QBLK + r0, ch), :] = v0.astype(BF16)
        va_ref[1, pl.ds(QBLK + r0, ch), :] = v1.astype(BF16)
        return carry

    lax.fori_loop(0, t // ch, chunk, 0)


def _band_mask(n):
    qi = lax.broadcasted_iota(jnp.int32, (2 * QBLK, 2 * QBLK), 0) % QBLK
    kj = lax.broadcasted_iota(jnp.int32, (2 * QBLK, 2 * QBLK), 1)
    local = (kj > qi) & (kj <= qi + QBLK)
    return local & ((n > 0) | (kj >= QBLK))


def _softmax_pair(s, mask, sink0, sink1):
    row = lax.broadcasted_iota(jnp.int32, (2 * QBLK, 1), 0)
    sink = jnp.where(row < QBLK, sink0, sink1)
    s = jnp.where(mask, s, -jnp.inf)
    m = jnp.maximum(jnp.max(s, axis=-1, keepdims=True), sink)
    e = jnp.exp(s - m)
    es = jnp.exp(sink - m)
    inv = 1.0 / (jnp.sum(e, axis=-1, keepdims=True) + es)
    return e * inv, es * inv


def _stack_heads(v, left):
    return jnp.concatenate([jnp.where(left, v, 0.0), jnp.where(left, 0.0, v)], axis=0)


def _attn_fwd(q_raw, kv_raw, ga, qw2, kw2, sinks, cos_f, sin_s):
    t = q_raw.shape[0]
    nblk = t // QBLK

    def body(q_ref, kv_ref, ga_ref, qw_ref, kw_ref, sk_ref, cos_ref, sin_ref, o_ref, mix_ref, ka_ref, va_ref):
        _prep_kv(kv_ref, kw_ref, cos_ref, sin_ref, ka_ref, va_ref, t)

        def blk(n, carry):
            r0 = pl.multiple_of(n * QBLK, QBLK)
            left = _lane((QBLK, 128)) < 64
            first = (_lane((QBLK, 128)) % 64) < 32
            cos = cos_ref[pl.ds(r0, QBLK), :]
            sin = sin_ref[pl.ds(r0, QBLK), :]
            mask = _band_mask(n)
            for p in range(4):
                g = p // 2
                lanes = slice(p * 128, (p + 1) * 128)
                qr, _, _ = _norm_rope(q_ref[pl.ds(r0, QBLK), lanes], qw_ref[...], cos, sin, left, first)
                q2 = _stack_heads(qr * 0.125, left).astype(BF16)
                s = lax.dot_general(q2, ka_ref[g, pl.ds(r0, 2 * QBLK), :], (((1,), (1,)), ((), ())),
                                    preferred_element_type=F32)
                pm, _ = _softmax_pair(s, mask, sk_ref[0, 2 * p], sk_ref[0, 2 * p + 1])
                o2 = jnp.dot(pm.astype(BF16), va_ref[g, pl.ds(r0, 2 * QBLK), :], preferred_element_type=F32)
                o = jnp.where(left, o2[0:QBLK], o2[QBLK:2 * QBLK])
                o_ref[pl.ds(r0, QBLK), lanes] = o
                mix_ref[pl.ds(r0, QBLK), lanes] = (o * _silu(ga_ref[pl.ds(r0, QBLK), lanes])).astype(BF16)
            return carry

        lax.fori_loop(0, nblk, blk, 0)

    vm = pl.BlockSpec(memory_space=pltpu.VMEM)
    return pl.pallas_call(
        body,
        name="attn_fwd",
        in_specs=[vm, vm, vm, vm, vm, pl.BlockSpec(memory_space=pltpu.SMEM), vm, vm],
        out_specs=[vm, vm],
        out_shape=[jax.ShapeDtypeStruct((t, ATTN_W), F32), jax.ShapeDtypeStruct((t, ATTN_W), BF16)],
        scratch_shapes=[pltpu.VMEM((2, t + QBLK, 128), BF16), pltpu.VMEM((2, t + QBLK, 128), BF16)],
        compiler_params=_cparams(),
    )(q_raw, kv_raw, ga, qw2, kw2, sinks, cos_f, sin_s)


def _attn_bwd(q_raw, kv_raw, ga, o, dmix, qw2, kw2, sinks, cos_f, sin_s):
    t = q_raw.shape[0]
    nblk = t // QBLK

    def body(q_ref, kv_ref, ga_ref, o_ref, dm_ref, qw_ref, kw_ref, sk_ref, cos_ref, sin_ref,
             dq_ref, dkv_ref, dga_ref, sm_ref, ka_ref, va_ref, dka_ref, dva_ref):
        _prep_kv(kv_ref, kw_ref, cos_ref, sin_ref, ka_ref, va_ref, t)
        dka_ref[...] = jnp.zeros_like(dka_ref)
        dva_ref[...] = jnp.zeros_like(dva_ref)

        def blk(n, carry):
            dqw, dsk = carry
            r0 = pl.multiple_of(n * QBLK, QBLK)
            left = _lane((QBLK, 128)) < 64
            first = (_lane((QBLK, 128)) % 64) < 32
            cos = cos_ref[pl.ds(r0, QBLK), :]
            sin = sin_ref[pl.ds(r0, QBLK), :]
            mask = _band_mask(n)
            row = lax.broadcasted_iota(jnp.int32, (2 * QBLK, 1), 0)
            for p in range(4):
                g = p // 2
                lanes = slice(p * 128, (p + 1) * 128)
                rows = pl.ds(r0, QBLK)
                win = pl.ds(r0, 2 * QBLK)
                qr, xh, r = _norm_rope(q_ref[rows, lanes], qw_ref[...], cos, sin, left, first)
                q2 = _stack_heads(qr * 0.125, left).astype(BF16)
                kwin = ka_ref[g, win, :]
                vwin = va_ref[g, win, :]
                s = lax.dot_general(q2, kwin, (((1,), (1,)), ((), ())), preferred_element_type=F32)
                pm, ps = _softmax_pair(s, mask, sk_ref[0, 2 * p], sk_ref[0, 2 * p + 1])
                gav = ga_ref[rows, lanes]
                dmv = dm_ref[rows, lanes]
                dga_ref[rows, lanes] = dmv * o_ref[rows, lanes] * _dsilu(gav)
                do2 = _stack_heads(dmv * _silu(gav), left).astype(BF16)
                dp = lax.dot_general(do2, vwin, (((1,), (1,)), ((), ())), preferred_element_type=F32)
                delta = jnp.sum(pm * dp, axis=-1, keepdims=True)
                ds = (pm * (dp - delta)).astype(BF16)
                pd = ps * delta
                d0 = jnp.sum(jnp.where(row < QBLK, pd, 0.0), axis=0, keepdims=True)
                d1 = jnp.sum(jnp.where(row < QBLK, 0.0, pd), axis=0, keepdims=True)
                l8 = _lane((1, 128))
                dsk = dsk - jnp.where(l8 == 2 * p, d0, 0.0) - jnp.where(l8 == 2 * p + 1, d1, 0.0)
                dva_ref[g, win, :] += lax.dot_general(pm.astype(BF16), do2, (((0,), (0,)), ((), ())),
                                                      preferred_element_type=F32)
                dka_ref[g, win, :] += lax.dot_general(ds, q2, (((0,), (0,)), ((), ())),
                                                      preferred_element_type=F32)
                dq2 = jnp.dot(ds, kwin, preferred_element_type=F32)
                dqr = jnp.where(left, dq2[0:QBLK], dq2[QBLK:2 * QBLK]) * 0.125
                dq, dw = _norm_rope_bwd(dqr, xh, r, qw_ref[...], cos, sin, left, first)
                dq_ref[rows, lanes] = dq
                dqw = dqw + dw
            return dqw, dsk

        zero = jnp.zeros((1, 128), F32)
        dqw, dsk = lax.fori_loop(0, nblk, blk, (zero, zero))

        ch = 256

        def chunk(i, dkw):
            r0 = pl.multiple_of(i * ch, ch)
            left = _lane((ch, 128)) < 64
            first = (_lane((ch, 128)) % 64) < 32
            rows = pl.ds(r0, ch)
            prow = pl.ds(QBLK + r0, ch)

            def fold(ref):
                a0 = ref[0, prow, :]
                a1 = ref[1, prow, :]
                return jnp.where(left, a0 + pltpu.roll(a0, 64, 1), a1 + pltpu.roll(a1, 64, 1))

            cos = cos_ref[rows, :]
            sin = sin_ref[rows, :]
            _, xh, r = _norm_rope(kv_ref[rows, 0:128], kw_ref[...], cos, sin, left, first)
            dk, dw = _norm_rope_bwd(fold(dka_ref), xh, r, kw_ref[...], cos, sin, left, first)
            dkv_ref[rows, 0:128] = dk
            dkv_ref[rows, 128:256] = fold(dva_ref)
            return dkw + dw

        dkw = lax.fori_loop(0, t // ch, chunk, zero)
        sm_ref[...] = jnp.zeros((8, 128), F32)
        sm_ref[0:1, :] = dqw + pltpu.roll(dqw, 64, 1)
        sm_ref[1:2, :] = dkw + pltpu.roll(dkw, 64, 1)
        sm_ref[2:3, :] = dsk

    vm = pl.BlockSpec(memory_space=pltpu.VMEM)
    return pl.pallas_call(
        body,
        name="attn_bwd",
        in_specs=[vm, vm, vm, vm, vm, vm, vm, pl.BlockSpec(memory_space=pltpu.SMEM), vm, vm],
        out_specs=[vm, vm, vm, vm],
        out_shape=[jax.ShapeDtypeStruct((t, ATTN_W), F32), jax.ShapeDtypeStruct((t, 2 * KV_W), F32),
                   jax.ShapeDtypeStruct((t, ATTN_W), F32), jax.ShapeDtypeStruct((8, 128), F32)],
        scratch_shapes=[pltpu.VMEM((2, t + QBLK, 128), BF16), pltpu.VMEM((2, t + QBLK, 128), BF16),
                        pltpu.VMEM((2, t + QBLK, 128), F32), pltpu.VMEM((2, t + QBLK, 128), F32)],
        compiler_params=_cparams(),
    )(q_raw, kv_raw, ga, o, dmix, qw2, kw2, sinks, cos_f, sin_s)


CONV_CH = 256
CONV_SUB = 64


def _shifted_windows(src_ref, r0, sh_ref):
    rows = CONV_CH + CONV_PAD
    win = src_ref[pl.ds(r0, rows), :]
    sh_ref[0] = win
    for b in range(1, 8):
        sh_ref[b] = pltpu.roll(win, rows - b, 0)


def _conv_fwd(ua, ug, gb, cw, cb, lw, lb):
    t = ua.shape[0]

    def body(ua_ref, ug_ref, gb_ref, cw_ref, cb_ref, lw_ref, lb_ref, cz_ref, mix_ref, zp_ref, sh_ref):
        zp_ref[0:CONV_PAD, :] = jnp.zeros((CONV_PAD, CONV_W), F32)

        def glu(i, carry):
            r0 = pl.multiple_of(i * CONV_CH, CONV_CH)
            rows = pl.ds(r0, CONV_CH)
            zp_ref[pl.ds(CONV_PAD + r0, CONV_CH), :] = ua_ref[rows, :] * _sigmoid(ug_ref[rows, :])
            return carry

        lax.fori_loop(0, t // CONV_CH, glu, 0)

        def chunk(i, carry):
            r0 = pl.multiple_of(i * CONV_CH, CONV_CH)
            _shifted_windows(zp_ref, r0, sh_ref)
            for c in range(CONV_W // 128):
                lanes = slice(c * 128, (c + 1) * 128)

                def sub(k, carry2):
                    b0 = pl.multiple_of(k * CONV_SUB, CONV_SUB)
                    acc = jnp.broadcast_to(cb_ref[0:1, lanes], (CONV_SUB, 128))
                    for j in range(CONV_TAPS):
                        off = j + CONV_PAD - (CONV_TAPS - 1)
                        acc = acc + sh_ref[off % 8, pl.ds(b0 + 8 * (off // 8), CONV_SUB), lanes] * cw_ref[j:j + 1, lanes]
                    cz_ref[pl.ds(r0 + b0, CONV_SUB), lanes] = acc
                    return carry2

                lax.fori_loop(0, CONV_CH // CONV_SUB, sub, 0)
            rows = pl.ds(r0, CONV_CH)
            cz = cz_ref[rows, :]
            mu = jnp.mean(cz, axis=-1, keepdims=True)
            xc = cz - mu
            rs = lax.rsqrt(jnp.mean(xc * xc, axis=-1, keepdims=True) + EPS)
            ln = xc * rs * lw_ref[...] + lb_ref[...]
            mix_ref[rows, :] = (_silu(ln) * _silu(gb_ref[rows, :])).astype(BF16)
            return carry

        lax.fori_loop(0, t // CONV_CH, chunk, 0)

    vm = pl.BlockSpec(memory_space=pltpu.VMEM)
    return pl.pallas_call(
        body,
        name="conv_fwd",
        in_specs=[vm] * 7,
        out_specs=[vm, vm],
        out_shape=[jax.ShapeDtypeStruct((t, CONV_W), F32), jax.ShapeDtypeStruct((t, CONV_W), BF16)],
        scratch_shapes=[pltpu.VMEM((t + CONV_PAD, CONV_W), F32),
                        pltpu.VMEM((8, CONV_CH + CONV_PAD, CONV_W), F32)],
        compiler_params=_cparams(),
    )(ua, ug, gb, cw, cb, lw, lb)


def _conv_bwd(ua, ug, gb, cz, dmix, cw, lw, lb):
    t = ua.shape[0]

    def body(ua_ref, ug_ref, gb_ref, cz_ref, dm_ref, cw_ref, lw_ref, lb_ref,
             dua_ref, dug_ref, dgb_ref, dcw_ref, dvec_ref, zp_ref, dp_ref, sh_ref, wacc_ref):
        zp_ref[0:CONV_PAD, :] = jnp.zeros((CONV_PAD, CONV_W), F32)
        dp_ref[t:t + CONV_PAD, :] = jnp.zeros((CONV_PAD, CONV_W), F32)
        wacc_ref[...] = jnp.zeros_like(wacc_ref)

        def pointwise(i, carry):
            dcb, dlw, dlb = carry
            r0 = pl.multiple_of(i * CONV_CH, CONV_CH)
            rows = pl.ds(r0, CONV_CH)
            zp_ref[pl.ds(CONV_PAD + r0, CONV_CH), :] = ua_ref[rows, :] * _sigmoid(ug_ref[rows, :])
            cz = cz_ref[rows, :]
            mu = jnp.mean(cz, axis=-1, keepdims=True)
            xc = cz - mu
            rs = lax.rsqrt(jnp.mean(xc * xc, axis=-1, keepdims=True) + EPS)
            xh = xc * rs
            ln = xh * lw_ref[...] + lb_ref[...]
            gbv = gb_ref[rows, :]
            dy = dm_ref[rows, :]
            dgb_ref[rows, :] = dy * _silu(ln) * _dsilu(gbv)
            dl = dy * _silu(gbv) * _dsilu(ln)
            dxh = dl * lw_ref[...]
            dcz = rs * (dxh - jnp.mean(dxh, axis=-1, keepdims=True)
                        - xh * jnp.mean(dxh * xh, axis=-1, keepdims=True))
            dp_ref[rows, :] = dcz
            return (dcb + jnp.sum(dcz, axis=0, keepdims=True),
                    dlw + jnp.sum(dl * xh, axis=0, keepdims=True),
                    dlb + jnp.sum(dl, axis=0, keepdims=True))

        zero = jnp.zeros((1, CONV_W), F32)
        dcb, dlw, dlb = lax.fori_loop(0, t // CONV_CH, pointwise, (zero, zero, zero))
        dvec_ref[...] = jnp.zeros((8, CONV_W), F32)
        dvec_ref[0:1, :] = dcb
        dvec_ref[1:2, :] = dlw
        dvec_ref[2:3, :] = dlb

        def chunk(i, carry):
            r0 = pl.multiple_of(i * CONV_CH, CONV_CH)
            _shifted_windows(dp_ref, r0, sh_ref)
            for c in range(CONV_W // 128):
                lanes = slice(c * 128, (c + 1) * 128)

                def sub(k, carry2):
                    b0 = pl.multiple_of(k * CONV_SUB, CONV_SUB)
                    acc = jnp.zeros((CONV_SUB, 128), F32)
                    for j in range(CONV_TAPS):
                        off = CONV_TAPS - 1 - j
                        acc = acc + sh_ref[off % 8, pl.ds(b0 + 8 * (off // 8), CONV_SUB), lanes] * cw_ref[j:j + 1, lanes]
                    rr = pl.ds(r0 + b0, CONV_SUB)
                    sg = _sigmoid(ug_ref[rr, lanes])
                    dua_ref[rr, lanes] = acc * sg
                    dug_ref[rr, lanes] = acc * ua_ref[rr, lanes] * sg * (1.0 - sg)
                    return carry2

                lax.fori_loop(0, CONV_CH // CONV_SUB, sub, 0)
            _shifted_windows(zp_ref, r0, sh_ref)
            for c in range(CONV_W // 128):
                lanes = slice(c * 128, (c + 1) * 128)

                def subw(k, carry2):
                    b0 = pl.multiple_of(k * CONV_SUB, CONV_SUB)
                    dcz = dp_ref[pl.ds(r0 + b0, CONV_SUB), lanes]
                    for j in range(CONV_TAPS):
                        off = j + CONV_PAD - (CONV_TAPS - 1)
                        pr = dcz * sh_ref[off % 8, pl.ds(b0 + 8 * (off // 8), CONV_SUB), lanes]
                        part = pr[0:8]
                        for q in range(1, CONV_SUB // 8):
                            part = part + pr[8 * q:8 * (q + 1)]
                        wacc_ref[8 * j:8 * (j + 1), lanes] += part
                    return carry2

                lax.fori_loop(0, CONV_CH // CONV_SUB, subw, 0)
            return carry

        lax.fori_loop(0, t // CONV_CH, chunk, 0)
        dcw_ref[...] = jnp.zeros((32, CONV_W), F32)
        for j in range(CONV_TAPS):
            dcw_ref[j:j + 1, :] = jnp.sum(wacc_ref[8 * j:8 * (j + 1), :], axis=0, keepdims=True)

    vm = pl.BlockSpec(memory_space=pltpu.VMEM)
    return pl.pallas_call(
        body,
        name="conv_bwd",
        in_specs=[vm] * 8,
        out_specs=[vm] * 5,
        out_shape=[jax.ShapeDtypeStruct((t, CONV_W), F32)] * 3
        + [jax.ShapeDtypeStruct((32, CONV_W), F32), jax.ShapeDtypeStruct((8, CONV_W), F32)],
        scratch_shapes=[pltpu.VMEM((t + CONV_PAD, CONV_W), F32), pltpu.VMEM((t + CONV_PAD, CONV_W), F32),
                        pltpu.VMEM((8, CONV_CH + CONV_PAD, CONV_W), F32), pltpu.VMEM((8 * 32, CONV_W), F32)],
        compiler_params=_cparams(),
    )(ua, ug, gb, cz, dmix, cw, lw, lb)


def _out_proj(mix_a, mix_b, x, tgt, gate, w_out):
    t = x.shape[0]
    tm = 256
    nstep = t // tm

    def body(ma_ref, mb_ref, x_ref, t_ref, g_ref, w_ref, dout_ref, dma_ref, dmb_ref, gw_ref, red_ref, acc_ref):
        i = pl.program_id(0)

        @pl.when(i == 0)
        def _():
            acc_ref[...] = jnp.zeros_like(acc_ref)
            red_ref[...] = jnp.zeros_like(red_ref)

        mix = jnp.concatenate([ma_ref[...], mb_ref[...]], axis=1)
        y = jnp.dot(mix, w_ref[...], preferred_element_type=F32)
        gate_v = g_ref[...]
        err = x_ref[...] + gate_v * y - t_ref[...]
        dout = err * (1.0 / D_MODEL)
        dout_ref[...] = dout
        red_ref[0:1, :] += jnp.sum(dout * y, axis=0, keepdims=True)
        red_ref[1:2, :] += jnp.sum(err * err, axis=0, keepdims=True)
        dy = (dout * gate_v).astype(BF16)
        dmix = lax.dot_general(dy, w_ref[...], (((1,), (1,)), ((), ())), preferred_element_type=F32)
        dma_ref[...] = dmix[:, 0:512]
        dmb_ref[...] = dmix[:, 512:1024]
        acc_ref[...] += lax.dot_general(mix, dy, (((0,), (0,)), ((), ())), preferred_element_type=F32)

        @pl.when(i == nstep - 1)
        def _():
            gw_ref[...] = acc_ref[...].astype(BF16)

    row = lambda w: pl.BlockSpec((tm, w), lambda i: (i, 0))
    const = lambda s: pl.BlockSpec(s, lambda i: (0, 0))
    return pl.pallas_call(
        body,
        name="out_proj",
        grid=(nstep,),
        in_specs=[row(512), row(512), row(D_MODEL), row(D_MODEL), const((1, D_MODEL)),
                  pl.BlockSpec((D_MODEL, D_MODEL), lambda i: (0, 0), pipeline_mode=pl.Buffered(1))],
        out_specs=[row(D_MODEL), row(512), row(512), const((D_MODEL, D_MODEL)), const((8, D_MODEL))],
        out_shape=[jax.ShapeDtypeStruct((t, D_MODEL), F32), jax.ShapeDtypeStruct((t, 512), F32),
                   jax.ShapeDtypeStruct((t, 512), F32), jax.ShapeDtypeStruct((D_MODEL, D_MODEL), BF16),
                   jax.ShapeDtypeStruct((8, D_MODEL), F32)],
        scratch_shapes=[pltpu.VMEM((D_MODEL, D_MODEL), F32)],
        compiler_params=_cparams(dimension_semantics=("arbitrary",)),
    )(mix_a, mix_b, x, tgt, gate, w_out)


def _in_proj_bwd(dq, dkv, dga, dua, dug, dgb, x, dout, s1, shift, nw, w_full):
    t = x.shape[0]
    tm = 256
    nstep = t // tm

    def body(dq_ref, dkv_ref, dga_ref, dua_ref, dug_ref, dgb_ref, x_ref, dout_ref, s1_ref, sh_ref, nw_ref, w_ref,
             gx_ref, g4_ref, red_ref, acc_ref):
        i = pl.program_id(0)

        @pl.when(i == 0)
        def _():
            acc_ref[...] = jnp.zeros_like(acc_ref)
            red_ref[...] = jnp.zeros_like(red_ref)

        xv = x_ref[...]
        r = lax.rsqrt(jnp.mean(xv * xv, axis=-1, keepdims=True) + EPS)
        xh = xv * r
        n = xh * nw_ref[...]
        h = (n * s1_ref[...] + sh_ref[...]).astype(BF16)
        dproj = jnp.concatenate([dq_ref[...], dkv_ref[...], dga_ref[...], dua_ref[...], dug_ref[...], dgb_ref[...]],
                                axis=1).astype(BF16)
        dh = lax.dot_general(dproj, w_ref[...], (((1,), (1,)), ((), ())), preferred_element_type=F32)
        acc_ref[...] += lax.dot_general(h, dproj, (((0,), (0,)), ((), ())), preferred_element_type=F32)
        red_ref[0:1, :] += jnp.sum(dh, axis=0, keepdims=True)
        red_ref[1:2, :] += jnp.sum(dh * n, axis=0, keepdims=True)
        dn = dh * s1_ref[...]
        red_ref[2:3, :] += jnp.sum(dn * xh, axis=0, keepdims=True)
        dxh = dn * nw_ref[...]
        gx_ref[...] = dout_ref[...] + r * (dxh - xh * jnp.mean(dxh * xh, axis=-1, keepdims=True))

        @pl.when(i == nstep - 1)
        def _():
            for j in range(N_CHIPS):
                g4_ref[j, :, 0:MAIN_W] = acc_ref[:, MAIN_START[j]:MAIN_START[j] + MAIN_W].astype(BF16)
                g4_ref[j, :, MAIN_W:WIRE_W] = acc_ref[:, GRP_START[j]:GRP_START[j] + GRP_W].astype(BF16)

    row = lambda w: pl.BlockSpec((tm, w), lambda i: (i, 0))
    vec = pl.BlockSpec((1, D_MODEL), lambda i: (0, 0))
    return pl.pallas_call(
        body,
        name="in_proj_bwd",
        grid=(nstep,),
        in_specs=[row(512), row(256), row(512), row(512), row(512), row(512), row(D_MODEL), row(D_MODEL), vec, vec, vec,
                  pl.BlockSpec((D_MODEL, IN_W), lambda i: (0, 0), pipeline_mode=pl.Buffered(1))],
        out_specs=[row(D_MODEL), pl.BlockSpec((N_CHIPS, D_MODEL, WIRE_W), lambda i: (0, 0, 0)),
                   pl.BlockSpec((8, D_MODEL), lambda i: (0, 0))],
        out_shape=[jax.ShapeDtypeStruct((t, D_MODEL), F32), jax.ShapeDtypeStruct((N_CHIPS, D_MODEL, WIRE_W), BF16),
                   jax.ShapeDtypeStruct((8, D_MODEL), F32)],
        scratch_shapes=[pltpu.VMEM((D_MODEL, IN_W), F32)],
        compiler_params=_cparams(dimension_semantics=("arbitrary",)),
    )(dq, dkv, dga, dua, dug, dgb, x, dout, s1, shift, nw, w_full)


MESH = pl.DeviceIdType.MESH


def _place():
    x, y, c = lax.axis_index("x"), lax.axis_index("y"), lax.axis_index("c")
    chips = [(1 - x, y), (x, 1 - y), (1 - x, 1 - y)]
    return x, y, c, chips


def _remote(sems_s, sems_r, k, src, dst, to):
    return pltpu.make_async_remote_copy(src_ref=src, dst_ref=dst, send_sem=sems_s.at[k], recv_sem=sems_r.at[k],
                                        device_id=to, device_id_type=MESH)


def _gather_weights(wm, wb, wo, cw, c_row, w_ada, b_sh):
    n_sem = 25

    def body(wm_ref, wb_ref, wo_ref, cw_ref, c_ref, wada_ref, bsh_ref,
             w4_ref, wo4_ref, cw4_ref, call_ref, ada_ref, part_ref, ssem, rsem):
        x, y, c, chips = _place()
        j = 2 * x + y
        dev = 2 * j + c
        sib = (x, y, 1 - c)
        idx = [2 * cx + cy for cx, cy in chips]
        half = pl.ds(pl.multiple_of(c * 512, 512), 512)
        ohalf = pl.ds(pl.multiple_of((1 - c) * 512, 512), 512)
        hrow = pl.ds(pl.multiple_of(c * 128, 128), 128)
        ohrow = pl.ds(pl.multiple_of((1 - c) * 128, 128), 128)
        rc = functools.partial(_remote, ssem, rsem)

        w4_ref[j, :, 0:MAIN_W] = wm_ref[...].astype(BF16)
        w4_ref[j, :, MAIN_W:WIRE_W] = wb_ref[...].astype(BF16)
        wo4_ref[j] = wo_ref[...].astype(BF16)
        cw4_ref[j] = cw_ref[...]
        call_ref[dev] = c_ref[...]

        sends = []
        peers = [(px, py, pc) for px in (x, 1 - x) for py in (y, 1 - y) for pc in (c, 1 - c)][1:]
        for k, peer in enumerate(peers):
            sends.append(rc(k, call_ref.at[dev], call_ref.at[dev], peer))
        for k, chip in enumerate(chips):
            to = (*chip, c)
            sends.append(rc(7 + k, w4_ref.at[j, half, :], w4_ref.at[j, half, :], to))
            sends.append(rc(13 + k, wo4_ref.at[j, hrow, :], wo4_ref.at[j, hrow, :], to))
            sends.append(rc(19 + k, cw4_ref.at[j], cw4_ref.at[j], to))
        for cp in sends:
            cp.start()

        for k, (px, py, pc) in enumerate(peers):
            pdev = 4 * px + 2 * py + pc
            rc(k, call_ref.at[pdev], call_ref.at[pdev], (px, py, pc)).wait_recv()
        rowid = lax.broadcasted_iota(jnp.int32, (N_DEV, D_MODEL), 0)
        call = jnp.zeros((N_DEV, D_MODEL), F32)
        for r in range(N_DEV):
            call = jnp.where(rowid == r, jnp.broadcast_to(call_ref[r], (N_DEV, D_MODEL)), call)
        part = jnp.dot(_silu(call).astype(BF16), wada_ref[...].astype(BF16), preferred_element_type=F32) + bsh_ref[...]
        for r in range(N_DEV):
            part_ref[r] = part[r:r + 1, :]
        ada_ref[j] = part_ref[dev]
        rows_out = []
        for k, chip in enumerate(chips):
            rows_out.append(rc(22 + k, part_ref.at[2 * idx[k] + c], ada_ref.at[j], (*chip, c)))
            rows_out[-1].start()

        passed = []
        for k, chip in enumerate(chips):
            jk = idx[k]
            rc(7 + k, w4_ref.at[jk, half, :], w4_ref.at[jk, half, :], sib).wait_recv()
            passed.append(rc(10 + k, w4_ref.at[jk, half, :], w4_ref.at[jk, half, :], sib))
            passed[-1].start()
            rc(13 + k, wo4_ref.at[jk, hrow, :], wo4_ref.at[jk, hrow, :], sib).wait_recv()
            passed.append(rc(16 + k, wo4_ref.at[jk, hrow, :], wo4_ref.at[jk, hrow, :], sib))
            passed[-1].start()
        for k, chip in enumerate(chips):
            jk = idx[k]
            rc(10 + k, w4_ref.at[jk, ohalf, :], w4_ref.at[jk, ohalf, :], sib).wait_recv()
            rc(16 + k, wo4_ref.at[jk, ohrow, :], wo4_ref.at[jk, ohrow, :], sib).wait_recv()
            rc(19 + k, cw4_ref.at[jk], cw4_ref.at[jk], sib).wait_recv()
            rc(22 + k, ada_ref.at[jk], ada_ref.at[jk], sib).wait_recv()
        for cp in sends + rows_out + passed:
            cp.wait_send()

    vm = pl.BlockSpec(memory_space=pltpu.VMEM)
    return pl.pallas_call(
        body,
        name="gather_weights",
        in_specs=[vm] * 7,
        out_specs=[vm] * 5,
        out_shape=[jax.ShapeDtypeStruct((N_CHIPS, D_MODEL, WIRE_W), BF16),
                   jax.ShapeDtypeStruct((N_CHIPS, D_MODEL // N_CHIPS, D_MODEL), BF16),
                   jax.ShapeDtypeStruct((N_CHIPS, 32, 128), F32),
                   jax.ShapeDtypeStruct((N_DEV, 1, D_MODEL), F32),
                   jax.ShapeDtypeStruct((N_CHIPS, 1, ADA_SHARD), F32)],
        scratch_shapes=[pltpu.VMEM((N_DEV, 1, ADA_SHARD), F32),
                        pltpu.SemaphoreType.DMA((n_sem,)), pltpu.SemaphoreType.DMA((n_sem,))],
        compiler_params=_cparams(),
    )(wm, wb, wo, cw, c_row, w_ada, b_sh)


def _reduce_grads(g4, go, small):
    n_sem = 17
    rows = D_MODEL // 2

    def body(g4_ref, go_ref, sm_ref, gw_ref, gwo_ref, ssum_ref, sall_ref,
             sib4_ref, sibo_ref, out4_ref, outo_ref, in4_ref, ino_ref, ssem, rsem):
        x, y, c, chips = _place()
        j = 2 * x + y
        dev = 2 * j + c
        sib = (x, y, 1 - c)
        idx = [2 * cx + cy for cx, cy in chips]
        mine = pl.ds(pl.multiple_of(c * rows, rows), rows)
        other = pl.ds(pl.multiple_of((1 - c) * rows, rows), rows)
        rc = functools.partial(_remote, ssem, rsem)

        sall_ref[dev] = sm_ref[...]
        sends = []
        peers = [(px, py, pc) for px in (x, 1 - x) for py in (y, 1 - y) for pc in (c, 1 - c)][1:]
        for k, peer in enumerate(peers):
            sends.append(rc(k, sall_ref.at[dev], sall_ref.at[dev], peer))
        sends.append(rc(7, g4_ref.at[:, other, :], sib4_ref, sib))
        sends.append(rc(8, go_ref.at[:, 1 - c], sibo_ref, sib))
        for cp in sends:
            cp.start()

        rc(7, g4_ref.at[:, other, :], sib4_ref, sib).wait_recv()
        rc(8, go_ref.at[:, 1 - c], sibo_ref, sib).wait_recv()
        ch = 128
        traded = []
        for k, chip in enumerate(chips):
            jk = idx[k]

            def add4(i, carry, jk=jk, k=k):
                r0 = pl.multiple_of(i * ch, ch)
                mrow = pl.ds(pl.multiple_of(c * rows + r0, ch), ch)
                out4_ref[k, pl.ds(r0, ch), :] = (g4_ref[jk, mrow, :].astype(F32)
                                                 + sib4_ref[jk, pl.ds(r0, ch), :].astype(F32)).astype(BF16)
                return carry

            lax.fori_loop(0, rows // ch, add4, 0)
            outo_ref[k] = (go_ref[jk, c].astype(F32) + sibo_ref[jk].astype(F32)).astype(BF16)
            traded.append(rc(9 + k, out4_ref.at[k], in4_ref.at[k], (*chip, c)))
            traded.append(rc(12 + k, outo_ref.at[k], ino_ref.at[k], (*chip, c)))
            traded[-2].start()
            traded[-1].start()

        for k, chip in enumerate(chips):
            rc(9 + k, out4_ref.at[k], in4_ref.at[k], sib).wait_recv()
            rc(12 + k, outo_ref.at[k], ino_ref.at[k], sib).wait_recv()

        def sum4(i, carry):
            r0 = pl.multiple_of(i * ch, ch)
            mrow = pl.ds(pl.multiple_of(c * rows + r0, ch), ch)
            acc = g4_ref[j, mrow, :].astype(F32) + sib4_ref[j, pl.ds(r0, ch), :].astype(F32)
            for k in range(3):
                acc = acc + in4_ref[k, pl.ds(r0, ch), :].astype(F32)
            gw_ref[mrow, :] = acc
            return carry

        lax.fori_loop(0, rows // ch, sum4, 0)
        acc = go_ref[j, c].astype(F32) + sibo_ref[j].astype(F32)
        for k in range(3):
            acc = acc + ino_ref[k].astype(F32)
        gwo_ref[c] = acc
        final = [rc(15, gw_ref.at[mine, :], gw_ref.at[mine, :], sib), rc(16, gwo_ref.at[c], gwo_ref.at[c], sib)]
        for cp in final:
            cp.start()

        for k, (px, py, pc) in enumerate(peers):
            pdev = 4 * px + 2 * py + pc
            rc(k, sall_ref.at[pdev], sall_ref.at[pdev], (px, py, pc)).wait_recv()
        tot = sall_ref[0]
        for d in range(1, N_DEV):
            tot = tot + sall_ref[d]
        ssum_ref[...] = tot

        rc(15, gw_ref.at[other, :], gw_ref.at[other, :], sib).wait_recv()
        rc(16, gwo_ref.at[1 - c], gwo_ref.at[1 - c], sib).wait_recv()
        for cp in sends + traded + final:
            cp.wait_send()

    vm = pl.BlockSpec(memory_space=pltpu.VMEM)
    return pl.pallas_call(
        body,
        name="reduce_grads",
        in_specs=[vm] * 3,
        out_specs=[vm] * 4,
        out_shape=[jax.ShapeDtypeStruct((D_MODEL, WIRE_W), F32), jax.ShapeDtypeStruct((2, 128, D_MODEL), F32),
                   jax.ShapeDtypeStruct((SMALL_ROWS, 512), F32), jax.ShapeDtypeStruct((N_DEV, SMALL_ROWS, 512), F32)],
        scratch_shapes=[pltpu.VMEM((N_CHIPS, rows, WIRE_W), BF16), pltpu.VMEM((N_CHIPS, 128, D_MODEL), BF16),
                        pltpu.VMEM((3, rows, WIRE_W), BF16), pltpu.VMEM((3, 128, D_MODEL), BF16),
                        pltpu.VMEM((3, rows, WIRE_W), BF16), pltpu.VMEM((3, 128, D_MODEL), BF16),
                        pltpu.SemaphoreType.DMA((n_sem,)), pltpu.SemaphoreType.DMA((n_sem,))],
        compiler_params=_cparams(),
    )(g4, go, small)


def _adamw_math(w, g, m, v):
    m2 = ADAM_B1 * m + (1.0 - ADAM_B1) * g
    v2 = ADAM_B2 * v + (1.0 - ADAM_B2) * (g * g)
    m_hat = m2 / (1.0 - ADAM_B1 ** ADAM_STEP)
    v_hat = v2 / (1.0 - ADAM_B2 ** ADAM_STEP)
    delta = -ADAM_LR * (m_hat / (jnp.sqrt(v_hat) + ADAM_EPS) + ADAM_WD * w)
    return delta, m2, v2


def _adamw(name, w, g, m, v, tm):
    r, cdim = w.shape

    def body(w_ref, g_ref, m_ref, v_ref, d_ref, m2_ref, v2_ref):
        d_ref[...], m2_ref[...], v2_ref[...] = _adamw_math(w_ref[...], g_ref[...], m_ref[...], v_ref[...])

    blk = pl.BlockSpec((tm, cdim), lambda i: (i, 0))
    return pl.pallas_call(
        body,
        name=name,
        grid=(r // tm,),
        in_specs=[blk] * 4,
        out_specs=[blk] * 3,
        out_shape=[jax.ShapeDtypeStruct((r, cdim), F32)] * 3,
        compiler_params=_cparams(dimension_semantics=("arbitrary",)),
    )(w, g, m, v)


def _adamw_ada(w, m, v, cact_t, dcols):
    r, cdim = w.shape
    tm = 256

    def body(w_ref, m_ref, v_ref, ct_ref, dc_ref, g_ref, d_ref, m2_ref, v2_ref):
        g = jnp.dot(ct_ref[...], dc_ref[...], preferred_element_type=F32, precision=lax.Precision.HIGHEST)
        g_ref[...] = g
        d_ref[...], m2_ref[...], v2_ref[...] = _adamw_math(w_ref[...], g, m_ref[...], v_ref[...])

    blk = pl.BlockSpec((tm, cdim), lambda i: (i, 0))
    return pl.pallas_call(
        body,
        name="adamw_w_ada",
        grid=(r // tm,),
        in_specs=[blk] * 3 + [pl.BlockSpec((tm, N_DEV), lambda i: (i, 0)), pl.BlockSpec((N_DEV, cdim), lambda i: (0, 0))],
        out_specs=[blk] * 4,
        out_shape=[jax.ShapeDtypeStruct((r, cdim), F32)] * 4,
        compiler_params=_cparams(dimension_semantics=("arbitrary",)),
    )(w, m, v, cact_t, dcols)


def _adamw_small(ws, gs, ms, vs):
    n = len(ws)

    def body(*refs):
        w_r, g_r, m_r, v_r = refs[0:n], refs[n:2 * n], refs[2 * n:3 * n], refs[3 * n:4 * n]
        d_r, m2_r, v2_r = refs[4 * n:5 * n], refs[5 * n:6 * n], refs[6 * n:7 * n]
        for i in range(n):
            d_r[i][...], m2_r[i][...], v2_r[i][...] = _adamw_math(w_r[i][...], g_r[i][...], m_r[i][...], v_r[i][...])

    vm = pl.BlockSpec(memory_space=pltpu.VMEM)
    shapes = [jax.ShapeDtypeStruct(w.shape, F32) for w in ws]
    out = pl.pallas_call(
        body,
        name="adamw_small",
        in_specs=[vm] * (4 * n),
        out_specs=[vm] * (3 * n),
        out_shape=shapes * 3,
        compiler_params=_cparams(),
    )(*ws, *gs, *ms, *vs)
    return out[0:n], out[n:2 * n], out[2 * n:3 * n]


def _rope_tables(t):
    inv = ROPE_THETA ** (-jnp.arange(0, HEAD_DIM, 2, dtype=F32) / HEAD_DIM)
    ang = jnp.arange(t, dtype=F32)[:, None] * inv[None, :]
    cos, sin = jnp.cos(ang), jnp.sin(ang)
    return jnp.tile(cos, (1, 4)), jnp.tile(jnp.concatenate([-sin, sin], axis=1), (1, 2))


def _pad_lanes(v, width):
    return jnp.pad(v, ((0, 0), (0, width - v.shape[1])))


def kernel(x, c, w_ada, b_ada, norm_w, w_in, q_norm_w, k_norm_w, sinks, conv_w, conv_b, ln_w, ln_b, w_out, loss_target, m_w_ada, m_b_ada, m_norm_w, m_w_in, m_q_norm_w, m_k_norm_w, m_sinks, m_conv_w, m_conv_b, m_ln_w, m_ln_b, m_w_out, v_w_ada, v_b_ada, v_norm_w, v_w_in, v_q_norm_w, v_k_norm_w, v_sinks, v_conv_w, v_conv_b, v_ln_w, v_ln_b, v_w_out):
    xi, yi = lax.axis_index("x"), lax.axis_index("y")
    j = 2 * xi + yi
    x2, tgt = x[0], loss_target[0]
    t = x2.shape[0]

    w_in_s = w_in[0]
    wm = lax.dynamic_slice(w_in_s, (0, 64 * yi), (D_MODEL, MAIN_W))
    hb = lax.dynamic_slice(w_in_s, (0, MAIN_W * (1 - yi)), (D_MODEL, 64))
    z64 = jnp.zeros((D_MODEL, 64), F32)
    wb = jnp.where(yi == 0, jnp.concatenate([hb, z64], axis=1), jnp.concatenate([z64, hb], axis=1))
    cw_pad = jnp.pad(conv_w[0], ((0, 1), (0, 0)))
    b_sh = lax.dynamic_slice(b_ada, (0, ADA_SHARD * j), (1, ADA_SHARD))

    w4, wo4, cw4, call, ada4 = _gather_weights(wm, wb, w_out[0], cw_pad, c, w_ada[0], b_sh)
    w_full = jnp.concatenate([w4[0, :, :MAIN_W], w4[0, :, MAIN_W:] + w4[1, :, MAIN_W:], w4[1, :, :MAIN_W],
                              w4[2, :, :MAIN_W], w4[2, :, MAIN_W:] + w4[3, :, MAIN_W:], w4[3, :, :MAIN_W]], axis=1)
    w_out_full = wo4.reshape(D_MODEL, D_MODEL)
    cw_full = jnp.concatenate([cw4[i] for i in range(N_CHIPS)], axis=1)
    ada = ada4.reshape(1, 3 * D_MODEL)
    shift, s1, gate = ada[:, :D_MODEL], 1.0 + ada[:, D_MODEL:2 * D_MODEL], ada[:, 2 * D_MODEL:]

    cos_f, sin_s = _rope_tables(t)
    qw2, kw2 = jnp.tile(q_norm_w, (1, 2)), jnp.tile(k_norm_w, (1, 2))

    q_raw, kv_raw, ga, ua, ug, gb = _in_proj(x2, s1, shift, norm_w, w_full)
    o, mix_a = _attn_fwd(q_raw, kv_raw, ga, qw2, kw2, sinks, cos_f, sin_s)
    cz, mix_b = _conv_fwd(ua, ug, gb, cw_full, conv_b, ln_w, ln_b)
    dout, dmix_a, dmix_b, gwo_bf, red_o = _out_proj(mix_a, mix_b, x2, tgt, gate, w_out_full)
    loss = lax.psum(0.5 / D_MODEL * jnp.sum(red_o[1]), ("x", "y", "c"))

    dq, dkv, dga, sm_a = _attn_bwd(q_raw, kv_raw, ga, o, dmix_a, qw2, kw2, sinks, cos_f, sin_s)
    dua, dug, dgb, dcw, dvec = _conv_bwd(ua, ug, gb, cz, dmix_b, cw_full, ln_w, ln_b)
    grad_x, g4, red_i = _in_proj_bwd(dq, dkv, dga, dua, dug, dgb, x2, dout, s1, shift, norm_w, w_full)

    small = jnp.concatenate([
        dcw, dvec,
        jnp.concatenate([red_i[2:3], red_i[0:1], red_i[1:2], red_o[0:1]], axis=1).reshape(8, 512),
        _pad_lanes(sm_a, 512)], axis=0)
    gw, gwo, ssum, sall = _reduce_grads(g4, gwo_bf.reshape(N_CHIPS, 2, 128, D_MODEL), small)

    g_w_in = jnp.where(yi == 0, gw[:, :SHARD_W], jnp.concatenate([gw[:, SHARD_W:WIRE_W], gw[:, :MAIN_W]], axis=1))
    g_w_out = gwo.reshape(D_MODEL // N_CHIPS, D_MODEL)
    g_conv_w = lax.dynamic_slice(ssum, (0, 128 * j), (CONV_TAPS, 128))
    g_conv_b, g_ln_w, g_ln_b = ssum[32:33], ssum[33:34], ssum[34:35]
    g_norm_w = ssum[40:42].reshape(1, D_MODEL)
    g_b_ada = ssum[42:48].reshape(1, 3 * D_MODEL)
    g_qw, g_kw, g_sinks = ssum[48:49, 0:HEAD_DIM], ssum[49:50, 0:HEAD_DIM], ssum[50:51, 0:8]
    d_ada_all = sall[:, 42:48, :].reshape(N_DEV, 3 * D_MODEL)
    dcols = lax.dynamic_slice(d_ada_all, (0, ADA_SHARD * j), (N_DEV, ADA_SHARD))
    cact_t = jax.nn.silu(call.reshape(N_DEV, D_MODEL)).T

    g_w_ada, d_w_ada, nm_w_ada, nv_w_ada = _adamw_ada(w_ada[0], m_w_ada[0], v_w_ada[0], cact_t, dcols)
    d_w_in, nm_w_in, nv_w_in = _adamw("adamw_w_in", w_in_s, g_w_in, m_w_in[0], v_w_in[0], 256)
    d_w_out, nm_w_out, nv_w_out = _adamw("adamw_w_out", w_out[0], g_w_out, m_w_out[0], v_w_out[0], 128)
    ws = [b_ada, norm_w, q_norm_w, k_norm_w, sinks, conv_w[0], conv_b, ln_w, ln_b]
    gs = [g_b_ada, g_norm_w, g_qw, g_kw, g_sinks, g_conv_w, g_conv_b, g_ln_w, g_ln_b]
    ms = [m_b_ada, m_norm_w, m_q_norm_w, m_k_norm_w, m_sinks, m_conv_w[0], m_conv_b, m_ln_w, m_ln_b]
    vs = [v_b_ada, v_norm_w, v_q_norm_w, v_k_norm_w, v_sinks, v_conv_w[0], v_conv_b, v_ln_w, v_ln_b]
    ds, nms, nvs = _adamw_small(ws, gs, ms, vs)

    def order(ada_v, in_v, out_v, sm):
        b, nw_, qw_, kw_, sk_, cw_, cb_, lw_, lb_ = sm
        return [ada_v[None], b, nw_, in_v[None], qw_, kw_, sk_, cw_[None], cb_, lw_, lb_, out_v[None]]

    grads = order(g_w_ada, g_w_in, g_w_out, gs)
    deltas = order(d_w_ada, d_w_in, d_w_out, ds)
    new_m = order(nm_w_ada, nm_w_in, nm_w_out, nms)
    new_v = order(nv_w_ada, nv_w_in, nv_w_out, nvs)
    return (loss, grad_x[None], *grads, *deltas, *new_m, *new_v)
```

```python
import functools

import jax
import jax.numpy as jnp
from jax import lax
from jax.experimental import pallas as pl
from jax.experimental.pallas import tpu as pltpu

F32 = jnp.float32
BF16 = jnp.bfloat16

D_MODEL = 1024
ATTN_W = 512
KV_W = 128
CONV_W = 512
IN_W = 2816
HEAD_DIM = 64
CONV_TAPS = 31
QBLK = 128
EPS = 1e-6
ROPE_THETA = 10000.0

ADAM_LR = 0.001
ADAM_B1 = 0.9
ADAM_B2 = 0.999
ADAM_EPS = 1e-08
ADAM_WD = 0.01
ADAM_STEP = 10

N_CHIPS = 4
N_DEV = 8
IN_HALF = IN_W // N_CHIPS // 2
OUT_HALF = D_MODEL // N_CHIPS // 2
ADA_SHARD = 3 * D_MODEL // N_CHIPS

VMEM_LIMIT = 56 * 1024 * 1024
CONV_PAD = 32
SMALL_ROWS = 64


def _cparams(**kw):
    return pltpu.CompilerParams(vmem_limit_bytes=VMEM_LIMIT, **kw)


def _sigmoid(v):
    return 1.0 / (1.0 + jnp.exp(-v))


def _silu(v):
    return v * _sigmoid(v)


def _dsilu(v):
    s = _sigmoid(v)
    return s * (1.0 + v * (1.0 - s))


def _lane(shape):
    return lax.broadcasted_iota(jnp.int32, shape, len(shape) - 1)


def _in_proj(x, s1, shift, nw, wt_full):
    t = x.shape[0]
    tm = 256

    def body(x_ref, s1_ref, sh_ref, nw_ref, w_ref, q_ref, kv_ref, ga_ref, ua_ref, ug_ref, gb_ref):
        xv = x_ref[...]
        r = lax.rsqrt(jnp.mean(xv * xv, axis=-1, keepdims=True) + EPS)
        h = (xv * r) * nw_ref[...] * s1_ref[...] + sh_ref[...]
        p = lax.dot_general(h.astype(BF16), w_ref[...], (((1,), (1,)), ((), ())), preferred_element_type=F32)
        q_ref[...] = p[:, 0:512]
        kv_ref[...] = p[:, 512:768]
        ga_ref[...] = p[:, 768:1280]
        ua_ref[...] = p[:, 1280:1792]
        ug_ref[...] = p[:, 1792:2304]
        gb_ref[...] = p[:, 2304:2816]

    row = lambda w: pl.BlockSpec((tm, w), lambda i: (i, 0))
    vec = pl.BlockSpec((1, D_MODEL), lambda i: (0, 0))
    return pl.pallas_call(
        body,
        name="in_proj",
        grid=(t // tm,),
        in_specs=[row(D_MODEL), vec, vec, vec,
                  pl.BlockSpec((IN_W, D_MODEL), lambda i: (0, 0), pipeline_mode=pl.Buffered(1))],
        out_specs=[row(512), row(256), row(512), row(512), row(512), row(512)],
        out_shape=[jax.ShapeDtypeStruct((t, w), F32) for w in (512, 256, 512, 512, 512, 512)],
        compiler_params=_cparams(dimension_semantics=("arbitrary",)),
    )(x, s1, shift, nw, wt_full)


def _head_mean(s, left):
    sl = jnp.sum(jnp.where(left, s, 0.0), axis=-1, keepdims=True)
    sr = jnp.sum(jnp.where(left, 0.0, s), axis=-1, keepdims=True)
    return jnp.where(left, sl, sr) * (1.0 / HEAD_DIM)


def _rot(v, first):
    return jnp.where(first, pltpu.roll(v, 96, 1), pltpu.roll(v, 32, 1))


def _norm_rope(v, w, cos, sin_s, left, first):
    r = lax.rsqrt(_head_mean(v * v, left) + EPS)
    xh = v * r
    n = xh * w
    return n * cos + _rot(n, first) * sin_s, xh, r


def _norm_rope_bwd(d, xh, r, w, cos, sin_s, left, first):
    dn = d * cos - _rot(d, first) * sin_s
    dw = jnp.sum(dn * xh, axis=0, keepdims=True)
    dxh = dn * w
    return r * (dxh - xh * _head_mean(dxh * xh, left)), dw


def _dup_heads(v, left):
    sw = pltpu.roll(v, 64, 1)
    return jnp.where(left, v, sw), jnp.where(left, sw, v)


def _prep_kv(kv_ref, kw_ref, cos_ref, sin_ref, ka_ref, va_ref, t):
    ch = 256
    for g in range(2):
        ka_ref[g, 0:QBLK, :] = jnp.zeros((QBLK, 128), BF16)
        va_ref[g, 0:QBLK, :] = jnp.zeros((QBLK, 128), BF16)

    def chunk(i, carry):
        r0 = pl.multiple_of(i * ch, ch)
        left = _lane((ch, 128)) < 64
        first = (_lane((ch, 128)) % 64) < 32
        k = kv_ref[pl.ds(r0, ch), 0:128]
        v = kv_ref[pl.ds(r0, ch), 128:256]
        kr, _, _ = _norm_rope(k, kw_ref[...], cos_ref[pl.ds(r0, ch), :], sin_ref[pl.ds(r0, ch), :], left, first)
        k0, k1 = _dup_heads(kr, left)
        v0, v1 = _dup_heads(v, left)
        ka_ref[0, pl.ds(QBLK + r0, ch), :] = k0.astype(BF16)
        ka_ref[1, pl.ds(QBLK + r0, ch), :] = k1.astype(BF16)
        va_ref[0, pl.ds(QBLK + r0, ch), :] = v0.astype(BF16)
        va_ref[1, pl.ds(QBLK + r0, ch), :] = v1.astype(BF16)
        return carry

    lax.fori_loop(0, t // ch, chunk, 0)


def _band_mask(n):
    qi = lax.broadcasted_iota(jnp.int32, (2 * QBLK, 2 * QBLK), 0) % QBLK
    kj = lax.broadcasted_iota(jnp.int32, (2 * QBLK, 2 * QBLK), 1)
    local = (kj > qi) & (kj <= qi + QBLK)
    return local & ((n > 0) | (kj >= QBLK))


def _softmax_pair(s, mask, sink0, sink1):
    row = lax.broadcasted_iota(jnp.int32, (2 * QBLK, 1), 0)
    sink = jnp.where(row < QBLK, sink0, sink1)
    s = jnp.where(mask, s, -jnp.inf)
    m = jnp.maximum(jnp.max(s, axis=-1, keepdims=True), sink)
    e = jnp.exp(s - m)
    es = jnp.exp(sink - m)
    inv = 1.0 / (jnp.sum(e, axis=-1, keepdims=True) + es)
    return e * inv, es * inv


def _stack_heads(v, left):
    return jnp.concatenate([jnp.where(left, v, 0.0), jnp.where(left, 0.0, v)], axis=0)


def _attn_fwd(q_raw, kv_raw, ga, qw2, kw2, sinks, cos_f, sin_s):
    t = q_raw.shape[0]
    nblk = t // QBLK

    def body(q_ref, kv_ref, ga_ref, qw_ref, kw_ref, sk_ref, cos_ref, sin_ref, o_ref, mix_ref, ka_ref, va_ref):
        _prep_kv(kv_ref, kw_ref, cos_ref, sin_ref, ka_ref, va_ref, t)

        def blk(n, carry):
            r0 = pl.multiple_of(n * QBLK, QBLK)
            left = _lane((QBLK, 128)) < 64
            first = (_lane((QBLK, 128)) % 64) < 32
            cos = cos_ref[pl.ds(r0, QBLK), :]
            sin = sin_ref[pl.ds(r0, QBLK), :]
            mask = _band_mask(n)
            for p in range(4):
                g = p // 2
                lanes = slice(p * 128, (p + 1) * 128)
                qr, _, _ = _norm_rope(q_ref[pl.ds(r0, QBLK), lanes], qw_ref[...], cos, sin, left, first)
                q2 = _stack_heads(qr * 0.125, left).astype(BF16)
                s = lax.dot_general(q2, ka_ref[g, pl.ds(r0, 2 * QBLK), :], (((1,), (1,)), ((), ())),
                                    preferred_element_type=F32)
                pm, _ = _softmax_pair(s, mask, sk_ref[0, 2 * p], sk_ref[0, 2 * p + 1])
                o2 = jnp.dot(pm.astype(BF16), va_ref[g, pl.ds(r0, 2 * QBLK), :], preferred_element_type=F32)
                o = jnp.where(left, o2[0:QBLK], o2[QBLK:2 * QBLK])
                o_ref[pl.ds(r0, QBLK), lanes] = o
                mix_ref[pl.ds(r0, QBLK), lanes] = (o * _silu(ga_ref[pl.ds(r0, QBLK), lanes])).astype(BF16)
            return carry

        lax.fori_loop(0, nblk, blk, 0)

    vm = pl.BlockSpec(memory_space=pltpu.VMEM)
    return pl.pallas_call(
        body,
        name="attn_fwd",
        in_specs=[vm, vm, vm, vm, vm, pl.BlockSpec(memory_space=pltpu.SMEM), vm, vm],
        out_specs=[vm, vm],
        out_shape=[jax.ShapeDtypeStruct((t, ATTN_W), F32), jax.ShapeDtypeStruct((t, ATTN_W), BF16)],
        scratch_shapes=[pltpu.VMEM((2, t + QBLK, 128), BF16), pltpu.VMEM((2, t + QBLK, 128), BF16)],
        compiler_params=_cparams(),
    )(q_raw, kv_raw, ga, qw2, kw2, sinks, cos_f, sin_s)


def _attn_bwd(q_raw, kv_raw, ga, o, dmix, qw2, kw2, sinks, cos_f, sin_s):
    t = q_raw.shape[0]
    nblk = t // QBLK

    def body(q_ref, kv_ref, ga_ref, o_ref, dm_ref, qw_ref, kw_ref, sk_ref, cos_ref, sin_ref,
             dq_ref, dkv_ref, dga_ref, sm_ref, ka_ref, va_ref, dka_ref, dva_ref):
        _prep_kv(kv_ref, kw_ref, cos_ref, sin_ref, ka_ref, va_ref, t)
        dka_ref[...] = jnp.zeros_like(dka_ref)
        dva_ref[...] = jnp.zeros_like(dva_ref)

        def blk(n, carry):
            dqw, dsk = carry
            r0 = pl.multiple_of(n * QBLK, QBLK)
            left = _lane((QBLK, 128)) < 64
            first = (_lane((QBLK, 128)) % 64) < 32
            cos = cos_ref[pl.ds(r0, QBLK), :]
            sin = sin_ref[pl.ds(r0, QBLK), :]
            mask = _band_mask(n)
            row = lax.broadcasted_iota(jnp.int32, (2 * QBLK, 1), 0)
            for p in range(4):
                g = p // 2
                lanes = slice(p * 128, (p + 1) * 128)
                rows = pl.ds(r0, QBLK)
                win = pl.ds(r0, 2 * QBLK)
                qr, xh, r = _norm_rope(q_ref[rows, lanes], qw_ref[...], cos, sin, left, first)
                q2 = _stack_heads(qr * 0.125, left).astype(BF16)
                kwin = ka_ref[g, win, :]
                vwin = va_ref[g, win, :]
                s = lax.dot_general(q2, kwin, (((1,), (1,)), ((), ())), preferred_element_type=F32)
                pm, ps = _softmax_pair(s, mask, sk_ref[0, 2 * p], sk_ref[0, 2 * p + 1])
                gav = ga_ref[rows, lanes]
                dmv = dm_ref[rows, lanes]
                dga_ref[rows, lanes] = (dmv * o_ref[rows, lanes] * _dsilu(gav)).astype(BF16)
                do2 = _stack_heads(dmv * _silu(gav), left).astype(BF16)
                dp = lax.dot_general(do2, vwin, (((1,), (1,)), ((), ())), preferred_element_type=F32)
                delta = jnp.sum(pm * dp, axis=-1, keepdims=True)
                ds = (pm * (dp - delta)).astype(BF16)
                pd = ps * delta
                d0 = jnp.sum(jnp.where(row < QBLK, pd, 0.0), axis=0, keepdims=True)
                d1 = jnp.sum(jnp.where(row < QBLK, 0.0, pd), axis=0, keepdims=True)
                l8 = _lane((1, 128))
                dsk = dsk - jnp.where(l8 == 2 * p, d0, 0.0) - jnp.where(l8 == 2 * p + 1, d1, 0.0)
                dva_ref[g, win, :] += lax.dot_general(pm.astype(BF16), do2, (((0,), (0,)), ((), ())),
                                                      preferred_element_type=F32)
                dka_ref[g, win, :] += lax.dot_general(ds, q2, (((0,), (0,)), ((), ())),
                                                      preferred_element_type=F32)
                dq2 = jnp.dot(ds, kwin, preferred_element_type=F32)
                dqr = jnp.where(left, dq2[0:QBLK], dq2[QBLK:2 * QBLK]) * 0.125
                dq, dw = _norm_rope_bwd(dqr, xh, r, qw_ref[...], cos, sin, left, first)
                dq_ref[rows, lanes] = dq.astype(BF16)
                dqw = dqw + dw
            return dqw, dsk

        zero = jnp.zeros((1, 128), F32)
        dqw, dsk = lax.fori_loop(0, nblk, blk, (zero, zero))

        ch = 256

        def chunk(i, dkw):
            r0 = pl.multiple_of(i * ch, ch)
            left = _lane((ch, 128)) < 64
            first = (_lane((ch, 128)) % 64) < 32
            rows = pl.ds(r0, ch)
            prow = pl.ds(QBLK + r0, ch)

            def fold(ref):
                a0 = ref[0, prow, :]
                a1 = ref[1, prow, :]
                return jnp.where(left, a0 + pltpu.roll(a0, 64, 1), a1 + pltpu.roll(a1, 64, 1))

            cos = cos_ref[rows, :]
            sin = sin_ref[rows, :]
            _, xh, r = _norm_rope(kv_ref[rows, 0:128], kw_ref[...], cos, sin, left, first)
            dk, dw = _norm_rope_bwd(fold(dka_ref), xh, r, kw_ref[...], cos, sin, left, first)
            dkv_ref[rows, 0:128] = dk.astype(BF16)
            dkv_ref[rows, 128:256] = fold(dva_ref).astype(BF16)
            return dkw + dw

        dkw = lax.fori_loop(0, t // ch, chunk, zero)
        sm_ref[...] = jnp.zeros((8, 128), F32)
        sm_ref[0:1, :] = dqw + pltpu.roll(dqw, 64, 1)
        sm_ref[1:2, :] = dkw + pltpu.roll(dkw, 64, 1)
        sm_ref[2:3, :] = dsk

    vm = pl.BlockSpec(memory_space=pltpu.VMEM)
    return pl.pallas_call(
        body,
        name="attn_bwd",
        in_specs=[vm, vm, vm, vm, vm, vm, vm, pl.BlockSpec(memory_space=pltpu.SMEM), vm, vm],
        out_specs=[vm, vm, vm, vm],
        out_shape=[jax.ShapeDtypeStruct((t, ATTN_W), BF16), jax.ShapeDtypeStruct((t, 2 * KV_W), BF16),
                   jax.ShapeDtypeStruct((t, ATTN_W), BF16), jax.ShapeDtypeStruct((8, 128), F32)],
        scratch_shapes=[pltpu.VMEM((2, t + QBLK, 128), BF16), pltpu.VMEM((2, t + QBLK, 128), BF16),
                        pltpu.VMEM((2, t + QBLK, 128), F32), pltpu.VMEM((2, t + QBLK, 128), F32)],
        compiler_params=_cparams(),
    )(q_raw, kv_raw, ga, o, dmix, qw2, kw2, sinks, cos_f, sin_s)


CONV_CH = 256
CONV_SUB = 64


def _shifted_windows(src_ref, r0, sh_ref):
    rows = CONV_CH + CONV_PAD
    win = src_ref[pl.ds(r0, rows), :]
    sh_ref[0] = win
    for b in range(1, 8):
        sh_ref[b] = pltpu.roll(win, rows - b, 0)


def _conv_fwd(ua, ug, gb, cw, cb, lw, lb):
    t = ua.shape[0]

    def body(ua_ref, ug_ref, gb_ref, cw_ref, cb_ref, lw_ref, lb_ref, cz_ref, mix_ref, zp_ref, sh_ref):
        zp_ref[0:CONV_PAD, :] = jnp.zeros((CONV_PAD, CONV_W), F32)

        def glu(i, carry):
            r0 = pl.multiple_of(i * CONV_CH, CONV_CH)
            rows = pl.ds(r0, CONV_CH)
            zp_ref[pl.ds(CONV_PAD + r0, CONV_CH), :] = ua_ref[rows, :] * _sigmoid(ug_ref[rows, :])
            return carry

        lax.fori_loop(0, t // CONV_CH, glu, 0)

        def chunk(i, carry):
            r0 = pl.multiple_of(i * CONV_CH, CONV_CH)
            _shifted_windows(zp_ref, r0, sh_ref)
            for c in range(CONV_W // 128):
                lanes = slice(c * 128, (c + 1) * 128)

                def sub(k, carry2):
                    b0 = pl.multiple_of(k * CONV_SUB, CONV_SUB)
                    acc = jnp.broadcast_to(cb_ref[0:1, lanes], (CONV_SUB, 128))
                    for j in range(CONV_TAPS):
                        off = j + CONV_PAD - (CONV_TAPS - 1)
                        acc = acc + sh_ref[off % 8, pl.ds(b0 + 8 * (off // 8), CONV_SUB), lanes] * cw_ref[j:j + 1, lanes]
                    cz_ref[pl.ds(r0 + b0, CONV_SUB), lanes] = acc
                    return carry2

                lax.fori_loop(0, CONV_CH // CONV_SUB, sub, 0)
            rows = pl.ds(r0, CONV_CH)
            cz = cz_ref[rows, :]
            mu = jnp.mean(cz, axis=-1, keepdims=True)
            xc = cz - mu
            rs = lax.rsqrt(jnp.mean(xc * xc, axis=-1, keepdims=True) + EPS)
            ln = xc * rs * lw_ref[...] + lb_ref[...]
            mix_ref[rows, :] = (_silu(ln) * _silu(gb_ref[rows, :])).astype(BF16)
            return carry

        lax.fori_loop(0, t // CONV_CH, chunk, 0)

    vm = pl.BlockSpec(memory_space=pltpu.VMEM)
    return pl.pallas_call(
        body,
        name="conv_fwd",
        in_specs=[vm] * 7,
        out_specs=[vm, vm],
        out_shape=[jax.ShapeDtypeStruct((t, CONV_W), F32), jax.ShapeDtypeStruct((t, CONV_W), BF16)],
        scratch_shapes=[pltpu.VMEM((t + CONV_PAD, CONV_W), F32),
                        pltpu.VMEM((8, CONV_CH + CONV_PAD, CONV_W), F32)],
        compiler_params=_cparams(),
    )(ua, ug, gb, cw, cb, lw, lb)


def _conv_bwd(ua, ug, gb, cz, dmix, cw, lw, lb):
    t = ua.shape[0]

    def body(ua_ref, ug_ref, gb_ref, cz_ref, dm_ref, cw_ref, lw_ref, lb_ref,
             dua_ref, dug_ref, dgb_ref, dcw_ref, dvec_ref, zp_ref, dp_ref, sh_ref, wacc_ref):
        zp_ref[0:CONV_PAD, :] = jnp.zeros((CONV_PAD, CONV_W), F32)
        dp_ref[t:t + CONV_PAD, :] = jnp.zeros((CONV_PAD, CONV_W), F32)
        wacc_ref[...] = jnp.zeros_like(wacc_ref)

        def pointwise(i, carry):
            dcb, dlw, dlb = carry
            r0 = pl.multiple_of(i * CONV_CH, CONV_CH)
            rows = pl.ds(r0, CONV_CH)
            zp_ref[pl.ds(CONV_PAD + r0, CONV_CH), :] = ua_ref[rows, :] * _sigmoid(ug_ref[rows, :])
            cz = cz_ref[rows, :]
            mu = jnp.mean(cz, axis=-1, keepdims=True)
            xc = cz - mu
            rs = lax.rsqrt(jnp.mean(xc * xc, axis=-1, keepdims=True) + EPS)
            xh = xc * rs
            ln = xh * lw_ref[...] + lb_ref[...]
            gbv = gb_ref[rows, :]
            dy = dm_ref[rows, :]
            dgb_ref[rows, :] = (dy * _silu(ln) * _dsilu(gbv)).astype(BF16)
            dl = dy * _silu(gbv) * _dsilu(ln)
            dxh = dl * lw_ref[...]
            dcz = rs * (dxh - jnp.mean(dxh, axis=-1, keepdims=True)
                        - xh * jnp.mean(dxh * xh, axis=-1, keepdims=True))
            dp_ref[rows, :] = dcz
            return (dcb + jnp.sum(dcz, axis=0, keepdims=True),
                    dlw + jnp.sum(dl * xh, axis=0, keepdims=True),
                    dlb + jnp.sum(dl, axis=0, keepdims=True))

        zero = jnp.zeros((1, CONV_W), F32)
        dcb, dlw, dlb = lax.fori_loop(0, t // CONV_CH, pointwise, (zero, zero, zero))
        dvec_ref[...] = jnp.zeros((8, CONV_W), F32)
        dvec_ref[0:1, :] = dcb
        dvec_ref[1:2, :] = dlw
        dvec_ref[2:3, :] = dlb

        def chunk(i, carry):
            r0 = pl.multiple_of(i * CONV_CH, CONV_CH)
            _shifted_windows(dp_ref, r0, sh_ref)
            for c in range(CONV_W // 128):
                lanes = slice(c * 128, (c + 1) * 128)

                def sub(k, carry2):
                    b0 = pl.multiple_of(k * CONV_SUB, CONV_SUB)
                    acc = jnp.zeros((CONV_SUB, 128), F32)
                    for j in range(CONV_TAPS):
                        off = CONV_TAPS - 1 - j
                        acc = acc + sh_ref[off % 8, pl.ds(b0 + 8 * (off // 8), CONV_SUB), lanes] * cw_ref[j:j + 1, lanes]
                    rr = pl.ds(r0 + b0, CONV_SUB)
                    sg = _sigmoid(ug_ref[rr, lanes])
                    dua_ref[rr, lanes] = (acc * sg).astype(BF16)
                    dug_ref[rr, lanes] = (acc * ua_ref[rr, lanes] * sg * (1.0 - sg)).astype(BF16)
                    return carry2

                lax.fori_loop(0, CONV_CH // CONV_SUB, sub, 0)
            _shifted_windows(zp_ref, r0, sh_ref)
            for c in range(CONV_W // 128):
                lanes = slice(c * 128, (c + 1) * 128)

                def subw(k, carry2):
                    b0 = pl.multiple_of(k * CONV_SUB, CONV_SUB)
                    dcz = dp_ref[pl.ds(r0 + b0, CONV_SUB), lanes]
                    for j in range(CONV_TAPS):
                        off = j + CONV_PAD - (CONV_TAPS - 1)
                        pr = dcz * sh_ref[off % 8, pl.ds(b0 + 8 * (off // 8), CONV_SUB), lanes]
                        part = pr[0:8]
                        for q in range(1, CONV_SUB // 8):
                            part = part + pr[8 * q:8 * (q + 1)]
                        wacc_ref[8 * j:8 * (j + 1), lanes] += part
                    return carry2

                lax.fori_loop(0, CONV_CH // CONV_SUB, subw, 0)
            return carry

        lax.fori_loop(0, t // CONV_CH, chunk, 0)
        dcw_ref[...] = jnp.zeros((32, CONV_W), F32)
        for j in range(CONV_TAPS):
            dcw_ref[j:j + 1, :] = jnp.sum(wacc_ref[8 * j:8 * (j + 1), :], axis=0, keepdims=True)

    vm = pl.BlockSpec(memory_space=pltpu.VMEM)
    return pl.pallas_call(
        body,
        name="conv_bwd",
        in_specs=[vm] * 8,
        out_specs=[vm] * 5,
        out_shape=[jax.ShapeDtypeStruct((t, CONV_W), BF16)] * 3
        + [jax.ShapeDtypeStruct((32, CONV_W), F32), jax.ShapeDtypeStruct((8, CONV_W), F32)],
        scratch_shapes=[pltpu.VMEM((t + CONV_PAD, CONV_W), F32), pltpu.VMEM((t + CONV_PAD, CONV_W), F32),
                        pltpu.VMEM((8, CONV_CH + CONV_PAD, CONV_W), F32), pltpu.VMEM((8 * 32, CONV_W), F32)],
        compiler_params=_cparams(),
    )(ua, ug, gb, cz, dmix, cw, lw, lb)


def _out_proj(mix_a, mix_b, x, tgt, gate, w_out):
    t = x.shape[0]
    tm = 256
    nstep = t // tm

    def body(ma_ref, mb_ref, x_ref, t_ref, g_ref, w_ref, dout_ref, dma_ref, dmb_ref, gw_ref, red_ref, acc_ref):
        i = pl.program_id(0)

        @pl.when(i == 0)
        def _():
            acc_ref[...] = jnp.zeros_like(acc_ref)
            red_ref[...] = jnp.zeros_like(red_ref)

        mix = jnp.concatenate([ma_ref[...], mb_ref[...]], axis=1)
        y = jnp.dot(mix, w_ref[...], preferred_element_type=F32)
        gate_v = g_ref[...]
        err = x_ref[...] + gate_v * y - t_ref[...]
        dout = err * (1.0 / D_MODEL)
        dout_ref[...] = dout
        red_ref[0:1, :] += jnp.sum(dout * y, axis=0, keepdims=True)
        red_ref[1:2, :] += jnp.sum(err * err, axis=0, keepdims=True)
        dy = (dout * gate_v).astype(BF16)
        dmix = lax.dot_general(dy, w_ref[...], (((1,), (1,)), ((), ())), preferred_element_type=F32)
        dma_ref[...] = dmix[:, 0:512]
        dmb_ref[...] = dmix[:, 512:1024]
        acc_ref[...] += lax.dot_general(mix, dy, (((0,), (0,)), ((), ())), preferred_element_type=F32)

        @pl.when(i == nstep - 1)
        def _():
            gw_ref[...] = acc_ref[...].astype(BF16)

    row = lambda w: pl.BlockSpec((tm, w), lambda i: (i, 0))
    const = lambda s: pl.BlockSpec(s, lambda i: (0, 0))
    return pl.pallas_call(
        body,
        name="out_proj",
        grid=(nstep,),
        in_specs=[row(512), row(512), row(D_MODEL), row(D_MODEL), const((1, D_MODEL)),
                  pl.BlockSpec((D_MODEL, D_MODEL), lambda i: (0, 0), pipeline_mode=pl.Buffered(1))],
        out_specs=[row(D_MODEL), row(512), row(512), const((D_MODEL, D_MODEL)), const((8, D_MODEL))],
        out_shape=[jax.ShapeDtypeStruct((t, D_MODEL), F32), jax.ShapeDtypeStruct((t, 512), F32),
                   jax.ShapeDtypeStruct((t, 512), F32), jax.ShapeDtypeStruct((D_MODEL, D_MODEL), BF16),
                   jax.ShapeDtypeStruct((8, D_MODEL), F32)],
        scratch_shapes=[pltpu.VMEM((D_MODEL, D_MODEL), F32)],
        compiler_params=_cparams(dimension_semantics=("arbitrary",)),
    )(mix_a, mix_b, x, tgt, gate, w_out)


def _in_proj_bwd(dq, dkv, dga, dua, dug, dgb, x, dout, s1, shift, nw, wt_full):
    t = x.shape[0]
    tm = 256
    nstep = t // tm

    def body(dq_ref, dkv_ref, dga_ref, dua_ref, dug_ref, dgb_ref, x_ref, dout_ref, s1_ref, sh_ref, nw_ref, w_ref,
             gx_ref, g4_ref, red_ref, acc_ref):
        i = pl.program_id(0)

        @pl.when(i == 0)
        def _():
            acc_ref[...] = jnp.zeros_like(acc_ref)
            red_ref[...] = jnp.zeros_like(red_ref)

        xv = x_ref[...]
        r = lax.rsqrt(jnp.mean(xv * xv, axis=-1, keepdims=True) + EPS)
        xh = xv * r
        n = xh * nw_ref[...]
        h = (n * s1_ref[...] + sh_ref[...]).astype(BF16)
        dproj = jnp.concatenate([dq_ref[...], dkv_ref[...], dga_ref[...], dua_ref[...], dug_ref[...], dgb_ref[...]], axis=1)
        dh = jnp.dot(dproj, w_ref[...], preferred_element_type=F32)
        acc_ref[...] += lax.dot_general(dproj, h, (((0,), (0,)), ((), ())), preferred_element_type=F32)
        red_ref[0:1, :] += jnp.sum(dh, axis=0, keepdims=True)
        red_ref[1:2, :] += jnp.sum(dh * n, axis=0, keepdims=True)
        dn = dh * s1_ref[...]
        red_ref[2:3, :] += jnp.sum(dn * xh, axis=0, keepdims=True)
        dxh = dn * nw_ref[...]
        gx_ref[...] = dout_ref[...] + r * (dxh - xh * jnp.mean(dxh * xh, axis=-1, keepdims=True))

        @pl.when(i == nstep - 1)
        def _():
            g4_ref[...] = acc_ref[...].astype(BF16)

    row = lambda w: pl.BlockSpec((tm, w), lambda i: (i, 0))
    vec = pl.BlockSpec((1, D_MODEL), lambda i: (0, 0))
    return pl.pallas_call(
        body,
        name="in_proj_bwd",
        grid=(nstep,),
        in_specs=[row(512), row(256), row(512), row(512), row(512), row(512), row(D_MODEL), row(D_MODEL), vec, vec, vec,
                  pl.BlockSpec((IN_W, D_MODEL), lambda i: (0, 0), pipeline_mode=pl.Buffered(1))],
        out_specs=[row(D_MODEL), pl.BlockSpec((IN_W, D_MODEL), lambda i: (0, 0)),
                   pl.BlockSpec((8, D_MODEL), lambda i: (0, 0))],
        out_shape=[jax.ShapeDtypeStruct((t, D_MODEL), F32), jax.ShapeDtypeStruct((IN_W, D_MODEL), BF16),
                   jax.ShapeDtypeStruct((8, D_MODEL), F32)],
        scratch_shapes=[pltpu.VMEM((IN_W, D_MODEL), F32)],
        compiler_params=_cparams(dimension_semantics=("arbitrary",)),
    )(dq, dkv, dga, dua, dug, dgb, x, dout, s1, shift, nw, wt_full)


MESH = pl.DeviceIdType.MESH


def _place():
    x, y, c = lax.axis_index("x"), lax.axis_index("y"), lax.axis_index("c")
    chips = [(1 - x, y), (x, 1 - y), (1 - x, 1 - y)]
    return x, y, c, chips


def _remote(sems_s, sems_r, k, src, dst, to):
    return pltpu.make_async_remote_copy(src_ref=src, dst_ref=dst, send_sem=sems_s.at[k], recv_sem=sems_r.at[k],
                                        device_id=to, device_id_type=MESH)


def _gather_weights(wt, wo, cw, c_row, w_ada, b_sh):
    n_sem = 25

    def body(wt_ref, wo_ref, cw_ref, c_ref, wada_ref, bsh_ref,
             w4_ref, wo4_ref, cw4_ref, call_ref, ada_ref, part_ref, ssem, rsem):
        x, y, c, chips = _place()
        j = 2 * x + y
        dev = 2 * j + c
        sib = (x, y, 1 - c)
        idx = [2 * cx + cy for cx, cy in chips]
        rc = functools.partial(_remote, ssem, rsem)

        w4_ref[j] = wt_ref[...].astype(BF16)
        wo4_ref[j] = wo_ref[...].astype(BF16)
        cw4_ref[j] = cw_ref[...]
        call_ref[dev] = c_ref[...]

        sends = []
        peers = [(px, py, pc) for px in (x, 1 - x) for py in (y, 1 - y) for pc in (c, 1 - c)][1:]
        for k, peer in enumerate(peers):
            sends.append(rc(k, call_ref.at[dev], call_ref.at[dev], peer))
        for k, chip in enumerate(chips):
            to = (*chip, c)
            sends.append(rc(7 + k, w4_ref.at[j, c], w4_ref.at[j, c], to))
            sends.append(rc(13 + k, wo4_ref.at[j, c], wo4_ref.at[j, c], to))
            sends.append(rc(19 + k, cw4_ref.at[j], cw4_ref.at[j], to))
        for cp in sends:
            cp.start()

        for k, (px, py, pc) in enumerate(peers):
            pdev = 4 * px + 2 * py + pc
            rc(k, call_ref.at[pdev], call_ref.at[pdev], (px, py, pc)).wait_recv()
        rowid = lax.broadcasted_iota(jnp.int32, (N_DEV, D_MODEL), 0)
        call = jnp.zeros((N_DEV, D_MODEL), F32)
        for r in range(N_DEV):
            call = jnp.where(rowid == r, jnp.broadcast_to(call_ref[r], (N_DEV, D_MODEL)), call)
        part = jnp.dot(_silu(call).astype(BF16), wada_ref[...].astype(BF16), preferred_element_type=F32) + bsh_ref[...]
        for r in range(N_DEV):
            part_ref[r] = part[r:r + 1, :]
        ada_ref[j] = part_ref[dev]
        rows_out = []
        for k, chip in enumerate(chips):
            rows_out.append(rc(22 + k, part_ref.at[2 * idx[k] + c], ada_ref.at[j], (*chip, c)))
            rows_out[-1].start()

        passed = []
        for k, chip in enumerate(chips):
            jk = idx[k]
            rc(7 + k, w4_ref.at[jk, c], w4_ref.at[jk, c], sib).wait_recv()
            passed.append(rc(10 + k, w4_ref.at[jk, c], w4_ref.at[jk, c], sib))
            passed[-1].start()
            rc(13 + k, wo4_ref.at[jk, c], wo4_ref.at[jk, c], sib).wait_recv()
            passed.append(rc(16 + k, wo4_ref.at[jk, c], wo4_ref.at[jk, c], sib))
            passed[-1].start()
        for k, chip in enumerate(chips):
            jk = idx[k]
            rc(10 + k, w4_ref.at[jk, 1 - c], w4_ref.at[jk, 1 - c], sib).wait_recv()
            rc(16 + k, wo4_ref.at[jk, 1 - c], wo4_ref.at[jk, 1 - c], sib).wait_recv()
            rc(19 + k, cw4_ref.at[jk], cw4_ref.at[jk], sib).wait_recv()
            rc(22 + k, ada_ref.at[jk], ada_ref.at[jk], sib).wait_recv()
        for cp in sends + rows_out + passed:
            cp.wait_send()

    vm = pl.BlockSpec(memory_space=pltpu.VMEM)
    return pl.pallas_call(
        body,
        name="gather_weights",
        in_specs=[vm] * 6,
        out_specs=[vm] * 5,
        out_shape=[jax.ShapeDtypeStruct((N_CHIPS, 2, IN_HALF, D_MODEL), BF16),
                   jax.ShapeDtypeStruct((N_CHIPS, 2, OUT_HALF, D_MODEL), BF16),
                   jax.ShapeDtypeStruct((N_CHIPS, 32, 128), F32),
                   jax.ShapeDtypeStruct((N_DEV, 1, D_MODEL), F32),
                   jax.ShapeDtypeStruct((N_CHIPS, 1, ADA_SHARD), F32)],
        scratch_shapes=[pltpu.VMEM((N_DEV, 1, ADA_SHARD), F32),
                        pltpu.SemaphoreType.DMA((n_sem,)), pltpu.SemaphoreType.DMA((n_sem,))],
        compiler_params=_cparams(),
    )(wt, wo, cw, c_row, w_ada, b_sh)


def _reduce_grads(g4, go, small):
    n_sem = 17
    ch = 32

    def body(g4_ref, go_ref, sm_ref, gw_ref, gwo_ref, ssum_ref, sall_ref,
             sib4_ref, sibo_ref, out4_ref, outo_ref, in4_ref, ino_ref, ssem, rsem):
        x, y, c, chips = _place()
        j = 2 * x + y
        dev = 2 * j + c
        sib = (x, y, 1 - c)
        idx = [2 * cx + cy for cx, cy in chips]
        rc = functools.partial(_remote, ssem, rsem)
        big = [(g4_ref, sib4_ref, out4_ref, in4_ref, gw_ref, IN_HALF, 7),
               (go_ref, sibo_ref, outo_ref, ino_ref, gwo_ref, OUT_HALF, 12)]

        sall_ref[dev] = sm_ref[...]
        sends = []
        peers = [(px, py, pc) for px in (x, 1 - x) for py in (y, 1 - y) for pc in (c, 1 - c)][1:]
        for k, peer in enumerate(peers):
            sends.append(rc(k, sall_ref.at[dev], sall_ref.at[dev], peer))
        for g_ref, sib_ref, _, _, _, _, s0 in big:
            sends.append(rc(s0, g_ref.at[:, 1 - c], sib_ref, sib))
        for cp in sends:
            cp.start()

        traded = []
        for g_ref, sib_ref, out_ref, in_ref, _, rows, s0 in big:
            rc(s0, g_ref.at[:, 1 - c], sib_ref, sib).wait_recv()
            for k, chip in enumerate(chips):
                jk = idx[k]

                def add(i, carry, jk=jk, k=k, g_ref=g_ref, sib_ref=sib_ref, out_ref=out_ref):
                    rr = pl.ds(pl.multiple_of(i * ch, ch), ch)
                    out_ref[k, rr, :] = (g_ref[jk, c, rr, :].astype(F32) + sib_ref[jk, rr, :].astype(F32)).astype(BF16)
                    return carry

                lax.fori_loop(0, rows // ch, add, 0)
                traded.append(rc(s0 + 1 + k, out_ref.at[k], in_ref.at[k], (*chip, c)))
                traded[-1].start()

        final = []
        for g_ref, sib_ref, out_ref, in_ref, res_ref, rows, s0 in big:
            for k in range(3):
                rc(s0 + 1 + k, out_ref.at[k], in_ref.at[k], sib).wait_recv()

            def total(i, carry, g_ref=g_ref, sib_ref=sib_ref, in_ref=in_ref, res_ref=res_ref):
                rr = pl.ds(pl.multiple_of(i * ch, ch), ch)
                acc = g_ref[j, c, rr, :].astype(F32) + sib_ref[j, rr, :].astype(F32)
                for k in range(3):
                    acc = acc + in_ref[k, rr, :].astype(F32)
                res_ref[c, rr, :] = acc
                return carry

            lax.fori_loop(0, rows // ch, total, 0)
            final.append(rc(s0 + 4, res_ref.at[c], res_ref.at[c], sib))
            final[-1].start()

        for k, (px, py, pc) in enumerate(peers):
            pdev = 4 * px + 2 * py + pc
            rc(k, sall_ref.at[pdev], sall_ref.at[pdev], (px, py, pc)).wait_recv()
        tot = sall_ref[0]
        for d in range(1, N_DEV):
            tot = tot + sall_ref[d]
        ssum_ref[...] = tot

        for _, _, _, _, res_ref, _, s0 in big:
            rc(s0 + 4, res_ref.at[1 - c], res_ref.at[1 - c], sib).wait_recv()
        for cp in sends + traded + final:
            cp.wait_send()

    vm = pl.BlockSpec(memory_space=pltpu.VMEM)
    halves = (IN_HALF, OUT_HALF)
    return pl.pallas_call(
        body,
        name="reduce_grads",
        in_specs=[vm] * 3,
        out_specs=[vm] * 4,
        out_shape=[jax.ShapeDtypeStruct((2, r, D_MODEL), F32) for r in halves]
        + [jax.ShapeDtypeStruct((SMALL_ROWS, 512), F32), jax.ShapeDtypeStruct((N_DEV, SMALL_ROWS, 512), F32)],
        scratch_shapes=[pltpu.VMEM((N_CHIPS, r, D_MODEL), BF16) for r in halves]
        + [pltpu.VMEM((3, r, D_MODEL), BF16) for r in halves] + [pltpu.VMEM((3, r, D_MODEL), BF16) for r in halves]
        + [pltpu.SemaphoreType.DMA((n_sem,)), pltpu.SemaphoreType.DMA((n_sem,))],
        compiler_params=_cparams(),
    )(g4, go, small)


def _adamw_math(w, g, m, v):
    m2 = ADAM_B1 * m + (1.0 - ADAM_B1) * g
    v2 = ADAM_B2 * v + (1.0 - ADAM_B2) * (g * g)
    m_hat = m2 / (1.0 - ADAM_B1 ** ADAM_STEP)
    v_hat = v2 / (1.0 - ADAM_B2 ** ADAM_STEP)
    delta = -ADAM_LR * (m_hat / (jnp.sqrt(v_hat) + ADAM_EPS) + ADAM_WD * w)
    return delta, m2, v2


def _adamw(name, w, g, m, v, tm):
    r, cdim = w.shape

    def body(w_ref, g_ref, m_ref, v_ref, d_ref, m2_ref, v2_ref):
        d_ref[...], m2_ref[...], v2_ref[...] = _adamw_math(w_ref[...], g_ref[...], m_ref[...], v_ref[...])

    blk = pl.BlockSpec((tm, cdim), lambda i: (i, 0))
    return pl.pallas_call(
        body,
        name=name,
        grid=(r // tm,),
        in_specs=[blk] * 4,
        out_specs=[blk] * 3,
        out_shape=[jax.ShapeDtypeStruct((r, cdim), F32)] * 3,
        compiler_params=_cparams(dimension_semantics=("arbitrary",)),
    )(w, g, m, v)


def _adamw_ada(w, m, v, cact_t, dcols):
    r, cdim = w.shape
    tm = 256

    def body(w_ref, m_ref, v_ref, ct_ref, dc_ref, g_ref, d_ref, m2_ref, v2_ref):
        g = jnp.dot(ct_ref[...], dc_ref[...], preferred_element_type=F32, precision=lax.Precision.HIGHEST)
        g_ref[...] = g
        d_ref[...], m2_ref[...], v2_ref[...] = _adamw_math(w_ref[...], g, m_ref[...], v_ref[...])

    blk = pl.BlockSpec((tm, cdim), lambda i: (i, 0))
    return pl.pallas_call(
        body,
        name="adamw_w_ada",
        grid=(r // tm,),
        in_specs=[blk] * 3 + [pl.BlockSpec((tm, N_DEV), lambda i: (i, 0)), pl.BlockSpec((N_DEV, cdim), lambda i: (0, 0))],
        out_specs=[blk] * 4,
        out_shape=[jax.ShapeDtypeStruct((r, cdim), F32)] * 4,
        compiler_params=_cparams(dimension_semantics=("arbitrary",)),
    )(w, m, v, cact_t, dcols)


def _adamw_small(ws, gs, ms, vs):
    n = len(ws)

    def body(*refs):
        w_r, g_r, m_r, v_r = refs[0:n], refs[n:2 * n], refs[2 * n:3 * n], refs[3 * n:4 * n]
        d_r, m2_r, v2_r = refs[4 * n:5 * n], refs[5 * n:6 * n], refs[6 * n:7 * n]
        for i in range(n):
            d_r[i][...], m2_r[i][...], v2_r[i][...] = _adamw_math(w_r[i][...], g_r[i][...], m_r[i][...], v_r[i][...])

    vm = pl.BlockSpec(memory_space=pltpu.VMEM)
    shapes = [jax.ShapeDtypeStruct(w.shape, F32) for w in ws]
    out = pl.pallas_call(
        body,
        name="adamw_small",
        in_specs=[vm] * (4 * n),
        out_specs=[vm] * (3 * n),
        out_shape=shapes * 3,
        compiler_params=_cparams(),
    )(*ws, *gs, *ms, *vs)
    return out[0:n], out[n:2 * n], out[2 * n:3 * n]


def _rope_tables(t):
    inv = ROPE_THETA ** (-jnp.arange(0, HEAD_DIM, 2, dtype=F32) / HEAD_DIM)
    ang = jnp.arange(t, dtype=F32)[:, None] * inv[None, :]
    cos, sin = jnp.cos(ang), jnp.sin(ang)
    return jnp.tile(cos, (1, 4)), jnp.tile(jnp.concatenate([-sin, sin], axis=1), (1, 2))


def _pad_lanes(v, width):
    return jnp.pad(v, ((0, 0), (0, width - v.shape[1])))


def kernel(x, c, w_ada, b_ada, norm_w, w_in, q_norm_w, k_norm_w, sinks, conv_w, conv_b, ln_w, ln_b, w_out, loss_target, m_w_ada, m_b_ada, m_norm_w, m_w_in, m_q_norm_w, m_k_norm_w, m_sinks, m_conv_w, m_conv_b, m_ln_w, m_ln_b, m_w_out, v_w_ada, v_b_ada, v_norm_w, v_w_in, v_q_norm_w, v_k_norm_w, v_sinks, v_conv_w, v_conv_b, v_ln_w, v_ln_b, v_w_out):
    xi, yi = lax.axis_index("x"), lax.axis_index("y")
    j = 2 * xi + yi
    x2, tgt = x[0], loss_target[0]
    t = x2.shape[0]

    wt_s, mt_s, vt_s = w_in[0].T, m_w_in[0].T, v_w_in[0].T
    cw_pad = jnp.pad(conv_w[0], ((0, 1), (0, 0)))
    b_sh = lax.dynamic_slice(b_ada, (0, ADA_SHARD * j), (1, ADA_SHARD))

    w4, wo4, cw4, call, ada4 = _gather_weights(wt_s.reshape(2, IN_HALF, D_MODEL), w_out[0].reshape(2, OUT_HALF, D_MODEL),
                                               cw_pad, c, w_ada[0], b_sh)
    w_full = w4.reshape(IN_W, D_MODEL)
    w_out_full = wo4.reshape(D_MODEL, D_MODEL)
    cw_full = jnp.concatenate([cw4[i] for i in range(N_CHIPS)], axis=1)
    ada = ada4.reshape(1, 3 * D_MODEL)
    shift, s1, gate = ada[:, :D_MODEL], 1.0 + ada[:, D_MODEL:2 * D_MODEL], ada[:, 2 * D_MODEL:]

    cos_f, sin_s = _rope_tables(t)
    qw2, kw2 = jnp.tile(q_norm_w, (1, 2)), jnp.tile(k_norm_w, (1, 2))

    q_raw, kv_raw, ga, ua, ug, gb = _in_proj(x2, s1, shift, norm_w, w_full)
    o, mix_a = _attn_fwd(q_raw, kv_raw, ga, qw2, kw2, sinks, cos_f, sin_s)
    cz, mix_b = _conv_fwd(ua, ug, gb, cw_full, conv_b, ln_w, ln_b)
    dout, dmix_a, dmix_b, gwo_bf, red_o = _out_proj(mix_a, mix_b, x2, tgt, gate, w_out_full)

    dq, dkv, dga, sm_a = _attn_bwd(q_raw, kv_raw, ga, o, dmix_a, qw2, kw2, sinks, cos_f, sin_s)
    dua, dug, dgb, dcw, dvec = _conv_bwd(ua, ug, gb, cz, dmix_b, cw_full, ln_w, ln_b)
    grad_x, g4, red_i = _in_proj_bwd(dq, dkv, dga, dua, dug, dgb, x2, dout, s1, shift, norm_w, w_full)

    small = jnp.concatenate([
        dcw, dvec,
        jnp.concatenate([red_i[2:3], red_i[0:1], red_i[1:2], red_o[0:1]], axis=1).reshape(8, 512),
        _pad_lanes(sm_a, 512),
        jnp.pad(red_o[1:2].reshape(2, 512), ((0, 6), (0, 0)))], axis=0)
    gw, gwo, ssum, sall = _reduce_grads(g4.reshape(N_CHIPS, 2, IN_HALF, D_MODEL),
                                        gwo_bf.reshape(N_CHIPS, 2, OUT_HALF, D_MODEL), small)

    loss = (0.5 / D_MODEL) * jnp.sum(ssum[56:58])
    gt_w_in = gw.reshape(2 * IN_HALF, D_MODEL)
    g_w_out = gwo.reshape(D_MODEL // N_CHIPS, D_MODEL)
    g_conv_w = lax.dynamic_slice(ssum, (0, 128 * j), (CONV_TAPS, 128))
    g_conv_b, g_ln_w, g_ln_b = ssum[32:33], ssum[33:34], ssum[34:35]
    g_norm_w = ssum[40:42].reshape(1, D_MODEL)
    g_b_ada = ssum[42:48].reshape(1, 3 * D_MODEL)
    g_qw, g_kw, g_sinks = ssum[48:49, 0:HEAD_DIM], ssum[49:50, 0:HEAD_DIM], ssum[50:51, 0:8]
    d_ada_all = sall[:, 42:48, :].reshape(N_DEV, 3 * D_MODEL)
    dcols = lax.dynamic_slice(d_ada_all, (0, ADA_SHARD * j), (N_DEV, ADA_SHARD))
    cact_t = jax.nn.silu(call.reshape(N_DEV, D_MODEL)).T

    g_w_ada, d_w_ada, nm_w_ada, nv_w_ada = _adamw_ada(w_ada[0], m_w_ada[0], v_w_ada[0], cact_t, dcols)
    dt_w_in, nmt_w_in, nvt_w_in = _adamw("adamw_w_in", wt_s, gt_w_in, mt_s, vt_s, 176)
    g_w_in, d_w_in, nm_w_in, nv_w_in = gt_w_in.T, dt_w_in.T, nmt_w_in.T, nvt_w_in.T
    d_w_out, nm_w_out, nv_w_out = _adamw("adamw_w_out", w_out[0], g_w_out, m_w_out[0], v_w_out[0], 128)
    ws = [b_ada, norm_w, q_norm_w, k_norm_w, sinks, conv_w[0], conv_b, ln_w, ln_b]
    gs = [g_b_ada, g_norm_w, g_qw, g_kw, g_sinks, g_conv_w, g_conv_b, g_ln_w, g_ln_b]
    ms = [m_b_ada, m_norm_w, m_q_norm_w, m_k_norm_w, m_sinks, m_conv_w[0], m_conv_b, m_ln_w, m_ln_b]
    vs = [v_b_ada, v_norm_w, v_q_norm_w, v_k_norm_w, v_sinks, v_conv_w[0], v_conv_b, v_ln_w, v_ln_b]
    ds, nms, nvs = _adamw_small(ws, gs, ms, vs)

    def order(ada_v, in_v, out_v, sm):
        b, nw_, qw_, kw_, sk_, cw_, cb_, lw_, lb_ = sm
        return [ada_v[None], b, nw_, in_v[None], qw_, kw_, sk_, cw_[None], cb_, lw_, lb_, out_v[None]]

    grads = order(g_w_ada, g_w_in, g_w_out, gs)
    deltas = order(d_w_ada, d_w_in, d_w_out, ds)
    new_m = order(nm_w_ada, nm_w_in, nm_w_out, nms)
    new_v = order(nv_w_ada, nv_w_in, nv_w_out, nvs)
    return (loss, grad_x[None], *grads, *deltas, *new_m, *new_v)
```

```python
import functools

import jax
import jax.numpy as jnp
from jax import lax
from jax.experimental import pallas as pl
from jax.experimental.pallas import tpu as pltpu

F32 = jnp.float32
BF16 = jnp.bfloat16

D_MODEL = 1024
ATTN_W = 512
KV_W = 128
CONV_W = 512
IN_W = 2816
HEAD_DIM = 64
CONV_TAPS = 31
QBLK = 128
EPS = 1e-6
ROPE_THETA = 10000.0

ADAM_LR = 0.001
ADAM_B1 = 0.9
ADAM_B2 = 0.999
ADAM_EPS = 1e-08
ADAM_WD = 0.01
ADAM_STEP = 10

N_CHIPS = 4
N_DEV = 8
IN_HALF = IN_W // N_CHIPS // 2
OUT_HALF = D_MODEL // N_CHIPS // 2
ADA_SHARD = 3 * D_MODEL // N_CHIPS

VMEM_LIMIT = 56 * 1024 * 1024
CONV_PAD = 32
SMALL_ROWS = 64


def _cparams(**kw):
    return pltpu.CompilerParams(vmem_limit_bytes=VMEM_LIMIT, **kw)


def _sigmoid(v):
    return 1.0 / (1.0 + jnp.exp(-v))


def _silu(v):
    return v * _sigmoid(v)


def _dsilu(v):
    s = _sigmoid(v)
    return s * (1.0 + v * (1.0 - s))


def _lane(shape):
    return lax.broadcasted_iota(jnp.int32, shape, len(shape) - 1)


def _in_proj(x, s1, shift, nw, wt_full):
    t = x.shape[0]
    tm = 256

    def body(x_ref, s1_ref, sh_ref, nw_ref, w_ref, q_ref, kv_ref, ga_ref, ua_ref, ug_ref, gb_ref):
        xv = x_ref[...]
        r = lax.rsqrt(jnp.mean(xv * xv, axis=-1, keepdims=True) + EPS)
        h = (xv * r) * nw_ref[...] * s1_ref[...] + sh_ref[...]
        p = lax.dot_general(h.astype(BF16), w_ref[...], (((1,), (1,)), ((), ())), preferred_element_type=F32)
        q_ref[...] = p[:, 0:512]
        kv_ref[...] = p[:, 512:768]
        ga_ref[...] = p[:, 768:1280]
        ua_ref[...] = p[:, 1280:1792]
        ug_ref[...] = p[:, 1792:2304]
        gb_ref[...] = p[:, 2304:2816]

    row = lambda w: pl.BlockSpec((tm, w), lambda i: (i, 0))
    vec = pl.BlockSpec((1, D_MODEL), lambda i: (0, 0))
    return pl.pallas_call(
        body,
        name="in_proj",
        grid=(t // tm,),
        in_specs=[row(D_MODEL), vec, vec, vec,
                  pl.BlockSpec((IN_W, D_MODEL), lambda i: (0, 0), pipeline_mode=pl.Buffered(1))],
        out_specs=[row(512), row(256), row(512), row(512), row(512), row(512)],
        out_shape=[jax.ShapeDtypeStruct((t, w), F32) for w in (512, 256, 512, 512, 512, 512)],
        compiler_params=_cparams(dimension_semantics=("arbitrary",)),
    )(x, s1, shift, nw, wt_full)


def _head_mean(s, left):
    sl = jnp.sum(jnp.where(left, s, 0.0), axis=-1, keepdims=True)
    sr = jnp.sum(jnp.where(left, 0.0, s), axis=-1, keepdims=True)
    return jnp.where(left, sl, sr) * (1.0 / HEAD_DIM)


def _rot(v, first):
    return jnp.where(first, pltpu.roll(v, 96, 1), pltpu.roll(v, 32, 1))


def _norm_rope(v, w, cos, sin_s, left, first):
    r = lax.rsqrt(_head_mean(v * v, left) + EPS)
    xh = v * r
    n = xh * w
    return n * cos + _rot(n, first) * sin_s, xh, r


def _norm_rope_bwd(d, xh, r, w, cos, sin_s, left, first):
    dn = d * cos - _rot(d, first) * sin_s
    dw = jnp.sum(dn * xh, axis=0, keepdims=True)
    dxh = dn * w
    return r * (dxh - xh * _head_mean(dxh * xh, left)), dw


def _dup_heads(v, left):
    sw = pltpu.roll(v, 64, 1)
    return jnp.where(left, v, sw), jnp.where(left, sw, v)


def _prep_kv(kv_ref, kw_ref, cos_ref, sin_ref, ka_ref, va_ref, t):
    ch = 256
    for g in range(2):
        ka_ref[g, 0:QBLK, :] = jnp.zeros((QBLK, 128), BF16)
        va_ref[g, 0:QBLK, :] = jnp.zeros((QBLK, 128), BF16)

    def chunk(i, carry):
        r0 = pl.multiple_of(i * ch, ch)
        left = _lane((ch, 128)) < 64
        first = (_lane((ch, 128)) % 64) < 32
        k = kv_ref[pl.ds(r0, ch), 0:128]
        v = kv_ref[pl.ds(r0, ch), 128:256]
        kr, _, _ = _norm_rope(k, kw_ref[...], cos_ref[pl.ds(r0, ch), :], sin_ref[pl.ds(r0, ch), :], left, first)
        k0, k1 = _dup_heads(kr, left)
        v0, v1 = _dup_heads(v, left)
        ka_ref[0, pl.ds(QBLK + r0, ch), :] = k0.astype(BF16)
        ka_ref[1, pl.ds(QBLK + r0, ch), :] = k1.astype(BF16)
        va_ref[0, pl.ds(QBLK + r0, ch), :] = v0.astype(BF16)
        va_ref[1, pl.ds(QBLK + r0, ch), :] = v1.astype(BF16)
        return carry

    lax.fori_loop(0, t // ch, chunk, 0)


def _band_mask(n):
    qi = lax.broadcasted_iota(jnp.int32, (2 * QBLK, 2 * QBLK), 0) % QBLK
    kj = lax.broadcasted_iota(jnp.int32, (2 * QBLK, 2 * QBLK), 1)
    local = (kj > qi) & (kj <= qi + QBLK)
    return local & ((n > 0) | (kj >= QBLK))


def _softmax_pair(s, mask, sink0, sink1):
    row = lax.broadcasted_iota(jnp.int32, (2 * QBLK, 1), 0)
    sink = jnp.where(row < QBLK, sink0, sink1)
    s = jnp.where(mask, s, -jnp.inf)
    m = jnp.maximum(jnp.max(s, axis=-1, keepdims=True), sink)
    e = jnp.exp(s - m)
    es = jnp.exp(sink - m)
    inv = 1.0 / (jnp.sum(e, axis=-1, keepdims=True) + es)
    return e * inv, es * inv


def _stack_heads(v, left):
    return jnp.concatenate([jnp.where(left, v, 0.0), jnp.where(left, 0.0, v)], axis=0)


def _attn_fwd(q_raw, kv_raw, ga, qw2, kw2, sinks, cos_f, sin_s, wo, cw):
    t = q_raw.shape[0]
    nblk = t // QBLK

    def body(q_ref, kv_ref, ga_ref, qw_ref, kw_ref, sk_ref, cos_ref, sin_ref, wo_ref, cw_ref,
             o_ref, mix_ref, wo4_ref, cw4_ref, ka_ref, va_ref, ssem, rsem):
        x, y, c, chips = _place()
        j = 2 * x + y
        sib = (x, y, 1 - c)
        idx = [2 * cx + cy for cx, cy in chips]
        rc = functools.partial(_remote, ssem, rsem)
        wo4_ref[j] = wo_ref[...].astype(BF16)
        cw4_ref[j] = cw_ref[...]
        sends = []
        for k, chip in enumerate(chips):
            sends.append(rc(k, wo4_ref.at[j, c], wo4_ref.at[j, c], (*chip, c)))
            sends.append(rc(6 + k, cw4_ref.at[j], cw4_ref.at[j], (*chip, c)))
        for cp in sends:
            cp.start()

        _prep_kv(kv_ref, kw_ref, cos_ref, sin_ref, ka_ref, va_ref, t)

        def blk(n, carry):
            r0 = pl.multiple_of(n * QBLK, QBLK)
            left = _lane((QBLK, 128)) < 64
            first = (_lane((QBLK, 128)) % 64) < 32
            cos = cos_ref[pl.ds(r0, QBLK), :]
            sin = sin_ref[pl.ds(r0, QBLK), :]
            mask = _band_mask(n)
            for p in range(4):
                g = p // 2
                lanes = slice(p * 128, (p + 1) * 128)
                qr, _, _ = _norm_rope(q_ref[pl.ds(r0, QBLK), lanes], qw_ref[...], cos, sin, left, first)
                q2 = _stack_heads(qr * 0.125, left).astype(BF16)
                s = lax.dot_general(q2, ka_ref[g, pl.ds(r0, 2 * QBLK), :], (((1,), (1,)), ((), ())),
                                    preferred_element_type=F32)
                pm, _ = _softmax_pair(s, mask, sk_ref[0, 2 * p], sk_ref[0, 2 * p + 1])
                o2 = jnp.dot(pm.astype(BF16), va_ref[g, pl.ds(r0, 2 * QBLK), :], preferred_element_type=F32)
                o = jnp.where(left, o2[0:QBLK], o2[QBLK:2 * QBLK])
                o_ref[pl.ds(r0, QBLK), lanes] = o
                mix_ref[pl.ds(r0, QBLK), lanes] = (o * _silu(ga_ref[pl.ds(r0, QBLK), lanes])).astype(BF16)
            return carry

        lax.fori_loop(0, nblk, blk, 0)

        passed = []
        for k, chip in enumerate(chips):
            jk = idx[k]
            rc(k, wo4_ref.at[jk, c], wo4_ref.at[jk, c], sib).wait_recv()
            passed.append(rc(3 + k, wo4_ref.at[jk, c], wo4_ref.at[jk, c], sib))
            passed[-1].start()
        for k, chip in enumerate(chips):
            jk = idx[k]
            rc(3 + k, wo4_ref.at[jk, 1 - c], wo4_ref.at[jk, 1 - c], sib).wait_recv()
            rc(6 + k, cw4_ref.at[jk], cw4_ref.at[jk], sib).wait_recv()
        for cp in sends + passed:
            cp.wait_send()

    vm = pl.BlockSpec(memory_space=pltpu.VMEM)
    n_sem = 9
    return pl.pallas_call(
        body,
        name="attn_fwd",
        in_specs=[vm, vm, vm, vm, vm, pl.BlockSpec(memory_space=pltpu.SMEM), vm, vm, vm, vm],
        out_specs=[vm] * 4,
        out_shape=[jax.ShapeDtypeStruct((t, ATTN_W), F32), jax.ShapeDtypeStruct((t, ATTN_W), BF16),
                   jax.ShapeDtypeStruct((N_CHIPS, 2, OUT_HALF, D_MODEL), BF16),
                   jax.ShapeDtypeStruct((N_CHIPS, 32, 128), F32)],
        scratch_shapes=[pltpu.VMEM((2, t + QBLK, 128), BF16), pltpu.VMEM((2, t + QBLK, 128), BF16),
                        pltpu.SemaphoreType.DMA((n_sem,)), pltpu.SemaphoreType.DMA((n_sem,))],
        compiler_params=_cparams(),
    )(q_raw, kv_raw, ga, qw2, kw2, sinks, cos_f, sin_s, wo, cw)


def _attn_bwd(q_raw, kv_raw, ga, o, dmix, qw2, kw2, sinks, cos_f, sin_s, go):
    t = q_raw.shape[0]
    nblk = t // QBLK

    def body(q_ref, kv_ref, ga_ref, o_ref, dm_ref, qw_ref, kw_ref, sk_ref, cos_ref, sin_ref, go_ref,
             dq_ref, dkv_ref, dga_ref, sm_ref, gwo_ref, ka_ref, va_ref, dka_ref, dva_ref,
             sibo_ref, outo_ref, ino_ref, ssem, rsem):
        x, y, c, chips = _place()
        sib = (x, y, 1 - c)
        rc = functools.partial(_remote, ssem, rsem)
        sends = [_rs_to_sibling(rc, 0, go_ref, sibo_ref, c, sib)]
        _prep_kv(kv_ref, kw_ref, cos_ref, sin_ref, ka_ref, va_ref, t)
        dka_ref[...] = jnp.zeros_like(dka_ref)
        dva_ref[...] = jnp.zeros_like(dva_ref)
        sends += _rs_trade(rc, 0, go_ref, sibo_ref, outo_ref, ino_ref, OUT_HALF, c, sib, chips)

        def blk(n, carry):
            dqw, dsk = carry
            r0 = pl.multiple_of(n * QBLK, QBLK)
            left = _lane((QBLK, 128)) < 64
            first = (_lane((QBLK, 128)) % 64) < 32
            cos = cos_ref[pl.ds(r0, QBLK), :]
            sin = sin_ref[pl.ds(r0, QBLK), :]
            mask = _band_mask(n)
            row = lax.broadcasted_iota(jnp.int32, (2 * QBLK, 1), 0)
            for p in range(4):
                g = p // 2
                lanes = slice(p * 128, (p + 1) * 128)
                rows = pl.ds(r0, QBLK)
                win = pl.ds(r0, 2 * QBLK)
                qr, xh, r = _norm_rope(q_ref[rows, lanes], qw_ref[...], cos, sin, left, first)
                q2 = _stack_heads(qr * 0.125, left).astype(BF16)
                kwin = ka_ref[g, win, :]
                vwin = va_ref[g, win, :]
                s = lax.dot_general(q2, kwin, (((1,), (1,)), ((), ())), preferred_element_type=F32)
                pm, ps = _softmax_pair(s, mask, sk_ref[0, 2 * p], sk_ref[0, 2 * p + 1])
                gav = ga_ref[rows, lanes]
                dmv = dm_ref[rows, lanes]
                dga_ref[rows, lanes] = (dmv * o_ref[rows, lanes] * _dsilu(gav)).astype(BF16)
                do2 = _stack_heads(dmv * _silu(gav), left).astype(BF16)
                dp = lax.dot_general(do2, vwin, (((1,), (1,)), ((), ())), preferred_element_type=F32)
                delta = jnp.sum(pm * dp, axis=-1, keepdims=True)
                ds = (pm * (dp - delta)).astype(BF16)
                pd = ps * delta
                d0 = jnp.sum(jnp.where(row < QBLK, pd, 0.0), axis=0, keepdims=True)
                d1 = jnp.sum(jnp.where(row < QBLK, 0.0, pd), axis=0, keepdims=True)
                l8 = _lane((1, 128))
                dsk = dsk - jnp.where(l8 == 2 * p, d0, 0.0) - jnp.where(l8 == 2 * p + 1, d1, 0.0)
                dva_ref[g, win, :] += lax.dot_general(pm.astype(BF16), do2, (((0,), (0,)), ((), ())),
                                                      preferred_element_type=F32)
                dka_ref[g, win, :] += lax.dot_general(ds, q2, (((0,), (0,)), ((), ())),
                                                      preferred_element_type=F32)
                dq2 = jnp.dot(ds, kwin, preferred_element_type=F32)
                dqr = jnp.where(left, dq2[0:QBLK], dq2[QBLK:2 * QBLK]) * 0.125
                dq, dw = _norm_rope_bwd(dqr, xh, r, qw_ref[...], cos, sin, left, first)
                dq_ref[rows, lanes] = dq.astype(BF16)
                dqw = dqw + dw
            return dqw, dsk

        zero = jnp.zeros((1, 128), F32)
        dqw, dsk = lax.fori_loop(0, nblk, blk, (zero, zero))

        ch = 256

        def chunk(i, dkw):
            r0 = pl.multiple_of(i * ch, ch)
            left = _lane((ch, 128)) < 64
            first = (_lane((ch, 128)) % 64) < 32
            rows = pl.ds(r0, ch)
            prow = pl.ds(QBLK + r0, ch)

            def fold(ref):
                a0 = ref[0, prow, :]
                a1 = ref[1, prow, :]
                return jnp.where(left, a0 + pltpu.roll(a0, 64, 1), a1 + pltpu.roll(a1, 64, 1))

            cos = cos_ref[rows, :]
            sin = sin_ref[rows, :]
            _, xh, r = _norm_rope(kv_ref[rows, 0:128], kw_ref[...], cos, sin, left, first)
            dk, dw = _norm_rope_bwd(fold(dka_ref), xh, r, kw_ref[...], cos, sin, left, first)
            dkv_ref[rows, 0:128] = dk.astype(BF16)
            dkv_ref[rows, 128:256] = fold(dva_ref).astype(BF16)
            return dkw + dw

        dkw = lax.fori_loop(0, t // ch, chunk, zero)
        sm_ref[...] = jnp.zeros((8, 128), F32)
        sm_ref[0:1, :] = dqw + pltpu.roll(dqw, 64, 1)
        sm_ref[1:2, :] = dkw + pltpu.roll(dkw, 64, 1)
        sm_ref[2:3, :] = dsk

        j = 2 * x + y
        sends.append(_rs_total(rc, 0, go_ref, sibo_ref, outo_ref, ino_ref, gwo_ref, OUT_HALF, j, c, sib))
        _rs_done(rc, 0, gwo_ref, c, sib)
        for cp in sends:
            cp.wait_send()

    vm = pl.BlockSpec(memory_space=pltpu.VMEM)
    return pl.pallas_call(
        body,
        name="attn_bwd",
        in_specs=[vm, vm, vm, vm, vm, vm, vm, pl.BlockSpec(memory_space=pltpu.SMEM), vm, vm, vm],
        out_specs=[vm] * 5,
        out_shape=[jax.ShapeDtypeStruct((t, ATTN_W), BF16), jax.ShapeDtypeStruct((t, 2 * KV_W), BF16),
                   jax.ShapeDtypeStruct((t, ATTN_W), BF16), jax.ShapeDtypeStruct((8, 128), F32),
                   jax.ShapeDtypeStruct((2, OUT_HALF, D_MODEL), F32)],
        scratch_shapes=[pltpu.VMEM((2, t + QBLK, 128), BF16), pltpu.VMEM((2, t + QBLK, 128), BF16),
                        pltpu.VMEM((2, t + QBLK, 128), F32), pltpu.VMEM((2, t + QBLK, 128), F32)]
        + _rs_scratch(OUT_HALF) + [pltpu.SemaphoreType.DMA((RS_SEMS,)), pltpu.SemaphoreType.DMA((RS_SEMS,))],
        compiler_params=_cparams(),
    )(q_raw, kv_raw, ga, o, dmix, qw2, kw2, sinks, cos_f, sin_s, go)


CONV_CH = 256
CONV_SUB = 64


def _shifted_windows(src_ref, r0, sh_ref):
    rows = CONV_CH + CONV_PAD
    win = src_ref[pl.ds(r0, rows), :]
    sh_ref[0] = win
    for b in range(1, 8):
        sh_ref[b] = pltpu.roll(win, rows - b, 0)


def _conv_fwd(ua, ug, gb, cw, cb, lw, lb):
    t = ua.shape[0]

    def body(ua_ref, ug_ref, gb_ref, cw_ref, cb_ref, lw_ref, lb_ref, cz_ref, mix_ref, zp_ref, sh_ref):
        zp_ref[0:CONV_PAD, :] = jnp.zeros((CONV_PAD, CONV_W), F32)

        def glu(i, carry):
            r0 = pl.multiple_of(i * CONV_CH, CONV_CH)
            rows = pl.ds(r0, CONV_CH)
            zp_ref[pl.ds(CONV_PAD + r0, CONV_CH), :] = ua_ref[rows, :] * _sigmoid(ug_ref[rows, :])
            return carry

        lax.fori_loop(0, t // CONV_CH, glu, 0)

        def chunk(i, carry):
            r0 = pl.multiple_of(i * CONV_CH, CONV_CH)
            _shifted_windows(zp_ref, r0, sh_ref)
            for c in range(CONV_W // 128):
                lanes = slice(c * 128, (c + 1) * 128)

                def sub(k, carry2):
                    b0 = pl.multiple_of(k * CONV_SUB, CONV_SUB)
                    acc = jnp.broadcast_to(cb_ref[0:1, lanes], (CONV_SUB, 128))
                    for j in range(CONV_TAPS):
                        off = j + CONV_PAD - (CONV_TAPS - 1)
                        acc = acc + sh_ref[off % 8, pl.ds(b0 + 8 * (off // 8), CONV_SUB), lanes] * cw_ref[j:j + 1, lanes]
                    cz_ref[pl.ds(r0 + b0, CONV_SUB), lanes] = acc
                    return carry2

                lax.fori_loop(0, CONV_CH // CONV_SUB, sub, 0)
            rows = pl.ds(r0, CONV_CH)
            cz = cz_ref[rows, :]
            mu = jnp.mean(cz, axis=-1, keepdims=True)
            xc = cz - mu
            rs = lax.rsqrt(jnp.mean(xc * xc, axis=-1, keepdims=True) + EPS)
            ln = xc * rs * lw_ref[...] + lb_ref[...]
            mix_ref[rows, :] = (_silu(ln) * _silu(gb_ref[rows, :])).astype(BF16)
            return carry

        lax.fori_loop(0, t // CONV_CH, chunk, 0)

    vm = pl.BlockSpec(memory_space=pltpu.VMEM)
    return pl.pallas_call(
        body,
        name="conv_fwd",
        in_specs=[vm] * 7,
        out_specs=[vm, vm],
        out_shape=[jax.ShapeDtypeStruct((t, CONV_W), F32), jax.ShapeDtypeStruct((t, CONV_W), BF16)],
        scratch_shapes=[pltpu.VMEM((t + CONV_PAD, CONV_W), F32),
                        pltpu.VMEM((8, CONV_CH + CONV_PAD, CONV_W), F32)],
        compiler_params=_cparams(),
    )(ua, ug, gb, cw, cb, lw, lb)


def _conv_bwd(ua, ug, gb, cz, dmix, cw, lw, lb):
    t = ua.shape[0]

    def body(ua_ref, ug_ref, gb_ref, cz_ref, dm_ref, cw_ref, lw_ref, lb_ref,
             dua_ref, dug_ref, dgb_ref, dcw_ref, dvec_ref, zp_ref, dp_ref, sh_ref, wacc_ref):
        zp_ref[0:CONV_PAD, :] = jnp.zeros((CONV_PAD, CONV_W), F32)
        dp_ref[t:t + CONV_PAD, :] = jnp.zeros((CONV_PAD, CONV_W), F32)
        wacc_ref[...] = jnp.zeros_like(wacc_ref)

        def pointwise(i, carry):
            dcb, dlw, dlb = carry
            r0 = pl.multiple_of(i * CONV_CH, CONV_CH)
            rows = pl.ds(r0, CONV_CH)
            zp_ref[pl.ds(CONV_PAD + r0, CONV_CH), :] = ua_ref[rows, :] * _sigmoid(ug_ref[rows, :])
            cz = cz_ref[rows, :]
            mu = jnp.mean(cz, axis=-1, keepdims=True)
            xc = cz - mu
            rs = lax.rsqrt(jnp.mean(xc * xc, axis=-1, keepdims=True) + EPS)
            xh = xc * rs
            ln = xh * lw_ref[...] + lb_ref[...]
            gbv = gb_ref[rows, :]
            dy = dm_ref[rows, :]
            dgb_ref[rows, :] = (dy * _silu(ln) * _dsilu(gbv)).astype(BF16)
            dl = dy * _silu(gbv) * _dsilu(ln)
            dxh = dl * lw_ref[...]
            dcz = rs * (dxh - jnp.mean(dxh, axis=-1, keepdims=True)
                        - xh * jnp.mean(dxh * xh, axis=-1, keepdims=True))
            dp_ref[rows, :] = dcz
            return (dcb + jnp.sum(dcz, axis=0, keepdims=True),
                    dlw + jnp.sum(dl * xh, axis=0, keepdims=True),
                    dlb + jnp.sum(dl, axis=0, keepdims=True))

        zero = jnp.zeros((1, CONV_W), F32)
        dcb, dlw, dlb = lax.fori_loop(0, t // CONV_CH, pointwise, (zero, zero, zero))
        dvec_ref[...] = jnp.zeros((8, CONV_W), F32)
        dvec_ref[0:1, :] = dcb
        dvec_ref[1:2, :] = dlw
        dvec_ref[2:3, :] = dlb

        def chunk(i, carry):
            r0 = pl.multiple_of(i * CONV_CH, CONV_CH)
            _shifted_windows(dp_ref, r0, sh_ref)
            for c in range(CONV_W // 128):
                lanes = slice(c * 128, (c + 1) * 128)

                def sub(k, carry2):
                    b0 = pl.multiple_of(k * CONV_SUB, CONV_SUB)
                    acc = jnp.zeros((CONV_SUB, 128), F32)
                    for j in range(CONV_TAPS):
                        off = CONV_TAPS - 1 - j
                        acc = acc + sh_ref[off % 8, pl.ds(b0 + 8 * (off // 8), CONV_SUB), lanes] * cw_ref[j:j + 1, lanes]
                    rr = pl.ds(r0 + b0, CONV_SUB)
                    sg = _sigmoid(ug_ref[rr, lanes])
                    dua_ref[rr, lanes] = (acc * sg).astype(BF16)
                    dug_ref[rr, lanes] = (acc * ua_ref[rr, lanes] * sg * (1.0 - sg)).astype(BF16)
                    return carry2

                lax.fori_loop(0, CONV_CH // CONV_SUB, sub, 0)
            _shifted_windows(zp_ref, r0, sh_ref)
            for c in range(CONV_W // 128):
                lanes = slice(c * 128, (c + 1) * 128)

                def subw(k, carry2):
                    b0 = pl.multiple_of(k * CONV_SUB, CONV_SUB)
                    dcz = dp_ref[pl.ds(r0 + b0, CONV_SUB), lanes]
                    for j in range(CONV_TAPS):
                        off = j + CONV_PAD - (CONV_TAPS - 1)
                        pr = dcz * sh_ref[off % 8, pl.ds(b0 + 8 * (off // 8), CONV_SUB), lanes]
                        part = pr[0:8]
                        for q in range(1, CONV_SUB // 8):
                            part = part + pr[8 * q:8 * (q + 1)]
                        wacc_ref[8 * j:8 * (j + 1), lanes] += part
                    return carry2

                lax.fori_loop(0, CONV_CH // CONV_SUB, subw, 0)
            return carry

        lax.fori_loop(0, t // CONV_CH, chunk, 0)
        dcw_ref[...] = jnp.zeros((32, CONV_W), F32)
        for j in range(CONV_TAPS):
            dcw_ref[j:j + 1, :] = jnp.sum(wacc_ref[8 * j:8 * (j + 1), :], axis=0, keepdims=True)

    vm = pl.BlockSpec(memory_space=pltpu.VMEM)
    return pl.pallas_call(
        body,
        name="conv_bwd",
        in_specs=[vm] * 8,
        out_specs=[vm] * 5,
        out_shape=[jax.ShapeDtypeStruct((t, CONV_W), BF16)] * 3
        + [jax.ShapeDtypeStruct((32, CONV_W), F32), jax.ShapeDtypeStruct((8, CONV_W), F32)],
        scratch_shapes=[pltpu.VMEM((t + CONV_PAD, CONV_W), F32), pltpu.VMEM((t + CONV_PAD, CONV_W), F32),
                        pltpu.VMEM((8, CONV_CH + CONV_PAD, CONV_W), F32), pltpu.VMEM((8 * 32, CONV_W), F32)],
        compiler_params=_cparams(),
    )(ua, ug, gb, cz, dmix, cw, lw, lb)


def _out_proj(mix_a, mix_b, x, tgt, gate, w_out):
    t = x.shape[0]
    tm = 256
    nstep = t // tm

    def body(ma_ref, mb_ref, x_ref, t_ref, g_ref, w_ref, dout_ref, dma_ref, dmb_ref, gw_ref, red_ref, acc_ref):
        i = pl.program_id(0)

        @pl.when(i == 0)
        def _():
            acc_ref[...] = jnp.zeros_like(acc_ref)
            red_ref[...] = jnp.zeros_like(red_ref)

        mix = jnp.concatenate([ma_ref[...], mb_ref[...]], axis=1)
        y = jnp.dot(mix, w_ref[...], preferred_element_type=F32)
        gate_v = g_ref[...]
        err = x_ref[...] + gate_v * y - t_ref[...]
        dout = err * (1.0 / D_MODEL)
        dout_ref[...] = dout
        red_ref[0:1, :] += jnp.sum(dout * y, axis=0, keepdims=True)
        red_ref[1:2, :] += jnp.sum(err * err, axis=0, keepdims=True)
        dy = (dout * gate_v).astype(BF16)
        dmix = lax.dot_general(dy, w_ref[...], (((1,), (1,)), ((), ())), preferred_element_type=F32)
        dma_ref[...] = dmix[:, 0:512]
        dmb_ref[...] = dmix[:, 512:1024]
        acc_ref[...] += lax.dot_general(mix, dy, (((0,), (0,)), ((), ())), preferred_element_type=F32)

        @pl.when(i == nstep - 1)
        def _():
            gw_ref[...] = acc_ref[...].astype(BF16)

    row = lambda w: pl.BlockSpec((tm, w), lambda i: (i, 0))
    const = lambda s: pl.BlockSpec(s, lambda i: (0, 0))
    return pl.pallas_call(
        body,
        name="out_proj",
        grid=(nstep,),
        in_specs=[row(512), row(512), row(D_MODEL), row(D_MODEL), const((1, D_MODEL)),
                  pl.BlockSpec((D_MODEL, D_MODEL), lambda i: (0, 0), pipeline_mode=pl.Buffered(1))],
        out_specs=[row(D_MODEL), row(512), row(512), const((D_MODEL, D_MODEL)), const((8, D_MODEL))],
        out_shape=[jax.ShapeDtypeStruct((t, D_MODEL), F32), jax.ShapeDtypeStruct((t, 512), F32),
                   jax.ShapeDtypeStruct((t, 512), F32), jax.ShapeDtypeStruct((D_MODEL, D_MODEL), BF16),
                   jax.ShapeDtypeStruct((8, D_MODEL), F32)],
        scratch_shapes=[pltpu.VMEM((D_MODEL, D_MODEL), F32)],
        compiler_params=_cparams(dimension_semantics=("arbitrary",)),
    )(mix_a, mix_b, x, tgt, gate, w_out)


def _in_proj_bwd(dq, dkv, dga, dua, dug, dgb, x, dout, s1, shift, nw, wt_full):
    t = x.shape[0]
    tm = 256
    nstep = t // tm

    def body(dq_ref, dkv_ref, dga_ref, dua_ref, dug_ref, dgb_ref, x_ref, dout_ref, s1_ref, sh_ref, nw_ref, w_ref,
             gx_ref, g4_ref, red_ref, acc_ref):
        i = pl.program_id(0)

        @pl.when(i == 0)
        def _():
            acc_ref[...] = jnp.zeros_like(acc_ref)
            red_ref[...] = jnp.zeros_like(red_ref)

        xv = x_ref[...]
        r = lax.rsqrt(jnp.mean(xv * xv, axis=-1, keepdims=True) + EPS)
        xh = xv * r
        n = xh * nw_ref[...]
        h = (n * s1_ref[...] + sh_ref[...]).astype(BF16)
        dproj = jnp.concatenate([dq_ref[...], dkv_ref[...], dga_ref[...], dua_ref[...], dug_ref[...], dgb_ref[...]], axis=1)
        dh = jnp.dot(dproj, w_ref[...], preferred_element_type=F32)
        acc_ref[...] += lax.dot_general(dproj, h, (((0,), (0,)), ((), ())), preferred_element_type=F32)
        red_ref[0:1, :] += jnp.sum(dh, axis=0, keepdims=True)
        red_ref[1:2, :] += jnp.sum(dh * n, axis=0, keepdims=True)
        dn = dh * s1_ref[...]
        red_ref[2:3, :] += jnp.sum(dn * xh, axis=0, keepdims=True)
        dxh = dn * nw_ref[...]
        gx_ref[...] = dout_ref[...] + r * (dxh - xh * jnp.mean(dxh * xh, axis=-1, keepdims=True))

        @pl.when(i == nstep - 1)
        def _():
            g4_ref[...] = acc_ref[...].astype(BF16)

    row = lambda w: pl.BlockSpec((tm, w), lambda i: (i, 0))
    vec = pl.BlockSpec((1, D_MODEL), lambda i: (0, 0))
    return pl.pallas_call(
        body,
        name="in_proj_bwd",
        grid=(nstep,),
        in_specs=[row(512), row(256), row(512), row(512), row(512), row(512), row(D_MODEL), row(D_MODEL), vec, vec, vec,
                  pl.BlockSpec((IN_W, D_MODEL), lambda i: (0, 0), pipeline_mode=pl.Buffered(1))],
        out_specs=[row(D_MODEL), pl.BlockSpec((IN_W, D_MODEL), lambda i: (0, 0)),
                   pl.BlockSpec((8, D_MODEL), lambda i: (0, 0))],
        out_shape=[jax.ShapeDtypeStruct((t, D_MODEL), F32), jax.ShapeDtypeStruct((IN_W, D_MODEL), BF16),
                   jax.ShapeDtypeStruct((8, D_MODEL), F32)],
        scratch_shapes=[pltpu.VMEM((IN_W, D_MODEL), F32)],
        compiler_params=_cparams(dimension_semantics=("arbitrary",)),
    )(dq, dkv, dga, dua, dug, dgb, x, dout, s1, shift, nw, wt_full)


MESH = pl.DeviceIdType.MESH


def _place():
    x, y, c = lax.axis_index("x"), lax.axis_index("y"), lax.axis_index("c")
    chips = [(1 - x, y), (x, 1 - y), (1 - x, 1 - y)]
    return x, y, c, chips


def _remote(sems_s, sems_r, k, src, dst, to):
    return pltpu.make_async_remote_copy(src_ref=src, dst_ref=dst, send_sem=sems_s.at[k], recv_sem=sems_r.at[k],
                                        device_id=to, device_id_type=MESH)


RS_CH = 32
RS_SEMS = 5


def _rs_to_sibling(rc, s0, g_ref, sib_ref, c, sib):
    cp = rc(s0, g_ref.at[:, 1 - c], sib_ref, sib)
    cp.start()
    return cp


def _rs_trade(rc, s0, g_ref, sib_ref, out_ref, in_ref, rows, c, sib, chips):
    rc(s0, g_ref.at[:, 1 - c], sib_ref, sib).wait_recv()
    cps = []
    for k, (cx, cy) in enumerate(chips):
        jk = 2 * cx + cy

        def add(i, carry, jk=jk, k=k):
            rr = pl.ds(pl.multiple_of(i * RS_CH, RS_CH), RS_CH)
            out_ref[k, rr, :] = (g_ref[jk, c, rr, :].astype(F32) + sib_ref[jk, rr, :].astype(F32)).astype(BF16)
            return carry

        lax.fori_loop(0, rows // RS_CH, add, 0)
        cps.append(rc(s0 + 1 + k, out_ref.at[k], in_ref.at[k], (cx, cy, c)))
        cps[-1].start()
    return cps


def _rs_total(rc, s0, g_ref, sib_ref, out_ref, in_ref, res_ref, rows, j, c, sib):
    for k in range(3):
        rc(s0 + 1 + k, out_ref.at[k], in_ref.at[k], sib).wait_recv()

    def total(i, carry):
        rr = pl.ds(pl.multiple_of(i * RS_CH, RS_CH), RS_CH)
        acc = g_ref[j, c, rr, :].astype(F32) + sib_ref[j, rr, :].astype(F32)
        for k in range(3):
            acc = acc + in_ref[k, rr, :].astype(F32)
        res_ref[c, rr, :] = acc
        return carry

    lax.fori_loop(0, rows // RS_CH, total, 0)
    cp = rc(s0 + 4, res_ref.at[c], res_ref.at[c], sib)
    cp.start()
    return cp


def _rs_done(rc, s0, res_ref, c, sib):
    rc(s0 + 4, res_ref.at[1 - c], res_ref.at[1 - c], sib).wait_recv()


def _rs_scratch(rows):
    return [pltpu.VMEM((N_CHIPS, rows, D_MODEL), BF16), pltpu.VMEM((3, rows, D_MODEL), BF16),
            pltpu.VMEM((3, rows, D_MODEL), BF16)]


def _gather_weights(wt, c_row, w_ada, b_sh):
    n_sem = 16

    def body(wt_ref, c_ref, wada_ref, bsh_ref, w4_ref, call_ref, ada_ref, part_ref, ssem, rsem):
        x, y, c, chips = _place()
        j = 2 * x + y
        dev = 2 * j + c
        sib = (x, y, 1 - c)
        idx = [2 * cx + cy for cx, cy in chips]
        rc = functools.partial(_remote, ssem, rsem)

        w4_ref[j] = wt_ref[...].astype(BF16)
        call_ref[dev] = c_ref[...]

        sends = []
        peers = [(px, py, pc) for px in (x, 1 - x) for py in (y, 1 - y) for pc in (c, 1 - c)][1:]
        for k, peer in enumerate(peers):
            sends.append(rc(k, call_ref.at[dev], call_ref.at[dev], peer))
        for k, chip in enumerate(chips):
            sends.append(rc(7 + k, w4_ref.at[j, c], w4_ref.at[j, c], (*chip, c)))
        for cp in sends:
            cp.start()

        for k, (px, py, pc) in enumerate(peers):
            pdev = 4 * px + 2 * py + pc
            rc(k, call_ref.at[pdev], call_ref.at[pdev], (px, py, pc)).wait_recv()
        rowid = lax.broadcasted_iota(jnp.int32, (N_DEV, D_MODEL), 0)
        call = jnp.zeros((N_DEV, D_MODEL), F32)
        for r in range(N_DEV):
            call = jnp.where(rowid == r, jnp.broadcast_to(call_ref[r], (N_DEV, D_MODEL)), call)
        part = jnp.dot(_silu(call).astype(BF16), wada_ref[...].astype(BF16), preferred_element_type=F32) + bsh_ref[...]
        for r in range(N_DEV):
            part_ref[r] = part[r:r + 1, :]
        ada_ref[j] = part_ref[dev]
        rows_out = []
        for k, chip in enumerate(chips):
            rows_out.append(rc(13 + k, part_ref.at[2 * idx[k] + c], ada_ref.at[j], (*chip, c)))
            rows_out[-1].start()

        passed = []
        for k, chip in enumerate(chips):
            jk = idx[k]
            rc(7 + k, w4_ref.at[jk, c], w4_ref.at[jk, c], sib).wait_recv()
            passed.append(rc(10 + k, w4_ref.at[jk, c], w4_ref.at[jk, c], sib))
            passed[-1].start()
        for k, chip in enumerate(chips):
            jk = idx[k]
            rc(10 + k, w4_ref.at[jk, 1 - c], w4_ref.at[jk, 1 - c], sib).wait_recv()
            rc(13 + k, ada_ref.at[jk], ada_ref.at[jk], sib).wait_recv()
        for cp in sends + rows_out + passed:
            cp.wait_send()

    vm = pl.BlockSpec(memory_space=pltpu.VMEM)
    return pl.pallas_call(
        body,
        name="gather_weights",
        in_specs=[vm] * 4,
        out_specs=[vm] * 3,
        out_shape=[jax.ShapeDtypeStruct((N_CHIPS, 2, IN_HALF, D_MODEL), BF16),
                   jax.ShapeDtypeStruct((N_DEV, 1, D_MODEL), F32),
                   jax.ShapeDtypeStruct((N_CHIPS, 1, ADA_SHARD), F32)],
        scratch_shapes=[pltpu.VMEM((N_DEV, 1, ADA_SHARD), F32),
                        pltpu.SemaphoreType.DMA((n_sem,)), pltpu.SemaphoreType.DMA((n_sem,))],
        compiler_params=_cparams(),
    )(wt, c_row, w_ada, b_sh)


def _reduce_grads(g4, small):
    n_sem = 7 + RS_SEMS

    def body(g4_ref, sm_ref, gw_ref, ssum_ref, sall_ref, sib4_ref, out4_ref, in4_ref, ssem, rsem):
        x, y, c, chips = _place()
        j = 2 * x + y
        dev = 2 * j + c
        sib = (x, y, 1 - c)
        rc = functools.partial(_remote, ssem, rsem)

        sall_ref[dev] = sm_ref[...]
        sends = []
        peers = [(px, py, pc) for px in (x, 1 - x) for py in (y, 1 - y) for pc in (c, 1 - c)][1:]
        for k, peer in enumerate(peers):
            sends.append(rc(k, sall_ref.at[dev], sall_ref.at[dev], peer))
            sends[-1].start()
        sends.append(_rs_to_sibling(rc, 7, g4_ref, sib4_ref, c, sib))
        sends += _rs_trade(rc, 7, g4_ref, sib4_ref, out4_ref, in4_ref, IN_HALF, c, sib, chips)
        sends.append(_rs_total(rc, 7, g4_ref, sib4_ref, out4_ref, in4_ref, gw_ref, IN_HALF, j, c, sib))

        for k, (px, py, pc) in enumerate(peers):
            pdev = 4 * px + 2 * py + pc
            rc(k, sall_ref.at[pdev], sall_ref.at[pdev], (px, py, pc)).wait_recv()
        tot = sall_ref[0]
        for d in range(1, N_DEV):
            tot = tot + sall_ref[d]
        ssum_ref[...] = tot

        _rs_done(rc, 7, gw_ref, c, sib)
        for cp in sends:
            cp.wait_send()

    vm = pl.BlockSpec(memory_space=pltpu.VMEM)
    return pl.pallas_call(
        body,
        name="reduce_grads",
        in_specs=[vm] * 2,
        out_specs=[vm] * 3,
        out_shape=[jax.ShapeDtypeStruct((2, IN_HALF, D_MODEL), F32),
                   jax.ShapeDtypeStruct((SMALL_ROWS, 512), F32), jax.ShapeDtypeStruct((N_DEV, SMALL_ROWS, 512), F32)],
        scratch_shapes=_rs_scratch(IN_HALF) + [pltpu.SemaphoreType.DMA((n_sem,)), pltpu.SemaphoreType.DMA((n_sem,))],
        compiler_params=_cparams(),
    )(g4, small)


def _adamw_math(w, g, m, v):
    m2 = ADAM_B1 * m + (1.0 - ADAM_B1) * g
    v2 = ADAM_B2 * v + (1.0 - ADAM_B2) * (g * g)
    m_hat = m2 / (1.0 - ADAM_B1 ** ADAM_STEP)
    v_hat = v2 / (1.0 - ADAM_B2 ** ADAM_STEP)
    delta = -ADAM_LR * (m_hat / (jnp.sqrt(v_hat) + ADAM_EPS) + ADAM_WD * w)
    return delta, m2, v2


def _adamw(name, w, g, m, v, tm):
    r, cdim = w.shape

    def body(w_ref, g_ref, m_ref, v_ref, d_ref, m2_ref, v2_ref):
        d_ref[...], m2_ref[...], v2_ref[...] = _adamw_math(w_ref[...], g_ref[...], m_ref[...], v_ref[...])

    blk = pl.BlockSpec((tm, cdim), lambda i: (i, 0))
    return pl.pallas_call(
        body,
        name=name,
        grid=(r // tm,),
        in_specs=[blk] * 4,
        out_specs=[blk] * 3,
        out_shape=[jax.ShapeDtypeStruct((r, cdim), F32)] * 3,
        compiler_params=_cparams(dimension_semantics=("arbitrary",)),
    )(w, g, m, v)


def _adamw_ada(w, m, v, cact_t, dcols):
    r, cdim = w.shape
    tm = 256

    def body(w_ref, m_ref, v_ref, ct_ref, dc_ref, g_ref, d_ref, m2_ref, v2_ref):
        g = jnp.dot(ct_ref[...], dc_ref[...], preferred_element_type=F32, precision=lax.Precision.HIGHEST)
        g_ref[...] = g
        d_ref[...], m2_ref[...], v2_ref[...] = _adamw_math(w_ref[...], g, m_ref[...], v_ref[...])

    blk = pl.BlockSpec((tm, cdim), lambda i: (i, 0))
    return pl.pallas_call(
        body,
        name="adamw_w_ada",
        grid=(r // tm,),
        in_specs=[blk] * 3 + [pl.BlockSpec((tm, N_DEV), lambda i: (i, 0)), pl.BlockSpec((N_DEV, cdim), lambda i: (0, 0))],
        out_specs=[blk] * 4,
        out_shape=[jax.ShapeDtypeStruct((r, cdim), F32)] * 4,
        compiler_params=_cparams(dimension_semantics=("arbitrary",)),
    )(w, m, v, cact_t, dcols)


def _adamw_small(ws, gs, ms, vs):
    n = len(ws)

    def body(*refs):
        w_r, g_r, m_r, v_r = refs[0:n], refs[n:2 * n], refs[2 * n:3 * n], refs[3 * n:4 * n]
        d_r, m2_r, v2_r = refs[4 * n:5 * n], refs[5 * n:6 * n], refs[6 * n:7 * n]
        for i in range(n):
            d_r[i][...], m2_r[i][...], v2_r[i][...] = _adamw_math(w_r[i][...], g_r[i][...], m_r[i][...], v_r[i][...])

    vm = pl.BlockSpec(memory_space=pltpu.VMEM)
    shapes = [jax.ShapeDtypeStruct(w.shape, F32) for w in ws]
    out = pl.pallas_call(
        body,
        name="adamw_small",
        in_specs=[vm] * (4 * n),
        out_specs=[vm] * (3 * n),
        out_shape=shapes * 3,
        compiler_params=_cparams(),
    )(*ws, *gs, *ms, *vs)
    return out[0:n], out[n:2 * n], out[2 * n:3 * n]


def _rope_tables(t):
    inv = ROPE_THETA ** (-jnp.arange(0, HEAD_DIM, 2, dtype=F32) / HEAD_DIM)
    ang = jnp.arange(t, dtype=F32)[:, None] * inv[None, :]
    cos, sin = jnp.cos(ang), jnp.sin(ang)
    return jnp.tile(cos, (1, 4)), jnp.tile(jnp.concatenate([-sin, sin], axis=1), (1, 2))


def _pad_lanes(v, width):
    return jnp.pad(v, ((0, 0), (0, width - v.shape[1])))


def kernel(x, c, w_ada, b_ada, norm_w, w_in, q_norm_w, k_norm_w, sinks, conv_w, conv_b, ln_w, ln_b, w_out, loss_target, m_w_ada, m_b_ada, m_norm_w, m_w_in, m_q_norm_w, m_k_norm_w, m_sinks, m_conv_w, m_conv_b, m_ln_w, m_ln_b, m_w_out, v_w_ada, v_b_ada, v_norm_w, v_w_in, v_q_norm_w, v_k_norm_w, v_sinks, v_conv_w, v_conv_b, v_ln_w, v_ln_b, v_w_out):
    xi, yi = lax.axis_index("x"), lax.axis_index("y")
    j = 2 * xi + yi
    x2, tgt = x[0], loss_target[0]
    t = x2.shape[0]

    wt_s, mt_s, vt_s = w_in[0].T, m_w_in[0].T, v_w_in[0].T
    cw_pad = jnp.pad(conv_w[0], ((0, 1), (0, 0)))
    b_sh = lax.dynamic_slice(b_ada, (0, ADA_SHARD * j), (1, ADA_SHARD))

    w4, call, ada4 = _gather_weights(wt_s.reshape(2, IN_HALF, D_MODEL), c, w_ada[0], b_sh)
    w_full = w4.reshape(IN_W, D_MODEL)
    ada = ada4.reshape(1, 3 * D_MODEL)
    shift, s1, gate = ada[:, :D_MODEL], 1.0 + ada[:, D_MODEL:2 * D_MODEL], ada[:, 2 * D_MODEL:]

    cos_f, sin_s = _rope_tables(t)
    qw2, kw2 = jnp.tile(q_norm_w, (1, 2)), jnp.tile(k_norm_w, (1, 2))

    q_raw, kv_raw, ga, ua, ug, gb = _in_proj(x2, s1, shift, norm_w, w_full)
    o, mix_a, wo4, cw4 = _attn_fwd(q_raw, kv_raw, ga, qw2, kw2, sinks, cos_f, sin_s,
                                   w_out[0].reshape(2, OUT_HALF, D_MODEL), cw_pad)
    w_out_full = wo4.reshape(D_MODEL, D_MODEL)
    cw_full = jnp.concatenate([cw4[i] for i in range(N_CHIPS)], axis=1)
    cz, mix_b = _conv_fwd(ua, ug, gb, cw_full, conv_b, ln_w, ln_b)
    dout, dmix_a, dmix_b, gwo_bf, red_o = _out_proj(mix_a, mix_b, x2, tgt, gate, w_out_full)

    dq, dkv, dga, sm_a, gwo = _attn_bwd(q_raw, kv_raw, ga, o, dmix_a, qw2, kw2, sinks, cos_f, sin_s,
                                        gwo_bf.reshape(N_CHIPS, 2, OUT_HALF, D_MODEL))
    dua, dug, dgb, dcw, dvec = _conv_bwd(ua, ug, gb, cz, dmix_b, cw_full, ln_w, ln_b)
    grad_x, g4, red_i = _in_proj_bwd(dq, dkv, dga, dua, dug, dgb, x2, dout, s1, shift, norm_w, w_full)

    small = jnp.concatenate([
        dcw, dvec,
        jnp.concatenate([red_i[2:3], red_i[0:1], red_i[1:2], red_o[0:1]], axis=1).reshape(8, 512),
        _pad_lanes(sm_a, 512),
        jnp.pad(red_o[1:2].reshape(2, 512), ((0, 6), (0, 0)))], axis=0)
    gw, ssum, sall = _reduce_grads(g4.reshape(N_CHIPS, 2, IN_HALF, D_MODEL), small)

    loss = (0.5 / D_MODEL) * jnp.sum(ssum[56:58])
    gt_w_in = gw.reshape(2 * IN_HALF, D_MODEL)
    g_w_out = gwo.reshape(D_MODEL // N_CHIPS, D_MODEL)
    g_conv_w = lax.dynamic_slice(ssum, (0, 128 * j), (CONV_TAPS, 128))
    g_conv_b, g_ln_w, g_ln_b = ssum[32:33], ssum[33:34], ssum[34:35]
    g_norm_w = ssum[40:42].reshape(1, D_MODEL)
    g_b_ada = ssum[42:48].reshape(1, 3 * D_MODEL)
    g_qw, g_kw, g_sinks = ssum[48:49, 0:HEAD_DIM], ssum[49:50, 0:HEAD_DIM], ssum[50:51, 0:8]
    d_ada_all = sall[:, 42:48, :].reshape(N_DEV, 3 * D_MODEL)
    dcols = lax.dynamic_slice(d_ada_all, (0, ADA_SHARD * j), (N_DEV, ADA_SHARD))
    cact_t = jax.nn.silu(call.reshape(N_DEV, D_MODEL)).T

    g_w_ada, d_w_ada, nm_w_ada, nv_w_ada = _adamw_ada(w_ada[0], m_w_ada[0], v_w_ada[0], cact_t, dcols)
    dt_w_in, nmt_w_in, nvt_w_in = _adamw("adamw_w_in", wt_s, gt_w_in, mt_s, vt_s, 176)
    g_w_in, d_w_in, nm_w_in, nv_w_in = gt_w_in.T, dt_w_in.T, nmt_w_in.T, nvt_w_in.T
    d_w_out, nm_w_out, nv_w_out = _adamw("adamw_w_out", w_out[0], g_w_out, m_w_out[0], v_w_out[0], 128)
    ws = [b_ada, norm_w, q_norm_w, k_norm_w, sinks, conv_w[0], conv_b, ln_w, ln_b]
    gs = [g_b_ada, g_norm_w, g_qw, g_kw, g_sinks, g_conv_w, g_conv_b, g_ln_w, g_ln_b]
    ms = [m_b_ada, m_norm_w, m_q_norm_w, m_k_norm_w, m_sinks, m_conv_w[0], m_conv_b, m_ln_w, m_ln_b]
    vs = [v_b_ada, v_norm_w, v_q_norm_w, v_k_norm_w, v_sinks, v_conv_w[0], v_conv_b, v_ln_w, v_ln_b]
    ds, nms, nvs = _adamw_small(ws, gs, ms, vs)

    def order(ada_v, in_v, out_v, sm):
        b, nw_, qw_, kw_, sk_, cw_, cb_, lw_, lb_ = sm
        return [ada_v[None], b, nw_, in_v[None], qw_, kw_, sk_, cw_[None], cb_, lw_, lb_, out_v[None]]

    grads = order(g_w_ada, g_w_in, g_w_out, gs)
    deltas = order(d_w_ada, d_w_in, d_w_out, ds)
    new_m = order(nm_w_ada, nm_w_in, nm_w_out, nms)
    new_v = order(nv_w_ada, nv_w_in, nv_w_out, nvs)
    return (loss, grad_x[None], *grads, *deltas, *new_m, *new_v)
```

```python
import functools

import jax
import jax.numpy as jnp
from jax import lax
from jax.experimental import pallas as pl
from jax.experimental.pallas import tpu as pltpu

F32 = jnp.float32
BF16 = jnp.bfloat16

D_MODEL = 1024
ATTN_W = 512
KV_W = 128
CONV_W = 512
IN_W = 2816
HEAD_DIM = 64
CONV_TAPS = 31
QBLK = 128
EPS = 1e-6
ROPE_THETA = 10000.0

ADAM_LR = 0.001
ADAM_B1 = 0.9
ADAM_B2 = 0.999
ADAM_EPS = 1e-08
ADAM_WD = 0.01
ADAM_STEP = 10

N_CHIPS = 4
N_DEV = 8
IN_HALF = IN_W // N_CHIPS // 2
OUT_HALF = D_MODEL // N_CHIPS // 2
ADA_SHARD = 3 * D_MODEL // N_CHIPS

VMEM_LIMIT = 56 * 1024 * 1024
CONV_PAD = 32


def _cparams(**kw):
    return pltpu.CompilerParams(vmem_limit_bytes=VMEM_LIMIT, **kw)


def _sigmoid(v):
    return 1.0 / (1.0 + jnp.exp(-v))


def _silu(v):
    return v * _sigmoid(v)


def _dsilu(v):
    s = _sigmoid(v)
    return s * (1.0 + v * (1.0 - s))


def _lane(shape):
    return lax.broadcasted_iota(jnp.int32, shape, len(shape) - 1)


def _in_proj(x, s1, shift, nw, wt_full):
    t = x.shape[0]
    tm = 256

    def body(x_ref, s1_ref, sh_ref, nw_ref, w_ref, q_ref, kv_ref, ga_ref, ua_ref, ug_ref, gb_ref, h_ref):
        xv = x_ref[...]
        r = lax.rsqrt(jnp.mean(xv * xv, axis=-1, keepdims=True) + EPS)
        h = ((xv * r) * nw_ref[...] * s1_ref[...] + sh_ref[...]).astype(BF16)
        h_ref[...] = h
        p = lax.dot_general(h, w_ref[...], (((1,), (1,)), ((), ())), preferred_element_type=F32)
        q_ref[...] = p[:, 0:512]
        kv_ref[...] = p[:, 512:768]
        ga_ref[...] = p[:, 768:1280]
        ua_ref[...] = p[:, 1280:1792]
        ug_ref[...] = p[:, 1792:2304]
        gb_ref[...] = p[:, 2304:2816]

    row = lambda w: pl.BlockSpec((tm, w), lambda i: (i, 0))
    vec = pl.BlockSpec((1, D_MODEL), lambda i: (0, 0))
    return pl.pallas_call(
        body,
        name="in_proj",
        grid=(t // tm,),
        in_specs=[row(D_MODEL), vec, vec, vec,
                  pl.BlockSpec((IN_W, D_MODEL), lambda i: (0, 0), pipeline_mode=pl.Buffered(1))],
        out_specs=[row(512), row(256), row(512), row(512), row(512), row(512), row(D_MODEL)],
        out_shape=[jax.ShapeDtypeStruct((t, w), F32) for w in (512, 256, 512, 512, 512, 512)]
        + [jax.ShapeDtypeStruct((t, D_MODEL), BF16)],
        compiler_params=_cparams(dimension_semantics=("arbitrary",)),
    )(x, s1, shift, nw, wt_full)


def _head_mean(s, left):
    sl = jnp.sum(jnp.where(left, s, 0.0), axis=-1, keepdims=True)
    sr = jnp.sum(jnp.where(left, 0.0, s), axis=-1, keepdims=True)
    return jnp.where(left, sl, sr) * (1.0 / HEAD_DIM)


def _rot(v, first):
    return jnp.where(first, pltpu.roll(v, 96, 1), pltpu.roll(v, 32, 1))


def _norm_rope(v, w, cos, sin_s, left, first):
    r = lax.rsqrt(_head_mean(v * v, left) + EPS)
    xh = v * r
    n = xh * w
    return n * cos + _rot(n, first) * sin_s, xh, r


def _norm_rope_bwd(d, xh, r, w, cos, sin_s, left, first):
    dn = d * cos - _rot(d, first) * sin_s
    dw = jnp.sum(dn * xh, axis=0, keepdims=True)
    dxh = dn * w
    return r * (dxh - xh * _head_mean(dxh * xh, left)), dw


def _dup_heads(v, left):
    sw = pltpu.roll(v, 64, 1)
    return jnp.where(left, v, sw), jnp.where(left, sw, v)


def _prep_kv(kv_ref, kw_ref, cos_ref, sin_ref, ka_ref, va_ref, t):
    ch = 256
    for g in range(2):
        ka_ref[g, 0:QBLK, :] = jnp.zeros((QBLK, 128), BF16)
        va_ref[g, 0:QBLK, :] = jnp.zeros((QBLK, 128), BF16)

    def chunk(i, carry):
        r0 = pl.multiple_of(i * ch, ch)
        left = _lane((ch, 128)) < 64
        first = (_lane((ch, 128)) % 64) < 32
        k = kv_ref[pl.ds(r0, ch), 0:128]
        v = kv_ref[pl.ds(r0, ch), 128:256]
        kr, _, _ = _norm_rope(k, kw_ref[...], cos_ref[pl.ds(r0, ch), :], sin_ref[pl.ds(r0, ch), :], left, first)
        k0, k1 = _dup_heads(kr, left)
        v0, v1 = _dup_heads(v, left)
        ka_ref[0, pl.ds(QBLK + r0, ch), :] = k0.astype(BF16)
        ka_ref[1, pl.ds(QBLK + r0, ch), :] = k1.astype(BF16)
        va_ref[0, pl.ds(QBLK + r0, ch), :] = v0.astype(BF16)
        va_ref[1, pl.ds(QBLK + r0, ch), :] = v1.astype(BF16)
        return carry

    lax.fori_loop(0, t // ch, chunk, 0)


def _band_mask(n):
    qi = lax.broadcasted_iota(jnp.int32, (2 * QBLK, 2 * QBLK), 0) % QBLK
    kj = lax.broadcasted_iota(jnp.int32, (2 * QBLK, 2 * QBLK), 1)
    local = (kj > qi) & (kj <= qi + QBLK)
    return local & ((n > 0) | (kj >= QBLK))


def _softmax_pair(s, mask, sink0, sink1):
    row = lax.broadcasted_iota(jnp.int32, (2 * QBLK, 1), 0)
    sink = jnp.where(row < QBLK, sink0, sink1)
    s = jnp.where(mask, s, -jnp.inf)
    m = jnp.maximum(jnp.max(s, axis=-1, keepdims=True), sink)
    e = jnp.exp(s - m)
    es = jnp.exp(sink - m)
    inv = 1.0 / (jnp.sum(e, axis=-1, keepdims=True) + es)
    return e * inv, es * inv


def _stack_heads(v, left):
    return jnp.concatenate([jnp.where(left, v, 0.0), jnp.where(left, 0.0, v)], axis=0)


def _attn_fwd(q_raw, kv_raw, ga, qw2, kw2, sinks, cos_f, sin_s, wo, cw):
    t = q_raw.shape[0]
    nblk = t // QBLK

    def body(q_ref, kv_ref, ga_ref, qw_ref, kw_ref, sk_ref, cos_ref, sin_ref, wo_ref, cw_ref,
             o_ref, mix_ref, wo4_ref, cw4_ref, ka_ref, va_ref, ssem, rsem):
        x, y, c, chips = _place()
        j = 2 * x + y
        sib = (x, y, 1 - c)
        idx = [2 * cx + cy for cx, cy in chips]
        rc = functools.partial(_remote, ssem, rsem)
        wo4_ref[j] = wo_ref[...].astype(BF16)
        cw4_ref[j] = cw_ref[...]
        sends = []
        for k, chip in enumerate(chips):
            sends.append(rc(k, wo4_ref.at[j, c], wo4_ref.at[j, c], (*chip, c)))
            sends.append(rc(6 + k, cw4_ref.at[j], cw4_ref.at[j], (*chip, c)))
        for cp in sends:
            cp.start()

        _prep_kv(kv_ref, kw_ref, cos_ref, sin_ref, ka_ref, va_ref, t)

        def blk(n, carry):
            r0 = pl.multiple_of(n * QBLK, QBLK)
            left = _lane((QBLK, 128)) < 64
            first = (_lane((QBLK, 128)) % 64) < 32
            cos = cos_ref[pl.ds(r0, QBLK), :]
            sin = sin_ref[pl.ds(r0, QBLK), :]
            mask = _band_mask(n)
            for p in range(4):
                g = p // 2
                lanes = slice(p * 128, (p + 1) * 128)
                qr, _, _ = _norm_rope(q_ref[pl.ds(r0, QBLK), lanes], qw_ref[...], cos, sin, left, first)
                q2 = _stack_heads(qr * 0.125, left).astype(BF16)
                s = lax.dot_general(q2, ka_ref[g, pl.ds(r0, 2 * QBLK), :], (((1,), (1,)), ((), ())),
                                    preferred_element_type=F32)
                pm, _ = _softmax_pair(s, mask, sk_ref[0, 2 * p], sk_ref[0, 2 * p + 1])
                o2 = jnp.dot(pm.astype(BF16), va_ref[g, pl.ds(r0, 2 * QBLK), :], preferred_element_type=F32)
                o = jnp.where(left, o2[0:QBLK], o2[QBLK:2 * QBLK])
                o_ref[pl.ds(r0, QBLK), lanes] = o
                mix_ref[pl.ds(r0, QBLK), lanes] = (o * _silu(ga_ref[pl.ds(r0, QBLK), lanes])).astype(BF16)
            return carry

        lax.fori_loop(0, nblk, blk, 0)

        passed = []
        for k, chip in enumerate(chips):
            jk = idx[k]
            rc(k, wo4_ref.at[jk, c], wo4_ref.at[jk, c], sib).wait_recv()
            passed.append(rc(3 + k, wo4_ref.at[jk, c], wo4_ref.at[jk, c], sib))
            passed[-1].start()
        for k, chip in enumerate(chips):
            jk = idx[k]
            rc(3 + k, wo4_ref.at[jk, 1 - c], wo4_ref.at[jk, 1 - c], sib).wait_recv()
            rc(6 + k, cw4_ref.at[jk], cw4_ref.at[jk], sib).wait_recv()
        for cp in sends + passed:
            cp.wait_send()

    vm = pl.BlockSpec(memory_space=pltpu.VMEM)
    n_sem = 9
    return pl.pallas_call(
        body,
        name="attn_fwd",
        in_specs=[vm, vm, vm, vm, vm, pl.BlockSpec(memory_space=pltpu.SMEM), vm, vm, vm, vm],
        out_specs=[vm] * 4,
        out_shape=[jax.ShapeDtypeStruct((t, ATTN_W), F32), jax.ShapeDtypeStruct((t, ATTN_W), BF16),
                   jax.ShapeDtypeStruct((N_CHIPS, 2, OUT_HALF, D_MODEL), BF16),
                   jax.ShapeDtypeStruct((N_CHIPS, 32, 128), F32)],
        scratch_shapes=[pltpu.VMEM((2, t + QBLK, 128), BF16), pltpu.VMEM((2, t + QBLK, 128), BF16),
                        pltpu.SemaphoreType.DMA((n_sem,)), pltpu.SemaphoreType.DMA((n_sem,))],
        compiler_params=_cparams(),
    )(q_raw, kv_raw, ga, qw2, kw2, sinks, cos_f, sin_s, wo, cw)


def _attn_bwd(q_raw, kv_raw, ga, o, dmix, qw2, kw2, sinks, cos_f, sin_s, go):
    t = q_raw.shape[0]
    nblk = t // QBLK

    def body(q_ref, kv_ref, ga_ref, o_ref, dm_ref, qw_ref, kw_ref, sk_ref, cos_ref, sin_ref, go_ref,
             dq_ref, dkv_ref, dga_ref, sm_ref, gwo_ref, ka_ref, va_ref, dka_ref, dva_ref,
             sibo_ref, outo_ref, ino_ref, ssem, rsem):
        x, y, c, chips = _place()
        sib = (x, y, 1 - c)
        rc = functools.partial(_remote, ssem, rsem)
        theirs, mine = go_ref.at[:, 1 - c], go_ref.at[:, c]
        sends = [_rs_to_sibling(rc, 0, theirs, sibo_ref, sib)]
        _prep_kv(kv_ref, kw_ref, cos_ref, sin_ref, ka_ref, va_ref, t)
        dka_ref[...] = jnp.zeros_like(dka_ref)
        dva_ref[...] = jnp.zeros_like(dva_ref)
        sends += _rs_trade(rc, 0, theirs, mine, sibo_ref, outo_ref, ino_ref, OUT_HALF, c, sib, chips)

        def blk(n, carry):
            dqw, dsk = carry
            r0 = pl.multiple_of(n * QBLK, QBLK)
            left = _lane((QBLK, 128)) < 64
            first = (_lane((QBLK, 128)) % 64) < 32
            cos = cos_ref[pl.ds(r0, QBLK), :]
            sin = sin_ref[pl.ds(r0, QBLK), :]
            mask = _band_mask(n)
            row = lax.broadcasted_iota(jnp.int32, (2 * QBLK, 1), 0)
            for p in range(4):
                g = p // 2
                lanes = slice(p * 128, (p + 1) * 128)
                rows = pl.ds(r0, QBLK)
                win = pl.ds(r0, 2 * QBLK)
                qr, xh, r = _norm_rope(q_ref[rows, lanes], qw_ref[...], cos, sin, left, first)
                q2 = _stack_heads(qr * 0.125, left).astype(BF16)
                kwin = ka_ref[g, win, :]
                vwin = va_ref[g, win, :]
                s = lax.dot_general(q2, kwin, (((1,), (1,)), ((), ())), preferred_element_type=F32)
                pm, ps = _softmax_pair(s, mask, sk_ref[0, 2 * p], sk_ref[0, 2 * p + 1])
                gav = ga_ref[rows, lanes]
                dmv = dm_ref[rows, lanes]
                dga_ref[rows, lanes] = (dmv * o_ref[rows, lanes] * _dsilu(gav)).astype(BF16)
                do2 = _stack_heads(dmv * _silu(gav), left).astype(BF16)
                dp = lax.dot_general(do2, vwin, (((1,), (1,)), ((), ())), preferred_element_type=F32)
                delta = jnp.sum(pm * dp, axis=-1, keepdims=True)
                ds = (pm * (dp - delta)).astype(BF16)
                pd = ps * delta
                d0 = jnp.sum(jnp.where(row < QBLK, pd, 0.0), axis=0, keepdims=True)
                d1 = jnp.sum(jnp.where(row < QBLK, 0.0, pd), axis=0, keepdims=True)
                l8 = _lane((1, 128))
                dsk = dsk - jnp.where(l8 == 2 * p, d0, 0.0) - jnp.where(l8 == 2 * p + 1, d1, 0.0)
                dva_ref[g, win, :] += lax.dot_general(pm.astype(BF16), do2, (((0,), (0,)), ((), ())),
                                                      preferred_element_type=F32)
                dka_ref[g, win, :] += lax.dot_general(ds, q2, (((0,), (0,)), ((), ())),
                                                      preferred_element_type=F32)
                dq2 = jnp.dot(ds, kwin, preferred_element_type=F32)
                dqr = jnp.where(left, dq2[0:QBLK], dq2[QBLK:2 * QBLK]) * 0.125
                dq, dw = _norm_rope_bwd(dqr, xh, r, qw_ref[...], cos, sin, left, first)
                dq_ref[rows, lanes] = dq.astype(BF16)
                dqw = dqw + dw
            return dqw, dsk

        zero = jnp.zeros((1, 128), F32)
        dqw, dsk = lax.fori_loop(0, nblk, blk, (zero, zero))

        ch = 256

        def chunk(i, dkw):
            r0 = pl.multiple_of(i * ch, ch)
            left = _lane((ch, 128)) < 64
            first = (_lane((ch, 128)) % 64) < 32
            rows = pl.ds(r0, ch)
            prow = pl.ds(QBLK + r0, ch)

            def fold(ref):
                a0 = ref[0, prow, :]
                a1 = ref[1, prow, :]
                return jnp.where(left, a0 + pltpu.roll(a0, 64, 1), a1 + pltpu.roll(a1, 64, 1))

            cos = cos_ref[rows, :]
            sin = sin_ref[rows, :]
            _, xh, r = _norm_rope(kv_ref[rows, 0:128], kw_ref[...], cos, sin, left, first)
            dk, dw = _norm_rope_bwd(fold(dka_ref), xh, r, kw_ref[...], cos, sin, left, first)
            dkv_ref[rows, 0:128] = dk.astype(BF16)
            dkv_ref[rows, 128:256] = fold(dva_ref).astype(BF16)
            return dkw + dw

        dkw = lax.fori_loop(0, t // ch, chunk, zero)
        sm_ref[...] = jnp.zeros((8, 128), F32)
        sm_ref[0:1, :] = dqw + pltpu.roll(dqw, 64, 1)
        sm_ref[1:2, :] = dkw + pltpu.roll(dkw, 64, 1)
        sm_ref[2:3, :] = dsk

        j = 2 * x + y
        sends.append(_rs_total(rc, 0, mine, sibo_ref, outo_ref, ino_ref, gwo_ref, OUT_HALF, j, c, sib))
        _rs_done(rc, 0, gwo_ref, c, sib)
        for cp in sends:
            cp.wait_send()

    vm = pl.BlockSpec(memory_space=pltpu.VMEM)
    return pl.pallas_call(
        body,
        name="attn_bwd",
        in_specs=[vm, vm, vm, vm, vm, vm, vm, pl.BlockSpec(memory_space=pltpu.SMEM), vm, vm, vm],
        out_specs=[vm] * 5,
        out_shape=[jax.ShapeDtypeStruct((t, ATTN_W), BF16), jax.ShapeDtypeStruct((t, 2 * KV_W), BF16),
                   jax.ShapeDtypeStruct((t, ATTN_W), BF16), jax.ShapeDtypeStruct((8, 128), F32),
                   jax.ShapeDtypeStruct((2, OUT_HALF, D_MODEL), F32)],
        scratch_shapes=[pltpu.VMEM((2, t + QBLK, 128), BF16), pltpu.VMEM((2, t + QBLK, 128), BF16),
                        pltpu.VMEM((2, t + QBLK, 128), F32), pltpu.VMEM((2, t + QBLK, 128), F32)]
        + _rs_scratch(OUT_HALF) + [pltpu.SemaphoreType.DMA((RS_SEMS,)), pltpu.SemaphoreType.DMA((RS_SEMS,))],
        compiler_params=_cparams(),
    )(q_raw, kv_raw, ga, o, dmix, qw2, kw2, sinks, cos_f, sin_s, go)


CONV_CH = 256
CONV_SUB = 64


def _shifted_windows(src_ref, r0, sh_ref):
    rows = CONV_CH + CONV_PAD
    win = src_ref[pl.ds(r0, rows), :]
    sh_ref[0] = win
    for b in range(1, 8):
        sh_ref[b] = pltpu.roll(win, rows - b, 0)


def _conv_fwd(ua, ug, gb, cw, cb, lw, lb):
    t = ua.shape[0]

    def body(ua_ref, ug_ref, gb_ref, cw_ref, cb_ref, lw_ref, lb_ref, cz_ref, mix_ref, zp_ref, sh_ref):
        zp_ref[0:CONV_PAD, :] = jnp.zeros((CONV_PAD, CONV_W), F32)

        def glu(i, carry):
            r0 = pl.multiple_of(i * CONV_CH, CONV_CH)
            rows = pl.ds(r0, CONV_CH)
            zp_ref[pl.ds(CONV_PAD + r0, CONV_CH), :] = ua_ref[rows, :] * _sigmoid(ug_ref[rows, :])
            return carry

        lax.fori_loop(0, t // CONV_CH, glu, 0)

        def chunk(i, carry):
            r0 = pl.multiple_of(i * CONV_CH, CONV_CH)
            _shifted_windows(zp_ref, r0, sh_ref)
            for c in range(CONV_W // 128):
                lanes = slice(c * 128, (c + 1) * 128)

                def sub(k, carry2):
                    b0 = pl.multiple_of(k * CONV_SUB, CONV_SUB)
                    acc = jnp.broadcast_to(cb_ref[0:1, lanes], (CONV_SUB, 128))
                    for j in range(CONV_TAPS):
                        off = j + CONV_PAD - (CONV_TAPS - 1)
                        acc = acc + sh_ref[off % 8, pl.ds(b0 + 8 * (off // 8), CONV_SUB), lanes] * cw_ref[j:j + 1, lanes]
                    cz_ref[pl.ds(r0 + b0, CONV_SUB), lanes] = acc
                    return carry2

                lax.fori_loop(0, CONV_CH // CONV_SUB, sub, 0)
            rows = pl.ds(r0, CONV_CH)
            cz = cz_ref[rows, :]
            mu = jnp.mean(cz, axis=-1, keepdims=True)
            xc = cz - mu
            rs = lax.rsqrt(jnp.mean(xc * xc, axis=-1, keepdims=True) + EPS)
            ln = xc * rs * lw_ref[...] + lb_ref[...]
            mix_ref[rows, :] = (_silu(ln) * _silu(gb_ref[rows, :])).astype(BF16)
            return carry

        lax.fori_loop(0, t // CONV_CH, chunk, 0)

    vm = pl.BlockSpec(memory_space=pltpu.VMEM)
    return pl.pallas_call(
        body,
        name="conv_fwd",
        in_specs=[vm] * 7,
        out_specs=[vm, vm],
        out_shape=[jax.ShapeDtypeStruct((t, CONV_W), F32), jax.ShapeDtypeStruct((t, CONV_W), BF16)],
        scratch_shapes=[pltpu.VMEM((t + CONV_PAD, CONV_W), F32),
                        pltpu.VMEM((8, CONV_CH + CONV_PAD, CONV_W), F32)],
        compiler_params=_cparams(),
    )(ua, ug, gb, cw, cb, lw, lb)


def _conv_bwd(ua, ug, gb, cz, dmix, cw, lw, lb):
    t = ua.shape[0]

    def body(ua_ref, ug_ref, gb_ref, cz_ref, dm_ref, cw_ref, lw_ref, lb_ref,
             dua_ref, dug_ref, dgb_ref, dcw_ref, dvec_ref, zp_ref, dp_ref, sh_ref, wacc_ref):
        zp_ref[0:CONV_PAD, :] = jnp.zeros((CONV_PAD, CONV_W), F32)
        dp_ref[t:t + CONV_PAD, :] = jnp.zeros((CONV_PAD, CONV_W), F32)
        wacc_ref[...] = jnp.zeros_like(wacc_ref)

        def pointwise(i, carry):
            dcb, dlw, dlb = carry
            r0 = pl.multiple_of(i * CONV_CH, CONV_CH)
            rows = pl.ds(r0, CONV_CH)
            zp_ref[pl.ds(CONV_PAD + r0, CONV_CH), :] = ua_ref[rows, :] * _sigmoid(ug_ref[rows, :])
            cz = cz_ref[rows, :]
            mu = jnp.mean(cz, axis=-1, keepdims=True)
            xc = cz - mu
            rs = lax.rsqrt(jnp.mean(xc * xc, axis=-1, keepdims=True) + EPS)
            xh = xc * rs
            ln = xh * lw_ref[...] + lb_ref[...]
            gbv = gb_ref[rows, :]
            dy = dm_ref[rows, :]
            dgb_ref[rows, :] = (dy * _silu(ln) * _dsilu(gbv)).astype(BF16)
            dl = dy * _silu(gbv) * _dsilu(ln)
            dxh = dl * lw_ref[...]
            dcz = rs * (dxh - jnp.mean(dxh, axis=-1, keepdims=True)
                        - xh * jnp.mean(dxh * xh, axis=-1, keepdims=True))
            dp_ref[rows, :] = dcz
            return (dcb + jnp.sum(dcz, axis=0, keepdims=True),
                    dlw + jnp.sum(dl * xh, axis=0, keepdims=True),
                    dlb + jnp.sum(dl, axis=0, keepdims=True))

        zero = jnp.zeros((1, CONV_W), F32)
        dcb, dlw, dlb = lax.fori_loop(0, t // CONV_CH, pointwise, (zero, zero, zero))
        dvec_ref[...] = jnp.zeros((8, CONV_W), F32)
        dvec_ref[0:1, :] = dcb
        dvec_ref[1:2, :] = dlw
        dvec_ref[2:3, :] = dlb

        def chunk(i, carry):
            r0 = pl.multiple_of(i * CONV_CH, CONV_CH)
            _shifted_windows(dp_ref, r0, sh_ref)
            for c in range(CONV_W // 128):
                lanes = slice(c * 128, (c + 1) * 128)

                def sub(k, carry2):
                    b0 = pl.multiple_of(k * CONV_SUB, CONV_SUB)
                    acc = jnp.zeros((CONV_SUB, 128), F32)
                    for j in range(CONV_TAPS):
                        off = CONV_TAPS - 1 - j
                        acc = acc + sh_ref[off % 8, pl.ds(b0 + 8 * (off // 8), CONV_SUB), lanes] * cw_ref[j:j + 1, lanes]
                    rr = pl.ds(r0 + b0, CONV_SUB)
                    sg = _sigmoid(ug_ref[rr, lanes])
                    dua_ref[rr, lanes] = (acc * sg).astype(BF16)
                    dug_ref[rr, lanes] = (acc * ua_ref[rr, lanes] * sg * (1.0 - sg)).astype(BF16)
                    return carry2

                lax.fori_loop(0, CONV_CH // CONV_SUB, sub, 0)
            _shifted_windows(zp_ref, r0, sh_ref)
            for c in range(CONV_W // 128):
                lanes = slice(c * 128, (c + 1) * 128)

                def subw(k, carry2):
                    b0 = pl.multiple_of(k * CONV_SUB, CONV_SUB)
                    dcz = dp_ref[pl.ds(r0 + b0, CONV_SUB), lanes]
                    for j in range(CONV_TAPS):
                        off = j + CONV_PAD - (CONV_TAPS - 1)
                        pr = dcz * sh_ref[off % 8, pl.ds(b0 + 8 * (off // 8), CONV_SUB), lanes]
                        part = pr[0:8]
                        for q in range(1, CONV_SUB // 8):
                            part = part + pr[8 * q:8 * (q + 1)]
                        wacc_ref[8 * j:8 * (j + 1), lanes] += part
                    return carry2

                lax.fori_loop(0, CONV_CH // CONV_SUB, subw, 0)
            return carry

        lax.fori_loop(0, t // CONV_CH, chunk, 0)
        dcw_ref[...] = jnp.zeros((32, CONV_W), F32)
        for j in range(CONV_TAPS):
            dcw_ref[j:j + 1, :] = jnp.sum(wacc_ref[8 * j:8 * (j + 1), :], axis=0, keepdims=True)

    vm = pl.BlockSpec(memory_space=pltpu.VMEM)
    return pl.pallas_call(
        body,
        name="conv_bwd",
        in_specs=[vm] * 8,
        out_specs=[vm] * 5,
        out_shape=[jax.ShapeDtypeStruct((t, CONV_W), BF16)] * 3
        + [jax.ShapeDtypeStruct((32, CONV_W), F32), jax.ShapeDtypeStruct((8, CONV_W), F32)],
        scratch_shapes=[pltpu.VMEM((t + CONV_PAD, CONV_W), F32), pltpu.VMEM((t + CONV_PAD, CONV_W), F32),
                        pltpu.VMEM((8, CONV_CH + CONV_PAD, CONV_W), F32), pltpu.VMEM((8 * 32, CONV_W), F32)],
        compiler_params=_cparams(),
    )(ua, ug, gb, cz, dmix, cw, lw, lb)


def _out_proj(mix_a, mix_b, x, tgt, gate, w_out):
    t = x.shape[0]
    tm = 256
    nstep = t // tm

    def body(ma_ref, mb_ref, x_ref, t_ref, g_ref, w_ref, dout_ref, dma_ref, dmb_ref, gw_ref, red_ref, acc_ref):
        i = pl.program_id(0)

        @pl.when(i == 0)
        def _():
            acc_ref[...] = jnp.zeros_like(acc_ref)
            red_ref[...] = jnp.zeros_like(red_ref)

        mix = jnp.concatenate([ma_ref[...], mb_ref[...]], axis=1)
        y = jnp.dot(mix, w_ref[...], preferred_element_type=F32)
        gate_v = g_ref[...]
        err = x_ref[...] + gate_v * y - t_ref[...]
        dout = err * (1.0 / D_MODEL)
        dout_ref[...] = dout
        red_ref[0:1, :] += jnp.sum(dout * y, axis=0, keepdims=True)
        red_ref[1:2, :] += jnp.sum(err * err, axis=0, keepdims=True)
        dy = (dout * gate_v).astype(BF16)
        dmix = lax.dot_general(dy, w_ref[...], (((1,), (1,)), ((), ())), preferred_element_type=F32)
        dma_ref[...] = dmix[:, 0:512]
        dmb_ref[...] = dmix[:, 512:1024]
        acc_ref[...] += lax.dot_general(mix, dy, (((0,), (0,)), ((), ())), preferred_element_type=F32)

        @pl.when(i == nstep - 1)
        def _():
            gw_ref[...] = acc_ref[...].astype(BF16)

    row = lambda w: pl.BlockSpec((tm, w), lambda i: (i, 0))
    const = lambda s: pl.BlockSpec(s, lambda i: (0, 0))
    return pl.pallas_call(
        body,
        name="out_proj",
        grid=(nstep,),
        in_specs=[row(512), row(512), row(D_MODEL), row(D_MODEL), const((1, D_MODEL)),
                  pl.BlockSpec((D_MODEL, D_MODEL), lambda i: (0, 0), pipeline_mode=pl.Buffered(1))],
        out_specs=[row(D_MODEL), row(512), row(512), const((D_MODEL, D_MODEL)), const((8, D_MODEL))],
        out_shape=[jax.ShapeDtypeStruct((t, D_MODEL), F32), jax.ShapeDtypeStruct((t, 512), F32),
                   jax.ShapeDtypeStruct((t, 512), F32), jax.ShapeDtypeStruct((D_MODEL, D_MODEL), BF16),
                   jax.ShapeDtypeStruct((8, D_MODEL), F32)],
        scratch_shapes=[pltpu.VMEM((D_MODEL, D_MODEL), F32)],
        compiler_params=_cparams(dimension_semantics=("arbitrary",)),
    )(mix_a, mix_b, x, tgt, gate, w_out)


DPROJ_WIDTHS = (512, 256, 512, 512, 512, 512)
GW_BLK = 256


def _gw_in(dparts, h):
    t = h.shape[0]
    nblk = [w // GW_BLK for w in DPROJ_WIDTHS]
    first = [sum(nblk[:p]) for p in range(len(nblk))]
    nstep = sum(nblk)

    def body(*refs):
        d_refs, h_ref, g_ref = refs[:len(nblk)], refs[len(nblk)], refs[len(nblk) + 1]
        i = pl.program_id(0)
        for p, d_ref in enumerate(d_refs):
            @pl.when((i >= first[p]) & (i < first[p] + nblk[p]))
            def _(d_ref=d_ref):
                g_ref[...] = lax.dot_general(d_ref[...], h_ref[...], (((0,), (0,)), ((), ())),
                                             preferred_element_type=F32).astype(BF16)

    def part_spec(p):
        return pl.BlockSpec((t, GW_BLK), lambda i: (0, jnp.clip(i - first[p], 0, nblk[p] - 1)))

    return pl.pallas_call(
        body,
        name="gw_in",
        grid=(nstep,),
        in_specs=[part_spec(p) for p in range(len(nblk))]
        + [pl.BlockSpec((t, D_MODEL), lambda i: (0, 0), pipeline_mode=pl.Buffered(1))],
        out_specs=pl.BlockSpec((GW_BLK, D_MODEL), lambda i: (i, 0)),
        out_shape=jax.ShapeDtypeStruct((IN_W, D_MODEL), BF16),
        compiler_params=_cparams(dimension_semantics=("arbitrary",)),
    )(*dparts, h)


def _in_proj_bwd(dparts, x, dout, s1, nw, wt_full, gt, small0):
    t = x.shape[0]
    tm = 256
    nstep = t // tm
    n_sem = 7 + RS_SEMS
    rows0 = small0.shape[0]

    def body(dq_ref, dkv_ref, dga_ref, dua_ref, dug_ref, dgb_ref, x_ref, dout_ref, s1_ref, nw_ref, w_ref, gt_ref, sm0_ref,
             gx_ref, gw_ref, ssum_ref, tail_ref,
             mine_ref, sib_ref, out_ref, in_ref, res_ref, sall_ref, red_ref, lsem, ssem, rsem):
        i = pl.program_id(0)
        x_, y_, c, chips = _place()
        j = 2 * x_ + y_
        dev = 2 * j + c
        sib = (x_, y_, 1 - c)
        rc = functools.partial(_remote, ssem, rsem)
        theirs = gt_ref.at[:, 1 - c]
        fetch = pltpu.make_async_copy(gt_ref.at[:, c], mine_ref, lsem)
        peers = [(px, py, pc) for px in (x_, 1 - x_) for py in (y_, 1 - y_) for pc in (c, 1 - c)][1:]

        @pl.when(i == 0)
        def _():
            red_ref[...] = jnp.zeros_like(red_ref)
            fetch.start()
            _rs_to_sibling(rc, 7, theirs, sib_ref, sib)

        xv = x_ref[...]
        r = lax.rsqrt(jnp.mean(xv * xv, axis=-1, keepdims=True) + EPS)
        xh = xv * r
        n = xh * nw_ref[...]
        dproj = jnp.concatenate([dq_ref[...], dkv_ref[...], dga_ref[...], dua_ref[...], dug_ref[...], dgb_ref[...]], axis=1)
        dh = jnp.dot(dproj, w_ref[...], preferred_element_type=F32)
        red_ref[0:1, :] += jnp.sum(dh, axis=0, keepdims=True)
        red_ref[1:2, :] += jnp.sum(dh * n, axis=0, keepdims=True)
        dn = dh * s1_ref[...]
        red_ref[2:3, :] += jnp.sum(dn * xh, axis=0, keepdims=True)
        dxh = dn * nw_ref[...]
        gx_ref[...] = dout_ref[...] + r * (dxh - xh * jnp.mean(dxh * xh, axis=-1, keepdims=True))

        @pl.when(i == 0)
        def _():
            fetch.wait()
            _rs_trade(rc, 7, theirs, mine_ref, sib_ref, out_ref, in_ref, IN_HALF, c, sib, chips)

        @pl.when(i == nstep - 1)
        def _():
            sall_ref[dev, 0:rows0, :] = sm0_ref[...]
            sall_ref[dev, rows0:rows0 + 8, :] = red_ref[...]
            sends = [rc(k, sall_ref.at[dev], sall_ref.at[dev], peer) for k, peer in enumerate(peers)]
            for cp in sends:
                cp.start()
            sends.append(rc(7, theirs, sib_ref, sib))
            sends += [rc(8 + k, out_ref.at[k], in_ref.at[k], (cx, cy, c)) for k, (cx, cy) in enumerate(chips)]
            sends.append(_rs_total(rc, 7, mine_ref, sib_ref, out_ref, in_ref, res_ref, IN_HALF, j, c, sib))
            for k, (px, py, pc) in enumerate(peers):
                pdev = 4 * px + 2 * py + pc
                rc(k, sall_ref.at[pdev], sall_ref.at[pdev], (px, py, pc)).wait_recv()
            tot = sall_ref[0]
            for d in range(1, N_DEV):
                tot = tot + sall_ref[d]
            ssum_ref[...] = tot
            tail_ref[...] = sall_ref[:, rows0 - 8:rows0 + 8, :]
            _rs_done(rc, 7, res_ref, c, sib)
            gw_ref[...] = res_ref[...]
            for cp in sends:
                cp.wait_send()

    row = lambda w: pl.BlockSpec((tm, w), lambda i: (i, 0))
    vec = pl.BlockSpec((1, D_MODEL), lambda i: (0, 0))
    const = lambda shape: pl.BlockSpec(shape, lambda i: (0,) * len(shape))
    return pl.pallas_call(
        body,
        name="in_proj_bwd",
        grid=(nstep,),
        in_specs=[row(w) for w in DPROJ_WIDTHS] + [row(D_MODEL), row(D_MODEL), vec, vec,
                  pl.BlockSpec((IN_W, D_MODEL), lambda i: (0, 0), pipeline_mode=pl.Buffered(1)),
                  pl.BlockSpec(memory_space=pl.ANY), const((rows0, D_MODEL))],
        out_specs=[row(D_MODEL), const((2, IN_HALF, D_MODEL)), const((rows0 + 8, D_MODEL)), const((N_DEV, 16, D_MODEL))],
        out_shape=[jax.ShapeDtypeStruct((t, D_MODEL), F32), jax.ShapeDtypeStruct((2, IN_HALF, D_MODEL), F32),
                   jax.ShapeDtypeStruct((rows0 + 8, D_MODEL), F32), jax.ShapeDtypeStruct((N_DEV, 16, D_MODEL), F32)],
        scratch_shapes=[pltpu.VMEM((N_CHIPS, IN_HALF, D_MODEL), BF16)] + _rs_scratch(IN_HALF)
        + [pltpu.VMEM((2, IN_HALF, D_MODEL), F32), pltpu.VMEM((N_DEV, rows0 + 8, D_MODEL), F32),
           pltpu.VMEM((8, D_MODEL), F32), pltpu.SemaphoreType.DMA(()),
           pltpu.SemaphoreType.DMA((n_sem,)), pltpu.SemaphoreType.DMA((n_sem,))],
        compiler_params=_cparams(dimension_semantics=("arbitrary",)),
    )(*dparts, x, dout, s1, nw, wt_full, gt, small0)


MESH = pl.DeviceIdType.MESH


def _place():
    x, y, c = lax.axis_index("x"), lax.axis_index("y"), lax.axis_index("c")
    chips = [(1 - x, y), (x, 1 - y), (1 - x, 1 - y)]
    return x, y, c, chips


def _remote(sems_s, sems_r, k, src, dst, to):
    return pltpu.make_async_remote_copy(src_ref=src, dst_ref=dst, send_sem=sems_s.at[k], recv_sem=sems_r.at[k],
                                        device_id=to, device_id_type=MESH)


RS_CH = 32
RS_SEMS = 5


def _rs_to_sibling(rc, s0, theirs, sib_ref, sib):
    cp = rc(s0, theirs, sib_ref, sib)
    cp.start()
    return cp


def _rs_trade(rc, s0, theirs, mine, sib_ref, out_ref, in_ref, rows, c, sib, chips):
    rc(s0, theirs, sib_ref, sib).wait_recv()
    cps = []
    for k, (cx, cy) in enumerate(chips):
        jk = 2 * cx + cy

        def add(i, carry, jk=jk, k=k):
            rr = pl.ds(pl.multiple_of(i * RS_CH, RS_CH), RS_CH)
            out_ref[k, rr, :] = (mine[jk, rr, :].astype(F32) + sib_ref[jk, rr, :].astype(F32)).astype(BF16)
            return carry

        lax.fori_loop(0, rows // RS_CH, add, 0)
        cps.append(rc(s0 + 1 + k, out_ref.at[k], in_ref.at[k], (cx, cy, c)))
        cps[-1].start()
    return cps


def _rs_total(rc, s0, mine, sib_ref, out_ref, in_ref, res_ref, rows, j, c, sib):
    for k in range(3):
        rc(s0 + 1 + k, out_ref.at[k], in_ref.at[k], sib).wait_recv()

    def total(i, carry):
        rr = pl.ds(pl.multiple_of(i * RS_CH, RS_CH), RS_CH)
        acc = mine[j, rr, :].astype(F32) + sib_ref[j, rr, :].astype(F32)
        for k in range(3):
            acc = acc + in_ref[k, rr, :].astype(F32)
        res_ref[c, rr, :] = acc
        return carry

    lax.fori_loop(0, rows // RS_CH, total, 0)
    cp = rc(s0 + 4, res_ref.at[c], res_ref.at[c], sib)
    cp.start()
    return cp


def _rs_done(rc, s0, res_ref, c, sib):
    rc(s0 + 4, res_ref.at[1 - c], res_ref.at[1 - c], sib).wait_recv()


def _rs_scratch(rows):
    return [pltpu.VMEM((N_CHIPS, rows, D_MODEL), BF16), pltpu.VMEM((3, rows, D_MODEL), BF16),
            pltpu.VMEM((3, rows, D_MODEL), BF16)]


def _gather_weights(wt, c_row, w_ada, b_sh):
    n_sem = 16

    def body(wt_ref, c_ref, wada_ref, bsh_ref, w4_ref, call_ref, ada_ref, part_ref, ssem, rsem):
        x, y, c, chips = _place()
        j = 2 * x + y
        dev = 2 * j + c
        sib = (x, y, 1 - c)
        idx = [2 * cx + cy for cx, cy in chips]
        rc = functools.partial(_remote, ssem, rsem)

        w4_ref[j] = wt_ref[...].astype(BF16)
        call_ref[dev] = c_ref[...]

        sends = []
        peers = [(px, py, pc) for px in (x, 1 - x) for py in (y, 1 - y) for pc in (c, 1 - c)][1:]
        for k, peer in enumerate(peers):
            sends.append(rc(k, call_ref.at[dev], call_ref.at[dev], peer))
        for k, chip in enumerate(chips):
            sends.append(rc(7 + k, w4_ref.at[j, c], w4_ref.at[j, c], (*chip, c)))
        for cp in sends:
            cp.start()

        for k, (px, py, pc) in enumerate(peers):
            pdev = 4 * px + 2 * py + pc
            rc(k, call_ref.at[pdev], call_ref.at[pdev], (px, py, pc)).wait_recv()
        rowid = lax.broadcasted_iota(jnp.int32, (N_DEV, D_MODEL), 0)
        call = jnp.zeros((N_DEV, D_MODEL), F32)
        for r in range(N_DEV):
            call = jnp.where(rowid == r, jnp.broadcast_to(call_ref[r], (N_DEV, D_MODEL)), call)
        part = jnp.dot(_silu(call).astype(BF16), wada_ref[...].astype(BF16), preferred_element_type=F32) + bsh_ref[...]
        for r in range(N_DEV):
            part_ref[r] = part[r:r + 1, :]
        ada_ref[j] = part_ref[dev]
        rows_out = []
        for k, chip in enumerate(chips):
            rows_out.append(rc(13 + k, part_ref.at[2 * idx[k] + c], ada_ref.at[j], (*chip, c)))
            rows_out[-1].start()

        passed = []
        for k, chip in enumerate(chips):
            jk = idx[k]
            rc(7 + k, w4_ref.at[jk, c], w4_ref.at[jk, c], sib).wait_recv()
            passed.append(rc(10 + k, w4_ref.at[jk, c], w4_ref.at[jk, c], sib))
            passed[-1].start()
        for k, chip in enumerate(chips):
            jk = idx[k]
            rc(10 + k, w4_ref.at[jk, 1 - c], w4_ref.at[jk, 1 - c], sib).wait_recv()
            rc(13 + k, ada_ref.at[jk], ada_ref.at[jk], sib).wait_recv()
        for cp in sends + rows_out + passed:
            cp.wait_send()

    vm = pl.BlockSpec(memory_space=pltpu.VMEM)
    return pl.pallas_call(
        body,
        name="gather_weights",
        in_specs=[vm] * 4,
        out_specs=[vm] * 3,
        out_shape=[jax.ShapeDtypeStruct((N_CHIPS, 2, IN_HALF, D_MODEL), BF16),
                   jax.ShapeDtypeStruct((N_DEV, 1, D_MODEL), F32),
                   jax.ShapeDtypeStruct((N_CHIPS, 1, ADA_SHARD), F32)],
        scratch_shapes=[pltpu.VMEM((N_DEV, 1, ADA_SHARD), F32),
                        pltpu.SemaphoreType.DMA((n_sem,)), pltpu.SemaphoreType.DMA((n_sem,))],
        compiler_params=_cparams(),
    )(wt, c_row, w_ada, b_sh)


def _adamw_math(w, g, m, v):
    m2 = ADAM_B1 * m + (1.0 - ADAM_B1) * g
    v2 = ADAM_B2 * v + (1.0 - ADAM_B2) * (g * g)
    m_hat = m2 / (1.0 - ADAM_B1 ** ADAM_STEP)
    v_hat = v2 / (1.0 - ADAM_B2 ** ADAM_STEP)
    delta = -ADAM_LR * (m_hat / (jnp.sqrt(v_hat) + ADAM_EPS) + ADAM_WD * w)
    return delta, m2, v2


def _adamw(name, w, g, m, v, tm):
    r, cdim = w.shape

    def body(w_ref, g_ref, m_ref, v_ref, d_ref, m2_ref, v2_ref):
        d_ref[...], m2_ref[...], v2_ref[...] = _adamw_math(w_ref[...], g_ref[...], m_ref[...], v_ref[...])

    blk = pl.BlockSpec((tm, cdim), lambda i: (i, 0))
    return pl.pallas_call(
        body,
        name=name,
        grid=(r // tm,),
        in_specs=[blk] * 4,
        out_specs=[blk] * 3,
        out_shape=[jax.ShapeDtypeStruct((r, cdim), F32)] * 3,
        compiler_params=_cparams(dimension_semantics=("arbitrary",)),
    )(w, g, m, v)


def _adamw_ada(w, m, v, cact_t, dcols):
    r, cdim = w.shape
    tm = 256

    def body(w_ref, m_ref, v_ref, ct_ref, dc_ref, g_ref, d_ref, m2_ref, v2_ref):
        g = jnp.dot(ct_ref[...], dc_ref[...], preferred_element_type=F32, precision=lax.Precision.HIGHEST)
        g_ref[...] = g
        d_ref[...], m2_ref[...], v2_ref[...] = _adamw_math(w_ref[...], g, m_ref[...], v_ref[...])

    blk = pl.BlockSpec((tm, cdim), lambda i: (i, 0))
    return pl.pallas_call(
        body,
        name="adamw_w_ada",
        grid=(r // tm,),
        in_specs=[blk] * 3 + [pl.BlockSpec((tm, N_DEV), lambda i: (i, 0)), pl.BlockSpec((N_DEV, cdim), lambda i: (0, 0))],
        out_specs=[blk] * 4,
        out_shape=[jax.ShapeDtypeStruct((r, cdim), F32)] * 4,
        compiler_params=_cparams(dimension_semantics=("arbitrary",)),
    )(w, m, v, cact_t, dcols)


def _adamw_small(ws, gs, ms, vs):
    n = len(ws)

    def body(*refs):
        w_r, g_r, m_r, v_r = refs[0:n], refs[n:2 * n], refs[2 * n:3 * n], refs[3 * n:4 * n]
        d_r, m2_r, v2_r = refs[4 * n:5 * n], refs[5 * n:6 * n], refs[6 * n:7 * n]
        for i in range(n):
            d_r[i][...], m2_r[i][...], v2_r[i][...] = _adamw_math(w_r[i][...], g_r[i][...], m_r[i][...], v_r[i][...])

    vm = pl.BlockSpec(memory_space=pltpu.VMEM)
    shapes = [jax.ShapeDtypeStruct(w.shape, F32) for w in ws]
    out = pl.pallas_call(
        body,
        name="adamw_small",
        in_specs=[vm] * (4 * n),
        out_specs=[vm] * (3 * n),
        out_shape=shapes * 3,
        compiler_params=_cparams(),
    )(*ws, *gs, *ms, *vs)
    return out[0:n], out[n:2 * n], out[2 * n:3 * n]


def _rope_tables(t):
    inv = ROPE_THETA ** (-jnp.arange(0, HEAD_DIM, 2, dtype=F32) / HEAD_DIM)
    ang = jnp.arange(t, dtype=F32)[:, None] * inv[None, :]
    cos, sin = jnp.cos(ang), jnp.sin(ang)
    return jnp.tile(cos, (1, 4)), jnp.tile(jnp.concatenate([-sin, sin], axis=1), (1, 2))


def _pad_lanes(v, width):
    return jnp.pad(v, ((0, 0), (0, width - v.shape[1])))


def kernel(x, c, w_ada, b_ada, norm_w, w_in, q_norm_w, k_norm_w, sinks, conv_w, conv_b, ln_w, ln_b, w_out, loss_target, m_w_ada, m_b_ada, m_norm_w, m_w_in, m_q_norm_w, m_k_norm_w, m_sinks, m_conv_w, m_conv_b, m_ln_w, m_ln_b, m_w_out, v_w_ada, v_b_ada, v_norm_w, v_w_in, v_q_norm_w, v_k_norm_w, v_sinks, v_conv_w, v_conv_b, v_ln_w, v_ln_b, v_w_out):
    xi, yi = lax.axis_index("x"), lax.axis_index("y")
    j = 2 * xi + yi
    x2, tgt = x[0], loss_target[0]
    t = x2.shape[0]

    wt_s, mt_s, vt_s = w_in[0].T, m_w_in[0].T, v_w_in[0].T
    cw_pad = jnp.pad(conv_w[0], ((0, 1), (0, 0)))
    b_sh = lax.dynamic_slice(b_ada, (0, ADA_SHARD * j), (1, ADA_SHARD))

    w4, call, ada4 = _gather_weights(wt_s.reshape(2, IN_HALF, D_MODEL), c, w_ada[0], b_sh)
    w_full = w4.reshape(IN_W, D_MODEL)
    ada = ada4.reshape(1, 3 * D_MODEL)
    shift, s1, gate = ada[:, :D_MODEL], 1.0 + ada[:, D_MODEL:2 * D_MODEL], ada[:, 2 * D_MODEL:]

    cos_f, sin_s = _rope_tables(t)
    qw2, kw2 = jnp.tile(q_norm_w, (1, 2)), jnp.tile(k_norm_w, (1, 2))

    q_raw, kv_raw, ga, ua, ug, gb, h = _in_proj(x2, s1, shift, norm_w, w_full)
    o, mix_a, wo4, cw4 = _attn_fwd(q_raw, kv_raw, ga, qw2, kw2, sinks, cos_f, sin_s,
                                   w_out[0].reshape(2, OUT_HALF, D_MODEL), cw_pad)
    w_out_full = wo4.reshape(D_MODEL, D_MODEL)
    cw_full = jnp.concatenate([cw4[i] for i in range(N_CHIPS)], axis=1)
    cz, mix_b = _conv_fwd(ua, ug, gb, cw_full, conv_b, ln_w, ln_b)
    dout, dmix_a, dmix_b, gwo_bf, red_o = _out_proj(mix_a, mix_b, x2, tgt, gate, w_out_full)

    dq, dkv, dga, sm_a, gwo = _attn_bwd(q_raw, kv_raw, ga, o, dmix_a, qw2, kw2, sinks, cos_f, sin_s,
                                        gwo_bf.reshape(N_CHIPS, 2, OUT_HALF, D_MODEL))
    dua, dug, dgb, dcw, dvec = _conv_bwd(ua, ug, gb, cz, dmix_b, cw_full, ln_w, ln_b)
    dparts = (dq, dkv, dga, dua, dug, dgb)
    gt = _gw_in(dparts, h)

    small0 = jnp.concatenate([
        dcw.reshape(16, D_MODEL), jnp.pad(dvec.reshape(4, D_MODEL), ((0, 4), (0, 0))), _pad_lanes(sm_a, D_MODEL), red_o], axis=0)
    grad_x, gw, ssum, tail = _in_proj_bwd(dparts, x2, dout, s1, norm_w, w_full,
                                          gt.reshape(N_CHIPS, 2, IN_HALF, D_MODEL), small0)

    loss = (0.5 / D_MODEL) * jnp.sum(ssum[33])
    gt_w_in = gw.reshape(2 * IN_HALF, D_MODEL)
    g_w_out = gwo.reshape(D_MODEL // N_CHIPS, D_MODEL)
    g_conv_w = lax.dynamic_slice(ssum[0:16].reshape(32, CONV_W), (0, 128 * j), (CONV_TAPS, 128))
    g_vec = ssum[16:20].reshape(8, CONV_W)
    g_conv_b, g_ln_w, g_ln_b = g_vec[0:1], g_vec[1:2], g_vec[2:3]
    g_qw, g_kw, g_sinks = ssum[24:25, 0:HEAD_DIM], ssum[25:26, 0:HEAD_DIM], ssum[26:27, 0:8]
    g_norm_w = ssum[42:43]
    g_b_ada = jnp.concatenate([ssum[40:41], ssum[41:42], ssum[32:33]], axis=1)
    d_ada_all = jnp.concatenate([tail[:, 8], tail[:, 9], tail[:, 0]], axis=1)
    dcols = lax.dynamic_slice(d_ada_all, (0, ADA_SHARD * j), (N_DEV, ADA_SHARD))
    cact_t = jax.nn.silu(call.reshape(N_DEV, D_MODEL)).T

    g_w_ada, d_w_ada, nm_w_ada, nv_w_ada = _adamw_ada(w_ada[0], m_w_ada[0], v_w_ada[0], cact_t, dcols)
    dt_w_in, nmt_w_in, nvt_w_in = _adamw("adamw_w_in", wt_s, gt_w_in, mt_s, vt_s, 176)
    g_w_in, d_w_in, nm_w_in, nv_w_in = gt_w_in.T, dt_w_in.T, nmt_w_in.T, nvt_w_in.T
    d_w_out, nm_w_out, nv_w_out = _adamw("adamw_w_out", w_out[0], g_w_out, m_w_out[0], v_w_out[0], 128)
    ws = [b_ada, norm_w, q_norm_w, k_norm_w, sinks, conv_w[0], conv_b, ln_w, ln_b]
    gs = [g_b_ada, g_norm_w, g_qw, g_kw, g_sinks, g_conv_w, g_conv_b, g_ln_w, g_ln_b]
    ms = [m_b_ada, m_norm_w, m_q_norm_w, m_k_norm_w, m_sinks, m_conv_w[0], m_conv_b, m_ln_w, m_ln_b]
    vs = [v_b_ada, v_norm_w, v_q_norm_w, v_k_norm_w, v_sinks, v_conv_w[0], v_conv_b, v_ln_w, v_ln_b]
    ds, nms, nvs = _adamw_small(ws, gs, ms, vs)

    def order(ada_v, in_v, out_v, sm):
        b, nw_, qw_, kw_, sk_, cw_, cb_, lw_, lb_ = sm
        return [ada_v[None], b, nw_, in_v[None], qw_, kw_, sk_, cw_[None], cb_, lw_, lb_, out_v[None]]

    grads = order(g_w_ada, g_w_in, g_w_out, gs)
    deltas = order(d_w_ada, d_w_in, d_w_out, ds)
    new_m = order(nm_w_ada, nm_w_in, nm_w_out, nms)
    new_v = order(nv_w_ada, nv_w_in, nv_w_out, nvs)
    return (loss, grad_x[None], *grads, *deltas, *new_m, *new_v)
```

```python
import functools

import jax
import jax.numpy as jnp
from jax import lax
from jax.experimental import pallas as pl
from jax.experimental.pallas import tpu as pltpu

F32 = jnp.float32
BF16 = jnp.bfloat16

D_MODEL = 1024
ATTN_W = 512
KV_W = 128
CONV_W = 512
IN_W = 2816
HEAD_DIM = 64
CONV_TAPS = 31
QBLK = 128
EPS = 1e-6
ROPE_THETA = 10000.0

ADAM_LR = 0.001
ADAM_B1 = 0.9
ADAM_B2 = 0.999
ADAM_EPS = 1e-08
ADAM_WD = 0.01
ADAM_STEP = 10

N_CHIPS = 4
N_DEV = 8
IN_HALF = IN_W // N_CHIPS // 2
OUT_HALF = D_MODEL // N_CHIPS // 2
ADA_SHARD = 3 * D_MODEL // N_CHIPS

VMEM_LIMIT = 56 * 1024 * 1024
CONV_PAD = 32


def _cparams(**kw):
    return pltpu.CompilerParams(vmem_limit_bytes=VMEM_LIMIT, **kw)


def _sigmoid(v):
    return 1.0 / (1.0 + jnp.exp(-v))


def _silu(v):
    return v * _sigmoid(v)


def _dsilu(v):
    s = _sigmoid(v)
    return s * (1.0 + v * (1.0 - s))


def _lane(shape):
    return lax.broadcasted_iota(jnp.int32, shape, len(shape) - 1)


def _in_proj(x, s1, shift, nw, wt_full):
    t = x.shape[0]
    tm = 256

    def body(x_ref, s1_ref, sh_ref, nw_ref, w_ref, q_ref, kv_ref, ga_ref, ua_ref, ug_ref, gb_ref, h_ref):
        xv = x_ref[...]
        r = lax.rsqrt(jnp.mean(xv * xv, axis=-1, keepdims=True) + EPS)
        h = ((xv * r) * nw_ref[...] * s1_ref[...] + sh_ref[...]).astype(BF16)
        h_ref[...] = h
        p = lax.dot_general(h, w_ref[...], (((1,), (1,)), ((), ())), preferred_element_type=F32)
        q_ref[...] = p[:, 0:512]
        kv_ref[...] = p[:, 512:768]
        ga_ref[...] = p[:, 768:1280]
        ua_ref[...] = p[:, 1280:1792]
        ug_ref[...] = p[:, 1792:2304]
        gb_ref[...] = p[:, 2304:2816]

    row = lambda w: pl.BlockSpec((tm, w), lambda i: (i, 0))
    vec = pl.BlockSpec((1, D_MODEL), lambda i: (0, 0))
    return pl.pallas_call(
        body,
        name="in_proj",
        grid=(t // tm,),
        in_specs=[row(D_MODEL), vec, vec, vec,
                  pl.BlockSpec((IN_W, D_MODEL), lambda i: (0, 0), pipeline_mode=pl.Buffered(1))],
        out_specs=[row(512), row(256), row(512), row(512), row(512), row(512), row(D_MODEL)],
        out_shape=[jax.ShapeDtypeStruct((t, w), F32) for w in (512, 256, 512, 512, 512, 512)]
        + [jax.ShapeDtypeStruct((t, D_MODEL), BF16)],
        compiler_params=_cparams(dimension_semantics=("arbitrary",)),
    )(x, s1, shift, nw, wt_full)


def _head_mean(s, left):
    sl = jnp.sum(jnp.where(left, s, 0.0), axis=-1, keepdims=True)
    sr = jnp.sum(jnp.where(left, 0.0, s), axis=-1, keepdims=True)
    return jnp.where(left, sl, sr) * (1.0 / HEAD_DIM)


def _rot(v, first):
    return jnp.where(first, pltpu.roll(v, 96, 1), pltpu.roll(v, 32, 1))


def _norm_rope(v, w, cos, sin_s, left, first):
    r = lax.rsqrt(_head_mean(v * v, left) + EPS)
    xh = v * r
    n = xh * w
    return n * cos + _rot(n, first) * sin_s, xh, r


def _norm_rope_bwd(d, xh, r, w, cos, sin_s, left, first):
    dn = d * cos - _rot(d, first) * sin_s
    dw = jnp.sum(dn * xh, axis=0, keepdims=True)
    dxh = dn * w
    return r * (dxh - xh * _head_mean(dxh * xh, left)), dw


def _dup_heads(v, left):
    sw = pltpu.roll(v, 64, 1)
    return jnp.where(left, v, sw), jnp.where(left, sw, v)


def _prep_kv(kv_ref, kw_ref, cos_ref, sin_ref, ka_ref, va_ref, t):
    ch = 256
    for g in range(2):
        ka_ref[g, 0:QBLK, :] = jnp.zeros((QBLK, 128), BF16)
        va_ref[g, 0:QBLK, :] = jnp.zeros((QBLK, 128), BF16)

    def chunk(i, carry):
        r0 = pl.multiple_of(i * ch, ch)
        left = _lane((ch, 128)) < 64
        first = (_lane((ch, 128)) % 64) < 32
        k = kv_ref[pl.ds(r0, ch), 0:128]
        v = kv_ref[pl.ds(r0, ch), 128:256]
        kr, _, _ = _norm_rope(k, kw_ref[...], cos_ref[pl.ds(r0, ch), :], sin_ref[pl.ds(r0, ch), :], left, first)
        k0, k1 = _dup_heads(kr, left)
        v0, v1 = _dup_heads(v, left)
        ka_ref[0, pl.ds(QBLK + r0, ch), :] = k0.astype(BF16)
        ka_ref[1, pl.ds(QBLK + r0, ch), :] = k1.astype(BF16)
        va_ref[0, pl.ds(QBLK + r0, ch), :] = v0.astype(BF16)
        va_ref[1, pl.ds(QBLK + r0, ch), :] = v1.astype(BF16)
        return carry

    lax.fori_loop(0, t // ch, chunk, 0)


def _band_mask(n):
    qi = lax.broadcasted_iota(jnp.int32, (2 * QBLK, 2 * QBLK), 0) % QBLK
    kj = lax.broadcasted_iota(jnp.int32, (2 * QBLK, 2 * QBLK), 1)
    local = (kj > qi) & (kj <= qi + QBLK)
    return local & ((n > 0) | (kj >= QBLK))


def _softmax_pair(s, mask, sink0, sink1):
    row = lax.broadcasted_iota(jnp.int32, (2 * QBLK, 1), 0)
    sink = jnp.where(row < QBLK, sink0, sink1)
    s = jnp.where(mask, s, -jnp.inf)
    m = jnp.maximum(jnp.max(s, axis=-1, keepdims=True), sink)
    e = jnp.exp(s - m)
    es = jnp.exp(sink - m)
    inv = 1.0 / (jnp.sum(e, axis=-1, keepdims=True) + es)
    return e * inv, es * inv


def _stack_heads(v, left):
    return jnp.concatenate([jnp.where(left, v, 0.0), jnp.where(left, 0.0, v)], axis=0)


def _attn_fwd(q_raw, kv_raw, ga, qw2, kw2, sinks, cos_f, sin_s, wo, cw):
    t = q_raw.shape[0]
    nblk = t // QBLK

    def body(q_ref, kv_ref, ga_ref, qw_ref, kw_ref, sk_ref, cos_ref, sin_ref, wo_ref, cw_ref,
             o_ref, mix_ref, wo4_ref, cw4_ref, ka_ref, va_ref, ssem, rsem):
        x, y, c, chips = _place()
        j = 2 * x + y
        sib = (x, y, 1 - c)
        idx = [2 * cx + cy for cx, cy in chips]
        rc = functools.partial(_remote, ssem, rsem)
        wo4_ref[j] = wo_ref[...].astype(BF16)
        cw4_ref[j] = cw_ref[...]
        sends = []
        for k, chip in enumerate(chips):
            sends.append(rc(k, wo4_ref.at[j, c], wo4_ref.at[j, c], (*chip, c)))
            sends.append(rc(6 + k, cw4_ref.at[j], cw4_ref.at[j], (*chip, c)))
        for cp in sends:
            cp.start()

        _prep_kv(kv_ref, kw_ref, cos_ref, sin_ref, ka_ref, va_ref, t)

        def blk(n, carry):
            r0 = pl.multiple_of(n * QBLK, QBLK)
            left = _lane((QBLK, 128)) < 64
            first = (_lane((QBLK, 128)) % 64) < 32
            cos = cos_ref[pl.ds(r0, QBLK), :]
            sin = sin_ref[pl.ds(r0, QBLK), :]
            mask = _band_mask(n)
            for p in range(4):
                g = p // 2
                lanes = slice(p * 128, (p + 1) * 128)
                qr, _, _ = _norm_rope(q_ref[pl.ds(r0, QBLK), lanes], qw_ref[...], cos, sin, left, first)
                q2 = _stack_heads(qr * 0.125, left).astype(BF16)
                s = lax.dot_general(q2, ka_ref[g, pl.ds(r0, 2 * QBLK), :], (((1,), (1,)), ((), ())),
                                    preferred_element_type=F32)
                pm, _ = _softmax_pair(s, mask, sk_ref[0, 2 * p], sk_ref[0, 2 * p + 1])
                o2 = jnp.dot(pm.astype(BF16), va_ref[g, pl.ds(r0, 2 * QBLK), :], preferred_element_type=F32)
                o = jnp.where(left, o2[0:QBLK], o2[QBLK:2 * QBLK])
                o_ref[pl.ds(r0, QBLK), lanes] = o
                mix_ref[pl.ds(r0, QBLK), lanes] = (o * _silu(ga_ref[pl.ds(r0, QBLK), lanes])).astype(BF16)
            return carry

        lax.fori_loop(0, nblk, blk, 0)

        passed = []
        for k, chip in enumerate(chips):
            jk = idx[k]
            rc(k, wo4_ref.at[jk, c], wo4_ref.at[jk, c], sib).wait_recv()
            passed.append(rc(3 + k, wo4_ref.at[jk, c], wo4_ref.at[jk, c], sib))
            passed[-1].start()
        for k, chip in enumerate(chips):
            jk = idx[k]
            rc(3 + k, wo4_ref.at[jk, 1 - c], wo4_ref.at[jk, 1 - c], sib).wait_recv()
            rc(6 + k, cw4_ref.at[jk], cw4_ref.at[jk], sib).wait_recv()
        for cp in sends + passed:
            cp.wait_send()

    vm = pl.BlockSpec(memory_space=pltpu.VMEM)
    n_sem = 9
    return pl.pallas_call(
        body,
        name="attn_fwd",
        in_specs=[vm, vm, vm, vm, vm, pl.BlockSpec(memory_space=pltpu.SMEM), vm, vm, vm, vm],
        out_specs=[vm] * 4,
        out_shape=[jax.ShapeDtypeStruct((t, ATTN_W), F32), jax.ShapeDtypeStruct((t, ATTN_W), BF16),
                   jax.ShapeDtypeStruct((N_CHIPS, 2, OUT_HALF, D_MODEL), BF16),
                   jax.ShapeDtypeStruct((N_CHIPS, 32, 128), F32)],
        scratch_shapes=[pltpu.VMEM((2, t + QBLK, 128), BF16), pltpu.VMEM((2, t + QBLK, 128), BF16),
                        pltpu.SemaphoreType.DMA((n_sem,)), pltpu.SemaphoreType.DMA((n_sem,))],
        compiler_params=_cparams(),
    )(q_raw, kv_raw, ga, qw2, kw2, sinks, cos_f, sin_s, wo, cw)


def _attn_bwd(q_raw, kv_raw, ga, o, dmix, qw2, kw2, sinks, cos_f, sin_s, go):
    t = q_raw.shape[0]
    nblk = t // QBLK

    def body(q_ref, kv_ref, ga_ref, o_ref, dm_ref, qw_ref, kw_ref, sk_ref, cos_ref, sin_ref, go_ref,
             dq_ref, dkv_ref, dga_ref, sm_ref, gwo_ref, ka_ref, va_ref, dka_ref, dva_ref,
             sibo_ref, outo_ref, ino_ref, ssem, rsem):
        x, y, c, chips = _place()
        sib = (x, y, 1 - c)
        rc = functools.partial(_remote, ssem, rsem)
        theirs, mine = go_ref.at[:, 1 - c], go_ref.at[:, c]
        sends = [_rs_to_sibling(rc, 0, theirs, sibo_ref, sib)]
        _prep_kv(kv_ref, kw_ref, cos_ref, sin_ref, ka_ref, va_ref, t)
        dka_ref[...] = jnp.zeros_like(dka_ref)
        dva_ref[...] = jnp.zeros_like(dva_ref)
        sends += _rs_trade(rc, 0, theirs, mine, sibo_ref, outo_ref, ino_ref, OUT_HALF, c, sib, chips)

        def blk(n, carry):
            dqw, dsk = carry
            r0 = pl.multiple_of(n * QBLK, QBLK)
            left = _lane((QBLK, 128)) < 64
            first = (_lane((QBLK, 128)) % 64) < 32
            cos = cos_ref[pl.ds(r0, QBLK), :]
            sin = sin_ref[pl.ds(r0, QBLK), :]
            mask = _band_mask(n)
            row = lax.broadcasted_iota(jnp.int32, (2 * QBLK, 1), 0)
            for p in range(4):
                g = p // 2
                lanes = slice(p * 128, (p + 1) * 128)
                rows = pl.ds(r0, QBLK)
                win = pl.ds(r0, 2 * QBLK)
                qr, xh, r = _norm_rope(q_ref[rows, lanes], qw_ref[...], cos, sin, left, first)
                q2 = _stack_heads(qr * 0.125, left).astype(BF16)
                kwin = ka_ref[g, win, :]
                vwin = va_ref[g, win, :]
                s = lax.dot_general(q2, kwin, (((1,), (1,)), ((), ())), preferred_element_type=F32)
                pm, ps = _softmax_pair(s, mask, sk_ref[0, 2 * p], sk_ref[0, 2 * p + 1])
                gav = ga_ref[rows, lanes]
                dmv = dm_ref[rows, lanes]
                dga_ref[rows, lanes] = (dmv * o_ref[rows, lanes] * _dsilu(gav)).astype(BF16)
                do2 = _stack_heads(dmv * _silu(gav), left).astype(BF16)
                dp = lax.dot_general(do2, vwin, (((1,), (1,)), ((), ())), preferred_element_type=F32)
                delta = jnp.sum(pm * dp, axis=-1, keepdims=True)
                ds = (pm * (dp - delta)).astype(BF16)
                pd = ps * delta
                d0 = jnp.sum(jnp.where(row < QBLK, pd, 0.0), axis=0, keepdims=True)
                d1 = jnp.sum(jnp.where(row < QBLK, 0.0, pd), axis=0, keepdims=True)
                l8 = _lane((1, 128))
                dsk = dsk - jnp.where(l8 == 2 * p, d0, 0.0) - jnp.where(l8 == 2 * p + 1, d1, 0.0)
                dva_ref[g, win, :] += lax.dot_general(pm.astype(BF16), do2, (((0,), (0,)), ((), ())),
                                                      preferred_element_type=F32)
                dka_ref[g, win, :] += lax.dot_general(ds, q2, (((0,), (0,)), ((), ())),
                                                      preferred_element_type=F32)
                dq2 = jnp.dot(ds, kwin, preferred_element_type=F32)
                dqr = jnp.where(left, dq2[0:QBLK], dq2[QBLK:2 * QBLK]) * 0.125
                dq, dw = _norm_rope_bwd(dqr, xh, r, qw_ref[...], cos, sin, left, first)
                dq_ref[rows, lanes] = dq.astype(BF16)
                dqw = dqw + dw
            return dqw, dsk

        zero = jnp.zeros((1, 128), F32)
        dqw, dsk = lax.fori_loop(0, nblk, blk, (zero, zero))

        ch = 256

        def chunk(i, dkw):
            r0 = pl.multiple_of(i * ch, ch)
            left = _lane((ch, 128)) < 64
            first = (_lane((ch, 128)) % 64) < 32
            rows = pl.ds(r0, ch)
            prow = pl.ds(QBLK + r0, ch)

            def fold(ref):
                a0 = ref[0, prow, :]
                a1 = ref[1, prow, :]
                return jnp.where(left, a0 + pltpu.roll(a0, 64, 1), a1 + pltpu.roll(a1, 64, 1))

            cos = cos_ref[rows, :]
            sin = sin_ref[rows, :]
            _, xh, r = _norm_rope(kv_ref[rows, 0:128], kw_ref[...], cos, sin, left, first)
            dk, dw = _norm_rope_bwd(fold(dka_ref), xh, r, kw_ref[...], cos, sin, left, first)
            dkv_ref[rows, 0:128] = dk.astype(BF16)
            dkv_ref[rows, 128:256] = fold(dva_ref).astype(BF16)
            return dkw + dw

        dkw = lax.fori_loop(0, t // ch, chunk, zero)
        sm_ref[...] = jnp.zeros((8, 128), F32)
        sm_ref[0:1, :] = dqw + pltpu.roll(dqw, 64, 1)
        sm_ref[1:2, :] = dkw + pltpu.roll(dkw, 64, 1)
        sm_ref[2:3, :] = dsk

        j = 2 * x + y
        sends.append(_rs_total(rc, 0, mine, sibo_ref, outo_ref, ino_ref, gwo_ref, OUT_HALF, j, c, sib))
        _rs_done(rc, 0, gwo_ref, c, sib)
        for cp in sends:
            cp.wait_send()

    vm = pl.BlockSpec(memory_space=pltpu.VMEM)
    return pl.pallas_call(
        body,
        name="attn_bwd",
        in_specs=[vm, vm, vm, vm, vm, vm, vm, pl.BlockSpec(memory_space=pltpu.SMEM), vm, vm, vm],
        out_specs=[vm] * 5,
        out_shape=[jax.ShapeDtypeStruct((t, ATTN_W), BF16), jax.ShapeDtypeStruct((t, 2 * KV_W), BF16),
                   jax.ShapeDtypeStruct((t, ATTN_W), BF16), jax.ShapeDtypeStruct((8, 128), F32),
                   jax.ShapeDtypeStruct((2, OUT_HALF, D_MODEL), F32)],
        scratch_shapes=[pltpu.VMEM((2, t + QBLK, 128), BF16), pltpu.VMEM((2, t + QBLK, 128), BF16),
                        pltpu.VMEM((2, t + QBLK, 128), F32), pltpu.VMEM((2, t + QBLK, 128), F32)]
        + _rs_scratch(OUT_HALF) + [pltpu.SemaphoreType.DMA((RS_SEMS,)), pltpu.SemaphoreType.DMA((RS_SEMS,))],
        compiler_params=_cparams(),
    )(q_raw, kv_raw, ga, o, dmix, qw2, kw2, sinks, cos_f, sin_s, go)


CONV_CH = 256
CONV_SUB = 64


def _shifted_windows(src_ref, r0, sh_ref):
    rows = CONV_CH + CONV_PAD
    win = src_ref[pl.ds(r0, rows), :]
    sh_ref[0] = win
    for b in range(1, 8):
        sh_ref[b] = pltpu.roll(win, rows - b, 0)


def _conv_fwd(ua, ug, gb, cw, cb, lw, lb):
    t = ua.shape[0]

    def body(ua_ref, ug_ref, gb_ref, cw_ref, cb_ref, lw_ref, lb_ref, cz_ref, mix_ref, zp_ref, sh_ref):
        zp_ref[0:CONV_PAD, :] = jnp.zeros((CONV_PAD, CONV_W), F32)

        def glu(i, carry):
            r0 = pl.multiple_of(i * CONV_CH, CONV_CH)
            rows = pl.ds(r0, CONV_CH)
            zp_ref[pl.ds(CONV_PAD + r0, CONV_CH), :] = ua_ref[rows, :] * _sigmoid(ug_ref[rows, :])
            return carry

        lax.fori_loop(0, t // CONV_CH, glu, 0)

        def chunk(i, carry):
            r0 = pl.multiple_of(i * CONV_CH, CONV_CH)
            _shifted_windows(zp_ref, r0, sh_ref)
            for c in range(CONV_W // 128):
                lanes = slice(c * 128, (c + 1) * 128)

                def sub(k, carry2):
                    b0 = pl.multiple_of(k * CONV_SUB, CONV_SUB)
                    acc = jnp.broadcast_to(cb_ref[0:1, lanes], (CONV_SUB, 128))
                    for j in range(CONV_TAPS):
                        off = j + CONV_PAD - (CONV_TAPS - 1)
                        acc = acc + sh_ref[off % 8, pl.ds(b0 + 8 * (off // 8), CONV_SUB), lanes] * cw_ref[j:j + 1, lanes]
                    cz_ref[pl.ds(r0 + b0, CONV_SUB), lanes] = acc
                    return carry2

                lax.fori_loop(0, CONV_CH // CONV_SUB, sub, 0)
            rows = pl.ds(r0, CONV_CH)
            cz = cz_ref[rows, :]
            mu = jnp.mean(cz, axis=-1, keepdims=True)
            xc = cz - mu
            rs = lax.rsqrt(jnp.mean(xc * xc, axis=-1, keepdims=True) + EPS)
            ln = xc * rs * lw_ref[...] + lb_ref[...]
            mix_ref[rows, :] = (_silu(ln) * _silu(gb_ref[rows, :])).astype(BF16)
            return carry

        lax.fori_loop(0, t // CONV_CH, chunk, 0)

    vm = pl.BlockSpec(memory_space=pltpu.VMEM)
    return pl.pallas_call(
        body,
        name="conv_fwd",
        in_specs=[vm] * 7,
        out_specs=[vm, vm],
        out_shape=[jax.ShapeDtypeStruct((t, CONV_W), F32), jax.ShapeDtypeStruct((t, CONV_W), BF16)],
        scratch_shapes=[pltpu.VMEM((t + CONV_PAD, CONV_W), F32),
                        pltpu.VMEM((8, CONV_CH + CONV_PAD, CONV_W), F32)],
        compiler_params=_cparams(),
    )(ua, ug, gb, cw, cb, lw, lb)


def _conv_bwd(ua, ug, gb, cz, dmix, cw, lw, lb):
    t = ua.shape[0]

    def body(ua_ref, ug_ref, gb_ref, cz_ref, dm_ref, cw_ref, lw_ref, lb_ref,
             dua_ref, dug_ref, dgb_ref, dcw_ref, dvec_ref, zp_ref, dp_ref, sh_ref, wacc_ref):
        zp_ref[0:CONV_PAD, :] = jnp.zeros((CONV_PAD, CONV_W), F32)
        dp_ref[t:t + CONV_PAD, :] = jnp.zeros((CONV_PAD, CONV_W), F32)
        wacc_ref[...] = jnp.zeros_like(wacc_ref)

        def pointwise(i, carry):
            dcb, dlw, dlb = carry
            r0 = pl.multiple_of(i * CONV_CH, CONV_CH)
            rows = pl.ds(r0, CONV_CH)
            zp_ref[pl.ds(CONV_PAD + r0, CONV_CH), :] = ua_ref[rows, :] * _sigmoid(ug_ref[rows, :])
            cz = cz_ref[rows, :]
            mu = jnp.mean(cz, axis=-1, keepdims=True)
            xc = cz - mu
            rs = lax.rsqrt(jnp.mean(xc * xc, axis=-1, keepdims=True) + EPS)
            xh = xc * rs
            ln = xh * lw_ref[...] + lb_ref[...]
            gbv = gb_ref[rows, :]
            dy = dm_ref[rows, :]
            dgb_ref[rows, :] = (dy * _silu(ln) * _dsilu(gbv)).astype(BF16)
            dl = dy * _silu(gbv) * _dsilu(ln)
            dxh = dl * lw_ref[...]
            dcz = rs * (dxh - jnp.mean(dxh, axis=-1, keepdims=True)
                        - xh * jnp.mean(dxh * xh, axis=-1, keepdims=True))
            dp_ref[rows, :] = dcz
            return (dcb + jnp.sum(dcz, axis=0, keepdims=True),
                    dlw + jnp.sum(dl * xh, axis=0, keepdims=True),
                    dlb + jnp.sum(dl, axis=0, keepdims=True))

        zero = jnp.zeros((1, CONV_W), F32)
        dcb, dlw, dlb = lax.fori_loop(0, t // CONV_CH, pointwise, (zero, zero, zero))
        dvec_ref[...] = jnp.zeros((8, CONV_W), F32)
        dvec_ref[0:1, :] = dcb
        dvec_ref[1:2, :] = dlw
        dvec_ref[2:3, :] = dlb

        def chunk(i, carry):
            r0 = pl.multiple_of(i * CONV_CH, CONV_CH)
            _shifted_windows(dp_ref, r0, sh_ref)
            for c in range(CONV_W // 128):
                lanes = slice(c * 128, (c + 1) * 128)

                def sub(k, carry2):
                    b0 = pl.multiple_of(k * CONV_SUB, CONV_SUB)
                    acc = jnp.zeros((CONV_SUB, 128), F32)
                    for j in range(CONV_TAPS):
                        off = CONV_TAPS - 1 - j
                        acc = acc + sh_ref[off % 8, pl.ds(b0 + 8 * (off // 8), CONV_SUB), lanes] * cw_ref[j:j + 1, lanes]
                    rr = pl.ds(r0 + b0, CONV_SUB)
                    sg = _sigmoid(ug_ref[rr, lanes])
                    dua_ref[rr, lanes] = (acc * sg).astype(BF16)
                    dug_ref[rr, lanes] = (acc * ua_ref[rr, lanes] * sg * (1.0 - sg)).astype(BF16)
                    return carry2

                lax.fori_loop(0, CONV_CH // CONV_SUB, sub, 0)
            _shifted_windows(zp_ref, r0, sh_ref)
            for c in range(CONV_W // 128):
                lanes = slice(c * 128, (c + 1) * 128)

                def subw(k, carry2):
                    b0 = pl.multiple_of(k * CONV_SUB, CONV_SUB)
                    dcz = dp_ref[pl.ds(r0 + b0, CONV_SUB), lanes]
                    for j in range(CONV_TAPS):
                        off = j + CONV_PAD - (CONV_TAPS - 1)
                        pr = dcz * sh_ref[off % 8, pl.ds(b0 + 8 * (off // 8), CONV_SUB), lanes]
                        part = pr[0:8]
                        for q in range(1, CONV_SUB // 8):
                            part = part + pr[8 * q:8 * (q + 1)]
                        wacc_ref[8 * j:8 * (j + 1), lanes] += part
                    return carry2

                lax.fori_loop(0, CONV_CH // CONV_SUB, subw, 0)
            return carry

        lax.fori_loop(0, t // CONV_CH, chunk, 0)
        dcw_ref[...] = jnp.zeros((32, CONV_W), F32)
        for j in range(CONV_TAPS):
            dcw_ref[j:j + 1, :] = jnp.sum(wacc_ref[8 * j:8 * (j + 1), :], axis=0, keepdims=True)

    vm = pl.BlockSpec(memory_space=pltpu.VMEM)
    return pl.pallas_call(
        body,
        name="conv_bwd",
        in_specs=[vm] * 8,
        out_specs=[vm] * 5,
        out_shape=[jax.ShapeDtypeStruct((t, CONV_W), BF16)] * 3
        + [jax.ShapeDtypeStruct((32, CONV_W), F32), jax.ShapeDtypeStruct((8, CONV_W), F32)],
        scratch_shapes=[pltpu.VMEM((t + CONV_PAD, CONV_W), F32), pltpu.VMEM((t + CONV_PAD, CONV_W), F32),
                        pltpu.VMEM((8, CONV_CH + CONV_PAD, CONV_W), F32), pltpu.VMEM((8 * 32, CONV_W), F32)],
        compiler_params=_cparams(),
    )(ua, ug, gb, cz, dmix, cw, lw, lb)


def _out_proj(mix_a, mix_b, x, tgt, gate, w_out):
    t = x.shape[0]
    tm = 256
    nstep = t // tm

    def body(ma_ref, mb_ref, x_ref, t_ref, g_ref, w_ref, dout_ref, dma_ref, dmb_ref, gw_ref, red_ref, acc_ref):
        i = pl.program_id(0)

        @pl.when(i == 0)
        def _():
            acc_ref[...] = jnp.zeros_like(acc_ref)
            red_ref[...] = jnp.zeros_like(red_ref)

        mix = jnp.concatenate([ma_ref[...], mb_ref[...]], axis=1)
        y = jnp.dot(mix, w_ref[...], preferred_element_type=F32)
        gate_v = g_ref[...]
        err = x_ref[...] + gate_v * y - t_ref[...]
        dout = err * (1.0 / D_MODEL)
        dout_ref[...] = dout
        red_ref[0:1, :] += jnp.sum(dout * y, axis=0, keepdims=True)
        red_ref[1:2, :] += jnp.sum(err * err, axis=0, keepdims=True)
        dy = (dout * gate_v).astype(BF16)
        dmix = lax.dot_general(dy, w_ref[...], (((1,), (1,)), ((), ())), preferred_element_type=F32)
        dma_ref[...] = dmix[:, 0:512]
        dmb_ref[...] = dmix[:, 512:1024]
        acc_ref[...] += lax.dot_general(mix, dy, (((0,), (0,)), ((), ())), preferred_element_type=F32)

        @pl.when(i == nstep - 1)
        def _():
            gw_ref[...] = acc_ref[...].astype(BF16)

    row = lambda w: pl.BlockSpec((tm, w), lambda i: (i, 0))
    const = lambda s: pl.BlockSpec(s, lambda i: (0, 0))
    return pl.pallas_call(
        body,
        name="out_proj",
        grid=(nstep,),
        in_specs=[row(512), row(512), row(D_MODEL), row(D_MODEL), const((1, D_MODEL)),
                  pl.BlockSpec((D_MODEL, D_MODEL), lambda i: (0, 0), pipeline_mode=pl.Buffered(1))],
        out_specs=[row(D_MODEL), row(512), row(512), const((D_MODEL, D_MODEL)), const((8, D_MODEL))],
        out_shape=[jax.ShapeDtypeStruct((t, D_MODEL), F32), jax.ShapeDtypeStruct((t, 512), F32),
                   jax.ShapeDtypeStruct((t, 512), F32), jax.ShapeDtypeStruct((D_MODEL, D_MODEL), BF16),
                   jax.ShapeDtypeStruct((8, D_MODEL), F32)],
        scratch_shapes=[pltpu.VMEM((D_MODEL, D_MODEL), F32)],
        compiler_params=_cparams(dimension_semantics=("arbitrary",)),
    )(mix_a, mix_b, x, tgt, gate, w_out)


DPROJ_WIDTHS = (512, 256, 512, 512, 512, 512)
DPROJ_STARTS = (0, 512, 768, 1280, 1792, 2304)
WIN_W = 768
WIN_START = (0, 640, 1408, 2048)
WIN_OFF = (0, 64, 0, 64)
N_GW = N_CHIPS


def _window_pieces(s):
    lo, hi = WIN_START[s], WIN_START[s] + WIN_W
    out = []
    for p, (st, w) in enumerate(zip(DPROJ_STARTS, DPROJ_WIDTHS)):
        a, b = max(lo, st), min(hi, st + w)
        if a < b:
            out.append((p, a - st, b - a, a - lo))
    return out


def _in_proj_bwd(dparts, h, x, dout, s1, nw, wt_full, small0):
    t = x.shape[0]
    tm = 256
    nstep = N_GW + t // tm
    n_sem = 15
    rows0 = small0.shape[0]
    npart = len(DPROJ_WIDTHS)

    def body(*refs):
        d_hbm, d_ref = refs[:npart], refs[npart:2 * npart]
        (x_ref, dout_ref, s1_ref, nw_ref, h_ref, wt_hbm, sm0_ref,
         gx_ref, gw_hbm, ssum_ref, tail_ref,
         stg_ref, wt_ref, gt_ref, sib_ref, out_ref, in_ref, res_ref, sall_ref, red_ref,
         wsem, lsem, ssem, rsem) = refs[2 * npart:]
        i = pl.program_id(0)
        x_, y_, c, chips = _place()
        j = 2 * x_ + y_
        dev = 2 * j + c
        sib = (x_, y_, 1 - c)
        rc = functools.partial(_remote, ssem, rsem)
        rel_chip = [2 * cx + cy for cx, cy in chips] + [j]
        peers = [(px, py, pc) for px in (x_, 1 - x_) for py in (y_, 1 - y_) for pc in (c, 1 - c)][1:]
        wt_copy = pltpu.make_async_copy(wt_hbm, wt_ref, lsem.at[0])

        def window(case, slot):
            return [pltpu.make_async_copy(d_hbm[p].at[:, pl.ds(c0, w)], stg_ref.at[slot, :, pl.ds(w0, w)], wsem.at[slot, n])
                    for n, (p, c0, w, w0) in enumerate(_window_pieces(case))]

        def to_sibling(k):
            return rc(k, gt_ref.at[k, 1 - c], sib_ref.at[k], sib)

        def to_chip(k):
            return rc(4 + k, out_ref.at[k], in_ref.at[k], (*chips[k], c))

        def trade(k):
            to_sibling(k).wait_recv()

            def add(n, carry):
                rr = pl.ds(pl.multiple_of(n * RS_CH, RS_CH), RS_CH)
                out_ref[k, rr, :] = (gt_ref[k, c, rr, :].astype(F32) + sib_ref[k, rr, :].astype(F32)).astype(BF16)
                return carry

            lax.fori_loop(0, IN_HALF // RS_CH, add, 0)
            to_chip(k).start()

        for k in range(N_GW):
            @pl.when(i == k)
            def _(k=k):
                slot = k % 2
                if k == 0:
                    red_ref[...] = jnp.zeros_like(red_ref)
                    wt_copy.start()
                for case in range(N_CHIPS):
                    if k == 0:
                        @pl.when(rel_chip[0] == case)
                        def _():
                            for cp in window(case, 0):
                                cp.start()
                    if k + 1 < N_GW:
                        @pl.when(rel_chip[k + 1] == case)
                        def _():
                            for cp in window(case, 1 - slot):
                                cp.start()
                for case in range(N_CHIPS):
                    @pl.when(rel_chip[k] == case)
                    def _():
                        for cp in window(case, slot):
                            cp.wait()
                g = lax.dot_general(stg_ref[slot], h_ref[...], (((0,), (0,)), ((), ())), preferred_element_type=F32)
                for off in sorted(set(WIN_OFF)):
                    @pl.when(rel_chip[k] % 2 == (1 if off else 0))
                    def _():
                        gt_ref[k, 0] = g[off:off + IN_HALF].astype(BF16)
                        gt_ref[k, 1] = g[off + IN_HALF:off + 2 * IN_HALF].astype(BF16)
                to_sibling(k).start()
                if k >= 1:
                    trade(k - 1)

        @pl.when(i == N_GW)
        def _():
            wt_copy.wait()

        @pl.when(i >= N_GW)
        def _():
            xv = x_ref[...]
            r = lax.rsqrt(jnp.mean(xv * xv, axis=-1, keepdims=True) + EPS)
            xh = xv * r
            n = xh * nw_ref[...]
            dproj = jnp.concatenate([ref[...] for ref in d_ref], axis=1)
            dh = jnp.dot(dproj, wt_ref[...], preferred_element_type=F32)
            red_ref[0:1, :] += jnp.sum(dh, axis=0, keepdims=True)
            red_ref[1:2, :] += jnp.sum(dh * n, axis=0, keepdims=True)
            dn = dh * s1_ref[...]
            red_ref[2:3, :] += jnp.sum(dn * xh, axis=0, keepdims=True)
            dxh = dn * nw_ref[...]
            gx_ref[...] = dout_ref[...] + r * (dxh - xh * jnp.mean(dxh * xh, axis=-1, keepdims=True))

        @pl.when(i == nstep - 1)
        def _():
            sall_ref[dev, 0:rows0, :] = sm0_ref[...]
            sall_ref[dev, rows0:rows0 + 8, :] = red_ref[...]
            sends = [rc(8 + k, sall_ref.at[dev], sall_ref.at[dev], peer) for k, peer in enumerate(peers)]
            for cp in sends:
                cp.start()
            sends += [to_sibling(k) for k in range(N_GW)] + [to_chip(k) for k in range(3)]
            own = N_GW - 1
            to_sibling(own).wait_recv()
            for k in range(3):
                to_chip(k).wait_recv()

            def total(n, carry):
                rr = pl.ds(pl.multiple_of(n * RS_CH, RS_CH), RS_CH)
                acc = gt_ref[own, c, rr, :].astype(F32) + sib_ref[own, rr, :].astype(F32)
                for k in range(3):
                    acc = acc + in_ref[k, rr, :].astype(F32)
                res_ref[c, rr, :] = acc
                return carry

            lax.fori_loop(0, IN_HALF // RS_CH, total, 0)
            share = rc(7, res_ref.at[c], res_ref.at[c], sib)
            share.start()
            sends.append(share)
            for k, (px, py, pc) in enumerate(peers):
                pdev = 4 * px + 2 * py + pc
                rc(8 + k, sall_ref.at[pdev], sall_ref.at[pdev], (px, py, pc)).wait_recv()
            tot = sall_ref[0]
            for d in range(1, N_DEV):
                tot = tot + sall_ref[d]
            ssum_ref[...] = tot
            tail_ref[...] = sall_ref[:, rows0 - 8:rows0 + 8, :]
            rc(7, res_ref.at[1 - c], res_ref.at[1 - c], sib).wait_recv()
            back = pltpu.make_async_copy(res_ref, gw_hbm, lsem.at[1])
            back.start()
            for cp in sends:
                cp.wait_send()
            back.wait()

    blk = lambda i: jnp.maximum(i - N_GW, 0)
    row = lambda w: pl.BlockSpec((tm, w), lambda i: (blk(i), 0))
    vec = pl.BlockSpec((1, D_MODEL), lambda i: (0, 0))
    const = lambda shape: pl.BlockSpec(shape, lambda i: (0,) * len(shape))
    hbm = pl.BlockSpec(memory_space=pl.ANY)
    return pl.pallas_call(
        body,
        name="in_proj_bwd",
        grid=(nstep,),
        in_specs=[hbm] * npart + [row(w) for w in DPROJ_WIDTHS] + [row(D_MODEL), row(D_MODEL), vec, vec,
                  pl.BlockSpec((t, D_MODEL), lambda i: (0, 0), pipeline_mode=pl.Buffered(1)), hbm, const((rows0, D_MODEL))],
        out_specs=[row(D_MODEL), hbm, const((rows0 + 8, D_MODEL)), const((N_DEV, 16, D_MODEL))],
        out_shape=[jax.ShapeDtypeStruct((t, D_MODEL), F32), jax.ShapeDtypeStruct((2, IN_HALF, D_MODEL), F32),
                   jax.ShapeDtypeStruct((rows0 + 8, D_MODEL), F32), jax.ShapeDtypeStruct((N_DEV, 16, D_MODEL), F32)],
        scratch_shapes=[pltpu.VMEM((2, t, WIN_W), BF16), pltpu.VMEM((IN_W, D_MODEL), BF16),
                        pltpu.VMEM((N_CHIPS, 2, IN_HALF, D_MODEL), BF16), pltpu.VMEM((N_CHIPS, IN_HALF, D_MODEL), BF16),
                        pltpu.VMEM((3, IN_HALF, D_MODEL), BF16), pltpu.VMEM((3, IN_HALF, D_MODEL), BF16),
                        pltpu.VMEM((2, IN_HALF, D_MODEL), F32), pltpu.VMEM((N_DEV, rows0 + 8, D_MODEL), F32),
                        pltpu.VMEM((8, D_MODEL), F32), pltpu.SemaphoreType.DMA((2, 3)), pltpu.SemaphoreType.DMA((2,)),
                        pltpu.SemaphoreType.DMA((n_sem,)), pltpu.SemaphoreType.DMA((n_sem,))],
        compiler_params=_cparams(dimension_semantics=("arbitrary",)),
    )(*dparts, *dparts, x, dout, s1, nw, h, wt_full, small0)


MESH = pl.DeviceIdType.MESH


def _place():
    x, y, c = lax.axis_index("x"), lax.axis_index("y"), lax.axis_index("c")
    chips = [(1 - x, y), (x, 1 - y), (1 - x, 1 - y)]
    return x, y, c, chips


def _remote(sems_s, sems_r, k, src, dst, to):
    return pltpu.make_async_remote_copy(src_ref=src, dst_ref=dst, send_sem=sems_s.at[k], recv_sem=sems_r.at[k],
                                        device_id=to, device_id_type=MESH)


RS_CH = 32
RS_SEMS = 5


def _rs_to_sibling(rc, s0, theirs, sib_ref, sib):
    cp = rc(s0, theirs, sib_ref, sib)
    cp.start()
    return cp


def _rs_trade(rc, s0, theirs, mine, sib_ref, out_ref, in_ref, rows, c, sib, chips):
    rc(s0, theirs, sib_ref, sib).wait_recv()
    cps = []
    for k, (cx, cy) in enumerate(chips):
        jk = 2 * cx + cy

        def add(i, carry, jk=jk, k=k):
            rr = pl.ds(pl.multiple_of(i * RS_CH, RS_CH), RS_CH)
            out_ref[k, rr, :] = (mine[jk, rr, :].astype(F32) + sib_ref[jk, rr, :].astype(F32)).astype(BF16)
            return carry

        lax.fori_loop(0, rows // RS_CH, add, 0)
        cps.append(rc(s0 + 1 + k, out_ref.at[k], in_ref.at[k], (cx, cy, c)))
        cps[-1].start()
    return cps


def _rs_total(rc, s0, mine, sib_ref, out_ref, in_ref, res_ref, rows, j, c, sib):
    for k in range(3):
        rc(s0 + 1 + k, out_ref.at[k], in_ref.at[k], sib).wait_recv()

    def total(i, carry):
        rr = pl.ds(pl.multiple_of(i * RS_CH, RS_CH), RS_CH)
        acc = mine[j, rr, :].astype(F32) + sib_ref[j, rr, :].astype(F32)
        for k in range(3):
            acc = acc + in_ref[k, rr, :].astype(F32)
        res_ref[c, rr, :] = acc
        return carry

    lax.fori_loop(0, rows // RS_CH, total, 0)
    cp = rc(s0 + 4, res_ref.at[c], res_ref.at[c], sib)
    cp.start()
    return cp


def _rs_done(rc, s0, res_ref, c, sib):
    rc(s0 + 4, res_ref.at[1 - c], res_ref.at[1 - c], sib).wait_recv()


def _rs_scratch(rows):
    return [pltpu.VMEM((N_CHIPS, rows, D_MODEL), BF16), pltpu.VMEM((3, rows, D_MODEL), BF16),
            pltpu.VMEM((3, rows, D_MODEL), BF16)]


def _gather_weights(wt, c_row, w_ada, b_sh):
    n_sem = 16

    def body(wt_ref, c_ref, wada_ref, bsh_ref, w4_ref, call_ref, ada_ref, part_ref, ssem, rsem):
        x, y, c, chips = _place()
        j = 2 * x + y
        dev = 2 * j + c
        sib = (x, y, 1 - c)
        idx = [2 * cx + cy for cx, cy in chips]
        rc = functools.partial(_remote, ssem, rsem)

        w4_ref[j] = wt_ref[...].astype(BF16)
        call_ref[dev] = c_ref[...]

        sends = []
        peers = [(px, py, pc) for px in (x, 1 - x) for py in (y, 1 - y) for pc in (c, 1 - c)][1:]
        for k, peer in enumerate(peers):
            sends.append(rc(k, call_ref.at[dev], call_ref.at[dev], peer))
        for k, chip in enumerate(chips):
            sends.append(rc(7 + k, w4_ref.at[j, c], w4_ref.at[j, c], (*chip, c)))
        for cp in sends:
            cp.start()

        for k, (px, py, pc) in enumerate(peers):
            pdev = 4 * px + 2 * py + pc
            rc(k, call_ref.at[pdev], call_ref.at[pdev], (px, py, pc)).wait_recv()
        rowid = lax.broadcasted_iota(jnp.int32, (N_DEV, D_MODEL), 0)
        call = jnp.zeros((N_DEV, D_MODEL), F32)
        for r in range(N_DEV):
            call = jnp.where(rowid == r, jnp.broadcast_to(call_ref[r], (N_DEV, D_MODEL)), call)
        part = jnp.dot(_silu(call).astype(BF16), wada_ref[...].astype(BF16), preferred_element_type=F32) + bsh_ref[...]
        for r in range(N_DEV):
            part_ref[r] = part[r:r + 1, :]
        ada_ref[j] = part_ref[dev]
        rows_out = []
        for k, chip in enumerate(chips):
            rows_out.append(rc(13 + k, part_ref.at[2 * idx[k] + c], ada_ref.at[j], (*chip, c)))
            rows_out[-1].start()

        passed = []
        for k, chip in enumerate(chips):
            jk = idx[k]
            rc(7 + k, w4_ref.at[jk, c], w4_ref.at[jk, c], sib).wait_recv()
            passed.append(rc(10 + k, w4_ref.at[jk, c], w4_ref.at[jk, c], sib))
            passed[-1].start()
        for k, chip in enumerate(chips):
            jk = idx[k]
            rc(10 + k, w4_ref.at[jk, 1 - c], w4_ref.at[jk, 1 - c], sib).wait_recv()
            rc(13 + k, ada_ref.at[jk], ada_ref.at[jk], sib).wait_recv()
        for cp in sends + rows_out + passed:
            cp.wait_send()

    vm = pl.BlockSpec(memory_space=pltpu.VMEM)
    return pl.pallas_call(
        body,
        name="gather_weights",
        in_specs=[vm] * 4,
        out_specs=[vm] * 3,
        out_shape=[jax.ShapeDtypeStruct((N_CHIPS, 2, IN_HALF, D_MODEL), BF16),
                   jax.ShapeDtypeStruct((N_DEV, 1, D_MODEL), F32),
                   jax.ShapeDtypeStruct((N_CHIPS, 1, ADA_SHARD), F32)],
        scratch_shapes=[pltpu.VMEM((N_DEV, 1, ADA_SHARD), F32),
                        pltpu.SemaphoreType.DMA((n_sem,)), pltpu.SemaphoreType.DMA((n_sem,))],
        compiler_params=_cparams(),
    )(wt, c_row, w_ada, b_sh)


def _adamw_math(w, g, m, v):
    m2 = ADAM_B1 * m + (1.0 - ADAM_B1) * g
    v2 = ADAM_B2 * v + (1.0 - ADAM_B2) * (g * g)
    m_hat = m2 / (1.0 - ADAM_B1 ** ADAM_STEP)
    v_hat = v2 / (1.0 - ADAM_B2 ** ADAM_STEP)
    delta = -ADAM_LR * (m_hat / (jnp.sqrt(v_hat) + ADAM_EPS) + ADAM_WD * w)
    return delta, m2, v2


def _adamw(name, w, g, m, v, tm):
    r, cdim = w.shape

    def body(w_ref, g_ref, m_ref, v_ref, d_ref, m2_ref, v2_ref):
        d_ref[...], m2_ref[...], v2_ref[...] = _adamw_math(w_ref[...], g_ref[...], m_ref[...], v_ref[...])

    blk = pl.BlockSpec((tm, cdim), lambda i: (i, 0))
    return pl.pallas_call(
        body,
        name=name,
        grid=(r // tm,),
        in_specs=[blk] * 4,
        out_specs=[blk] * 3,
        out_shape=[jax.ShapeDtypeStruct((r, cdim), F32)] * 3,
        compiler_params=_cparams(dimension_semantics=("arbitrary",)),
    )(w, g, m, v)


def _adamw_ada(w, m, v, cact_t, dcols):
    r, cdim = w.shape
    tm = 256

    def body(w_ref, m_ref, v_ref, ct_ref, dc_ref, g_ref, d_ref, m2_ref, v2_ref):
        g = jnp.dot(ct_ref[...], dc_ref[...], preferred_element_type=F32, precision=lax.Precision.HIGHEST)
        g_ref[...] = g
        d_ref[...], m2_ref[...], v2_ref[...] = _adamw_math(w_ref[...], g, m_ref[...], v_ref[...])

    blk = pl.BlockSpec((tm, cdim), lambda i: (i, 0))
    return pl.pallas_call(
        body,
        name="adamw_w_ada",
        grid=(r // tm,),
        in_specs=[blk] * 3 + [pl.BlockSpec((tm, N_DEV), lambda i: (i, 0)), pl.BlockSpec((N_DEV, cdim), lambda i: (0, 0))],
        out_specs=[blk] * 4,
        out_shape=[jax.ShapeDtypeStruct((r, cdim), F32)] * 4,
        compiler_params=_cparams(dimension_semantics=("arbitrary",)),
    )(w, m, v, cact_t, dcols)


def _adamw_small(ws, gs, ms, vs):
    n = len(ws)

    def body(*refs):
        w_r, g_r, m_r, v_r = refs[0:n], refs[n:2 * n], refs[2 * n:3 * n], refs[3 * n:4 * n]
        d_r, m2_r, v2_r = refs[4 * n:5 * n], refs[5 * n:6 * n], refs[6 * n:7 * n]
        for i in range(n):
            d_r[i][...], m2_r[i][...], v2_r[i][...] = _adamw_math(w_r[i][...], g_r[i][...], m_r[i][...], v_r[i][...])

    vm = pl.BlockSpec(memory_space=pltpu.VMEM)
    shapes = [jax.ShapeDtypeStruct(w.shape, F32) for w in ws]
    out = pl.pallas_call(
        body,
        name="adamw_small",
        in_specs=[vm] * (4 * n),
        out_specs=[vm] * (3 * n),
        out_shape=shapes * 3,
        compiler_params=_cparams(),
    )(*ws, *gs, *ms, *vs)
    return out[0:n], out[n:2 * n], out[2 * n:3 * n]


def _rope_tables(t):
    inv = ROPE_THETA ** (-jnp.arange(0, HEAD_DIM, 2, dtype=F32) / HEAD_DIM)
    ang = jnp.arange(t, dtype=F32)[:, None] * inv[None, :]
    cos, sin = jnp.cos(ang), jnp.sin(ang)
    return jnp.tile(cos, (1, 4)), jnp.tile(jnp.concatenate([-sin, sin], axis=1), (1, 2))


def _pad_lanes(v, width):
    return jnp.pad(v, ((0, 0), (0, width - v.shape[1])))


def kernel(x, c, w_ada, b_ada, norm_w, w_in, q_norm_w, k_norm_w, sinks, conv_w, conv_b, ln_w, ln_b, w_out, loss_target, m_w_ada, m_b_ada, m_norm_w, m_w_in, m_q_norm_w, m_k_norm_w, m_sinks, m_conv_w, m_conv_b, m_ln_w, m_ln_b, m_w_out, v_w_ada, v_b_ada, v_norm_w, v_w_in, v_q_norm_w, v_k_norm_w, v_sinks, v_conv_w, v_conv_b, v_ln_w, v_ln_b, v_w_out):
    xi, yi = lax.axis_index("x"), lax.axis_index("y")
    j = 2 * xi + yi
    x2, tgt = x[0], loss_target[0]
    t = x2.shape[0]

    wt_s, mt_s, vt_s = w_in[0].T, m_w_in[0].T, v_w_in[0].T
    cw_pad = jnp.pad(conv_w[0], ((0, 1), (0, 0)))
    b_sh = lax.dynamic_slice(b_ada, (0, ADA_SHARD * j), (1, ADA_SHARD))

    w4, call, ada4 = _gather_weights(wt_s.reshape(2, IN_HALF, D_MODEL), c, w_ada[0], b_sh)
    w_full = w4.reshape(IN_W, D_MODEL)
    ada = ada4.reshape(1, 3 * D_MODEL)
    shift, s1, gate = ada[:, :D_MODEL], 1.0 + ada[:, D_MODEL:2 * D_MODEL], ada[:, 2 * D_MODEL:]

    cos_f, sin_s = _rope_tables(t)
    qw2, kw2 = jnp.tile(q_norm_w, (1, 2)), jnp.tile(k_norm_w, (1, 2))

    q_raw, kv_raw, ga, ua, ug, gb, h = _in_proj(x2, s1, shift, norm_w, w_full)
    o, mix_a, wo4, cw4 = _attn_fwd(q_raw, kv_raw, ga, qw2, kw2, sinks, cos_f, sin_s,
                                   w_out[0].reshape(2, OUT_HALF, D_MODEL), cw_pad)
    w_out_full = wo4.reshape(D_MODEL, D_MODEL)
    cw_full = jnp.concatenate([cw4[i] for i in range(N_CHIPS)], axis=1)
    cz, mix_b = _conv_fwd(ua, ug, gb, cw_full, conv_b, ln_w, ln_b)
    dout, dmix_a, dmix_b, gwo_bf, red_o = _out_proj(mix_a, mix_b, x2, tgt, gate, w_out_full)

    dq, dkv, dga, sm_a, gwo = _attn_bwd(q_raw, kv_raw, ga, o, dmix_a, qw2, kw2, sinks, cos_f, sin_s,
                                        gwo_bf.reshape(N_CHIPS, 2, OUT_HALF, D_MODEL))
    dua, dug, dgb, dcw, dvec = _conv_bwd(ua, ug, gb, cz, dmix_b, cw_full, ln_w, ln_b)
    dparts = (dq, dkv, dga, dua, dug, dgb)

    small0 = jnp.concatenate([
        dcw.reshape(16, D_MODEL), jnp.pad(dvec.reshape(4, D_MODEL), ((0, 4), (0, 0))), _pad_lanes(sm_a, D_MODEL), red_o], axis=0)
    grad_x, gw, ssum, tail = _in_proj_bwd(dparts, h, x2, dout, s1, norm_w, w_full, small0)

    loss = (0.5 / D_MODEL) * jnp.sum(ssum[33])
    gt_w_in = gw.reshape(2 * IN_HALF, D_MODEL)
    g_w_out = gwo.reshape(D_MODEL // N_CHIPS, D_MODEL)
    g_conv_w = lax.dynamic_slice(ssum[0:16].reshape(32, CONV_W), (0, 128 * j), (CONV_TAPS, 128))
    g_vec = ssum[16:20].reshape(8, CONV_W)
    g_conv_b, g_ln_w, g_ln_b = g_vec[0:1], g_vec[1:2], g_vec[2:3]
    g_qw, g_kw, g_sinks = ssum[24:25, 0:HEAD_DIM], ssum[25:26, 0:HEAD_DIM], ssum[26:27, 0:8]
    g_norm_w = ssum[42:43]
    g_b_ada = jnp.concatenate([ssum[40:41], ssum[41:42], ssum[32:33]], axis=1)
    d_ada_all = jnp.concatenate([tail[:, 8], tail[:, 9], tail[:, 0]], axis=1)
    dcols = lax.dynamic_slice(d_ada_all, (0, ADA_SHARD * j), (N_DEV, ADA_SHARD))
    cact_t = jax.nn.silu(call.reshape(N_DEV, D_MODEL)).T

    g_w_ada, d_w_ada, nm_w_ada, nv_w_ada = _adamw_ada(w_ada[0], m_w_ada[0], v_w_ada[0], cact_t, dcols)
    dt_w_in, nmt_w_in, nvt_w_in = _adamw("adamw_w_in", wt_s, gt_w_in, mt_s, vt_s, 176)
    g_w_in, d_w_in, nm_w_in, nv_w_in = gt_w_in.T, dt_w_in.T, nmt_w_in.T, nvt_w_in.T
    d_w_out, nm_w_out, nv_w_out = _adamw("adamw_w_out", w_out[0], g_w_out, m_w_out[0], v_w_out[0], 128)
    ws = [b_ada, norm_w, q_norm_w, k_norm_w, sinks, conv_w[0], conv_b, ln_w, ln_b]
    gs = [g_b_ada, g_norm_w, g_qw, g_kw, g_sinks, g_conv_w, g_conv_b, g_ln_w, g_ln_b]
    ms = [m_b_ada, m_norm_w, m_q_norm_w, m_k_norm_w, m_sinks, m_conv_w[0], m_conv_b, m_ln_w, m_ln_b]
    vs = [v_b_ada, v_norm_w, v_q_norm_w, v_k_norm_w, v_sinks, v_conv_w[0], v_conv_b, v_ln_w, v_ln_b]
    ds, nms, nvs = _adamw_small(ws, gs, ms, vs)

    def order(ada_v, in_v, out_v, sm):
        b, nw_, qw_, kw_, sk_, cw_, cb_, lw_, lb_ = sm
        return [ada_v[None], b, nw_, in_v[None], qw_, kw_, sk_, cw_[None], cb_, lw_, lb_, out_v[None]]

    grads = order(g_w_ada, g_w_in, g_w_out, gs)
    deltas = order(d_w_ada, d_w_in, d_w_out, ds)
    new_m = order(nm_w_ada, nm_w_in, nm_w_out, nms)
    new_v = order(nv_w_ada, nv_w_in, nv_w_out, nvs)
    return (loss, grad_x[None], *grads, *deltas, *new_m, *new_v)
```

```python
import functools

import jax
import jax.numpy as jnp
from jax import lax
from jax.experimental import pallas as pl
from jax.experimental.pallas import tpu as pltpu

F32 = jnp.float32
BF16 = jnp.bfloat16

D_MODEL = 1024
ATTN_W = 512
KV_W = 128
CONV_W = 512
IN_W = 2816
HEAD_DIM = 64
CONV_TAPS = 31
QBLK = 128
EPS = 1e-6
ROPE_THETA = 10000.0

ADAM_LR = 0.001
ADAM_B1 = 0.9
ADAM_B2 = 0.999
ADAM_EPS = 1e-08
ADAM_WD = 0.01
ADAM_STEP = 10

N_CHIPS = 4
N_DEV = 8
IN_HALF = IN_W // N_CHIPS // 2
OUT_HALF = D_MODEL // N_CHIPS // 2
ADA_SHARD = 3 * D_MODEL // N_CHIPS

VMEM_LIMIT = 56 * 1024 * 1024
CONV_PAD = 32


def _cparams(**kw):
    return pltpu.CompilerParams(vmem_limit_bytes=VMEM_LIMIT, **kw)


def _sigmoid(v):
    return 1.0 / (1.0 + jnp.exp(-v))


def _silu(v):
    return v * _sigmoid(v)


def _dsilu(v):
    s = _sigmoid(v)
    return s * (1.0 + v * (1.0 - s))


def _lane(shape):
    return lax.broadcasted_iota(jnp.int32, shape, len(shape) - 1)


def _in_proj(x, s1, shift, nw, wt_full):
    t = x.shape[0]
    tm = 256

    def body(x_ref, s1_ref, sh_ref, nw_ref, w_ref, q_ref, kv_ref, ga_ref, ua_ref, ug_ref, gb_ref, h_ref):
        xv = x_ref[...]
        r = lax.rsqrt(jnp.mean(xv * xv, axis=-1, keepdims=True) + EPS)
        h = ((xv * r) * nw_ref[...] * s1_ref[...] + sh_ref[...]).astype(BF16)
        h_ref[...] = h
        p = lax.dot_general(h, w_ref[...], (((1,), (1,)), ((), ())), preferred_element_type=F32)
        q_ref[...] = p[:, 0:512]
        kv_ref[...] = p[:, 512:768]
        ga_ref[...] = p[:, 768:1280]
        ua_ref[...] = p[:, 1280:1792]
        ug_ref[...] = p[:, 1792:2304]
        gb_ref[...] = p[:, 2304:2816]

    row = lambda w: pl.BlockSpec((tm, w), lambda i: (i, 0))
    vec = pl.BlockSpec((1, D_MODEL), lambda i: (0, 0))
    return pl.pallas_call(
        body,
        name="in_proj",
        grid=(t // tm,),
        in_specs=[row(D_MODEL), vec, vec, vec,
                  pl.BlockSpec((IN_W, D_MODEL), lambda i: (0, 0), pipeline_mode=pl.Buffered(1))],
        out_specs=[row(512), row(256), row(512), row(512), row(512), row(512), row(D_MODEL)],
        out_shape=[jax.ShapeDtypeStruct((t, w), F32) for w in (512, 256, 512, 512, 512, 512)]
        + [jax.ShapeDtypeStruct((t, D_MODEL), BF16)],
        compiler_params=_cparams(dimension_semantics=("arbitrary",)),
    )(x, s1, shift, nw, wt_full)


def _head_mean(s, left):
    sl = jnp.sum(jnp.where(left, s, 0.0), axis=-1, keepdims=True)
    sr = jnp.sum(jnp.where(left, 0.0, s), axis=-1, keepdims=True)
    return jnp.where(left, sl, sr) * (1.0 / HEAD_DIM)


def _rot(v, first):
    return jnp.where(first, pltpu.roll(v, 96, 1), pltpu.roll(v, 32, 1))


def _norm_rope(v, w, cos, sin_s, left, first):
    r = lax.rsqrt(_head_mean(v * v, left) + EPS)
    xh = v * r
    n = xh * w
    return n * cos + _rot(n, first) * sin_s, xh, r


def _norm_rope_bwd(d, xh, r, w, cos, sin_s, left, first):
    dn = d * cos - _rot(d, first) * sin_s
    dw = jnp.sum(dn * xh, axis=0, keepdims=True)
    dxh = dn * w
    return r * (dxh - xh * _head_mean(dxh * xh, left)), dw


def _dup_heads(v, left):
    sw = pltpu.roll(v, 64, 1)
    return jnp.where(left, v, sw), jnp.where(left, sw, v)


def _prep_kv(kv_ref, kw_ref, cos_ref, sin_ref, ka_ref, va_ref, t):
    ch = 256
    for g in range(2):
        ka_ref[g, 0:QBLK, :] = jnp.zeros((QBLK, 128), BF16)
        va_ref[g, 0:QBLK, :] = jnp.zeros((QBLK, 128), BF16)

    def chunk(i, carry):
        r0 = pl.multiple_of(i * ch, ch)
        left = _lane((ch, 128)) < 64
        first = (_lane((ch, 128)) % 64) < 32
        k = kv_ref[pl.ds(r0, ch), 0:128]
        v = kv_ref[pl.ds(r0, ch), 128:256]
        kr, _, _ = _norm_rope(k, kw_ref[...], cos_ref[pl.ds(r0, ch), :], sin_ref[pl.ds(r0, ch), :], left, first)
        k0, k1 = _dup_heads(kr, left)
        v0, v1 = _dup_heads(v, left)
        ka_ref[0, pl.ds(QBLK + r0, ch), :] = k0.astype(BF16)
        ka_ref[1, pl.ds(QBLK + r0, ch), :] = k1.astype(BF16)
        va_ref[0, pl.ds(QBLK + r0, ch), :] = v0.astype(BF16)
        va_ref[1, pl.ds(QBLK + r0, ch), :] = v1.astype(BF16)
        return carry

    lax.fori_loop(0, t // ch, chunk, 0)


def _band_mask(n):
    qi = lax.broadcasted_iota(jnp.int32, (2 * QBLK, 2 * QBLK), 0) % QBLK
    kj = lax.broadcasted_iota(jnp.int32, (2 * QBLK, 2 * QBLK), 1)
    local = (kj > qi) & (kj <= qi + QBLK)
    return local & ((n > 0) | (kj >= QBLK))


def _softmax_pair(s, mask, sink0, sink1):
    row = lax.broadcasted_iota(jnp.int32, (2 * QBLK, 1), 0)
    sink = jnp.where(row < QBLK, sink0, sink1)
    s = jnp.where(mask, s, -jnp.inf)
    m = jnp.maximum(jnp.max(s, axis=-1, keepdims=True), sink)
    e = jnp.exp(s - m)
    es = jnp.exp(sink - m)
    inv = 1.0 / (jnp.sum(e, axis=-1, keepdims=True) + es)
    return e * inv, es * inv


def _stack_heads(v, left):
    return jnp.concatenate([jnp.where(left, v, 0.0), jnp.where(left, 0.0, v)], axis=0)


def _attn_fwd(q_raw, kv_raw, ga, qw2, kw2, sinks, cos_f, sin_s, wo, cw):
    t = q_raw.shape[0]
    nblk = t // QBLK

    def body(q_ref, kv_ref, ga_ref, qw_ref, kw_ref, sk_ref, cos_ref, sin_ref, wo_ref, cw_ref,
             o_ref, mix_ref, wo4_ref, cw4_ref, ka_ref, va_ref, ssem, rsem):
        x, y, c, chips = _place()
        j = 2 * x + y
        sib = (x, y, 1 - c)
        idx = [2 * cx + cy for cx, cy in chips]
        rc = functools.partial(_remote, ssem, rsem)
        wo4_ref[j] = wo_ref[...].astype(BF16)
        cw4_ref[j] = cw_ref[...]
        sends = []
        for k, chip in enumerate(chips):
            sends.append(rc(k, wo4_ref.at[j, c], wo4_ref.at[j, c], (*chip, c)))
            sends.append(rc(6 + k, cw4_ref.at[j], cw4_ref.at[j], (*chip, c)))
        for cp in sends:
            cp.start()

        _prep_kv(kv_ref, kw_ref, cos_ref, sin_ref, ka_ref, va_ref, t)

        def blk(n, carry):
            r0 = pl.multiple_of(n * QBLK, QBLK)
            left = _lane((QBLK, 128)) < 64
            first = (_lane((QBLK, 128)) % 64) < 32
            cos = cos_ref[pl.ds(r0, QBLK), :]
            sin = sin_ref[pl.ds(r0, QBLK), :]
            mask = _band_mask(n)
            for p in range(4):
                g = p // 2
                lanes = slice(p * 128, (p + 1) * 128)
                qr, _, _ = _norm_rope(q_ref[pl.ds(r0, QBLK), lanes], qw_ref[...], cos, sin, left, first)
                q2 = _stack_heads(qr * 0.125, left).astype(BF16)
                s = lax.dot_general(q2, ka_ref[g, pl.ds(r0, 2 * QBLK), :], (((1,), (1,)), ((), ())),
                                    preferred_element_type=F32)
                pm, _ = _softmax_pair(s, mask, sk_ref[0, 2 * p], sk_ref[0, 2 * p + 1])
                o2 = jnp.dot(pm.astype(BF16), va_ref[g, pl.ds(r0, 2 * QBLK), :], preferred_element_type=F32)
                o = jnp.where(left, o2[0:QBLK], o2[QBLK:2 * QBLK])
                o_ref[pl.ds(r0, QBLK), lanes] = o
                mix_ref[pl.ds(r0, QBLK), lanes] = (o * _silu(ga_ref[pl.ds(r0, QBLK), lanes])).astype(BF16)
            return carry

        lax.fori_loop(0, nblk, blk, 0)

        passed = []
        for k, chip in enumerate(chips):
            jk = idx[k]
            rc(k, wo4_ref.at[jk, c], wo4_ref.at[jk, c], sib).wait_recv()
            passed.append(rc(3 + k, wo4_ref.at[jk, c], wo4_ref.at[jk, c], sib))
            passed[-1].start()
        for k, chip in enumerate(chips):
            jk = idx[k]
            rc(3 + k, wo4_ref.at[jk, 1 - c], wo4_ref.at[jk, 1 - c], sib).wait_recv()
            rc(6 + k, cw4_ref.at[jk], cw4_ref.at[jk], sib).wait_recv()
        for cp in sends + passed:
            cp.wait_send()

    vm = pl.BlockSpec(memory_space=pltpu.VMEM)
    n_sem = 9
    return pl.pallas_call(
        body,
        name="attn_fwd",
        in_specs=[vm, vm, vm, vm, vm, pl.BlockSpec(memory_space=pltpu.SMEM), vm, vm, vm, vm],
        out_specs=[vm] * 4,
        out_shape=[jax.ShapeDtypeStruct((t, ATTN_W), F32), jax.ShapeDtypeStruct((t, ATTN_W), BF16),
                   jax.ShapeDtypeStruct((N_CHIPS, 2, OUT_HALF, D_MODEL), BF16),
                   jax.ShapeDtypeStruct((N_CHIPS, 32, 128), F32)],
        scratch_shapes=[pltpu.VMEM((2, t + QBLK, 128), BF16), pltpu.VMEM((2, t + QBLK, 128), BF16),
                        pltpu.SemaphoreType.DMA((n_sem,)), pltpu.SemaphoreType.DMA((n_sem,))],
        compiler_params=_cparams(),
    )(q_raw, kv_raw, ga, qw2, kw2, sinks, cos_f, sin_s, wo, cw)


def _attn_bwd(q_raw, kv_raw, ga, o, dmix, qw2, kw2, sinks, cos_f, sin_s, go):
    t = q_raw.shape[0]
    nblk = t // QBLK

    def body(q_ref, kv_ref, ga_ref, o_ref, dm_ref, qw_ref, kw_ref, sk_ref, cos_ref, sin_ref, go_ref,
             dq_ref, dkv_ref, dga_ref, sm_ref, gwo_ref, ka_ref, va_ref, dka_ref, dva_ref,
             sibo_ref, outo_ref, ino_ref, ssem, rsem):
        x, y, c, chips = _place()
        sib = (x, y, 1 - c)
        rc = functools.partial(_remote, ssem, rsem)
        theirs, mine = go_ref.at[:, 1 - c], go_ref.at[:, c]
        sends = [_rs_to_sibling(rc, 0, theirs, sibo_ref, sib)]
        _prep_kv(kv_ref, kw_ref, cos_ref, sin_ref, ka_ref, va_ref, t)
        dka_ref[...] = jnp.zeros_like(dka_ref)
        dva_ref[...] = jnp.zeros_like(dva_ref)
        sends += _rs_trade(rc, 0, theirs, mine, sibo_ref, outo_ref, ino_ref, OUT_HALF, c, sib, chips)

        def blk(n, carry):
            dqw, dsk = carry
            r0 = pl.multiple_of(n * QBLK, QBLK)
            left = _lane((QBLK, 128)) < 64
            first = (_lane((QBLK, 128)) % 64) < 32
            cos = cos_ref[pl.ds(r0, QBLK), :]
            sin = sin_ref[pl.ds(r0, QBLK), :]
            mask = _band_mask(n)
            row = lax.broadcasted_iota(jnp.int32, (2 * QBLK, 1), 0)
            for p in range(4):
                g = p // 2
                lanes = slice(p * 128, (p + 1) * 128)
                rows = pl.ds(r0, QBLK)
                win = pl.ds(r0, 2 * QBLK)
                qr, xh, r = _norm_rope(q_ref[rows, lanes], qw_ref[...], cos, sin, left, first)
                q2 = _stack_heads(qr * 0.125, left).astype(BF16)
                kwin = ka_ref[g, win, :]
                vwin = va_ref[g, win, :]
                s = lax.dot_general(q2, kwin, (((1,), (1,)), ((), ())), preferred_element_type=F32)
                pm, ps = _softmax_pair(s, mask, sk_ref[0, 2 * p], sk_ref[0, 2 * p + 1])
                gav = ga_ref[rows, lanes]
                dmv = dm_ref[rows, lanes]
                dga_ref[rows, lanes] = (dmv * o_ref[rows, lanes] * _dsilu(gav)).astype(BF16)
                do2 = _stack_heads(dmv * _silu(gav), left).astype(BF16)
                dp = lax.dot_general(do2, vwin, (((1,), (1,)), ((), ())), preferred_element_type=F32)
                delta = jnp.sum(pm * dp, axis=-1, keepdims=True)
                ds = (pm * (dp - delta)).astype(BF16)
                pd = ps * delta
                d0 = jnp.sum(jnp.where(row < QBLK, pd, 0.0), axis=0, keepdims=True)
                d1 = jnp.sum(jnp.where(row < QBLK, 0.0, pd), axis=0, keepdims=True)
                l8 = _lane((1, 128))
                dsk = dsk - jnp.where(l8 == 2 * p, d0, 0.0) - jnp.where(l8 == 2 * p + 1, d1, 0.0)
                dva_ref[g, win, :] += lax.dot_general(pm.astype(BF16), do2, (((0,), (0,)), ((), ())),
                                                      preferred_element_type=F32)
                dka_ref[g, win, :] += lax.dot_general(ds, q2, (((0,), (0,)), ((), ())),
                                                      preferred_element_type=F32)
                dq2 = jnp.dot(ds, kwin, preferred_element_type=F32)
                dqr = jnp.where(left, dq2[0:QBLK], dq2[QBLK:2 * QBLK]) * 0.125
                dq, dw = _norm_rope_bwd(dqr, xh, r, qw_ref[...], cos, sin, left, first)
                dq_ref[rows, lanes] = dq.astype(BF16)
                dqw = dqw + dw
            return dqw, dsk

        zero = jnp.zeros((1, 128), F32)
        dqw, dsk = lax.fori_loop(0, nblk, blk, (zero, zero))

        ch = 256

        def chunk(i, dkw):
            r0 = pl.multiple_of(i * ch, ch)
            left = _lane((ch, 128)) < 64
            first = (_lane((ch, 128)) % 64) < 32
            rows = pl.ds(r0, ch)
            prow = pl.ds(QBLK + r0, ch)

            def fold(ref):
                a0 = ref[0, prow, :]
                a1 = ref[1, prow, :]
                return jnp.where(left, a0 + pltpu.roll(a0, 64, 1), a1 + pltpu.roll(a1, 64, 1))

            cos = cos_ref[rows, :]
            sin = sin_ref[rows, :]
            _, xh, r = _norm_rope(kv_ref[rows, 0:128], kw_ref[...], cos, sin, left, first)
            dk, dw = _norm_rope_bwd(fold(dka_ref), xh, r, kw_ref[...], cos, sin, left, first)
            dkv_ref[rows, 0:128] = dk.astype(BF16)
            dkv_ref[rows, 128:256] = fold(dva_ref).astype(BF16)
            return dkw + dw

        dkw = lax.fori_loop(0, t // ch, chunk, zero)
        sm_ref[...] = jnp.zeros((8, 128), F32)
        sm_ref[0:1, :] = dqw + pltpu.roll(dqw, 64, 1)
        sm_ref[1:2, :] = dkw + pltpu.roll(dkw, 64, 1)
        sm_ref[2:3, :] = dsk

        j = 2 * x + y
        sends.append(_rs_total(rc, 0, mine, sibo_ref, outo_ref, ino_ref, gwo_ref, OUT_HALF, j, c, sib))
        _rs_done(rc, 0, gwo_ref, c, sib)
        for cp in sends:
            cp.wait_send()

    vm = pl.BlockSpec(memory_space=pltpu.VMEM)
    return pl.pallas_call(
        body,
        name="attn_bwd",
        in_specs=[vm, vm, vm, vm, vm, vm, vm, pl.BlockSpec(memory_space=pltpu.SMEM), vm, vm, vm],
        out_specs=[vm] * 5,
        out_shape=[jax.ShapeDtypeStruct((t, ATTN_W), BF16), jax.ShapeDtypeStruct((t, 2 * KV_W), BF16),
                   jax.ShapeDtypeStruct((t, ATTN_W), BF16), jax.ShapeDtypeStruct((8, 128), F32),
                   jax.ShapeDtypeStruct((2, OUT_HALF, D_MODEL), F32)],
        scratch_shapes=[pltpu.VMEM((2, t + QBLK, 128), BF16), pltpu.VMEM((2, t + QBLK, 128), BF16),
                        pltpu.VMEM((2, t + QBLK, 128), F32), pltpu.VMEM((2, t + QBLK, 128), F32)]
        + _rs_scratch(OUT_HALF) + [pltpu.SemaphoreType.DMA((RS_SEMS,)), pltpu.SemaphoreType.DMA((RS_SEMS,))],
        compiler_params=_cparams(),
    )(q_raw, kv_raw, ga, o, dmix, qw2, kw2, sinks, cos_f, sin_s, go)


CONV_CH = 256
CONV_SUB = 64
CONV_ACCS = 3


def _shifted_windows(src_ref, r0, sh_ref):
    rows = CONV_CH + CONV_PAD
    win = src_ref[pl.ds(r0, rows), :]
    for b in range(8):
        sh = win if b == 0 else pltpu.roll(win, rows - b, 0)
        for c in range(CONV_W // 128):
            sh_ref[b, c] = sh[:, c * 128:(c + 1) * 128]


def _conv_fwd(ua, ug, gb, cw, cb, lw, lb):
    t = ua.shape[0]

    def body(ua_ref, ug_ref, gb_ref, cw_ref, cb_ref, lw_ref, lb_ref, cz_ref, mix_ref, zp_ref, sh_ref):
        zp_ref[0:CONV_PAD, :] = jnp.zeros((CONV_PAD, CONV_W), F32)

        def glu(i, carry):
            r0 = pl.multiple_of(i * CONV_CH, CONV_CH)
            rows = pl.ds(r0, CONV_CH)
            zp_ref[pl.ds(CONV_PAD + r0, CONV_CH), :] = ua_ref[rows, :] * _sigmoid(ug_ref[rows, :])
            return carry

        lax.fori_loop(0, t // CONV_CH, glu, 0)

        def chunk(i, carry):
            r0 = pl.multiple_of(i * CONV_CH, CONV_CH)
            _shifted_windows(zp_ref, r0, sh_ref)
            for c in range(CONV_W // 128):
                lanes = slice(c * 128, (c + 1) * 128)

                def sub(k, carry2):
                    b0 = pl.multiple_of(k * CONV_SUB, CONV_SUB)
                    acc = [jnp.broadcast_to(cb_ref[0:1, lanes], (CONV_SUB, 128))] + [None] * (CONV_ACCS - 1)
                    for j in range(CONV_TAPS):
                        off = j + CONV_PAD - (CONV_TAPS - 1)
                        term = sh_ref[off % 8, c, pl.ds(b0 + 8 * (off // 8), CONV_SUB), :] * cw_ref[j:j + 1, lanes]
                        acc[j % CONV_ACCS] = term if acc[j % CONV_ACCS] is None else acc[j % CONV_ACCS] + term
                    cz_ref[pl.ds(r0 + b0, CONV_SUB), lanes] = functools.reduce(lambda a, b: a + b, acc)
                    return carry2

                lax.fori_loop(0, CONV_CH // CONV_SUB, sub, 0)
            rows = pl.ds(r0, CONV_CH)
            cz = cz_ref[rows, :]
            mu = jnp.mean(cz, axis=-1, keepdims=True)
            xc = cz - mu
            rs = lax.rsqrt(jnp.mean(xc * xc, axis=-1, keepdims=True) + EPS)
            ln = xc * rs * lw_ref[...] + lb_ref[...]
            mix_ref[rows, :] = (_silu(ln) * _silu(gb_ref[rows, :])).astype(BF16)
            return carry

        lax.fori_loop(0, t // CONV_CH, chunk, 0)

    vm = pl.BlockSpec(memory_space=pltpu.VMEM)
    return pl.pallas_call(
        body,
        name="conv_fwd",
        in_specs=[vm] * 7,
        out_specs=[vm, vm],
        out_shape=[jax.ShapeDtypeStruct((t, CONV_W), F32), jax.ShapeDtypeStruct((t, CONV_W), BF16)],
        scratch_shapes=[pltpu.VMEM((t + CONV_PAD, CONV_W), F32),
                        pltpu.VMEM((8, CONV_W // 128, CONV_CH + CONV_PAD, 128), F32)],
        compiler_params=_cparams(),
    )(ua, ug, gb, cw, cb, lw, lb)


def _conv_bwd(ua, ug, gb, cz, dmix, cw, lw, lb):
    t = ua.shape[0]

    def body(ua_ref, ug_ref, gb_ref, cz_ref, dm_ref, cw_ref, lw_ref, lb_ref,
             dua_ref, dug_ref, dgb_ref, dcw_ref, dvec_ref, zp_ref, dp_ref, sh_ref, wacc_ref):
        zp_ref[0:CONV_PAD, :] = jnp.zeros((CONV_PAD, CONV_W), F32)
        dp_ref[t:t + CONV_PAD, :] = jnp.zeros((CONV_PAD, CONV_W), F32)
        wacc_ref[...] = jnp.zeros_like(wacc_ref)

        def pointwise(i, carry):
            dcb, dlw, dlb = carry
            r0 = pl.multiple_of(i * CONV_CH, CONV_CH)
            rows = pl.ds(r0, CONV_CH)
            zp_ref[pl.ds(CONV_PAD + r0, CONV_CH), :] = ua_ref[rows, :] * _sigmoid(ug_ref[rows, :])
            cz = cz_ref[rows, :]
            mu = jnp.mean(cz, axis=-1, keepdims=True)
            xc = cz - mu
            rs = lax.rsqrt(jnp.mean(xc * xc, axis=-1, keepdims=True) + EPS)
            xh = xc * rs
            ln = xh * lw_ref[...] + lb_ref[...]
            gbv = gb_ref[rows, :]
            dy = dm_ref[rows, :]
            dgb_ref[rows, :] = (dy * _silu(ln) * _dsilu(gbv)).astype(BF16)
            dl = dy * _silu(gbv) * _dsilu(ln)
            dxh = dl * lw_ref[...]
            dcz = rs * (dxh - jnp.mean(dxh, axis=-1, keepdims=True)
                        - xh * jnp.mean(dxh * xh, axis=-1, keepdims=True))
            dp_ref[rows, :] = dcz
            return (dcb + jnp.sum(dcz, axis=0, keepdims=True),
                    dlw + jnp.sum(dl * xh, axis=0, keepdims=True),
                    dlb + jnp.sum(dl, axis=0, keepdims=True))

        zero = jnp.zeros((1, CONV_W), F32)
        dcb, dlw, dlb = lax.fori_loop(0, t // CONV_CH, pointwise, (zero, zero, zero))
        dvec_ref[...] = jnp.zeros((8, CONV_W), F32)
        dvec_ref[0:1, :] = dcb
        dvec_ref[1:2, :] = dlw
        dvec_ref[2:3, :] = dlb

        def chunk(i, carry):
            r0 = pl.multiple_of(i * CONV_CH, CONV_CH)
            _shifted_windows(dp_ref, r0, sh_ref)
            for c in range(CONV_W // 128):
                lanes = slice(c * 128, (c + 1) * 128)

                def sub(k, carry2):
                    b0 = pl.multiple_of(k * CONV_SUB, CONV_SUB)
                    acc = [None] * CONV_ACCS
                    for j in range(CONV_TAPS):
                        off = CONV_TAPS - 1 - j
                        term = sh_ref[off % 8, c, pl.ds(b0 + 8 * (off // 8), CONV_SUB), :] * cw_ref[j:j + 1, lanes]
                        acc[j % CONV_ACCS] = term if acc[j % CONV_ACCS] is None else acc[j % CONV_ACCS] + term
                    acc = functools.reduce(lambda a, b: a + b, acc)
                    rr = pl.ds(r0 + b0, CONV_SUB)
                    sg = _sigmoid(ug_ref[rr, lanes])
                    dua_ref[rr, lanes] = (acc * sg).astype(BF16)
                    dug_ref[rr, lanes] = (acc * ua_ref[rr, lanes] * sg * (1.0 - sg)).astype(BF16)
                    return carry2

                lax.fori_loop(0, CONV_CH // CONV_SUB, sub, 0)
            _shifted_windows(zp_ref, r0, sh_ref)
            for c in range(CONV_W // 128):
                lanes = slice(c * 128, (c + 1) * 128)

                def subw(k, carry2):
                    b0 = pl.multiple_of(k * CONV_SUB, CONV_SUB)
                    dcz = dp_ref[pl.ds(r0 + b0, CONV_SUB), lanes]
                    for j in range(CONV_TAPS):
                        off = j + CONV_PAD - (CONV_TAPS - 1)
                        pr = dcz * sh_ref[off % 8, c, pl.ds(b0 + 8 * (off // 8), CONV_SUB), :]
                        parts = [pr[8 * q:8 * (q + 1)] for q in range(CONV_SUB // 8)]
                        while len(parts) > 1:
                            parts = [a + b for a, b in zip(parts[0::2], parts[1::2])]
                        wacc_ref[8 * j:8 * (j + 1), lanes] += parts[0]
                    return carry2

                lax.fori_loop(0, CONV_CH // CONV_SUB, subw, 0)
            return carry

        lax.fori_loop(0, t // CONV_CH, chunk, 0)
        dcw_ref[...] = jnp.zeros((32, CONV_W), F32)
        for j in range(CONV_TAPS):
            dcw_ref[j:j + 1, :] = jnp.sum(wacc_ref[8 * j:8 * (j + 1), :], axis=0, keepdims=True)

    vm = pl.BlockSpec(memory_space=pltpu.VMEM)
    return pl.pallas_call(
        body,
        name="conv_bwd",
        in_specs=[vm] * 8,
        out_specs=[vm] * 5,
        out_shape=[jax.ShapeDtypeStruct((t, CONV_W), BF16)] * 3
        + [jax.ShapeDtypeStruct((32, CONV_W), F32), jax.ShapeDtypeStruct((8, CONV_W), F32)],
        scratch_shapes=[pltpu.VMEM((t + CONV_PAD, CONV_W), F32), pltpu.VMEM((t + CONV_PAD, CONV_W), F32),
                        pltpu.VMEM((8, CONV_W // 128, CONV_CH + CONV_PAD, 128), F32), pltpu.VMEM((8 * 32, CONV_W), F32)],
        compiler_params=_cparams(),
    )(ua, ug, gb, cz, dmix, cw, lw, lb)


def _out_proj(mix_a, mix_b, x, tgt, gate, w_out):
    t = x.shape[0]
    tm = 256
    nstep = t // tm

    def body(ma_ref, mb_ref, x_ref, t_ref, g_ref, w_ref, dout_ref, dma_ref, dmb_ref, gw_ref, red_ref, acc_ref):
        i = pl.program_id(0)

        @pl.when(i == 0)
        def _():
            acc_ref[...] = jnp.zeros_like(acc_ref)
            red_ref[...] = jnp.zeros_like(red_ref)

        mix = jnp.concatenate([ma_ref[...], mb_ref[...]], axis=1)
        y = jnp.dot(mix, w_ref[...], preferred_element_type=F32)
        gate_v = g_ref[...]
        err = x_ref[...] + gate_v * y - t_ref[...]
        dout = err * (1.0 / D_MODEL)
        dout_ref[...] = dout
        red_ref[0:1, :] += jnp.sum(dout * y, axis=0, keepdims=True)
        red_ref[1:2, :] += jnp.sum(err * err, axis=0, keepdims=True)
        dy = (dout * gate_v).astype(BF16)
        dmix = lax.dot_general(dy, w_ref[...], (((1,), (1,)), ((), ())), preferred_element_type=F32)
        dma_ref[...] = dmix[:, 0:512]
        dmb_ref[...] = dmix[:, 512:1024]
        acc_ref[...] += lax.dot_general(mix, dy, (((0,), (0,)), ((), ())), preferred_element_type=F32)

        @pl.when(i == nstep - 1)
        def _():
            gw_ref[...] = acc_ref[...].astype(BF16)

    row = lambda w: pl.BlockSpec((tm, w), lambda i: (i, 0))
    const = lambda s: pl.BlockSpec(s, lambda i: (0, 0))
    return pl.pallas_call(
        body,
        name="out_proj",
        grid=(nstep,),
        in_specs=[row(512), row(512), row(D_MODEL), row(D_MODEL), const((1, D_MODEL)),
                  pl.BlockSpec((D_MODEL, D_MODEL), lambda i: (0, 0), pipeline_mode=pl.Buffered(1))],
        out_specs=[row(D_MODEL), row(512), row(512), const((D_MODEL, D_MODEL)), const((8, D_MODEL))],
        out_shape=[jax.ShapeDtypeStruct((t, D_MODEL), F32), jax.ShapeDtypeStruct((t, 512), F32),
                   jax.ShapeDtypeStruct((t, 512), F32), jax.ShapeDtypeStruct((D_MODEL, D_MODEL), BF16),
                   jax.ShapeDtypeStruct((8, D_MODEL), F32)],
        scratch_shapes=[pltpu.VMEM((D_MODEL, D_MODEL), F32)],
        compiler_params=_cparams(dimension_semantics=("arbitrary",)),
    )(mix_a, mix_b, x, tgt, gate, w_out)


DPROJ_WIDTHS = (512, 256, 512, 512, 512, 512)
DPROJ_STARTS = (0, 512, 768, 1280, 1792, 2304)
WIN_W = 768
WIN_START = (0, 640, 1408, 2048)
WIN_OFF = (0, 64, 0, 64)
N_GW = N_CHIPS


def _window_pieces(s):
    lo, hi = WIN_START[s], WIN_START[s] + WIN_W
    out = []
    for p, (st, w) in enumerate(zip(DPROJ_STARTS, DPROJ_WIDTHS)):
        a, b = max(lo, st), min(hi, st + w)
        if a < b:
            out.append((p, a - st, b - a, a - lo))
    return out


def _in_proj_bwd(dparts, h, x, dout, s1, nw, wt_full, small0):
    t = x.shape[0]
    tm = 256
    nstep = N_GW + t // tm
    n_sem = 15
    rows0 = small0.shape[0]
    npart = len(DPROJ_WIDTHS)

    def body(*refs):
        d_hbm, d_ref = refs[:npart], refs[npart:2 * npart]
        (x_ref, dout_ref, s1_ref, nw_ref, h_ref, wt_hbm, sm0_ref,
         gx_ref, gw_hbm, ssum_ref, tail_ref,
         stg_ref, wt_ref, gt_ref, sib_ref, out_ref, in_ref, res_ref, sall_ref, red_ref,
         wsem, lsem, ssem, rsem) = refs[2 * npart:]
        i = pl.program_id(0)
        x_, y_, c, chips = _place()
        j = 2 * x_ + y_
        dev = 2 * j + c
        sib = (x_, y_, 1 - c)
        rc = functools.partial(_remote, ssem, rsem)
        rel_chip = [2 * cx + cy for cx, cy in chips] + [j]
        peers = [(px, py, pc) for px in (x_, 1 - x_) for py in (y_, 1 - y_) for pc in (c, 1 - c)][1:]
        wt_copy = pltpu.make_async_copy(wt_hbm, wt_ref, lsem.at[0])

        def window(case, slot):
            return [pltpu.make_async_copy(d_hbm[p].at[:, pl.ds(c0, w)], stg_ref.at[slot, :, pl.ds(w0, w)], wsem.at[slot, n])
                    for n, (p, c0, w, w0) in enumerate(_window_pieces(case))]

        def to_sibling(k):
            return rc(k, gt_ref.at[k, 1 - c], sib_ref.at[k], sib)

        def to_chip(k):
            return rc(4 + k, out_ref.at[k], in_ref.at[k], (*chips[k], c))

        def trade(k):
            to_sibling(k).wait_recv()

            def add(n, carry):
                rr = pl.ds(pl.multiple_of(n * RS_CH, RS_CH), RS_CH)
                out_ref[k, rr, :] = (gt_ref[k, c, rr, :].astype(F32) + sib_ref[k, rr, :].astype(F32)).astype(BF16)
                return carry

            lax.fori_loop(0, IN_HALF // RS_CH, add, 0)
            to_chip(k).start()

        for k in range(N_GW):
            @pl.when(i == k)
            def _(k=k):
                slot = k % 2
                if k == 0:
                    red_ref[...] = jnp.zeros_like(red_ref)
                    wt_copy.start()
                for case in range(N_CHIPS):
                    if k == 0:
                        @pl.when(rel_chip[0] == case)
                        def _():
                            for cp in window(case, 0):
                                cp.start()
                    if k + 1 < N_GW:
                        @pl.when(rel_chip[k + 1] == case)
                        def _():
                            for cp in window(case, 1 - slot):
                                cp.start()
                for case in range(N_CHIPS):
                    @pl.when(rel_chip[k] == case)
                    def _():
                        for cp in window(case, slot):
                            cp.wait()
                g = lax.dot_general(stg_ref[slot], h_ref[...], (((0,), (0,)), ((), ())), preferred_element_type=F32)
                for off in sorted(set(WIN_OFF)):
                    @pl.when(rel_chip[k] % 2 == (1 if off else 0))
                    def _():
                        gt_ref[k, 0] = g[off:off + IN_HALF].astype(BF16)
                        gt_ref[k, 1] = g[off + IN_HALF:off + 2 * IN_HALF].astype(BF16)
                to_sibling(k).start()
                if k >= 1:
                    trade(k - 1)

        @pl.when(i == N_GW)
        def _():
            wt_copy.wait()

        @pl.when(i >= N_GW)
        def _():
            xv = x_ref[...]
            r = lax.rsqrt(jnp.mean(xv * xv, axis=-1, keepdims=True) + EPS)
            xh = xv * r
            n = xh * nw_ref[...]
            dproj = jnp.concatenate([ref[...] for ref in d_ref], axis=1)
            dh = jnp.dot(dproj, wt_ref[...], preferred_element_type=F32)
            red_ref[0:1, :] += jnp.sum(dh, axis=0, keepdims=True)
            red_ref[1:2, :] += jnp.sum(dh * n, axis=0, keepdims=True)
            dn = dh * s1_ref[...]
            red_ref[2:3, :] += jnp.sum(dn * xh, axis=0, keepdims=True)
            dxh = dn * nw_ref[...]
            gx_ref[...] = dout_ref[...] + r * (dxh - xh * jnp.mean(dxh * xh, axis=-1, keepdims=True))

        @pl.when(i == nstep - 1)
        def _():
            sall_ref[dev, 0:rows0, :] = sm0_ref[...]
            sall_ref[dev, rows0:rows0 + 8, :] = red_ref[...]
            sends = [rc(8 + k, sall_ref.at[dev], sall_ref.at[dev], peer) for k, peer in enumerate(peers)]
            for cp in sends:
                cp.start()
            sends += [to_sibling(k) for k in range(N_GW)] + [to_chip(k) for k in range(3)]
            own = N_GW - 1
            to_sibling(own).wait_recv()
            for k in range(3):
                to_chip(k).wait_recv()

            def total(n, carry):
                rr = pl.ds(pl.multiple_of(n * RS_CH, RS_CH), RS_CH)
                acc = gt_ref[own, c, rr, :].astype(F32) + sib_ref[own, rr, :].astype(F32)
                for k in range(3):
                    acc = acc + in_ref[k, rr, :].astype(F32)
                res_ref[c, rr, :] = acc
                return carry

            lax.fori_loop(0, IN_HALF // RS_CH, total, 0)
            share = rc(7, res_ref.at[c], res_ref.at[c], sib)
            share.start()
            sends.append(share)
            for k, (px, py, pc) in enumerate(peers):
                pdev = 4 * px + 2 * py + pc
                rc(8 + k, sall_ref.at[pdev], sall_ref.at[pdev], (px, py, pc)).wait_recv()
            tot = sall_ref[0]
            for d in range(1, N_DEV):
                tot = tot + sall_ref[d]
            ssum_ref[...] = tot
            tail_ref[...] = sall_ref[:, rows0 - 8:rows0 + 8, :]
            rc(7, res_ref.at[1 - c], res_ref.at[1 - c], sib).wait_recv()
            back = pltpu.make_async_copy(res_ref, gw_hbm, lsem.at[1])
            back.start()
            for cp in sends:
                cp.wait_send()
            back.wait()

    blk = lambda i: jnp.maximum(i - N_GW, 0)
    row = lambda w: pl.BlockSpec((tm, w), lambda i: (blk(i), 0))
    vec = pl.BlockSpec((1, D_MODEL), lambda i: (0, 0))
    const = lambda shape: pl.BlockSpec(shape, lambda i: (0,) * len(shape))
    hbm = pl.BlockSpec(memory_space=pl.ANY)
    return pl.pallas_call(
        body,
        name="in_proj_bwd",
        grid=(nstep,),
        in_specs=[hbm] * npart + [row(w) for w in DPROJ_WIDTHS] + [row(D_MODEL), row(D_MODEL), vec, vec,
                  pl.BlockSpec((t, D_MODEL), lambda i: (0, 0), pipeline_mode=pl.Buffered(1)), hbm, const((rows0, D_MODEL))],
        out_specs=[row(D_MODEL), hbm, const((rows0 + 8, D_MODEL)), const((N_DEV, 16, D_MODEL))],
        out_shape=[jax.ShapeDtypeStruct((t, D_MODEL), F32), jax.ShapeDtypeStruct((2, IN_HALF, D_MODEL), F32),
                   jax.ShapeDtypeStruct((rows0 + 8, D_MODEL), F32), jax.ShapeDtypeStruct((N_DEV, 16, D_MODEL), F32)],
        scratch_shapes=[pltpu.VMEM((2, t, WIN_W), BF16), pltpu.VMEM((IN_W, D_MODEL), BF16),
                        pltpu.VMEM((N_CHIPS, 2, IN_HALF, D_MODEL), BF16), pltpu.VMEM((N_CHIPS, IN_HALF, D_MODEL), BF16),
                        pltpu.VMEM((3, IN_HALF, D_MODEL), BF16), pltpu.VMEM((3, IN_HALF, D_MODEL), BF16),
                        pltpu.VMEM((2, IN_HALF, D_MODEL), F32), pltpu.VMEM((N_DEV, rows0 + 8, D_MODEL), F32),
                        pltpu.VMEM((8, D_MODEL), F32), pltpu.SemaphoreType.DMA((2, 3)), pltpu.SemaphoreType.DMA((2,)),
                        pltpu.SemaphoreType.DMA((n_sem,)), pltpu.SemaphoreType.DMA((n_sem,))],
        compiler_params=_cparams(dimension_semantics=("arbitrary",)),
    )(*dparts, *dparts, x, dout, s1, nw, h, wt_full, small0)


MESH = pl.DeviceIdType.MESH


def _place():
    x, y, c = lax.axis_index("x"), lax.axis_index("y"), lax.axis_index("c")
    chips = [(1 - x, y), (x, 1 - y), (1 - x, 1 - y)]
    return x, y, c, chips


def _remote(sems_s, sems_r, k, src, dst, to):
    return pltpu.make_async_remote_copy(src_ref=src, dst_ref=dst, send_sem=sems_s.at[k], recv_sem=sems_r.at[k],
                                        device_id=to, device_id_type=MESH)


RS_CH = 32
RS_SEMS = 5


def _rs_to_sibling(rc, s0, theirs, sib_ref, sib):
    cp = rc(s0, theirs, sib_ref, sib)
    cp.start()
    return cp


def _rs_trade(rc, s0, theirs, mine, sib_ref, out_ref, in_ref, rows, c, sib, chips):
    rc(s0, theirs, sib_ref, sib).wait_recv()
    cps = []
    for k, (cx, cy) in enumerate(chips):
        jk = 2 * cx + cy

        def add(i, carry, jk=jk, k=k):
            rr = pl.ds(pl.multiple_of(i * RS_CH, RS_CH), RS_CH)
            out_ref[k, rr, :] = (mine[jk, rr, :].astype(F32) + sib_ref[jk, rr, :].astype(F32)).astype(BF16)
            return carry

        lax.fori_loop(0, rows // RS_CH, add, 0)
        cps.append(rc(s0 + 1 + k, out_ref.at[k], in_ref.at[k], (cx, cy, c)))
        cps[-1].start()
    return cps


def _rs_total(rc, s0, mine, sib_ref, out_ref, in_ref, res_ref, rows, j, c, sib):
    for k in range(3):
        rc(s0 + 1 + k, out_ref.at[k], in_ref.at[k], sib).wait_recv()

    def total(i, carry):
        rr = pl.ds(pl.multiple_of(i * RS_CH, RS_CH), RS_CH)
        acc = mine[j, rr, :].astype(F32) + sib_ref[j, rr, :].astype(F32)
        for k in range(3):
            acc = acc + in_ref[k, rr, :].astype(F32)
        res_ref[c, rr, :] = acc
        return carry

    lax.fori_loop(0, rows // RS_CH, total, 0)
    cp = rc(s0 + 4, res_ref.at[c], res_ref.at[c], sib)
    cp.start()
    return cp


def _rs_done(rc, s0, res_ref, c, sib):
    rc(s0 + 4, res_ref.at[1 - c], res_ref.at[1 - c], sib).wait_recv()


def _rs_scratch(rows):
    return [pltpu.VMEM((N_CHIPS, rows, D_MODEL), BF16), pltpu.VMEM((3, rows, D_MODEL), BF16),
            pltpu.VMEM((3, rows, D_MODEL), BF16)]


def _gather_weights(wt, c_row, w_ada, b_sh):
    n_sem = 16

    def body(wt_ref, c_ref, wada_ref, bsh_ref, w4_ref, call_ref, ada_ref, part_ref, ssem, rsem):
        x, y, c, chips = _place()
        j = 2 * x + y
        dev = 2 * j + c
        sib = (x, y, 1 - c)
        idx = [2 * cx + cy for cx, cy in chips]
        rc = functools.partial(_remote, ssem, rsem)

        w4_ref[j] = wt_ref[...].astype(BF16)
        call_ref[dev] = c_ref[...]

        sends = []
        peers = [(px, py, pc) for px in (x, 1 - x) for py in (y, 1 - y) for pc in (c, 1 - c)][1:]
        for k, peer in enumerate(peers):
            sends.append(rc(k, call_ref.at[dev], call_ref.at[dev], peer))
        for k, chip in enumerate(chips):
            sends.append(rc(7 + k, w4_ref.at[j, c], w4_ref.at[j, c], (*chip, c)))
        for cp in sends:
            cp.start()

        for k, (px, py, pc) in enumerate(peers):
            pdev = 4 * px + 2 * py + pc
            rc(k, call_ref.at[pdev], call_ref.at[pdev], (px, py, pc)).wait_recv()
        rowid = lax.broadcasted_iota(jnp.int32, (N_DEV, D_MODEL), 0)
        call = jnp.zeros((N_DEV, D_MODEL), F32)
        for r in range(N_DEV):
            call = jnp.where(rowid == r, jnp.broadcast_to(call_ref[r], (N_DEV, D_MODEL)), call)
        part = jnp.dot(_silu(call).astype(BF16), wada_ref[...].astype(BF16), preferred_element_type=F32) + bsh_ref[...]
        for r in range(N_DEV):
            part_ref[r] = part[r:r + 1, :]
        ada_ref[j] = part_ref[dev]
        rows_out = []
        for k, chip in enumerate(chips):
            rows_out.append(rc(13 + k, part_ref.at[2 * idx[k] + c], ada_ref.at[j], (*chip, c)))
            rows_out[-1].start()

        passed = []
        for k, chip in enumerate(chips):
            jk = idx[k]
            rc(7 + k, w4_ref.at[jk, c], w4_ref.at[jk, c], sib).wait_recv()
            passed.append(rc(10 + k, w4_ref.at[jk, c], w4_ref.at[jk, c], sib))
            passed[-1].start()
        for k, chip in enumerate(chips):
            jk = idx[k]
            rc(10 + k, w4_ref.at[jk, 1 - c], w4_ref.at[jk, 1 - c], sib).wait_recv()
            rc(13 + k, ada_ref.at[jk], ada_ref.at[jk], sib).wait_recv()
        for cp in sends + rows_out + passed:
            cp.wait_send()

    vm = pl.BlockSpec(memory_space=pltpu.VMEM)
    return pl.pallas_call(
        body,
        name="gather_weights",
        in_specs=[vm] * 4,
        out_specs=[vm] * 3,
        out_shape=[jax.ShapeDtypeStruct((N_CHIPS, 2, IN_HALF, D_MODEL), BF16),
                   jax.ShapeDtypeStruct((N_DEV, 1, D_MODEL), F32),
                   jax.ShapeDtypeStruct((N_CHIPS, 1, ADA_SHARD), F32)],
        scratch_shapes=[pltpu.VMEM((N_DEV, 1, ADA_SHARD), F32),
                        pltpu.SemaphoreType.DMA((n_sem,)), pltpu.SemaphoreType.DMA((n_sem,))],
        compiler_params=_cparams(),
    )(wt, c_row, w_ada, b_sh)


def _adamw_math(w, g, m, v):
    m2 = ADAM_B1 * m + (1.0 - ADAM_B1) * g
    v2 = ADAM_B2 * v + (1.0 - ADAM_B2) * (g * g)
    m_hat = m2 / (1.0 - ADAM_B1 ** ADAM_STEP)
    v_hat = v2 / (1.0 - ADAM_B2 ** ADAM_STEP)
    delta = -ADAM_LR * (m_hat / (jnp.sqrt(v_hat) + ADAM_EPS) + ADAM_WD * w)
    return delta, m2, v2


def _adamw(name, w, g, m, v, tm):
    r, cdim = w.shape

    def body(w_ref, g_ref, m_ref, v_ref, d_ref, m2_ref, v2_ref):
        d_ref[...], m2_ref[...], v2_ref[...] = _adamw_math(w_ref[...], g_ref[...], m_ref[...], v_ref[...])

    blk = pl.BlockSpec((tm, cdim), lambda i: (i, 0))
    return pl.pallas_call(
        body,
        name=name,
        grid=(r // tm,),
        in_specs=[blk] * 4,
        out_specs=[blk] * 3,
        out_shape=[jax.ShapeDtypeStruct((r, cdim), F32)] * 3,
        compiler_params=_cparams(dimension_semantics=("arbitrary",)),
    )(w, g, m, v)


def _adamw_ada(w, m, v, cact_t, dcols):
    r, cdim = w.shape
    tm = 256

    def body(w_ref, m_ref, v_ref, ct_ref, dc_ref, g_ref, d_ref, m2_ref, v2_ref):
        g = jnp.dot(ct_ref[...], dc_ref[...], preferred_element_type=F32, precision=lax.Precision.HIGHEST)
        g_ref[...] = g
        d_ref[...], m2_ref[...], v2_ref[...] = _adamw_math(w_ref[...], g, m_ref[...], v_ref[...])

    blk = pl.BlockSpec((tm, cdim), lambda i: (i, 0))
    return pl.pallas_call(
        body,
        name="adamw_w_ada",
        grid=(r // tm,),
        in_specs=[blk] * 3 + [pl.BlockSpec((tm, N_DEV), lambda i: (i, 0)), pl.BlockSpec((N_DEV, cdim), lambda i: (0, 0))],
        out_specs=[blk] * 4,
        out_shape=[jax.ShapeDtypeStruct((r, cdim), F32)] * 4,
        compiler_params=_cparams(dimension_semantics=("arbitrary",)),
    )(w, m, v, cact_t, dcols)


def _adamw_small(ws, gs, ms, vs):
    n = len(ws)

    def body(*refs):
        w_r, g_r, m_r, v_r = refs[0:n], refs[n:2 * n], refs[2 * n:3 * n], refs[3 * n:4 * n]
        d_r, m2_r, v2_r = refs[4 * n:5 * n], refs[5 * n:6 * n], refs[6 * n:7 * n]
        for i in range(n):
            d_r[i][...], m2_r[i][...], v2_r[i][...] = _adamw_math(w_r[i][...], g_r[i][...], m_r[i][...], v_r[i][...])

    vm = pl.BlockSpec(memory_space=pltpu.VMEM)
    shapes = [jax.ShapeDtypeStruct(w.shape, F32) for w in ws]
    out = pl.pallas_call(
        body,
        name="adamw_small",
        in_specs=[vm] * (4 * n),
        out_specs=[vm] * (3 * n),
        out_shape=shapes * 3,
        compiler_params=_cparams(),
    )(*ws, *gs, *ms, *vs)
    return out[0:n], out[n:2 * n], out[2 * n:3 * n]


def _rope_tables(t):
    inv = ROPE_THETA ** (-jnp.arange(0, HEAD_DIM, 2, dtype=F32) / HEAD_DIM)
    ang = jnp.arange(t, dtype=F32)[:, None] * inv[None, :]
    cos, sin = jnp.cos(ang), jnp.sin(ang)
    return jnp.tile(cos, (1, 4)), jnp.tile(jnp.concatenate([-sin, sin], axis=1), (1, 2))


def _pad_lanes(v, width):
    return jnp.pad(v, ((0, 0), (0, width - v.shape[1])))


def kernel(x, c, w_ada, b_ada, norm_w, w_in, q_norm_w, k_norm_w, sinks, conv_w, conv_b, ln_w, ln_b, w_out, loss_target, m_w_ada, m_b_ada, m_norm_w, m_w_in, m_q_norm_w, m_k_norm_w, m_sinks, m_conv_w, m_conv_b, m_ln_w, m_ln_b, m_w_out, v_w_ada, v_b_ada, v_norm_w, v_w_in, v_q_norm_w, v_k_norm_w, v_sinks, v_conv_w, v_conv_b, v_ln_w, v_ln_b, v_w_out):
    xi, yi = lax.axis_index("x"), lax.axis_index("y")
    j = 2 * xi + yi
    x2, tgt = x[0], loss_target[0]
    t = x2.shape[0]

    wt_s, mt_s, vt_s = w_in[0].T, m_w_in[0].T, v_w_in[0].T
    cw_pad = jnp.pad(conv_w[0], ((0, 1), (0, 0)))
    b_sh = lax.dynamic_slice(b_ada, (0, ADA_SHARD * j), (1, ADA_SHARD))

    w4, call, ada4 = _gather_weights(wt_s.reshape(2, IN_HALF, D_MODEL), c, w_ada[0], b_sh)
    w_full = w4.reshape(IN_W, D_MODEL)
    ada = ada4.reshape(1, 3 * D_MODEL)
    shift, s1, gate = ada[:, :D_MODEL], 1.0 + ada[:, D_MODEL:2 * D_MODEL], ada[:, 2 * D_MODEL:]

    cos_f, sin_s = _rope_tables(t)
    qw2, kw2 = jnp.tile(q_norm_w, (1, 2)), jnp.tile(k_norm_w, (1, 2))

    q_raw, kv_raw, ga, ua, ug, gb, h = _in_proj(x2, s1, shift, norm_w, w_full)
    o, mix_a, wo4, cw4 = _attn_fwd(q_raw, kv_raw, ga, qw2, kw2, sinks, cos_f, sin_s,
                                   w_out[0].reshape(2, OUT_HALF, D_MODEL), cw_pad)
    w_out_full = wo4.reshape(D_MODEL, D_MODEL)
    cw_full = jnp.concatenate([cw4[i] for i in range(N_CHIPS)], axis=1)
    cz, mix_b = _conv_fwd(ua, ug, gb, cw_full, conv_b, ln_w, ln_b)
    dout, dmix_a, dmix_b, gwo_bf, red_o = _out_proj(mix_a, mix_b, x2, tgt, gate, w_out_full)

    dq, dkv, dga, sm_a, gwo = _attn_bwd(q_raw, kv_raw, ga, o, dmix_a, qw2, kw2, sinks, cos_f, sin_s,
                                        gwo_bf.reshape(N_CHIPS, 2, OUT_HALF, D_MODEL))
    dua, dug, dgb, dcw, dvec = _conv_bwd(ua, ug, gb, cz, dmix_b, cw_full, ln_w, ln_b)
    dparts = (dq, dkv, dga, dua, dug, dgb)

    small0 = jnp.concatenate([
        dcw.reshape(16, D_MODEL), jnp.pad(dvec.reshape(4, D_MODEL), ((0, 4), (0, 0))), _pad_lanes(sm_a, D_MODEL), red_o], axis=0)
    grad_x, gw, ssum, tail = _in_proj_bwd(dparts, h, x2, dout, s1, norm_w, w_full, small0)

    loss = (0.5 / D_MODEL) * jnp.sum(ssum[33])
    gt_w_in = gw.reshape(2 * IN_HALF, D_MODEL)
    g_w_out = gwo.reshape(D_MODEL // N_CHIPS, D_MODEL)
    g_conv_w = lax.dynamic_slice(ssum[0:16].reshape(32, CONV_W), (0, 128 * j), (CONV_TAPS, 128))
    g_vec = ssum[16:20].reshape(8, CONV_W)
    g_conv_b, g_ln_w, g_ln_b = g_vec[0:1], g_vec[1:2], g_vec[2:3]
    g_qw, g_kw, g_sinks = ssum[24:25, 0:HEAD_DIM], ssum[25:26, 0:HEAD_DIM], ssum[26:27, 0:8]
    g_norm_w = ssum[42:43]
    g_b_ada = jnp.concatenate([ssum[40:41], ssum[41:42], ssum[32:33]], axis=1)
    d_ada_all = jnp.concatenate([tail[:, 8], tail[:, 9], tail[:, 0]], axis=1)
    dcols = lax.dynamic_slice(d_ada_all, (0, ADA_SHARD * j), (N_DEV, ADA_SHARD))
    cact_t = jax.nn.silu(call.reshape(N_DEV, D_MODEL)).T

    g_w_ada, d_w_ada, nm_w_ada, nv_w_ada = _adamw_ada(w_ada[0], m_w_ada[0], v_w_ada[0], cact_t, dcols)
    dt_w_in, nmt_w_in, nvt_w_in = _adamw("adamw_w_in", wt_s, gt_w_in, mt_s, vt_s, 176)
    g_w_in, d_w_in, nm_w_in, nv_w_in = gt_w_in.T, dt_w_in.T, nmt_w_in.T, nvt_w_in.T
    d_w_out, nm_w_out, nv_w_out = _adamw("adamw_w_out", w_out[0], g_w_out, m_w_out[0], v_w_out[0], 128)
    ws = [b_ada, norm_w, q_norm_w, k_norm_w, sinks, conv_w[0], conv_b, ln_w, ln_b]
    gs = [g_b_ada, g_norm_w, g_qw, g_kw, g_sinks, g_conv_w, g_conv_b, g_ln_w, g_ln_b]
    ms = [m_b_ada, m_norm_w, m_q_norm_w, m_k_norm_w, m_sinks, m_conv_w[0], m_conv_b, m_ln_w, m_ln_b]
    vs = [v_b_ada, v_norm_w, v_q_norm_w, v_k_norm_w, v_sinks, v_conv_w[0], v_conv_b, v_ln_w, v_ln_b]
    ds, nms, nvs = _adamw_small(ws, gs, ms, vs)

    def order(ada_v, in_v, out_v, sm):
        b, nw_, qw_, kw_, sk_, cw_, cb_, lw_, lb_ = sm
        return [ada_v[None], b, nw_, in_v[None], qw_, kw_, sk_, cw_[None], cb_, lw_, lb_, out_v[None]]

    grads = order(g_w_ada, g_w_in, g_w_out, gs)
    deltas = order(d_w_ada, d_w_in, d_w_out, ds)
    new_m = order(nm_w_ada, nm_w_in, nm_w_out, nms)
    new_v = order(nv_w_ada, nv_w_in, nv_w_out, nvs)
    return (loss, grad_x[None], *grads, *deltas, *new_m, *new_v)
```

```python
import functools

import jax
import jax.numpy as jnp
from jax import lax
from jax.experimental import pallas as pl
from jax.experimental.pallas import tpu as pltpu

F32 = jnp.float32
BF16 = jnp.bfloat16

D_MODEL = 1024
ATTN_W = 512
KV_W = 128
CONV_W = 512
IN_W = 2816
HEAD_DIM = 64
CONV_TAPS = 31
QBLK = 128
EPS = 1e-6
ROPE_THETA = 10000.0

ADAM_LR = 0.001
ADAM_B1 = 0.9
ADAM_B2 = 0.999
ADAM_EPS = 1e-08
ADAM_WD = 0.01
ADAM_STEP = 10

N_CHIPS = 4
N_DEV = 8
IN_HALF = IN_W // N_CHIPS // 2
OUT_HALF = D_MODEL // N_CHIPS // 2
ADA_SHARD = 3 * D_MODEL // N_CHIPS

VMEM_LIMIT = 56 * 1024 * 1024
CONV_PAD = 32


def _cparams(**kw):
    return pltpu.CompilerParams(vmem_limit_bytes=VMEM_LIMIT, **kw)


def _sigmoid(v):
    return 1.0 / (1.0 + jnp.exp(-v))


def _silu(v):
    return v * _sigmoid(v)


def _dsilu(v):
    s = _sigmoid(v)
    return s * (1.0 + v * (1.0 - s))


def _lane(shape):
    return lax.broadcasted_iota(jnp.int32, shape, len(shape) - 1)


PUT_ROWS = 512


def _fetch(hbm_refs, vmem_refs, sem):
    cps = [pltpu.make_async_copy(h, v, sem.at[i]) for i, (h, v) in enumerate(zip(hbm_refs, vmem_refs))]
    for cp in cps:
        cp.start()
    return cps


def _put(vmem_ref, hbm_ref, sem, m):
    r = pl.ds(pl.multiple_of(m * PUT_ROWS, PUT_ROWS), PUT_ROWS)
    return pltpu.make_async_copy(vmem_ref.at[r], hbm_ref.at[r], sem.at[m])


def _put_all(pairs, sems, m):
    for (v, h), sem in zip(pairs, sems):
        _put(v, h, sem, m).start()


def _put_wait(pairs, sems, n):
    for (v, h), sem in zip(pairs, sems):
        for m in range(n):
            _put(v, h, sem, m).wait()


def _in_proj(x, s1, shift, nw, wt_full):
    t = x.shape[0]
    tm = 256

    def body(x_ref, s1_ref, sh_ref, nw_ref, w_ref, q_ref, kv_ref, ga_ref, ua_ref, ug_ref, gb_ref, h_ref):
        xv = x_ref[...]
        r = lax.rsqrt(jnp.mean(xv * xv, axis=-1, keepdims=True) + EPS)
        h = ((xv * r) * nw_ref[...] * s1_ref[...] + sh_ref[...]).astype(BF16)
        h_ref[...] = h
        p = lax.dot_general(h, w_ref[...], (((1,), (1,)), ((), ())), preferred_element_type=F32)
        q_ref[...] = p[:, 0:512]
        kv_ref[...] = p[:, 512:768]
        ga_ref[...] = p[:, 768:1280]
        ua_ref[...] = p[:, 1280:1792]
        ug_ref[...] = p[:, 1792:2304]
        gb_ref[...] = p[:, 2304:2816]

    row = lambda w: pl.BlockSpec((tm, w), lambda i: (i, 0))
    vec = pl.BlockSpec((1, D_MODEL), lambda i: (0, 0))
    return pl.pallas_call(
        body,
        name="in_proj",
        grid=(t // tm,),
        in_specs=[row(D_MODEL), vec, vec, vec,
                  pl.BlockSpec((IN_W, D_MODEL), lambda i: (0, 0), pipeline_mode=pl.Buffered(1))],
        out_specs=[row(512), row(256), row(512), row(512), row(512), row(512), row(D_MODEL)],
        out_shape=[jax.ShapeDtypeStruct((t, w), F32) for w in (512, 256, 512, 512, 512, 512)]
        + [jax.ShapeDtypeStruct((t, D_MODEL), BF16)],
        compiler_params=_cparams(dimension_semantics=("arbitrary",)),
    )(x, s1, shift, nw, wt_full)


def _head_mean(s, left):
    sl = jnp.sum(jnp.where(left, s, 0.0), axis=-1, keepdims=True)
    sr = jnp.sum(jnp.where(left, 0.0, s), axis=-1, keepdims=True)
    return jnp.where(left, sl, sr) * (1.0 / HEAD_DIM)


def _rot(v, first):
    return jnp.where(first, pltpu.roll(v, 96, 1), pltpu.roll(v, 32, 1))


def _norm_rope(v, w, cos, sin_s, left, first):
    r = lax.rsqrt(_head_mean(v * v, left) + EPS)
    xh = v * r
    n = xh * w
    return n * cos + _rot(n, first) * sin_s, xh, r


def _norm_rope_bwd(d, xh, r, w, cos, sin_s, left, first):
    dn = d * cos - _rot(d, first) * sin_s
    dw = jnp.sum(dn * xh, axis=0, keepdims=True)
    dxh = dn * w
    return r * (dxh - xh * _head_mean(dxh * xh, left)), dw


def _dup_heads(v, left):
    sw = pltpu.roll(v, 64, 1)
    return jnp.where(left, v, sw), jnp.where(left, sw, v)


def _prep_kv(kv_ref, kw_ref, cos_ref, sin_ref, ka_ref, va_ref, t):
    ch = 256
    for g in range(2):
        ka_ref[g, 0:QBLK, :] = jnp.zeros((QBLK, 128), BF16)
        va_ref[g, 0:QBLK, :] = jnp.zeros((QBLK, 128), BF16)

    def chunk(i, carry):
        r0 = pl.multiple_of(i * ch, ch)
        left = _lane((ch, 128)) < 64
        first = (_lane((ch, 128)) % 64) < 32
        k = kv_ref[pl.ds(r0, ch), 0:128]
        v = kv_ref[pl.ds(r0, ch), 128:256]
        kr, _, _ = _norm_rope(k, kw_ref[...], cos_ref[pl.ds(r0, ch), :], sin_ref[pl.ds(r0, ch), :], left, first)
        k0, k1 = _dup_heads(kr, left)
        v0, v1 = _dup_heads(v, left)
        ka_ref[0, pl.ds(QBLK + r0, ch), :] = k0.astype(BF16)
        ka_ref[1, pl.ds(QBLK + r0, ch), :] = k1.astype(BF16)
        va_ref[0, pl.ds(QBLK + r0, ch), :] = v0.astype(BF16)
        va_ref[1, pl.ds(QBLK + r0, ch), :] = v1.astype(BF16)
        return carry

    lax.fori_loop(0, t // ch, chunk, 0)


def _band_mask(n):
    qi = lax.broadcasted_iota(jnp.int32, (2 * QBLK, 2 * QBLK), 0) % QBLK
    kj = lax.broadcasted_iota(jnp.int32, (2 * QBLK, 2 * QBLK), 1)
    local = (kj > qi) & (kj <= qi + QBLK)
    return local & ((n > 0) | (kj >= QBLK))


def _softmax_pair(s, mask, sink0, sink1):
    row = lax.broadcasted_iota(jnp.int32, (2 * QBLK, 1), 0)
    sink = jnp.where(row < QBLK, sink0, sink1)
    s = jnp.where(mask, s, -jnp.inf)
    m = jnp.maximum(jnp.max(s, axis=-1, keepdims=True), sink)
    e = jnp.exp(s - m)
    es = jnp.exp(sink - m)
    inv = 1.0 / (jnp.sum(e, axis=-1, keepdims=True) + es)
    return e * inv, es * inv


def _stack_heads(v, left):
    return jnp.concatenate([jnp.where(left, v, 0.0), jnp.where(left, 0.0, v)], axis=0)


def _attn_fwd(q_raw, kv_raw, ga, qw2, kw2, sinks, cos_f, sin_s, wo, cw):
    t = q_raw.shape[0]
    nblk = t // QBLK
    per_put = PUT_ROWS // QBLK

    def body(q_hbm, kv_ref, ga_hbm, qw_ref, kw_ref, sk_ref, cos_ref, sin_ref, wo_ref, cw_ref,
             o_hbm, mix_hbm, wo4_ref, cw4_ref, ka_ref, va_ref, q_ref, ga_ref, o_ref, mix_ref, isem, osem0, osem1,
             ssem, rsem):
        loads = _fetch((q_hbm, ga_hbm), (q_ref, ga_ref), isem)
        outs, osems = ((o_ref, o_hbm), (mix_ref, mix_hbm)), (osem0, osem1)
        x, y, c, chips = _place()
        j = 2 * x + y
        sib = (x, y, 1 - c)
        idx = [2 * cx + cy for cx, cy in chips]
        rc = functools.partial(_remote, ssem, rsem)
        wo4_ref[j] = wo_ref[...].astype(BF16)
        cw4_ref[j] = cw_ref[...]
        sends = []
        for k, chip in enumerate(chips):
            sends.append(rc(k, wo4_ref.at[j, c], wo4_ref.at[j, c], (*chip, c)))
            sends.append(rc(6 + k, cw4_ref.at[j], cw4_ref.at[j], (*chip, c)))
        for cp in sends:
            cp.start()

        _prep_kv(kv_ref, kw_ref, cos_ref, sin_ref, ka_ref, va_ref, t)
        for cp in loads:
            cp.wait()

        def blk(n, carry):
            r0 = pl.multiple_of(n * QBLK, QBLK)
            left = _lane((QBLK, 128)) < 64
            first = (_lane((QBLK, 128)) % 64) < 32
            cos = cos_ref[pl.ds(r0, QBLK), :]
            sin = sin_ref[pl.ds(r0, QBLK), :]
            mask = _band_mask(n)
            for p in range(4):
                g = p // 2
                lanes = slice(p * 128, (p + 1) * 128)
                qr, _, _ = _norm_rope(q_ref[pl.ds(r0, QBLK), lanes], qw_ref[...], cos, sin, left, first)
                q2 = _stack_heads(qr * 0.125, left).astype(BF16)
                s = lax.dot_general(q2, ka_ref[g, pl.ds(r0, 2 * QBLK), :], (((1,), (1,)), ((), ())),
                                    preferred_element_type=F32)
                pm, _ = _softmax_pair(s, mask, sk_ref[0, 2 * p], sk_ref[0, 2 * p + 1])
                o2 = jnp.dot(pm.astype(BF16), va_ref[g, pl.ds(r0, 2 * QBLK), :], preferred_element_type=F32)
                o = jnp.where(left, o2[0:QBLK], o2[QBLK:2 * QBLK])
                o_ref[pl.ds(r0, QBLK), lanes] = o.astype(BF16)
                mix_ref[pl.ds(r0, QBLK), lanes] = (o * _silu(ga_ref[pl.ds(r0, QBLK), lanes])).astype(BF16)

            @pl.when(n % per_put == per_put - 1)
            def _():
                _put_all(outs, osems, n // per_put)

            return carry

        lax.fori_loop(0, nblk, blk, 0)
        _put_wait(outs, osems, t // PUT_ROWS)

        passed = []
        for k, chip in enumerate(chips):
            jk = idx[k]
            rc(k, wo4_ref.at[jk, c], wo4_ref.at[jk, c], sib).wait_recv()
            passed.append(rc(3 + k, wo4_ref.at[jk, c], wo4_ref.at[jk, c], sib))
            passed[-1].start()
        for k, chip in enumerate(chips):
            jk = idx[k]
            rc(3 + k, wo4_ref.at[jk, 1 - c], wo4_ref.at[jk, 1 - c], sib).wait_recv()
            rc(6 + k, cw4_ref.at[jk], cw4_ref.at[jk], sib).wait_recv()
        for cp in sends + passed:
            cp.wait_send()

    vm = pl.BlockSpec(memory_space=pltpu.VMEM)
    hbm = pl.BlockSpec(memory_space=pl.ANY)
    n_sem = 9
    return pl.pallas_call(
        body,
        name="attn_fwd",
        in_specs=[hbm, vm, hbm, vm, vm, pl.BlockSpec(memory_space=pltpu.SMEM), vm, vm, vm, vm],
        out_specs=[hbm, hbm, vm, vm],
        out_shape=[jax.ShapeDtypeStruct((t, ATTN_W), BF16), jax.ShapeDtypeStruct((t, ATTN_W), BF16),
                   jax.ShapeDtypeStruct((N_CHIPS, 2, OUT_HALF, D_MODEL), BF16),
                   jax.ShapeDtypeStruct((N_CHIPS, 32, 128), F32)],
        scratch_shapes=[pltpu.VMEM((2, t + QBLK, 128), BF16), pltpu.VMEM((2, t + QBLK, 128), BF16),
                        pltpu.VMEM((t, ATTN_W), F32), pltpu.VMEM((t, ATTN_W), F32),
                        pltpu.VMEM((t, ATTN_W), BF16), pltpu.VMEM((t, ATTN_W), BF16),
                        pltpu.SemaphoreType.DMA((2,)), pltpu.SemaphoreType.DMA((t // PUT_ROWS,)),
                        pltpu.SemaphoreType.DMA((t // PUT_ROWS,)),
                        pltpu.SemaphoreType.DMA((n_sem,)), pltpu.SemaphoreType.DMA((n_sem,))],
        compiler_params=_cparams(),
    )(q_raw, kv_raw, ga, qw2, kw2, sinks, cos_f, sin_s, wo, cw)


def _attn_bwd(q_raw, kv_raw, ga, o, dmix, qw2, kw2, sinks, cos_f, sin_s, go):
    t = q_raw.shape[0]
    nblk = t // QBLK
    per_put = PUT_ROWS // QBLK

    def body(q_hbm, kv_ref, ga_hbm, o_hbm, dm_hbm, qw_ref, kw_ref, sk_ref, cos_ref, sin_ref, go_ref,
             dq_hbm, dkv_ref, dga_hbm, sm_ref, gwo_ref, ka_ref, va_ref, dka_ref, dva_ref,
             sibo_ref, outo_ref, ino_ref, q_ref, ga_ref, o_ref, dm_ref, dq_ref, dga_ref, isem, osem0, osem1, ssem, rsem):
        loads = _fetch((q_hbm, ga_hbm, o_hbm, dm_hbm), (q_ref, ga_ref, o_ref, dm_ref), isem)
        outs, osems = ((dq_ref, dq_hbm), (dga_ref, dga_hbm)), (osem0, osem1)
        x, y, c, chips = _place()
        sib = (x, y, 1 - c)
        rc = functools.partial(_remote, ssem, rsem)
        theirs, mine = go_ref.at[:, 1 - c], go_ref.at[:, c]
        sends = [_rs_to_sibling(rc, 0, theirs, sibo_ref, sib)]
        _prep_kv(kv_ref, kw_ref, cos_ref, sin_ref, ka_ref, va_ref, t)
        dka_ref[...] = jnp.zeros_like(dka_ref)
        dva_ref[...] = jnp.zeros_like(dva_ref)
        sends += _rs_trade(rc, 0, theirs, mine, sibo_ref, outo_ref, ino_ref, OUT_HALF, c, sib, chips)
        for cp in loads:
            cp.wait()

        def blk(n, carry):
            dqw, dsk = carry
            r0 = pl.multiple_of(n * QBLK, QBLK)
            left = _lane((QBLK, 128)) < 64
            first = (_lane((QBLK, 128)) % 64) < 32
            cos = cos_ref[pl.ds(r0, QBLK), :]
            sin = sin_ref[pl.ds(r0, QBLK), :]
            mask = _band_mask(n)
            row = lax.broadcasted_iota(jnp.int32, (2 * QBLK, 1), 0)
            for p in range(4):
                g = p // 2
                lanes = slice(p * 128, (p + 1) * 128)
                rows = pl.ds(r0, QBLK)
                win = pl.ds(r0, 2 * QBLK)
                qr, xh, r = _norm_rope(q_ref[rows, lanes], qw_ref[...], cos, sin, left, first)
                q2 = _stack_heads(qr * 0.125, left).astype(BF16)
                kwin = ka_ref[g, win, :]
                vwin = va_ref[g, win, :]
                s = lax.dot_general(q2, kwin, (((1,), (1,)), ((), ())), preferred_element_type=F32)
                pm, ps = _softmax_pair(s, mask, sk_ref[0, 2 * p], sk_ref[0, 2 * p + 1])
                gav = ga_ref[rows, lanes]
                dmv = dm_ref[rows, lanes].astype(F32)
                dga_ref[rows, lanes] = (dmv * o_ref[rows, lanes].astype(F32) * _dsilu(gav)).astype(BF16)
                do2 = _stack_heads(dmv * _silu(gav), left).astype(BF16)
                dp = lax.dot_general(do2, vwin, (((1,), (1,)), ((), ())), preferred_element_type=F32)
                delta = jnp.sum(pm * dp, axis=-1, keepdims=True)
                ds = (pm * (dp - delta)).astype(BF16)
                pd = ps * delta
                d0 = jnp.sum(jnp.where(row < QBLK, pd, 0.0), axis=0, keepdims=True)
                d1 = jnp.sum(jnp.where(row < QBLK, 0.0, pd), axis=0, keepdims=True)
                l8 = _lane((1, 128))
                dsk = dsk - jnp.where(l8 == 2 * p, d0, 0.0) - jnp.where(l8 == 2 * p + 1, d1, 0.0)
                dva_ref[g, win, :] += lax.dot_general(pm.astype(BF16), do2, (((0,), (0,)), ((), ())),
                                                      preferred_element_type=F32)
                dka_ref[g, win, :] += lax.dot_general(ds, q2, (((0,), (0,)), ((), ())),
                                                      preferred_element_type=F32)
                dq2 = jnp.dot(ds, kwin, preferred_element_type=F32)
                dqr = jnp.where(left, dq2[0:QBLK], dq2[QBLK:2 * QBLK]) * 0.125
                dq, dw = _norm_rope_bwd(dqr, xh, r, qw_ref[...], cos, sin, left, first)
                dq_ref[rows, lanes] = dq.astype(BF16)
                dqw = dqw + dw

            @pl.when(n % per_put == per_put - 1)
            def _():
                _put_all(outs, osems, n // per_put)

            return dqw, dsk

        zero = jnp.zeros((1, 128), F32)
        dqw, dsk = lax.fori_loop(0, nblk, blk, (zero, zero))

        ch = 256

        def chunk(i, dkw):
            r0 = pl.multiple_of(i * ch, ch)
            left = _lane((ch, 128)) < 64
            first = (_lane((ch, 128)) % 64) < 32
            rows = pl.ds(r0, ch)
            prow = pl.ds(QBLK + r0, ch)

            def fold(ref):
                a0 = ref[0, prow, :]
                a1 = ref[1, prow, :]
                return jnp.where(left, a0 + pltpu.roll(a0, 64, 1), a1 + pltpu.roll(a1, 64, 1))

            cos = cos_ref[rows, :]
            sin = sin_ref[rows, :]
            _, xh, r = _norm_rope(kv_ref[rows, 0:128], kw_ref[...], cos, sin, left, first)
            dk, dw = _norm_rope_bwd(fold(dka_ref), xh, r, kw_ref[...], cos, sin, left, first)
            dkv_ref[rows, 0:128] = dk.astype(BF16)
            dkv_ref[rows, 128:256] = fold(dva_ref).astype(BF16)
            return dkw + dw

        dkw = lax.fori_loop(0, t // ch, chunk, zero)
        sm_ref[...] = jnp.zeros((8, 128), F32)
        sm_ref[0:1, :] = dqw + pltpu.roll(dqw, 64, 1)
        sm_ref[1:2, :] = dkw + pltpu.roll(dkw, 64, 1)
        sm_ref[2:3, :] = dsk

        j = 2 * x + y
        sends.append(_rs_total(rc, 0, mine, sibo_ref, outo_ref, ino_ref, gwo_ref, OUT_HALF, j, c, sib))
        _rs_done(rc, 0, gwo_ref, c, sib)
        for cp in sends:
            cp.wait_send()
        _put_wait(outs, osems, t // PUT_ROWS)

    vm = pl.BlockSpec(memory_space=pltpu.VMEM)
    hbm = pl.BlockSpec(memory_space=pl.ANY)
    return pl.pallas_call(
        body,
        name="attn_bwd",
        in_specs=[hbm, vm, hbm, hbm, hbm, vm, vm, pl.BlockSpec(memory_space=pltpu.SMEM), vm, vm, vm],
        out_specs=[hbm, vm, hbm, vm, vm],
        out_shape=[jax.ShapeDtypeStruct((t, ATTN_W), BF16), jax.ShapeDtypeStruct((t, 2 * KV_W), BF16),
                   jax.ShapeDtypeStruct((t, ATTN_W), BF16), jax.ShapeDtypeStruct((8, 128), F32),
                   jax.ShapeDtypeStruct((2, OUT_HALF, D_MODEL), F32)],
        scratch_shapes=[pltpu.VMEM((2, t + QBLK, 128), BF16), pltpu.VMEM((2, t + QBLK, 128), BF16),
                        pltpu.VMEM((2, t + QBLK, 128), F32), pltpu.VMEM((2, t + QBLK, 128), F32)]
        + _rs_scratch(OUT_HALF)
        + [pltpu.VMEM((t, ATTN_W), F32), pltpu.VMEM((t, ATTN_W), F32), pltpu.VMEM((t, ATTN_W), BF16),
           pltpu.VMEM((t, ATTN_W), BF16), pltpu.VMEM((t, ATTN_W), BF16), pltpu.VMEM((t, ATTN_W), BF16),
           pltpu.SemaphoreType.DMA((4,)), pltpu.SemaphoreType.DMA((t // PUT_ROWS,)), pltpu.SemaphoreType.DMA((t // PUT_ROWS,)),
           pltpu.SemaphoreType.DMA((RS_SEMS,)), pltpu.SemaphoreType.DMA((RS_SEMS,))],
        compiler_params=_cparams(),
    )(q_raw, kv_raw, ga, o, dmix, qw2, kw2, sinks, cos_f, sin_s, go)


CONV_CH = 256
CONV_SUB = 64
CONV_ACCS = 3


def _shifted_windows(src_ref, r0, sh_ref):
    rows = CONV_CH + CONV_PAD
    win = src_ref[pl.ds(r0, rows), :]
    for b in range(8):
        sh = win if b == 0 else pltpu.roll(win, rows - b, 0)
        for c in range(CONV_W // 128):
            sh_ref[b, c] = sh[:, c * 128:(c + 1) * 128]


def _conv_fwd(ua, ug, gb, cw, cb, lw, lb):
    t = ua.shape[0]

    def body(ua_hbm, ug_hbm, gb_hbm, cw_ref, cb_ref, lw_ref, lb_ref, cz_hbm, mix_hbm, zp_ref, sh_ref,
             ua_ref, ug_ref, gb_ref, cz_ref, mix_ref, isem, osem0, osem1):
        loads = _fetch((ua_hbm, ug_hbm, gb_hbm), (ua_ref, ug_ref, gb_ref), isem)
        outs, osems = ((cz_ref, cz_hbm), (mix_ref, mix_hbm)), (osem0, osem1)
        per_put = PUT_ROWS // CONV_CH
        zp_ref[0:CONV_PAD, :] = jnp.zeros((CONV_PAD, CONV_W), F32)
        loads[0].wait()
        loads[1].wait()

        def glu(i, carry):
            r0 = pl.multiple_of(i * CONV_CH, CONV_CH)
            rows = pl.ds(r0, CONV_CH)
            zp_ref[pl.ds(CONV_PAD + r0, CONV_CH), :] = ua_ref[rows, :] * _sigmoid(ug_ref[rows, :])
            return carry

        lax.fori_loop(0, t // CONV_CH, glu, 0)
        loads[2].wait()

        def chunk(i, carry):
            r0 = pl.multiple_of(i * CONV_CH, CONV_CH)
            _shifted_windows(zp_ref, r0, sh_ref)
            for c in range(CONV_W // 128):
                lanes = slice(c * 128, (c + 1) * 128)

                def sub(k, carry2):
                    b0 = pl.multiple_of(k * CONV_SUB, CONV_SUB)
                    acc = [jnp.broadcast_to(cb_ref[0:1, lanes], (CONV_SUB, 128))] + [None] * (CONV_ACCS - 1)
                    for j in range(CONV_TAPS):
                        off = j + CONV_PAD - (CONV_TAPS - 1)
                        term = sh_ref[off % 8, c, pl.ds(b0 + 8 * (off // 8), CONV_SUB), :] * cw_ref[j:j + 1, lanes]
                        acc[j % CONV_ACCS] = term if acc[j % CONV_ACCS] is None else acc[j % CONV_ACCS] + term
                    cz_ref[pl.ds(r0 + b0, CONV_SUB), lanes] = functools.reduce(lambda a, b: a + b, acc)
                    return carry2

                lax.fori_loop(0, CONV_CH // CONV_SUB, sub, 0)
            rows = pl.ds(r0, CONV_CH)
            cz = cz_ref[rows, :]
            mu = jnp.mean(cz, axis=-1, keepdims=True)
            xc = cz - mu
            rs = lax.rsqrt(jnp.mean(xc * xc, axis=-1, keepdims=True) + EPS)
            ln = xc * rs * lw_ref[...] + lb_ref[...]
            mix_ref[rows, :] = (_silu(ln) * _silu(gb_ref[rows, :])).astype(BF16)

            @pl.when(i % per_put == per_put - 1)
            def _():
                _put_all(outs, osems, i // per_put)

            return carry

        lax.fori_loop(0, t // CONV_CH, chunk, 0)
        _put_wait(outs, osems, t // PUT_ROWS)

    vm = pl.BlockSpec(memory_space=pltpu.VMEM)
    hbm = pl.BlockSpec(memory_space=pl.ANY)
    nput = t // PUT_ROWS
    return pl.pallas_call(
        body,
        name="conv_fwd",
        in_specs=[hbm] * 3 + [vm] * 4,
        out_specs=[hbm, hbm],
        out_shape=[jax.ShapeDtypeStruct((t, CONV_W), F32), jax.ShapeDtypeStruct((t, CONV_W), BF16)],
        scratch_shapes=[pltpu.VMEM((t + CONV_PAD, CONV_W), F32),
                        pltpu.VMEM((8, CONV_W // 128, CONV_CH + CONV_PAD, 128), F32),
                        pltpu.VMEM((t, CONV_W), F32), pltpu.VMEM((t, CONV_W), F32), pltpu.VMEM((t, CONV_W), F32),
                        pltpu.VMEM((t, CONV_W), F32), pltpu.VMEM((t, CONV_W), BF16),
                        pltpu.SemaphoreType.DMA((3,)), pltpu.SemaphoreType.DMA((nput,)), pltpu.SemaphoreType.DMA((nput,))],
        compiler_params=_cparams(),
    )(ua, ug, gb, cw, cb, lw, lb)


def _conv_bwd(ua, ug, gb, cz, dmix, cw, lw, lb):
    t = ua.shape[0]

    def body(ua_hbm, ug_hbm, gb_hbm, cz_hbm, dm_hbm, cw_ref, lw_ref, lb_ref,
             dua_hbm, dug_hbm, dgb_hbm, dcw_ref, dvec_ref, zp_ref, dp_ref, sh_ref, wacc_ref,
             ua_ref, ug_ref, gb_ref, cz_ref, dm_ref, dua_ref, dug_ref, dgb_ref, isem, osem0, osem1, osem2):
        loads = _fetch((ua_hbm, ug_hbm, gb_hbm, cz_hbm, dm_hbm), (ua_ref, ug_ref, gb_ref, cz_ref, dm_ref), isem)
        per_put = PUT_ROWS // CONV_CH
        zp_ref[0:CONV_PAD, :] = jnp.zeros((CONV_PAD, CONV_W), F32)
        dp_ref[t:t + CONV_PAD, :] = jnp.zeros((CONV_PAD, CONV_W), F32)
        wacc_ref[...] = jnp.zeros_like(wacc_ref)
        for cp in loads:
            cp.wait()

        def pointwise(i, carry):
            dcb, dlw, dlb = carry
            r0 = pl.multiple_of(i * CONV_CH, CONV_CH)
            rows = pl.ds(r0, CONV_CH)
            zp_ref[pl.ds(CONV_PAD + r0, CONV_CH), :] = ua_ref[rows, :] * _sigmoid(ug_ref[rows, :])
            cz = cz_ref[rows, :]
            mu = jnp.mean(cz, axis=-1, keepdims=True)
            xc = cz - mu
            rs = lax.rsqrt(jnp.mean(xc * xc, axis=-1, keepdims=True) + EPS)
            xh = xc * rs
            ln = xh * lw_ref[...] + lb_ref[...]
            gbv = gb_ref[rows, :]
            dy = dm_ref[rows, :].astype(F32)
            dgb_ref[rows, :] = (dy * _silu(ln) * _dsilu(gbv)).astype(BF16)
            dl = dy * _silu(gbv) * _dsilu(ln)
            dxh = dl * lw_ref[...]
            dcz = rs * (dxh - jnp.mean(dxh, axis=-1, keepdims=True)
                        - xh * jnp.mean(dxh * xh, axis=-1, keepdims=True))
            dp_ref[rows, :] = dcz

            @pl.when(i % per_put == per_put - 1)
            def _():
                _put(dgb_ref, dgb_hbm, osem2, i // per_put).start()

            return (dcb + jnp.sum(dcz, axis=0, keepdims=True),
                    dlw + jnp.sum(dl * xh, axis=0, keepdims=True),
                    dlb + jnp.sum(dl, axis=0, keepdims=True))

        zero = jnp.zeros((1, CONV_W), F32)
        dcb, dlw, dlb = lax.fori_loop(0, t // CONV_CH, pointwise, (zero, zero, zero))
        dvec_ref[...] = jnp.zeros((8, CONV_W), F32)
        dvec_ref[0:1, :] = dcb
        dvec_ref[1:2, :] = dlw
        dvec_ref[2:3, :] = dlb

        def chunk(i, carry):
            r0 = pl.multiple_of(i * CONV_CH, CONV_CH)
            _shifted_windows(dp_ref, r0, sh_ref)
            for c in range(CONV_W // 128):
                lanes = slice(c * 128, (c + 1) * 128)

                def sub(k, carry2):
                    b0 = pl.multiple_of(k * CONV_SUB, CONV_SUB)
                    acc = [None] * CONV_ACCS
                    for j in range(CONV_TAPS):
                        off = CONV_TAPS - 1 - j
                        term = sh_ref[off % 8, c, pl.ds(b0 + 8 * (off // 8), CONV_SUB), :] * cw_ref[j:j + 1, lanes]
                        acc[j % CONV_ACCS] = term if acc[j % CONV_ACCS] is None else acc[j % CONV_ACCS] + term
                    acc = functools.reduce(lambda a, b: a + b, acc)
                    rr = pl.ds(r0 + b0, CONV_SUB)
                    sg = _sigmoid(ug_ref[rr, lanes])
                    dua_ref[rr, lanes] = (acc * sg).astype(BF16)
                    dug_ref[rr, lanes] = (acc * ua_ref[rr, lanes] * sg * (1.0 - sg)).astype(BF16)
                    return carry2

                lax.fori_loop(0, CONV_CH // CONV_SUB, sub, 0)
            _shifted_windows(zp_ref, r0, sh_ref)
            for c in range(CONV_W // 128):
                lanes = slice(c * 128, (c + 1) * 128)

                def subw(k, carry2):
                    b0 = pl.multiple_of(k * CONV_SUB, CONV_SUB)
                    dcz = dp_ref[pl.ds(r0 + b0, CONV_SUB), lanes]
                    for j in range(CONV_TAPS):
                        off = j + CONV_PAD - (CONV_TAPS - 1)
                        pr = dcz * sh_ref[off % 8, c, pl.ds(b0 + 8 * (off // 8), CONV_SUB), :]
                        parts = [pr[8 * q:8 * (q + 1)] for q in range(CONV_SUB // 8)]
                        while len(parts) > 1:
                            parts = [a + b for a, b in zip(parts[0::2], parts[1::2])]
                        wacc_ref[8 * j:8 * (j + 1), lanes] += parts[0]
                    return carry2

                lax.fori_loop(0, CONV_CH // CONV_SUB, subw, 0)

            @pl.when(i % per_put == per_put - 1)
            def _():
                _put_all(((dua_ref, dua_hbm), (dug_ref, dug_hbm)), (osem0, osem1), i // per_put)

            return carry

        lax.fori_loop(0, t // CONV_CH, chunk, 0)
        _put_wait(((dua_ref, dua_hbm), (dug_ref, dug_hbm), (dgb_ref, dgb_hbm)), (osem0, osem1, osem2), t // PUT_ROWS)
        dcw_ref[...] = jnp.zeros((32, CONV_W), F32)
        for j in range(CONV_TAPS):
            dcw_ref[j:j + 1, :] = jnp.sum(wacc_ref[8 * j:8 * (j + 1), :], axis=0, keepdims=True)

    vm = pl.BlockSpec(memory_space=pltpu.VMEM)
    hbm = pl.BlockSpec(memory_space=pl.ANY)
    return pl.pallas_call(
        body,
        name="conv_bwd",
        in_specs=[hbm] * 5 + [vm] * 3,
        out_specs=[hbm] * 3 + [vm] * 2,
        out_shape=[jax.ShapeDtypeStruct((t, CONV_W), BF16)] * 3
        + [jax.ShapeDtypeStruct((32, CONV_W), F32), jax.ShapeDtypeStruct((8, CONV_W), F32)],
        scratch_shapes=[pltpu.VMEM((t + CONV_PAD, CONV_W), F32), pltpu.VMEM((t + CONV_PAD, CONV_W), F32),
                        pltpu.VMEM((8, CONV_W // 128, CONV_CH + CONV_PAD, 128), F32), pltpu.VMEM((8 * 32, CONV_W), F32)]
        + [pltpu.VMEM((t, CONV_W), F32)] * 4 + [pltpu.VMEM((t, CONV_W), BF16)] * 4
        + [pltpu.SemaphoreType.DMA((5,))] + [pltpu.SemaphoreType.DMA((t // PUT_ROWS,))] * 3,
        compiler_params=_cparams(),
    )(ua, ug, gb, cz, dmix, cw, lw, lb)


def _out_proj(mix_a, mix_b, x, tgt, gate, w_out):
    t = x.shape[0]
    tm = 256
    nstep = t // tm

    def body(ma_ref, mb_ref, x_ref, t_ref, g_ref, w_ref, dout_ref, dma_ref, dmb_ref, gw_ref, red_ref, acc_ref):
        i = pl.program_id(0)

        @pl.when(i == 0)
        def _():
            acc_ref[...] = jnp.zeros_like(acc_ref)
            red_ref[...] = jnp.zeros_like(red_ref)

        mix = jnp.concatenate([ma_ref[...], mb_ref[...]], axis=1)
        y = jnp.dot(mix, w_ref[...], preferred_element_type=F32)
        gate_v = g_ref[...]
        err = x_ref[...] + gate_v * y - t_ref[...]
        dout = err * (1.0 / D_MODEL)
        dout_ref[...] = dout
        red_ref[0:1, :] += jnp.sum(dout * y, axis=0, keepdims=True)
        red_ref[1:2, :] += jnp.sum(err * err, axis=0, keepdims=True)
        dy = (dout * gate_v).astype(BF16)
        dmix = lax.dot_general(dy, w_ref[...], (((1,), (1,)), ((), ())), preferred_element_type=F32)
        dma_ref[...] = dmix[:, 0:512].astype(BF16)
        dmb_ref[...] = dmix[:, 512:1024].astype(BF16)
        acc_ref[...] += lax.dot_general(mix, dy, (((0,), (0,)), ((), ())), preferred_element_type=F32)

        @pl.when(i == nstep - 1)
        def _():
            gw_ref[...] = acc_ref[...].astype(BF16)

    row = lambda w: pl.BlockSpec((tm, w), lambda i: (i, 0))
    const = lambda s: pl.BlockSpec(s, lambda i: (0, 0))
    return pl.pallas_call(
        body,
        name="out_proj",
        grid=(nstep,),
        in_specs=[row(512), row(512), row(D_MODEL), row(D_MODEL), const((1, D_MODEL)),
                  pl.BlockSpec((D_MODEL, D_MODEL), lambda i: (0, 0), pipeline_mode=pl.Buffered(1))],
        out_specs=[row(D_MODEL), row(512), row(512), const((D_MODEL, D_MODEL)), const((8, D_MODEL))],
        out_shape=[jax.ShapeDtypeStruct((t, D_MODEL), F32), jax.ShapeDtypeStruct((t, 512), BF16),
                   jax.ShapeDtypeStruct((t, 512), BF16), jax.ShapeDtypeStruct((D_MODEL, D_MODEL), BF16),
                   jax.ShapeDtypeStruct((8, D_MODEL), F32)],
        scratch_shapes=[pltpu.VMEM((D_MODEL, D_MODEL), F32)],
        compiler_params=_cparams(dimension_semantics=("arbitrary",)),
    )(mix_a, mix_b, x, tgt, gate, w_out)


DPROJ_WIDTHS = (512, 256, 512, 512, 512, 512)
DPROJ_STARTS = (0, 512, 768, 1280, 1792, 2304)
WIN_W = 768
WIN_START = (0, 640, 1408, 2048)
WIN_OFF = (0, 64, 0, 64)
N_GW = N_CHIPS


def _window_pieces(s):
    lo, hi = WIN_START[s], WIN_START[s] + WIN_W
    out = []
    for p, (st, w) in enumerate(zip(DPROJ_STARTS, DPROJ_WIDTHS)):
        a, b = max(lo, st), min(hi, st + w)
        if a < b:
            out.append((p, a - st, b - a, a - lo))
    return out


def _in_proj_bwd(dparts, h, x, dout, s1, nw, wt_full, small0):
    t = x.shape[0]
    tm = 256
    nstep = N_GW + t // tm
    n_sem = 15
    rows0 = small0.shape[0]
    npart = len(DPROJ_WIDTHS)

    def body(*refs):
        d_hbm, d_ref = refs[:npart], refs[npart:2 * npart]
        (x_ref, dout_ref, s1_ref, nw_ref, h_ref, wt_hbm, sm0_ref,
         gx_ref, gw_hbm, ssum_ref, tail_ref,
         stg_ref, wt_ref, gt_ref, sib_ref, out_ref, in_ref, res_ref, sall_ref, red_ref,
         wsem, lsem, ssem, rsem) = refs[2 * npart:]
        i = pl.program_id(0)
        x_, y_, c, chips = _place()
        j = 2 * x_ + y_
        dev = 2 * j + c
        sib = (x_, y_, 1 - c)
        rc = functools.partial(_remote, ssem, rsem)
        rel_chip = [2 * cx + cy for cx, cy in chips] + [j]
        peers = [(px, py, pc) for px in (x_, 1 - x_) for py in (y_, 1 - y_) for pc in (c, 1 - c)][1:]
        wt_copy = pltpu.make_async_copy(wt_hbm, wt_ref, lsem.at[0])

        def window(case, slot):
            return [pltpu.make_async_copy(d_hbm[p].at[:, pl.ds(c0, w)], stg_ref.at[slot, :, pl.ds(w0, w)], wsem.at[slot, n])
                    for n, (p, c0, w, w0) in enumerate(_window_pieces(case))]

        def to_sibling(k):
            return rc(k, gt_ref.at[k, 1 - c], sib_ref.at[k], sib)

        def to_chip(k):
            return rc(4 + k, out_ref.at[k], in_ref.at[k], (*chips[k], c))

        def trade(k):
            to_sibling(k).wait_recv()

            def add(n, carry):
                rr = pl.ds(pl.multiple_of(n * RS_CH, RS_CH), RS_CH)
                out_ref[k, rr, :] = (gt_ref[k, c, rr, :].astype(F32) + sib_ref[k, rr, :].astype(F32)).astype(BF16)
                return carry

            lax.fori_loop(0, IN_HALF // RS_CH, add, 0)
            to_chip(k).start()

        for k in range(N_GW):
            @pl.when(i == k)
            def _(k=k):
                slot = k % 2
                if k == 0:
                    red_ref[...] = jnp.zeros_like(red_ref)
                    wt_copy.start()
                for case in range(N_CHIPS):
                    if k == 0:
                        @pl.when(rel_chip[0] == case)
                        def _():
                            for cp in window(case, 0):
                                cp.start()
                    if k + 1 < N_GW:
                        @pl.when(rel_chip[k + 1] == case)
                        def _():
                            for cp in window(case, 1 - slot):
                                cp.start()
                for case in range(N_CHIPS):
                    @pl.when(rel_chip[k] == case)
                    def _():
                        for cp in window(case, slot):
                            cp.wait()
                g = lax.dot_general(stg_ref[slot], h_ref[...], (((0,), (0,)), ((), ())), preferred_element_type=F32)
                for off in sorted(set(WIN_OFF)):
                    @pl.when(rel_chip[k] % 2 == (1 if off else 0))
                    def _():
                        gt_ref[k, 0] = g[off:off + IN_HALF].astype(BF16)
                        gt_ref[k, 1] = g[off + IN_HALF:off + 2 * IN_HALF].astype(BF16)
                to_sibling(k).start()
                if k >= 1:
                    trade(k - 1)

        @pl.when(i == N_GW)
        def _():
            wt_copy.wait()

        @pl.when(i >= N_GW)
        def _():
            xv = x_ref[...]
            r = lax.rsqrt(jnp.mean(xv * xv, axis=-1, keepdims=True) + EPS)
            xh = xv * r
            n = xh * nw_ref[...]
            dproj = jnp.concatenate([ref[...] for ref in d_ref], axis=1)
            dh = jnp.dot(dproj, wt_ref[...], preferred_element_type=F32)
            red_ref[0:1, :] += jnp.sum(dh, axis=0, keepdims=True)
            red_ref[1:2, :] += jnp.sum(dh * n, axis=0, keepdims=True)
            dn = dh * s1_ref[...]
            red_ref[2:3, :] += jnp.sum(dn * xh, axis=0, keepdims=True)
            dxh = dn * nw_ref[...]
            gx_ref[...] = dout_ref[...] + r * (dxh - xh * jnp.mean(dxh * xh, axis=-1, keepdims=True))

        @pl.when(i == nstep - 1)
        def _():
            sall_ref[dev, 0:rows0, :] = sm0_ref[...]
            sall_ref[dev, rows0:rows0 + 8, :] = red_ref[...]
            sends = [rc(8 + k, sall_ref.at[dev], sall_ref.at[dev], peer) for k, peer in enumerate(peers)]
            for cp in sends:
                cp.start()
            sends += [to_sibling(k) for k in range(N_GW)] + [to_chip(k) for k in range(3)]
            own = N_GW - 1
            to_sibling(own).wait_recv()
            for k in range(3):
                to_chip(k).wait_recv()

            def total(n, carry):
                rr = pl.ds(pl.multiple_of(n * RS_CH, RS_CH), RS_CH)
                acc = gt_ref[own, c, rr, :].astype(F32) + sib_ref[own, rr, :].astype(F32)
                for k in range(3):
                    acc = acc + in_ref[k, rr, :].astype(F32)
                res_ref[c, rr, :] = acc
                return carry

            lax.fori_loop(0, IN_HALF // RS_CH, total, 0)
            share = rc(7, res_ref.at[c], res_ref.at[c], sib)
            share.start()
            sends.append(share)
            for k, (px, py, pc) in enumerate(peers):
                pdev = 4 * px + 2 * py + pc
                rc(8 + k, sall_ref.at[pdev], sall_ref.at[pdev], (px, py, pc)).wait_recv()
            tot = sall_ref[0]
            for d in range(1, N_DEV):
                tot = tot + sall_ref[d]
            ssum_ref[...] = tot
            tail_ref[...] = sall_ref[:, rows0 - 8:rows0 + 8, :]
            rc(7, res_ref.at[1 - c], res_ref.at[1 - c], sib).wait_recv()
            back = pltpu.make_async_copy(res_ref, gw_hbm, lsem.at[1])
            back.start()
            for cp in sends:
                cp.wait_send()
            back.wait()

    blk = lambda i: jnp.maximum(i - N_GW, 0)
    row = lambda w: pl.BlockSpec((tm, w), lambda i: (blk(i), 0))
    vec = pl.BlockSpec((1, D_MODEL), lambda i: (0, 0))
    const = lambda shape: pl.BlockSpec(shape, lambda i: (0,) * len(shape))
    hbm = pl.BlockSpec(memory_space=pl.ANY)
    return pl.pallas_call(
        body,
        name="in_proj_bwd",
        grid=(nstep,),
        in_specs=[hbm] * npart + [row(w) for w in DPROJ_WIDTHS] + [row(D_MODEL), row(D_MODEL), vec, vec,
                  pl.BlockSpec((t, D_MODEL), lambda i: (0, 0), pipeline_mode=pl.Buffered(1)), hbm, const((rows0, D_MODEL))],
        out_specs=[row(D_MODEL), hbm, const((rows0 + 8, D_MODEL)), const((N_DEV, 16, D_MODEL))],
        out_shape=[jax.ShapeDtypeStruct((t, D_MODEL), F32), jax.ShapeDtypeStruct((2, IN_HALF, D_MODEL), F32),
                   jax.ShapeDtypeStruct((rows0 + 8, D_MODEL), F32), jax.ShapeDtypeStruct((N_DEV, 16, D_MODEL), F32)],
        scratch_shapes=[pltpu.VMEM((2, t, WIN_W), BF16), pltpu.VMEM((IN_W, D_MODEL), BF16),
                        pltpu.VMEM((N_CHIPS, 2, IN_HALF, D_MODEL), BF16), pltpu.VMEM((N_CHIPS, IN_HALF, D_MODEL), BF16),
                        pltpu.VMEM((3, IN_HALF, D_MODEL), BF16), pltpu.VMEM((3, IN_HALF, D_MODEL), BF16),
                        pltpu.VMEM((2, IN_HALF, D_MODEL), F32), pltpu.VMEM((N_DEV, rows0 + 8, D_MODEL), F32),
                        pltpu.VMEM((8, D_MODEL), F32), pltpu.SemaphoreType.DMA((2, 3)), pltpu.SemaphoreType.DMA((2,)),
                        pltpu.SemaphoreType.DMA((n_sem,)), pltpu.SemaphoreType.DMA((n_sem,))],
        compiler_params=_cparams(dimension_semantics=("arbitrary",)),
    )(*dparts, *dparts, x, dout, s1, nw, h, wt_full, small0)


MESH = pl.DeviceIdType.MESH


def _place():
    x, y, c = lax.axis_index("x"), lax.axis_index("y"), lax.axis_index("c")
    chips = [(1 - x, y), (x, 1 - y), (1 - x, 1 - y)]
    return x, y, c, chips


def _remote(sems_s, sems_r, k, src, dst, to):
    return pltpu.make_async_remote_copy(src_ref=src, dst_ref=dst, send_sem=sems_s.at[k], recv_sem=sems_r.at[k],
                                        device_id=to, device_id_type=MESH)


RS_CH = 32
RS_SEMS = 5


def _rs_to_sibling(rc, s0, theirs, sib_ref, sib):
    cp = rc(s0, theirs, sib_ref, sib)
    cp.start()
    return cp


def _rs_trade(rc, s0, theirs, mine, sib_ref, out_ref, in_ref, rows, c, sib, chips):
    rc(s0, theirs, sib_ref, sib).wait_recv()
    cps = []
    for k, (cx, cy) in enumerate(chips):
        jk = 2 * cx + cy

        def add(i, carry, jk=jk, k=k):
            rr = pl.ds(pl.multiple_of(i * RS_CH, RS_CH), RS_CH)
            out_ref[k, rr, :] = (mine[jk, rr, :].astype(F32) + sib_ref[jk, rr, :].astype(F32)).astype(BF16)
            return carry

        lax.fori_loop(0, rows // RS_CH, add, 0)
        cps.append(rc(s0 + 1 + k, out_ref.at[k], in_ref.at[k], (cx, cy, c)))
        cps[-1].start()
    return cps


def _rs_total(rc, s0, mine, sib_ref, out_ref, in_ref, res_ref, rows, j, c, sib):
    for k in range(3):
        rc(s0 + 1 + k, out_ref.at[k], in_ref.at[k], sib).wait_recv()

    def total(i, carry):
        rr = pl.ds(pl.multiple_of(i * RS_CH, RS_CH), RS_CH)
        acc = mine[j, rr, :].astype(F32) + sib_ref[j, rr, :].astype(F32)
        for k in range(3):
            acc = acc + in_ref[k, rr, :].astype(F32)
        res_ref[c, rr, :] = acc
        return carry

    lax.fori_loop(0, rows // RS_CH, total, 0)
    cp = rc(s0 + 4, res_ref.at[c], res_ref.at[c], sib)
    cp.start()
    return cp


def _rs_done(rc, s0, res_ref, c, sib):
    rc(s0 + 4, res_ref.at[1 - c], res_ref.at[1 - c], sib).wait_recv()


def _rs_scratch(rows):
    return [pltpu.VMEM((N_CHIPS, rows, D_MODEL), BF16), pltpu.VMEM((3, rows, D_MODEL), BF16),
            pltpu.VMEM((3, rows, D_MODEL), BF16)]


def _gather_weights(wt, c_row, w_ada, b_sh):
    n_sem = 16

    def body(wt_ref, c_ref, wada_ref, bsh_ref, w4_ref, call_ref, ada_ref, part_ref, ssem, rsem):
        x, y, c, chips = _place()
        j = 2 * x + y
        dev = 2 * j + c
        sib = (x, y, 1 - c)
        idx = [2 * cx + cy for cx, cy in chips]
        rc = functools.partial(_remote, ssem, rsem)

        w4_ref[j] = wt_ref[...].astype(BF16)
        call_ref[dev] = c_ref[...]

        sends = []
        peers = [(px, py, pc) for px in (x, 1 - x) for py in (y, 1 - y) for pc in (c, 1 - c)][1:]
        for k, peer in enumerate(peers):
            sends.append(rc(k, call_ref.at[dev], call_ref.at[dev], peer))
        for k, chip in enumerate(chips):
            sends.append(rc(7 + k, w4_ref.at[j, c], w4_ref.at[j, c], (*chip, c)))
        for cp in sends:
            cp.start()

        for k, (px, py, pc) in enumerate(peers):
            pdev = 4 * px + 2 * py + pc
            rc(k, call_ref.at[pdev], call_ref.at[pdev], (px, py, pc)).wait_recv()
        rowid = lax.broadcasted_iota(jnp.int32, (N_DEV, D_MODEL), 0)
        call = jnp.zeros((N_DEV, D_MODEL), F32)
        for r in range(N_DEV):
            call = jnp.where(rowid == r, jnp.broadcast_to(call_ref[r], (N_DEV, D_MODEL)), call)
        part = jnp.dot(_silu(call).astype(BF16), wada_ref[...].astype(BF16), preferred_element_type=F32) + bsh_ref[...]
        for r in range(N_DEV):
            part_ref[r] = part[r:r + 1, :]
        ada_ref[j] = part_ref[dev]
        rows_out = []
        for k, chip in enumerate(chips):
            rows_out.append(rc(13 + k, part_ref.at[2 * idx[k] + c], ada_ref.at[j], (*chip, c)))
            rows_out[-1].start()

        passed = []
        for k, chip in enumerate(chips):
            jk = idx[k]
            rc(7 + k, w4_ref.at[jk, c], w4_ref.at[jk, c], sib).wait_recv()
            passed.append(rc(10 + k, w4_ref.at[jk, c], w4_ref.at[jk, c], sib))
            passed[-1].start()
        for k, chip in enumerate(chips):
            jk = idx[k]
            rc(10 + k, w4_ref.at[jk, 1 - c], w4_ref.at[jk, 1 - c], sib).wait_recv()
            rc(13 + k, ada_ref.at[jk], ada_ref.at[jk], sib).wait_recv()
        for cp in sends + rows_out + passed:
            cp.wait_send()

    vm = pl.BlockSpec(memory_space=pltpu.VMEM)
    return pl.pallas_call(
        body,
        name="gather_weights",
        in_specs=[vm] * 4,
        out_specs=[vm] * 3,
        out_shape=[jax.ShapeDtypeStruct((N_CHIPS, 2, IN_HALF, D_MODEL), BF16),
                   jax.ShapeDtypeStruct((N_DEV, 1, D_MODEL), F32),
                   jax.ShapeDtypeStruct((N_CHIPS, 1, ADA_SHARD), F32)],
        scratch_shapes=[pltpu.VMEM((N_DEV, 1, ADA_SHARD), F32),
                        pltpu.SemaphoreType.DMA((n_sem,)), pltpu.SemaphoreType.DMA((n_sem,))],
        compiler_params=_cparams(),
    )(wt, c_row, w_ada, b_sh)


def _adamw_math(w, g, m, v):
    m2 = ADAM_B1 * m + (1.0 - ADAM_B1) * g
    v2 = ADAM_B2 * v + (1.0 - ADAM_B2) * (g * g)
    m_hat = m2 / (1.0 - ADAM_B1 ** ADAM_STEP)
    v_hat = v2 / (1.0 - ADAM_B2 ** ADAM_STEP)
    delta = -ADAM_LR * (m_hat / (jnp.sqrt(v_hat) + ADAM_EPS) + ADAM_WD * w)
    return delta, m2, v2


def _adamw(name, w, g, m, v, tm):
    r, cdim = w.shape

    def body(w_ref, g_ref, m_ref, v_ref, d_ref, m2_ref, v2_ref):
        d_ref[...], m2_ref[...], v2_ref[...] = _adamw_math(w_ref[...], g_ref[...], m_ref[...], v_ref[...])

    blk = pl.BlockSpec((tm, cdim), lambda i: (i, 0))
    return pl.pallas_call(
        body,
        name=name,
        grid=(r // tm,),
        in_specs=[blk] * 4,
        out_specs=[blk] * 3,
        out_shape=[jax.ShapeDtypeStruct((r, cdim), F32)] * 3,
        compiler_params=_cparams(dimension_semantics=("arbitrary",)),
    )(w, g, m, v)


def _adamw_ada(w, m, v, cact_t, dcols):
    r, cdim = w.shape
    tm = 256

    def body(w_ref, m_ref, v_ref, ct_ref, dc_ref, g_ref, d_ref, m2_ref, v2_ref):
        g = jnp.dot(ct_ref[...], dc_ref[...], preferred_element_type=F32, precision=lax.Precision.HIGHEST)
        g_ref[...] = g
        d_ref[...], m2_ref[...], v2_ref[...] = _adamw_math(w_ref[...], g, m_ref[...], v_ref[...])

    blk = pl.BlockSpec((tm, cdim), lambda i: (i, 0))
    return pl.pallas_call(
        body,
        name="adamw_w_ada",
        grid=(r // tm,),
        in_specs=[blk] * 3 + [pl.BlockSpec((tm, N_DEV), lambda i: (i, 0)), pl.BlockSpec((N_DEV, cdim), lambda i: (0, 0))],
        out_specs=[blk] * 4,
        out_shape=[jax.ShapeDtypeStruct((r, cdim), F32)] * 4,
        compiler_params=_cparams(dimension_semantics=("arbitrary",)),
    )(w, m, v, cact_t, dcols)


def _adamw_small(ws, gs, ms, vs):
    n = len(ws)

    def body(*refs):
        w_r, g_r, m_r, v_r = refs[0:n], refs[n:2 * n], refs[2 * n:3 * n], refs[3 * n:4 * n]
        d_r, m2_r, v2_r = refs[4 * n:5 * n], refs[5 * n:6 * n], refs[6 * n:7 * n]
        for i in range(n):
            d_r[i][...], m2_r[i][...], v2_r[i][...] = _adamw_math(w_r[i][...], g_r[i][...], m_r[i][...], v_r[i][...])

    vm = pl.BlockSpec(memory_space=pltpu.VMEM)
    shapes = [jax.ShapeDtypeStruct(w.shape, F32) for w in ws]
    out = pl.pallas_call(
        body,
        name="adamw_small",
        in_specs=[vm] * (4 * n),
        out_specs=[vm] * (3 * n),
        out_shape=shapes * 3,
        compiler_params=_cparams(),
    )(*ws, *gs, *ms, *vs)
    return out[0:n], out[n:2 * n], out[2 * n:3 * n]


def _rope_tables(t):
    inv = ROPE_THETA ** (-jnp.arange(0, HEAD_DIM, 2, dtype=F32) / HEAD_DIM)
    ang = jnp.arange(t, dtype=F32)[:, None] * inv[None, :]
    cos, sin = jnp.cos(ang), jnp.sin(ang)
    return jnp.tile(cos, (1, 4)), jnp.tile(jnp.concatenate([-sin, sin], axis=1), (1, 2))


def _pad_lanes(v, width):
    return jnp.pad(v, ((0, 0), (0, width - v.shape[1])))


def kernel(x, c, w_ada, b_ada, norm_w, w_in, q_norm_w, k_norm_w, sinks, conv_w, conv_b, ln_w, ln_b, w_out, loss_target, m_w_ada, m_b_ada, m_norm_w, m_w_in, m_q_norm_w, m_k_norm_w, m_sinks, m_conv_w, m_conv_b, m_ln_w, m_ln_b, m_w_out, v_w_ada, v_b_ada, v_norm_w, v_w_in, v_q_norm_w, v_k_norm_w, v_sinks, v_conv_w, v_conv_b, v_ln_w, v_ln_b, v_w_out):
    xi, yi = lax.axis_index("x"), lax.axis_index("y")
    j = 2 * xi + yi
    x2, tgt = x[0], loss_target[0]
    t = x2.shape[0]

    wt_s, mt_s, vt_s = w_in[0].T, m_w_in[0].T, v_w_in[0].T
    cw_pad = jnp.pad(conv_w[0], ((0, 1), (0, 0)))
    b_sh = lax.dynamic_slice(b_ada, (0, ADA_SHARD * j), (1, ADA_SHARD))

    w4, call, ada4 = _gather_weights(wt_s.reshape(2, IN_HALF, D_MODEL), c, w_ada[0], b_sh)
    w_full = w4.reshape(IN_W, D_MODEL)
    ada = ada4.reshape(1, 3 * D_MODEL)
    shift, s1, gate = ada[:, :D_MODEL], 1.0 + ada[:, D_MODEL:2 * D_MODEL], ada[:, 2 * D_MODEL:]

    cos_f, sin_s = _rope_tables(t)
    qw2, kw2 = jnp.tile(q_norm_w, (1, 2)), jnp.tile(k_norm_w, (1, 2))

    q_raw, kv_raw, ga, ua, ug, gb, h = _in_proj(x2, s1, shift, norm_w, w_full)
    o, mix_a, wo4, cw4 = _attn_fwd(q_raw, kv_raw, ga, qw2, kw2, sinks, cos_f, sin_s,
                                   w_out[0].reshape(2, OUT_HALF, D_MODEL), cw_pad)
    w_out_full = wo4.reshape(D_MODEL, D_MODEL)
    cw_full = jnp.concatenate([cw4[i] for i in range(N_CHIPS)], axis=1)
    cz, mix_b = _conv_fwd(ua, ug, gb, cw_full, conv_b, ln_w, ln_b)
    dout, dmix_a, dmix_b, gwo_bf, red_o = _out_proj(mix_a, mix_b, x2, tgt, gate, w_out_full)

    dq, dkv, dga, sm_a, gwo = _attn_bwd(q_raw, kv_raw, ga, o, dmix_a, qw2, kw2, sinks, cos_f, sin_s,
                                        gwo_bf.reshape(N_CHIPS, 2, OUT_HALF, D_MODEL))
    dua, dug, dgb, dcw, dvec = _conv_bwd(ua, ug, gb, cz, dmix_b, cw_full, ln_w, ln_b)
    dparts = (dq, dkv, dga, dua, dug, dgb)

    small0 = jnp.concatenate([
        dcw.reshape(16, D_MODEL), jnp.pad(dvec.reshape(4, D_MODEL), ((0, 4), (0, 0))), _pad_lanes(sm_a, D_MODEL), red_o], axis=0)
    grad_x, gw, ssum, tail = _in_proj_bwd(dparts, h, x2, dout, s1, norm_w, w_full, small0)

    loss = (0.5 / D_MODEL) * jnp.sum(ssum[33])
    gt_w_in = gw.reshape(2 * IN_HALF, D_MODEL)
    g_w_out = gwo.reshape(D_MODEL // N_CHIPS, D_MODEL)
    g_conv_w = lax.dynamic_slice(ssum[0:16].reshape(32, CONV_W), (0, 128 * j), (CONV_TAPS, 128))
    g_vec = ssum[16:20].reshape(8, CONV_W)
    g_conv_b, g_ln_w, g_ln_b = g_vec[0:1], g_vec[1:2], g_vec[2:3]
    g_qw, g_kw, g_sinks = ssum[24:25, 0:HEAD_DIM], ssum[25:26, 0:HEAD_DIM], ssum[26:27, 0:8]
    g_norm_w = ssum[42:43]
    g_b_ada = jnp.concatenate([ssum[40:41], ssum[41:42], ssum[32:33]], axis=1)
    d_ada_all = jnp.concatenate([tail[:, 8], tail[:, 9], tail[:, 0]], axis=1)
    dcols = lax.dynamic_slice(d_ada_all, (0, ADA_SHARD * j), (N_DEV, ADA_SHARD))
    cact_t = jax.nn.silu(call.reshape(N_DEV, D_MODEL)).T

    g_w_ada, d_w_ada, nm_w_ada, nv_w_ada = _adamw_ada(w_ada[0], m_w_ada[0], v_w_ada[0], cact_t, dcols)
    dt_w_in, nmt_w_in, nvt_w_in = _adamw("adamw_w_in", wt_s, gt_w_in, mt_s, vt_s, 176)
    g_w_in, d_w_in, nm_w_in, nv_w_in = gt_w_in.T, dt_w_in.T, nmt_w_in.T, nvt_w_in.T
    d_w_out, nm_w_out, nv_w_out = _adamw("adamw_w_out", w_out[0], g_w_out, m_w_out[0], v_w_out[0], 128)
    ws = [b_ada, norm_w, q_norm_w, k_norm_w, sinks, conv_w[0], conv_b, ln_w, ln_b]
    gs = [g_b_ada, g_norm_w, g_qw, g_kw, g_sinks, g_conv_w, g_conv_b, g_ln_w, g_ln_b]
    ms = [m_b_ada, m_norm_w, m_q_norm_w, m_k_norm_w, m_sinks, m_conv_w[0], m_conv_b, m_ln_w, m_ln_b]
    vs = [v_b_ada, v_norm_w, v_q_norm_w, v_k_norm_w, v_sinks, v_conv_w[0], v_conv_b, v_ln_w, v_ln_b]
    ds, nms, nvs = _adamw_small(ws, gs, ms, vs)

    def order(ada_v, in_v, out_v, sm):
        b, nw_, qw_, kw_, sk_, cw_, cb_, lw_, lb_ = sm
        return [ada_v[None], b, nw_, in_v[None], qw_, kw_, sk_, cw_[None], cb_, lw_, lb_, out_v[None]]

    grads = order(g_w_ada, g_w_in, g_w_out, gs)
    deltas = order(d_w_ada, d_w_in, d_w_out, ds)
    new_m = order(nm_w_ada, nm_w_in, nm_w_out, nms)
    new_v = order(nv_w_ada, nv_w_in, nv_w_out, nvs)
    return (loss, grad_x[None], *grads, *deltas, *new_m, *new_v)
```

```python
import functools

import jax
import jax.numpy as jnp
from jax import lax
from jax.experimental import pallas as pl
from jax.experimental.pallas import tpu as pltpu

F32 = jnp.float32
BF16 = jnp.bfloat16

D_MODEL = 1024
ATTN_W = 512
KV_W = 128
CONV_W = 512
IN_W = 2816
HEAD_DIM = 64
CONV_TAPS = 31
QBLK = 128
EPS = 1e-6
ROPE_THETA = 10000.0

ADAM_LR = 0.001
ADAM_B1 = 0.9
ADAM_B2 = 0.999
ADAM_EPS = 1e-08
ADAM_WD = 0.01
ADAM_STEP = 10

N_CHIPS = 4
N_DEV = 8
IN_HALF = IN_W // N_CHIPS // 2
OUT_HALF = D_MODEL // N_CHIPS // 2
ADA_SHARD = 3 * D_MODEL // N_CHIPS

VMEM_LIMIT = 56 * 1024 * 1024
CONV_PAD = 32


def _cparams(**kw):
    return pltpu.CompilerParams(vmem_limit_bytes=VMEM_LIMIT, **kw)


def _sigmoid(v):
    return 1.0 / (1.0 + jnp.exp(-v))


def _silu(v):
    return v * _sigmoid(v)


def _dsilu(v):
    s = _sigmoid(v)
    return s * (1.0 + v * (1.0 - s))


def _lane(shape):
    return lax.broadcasted_iota(jnp.int32, shape, len(shape) - 1)


PUT_ROWS = 512


def _fetch(hbm_refs, vmem_refs, sem):
    cps = [pltpu.make_async_copy(h, v, sem.at[i]) for i, (h, v) in enumerate(zip(hbm_refs, vmem_refs))]
    for cp in cps:
        cp.start()
    return cps


def _put(vmem_ref, hbm_ref, sem, m):
    r = pl.ds(pl.multiple_of(m * PUT_ROWS, PUT_ROWS), PUT_ROWS)
    return pltpu.make_async_copy(vmem_ref.at[r], hbm_ref.at[r], sem.at[m])


def _put_all(pairs, sems, m):
    for (v, h), sem in zip(pairs, sems):
        _put(v, h, sem, m).start()


def _put_wait(pairs, sems, n):
    for (v, h), sem in zip(pairs, sems):
        for m in range(n):
            _put(v, h, sem, m).wait()


def _in_proj(x, s1, shift, nw, wt_full):
    t = x.shape[0]
    tm = 256

    def body(x_ref, s1_ref, sh_ref, nw_ref, w_ref, q_ref, kv_ref, ga_ref, ua_ref, ug_ref, gb_ref, h_ref):
        xv = x_ref[...]
        r = lax.rsqrt(jnp.mean(xv * xv, axis=-1, keepdims=True) + EPS)
        h = ((xv * r) * nw_ref[...] * s1_ref[...] + sh_ref[...]).astype(BF16)
        h_ref[...] = h
        p = lax.dot_general(h, w_ref[...], (((1,), (1,)), ((), ())), preferred_element_type=F32)
        q_ref[...] = p[:, 0:512]
        kv_ref[...] = p[:, 512:768]
        ga_ref[...] = p[:, 768:1280]
        ua_ref[...] = p[:, 1280:1792]
        ug_ref[...] = p[:, 1792:2304]
        gb_ref[...] = p[:, 2304:2816]

    row = lambda w: pl.BlockSpec((tm, w), lambda i: (i, 0))
    vec = pl.BlockSpec((1, D_MODEL), lambda i: (0, 0))
    return pl.pallas_call(
        body,
        name="in_proj",
        grid=(t // tm,),
        in_specs=[row(D_MODEL), vec, vec, vec,
                  pl.BlockSpec((IN_W, D_MODEL), lambda i: (0, 0), pipeline_mode=pl.Buffered(1))],
        out_specs=[row(512), row(256), row(512), row(512), row(512), row(512), row(D_MODEL)],
        out_shape=[jax.ShapeDtypeStruct((t, w), F32) for w in (512, 256, 512, 512, 512, 512)]
        + [jax.ShapeDtypeStruct((t, D_MODEL), BF16)],
        compiler_params=_cparams(dimension_semantics=("arbitrary",)),
    )(x, s1, shift, nw, wt_full)


def _head_mean(s, left):
    sl = jnp.sum(jnp.where(left, s, 0.0), axis=-1, keepdims=True)
    sr = jnp.sum(jnp.where(left, 0.0, s), axis=-1, keepdims=True)
    return jnp.where(left, sl, sr) * (1.0 / HEAD_DIM)


def _rot(v, first):
    return jnp.where(first, pltpu.roll(v, 96, 1), pltpu.roll(v, 32, 1))


def _norm_rope(v, w, cos, sin_s, left, first):
    r = lax.rsqrt(_head_mean(v * v, left) + EPS)
    xh = v * r
    n = xh * w
    return n * cos + _rot(n, first) * sin_s, xh, r


def _norm_rope_bwd(d, xh, r, w, cos, sin_s, left, first):
    dn = d * cos - _rot(d, first) * sin_s
    dw = jnp.sum(dn * xh, axis=0, keepdims=True)
    dxh = dn * w
    return r * (dxh - xh * _head_mean(dxh * xh, left)), dw


def _dup_heads(v, left):
    sw = pltpu.roll(v, 64, 1)
    return jnp.where(left, v, sw), jnp.where(left, sw, v)


def _prep_kv(kv_ref, kw_ref, cos_ref, sin_ref, ka_ref, va_ref, t):
    ch = 256
    for g in range(2):
        ka_ref[g, 0:QBLK, :] = jnp.zeros((QBLK, 128), BF16)
        va_ref[g, 0:QBLK, :] = jnp.zeros((QBLK, 128), BF16)

    def chunk(i, carry):
        r0 = pl.multiple_of(i * ch, ch)
        left = _lane((ch, 128)) < 64
        first = (_lane((ch, 128)) % 64) < 32
        k = kv_ref[pl.ds(r0, ch), 0:128]
        v = kv_ref[pl.ds(r0, ch), 128:256]
        kr, _, _ = _norm_rope(k, kw_ref[...], cos_ref[pl.ds(r0, ch), :], sin_ref[pl.ds(r0, ch), :], left, first)
        k0, k1 = _dup_heads(kr, left)
        v0, v1 = _dup_heads(v, left)
        ka_ref[0, pl.ds(QBLK + r0, ch), :] = k0.astype(BF16)
        ka_ref[1, pl.ds(QBLK + r0, ch), :] = k1.astype(BF16)
        va_ref[0, pl.ds(QBLK + r0, ch), :] = v0.astype(BF16)
        va_ref[1, pl.ds(QBLK + r0, ch), :] = v1.astype(BF16)
        return carry

    lax.fori_loop(0, t // ch, chunk, 0)


def _band_mask(n):
    qi = lax.broadcasted_iota(jnp.int32, (2 * QBLK, 2 * QBLK), 0) % QBLK
    kj = lax.broadcasted_iota(jnp.int32, (2 * QBLK, 2 * QBLK), 1)
    local = (kj > qi) & (kj <= qi + QBLK)
    return local & ((n > 0) | (kj >= QBLK))


def _softmax_pair(s, mask, sink0, sink1):
    row = lax.broadcasted_iota(jnp.int32, (2 * QBLK, 1), 0)
    sink = jnp.where(row < QBLK, sink0, sink1)
    s = jnp.where(mask, s, -jnp.inf)
    m = jnp.maximum(jnp.max(s, axis=-1, keepdims=True), sink)
    e = jnp.exp(s - m)
    es = jnp.exp(sink - m)
    inv = 1.0 / (jnp.sum(e, axis=-1, keepdims=True) + es)
    return e * inv, es * inv


def _stack_heads(v, left):
    return jnp.concatenate([jnp.where(left, v, 0.0), jnp.where(left, 0.0, v)], axis=0)


def _attn_fwd(q_raw, kv_raw, ga, qw2, kw2, sinks, cos_f, sin_s, wo, cw):
    t = q_raw.shape[0]
    nblk = t // QBLK
    per_put = PUT_ROWS // QBLK

    def body(q_hbm, kv_ref, ga_hbm, qw_ref, kw_ref, sk_ref, cos_ref, sin_ref, wo_ref, cw_ref,
             o_hbm, mix_hbm, wo4_ref, cw4_ref, ka_ref, va_ref, q_ref, ga_ref, o_ref, mix_ref, isem, osem0, osem1,
             ssem, rsem):
        loads = _fetch((q_hbm, ga_hbm), (q_ref, ga_ref), isem)
        outs, osems = ((o_ref, o_hbm), (mix_ref, mix_hbm)), (osem0, osem1)
        x, y, c, chips = _place()
        j = 2 * x + y
        sib = (x, y, 1 - c)
        idx = [2 * cx + cy for cx, cy in chips]
        rc = functools.partial(_remote, ssem, rsem)
        wo4_ref[j] = wo_ref[...].astype(BF16)
        cw4_ref[j] = cw_ref[...]
        sends = []
        for k, chip in enumerate(chips):
            sends.append(rc(k, wo4_ref.at[j, c], wo4_ref.at[j, c], (*chip, c)))
            sends.append(rc(6 + k, cw4_ref.at[j], cw4_ref.at[j], (*chip, c)))
        for cp in sends:
            cp.start()

        _prep_kv(kv_ref, kw_ref, cos_ref, sin_ref, ka_ref, va_ref, t)
        for cp in loads:
            cp.wait()

        def blk(n, carry):
            r0 = pl.multiple_of(n * QBLK, QBLK)
            left = _lane((QBLK, 128)) < 64
            first = (_lane((QBLK, 128)) % 64) < 32
            cos = cos_ref[pl.ds(r0, QBLK), :]
            sin = sin_ref[pl.ds(r0, QBLK), :]
            mask = _band_mask(n)
            for p in range(4):
                g = p // 2
                lanes = slice(p * 128, (p + 1) * 128)
                qr, _, _ = _norm_rope(q_ref[pl.ds(r0, QBLK), lanes], qw_ref[...], cos, sin, left, first)
                q2 = _stack_heads(qr * 0.125, left).astype(BF16)
                s = lax.dot_general(q2, ka_ref[g, pl.ds(r0, 2 * QBLK), :], (((1,), (1,)), ((), ())),
                                    preferred_element_type=F32)
                pm, _ = _softmax_pair(s, mask, sk_ref[0, 2 * p], sk_ref[0, 2 * p + 1])
                o2 = jnp.dot(pm.astype(BF16), va_ref[g, pl.ds(r0, 2 * QBLK), :], preferred_element_type=F32)
                o = jnp.where(left, o2[0:QBLK], o2[QBLK:2 * QBLK])
                o_ref[pl.ds(r0, QBLK), lanes] = o.astype(BF16)
                mix_ref[pl.ds(r0, QBLK), lanes] = (o * _silu(ga_ref[pl.ds(r0, QBLK), lanes])).astype(BF16)

            @pl.when(n % per_put == per_put - 1)
            def _():
                _put_all(outs, osems, n // per_put)

            return carry

        lax.fori_loop(0, nblk, blk, 0)
        _put_wait(outs, osems, t // PUT_ROWS)

        passed = []
        for k, chip in enumerate(chips):
            jk = idx[k]
            rc(k, wo4_ref.at[jk, c], wo4_ref.at[jk, c], sib).wait_recv()
            passed.append(rc(3 + k, wo4_ref.at[jk, c], wo4_ref.at[jk, c], sib))
            passed[-1].start()
        for k, chip in enumerate(chips):
            jk = idx[k]
            rc(3 + k, wo4_ref.at[jk, 1 - c], wo4_ref.at[jk, 1 - c], sib).wait_recv()
            rc(6 + k, cw4_ref.at[jk], cw4_ref.at[jk], sib).wait_recv()
        for cp in sends + passed:
            cp.wait_send()

    vm = pl.BlockSpec(memory_space=pltpu.VMEM)
    hbm = pl.BlockSpec(memory_space=pl.ANY)
    n_sem = 9
    return pl.pallas_call(
        body,
        name="attn_fwd",
        in_specs=[hbm, vm, hbm, vm, vm, pl.BlockSpec(memory_space=pltpu.SMEM), vm, vm, vm, vm],
        out_specs=[hbm, hbm, vm, vm],
        out_shape=[jax.ShapeDtypeStruct((t, ATTN_W), BF16), jax.ShapeDtypeStruct((t, ATTN_W), BF16),
                   jax.ShapeDtypeStruct((N_CHIPS, 2, OUT_HALF, D_MODEL), BF16),
                   jax.ShapeDtypeStruct((N_CHIPS, 32, 128), F32)],
        scratch_shapes=[pltpu.VMEM((2, t + QBLK, 128), BF16), pltpu.VMEM((2, t + QBLK, 128), BF16),
                        pltpu.VMEM((t, ATTN_W), F32), pltpu.VMEM((t, ATTN_W), F32),
                        pltpu.VMEM((t, ATTN_W), BF16), pltpu.VMEM((t, ATTN_W), BF16),
                        pltpu.SemaphoreType.DMA((2,)), pltpu.SemaphoreType.DMA((t // PUT_ROWS,)),
                        pltpu.SemaphoreType.DMA((t // PUT_ROWS,)),
                        pltpu.SemaphoreType.DMA((n_sem,)), pltpu.SemaphoreType.DMA((n_sem,))],
        compiler_params=_cparams(),
    )(q_raw, kv_raw, ga, qw2, kw2, sinks, cos_f, sin_s, wo, cw)


def _attn_bwd(q_raw, kv_raw, ga, o, dmix, qw2, kw2, sinks, cos_f, sin_s, go):
    t = q_raw.shape[0]
    nblk = t // QBLK
    per_put = PUT_ROWS // QBLK

    def body(q_hbm, kv_ref, ga_hbm, o_hbm, dm_hbm, qw_ref, kw_ref, sk_ref, cos_ref, sin_ref, go_ref,
             dq_hbm, dkv_ref, dga_hbm, sm_ref, gwo_ref, ka_ref, va_ref, dka_ref, dva_ref,
             sibo_ref, outo_ref, ino_ref, q_ref, ga_ref, o_ref, dm_ref, dq_ref, dga_ref, isem, osem0, osem1, ssem, rsem):
        loads = _fetch((q_hbm, ga_hbm, o_hbm, dm_hbm), (q_ref, ga_ref, o_ref, dm_ref), isem)
        outs, osems = ((dq_ref, dq_hbm), (dga_ref, dga_hbm)), (osem0, osem1)
        x, y, c, chips = _place()
        sib = (x, y, 1 - c)
        rc = functools.partial(_remote, ssem, rsem)
        theirs, mine = go_ref.at[:, 1 - c], go_ref.at[:, c]
        sends = [_rs_to_sibling(rc, 0, theirs, sibo_ref, sib)]
        _prep_kv(kv_ref, kw_ref, cos_ref, sin_ref, ka_ref, va_ref, t)
        dka_ref[...] = jnp.zeros_like(dka_ref)
        dva_ref[...] = jnp.zeros_like(dva_ref)
        sends += _rs_trade(rc, 0, theirs, mine, sibo_ref, outo_ref, ino_ref, OUT_HALF, c, sib, chips)
        for cp in loads:
            cp.wait()

        def blk(n, carry):
            dqw, dsk = carry
            r0 = pl.multiple_of(n * QBLK, QBLK)
            left = _lane((QBLK, 128)) < 64
            first = (_lane((QBLK, 128)) % 64) < 32
            cos = cos_ref[pl.ds(r0, QBLK), :]
            sin = sin_ref[pl.ds(r0, QBLK), :]
            mask = _band_mask(n)
            row = lax.broadcasted_iota(jnp.int32, (2 * QBLK, 1), 0)
            for p in range(4):
                g = p // 2
                lanes = slice(p * 128, (p + 1) * 128)
                rows = pl.ds(r0, QBLK)
                win = pl.ds(r0, 2 * QBLK)
                qr, xh, r = _norm_rope(q_ref[rows, lanes], qw_ref[...], cos, sin, left, first)
                q2 = _stack_heads(qr * 0.125, left).astype(BF16)
                kwin = ka_ref[g, win, :]
                vwin = va_ref[g, win, :]
                s = lax.dot_general(q2, kwin, (((1,), (1,)), ((), ())), preferred_element_type=F32)
                pm, ps = _softmax_pair(s, mask, sk_ref[0, 2 * p], sk_ref[0, 2 * p + 1])
                gav = ga_ref[rows, lanes]
                dmv = dm_ref[rows, lanes].astype(F32)
                dga_ref[rows, lanes] = (dmv * o_ref[rows, lanes].astype(F32) * _dsilu(gav)).astype(BF16)
                do2 = _stack_heads(dmv * _silu(gav), left).astype(BF16)
                dp = lax.dot_general(do2, vwin, (((1,), (1,)), ((), ())), preferred_element_type=F32)
                delta = jnp.sum(pm * dp, axis=-1, keepdims=True)
                ds = (pm * (dp - delta)).astype(BF16)
                pd = ps * delta
                d0 = jnp.sum(jnp.where(row < QBLK, pd, 0.0), axis=0, keepdims=True)
                d1 = jnp.sum(jnp.where(row < QBLK, 0.0, pd), axis=0, keepdims=True)
                l8 = _lane((1, 128))
                dsk = dsk - jnp.where(l8 == 2 * p, d0, 0.0) - jnp.where(l8 == 2 * p + 1, d1, 0.0)
                dva_ref[g, win, :] += lax.dot_general(pm.astype(BF16), do2, (((0,), (0,)), ((), ())),
                                                      preferred_element_type=F32)
                dka_ref[g, win, :] += lax.dot_general(ds, q2, (((0,), (0,)), ((), ())),
                                                      preferred_element_type=F32)
                dq2 = jnp.dot(ds, kwin, preferred_element_type=F32)
                dqr = jnp.where(left, dq2[0:QBLK], dq2[QBLK:2 * QBLK]) * 0.125
                dq, dw = _norm_rope_bwd(dqr, xh, r, qw_ref[...], cos, sin, left, first)
                dq_ref[rows, lanes] = dq.astype(BF16)
                dqw = dqw + dw

            @pl.when(n % per_put == per_put - 1)
            def _():
                _put_all(outs, osems, n // per_put)

            return dqw, dsk

        zero = jnp.zeros((1, 128), F32)
        dqw, dsk = lax.fori_loop(0, nblk, blk, (zero, zero))

        ch = 256

        def chunk(i, dkw):
            r0 = pl.multiple_of(i * ch, ch)
            left = _lane((ch, 128)) < 64
            first = (_lane((ch, 128)) % 64) < 32
            rows = pl.ds(r0, ch)
            prow = pl.ds(QBLK + r0, ch)

            def fold(ref):
                a0 = ref[0, prow, :]
                a1 = ref[1, prow, :]
                return jnp.where(left, a0 + pltpu.roll(a0, 64, 1), a1 + pltpu.roll(a1, 64, 1))

            cos = cos_ref[rows, :]
            sin = sin_ref[rows, :]
            _, xh, r = _norm_rope(kv_ref[rows, 0:128], kw_ref[...], cos, sin, left, first)
            dk, dw = _norm_rope_bwd(fold(dka_ref), xh, r, kw_ref[...], cos, sin, left, first)
            dkv_ref[rows, 0:128] = dk.astype(BF16)
            dkv_ref[rows, 128:256] = fold(dva_ref).astype(BF16)
            return dkw + dw

        dkw = lax.fori_loop(0, t // ch, chunk, zero)
        sm_ref[...] = jnp.zeros((8, 128), F32)
        sm_ref[0:1, :] = dqw + pltpu.roll(dqw, 64, 1)
        sm_ref[1:2, :] = dkw + pltpu.roll(dkw, 64, 1)
        sm_ref[2:3, :] = dsk

        j = 2 * x + y
        sends.append(_rs_total(rc, 0, mine, sibo_ref, outo_ref, ino_ref, gwo_ref, OUT_HALF, j, c, sib))
        _rs_done(rc, 0, gwo_ref, c, sib)
        for cp in sends:
            cp.wait_send()
        _put_wait(outs, osems, t // PUT_ROWS)

    vm = pl.BlockSpec(memory_space=pltpu.VMEM)
    hbm = pl.BlockSpec(memory_space=pl.ANY)
    return pl.pallas_call(
        body,
        name="attn_bwd",
        in_specs=[hbm, vm, hbm, hbm, hbm, vm, vm, pl.BlockSpec(memory_space=pltpu.SMEM), vm, vm, vm],
        out_specs=[hbm, vm, hbm, vm, vm],
        out_shape=[jax.ShapeDtypeStruct((t, ATTN_W), BF16), jax.ShapeDtypeStruct((t, 2 * KV_W), BF16),
                   jax.ShapeDtypeStruct((t, ATTN_W), BF16), jax.ShapeDtypeStruct((8, 128), F32),
                   jax.ShapeDtypeStruct((2, OUT_HALF, D_MODEL), F32)],
        scratch_shapes=[pltpu.VMEM((2, t + QBLK, 128), BF16), pltpu.VMEM((2, t + QBLK, 128), BF16),
                        pltpu.VMEM((2, t + QBLK, 128), F32), pltpu.VMEM((2, t + QBLK, 128), F32)]
        + _rs_scratch(OUT_HALF)
        + [pltpu.VMEM((t, ATTN_W), F32), pltpu.VMEM((t, ATTN_W), F32), pltpu.VMEM((t, ATTN_W), BF16),
           pltpu.VMEM((t, ATTN_W), BF16), pltpu.VMEM((t, ATTN_W), BF16), pltpu.VMEM((t, ATTN_W), BF16),
           pltpu.SemaphoreType.DMA((4,)), pltpu.SemaphoreType.DMA((t // PUT_ROWS,)), pltpu.SemaphoreType.DMA((t // PUT_ROWS,)),
           pltpu.SemaphoreType.DMA((RS_SEMS,)), pltpu.SemaphoreType.DMA((RS_SEMS,))],
        compiler_params=_cparams(),
    )(q_raw, kv_raw, ga, o, dmix, qw2, kw2, sinks, cos_f, sin_s, go)


CONV_CH = 256
CONV_SUB = 64
CONV_ACCS = 3


def _shifted_windows(src_ref, r0, sh_ref):
    rows = CONV_CH + CONV_PAD
    win = src_ref[pl.ds(r0, rows), :]
    for b in range(8):
        sh = win if b == 0 else pltpu.roll(win, rows - b, 0)
        for c in range(CONV_W // 128):
            sh_ref[b, c] = sh[:, c * 128:(c + 1) * 128]


def _conv_fwd(ua, ug, gb, cw, cb, lw, lb):
    t = ua.shape[0]

    def body(ua_hbm, ug_hbm, gb_hbm, cw_ref, cb_ref, lw_ref, lb_ref, cz_hbm, mix_hbm, zp_ref, sh_ref,
             ua_ref, ug_ref, gb_ref, cz_ref, mix_ref, isem, osem0, osem1):
        loads = _fetch((ua_hbm, ug_hbm, gb_hbm), (ua_ref, ug_ref, gb_ref), isem)
        outs, osems = ((cz_ref, cz_hbm), (mix_ref, mix_hbm)), (osem0, osem1)
        per_put = PUT_ROWS // CONV_CH
        zp_ref[0:CONV_PAD, :] = jnp.zeros((CONV_PAD, CONV_W), F32)
        loads[0].wait()
        loads[1].wait()

        def glu(i, carry):
            r0 = pl.multiple_of(i * CONV_CH, CONV_CH)
            rows = pl.ds(r0, CONV_CH)
            zp_ref[pl.ds(CONV_PAD + r0, CONV_CH), :] = ua_ref[rows, :] * _sigmoid(ug_ref[rows, :])
            return carry

        lax.fori_loop(0, t // CONV_CH, glu, 0)
        loads[2].wait()

        def chunk(i, carry):
            r0 = pl.multiple_of(i * CONV_CH, CONV_CH)
            _shifted_windows(zp_ref, r0, sh_ref)
            for c in range(CONV_W // 128):
                lanes = slice(c * 128, (c + 1) * 128)

                def sub(k, carry2):
                    b0 = pl.multiple_of(k * CONV_SUB, CONV_SUB)
                    acc = [jnp.broadcast_to(cb_ref[0:1, lanes], (CONV_SUB, 128))] + [None] * (CONV_ACCS - 1)
                    for j in range(CONV_TAPS):
                        off = j + CONV_PAD - (CONV_TAPS - 1)
                        term = sh_ref[off % 8, c, pl.ds(b0 + 8 * (off // 8), CONV_SUB), :] * cw_ref[j:j + 1, lanes]
                        acc[j % CONV_ACCS] = term if acc[j % CONV_ACCS] is None else acc[j % CONV_ACCS] + term
                    cz_ref[pl.ds(r0 + b0, CONV_SUB), lanes] = functools.reduce(lambda a, b: a + b, acc)
                    return carry2

                lax.fori_loop(0, CONV_CH // CONV_SUB, sub, 0)
            rows = pl.ds(r0, CONV_CH)
            cz = cz_ref[rows, :]
            mu = jnp.mean(cz, axis=-1, keepdims=True)
            xc = cz - mu
            rs = lax.rsqrt(jnp.mean(xc * xc, axis=-1, keepdims=True) + EPS)
            ln = xc * rs * lw_ref[...] + lb_ref[...]
            mix_ref[rows, :] = (_silu(ln) * _silu(gb_ref[rows, :])).astype(BF16)

            @pl.when(i % per_put == per_put - 1)
            def _():
                _put_all(outs, osems, i // per_put)

            return carry

        lax.fori_loop(0, t // CONV_CH, chunk, 0)
        _put_wait(outs, osems, t // PUT_ROWS)

    vm = pl.BlockSpec(memory_space=pltpu.VMEM)
    hbm = pl.BlockSpec(memory_space=pl.ANY)
    nput = t // PUT_ROWS
    return pl.pallas_call(
        body,
        name="conv_fwd",
        in_specs=[hbm] * 3 + [vm] * 4,
        out_specs=[hbm, hbm],
        out_shape=[jax.ShapeDtypeStruct((t, CONV_W), F32), jax.ShapeDtypeStruct((t, CONV_W), BF16)],
        scratch_shapes=[pltpu.VMEM((t + CONV_PAD, CONV_W), F32),
                        pltpu.VMEM((8, CONV_W // 128, CONV_CH + CONV_PAD, 128), F32),
                        pltpu.VMEM((t, CONV_W), F32), pltpu.VMEM((t, CONV_W), F32), pltpu.VMEM((t, CONV_W), F32),
                        pltpu.VMEM((t, CONV_W), F32), pltpu.VMEM((t, CONV_W), BF16),
                        pltpu.SemaphoreType.DMA((3,)), pltpu.SemaphoreType.DMA((nput,)), pltpu.SemaphoreType.DMA((nput,))],
        compiler_params=_cparams(),
    )(ua, ug, gb, cw, cb, lw, lb)


def _conv_bwd(ua, ug, gb, cz, dmix, cw, lw, lb):
    t = ua.shape[0]

    def body(ua_hbm, ug_hbm, gb_hbm, cz_hbm, dm_hbm, cw_ref, lw_ref, lb_ref,
             dua_hbm, dug_hbm, dgb_hbm, dcw_ref, dvec_ref, zp_ref, dp_ref, sh_ref, wacc_ref,
             ua_ref, ug_ref, gb_ref, cz_ref, dm_ref, dua_ref, dug_ref, dgb_ref, isem, osem0, osem1, osem2):
        loads = _fetch((ua_hbm, ug_hbm, gb_hbm, cz_hbm, dm_hbm), (ua_ref, ug_ref, gb_ref, cz_ref, dm_ref), isem)
        per_put = PUT_ROWS // CONV_CH
        zp_ref[0:CONV_PAD, :] = jnp.zeros((CONV_PAD, CONV_W), F32)
        dp_ref[t:t + CONV_PAD, :] = jnp.zeros((CONV_PAD, CONV_W), F32)
        wacc_ref[...] = jnp.zeros_like(wacc_ref)
        for cp in loads:
            cp.wait()

        def pointwise(i, carry):
            dcb, dlw, dlb = carry
            r0 = pl.multiple_of(i * CONV_CH, CONV_CH)
            rows = pl.ds(r0, CONV_CH)
            zp_ref[pl.ds(CONV_PAD + r0, CONV_CH), :] = ua_ref[rows, :] * _sigmoid(ug_ref[rows, :])
            cz = cz_ref[rows, :]
            mu = jnp.mean(cz, axis=-1, keepdims=True)
            xc = cz - mu
            rs = lax.rsqrt(jnp.mean(xc * xc, axis=-1, keepdims=True) + EPS)
            xh = xc * rs
            ln = xh * lw_ref[...] + lb_ref[...]
            gbv = gb_ref[rows, :]
            dy = dm_ref[rows, :].astype(F32)
            dgb_ref[rows, :] = (dy * _silu(ln) * _dsilu(gbv)).astype(BF16)
            dl = dy * _silu(gbv) * _dsilu(ln)
            dxh = dl * lw_ref[...]
            dcz = rs * (dxh - jnp.mean(dxh, axis=-1, keepdims=True)
                        - xh * jnp.mean(dxh * xh, axis=-1, keepdims=True))
            dp_ref[rows, :] = dcz

            @pl.when(i % per_put == per_put - 1)
            def _():
                _put(dgb_ref, dgb_hbm, osem2, i // per_put).start()

            return (dcb + jnp.sum(dcz, axis=0, keepdims=True),
                    dlw + jnp.sum(dl * xh, axis=0, keepdims=True),
                    dlb + jnp.sum(dl, axis=0, keepdims=True))

        zero = jnp.zeros((1, CONV_W), F32)
        dcb, dlw, dlb = lax.fori_loop(0, t // CONV_CH, pointwise, (zero, zero, zero))
        dvec_ref[...] = jnp.zeros((8, CONV_W), F32)
        dvec_ref[0:1, :] = dcb
        dvec_ref[1:2, :] = dlw
        dvec_ref[2:3, :] = dlb

        def chunk(i, carry):
            r0 = pl.multiple_of(i * CONV_CH, CONV_CH)
            _shifted_windows(dp_ref, r0, sh_ref)
            for c in range(CONV_W // 128):
                lanes = slice(c * 128, (c + 1) * 128)

                def sub(k, carry2):
                    b0 = pl.multiple_of(k * CONV_SUB, CONV_SUB)
                    acc = [None] * CONV_ACCS
                    for j in range(CONV_TAPS):
                        off = CONV_TAPS - 1 - j
                        term = sh_ref[off % 8, c, pl.ds(b0 + 8 * (off // 8), CONV_SUB), :] * cw_ref[j:j + 1, lanes]
                        acc[j % CONV_ACCS] = term if acc[j % CONV_ACCS] is None else acc[j % CONV_ACCS] + term
                    acc = functools.reduce(lambda a, b: a + b, acc)
                    rr = pl.ds(r0 + b0, CONV_SUB)
                    sg = _sigmoid(ug_ref[rr, lanes])
                    dua_ref[rr, lanes] = (acc * sg).astype(BF16)
                    dug_ref[rr, lanes] = (acc * ua_ref[rr, lanes] * sg * (1.0 - sg)).astype(BF16)
                    return carry2

                lax.fori_loop(0, CONV_CH // CONV_SUB, sub, 0)
            _shifted_windows(zp_ref, r0, sh_ref)
            for c in range(CONV_W // 128):
                lanes = slice(c * 128, (c + 1) * 128)

                def subw(k, carry2):
                    b0 = pl.multiple_of(k * CONV_SUB, CONV_SUB)
                    dcz = dp_ref[pl.ds(r0 + b0, CONV_SUB), lanes]
                    for j in range(CONV_TAPS):
                        off = j + CONV_PAD - (CONV_TAPS - 1)
                        pr = dcz * sh_ref[off % 8, c, pl.ds(b0 + 8 * (off // 8), CONV_SUB), :]
                        parts = [pr[8 * q:8 * (q + 1)] for q in range(CONV_SUB // 8)]
                        while len(parts) > 1:
                            parts = [a + b for a, b in zip(parts[0::2], parts[1::2])]
                        wacc_ref[8 * j:8 * (j + 1), lanes] += parts[0]
                    return carry2

                lax.fori_loop(0, CONV_CH // CONV_SUB, subw, 0)

            @pl.when(i % per_put == per_put - 1)
            def _():
                _put_all(((dua_ref, dua_hbm), (dug_ref, dug_hbm)), (osem0, osem1), i // per_put)

            return carry

        lax.fori_loop(0, t // CONV_CH, chunk, 0)
        _put_wait(((dua_ref, dua_hbm), (dug_ref, dug_hbm), (dgb_ref, dgb_hbm)), (osem0, osem1, osem2), t // PUT_ROWS)
        dcw_ref[...] = jnp.zeros((32, CONV_W), F32)
        for j in range(CONV_TAPS):
            dcw_ref[j:j + 1, :] = jnp.sum(wacc_ref[8 * j:8 * (j + 1), :], axis=0, keepdims=True)

    vm = pl.BlockSpec(memory_space=pltpu.VMEM)
    hbm = pl.BlockSpec(memory_space=pl.ANY)
    return pl.pallas_call(
        body,
        name="conv_bwd",
        in_specs=[hbm] * 5 + [vm] * 3,
        out_specs=[hbm] * 3 + [vm] * 2,
        out_shape=[jax.ShapeDtypeStruct((t, CONV_W), BF16)] * 3
        + [jax.ShapeDtypeStruct((32, CONV_W), F32), jax.ShapeDtypeStruct((8, CONV_W), F32)],
        scratch_shapes=[pltpu.VMEM((t + CONV_PAD, CONV_W), F32), pltpu.VMEM((t + CONV_PAD, CONV_W), F32),
                        pltpu.VMEM((8, CONV_W // 128, CONV_CH + CONV_PAD, 128), F32), pltpu.VMEM((8 * 32, CONV_W), F32)]
        + [pltpu.VMEM((t, CONV_W), F32)] * 4 + [pltpu.VMEM((t, CONV_W), BF16)] * 4
        + [pltpu.SemaphoreType.DMA((5,))] + [pltpu.SemaphoreType.DMA((t // PUT_ROWS,))] * 3,
        compiler_params=_cparams(),
    )(ua, ug, gb, cz, dmix, cw, lw, lb)


def _out_proj(mix_a, mix_b, x, tgt, gate, w_out):
    t = x.shape[0]
    tm = 512
    nstep = t // tm

    def body(ma_ref, mb_ref, x_ref, t_ref, g_ref, w_ref, dout_ref, dma_ref, dmb_ref, gw_ref, red_ref, acc_ref):
        i = pl.program_id(0)

        @pl.when(i == 0)
        def _():
            acc_ref[...] = jnp.zeros_like(acc_ref)
            red_ref[...] = jnp.zeros_like(red_ref)

        mix = jnp.concatenate([ma_ref[...], mb_ref[...]], axis=1)
        y = jnp.dot(mix, w_ref[...], preferred_element_type=F32)
        gate_v = g_ref[...]
        err = x_ref[...] + gate_v * y - t_ref[...]
        dout = err * (1.0 / D_MODEL)
        dout_ref[...] = dout
        red_ref[0:1, :] += jnp.sum(dout * y, axis=0, keepdims=True)
        red_ref[1:2, :] += jnp.sum(err * err, axis=0, keepdims=True)
        dy = (dout * gate_v).astype(BF16)
        dmix = lax.dot_general(dy, w_ref[...], (((1,), (1,)), ((), ())), preferred_element_type=F32)
        dma_ref[...] = dmix[:, 0:512].astype(BF16)
        dmb_ref[...] = dmix[:, 512:1024].astype(BF16)
        acc_ref[...] += lax.dot_general(mix, dy, (((0,), (0,)), ((), ())), preferred_element_type=F32)

        @pl.when(i == nstep - 1)
        def _():
            gw_ref[...] = acc_ref[...].astype(BF16)

    row = lambda w: pl.BlockSpec((tm, w), lambda i: (i, 0))
    const = lambda s: pl.BlockSpec(s, lambda i: (0, 0))
    return pl.pallas_call(
        body,
        name="out_proj",
        grid=(nstep,),
        in_specs=[row(512), row(512), row(D_MODEL), row(D_MODEL), const((1, D_MODEL)),
                  pl.BlockSpec((D_MODEL, D_MODEL), lambda i: (0, 0), pipeline_mode=pl.Buffered(1))],
        out_specs=[row(D_MODEL), row(512), row(512), const((D_MODEL, D_MODEL)), const((8, D_MODEL))],
        out_shape=[jax.ShapeDtypeStruct((t, D_MODEL), F32), jax.ShapeDtypeStruct((t, 512), BF16),
                   jax.ShapeDtypeStruct((t, 512), BF16), jax.ShapeDtypeStruct((D_MODEL, D_MODEL), BF16),
                   jax.ShapeDtypeStruct((8, D_MODEL), F32)],
        scratch_shapes=[pltpu.VMEM((D_MODEL, D_MODEL), F32)],
        compiler_params=_cparams(dimension_semantics=("arbitrary",)),
    )(mix_a, mix_b, x, tgt, gate, w_out)


DPROJ_WIDTHS = (512, 256, 512, 512, 512, 512)
DPROJ_STARTS = (0, 512, 768, 1280, 1792, 2304)
WIN_W = 768
WIN_START = (0, 640, 1408, 2048)
WIN_OFF = (0, 64, 0, 64)
N_GW = N_CHIPS


def _window_pieces(s):
    lo, hi = WIN_START[s], WIN_START[s] + WIN_W
    out = []
    for p, (st, w) in enumerate(zip(DPROJ_STARTS, DPROJ_WIDTHS)):
        a, b = max(lo, st), min(hi, st + w)
        if a < b:
            out.append((p, a - st, b - a, a - lo))
    return out


def _in_proj_bwd(dparts, h, x, dout, s1, nw, wt_full, small0, row0):
    t = x.shape[0]
    tm = 256
    nstep = N_GW + t // tm
    n_sem = 20
    rows0 = small0.shape[0]
    hs = rows0 // 2
    npart = len(DPROJ_WIDTHS)

    def body(*refs):
        d_hbm, d_ref = refs[:npart], refs[npart:2 * npart]
        (x_ref, dout_ref, s1_ref, nw_ref, h_ref, wt_hbm, sm0_ref, row0_ref,
         gx_ref, gw_hbm, ssum_ref, rows_ref,
         stg_ref, wt_ref, gt_ref, sib_ref, out_ref, in_ref, res_ref, sall_ref, red_ref, ssib_ref, schip_ref, sres_ref,
         wsem, lsem, ssem, rsem) = refs[2 * npart:]
        i = pl.program_id(0)
        x_, y_, c, chips = _place()
        j = 2 * x_ + y_
        dev = 2 * j + c
        sib = (x_, y_, 1 - c)
        rc = functools.partial(_remote, ssem, rsem)
        rel_chip = [2 * cx + cy for cx, cy in chips] + [j]
        peers = [(px, py, pc) for px in (x_, 1 - x_) for py in (y_, 1 - y_) for pc in (c, 1 - c)][1:]
        wt_copy = pltpu.make_async_copy(wt_hbm, wt_ref, lsem.at[0])

        def window(case, slot):
            return [pltpu.make_async_copy(d_hbm[p].at[:, pl.ds(c0, w)], stg_ref.at[slot, :, pl.ds(w0, w)], wsem.at[slot, n])
                    for n, (p, c0, w, w0) in enumerate(_window_pieces(case))]

        def to_sibling(k):
            return rc(k, gt_ref.at[k, 1 - c], sib_ref.at[k], sib)

        def to_chip(k):
            return rc(4 + k, out_ref.at[k], in_ref.at[k], (*chips[k], c))

        def trade(k):
            to_sibling(k).wait_recv()

            def add(n, carry):
                rr = pl.ds(pl.multiple_of(n * RS_CH, RS_CH), RS_CH)
                out_ref[k, rr, :] = (gt_ref[k, c, rr, :].astype(F32) + sib_ref[k, rr, :].astype(F32)).astype(BF16)
                return carry

            lax.fori_loop(0, IN_HALF // RS_CH, add, 0)
            to_chip(k).start()

        mine_s = pl.ds(pl.multiple_of(c * hs, 8), hs)
        other_s = pl.ds(pl.multiple_of((1 - c) * hs, 8), hs)

        def small_to_sibling():
            return rc(15, sm0_ref.at[other_s], ssib_ref, sib)

        def small_to_chip(k):
            return rc(16 + k, schip_ref.at[j], schip_ref.at[j], (*chips[k], c))

        def small_share():
            return rc(19, sres_ref.at[c], sres_ref.at[c], sib)

        for k in range(N_GW):
            @pl.when(i == k)
            def _(k=k):
                slot = k % 2
                if k == 0:
                    red_ref[...] = jnp.zeros_like(red_ref)
                    wt_copy.start()
                    small_to_sibling().start()
                if k == 1:
                    small_to_sibling().wait_recv()
                    schip_ref[j] = sm0_ref[mine_s, :] + ssib_ref[...]
                    for kk in range(3):
                        small_to_chip(kk).start()
                if k == N_GW - 1:
                    for kk in range(3):
                        jk = rel_chip[kk]
                        rc(16 + kk, schip_ref.at[jk], schip_ref.at[jk], sib).wait_recv()
                    tot = schip_ref[0]
                    for d in range(1, N_CHIPS):
                        tot = tot + schip_ref[d]
                    sres_ref[c] = tot
                    small_share().start()
                for case in range(N_CHIPS):
                    if k == 0:
                        @pl.when(rel_chip[0] == case)
                        def _():
                            for cp in window(case, 0):
                                cp.start()
                    if k + 1 < N_GW:
                        @pl.when(rel_chip[k + 1] == case)
                        def _():
                            for cp in window(case, 1 - slot):
                                cp.start()
                for case in range(N_CHIPS):
                    @pl.when(rel_chip[k] == case)
                    def _():
                        for cp in window(case, slot):
                            cp.wait()
                g = lax.dot_general(stg_ref[slot], h_ref[...], (((0,), (0,)), ((), ())), preferred_element_type=F32)
                for off in sorted(set(WIN_OFF)):
                    @pl.when(rel_chip[k] % 2 == (1 if off else 0))
                    def _():
                        gt_ref[k, 0] = g[off:off + IN_HALF].astype(BF16)
                        gt_ref[k, 1] = g[off + IN_HALF:off + 2 * IN_HALF].astype(BF16)
                to_sibling(k).start()
                if k >= 1:
                    trade(k - 1)

        @pl.when(i == N_GW)
        def _():
            wt_copy.wait()

        @pl.when(i >= N_GW)
        def _():
            xv = x_ref[...]
            r = lax.rsqrt(jnp.mean(xv * xv, axis=-1, keepdims=True) + EPS)
            xh = xv * r
            n = xh * nw_ref[...]
            dproj = jnp.concatenate([ref[...] for ref in d_ref], axis=1)
            dh = jnp.dot(dproj, wt_ref[...], preferred_element_type=F32)
            red_ref[0:1, :] += jnp.sum(dh, axis=0, keepdims=True)
            red_ref[1:2, :] += jnp.sum(dh * n, axis=0, keepdims=True)
            dn = dh * s1_ref[...]
            red_ref[2:3, :] += jnp.sum(dn * xh, axis=0, keepdims=True)
            dxh = dn * nw_ref[...]
            gx_ref[...] = dout_ref[...] + r * (dxh - xh * jnp.mean(dxh * xh, axis=-1, keepdims=True))

        @pl.when(i == nstep - 1)
        def _():
            sall_ref[dev] = row0_ref[...]
            sall_ref[dev, 2:5, :] = red_ref[0:3, :]
            sends = [rc(8 + k, sall_ref.at[dev], sall_ref.at[dev], peer) for k, peer in enumerate(peers)]
            for cp in sends:
                cp.start()
            sends += [to_sibling(k) for k in range(N_GW)] + [to_chip(k) for k in range(3)]
            sends += [small_to_sibling(), small_share()] + [small_to_chip(k) for k in range(3)]
            own = N_GW - 1
            to_sibling(own).wait_recv()
            for k in range(3):
                to_chip(k).wait_recv()

            def total(n, carry):
                rr = pl.ds(pl.multiple_of(n * RS_CH, RS_CH), RS_CH)
                acc = gt_ref[own, c, rr, :].astype(F32) + sib_ref[own, rr, :].astype(F32)
                for k in range(3):
                    acc = acc + in_ref[k, rr, :].astype(F32)
                res_ref[c, rr, :] = acc
                return carry

            lax.fori_loop(0, IN_HALF // RS_CH, total, 0)
            share = rc(7, res_ref.at[c], res_ref.at[c], sib)
            share.start()
            sends.append(share)
            for k, (px, py, pc) in enumerate(peers):
                pdev = 4 * px + 2 * py + pc
                rc(8 + k, sall_ref.at[pdev], sall_ref.at[pdev], (px, py, pc)).wait_recv()
            rows_ref[...] = sall_ref[...]
            rc(19, sres_ref.at[1 - c], sres_ref.at[1 - c], sib).wait_recv()
            ssum_ref[0:hs, :] = sres_ref[0]
            ssum_ref[hs:rows0, :] = sres_ref[1]
            rc(7, res_ref.at[1 - c], res_ref.at[1 - c], sib).wait_recv()
            back = pltpu.make_async_copy(res_ref, gw_hbm, lsem.at[1])
            back.start()
            for cp in sends:
                cp.wait_send()
            back.wait()

    blk = lambda i: jnp.maximum(i - N_GW, 0)
    row = lambda w: pl.BlockSpec((tm, w), lambda i: (blk(i), 0))
    vec = pl.BlockSpec((1, D_MODEL), lambda i: (0, 0))
    const = lambda shape: pl.BlockSpec(shape, lambda i: (0,) * len(shape))
    hbm = pl.BlockSpec(memory_space=pl.ANY)
    return pl.pallas_call(
        body,
        name="in_proj_bwd",
        grid=(nstep,),
        in_specs=[hbm] * npart + [row(w) for w in DPROJ_WIDTHS] + [row(D_MODEL), row(D_MODEL), vec, vec,
                  pl.BlockSpec((t, D_MODEL), lambda i: (0, 0), pipeline_mode=pl.Buffered(1)), hbm, const((rows0, D_MODEL)),
                  const((8, D_MODEL))],
        out_specs=[row(D_MODEL), hbm, const((rows0, D_MODEL)), const((N_DEV, 8, D_MODEL))],
        out_shape=[jax.ShapeDtypeStruct((t, D_MODEL), F32), jax.ShapeDtypeStruct((2, IN_HALF, D_MODEL), F32),
                   jax.ShapeDtypeStruct((rows0, D_MODEL), F32), jax.ShapeDtypeStruct((N_DEV, 8, D_MODEL), F32)],
        scratch_shapes=[pltpu.VMEM((2, t, WIN_W), BF16), pltpu.VMEM((IN_W, D_MODEL), BF16),
                        pltpu.VMEM((N_CHIPS, 2, IN_HALF, D_MODEL), BF16), pltpu.VMEM((N_CHIPS, IN_HALF, D_MODEL), BF16),
                        pltpu.VMEM((3, IN_HALF, D_MODEL), BF16), pltpu.VMEM((3, IN_HALF, D_MODEL), BF16),
                        pltpu.VMEM((2, IN_HALF, D_MODEL), F32), pltpu.VMEM((N_DEV, 8, D_MODEL), F32),
                        pltpu.VMEM((8, D_MODEL), F32), pltpu.VMEM((hs, D_MODEL), F32), pltpu.VMEM((N_CHIPS, hs, D_MODEL), F32),
                        pltpu.VMEM((2, hs, D_MODEL), F32), pltpu.SemaphoreType.DMA((2, 3)), pltpu.SemaphoreType.DMA((2,)),
                        pltpu.SemaphoreType.DMA((n_sem,)), pltpu.SemaphoreType.DMA((n_sem,))],
        compiler_params=_cparams(dimension_semantics=("arbitrary",)),
    )(*dparts, *dparts, x, dout, s1, nw, h, wt_full, small0, row0)


MESH = pl.DeviceIdType.MESH


def _place():
    x, y, c = lax.axis_index("x"), lax.axis_index("y"), lax.axis_index("c")
    chips = [(1 - x, y), (x, 1 - y), (1 - x, 1 - y)]
    return x, y, c, chips


def _remote(sems_s, sems_r, k, src, dst, to):
    return pltpu.make_async_remote_copy(src_ref=src, dst_ref=dst, send_sem=sems_s.at[k], recv_sem=sems_r.at[k],
                                        device_id=to, device_id_type=MESH)


RS_CH = 32
RS_SEMS = 5


def _rs_to_sibling(rc, s0, theirs, sib_ref, sib):
    cp = rc(s0, theirs, sib_ref, sib)
    cp.start()
    return cp


def _rs_trade(rc, s0, theirs, mine, sib_ref, out_ref, in_ref, rows, c, sib, chips):
    rc(s0, theirs, sib_ref, sib).wait_recv()
    cps = []
    for k, (cx, cy) in enumerate(chips):
        jk = 2 * cx + cy

        def add(i, carry, jk=jk, k=k):
            rr = pl.ds(pl.multiple_of(i * RS_CH, RS_CH), RS_CH)
            out_ref[k, rr, :] = (mine[jk, rr, :].astype(F32) + sib_ref[jk, rr, :].astype(F32)).astype(BF16)
            return carry

        lax.fori_loop(0, rows // RS_CH, add, 0)
        cps.append(rc(s0 + 1 + k, out_ref.at[k], in_ref.at[k], (cx, cy, c)))
        cps[-1].start()
    return cps


def _rs_total(rc, s0, mine, sib_ref, out_ref, in_ref, res_ref, rows, j, c, sib):
    for k in range(3):
        rc(s0 + 1 + k, out_ref.at[k], in_ref.at[k], sib).wait_recv()

    def total(i, carry):
        rr = pl.ds(pl.multiple_of(i * RS_CH, RS_CH), RS_CH)
        acc = mine[j, rr, :].astype(F32) + sib_ref[j, rr, :].astype(F32)
        for k in range(3):
            acc = acc + in_ref[k, rr, :].astype(F32)
        res_ref[c, rr, :] = acc
        return carry

    lax.fori_loop(0, rows // RS_CH, total, 0)
    cp = rc(s0 + 4, res_ref.at[c], res_ref.at[c], sib)
    cp.start()
    return cp


def _rs_done(rc, s0, res_ref, c, sib):
    rc(s0 + 4, res_ref.at[1 - c], res_ref.at[1 - c], sib).wait_recv()


def _rs_scratch(rows):
    return [pltpu.VMEM((N_CHIPS, rows, D_MODEL), BF16), pltpu.VMEM((3, rows, D_MODEL), BF16),
            pltpu.VMEM((3, rows, D_MODEL), BF16)]


def _gather_weights(wt, c_row, w_ada, b_sh):
    n_sem = 16

    def body(wt_ref, c_ref, wada_ref, bsh_ref, w4_ref, call_ref, ada_ref, part_ref, ssem, rsem):
        x, y, c, chips = _place()
        j = 2 * x + y
        dev = 2 * j + c
        sib = (x, y, 1 - c)
        idx = [2 * cx + cy for cx, cy in chips]
        rc = functools.partial(_remote, ssem, rsem)

        w4_ref[j] = wt_ref[...].astype(BF16)
        call_ref[dev] = c_ref[...]

        sends = []
        peers = [(px, py, pc) for px in (x, 1 - x) for py in (y, 1 - y) for pc in (c, 1 - c)][1:]
        for k, peer in enumerate(peers):
            sends.append(rc(k, call_ref.at[dev], call_ref.at[dev], peer))
        for k, chip in enumerate(chips):
            sends.append(rc(7 + k, w4_ref.at[j, c], w4_ref.at[j, c], (*chip, c)))
        for cp in sends:
            cp.start()

        for k, (px, py, pc) in enumerate(peers):
            pdev = 4 * px + 2 * py + pc
            rc(k, call_ref.at[pdev], call_ref.at[pdev], (px, py, pc)).wait_recv()
        rowid = lax.broadcasted_iota(jnp.int32, (N_DEV, D_MODEL), 0)
        call = jnp.zeros((N_DEV, D_MODEL), F32)
        for r in range(N_DEV):
            call = jnp.where(rowid == r, jnp.broadcast_to(call_ref[r], (N_DEV, D_MODEL)), call)
        part = jnp.dot(_silu(call).astype(BF16), wada_ref[...].astype(BF16), preferred_element_type=F32) + bsh_ref[...]
        for r in range(N_DEV):
            part_ref[r] = part[r:r + 1, :]
        ada_ref[j] = part_ref[dev]
        rows_out = []
        for k, chip in enumerate(chips):
            rows_out.append(rc(13 + k, part_ref.at[2 * idx[k] + c], ada_ref.at[j], (*chip, c)))
            rows_out[-1].start()

        passed = []
        for k, chip in enumerate(chips):
            jk = idx[k]
            rc(7 + k, w4_ref.at[jk, c], w4_ref.at[jk, c], sib).wait_recv()
            passed.append(rc(10 + k, w4_ref.at[jk, c], w4_ref.at[jk, c], sib))
            passed[-1].start()
        for k, chip in enumerate(chips):
            jk = idx[k]
            rc(10 + k, w4_ref.at[jk, 1 - c], w4_ref.at[jk, 1 - c], sib).wait_recv()
            rc(13 + k, ada_ref.at[jk], ada_ref.at[jk], sib).wait_recv()
        for cp in sends + rows_out + passed:
            cp.wait_send()

    vm = pl.BlockSpec(memory_space=pltpu.VMEM)
    return pl.pallas_call(
        body,
        name="gather_weights",
        in_specs=[vm] * 4,
        out_specs=[vm] * 3,
        out_shape=[jax.ShapeDtypeStruct((N_CHIPS, 2, IN_HALF, D_MODEL), BF16),
                   jax.ShapeDtypeStruct((N_DEV, 1, D_MODEL), F32),
                   jax.ShapeDtypeStruct((N_CHIPS, 1, ADA_SHARD), F32)],
        scratch_shapes=[pltpu.VMEM((N_DEV, 1, ADA_SHARD), F32),
                        pltpu.SemaphoreType.DMA((n_sem,)), pltpu.SemaphoreType.DMA((n_sem,))],
        compiler_params=_cparams(),
    )(wt, c_row, w_ada, b_sh)


def _adamw_math(w, g, m, v):
    m2 = ADAM_B1 * m + (1.0 - ADAM_B1) * g
    v2 = ADAM_B2 * v + (1.0 - ADAM_B2) * (g * g)
    m_hat = m2 / (1.0 - ADAM_B1 ** ADAM_STEP)
    v_hat = v2 / (1.0 - ADAM_B2 ** ADAM_STEP)
    delta = -ADAM_LR * (m_hat / (jnp.sqrt(v_hat) + ADAM_EPS) + ADAM_WD * w)
    return delta, m2, v2


def _adamw(name, w, g, m, v, tm):
    r, cdim = w.shape

    def body(w_ref, g_ref, m_ref, v_ref, d_ref, m2_ref, v2_ref):
        d_ref[...], m2_ref[...], v2_ref[...] = _adamw_math(w_ref[...], g_ref[...], m_ref[...], v_ref[...])

    blk = pl.BlockSpec((tm, cdim), lambda i: (i, 0))
    return pl.pallas_call(
        body,
        name=name,
        grid=(r // tm,),
        in_specs=[blk] * 4,
        out_specs=[blk] * 3,
        out_shape=[jax.ShapeDtypeStruct((r, cdim), F32)] * 3,
        compiler_params=_cparams(dimension_semantics=("arbitrary",)),
    )(w, g, m, v)


def _adamw_ada(w, m, v, cact_t, dcols):
    r, cdim = w.shape
    tm = 256

    def body(w_ref, m_ref, v_ref, ct_ref, dc_ref, g_ref, d_ref, m2_ref, v2_ref):
        g = jnp.dot(ct_ref[...], dc_ref[...], preferred_element_type=F32, precision=lax.Precision.HIGHEST)
        g_ref[...] = g
        d_ref[...], m2_ref[...], v2_ref[...] = _adamw_math(w_ref[...], g, m_ref[...], v_ref[...])

    blk = pl.BlockSpec((tm, cdim), lambda i: (i, 0))
    return pl.pallas_call(
        body,
        name="adamw_w_ada",
        grid=(r // tm,),
        in_specs=[blk] * 3 + [pl.BlockSpec((tm, N_DEV), lambda i: (i, 0)), pl.BlockSpec((N_DEV, cdim), lambda i: (0, 0))],
        out_specs=[blk] * 4,
        out_shape=[jax.ShapeDtypeStruct((r, cdim), F32)] * 4,
        compiler_params=_cparams(dimension_semantics=("arbitrary",)),
    )(w, m, v, cact_t, dcols)


def _adamw_small(ws, gs, ms, vs):
    n = len(ws)

    def body(*refs):
        w_r, g_r, m_r, v_r = refs[0:n], refs[n:2 * n], refs[2 * n:3 * n], refs[3 * n:4 * n]
        d_r, m2_r, v2_r = refs[4 * n:5 * n], refs[5 * n:6 * n], refs[6 * n:7 * n]
        for i in range(n):
            d_r[i][...], m2_r[i][...], v2_r[i][...] = _adamw_math(w_r[i][...], g_r[i][...], m_r[i][...], v_r[i][...])

    vm = pl.BlockSpec(memory_space=pltpu.VMEM)
    shapes = [jax.ShapeDtypeStruct(w.shape, F32) for w in ws]
    out = pl.pallas_call(
        body,
        name="adamw_small",
        in_specs=[vm] * (4 * n),
        out_specs=[vm] * (3 * n),
        out_shape=shapes * 3,
        compiler_params=_cparams(),
    )(*ws, *gs, *ms, *vs)
    return out[0:n], out[n:2 * n], out[2 * n:3 * n]


def _rope_tables(t):
    inv = ROPE_THETA ** (-jnp.arange(0, HEAD_DIM, 2, dtype=F32) / HEAD_DIM)
    ang = jnp.arange(t, dtype=F32)[:, None] * inv[None, :]
    cos, sin = jnp.cos(ang), jnp.sin(ang)
    return jnp.tile(cos, (1, 4)), jnp.tile(jnp.concatenate([-sin, sin], axis=1), (1, 2))


def _pad_lanes(v, width):
    return jnp.pad(v, ((0, 0), (0, width - v.shape[1])))


def kernel(x, c, w_ada, b_ada, norm_w, w_in, q_norm_w, k_norm_w, sinks, conv_w, conv_b, ln_w, ln_b, w_out, loss_target, m_w_ada, m_b_ada, m_norm_w, m_w_in, m_q_norm_w, m_k_norm_w, m_sinks, m_conv_w, m_conv_b, m_ln_w, m_ln_b, m_w_out, v_w_ada, v_b_ada, v_norm_w, v_w_in, v_q_norm_w, v_k_norm_w, v_sinks, v_conv_w, v_conv_b, v_ln_w, v_ln_b, v_w_out):
    xi, yi = lax.axis_index("x"), lax.axis_index("y")
    j = 2 * xi + yi
    x2, tgt = x[0], loss_target[0]
    t = x2.shape[0]

    wt_s, mt_s, vt_s = w_in[0].T, m_w_in[0].T, v_w_in[0].T
    cw_pad = jnp.pad(conv_w[0], ((0, 1), (0, 0)))
    b_sh = lax.dynamic_slice(b_ada, (0, ADA_SHARD * j), (1, ADA_SHARD))

    w4, call, ada4 = _gather_weights(wt_s.reshape(2, IN_HALF, D_MODEL), c, w_ada[0], b_sh)
    w_full = w4.reshape(IN_W, D_MODEL)
    ada = ada4.reshape(1, 3 * D_MODEL)
    shift, s1, gate = ada[:, :D_MODEL], 1.0 + ada[:, D_MODEL:2 * D_MODEL], ada[:, 2 * D_MODEL:]

    cos_f, sin_s = _rope_tables(t)
    qw2, kw2 = jnp.tile(q_norm_w, (1, 2)), jnp.tile(k_norm_w, (1, 2))

    q_raw, kv_raw, ga, ua, ug, gb, h = _in_proj(x2, s1, shift, norm_w, w_full)
    o, mix_a, wo4, cw4 = _attn_fwd(q_raw, kv_raw, ga, qw2, kw2, sinks, cos_f, sin_s,
                                   w_out[0].reshape(2, OUT_HALF, D_MODEL), cw_pad)
    w_out_full = wo4.reshape(D_MODEL, D_MODEL)
    cw_full = jnp.concatenate([cw4[i] for i in range(N_CHIPS)], axis=1)
    cz, mix_b = _conv_fwd(ua, ug, gb, cw_full, conv_b, ln_w, ln_b)
    dout, dmix_a, dmix_b, gwo_bf, red_o = _out_proj(mix_a, mix_b, x2, tgt, gate, w_out_full)

    dq, dkv, dga, sm_a, gwo = _attn_bwd(q_raw, kv_raw, ga, o, dmix_a, qw2, kw2, sinks, cos_f, sin_s,
                                        gwo_bf.reshape(N_CHIPS, 2, OUT_HALF, D_MODEL))
    dua, dug, dgb, dcw, dvec = _conv_bwd(ua, ug, gb, cz, dmix_b, cw_full, ln_w, ln_b)
    dparts = (dq, dkv, dga, dua, dug, dgb)

    misc = jnp.concatenate([dvec[2:3], sm_a[0:1], sm_a[1:2], sm_a[2:3], jnp.zeros((1, 128), F32)], axis=1)
    small0 = jnp.concatenate([
        dcw.reshape(16, D_MODEL),
        jnp.concatenate([dvec[0:2].reshape(1, D_MODEL), misc, red_o[1:2], jnp.zeros((13, D_MODEL), F32)], axis=0)], axis=0)
    grad_x, gw, ssum, rows = _in_proj_bwd(dparts, h, x2, dout, s1, norm_w, w_full, small0, red_o)

    loss = (0.5 / D_MODEL) * jnp.sum(ssum[18])
    gt_w_in = gw.reshape(2 * IN_HALF, D_MODEL)
    g_w_out = gwo.reshape(D_MODEL // N_CHIPS, D_MODEL)
    g_conv_w = lax.dynamic_slice(ssum[0:16].reshape(32, CONV_W), (0, 128 * j), (CONV_TAPS, 128))
    g_conv_b, g_ln_w, g_ln_b = ssum[16:17, 0:CONV_W], ssum[16:17, CONV_W:], ssum[17:18, 0:CONV_W]
    g_qw, g_kw, g_sinks = ssum[17:18, 512:512 + HEAD_DIM], ssum[17:18, 640:640 + HEAD_DIM], ssum[17:18, 768:776]
    rsum = rows[0]
    for d in range(1, N_DEV):
        rsum = rsum + rows[d]
    g_norm_w = rsum[4:5]
    g_b_ada = jnp.concatenate([rsum[2:3], rsum[3:4], rsum[0:1]], axis=1)
    d_ada_all = jnp.concatenate([rows[:, 2], rows[:, 3], rows[:, 0]], axis=1)
    dcols = lax.dynamic_slice(d_ada_all, (0, ADA_SHARD * j), (N_DEV, ADA_SHARD))
    cact_t = jax.nn.silu(call.reshape(N_DEV, D_MODEL)).T

    g_w_ada, d_w_ada, nm_w_ada, nv_w_ada = _adamw_ada(w_ada[0], m_w_ada[0], v_w_ada[0], cact_t, dcols)
    dt_w_in, nmt_w_in, nvt_w_in = _adamw("adamw_w_in", wt_s, gt_w_in, mt_s, vt_s, 176)
    g_w_in, d_w_in, nm_w_in, nv_w_in = gt_w_in.T, dt_w_in.T, nmt_w_in.T, nvt_w_in.T
    d_w_out, nm_w_out, nv_w_out = _adamw("adamw_w_out", w_out[0], g_w_out, m_w_out[0], v_w_out[0], 128)
    ws = [b_ada, norm_w, q_norm_w, k_norm_w, sinks, conv_w[0], conv_b, ln_w, ln_b]
    gs = [g_b_ada, g_norm_w, g_qw, g_kw, g_sinks, g_conv_w, g_conv_b, g_ln_w, g_ln_b]
    ms = [m_b_ada, m_norm_w, m_q_norm_w, m_k_norm_w, m_sinks, m_conv_w[0], m_conv_b, m_ln_w, m_ln_b]
    vs = [v_b_ada, v_norm_w, v_q_norm_w, v_k_norm_w, v_sinks, v_conv_w[0], v_conv_b, v_ln_w, v_ln_b]
    ds, nms, nvs = _adamw_small(ws, gs, ms, vs)

    def order(ada_v, in_v, out_v, sm):
        b, nw_, qw_, kw_, sk_, cw_, cb_, lw_, lb_ = sm
        return [ada_v[None], b, nw_, in_v[None], qw_, kw_, sk_, cw_[None], cb_, lw_, lb_, out_v[None]]

    grads = order(g_w_ada, g_w_in, g_w_out, gs)
    deltas = order(d_w_ada, d_w_in, d_w_out, ds)
    new_m = order(nm_w_ada, nm_w_in, nm_w_out, nms)
    new_v = order(nv_w_ada, nv_w_in, nv_w_out, nvs)
    return (loss, grad_x[None], *grads, *deltas, *new_m, *new_v)
```

```python
import functools

import jax
import jax.numpy as jnp
from jax import lax
from jax.experimental import pallas as pl
from jax.experimental.pallas import tpu as pltpu

F32 = jnp.float32
BF16 = jnp.bfloat16

D_MODEL = 1024
ATTN_W = 512
KV_W = 128
CONV_W = 512
IN_W = 2816
HEAD_DIM = 64
CONV_TAPS = 31
QBLK = 128
EPS = 1e-6
ROPE_THETA = 10000.0

ADAM_LR = 0.001
ADAM_B1 = 0.9
ADAM_B2 = 0.999
ADAM_EPS = 1e-08
ADAM_WD = 0.01
ADAM_STEP = 10

N_CHIPS = 4
N_DEV = 8
IN_HALF = IN_W // N_CHIPS // 2
OUT_HALF = D_MODEL // N_CHIPS // 2
ADA_SHARD = 3 * D_MODEL // N_CHIPS

VMEM_LIMIT = 56 * 1024 * 1024
CONV_PAD = 32


def _cparams(**kw):
    return pltpu.CompilerParams(vmem_limit_bytes=VMEM_LIMIT, **kw)


def _sigmoid(v):
    return 1.0 / (1.0 + jnp.exp(-v))


def _silu(v):
    return v * _sigmoid(v)


def _dsilu(v):
    s = _sigmoid(v)
    return s * (1.0 + v * (1.0 - s))


def _lane(shape):
    return lax.broadcasted_iota(jnp.int32, shape, len(shape) - 1)


PUT_ROWS = 512


def _fetch(hbm_refs, vmem_refs, sem):
    cps = [pltpu.make_async_copy(h, v, sem.at[i]) for i, (h, v) in enumerate(zip(hbm_refs, vmem_refs))]
    for cp in cps:
        cp.start()
    return cps


def _put(vmem_ref, hbm_ref, sem, m):
    r = pl.ds(pl.multiple_of(m * PUT_ROWS, PUT_ROWS), PUT_ROWS)
    return pltpu.make_async_copy(vmem_ref.at[r], hbm_ref.at[r], sem.at[m])


def _put_all(pairs, sems, m):
    for (v, h), sem in zip(pairs, sems):
        _put(v, h, sem, m).start()


def _put_wait(pairs, sems, n):
    for (v, h), sem in zip(pairs, sems):
        for m in range(n):
            _put(v, h, sem, m).wait()


def _head_mean(s, left):
    sl = jnp.sum(jnp.where(left, s, 0.0), axis=-1, keepdims=True)
    sr = jnp.sum(jnp.where(left, 0.0, s), axis=-1, keepdims=True)
    return jnp.where(left, sl, sr) * (1.0 / HEAD_DIM)


def _rot(v, first):
    return jnp.where(first, pltpu.roll(v, 96, 1), pltpu.roll(v, 32, 1))


def _norm_rope(v, w, cos, sin_s, left, first):
    r = lax.rsqrt(_head_mean(v * v, left) + EPS)
    xh = v * r
    n = xh * w
    return n * cos + _rot(n, first) * sin_s, xh, r


def _norm_rope_bwd(d, xh, r, w, cos, sin_s, left, first):
    dn = d * cos - _rot(d, first) * sin_s
    dw = jnp.sum(dn * xh, axis=0, keepdims=True)
    dxh = dn * w
    return r * (dxh - xh * _head_mean(dxh * xh, left)), dw


def _dup_heads(v, left):
    sw = pltpu.roll(v, 64, 1)
    return jnp.where(left, v, sw), jnp.where(left, sw, v)


def _prep_kv(kv_ref, kw_ref, cos_ref, sin_ref, ka_ref, va_ref, t):
    ch = 256
    for g in range(2):
        ka_ref[g, 0:QBLK, :] = jnp.zeros((QBLK, 128), BF16)
        va_ref[g, 0:QBLK, :] = jnp.zeros((QBLK, 128), BF16)

    def chunk(i, carry):
        r0 = pl.multiple_of(i * ch, ch)
        left = _lane((ch, 128)) < 64
        first = (_lane((ch, 128)) % 64) < 32
        k = kv_ref[pl.ds(r0, ch), 0:128]
        v = kv_ref[pl.ds(r0, ch), 128:256]
        kr, _, _ = _norm_rope(k, kw_ref[...], cos_ref[pl.ds(r0, ch), :], sin_ref[pl.ds(r0, ch), :], left, first)
        k0, k1 = _dup_heads(kr, left)
        v0, v1 = _dup_heads(v, left)
        ka_ref[0, pl.ds(QBLK + r0, ch), :] = k0.astype(BF16)
        ka_ref[1, pl.ds(QBLK + r0, ch), :] = k1.astype(BF16)
        va_ref[0, pl.ds(QBLK + r0, ch), :] = v0.astype(BF16)
        va_ref[1, pl.ds(QBLK + r0, ch), :] = v1.astype(BF16)
        return carry

    lax.fori_loop(0, t // ch, chunk, 0)


def _band_mask(n):
    qi = lax.broadcasted_iota(jnp.int32, (2 * QBLK, 2 * QBLK), 0) % QBLK
    kj = lax.broadcasted_iota(jnp.int32, (2 * QBLK, 2 * QBLK), 1)
    local = (kj > qi) & (kj <= qi + QBLK)
    return local & ((n > 0) | (kj >= QBLK))


def _softmax_pair(s, mask, sink0, sink1):
    row = lax.broadcasted_iota(jnp.int32, (2 * QBLK, 1), 0)
    sink = jnp.where(row < QBLK, sink0, sink1)
    s = jnp.where(mask, s, -jnp.inf)
    m = jnp.maximum(jnp.max(s, axis=-1, keepdims=True), sink)
    e = jnp.exp(s - m)
    es = jnp.exp(sink - m)
    inv = 1.0 / (jnp.sum(e, axis=-1, keepdims=True) + es)
    return e * inv, es * inv


def _stack_heads(v, left):
    return jnp.concatenate([jnp.where(left, v, 0.0), jnp.where(left, 0.0, v)], axis=0)


def _attn_fwd(q_raw, kv_raw, ga, qw2, kw2, sinks, cos_f, sin_s, wo, cw):
    t = q_raw.shape[0]
    nblk = t // QBLK
    per_put = PUT_ROWS // QBLK

    def body(q_hbm, kv_ref, ga_hbm, qw_ref, kw_ref, sk_ref, cos_ref, sin_ref, wo_ref, cw_ref,
             o_hbm, mix_hbm, wo4_ref, cw4_ref, ka_ref, va_ref, q_ref, ga_ref, o_ref, mix_ref, isem, osem0, osem1,
             ssem, rsem):
        loads = _fetch((q_hbm, ga_hbm), (q_ref, ga_ref), isem)
        outs, osems = ((o_ref, o_hbm), (mix_ref, mix_hbm)), (osem0, osem1)
        x, y, c, chips = _place()
        j = 2 * x + y
        sib = (x, y, 1 - c)
        idx = [2 * cx + cy for cx, cy in chips]
        rc = functools.partial(_remote, ssem, rsem)
        wo4_ref[j] = wo_ref[...].astype(BF16)
        cw4_ref[j] = cw_ref[...]
        sends = []
        for k, chip in enumerate(chips):
            sends.append(rc(k, wo4_ref.at[j, c], wo4_ref.at[j, c], (*chip, c)))
            sends.append(rc(6 + k, cw4_ref.at[j], cw4_ref.at[j], (*chip, c)))
        for cp in sends:
            cp.start()

        _prep_kv(kv_ref, kw_ref, cos_ref, sin_ref, ka_ref, va_ref, t)
        for cp in loads:
            cp.wait()

        def blk(n, carry):
            r0 = pl.multiple_of(n * QBLK, QBLK)
            left = _lane((QBLK, 128)) < 64
            first = (_lane((QBLK, 128)) % 64) < 32
            cos = cos_ref[pl.ds(r0, QBLK), :]
            sin = sin_ref[pl.ds(r0, QBLK), :]
            mask = _band_mask(n)
            for p in range(4):
                g = p // 2
                lanes = slice(p * 128, (p + 1) * 128)
                qr, _, _ = _norm_rope(q_ref[pl.ds(r0, QBLK), lanes], qw_ref[...], cos, sin, left, first)
                q2 = _stack_heads(qr * 0.125, left).astype(BF16)
                s = lax.dot_general(q2, ka_ref[g, pl.ds(r0, 2 * QBLK), :], (((1,), (1,)), ((), ())),
                                    preferred_element_type=F32)
                pm, _ = _softmax_pair(s, mask, sk_ref[0, 2 * p], sk_ref[0, 2 * p + 1])
                o2 = jnp.dot(pm.astype(BF16), va_ref[g, pl.ds(r0, 2 * QBLK), :], preferred_element_type=F32)
                o = jnp.where(left, o2[0:QBLK], o2[QBLK:2 * QBLK])
                o_ref[pl.ds(r0, QBLK), lanes] = o.astype(BF16)
                mix_ref[pl.ds(r0, QBLK), lanes] = (o * _silu(ga_ref[pl.ds(r0, QBLK), lanes])).astype(BF16)

            @pl.when(n % per_put == per_put - 1)
            def _():
                _put_all(outs, osems, n // per_put)

            return carry

        lax.fori_loop(0, nblk, blk, 0)
        _put_wait(outs, osems, t // PUT_ROWS)

        passed = []
        for k, chip in enumerate(chips):
            jk = idx[k]
            rc(k, wo4_ref.at[jk, c], wo4_ref.at[jk, c], sib).wait_recv()
            passed.append(rc(3 + k, wo4_ref.at[jk, c], wo4_ref.at[jk, c], sib))
            passed[-1].start()
        for k, chip in enumerate(chips):
            jk = idx[k]
            rc(3 + k, wo4_ref.at[jk, 1 - c], wo4_ref.at[jk, 1 - c], sib).wait_recv()
            rc(6 + k, cw4_ref.at[jk], cw4_ref.at[jk], sib).wait_recv()
        for cp in sends + passed:
            cp.wait_send()

    vm = pl.BlockSpec(memory_space=pltpu.VMEM)
    hbm = pl.BlockSpec(memory_space=pl.ANY)
    n_sem = 9
    return pl.pallas_call(
        body,
        name="attn_fwd",
        in_specs=[hbm, vm, hbm, vm, vm, pl.BlockSpec(memory_space=pltpu.SMEM), vm, vm, vm, vm],
        out_specs=[hbm, hbm, vm, vm],
        out_shape=[jax.ShapeDtypeStruct((t, ATTN_W), BF16), jax.ShapeDtypeStruct((t, ATTN_W), BF16),
                   jax.ShapeDtypeStruct((N_CHIPS, 2, OUT_HALF, D_MODEL), BF16),
                   jax.ShapeDtypeStruct((N_CHIPS, 32, 128), F32)],
        scratch_shapes=[pltpu.VMEM((2, t + QBLK, 128), BF16), pltpu.VMEM((2, t + QBLK, 128), BF16),
                        pltpu.VMEM((t, ATTN_W), F32), pltpu.VMEM((t, ATTN_W), F32),
                        pltpu.VMEM((t, ATTN_W), BF16), pltpu.VMEM((t, ATTN_W), BF16),
                        pltpu.SemaphoreType.DMA((2,)), pltpu.SemaphoreType.DMA((t // PUT_ROWS,)),
                        pltpu.SemaphoreType.DMA((t // PUT_ROWS,)),
                        pltpu.SemaphoreType.DMA((n_sem,)), pltpu.SemaphoreType.DMA((n_sem,))],
        compiler_params=_cparams(),
    )(q_raw, kv_raw, ga, qw2, kw2, sinks, cos_f, sin_s, wo, cw)


def _attn_bwd(q_raw, kv_raw, ga, o, dmix, qw2, kw2, sinks, cos_f, sin_s, go):
    t = q_raw.shape[0]
    nblk = t // QBLK
    per_put = PUT_ROWS // QBLK

    def body(q_hbm, kv_ref, ga_hbm, o_hbm, dm_hbm, qw_ref, kw_ref, sk_ref, cos_ref, sin_ref, go_ref,
             dq_hbm, dkv_ref, dga_hbm, sm_ref, gwo_ref, ka_ref, va_ref, dka_ref, dva_ref,
             sibo_ref, outo_ref, ino_ref, q_ref, ga_ref, o_ref, dm_ref, dq_ref, dga_ref, isem, osem0, osem1, ssem, rsem):
        loads = _fetch((q_hbm, ga_hbm, o_hbm, dm_hbm), (q_ref, ga_ref, o_ref, dm_ref), isem)
        outs, osems = ((dq_ref, dq_hbm), (dga_ref, dga_hbm)), (osem0, osem1)
        x, y, c, chips = _place()
        sib = (x, y, 1 - c)
        rc = functools.partial(_remote, ssem, rsem)
        theirs, mine = go_ref.at[:, 1 - c], go_ref.at[:, c]
        sends = [_rs_to_sibling(rc, 0, theirs, sibo_ref, sib)]
        _prep_kv(kv_ref, kw_ref, cos_ref, sin_ref, ka_ref, va_ref, t)
        dka_ref[...] = jnp.zeros_like(dka_ref)
        dva_ref[...] = jnp.zeros_like(dva_ref)
        sends += _rs_trade(rc, 0, theirs, mine, sibo_ref, outo_ref, ino_ref, OUT_HALF, c, sib, chips)
        for cp in loads:
            cp.wait()

        def blk(n, carry):
            dqw, dsk = carry
            r0 = pl.multiple_of(n * QBLK, QBLK)
            left = _lane((QBLK, 128)) < 64
            first = (_lane((QBLK, 128)) % 64) < 32
            cos = cos_ref[pl.ds(r0, QBLK), :]
            sin = sin_ref[pl.ds(r0, QBLK), :]
            mask = _band_mask(n)
            row = lax.broadcasted_iota(jnp.int32, (2 * QBLK, 1), 0)
            for p in range(4):
                g = p // 2
                lanes = slice(p * 128, (p + 1) * 128)
                rows = pl.ds(r0, QBLK)
                win = pl.ds(r0, 2 * QBLK)
                qr, xh, r = _norm_rope(q_ref[rows, lanes], qw_ref[...], cos, sin, left, first)
                q2 = _stack_heads(qr * 0.125, left).astype(BF16)
                kwin = ka_ref[g, win, :]
                vwin = va_ref[g, win, :]
                s = lax.dot_general(q2, kwin, (((1,), (1,)), ((), ())), preferred_element_type=F32)
                pm, ps = _softmax_pair(s, mask, sk_ref[0, 2 * p], sk_ref[0, 2 * p + 1])
                gav = ga_ref[rows, lanes]
                dmv = dm_ref[rows, lanes].astype(F32)
                dga_ref[rows, lanes] = (dmv * o_ref[rows, lanes].astype(F32) * _dsilu(gav)).astype(BF16)
                do2 = _stack_heads(dmv * _silu(gav), left).astype(BF16)
                dp = lax.dot_general(do2, vwin, (((1,), (1,)), ((), ())), preferred_element_type=F32)
                delta = jnp.sum(pm * dp, axis=-1, keepdims=True)
                ds = (pm * (dp - delta)).astype(BF16)
                pd = ps * delta
                d0 = jnp.sum(jnp.where(row < QBLK, pd, 0.0), axis=0, keepdims=True)
                d1 = jnp.sum(jnp.where(row < QBLK, 0.0, pd), axis=0, keepdims=True)
                l8 = _lane((1, 128))
                dsk = dsk - jnp.where(l8 == 2 * p, d0, 0.0) - jnp.where(l8 == 2 * p + 1, d1, 0.0)
                dva_ref[g, win, :] += lax.dot_general(pm.astype(BF16), do2, (((0,), (0,)), ((), ())),
                                                      preferred_element_type=F32)
                dka_ref[g, win, :] += lax.dot_general(ds, q2, (((0,), (0,)), ((), ())),
                                                      preferred_element_type=F32)
                dq2 = jnp.dot(ds, kwin, preferred_element_type=F32)
                dqr = jnp.where(left, dq2[0:QBLK], dq2[QBLK:2 * QBLK]) * 0.125
                dq, dw = _norm_rope_bwd(dqr, xh, r, qw_ref[...], cos, sin, left, first)
                dq_ref[rows, lanes] = dq.astype(BF16)
                dqw = dqw + dw

            @pl.when(n % per_put == per_put - 1)
            def _():
                _put_all(outs, osems, n // per_put)

            return dqw, dsk

        zero = jnp.zeros((1, 128), F32)
        dqw, dsk = lax.fori_loop(0, nblk, blk, (zero, zero))

        ch = 256

        def chunk(i, dkw):
            r0 = pl.multiple_of(i * ch, ch)
            left = _lane((ch, 128)) < 64
            first = (_lane((ch, 128)) % 64) < 32
            rows = pl.ds(r0, ch)
            prow = pl.ds(QBLK + r0, ch)

            def fold(ref):
                a0 = ref[0, prow, :]
                a1 = ref[1, prow, :]
                return jnp.where(left, a0 + pltpu.roll(a0, 64, 1), a1 + pltpu.roll(a1, 64, 1))

            cos = cos_ref[rows, :]
            sin = sin_ref[rows, :]
            _, xh, r = _norm_rope(kv_ref[rows, 0:128], kw_ref[...], cos, sin, left, first)
            dk, dw = _norm_rope_bwd(fold(dka_ref), xh, r, kw_ref[...], cos, sin, left, first)
            dkv_ref[rows, 0:128] = dk.astype(BF16)
            dkv_ref[rows, 128:256] = fold(dva_ref).astype(BF16)
            return dkw + dw

        dkw = lax.fori_loop(0, t // ch, chunk, zero)
        sm_ref[...] = jnp.zeros((8, 128), F32)
        sm_ref[0:1, :] = dqw + pltpu.roll(dqw, 64, 1)
        sm_ref[1:2, :] = dkw + pltpu.roll(dkw, 64, 1)
        sm_ref[2:3, :] = dsk

        j = 2 * x + y
        sends.append(_rs_total(rc, 0, mine, sibo_ref, outo_ref, ino_ref, gwo_ref, OUT_HALF, j, c, sib))
        _rs_done(rc, 0, gwo_ref, c, sib)
        for cp in sends:
            cp.wait_send()
        _put_wait(outs, osems, t // PUT_ROWS)

    vm = pl.BlockSpec(memory_space=pltpu.VMEM)
    hbm = pl.BlockSpec(memory_space=pl.ANY)
    return pl.pallas_call(
        body,
        name="attn_bwd",
        in_specs=[hbm, vm, hbm, hbm, hbm, vm, vm, pl.BlockSpec(memory_space=pltpu.SMEM), vm, vm, vm],
        out_specs=[hbm, vm, hbm, vm, vm],
        out_shape=[jax.ShapeDtypeStruct((t, ATTN_W), BF16), jax.ShapeDtypeStruct((t, 2 * KV_W), BF16),
                   jax.ShapeDtypeStruct((t, ATTN_W), BF16), jax.ShapeDtypeStruct((8, 128), F32),
                   jax.ShapeDtypeStruct((2, OUT_HALF, D_MODEL), F32)],
        scratch_shapes=[pltpu.VMEM((2, t + QBLK, 128), BF16), pltpu.VMEM((2, t + QBLK, 128), BF16),
                        pltpu.VMEM((2, t + QBLK, 128), F32), pltpu.VMEM((2, t + QBLK, 128), F32)]
        + _rs_scratch(OUT_HALF)
        + [pltpu.VMEM((t, ATTN_W), F32), pltpu.VMEM((t, ATTN_W), F32), pltpu.VMEM((t, ATTN_W), BF16),
           pltpu.VMEM((t, ATTN_W), BF16), pltpu.VMEM((t, ATTN_W), BF16), pltpu.VMEM((t, ATTN_W), BF16),
           pltpu.SemaphoreType.DMA((4,)), pltpu.SemaphoreType.DMA((t // PUT_ROWS,)), pltpu.SemaphoreType.DMA((t // PUT_ROWS,)),
           pltpu.SemaphoreType.DMA((RS_SEMS,)), pltpu.SemaphoreType.DMA((RS_SEMS,))],
        compiler_params=_cparams(),
    )(q_raw, kv_raw, ga, o, dmix, qw2, kw2, sinks, cos_f, sin_s, go)


CONV_CH = 256
CONV_SUB = 64
CONV_ACCS = 3


def _shifted_windows(src_ref, r0, sh_ref):
    rows = CONV_CH + CONV_PAD
    win = src_ref[pl.ds(r0, rows), :]
    for b in range(8):
        sh = win if b == 0 else pltpu.roll(win, rows - b, 0)
        for c in range(CONV_W // 128):
            sh_ref[b, c] = sh[:, c * 128:(c + 1) * 128]


def _conv_fwd(ua, ug, gb, cw, cb, lw, lb):
    t = ua.shape[0]

    def body(ua_hbm, ug_hbm, gb_hbm, cw_ref, cb_ref, lw_ref, lb_ref, cz_hbm, mix_hbm, zp_ref, sh_ref,
             ua_ref, ug_ref, gb_ref, cz_ref, mix_ref, isem, osem0, osem1):
        loads = _fetch((ua_hbm, ug_hbm, gb_hbm), (ua_ref, ug_ref, gb_ref), isem)
        outs, osems = ((cz_ref, cz_hbm), (mix_ref, mix_hbm)), (osem0, osem1)
        per_put = PUT_ROWS // CONV_CH
        zp_ref[0:CONV_PAD, :] = jnp.zeros((CONV_PAD, CONV_W), F32)
        loads[0].wait()
        loads[1].wait()

        def glu(i, carry):
            r0 = pl.multiple_of(i * CONV_CH, CONV_CH)
            rows = pl.ds(r0, CONV_CH)
            zp_ref[pl.ds(CONV_PAD + r0, CONV_CH), :] = ua_ref[rows, :] * _sigmoid(ug_ref[rows, :])
            return carry

        lax.fori_loop(0, t // CONV_CH, glu, 0)
        loads[2].wait()

        def chunk(i, carry):
            r0 = pl.multiple_of(i * CONV_CH, CONV_CH)
            _shifted_windows(zp_ref, r0, sh_ref)
            for c in range(CONV_W // 128):
                lanes = slice(c * 128, (c + 1) * 128)

                def sub(k, carry2):
                    b0 = pl.multiple_of(k * CONV_SUB, CONV_SUB)
                    acc = [jnp.broadcast_to(cb_ref[0:1, lanes], (CONV_SUB, 128))] + [None] * (CONV_ACCS - 1)
                    for j in range(CONV_TAPS):
                        off = j + CONV_PAD - (CONV_TAPS - 1)
                        term = sh_ref[off % 8, c, pl.ds(b0 + 8 * (off // 8), CONV_SUB), :] * cw_ref[j:j + 1, lanes]
                        acc[j % CONV_ACCS] = term if acc[j % CONV_ACCS] is None else acc[j % CONV_ACCS] + term
                    cz_ref[pl.ds(r0 + b0, CONV_SUB), lanes] = functools.reduce(lambda a, b: a + b, acc)
                    return carry2

                lax.fori_loop(0, CONV_CH // CONV_SUB, sub, 0)
            rows = pl.ds(r0, CONV_CH)
            cz = cz_ref[rows, :]
            mu = jnp.mean(cz, axis=-1, keepdims=True)
            xc = cz - mu
            rs = lax.rsqrt(jnp.mean(xc * xc, axis=-1, keepdims=True) + EPS)
            ln = xc * rs * lw_ref[...] + lb_ref[...]
            mix_ref[rows, :] = (_silu(ln) * _silu(gb_ref[rows, :])).astype(BF16)

            @pl.when(i % per_put == per_put - 1)
            def _():
                _put_all(outs, osems, i // per_put)

            return carry

        lax.fori_loop(0, t // CONV_CH, chunk, 0)
        _put_wait(outs, osems, t // PUT_ROWS)

    vm = pl.BlockSpec(memory_space=pltpu.VMEM)
    hbm = pl.BlockSpec(memory_space=pl.ANY)
    nput = t // PUT_ROWS
    return pl.pallas_call(
        body,
        name="conv_fwd",
        in_specs=[hbm] * 3 + [vm] * 4,
        out_specs=[hbm, hbm],
        out_shape=[jax.ShapeDtypeStruct((t, CONV_W), F32), jax.ShapeDtypeStruct((t, CONV_W), BF16)],
        scratch_shapes=[pltpu.VMEM((t + CONV_PAD, CONV_W), F32),
                        pltpu.VMEM((8, CONV_W // 128, CONV_CH + CONV_PAD, 128), F32),
                        pltpu.VMEM((t, CONV_W), F32), pltpu.VMEM((t, CONV_W), F32), pltpu.VMEM((t, CONV_W), F32),
                        pltpu.VMEM((t, CONV_W), F32), pltpu.VMEM((t, CONV_W), BF16),
                        pltpu.SemaphoreType.DMA((3,)), pltpu.SemaphoreType.DMA((nput,)), pltpu.SemaphoreType.DMA((nput,))],
        compiler_params=_cparams(),
    )(ua, ug, gb, cw, cb, lw, lb)


def _conv_bwd(ua, ug, gb, cz, dmix, cw, lw, lb):
    t = ua.shape[0]

    def body(ua_hbm, ug_hbm, gb_hbm, cz_hbm, dm_hbm, cw_ref, lw_ref, lb_ref,
             dua_hbm, dug_hbm, dgb_hbm, dcw_ref, dvec_ref, zp_ref, dp_ref, sh_ref, wacc_ref,
             ua_ref, ug_ref, gb_ref, cz_ref, dm_ref, dua_ref, dug_ref, dgb_ref, isem, osem0, osem1, osem2):
        loads = _fetch((ua_hbm, ug_hbm, gb_hbm, cz_hbm, dm_hbm), (ua_ref, ug_ref, gb_ref, cz_ref, dm_ref), isem)
        per_put = PUT_ROWS // CONV_CH
        zp_ref[0:CONV_PAD, :] = jnp.zeros((CONV_PAD, CONV_W), F32)
        dp_ref[t:t + CONV_PAD, :] = jnp.zeros((CONV_PAD, CONV_W), F32)
        wacc_ref[...] = jnp.zeros_like(wacc_ref)
        for cp in loads:
            cp.wait()

        def pointwise(i, carry):
            dcb, dlw, dlb = carry
            r0 = pl.multiple_of(i * CONV_CH, CONV_CH)
            rows = pl.ds(r0, CONV_CH)
            zp_ref[pl.ds(CONV_PAD + r0, CONV_CH), :] = ua_ref[rows, :] * _sigmoid(ug_ref[rows, :])
            cz = cz_ref[rows, :]
            mu = jnp.mean(cz, axis=-1, keepdims=True)
            xc = cz - mu
            rs = lax.rsqrt(jnp.mean(xc * xc, axis=-1, keepdims=True) + EPS)
            xh = xc * rs
            ln = xh * lw_ref[...] + lb_ref[...]
            gbv = gb_ref[rows, :]
            dy = dm_ref[rows, :].astype(F32)
            dgb_ref[rows, :] = (dy * _silu(ln) * _dsilu(gbv)).astype(BF16)
            dl = dy * _silu(gbv) * _dsilu(ln)
            dxh = dl * lw_ref[...]
            dcz = rs * (dxh - jnp.mean(dxh, axis=-1, keepdims=True)
                        - xh * jnp.mean(dxh * xh, axis=-1, keepdims=True))
            dp_ref[rows, :] = dcz

            @pl.when(i % per_put == per_put - 1)
            def _():
                _put(dgb_ref, dgb_hbm, osem2, i // per_put).start()

            return (dcb + jnp.sum(dcz, axis=0, keepdims=True),
                    dlw + jnp.sum(dl * xh, axis=0, keepdims=True),
                    dlb + jnp.sum(dl, axis=0, keepdims=True))

        zero = jnp.zeros((1, CONV_W), F32)
        dcb, dlw, dlb = lax.fori_loop(0, t // CONV_CH, pointwise, (zero, zero, zero))
        dvec_ref[...] = jnp.zeros((8, CONV_W), F32)
        dvec_ref[0:1, :] = dcb
        dvec_ref[1:2, :] = dlw
        dvec_ref[2:3, :] = dlb

        def chunk(i, carry):
            r0 = pl.multiple_of(i * CONV_CH, CONV_CH)
            _shifted_windows(dp_ref, r0, sh_ref)
            for c in range(CONV_W // 128):
                lanes = slice(c * 128, (c + 1) * 128)

                def sub(k, carry2):
                    b0 = pl.multiple_of(k * CONV_SUB, CONV_SUB)
                    acc = [None] * CONV_ACCS
                    for j in range(CONV_TAPS):
                        off = CONV_TAPS - 1 - j
                        term = sh_ref[off % 8, c, pl.ds(b0 + 8 * (off // 8), CONV_SUB), :] * cw_ref[j:j + 1, lanes]
                        acc[j % CONV_ACCS] = term if acc[j % CONV_ACCS] is None else acc[j % CONV_ACCS] + term
                    acc = functools.reduce(lambda a, b: a + b, acc)
                    rr = pl.ds(r0 + b0, CONV_SUB)
                    sg = _sigmoid(ug_ref[rr, lanes])
                    dua_ref[rr, lanes] = (acc * sg).astype(BF16)
                    dug_ref[rr, lanes] = (acc * ua_ref[rr, lanes] * sg * (1.0 - sg)).astype(BF16)
                    return carry2

                lax.fori_loop(0, CONV_CH // CONV_SUB, sub, 0)
            _shifted_windows(zp_ref, r0, sh_ref)
            for c in range(CONV_W // 128):
                lanes = slice(c * 128, (c + 1) * 128)

                def subw(k, carry2):
                    b0 = pl.multiple_of(k * CONV_SUB, CONV_SUB)
                    dcz = dp_ref[pl.ds(r0 + b0, CONV_SUB), lanes]
                    for j in range(CONV_TAPS):
                        off = j + CONV_PAD - (CONV_TAPS - 1)
                        pr = dcz * sh_ref[off % 8, c, pl.ds(b0 + 8 * (off // 8), CONV_SUB), :]
                        parts = [pr[8 * q:8 * (q + 1)] for q in range(CONV_SUB // 8)]
                        while len(parts) > 1:
                            parts = [a + b for a, b in zip(parts[0::2], parts[1::2])]
                        wacc_ref[8 * j:8 * (j + 1), lanes] += parts[0]
                    return carry2

                lax.fori_loop(0, CONV_CH // CONV_SUB, subw, 0)

            @pl.when(i % per_put == per_put - 1)
            def _():
                _put_all(((dua_ref, dua_hbm), (dug_ref, dug_hbm)), (osem0, osem1), i // per_put)

            return carry

        lax.fori_loop(0, t // CONV_CH, chunk, 0)
        _put_wait(((dua_ref, dua_hbm), (dug_ref, dug_hbm), (dgb_ref, dgb_hbm)), (osem0, osem1, osem2), t // PUT_ROWS)
        dcw_ref[...] = jnp.zeros((32, CONV_W), F32)
        for j in range(CONV_TAPS):
            dcw_ref[j:j + 1, :] = jnp.sum(wacc_ref[8 * j:8 * (j + 1), :], axis=0, keepdims=True)

    vm = pl.BlockSpec(memory_space=pltpu.VMEM)
    hbm = pl.BlockSpec(memory_space=pl.ANY)
    return pl.pallas_call(
        body,
        name="conv_bwd",
        in_specs=[hbm] * 5 + [vm] * 3,
        out_specs=[hbm] * 3 + [vm] * 2,
        out_shape=[jax.ShapeDtypeStruct((t, CONV_W), BF16)] * 3
        + [jax.ShapeDtypeStruct((32, CONV_W), F32), jax.ShapeDtypeStruct((8, CONV_W), F32)],
        scratch_shapes=[pltpu.VMEM((t + CONV_PAD, CONV_W), F32), pltpu.VMEM((t + CONV_PAD, CONV_W), F32),
                        pltpu.VMEM((8, CONV_W // 128, CONV_CH + CONV_PAD, 128), F32), pltpu.VMEM((8 * 32, CONV_W), F32)]
        + [pltpu.VMEM((t, CONV_W), F32)] * 4 + [pltpu.VMEM((t, CONV_W), BF16)] * 4
        + [pltpu.SemaphoreType.DMA((5,))] + [pltpu.SemaphoreType.DMA((t // PUT_ROWS,))] * 3,
        compiler_params=_cparams(),
    )(ua, ug, gb, cz, dmix, cw, lw, lb)


def _out_proj(mix_a, mix_b, x, tgt, gate, w_out):
    t = x.shape[0]
    tm = 512
    nstep = t // tm

    def body(ma_ref, mb_ref, x_ref, t_ref, g_ref, w_ref, dout_ref, dma_ref, dmb_ref, gw_ref, red_ref, acc_ref):
        i = pl.program_id(0)

        @pl.when(i == 0)
        def _():
            acc_ref[...] = jnp.zeros_like(acc_ref)
            red_ref[...] = jnp.zeros_like(red_ref)

        mix = jnp.concatenate([ma_ref[...], mb_ref[...]], axis=1)
        y = jnp.dot(mix, w_ref[...], preferred_element_type=F32)
        gate_v = g_ref[...]
        err = x_ref[...] + gate_v * y - t_ref[...]
        dout = err * (1.0 / D_MODEL)
        dout_ref[...] = dout
        red_ref[0:1, :] += jnp.sum(dout * y, axis=0, keepdims=True)
        red_ref[1:2, :] += jnp.sum(err * err, axis=0, keepdims=True)
        dy = (dout * gate_v).astype(BF16)
        dmix = lax.dot_general(dy, w_ref[...], (((1,), (1,)), ((), ())), preferred_element_type=F32)
        dma_ref[...] = dmix[:, 0:512].astype(BF16)
        dmb_ref[...] = dmix[:, 512:1024].astype(BF16)
        acc_ref[...] += lax.dot_general(mix, dy, (((0,), (0,)), ((), ())), preferred_element_type=F32)

        @pl.when(i == nstep - 1)
        def _():
            gw_ref[...] = acc_ref[...].astype(BF16)

    row = lambda w: pl.BlockSpec((tm, w), lambda i: (i, 0))
    const = lambda s: pl.BlockSpec(s, lambda i: (0, 0))
    return pl.pallas_call(
        body,
        name="out_proj",
        grid=(nstep,),
        in_specs=[row(512), row(512), row(D_MODEL), row(D_MODEL), const((1, D_MODEL)),
                  pl.BlockSpec((D_MODEL, D_MODEL), lambda i: (0, 0), pipeline_mode=pl.Buffered(1))],
        out_specs=[row(D_MODEL), row(512), row(512), const((D_MODEL, D_MODEL)), const((8, D_MODEL))],
        out_shape=[jax.ShapeDtypeStruct((t, D_MODEL), F32), jax.ShapeDtypeStruct((t, 512), BF16),
                   jax.ShapeDtypeStruct((t, 512), BF16), jax.ShapeDtypeStruct((D_MODEL, D_MODEL), BF16),
                   jax.ShapeDtypeStruct((8, D_MODEL), F32)],
        scratch_shapes=[pltpu.VMEM((D_MODEL, D_MODEL), F32)],
        compiler_params=_cparams(dimension_semantics=("arbitrary",)),
    )(mix_a, mix_b, x, tgt, gate, w_out)


DPROJ_WIDTHS = (512, 256, 512, 512, 512, 512)
DPROJ_STARTS = (0, 512, 768, 1280, 1792, 2304)
WIN_W = 768
WIN_START = (0, 640, 1408, 2048)
WIN_OFF = (0, 64, 0, 64)
N_GW = N_CHIPS


def _window_pieces(s):
    lo, hi = WIN_START[s], WIN_START[s] + WIN_W
    out = []
    for p, (st, w) in enumerate(zip(DPROJ_STARTS, DPROJ_WIDTHS)):
        a, b = max(lo, st), min(hi, st + w)
        if a < b:
            out.append((p, a - st, b - a, a - lo))
    return out


def _in_proj_bwd(dparts, h, x, dout, s1, nw, wt_full, small0, row0):
    t = x.shape[0]
    tm = 256
    nstep = N_GW + t // tm
    n_sem = 20
    rows0 = small0.shape[0]
    hs = rows0 // 2
    npart = len(DPROJ_WIDTHS)

    def body(*refs):
        d_hbm, d_ref = refs[:npart], refs[npart:2 * npart]
        (x_ref, dout_ref, s1_ref, nw_ref, h_ref, wt_hbm, sm0_ref, row0_ref,
         gx_ref, gw_hbm, ssum_ref, rows_ref,
         stg_ref, wt_ref, gt_ref, sib_ref, out_ref, in_ref, res_ref, sall_ref, red_ref, ssib_ref, schip_ref, sres_ref,
         wsem, lsem, ssem, rsem) = refs[2 * npart:]
        i = pl.program_id(0)
        x_, y_, c, chips = _place()
        j = 2 * x_ + y_
        dev = 2 * j + c
        sib = (x_, y_, 1 - c)
        rc = functools.partial(_remote, ssem, rsem)
        rel_chip = [2 * cx + cy for cx, cy in chips] + [j]
        peers = [(px, py, pc) for px in (x_, 1 - x_) for py in (y_, 1 - y_) for pc in (c, 1 - c)][1:]
        wt_copy = pltpu.make_async_copy(wt_hbm, wt_ref, lsem.at[0])

        def window(case, slot):
            return [pltpu.make_async_copy(d_hbm[p].at[:, pl.ds(c0, w)], stg_ref.at[slot, :, pl.ds(w0, w)], wsem.at[slot, n])
                    for n, (p, c0, w, w0) in enumerate(_window_pieces(case))]

        def to_sibling(k):
            return rc(k, gt_ref.at[k, 1 - c], sib_ref.at[k], sib)

        def to_chip(k):
            return rc(4 + k, out_ref.at[k], in_ref.at[k], (*chips[k], c))

        def trade(k):
            to_sibling(k).wait_recv()

            def add(n, carry):
                rr = pl.ds(pl.multiple_of(n * RS_CH, RS_CH), RS_CH)
                out_ref[k, rr, :] = (gt_ref[k, c, rr, :].astype(F32) + sib_ref[k, rr, :].astype(F32)).astype(BF16)
                return carry

            lax.fori_loop(0, IN_HALF // RS_CH, add, 0)
            to_chip(k).start()

        mine_s = pl.ds(pl.multiple_of(c * hs, 8), hs)
        other_s = pl.ds(pl.multiple_of((1 - c) * hs, 8), hs)

        def small_to_sibling():
            return rc(15, sm0_ref.at[other_s], ssib_ref, sib)

        def small_to_chip(k):
            return rc(16 + k, schip_ref.at[j], schip_ref.at[j], (*chips[k], c))

        def small_share():
            return rc(19, sres_ref.at[c], sres_ref.at[c], sib)

        for k in range(N_GW):
            @pl.when(i == k)
            def _(k=k):
                slot = k % 2
                if k == 0:
                    red_ref[...] = jnp.zeros_like(red_ref)
                    wt_copy.start()
                    small_to_sibling().start()
                if k == 1:
                    small_to_sibling().wait_recv()
                    schip_ref[j] = sm0_ref[mine_s, :] + ssib_ref[...]
                    for kk in range(3):
                        small_to_chip(kk).start()
                if k == N_GW - 1:
                    for kk in range(3):
                        jk = rel_chip[kk]
                        rc(16 + kk, schip_ref.at[jk], schip_ref.at[jk], sib).wait_recv()
                    tot = schip_ref[0]
                    for d in range(1, N_CHIPS):
                        tot = tot + schip_ref[d]
                    sres_ref[c] = tot
                    small_share().start()
                for case in range(N_CHIPS):
                    if k == 0:
                        @pl.when(rel_chip[0] == case)
                        def _():
                            for cp in window(case, 0):
                                cp.start()
                    if k + 1 < N_GW:
                        @pl.when(rel_chip[k + 1] == case)
                        def _():
                            for cp in window(case, 1 - slot):
                                cp.start()
                for case in range(N_CHIPS):
                    @pl.when(rel_chip[k] == case)
                    def _():
                        for cp in window(case, slot):
                            cp.wait()
                g = lax.dot_general(stg_ref[slot], h_ref[...], (((0,), (0,)), ((), ())), preferred_element_type=F32)
                for off in sorted(set(WIN_OFF)):
                    @pl.when(rel_chip[k] % 2 == (1 if off else 0))
                    def _():
                        gt_ref[k, 0] = g[off:off + IN_HALF].astype(BF16)
                        gt_ref[k, 1] = g[off + IN_HALF:off + 2 * IN_HALF].astype(BF16)
                to_sibling(k).start()
                if k >= 1:
                    trade(k - 1)

        @pl.when(i == N_GW)
        def _():
            wt_copy.wait()

        @pl.when(i >= N_GW)
        def _():
            xv = x_ref[...]
            r = lax.rsqrt(jnp.mean(xv * xv, axis=-1, keepdims=True) + EPS)
            xh = xv * r
            n = xh * nw_ref[...]
            dproj = jnp.concatenate([ref[...] for ref in d_ref], axis=1)
            dh = jnp.dot(dproj, wt_ref[...], preferred_element_type=F32)
            red_ref[0:1, :] += jnp.sum(dh, axis=0, keepdims=True)
            red_ref[1:2, :] += jnp.sum(dh * n, axis=0, keepdims=True)
            dn = dh * s1_ref[...]
            red_ref[2:3, :] += jnp.sum(dn * xh, axis=0, keepdims=True)
            dxh = dn * nw_ref[...]
            gx_ref[...] = dout_ref[...] + r * (dxh - xh * jnp.mean(dxh * xh, axis=-1, keepdims=True))

        @pl.when(i == nstep - 1)
        def _():
            sall_ref[dev] = row0_ref[...]
            sall_ref[dev, 2:5, :] = red_ref[0:3, :]
            sends = [rc(8 + k, sall_ref.at[dev], sall_ref.at[dev], peer) for k, peer in enumerate(peers)]
            for cp in sends:
                cp.start()
            sends += [to_sibling(k) for k in range(N_GW)] + [to_chip(k) for k in range(3)]
            sends += [small_to_sibling(), small_share()] + [small_to_chip(k) for k in range(3)]
            own = N_GW - 1
            to_sibling(own).wait_recv()
            for k in range(3):
                to_chip(k).wait_recv()

            def total(n, carry):
                rr = pl.ds(pl.multiple_of(n * RS_CH, RS_CH), RS_CH)
                acc = gt_ref[own, c, rr, :].astype(F32) + sib_ref[own, rr, :].astype(F32)
                for k in range(3):
                    acc = acc + in_ref[k, rr, :].astype(F32)
                res_ref[c, rr, :] = acc
                return carry

            lax.fori_loop(0, IN_HALF // RS_CH, total, 0)
            share = rc(7, res_ref.at[c], res_ref.at[c], sib)
            share.start()
            sends.append(share)
            for k, (px, py, pc) in enumerate(peers):
                pdev = 4 * px + 2 * py + pc
                rc(8 + k, sall_ref.at[pdev], sall_ref.at[pdev], (px, py, pc)).wait_recv()
            rows_ref[...] = sall_ref[...]
            rc(19, sres_ref.at[1 - c], sres_ref.at[1 - c], sib).wait_recv()
            ssum_ref[0:hs, :] = sres_ref[0]
            ssum_ref[hs:rows0, :] = sres_ref[1]
            rc(7, res_ref.at[1 - c], res_ref.at[1 - c], sib).wait_recv()
            back = pltpu.make_async_copy(res_ref, gw_hbm, lsem.at[1])
            back.start()
            for cp in sends:
                cp.wait_send()
            back.wait()

    blk = lambda i: jnp.maximum(i - N_GW, 0)
    row = lambda w: pl.BlockSpec((tm, w), lambda i: (blk(i), 0))
    vec = pl.BlockSpec((1, D_MODEL), lambda i: (0, 0))
    const = lambda shape: pl.BlockSpec(shape, lambda i: (0,) * len(shape))
    hbm = pl.BlockSpec(memory_space=pl.ANY)
    return pl.pallas_call(
        body,
        name="in_proj_bwd",
        grid=(nstep,),
        in_specs=[hbm] * npart + [row(w) for w in DPROJ_WIDTHS] + [row(D_MODEL), row(D_MODEL), vec, vec,
                  pl.BlockSpec((t, D_MODEL), lambda i: (0, 0), pipeline_mode=pl.Buffered(1)), hbm, const((rows0, D_MODEL)),
                  const((8, D_MODEL))],
        out_specs=[row(D_MODEL), hbm, const((rows0, D_MODEL)), const((N_DEV, 8, D_MODEL))],
        out_shape=[jax.ShapeDtypeStruct((t, D_MODEL), F32), jax.ShapeDtypeStruct((2, IN_HALF, D_MODEL), F32),
                   jax.ShapeDtypeStruct((rows0, D_MODEL), F32), jax.ShapeDtypeStruct((N_DEV, 8, D_MODEL), F32)],
        scratch_shapes=[pltpu.VMEM((2, t, WIN_W), BF16), pltpu.VMEM((IN_W, D_MODEL), BF16),
                        pltpu.VMEM((N_CHIPS, 2, IN_HALF, D_MODEL), BF16), pltpu.VMEM((N_CHIPS, IN_HALF, D_MODEL), BF16),
                        pltpu.VMEM((3, IN_HALF, D_MODEL), BF16), pltpu.VMEM((3, IN_HALF, D_MODEL), BF16),
                        pltpu.VMEM((2, IN_HALF, D_MODEL), F32), pltpu.VMEM((N_DEV, 8, D_MODEL), F32),
                        pltpu.VMEM((8, D_MODEL), F32), pltpu.VMEM((hs, D_MODEL), F32), pltpu.VMEM((N_CHIPS, hs, D_MODEL), F32),
                        pltpu.VMEM((2, hs, D_MODEL), F32), pltpu.SemaphoreType.DMA((2, 3)), pltpu.SemaphoreType.DMA((2,)),
                        pltpu.SemaphoreType.DMA((n_sem,)), pltpu.SemaphoreType.DMA((n_sem,))],
        compiler_params=_cparams(dimension_semantics=("arbitrary",)),
    )(*dparts, *dparts, x, dout, s1, nw, h, wt_full, small0, row0)


MESH = pl.DeviceIdType.MESH


def _place():
    x, y, c = lax.axis_index("x"), lax.axis_index("y"), lax.axis_index("c")
    chips = [(1 - x, y), (x, 1 - y), (1 - x, 1 - y)]
    return x, y, c, chips


def _remote(sems_s, sems_r, k, src, dst, to):
    return pltpu.make_async_remote_copy(src_ref=src, dst_ref=dst, send_sem=sems_s.at[k], recv_sem=sems_r.at[k],
                                        device_id=to, device_id_type=MESH)


RS_CH = 32
RS_SEMS = 5


def _rs_to_sibling(rc, s0, theirs, sib_ref, sib):
    cp = rc(s0, theirs, sib_ref, sib)
    cp.start()
    return cp


def _rs_trade(rc, s0, theirs, mine, sib_ref, out_ref, in_ref, rows, c, sib, chips):
    rc(s0, theirs, sib_ref, sib).wait_recv()
    cps = []
    for k, (cx, cy) in enumerate(chips):
        jk = 2 * cx + cy

        def add(i, carry, jk=jk, k=k):
            rr = pl.ds(pl.multiple_of(i * RS_CH, RS_CH), RS_CH)
            out_ref[k, rr, :] = (mine[jk, rr, :].astype(F32) + sib_ref[jk, rr, :].astype(F32)).astype(BF16)
            return carry

        lax.fori_loop(0, rows // RS_CH, add, 0)
        cps.append(rc(s0 + 1 + k, out_ref.at[k], in_ref.at[k], (cx, cy, c)))
        cps[-1].start()
    return cps


def _rs_total(rc, s0, mine, sib_ref, out_ref, in_ref, res_ref, rows, j, c, sib):
    for k in range(3):
        rc(s0 + 1 + k, out_ref.at[k], in_ref.at[k], sib).wait_recv()

    def total(i, carry):
        rr = pl.ds(pl.multiple_of(i * RS_CH, RS_CH), RS_CH)
        acc = mine[j, rr, :].astype(F32) + sib_ref[j, rr, :].astype(F32)
        for k in range(3):
            acc = acc + in_ref[k, rr, :].astype(F32)
        res_ref[c, rr, :] = acc
        return carry

    lax.fori_loop(0, rows // RS_CH, total, 0)
    cp = rc(s0 + 4, res_ref.at[c], res_ref.at[c], sib)
    cp.start()
    return cp


def _rs_done(rc, s0, res_ref, c, sib):
    rc(s0 + 4, res_ref.at[1 - c], res_ref.at[1 - c], sib).wait_recv()


def _rs_scratch(rows):
    return [pltpu.VMEM((N_CHIPS, rows, D_MODEL), BF16), pltpu.VMEM((3, rows, D_MODEL), BF16),
            pltpu.VMEM((3, rows, D_MODEL), BF16)]


MAIN_W = 640
MAIN_DST = (((0, 0, 512), (1, 0, 128)), ((2, 0, 512), (3, 0, 128)), ((3, 128, 384), (4, 0, 256)), ((4, 384, 128), (5, 0, 512)))
PAIR_DST = ((1, 128, 128), (4, 256, 128))


def _in_proj_gather(x, wt, c_row, w_ada, b_sh, nw):
    t = x.shape[0]
    ch = 256
    n_sem = 16

    def body(x_hbm, wt_ref, c_ref, wada_ref, bsh_ref, nw_ref,
             q_hbm, kv_hbm, ga_hbm, ua_hbm, ug_hbm, gb_hbm, h_hbm, w4_hbm, call_ref, ada_ref,
             x_ref, h_ref, w4_ref, stg_ref, pstg_ref, part_ref, lsem, osem, wsem, ssem, rsem):
        outs = (q_hbm, kv_hbm, ga_hbm, ua_hbm, ug_hbm, gb_hbm)
        x_, y_, c, chips = _place()
        j = 2 * x_ + y_
        dev = 2 * j + c
        sib = (x_, y_, 1 - c)
        idx = [2 * cx + cy for cx, cy in chips]
        rc = functools.partial(_remote, ssem, rsem)
        x_copy = pltpu.make_async_copy(x_hbm, x_ref, lsem.at[0])
        x_copy.start()

        def rows_of(s, cc):
            return pl.ds(pl.multiple_of(2 * IN_HALF * s + IN_HALF * cc, 16), IN_HALF)

        w4_ref[rows_of(j, 0), :] = wt_ref[0].astype(BF16)
        w4_ref[rows_of(j, 1), :] = wt_ref[1].astype(BF16)
        call_ref[dev] = c_ref[...]
        sends = []
        peers = [(px, py, pc) for px in (x_, 1 - x_) for py in (y_, 1 - y_) for pc in (c, 1 - c)][1:]
        for k, peer in enumerate(peers):
            sends.append(rc(k, call_ref.at[dev], call_ref.at[dev], peer))
        for k, chip in enumerate(chips):
            sends.append(rc(7 + k, w4_ref.at[rows_of(j, c)], w4_ref.at[rows_of(j, c)], (*chip, c)))
        for cp in sends:
            cp.start()

        for k, (px, py, pc) in enumerate(peers):
            pdev = 4 * px + 2 * py + pc
            rc(k, call_ref.at[pdev], call_ref.at[pdev], (px, py, pc)).wait_recv()
        rowid = lax.broadcasted_iota(jnp.int32, (N_DEV, D_MODEL), 0)
        call = jnp.zeros((N_DEV, D_MODEL), F32)
        for r in range(N_DEV):
            call = jnp.where(rowid == r, jnp.broadcast_to(call_ref[r], (N_DEV, D_MODEL)), call)
        part = jnp.dot(_silu(call).astype(BF16), wada_ref[...].astype(BF16), preferred_element_type=F32) + bsh_ref[...]
        for r in range(N_DEV):
            part_ref[r] = part[r:r + 1, :]
        ada_ref[j] = part_ref[dev]
        for k, chip in enumerate(chips):
            sends.append(rc(13 + k, part_ref.at[2 * idx[k] + c], ada_ref.at[j], (*chip, c)))
            sends[-1].start()
        for k in range(3):
            rc(13 + k, ada_ref.at[idx[k]], ada_ref.at[idx[k]], sib).wait_recv()

        shift = jnp.concatenate([ada_ref[0], ada_ref[1][:, 0:256]], axis=1)
        s1 = 1.0 + jnp.concatenate([ada_ref[1][:, 256:768], ada_ref[2][:, 0:512]], axis=1)
        x_copy.wait()

        def norm(i, carry):
            rr = pl.ds(pl.multiple_of(i * ch, ch), ch)
            xv = x_ref[rr, :]
            r = lax.rsqrt(jnp.mean(xv * xv, axis=-1, keepdims=True) + EPS)
            h_ref[rr, :] = ((xv * r) * nw_ref[...] * s1 + shift).astype(BF16)
            return carry

        lax.fori_loop(0, t // ch, norm, 0)
        h_copy = pltpu.make_async_copy(h_ref, h_hbm, lsem.at[1])
        h_copy.start()

        def put_main(case, slot):
            cps, col = [], 0
            for n, (a, c0, w) in enumerate(MAIN_DST[case]):
                cps.append(pltpu.make_async_copy(stg_ref.at[slot, :, pl.ds(col, w)], outs[a].at[:, pl.ds(c0, w)], osem.at[slot, n]))
                col += w
            return cps

        def put_pair(case, slot):
            a, c0, w = PAIR_DST[case]
            return pltpu.make_async_copy(pstg_ref.at[slot], outs[a].at[:, pl.ds(c0, w)], osem.at[slot, 2])

        def project(first_row, width, dst, slot):
            wrows = pl.ds(pl.multiple_of(first_row, 128), width)

            def blk(i, carry):
                rr = pl.ds(pl.multiple_of(i * ch, ch), ch)
                dst[slot, rr, :] = lax.dot_general(h_ref[rr, :], w4_ref[wrows, :], (((1,), (1,)), ((), ())),
                                                   preferred_element_type=F32)
                return carry

            lax.fori_loop(0, t // ch, blk, 0)

        def phase(p, s, pair):
            slot = p % 2
            if p >= 2:
                for case in range(N_CHIPS):
                    @pl.when(order[p - 2] == case)
                    def _():
                        for cp in put_main(case, slot):
                            cp.wait()
            if p == 3:
                for case in range(2):
                    @pl.when(j // 2 == case)
                    def _():
                        put_pair(case, 0).wait()
            project(2 * IN_HALF * s + 64 * (s % 2), MAIN_W, stg_ref, slot)
            for case in range(N_CHIPS):
                @pl.when(s == case)
                def _():
                    for cp in put_main(case, slot):
                        cp.start()
            if pair is not None:
                project(MAIN_W + 2 * (2 * IN_HALF) * pair, 128, pstg_ref, slot % 2 if p == 2 else 1)
                for case in range(2):
                    @pl.when(pair == case)
                    def _():
                        put_pair(case, 0 if p == 2 else 1).start()

        order = [j] + idx
        w_out = [pltpu.make_async_copy(w4_ref.at[pl.ds(pl.multiple_of(2 * IN_HALF * s, 32), 2 * IN_HALF)],
                                       w4_hbm.at[pl.ds(pl.multiple_of(2 * IN_HALF * s, 32), 2 * IN_HALF)], wsem.at[p])
                 for p, s in enumerate(order)]
        w_out[0].start()
        phase(0, j, None)
        passed = []
        for k in range(3):
            jk = idx[k]
            rc(7 + k, w4_ref.at[rows_of(jk, c)], w4_ref.at[rows_of(jk, c)], sib).wait_recv()
            passed.append(rc(10 + k, w4_ref.at[rows_of(jk, c)], w4_ref.at[rows_of(jk, c)], sib))
            passed[-1].start()
            rc(10 + k, w4_ref.at[rows_of(jk, 1 - c)], w4_ref.at[rows_of(jk, 1 - c)], sib).wait_recv()
            w_out[1 + k].start()
            if k == 0:
                phase(1, jk, None)
            elif k == 1:
                phase(2, jk, j // 2)
            else:
                phase(3, jk, 1 - j // 2)

        for case in range(N_CHIPS):
            for p in (2, 3):
                @pl.when(order[p] == case)
                def _():
                    for cp in put_main(case, p % 2):
                        cp.wait()
        for case in range(2):
            @pl.when(1 - j // 2 == case)
            def _():
                put_pair(case, 1).wait()
        h_copy.wait()
        for cp in w_out:
            cp.wait()
        for cp in sends + passed:
            cp.wait_send()

    vm = pl.BlockSpec(memory_space=pltpu.VMEM)
    hbm = pl.BlockSpec(memory_space=pl.ANY)
    widths = (512, 256, 512, 512, 512, 512)
    return pl.pallas_call(
        body,
        name="in_proj",
        in_specs=[hbm, vm, vm, vm, vm, vm],
        out_specs=[hbm] * 8 + [vm, vm],
        out_shape=[jax.ShapeDtypeStruct((t, w), F32) for w in widths]
        + [jax.ShapeDtypeStruct((t, D_MODEL), BF16), jax.ShapeDtypeStruct((IN_W, D_MODEL), BF16),
           jax.ShapeDtypeStruct((N_DEV, 1, D_MODEL), F32), jax.ShapeDtypeStruct((N_CHIPS, 1, ADA_SHARD), F32)],
        scratch_shapes=[pltpu.VMEM((t, D_MODEL), F32), pltpu.VMEM((t, D_MODEL), BF16), pltpu.VMEM((IN_W, D_MODEL), BF16),
                        pltpu.VMEM((2, t, MAIN_W), F32), pltpu.VMEM((2, t, 128), F32), pltpu.VMEM((N_DEV, 1, ADA_SHARD), F32),
                        pltpu.SemaphoreType.DMA((2,)), pltpu.SemaphoreType.DMA((2, 3)), pltpu.SemaphoreType.DMA((N_CHIPS,)),
                        pltpu.SemaphoreType.DMA((n_sem,)), pltpu.SemaphoreType.DMA((n_sem,))],
        compiler_params=_cparams(),
    )(x, wt, c_row, w_ada, b_sh, nw)


def _adamw_math(w, g, m, v):
    m2 = ADAM_B1 * m + (1.0 - ADAM_B1) * g
    v2 = ADAM_B2 * v + (1.0 - ADAM_B2) * (g * g)
    m_hat = m2 / (1.0 - ADAM_B1 ** ADAM_STEP)
    v_hat = v2 / (1.0 - ADAM_B2 ** ADAM_STEP)
    delta = -ADAM_LR * (m_hat / (jnp.sqrt(v_hat) + ADAM_EPS) + ADAM_WD * w)
    return delta, m2, v2


def _adamw(name, w, g, m, v, tm):
    r, cdim = w.shape

    def body(w_ref, g_ref, m_ref, v_ref, d_ref, m2_ref, v2_ref):
        d_ref[...], m2_ref[...], v2_ref[...] = _adamw_math(w_ref[...], g_ref[...], m_ref[...], v_ref[...])

    blk = pl.BlockSpec((tm, cdim), lambda i: (i, 0))
    return pl.pallas_call(
        body,
        name=name,
        grid=(r // tm,),
        in_specs=[blk] * 4,
        out_specs=[blk] * 3,
        out_shape=[jax.ShapeDtypeStruct((r, cdim), F32)] * 3,
        compiler_params=_cparams(dimension_semantics=("arbitrary",)),
    )(w, g, m, v)


def _adamw_ada(w, m, v, cact_t, dcols):
    r, cdim = w.shape
    tm = 256

    def body(w_ref, m_ref, v_ref, ct_ref, dc_ref, g_ref, d_ref, m2_ref, v2_ref):
        g = jnp.dot(ct_ref[...], dc_ref[...], preferred_element_type=F32, precision=lax.Precision.HIGHEST)
        g_ref[...] = g
        d_ref[...], m2_ref[...], v2_ref[...] = _adamw_math(w_ref[...], g, m_ref[...], v_ref[...])

    blk = pl.BlockSpec((tm, cdim), lambda i: (i, 0))
    return pl.pallas_call(
        body,
        name="adamw_w_ada",
        grid=(r // tm,),
        in_specs=[blk] * 3 + [pl.BlockSpec((tm, N_DEV), lambda i: (i, 0)), pl.BlockSpec((N_DEV, cdim), lambda i: (0, 0))],
        out_specs=[blk] * 4,
        out_shape=[jax.ShapeDtypeStruct((r, cdim), F32)] * 4,
        compiler_params=_cparams(dimension_semantics=("arbitrary",)),
    )(w, m, v, cact_t, dcols)


def _adamw_small(ws, gs, ms, vs):
    n = len(ws)

    def body(*refs):
        w_r, g_r, m_r, v_r = refs[0:n], refs[n:2 * n], refs[2 * n:3 * n], refs[3 * n:4 * n]
        d_r, m2_r, v2_r = refs[4 * n:5 * n], refs[5 * n:6 * n], refs[6 * n:7 * n]
        for i in range(n):
            d_r[i][...], m2_r[i][...], v2_r[i][...] = _adamw_math(w_r[i][...], g_r[i][...], m_r[i][...], v_r[i][...])

    vm = pl.BlockSpec(memory_space=pltpu.VMEM)
    shapes = [jax.ShapeDtypeStruct(w.shape, F32) for w in ws]
    out = pl.pallas_call(
        body,
        name="adamw_small",
        in_specs=[vm] * (4 * n),
        out_specs=[vm] * (3 * n),
        out_shape=shapes * 3,
        compiler_params=_cparams(),
    )(*ws, *gs, *ms, *vs)
    return out[0:n], out[n:2 * n], out[2 * n:3 * n]


def _rope_tables(t):
    inv = ROPE_THETA ** (-jnp.arange(0, HEAD_DIM, 2, dtype=F32) / HEAD_DIM)
    ang = jnp.arange(t, dtype=F32)[:, None] * inv[None, :]
    cos, sin = jnp.cos(ang), jnp.sin(ang)
    return jnp.tile(cos, (1, 4)), jnp.tile(jnp.concatenate([-sin, sin], axis=1), (1, 2))


def _pad_lanes(v, width):
    return jnp.pad(v, ((0, 0), (0, width - v.shape[1])))


def kernel(x, c, w_ada, b_ada, norm_w, w_in, q_norm_w, k_norm_w, sinks, conv_w, conv_b, ln_w, ln_b, w_out, loss_target, m_w_ada, m_b_ada, m_norm_w, m_w_in, m_q_norm_w, m_k_norm_w, m_sinks, m_conv_w, m_conv_b, m_ln_w, m_ln_b, m_w_out, v_w_ada, v_b_ada, v_norm_w, v_w_in, v_q_norm_w, v_k_norm_w, v_sinks, v_conv_w, v_conv_b, v_ln_w, v_ln_b, v_w_out):
    xi, yi = lax.axis_index("x"), lax.axis_index("y")
    j = 2 * xi + yi
    x2, tgt = x[0], loss_target[0]
    t = x2.shape[0]

    wt_s, mt_s, vt_s = w_in[0].T, m_w_in[0].T, v_w_in[0].T
    cw_pad = jnp.pad(conv_w[0], ((0, 1), (0, 0)))
    b_sh = lax.dynamic_slice(b_ada, (0, ADA_SHARD * j), (1, ADA_SHARD))

    q_raw, kv_raw, ga, ua, ug, gb, h, w_full, call, ada4 = _in_proj_gather(
        x2, wt_s.reshape(2, IN_HALF, D_MODEL), c, w_ada[0], b_sh, norm_w)
    ada = ada4.reshape(1, 3 * D_MODEL)
    s1, gate = 1.0 + ada[:, D_MODEL:2 * D_MODEL], ada[:, 2 * D_MODEL:]

    cos_f, sin_s = _rope_tables(t)
    qw2, kw2 = jnp.tile(q_norm_w, (1, 2)), jnp.tile(k_norm_w, (1, 2))

    o, mix_a, wo4, cw4 = _attn_fwd(q_raw, kv_raw, ga, qw2, kw2, sinks, cos_f, sin_s,
                                   w_out[0].reshape(2, OUT_HALF, D_MODEL), cw_pad)
    w_out_full = wo4.reshape(D_MODEL, D_MODEL)
    cw_full = jnp.concatenate([cw4[i] for i in range(N_CHIPS)], axis=1)
    cz, mix_b = _conv_fwd(ua, ug, gb, cw_full, conv_b, ln_w, ln_b)
    dout, dmix_a, dmix_b, gwo_bf, red_o = _out_proj(mix_a, mix_b, x2, tgt, gate, w_out_full)

    dq, dkv, dga, sm_a, gwo = _attn_bwd(q_raw, kv_raw, ga, o, dmix_a, qw2, kw2, sinks, cos_f, sin_s,
                                        gwo_bf.reshape(N_CHIPS, 2, OUT_HALF, D_MODEL))
    dua, dug, dgb, dcw, dvec = _conv_bwd(ua, ug, gb, cz, dmix_b, cw_full, ln_w, ln_b)
    dparts = (dq, dkv, dga, dua, dug, dgb)

    misc = jnp.concatenate([dvec[2:3], sm_a[0:1], sm_a[1:2], sm_a[2:3], jnp.zeros((1, 128), F32)], axis=1)
    small0 = jnp.concatenate([
        dcw.reshape(16, D_MODEL),
        jnp.concatenate([dvec[0:2].reshape(1, D_MODEL), misc, red_o[1:2], jnp.zeros((13, D_MODEL), F32)], axis=0)], axis=0)
    grad_x, gw, ssum, rows = _in_proj_bwd(dparts, h, x2, dout, s1, norm_w, w_full, small0, red_o)

    loss = (0.5 / D_MODEL) * jnp.sum(ssum[18])
    gt_w_in = gw.reshape(2 * IN_HALF, D_MODEL)
    g_w_out = gwo.reshape(D_MODEL // N_CHIPS, D_MODEL)
    g_conv_w = lax.dynamic_slice(ssum[0:16].reshape(32, CONV_W), (0, 128 * j), (CONV_TAPS, 128))
    g_conv_b, g_ln_w, g_ln_b = ssum[16:17, 0:CONV_W], ssum[16:17, CONV_W:], ssum[17:18, 0:CONV_W]
    g_qw, g_kw, g_sinks = ssum[17:18, 512:512 + HEAD_DIM], ssum[17:18, 640:640 + HEAD_DIM], ssum[17:18, 768:776]
    rsum = rows[0]
    for d in range(1, N_DEV):
        rsum = rsum + rows[d]
    g_norm_w = rsum[4:5]
    g_b_ada = jnp.concatenate([rsum[2:3], rsum[3:4], rsum[0:1]], axis=1)
    d_ada_all = jnp.concatenate([rows[:, 2], rows[:, 3], rows[:, 0]], axis=1)
    dcols = lax.dynamic_slice(d_ada_all, (0, ADA_SHARD * j), (N_DEV, ADA_SHARD))
    cact_t = jax.nn.silu(call.reshape(N_DEV, D_MODEL)).T

    g_w_ada, d_w_ada, nm_w_ada, nv_w_ada = _adamw_ada(w_ada[0], m_w_ada[0], v_w_ada[0], cact_t, dcols)
    dt_w_in, nmt_w_in, nvt_w_in = _adamw("adamw_w_in", wt_s, gt_w_in, mt_s, vt_s, 176)
    g_w_in, d_w_in, nm_w_in, nv_w_in = gt_w_in.T, dt_w_in.T, nmt_w_in.T, nvt_w_in.T
    d_w_out, nm_w_out, nv_w_out = _adamw("adamw_w_out", w_out[0], g_w_out, m_w_out[0], v_w_out[0], 128)
    ws = [b_ada, norm_w, q_norm_w, k_norm_w, sinks, conv_w[0], conv_b, ln_w, ln_b]
    gs = [g_b_ada, g_norm_w, g_qw, g_kw, g_sinks, g_conv_w, g_conv_b, g_ln_w, g_ln_b]
    ms = [m_b_ada, m_norm_w, m_q_norm_w, m_k_norm_w, m_sinks, m_conv_w[0], m_conv_b, m_ln_w, m_ln_b]
    vs = [v_b_ada, v_norm_w, v_q_norm_w, v_k_norm_w, v_sinks, v_conv_w[0], v_conv_b, v_ln_w, v_ln_b]
    ds, nms, nvs = _adamw_small(ws, gs, ms, vs)

    def order(ada_v, in_v, out_v, sm):
        b, nw_, qw_, kw_, sk_, cw_, cb_, lw_, lb_ = sm
        return [ada_v[None], b, nw_, in_v[None], qw_, kw_, sk_, cw_[None], cb_, lw_, lb_, out_v[None]]

    grads = order(g_w_ada, g_w_in, g_w_out, gs)
    deltas = order(d_w_ada, d_w_in, d_w_out, ds)
    new_m = order(nm_w_ada, nm_w_in, nm_w_out, nms)
    new_v = order(nv_w_ada, nv_w_in, nv_w_out, nvs)
    return (loss, grad_x[None], *grads, *deltas, *new_m, *new_v)
```

```python
import functools

import jax
import jax.numpy as jnp
from jax import lax
from jax.experimental import pallas as pl
from jax.experimental.pallas import tpu as pltpu

F32 = jnp.float32
BF16 = jnp.bfloat16

D_MODEL = 1024
ATTN_W = 512
KV_W = 128
CONV_W = 512
IN_W = 2816
HEAD_DIM = 64
CONV_TAPS = 31
QBLK = 128
EPS = 1e-6
ROPE_THETA = 10000.0

ADAM_LR = 0.001
ADAM_B1 = 0.9
ADAM_B2 = 0.999
ADAM_EPS = 1e-08
ADAM_WD = 0.01
ADAM_STEP = 10

N_CHIPS = 4
N_DEV = 8
IN_HALF = IN_W // N_CHIPS // 2
OUT_HALF = D_MODEL // N_CHIPS // 2
ADA_SHARD = 3 * D_MODEL // N_CHIPS

VMEM_LIMIT = 56 * 1024 * 1024
CONV_PAD = 32


def _cparams(**kw):
    return pltpu.CompilerParams(vmem_limit_bytes=VMEM_LIMIT, **kw)


def _sigmoid(v):
    return 1.0 / (1.0 + jnp.exp(-v))


def _silu(v):
    return v * _sigmoid(v)


def _dsilu(v):
    s = _sigmoid(v)
    return s * (1.0 + v * (1.0 - s))


def _lane(shape):
    return lax.broadcasted_iota(jnp.int32, shape, len(shape) - 1)


PUT_ROWS = 512


def _fetch(hbm_refs, vmem_refs, sem):
    cps = [pltpu.make_async_copy(h, v, sem.at[i]) for i, (h, v) in enumerate(zip(hbm_refs, vmem_refs))]
    for cp in cps:
        cp.start()
    return cps


def _put(vmem_ref, hbm_ref, sem, m):
    r = pl.ds(pl.multiple_of(m * PUT_ROWS, PUT_ROWS), PUT_ROWS)
    return pltpu.make_async_copy(vmem_ref.at[r], hbm_ref.at[r], sem.at[m])


def _put_all(pairs, sems, m):
    for (v, h), sem in zip(pairs, sems):
        _put(v, h, sem, m).start()


def _put_wait(pairs, sems, n):
    for (v, h), sem in zip(pairs, sems):
        for m in range(n):
            _put(v, h, sem, m).wait()


def _head_mean(s, left):
    sl = jnp.sum(jnp.where(left, s, 0.0), axis=-1, keepdims=True)
    sr = jnp.sum(jnp.where(left, 0.0, s), axis=-1, keepdims=True)
    return jnp.where(left, sl, sr) * (1.0 / HEAD_DIM)


def _rot(v, first):
    return jnp.where(first, pltpu.roll(v, 96, 1), pltpu.roll(v, 32, 1))


def _norm_rope(v, w, cos, sin_s, left, first):
    r = lax.rsqrt(_head_mean(v * v, left) + EPS)
    xh = v * r
    n = xh * w
    return n * cos + _rot(n, first) * sin_s, xh, r


def _norm_rope_bwd(d, xh, r, w, cos, sin_s, left, first):
    dn = d * cos - _rot(d, first) * sin_s
    dw = jnp.sum(dn * xh, axis=0, keepdims=True)
    dxh = dn * w
    return r * (dxh - xh * _head_mean(dxh * xh, left)), dw


def _dup_heads(v, left):
    sw = pltpu.roll(v, 64, 1)
    return jnp.where(left, v, sw), jnp.where(left, sw, v)


def _prep_kv(kv_ref, kw_ref, cos_ref, sin_ref, ka_ref, va_ref, t):
    ch = 256
    for g in range(2):
        ka_ref[g, 0:QBLK, :] = jnp.zeros((QBLK, 128), BF16)
        va_ref[g, 0:QBLK, :] = jnp.zeros((QBLK, 128), BF16)

    def chunk(i, carry):
        r0 = pl.multiple_of(i * ch, ch)
        left = _lane((ch, 128)) < 64
        first = (_lane((ch, 128)) % 64) < 32
        k = kv_ref[pl.ds(r0, ch), 0:128]
        v = kv_ref[pl.ds(r0, ch), 128:256]
        kr, _, _ = _norm_rope(k, kw_ref[...], cos_ref[pl.ds(r0, ch), :], sin_ref[pl.ds(r0, ch), :], left, first)
        k0, k1 = _dup_heads(kr, left)
        v0, v1 = _dup_heads(v, left)
        ka_ref[0, pl.ds(QBLK + r0, ch), :] = k0.astype(BF16)
        ka_ref[1, pl.ds(QBLK + r0, ch), :] = k1.astype(BF16)
        va_ref[0, pl.ds(QBLK + r0, ch), :] = v0.astype(BF16)
        va_ref[1, pl.ds(QBLK + r0, ch), :] = v1.astype(BF16)
        return carry

    lax.fori_loop(0, t // ch, chunk, 0)


def _band_mask(n):
    qi = lax.broadcasted_iota(jnp.int32, (2 * QBLK, 2 * QBLK), 0) % QBLK
    kj = lax.broadcasted_iota(jnp.int32, (2 * QBLK, 2 * QBLK), 1)
    local = (kj > qi) & (kj <= qi + QBLK)
    return local & ((n > 0) | (kj >= QBLK))


def _softmax_pair(s, mask, sink0, sink1):
    row = lax.broadcasted_iota(jnp.int32, (2 * QBLK, 1), 0)
    sink = jnp.where(row < QBLK, sink0, sink1)
    s = jnp.where(mask, s, -jnp.inf)
    m = jnp.maximum(jnp.max(s, axis=-1, keepdims=True), sink)
    e = jnp.exp(s - m)
    es = jnp.exp(sink - m)
    inv = 1.0 / (jnp.sum(e, axis=-1, keepdims=True) + es)
    return e * inv, es * inv


def _stack_heads(v, left):
    return jnp.concatenate([jnp.where(left, v, 0.0), jnp.where(left, 0.0, v)], axis=0)


def _attn_fwd(q_raw, kv_raw, ga, qw2, kw2, sinks, cos_f, sin_s, wo, cw):
    t = q_raw.shape[0]
    nblk = t // QBLK
    per_put = PUT_ROWS // QBLK

    def body(q_hbm, kv_ref, ga_hbm, qw_ref, kw_ref, sk_ref, cos_ref, sin_ref, wo_ref, cw_ref,
             o_hbm, mix_hbm, wo4_ref, cw4_ref, ka_ref, va_ref, q_ref, ga_ref, o_ref, mix_ref, isem, osem0, osem1,
             ssem, rsem):
        loads = _fetch((q_hbm, ga_hbm), (q_ref, ga_ref), isem)
        outs, osems = ((o_ref, o_hbm), (mix_ref, mix_hbm)), (osem0, osem1)
        x, y, c, chips = _place()
        j = 2 * x + y
        sib = (x, y, 1 - c)
        idx = [2 * cx + cy for cx, cy in chips]
        rc = functools.partial(_remote, ssem, rsem)
        wo4_ref[j] = wo_ref[...].astype(BF16)
        cw4_ref[j] = cw_ref[...]
        sends = []
        for k, chip in enumerate(chips):
            sends.append(rc(k, wo4_ref.at[j, c], wo4_ref.at[j, c], (*chip, c)))
            sends.append(rc(6 + k, cw4_ref.at[j], cw4_ref.at[j], (*chip, c)))
        for cp in sends:
            cp.start()

        _prep_kv(kv_ref, kw_ref, cos_ref, sin_ref, ka_ref, va_ref, t)
        for cp in loads:
            cp.wait()

        def blk(n, carry):
            r0 = pl.multiple_of(n * QBLK, QBLK)
            left = _lane((QBLK, 128)) < 64
            first = (_lane((QBLK, 128)) % 64) < 32
            cos = cos_ref[pl.ds(r0, QBLK), :]
            sin = sin_ref[pl.ds(r0, QBLK), :]
            mask = _band_mask(n)
            for p in range(4):
                g = p // 2
                lanes = slice(p * 128, (p + 1) * 128)
                qr, _, _ = _norm_rope(q_ref[pl.ds(r0, QBLK), lanes], qw_ref[...], cos, sin, left, first)
                q2 = _stack_heads(qr * 0.125, left).astype(BF16)
                s = lax.dot_general(q2, ka_ref[g, pl.ds(r0, 2 * QBLK), :], (((1,), (1,)), ((), ())),
                                    preferred_element_type=F32)
                pm, _ = _softmax_pair(s, mask, sk_ref[0, 2 * p], sk_ref[0, 2 * p + 1])
                o2 = jnp.dot(pm.astype(BF16), va_ref[g, pl.ds(r0, 2 * QBLK), :], preferred_element_type=F32)
                o = jnp.where(left, o2[0:QBLK], o2[QBLK:2 * QBLK])
                o_ref[pl.ds(r0, QBLK), lanes] = o.astype(BF16)
                mix_ref[pl.ds(r0, QBLK), lanes] = (o * _silu(ga_ref[pl.ds(r0, QBLK), lanes])).astype(BF16)

            @pl.when(n % per_put == per_put - 1)
            def _():
                _put_all(outs, osems, n // per_put)

            return carry

        lax.fori_loop(0, nblk, blk, 0)
        _put_wait(outs, osems, t // PUT_ROWS)

        passed = []
        for k, chip in enumerate(chips):
            jk = idx[k]
            rc(k, wo4_ref.at[jk, c], wo4_ref.at[jk, c], sib).wait_recv()
            passed.append(rc(3 + k, wo4_ref.at[jk, c], wo4_ref.at[jk, c], sib))
            passed[-1].start()
        for k, chip in enumerate(chips):
            jk = idx[k]
            rc(3 + k, wo4_ref.at[jk, 1 - c], wo4_ref.at[jk, 1 - c], sib).wait_recv()
            rc(6 + k, cw4_ref.at[jk], cw4_ref.at[jk], sib).wait_recv()
        for cp in sends + passed:
            cp.wait_send()

    vm = pl.BlockSpec(memory_space=pltpu.VMEM)
    hbm = pl.BlockSpec(memory_space=pl.ANY)
    n_sem = 9
    return pl.pallas_call(
        body,
        name="attn_fwd",
        in_specs=[hbm, vm, hbm, vm, vm, pl.BlockSpec(memory_space=pltpu.SMEM), vm, vm, vm, vm],
        out_specs=[hbm, hbm, vm, vm],
        out_shape=[jax.ShapeDtypeStruct((t, ATTN_W), BF16), jax.ShapeDtypeStruct((t, ATTN_W), BF16),
                   jax.ShapeDtypeStruct((N_CHIPS, 2, OUT_HALF, D_MODEL), BF16),
                   jax.ShapeDtypeStruct((N_CHIPS, 32, 128), F32)],
        scratch_shapes=[pltpu.VMEM((2, t + QBLK, 128), BF16), pltpu.VMEM((2, t + QBLK, 128), BF16),
                        pltpu.VMEM((t, ATTN_W), F32), pltpu.VMEM((t, ATTN_W), F32),
                        pltpu.VMEM((t, ATTN_W), BF16), pltpu.VMEM((t, ATTN_W), BF16),
                        pltpu.SemaphoreType.DMA((2,)), pltpu.SemaphoreType.DMA((t // PUT_ROWS,)),
                        pltpu.SemaphoreType.DMA((t // PUT_ROWS,)),
                        pltpu.SemaphoreType.DMA((n_sem,)), pltpu.SemaphoreType.DMA((n_sem,))],
        compiler_params=_cparams(),
    )(q_raw, kv_raw, ga, qw2, kw2, sinks, cos_f, sin_s, wo, cw)


def _attn_bwd(q_raw, kv_raw, ga, o, dmix, qw2, kw2, sinks, cos_f, sin_s, go):
    t = q_raw.shape[0]
    nblk = t // QBLK
    per_put = PUT_ROWS // QBLK

    def body(q_hbm, kv_ref, ga_hbm, o_hbm, dm_hbm, qw_ref, kw_ref, sk_ref, cos_ref, sin_ref, go_ref,
             dq_hbm, dkv_ref, dga_hbm, sm_ref, gwo_ref, ka_ref, va_ref, dka_ref, dva_ref,
             sibo_ref, outo_ref, ino_ref, q_ref, ga_ref, o_ref, dm_ref, dq_ref, dga_ref, isem, osem0, osem1, ssem, rsem):
        loads = _fetch((q_hbm, ga_hbm, o_hbm, dm_hbm), (q_ref, ga_ref, o_ref, dm_ref), isem)
        outs, osems = ((dq_ref, dq_hbm), (dga_ref, dga_hbm)), (osem0, osem1)
        x, y, c, chips = _place()
        sib = (x, y, 1 - c)
        rc = functools.partial(_remote, ssem, rsem)
        theirs, mine = go_ref.at[:, 1 - c], go_ref.at[:, c]
        sends = [_rs_to_sibling(rc, 0, theirs, sibo_ref, sib)]
        _prep_kv(kv_ref, kw_ref, cos_ref, sin_ref, ka_ref, va_ref, t)
        dka_ref[...] = jnp.zeros_like(dka_ref)
        dva_ref[...] = jnp.zeros_like(dva_ref)
        sends += _rs_trade(rc, 0, theirs, mine, sibo_ref, outo_ref, ino_ref, OUT_HALF, c, sib, chips)
        for cp in loads:
            cp.wait()

        def blk(n, carry):
            dqw, dsk = carry
            r0 = pl.multiple_of(n * QBLK, QBLK)
            left = _lane((QBLK, 128)) < 64
            first = (_lane((QBLK, 128)) % 64) < 32
            cos = cos_ref[pl.ds(r0, QBLK), :]
            sin = sin_ref[pl.ds(r0, QBLK), :]
            mask = _band_mask(n)
            row = lax.broadcasted_iota(jnp.int32, (2 * QBLK, 1), 0)
            for p in range(4):
                g = p // 2
                lanes = slice(p * 128, (p + 1) * 128)
                rows = pl.ds(r0, QBLK)
                win = pl.ds(r0, 2 * QBLK)
                qr, xh, r = _norm_rope(q_ref[rows, lanes], qw_ref[...], cos, sin, left, first)
                q2 = _stack_heads(qr * 0.125, left).astype(BF16)
                kwin = ka_ref[g, win, :]
                vwin = va_ref[g, win, :]
                s = lax.dot_general(q2, kwin, (((1,), (1,)), ((), ())), preferred_element_type=F32)
                pm, ps = _softmax_pair(s, mask, sk_ref[0, 2 * p], sk_ref[0, 2 * p + 1])
                gav = ga_ref[rows, lanes]
                dmv = dm_ref[rows, lanes].astype(F32)
                dga_ref[rows, lanes] = (dmv * o_ref[rows, lanes].astype(F32) * _dsilu(gav)).astype(BF16)
                do2 = _stack_heads(dmv * _silu(gav), left).astype(BF16)
                dp = lax.dot_general(do2, vwin, (((1,), (1,)), ((), ())), preferred_element_type=F32)
                delta = jnp.sum(pm * dp, axis=-1, keepdims=True)
                ds = (pm * (dp - delta)).astype(BF16)
                pd = ps * delta
                d0 = jnp.sum(jnp.where(row < QBLK, pd, 0.0), axis=0, keepdims=True)
                d1 = jnp.sum(jnp.where(row < QBLK, 0.0, pd), axis=0, keepdims=True)
                l8 = _lane((1, 128))
                dsk = dsk - jnp.where(l8 == 2 * p, d0, 0.0) - jnp.where(l8 == 2 * p + 1, d1, 0.0)
                dva_ref[g, win, :] += lax.dot_general(pm.astype(BF16), do2, (((0,), (0,)), ((), ())),
                                                      preferred_element_type=F32)
                dka_ref[g, win, :] += lax.dot_general(ds, q2, (((0,), (0,)), ((), ())),
                                                      preferred_element_type=F32)
                dq2 = jnp.dot(ds, kwin, preferred_element_type=F32)
                dqr = jnp.where(left, dq2[0:QBLK], dq2[QBLK:2 * QBLK]) * 0.125
                dq, dw = _norm_rope_bwd(dqr, xh, r, qw_ref[...], cos, sin, left, first)
                dq_ref[rows, lanes] = dq.astype(BF16)
                dqw = dqw + dw

            @pl.when(n % per_put == per_put - 1)
            def _():
                _put_all(outs, osems, n // per_put)

            return dqw, dsk

        zero = jnp.zeros((1, 128), F32)
        dqw, dsk = lax.fori_loop(0, nblk, blk, (zero, zero))

        ch = 256

        def chunk(i, dkw):
            r0 = pl.multiple_of(i * ch, ch)
            left = _lane((ch, 128)) < 64
            first = (_lane((ch, 128)) % 64) < 32
            rows = pl.ds(r0, ch)
            prow = pl.ds(QBLK + r0, ch)

            def fold(ref):
                a0 = ref[0, prow, :]
                a1 = ref[1, prow, :]
                return jnp.where(left, a0 + pltpu.roll(a0, 64, 1), a1 + pltpu.roll(a1, 64, 1))

            cos = cos_ref[rows, :]
            sin = sin_ref[rows, :]
            _, xh, r = _norm_rope(kv_ref[rows, 0:128], kw_ref[...], cos, sin, left, first)
            dk, dw = _norm_rope_bwd(fold(dka_ref), xh, r, kw_ref[...], cos, sin, left, first)
            dkv_ref[rows, 0:128] = dk.astype(BF16)
            dkv_ref[rows, 128:256] = fold(dva_ref).astype(BF16)
            return dkw + dw

        dkw = lax.fori_loop(0, t // ch, chunk, zero)
        sm_ref[...] = jnp.zeros((8, 128), F32)
        sm_ref[0:1, :] = dqw + pltpu.roll(dqw, 64, 1)
        sm_ref[1:2, :] = dkw + pltpu.roll(dkw, 64, 1)
        sm_ref[2:3, :] = dsk

        j = 2 * x + y
        sends.append(_rs_total(rc, 0, mine, sibo_ref, outo_ref, ino_ref, gwo_ref, OUT_HALF, j, c, sib))
        _rs_done(rc, 0, gwo_ref, c, sib)
        for cp in sends:
            cp.wait_send()
        _put_wait(outs, osems, t // PUT_ROWS)

    vm = pl.BlockSpec(memory_space=pltpu.VMEM)
    hbm = pl.BlockSpec(memory_space=pl.ANY)
    return pl.pallas_call(
        body,
        name="attn_bwd",
        in_specs=[hbm, vm, hbm, hbm, hbm, vm, vm, pl.BlockSpec(memory_space=pltpu.SMEM), vm, vm, vm],
        out_specs=[hbm, vm, hbm, vm, vm],
        out_shape=[jax.ShapeDtypeStruct((t, ATTN_W), BF16), jax.ShapeDtypeStruct((t, 2 * KV_W), BF16),
                   jax.ShapeDtypeStruct((t, ATTN_W), BF16), jax.ShapeDtypeStruct((8, 128), F32),
                   jax.ShapeDtypeStruct((2, OUT_HALF, D_MODEL), F32)],
        scratch_shapes=[pltpu.VMEM((2, t + QBLK, 128), BF16), pltpu.VMEM((2, t + QBLK, 128), BF16),
                        pltpu.VMEM((2, t + QBLK, 128), F32), pltpu.VMEM((2, t + QBLK, 128), F32)]
        + _rs_scratch(OUT_HALF)
        + [pltpu.VMEM((t, ATTN_W), F32), pltpu.VMEM((t, ATTN_W), F32), pltpu.VMEM((t, ATTN_W), BF16),
           pltpu.VMEM((t, ATTN_W), BF16), pltpu.VMEM((t, ATTN_W), BF16), pltpu.VMEM((t, ATTN_W), BF16),
           pltpu.SemaphoreType.DMA((4,)), pltpu.SemaphoreType.DMA((t // PUT_ROWS,)), pltpu.SemaphoreType.DMA((t // PUT_ROWS,)),
           pltpu.SemaphoreType.DMA((RS_SEMS,)), pltpu.SemaphoreType.DMA((RS_SEMS,))],
        compiler_params=_cparams(),
    )(q_raw, kv_raw, ga, o, dmix, qw2, kw2, sinks, cos_f, sin_s, go)


CONV_CH = 256
CONV_SUB = 64
CONV_ACCS = 3


def _shifted_windows(src_ref, r0, sh_ref):
    rows = CONV_CH + CONV_PAD
    win = src_ref[pl.ds(r0, rows), :]
    for b in range(8):
        sh = win if b == 0 else pltpu.roll(win, rows - b, 0)
        for c in range(CONV_W // 128):
            sh_ref[b, c] = sh[:, c * 128:(c + 1) * 128]


def _conv_fwd(ua, ug, gb, cw, cb, lw, lb):
    t = ua.shape[0]

    def body(ua_hbm, ug_hbm, gb_hbm, cw_ref, cb_ref, lw_ref, lb_ref, cz_hbm, mix_hbm, zp_ref, sh_ref,
             ua_ref, ug_ref, gb_ref, cz_ref, mix_ref, isem, osem0, osem1):
        loads = _fetch((ua_hbm, ug_hbm, gb_hbm), (ua_ref, ug_ref, gb_ref), isem)
        outs, osems = ((cz_ref, cz_hbm), (mix_ref, mix_hbm)), (osem0, osem1)
        per_put = PUT_ROWS // CONV_CH
        zp_ref[0:CONV_PAD, :] = jnp.zeros((CONV_PAD, CONV_W), F32)
        loads[0].wait()
        loads[1].wait()

        def glu(i, carry):
            r0 = pl.multiple_of(i * CONV_CH, CONV_CH)
            rows = pl.ds(r0, CONV_CH)
            zp_ref[pl.ds(CONV_PAD + r0, CONV_CH), :] = ua_ref[rows, :] * _sigmoid(ug_ref[rows, :])
            return carry

        lax.fori_loop(0, t // CONV_CH, glu, 0)
        loads[2].wait()

        def chunk(i, carry):
            r0 = pl.multiple_of(i * CONV_CH, CONV_CH)
            _shifted_windows(zp_ref, r0, sh_ref)
            for c in range(CONV_W // 128):
                lanes = slice(c * 128, (c + 1) * 128)

                def sub(k, carry2):
                    b0 = pl.multiple_of(k * CONV_SUB, CONV_SUB)
                    acc = [jnp.broadcast_to(cb_ref[0:1, lanes], (CONV_SUB, 128))] + [None] * (CONV_ACCS - 1)
                    for j in range(CONV_TAPS):
                        off = j + CONV_PAD - (CONV_TAPS - 1)
                        term = sh_ref[off % 8, c, pl.ds(b0 + 8 * (off // 8), CONV_SUB), :] * cw_ref[j:j + 1, lanes]
                        acc[j % CONV_ACCS] = term if acc[j % CONV_ACCS] is None else acc[j % CONV_ACCS] + term
                    cz_ref[pl.ds(r0 + b0, CONV_SUB), lanes] = functools.reduce(lambda a, b: a + b, acc)
                    return carry2

                lax.fori_loop(0, CONV_CH // CONV_SUB, sub, 0)
            rows = pl.ds(r0, CONV_CH)
            cz = cz_ref[rows, :]
            mu = jnp.mean(cz, axis=-1, keepdims=True)
            xc = cz - mu
            rs = lax.rsqrt(jnp.mean(xc * xc, axis=-1, keepdims=True) + EPS)
            ln = xc * rs * lw_ref[...] + lb_ref[...]
            mix_ref[rows, :] = (_silu(ln) * _silu(gb_ref[rows, :])).astype(BF16)

            @pl.when(i % per_put == per_put - 1)
            def _():
                _put_all(outs, osems, i // per_put)

            return carry

        lax.fori_loop(0, t // CONV_CH, chunk, 0)
        _put_wait(outs, osems, t // PUT_ROWS)

    vm = pl.BlockSpec(memory_space=pltpu.VMEM)
    hbm = pl.BlockSpec(memory_space=pl.ANY)
    nput = t // PUT_ROWS
    return pl.pallas_call(
        body,
        name="conv_fwd",
        in_specs=[hbm] * 3 + [vm] * 4,
        out_specs=[hbm, hbm],
        out_shape=[jax.ShapeDtypeStruct((t, CONV_W), F32), jax.ShapeDtypeStruct((t, CONV_W), BF16)],
        scratch_shapes=[pltpu.VMEM((t + CONV_PAD, CONV_W), F32),
                        pltpu.VMEM((8, CONV_W // 128, CONV_CH + CONV_PAD, 128), F32),
                        pltpu.VMEM((t, CONV_W), F32), pltpu.VMEM((t, CONV_W), F32), pltpu.VMEM((t, CONV_W), F32),
                        pltpu.VMEM((t, CONV_W), F32), pltpu.VMEM((t, CONV_W), BF16),
                        pltpu.SemaphoreType.DMA((3,)), pltpu.SemaphoreType.DMA((nput,)), pltpu.SemaphoreType.DMA((nput,))],
        compiler_params=_cparams(),
    )(ua, ug, gb, cw, cb, lw, lb)


def _conv_bwd(ua, ug, gb, cz, dmix, cw, lw, lb):
    t = ua.shape[0]

    def body(ua_hbm, ug_hbm, gb_hbm, cz_hbm, dm_hbm, cw_ref, lw_ref, lb_ref,
             dua_hbm, dug_hbm, dgb_hbm, dcw_ref, dvec_ref, zp_ref, dp_ref, sh_ref, wacc_ref,
             ua_ref, ug_ref, gb_ref, cz_ref, dm_ref, dua_ref, dug_ref, dgb_ref, isem, osem0, osem1, osem2):
        loads = _fetch((ua_hbm, ug_hbm, gb_hbm, cz_hbm, dm_hbm), (ua_ref, ug_ref, gb_ref, cz_ref, dm_ref), isem)
        per_put = PUT_ROWS // CONV_CH
        zp_ref[0:CONV_PAD, :] = jnp.zeros((CONV_PAD, CONV_W), F32)
        dp_ref[t:t + CONV_PAD, :] = jnp.zeros((CONV_PAD, CONV_W), F32)
        wacc_ref[...] = jnp.zeros_like(wacc_ref)
        for cp in loads:
            cp.wait()

        def pointwise(i, carry):
            dcb, dlw, dlb = carry
            r0 = pl.multiple_of(i * CONV_CH, CONV_CH)
            rows = pl.ds(r0, CONV_CH)
            zp_ref[pl.ds(CONV_PAD + r0, CONV_CH), :] = ua_ref[rows, :] * _sigmoid(ug_ref[rows, :])
            cz = cz_ref[rows, :]
            mu = jnp.mean(cz, axis=-1, keepdims=True)
            xc = cz - mu
            rs = lax.rsqrt(jnp.mean(xc * xc, axis=-1, keepdims=True) + EPS)
            xh = xc * rs
            ln = xh * lw_ref[...] + lb_ref[...]
            gbv = gb_ref[rows, :]
            dy = dm_ref[rows, :].astype(F32)
            dgb_ref[rows, :] = (dy * _silu(ln) * _dsilu(gbv)).astype(BF16)
            dl = dy * _silu(gbv) * _dsilu(ln)
            dxh = dl * lw_ref[...]
            dcz = rs * (dxh - jnp.mean(dxh, axis=-1, keepdims=True)
                        - xh * jnp.mean(dxh * xh, axis=-1, keepdims=True))
            dp_ref[rows, :] = dcz

            @pl.when(i % per_put == per_put - 1)
            def _():
                _put(dgb_ref, dgb_hbm, osem2, i // per_put).start()

            return (dcb + jnp.sum(dcz, axis=0, keepdims=True),
                    dlw + jnp.sum(dl * xh, axis=0, keepdims=True),
                    dlb + jnp.sum(dl, axis=0, keepdims=True))

        zero = jnp.zeros((1, CONV_W), F32)
        dcb, dlw, dlb = lax.fori_loop(0, t // CONV_CH, pointwise, (zero, zero, zero))
        dvec_ref[...] = jnp.zeros((8, CONV_W), F32)
        dvec_ref[0:1, :] = dcb
        dvec_ref[1:2, :] = dlw
        dvec_ref[2:3, :] = dlb

        def chunk(i, carry):
            r0 = pl.multiple_of(i * CONV_CH, CONV_CH)
            _shifted_windows(dp_ref, r0, sh_ref)
            for c in range(CONV_W // 128):
                lanes = slice(c * 128, (c + 1) * 128)

                def sub(k, carry2):
                    b0 = pl.multiple_of(k * CONV_SUB, CONV_SUB)
                    acc = [None] * CONV_ACCS
                    for j in range(CONV_TAPS):
                        off = CONV_TAPS - 1 - j
                        term = sh_ref[off % 8, c, pl.ds(b0 + 8 * (off // 8), CONV_SUB), :] * cw_ref[j:j + 1, lanes]
                        acc[j % CONV_ACCS] = term if acc[j % CONV_ACCS] is None else acc[j % CONV_ACCS] + term
                    acc = functools.reduce(lambda a, b: a + b, acc)
                    rr = pl.ds(r0 + b0, CONV_SUB)
                    sg = _sigmoid(ug_ref[rr, lanes])
                    dua_ref[rr, lanes] = (acc * sg).astype(BF16)
                    dug_ref[rr, lanes] = (acc * ua_ref[rr, lanes] * sg * (1.0 - sg)).astype(BF16)
                    return carry2

                lax.fori_loop(0, CONV_CH // CONV_SUB, sub, 0)
            _shifted_windows(zp_ref, r0, sh_ref)
            for c in range(CONV_W // 128):
                lanes = slice(c * 128, (c + 1) * 128)

                def subw(k, carry2):
                    b0 = pl.multiple_of(k * CONV_SUB, CONV_SUB)
                    dcz = dp_ref[pl.ds(r0 + b0, CONV_SUB), lanes]
                    for j in range(CONV_TAPS):
                        off = j + CONV_PAD - (CONV_TAPS - 1)
                        pr = dcz * sh_ref[off % 8, c, pl.ds(b0 + 8 * (off // 8), CONV_SUB), :]
                        parts = [pr[8 * q:8 * (q + 1)] for q in range(CONV_SUB // 8)]
                        while len(parts) > 1:
                            parts = [a + b for a, b in zip(parts[0::2], parts[1::2])]
                        wacc_ref[8 * j:8 * (j + 1), lanes] += parts[0]
                    return carry2

                lax.fori_loop(0, CONV_CH // CONV_SUB, subw, 0)

            @pl.when(i % per_put == per_put - 1)
            def _():
                _put_all(((dua_ref, dua_hbm), (dug_ref, dug_hbm)), (osem0, osem1), i // per_put)

            return carry

        lax.fori_loop(0, t // CONV_CH, chunk, 0)
        _put_wait(((dua_ref, dua_hbm), (dug_ref, dug_hbm), (dgb_ref, dgb_hbm)), (osem0, osem1, osem2), t // PUT_ROWS)
        dcw_ref[...] = jnp.zeros((32, CONV_W), F32)
        for j in range(CONV_TAPS):
            dcw_ref[j:j + 1, :] = jnp.sum(wacc_ref[8 * j:8 * (j + 1), :], axis=0, keepdims=True)

    vm = pl.BlockSpec(memory_space=pltpu.VMEM)
    hbm = pl.BlockSpec(memory_space=pl.ANY)
    return pl.pallas_call(
        body,
        name="conv_bwd",
        in_specs=[hbm] * 5 + [vm] * 3,
        out_specs=[hbm] * 3 + [vm] * 2,
        out_shape=[jax.ShapeDtypeStruct((t, CONV_W), BF16)] * 3
        + [jax.ShapeDtypeStruct((32, CONV_W), F32), jax.ShapeDtypeStruct((8, CONV_W), F32)],
        scratch_shapes=[pltpu.VMEM((t + CONV_PAD, CONV_W), F32), pltpu.VMEM((t + CONV_PAD, CONV_W), F32),
                        pltpu.VMEM((8, CONV_W // 128, CONV_CH + CONV_PAD, 128), F32), pltpu.VMEM((8 * 32, CONV_W), F32)]
        + [pltpu.VMEM((t, CONV_W), F32)] * 4 + [pltpu.VMEM((t, CONV_W), BF16)] * 4
        + [pltpu.SemaphoreType.DMA((5,))] + [pltpu.SemaphoreType.DMA((t // PUT_ROWS,))] * 3,
        compiler_params=_cparams(),
    )(ua, ug, gb, cz, dmix, cw, lw, lb)


def _out_proj(mix_a, mix_b, x, tgt, gate, w_out):
    t = x.shape[0]
    tm = 512
    nstep = t // tm

    def body(ma_ref, mb_ref, x_ref, t_ref, g_ref, w_ref, dout_ref, dma_ref, dmb_ref, gw_ref, red_ref, acc_ref):
        i = pl.program_id(0)

        @pl.when(i == 0)
        def _():
            acc_ref[...] = jnp.zeros_like(acc_ref)
            red_ref[...] = jnp.zeros_like(red_ref)

        mix = jnp.concatenate([ma_ref[...], mb_ref[...]], axis=1)
        y = jnp.dot(mix, w_ref[...], preferred_element_type=F32)
        gate_v = g_ref[...]
        err = x_ref[...] + gate_v * y - t_ref[...]
        dout = err * (1.0 / D_MODEL)
        dout_ref[...] = dout
        red_ref[0:1, :] += jnp.sum(dout * y, axis=0, keepdims=True)
        red_ref[1:2, :] += jnp.sum(err * err, axis=0, keepdims=True)
        dy = (dout * gate_v).astype(BF16)
        dmix = lax.dot_general(dy, w_ref[...], (((1,), (1,)), ((), ())), preferred_element_type=F32)
        dma_ref[...] = dmix[:, 0:512].astype(BF16)
        dmb_ref[...] = dmix[:, 512:1024].astype(BF16)
        acc_ref[...] += lax.dot_general(mix, dy, (((0,), (0,)), ((), ())), preferred_element_type=F32)

        @pl.when(i == nstep - 1)
        def _():
            gw_ref[...] = acc_ref[...].astype(BF16)

    row = lambda w: pl.BlockSpec((tm, w), lambda i: (i, 0))
    const = lambda s: pl.BlockSpec(s, lambda i: (0, 0))
    return pl.pallas_call(
        body,
        name="out_proj",
        grid=(nstep,),
        in_specs=[row(512), row(512), row(D_MODEL), row(D_MODEL), const((1, D_MODEL)),
                  pl.BlockSpec((D_MODEL, D_MODEL), lambda i: (0, 0), pipeline_mode=pl.Buffered(1))],
        out_specs=[row(D_MODEL), row(512), row(512), const((D_MODEL, D_MODEL)), const((8, D_MODEL))],
        out_shape=[jax.ShapeDtypeStruct((t, D_MODEL), F32), jax.ShapeDtypeStruct((t, 512), BF16),
                   jax.ShapeDtypeStruct((t, 512), BF16), jax.ShapeDtypeStruct((D_MODEL, D_MODEL), BF16),
                   jax.ShapeDtypeStruct((8, D_MODEL), F32)],
        scratch_shapes=[pltpu.VMEM((D_MODEL, D_MODEL), F32)],
        compiler_params=_cparams(dimension_semantics=("arbitrary",)),
    )(mix_a, mix_b, x, tgt, gate, w_out)


DPROJ_WIDTHS = (512, 256, 512, 512, 512, 512)
DPROJ_STARTS = (0, 512, 768, 1280, 1792, 2304)
WIN_W = 768
WIN_START = (0, 640, 1408, 2048)
WIN_OFF = (0, 64, 0, 64)
N_GW = N_CHIPS


def _window_pieces(s):
    lo, hi = WIN_START[s], WIN_START[s] + WIN_W
    out = []
    for p, (st, w) in enumerate(zip(DPROJ_STARTS, DPROJ_WIDTHS)):
        a, b = max(lo, st), min(hi, st + w)
        if a < b:
            out.append((p, a - st, b - a, a - lo))
    return out


def _in_proj_bwd(dparts, h, x, dout, s1, nw, wt_full, small0, row0):
    t = x.shape[0]
    tm = 256
    nstep = N_GW + t // tm
    n_sem = 20
    rows0 = small0.shape[0]
    hs = rows0 // 2
    npart = len(DPROJ_WIDTHS)

    def body(*refs):
        d_hbm, d_ref = refs[:npart], refs[npart:2 * npart]
        (x_ref, dout_ref, s1_ref, nw_ref, h_ref, wt_hbm, sm0_ref, row0_ref,
         gx_ref, gw_hbm, ssum_ref, rows_ref,
         stg_ref, wt_ref, gt_ref, sib_ref, out_ref, in_ref, res_ref, sall_ref, red_ref, ssib_ref, schip_ref, sres_ref,
         wsem, lsem, ssem, rsem) = refs[2 * npart:]
        i = pl.program_id(0)
        x_, y_, c, chips = _place()
        j = 2 * x_ + y_
        dev = 2 * j + c
        sib = (x_, y_, 1 - c)
        rc = functools.partial(_remote, ssem, rsem)
        rel_chip = [2 * cx + cy for cx, cy in chips] + [j]
        peers = [(px, py, pc) for px in (x_, 1 - x_) for py in (y_, 1 - y_) for pc in (c, 1 - c)][1:]
        wt_copy = pltpu.make_async_copy(wt_hbm, wt_ref, lsem.at[0])

        def window(case, slot):
            return [pltpu.make_async_copy(d_hbm[p].at[:, pl.ds(c0, w)], stg_ref.at[slot, :, pl.ds(w0, w)], wsem.at[slot, n])
                    for n, (p, c0, w, w0) in enumerate(_window_pieces(case))]

        def to_sibling(k):
            return rc(k, gt_ref.at[k, 1 - c], sib_ref.at[k], sib)

        def to_chip(k):
            return rc(4 + k, out_ref.at[k], in_ref.at[k], (*chips[k], c))

        def trade(k):
            to_sibling(k).wait_recv()

            def add(n, carry):
                rr = pl.ds(pl.multiple_of(n * RS_CH, RS_CH), RS_CH)
                out_ref[k, rr, :] = (gt_ref[k, c, rr, :].astype(F32) + sib_ref[k, rr, :].astype(F32)).astype(BF16)
                return carry

            lax.fori_loop(0, IN_HALF // RS_CH, add, 0)
            to_chip(k).start()

        mine_s = pl.ds(pl.multiple_of(c * hs, 8), hs)
        other_s = pl.ds(pl.multiple_of((1 - c) * hs, 8), hs)

        def small_to_sibling():
            return rc(15, sm0_ref.at[other_s], ssib_ref, sib)

        def small_to_chip(k):
            return rc(16 + k, schip_ref.at[j], schip_ref.at[j], (*chips[k], c))

        def small_share():
            return rc(19, sres_ref.at[c], sres_ref.at[c], sib)

        for k in range(N_GW):
            @pl.when(i == k)
            def _(k=k):
                slot = k % 2
                if k == 0:
                    red_ref[...] = jnp.zeros_like(red_ref)
                    wt_copy.start()
                    small_to_sibling().start()
                if k == 1:
                    small_to_sibling().wait_recv()
                    schip_ref[j] = sm0_ref[mine_s, :] + ssib_ref[...]
                    for kk in range(3):
                        small_to_chip(kk).start()
                if k == N_GW - 1:
                    for kk in range(3):
                        jk = rel_chip[kk]
                        rc(16 + kk, schip_ref.at[jk], schip_ref.at[jk], sib).wait_recv()
                    tot = schip_ref[0]
                    for d in range(1, N_CHIPS):
                        tot = tot + schip_ref[d]
                    sres_ref[c] = tot
                    small_share().start()
                for case in range(N_CHIPS):
                    if k == 0:
                        @pl.when(rel_chip[0] == case)
                        def _():
                            for cp in window(case, 0):
                                cp.start()
                    if k + 1 < N_GW:
                        @pl.when(rel_chip[k + 1] == case)
                        def _():
                            for cp in window(case, 1 - slot):
                                cp.start()
                for case in range(N_CHIPS):
                    @pl.when(rel_chip[k] == case)
                    def _():
                        for cp in window(case, slot):
                            cp.wait()
                g = lax.dot_general(stg_ref[slot], h_ref[...], (((0,), (0,)), ((), ())), preferred_element_type=F32)
                for off in sorted(set(WIN_OFF)):
                    @pl.when(rel_chip[k] % 2 == (1 if off else 0))
                    def _():
                        gt_ref[k, 0] = g[off:off + IN_HALF].astype(BF16)
                        gt_ref[k, 1] = g[off + IN_HALF:off + 2 * IN_HALF].astype(BF16)
                to_sibling(k).start()
                if k >= 1:
                    trade(k - 1)

        @pl.when(i == N_GW)
        def _():
            wt_copy.wait()

        @pl.when(i >= N_GW)
        def _():
            xv = x_ref[...]
            r = lax.rsqrt(jnp.mean(xv * xv, axis=-1, keepdims=True) + EPS)
            xh = xv * r
            n = xh * nw_ref[...]
            dproj = jnp.concatenate([ref[...] for ref in d_ref], axis=1)
            dh = jnp.dot(dproj, wt_ref[...], preferred_element_type=F32)
            red_ref[0:1, :] += jnp.sum(dh, axis=0, keepdims=True)
            red_ref[1:2, :] += jnp.sum(dh * n, axis=0, keepdims=True)
            dn = dh * s1_ref[...]
            red_ref[2:3, :] += jnp.sum(dn * xh, axis=0, keepdims=True)
            dxh = dn * nw_ref[...]
            gx_ref[...] = dout_ref[...] + r * (dxh - xh * jnp.mean(dxh * xh, axis=-1, keepdims=True))

        @pl.when(i == nstep - 1)
        def _():
            sall_ref[dev] = row0_ref[...]
            sall_ref[dev, 2:5, :] = red_ref[0:3, :]
            sends = [rc(8 + k, sall_ref.at[dev], sall_ref.at[dev], peer) for k, peer in enumerate(peers)]
            for cp in sends:
                cp.start()
            sends += [to_sibling(k) for k in range(N_GW)] + [to_chip(k) for k in range(3)]
            sends += [small_to_sibling(), small_share()] + [small_to_chip(k) for k in range(3)]
            own = N_GW - 1
            to_sibling(own).wait_recv()
            for k in range(3):
                to_chip(k).wait_recv()

            def total(n, carry):
                rr = pl.ds(pl.multiple_of(n * RS_CH, RS_CH), RS_CH)
                acc = gt_ref[own, c, rr, :].astype(F32) + sib_ref[own, rr, :].astype(F32)
                for k in range(3):
                    acc = acc + in_ref[k, rr, :].astype(F32)
                res_ref[c, rr, :] = acc
                return carry

            lax.fori_loop(0, IN_HALF // RS_CH, total, 0)
            share = rc(7, res_ref.at[c], res_ref.at[c], sib)
            share.start()
            sends.append(share)
            for k, (px, py, pc) in enumerate(peers):
                pdev = 4 * px + 2 * py + pc
                rc(8 + k, sall_ref.at[pdev], sall_ref.at[pdev], (px, py, pc)).wait_recv()
            rows_ref[...] = sall_ref[...]
            rc(19, sres_ref.at[1 - c], sres_ref.at[1 - c], sib).wait_recv()
            ssum_ref[0:hs, :] = sres_ref[0]
            ssum_ref[hs:rows0, :] = sres_ref[1]
            rc(7, res_ref.at[1 - c], res_ref.at[1 - c], sib).wait_recv()
            back = pltpu.make_async_copy(res_ref, gw_hbm, lsem.at[1])
            back.start()
            for cp in sends:
                cp.wait_send()
            back.wait()

    blk = lambda i: jnp.maximum(i - N_GW, 0)
    row = lambda w: pl.BlockSpec((tm, w), lambda i: (blk(i), 0))
    vec = pl.BlockSpec((1, D_MODEL), lambda i: (0, 0))
    const = lambda shape: pl.BlockSpec(shape, lambda i: (0,) * len(shape))
    hbm = pl.BlockSpec(memory_space=pl.ANY)
    return pl.pallas_call(
        body,
        name="in_proj_bwd",
        grid=(nstep,),
        in_specs=[hbm] * npart + [row(w) for w in DPROJ_WIDTHS] + [row(D_MODEL), row(D_MODEL), vec, vec,
                  pl.BlockSpec((t, D_MODEL), lambda i: (0, 0), pipeline_mode=pl.Buffered(1)), hbm, const((rows0, D_MODEL)),
                  const((8, D_MODEL))],
        out_specs=[row(D_MODEL), hbm, const((rows0, D_MODEL)), const((N_DEV, 8, D_MODEL))],
        out_shape=[jax.ShapeDtypeStruct((t, D_MODEL), F32), jax.ShapeDtypeStruct((2, IN_HALF, D_MODEL), F32),
                   jax.ShapeDtypeStruct((rows0, D_MODEL), F32), jax.ShapeDtypeStruct((N_DEV, 8, D_MODEL), F32)],
        scratch_shapes=[pltpu.VMEM((2, t, WIN_W), BF16), pltpu.VMEM((IN_W, D_MODEL), BF16),
                        pltpu.VMEM((N_CHIPS, 2, IN_HALF, D_MODEL), BF16), pltpu.VMEM((N_CHIPS, IN_HALF, D_MODEL), BF16),
                        pltpu.VMEM((3, IN_HALF, D_MODEL), BF16), pltpu.VMEM((3, IN_HALF, D_MODEL), BF16),
                        pltpu.VMEM((2, IN_HALF, D_MODEL), F32), pltpu.VMEM((N_DEV, 8, D_MODEL), F32),
                        pltpu.VMEM((8, D_MODEL), F32), pltpu.VMEM((hs, D_MODEL), F32), pltpu.VMEM((N_CHIPS, hs, D_MODEL), F32),
                        pltpu.VMEM((2, hs, D_MODEL), F32), pltpu.SemaphoreType.DMA((2, 3)), pltpu.SemaphoreType.DMA((2,)),
                        pltpu.SemaphoreType.DMA((n_sem,)), pltpu.SemaphoreType.DMA((n_sem,))],
        compiler_params=_cparams(dimension_semantics=("arbitrary",)),
    )(*dparts, *dparts, x, dout, s1, nw, h, wt_full, small0, row0)


MESH = pl.DeviceIdType.MESH


def _place():
    x, y, c = lax.axis_index("x"), lax.axis_index("y"), lax.axis_index("c")
    chips = [(1 - x, y), (x, 1 - y), (1 - x, 1 - y)]
    return x, y, c, chips


def _remote(sems_s, sems_r, k, src, dst, to):
    return pltpu.make_async_remote_copy(src_ref=src, dst_ref=dst, send_sem=sems_s.at[k], recv_sem=sems_r.at[k],
                                        device_id=to, device_id_type=MESH)


RS_CH = 32
RS_SEMS = 5


def _rs_to_sibling(rc, s0, theirs, sib_ref, sib):
    cp = rc(s0, theirs, sib_ref, sib)
    cp.start()
    return cp


def _rs_trade(rc, s0, theirs, mine, sib_ref, out_ref, in_ref, rows, c, sib, chips):
    rc(s0, theirs, sib_ref, sib).wait_recv()
    cps = []
    for k, (cx, cy) in enumerate(chips):
        jk = 2 * cx + cy

        def add(i, carry, jk=jk, k=k):
            rr = pl.ds(pl.multiple_of(i * RS_CH, RS_CH), RS_CH)
            out_ref[k, rr, :] = (mine[jk, rr, :].astype(F32) + sib_ref[jk, rr, :].astype(F32)).astype(BF16)
            return carry

        lax.fori_loop(0, rows // RS_CH, add, 0)
        cps.append(rc(s0 + 1 + k, out_ref.at[k], in_ref.at[k], (cx, cy, c)))
        cps[-1].start()
    return cps


def _rs_total(rc, s0, mine, sib_ref, out_ref, in_ref, res_ref, rows, j, c, sib):
    for k in range(3):
        rc(s0 + 1 + k, out_ref.at[k], in_ref.at[k], sib).wait_recv()

    def total(i, carry):
        rr = pl.ds(pl.multiple_of(i * RS_CH, RS_CH), RS_CH)
        acc = mine[j, rr, :].astype(F32) + sib_ref[j, rr, :].astype(F32)
        for k in range(3):
            acc = acc + in_ref[k, rr, :].astype(F32)
        res_ref[c, rr, :] = acc
        return carry

    lax.fori_loop(0, rows // RS_CH, total, 0)
    cp = rc(s0 + 4, res_ref.at[c], res_ref.at[c], sib)
    cp.start()
    return cp


def _rs_done(rc, s0, res_ref, c, sib):
    rc(s0 + 4, res_ref.at[1 - c], res_ref.at[1 - c], sib).wait_recv()


def _rs_scratch(rows):
    return [pltpu.VMEM((N_CHIPS, rows, D_MODEL), BF16), pltpu.VMEM((3, rows, D_MODEL), BF16),
            pltpu.VMEM((3, rows, D_MODEL), BF16)]


MAIN_W = 640
MAIN_DST = (((0, 0, 512), (1, 0, 128)), ((2, 0, 512), (3, 0, 128)), ((3, 128, 384), (4, 0, 256)), ((4, 384, 128), (5, 0, 512)))
PAIR_DST = ((1, 128, 128), (4, 256, 128))


def _in_proj_gather(x, wt, c_row, w_ada, b_sh, nw):
    t = x.shape[0]
    ch = 256
    n_sem = 16

    def body(x_hbm, wt_ref, c_ref, wada_ref, bsh_ref, nw_ref,
             q_hbm, kv_hbm, ga_hbm, ua_hbm, ug_hbm, gb_hbm, h_hbm, w4_hbm, call_ref, ada_ref,
             x_ref, h_ref, w4_ref, stg_ref, pstg_ref, part_ref, lsem, osem, wsem, ssem, rsem):
        outs = (q_hbm, kv_hbm, ga_hbm, ua_hbm, ug_hbm, gb_hbm)
        x_, y_, c, chips = _place()
        j = 2 * x_ + y_
        dev = 2 * j + c
        sib = (x_, y_, 1 - c)
        idx = [2 * cx + cy for cx, cy in chips]
        rc = functools.partial(_remote, ssem, rsem)
        x_copy = pltpu.make_async_copy(x_hbm, x_ref, lsem.at[0])
        x_copy.start()

        def rows_of(s, cc):
            return pl.ds(pl.multiple_of(2 * IN_HALF * s + IN_HALF * cc, 16), IN_HALF)

        w4_ref[rows_of(j, 0), :] = wt_ref[0].astype(BF16)
        w4_ref[rows_of(j, 1), :] = wt_ref[1].astype(BF16)
        call_ref[dev] = c_ref[...]
        sends = []
        peers = [(px, py, pc) for px in (x_, 1 - x_) for py in (y_, 1 - y_) for pc in (c, 1 - c)][1:]
        for k, peer in enumerate(peers):
            sends.append(rc(k, call_ref.at[dev], call_ref.at[dev], peer))
        for cp in sends:
            cp.start()

        for k, (px, py, pc) in enumerate(peers):
            pdev = 4 * px + 2 * py + pc
            rc(k, call_ref.at[pdev], call_ref.at[pdev], (px, py, pc)).wait_recv()
        rowid = lax.broadcasted_iota(jnp.int32, (N_DEV, D_MODEL), 0)
        call = jnp.zeros((N_DEV, D_MODEL), F32)
        for r in range(N_DEV):
            call = jnp.where(rowid == r, jnp.broadcast_to(call_ref[r], (N_DEV, D_MODEL)), call)
        part = jnp.dot(_silu(call).astype(BF16), wada_ref[...].astype(BF16), preferred_element_type=F32) + bsh_ref[...]
        for r in range(N_DEV):
            part_ref[r] = part[r:r + 1, :]
        ada_ref[j] = part_ref[dev]
        for k, chip in enumerate(chips):
            sends.append(rc(13 + k, part_ref.at[2 * idx[k] + c], ada_ref.at[j], (*chip, c)))
            sends[-1].start()
        for k, chip in enumerate(chips):
            sends.append(rc(7 + k, w4_ref.at[rows_of(j, c)], w4_ref.at[rows_of(j, c)], (*chip, c)))
            sends[-1].start()
        for k in range(3):
            rc(13 + k, ada_ref.at[idx[k]], ada_ref.at[idx[k]], sib).wait_recv()

        shift = jnp.concatenate([ada_ref[0], ada_ref[1][:, 0:256]], axis=1)
        s1 = 1.0 + jnp.concatenate([ada_ref[1][:, 256:768], ada_ref[2][:, 0:512]], axis=1)
        x_copy.wait()

        def norm(i, carry):
            rr = pl.ds(pl.multiple_of(i * ch, ch), ch)
            xv = x_ref[rr, :]
            r = lax.rsqrt(jnp.mean(xv * xv, axis=-1, keepdims=True) + EPS)
            h_ref[rr, :] = ((xv * r) * nw_ref[...] * s1 + shift).astype(BF16)
            return carry

        lax.fori_loop(0, t // ch, norm, 0)
        h_copy = pltpu.make_async_copy(h_ref, h_hbm, lsem.at[1])
        h_copy.start()

        def put_main(case, slot):
            cps, col = [], 0
            for n, (a, c0, w) in enumerate(MAIN_DST[case]):
                cps.append(pltpu.make_async_copy(stg_ref.at[slot, :, pl.ds(col, w)], outs[a].at[:, pl.ds(c0, w)], osem.at[slot, n]))
                col += w
            return cps

        def put_pair(case, slot):
            a, c0, w = PAIR_DST[case]
            return pltpu.make_async_copy(pstg_ref.at[slot], outs[a].at[:, pl.ds(c0, w)], osem.at[slot, 2])

        def project(first_row, width, dst, slot):
            wrows = pl.ds(pl.multiple_of(first_row, 128), width)

            def blk(i, carry):
                rr = pl.ds(pl.multiple_of(i * ch, ch), ch)
                dst[slot, rr, :] = lax.dot_general(h_ref[rr, :], w4_ref[wrows, :], (((1,), (1,)), ((), ())),
                                                   preferred_element_type=F32)
                return carry

            lax.fori_loop(0, t // ch, blk, 0)

        def phase(p, s, pair):
            slot = p % 2
            if p >= 2:
                for case in range(N_CHIPS):
                    @pl.when(order[p - 2] == case)
                    def _():
                        for cp in put_main(case, slot):
                            cp.wait()
            if p == 3:
                for case in range(2):
                    @pl.when(j // 2 == case)
                    def _():
                        put_pair(case, 0).wait()
            project(2 * IN_HALF * s + 64 * (s % 2), MAIN_W, stg_ref, slot)
            for case in range(N_CHIPS):
                @pl.when(s == case)
                def _():
                    for cp in put_main(case, slot):
                        cp.start()
            if pair is not None:
                project(MAIN_W + 2 * (2 * IN_HALF) * pair, 128, pstg_ref, slot % 2 if p == 2 else 1)
                for case in range(2):
                    @pl.when(pair == case)
                    def _():
                        put_pair(case, 0 if p == 2 else 1).start()

        order = [j] + idx
        w_out = [pltpu.make_async_copy(w4_ref.at[pl.ds(pl.multiple_of(2 * IN_HALF * s, 32), 2 * IN_HALF)],
                                       w4_hbm.at[pl.ds(pl.multiple_of(2 * IN_HALF * s, 32), 2 * IN_HALF)], wsem.at[p])
                 for p, s in enumerate(order)]
        w_out[0].start()
        phase(0, j, None)
        passed = []
        for k in range(3):
            jk = idx[k]
            rc(7 + k, w4_ref.at[rows_of(jk, c)], w4_ref.at[rows_of(jk, c)], sib).wait_recv()
            passed.append(rc(10 + k, w4_ref.at[rows_of(jk, c)], w4_ref.at[rows_of(jk, c)], sib))
            passed[-1].start()
            rc(10 + k, w4_ref.at[rows_of(jk, 1 - c)], w4_ref.at[rows_of(jk, 1 - c)], sib).wait_recv()
            w_out[1 + k].start()
            if k == 0:
                phase(1, jk, None)
            elif k == 1:
                phase(2, jk, j // 2)
            else:
                phase(3, jk, 1 - j // 2)

        for case in range(N_CHIPS):
            for p in (2, 3):
                @pl.when(order[p] == case)
                def _():
                    for cp in put_main(case, p % 2):
                        cp.wait()
        for case in range(2):
            @pl.when(1 - j // 2 == case)
            def _():
                put_pair(case, 1).wait()
        h_copy.wait()
        for cp in w_out:
            cp.wait()
        for cp in sends + passed:
            cp.wait_send()

    vm = pl.BlockSpec(memory_space=pltpu.VMEM)
    hbm = pl.BlockSpec(memory_space=pl.ANY)
    widths = (512, 256, 512, 512, 512, 512)
    return pl.pallas_call(
        body,
        name="in_proj",
        in_specs=[hbm, vm, vm, vm, vm, vm],
        out_specs=[hbm] * 8 + [vm, vm],
        out_shape=[jax.ShapeDtypeStruct((t, w), F32) for w in widths]
        + [jax.ShapeDtypeStruct((t, D_MODEL), BF16), jax.ShapeDtypeStruct((IN_W, D_MODEL), BF16),
           jax.ShapeDtypeStruct((N_DEV, 1, D_MODEL), F32), jax.ShapeDtypeStruct((N_CHIPS, 1, ADA_SHARD), F32)],
        scratch_shapes=[pltpu.VMEM((t, D_MODEL), F32), pltpu.VMEM((t, D_MODEL), BF16), pltpu.VMEM((IN_W, D_MODEL), BF16),
                        pltpu.VMEM((2, t, MAIN_W), F32), pltpu.VMEM((2, t, 128), F32), pltpu.VMEM((N_DEV, 1, ADA_SHARD), F32),
                        pltpu.SemaphoreType.DMA((2,)), pltpu.SemaphoreType.DMA((2, 3)), pltpu.SemaphoreType.DMA((N_CHIPS,)),
                        pltpu.SemaphoreType.DMA((n_sem,)), pltpu.SemaphoreType.DMA((n_sem,))],
        compiler_params=_cparams(),
    )(x, wt, c_row, w_ada, b_sh, nw)


def _adamw_math(w, g, m, v):
    m2 = ADAM_B1 * m + (1.0 - ADAM_B1) * g
    v2 = ADAM_B2 * v + (1.0 - ADAM_B2) * (g * g)
    m_hat = m2 / (1.0 - ADAM_B1 ** ADAM_STEP)
    v_hat = v2 / (1.0 - ADAM_B2 ** ADAM_STEP)
    delta = -ADAM_LR * (m_hat / (jnp.sqrt(v_hat) + ADAM_EPS) + ADAM_WD * w)
    return delta, m2, v2


def _adamw(name, w, g, m, v, tm):
    r, cdim = w.shape

    def body(w_ref, g_ref, m_ref, v_ref, d_ref, m2_ref, v2_ref):
        d_ref[...], m2_ref[...], v2_ref[...] = _adamw_math(w_ref[...], g_ref[...], m_ref[...], v_ref[...])

    blk = pl.BlockSpec((tm, cdim), lambda i: (i, 0))
    return pl.pallas_call(
        body,
        name=name,
        grid=(r // tm,),
        in_specs=[blk] * 4,
        out_specs=[blk] * 3,
        out_shape=[jax.ShapeDtypeStruct((r, cdim), F32)] * 3,
        compiler_params=_cparams(dimension_semantics=("arbitrary",)),
    )(w, g, m, v)


def _adamw_ada(w, m, v, cact_t, dcols):
    r, cdim = w.shape
    tm = 256

    def body(w_ref, m_ref, v_ref, ct_ref, dc_ref, g_ref, d_ref, m2_ref, v2_ref):
        g = jnp.dot(ct_ref[...], dc_ref[...], preferred_element_type=F32, precision=lax.Precision.HIGHEST)
        g_ref[...] = g
        d_ref[...], m2_ref[...], v2_ref[...] = _adamw_math(w_ref[...], g, m_ref[...], v_ref[...])

    blk = pl.BlockSpec((tm, cdim), lambda i: (i, 0))
    return pl.pallas_call(
        body,
        name="adamw_w_ada",
        grid=(r // tm,),
        in_specs=[blk] * 3 + [pl.BlockSpec((tm, N_DEV), lambda i: (i, 0)), pl.BlockSpec((N_DEV, cdim), lambda i: (0, 0))],
        out_specs=[blk] * 4,
        out_shape=[jax.ShapeDtypeStruct((r, cdim), F32)] * 4,
        compiler_params=_cparams(dimension_semantics=("arbitrary",)),
    )(w, m, v, cact_t, dcols)


def _adamw_small(ws, gs, ms, vs):
    n = len(ws)

    def body(*refs):
        w_r, g_r, m_r, v_r = refs[0:n], refs[n:2 * n], refs[2 * n:3 * n], refs[3 * n:4 * n]
        d_r, m2_r, v2_r = refs[4 * n:5 * n], refs[5 * n:6 * n], refs[6 * n:7 * n]
        for i in range(n):
            d_r[i][...], m2_r[i][...], v2_r[i][...] = _adamw_math(w_r[i][...], g_r[i][...], m_r[i][...], v_r[i][...])

    vm = pl.BlockSpec(memory_space=pltpu.VMEM)
    shapes = [jax.ShapeDtypeStruct(w.shape, F32) for w in ws]
    out = pl.pallas_call(
        body,
        name="adamw_small",
        in_specs=[vm] * (4 * n),
        out_specs=[vm] * (3 * n),
        out_shape=shapes * 3,
        compiler_params=_cparams(),
    )(*ws, *gs, *ms, *vs)
    return out[0:n], out[n:2 * n], out[2 * n:3 * n]


def _rope_tables(t):
    inv = ROPE_THETA ** (-jnp.arange(0, HEAD_DIM, 2, dtype=F32) / HEAD_DIM)
    ang = jnp.arange(t, dtype=F32)[:, None] * inv[None, :]
    cos, sin = jnp.cos(ang), jnp.sin(ang)
    return jnp.tile(cos, (1, 4)), jnp.tile(jnp.concatenate([-sin, sin], axis=1), (1, 2))


def _pad_lanes(v, width):
    return jnp.pad(v, ((0, 0), (0, width - v.shape[1])))


def kernel(x, c, w_ada, b_ada, norm_w, w_in, q_norm_w, k_norm_w, sinks, conv_w, conv_b, ln_w, ln_b, w_out, loss_target, m_w_ada, m_b_ada, m_norm_w, m_w_in, m_q_norm_w, m_k_norm_w, m_sinks, m_conv_w, m_conv_b, m_ln_w, m_ln_b, m_w_out, v_w_ada, v_b_ada, v_norm_w, v_w_in, v_q_norm_w, v_k_norm_w, v_sinks, v_conv_w, v_conv_b, v_ln_w, v_ln_b, v_w_out):
    xi, yi = lax.axis_index("x"), lax.axis_index("y")
    j = 2 * xi + yi
    x2, tgt = x[0], loss_target[0]
    t = x2.shape[0]

    wt_s, mt_s, vt_s = w_in[0].T, m_w_in[0].T, v_w_in[0].T
    cw_pad = jnp.pad(conv_w[0], ((0, 1), (0, 0)))
    b_sh = lax.dynamic_slice(b_ada, (0, ADA_SHARD * j), (1, ADA_SHARD))

    q_raw, kv_raw, ga, ua, ug, gb, h, w_full, call, ada4 = _in_proj_gather(
        x2, wt_s.reshape(2, IN_HALF, D_MODEL), c, w_ada[0], b_sh, norm_w)
    ada = ada4.reshape(1, 3 * D_MODEL)
    s1, gate = 1.0 + ada[:, D_MODEL:2 * D_MODEL], ada[:, 2 * D_MODEL:]

    cos_f, sin_s = _rope_tables(t)
    qw2, kw2 = jnp.tile(q_norm_w, (1, 2)), jnp.tile(k_norm_w, (1, 2))

    o, mix_a, wo4, cw4 = _attn_fwd(q_raw, kv_raw, ga, qw2, kw2, sinks, cos_f, sin_s,
                                   w_out[0].reshape(2, OUT_HALF, D_MODEL), cw_pad)
    w_out_full = wo4.reshape(D_MODEL, D_MODEL)
    cw_full = jnp.concatenate([cw4[i] for i in range(N_CHIPS)], axis=1)
    cz, mix_b = _conv_fwd(ua, ug, gb, cw_full, conv_b, ln_w, ln_b)
    dout, dmix_a, dmix_b, gwo_bf, red_o = _out_proj(mix_a, mix_b, x2, tgt, gate, w_out_full)

    dq, dkv, dga, sm_a, gwo = _attn_bwd(q_raw, kv_raw, ga, o, dmix_a, qw2, kw2, sinks, cos_f, sin_s,
                                        gwo_bf.reshape(N_CHIPS, 2, OUT_HALF, D_MODEL))
    dua, dug, dgb, dcw, dvec = _conv_bwd(ua, ug, gb, cz, dmix_b, cw_full, ln_w, ln_b)
    dparts = (dq, dkv, dga, dua, dug, dgb)

    misc = jnp.concatenate([dvec[2:3], sm_a[0:1], sm_a[1:2], sm_a[2:3], jnp.zeros((1, 128), F32)], axis=1)
    small0 = jnp.concatenate([
        dcw.reshape(16, D_MODEL),
        jnp.concatenate([dvec[0:2].reshape(1, D_MODEL), misc, red_o[1:2], jnp.zeros((13, D_MODEL), F32)], axis=0)], axis=0)
    grad_x, gw, ssum, rows = _in_proj_bwd(dparts, h, x2, dout, s1, norm_w, w_full, small0, red_o)

    loss = (0.5 / D_MODEL) * jnp.sum(ssum[18])
    gt_w_in = gw.reshape(2 * IN_HALF, D_MODEL)
    g_w_out = gwo.reshape(D_MODEL // N_CHIPS, D_MODEL)
    g_conv_w = lax.dynamic_slice(ssum[0:16].reshape(32, CONV_W), (0, 128 * j), (CONV_TAPS, 128))
    g_conv_b, g_ln_w, g_ln_b = ssum[16:17, 0:CONV_W], ssum[16:17, CONV_W:], ssum[17:18, 0:CONV_W]
    g_qw, g_kw, g_sinks = ssum[17:18, 512:512 + HEAD_DIM], ssum[17:18, 640:640 + HEAD_DIM], ssum[17:18, 768:776]
    rsum = rows[0]
    for d in range(1, N_DEV):
        rsum = rsum + rows[d]
    g_norm_w = rsum[4:5]
    g_b_ada = jnp.concatenate([rsum[2:3], rsum[3:4], rsum[0:1]], axis=1)
    d_ada_all = jnp.concatenate([rows[:, 2], rows[:, 3], rows[:, 0]], axis=1)
    dcols = lax.dynamic_slice(d_ada_all, (0, ADA_SHARD * j), (N_DEV, ADA_SHARD))
    cact_t = jax.nn.silu(call.reshape(N_DEV, D_MODEL)).T

    g_w_ada, d_w_ada, nm_w_ada, nv_w_ada = _adamw_ada(w_ada[0], m_w_ada[0], v_w_ada[0], cact_t, dcols)
    dt_w_in, nmt_w_in, nvt_w_in = _adamw("adamw_w_in", wt_s, gt_w_in, mt_s, vt_s, 176)
    g_w_in, d_w_in, nm_w_in, nv_w_in = gt_w_in.T, dt_w_in.T, nmt_w_in.T, nvt_w_in.T
    d_w_out, nm_w_out, nv_w_out = _adamw("adamw_w_out", w_out[0], g_w_out, m_w_out[0], v_w_out[0], 128)
    ws = [b_ada, norm_w, q_norm_w, k_norm_w, sinks, conv_w[0], conv_b, ln_w, ln_b]
    gs = [g_b_ada, g_norm_w, g_qw, g_kw, g_sinks, g_conv_w, g_conv_b, g_ln_w, g_ln_b]
    ms = [m_b_ada, m_norm_w, m_q_norm_w, m_k_norm_w, m_sinks, m_conv_w[0], m_conv_b, m_ln_w, m_ln_b]
    vs = [v_b_ada, v_norm_w, v_q_norm_w, v_k_norm_w, v_sinks, v_conv_w[0], v_conv_b, v_ln_w, v_ln_b]
    ds, nms, nvs = _adamw_small(ws, gs, ms, vs)

    def order(ada_v, in_v, out_v, sm):
        b, nw_, qw_, kw_, sk_, cw_, cb_, lw_, lb_ = sm
        return [ada_v[None], b, nw_, in_v[None], qw_, kw_, sk_, cw_[None], cb_, lw_, lb_, out_v[None]]

    grads = order(g_w_ada, g_w_in, g_w_out, gs)
    deltas = order(d_w_ada, d_w_in, d_w_out, ds)
    new_m = order(nm_w_ada, nm_w_in, nm_w_out, nms)
    new_v = order(nv_w_ada, nv_w_in, nv_w_out, nvs)
    return (loss, grad_x[None], *grads, *deltas, *new_m, *new_v)
```

```python
import functools

import jax
import jax.numpy as jnp
from jax import lax
from jax.experimental import pallas as pl
from jax.experimental.pallas import tpu as pltpu

F32 = jnp.float32
BF16 = jnp.bfloat16

D_MODEL = 1024
ATTN_W = 512
KV_W = 128
CONV_W = 512
IN_W = 2816
HEAD_DIM = 64
CONV_TAPS = 31
QBLK = 128
EPS = 1e-6
ROPE_THETA = 10000.0

ADAM_LR = 0.001
ADAM_B1 = 0.9
ADAM_B2 = 0.999
ADAM_EPS = 1e-08
ADAM_WD = 0.01
ADAM_STEP = 10

N_CHIPS = 4
N_DEV = 8
IN_HALF = IN_W // N_CHIPS // 2
OUT_HALF = D_MODEL // N_CHIPS // 2
ADA_SHARD = 3 * D_MODEL // N_CHIPS

VMEM_LIMIT = 56 * 1024 * 1024
CONV_PAD = 32


def _cparams(**kw):
    return pltpu.CompilerParams(vmem_limit_bytes=VMEM_LIMIT, **kw)


def _sigmoid(v):
    return 1.0 / (1.0 + jnp.exp(-v))


def _silu(v):
    return v * _sigmoid(v)


def _dsilu(v):
    s = _sigmoid(v)
    return s * (1.0 + v * (1.0 - s))


def _lane(shape):
    return lax.broadcasted_iota(jnp.int32, shape, len(shape) - 1)


PUT_ROWS = 512


def _fetch(hbm_refs, vmem_refs, sem):
    cps = [pltpu.make_async_copy(h, v, sem.at[i]) for i, (h, v) in enumerate(zip(hbm_refs, vmem_refs))]
    for cp in cps:
        cp.start()
    return cps


def _put(vmem_ref, hbm_ref, sem, m):
    r = pl.ds(pl.multiple_of(m * PUT_ROWS, PUT_ROWS), PUT_ROWS)
    return pltpu.make_async_copy(vmem_ref.at[r], hbm_ref.at[r], sem.at[m])


def _put_all(pairs, sems, m):
    for (v, h), sem in zip(pairs, sems):
        _put(v, h, sem, m).start()


def _put_wait(pairs, sems, n):
    for (v, h), sem in zip(pairs, sems):
        for m in range(n):
            _put(v, h, sem, m).wait()


def _head_mean(s, left):
    sl = jnp.sum(jnp.where(left, s, 0.0), axis=-1, keepdims=True)
    sr = jnp.sum(jnp.where(left, 0.0, s), axis=-1, keepdims=True)
    return jnp.where(left, sl, sr) * (1.0 / HEAD_DIM)


def _rot(v, first):
    return jnp.where(first, pltpu.roll(v, 96, 1), pltpu.roll(v, 32, 1))


def _norm_rope(v, w, cos, sin_s, left, first):
    r = lax.rsqrt(_head_mean(v * v, left) + EPS)
    xh = v * r
    n = xh * w
    return n * cos + _rot(n, first) * sin_s, xh, r


def _norm_rope_bwd(d, xh, r, w, cos, sin_s, left, first):
    dn = d * cos - _rot(d, first) * sin_s
    dw = jnp.sum(dn * xh, axis=0, keepdims=True)
    dxh = dn * w
    return r * (dxh - xh * _head_mean(dxh * xh, left)), dw


def _dup_heads(v, left):
    sw = pltpu.roll(v, 64, 1)
    return jnp.where(left, v, sw), jnp.where(left, sw, v)


def _prep_kv(kv_ref, kw_ref, cos_ref, sin_ref, ka_ref, va_ref, t):
    ch = 256
    for g in range(2):
        ka_ref[g, 0:QBLK, :] = jnp.zeros((QBLK, 128), BF16)
        va_ref[g, 0:QBLK, :] = jnp.zeros((QBLK, 128), BF16)

    def chunk(i, carry):
        r0 = pl.multiple_of(i * ch, ch)
        left = _lane((ch, 128)) < 64
        first = (_lane((ch, 128)) % 64) < 32
        k = kv_ref[pl.ds(r0, ch), 0:128]
        v = kv_ref[pl.ds(r0, ch), 128:256]
        kr, _, _ = _norm_rope(k, kw_ref[...], cos_ref[pl.ds(r0, ch), :], sin_ref[pl.ds(r0, ch), :], left, first)
        k0, k1 = _dup_heads(kr, left)
        v0, v1 = _dup_heads(v, left)
        ka_ref[0, pl.ds(QBLK + r0, ch), :] = k0.astype(BF16)
        ka_ref[1, pl.ds(QBLK + r0, ch), :] = k1.astype(BF16)
        va_ref[0, pl.ds(QBLK + r0, ch), :] = v0.astype(BF16)
        va_ref[1, pl.ds(QBLK + r0, ch), :] = v1.astype(BF16)
        return carry

    lax.fori_loop(0, t // ch, chunk, 0)


def _band_mask(n):
    qi = lax.broadcasted_iota(jnp.int32, (2 * QBLK, 2 * QBLK), 0) % QBLK
    kj = lax.broadcasted_iota(jnp.int32, (2 * QBLK, 2 * QBLK), 1)
    local = (kj > qi) & (kj <= qi + QBLK)
    return local & ((n > 0) | (kj >= QBLK))


def _softmax_pair(s, mask, sink0, sink1):
    row = lax.broadcasted_iota(jnp.int32, (2 * QBLK, 1), 0)
    sink = jnp.where(row < QBLK, sink0, sink1)
    s = jnp.where(mask, s, -jnp.inf)
    m = jnp.maximum(jnp.max(s, axis=-1, keepdims=True), sink)
    e = jnp.exp(s - m)
    es = jnp.exp(sink - m)
    inv = 1.0 / (jnp.sum(e, axis=-1, keepdims=True) + es)
    return e * inv, es * inv


def _stack_heads(v, left):
    return jnp.concatenate([jnp.where(left, v, 0.0), jnp.where(left, 0.0, v)], axis=0)


def _attn_fwd(q_raw, kv_raw, ga, qw2, kw2, sinks, cos_f, sin_s, wo, cw):
    t = q_raw.shape[0]
    nblk = t // QBLK
    per_put = PUT_ROWS // QBLK

    def body(q_hbm, kv_ref, ga_hbm, qw_ref, kw_ref, sk_ref, cos_ref, sin_ref, wo_ref, cw_ref,
             o_hbm, mix_hbm, wo4_ref, cw4_ref, ka_ref, va_ref, q_ref, ga_ref, o_ref, mix_ref, isem, osem0, osem1,
             ssem, rsem):
        loads = _fetch((q_hbm, ga_hbm), (q_ref, ga_ref), isem)
        outs, osems = ((o_ref, o_hbm), (mix_ref, mix_hbm)), (osem0, osem1)
        x, y, c, chips = _place()
        j = 2 * x + y
        sib = (x, y, 1 - c)
        idx = [2 * cx + cy for cx, cy in chips]
        rc = functools.partial(_remote, ssem, rsem)
        wo4_ref[j] = wo_ref[...].astype(BF16)
        cw4_ref[j] = cw_ref[...]
        sends = []
        for k, chip in enumerate(chips):
            sends.append(rc(k, wo4_ref.at[j, c], wo4_ref.at[j, c], (*chip, c)))
            sends.append(rc(6 + k, cw4_ref.at[j], cw4_ref.at[j], (*chip, c)))
        for cp in sends:
            cp.start()

        _prep_kv(kv_ref, kw_ref, cos_ref, sin_ref, ka_ref, va_ref, t)
        for cp in loads:
            cp.wait()

        def blk(n, carry):
            r0 = pl.multiple_of(n * QBLK, QBLK)
            left = _lane((QBLK, 128)) < 64
            first = (_lane((QBLK, 128)) % 64) < 32
            cos = cos_ref[pl.ds(r0, QBLK), :]
            sin = sin_ref[pl.ds(r0, QBLK), :]
            mask = _band_mask(n)
            for p in range(4):
                g = p // 2
                lanes = slice(p * 128, (p + 1) * 128)
                qr, _, _ = _norm_rope(q_ref[pl.ds(r0, QBLK), lanes], qw_ref[...], cos, sin, left, first)
                q2 = _stack_heads(qr * 0.125, left).astype(BF16)
                s = lax.dot_general(q2, ka_ref[g, pl.ds(r0, 2 * QBLK), :], (((1,), (1,)), ((), ())),
                                    preferred_element_type=F32)
                pm, _ = _softmax_pair(s, mask, sk_ref[0, 2 * p], sk_ref[0, 2 * p + 1])
                o2 = jnp.dot(pm.astype(BF16), va_ref[g, pl.ds(r0, 2 * QBLK), :], preferred_element_type=F32)
                o = jnp.where(left, o2[0:QBLK], o2[QBLK:2 * QBLK])
                o_ref[pl.ds(r0, QBLK), lanes] = o.astype(BF16)
                mix_ref[pl.ds(r0, QBLK), lanes] = (o * _silu(ga_ref[pl.ds(r0, QBLK), lanes])).astype(BF16)

            @pl.when(n % per_put == per_put - 1)
            def _():
                _put_all(outs, osems, n // per_put)

            return carry

        lax.fori_loop(0, nblk, blk, 0)
        _put_wait(outs, osems, t // PUT_ROWS)

        passed = []
        for k, chip in enumerate(chips):
            jk = idx[k]
            rc(k, wo4_ref.at[jk, c], wo4_ref.at[jk, c], sib).wait_recv()
            passed.append(rc(3 + k, wo4_ref.at[jk, c], wo4_ref.at[jk, c], sib))
            passed[-1].start()
        for k, chip in enumerate(chips):
            jk = idx[k]
            rc(3 + k, wo4_ref.at[jk, 1 - c], wo4_ref.at[jk, 1 - c], sib).wait_recv()
            rc(6 + k, cw4_ref.at[jk], cw4_ref.at[jk], sib).wait_recv()
        for cp in sends + passed:
            cp.wait_send()

    vm = pl.BlockSpec(memory_space=pltpu.VMEM)
    hbm = pl.BlockSpec(memory_space=pl.ANY)
    n_sem = 9
    return pl.pallas_call(
        body,
        name="attn_fwd",
        in_specs=[hbm, vm, hbm, vm, vm, pl.BlockSpec(memory_space=pltpu.SMEM), vm, vm, vm, vm],
        out_specs=[hbm, hbm, vm, vm],
        out_shape=[jax.ShapeDtypeStruct((t, ATTN_W), BF16), jax.ShapeDtypeStruct((t, ATTN_W), BF16),
                   jax.ShapeDtypeStruct((N_CHIPS, 2, OUT_HALF, D_MODEL), BF16),
                   jax.ShapeDtypeStruct((N_CHIPS, 32, 128), F32)],
        scratch_shapes=[pltpu.VMEM((2, t + QBLK, 128), BF16), pltpu.VMEM((2, t + QBLK, 128), BF16),
                        pltpu.VMEM((t, ATTN_W), F32), pltpu.VMEM((t, ATTN_W), F32),
                        pltpu.VMEM((t, ATTN_W), BF16), pltpu.VMEM((t, ATTN_W), BF16),
                        pltpu.SemaphoreType.DMA((2,)), pltpu.SemaphoreType.DMA((t // PUT_ROWS,)),
                        pltpu.SemaphoreType.DMA((t // PUT_ROWS,)),
                        pltpu.SemaphoreType.DMA((n_sem,)), pltpu.SemaphoreType.DMA((n_sem,))],
        compiler_params=_cparams(),
    )(q_raw, kv_raw, ga, qw2, kw2, sinks, cos_f, sin_s, wo, cw)


def _attn_bwd(q_raw, kv_raw, ga, o, dmix, qw2, kw2, sinks, cos_f, sin_s, go):
    t = q_raw.shape[0]
    nblk = t // QBLK
    per_put = PUT_ROWS // QBLK

    def body(q_hbm, kv_ref, ga_hbm, o_hbm, dm_hbm, qw_ref, kw_ref, sk_ref, cos_ref, sin_ref, go_ref,
             dq_hbm, dkv_ref, dga_hbm, sm_ref, gwo_ref, ka_ref, va_ref, dka_ref, dva_ref,
             sibo_ref, outo_ref, ino_ref, q_ref, ga_ref, o_ref, dm_ref, dq_ref, dga_ref, isem, osem0, osem1, ssem, rsem):
        loads = _fetch((q_hbm, ga_hbm, o_hbm, dm_hbm), (q_ref, ga_ref, o_ref, dm_ref), isem)
        outs, osems = ((dq_ref, dq_hbm), (dga_ref, dga_hbm)), (osem0, osem1)
        x, y, c, chips = _place()
        sib = (x, y, 1 - c)
        rc = functools.partial(_remote, ssem, rsem)
        theirs, mine = go_ref.at[:, 1 - c], go_ref.at[:, c]
        sends = [_rs_to_sibling(rc, 0, theirs, sibo_ref, sib)]
        _prep_kv(kv_ref, kw_ref, cos_ref, sin_ref, ka_ref, va_ref, t)
        dka_ref[...] = jnp.zeros_like(dka_ref)
        dva_ref[...] = jnp.zeros_like(dva_ref)
        sends += _rs_trade(rc, 0, theirs, mine, sibo_ref, outo_ref, ino_ref, OUT_HALF, c, sib, chips)
        for cp in loads:
            cp.wait()

        def blk(n, carry):
            dqw, dsk = carry
            r0 = pl.multiple_of(n * QBLK, QBLK)
            left = _lane((QBLK, 128)) < 64
            first = (_lane((QBLK, 128)) % 64) < 32
            cos = cos_ref[pl.ds(r0, QBLK), :]
            sin = sin_ref[pl.ds(r0, QBLK), :]
            mask = _band_mask(n)
            row = lax.broadcasted_iota(jnp.int32, (2 * QBLK, 1), 0)
            for p in range(4):
                g = p // 2
                lanes = slice(p * 128, (p + 1) * 128)
                rows = pl.ds(r0, QBLK)
                win = pl.ds(r0, 2 * QBLK)
                qr, xh, r = _norm_rope(q_ref[rows, lanes], qw_ref[...], cos, sin, left, first)
                q2 = _stack_heads(qr * 0.125, left).astype(BF16)
                kwin = ka_ref[g, win, :]
                vwin = va_ref[g, win, :]
                s = lax.dot_general(q2, kwin, (((1,), (1,)), ((), ())), preferred_element_type=F32)
                pm, ps = _softmax_pair(s, mask, sk_ref[0, 2 * p], sk_ref[0, 2 * p + 1])
                gav = ga_ref[rows, lanes]
                dmv = dm_ref[rows, lanes].astype(F32)
                dga_ref[rows, lanes] = (dmv * o_ref[rows, lanes].astype(F32) * _dsilu(gav)).astype(BF16)
                do2 = _stack_heads(dmv * _silu(gav), left).astype(BF16)
                dp = lax.dot_general(do2, vwin, (((1,), (1,)), ((), ())), preferred_element_type=F32)
                delta = jnp.sum(pm * dp, axis=-1, keepdims=True)
                ds = (pm * (dp - delta)).astype(BF16)
                pd = ps * delta
                d0 = jnp.sum(jnp.where(row < QBLK, pd, 0.0), axis=0, keepdims=True)
                d1 = jnp.sum(jnp.where(row < QBLK, 0.0, pd), axis=0, keepdims=True)
                l8 = _lane((1, 128))
                dsk = dsk - jnp.where(l8 == 2 * p, d0, 0.0) - jnp.where(l8 == 2 * p + 1, d1, 0.0)
                dva_ref[g, win, :] += lax.dot_general(pm.astype(BF16), do2, (((0,), (0,)), ((), ())),
                                                      preferred_element_type=F32)
                dka_ref[g, win, :] += lax.dot_general(ds, q2, (((0,), (0,)), ((), ())),
                                                      preferred_element_type=F32)
                dq2 = jnp.dot(ds, kwin, preferred_element_type=F32)
                dqr = jnp.where(left, dq2[0:QBLK], dq2[QBLK:2 * QBLK]) * 0.125
                dq, dw = _norm_rope_bwd(dqr, xh, r, qw_ref[...], cos, sin, left, first)
                dq_ref[rows, lanes] = dq.astype(BF16)
                dqw = dqw + dw

            @pl.when(n % per_put == per_put - 1)
            def _():
                _put_all(outs, osems, n // per_put)

            return dqw, dsk

        zero = jnp.zeros((1, 128), F32)
        dqw, dsk = lax.fori_loop(0, nblk, blk, (zero, zero))

        ch = 256

        def chunk(i, dkw):
            r0 = pl.multiple_of(i * ch, ch)
            left = _lane((ch, 128)) < 64
            first = (_lane((ch, 128)) % 64) < 32
            rows = pl.ds(r0, ch)
            prow = pl.ds(QBLK + r0, ch)

            def fold(ref):
                a0 = ref[0, prow, :]
                a1 = ref[1, prow, :]
                return jnp.where(left, a0 + pltpu.roll(a0, 64, 1), a1 + pltpu.roll(a1, 64, 1))

            cos = cos_ref[rows, :]
            sin = sin_ref[rows, :]
            _, xh, r = _norm_rope(kv_ref[rows, 0:128], kw_ref[...], cos, sin, left, first)
            dk, dw = _norm_rope_bwd(fold(dka_ref), xh, r, kw_ref[...], cos, sin, left, first)
            dkv_ref[rows, 0:128] = dk.astype(BF16)
            dkv_ref[rows, 128:256] = fold(dva_ref).astype(BF16)
            return dkw + dw

        dkw = lax.fori_loop(0, t // ch, chunk, zero)
        sm_ref[...] = jnp.zeros((8, 128), F32)
        sm_ref[0:1, :] = dqw + pltpu.roll(dqw, 64, 1)
        sm_ref[1:2, :] = dkw + pltpu.roll(dkw, 64, 1)
        sm_ref[2:3, :] = dsk

        j = 2 * x + y
        sends.append(_rs_total(rc, 0, mine, sibo_ref, outo_ref, ino_ref, gwo_ref, OUT_HALF, j, c, sib))
        _rs_done(rc, 0, gwo_ref, c, sib)
        for cp in sends:
            cp.wait_send()
        _put_wait(outs, osems, t // PUT_ROWS)

    vm = pl.BlockSpec(memory_space=pltpu.VMEM)
    hbm = pl.BlockSpec(memory_space=pl.ANY)
    return pl.pallas_call(
        body,
        name="attn_bwd",
        in_specs=[hbm, vm, hbm, hbm, hbm, vm, vm, pl.BlockSpec(memory_space=pltpu.SMEM), vm, vm, vm],
        out_specs=[hbm, vm, hbm, vm, vm],
        out_shape=[jax.ShapeDtypeStruct((t, ATTN_W), BF16), jax.ShapeDtypeStruct((t, 2 * KV_W), BF16),
                   jax.ShapeDtypeStruct((t, ATTN_W), BF16), jax.ShapeDtypeStruct((8, 128), F32),
                   jax.ShapeDtypeStruct((2, OUT_HALF, D_MODEL), F32)],
        scratch_shapes=[pltpu.VMEM((2, t + QBLK, 128), BF16), pltpu.VMEM((2, t + QBLK, 128), BF16),
                        pltpu.VMEM((2, t + QBLK, 128), F32), pltpu.VMEM((2, t + QBLK, 128), F32)]
        + _rs_scratch(OUT_HALF)
        + [pltpu.VMEM((t, ATTN_W), F32), pltpu.VMEM((t, ATTN_W), F32), pltpu.VMEM((t, ATTN_W), BF16),
           pltpu.VMEM((t, ATTN_W), BF16), pltpu.VMEM((t, ATTN_W), BF16), pltpu.VMEM((t, ATTN_W), BF16),
           pltpu.SemaphoreType.DMA((4,)), pltpu.SemaphoreType.DMA((t // PUT_ROWS,)), pltpu.SemaphoreType.DMA((t // PUT_ROWS,)),
           pltpu.SemaphoreType.DMA((RS_SEMS,)), pltpu.SemaphoreType.DMA((RS_SEMS,))],
        compiler_params=_cparams(),
    )(q_raw, kv_raw, ga, o, dmix, qw2, kw2, sinks, cos_f, sin_s, go)


CONV_CH = 256
CONV_SUB = 64
CONV_ACCS = 3


def _shifted_windows(src_ref, r0, sh_ref):
    rows = CONV_CH + CONV_PAD
    win = src_ref[pl.ds(r0, rows), :]
    for b in range(8):
        sh = win if b == 0 else pltpu.roll(win, rows - b, 0)
        for c in range(CONV_W // 128):
            sh_ref[b, c] = sh[:, c * 128:(c + 1) * 128]


def _conv_fwd(ua, ug, gb, cw, cb, lw, lb):
    t = ua.shape[0]

    def body(ua_hbm, ug_hbm, gb_hbm, cw_ref, cb_ref, lw_ref, lb_ref, cz_hbm, mix_hbm, zp_ref, sh_ref,
             ua_ref, ug_ref, gb_ref, cz_ref, mix_ref, isem, osem0, osem1):
        loads = _fetch((ua_hbm, ug_hbm, gb_hbm), (ua_ref, ug_ref, gb_ref), isem)
        outs, osems = ((cz_ref, cz_hbm), (mix_ref, mix_hbm)), (osem0, osem1)
        per_put = PUT_ROWS // CONV_CH
        zp_ref[0:CONV_PAD, :] = jnp.zeros((CONV_PAD, CONV_W), F32)
        loads[0].wait()
        loads[1].wait()

        def glu(i, carry):
            r0 = pl.multiple_of(i * CONV_CH, CONV_CH)
            rows = pl.ds(r0, CONV_CH)
            zp_ref[pl.ds(CONV_PAD + r0, CONV_CH), :] = ua_ref[rows, :] * _sigmoid(ug_ref[rows, :])
            return carry

        lax.fori_loop(0, t // CONV_CH, glu, 0)
        loads[2].wait()

        def chunk(i, carry):
            r0 = pl.multiple_of(i * CONV_CH, CONV_CH)
            _shifted_windows(zp_ref, r0, sh_ref)
            for c in range(CONV_W // 128):
                lanes = slice(c * 128, (c + 1) * 128)

                def sub(k, carry2):
                    b0 = pl.multiple_of(k * CONV_SUB, CONV_SUB)
                    acc = [jnp.broadcast_to(cb_ref[0:1, lanes], (CONV_SUB, 128))] + [None] * (CONV_ACCS - 1)
                    for j in range(CONV_TAPS):
                        off = j + CONV_PAD - (CONV_TAPS - 1)
                        term = sh_ref[off % 8, c, pl.ds(b0 + 8 * (off // 8), CONV_SUB), :] * cw_ref[j:j + 1, lanes]
                        acc[j % CONV_ACCS] = term if acc[j % CONV_ACCS] is None else acc[j % CONV_ACCS] + term
                    cz_ref[pl.ds(r0 + b0, CONV_SUB), lanes] = functools.reduce(lambda a, b: a + b, acc)
                    return carry2

                lax.fori_loop(0, CONV_CH // CONV_SUB, sub, 0)
            rows = pl.ds(r0, CONV_CH)
            cz = cz_ref[rows, :]
            mu = jnp.mean(cz, axis=-1, keepdims=True)
            xc = cz - mu
            rs = lax.rsqrt(jnp.mean(xc * xc, axis=-1, keepdims=True) + EPS)
            ln = xc * rs * lw_ref[...] + lb_ref[...]
            mix_ref[rows, :] = (_silu(ln) * _silu(gb_ref[rows, :])).astype(BF16)

            @pl.when(i % per_put == per_put - 1)
            def _():
                _put_all(outs, osems, i // per_put)

            return carry

        lax.fori_loop(0, t // CONV_CH, chunk, 0)
        _put_wait(outs, osems, t // PUT_ROWS)

    vm = pl.BlockSpec(memory_space=pltpu.VMEM)
    hbm = pl.BlockSpec(memory_space=pl.ANY)
    nput = t // PUT_ROWS
    return pl.pallas_call(
        body,
        name="conv_fwd",
        in_specs=[hbm] * 3 + [vm] * 4,
        out_specs=[hbm, hbm],
        out_shape=[jax.ShapeDtypeStruct((t, CONV_W), F32), jax.ShapeDtypeStruct((t, CONV_W), BF16)],
        scratch_shapes=[pltpu.VMEM((t + CONV_PAD, CONV_W), F32),
                        pltpu.VMEM((8, CONV_W // 128, CONV_CH + CONV_PAD, 128), F32),
                        pltpu.VMEM((t, CONV_W), F32), pltpu.VMEM((t, CONV_W), F32), pltpu.VMEM((t, CONV_W), F32),
                        pltpu.VMEM((t, CONV_W), F32), pltpu.VMEM((t, CONV_W), BF16),
                        pltpu.SemaphoreType.DMA((3,)), pltpu.SemaphoreType.DMA((nput,)), pltpu.SemaphoreType.DMA((nput,))],
        compiler_params=_cparams(),
    )(ua, ug, gb, cw, cb, lw, lb)


def _conv_bwd(ua, ug, gb, cz, dmix, cw, lw, lb):
    t = ua.shape[0]

    def body(ua_hbm, ug_hbm, gb_hbm, cz_hbm, dm_hbm, cw_ref, lw_ref, lb_ref,
             dua_hbm, dug_hbm, dgb_hbm, dcw_ref, dvec_ref, zp_ref, dp_ref, sh_ref, wacc_ref,
             ua_ref, ug_ref, gb_ref, cz_ref, dm_ref, dua_ref, dug_ref, dgb_ref, isem, osem0, osem1, osem2):
        loads = _fetch((ua_hbm, ug_hbm, gb_hbm, cz_hbm, dm_hbm), (ua_ref, ug_ref, gb_ref, cz_ref, dm_ref), isem)
        per_put = PUT_ROWS // CONV_CH
        zp_ref[0:CONV_PAD, :] = jnp.zeros((CONV_PAD, CONV_W), F32)
        dp_ref[t:t + CONV_PAD, :] = jnp.zeros((CONV_PAD, CONV_W), F32)
        wacc_ref[...] = jnp.zeros_like(wacc_ref)
        for cp in loads:
            cp.wait()

        def pointwise(i, carry):
            dcb, dlw, dlb = carry
            r0 = pl.multiple_of(i * CONV_CH, CONV_CH)
            rows = pl.ds(r0, CONV_CH)
            zp_ref[pl.ds(CONV_PAD + r0, CONV_CH), :] = ua_ref[rows, :] * _sigmoid(ug_ref[rows, :])
            cz = cz_ref[rows, :]
            mu = jnp.mean(cz, axis=-1, keepdims=True)
            xc = cz - mu
            rs = lax.rsqrt(jnp.mean(xc * xc, axis=-1, keepdims=True) + EPS)
            xh = xc * rs
            ln = xh * lw_ref[...] + lb_ref[...]
            gbv = gb_ref[rows, :]
            dy = dm_ref[rows, :].astype(F32)
            dgb_ref[rows, :] = (dy * _silu(ln) * _dsilu(gbv)).astype(BF16)
            dl = dy * _silu(gbv) * _dsilu(ln)
            dxh = dl * lw_ref[...]
            dcz = rs * (dxh - jnp.mean(dxh, axis=-1, keepdims=True)
                        - xh * jnp.mean(dxh * xh, axis=-1, keepdims=True))
            dp_ref[rows, :] = dcz

            @pl.when(i % per_put == per_put - 1)
            def _():
                _put(dgb_ref, dgb_hbm, osem2, i // per_put).start()

            return (dcb + jnp.sum(dcz, axis=0, keepdims=True),
                    dlw + jnp.sum(dl * xh, axis=0, keepdims=True),
                    dlb + jnp.sum(dl, axis=0, keepdims=True))

        zero = jnp.zeros((1, CONV_W), F32)
        dcb, dlw, dlb = lax.fori_loop(0, t // CONV_CH, pointwise, (zero, zero, zero))
        dvec_ref[...] = jnp.zeros((8, CONV_W), F32)
        dvec_ref[0:1, :] = dcb
        dvec_ref[1:2, :] = dlw
        dvec_ref[2:3, :] = dlb

        def chunk(i, carry):
            r0 = pl.multiple_of(i * CONV_CH, CONV_CH)
            _shifted_windows(dp_ref, r0, sh_ref)
            for c in range(CONV_W // 128):
                lanes = slice(c * 128, (c + 1) * 128)

                def sub(k, carry2):
                    b0 = pl.multiple_of(k * CONV_SUB, CONV_SUB)
                    acc = [None] * CONV_ACCS
                    for j in range(CONV_TAPS):
                        off = CONV_TAPS - 1 - j
                        term = sh_ref[off % 8, c, pl.ds(b0 + 8 * (off // 8), CONV_SUB), :] * cw_ref[j:j + 1, lanes]
                        acc[j % CONV_ACCS] = term if acc[j % CONV_ACCS] is None else acc[j % CONV_ACCS] + term
                    acc = functools.reduce(lambda a, b: a + b, acc)
                    rr = pl.ds(r0 + b0, CONV_SUB)
                    sg = _sigmoid(ug_ref[rr, lanes])
                    dua_ref[rr, lanes] = (acc * sg).astype(BF16)
                    dug_ref[rr, lanes] = (acc * ua_ref[rr, lanes] * sg * (1.0 - sg)).astype(BF16)
                    return carry2

                lax.fori_loop(0, CONV_CH // CONV_SUB, sub, 0)
            _shifted_windows(zp_ref, r0, sh_ref)
            for c in range(CONV_W // 128):
                lanes = slice(c * 128, (c + 1) * 128)

                def subw(k, carry2):
                    b0 = pl.multiple_of(k * CONV_SUB, CONV_SUB)
                    dcz = dp_ref[pl.ds(r0 + b0, CONV_SUB), lanes]
                    for j in range(CONV_TAPS):
                        off = j + CONV_PAD - (CONV_TAPS - 1)
                        pr = dcz * sh_ref[off % 8, c, pl.ds(b0 + 8 * (off // 8), CONV_SUB), :]
                        parts = [pr[8 * q:8 * (q + 1)] for q in range(CONV_SUB // 8)]
                        while len(parts) > 1:
                            parts = [a + b for a, b in zip(parts[0::2], parts[1::2])]
                        wacc_ref[8 * j:8 * (j + 1), lanes] += parts[0]
                    return carry2

                lax.fori_loop(0, CONV_CH // CONV_SUB, subw, 0)

            @pl.when(i % per_put == per_put - 1)
            def _():
                _put_all(((dua_ref, dua_hbm), (dug_ref, dug_hbm)), (osem0, osem1), i // per_put)

            return carry

        lax.fori_loop(0, t // CONV_CH, chunk, 0)
        _put_wait(((dua_ref, dua_hbm), (dug_ref, dug_hbm), (dgb_ref, dgb_hbm)), (osem0, osem1, osem2), t // PUT_ROWS)
        dcw_ref[...] = jnp.zeros((16, 2 * CONV_W), F32)
        for j in range(CONV_TAPS):
            dcw_ref[j // 2:j // 2 + 1, CONV_W * (j % 2):CONV_W * (j % 2 + 1)] = jnp.sum(
                wacc_ref[8 * j:8 * (j + 1), :], axis=0, keepdims=True)

    vm = pl.BlockSpec(memory_space=pltpu.VMEM)
    hbm = pl.BlockSpec(memory_space=pl.ANY)
    return pl.pallas_call(
        body,
        name="conv_bwd",
        in_specs=[hbm] * 5 + [vm] * 3,
        out_specs=[hbm] * 3 + [vm] * 2,
        out_shape=[jax.ShapeDtypeStruct((t, CONV_W), BF16)] * 3
        + [jax.ShapeDtypeStruct((16, 2 * CONV_W), F32), jax.ShapeDtypeStruct((8, CONV_W), F32)],
        scratch_shapes=[pltpu.VMEM((t + CONV_PAD, CONV_W), F32), pltpu.VMEM((t + CONV_PAD, CONV_W), F32),
                        pltpu.VMEM((8, CONV_W // 128, CONV_CH + CONV_PAD, 128), F32), pltpu.VMEM((8 * 32, CONV_W), F32)]
        + [pltpu.VMEM((t, CONV_W), F32)] * 4 + [pltpu.VMEM((t, CONV_W), BF16)] * 4
        + [pltpu.SemaphoreType.DMA((5,))] + [pltpu.SemaphoreType.DMA((t // PUT_ROWS,))] * 3,
        compiler_params=_cparams(),
    )(ua, ug, gb, cz, dmix, cw, lw, lb)


def _out_proj(mix_a, mix_b, x, tgt, gate, w_out):
    t = x.shape[0]
    tm = 512
    nstep = t // tm

    def body(ma_ref, mb_ref, x_ref, t_ref, g_ref, w_ref, dout_ref, dma_ref, dmb_ref, gw_ref, red_ref, acc_ref):
        i = pl.program_id(0)

        @pl.when(i == 0)
        def _():
            acc_ref[...] = jnp.zeros_like(acc_ref)
            red_ref[...] = jnp.zeros_like(red_ref)

        mix = jnp.concatenate([ma_ref[...], mb_ref[...]], axis=1)
        y = jnp.dot(mix, w_ref[...], preferred_element_type=F32)
        gate_v = g_ref[...]
        err = x_ref[...] + gate_v * y - t_ref[...]
        dout = err * (1.0 / D_MODEL)
        dout_ref[...] = dout
        red_ref[0:1, :] += jnp.sum(dout * y, axis=0, keepdims=True)
        red_ref[1:2, :] += jnp.sum(err * err, axis=0, keepdims=True)
        dy = (dout * gate_v).astype(BF16)
        dmix = lax.dot_general(dy, w_ref[...], (((1,), (1,)), ((), ())), preferred_element_type=F32)
        dma_ref[...] = dmix[:, 0:512].astype(BF16)
        dmb_ref[...] = dmix[:, 512:1024].astype(BF16)
        acc_ref[...] += lax.dot_general(mix, dy, (((0,), (0,)), ((), ())), preferred_element_type=F32)

        @pl.when(i == nstep - 1)
        def _():
            gw_ref[...] = acc_ref[...].astype(BF16)

    row = lambda w: pl.BlockSpec((tm, w), lambda i: (i, 0))
    const = lambda s: pl.BlockSpec(s, lambda i: (0, 0))
    return pl.pallas_call(
        body,
        name="out_proj",
        grid=(nstep,),
        in_specs=[row(512), row(512), row(D_MODEL), row(D_MODEL), const((1, D_MODEL)),
                  pl.BlockSpec((D_MODEL, D_MODEL), lambda i: (0, 0), pipeline_mode=pl.Buffered(1))],
        out_specs=[row(D_MODEL), row(512), row(512), const((D_MODEL, D_MODEL)), const((8, D_MODEL))],
        out_shape=[jax.ShapeDtypeStruct((t, D_MODEL), F32), jax.ShapeDtypeStruct((t, 512), BF16),
                   jax.ShapeDtypeStruct((t, 512), BF16), jax.ShapeDtypeStruct((D_MODEL, D_MODEL), BF16),
                   jax.ShapeDtypeStruct((8, D_MODEL), F32)],
        scratch_shapes=[pltpu.VMEM((D_MODEL, D_MODEL), F32)],
        compiler_params=_cparams(dimension_semantics=("arbitrary",)),
    )(mix_a, mix_b, x, tgt, gate, w_out)


DPROJ_WIDTHS = (512, 256, 512, 512, 512, 512)
DPROJ_STARTS = (0, 512, 768, 1280, 1792, 2304)
WIN_W = 768
WIN_START = (0, 640, 1408, 2048)
WIN_OFF = (0, 64, 0, 64)
N_GW = N_CHIPS


def _window_pieces(s):
    lo, hi = WIN_START[s], WIN_START[s] + WIN_W
    out = []
    for p, (st, w) in enumerate(zip(DPROJ_STARTS, DPROJ_WIDTHS)):
        a, b = max(lo, st), min(hi, st + w)
        if a < b:
            out.append((p, a - st, b - a, a - lo))
    return out


def _in_proj_bwd(dparts, h, x, dout, s1, nw, wt_full, dcw, dvec, sm_a, row0):
    t = x.shape[0]
    tm = 256
    nstep = N_GW + t // tm
    n_sem = 20
    rows0 = 32
    hs = rows0 // 2
    npart = len(DPROJ_WIDTHS)

    def body(*refs):
        d_hbm, d_ref = refs[:npart], refs[npart:2 * npart]
        (x_ref, dout_ref, s1_ref, nw_ref, h_ref, wt_hbm, dcw_ref, dvec_ref, sma_ref, row0_ref,
         gx_ref, gw_hbm, ssum_ref, rows_ref,
         stg_ref, wt_ref, gt_ref, sib_ref, out_ref, in_ref, res_ref, sall_ref, red_ref, sm0_ref, ssib_ref, schip_ref, sres_ref,
         wsem, lsem, ssem, rsem) = refs[2 * npart:]
        i = pl.program_id(0)
        x_, y_, c, chips = _place()
        j = 2 * x_ + y_
        dev = 2 * j + c
        sib = (x_, y_, 1 - c)
        rc = functools.partial(_remote, ssem, rsem)
        rel_chip = [2 * cx + cy for cx, cy in chips] + [j]
        peers = [(px, py, pc) for px in (x_, 1 - x_) for py in (y_, 1 - y_) for pc in (c, 1 - c)][1:]
        wt_copy = pltpu.make_async_copy(wt_hbm, wt_ref, lsem.at[0])

        def window(case, slot):
            return [pltpu.make_async_copy(d_hbm[p].at[:, pl.ds(c0, w)], stg_ref.at[slot, :, pl.ds(w0, w)], wsem.at[slot, n])
                    for n, (p, c0, w, w0) in enumerate(_window_pieces(case))]

        def to_sibling(k):
            return rc(k, gt_ref.at[k, 1 - c], sib_ref.at[k], sib)

        def to_chip(k):
            return rc(4 + k, out_ref.at[k], in_ref.at[k], (*chips[k], c))

        def trade(k):
            to_sibling(k).wait_recv()

            def add(n, carry):
                rr = pl.ds(pl.multiple_of(n * RS_CH, RS_CH), RS_CH)
                out_ref[k, rr, :] = (gt_ref[k, c, rr, :].astype(F32) + sib_ref[k, rr, :].astype(F32)).astype(BF16)
                return carry

            lax.fori_loop(0, IN_HALF // RS_CH, add, 0)
            to_chip(k).start()

        mine_s = pl.ds(pl.multiple_of(c * hs, 8), hs)
        other_s = pl.ds(pl.multiple_of((1 - c) * hs, 8), hs)

        def small_to_sibling():
            return rc(15, sm0_ref.at[other_s], ssib_ref, sib)

        def small_to_chip(k):
            return rc(16 + k, schip_ref.at[j], schip_ref.at[j], (*chips[k], c))

        def small_share():
            return rc(19, sres_ref.at[c], sres_ref.at[c], sib)

        for k in range(N_GW):
            @pl.when(i == k)
            def _(k=k):
                slot = k % 2
                if k == 0:
                    red_ref[...] = jnp.zeros_like(red_ref)
                    wt_copy.start()
                    sm0_ref[...] = jnp.zeros_like(sm0_ref)
                    sm0_ref[0:16, :] = dcw_ref[...]
                    sm0_ref[16:17, 0:CONV_W] = dvec_ref[0:1, :]
                    sm0_ref[16:17, CONV_W:2 * CONV_W] = dvec_ref[1:2, :]
                    sm0_ref[17:18, 0:CONV_W] = dvec_ref[2:3, :]
                    for r in range(3):
                        sm0_ref[17:18, CONV_W + 128 * r:CONV_W + 128 * (r + 1)] = sma_ref[r:r + 1, :]
                    sm0_ref[18:19, :] = row0_ref[1:2, :]
                    small_to_sibling().start()
                if k == 1:
                    small_to_sibling().wait_recv()
                    schip_ref[j] = sm0_ref[mine_s, :] + ssib_ref[...]
                    for kk in range(3):
                        small_to_chip(kk).start()
                if k == N_GW - 1:
                    for kk in range(3):
                        jk = rel_chip[kk]
                        rc(16 + kk, schip_ref.at[jk], schip_ref.at[jk], sib).wait_recv()
                    tot = schip_ref[0]
                    for d in range(1, N_CHIPS):
                        tot = tot + schip_ref[d]
                    sres_ref[c] = tot
                    small_share().start()
                for case in range(N_CHIPS):
                    if k == 0:
                        @pl.when(rel_chip[0] == case)
                        def _():
                            for cp in window(case, 0):
                                cp.start()
                    if k + 1 < N_GW:
                        @pl.when(rel_chip[k + 1] == case)
                        def _():
                            for cp in window(case, 1 - slot):
                                cp.start()
                for case in range(N_CHIPS):
                    @pl.when(rel_chip[k] == case)
                    def _():
                        for cp in window(case, slot):
                            cp.wait()
                g = lax.dot_general(stg_ref[slot], h_ref[...], (((0,), (0,)), ((), ())), preferred_element_type=F32)
                for off in sorted(set(WIN_OFF)):
                    @pl.when(rel_chip[k] % 2 == (1 if off else 0))
                    def _():
                        gt_ref[k, 0] = g[off:off + IN_HALF].astype(BF16)
                        gt_ref[k, 1] = g[off + IN_HALF:off + 2 * IN_HALF].astype(BF16)
                to_sibling(k).start()
                if k >= 1:
                    trade(k - 1)

        @pl.when(i == N_GW)
        def _():
            wt_copy.wait()

        @pl.when(i >= N_GW)
        def _():
            xv = x_ref[...]
            r = lax.rsqrt(jnp.mean(xv * xv, axis=-1, keepdims=True) + EPS)
            xh = xv * r
            n = xh * nw_ref[...]
            dproj = jnp.concatenate([ref[...] for ref in d_ref], axis=1)
            dh = jnp.dot(dproj, wt_ref[...], preferred_element_type=F32)
            red_ref[0:1, :] += jnp.sum(dh, axis=0, keepdims=True)
            red_ref[1:2, :] += jnp.sum(dh * n, axis=0, keepdims=True)
            dn = dh * s1_ref[...]
            red_ref[2:3, :] += jnp.sum(dn * xh, axis=0, keepdims=True)
            dxh = dn * nw_ref[...]
            gx_ref[...] = dout_ref[...] + r * (dxh - xh * jnp.mean(dxh * xh, axis=-1, keepdims=True))

        @pl.when(i == nstep - 1)
        def _():
            sall_ref[dev] = row0_ref[...]
            sall_ref[dev, 2:5, :] = red_ref[0:3, :]
            sends = [rc(8 + k, sall_ref.at[dev], sall_ref.at[dev], peer) for k, peer in enumerate(peers)]
            for cp in sends:
                cp.start()
            sends += [to_sibling(k) for k in range(N_GW)] + [to_chip(k) for k in range(3)]
            sends += [small_to_sibling(), small_share()] + [small_to_chip(k) for k in range(3)]
            own = N_GW - 1
            to_sibling(own).wait_recv()
            for k in range(3):
                to_chip(k).wait_recv()

            def total(n, carry):
                rr = pl.ds(pl.multiple_of(n * RS_CH, RS_CH), RS_CH)
                acc = gt_ref[own, c, rr, :].astype(F32) + sib_ref[own, rr, :].astype(F32)
                for k in range(3):
                    acc = acc + in_ref[k, rr, :].astype(F32)
                res_ref[c, rr, :] = acc
                return carry

            lax.fori_loop(0, IN_HALF // RS_CH, total, 0)
            share = rc(7, res_ref.at[c], res_ref.at[c], sib)
            share.start()
            sends.append(share)
            for k, (px, py, pc) in enumerate(peers):
                pdev = 4 * px + 2 * py + pc
                rc(8 + k, sall_ref.at[pdev], sall_ref.at[pdev], (px, py, pc)).wait_recv()
            rows_ref[...] = sall_ref[...]
            rc(19, sres_ref.at[1 - c], sres_ref.at[1 - c], sib).wait_recv()
            ssum_ref[0:hs, :] = sres_ref[0]
            ssum_ref[hs:rows0, :] = sres_ref[1]
            rc(7, res_ref.at[1 - c], res_ref.at[1 - c], sib).wait_recv()
            back = pltpu.make_async_copy(res_ref, gw_hbm, lsem.at[1])
            back.start()
            for cp in sends:
                cp.wait_send()
            back.wait()

    blk = lambda i: jnp.maximum(i - N_GW, 0)
    row = lambda w: pl.BlockSpec((tm, w), lambda i: (blk(i), 0))
    vec = pl.BlockSpec((1, D_MODEL), lambda i: (0, 0))
    const = lambda shape: pl.BlockSpec(shape, lambda i: (0,) * len(shape))
    hbm = pl.BlockSpec(memory_space=pl.ANY)
    return pl.pallas_call(
        body,
        name="in_proj_bwd",
        grid=(nstep,),
        in_specs=[hbm] * npart + [row(w) for w in DPROJ_WIDTHS] + [row(D_MODEL), row(D_MODEL), vec, vec,
                  pl.BlockSpec((t, D_MODEL), lambda i: (0, 0), pipeline_mode=pl.Buffered(1)), hbm, const((16, D_MODEL)),
                  const((8, CONV_W)), const((8, 128)), const((8, D_MODEL))],
        out_specs=[row(D_MODEL), hbm, const((rows0, D_MODEL)), const((N_DEV, 8, D_MODEL))],
        out_shape=[jax.ShapeDtypeStruct((t, D_MODEL), F32), jax.ShapeDtypeStruct((2, IN_HALF, D_MODEL), F32),
                   jax.ShapeDtypeStruct((rows0, D_MODEL), F32), jax.ShapeDtypeStruct((N_DEV, 8, D_MODEL), F32)],
        scratch_shapes=[pltpu.VMEM((2, t, WIN_W), BF16), pltpu.VMEM((IN_W, D_MODEL), BF16),
                        pltpu.VMEM((N_CHIPS, 2, IN_HALF, D_MODEL), BF16), pltpu.VMEM((N_CHIPS, IN_HALF, D_MODEL), BF16),
                        pltpu.VMEM((3, IN_HALF, D_MODEL), BF16), pltpu.VMEM((3, IN_HALF, D_MODEL), BF16),
                        pltpu.VMEM((2, IN_HALF, D_MODEL), F32), pltpu.VMEM((N_DEV, 8, D_MODEL), F32),
                        pltpu.VMEM((8, D_MODEL), F32), pltpu.VMEM((rows0, D_MODEL), F32), pltpu.VMEM((hs, D_MODEL), F32),
                        pltpu.VMEM((N_CHIPS, hs, D_MODEL), F32),
                        pltpu.VMEM((2, hs, D_MODEL), F32), pltpu.SemaphoreType.DMA((2, 3)), pltpu.SemaphoreType.DMA((2,)),
                        pltpu.SemaphoreType.DMA((n_sem,)), pltpu.SemaphoreType.DMA((n_sem,))],
        compiler_params=_cparams(dimension_semantics=("arbitrary",)),
    )(*dparts, *dparts, x, dout, s1, nw, h, wt_full, dcw, dvec, sm_a, row0)


MESH = pl.DeviceIdType.MESH


def _place():
    x, y, c = lax.axis_index("x"), lax.axis_index("y"), lax.axis_index("c")
    chips = [(1 - x, y), (x, 1 - y), (1 - x, 1 - y)]
    return x, y, c, chips


def _remote(sems_s, sems_r, k, src, dst, to):
    return pltpu.make_async_remote_copy(src_ref=src, dst_ref=dst, send_sem=sems_s.at[k], recv_sem=sems_r.at[k],
                                        device_id=to, device_id_type=MESH)


RS_CH = 32
RS_SEMS = 5


def _rs_to_sibling(rc, s0, theirs, sib_ref, sib):
    cp = rc(s0, theirs, sib_ref, sib)
    cp.start()
    return cp


def _rs_trade(rc, s0, theirs, mine, sib_ref, out_ref, in_ref, rows, c, sib, chips):
    rc(s0, theirs, sib_ref, sib).wait_recv()
    cps = []
    for k, (cx, cy) in enumerate(chips):
        jk = 2 * cx + cy

        def add(i, carry, jk=jk, k=k):
            rr = pl.ds(pl.multiple_of(i * RS_CH, RS_CH), RS_CH)
            out_ref[k, rr, :] = (mine[jk, rr, :].astype(F32) + sib_ref[jk, rr, :].astype(F32)).astype(BF16)
            return carry

        lax.fori_loop(0, rows // RS_CH, add, 0)
        cps.append(rc(s0 + 1 + k, out_ref.at[k], in_ref.at[k], (cx, cy, c)))
        cps[-1].start()
    return cps


def _rs_total(rc, s0, mine, sib_ref, out_ref, in_ref, res_ref, rows, j, c, sib):
    for k in range(3):
        rc(s0 + 1 + k, out_ref.at[k], in_ref.at[k], sib).wait_recv()

    def total(i, carry):
        rr = pl.ds(pl.multiple_of(i * RS_CH, RS_CH), RS_CH)
        acc = mine[j, rr, :].astype(F32) + sib_ref[j, rr, :].astype(F32)
        for k in range(3):
            acc = acc + in_ref[k, rr, :].astype(F32)
        res_ref[c, rr, :] = acc
        return carry

    lax.fori_loop(0, rows // RS_CH, total, 0)
    cp = rc(s0 + 4, res_ref.at[c], res_ref.at[c], sib)
    cp.start()
    return cp


def _rs_done(rc, s0, res_ref, c, sib):
    rc(s0 + 4, res_ref.at[1 - c], res_ref.at[1 - c], sib).wait_recv()


def _rs_scratch(rows):
    return [pltpu.VMEM((N_CHIPS, rows, D_MODEL), BF16), pltpu.VMEM((3, rows, D_MODEL), BF16),
            pltpu.VMEM((3, rows, D_MODEL), BF16)]


MAIN_W = 640
MAIN_DST = (((0, 0, 512), (1, 0, 128)), ((2, 0, 512), (3, 0, 128)), ((3, 128, 384), (4, 0, 256)), ((4, 384, 128), (5, 0, 512)))
PAIR_DST = ((1, 128, 128), (4, 256, 128))


def _in_proj_gather(x, wt, c_row, w_ada, b_sh, nw):
    t = x.shape[0]
    ch = 256
    n_sem = 16

    def body(x_hbm, wt_ref, c_ref, wada_ref, bsh_ref, nw_ref,
             q_hbm, kv_hbm, ga_hbm, ua_hbm, ug_hbm, gb_hbm, h_hbm, w4_hbm, call_ref, ada_ref,
             x_ref, h_ref, w4_ref, stg_ref, pstg_ref, part_ref, lsem, osem, wsem, ssem, rsem):
        outs = (q_hbm, kv_hbm, ga_hbm, ua_hbm, ug_hbm, gb_hbm)
        x_, y_, c, chips = _place()
        j = 2 * x_ + y_
        dev = 2 * j + c
        sib = (x_, y_, 1 - c)
        idx = [2 * cx + cy for cx, cy in chips]
        rc = functools.partial(_remote, ssem, rsem)
        x_copy = pltpu.make_async_copy(x_hbm, x_ref, lsem.at[0])
        x_copy.start()

        def rows_of(s, cc):
            return pl.ds(pl.multiple_of(2 * IN_HALF * s + IN_HALF * cc, 16), IN_HALF)

        w4_ref[rows_of(j, 0), :] = wt_ref[0].astype(BF16)
        w4_ref[rows_of(j, 1), :] = wt_ref[1].astype(BF16)
        call_ref[dev] = c_ref[...]
        sends = []
        peers = [(px, py, pc) for px in (x_, 1 - x_) for py in (y_, 1 - y_) for pc in (c, 1 - c)][1:]
        for k, peer in enumerate(peers):
            sends.append(rc(k, call_ref.at[dev], call_ref.at[dev], peer))
        for cp in sends:
            cp.start()

        for k, (px, py, pc) in enumerate(peers):
            pdev = 4 * px + 2 * py + pc
            rc(k, call_ref.at[pdev], call_ref.at[pdev], (px, py, pc)).wait_recv()
        rowid = lax.broadcasted_iota(jnp.int32, (N_DEV, D_MODEL), 0)
        call = jnp.zeros((N_DEV, D_MODEL), F32)
        for r in range(N_DEV):
            call = jnp.where(rowid == r, jnp.broadcast_to(call_ref[r], (N_DEV, D_MODEL)), call)
        part = jnp.dot(_silu(call).astype(BF16), wada_ref[...].astype(BF16), preferred_element_type=F32) + bsh_ref[...]
        for r in range(N_DEV):
            part_ref[r] = part[r:r + 1, :]
        ada_ref[j] = part_ref[dev]
        for k, chip in enumerate(chips):
            sends.append(rc(13 + k, part_ref.at[2 * idx[k] + c], ada_ref.at[j], (*chip, c)))
            sends[-1].start()
        for k, chip in enumerate(chips):
            sends.append(rc(7 + k, w4_ref.at[rows_of(j, c)], w4_ref.at[rows_of(j, c)], (*chip, c)))
            sends[-1].start()
        for k in range(3):
            rc(13 + k, ada_ref.at[idx[k]], ada_ref.at[idx[k]], sib).wait_recv()

        shift = jnp.concatenate([ada_ref[0], ada_ref[1][:, 0:256]], axis=1)
        s1 = 1.0 + jnp.concatenate([ada_ref[1][:, 256:768], ada_ref[2][:, 0:512]], axis=1)
        x_copy.wait()

        def norm(i, carry):
            rr = pl.ds(pl.multiple_of(i * ch, ch), ch)
            xv = x_ref[rr, :]
            r = lax.rsqrt(jnp.mean(xv * xv, axis=-1, keepdims=True) + EPS)
            h_ref[rr, :] = ((xv * r) * nw_ref[...] * s1 + shift).astype(BF16)
            return carry

        lax.fori_loop(0, t // ch, norm, 0)
        h_copy = pltpu.make_async_copy(h_ref, h_hbm, lsem.at[1])
        h_copy.start()

        def put_main(case, slot):
            cps, col = [], 0
            for n, (a, c0, w) in enumerate(MAIN_DST[case]):
                cps.append(pltpu.make_async_copy(stg_ref.at[slot, :, pl.ds(col, w)], outs[a].at[:, pl.ds(c0, w)], osem.at[slot, n]))
                col += w
            return cps

        def put_pair(case, slot):
            a, c0, w = PAIR_DST[case]
            return pltpu.make_async_copy(pstg_ref.at[slot], outs[a].at[:, pl.ds(c0, w)], osem.at[slot, 2])

        def project(first_row, width, dst, slot):
            wrows = pl.ds(pl.multiple_of(first_row, 128), width)

            def blk(i, carry):
                rr = pl.ds(pl.multiple_of(i * ch, ch), ch)
                dst[slot, rr, :] = lax.dot_general(h_ref[rr, :], w4_ref[wrows, :], (((1,), (1,)), ((), ())),
                                                   preferred_element_type=F32)
                return carry

            lax.fori_loop(0, t // ch, blk, 0)

        def phase(p, s, pair):
            slot = p % 2
            if p >= 2:
                for case in range(N_CHIPS):
                    @pl.when(order[p - 2] == case)
                    def _():
                        for cp in put_main(case, slot):
                            cp.wait()
            if p == 3:
                for case in range(2):
                    @pl.when(j // 2 == case)
                    def _():
                        put_pair(case, 0).wait()
            project(2 * IN_HALF * s + 64 * (s % 2), MAIN_W, stg_ref, slot)
            for case in range(N_CHIPS):
                @pl.when(s == case)
                def _():
                    for cp in put_main(case, slot):
                        cp.start()
            if pair is not None:
                project(MAIN_W + 2 * (2 * IN_HALF) * pair, 128, pstg_ref, slot % 2 if p == 2 else 1)
                for case in range(2):
                    @pl.when(pair == case)
                    def _():
                        put_pair(case, 0 if p == 2 else 1).start()

        order = [j] + idx
        w_out = [pltpu.make_async_copy(w4_ref.at[pl.ds(pl.multiple_of(2 * IN_HALF * s, 32), 2 * IN_HALF)],
                                       w4_hbm.at[pl.ds(pl.multiple_of(2 * IN_HALF * s, 32), 2 * IN_HALF)], wsem.at[p])
                 for p, s in enumerate(order)]
        w_out[0].start()
        phase(0, j, None)
        passed = []
        for k in range(3):
            jk = idx[k]
            rc(7 + k, w4_ref.at[rows_of(jk, c)], w4_ref.at[rows_of(jk, c)], sib).wait_recv()
            passed.append(rc(10 + k, w4_ref.at[rows_of(jk, c)], w4_ref.at[rows_of(jk, c)], sib))
            passed[-1].start()
            rc(10 + k, w4_ref.at[rows_of(jk, 1 - c)], w4_ref.at[rows_of(jk, 1 - c)], sib).wait_recv()
            w_out[1 + k].start()
            if k == 0:
                phase(1, jk, None)
            elif k == 1:
                phase(2, jk, j // 2)
            else:
                phase(3, jk, 1 - j // 2)

        for case in range(N_CHIPS):
            for p in (2, 3):
                @pl.when(order[p] == case)
                def _():
                    for cp in put_main(case, p % 2):
                        cp.wait()
        for case in range(2):
            @pl.when(1 - j // 2 == case)
            def _():
                put_pair(case, 1).wait()
        h_copy.wait()
        for cp in w_out:
            cp.wait()
        for cp in sends + passed:
            cp.wait_send()

    vm = pl.BlockSpec(memory_space=pltpu.VMEM)
    hbm = pl.BlockSpec(memory_space=pl.ANY)
    widths = (512, 256, 512, 512, 512, 512)
    return pl.pallas_call(
        body,
        name="in_proj",
        in_specs=[hbm, vm, vm, vm, vm, vm],
        out_specs=[hbm] * 8 + [vm, vm],
        out_shape=[jax.ShapeDtypeStruct((t, w), F32) for w in widths]
        + [jax.ShapeDtypeStruct((t, D_MODEL), BF16), jax.ShapeDtypeStruct((IN_W, D_MODEL), BF16),
           jax.ShapeDtypeStruct((N_DEV, 1, D_MODEL), F32), jax.ShapeDtypeStruct((N_CHIPS, 1, ADA_SHARD), F32)],
        scratch_shapes=[pltpu.VMEM((t, D_MODEL), F32), pltpu.VMEM((t, D_MODEL), BF16), pltpu.VMEM((IN_W, D_MODEL), BF16),
                        pltpu.VMEM((2, t, MAIN_W), F32), pltpu.VMEM((2, t, 128), F32), pltpu.VMEM((N_DEV, 1, ADA_SHARD), F32),
                        pltpu.SemaphoreType.DMA((2,)), pltpu.SemaphoreType.DMA((2, 3)), pltpu.SemaphoreType.DMA((N_CHIPS,)),
                        pltpu.SemaphoreType.DMA((n_sem,)), pltpu.SemaphoreType.DMA((n_sem,))],
        compiler_params=_cparams(),
    )(x, wt, c_row, w_ada, b_sh, nw)


def _adamw_math(w, g, m, v):
    m2 = ADAM_B1 * m + (1.0 - ADAM_B1) * g
    v2 = ADAM_B2 * v + (1.0 - ADAM_B2) * (g * g)
    m_hat = m2 / (1.0 - ADAM_B1 ** ADAM_STEP)
    v_hat = v2 / (1.0 - ADAM_B2 ** ADAM_STEP)
    delta = -ADAM_LR * (m_hat / (jnp.sqrt(v_hat) + ADAM_EPS) + ADAM_WD * w)
    return delta, m2, v2


def _adamw(name, w, g, m, v, tm):
    r, cdim = w.shape

    def body(w_ref, g_ref, m_ref, v_ref, d_ref, m2_ref, v2_ref):
        d_ref[...], m2_ref[...], v2_ref[...] = _adamw_math(w_ref[...], g_ref[...], m_ref[...], v_ref[...])

    blk = pl.BlockSpec((tm, cdim), lambda i: (i, 0))
    return pl.pallas_call(
        body,
        name=name,
        grid=(r // tm,),
        in_specs=[blk] * 4,
        out_specs=[blk] * 3,
        out_shape=[jax.ShapeDtypeStruct((r, cdim), F32)] * 3,
        compiler_params=_cparams(dimension_semantics=("arbitrary",)),
    )(w, g, m, v)


def _adamw_ada(w, m, v, cact_t, dcols):
    r, cdim = w.shape
    tm = 256

    def body(w_ref, m_ref, v_ref, ct_ref, dc_ref, g_ref, d_ref, m2_ref, v2_ref):
        g = jnp.dot(ct_ref[...], dc_ref[...], preferred_element_type=F32, precision=lax.Precision.HIGHEST)
        g_ref[...] = g
        d_ref[...], m2_ref[...], v2_ref[...] = _adamw_math(w_ref[...], g, m_ref[...], v_ref[...])

    blk = pl.BlockSpec((tm, cdim), lambda i: (i, 0))
    return pl.pallas_call(
        body,
        name="adamw_w_ada",
        grid=(r // tm,),
        in_specs=[blk] * 3 + [pl.BlockSpec((tm, N_DEV), lambda i: (i, 0)), pl.BlockSpec((N_DEV, cdim), lambda i: (0, 0))],
        out_specs=[blk] * 4,
        out_shape=[jax.ShapeDtypeStruct((r, cdim), F32)] * 4,
        compiler_params=_cparams(dimension_semantics=("arbitrary",)),
    )(w, m, v, cact_t, dcols)


def _adamw_small(ws, ms, vs, ssum, rows):
    n = len(ws)

    def body(*refs):
        w_r, m_r, v_r = refs[0:n], refs[n:2 * n], refs[2 * n:3 * n]
        ss_ref, rows_ref = refs[3 * n], refs[3 * n + 1]
        g_r, d_r, m2_r, v2_r = (refs[3 * n + 2 + k * n:3 * n + 2 + (k + 1) * n] for k in range(4))
        loss_ref = refs[7 * n + 2]
        j = 2 * lax.axis_index("x") + lax.axis_index("y")
        rsum = rows_ref[0]
        for d in range(1, N_DEV):
            rsum = rsum + rows_ref[d]
        taps = []
        for t in range(CONV_TAPS):
            row = ss_ref[t // 2:t // 2 + 1, :]
            c0 = CONV_W * (t % 2)
            pick = row[:, c0:c0 + 128]
            for k in range(1, N_CHIPS):
                pick = jnp.where(j == k, row[:, c0 + 128 * k:c0 + 128 * (k + 1)], pick)
            taps.append(pick)
        grads = [jnp.concatenate([rsum[2:3], rsum[3:4], rsum[0:1]], axis=1), rsum[4:5],
                 ss_ref[17:18, 512:512 + HEAD_DIM], ss_ref[17:18, 640:640 + HEAD_DIM], ss_ref[17:18, 768:776],
                 None, ss_ref[16:17, 0:CONV_W], ss_ref[16:17, CONV_W:2 * CONV_W], ss_ref[17:18, 0:CONV_W]]
        for i in range(n):
            if grads[i] is None:
                for t in range(CONV_TAPS):
                    g_r[i][t:t + 1, :] = taps[t]
                g = g_r[i][...]
            else:
                g = grads[i]
                g_r[i][...] = g
            d_r[i][...], m2_r[i][...], v2_r[i][...] = _adamw_math(w_r[i][...], g, m_r[i][...], v_r[i][...])
        loss_ref[...] = (0.5 / D_MODEL) * jnp.sum(ss_ref[18:19, :], axis=1, keepdims=True)

    vm = pl.BlockSpec(memory_space=pltpu.VMEM)
    shapes = [jax.ShapeDtypeStruct(w.shape, F32) for w in ws]
    out = pl.pallas_call(
        body,
        name="adamw_small",
        in_specs=[vm] * (3 * n + 2),
        out_specs=[vm] * (4 * n + 1),
        out_shape=shapes * 4 + [jax.ShapeDtypeStruct((1, 1), F32)],
        compiler_params=_cparams(),
    )(*ws, *ms, *vs, ssum, rows)
    return out[0:n], out[n:2 * n], out[2 * n:3 * n], out[3 * n:4 * n], out[4 * n]


def _rope_tables(t):
    inv = ROPE_THETA ** (-jnp.arange(0, HEAD_DIM, 2, dtype=F32) / HEAD_DIM)
    ang = jnp.arange(t, dtype=F32)[:, None] * inv[None, :]
    cos, sin = jnp.cos(ang), jnp.sin(ang)
    return jnp.tile(cos, (1, 4)), jnp.tile(jnp.concatenate([-sin, sin], axis=1), (1, 2))


def _pad_lanes(v, width):
    return jnp.pad(v, ((0, 0), (0, width - v.shape[1])))


def kernel(x, c, w_ada, b_ada, norm_w, w_in, q_norm_w, k_norm_w, sinks, conv_w, conv_b, ln_w, ln_b, w_out, loss_target, m_w_ada, m_b_ada, m_norm_w, m_w_in, m_q_norm_w, m_k_norm_w, m_sinks, m_conv_w, m_conv_b, m_ln_w, m_ln_b, m_w_out, v_w_ada, v_b_ada, v_norm_w, v_w_in, v_q_norm_w, v_k_norm_w, v_sinks, v_conv_w, v_conv_b, v_ln_w, v_ln_b, v_w_out):
    xi, yi = lax.axis_index("x"), lax.axis_index("y")
    j = 2 * xi + yi
    x2, tgt = x[0], loss_target[0]
    t = x2.shape[0]

    wt_s, mt_s, vt_s = w_in[0].T, m_w_in[0].T, v_w_in[0].T
    cw_pad = jnp.pad(conv_w[0], ((0, 1), (0, 0)))
    b_sh = lax.dynamic_slice(b_ada, (0, ADA_SHARD * j), (1, ADA_SHARD))

    q_raw, kv_raw, ga, ua, ug, gb, h, w_full, call, ada4 = _in_proj_gather(
        x2, wt_s.reshape(2, IN_HALF, D_MODEL), c, w_ada[0], b_sh, norm_w)
    ada = ada4.reshape(1, 3 * D_MODEL)
    s1, gate = 1.0 + ada[:, D_MODEL:2 * D_MODEL], ada[:, 2 * D_MODEL:]

    cos_f, sin_s = _rope_tables(t)
    qw2, kw2 = jnp.tile(q_norm_w, (1, 2)), jnp.tile(k_norm_w, (1, 2))

    o, mix_a, wo4, cw4 = _attn_fwd(q_raw, kv_raw, ga, qw2, kw2, sinks, cos_f, sin_s,
                                   w_out[0].reshape(2, OUT_HALF, D_MODEL), cw_pad)
    w_out_full = wo4.reshape(D_MODEL, D_MODEL)
    cw_full = jnp.concatenate([cw4[i] for i in range(N_CHIPS)], axis=1)
    cz, mix_b = _conv_fwd(ua, ug, gb, cw_full, conv_b, ln_w, ln_b)
    dout, dmix_a, dmix_b, gwo_bf, red_o = _out_proj(mix_a, mix_b, x2, tgt, gate, w_out_full)

    dq, dkv, dga, sm_a, gwo = _attn_bwd(q_raw, kv_raw, ga, o, dmix_a, qw2, kw2, sinks, cos_f, sin_s,
                                        gwo_bf.reshape(N_CHIPS, 2, OUT_HALF, D_MODEL))
    dua, dug, dgb, dcw, dvec = _conv_bwd(ua, ug, gb, cz, dmix_b, cw_full, ln_w, ln_b)
    dparts = (dq, dkv, dga, dua, dug, dgb)

    grad_x, gw, ssum, rows = _in_proj_bwd(dparts, h, x2, dout, s1, norm_w, w_full, dcw, dvec, sm_a, red_o)

    gt_w_in = gw.reshape(2 * IN_HALF, D_MODEL)
    g_w_out = gwo.reshape(D_MODEL // N_CHIPS, D_MODEL)
    d_ada_all = jnp.concatenate([rows[:, 2], rows[:, 3], rows[:, 0]], axis=1)
    dcols = lax.dynamic_slice(d_ada_all, (0, ADA_SHARD * j), (N_DEV, ADA_SHARD))
    cact_t = jax.nn.silu(call.reshape(N_DEV, D_MODEL)).T

    g_w_ada, d_w_ada, nm_w_ada, nv_w_ada = _adamw_ada(w_ada[0], m_w_ada[0], v_w_ada[0], cact_t, dcols)
    dt_w_in, nmt_w_in, nvt_w_in = _adamw("adamw_w_in", wt_s, gt_w_in, mt_s, vt_s, 176)
    g_w_in, d_w_in, nm_w_in, nv_w_in = gt_w_in.T, dt_w_in.T, nmt_w_in.T, nvt_w_in.T
    d_w_out, nm_w_out, nv_w_out = _adamw("adamw_w_out", w_out[0], g_w_out, m_w_out[0], v_w_out[0], 128)
    ws = [b_ada, norm_w, q_norm_w, k_norm_w, sinks, conv_w[0], conv_b, ln_w, ln_b]
    ms = [m_b_ada, m_norm_w, m_q_norm_w, m_k_norm_w, m_sinks, m_conv_w[0], m_conv_b, m_ln_w, m_ln_b]
    vs = [v_b_ada, v_norm_w, v_q_norm_w, v_k_norm_w, v_sinks, v_conv_w[0], v_conv_b, v_ln_w, v_ln_b]
    gs, ds, nms, nvs, loss11 = _adamw_small(ws, ms, vs, ssum, rows)
    loss = loss11[0, 0]

    def order(ada_v, in_v, out_v, sm):
        b, nw_, qw_, kw_, sk_, cw_, cb_, lw_, lb_ = sm
        return [ada_v[None], b, nw_, in_v[None], qw_, kw_, sk_, cw_[None], cb_, lw_, lb_, out_v[None]]

    grads = order(g_w_ada, g_w_in, g_w_out, gs)
    deltas = order(d_w_ada, d_w_in, d_w_out, ds)
    new_m = order(nm_w_ada, nm_w_in, nm_w_out, nms)
    new_v = order(nv_w_ada, nv_w_in, nv_w_out, nvs)
    return (loss, grad_x[None], *grads, *deltas, *new_m, *new_v)
```

```python
import functools

import jax
import jax.numpy as jnp
from jax import lax
from jax.experimental import pallas as pl
from jax.experimental.pallas import tpu as pltpu

F32 = jnp.float32
BF16 = jnp.bfloat16

D_MODEL = 1024
ATTN_W = 512
KV_W = 128
CONV_W = 512
IN_W = 2816
HEAD_DIM = 64
CONV_TAPS = 31
QBLK = 128
EPS = 1e-6
ROPE_THETA = 10000.0

ADAM_LR = 0.001
ADAM_B1 = 0.9
ADAM_B2 = 0.999
ADAM_EPS = 1e-08
ADAM_WD = 0.01
ADAM_STEP = 10

N_CHIPS = 4
N_DEV = 8
IN_HALF = IN_W // N_CHIPS // 2
OUT_HALF = D_MODEL // N_CHIPS // 2
ADA_SHARD = 3 * D_MODEL // N_CHIPS

VMEM_LIMIT = 56 * 1024 * 1024
CONV_PAD = 32


def _cparams(**kw):
    return pltpu.CompilerParams(vmem_limit_bytes=VMEM_LIMIT, **kw)


def _sigmoid(v):
    return 1.0 / (1.0 + jnp.exp(-v))


def _silu(v):
    return v * _sigmoid(v)


def _dsilu(v):
    s = _sigmoid(v)
    return s * (1.0 + v * (1.0 - s))


def _lane(shape):
    return lax.broadcasted_iota(jnp.int32, shape, len(shape) - 1)


PUT_ROWS = 512


def _fetch(hbm_refs, vmem_refs, sem):
    cps = [pltpu.make_async_copy(h, v, sem.at[i]) for i, (h, v) in enumerate(zip(hbm_refs, vmem_refs))]
    for cp in cps:
        cp.start()
    return cps


def _put(vmem_ref, hbm_ref, sem, m):
    r = pl.ds(pl.multiple_of(m * PUT_ROWS, PUT_ROWS), PUT_ROWS)
    return pltpu.make_async_copy(vmem_ref.at[r], hbm_ref.at[r], sem.at[m])


def _put_all(pairs, sems, m):
    for (v, h), sem in zip(pairs, sems):
        _put(v, h, sem, m).start()


def _put_wait(pairs, sems, n):
    for (v, h), sem in zip(pairs, sems):
        for m in range(n):
            _put(v, h, sem, m).wait()


def _head_mean(s, left):
    sl = jnp.sum(jnp.where(left, s, 0.0), axis=-1, keepdims=True)
    sr = jnp.sum(jnp.where(left, 0.0, s), axis=-1, keepdims=True)
    return jnp.where(left, sl, sr) * (1.0 / HEAD_DIM)


def _rot(v, first):
    return jnp.where(first, pltpu.roll(v, 96, 1), pltpu.roll(v, 32, 1))


def _norm_rope(v, w, cos, sin_s, left, first):
    r = lax.rsqrt(_head_mean(v * v, left) + EPS)
    xh = v * r
    n = xh * w
    return n * cos + _rot(n, first) * sin_s, xh, r


def _norm_rope_bwd(d, xh, r, w, cos, sin_s, left, first):
    dn = d * cos - _rot(d, first) * sin_s
    dw = jnp.sum(dn * xh, axis=0, keepdims=True)
    dxh = dn * w
    return r * (dxh - xh * _head_mean(dxh * xh, left)), dw


def _dup_heads(v, left):
    sw = pltpu.roll(v, 64, 1)
    return jnp.where(left, v, sw), jnp.where(left, sw, v)


def _prep_kv(kv_ref, kw_ref, cos_ref, sin_ref, ka_ref, va_ref, t):
    ch = 256
    for g in range(2):
        ka_ref[g, 0:QBLK, :] = jnp.zeros((QBLK, 128), BF16)
        va_ref[g, 0:QBLK, :] = jnp.zeros((QBLK, 128), BF16)

    def chunk(i, carry):
        r0 = pl.multiple_of(i * ch, ch)
        left = _lane((ch, 128)) < 64
        first = (_lane((ch, 128)) % 64) < 32
        k = kv_ref[pl.ds(r0, ch), 0:128]
        v = kv_ref[pl.ds(r0, ch), 128:256]
        kr, _, _ = _norm_rope(k, kw_ref[...], cos_ref[pl.ds(r0, ch), :], sin_ref[pl.ds(r0, ch), :], left, first)
        k0, k1 = _dup_heads(kr, left)
        v0, v1 = _dup_heads(v, left)
        ka_ref[0, pl.ds(QBLK + r0, ch), :] = k0.astype(BF16)
        ka_ref[1, pl.ds(QBLK + r0, ch), :] = k1.astype(BF16)
        va_ref[0, pl.ds(QBLK + r0, ch), :] = v0.astype(BF16)
        va_ref[1, pl.ds(QBLK + r0, ch), :] = v1.astype(BF16)
        return carry

    lax.fori_loop(0, t // ch, chunk, 0)


def _band_mask(n):
    qi = lax.broadcasted_iota(jnp.int32, (2 * QBLK, 2 * QBLK), 0) % QBLK
    kj = lax.broadcasted_iota(jnp.int32, (2 * QBLK, 2 * QBLK), 1)
    local = (kj > qi) & (kj <= qi + QBLK)
    return local & ((n > 0) | (kj >= QBLK))


def _softmax_pair(s, mask, sink0, sink1):
    row = lax.broadcasted_iota(jnp.int32, (2 * QBLK, 1), 0)
    sink = jnp.where(row < QBLK, sink0, sink1)
    s = jnp.where(mask, s, -jnp.inf)
    m = jnp.maximum(jnp.max(s, axis=-1, keepdims=True), sink)
    e = jnp.exp(s - m)
    es = jnp.exp(sink - m)
    inv = 1.0 / (jnp.sum(e, axis=-1, keepdims=True) + es)
    return e * inv, es * inv


def _stack_heads(v, left):
    return jnp.concatenate([jnp.where(left, v, 0.0), jnp.where(left, 0.0, v)], axis=0)


def _attn_fwd(q_raw, kv_raw, ga, qw2, kw2, sinks, cos_f, sin_s, wo, cw):
    t = q_raw.shape[0]
    nblk = t // QBLK
    per_put = PUT_ROWS // QBLK

    def body(q_hbm, kv_ref, ga_hbm, qw_ref, kw_ref, sk_ref, cos_hbm, sin_hbm, wo_ref, cw_ref,
             o_hbm, mix_hbm, wo4_ref, cw4_ref, ka_ref, va_ref, q_ref, ga_ref, o_ref, mix_ref, cos_ref, sin_ref,
             isem, osem0, osem1, ssem, rsem):
        loads = _fetch((cos_hbm, sin_hbm, q_hbm, ga_hbm), (cos_ref, sin_ref, q_ref, ga_ref), isem)
        outs, osems = ((o_ref, o_hbm), (mix_ref, mix_hbm)), (osem0, osem1)
        x, y, c, chips = _place()
        j = 2 * x + y
        sib = (x, y, 1 - c)
        idx = [2 * cx + cy for cx, cy in chips]
        rc = functools.partial(_remote, ssem, rsem)
        wo4_ref[j] = wo_ref[...].astype(BF16)
        cw4_ref[j] = cw_ref[...]
        sends = []
        for k, chip in enumerate(chips):
            sends.append(rc(k, wo4_ref.at[j, c], wo4_ref.at[j, c], (*chip, c)))
            sends.append(rc(6 + k, cw4_ref.at[j], cw4_ref.at[j], (*chip, c)))
        for cp in sends:
            cp.start()

        loads[0].wait()
        loads[1].wait()
        _prep_kv(kv_ref, kw_ref, cos_ref, sin_ref, ka_ref, va_ref, t)
        loads[2].wait()
        loads[3].wait()

        def blk(n, carry):
            r0 = pl.multiple_of(n * QBLK, QBLK)
            left = _lane((QBLK, 128)) < 64
            first = (_lane((QBLK, 128)) % 64) < 32
            cos = cos_ref[pl.ds(r0, QBLK), :]
            sin = sin_ref[pl.ds(r0, QBLK), :]
            mask = _band_mask(n)
            for p in range(4):
                g = p // 2
                lanes = slice(p * 128, (p + 1) * 128)
                qr, _, _ = _norm_rope(q_ref[pl.ds(r0, QBLK), lanes], qw_ref[...], cos, sin, left, first)
                q2 = _stack_heads(qr * 0.125, left).astype(BF16)
                s = lax.dot_general(q2, ka_ref[g, pl.ds(r0, 2 * QBLK), :], (((1,), (1,)), ((), ())),
                                    preferred_element_type=F32)
                pm, _ = _softmax_pair(s, mask, sk_ref[0, 2 * p], sk_ref[0, 2 * p + 1])
                o2 = jnp.dot(pm.astype(BF16), va_ref[g, pl.ds(r0, 2 * QBLK), :], preferred_element_type=F32)
                o = jnp.where(left, o2[0:QBLK], o2[QBLK:2 * QBLK])
                o_ref[pl.ds(r0, QBLK), lanes] = o.astype(BF16)
                mix_ref[pl.ds(r0, QBLK), lanes] = (o * _silu(ga_ref[pl.ds(r0, QBLK), lanes])).astype(BF16)

            @pl.when(n % per_put == per_put - 1)
            def _():
                _put_all(outs, osems, n // per_put)

            return carry

        lax.fori_loop(0, nblk, blk, 0)
        _put_wait(outs, osems, t // PUT_ROWS)

        passed = []
        for k, chip in enumerate(chips):
            jk = idx[k]
            rc(k, wo4_ref.at[jk, c], wo4_ref.at[jk, c], sib).wait_recv()
            passed.append(rc(3 + k, wo4_ref.at[jk, c], wo4_ref.at[jk, c], sib))
            passed[-1].start()
        for k, chip in enumerate(chips):
            jk = idx[k]
            rc(3 + k, wo4_ref.at[jk, 1 - c], wo4_ref.at[jk, 1 - c], sib).wait_recv()
            rc(6 + k, cw4_ref.at[jk], cw4_ref.at[jk], sib).wait_recv()
        for cp in sends + passed:
            cp.wait_send()

    vm = pl.BlockSpec(memory_space=pltpu.VMEM)
    hbm = pl.BlockSpec(memory_space=pl.ANY)
    n_sem = 9
    return pl.pallas_call(
        body,
        name="attn_fwd",
        in_specs=[hbm, vm, hbm, vm, vm, pl.BlockSpec(memory_space=pltpu.SMEM), hbm, hbm, vm, vm],
        out_specs=[hbm, hbm, vm, vm],
        out_shape=[jax.ShapeDtypeStruct((t, ATTN_W), BF16), jax.ShapeDtypeStruct((t, ATTN_W), BF16),
                   jax.ShapeDtypeStruct((N_CHIPS, 2, OUT_HALF, D_MODEL), BF16),
                   jax.ShapeDtypeStruct((N_CHIPS, 32, 128), F32)],
        scratch_shapes=[pltpu.VMEM((2, t + QBLK, 128), BF16), pltpu.VMEM((2, t + QBLK, 128), BF16),
                        pltpu.VMEM((t, ATTN_W), F32), pltpu.VMEM((t, ATTN_W), F32),
                        pltpu.VMEM((t, ATTN_W), BF16), pltpu.VMEM((t, ATTN_W), BF16),
                        pltpu.VMEM((t, 128), F32), pltpu.VMEM((t, 128), F32),
                        pltpu.SemaphoreType.DMA((4,)), pltpu.SemaphoreType.DMA((t // PUT_ROWS,)),
                        pltpu.SemaphoreType.DMA((t // PUT_ROWS,)),
                        pltpu.SemaphoreType.DMA((n_sem,)), pltpu.SemaphoreType.DMA((n_sem,))],
        compiler_params=_cparams(),
    )(q_raw, kv_raw, ga, qw2, kw2, sinks, cos_f, sin_s, wo, cw)


def _attn_bwd(q_raw, kv_raw, ga, o, dmix, qw2, kw2, sinks, cos_f, sin_s, go):
    t = q_raw.shape[0]
    nblk = t // QBLK
    per_put = PUT_ROWS // QBLK

    def body(q_hbm, kv_ref, ga_hbm, o_hbm, dm_hbm, qw_ref, kw_ref, sk_ref, cos_hbm, sin_hbm, go_ref,
             dq_hbm, dkv_ref, dga_hbm, sm_ref, gwo_ref, ka_ref, va_ref, dka_ref, dva_ref,
             sibo_ref, outo_ref, ino_ref, q_ref, ga_ref, o_ref, dm_ref, dq_ref, dga_ref, cos_ref, sin_ref,
             isem, osem0, osem1, ssem, rsem):
        loads = _fetch((cos_hbm, sin_hbm, q_hbm, ga_hbm, o_hbm, dm_hbm), (cos_ref, sin_ref, q_ref, ga_ref, o_ref, dm_ref), isem)
        outs, osems = ((dq_ref, dq_hbm), (dga_ref, dga_hbm)), (osem0, osem1)
        x, y, c, chips = _place()
        sib = (x, y, 1 - c)
        rc = functools.partial(_remote, ssem, rsem)
        theirs, mine = go_ref.at[:, 1 - c], go_ref.at[:, c]
        sends = [_rs_to_sibling(rc, 0, theirs, sibo_ref, sib)]
        loads[0].wait()
        loads[1].wait()
        _prep_kv(kv_ref, kw_ref, cos_ref, sin_ref, ka_ref, va_ref, t)
        dka_ref[...] = jnp.zeros_like(dka_ref)
        dva_ref[...] = jnp.zeros_like(dva_ref)
        sends += _rs_trade(rc, 0, theirs, mine, sibo_ref, outo_ref, ino_ref, OUT_HALF, c, sib, chips)
        for cp in loads[2:]:
            cp.wait()

        def blk(n, carry):
            dqw, dsk = carry
            r0 = pl.multiple_of(n * QBLK, QBLK)
            left = _lane((QBLK, 128)) < 64
            first = (_lane((QBLK, 128)) % 64) < 32
            cos = cos_ref[pl.ds(r0, QBLK), :]
            sin = sin_ref[pl.ds(r0, QBLK), :]
            mask = _band_mask(n)
            row = lax.broadcasted_iota(jnp.int32, (2 * QBLK, 1), 0)
            for p in range(4):
                g = p // 2
                lanes = slice(p * 128, (p + 1) * 128)
                rows = pl.ds(r0, QBLK)
                win = pl.ds(r0, 2 * QBLK)
                qr, xh, r = _norm_rope(q_ref[rows, lanes], qw_ref[...], cos, sin, left, first)
                q2 = _stack_heads(qr * 0.125, left).astype(BF16)
                kwin = ka_ref[g, win, :]
                vwin = va_ref[g, win, :]
                s = lax.dot_general(q2, kwin, (((1,), (1,)), ((), ())), preferred_element_type=F32)
                pm, ps = _softmax_pair(s, mask, sk_ref[0, 2 * p], sk_ref[0, 2 * p + 1])
                gav = ga_ref[rows, lanes]
                dmv = dm_ref[rows, lanes].astype(F32)
                dga_ref[rows, lanes] = (dmv * o_ref[rows, lanes].astype(F32) * _dsilu(gav)).astype(BF16)
                do2 = _stack_heads(dmv * _silu(gav), left).astype(BF16)
                dp = lax.dot_general(do2, vwin, (((1,), (1,)), ((), ())), preferred_element_type=F32)
                delta = jnp.sum(pm * dp, axis=-1, keepdims=True)
                ds = (pm * (dp - delta)).astype(BF16)
                pd = ps * delta
                d0 = jnp.sum(jnp.where(row < QBLK, pd, 0.0), axis=0, keepdims=True)
                d1 = jnp.sum(jnp.where(row < QBLK, 0.0, pd), axis=0, keepdims=True)
                l8 = _lane((1, 128))
                dsk = dsk - jnp.where(l8 == 2 * p, d0, 0.0) - jnp.where(l8 == 2 * p + 1, d1, 0.0)
                dva_ref[g, win, :] += lax.dot_general(pm.astype(BF16), do2, (((0,), (0,)), ((), ())),
                                                      preferred_element_type=F32)
                dka_ref[g, win, :] += lax.dot_general(ds, q2, (((0,), (0,)), ((), ())),
                                                      preferred_element_type=F32)
                dq2 = jnp.dot(ds, kwin, preferred_element_type=F32)
                dqr = jnp.where(left, dq2[0:QBLK], dq2[QBLK:2 * QBLK]) * 0.125
                dq, dw = _norm_rope_bwd(dqr, xh, r, qw_ref[...], cos, sin, left, first)
                dq_ref[rows, lanes] = dq.astype(BF16)
                dqw = dqw + dw

            @pl.when(n % per_put == per_put - 1)
            def _():
                _put_all(outs, osems, n // per_put)

            return dqw, dsk

        zero = jnp.zeros((1, 128), F32)
        dqw, dsk = lax.fori_loop(0, nblk, blk, (zero, zero))

        ch = 256

        def chunk(i, dkw):
            r0 = pl.multiple_of(i * ch, ch)
            left = _lane((ch, 128)) < 64
            first = (_lane((ch, 128)) % 64) < 32
            rows = pl.ds(r0, ch)
            prow = pl.ds(QBLK + r0, ch)

            def fold(ref):
                a0 = ref[0, prow, :]
                a1 = ref[1, prow, :]
                return jnp.where(left, a0 + pltpu.roll(a0, 64, 1), a1 + pltpu.roll(a1, 64, 1))

            cos = cos_ref[rows, :]
            sin = sin_ref[rows, :]
            _, xh, r = _norm_rope(kv_ref[rows, 0:128], kw_ref[...], cos, sin, left, first)
            dk, dw = _norm_rope_bwd(fold(dka_ref), xh, r, kw_ref[...], cos, sin, left, first)
            dkv_ref[rows, 0:128] = dk.astype(BF16)
            dkv_ref[rows, 128:256] = fold(dva_ref).astype(BF16)
            return dkw + dw

        dkw = lax.fori_loop(0, t // ch, chunk, zero)
        sm_ref[...] = jnp.zeros((8, 128), F32)
        sm_ref[0:1, :] = dqw + pltpu.roll(dqw, 64, 1)
        sm_ref[1:2, :] = dkw + pltpu.roll(dkw, 64, 1)
        sm_ref[2:3, :] = dsk

        j = 2 * x + y
        sends.append(_rs_total(rc, 0, mine, sibo_ref, outo_ref, ino_ref, gwo_ref, OUT_HALF, j, c, sib))
        _rs_done(rc, 0, gwo_ref, c, sib)
        for cp in sends:
            cp.wait_send()
        _put_wait(outs, osems, t // PUT_ROWS)

    vm = pl.BlockSpec(memory_space=pltpu.VMEM)
    hbm = pl.BlockSpec(memory_space=pl.ANY)
    return pl.pallas_call(
        body,
        name="attn_bwd",
        in_specs=[hbm, vm, hbm, hbm, hbm, vm, vm, pl.BlockSpec(memory_space=pltpu.SMEM), hbm, hbm, vm],
        out_specs=[hbm, vm, hbm, vm, vm],
        out_shape=[jax.ShapeDtypeStruct((t, ATTN_W), BF16), jax.ShapeDtypeStruct((t, 2 * KV_W), BF16),
                   jax.ShapeDtypeStruct((t, ATTN_W), BF16), jax.ShapeDtypeStruct((8, 128), F32),
                   jax.ShapeDtypeStruct((2, OUT_HALF, D_MODEL), F32)],
        scratch_shapes=[pltpu.VMEM((2, t + QBLK, 128), BF16), pltpu.VMEM((2, t + QBLK, 128), BF16),
                        pltpu.VMEM((2, t + QBLK, 128), F32), pltpu.VMEM((2, t + QBLK, 128), F32)]
        + _rs_scratch(OUT_HALF)
        + [pltpu.VMEM((t, ATTN_W), F32), pltpu.VMEM((t, ATTN_W), F32), pltpu.VMEM((t, ATTN_W), BF16),
           pltpu.VMEM((t, ATTN_W), BF16), pltpu.VMEM((t, ATTN_W), BF16), pltpu.VMEM((t, ATTN_W), BF16),
           pltpu.VMEM((t, 128), F32), pltpu.VMEM((t, 128), F32),
           pltpu.SemaphoreType.DMA((6,)), pltpu.SemaphoreType.DMA((t // PUT_ROWS,)), pltpu.SemaphoreType.DMA((t // PUT_ROWS,)),
           pltpu.SemaphoreType.DMA((RS_SEMS,)), pltpu.SemaphoreType.DMA((RS_SEMS,))],
        compiler_params=_cparams(),
    )(q_raw, kv_raw, ga, o, dmix, qw2, kw2, sinks, cos_f, sin_s, go)


CONV_CH = 256
CONV_SUB = 64
CONV_ACCS = 3


def _shifted_windows(src_ref, r0, sh_ref):
    rows = CONV_CH + CONV_PAD
    win = src_ref[pl.ds(r0, rows), :]
    for b in range(8):
        sh = win if b == 0 else pltpu.roll(win, rows - b, 0)
        for c in range(CONV_W // 128):
            sh_ref[b, c] = sh[:, c * 128:(c + 1) * 128]


def _conv_fwd(ua, ug, gb, cw, cb, lw, lb):
    t = ua.shape[0]

    def body(ua_hbm, ug_hbm, gb_hbm, cw_ref, cb_ref, lw_ref, lb_ref, cz_hbm, mix_hbm, zp_ref, sh_ref,
             ua_ref, ug_ref, gb_ref, cz_ref, mix_ref, isem, osem0, osem1):
        loads = _fetch((ua_hbm, ug_hbm, gb_hbm), (ua_ref, ug_ref, gb_ref), isem)
        outs, osems = ((cz_ref, cz_hbm), (mix_ref, mix_hbm)), (osem0, osem1)
        per_put = PUT_ROWS // CONV_CH
        zp_ref[0:CONV_PAD, :] = jnp.zeros((CONV_PAD, CONV_W), F32)
        loads[0].wait()
        loads[1].wait()

        def glu(i, carry):
            r0 = pl.multiple_of(i * CONV_CH, CONV_CH)
            rows = pl.ds(r0, CONV_CH)
            zp_ref[pl.ds(CONV_PAD + r0, CONV_CH), :] = ua_ref[rows, :] * _sigmoid(ug_ref[rows, :])
            return carry

        lax.fori_loop(0, t // CONV_CH, glu, 0)
        loads[2].wait()

        def chunk(i, carry):
            r0 = pl.multiple_of(i * CONV_CH, CONV_CH)
            _shifted_windows(zp_ref, r0, sh_ref)
            for c in range(CONV_W // 128):
                lanes = slice(c * 128, (c + 1) * 128)

                def sub(k, carry2):
                    b0 = pl.multiple_of(k * CONV_SUB, CONV_SUB)
                    acc = [jnp.broadcast_to(cb_ref[0:1, lanes], (CONV_SUB, 128))] + [None] * (CONV_ACCS - 1)
                    for j in range(CONV_TAPS):
                        off = j + CONV_PAD - (CONV_TAPS - 1)
                        term = sh_ref[off % 8, c, pl.ds(b0 + 8 * (off // 8), CONV_SUB), :] * cw_ref[j:j + 1, lanes]
                        acc[j % CONV_ACCS] = term if acc[j % CONV_ACCS] is None else acc[j % CONV_ACCS] + term
                    cz_ref[pl.ds(r0 + b0, CONV_SUB), lanes] = functools.reduce(lambda a, b: a + b, acc)
                    return carry2

                lax.fori_loop(0, CONV_CH // CONV_SUB, sub, 0)
            rows = pl.ds(r0, CONV_CH)
            cz = cz_ref[rows, :]
            mu = jnp.mean(cz, axis=-1, keepdims=True)
            xc = cz - mu
            rs = lax.rsqrt(jnp.mean(xc * xc, axis=-1, keepdims=True) + EPS)
            ln = xc * rs * lw_ref[...] + lb_ref[...]
            mix_ref[rows, :] = (_silu(ln) * _silu(gb_ref[rows, :])).astype(BF16)

            @pl.when(i % per_put == per_put - 1)
            def _():
                _put_all(outs, osems, i // per_put)

            return carry

        lax.fori_loop(0, t // CONV_CH, chunk, 0)
        _put_wait(outs, osems, t // PUT_ROWS)

    vm = pl.BlockSpec(memory_space=pltpu.VMEM)
    hbm = pl.BlockSpec(memory_space=pl.ANY)
    nput = t // PUT_ROWS
    return pl.pallas_call(
        body,
        name="conv_fwd",
        in_specs=[hbm] * 3 + [vm] * 4,
        out_specs=[hbm, hbm],
        out_shape=[jax.ShapeDtypeStruct((t, CONV_W), F32), jax.ShapeDtypeStruct((t, CONV_W), BF16)],
        scratch_shapes=[pltpu.VMEM((t + CONV_PAD, CONV_W), F32),
                        pltpu.VMEM((8, CONV_W // 128, CONV_CH + CONV_PAD, 128), F32),
                        pltpu.VMEM((t, CONV_W), F32), pltpu.VMEM((t, CONV_W), F32), pltpu.VMEM((t, CONV_W), F32),
                        pltpu.VMEM((t, CONV_W), F32), pltpu.VMEM((t, CONV_W), BF16),
                        pltpu.SemaphoreType.DMA((3,)), pltpu.SemaphoreType.DMA((nput,)), pltpu.SemaphoreType.DMA((nput,))],
        compiler_params=_cparams(),
    )(ua, ug, gb, cw, cb, lw, lb)


def _conv_bwd(ua, ug, gb, cz, dmix, cw, lw, lb):
    t = ua.shape[0]

    def body(ua_hbm, ug_hbm, gb_hbm, cz_hbm, dm_hbm, cw_ref, lw_ref, lb_ref,
             dua_hbm, dug_hbm, dgb_hbm, dcw_ref, dvec_ref, zp_ref, dp_ref, sh_ref, wacc_ref,
             ua_ref, ug_ref, gb_ref, cz_ref, dm_ref, dua_ref, dug_ref, dgb_ref, isem, osem0, osem1, osem2):
        loads = _fetch((ua_hbm, ug_hbm, gb_hbm, cz_hbm, dm_hbm), (ua_ref, ug_ref, gb_ref, cz_ref, dm_ref), isem)
        per_put = PUT_ROWS // CONV_CH
        zp_ref[0:CONV_PAD, :] = jnp.zeros((CONV_PAD, CONV_W), F32)
        dp_ref[t:t + CONV_PAD, :] = jnp.zeros((CONV_PAD, CONV_W), F32)
        wacc_ref[...] = jnp.zeros_like(wacc_ref)
        for cp in loads:
            cp.wait()

        def pointwise(i, carry):
            dcb, dlw, dlb = carry
            r0 = pl.multiple_of(i * CONV_CH, CONV_CH)
            rows = pl.ds(r0, CONV_CH)
            zp_ref[pl.ds(CONV_PAD + r0, CONV_CH), :] = ua_ref[rows, :] * _sigmoid(ug_ref[rows, :])
            cz = cz_ref[rows, :]
            mu = jnp.mean(cz, axis=-1, keepdims=True)
            xc = cz - mu
            rs = lax.rsqrt(jnp.mean(xc * xc, axis=-1, keepdims=True) + EPS)
            xh = xc * rs
            ln = xh * lw_ref[...] + lb_ref[...]
            gbv = gb_ref[rows, :]
            dy = dm_ref[rows, :].astype(F32)
            dgb_ref[rows, :] = (dy * _silu(ln) * _dsilu(gbv)).astype(BF16)
            dl = dy * _silu(gbv) * _dsilu(ln)
            dxh = dl * lw_ref[...]
            dcz = rs * (dxh - jnp.mean(dxh, axis=-1, keepdims=True)
                        - xh * jnp.mean(dxh * xh, axis=-1, keepdims=True))
            dp_ref[rows, :] = dcz

            @pl.when(i % per_put == per_put - 1)
            def _():
                _put(dgb_ref, dgb_hbm, osem2, i // per_put).start()

            return (dcb + jnp.sum(dcz, axis=0, keepdims=True),
                    dlw + jnp.sum(dl * xh, axis=0, keepdims=True),
                    dlb + jnp.sum(dl, axis=0, keepdims=True))

        zero = jnp.zeros((1, CONV_W), F32)
        dcb, dlw, dlb = lax.fori_loop(0, t // CONV_CH, pointwise, (zero, zero, zero))
        dvec_ref[...] = jnp.zeros((8, CONV_W), F32)
        dvec_ref[0:1, :] = dcb
        dvec_ref[1:2, :] = dlw
        dvec_ref[2:3, :] = dlb

        def chunk(i, carry):
            r0 = pl.multiple_of(i * CONV_CH, CONV_CH)
            _shifted_windows(dp_ref, r0, sh_ref)
            for c in range(CONV_W // 128):
                lanes = slice(c * 128, (c + 1) * 128)

                def sub(k, carry2):
                    b0 = pl.multiple_of(k * CONV_SUB, CONV_SUB)
                    acc = [None] * CONV_ACCS
                    for j in range(CONV_TAPS):
                        off = CONV_TAPS - 1 - j
                        term = sh_ref[off % 8, c, pl.ds(b0 + 8 * (off // 8), CONV_SUB), :] * cw_ref[j:j + 1, lanes]
                        acc[j % CONV_ACCS] = term if acc[j % CONV_ACCS] is None else acc[j % CONV_ACCS] + term
                    acc = functools.reduce(lambda a, b: a + b, acc)
                    rr = pl.ds(r0 + b0, CONV_SUB)
                    sg = _sigmoid(ug_ref[rr, lanes])
                    dua_ref[rr, lanes] = (acc * sg).astype(BF16)
                    dug_ref[rr, lanes] = (acc * ua_ref[rr, lanes] * sg * (1.0 - sg)).astype(BF16)
                    return carry2

                lax.fori_loop(0, CONV_CH // CONV_SUB, sub, 0)
            _shifted_windows(zp_ref, r0, sh_ref)
            for c in range(CONV_W // 128):
                lanes = slice(c * 128, (c + 1) * 128)

                def subw(k, carry2):
                    b0 = pl.multiple_of(k * CONV_SUB, CONV_SUB)
                    dcz = dp_ref[pl.ds(r0 + b0, CONV_SUB), lanes]
                    for j in range(CONV_TAPS):
                        off = j + CONV_PAD - (CONV_TAPS - 1)
                        pr = dcz * sh_ref[off % 8, c, pl.ds(b0 + 8 * (off // 8), CONV_SUB), :]
                        parts = [pr[8 * q:8 * (q + 1)] for q in range(CONV_SUB // 8)]
                        while len(parts) > 1:
                            parts = [a + b for a, b in zip(parts[0::2], parts[1::2])]
                        wacc_ref[8 * j:8 * (j + 1), lanes] += parts[0]
                    return carry2

                lax.fori_loop(0, CONV_CH // CONV_SUB, subw, 0)

            @pl.when(i % per_put == per_put - 1)
            def _():
                _put_all(((dua_ref, dua_hbm), (dug_ref, dug_hbm)), (osem0, osem1), i // per_put)

            return carry

        lax.fori_loop(0, t // CONV_CH, chunk, 0)
        _put_wait(((dua_ref, dua_hbm), (dug_ref, dug_hbm), (dgb_ref, dgb_hbm)), (osem0, osem1, osem2), t // PUT_ROWS)
        dcw_ref[...] = jnp.zeros((16, 2 * CONV_W), F32)
        for j in range(CONV_TAPS):
            dcw_ref[j // 2:j // 2 + 1, CONV_W * (j % 2):CONV_W * (j % 2 + 1)] = jnp.sum(
                wacc_ref[8 * j:8 * (j + 1), :], axis=0, keepdims=True)

    vm = pl.BlockSpec(memory_space=pltpu.VMEM)
    hbm = pl.BlockSpec(memory_space=pl.ANY)
    return pl.pallas_call(
        body,
        name="conv_bwd",
        in_specs=[hbm] * 5 + [vm] * 3,
        out_specs=[hbm] * 3 + [vm] * 2,
        out_shape=[jax.ShapeDtypeStruct((t, CONV_W), BF16)] * 3
        + [jax.ShapeDtypeStruct((16, 2 * CONV_W), F32), jax.ShapeDtypeStruct((8, CONV_W), F32)],
        scratch_shapes=[pltpu.VMEM((t + CONV_PAD, CONV_W), F32), pltpu.VMEM((t + CONV_PAD, CONV_W), F32),
                        pltpu.VMEM((8, CONV_W // 128, CONV_CH + CONV_PAD, 128), F32), pltpu.VMEM((8 * 32, CONV_W), F32)]
        + [pltpu.VMEM((t, CONV_W), F32)] * 4 + [pltpu.VMEM((t, CONV_W), BF16)] * 4
        + [pltpu.SemaphoreType.DMA((5,))] + [pltpu.SemaphoreType.DMA((t // PUT_ROWS,))] * 3,
        compiler_params=_cparams(),
    )(ua, ug, gb, cz, dmix, cw, lw, lb)


def _out_proj(mix_a, mix_b, x, tgt, gate, w_out):
    t = x.shape[0]
    tm = 512
    nstep = t // tm

    def body(ma_ref, mb_ref, x_ref, t_ref, g_ref, w_ref, dout_ref, dma_ref, dmb_ref, gw_ref, red_ref, acc_ref):
        i = pl.program_id(0)

        @pl.when(i == 0)
        def _():
            acc_ref[...] = jnp.zeros_like(acc_ref)
            red_ref[...] = jnp.zeros_like(red_ref)

        mix = jnp.concatenate([ma_ref[...], mb_ref[...]], axis=1)
        y = jnp.dot(mix, w_ref[...], preferred_element_type=F32)
        gate_v = g_ref[...]
        err = x_ref[...] + gate_v * y - t_ref[...]
        dout = err * (1.0 / D_MODEL)
        dout_ref[...] = dout
        red_ref[0:1, :] += jnp.sum(dout * y, axis=0, keepdims=True)
        red_ref[1:2, :] += jnp.sum(err * err, axis=0, keepdims=True)
        dy = (dout * gate_v).astype(BF16)
        dmix = lax.dot_general(dy, w_ref[...], (((1,), (1,)), ((), ())), preferred_element_type=F32)
        dma_ref[...] = dmix[:, 0:512].astype(BF16)
        dmb_ref[...] = dmix[:, 512:1024].astype(BF16)
        acc_ref[...] += lax.dot_general(mix, dy, (((0,), (0,)), ((), ())), preferred_element_type=F32)

        @pl.when(i == nstep - 1)
        def _():
            gw_ref[...] = acc_ref[...].astype(BF16)

    row = lambda w: pl.BlockSpec((tm, w), lambda i: (i, 0))
    const = lambda s: pl.BlockSpec(s, lambda i: (0, 0))
    return pl.pallas_call(
        body,
        name="out_proj",
        grid=(nstep,),
        in_specs=[row(512), row(512), row(D_MODEL), row(D_MODEL), const((1, D_MODEL)),
                  pl.BlockSpec((D_MODEL, D_MODEL), lambda i: (0, 0), pipeline_mode=pl.Buffered(1))],
        out_specs=[row(D_MODEL), row(512), row(512), const((D_MODEL, D_MODEL)), const((8, D_MODEL))],
        out_shape=[jax.ShapeDtypeStruct((t, D_MODEL), F32), jax.ShapeDtypeStruct((t, 512), BF16),
                   jax.ShapeDtypeStruct((t, 512), BF16), jax.ShapeDtypeStruct((D_MODEL, D_MODEL), BF16),
                   jax.ShapeDtypeStruct((8, D_MODEL), F32)],
        scratch_shapes=[pltpu.VMEM((D_MODEL, D_MODEL), F32)],
        compiler_params=_cparams(dimension_semantics=("arbitrary",)),
    )(mix_a, mix_b, x, tgt, gate, w_out)


DPROJ_WIDTHS = (512, 256, 512, 512, 512, 512)
DPROJ_STARTS = (0, 512, 768, 1280, 1792, 2304)
WIN_W = 768
WIN_START = (0, 640, 1408, 2048)
WIN_OFF = (0, 64, 0, 64)
N_GW = N_CHIPS


def _window_pieces(s):
    lo, hi = WIN_START[s], WIN_START[s] + WIN_W
    out = []
    for p, (st, w) in enumerate(zip(DPROJ_STARTS, DPROJ_WIDTHS)):
        a, b = max(lo, st), min(hi, st + w)
        if a < b:
            out.append((p, a - st, b - a, a - lo))
    return out


def _in_proj_bwd(dparts, h, x, dout, s1, nw, wt_full, dcw, dvec, sm_a, row0):
    t = x.shape[0]
    tm = 256
    nstep = N_GW + t // tm
    n_sem = 20
    rows0 = 32
    hs = rows0 // 2
    npart = len(DPROJ_WIDTHS)

    def body(*refs):
        d_hbm, d_ref = refs[:npart], refs[npart:2 * npart]
        (x_ref, dout_ref, s1_ref, nw_ref, h_ref, wt_hbm, dcw_ref, dvec_ref, sma_ref, row0_ref,
         gx_ref, gw_hbm, ssum_ref, rows_ref,
         stg_ref, wt_ref, gt_ref, sib_ref, out_ref, in_ref, res_ref, sall_ref, red_ref, sm0_ref, ssib_ref, schip_ref, sres_ref,
         wsem, lsem, ssem, rsem) = refs[2 * npart:]
        i = pl.program_id(0)
        x_, y_, c, chips = _place()
        j = 2 * x_ + y_
        dev = 2 * j + c
        sib = (x_, y_, 1 - c)
        rc = functools.partial(_remote, ssem, rsem)
        rel_chip = [2 * cx + cy for cx, cy in chips] + [j]
        peers = [(px, py, pc) for px in (x_, 1 - x_) for py in (y_, 1 - y_) for pc in (c, 1 - c)][1:]
        wt_copy = pltpu.make_async_copy(wt_hbm, wt_ref, lsem.at[0])

        def window(case, slot):
            return [pltpu.make_async_copy(d_hbm[p].at[:, pl.ds(c0, w)], stg_ref.at[slot, :, pl.ds(w0, w)], wsem.at[slot, n])
                    for n, (p, c0, w, w0) in enumerate(_window_pieces(case))]

        def to_sibling(k):
            return rc(k, gt_ref.at[k, 1 - c], sib_ref.at[k], sib)

        def to_chip(k):
            return rc(4 + k, out_ref.at[k], in_ref.at[k], (*chips[k], c))

        def trade(k):
            to_sibling(k).wait_recv()

            def add(n, carry):
                rr = pl.ds(pl.multiple_of(n * RS_CH, RS_CH), RS_CH)
                out_ref[k, rr, :] = (gt_ref[k, c, rr, :].astype(F32) + sib_ref[k, rr, :].astype(F32)).astype(BF16)
                return carry

            lax.fori_loop(0, IN_HALF // RS_CH, add, 0)
            to_chip(k).start()

        mine_s = pl.ds(pl.multiple_of(c * hs, 8), hs)
        other_s = pl.ds(pl.multiple_of((1 - c) * hs, 8), hs)

        def small_to_sibling():
            return rc(15, sm0_ref.at[other_s], ssib_ref, sib)

        def small_to_chip(k):
            return rc(16 + k, schip_ref.at[j], schip_ref.at[j], (*chips[k], c))

        def small_share():
            return rc(19, sres_ref.at[c], sres_ref.at[c], sib)

        for k in range(N_GW):
            @pl.when(i == k)
            def _(k=k):
                slot = k % 2
                if k == 0:
                    red_ref[...] = jnp.zeros_like(red_ref)
                    wt_copy.start()
                    sm0_ref[...] = jnp.zeros_like(sm0_ref)
                    sm0_ref[0:16, :] = dcw_ref[...]
                    sm0_ref[16:17, 0:CONV_W] = dvec_ref[0:1, :]
                    sm0_ref[16:17, CONV_W:2 * CONV_W] = dvec_ref[1:2, :]
                    sm0_ref[17:18, 0:CONV_W] = dvec_ref[2:3, :]
                    for r in range(3):
                        sm0_ref[17:18, CONV_W + 128 * r:CONV_W + 128 * (r + 1)] = sma_ref[r:r + 1, :]
                    sm0_ref[18:19, :] = row0_ref[1:2, :]
                    small_to_sibling().start()
                if k == 1:
                    small_to_sibling().wait_recv()
                    schip_ref[j] = sm0_ref[mine_s, :] + ssib_ref[...]
                    for kk in range(3):
                        small_to_chip(kk).start()
                if k == N_GW - 1:
                    for kk in range(3):
                        jk = rel_chip[kk]
                        rc(16 + kk, schip_ref.at[jk], schip_ref.at[jk], sib).wait_recv()
                    tot = schip_ref[0]
                    for d in range(1, N_CHIPS):
                        tot = tot + schip_ref[d]
                    sres_ref[c] = tot
                    small_share().start()
                for case in range(N_CHIPS):
                    if k == 0:
                        @pl.when(rel_chip[0] == case)
                        def _():
                            for cp in window(case, 0):
                                cp.start()
                    if k + 1 < N_GW:
                        @pl.when(rel_chip[k + 1] == case)
                        def _():
                            for cp in window(case, 1 - slot):
                                cp.start()
                for case in range(N_CHIPS):
                    @pl.when(rel_chip[k] == case)
                    def _():
                        for cp in window(case, slot):
                            cp.wait()
                g = lax.dot_general(stg_ref[slot], h_ref[...], (((0,), (0,)), ((), ())), preferred_element_type=F32)
                for off in sorted(set(WIN_OFF)):
                    @pl.when(rel_chip[k] % 2 == (1 if off else 0))
                    def _():
                        gt_ref[k, 0] = g[off:off + IN_HALF].astype(BF16)
                        gt_ref[k, 1] = g[off + IN_HALF:off + 2 * IN_HALF].astype(BF16)
                to_sibling(k).start()
                if k >= 1:
                    trade(k - 1)

        @pl.when(i == N_GW)
        def _():
            wt_copy.wait()

        @pl.when(i >= N_GW)
        def _():
            xv = x_ref[...]
            r = lax.rsqrt(jnp.mean(xv * xv, axis=-1, keepdims=True) + EPS)
            xh = xv * r
            n = xh * nw_ref[...]
            dproj = jnp.concatenate([ref[...] for ref in d_ref], axis=1)
            dh = jnp.dot(dproj, wt_ref[...], preferred_element_type=F32)
            red_ref[0:1, :] += jnp.sum(dh, axis=0, keepdims=True)
            red_ref[1:2, :] += jnp.sum(dh * n, axis=0, keepdims=True)
            dn = dh * s1_ref[...]
            red_ref[2:3, :] += jnp.sum(dn * xh, axis=0, keepdims=True)
            dxh = dn * nw_ref[...]
            gx_ref[...] = dout_ref[...] + r * (dxh - xh * jnp.mean(dxh * xh, axis=-1, keepdims=True))

        @pl.when(i == nstep - 1)
        def _():
            sall_ref[dev] = row0_ref[...]
            sall_ref[dev, 2:5, :] = red_ref[0:3, :]
            sends = [rc(8 + k, sall_ref.at[dev], sall_ref.at[dev], peer) for k, peer in enumerate(peers)]
            for cp in sends:
                cp.start()
            sends += [to_sibling(k) for k in range(N_GW)] + [to_chip(k) for k in range(3)]
            sends += [small_to_sibling(), small_share()] + [small_to_chip(k) for k in range(3)]
            own = N_GW - 1
            to_sibling(own).wait_recv()
            for k in range(3):
                to_chip(k).wait_recv()

            def total(n, carry):
                rr = pl.ds(pl.multiple_of(n * RS_CH, RS_CH), RS_CH)
                acc = gt_ref[own, c, rr, :].astype(F32) + sib_ref[own, rr, :].astype(F32)
                for k in range(3):
                    acc = acc + in_ref[k, rr, :].astype(F32)
                res_ref[c, rr, :] = acc
                return carry

            lax.fori_loop(0, IN_HALF // RS_CH, total, 0)
            share = rc(7, res_ref.at[c], res_ref.at[c], sib)
            share.start()
            sends.append(share)
            for k, (px, py, pc) in enumerate(peers):
                pdev = 4 * px + 2 * py + pc
                rc(8 + k, sall_ref.at[pdev], sall_ref.at[pdev], (px, py, pc)).wait_recv()
            rows_ref[...] = sall_ref[...]
            rc(19, sres_ref.at[1 - c], sres_ref.at[1 - c], sib).wait_recv()
            ssum_ref[0:hs, :] = sres_ref[0]
            ssum_ref[hs:rows0, :] = sres_ref[1]
            rc(7, res_ref.at[1 - c], res_ref.at[1 - c], sib).wait_recv()
            back = pltpu.make_async_copy(res_ref, gw_hbm, lsem.at[1])
            back.start()
            for cp in sends:
                cp.wait_send()
            back.wait()

    blk = lambda i: jnp.maximum(i - N_GW, 0)
    row = lambda w: pl.BlockSpec((tm, w), lambda i: (blk(i), 0))
    vec = pl.BlockSpec((1, D_MODEL), lambda i: (0, 0))
    const = lambda shape: pl.BlockSpec(shape, lambda i: (0,) * len(shape))
    hbm = pl.BlockSpec(memory_space=pl.ANY)
    return pl.pallas_call(
        body,
        name="in_proj_bwd",
        grid=(nstep,),
        in_specs=[hbm] * npart + [row(w) for w in DPROJ_WIDTHS] + [row(D_MODEL), row(D_MODEL), vec, vec,
                  pl.BlockSpec((t, D_MODEL), lambda i: (0, 0), pipeline_mode=pl.Buffered(1)), hbm, const((16, D_MODEL)),
                  const((8, CONV_W)), const((8, 128)), const((8, D_MODEL))],
        out_specs=[row(D_MODEL), hbm, const((rows0, D_MODEL)), const((N_DEV, 8, D_MODEL))],
        out_shape=[jax.ShapeDtypeStruct((t, D_MODEL), F32), jax.ShapeDtypeStruct((2, IN_HALF, D_MODEL), F32),
                   jax.ShapeDtypeStruct((rows0, D_MODEL), F32), jax.ShapeDtypeStruct((N_DEV, 8, D_MODEL), F32)],
        scratch_shapes=[pltpu.VMEM((2, t, WIN_W), BF16), pltpu.VMEM((IN_W, D_MODEL), BF16),
                        pltpu.VMEM((N_CHIPS, 2, IN_HALF, D_MODEL), BF16), pltpu.VMEM((N_CHIPS, IN_HALF, D_MODEL), BF16),
                        pltpu.VMEM((3, IN_HALF, D_MODEL), BF16), pltpu.VMEM((3, IN_HALF, D_MODEL), BF16),
                        pltpu.VMEM((2, IN_HALF, D_MODEL), F32), pltpu.VMEM((N_DEV, 8, D_MODEL), F32),
                        pltpu.VMEM((8, D_MODEL), F32), pltpu.VMEM((rows0, D_MODEL), F32), pltpu.VMEM((hs, D_MODEL), F32),
                        pltpu.VMEM((N_CHIPS, hs, D_MODEL), F32),
                        pltpu.VMEM((2, hs, D_MODEL), F32), pltpu.SemaphoreType.DMA((2, 3)), pltpu.SemaphoreType.DMA((2,)),
                        pltpu.SemaphoreType.DMA((n_sem,)), pltpu.SemaphoreType.DMA((n_sem,))],
        compiler_params=_cparams(dimension_semantics=("arbitrary",)),
    )(*dparts, *dparts, x, dout, s1, nw, h, wt_full, dcw, dvec, sm_a, row0)


MESH = pl.DeviceIdType.MESH


def _place():
    x, y, c = lax.axis_index("x"), lax.axis_index("y"), lax.axis_index("c")
    chips = [(1 - x, y), (x, 1 - y), (1 - x, 1 - y)]
    return x, y, c, chips


def _remote(sems_s, sems_r, k, src, dst, to):
    return pltpu.make_async_remote_copy(src_ref=src, dst_ref=dst, send_sem=sems_s.at[k], recv_sem=sems_r.at[k],
                                        device_id=to, device_id_type=MESH)


RS_CH = 32
RS_SEMS = 5


def _rs_to_sibling(rc, s0, theirs, sib_ref, sib):
    cp = rc(s0, theirs, sib_ref, sib)
    cp.start()
    return cp


def _rs_trade(rc, s0, theirs, mine, sib_ref, out_ref, in_ref, rows, c, sib, chips):
    rc(s0, theirs, sib_ref, sib).wait_recv()
    cps = []
    for k, (cx, cy) in enumerate(chips):
        jk = 2 * cx + cy

        def add(i, carry, jk=jk, k=k):
            rr = pl.ds(pl.multiple_of(i * RS_CH, RS_CH), RS_CH)
            out_ref[k, rr, :] = (mine[jk, rr, :].astype(F32) + sib_ref[jk, rr, :].astype(F32)).astype(BF16)
            return carry

        lax.fori_loop(0, rows // RS_CH, add, 0)
        cps.append(rc(s0 + 1 + k, out_ref.at[k], in_ref.at[k], (cx, cy, c)))
        cps[-1].start()
    return cps


def _rs_total(rc, s0, mine, sib_ref, out_ref, in_ref, res_ref, rows, j, c, sib):
    for k in range(3):
        rc(s0 + 1 + k, out_ref.at[k], in_ref.at[k], sib).wait_recv()

    def total(i, carry):
        rr = pl.ds(pl.multiple_of(i * RS_CH, RS_CH), RS_CH)
        acc = mine[j, rr, :].astype(F32) + sib_ref[j, rr, :].astype(F32)
        for k in range(3):
            acc = acc + in_ref[k, rr, :].astype(F32)
        res_ref[c, rr, :] = acc
        return carry

    lax.fori_loop(0, rows // RS_CH, total, 0)
    cp = rc(s0 + 4, res_ref.at[c], res_ref.at[c], sib)
    cp.start()
    return cp


def _rs_done(rc, s0, res_ref, c, sib):
    rc(s0 + 4, res_ref.at[1 - c], res_ref.at[1 - c], sib).wait_recv()


def _rs_scratch(rows):
    return [pltpu.VMEM((N_CHIPS, rows, D_MODEL), BF16), pltpu.VMEM((3, rows, D_MODEL), BF16),
            pltpu.VMEM((3, rows, D_MODEL), BF16)]


MAIN_W = 640
MAIN_DST = (((0, 0, 512), (1, 0, 128)), ((2, 0, 512), (3, 0, 128)), ((3, 128, 384), (4, 0, 256)), ((4, 384, 128), (5, 0, 512)))
PAIR_DST = ((1, 128, 128), (4, 256, 128))


def _in_proj_gather(x, wt, c_row, w_ada, b_ada, nw):
    t = x.shape[0]
    ch = 256
    n_sem = 16

    def body(x_hbm, wt_ref, c_ref, wada_ref, bada_ref, nw_ref,
             q_hbm, kv_hbm, ga_hbm, ua_hbm, ug_hbm, gb_hbm, h_hbm, w4_hbm, call_ref, ada_ref,
             x_ref, h_ref, w4_ref, stg_ref, pstg_ref, part_ref, lsem, osem, wsem, ssem, rsem):
        outs = (q_hbm, kv_hbm, ga_hbm, ua_hbm, ug_hbm, gb_hbm)
        x_, y_, c, chips = _place()
        j = 2 * x_ + y_
        dev = 2 * j + c
        sib = (x_, y_, 1 - c)
        idx = [2 * cx + cy for cx, cy in chips]
        rc = functools.partial(_remote, ssem, rsem)
        x_copy = pltpu.make_async_copy(x_hbm, x_ref, lsem.at[0])
        x_copy.start()

        def rows_of(s, cc):
            return pl.ds(pl.multiple_of(2 * IN_HALF * s + IN_HALF * cc, 16), IN_HALF)

        w4_ref[rows_of(j, 0), :] = wt_ref[0].astype(BF16)
        w4_ref[rows_of(j, 1), :] = wt_ref[1].astype(BF16)
        call_ref[dev] = c_ref[...]
        sends = []
        peers = [(px, py, pc) for px in (x_, 1 - x_) for py in (y_, 1 - y_) for pc in (c, 1 - c)][1:]
        for k, peer in enumerate(peers):
            sends.append(rc(k, call_ref.at[dev], call_ref.at[dev], peer))
        for cp in sends:
            cp.start()

        for k, (px, py, pc) in enumerate(peers):
            pdev = 4 * px + 2 * py + pc
            rc(k, call_ref.at[pdev], call_ref.at[pdev], (px, py, pc)).wait_recv()
        rowid = lax.broadcasted_iota(jnp.int32, (N_DEV, D_MODEL), 0)
        call = jnp.zeros((N_DEV, D_MODEL), F32)
        for r in range(N_DEV):
            call = jnp.where(rowid == r, jnp.broadcast_to(call_ref[r], (N_DEV, D_MODEL)), call)
        bsh = bada_ref[:, 0:ADA_SHARD]
        for k in range(1, N_CHIPS):
            bsh = jnp.where(j == k, bada_ref[:, ADA_SHARD * k:ADA_SHARD * (k + 1)], bsh)
        part = jnp.dot(_silu(call).astype(BF16), wada_ref[...].astype(BF16), preferred_element_type=F32) + bsh
        for r in range(N_DEV):
            part_ref[r] = part[r:r + 1, :]
        ada_ref[j] = part_ref[dev]
        for k, chip in enumerate(chips):
            sends.append(rc(13 + k, part_ref.at[2 * idx[k] + c], ada_ref.at[j], (*chip, c)))
            sends[-1].start()
        for k, chip in enumerate(chips):
            sends.append(rc(7 + k, w4_ref.at[rows_of(j, c)], w4_ref.at[rows_of(j, c)], (*chip, c)))
            sends[-1].start()
        for k in range(3):
            rc(13 + k, ada_ref.at[idx[k]], ada_ref.at[idx[k]], sib).wait_recv()

        shift = jnp.concatenate([ada_ref[0], ada_ref[1][:, 0:256]], axis=1)
        s1 = 1.0 + jnp.concatenate([ada_ref[1][:, 256:768], ada_ref[2][:, 0:512]], axis=1)
        x_copy.wait()

        def norm(i, carry):
            rr = pl.ds(pl.multiple_of(i * ch, ch), ch)
            xv = x_ref[rr, :]
            r = lax.rsqrt(jnp.mean(xv * xv, axis=-1, keepdims=True) + EPS)
            h_ref[rr, :] = ((xv * r) * nw_ref[...] * s1 + shift).astype(BF16)
            return carry

        lax.fori_loop(0, t // ch, norm, 0)
        h_copy = pltpu.make_async_copy(h_ref, h_hbm, lsem.at[1])
        h_copy.start()

        def put_main(case, slot):
            cps, col = [], 0
            for n, (a, c0, w) in enumerate(MAIN_DST[case]):
                cps.append(pltpu.make_async_copy(stg_ref.at[slot, :, pl.ds(col, w)], outs[a].at[:, pl.ds(c0, w)], osem.at[slot, n]))
                col += w
            return cps

        def put_pair(case, slot):
            a, c0, w = PAIR_DST[case]
            return pltpu.make_async_copy(pstg_ref.at[slot], outs[a].at[:, pl.ds(c0, w)], osem.at[slot, 2])

        def project(first_row, width, dst, slot):
            wrows = pl.ds(pl.multiple_of(first_row, 128), width)

            def blk(i, carry):
                rr = pl.ds(pl.multiple_of(i * ch, ch), ch)
                dst[slot, rr, :] = lax.dot_general(h_ref[rr, :], w4_ref[wrows, :], (((1,), (1,)), ((), ())),
                                                   preferred_element_type=F32)
                return carry

            lax.fori_loop(0, t // ch, blk, 0)

        def phase(p, s, pair):
            slot = p % 2
            if p >= 2:
                for case in range(N_CHIPS):
                    @pl.when(order[p - 2] == case)
                    def _():
                        for cp in put_main(case, slot):
                            cp.wait()
            if p == 3:
                for case in range(2):
                    @pl.when(j // 2 == case)
                    def _():
                        put_pair(case, 0).wait()
            project(2 * IN_HALF * s + 64 * (s % 2), MAIN_W, stg_ref, slot)
            for case in range(N_CHIPS):
                @pl.when(s == case)
                def _():
                    for cp in put_main(case, slot):
                        cp.start()
            if pair is not None:
                project(MAIN_W + 2 * (2 * IN_HALF) * pair, 128, pstg_ref, slot % 2 if p == 2 else 1)
                for case in range(2):
                    @pl.when(pair == case)
                    def _():
                        put_pair(case, 0 if p == 2 else 1).start()

        order = [j] + idx
        w_out = [pltpu.make_async_copy(w4_ref.at[pl.ds(pl.multiple_of(2 * IN_HALF * s, 32), 2 * IN_HALF)],
                                       w4_hbm.at[pl.ds(pl.multiple_of(2 * IN_HALF * s, 32), 2 * IN_HALF)], wsem.at[p])
                 for p, s in enumerate(order)]
        w_out[0].start()
        phase(0, j, None)
        passed = []
        for k in range(3):
            jk = idx[k]
            rc(7 + k, w4_ref.at[rows_of(jk, c)], w4_ref.at[rows_of(jk, c)], sib).wait_recv()
            passed.append(rc(10 + k, w4_ref.at[rows_of(jk, c)], w4_ref.at[rows_of(jk, c)], sib))
            passed[-1].start()
            rc(10 + k, w4_ref.at[rows_of(jk, 1 - c)], w4_ref.at[rows_of(jk, 1 - c)], sib).wait_recv()
            w_out[1 + k].start()
            if k == 0:
                phase(1, jk, None)
            elif k == 1:
                phase(2, jk, j // 2)
            else:
                phase(3, jk, 1 - j // 2)

        for case in range(N_CHIPS):
            for p in (2, 3):
                @pl.when(order[p] == case)
                def _():
                    for cp in put_main(case, p % 2):
                        cp.wait()
        for case in range(2):
            @pl.when(1 - j // 2 == case)
            def _():
                put_pair(case, 1).wait()
        h_copy.wait()
        for cp in w_out:
            cp.wait()
        for cp in sends + passed:
            cp.wait_send()

    vm = pl.BlockSpec(memory_space=pltpu.VMEM)
    hbm = pl.BlockSpec(memory_space=pl.ANY)
    widths = (512, 256, 512, 512, 512, 512)
    return pl.pallas_call(
        body,
        name="in_proj",
        in_specs=[hbm, vm, vm, vm, vm, vm],
        out_specs=[hbm] * 8 + [vm, vm],
        out_shape=[jax.ShapeDtypeStruct((t, w), F32) for w in widths]
        + [jax.ShapeDtypeStruct((t, D_MODEL), BF16), jax.ShapeDtypeStruct((IN_W, D_MODEL), BF16),
           jax.ShapeDtypeStruct((N_DEV, 1, D_MODEL), F32), jax.ShapeDtypeStruct((N_CHIPS, 1, ADA_SHARD), F32)],
        scratch_shapes=[pltpu.VMEM((t, D_MODEL), F32), pltpu.VMEM((t, D_MODEL), BF16), pltpu.VMEM((IN_W, D_MODEL), BF16),
                        pltpu.VMEM((2, t, MAIN_W), F32), pltpu.VMEM((2, t, 128), F32), pltpu.VMEM((N_DEV, 1, ADA_SHARD), F32),
                        pltpu.SemaphoreType.DMA((2,)), pltpu.SemaphoreType.DMA((2, 3)), pltpu.SemaphoreType.DMA((N_CHIPS,)),
                        pltpu.SemaphoreType.DMA((n_sem,)), pltpu.SemaphoreType.DMA((n_sem,))],
        compiler_params=_cparams(),
    )(x, wt, c_row, w_ada, b_ada, nw)


def _adamw_math(w, g, m, v):
    m2 = ADAM_B1 * m + (1.0 - ADAM_B1) * g
    v2 = ADAM_B2 * v + (1.0 - ADAM_B2) * (g * g)
    m_hat = m2 / (1.0 - ADAM_B1 ** ADAM_STEP)
    v_hat = v2 / (1.0 - ADAM_B2 ** ADAM_STEP)
    delta = -ADAM_LR * (m_hat / (jnp.sqrt(v_hat) + ADAM_EPS) + ADAM_WD * w)
    return delta, m2, v2


def _adamw(name, w, g, m, v, tm, through=None):
    r, cdim = w.shape
    nstep = r // tm
    extra = [] if through is None else [through]

    def body(w_ref, g_ref, m_ref, v_ref, *rest):
        g2_ref, d_ref, m2_ref, v2_ref = rest[len(extra):len(extra) + 4]
        g = g_ref[...]
        g2_ref[...] = g
        d_ref[...], m2_ref[...], v2_ref[...] = _adamw_math(w_ref[...], g, m_ref[...], v_ref[...])
        if extra:
            rest[-1][...] = rest[0][...]

    blk = pl.BlockSpec((tm, cdim), lambda i: (i, 0))
    eblk = [pl.BlockSpec((e.shape[0] // nstep, e.shape[1]), lambda i: (i, 0)) for e in extra]
    return pl.pallas_call(
        body,
        name=name,
        grid=(nstep,),
        in_specs=[blk] * 4 + eblk,
        out_specs=[blk] * 4 + eblk,
        out_shape=[jax.ShapeDtypeStruct((r, cdim), F32)] * 4 + [jax.ShapeDtypeStruct(e.shape, e.dtype) for e in extra],
        compiler_params=_cparams(dimension_semantics=("arbitrary",)),
    )(w, g, m, v, *extra)


def _adamw_ada(w, m, v, cact_t, dcols):
    r, cdim = w.shape
    tm = 256

    def body(w_ref, m_ref, v_ref, ct_ref, dc_ref, g_ref, d_ref, m2_ref, v2_ref):
        g = jnp.dot(ct_ref[...], dc_ref[...], preferred_element_type=F32, precision=lax.Precision.HIGHEST)
        g_ref[...] = g
        d_ref[...], m2_ref[...], v2_ref[...] = _adamw_math(w_ref[...], g, m_ref[...], v_ref[...])

    blk = pl.BlockSpec((tm, cdim), lambda i: (i, 0))
    return pl.pallas_call(
        body,
        name="adamw_w_ada",
        grid=(r // tm,),
        in_specs=[blk] * 3 + [pl.BlockSpec((tm, N_DEV), lambda i: (i, 0)), pl.BlockSpec((N_DEV, cdim), lambda i: (0, 0))],
        out_specs=[blk] * 4,
        out_shape=[jax.ShapeDtypeStruct((r, cdim), F32)] * 4,
        compiler_params=_cparams(dimension_semantics=("arbitrary",)),
    )(w, m, v, cact_t, dcols)


def _adamw_small(ws, ms, vs, ssum, rows):
    n = len(ws)

    def body(*refs):
        w_r, m_r, v_r = refs[0:n], refs[n:2 * n], refs[2 * n:3 * n]
        ss_ref, rows_ref = refs[3 * n], refs[3 * n + 1]
        g_r, d_r, m2_r, v2_r = (refs[3 * n + 2 + k * n:3 * n + 2 + (k + 1) * n] for k in range(4))
        loss_ref = refs[7 * n + 2]
        j = 2 * lax.axis_index("x") + lax.axis_index("y")
        rsum = rows_ref[0]
        for d in range(1, N_DEV):
            rsum = rsum + rows_ref[d]
        taps = []
        for t in range(CONV_TAPS):
            row = ss_ref[t // 2:t // 2 + 1, :]
            c0 = CONV_W * (t % 2)
            pick = row[:, c0:c0 + 128]
            for k in range(1, N_CHIPS):
                pick = jnp.where(j == k, row[:, c0 + 128 * k:c0 + 128 * (k + 1)], pick)
            taps.append(pick)
        grads = [jnp.concatenate([rsum[2:3], rsum[3:4], rsum[0:1]], axis=1), rsum[4:5],
                 ss_ref[17:18, 512:512 + HEAD_DIM], ss_ref[17:18, 640:640 + HEAD_DIM], ss_ref[17:18, 768:776],
                 None, ss_ref[16:17, 0:CONV_W], ss_ref[16:17, CONV_W:2 * CONV_W], ss_ref[17:18, 0:CONV_W]]
        for i in range(n):
            if grads[i] is None:
                for t in range(CONV_TAPS):
                    g_r[i][t:t + 1, :] = taps[t]
                g = g_r[i][...]
            else:
                g = grads[i]
                g_r[i][...] = g
            d_r[i][...], m2_r[i][...], v2_r[i][...] = _adamw_math(w_r[i][...], g, m_r[i][...], v_r[i][...])
        loss_ref[...] = (0.5 / D_MODEL) * jnp.sum(ss_ref[18:19, :], axis=1, keepdims=True)

    vm = pl.BlockSpec(memory_space=pltpu.VMEM)
    shapes = [jax.ShapeDtypeStruct(w.shape, F32) for w in ws]
    out = pl.pallas_call(
        body,
        name="adamw_small",
        in_specs=[vm] * (3 * n + 2),
        out_specs=[vm] * (4 * n + 1),
        out_shape=shapes * 4 + [jax.ShapeDtypeStruct((1, 1), F32)],
        compiler_params=_cparams(),
    )(*ws, *ms, *vs, ssum, rows)
    return out[0:n], out[n:2 * n], out[2 * n:3 * n], out[3 * n:4 * n], out[4 * n]


def _rope_tables(t):
    inv = ROPE_THETA ** (-jnp.arange(0, HEAD_DIM, 2, dtype=F32) / HEAD_DIM)
    ang = jnp.arange(t, dtype=F32)[:, None] * inv[None, :]
    cos, sin = jnp.cos(ang), jnp.sin(ang)
    return jnp.tile(cos, (1, 4)), jnp.tile(jnp.concatenate([-sin, sin], axis=1), (1, 2))


def kernel(x, c, w_ada, b_ada, norm_w, w_in, q_norm_w, k_norm_w, sinks, conv_w, conv_b, ln_w, ln_b, w_out, loss_target, m_w_ada, m_b_ada, m_norm_w, m_w_in, m_q_norm_w, m_k_norm_w, m_sinks, m_conv_w, m_conv_b, m_ln_w, m_ln_b, m_w_out, v_w_ada, v_b_ada, v_norm_w, v_w_in, v_q_norm_w, v_k_norm_w, v_sinks, v_conv_w, v_conv_b, v_ln_w, v_ln_b, v_w_out):
    xi, yi = lax.axis_index("x"), lax.axis_index("y")
    j = 2 * xi + yi
    x2, tgt = x[0], loss_target[0]
    t = x2.shape[0]

    wt_s, mt_s, vt_s = w_in[0].T, m_w_in[0].T, v_w_in[0].T
    cw_pad = jnp.pad(conv_w[0], ((0, 1), (0, 0)))

    q_raw, kv_raw, ga, ua, ug, gb, h, w_full, call, ada4 = _in_proj_gather(
        x2, wt_s.reshape(2, IN_HALF, D_MODEL), c, w_ada[0], b_ada, norm_w)
    ada = ada4.reshape(1, 3 * D_MODEL)
    s1, gate = 1.0 + ada[:, D_MODEL:2 * D_MODEL], ada[:, 2 * D_MODEL:]

    cos_f, sin_s = _rope_tables(t)
    qw2, kw2 = jnp.tile(q_norm_w, (1, 2)), jnp.tile(k_norm_w, (1, 2))

    o, mix_a, wo4, cw4 = _attn_fwd(q_raw, kv_raw, ga, qw2, kw2, sinks, cos_f, sin_s,
                                   w_out[0].reshape(2, OUT_HALF, D_MODEL), cw_pad)
    w_out_full = wo4.reshape(D_MODEL, D_MODEL)
    cw_full = jnp.concatenate([cw4[i] for i in range(N_CHIPS)], axis=1)
    cz, mix_b = _conv_fwd(ua, ug, gb, cw_full, conv_b, ln_w, ln_b)
    dout, dmix_a, dmix_b, gwo_bf, red_o = _out_proj(mix_a, mix_b, x2, tgt, gate, w_out_full)

    dq, dkv, dga, sm_a, gwo = _attn_bwd(q_raw, kv_raw, ga, o, dmix_a, qw2, kw2, sinks, cos_f, sin_s,
                                        gwo_bf.reshape(N_CHIPS, 2, OUT_HALF, D_MODEL))
    dua, dug, dgb, dcw, dvec = _conv_bwd(ua, ug, gb, cz, dmix_b, cw_full, ln_w, ln_b)
    dparts = (dq, dkv, dga, dua, dug, dgb)

    grad_x, gw, ssum, rows = _in_proj_bwd(dparts, h, x2, dout, s1, norm_w, w_full, dcw, dvec, sm_a, red_o)

    gt_w_in = gw.reshape(2 * IN_HALF, D_MODEL)
    g_w_out = gwo.reshape(D_MODEL // N_CHIPS, D_MODEL)
    d_ada_all = jnp.concatenate([rows[:, 2], rows[:, 3], rows[:, 0]], axis=1)
    dcols = lax.dynamic_slice(d_ada_all, (0, ADA_SHARD * j), (N_DEV, ADA_SHARD))
    cact_t = jax.nn.silu(call.reshape(N_DEV, D_MODEL)).T

    g_w_ada, d_w_ada, nm_w_ada, nv_w_ada = _adamw_ada(w_ada[0], m_w_ada[0], v_w_ada[0], cact_t, dcols)
    gt_w_in, dt_w_in, nmt_w_in, nvt_w_in, grad_x = _adamw("adamw_w_in", wt_s, gt_w_in, mt_s, vt_s, 176, through=grad_x)
    g_w_in, d_w_in, nm_w_in, nv_w_in = gt_w_in.T, dt_w_in.T, nmt_w_in.T, nvt_w_in.T
    g_w_out, d_w_out, nm_w_out, nv_w_out = _adamw("adamw_w_out", w_out[0], g_w_out, m_w_out[0], v_w_out[0], 128)
    ws = [b_ada, norm_w, q_norm_w, k_norm_w, sinks, conv_w[0], conv_b, ln_w, ln_b]
    ms = [m_b_ada, m_norm_w, m_q_norm_w, m_k_norm_w, m_sinks, m_conv_w[0], m_conv_b, m_ln_w, m_ln_b]
    vs = [v_b_ada, v_norm_w, v_q_norm_w, v_k_norm_w, v_sinks, v_conv_w[0], v_conv_b, v_ln_w, v_ln_b]
    gs, ds, nms, nvs, loss11 = _adamw_small(ws, ms, vs, ssum, rows)
    loss = loss11[0, 0]

    def order(ada_v, in_v, out_v, sm):
        b, nw_, qw_, kw_, sk_, cw_, cb_, lw_, lb_ = sm
        return [ada_v[None], b, nw_, in_v[None], qw_, kw_, sk_, cw_[None], cb_, lw_, lb_, out_v[None]]

    grads = order(g_w_ada, g_w_in, g_w_out, gs)
    deltas = order(d_w_ada, d_w_in, d_w_out, ds)
    new_m = order(nm_w_ada, nm_w_in, nm_w_out, nms)
    new_v = order(nv_w_ada, nv_w_in, nv_w_out, nvs)
    return (loss, grad_x[None], *grads, *deltas, *new_m, *new_v)
```

```python
import functools

import jax
import jax.numpy as jnp
from jax import lax
from jax.experimental import pallas as pl
from jax.experimental.pallas import tpu as pltpu

F32 = jnp.float32
BF16 = jnp.bfloat16

D_MODEL = 1024
ATTN_W = 512
KV_W = 128
CONV_W = 512
IN_W = 2816
HEAD_DIM = 64
CONV_TAPS = 31
QBLK = 128
EPS = 1e-6
ROPE_THETA = 10000.0

ADAM_LR = 0.001
ADAM_B1 = 0.9
ADAM_B2 = 0.999
ADAM_EPS = 1e-08
ADAM_WD = 0.01
ADAM_STEP = 10

N_CHIPS = 4
N_DEV = 8
IN_HALF = IN_W // N_CHIPS // 2
OUT_HALF = D_MODEL // N_CHIPS // 2
ADA_SHARD = 3 * D_MODEL // N_CHIPS

VMEM_LIMIT = 56 * 1024 * 1024
CONV_PAD = 32


def _cparams(**kw):
    return pltpu.CompilerParams(vmem_limit_bytes=VMEM_LIMIT, **kw)


def _sigmoid(v):
    return 1.0 / (1.0 + jnp.exp(-v))


def _silu(v):
    return v * _sigmoid(v)


def _dsilu(v):
    s = _sigmoid(v)
    return s * (1.0 + v * (1.0 - s))


def _lane(shape):
    return lax.broadcasted_iota(jnp.int32, shape, len(shape) - 1)


PUT_ROWS = 512


def _fetch(hbm_refs, vmem_refs, sem):
    cps = [pltpu.make_async_copy(h, v, sem.at[i]) for i, (h, v) in enumerate(zip(hbm_refs, vmem_refs))]
    for cp in cps:
        cp.start()
    return cps


def _put(vmem_ref, hbm_ref, sem, m):
    r = pl.ds(pl.multiple_of(m * PUT_ROWS, PUT_ROWS), PUT_ROWS)
    return pltpu.make_async_copy(vmem_ref.at[r], hbm_ref.at[r], sem.at[m])


def _put_all(pairs, sems, m):
    for (v, h), sem in zip(pairs, sems):
        _put(v, h, sem, m).start()


def _put_wait(pairs, sems, n):
    for (v, h), sem in zip(pairs, sems):
        for m in range(n):
            _put(v, h, sem, m).wait()


def _head_mean(s, left):
    sl = jnp.sum(jnp.where(left, s, 0.0), axis=-1, keepdims=True)
    sr = jnp.sum(jnp.where(left, 0.0, s), axis=-1, keepdims=True)
    return jnp.where(left, sl, sr) * (1.0 / HEAD_DIM)


def _rot(v, first):
    return jnp.where(first, pltpu.roll(v, 96, 1), pltpu.roll(v, 32, 1))


def _norm_rope(v, w, cos, sin_s, left, first):
    r = lax.rsqrt(_head_mean(v * v, left) + EPS)
    xh = v * r
    n = xh * w
    return n * cos + _rot(n, first) * sin_s, xh, r


def _norm_rope_bwd(d, xh, r, w, cos, sin_s, left, first):
    dn = d * cos - _rot(d, first) * sin_s
    dw = jnp.sum(dn * xh, axis=0, keepdims=True)
    dxh = dn * w
    return r * (dxh - xh * _head_mean(dxh * xh, left)), dw


def _dup_heads(v, left):
    sw = pltpu.roll(v, 64, 1)
    return jnp.where(left, v, sw), jnp.where(left, sw, v)


def _prep_kv(kv_ref, kw_ref, cos_ref, sin_ref, ka_ref, va_ref, t):
    ch = 256
    for g in range(2):
        ka_ref[g, 0:QBLK, :] = jnp.zeros((QBLK, 128), BF16)
        va_ref[g, 0:QBLK, :] = jnp.zeros((QBLK, 128), BF16)

    def chunk(i, carry):
        r0 = pl.multiple_of(i * ch, ch)
        left = _lane((ch, 128)) < 64
        first = (_lane((ch, 128)) % 64) < 32
        k = kv_ref[pl.ds(r0, ch), 0:128]
        v = kv_ref[pl.ds(r0, ch), 128:256]
        kr, _, _ = _norm_rope(k, kw_ref[...], cos_ref[pl.ds(r0, ch), :], sin_ref[pl.ds(r0, ch), :], left, first)
        k0, k1 = _dup_heads(kr, left)
        v0, v1 = _dup_heads(v, left)
        ka_ref[0, pl.ds(QBLK + r0, ch), :] = k0.astype(BF16)
        ka_ref[1, pl.ds(QBLK + r0, ch), :] = k1.astype(BF16)
        va_ref[0, pl.ds(QBLK + r0, ch), :] = v0.astype(BF16)
        va_ref[1, pl.ds(QBLK + r0, ch), :] = v1.astype(BF16)
        return carry

    lax.fori_loop(0, t // ch, chunk, 0)


def _band_mask(n):
    qi = lax.broadcasted_iota(jnp.int32, (2 * QBLK, 2 * QBLK), 0) % QBLK
    kj = lax.broadcasted_iota(jnp.int32, (2 * QBLK, 2 * QBLK), 1)
    local = (kj > qi) & (kj <= qi + QBLK)
    return local & ((n > 0) | (kj >= QBLK))


def _softmax_pair(s, mask, sink0, sink1):
    row = lax.broadcasted_iota(jnp.int32, (2 * QBLK, 1), 0)
    sink = jnp.where(row < QBLK, sink0, sink1)
    s = jnp.where(mask, s, -jnp.inf)
    m = jnp.maximum(jnp.max(s, axis=-1, keepdims=True), sink)
    e = jnp.exp(s - m)
    es = jnp.exp(sink - m)
    inv = 1.0 / (jnp.sum(e, axis=-1, keepdims=True) + es)
    return e * inv, es * inv


def _stack_heads(v, left):
    return jnp.concatenate([jnp.where(left, v, 0.0), jnp.where(left, 0.0, v)], axis=0)


def _attn_fwd(q_raw, kv_raw, ga, qw2, kw2, sinks, cos_f, sin_s, wo, cw):
    t = q_raw.shape[0]
    nblk = t // QBLK
    per_put = PUT_ROWS // QBLK

    def body(q_hbm, kv_ref, ga_hbm, qw_ref, kw_ref, sk_ref, cos_hbm, sin_hbm, wo_ref, cw_ref,
             o_hbm, mix_hbm, wo4_ref, cw4_ref, ka_ref, va_ref, q_ref, ga_ref, o_ref, mix_ref, cos_ref, sin_ref,
             isem, osem0, osem1, ssem, rsem):
        loads = _fetch((cos_hbm, sin_hbm, q_hbm, ga_hbm), (cos_ref, sin_ref, q_ref, ga_ref), isem)
        outs, osems = ((o_ref, o_hbm), (mix_ref, mix_hbm)), (osem0, osem1)
        x, y, c, chips = _place()
        j = 2 * x + y
        sib = (x, y, 1 - c)
        idx = [2 * cx + cy for cx, cy in chips]
        rc = functools.partial(_remote, ssem, rsem)
        wo4_ref[j] = wo_ref[...].astype(BF16)
        cw4_ref[j] = cw_ref[...]
        sends = []
        for k, chip in enumerate(chips):
            sends.append(rc(k, wo4_ref.at[j, c], wo4_ref.at[j, c], (*chip, c)))
            sends.append(rc(6 + k, cw4_ref.at[j], cw4_ref.at[j], (*chip, c)))
        for cp in sends:
            cp.start()

        loads[0].wait()
        loads[1].wait()
        _prep_kv(kv_ref, kw_ref, cos_ref, sin_ref, ka_ref, va_ref, t)
        loads[2].wait()
        loads[3].wait()

        def blk(n, carry):
            r0 = pl.multiple_of(n * QBLK, QBLK)
            left = _lane((QBLK, 128)) < 64
            first = (_lane((QBLK, 128)) % 64) < 32
            cos = cos_ref[pl.ds(r0, QBLK), :]
            sin = sin_ref[pl.ds(r0, QBLK), :]
            mask = _band_mask(n)
            for p in range(4):
                g = p // 2
                lanes = slice(p * 128, (p + 1) * 128)
                qr, _, _ = _norm_rope(q_ref[pl.ds(r0, QBLK), lanes], qw_ref[...], cos, sin, left, first)
                q2 = _stack_heads(qr * 0.125, left).astype(BF16)
                s = lax.dot_general(q2, ka_ref[g, pl.ds(r0, 2 * QBLK), :], (((1,), (1,)), ((), ())),
                                    preferred_element_type=F32)
                pm, _ = _softmax_pair(s, mask, sk_ref[0, 2 * p], sk_ref[0, 2 * p + 1])
                o2 = jnp.dot(pm.astype(BF16), va_ref[g, pl.ds(r0, 2 * QBLK), :], preferred_element_type=F32)
                o = jnp.where(left, o2[0:QBLK], o2[QBLK:2 * QBLK])
                o_ref[pl.ds(r0, QBLK), lanes] = o.astype(BF16)
                mix_ref[pl.ds(r0, QBLK), lanes] = (o * _silu(ga_ref[pl.ds(r0, QBLK), lanes])).astype(BF16)

            @pl.when(n % per_put == per_put - 1)
            def _():
                _put_all(outs, osems, n // per_put)

            return carry

        lax.fori_loop(0, nblk, blk, 0)
        _put_wait(outs, osems, t // PUT_ROWS)

        passed = []
        for k, chip in enumerate(chips):
            jk = idx[k]
            rc(k, wo4_ref.at[jk, c], wo4_ref.at[jk, c], sib).wait_recv()
            passed.append(rc(3 + k, wo4_ref.at[jk, c], wo4_ref.at[jk, c], sib))
            passed[-1].start()
        for k, chip in enumerate(chips):
            jk = idx[k]
            rc(3 + k, wo4_ref.at[jk, 1 - c], wo4_ref.at[jk, 1 - c], sib).wait_recv()
            rc(6 + k, cw4_ref.at[jk], cw4_ref.at[jk], sib).wait_recv()
        for cp in sends + passed:
            cp.wait_send()

    vm = pl.BlockSpec(memory_space=pltpu.VMEM)
    hbm = pl.BlockSpec(memory_space=pl.ANY)
    n_sem = 9
    return pl.pallas_call(
        body,
        name="attn_fwd",
        in_specs=[hbm, vm, hbm, vm, vm, pl.BlockSpec(memory_space=pltpu.SMEM), hbm, hbm, vm, vm],
        out_specs=[hbm, hbm, vm, vm],
        out_shape=[jax.ShapeDtypeStruct((t, ATTN_W), BF16), jax.ShapeDtypeStruct((t, ATTN_W), BF16),
                   jax.ShapeDtypeStruct((N_CHIPS, 2, OUT_HALF, D_MODEL), BF16),
                   jax.ShapeDtypeStruct((N_CHIPS, 32, 128), F32)],
        scratch_shapes=[pltpu.VMEM((2, t + QBLK, 128), BF16), pltpu.VMEM((2, t + QBLK, 128), BF16),
                        pltpu.VMEM((t, ATTN_W), F32), pltpu.VMEM((t, ATTN_W), F32),
                        pltpu.VMEM((t, ATTN_W), BF16), pltpu.VMEM((t, ATTN_W), BF16),
                        pltpu.VMEM((t, 128), F32), pltpu.VMEM((t, 128), F32),
                        pltpu.SemaphoreType.DMA((4,)), pltpu.SemaphoreType.DMA((t // PUT_ROWS,)),
                        pltpu.SemaphoreType.DMA((t // PUT_ROWS,)),
                        pltpu.SemaphoreType.DMA((n_sem,)), pltpu.SemaphoreType.DMA((n_sem,))],
        compiler_params=_cparams(),
    )(q_raw, kv_raw, ga, qw2, kw2, sinks, cos_f, sin_s, wo, cw)


def _attn_bwd(q_raw, kv_raw, ga, o, dmix, qw2, kw2, sinks, cos_f, sin_s, go):
    t = q_raw.shape[0]
    nblk = t // QBLK
    per_put = PUT_ROWS // QBLK

    def body(q_hbm, kv_ref, ga_hbm, o_hbm, dm_hbm, qw_ref, kw_ref, sk_ref, cos_hbm, sin_hbm, go_ref,
             dq_hbm, dkv_ref, dga_hbm, sm_ref, gwo_ref, ka_ref, va_ref, dka_ref, dva_ref,
             sibo_ref, outo_ref, ino_ref, q_ref, ga_ref, o_ref, dm_ref, dq_ref, dga_ref, cos_ref, sin_ref,
             isem, osem0, osem1, ssem, rsem):
        loads = _fetch((cos_hbm, sin_hbm, q_hbm, ga_hbm, o_hbm, dm_hbm), (cos_ref, sin_ref, q_ref, ga_ref, o_ref, dm_ref), isem)
        outs, osems = ((dq_ref, dq_hbm), (dga_ref, dga_hbm)), (osem0, osem1)
        x, y, c, chips = _place()
        sib = (x, y, 1 - c)
        rc = functools.partial(_remote, ssem, rsem)
        theirs, mine = go_ref.at[:, 1 - c], go_ref.at[:, c]
        sends = [_rs_to_sibling(rc, 0, theirs, sibo_ref, sib)]
        loads[0].wait()
        loads[1].wait()
        _prep_kv(kv_ref, kw_ref, cos_ref, sin_ref, ka_ref, va_ref, t)
        dka_ref[...] = jnp.zeros_like(dka_ref)
        dva_ref[...] = jnp.zeros_like(dva_ref)
        sends += _rs_trade(rc, 0, theirs, mine, sibo_ref, outo_ref, ino_ref, OUT_HALF, c, sib, chips)
        for cp in loads[2:]:
            cp.wait()

        def blk(n, carry):
            dqw, dsk = carry
            r0 = pl.multiple_of(n * QBLK, QBLK)
            left = _lane((QBLK, 128)) < 64
            first = (_lane((QBLK, 128)) % 64) < 32
            cos = cos_ref[pl.ds(r0, QBLK), :]
            sin = sin_ref[pl.ds(r0, QBLK), :]
            mask = _band_mask(n)
            row = lax.broadcasted_iota(jnp.int32, (2 * QBLK, 1), 0)
            for p in range(4):
                g = p // 2
                lanes = slice(p * 128, (p + 1) * 128)
                rows = pl.ds(r0, QBLK)
                win = pl.ds(r0, 2 * QBLK)
                qr, xh, r = _norm_rope(q_ref[rows, lanes], qw_ref[...], cos, sin, left, first)
                q2 = _stack_heads(qr * 0.125, left).astype(BF16)
                kwin = ka_ref[g, win, :]
                vwin = va_ref[g, win, :]
                s = lax.dot_general(q2, kwin, (((1,), (1,)), ((), ())), preferred_element_type=F32)
                pm, ps = _softmax_pair(s, mask, sk_ref[0, 2 * p], sk_ref[0, 2 * p + 1])
                gav = ga_ref[rows, lanes]
                dmv = dm_ref[rows, lanes].astype(F32)
                dga_ref[rows, lanes] = (dmv * o_ref[rows, lanes].astype(F32) * _dsilu(gav)).astype(BF16)
                do2 = _stack_heads(dmv * _silu(gav), left).astype(BF16)
                dp = lax.dot_general(do2, vwin, (((1,), (1,)), ((), ())), preferred_element_type=F32)
                delta = jnp.sum(pm * dp, axis=-1, keepdims=True)
                ds = (pm * (dp - delta)).astype(BF16)
                pd = ps * delta
                d0 = jnp.sum(jnp.where(row < QBLK, pd, 0.0), axis=0, keepdims=True)
                d1 = jnp.sum(jnp.where(row < QBLK, 0.0, pd), axis=0, keepdims=True)
                l8 = _lane((1, 128))
                dsk = dsk - jnp.where(l8 == 2 * p, d0, 0.0) - jnp.where(l8 == 2 * p + 1, d1, 0.0)
                dva_ref[g, win, :] += lax.dot_general(pm.astype(BF16), do2, (((0,), (0,)), ((), ())),
                                                      preferred_element_type=F32)
                dka_ref[g, win, :] += lax.dot_general(ds, q2, (((0,), (0,)), ((), ())),
                                                      preferred_element_type=F32)
                dq2 = jnp.dot(ds, kwin, preferred_element_type=F32)
                dqr = jnp.where(left, dq2[0:QBLK], dq2[QBLK:2 * QBLK]) * 0.125
                dq, dw = _norm_rope_bwd(dqr, xh, r, qw_ref[...], cos, sin, left, first)
                dq_ref[rows, lanes] = dq.astype(BF16)
                dqw = dqw + dw

            @pl.when(n % per_put == per_put - 1)
            def _():
                _put_all(outs, osems, n // per_put)

            return dqw, dsk

        zero = jnp.zeros((1, 128), F32)
        dqw, dsk = lax.fori_loop(0, nblk, blk, (zero, zero))

        ch = 256

        def chunk(i, dkw):
            r0 = pl.multiple_of(i * ch, ch)
            left = _lane((ch, 128)) < 64
            first = (_lane((ch, 128)) % 64) < 32
            rows = pl.ds(r0, ch)
            prow = pl.ds(QBLK + r0, ch)

            def fold(ref):
                a0 = ref[0, prow, :]
                a1 = ref[1, prow, :]
                return jnp.where(left, a0 + pltpu.roll(a0, 64, 1), a1 + pltpu.roll(a1, 64, 1))

            cos = cos_ref[rows, :]
            sin = sin_ref[rows, :]
            _, xh, r = _norm_rope(kv_ref[rows, 0:128], kw_ref[...], cos, sin, left, first)
            dk, dw = _norm_rope_bwd(fold(dka_ref), xh, r, kw_ref[...], cos, sin, left, first)
            dkv_ref[rows, 0:128] = dk.astype(BF16)
            dkv_ref[rows, 128:256] = fold(dva_ref).astype(BF16)
            return dkw + dw

        dkw = lax.fori_loop(0, t // ch, chunk, zero)
        sm_ref[...] = jnp.zeros((8, 128), F32)
        sm_ref[0:1, :] = dqw + pltpu.roll(dqw, 64, 1)
        sm_ref[1:2, :] = dkw + pltpu.roll(dkw, 64, 1)
        sm_ref[2:3, :] = dsk

        j = 2 * x + y
        sends.append(_rs_total(rc, 0, mine, sibo_ref, outo_ref, ino_ref, gwo_ref, OUT_HALF, j, c, sib))
        _rs_done(rc, 0, gwo_ref, c, sib)
        for cp in sends:
            cp.wait_send()
        _put_wait(outs, osems, t // PUT_ROWS)

    vm = pl.BlockSpec(memory_space=pltpu.VMEM)
    hbm = pl.BlockSpec(memory_space=pl.ANY)
    return pl.pallas_call(
        body,
        name="attn_bwd",
        in_specs=[hbm, vm, hbm, hbm, hbm, vm, vm, pl.BlockSpec(memory_space=pltpu.SMEM), hbm, hbm, vm],
        out_specs=[hbm, vm, hbm, vm, vm],
        out_shape=[jax.ShapeDtypeStruct((t, ATTN_W), BF16), jax.ShapeDtypeStruct((t, 2 * KV_W), BF16),
                   jax.ShapeDtypeStruct((t, ATTN_W), BF16), jax.ShapeDtypeStruct((8, 128), F32),
                   jax.ShapeDtypeStruct((2, OUT_HALF, D_MODEL), F32)],
        scratch_shapes=[pltpu.VMEM((2, t + QBLK, 128), BF16), pltpu.VMEM((2, t + QBLK, 128), BF16),
                        pltpu.VMEM((2, t + QBLK, 128), F32), pltpu.VMEM((2, t + QBLK, 128), F32)]
        + _rs_scratch(OUT_HALF)
        + [pltpu.VMEM((t, ATTN_W), F32), pltpu.VMEM((t, ATTN_W), F32), pltpu.VMEM((t, ATTN_W), BF16),
           pltpu.VMEM((t, ATTN_W), BF16), pltpu.VMEM((t, ATTN_W), BF16), pltpu.VMEM((t, ATTN_W), BF16),
           pltpu.VMEM((t, 128), F32), pltpu.VMEM((t, 128), F32),
           pltpu.SemaphoreType.DMA((6,)), pltpu.SemaphoreType.DMA((t // PUT_ROWS,)), pltpu.SemaphoreType.DMA((t // PUT_ROWS,)),
           pltpu.SemaphoreType.DMA((RS_SEMS,)), pltpu.SemaphoreType.DMA((RS_SEMS,))],
        compiler_params=_cparams(),
    )(q_raw, kv_raw, ga, o, dmix, qw2, kw2, sinks, cos_f, sin_s, go)


CONV_CH = 256
CONV_SUB = 64
CONV_ACCS = 3


def _shifted_windows(src_ref, r0, sh_ref):
    rows = CONV_CH + CONV_PAD
    win = src_ref[pl.ds(r0, rows), :]
    for b in range(8):
        sh = win if b == 0 else pltpu.roll(win, rows - b, 0)
        for c in range(CONV_W // 128):
            sh_ref[b, c] = sh[:, c * 128:(c + 1) * 128]


def _conv_fwd(ua, ug, gb, cw, cb, lw, lb):
    t = ua.shape[0]

    def body(ua_hbm, ug_hbm, gb_hbm, cw_ref, cb_ref, lw_ref, lb_ref, cz_hbm, mix_hbm, zp_ref, sh_ref,
             ua_ref, ug_ref, gb_ref, cz_ref, mix_ref, isem, osem0, osem1):
        loads = _fetch((ua_hbm, ug_hbm, gb_hbm), (ua_ref, ug_ref, gb_ref), isem)
        outs, osems = ((cz_ref, cz_hbm), (mix_ref, mix_hbm)), (osem0, osem1)
        per_put = PUT_ROWS // CONV_CH
        zp_ref[0:CONV_PAD, :] = jnp.zeros((CONV_PAD, CONV_W), F32)
        loads[0].wait()
        loads[1].wait()

        def glu(i, carry):
            r0 = pl.multiple_of(i * CONV_CH, CONV_CH)
            rows = pl.ds(r0, CONV_CH)
            zp_ref[pl.ds(CONV_PAD + r0, CONV_CH), :] = ua_ref[rows, :] * _sigmoid(ug_ref[rows, :])
            return carry

        lax.fori_loop(0, t // CONV_CH, glu, 0)
        loads[2].wait()

        def chunk(i, carry):
            r0 = pl.multiple_of(i * CONV_CH, CONV_CH)
            _shifted_windows(zp_ref, r0, sh_ref)
            for c in range(CONV_W // 128):
                lanes = slice(c * 128, (c + 1) * 128)

                def sub(k, carry2):
                    b0 = pl.multiple_of(k * CONV_SUB, CONV_SUB)
                    acc = [jnp.broadcast_to(cb_ref[0:1, lanes], (CONV_SUB, 128))] + [None] * (CONV_ACCS - 1)
                    for j in range(CONV_TAPS):
                        off = j + CONV_PAD - (CONV_TAPS - 1)
                        term = sh_ref[off % 8, c, pl.ds(b0 + 8 * (off // 8), CONV_SUB), :] * cw_ref[j:j + 1, lanes]
                        acc[j % CONV_ACCS] = term if acc[j % CONV_ACCS] is None else acc[j % CONV_ACCS] + term
                    cz_ref[pl.ds(r0 + b0, CONV_SUB), lanes] = functools.reduce(lambda a, b: a + b, acc)
                    return carry2

                lax.fori_loop(0, CONV_CH // CONV_SUB, sub, 0)
            rows = pl.ds(r0, CONV_CH)
            cz = cz_ref[rows, :]
            mu = jnp.mean(cz, axis=-1, keepdims=True)
            xc = cz - mu
            rs = lax.rsqrt(jnp.mean(xc * xc, axis=-1, keepdims=True) + EPS)
            ln = xc * rs * lw_ref[...] + lb_ref[...]
            mix_ref[rows, :] = (_silu(ln) * _silu(gb_ref[rows, :])).astype(BF16)

            @pl.when(i % per_put == per_put - 1)
            def _():
                _put_all(outs, osems, i // per_put)

            return carry

        lax.fori_loop(0, t // CONV_CH, chunk, 0)
        _put_wait(outs, osems, t // PUT_ROWS)

    vm = pl.BlockSpec(memory_space=pltpu.VMEM)
    hbm = pl.BlockSpec(memory_space=pl.ANY)
    nput = t // PUT_ROWS
    return pl.pallas_call(
        body,
        name="conv_fwd",
        in_specs=[hbm] * 3 + [vm] * 4,
        out_specs=[hbm, hbm],
        out_shape=[jax.ShapeDtypeStruct((t, CONV_W), F32), jax.ShapeDtypeStruct((t, CONV_W), BF16)],
        scratch_shapes=[pltpu.VMEM((t + CONV_PAD, CONV_W), F32),
                        pltpu.VMEM((8, CONV_W // 128, CONV_CH + CONV_PAD, 128), F32),
                        pltpu.VMEM((t, CONV_W), F32), pltpu.VMEM((t, CONV_W), F32), pltpu.VMEM((t, CONV_W), F32),
                        pltpu.VMEM((t, CONV_W), F32), pltpu.VMEM((t, CONV_W), BF16),
                        pltpu.SemaphoreType.DMA((3,)), pltpu.SemaphoreType.DMA((nput,)), pltpu.SemaphoreType.DMA((nput,))],
        compiler_params=_cparams(),
    )(ua, ug, gb, cw, cb, lw, lb)


def _conv_bwd(ua, ug, gb, cz, dmix, cw, lw, lb):
    t = ua.shape[0]

    def body(ua_hbm, ug_hbm, gb_hbm, cz_hbm, dm_hbm, cw_ref, lw_ref, lb_ref,
             dua_hbm, dug_hbm, dgb_hbm, dcw_ref, dvec_ref, zp_ref, dp_ref, sh_ref, wacc_ref,
             ua_ref, ug_ref, gb_ref, cz_ref, dm_ref, dua_ref, dug_ref, dgb_ref, isem, osem0, osem1, osem2):
        loads = _fetch((ua_hbm, ug_hbm, gb_hbm, cz_hbm, dm_hbm), (ua_ref, ug_ref, gb_ref, cz_ref, dm_ref), isem)
        per_put = PUT_ROWS // CONV_CH
        zp_ref[0:CONV_PAD, :] = jnp.zeros((CONV_PAD, CONV_W), F32)
        dp_ref[t:t + CONV_PAD, :] = jnp.zeros((CONV_PAD, CONV_W), F32)
        wacc_ref[...] = jnp.zeros_like(wacc_ref)
        for cp in loads:
            cp.wait()

        def pointwise(i, carry):
            dcb, dlw, dlb = carry
            r0 = pl.multiple_of(i * CONV_CH, CONV_CH)
            rows = pl.ds(r0, CONV_CH)
            zp_ref[pl.ds(CONV_PAD + r0, CONV_CH), :] = ua_ref[rows, :] * _sigmoid(ug_ref[rows, :])
            cz = cz_ref[rows, :]
            mu = jnp.mean(cz, axis=-1, keepdims=True)
            xc = cz - mu
            rs = lax.rsqrt(jnp.mean(xc * xc, axis=-1, keepdims=True) + EPS)
            xh = xc * rs
            ln = xh * lw_ref[...] + lb_ref[...]
            gbv = gb_ref[rows, :]
            dy = dm_ref[rows, :].astype(F32)
            dgb_ref[rows, :] = (dy * _silu(ln) * _dsilu(gbv)).astype(BF16)
            dl = dy * _silu(gbv) * _dsilu(ln)
            dxh = dl * lw_ref[...]
            dcz = rs * (dxh - jnp.mean(dxh, axis=-1, keepdims=True)
                        - xh * jnp.mean(dxh * xh, axis=-1, keepdims=True))
            dp_ref[rows, :] = dcz

            @pl.when(i % per_put == per_put - 1)
            def _():
                _put(dgb_ref, dgb_hbm, osem2, i // per_put).start()

            return (dcb + jnp.sum(dcz, axis=0, keepdims=True),
                    dlw + jnp.sum(dl * xh, axis=0, keepdims=True),
                    dlb + jnp.sum(dl, axis=0, keepdims=True))

        zero = jnp.zeros((1, CONV_W), F32)
        dcb, dlw, dlb = lax.fori_loop(0, t // CONV_CH, pointwise, (zero, zero, zero))
        dvec_ref[...] = jnp.zeros((8, CONV_W), F32)
        dvec_ref[0:1, :] = dcb
        dvec_ref[1:2, :] = dlw
        dvec_ref[2:3, :] = dlb

        def chunk(i, carry):
            r0 = pl.multiple_of(i * CONV_CH, CONV_CH)
            _shifted_windows(dp_ref, r0, sh_ref)
            for c in range(CONV_W // 128):
                lanes = slice(c * 128, (c + 1) * 128)

                def sub(k, carry2):
                    b0 = pl.multiple_of(k * CONV_SUB, CONV_SUB)
                    acc = [None] * CONV_ACCS
                    for j in range(CONV_TAPS):
                        off = CONV_TAPS - 1 - j
                        term = sh_ref[off % 8, c, pl.ds(b0 + 8 * (off // 8), CONV_SUB), :] * cw_ref[j:j + 1, lanes]
                        acc[j % CONV_ACCS] = term if acc[j % CONV_ACCS] is None else acc[j % CONV_ACCS] + term
                    acc = functools.reduce(lambda a, b: a + b, acc)
                    rr = pl.ds(r0 + b0, CONV_SUB)
                    sg = _sigmoid(ug_ref[rr, lanes])
                    dua_ref[rr, lanes] = (acc * sg).astype(BF16)
                    dug_ref[rr, lanes] = (acc * ua_ref[rr, lanes] * sg * (1.0 - sg)).astype(BF16)
                    return carry2

                lax.fori_loop(0, CONV_CH // CONV_SUB, sub, 0)
            _shifted_windows(zp_ref, r0, sh_ref)
            for c in range(CONV_W // 128):
                lanes = slice(c * 128, (c + 1) * 128)

                def subw(k, carry2):
                    b0 = pl.multiple_of(k * CONV_SUB, CONV_SUB)
                    dcz = dp_ref[pl.ds(r0 + b0, CONV_SUB), lanes]
                    for j in range(CONV_TAPS):
                        off = j + CONV_PAD - (CONV_TAPS - 1)
                        pr = dcz * sh_ref[off % 8, c, pl.ds(b0 + 8 * (off // 8), CONV_SUB), :]
                        parts = [pr[8 * q:8 * (q + 1)] for q in range(CONV_SUB // 8)]
                        while len(parts) > 1:
                            parts = [a + b for a, b in zip(parts[0::2], parts[1::2])]
                        wacc_ref[8 * j:8 * (j + 1), lanes] += parts[0]
                    return carry2

                lax.fori_loop(0, CONV_CH // CONV_SUB, subw, 0)

            @pl.when(i % per_put == per_put - 1)
            def _():
                _put_all(((dua_ref, dua_hbm), (dug_ref, dug_hbm)), (osem0, osem1), i // per_put)

            return carry

        lax.fori_loop(0, t // CONV_CH, chunk, 0)
        _put_wait(((dua_ref, dua_hbm), (dug_ref, dug_hbm), (dgb_ref, dgb_hbm)), (osem0, osem1, osem2), t // PUT_ROWS)
        dcw_ref[...] = jnp.zeros((16, 2 * CONV_W), F32)
        for j in range(CONV_TAPS):
            dcw_ref[j // 2:j // 2 + 1, CONV_W * (j % 2):CONV_W * (j % 2 + 1)] = jnp.sum(
                wacc_ref[8 * j:8 * (j + 1), :], axis=0, keepdims=True)

    vm = pl.BlockSpec(memory_space=pltpu.VMEM)
    hbm = pl.BlockSpec(memory_space=pl.ANY)
    return pl.pallas_call(
        body,
        name="conv_bwd",
        in_specs=[hbm] * 5 + [vm] * 3,
        out_specs=[hbm] * 3 + [vm] * 2,
        out_shape=[jax.ShapeDtypeStruct((t, CONV_W), BF16)] * 3
        + [jax.ShapeDtypeStruct((16, 2 * CONV_W), F32), jax.ShapeDtypeStruct((8, CONV_W), F32)],
        scratch_shapes=[pltpu.VMEM((t + CONV_PAD, CONV_W), F32), pltpu.VMEM((t + CONV_PAD, CONV_W), F32),
                        pltpu.VMEM((8, CONV_W // 128, CONV_CH + CONV_PAD, 128), F32), pltpu.VMEM((8 * 32, CONV_W), F32)]
        + [pltpu.VMEM((t, CONV_W), F32)] * 4 + [pltpu.VMEM((t, CONV_W), BF16)] * 4
        + [pltpu.SemaphoreType.DMA((5,))] + [pltpu.SemaphoreType.DMA((t // PUT_ROWS,))] * 3,
        compiler_params=_cparams(),
    )(ua, ug, gb, cz, dmix, cw, lw, lb)


def _out_proj(mix_a, mix_b, x, tgt, gate, w_out):
    t = x.shape[0]
    tm = 512
    nstep = t // tm

    def body(ma_ref, mb_ref, x_ref, t_ref, g_ref, w_ref, dout_ref, dma_ref, dmb_ref, gw_ref, red_ref, acc_ref):
        i = pl.program_id(0)

        @pl.when(i == 0)
        def _():
            acc_ref[...] = jnp.zeros_like(acc_ref)
            red_ref[...] = jnp.zeros_like(red_ref)

        mix = jnp.concatenate([ma_ref[...], mb_ref[...]], axis=1)
        y = jnp.dot(mix, w_ref[...], preferred_element_type=F32)
        gate_v = g_ref[...]
        err = x_ref[...] + gate_v * y - t_ref[...]
        dout = err * (1.0 / D_MODEL)
        dout_ref[...] = dout
        red_ref[0:1, :] += jnp.sum(dout * y, axis=0, keepdims=True)
        red_ref[1:2, :] += jnp.sum(err * err, axis=0, keepdims=True)
        dy = (dout * gate_v).astype(BF16)
        dmix = lax.dot_general(dy, w_ref[...], (((1,), (1,)), ((), ())), preferred_element_type=F32)
        dma_ref[...] = dmix[:, 0:512].astype(BF16)
        dmb_ref[...] = dmix[:, 512:1024].astype(BF16)
        acc_ref[...] += lax.dot_general(mix, dy, (((0,), (0,)), ((), ())), preferred_element_type=F32)

        @pl.when(i == nstep - 1)
        def _():
            gw_ref[...] = acc_ref[...].astype(BF16)

    row = lambda w: pl.BlockSpec((tm, w), lambda i: (i, 0))
    const = lambda s: pl.BlockSpec(s, lambda i: (0, 0))
    return pl.pallas_call(
        body,
        name="out_proj",
        grid=(nstep,),
        in_specs=[row(512), row(512), row(D_MODEL), row(D_MODEL), const((1, D_MODEL)),
                  pl.BlockSpec((D_MODEL, D_MODEL), lambda i: (0, 0), pipeline_mode=pl.Buffered(1))],
        out_specs=[row(D_MODEL), row(512), row(512), const((D_MODEL, D_MODEL)), const((8, D_MODEL))],
        out_shape=[jax.ShapeDtypeStruct((t, D_MODEL), F32), jax.ShapeDtypeStruct((t, 512), BF16),
                   jax.ShapeDtypeStruct((t, 512), BF16), jax.ShapeDtypeStruct((D_MODEL, D_MODEL), BF16),
                   jax.ShapeDtypeStruct((8, D_MODEL), F32)],
        scratch_shapes=[pltpu.VMEM((D_MODEL, D_MODEL), F32)],
        compiler_params=_cparams(dimension_semantics=("arbitrary",)),
    )(mix_a, mix_b, x, tgt, gate, w_out)


DPROJ_WIDTHS = (512, 256, 512, 512, 512, 512)
DPROJ_STARTS = (0, 512, 768, 1280, 1792, 2304)
WIN_W = 768
WIN_START = (0, 640, 1408, 2048)
WIN_OFF = (0, 64, 0, 64)
N_GW = N_CHIPS


def _window_pieces(s):
    lo, hi = WIN_START[s], WIN_START[s] + WIN_W
    out = []
    for p, (st, w) in enumerate(zip(DPROJ_STARTS, DPROJ_WIDTHS)):
        a, b = max(lo, st), min(hi, st + w)
        if a < b:
            out.append((p, a - st, b - a, a - lo))
    return out


def _in_proj_bwd(dparts, h, x, dout, s1, nw, wt_full, dcw, dvec, sm_a, row0):
    t = x.shape[0]
    tm = 512
    nstep = N_GW + t // tm
    n_sem = 20
    rows0 = 32
    hs = rows0 // 2
    npart = len(DPROJ_WIDTHS)

    def body(*refs):
        d_hbm, d_ref = refs[:npart], refs[npart:2 * npart]
        (x_ref, dout_ref, s1_ref, nw_ref, h_ref, wt_hbm, dcw_ref, dvec_ref, sma_ref, row0_ref,
         gx_ref, gw_hbm, ssum_ref, rows_ref,
         stg_ref, wt_ref, gt_ref, sib_ref, out_ref, in_ref, res_ref, sall_ref, red_ref, sm0_ref, ssib_ref, schip_ref, sres_ref,
         wsem, lsem, ssem, rsem) = refs[2 * npart:]
        i = pl.program_id(0)
        x_, y_, c, chips = _place()
        j = 2 * x_ + y_
        dev = 2 * j + c
        sib = (x_, y_, 1 - c)
        rc = functools.partial(_remote, ssem, rsem)
        rel_chip = [2 * cx + cy for cx, cy in chips] + [j]
        peers = [(px, py, pc) for px in (x_, 1 - x_) for py in (y_, 1 - y_) for pc in (c, 1 - c)][1:]
        wt_copy = pltpu.make_async_copy(wt_hbm, wt_ref, lsem.at[0])

        def window(case, slot):
            return [pltpu.make_async_copy(d_hbm[p].at[:, pl.ds(c0, w)], stg_ref.at[slot, :, pl.ds(w0, w)], wsem.at[slot, n])
                    for n, (p, c0, w, w0) in enumerate(_window_pieces(case))]

        def to_sibling(k):
            return rc(k, gt_ref.at[k, 1 - c], sib_ref.at[k], sib)

        def to_chip(k):
            return rc(4 + k, out_ref.at[k], in_ref.at[k], (*chips[k], c))

        def trade(k):
            to_sibling(k).wait_recv()

            def add(n, carry):
                rr = pl.ds(pl.multiple_of(n * RS_CH, RS_CH), RS_CH)
                out_ref[k, rr, :] = (gt_ref[k, c, rr, :].astype(F32) + sib_ref[k, rr, :].astype(F32)).astype(BF16)
                return carry

            lax.fori_loop(0, IN_HALF // RS_CH, add, 0)
            to_chip(k).start()

        mine_s = pl.ds(pl.multiple_of(c * hs, 8), hs)
        other_s = pl.ds(pl.multiple_of((1 - c) * hs, 8), hs)

        def small_to_sibling():
            return rc(15, sm0_ref.at[other_s], ssib_ref, sib)

        def small_to_chip(k):
            return rc(16 + k, schip_ref.at[j], schip_ref.at[j], (*chips[k], c))

        def small_share():
            return rc(19, sres_ref.at[c], sres_ref.at[c], sib)

        for k in range(N_GW):
            @pl.when(i == k)
            def _(k=k):
                slot = k % 2
                if k == 0:
                    red_ref[...] = jnp.zeros_like(red_ref)
                    wt_copy.start()
                    sm0_ref[...] = jnp.zeros_like(sm0_ref)
                    sm0_ref[0:16, :] = dcw_ref[...]
                    sm0_ref[16:17, 0:CONV_W] = dvec_ref[0:1, :]
                    sm0_ref[16:17, CONV_W:2 * CONV_W] = dvec_ref[1:2, :]
                    sm0_ref[17:18, 0:CONV_W] = dvec_ref[2:3, :]
                    for r in range(3):
                        sm0_ref[17:18, CONV_W + 128 * r:CONV_W + 128 * (r + 1)] = sma_ref[r:r + 1, :]
                    sm0_ref[18:19, :] = row0_ref[1:2, :]
                    small_to_sibling().start()
                if k == 1:
                    small_to_sibling().wait_recv()
                    schip_ref[j] = sm0_ref[mine_s, :] + ssib_ref[...]
                    for kk in range(3):
                        small_to_chip(kk).start()
                if k == N_GW - 1:
                    for kk in range(3):
                        jk = rel_chip[kk]
                        rc(16 + kk, schip_ref.at[jk], schip_ref.at[jk], sib).wait_recv()
                    tot = schip_ref[0]
                    for d in range(1, N_CHIPS):
                        tot = tot + schip_ref[d]
                    sres_ref[c] = tot
                    small_share().start()
                for case in range(N_CHIPS):
                    if k == 0:
                        @pl.when(rel_chip[0] == case)
                        def _():
                            for cp in window(case, 0):
                                cp.start()
                    if k + 1 < N_GW:
                        @pl.when(rel_chip[k + 1] == case)
                        def _():
                            for cp in window(case, 1 - slot):
                                cp.start()
                for case in range(N_CHIPS):
                    @pl.when(rel_chip[k] == case)
                    def _():
                        for cp in window(case, slot):
                            cp.wait()
                g = lax.dot_general(stg_ref[slot], h_ref[...], (((0,), (0,)), ((), ())), preferred_element_type=F32)
                for off in sorted(set(WIN_OFF)):
                    @pl.when(rel_chip[k] % 2 == (1 if off else 0))
                    def _():
                        gt_ref[k, 0] = g[off:off + IN_HALF].astype(BF16)
                        gt_ref[k, 1] = g[off + IN_HALF:off + 2 * IN_HALF].astype(BF16)
                to_sibling(k).start()
                if k >= 1:
                    trade(k - 1)

        @pl.when(i == N_GW)
        def _():
            wt_copy.wait()

        @pl.when(i >= N_GW)
        def _():
            xv = x_ref[...]
            r = lax.rsqrt(jnp.mean(xv * xv, axis=-1, keepdims=True) + EPS)
            xh = xv * r
            n = xh * nw_ref[...]
            dproj = jnp.concatenate([ref[...] for ref in d_ref], axis=1)
            dh = jnp.dot(dproj, wt_ref[...], preferred_element_type=F32)
            red_ref[0:1, :] += jnp.sum(dh, axis=0, keepdims=True)
            red_ref[1:2, :] += jnp.sum(dh * n, axis=0, keepdims=True)
            dn = dh * s1_ref[...]
            red_ref[2:3, :] += jnp.sum(dn * xh, axis=0, keepdims=True)
            dxh = dn * nw_ref[...]
            gx_ref[...] = dout_ref[...] + r * (dxh - xh * jnp.mean(dxh * xh, axis=-1, keepdims=True))

        @pl.when(i == nstep - 1)
        def _():
            sall_ref[dev] = row0_ref[...]
            sall_ref[dev, 2:5, :] = red_ref[0:3, :]
            sends = [rc(8 + k, sall_ref.at[dev], sall_ref.at[dev], peer) for k, peer in enumerate(peers)]
            for cp in sends:
                cp.start()
            sends += [to_sibling(k) for k in range(N_GW)] + [to_chip(k) for k in range(3)]
            sends += [small_to_sibling(), small_share()] + [small_to_chip(k) for k in range(3)]
            own = N_GW - 1
            to_sibling(own).wait_recv()
            for k in range(3):
                to_chip(k).wait_recv()

            def total(n, carry):
                rr = pl.ds(pl.multiple_of(n * RS_CH, RS_CH), RS_CH)
                acc = gt_ref[own, c, rr, :].astype(F32) + sib_ref[own, rr, :].astype(F32)
                for k in range(3):
                    acc = acc + in_ref[k, rr, :].astype(F32)
                res_ref[c, rr, :] = acc
                return carry

            lax.fori_loop(0, IN_HALF // RS_CH, total, 0)
            share = rc(7, res_ref.at[c], res_ref.at[c], sib)
            share.start()
            sends.append(share)
            for k, (px, py, pc) in enumerate(peers):
                pdev = 4 * px + 2 * py + pc
                rc(8 + k, sall_ref.at[pdev], sall_ref.at[pdev], (px, py, pc)).wait_recv()
            rows_ref[...] = sall_ref[...]
            rc(19, sres_ref.at[1 - c], sres_ref.at[1 - c], sib).wait_recv()
            ssum_ref[0:hs, :] = sres_ref[0]
            ssum_ref[hs:rows0, :] = sres_ref[1]
            rc(7, res_ref.at[1 - c], res_ref.at[1 - c], sib).wait_recv()
            back = pltpu.make_async_copy(res_ref, gw_hbm, lsem.at[1])
            back.start()
            for cp in sends:
                cp.wait_send()
            back.wait()

    blk = lambda i: jnp.maximum(i - N_GW, 0)
    row = lambda w: pl.BlockSpec((tm, w), lambda i: (blk(i), 0))
    vec = pl.BlockSpec((1, D_MODEL), lambda i: (0, 0))
    const = lambda shape: pl.BlockSpec(shape, lambda i: (0,) * len(shape))
    hbm = pl.BlockSpec(memory_space=pl.ANY)
    return pl.pallas_call(
        body,
        name="in_proj_bwd",
        grid=(nstep,),
        in_specs=[hbm] * npart + [row(w) for w in DPROJ_WIDTHS] + [row(D_MODEL), row(D_MODEL), vec, vec,
                  pl.BlockSpec((t, D_MODEL), lambda i: (0, 0), pipeline_mode=pl.Buffered(1)), hbm, const((16, D_MODEL)),
                  const((8, CONV_W)), const((8, 128)), const((8, D_MODEL))],
        out_specs=[row(D_MODEL), hbm, const((rows0, D_MODEL)), const((N_DEV, 8, D_MODEL))],
        out_shape=[jax.ShapeDtypeStruct((t, D_MODEL), F32), jax.ShapeDtypeStruct((2, IN_HALF, D_MODEL), F32),
                   jax.ShapeDtypeStruct((rows0, D_MODEL), F32), jax.ShapeDtypeStruct((N_DEV, 8, D_MODEL), F32)],
        scratch_shapes=[pltpu.VMEM((2, t, WIN_W), BF16), pltpu.VMEM((IN_W, D_MODEL), BF16),
                        pltpu.VMEM((N_CHIPS, 2, IN_HALF, D_MODEL), BF16), pltpu.VMEM((N_CHIPS, IN_HALF, D_MODEL), BF16),
                        pltpu.VMEM((3, IN_HALF, D_MODEL), BF16), pltpu.VMEM((3, IN_HALF, D_MODEL), BF16),
                        pltpu.VMEM((2, IN_HALF, D_MODEL), F32), pltpu.VMEM((N_DEV, 8, D_MODEL), F32),
                        pltpu.VMEM((8, D_MODEL), F32), pltpu.VMEM((rows0, D_MODEL), F32), pltpu.VMEM((hs, D_MODEL), F32),
                        pltpu.VMEM((N_CHIPS, hs, D_MODEL), F32),
                        pltpu.VMEM((2, hs, D_MODEL), F32), pltpu.SemaphoreType.DMA((2, 3)), pltpu.SemaphoreType.DMA((2,)),
                        pltpu.SemaphoreType.DMA((n_sem,)), pltpu.SemaphoreType.DMA((n_sem,))],
        compiler_params=_cparams(dimension_semantics=("arbitrary",)),
    )(*dparts, *dparts, x, dout, s1, nw, h, wt_full, dcw, dvec, sm_a, row0)


MESH = pl.DeviceIdType.MESH


def _place():
    x, y, c = lax.axis_index("x"), lax.axis_index("y"), lax.axis_index("c")
    chips = [(1 - x, y), (x, 1 - y), (1 - x, 1 - y)]
    return x, y, c, chips


def _remote(sems_s, sems_r, k, src, dst, to):
    return pltpu.make_async_remote_copy(src_ref=src, dst_ref=dst, send_sem=sems_s.at[k], recv_sem=sems_r.at[k],
                                        device_id=to, device_id_type=MESH)


RS_CH = 32
RS_SEMS = 5


def _rs_to_sibling(rc, s0, theirs, sib_ref, sib):
    cp = rc(s0, theirs, sib_ref, sib)
    cp.start()
    return cp


def _rs_trade(rc, s0, theirs, mine, sib_ref, out_ref, in_ref, rows, c, sib, chips):
    rc(s0, theirs, sib_ref, sib).wait_recv()
    cps = []
    for k, (cx, cy) in enumerate(chips):
        jk = 2 * cx + cy

        def add(i, carry, jk=jk, k=k):
            rr = pl.ds(pl.multiple_of(i * RS_CH, RS_CH), RS_CH)
            out_ref[k, rr, :] = (mine[jk, rr, :].astype(F32) + sib_ref[jk, rr, :].astype(F32)).astype(BF16)
            return carry

        lax.fori_loop(0, rows // RS_CH, add, 0)
        cps.append(rc(s0 + 1 + k, out_ref.at[k], in_ref.at[k], (cx, cy, c)))
        cps[-1].start()
    return cps


def _rs_total(rc, s0, mine, sib_ref, out_ref, in_ref, res_ref, rows, j, c, sib):
    for k in range(3):
        rc(s0 + 1 + k, out_ref.at[k], in_ref.at[k], sib).wait_recv()

    def total(i, carry):
        rr = pl.ds(pl.multiple_of(i * RS_CH, RS_CH), RS_CH)
        acc = mine[j, rr, :].astype(F32) + sib_ref[j, rr, :].astype(F32)
        for k in range(3):
            acc = acc + in_ref[k, rr, :].astype(F32)
        res_ref[c, rr, :] = acc
        return carry

    lax.fori_loop(0, rows // RS_CH, total, 0)
    cp = rc(s0 + 4, res_ref.at[c], res_ref.at[c], sib)
    cp.start()
    return cp


def _rs_done(rc, s0, res_ref, c, sib):
    rc(s0 + 4, res_ref.at[1 - c], res_ref.at[1 - c], sib).wait_recv()


def _rs_scratch(rows):
    return [pltpu.VMEM((N_CHIPS, rows, D_MODEL), BF16), pltpu.VMEM((3, rows, D_MODEL), BF16),
            pltpu.VMEM((3, rows, D_MODEL), BF16)]


MAIN_W = 640
MAIN_DST = (((0, 0, 512), (1, 0, 128)), ((2, 0, 512), (3, 0, 128)), ((3, 128, 384), (4, 0, 256)), ((4, 384, 128), (5, 0, 512)))
PAIR_DST = ((1, 128, 128), (4, 256, 128))


def _in_proj_gather(x, wt, c_row, w_ada, b_ada, nw):
    t = x.shape[0]
    ch = 512
    n_sem = 16

    def body(x_hbm, wt_ref, c_ref, wada_ref, bada_ref, nw_ref,
             q_hbm, kv_hbm, ga_hbm, ua_hbm, ug_hbm, gb_hbm, h_hbm, w4_hbm, call_ref, ada_ref,
             x_ref, h_ref, w4_ref, stg_ref, pstg_ref, part_ref, lsem, osem, wsem, ssem, rsem):
        outs = (q_hbm, kv_hbm, ga_hbm, ua_hbm, ug_hbm, gb_hbm)
        x_, y_, c, chips = _place()
        j = 2 * x_ + y_
        dev = 2 * j + c
        sib = (x_, y_, 1 - c)
        idx = [2 * cx + cy for cx, cy in chips]
        rc = functools.partial(_remote, ssem, rsem)
        x_copy = pltpu.make_async_copy(x_hbm, x_ref, lsem.at[0])
        x_copy.start()

        def rows_of(s, cc):
            return pl.ds(pl.multiple_of(2 * IN_HALF * s + IN_HALF * cc, 16), IN_HALF)

        w4_ref[rows_of(j, 0), :] = wt_ref[0].astype(BF16)
        w4_ref[rows_of(j, 1), :] = wt_ref[1].astype(BF16)
        call_ref[dev] = c_ref[...]
        sends = []
        peers = [(px, py, pc) for px in (x_, 1 - x_) for py in (y_, 1 - y_) for pc in (c, 1 - c)][1:]
        for k, peer in enumerate(peers):
            sends.append(rc(k, call_ref.at[dev], call_ref.at[dev], peer))
        for cp in sends:
            cp.start()

        for k, (px, py, pc) in enumerate(peers):
            pdev = 4 * px + 2 * py + pc
            rc(k, call_ref.at[pdev], call_ref.at[pdev], (px, py, pc)).wait_recv()
        rowid = lax.broadcasted_iota(jnp.int32, (N_DEV, D_MODEL), 0)
        call = jnp.zeros((N_DEV, D_MODEL), F32)
        for r in range(N_DEV):
            call = jnp.where(rowid == r, jnp.broadcast_to(call_ref[r], (N_DEV, D_MODEL)), call)
        bsh = bada_ref[:, 0:ADA_SHARD]
        for k in range(1, N_CHIPS):
            bsh = jnp.where(j == k, bada_ref[:, ADA_SHARD * k:ADA_SHARD * (k + 1)], bsh)
        part = jnp.dot(_silu(call).astype(BF16), wada_ref[...].astype(BF16), preferred_element_type=F32) + bsh
        for r in range(N_DEV):
            part_ref[r] = part[r:r + 1, :]
        ada_ref[j] = part_ref[dev]
        for k, chip in enumerate(chips):
            sends.append(rc(13 + k, part_ref.at[2 * idx[k] + c], ada_ref.at[j], (*chip, c)))
            sends[-1].start()
        for k, chip in enumerate(chips):
            sends.append(rc(7 + k, w4_ref.at[rows_of(j, c)], w4_ref.at[rows_of(j, c)], (*chip, c)))
            sends[-1].start()
        for k in range(3):
            rc(13 + k, ada_ref.at[idx[k]], ada_ref.at[idx[k]], sib).wait_recv()

        shift = jnp.concatenate([ada_ref[0], ada_ref[1][:, 0:256]], axis=1)
        s1 = 1.0 + jnp.concatenate([ada_ref[1][:, 256:768], ada_ref[2][:, 0:512]], axis=1)
        x_copy.wait()

        def norm(i, carry):
            rr = pl.ds(pl.multiple_of(i * ch, ch), ch)
            xv = x_ref[rr, :]
            r = lax.rsqrt(jnp.mean(xv * xv, axis=-1, keepdims=True) + EPS)
            h_ref[rr, :] = ((xv * r) * nw_ref[...] * s1 + shift).astype(BF16)
            return carry

        lax.fori_loop(0, t // ch, norm, 0)
        h_copy = pltpu.make_async_copy(h_ref, h_hbm, lsem.at[1])
        h_copy.start()

        def put_main(case, slot):
            cps, col = [], 0
            for n, (a, c0, w) in enumerate(MAIN_DST[case]):
                cps.append(pltpu.make_async_copy(stg_ref.at[slot, :, pl.ds(col, w)], outs[a].at[:, pl.ds(c0, w)], osem.at[slot, n]))
                col += w
            return cps

        def put_pair(case, slot):
            a, c0, w = PAIR_DST[case]
            return pltpu.make_async_copy(pstg_ref.at[slot], outs[a].at[:, pl.ds(c0, w)], osem.at[slot, 2])

        def project(first_row, width, dst, slot):
            wrows = pl.ds(pl.multiple_of(first_row, 128), width)

            def blk(i, carry):
                rr = pl.ds(pl.multiple_of(i * ch, ch), ch)
                dst[slot, rr, :] = lax.dot_general(h_ref[rr, :], w4_ref[wrows, :], (((1,), (1,)), ((), ())),
                                                   preferred_element_type=F32)
                return carry

            lax.fori_loop(0, t // ch, blk, 0)

        def phase(p, s, pair):
            slot = p % 2
            if p >= 2:
                for case in range(N_CHIPS):
                    @pl.when(order[p - 2] == case)
                    def _():
                        for cp in put_main(case, slot):
                            cp.wait()
            if p == 3:
                for case in range(2):
                    @pl.when(j // 2 == case)
                    def _():
                        put_pair(case, 0).wait()
            project(2 * IN_HALF * s + 64 * (s % 2), MAIN_W, stg_ref, slot)
            for case in range(N_CHIPS):
                @pl.when(s == case)
                def _():
                    for cp in put_main(case, slot):
                        cp.start()
            if pair is not None:
                project(MAIN_W + 2 * (2 * IN_HALF) * pair, 128, pstg_ref, slot % 2 if p == 2 else 1)
                for case in range(2):
                    @pl.when(pair == case)
                    def _():
                        put_pair(case, 0 if p == 2 else 1).start()

        order = [j] + idx
        w_out = [pltpu.make_async_copy(w4_ref.at[pl.ds(pl.multiple_of(2 * IN_HALF * s, 32), 2 * IN_HALF)],
                                       w4_hbm.at[pl.ds(pl.multiple_of(2 * IN_HALF * s, 32), 2 * IN_HALF)], wsem.at[p])
                 for p, s in enumerate(order)]
        w_out[0].start()
        phase(0, j, None)
        passed = []
        for k in range(3):
            jk = idx[k]
            rc(7 + k, w4_ref.at[rows_of(jk, c)], w4_ref.at[rows_of(jk, c)], sib).wait_recv()
            passed.append(rc(10 + k, w4_ref.at[rows_of(jk, c)], w4_ref.at[rows_of(jk, c)], sib))
            passed[-1].start()
            rc(10 + k, w4_ref.at[rows_of(jk, 1 - c)], w4_ref.at[rows_of(jk, 1 - c)], sib).wait_recv()
            w_out[1 + k].start()
            if k == 0:
                phase(1, jk, None)
            elif k == 1:
                phase(2, jk, j // 2)
            else:
                phase(3, jk, 1 - j // 2)

        for case in range(N_CHIPS):
            for p in (2, 3):
                @pl.when(order[p] == case)
                def _():
                    for cp in put_main(case, p % 2):
                        cp.wait()
        for case in range(2):
            @pl.when(1 - j // 2 == case)
            def _():
                put_pair(case, 1).wait()
        h_copy.wait()
        for cp in w_out:
            cp.wait()
        for cp in sends + passed:
            cp.wait_send()

    vm = pl.BlockSpec(memory_space=pltpu.VMEM)
    hbm = pl.BlockSpec(memory_space=pl.ANY)
    widths = (512, 256, 512, 512, 512, 512)
    return pl.pallas_call(
        body,
        name="in_proj",
        in_specs=[hbm, vm, vm, vm, vm, vm],
        out_specs=[hbm] * 8 + [vm, vm],
        out_shape=[jax.ShapeDtypeStruct((t, w), F32) for w in widths]
        + [jax.ShapeDtypeStruct((t, D_MODEL), BF16), jax.ShapeDtypeStruct((IN_W, D_MODEL), BF16),
           jax.ShapeDtypeStruct((N_DEV, 1, D_MODEL), F32), jax.ShapeDtypeStruct((N_CHIPS, 1, ADA_SHARD), F32)],
        scratch_shapes=[pltpu.VMEM((t, D_MODEL), F32), pltpu.VMEM((t, D_MODEL), BF16), pltpu.VMEM((IN_W, D_MODEL), BF16),
                        pltpu.VMEM((2, t, MAIN_W), F32), pltpu.VMEM((2, t, 128), F32), pltpu.VMEM((N_DEV, 1, ADA_SHARD), F32),
                        pltpu.SemaphoreType.DMA((2,)), pltpu.SemaphoreType.DMA((2, 3)), pltpu.SemaphoreType.DMA((N_CHIPS,)),
                        pltpu.SemaphoreType.DMA((n_sem,)), pltpu.SemaphoreType.DMA((n_sem,))],
        compiler_params=_cparams(),
    )(x, wt, c_row, w_ada, b_ada, nw)


def _adamw_math(w, g, m, v):
    m2 = ADAM_B1 * m + (1.0 - ADAM_B1) * g
    v2 = ADAM_B2 * v + (1.0 - ADAM_B2) * (g * g)
    m_hat = m2 / (1.0 - ADAM_B1 ** ADAM_STEP)
    v_hat = v2 / (1.0 - ADAM_B2 ** ADAM_STEP)
    delta = -ADAM_LR * (m_hat / (jnp.sqrt(v_hat) + ADAM_EPS) + ADAM_WD * w)
    return delta, m2, v2


def _adamw(name, w, g, m, v, tm, through=None):
    r, cdim = w.shape
    nstep = r // tm
    extra = [] if through is None else [through]

    def body(w_ref, g_ref, m_ref, v_ref, *rest):
        g2_ref, d_ref, m2_ref, v2_ref = rest[len(extra):len(extra) + 4]
        g = g_ref[...]
        g2_ref[...] = g
        d_ref[...], m2_ref[...], v2_ref[...] = _adamw_math(w_ref[...], g, m_ref[...], v_ref[...])
        if extra:
            rest[-1][...] = rest[0][...]

    blk = pl.BlockSpec((tm, cdim), lambda i: (i, 0))
    eblk = [pl.BlockSpec((e.shape[0] // nstep, e.shape[1]), lambda i: (i, 0)) for e in extra]
    return pl.pallas_call(
        body,
        name=name,
        grid=(nstep,),
        in_specs=[blk] * 4 + eblk,
        out_specs=[blk] * 4 + eblk,
        out_shape=[jax.ShapeDtypeStruct((r, cdim), F32)] * 4 + [jax.ShapeDtypeStruct(e.shape, e.dtype) for e in extra],
        compiler_params=_cparams(dimension_semantics=("arbitrary",)),
    )(w, g, m, v, *extra)


def _adamw_ada(w, m, v, cact_t, dcols):
    r, cdim = w.shape
    tm = 256

    def body(w_ref, m_ref, v_ref, ct_ref, dc_ref, g_ref, d_ref, m2_ref, v2_ref):
        g = jnp.dot(ct_ref[...], dc_ref[...], preferred_element_type=F32, precision=lax.Precision.HIGHEST)
        g_ref[...] = g
        d_ref[...], m2_ref[...], v2_ref[...] = _adamw_math(w_ref[...], g, m_ref[...], v_ref[...])

    blk = pl.BlockSpec((tm, cdim), lambda i: (i, 0))
    return pl.pallas_call(
        body,
        name="adamw_w_ada",
        grid=(r // tm,),
        in_specs=[blk] * 3 + [pl.BlockSpec((tm, N_DEV), lambda i: (i, 0)), pl.BlockSpec((N_DEV, cdim), lambda i: (0, 0))],
        out_specs=[blk] * 4,
        out_shape=[jax.ShapeDtypeStruct((r, cdim), F32)] * 4,
        compiler_params=_cparams(dimension_semantics=("arbitrary",)),
    )(w, m, v, cact_t, dcols)


def _adamw_small(ws, ms, vs, ssum, rows):
    n = len(ws)

    def body(*refs):
        w_r, m_r, v_r = refs[0:n], refs[n:2 * n], refs[2 * n:3 * n]
        ss_ref, rows_ref = refs[3 * n], refs[3 * n + 1]
        g_r, d_r, m2_r, v2_r = (refs[3 * n + 2 + k * n:3 * n + 2 + (k + 1) * n] for k in range(4))
        loss_ref = refs[7 * n + 2]
        j = 2 * lax.axis_index("x") + lax.axis_index("y")
        rsum = rows_ref[0]
        for d in range(1, N_DEV):
            rsum = rsum + rows_ref[d]
        taps = []
        for t in range(CONV_TAPS):
            row = ss_ref[t // 2:t // 2 + 1, :]
            c0 = CONV_W * (t % 2)
            pick = row[:, c0:c0 + 128]
            for k in range(1, N_CHIPS):
                pick = jnp.where(j == k, row[:, c0 + 128 * k:c0 + 128 * (k + 1)], pick)
            taps.append(pick)
        grads = [jnp.concatenate([rsum[2:3], rsum[3:4], rsum[0:1]], axis=1), rsum[4:5],
                 ss_ref[17:18, 512:512 + HEAD_DIM], ss_ref[17:18, 640:640 + HEAD_DIM], ss_ref[17:18, 768:776],
                 None, ss_ref[16:17, 0:CONV_W], ss_ref[16:17, CONV_W:2 * CONV_W], ss_ref[17:18, 0:CONV_W]]
        for i in range(n):
            if grads[i] is None:
                for t in range(CONV_TAPS):
                    g_r[i][t:t + 1, :] = taps[t]
                g = g_r[i][...]
            else:
                g = grads[i]
                g_r[i][...] = g
            d_r[i][...], m2_r[i][...], v2_r[i][...] = _adamw_math(w_r[i][...], g, m_r[i][...], v_r[i][...])
        loss_ref[...] = (0.5 / D_MODEL) * jnp.sum(ss_ref[18:19, :], axis=1, keepdims=True)

    vm = pl.BlockSpec(memory_space=pltpu.VMEM)
    shapes = [jax.ShapeDtypeStruct(w.shape, F32) for w in ws]
    out = pl.pallas_call(
        body,
        name="adamw_small",
        in_specs=[vm] * (3 * n + 2),
        out_specs=[vm] * (4 * n + 1),
        out_shape=shapes * 4 + [jax.ShapeDtypeStruct((1, 1), F32)],
        compiler_params=_cparams(),
    )(*ws, *ms, *vs, ssum, rows)
    return out[0:n], out[n:2 * n], out[2 * n:3 * n], out[3 * n:4 * n], out[4 * n]


def _rope_tables(t):
    inv = ROPE_THETA ** (-jnp.arange(0, HEAD_DIM, 2, dtype=F32) / HEAD_DIM)
    ang = jnp.arange(t, dtype=F32)[:, None] * inv[None, :]
    cos, sin = jnp.cos(ang), jnp.sin(ang)
    return jnp.tile(cos, (1, 4)), jnp.tile(jnp.concatenate([-sin, sin], axis=1), (1, 2))


def kernel(x, c, w_ada, b_ada, norm_w, w_in, q_norm_w, k_norm_w, sinks, conv_w, conv_b, ln_w, ln_b, w_out, loss_target, m_w_ada, m_b_ada, m_norm_w, m_w_in, m_q_norm_w, m_k_norm_w, m_sinks, m_conv_w, m_conv_b, m_ln_w, m_ln_b, m_w_out, v_w_ada, v_b_ada, v_norm_w, v_w_in, v_q_norm_w, v_k_norm_w, v_sinks, v_conv_w, v_conv_b, v_ln_w, v_ln_b, v_w_out):
    xi, yi = lax.axis_index("x"), lax.axis_index("y")
    j = 2 * xi + yi
    x2, tgt = x[0], loss_target[0]
    t = x2.shape[0]

    wt_s, mt_s, vt_s = w_in[0].T, m_w_in[0].T, v_w_in[0].T
    cw_pad = jnp.pad(conv_w[0], ((0, 1), (0, 0)))

    q_raw, kv_raw, ga, ua, ug, gb, h, w_full, call, ada4 = _in_proj_gather(
        x2, wt_s.reshape(2, IN_HALF, D_MODEL), c, w_ada[0], b_ada, norm_w)
    ada = ada4.reshape(1, 3 * D_MODEL)
    s1, gate = 1.0 + ada[:, D_MODEL:2 * D_MODEL], ada[:, 2 * D_MODEL:]

    cos_f, sin_s = _rope_tables(t)
    qw2, kw2 = jnp.tile(q_norm_w, (1, 2)), jnp.tile(k_norm_w, (1, 2))

    o, mix_a, wo4, cw4 = _attn_fwd(q_raw, kv_raw, ga, qw2, kw2, sinks, cos_f, sin_s,
                                   w_out[0].reshape(2, OUT_HALF, D_MODEL), cw_pad)
    w_out_full = wo4.reshape(D_MODEL, D_MODEL)
    cw_full = jnp.concatenate([cw4[i] for i in range(N_CHIPS)], axis=1)
    cz, mix_b = _conv_fwd(ua, ug, gb, cw_full, conv_b, ln_w, ln_b)
    dout, dmix_a, dmix_b, gwo_bf, red_o = _out_proj(mix_a, mix_b, x2, tgt, gate, w_out_full)

    dq, dkv, dga, sm_a, gwo = _attn_bwd(q_raw, kv_raw, ga, o, dmix_a, qw2, kw2, sinks, cos_f, sin_s,
                                        gwo_bf.reshape(N_CHIPS, 2, OUT_HALF, D_MODEL))
    dua, dug, dgb, dcw, dvec = _conv_bwd(ua, ug, gb, cz, dmix_b, cw_full, ln_w, ln_b)
    dparts = (dq, dkv, dga, dua, dug, dgb)

    grad_x, gw, ssum, rows = _in_proj_bwd(dparts, h, x2, dout, s1, norm_w, w_full, dcw, dvec, sm_a, red_o)

    gt_w_in = gw.reshape(2 * IN_HALF, D_MODEL)
    g_w_out = gwo.reshape(D_MODEL // N_CHIPS, D_MODEL)
    d_ada_all = jnp.concatenate([rows[:, 2], rows[:, 3], rows[:, 0]], axis=1)
    dcols = lax.dynamic_slice(d_ada_all, (0, ADA_SHARD * j), (N_DEV, ADA_SHARD))
    cact_t = jax.nn.silu(call.reshape(N_DEV, D_MODEL)).T

    g_w_ada, d_w_ada, nm_w_ada, nv_w_ada = _adamw_ada(w_ada[0], m_w_ada[0], v_w_ada[0], cact_t, dcols)
    gt_w_in, dt_w_in, nmt_w_in, nvt_w_in, grad_x = _adamw("adamw_w_in", wt_s, gt_w_in, mt_s, vt_s, 176, through=grad_x)
    g_w_in, d_w_in, nm_w_in, nv_w_in = gt_w_in.T, dt_w_in.T, nmt_w_in.T, nvt_w_in.T
    g_w_out, d_w_out, nm_w_out, nv_w_out = _adamw("adamw_w_out", w_out[0], g_w_out, m_w_out[0], v_w_out[0], 128)
    ws = [b_ada, norm_w, q_norm_w, k_norm_w, sinks, conv_w[0], conv_b, ln_w, ln_b]
    ms = [m_b_ada, m_norm_w, m_q_norm_w, m_k_norm_w, m_sinks, m_conv_w[0], m_conv_b, m_ln_w, m_ln_b]
    vs = [v_b_ada, v_norm_w, v_q_norm_w, v_k_norm_w, v_sinks, v_conv_w[0], v_conv_b, v_ln_w, v_ln_b]
    gs, ds, nms, nvs, loss11 = _adamw_small(ws, ms, vs, ssum, rows)
    loss = loss11[0, 0]

    def order(ada_v, in_v, out_v, sm):
        b, nw_, qw_, kw_, sk_, cw_, cb_, lw_, lb_ = sm
        return [ada_v[None], b, nw_, in_v[None], qw_, kw_, sk_, cw_[None], cb_, lw_, lb_, out_v[None]]

    grads = order(g_w_ada, g_w_in, g_w_out, gs)
    deltas = order(d_w_ada, d_w_in, d_w_out, ds)
    new_m = order(nm_w_ada, nm_w_in, nm_w_out, nms)
    new_v = order(nv_w_ada, nv_w_in, nv_w_out, nvs)
    return (loss, grad_x[None], *grads, *deltas, *new_m, *new_v)
```

```python
import functools

import jax
import jax.numpy as jnp
from jax import lax
from jax.experimental import pallas as pl
from jax.experimental.pallas import tpu as pltpu

F32 = jnp.float32
BF16 = jnp.bfloat16

D_MODEL = 1024
ATTN_W = 512
KV_W = 128
CONV_W = 512
IN_W = 2816
HEAD_DIM = 64
CONV_TAPS = 31
QBLK = 128
EPS = 1e-6
ROPE_THETA = 10000.0

ADAM_LR = 0.001
ADAM_B1 = 0.9
ADAM_B2 = 0.999
ADAM_EPS = 1e-08
ADAM_WD = 0.01
ADAM_STEP = 10

N_CHIPS = 4
N_DEV = 8
IN_HALF = IN_W // N_CHIPS // 2
OUT_HALF = D_MODEL // N_CHIPS // 2
ADA_SHARD = 3 * D_MODEL // N_CHIPS

VMEM_LIMIT = 56 * 1024 * 1024
CONV_PAD = 32


def _cparams(**kw):
    return pltpu.CompilerParams(vmem_limit_bytes=VMEM_LIMIT, **kw)


def _sigmoid(v):
    return 1.0 / (1.0 + jnp.exp(-v))


def _silu(v):
    return v * _sigmoid(v)


def _dsilu(v):
    s = _sigmoid(v)
    return s * (1.0 + v * (1.0 - s))


def _lane(shape):
    return lax.broadcasted_iota(jnp.int32, shape, len(shape) - 1)


PUT_ROWS = 512


def _fetch(hbm_refs, vmem_refs, sem):
    cps = [pltpu.make_async_copy(h, v, sem.at[i]) for i, (h, v) in enumerate(zip(hbm_refs, vmem_refs))]
    for cp in cps:
        cp.start()
    return cps


def _put(vmem_ref, hbm_ref, sem, m):
    r = pl.ds(pl.multiple_of(m * PUT_ROWS, PUT_ROWS), PUT_ROWS)
    return pltpu.make_async_copy(vmem_ref.at[r], hbm_ref.at[r], sem.at[m])


def _put_all(pairs, sems, m):
    for (v, h), sem in zip(pairs, sems):
        _put(v, h, sem, m).start()


def _put_wait(pairs, sems, n):
    for (v, h), sem in zip(pairs, sems):
        for m in range(n):
            _put(v, h, sem, m).wait()


def _head_mean(s, left):
    sl = jnp.sum(jnp.where(left, s, 0.0), axis=-1, keepdims=True)
    sr = jnp.sum(jnp.where(left, 0.0, s), axis=-1, keepdims=True)
    return jnp.where(left, sl, sr) * (1.0 / HEAD_DIM)


def _rot(v, first):
    return jnp.where(first, pltpu.roll(v, 96, 1), pltpu.roll(v, 32, 1))


def _norm_rope(v, w, cos, sin_s, left, first):
    r = lax.rsqrt(_head_mean(v * v, left) + EPS)
    xh = v * r
    n = xh * w
    return n * cos + _rot(n, first) * sin_s, xh, r


def _norm_rope_bwd(d, xh, r, w, cos, sin_s, left, first):
    dn = d * cos - _rot(d, first) * sin_s
    dw = jnp.sum(dn * xh, axis=0, keepdims=True)
    dxh = dn * w
    return r * (dxh - xh * _head_mean(dxh * xh, left)), dw


def _dup_heads(v, left):
    sw = pltpu.roll(v, 64, 1)
    return jnp.where(left, v, sw), jnp.where(left, sw, v)


def _prep_kv(kv_ref, kw_ref, cos_ref, sin_ref, ka_ref, va_ref, t):
    ch = 256
    for g in range(2):
        ka_ref[g, 0:QBLK, :] = jnp.zeros((QBLK, 128), BF16)
        va_ref[g, 0:QBLK, :] = jnp.zeros((QBLK, 128), BF16)

    def chunk(i, carry):
        r0 = pl.multiple_of(i * ch, ch)
        left = _lane((ch, 128)) < 64
        first = (_lane((ch, 128)) % 64) < 32
        k = kv_ref[pl.ds(r0, ch), 0:128]
        v = kv_ref[pl.ds(r0, ch), 128:256]
        kr, _, _ = _norm_rope(k, kw_ref[...], cos_ref[pl.ds(r0, ch), :], sin_ref[pl.ds(r0, ch), :], left, first)
        k0, k1 = _dup_heads(kr, left)
        v0, v1 = _dup_heads(v, left)
        ka_ref[0, pl.ds(QBLK + r0, ch), :] = k0.astype(BF16)
        ka_ref[1, pl.ds(QBLK + r0, ch), :] = k1.astype(BF16)
        va_ref[0, pl.ds(QBLK + r0, ch), :] = v0.astype(BF16)
        va_ref[1, pl.ds(QBLK + r0, ch), :] = v1.astype(BF16)
        return carry

    lax.fori_loop(0, t // ch, chunk, 0)


def _band_mask(n):
    qi = lax.broadcasted_iota(jnp.int32, (2 * QBLK, 2 * QBLK), 0) % QBLK
    kj = lax.broadcasted_iota(jnp.int32, (2 * QBLK, 2 * QBLK), 1)
    local = (kj > qi) & (kj <= qi + QBLK)
    return local & ((n > 0) | (kj >= QBLK))


def _softmax_pair(s, mask, sink0, sink1):
    row = lax.broadcasted_iota(jnp.int32, (2 * QBLK, 1), 0)
    sink = jnp.where(row < QBLK, sink0, sink1)
    s = jnp.where(mask, s, -jnp.inf)
    m = jnp.maximum(jnp.max(s, axis=-1, keepdims=True), sink)
    e = jnp.exp(s - m)
    es = jnp.exp(sink - m)
    inv = 1.0 / (jnp.sum(e, axis=-1, keepdims=True) + es)
    return e * inv, es * inv


def _stack_heads(v, left):
    return jnp.concatenate([jnp.where(left, v, 0.0), jnp.where(left, 0.0, v)], axis=0)


def _attn_fwd(q_raw, kv_raw, ga, qw2, kw2, sinks, cos_f, sin_s, wo, cw):
    t = q_raw.shape[0]
    nblk = t // QBLK
    per_put = PUT_ROWS // QBLK

    def body(q_hbm, kv_ref, ga_hbm, qw_ref, kw_ref, sk_ref, cos_hbm, sin_hbm, wo_ref, cw_ref,
             o_hbm, mix_hbm, wo4_ref, cw4_ref, ka_ref, va_ref, q_ref, ga_ref, o_ref, mix_ref, cos_ref, sin_ref,
             isem, osem0, osem1, ssem, rsem):
        loads = _fetch((cos_hbm, sin_hbm, q_hbm, ga_hbm), (cos_ref, sin_ref, q_ref, ga_ref), isem)
        outs, osems = ((o_ref, o_hbm), (mix_ref, mix_hbm)), (osem0, osem1)
        x, y, c, chips = _place()
        j = 2 * x + y
        sib = (x, y, 1 - c)
        idx = [2 * cx + cy for cx, cy in chips]
        rc = functools.partial(_remote, ssem, rsem)
        wo4_ref[j] = wo_ref[...].astype(BF16)
        cw4_ref[j] = cw_ref[...]
        sends = []
        for k, chip in enumerate(chips):
            sends.append(rc(k, wo4_ref.at[j, c], wo4_ref.at[j, c], (*chip, c)))
            sends.append(rc(6 + k, cw4_ref.at[j], cw4_ref.at[j], (*chip, c)))
        for cp in sends:
            cp.start()

        loads[0].wait()
        loads[1].wait()
        _prep_kv(kv_ref, kw_ref, cos_ref, sin_ref, ka_ref, va_ref, t)
        loads[2].wait()
        loads[3].wait()

        def blk(n, carry):
            r0 = pl.multiple_of(n * QBLK, QBLK)
            left = _lane((QBLK, 128)) < 64
            first = (_lane((QBLK, 128)) % 64) < 32
            cos = cos_ref[pl.ds(r0, QBLK), :]
            sin = sin_ref[pl.ds(r0, QBLK), :]
            mask = _band_mask(n)
            for p in range(4):
                g = p // 2
                lanes = slice(p * 128, (p + 1) * 128)
                qr, _, _ = _norm_rope(q_ref[pl.ds(r0, QBLK), lanes], qw_ref[...], cos, sin, left, first)
                q2 = _stack_heads(qr * 0.125, left).astype(BF16)
                s = lax.dot_general(q2, ka_ref[g, pl.ds(r0, 2 * QBLK), :], (((1,), (1,)), ((), ())),
                                    preferred_element_type=F32)
                pm, _ = _softmax_pair(s, mask, sk_ref[0, 2 * p], sk_ref[0, 2 * p + 1])
                o2 = jnp.dot(pm.astype(BF16), va_ref[g, pl.ds(r0, 2 * QBLK), :], preferred_element_type=F32)
                o = jnp.where(left, o2[0:QBLK], o2[QBLK:2 * QBLK])
                o_ref[pl.ds(r0, QBLK), lanes] = o.astype(BF16)
                mix_ref[pl.ds(r0, QBLK), lanes] = (o * _silu(ga_ref[pl.ds(r0, QBLK), lanes])).astype(BF16)

            @pl.when(n % per_put == per_put - 1)
            def _():
                _put_all(outs, osems, n // per_put)

            return carry

        lax.fori_loop(0, nblk, blk, 0)
        _put_wait(outs, osems, t // PUT_ROWS)

        passed = []
        for k, chip in enumerate(chips):
            jk = idx[k]
            rc(k, wo4_ref.at[jk, c], wo4_ref.at[jk, c], sib).wait_recv()
            passed.append(rc(3 + k, wo4_ref.at[jk, c], wo4_ref.at[jk, c], sib))
            passed[-1].start()
        for k, chip in enumerate(chips):
            jk = idx[k]
            rc(3 + k, wo4_ref.at[jk, 1 - c], wo4_ref.at[jk, 1 - c], sib).wait_recv()
            rc(6 + k, cw4_ref.at[jk], cw4_ref.at[jk], sib).wait_recv()
        for cp in sends + passed:
            cp.wait_send()

    vm = pl.BlockSpec(memory_space=pltpu.VMEM)
    hbm = pl.BlockSpec(memory_space=pl.ANY)
    n_sem = 9
    return pl.pallas_call(
        body,
        name="attn_fwd",
        in_specs=[hbm, vm, hbm, vm, vm, pl.BlockSpec(memory_space=pltpu.SMEM), hbm, hbm, vm, vm],
        out_specs=[hbm, hbm, vm, vm],
        out_shape=[jax.ShapeDtypeStruct((t, ATTN_W), BF16), jax.ShapeDtypeStruct((t, ATTN_W), BF16),
                   jax.ShapeDtypeStruct((N_CHIPS, 2, OUT_HALF, D_MODEL), BF16),
                   jax.ShapeDtypeStruct((N_CHIPS, 32, 128), F32)],
        scratch_shapes=[pltpu.VMEM((2, t + QBLK, 128), BF16), pltpu.VMEM((2, t + QBLK, 128), BF16),
                        pltpu.VMEM((t, ATTN_W), F32), pltpu.VMEM((t, ATTN_W), F32),
                        pltpu.VMEM((t, ATTN_W), BF16), pltpu.VMEM((t, ATTN_W), BF16),
                        pltpu.VMEM((t, 128), F32), pltpu.VMEM((t, 128), F32),
                        pltpu.SemaphoreType.DMA((4,)), pltpu.SemaphoreType.DMA((t // PUT_ROWS,)),
                        pltpu.SemaphoreType.DMA((t // PUT_ROWS,)),
                        pltpu.SemaphoreType.DMA((n_sem,)), pltpu.SemaphoreType.DMA((n_sem,))],
        compiler_params=_cparams(),
    )(q_raw, kv_raw, ga, qw2, kw2, sinks, cos_f, sin_s, wo, cw)


def _attn_bwd(q_raw, kv_raw, ga, o, dmix, qw2, kw2, sinks, cos_f, sin_s, go):
    t = q_raw.shape[0]
    nblk = t // QBLK
    per_put = PUT_ROWS // QBLK

    def body(q_hbm, kv_ref, ga_hbm, o_hbm, dm_hbm, qw_ref, kw_ref, sk_ref, cos_hbm, sin_hbm, go_ref,
             dq_hbm, dkv_ref, dga_hbm, sm_ref, gwo_ref, ka_ref, va_ref, dka_ref, dva_ref,
             sibo_ref, outo_ref, ino_ref, q_ref, ga_ref, o_ref, dm_ref, dq_ref, dga_ref, cos_ref, sin_ref,
             isem, osem0, osem1, ssem, rsem):
        loads = _fetch((cos_hbm, sin_hbm, q_hbm, ga_hbm, o_hbm, dm_hbm), (cos_ref, sin_ref, q_ref, ga_ref, o_ref, dm_ref), isem)
        outs, osems = ((dq_ref, dq_hbm), (dga_ref, dga_hbm)), (osem0, osem1)
        x, y, c, chips = _place()
        sib = (x, y, 1 - c)
        rc = functools.partial(_remote, ssem, rsem)
        theirs, mine = go_ref.at[:, 1 - c], go_ref.at[:, c]
        sends = [_rs_to_sibling(rc, 0, theirs, sibo_ref, sib)]
        loads[0].wait()
        loads[1].wait()
        _prep_kv(kv_ref, kw_ref, cos_ref, sin_ref, ka_ref, va_ref, t)
        dka_ref[...] = jnp.zeros_like(dka_ref)
        dva_ref[...] = jnp.zeros_like(dva_ref)
        sends += _rs_trade(rc, 0, theirs, mine, sibo_ref, outo_ref, ino_ref, OUT_HALF, c, sib, chips)
        for cp in loads[2:]:
            cp.wait()

        def blk(n, carry):
            dqw, dsk = carry
            r0 = pl.multiple_of(n * QBLK, QBLK)
            left = _lane((QBLK, 128)) < 64
            first = (_lane((QBLK, 128)) % 64) < 32
            cos = cos_ref[pl.ds(r0, QBLK), :]
            sin = sin_ref[pl.ds(r0, QBLK), :]
            mask = _band_mask(n)
            row = lax.broadcasted_iota(jnp.int32, (2 * QBLK, 1), 0)
            for p in range(4):
                g = p // 2
                lanes = slice(p * 128, (p + 1) * 128)
                rows = pl.ds(r0, QBLK)
                win = pl.ds(r0, 2 * QBLK)
                qr, xh, r = _norm_rope(q_ref[rows, lanes], qw_ref[...], cos, sin, left, first)
                q2 = _stack_heads(qr * 0.125, left).astype(BF16)
                kwin = ka_ref[g, win, :]
                vwin = va_ref[g, win, :]
                s = lax.dot_general(q2, kwin, (((1,), (1,)), ((), ())), preferred_element_type=F32)
                pm, ps = _softmax_pair(s, mask, sk_ref[0, 2 * p], sk_ref[0, 2 * p + 1])
                gav = ga_ref[rows, lanes]
                dmv = dm_ref[rows, lanes].astype(F32)
                dga_ref[rows, lanes] = (dmv * o_ref[rows, lanes].astype(F32) * _dsilu(gav)).astype(BF16)
                do2 = _stack_heads(dmv * _silu(gav), left).astype(BF16)
                dp = lax.dot_general(do2, vwin, (((1,), (1,)), ((), ())), preferred_element_type=F32)
                delta = jnp.sum(pm * dp, axis=-1, keepdims=True)
                ds = (pm * (dp - delta)).astype(BF16)
                pd = ps * delta
                d0 = jnp.sum(jnp.where(row < QBLK, pd, 0.0), axis=0, keepdims=True)
                d1 = jnp.sum(jnp.where(row < QBLK, 0.0, pd), axis=0, keepdims=True)
                l8 = _lane((1, 128))
                dsk = dsk - jnp.where(l8 == 2 * p, d0, 0.0) - jnp.where(l8 == 2 * p + 1, d1, 0.0)
                dva_ref[g, win, :] += lax.dot_general(pm.astype(BF16), do2, (((0,), (0,)), ((), ())),
                                                      preferred_element_type=F32)
                dka_ref[g, win, :] += lax.dot_general(ds, q2, (((0,), (0,)), ((), ())),
                                                      preferred_element_type=F32)
                dq2 = jnp.dot(ds, kwin, preferred_element_type=F32)
                dqr = jnp.where(left, dq2[0:QBLK], dq2[QBLK:2 * QBLK]) * 0.125
                dq, dw = _norm_rope_bwd(dqr, xh, r, qw_ref[...], cos, sin, left, first)
                dq_ref[rows, lanes] = dq.astype(BF16)
                dqw = dqw + dw

            @pl.when(n % per_put == per_put - 1)
            def _():
                _put_all(outs, osems, n // per_put)

            return dqw, dsk

        zero = jnp.zeros((1, 128), F32)
        dqw, dsk = lax.fori_loop(0, nblk, blk, (zero, zero))

        ch = 256

        def chunk(i, dkw):
            r0 = pl.multiple_of(i * ch, ch)
            left = _lane((ch, 128)) < 64
            first = (_lane((ch, 128)) % 64) < 32
            rows = pl.ds(r0, ch)
            prow = pl.ds(QBLK + r0, ch)

            def fold(ref):
                a0 = ref[0, prow, :]
                a1 = ref[1, prow, :]
                return jnp.where(left, a0 + pltpu.roll(a0, 64, 1), a1 + pltpu.roll(a1, 64, 1))

            cos = cos_ref[rows, :]
            sin = sin_ref[rows, :]
            _, xh, r = _norm_rope(kv_ref[rows, 0:128], kw_ref[...], cos, sin, left, first)
            dk, dw = _norm_rope_bwd(fold(dka_ref), xh, r, kw_ref[...], cos, sin, left, first)
            dkv_ref[rows, 0:128] = dk.astype(BF16)
            dkv_ref[rows, 128:256] = fold(dva_ref).astype(BF16)
            return dkw + dw

        dkw = lax.fori_loop(0, t // ch, chunk, zero)
        sm_ref[...] = jnp.zeros((8, 128), F32)
        sm_ref[0:1, :] = dqw + pltpu.roll(dqw, 64, 1)
        sm_ref[1:2, :] = dkw + pltpu.roll(dkw, 64, 1)
        sm_ref[2:3, :] = dsk

        j = 2 * x + y
        sends.append(_rs_total(rc, 0, mine, sibo_ref, outo_ref, ino_ref, gwo_ref, OUT_HALF, j, c, sib))
        _rs_done(rc, 0, gwo_ref, c, sib)
        for cp in sends:
            cp.wait_send()
        _put_wait(outs, osems, t // PUT_ROWS)

    vm = pl.BlockSpec(memory_space=pltpu.VMEM)
    hbm = pl.BlockSpec(memory_space=pl.ANY)
    return pl.pallas_call(
        body,
        name="attn_bwd",
        in_specs=[hbm, vm, hbm, hbm, hbm, vm, vm, pl.BlockSpec(memory_space=pltpu.SMEM), hbm, hbm, vm],
        out_specs=[hbm, vm, hbm, vm, vm],
        out_shape=[jax.ShapeDtypeStruct((t, ATTN_W), BF16), jax.ShapeDtypeStruct((t, 2 * KV_W), BF16),
                   jax.ShapeDtypeStruct((t, ATTN_W), BF16), jax.ShapeDtypeStruct((8, 128), F32),
                   jax.ShapeDtypeStruct((2, OUT_HALF, D_MODEL), F32)],
        scratch_shapes=[pltpu.VMEM((2, t + QBLK, 128), BF16), pltpu.VMEM((2, t + QBLK, 128), BF16),
                        pltpu.VMEM((2, t + QBLK, 128), F32), pltpu.VMEM((2, t + QBLK, 128), F32)]
        + _rs_scratch(OUT_HALF)
        + [pltpu.VMEM((t, ATTN_W), F32), pltpu.VMEM((t, ATTN_W), F32), pltpu.VMEM((t, ATTN_W), BF16),
           pltpu.VMEM((t, ATTN_W), BF16), pltpu.VMEM((t, ATTN_W), BF16), pltpu.VMEM((t, ATTN_W), BF16),
           pltpu.VMEM((t, 128), F32), pltpu.VMEM((t, 128), F32),
           pltpu.SemaphoreType.DMA((6,)), pltpu.SemaphoreType.DMA((t // PUT_ROWS,)), pltpu.SemaphoreType.DMA((t // PUT_ROWS,)),
           pltpu.SemaphoreType.DMA((RS_SEMS,)), pltpu.SemaphoreType.DMA((RS_SEMS,))],
        compiler_params=_cparams(),
    )(q_raw, kv_raw, ga, o, dmix, qw2, kw2, sinks, cos_f, sin_s, go)


CONV_CH = 256
CONV_SUB = 64
CONV_ACCS = 3


def _shifted_windows(src_ref, r0, sh_ref):
    rows = CONV_CH + CONV_PAD
    win = src_ref[pl.ds(r0, rows), :]
    for b in range(8):
        sh = win if b == 0 else pltpu.roll(win, rows - b, 0)
        for c in range(CONV_W // 128):
            sh_ref[b, c] = sh[:, c * 128:(c + 1) * 128]


def _conv_fwd(ua, ug, gb, cw, cb, lw, lb):
    t = ua.shape[0]

    def body(ua_hbm, ug_hbm, gb_hbm, cw_ref, cb_ref, lw_ref, lb_ref, cz_hbm, mix_hbm, zp_ref, sh_ref,
             ua_ref, ug_ref, gb_ref, cz_ref, mix_ref, isem, osem0, osem1):
        loads = _fetch((ua_hbm, ug_hbm, gb_hbm), (ua_ref, ug_ref, gb_ref), isem)
        outs, osems = ((cz_ref, cz_hbm), (mix_ref, mix_hbm)), (osem0, osem1)
        per_put = PUT_ROWS // CONV_CH
        zp_ref[0:CONV_PAD, :] = jnp.zeros((CONV_PAD, CONV_W), F32)
        loads[0].wait()
        loads[1].wait()

        def glu(i, carry):
            r0 = pl.multiple_of(i * CONV_CH, CONV_CH)
            rows = pl.ds(r0, CONV_CH)
            zp_ref[pl.ds(CONV_PAD + r0, CONV_CH), :] = ua_ref[rows, :] * _sigmoid(ug_ref[rows, :])
            return carry

        lax.fori_loop(0, t // CONV_CH, glu, 0)
        loads[2].wait()

        def chunk(i, carry):
            r0 = pl.multiple_of(i * CONV_CH, CONV_CH)
            _shifted_windows(zp_ref, r0, sh_ref)
            for c in range(CONV_W // 128):
                lanes = slice(c * 128, (c + 1) * 128)

                def sub(k, carry2):
                    b0 = pl.multiple_of(k * CONV_SUB, CONV_SUB)
                    acc = [jnp.broadcast_to(cb_ref[0:1, lanes], (CONV_SUB, 128))] + [None] * (CONV_ACCS - 1)
                    for j in range(CONV_TAPS):
                        off = j + CONV_PAD - (CONV_TAPS - 1)
                        term = sh_ref[off % 8, c, pl.ds(b0 + 8 * (off // 8), CONV_SUB), :] * cw_ref[j:j + 1, lanes]
                        acc[j % CONV_ACCS] = term if acc[j % CONV_ACCS] is None else acc[j % CONV_ACCS] + term
                    cz_ref[pl.ds(r0 + b0, CONV_SUB), lanes] = functools.reduce(lambda a, b: a + b, acc)
                    return carry2

                lax.fori_loop(0, CONV_CH // CONV_SUB, sub, 0)
            rows = pl.ds(r0, CONV_CH)
            cz = cz_ref[rows, :]
            mu = jnp.mean(cz, axis=-1, keepdims=True)
            xc = cz - mu
            rs = lax.rsqrt(jnp.mean(xc * xc, axis=-1, keepdims=True) + EPS)
            ln = xc * rs * lw_ref[...] + lb_ref[...]
            mix_ref[rows, :] = (_silu(ln) * _silu(gb_ref[rows, :])).astype(BF16)

            @pl.when(i % per_put == per_put - 1)
            def _():
                _put_all(outs, osems, i // per_put)

            return carry

        lax.fori_loop(0, t // CONV_CH, chunk, 0)
        _put_wait(outs, osems, t // PUT_ROWS)

    vm = pl.BlockSpec(memory_space=pltpu.VMEM)
    hbm = pl.BlockSpec(memory_space=pl.ANY)
    nput = t // PUT_ROWS
    return pl.pallas_call(
        body,
        name="conv_fwd",
        in_specs=[hbm] * 3 + [vm] * 4,
        out_specs=[hbm, hbm],
        out_shape=[jax.ShapeDtypeStruct((t, CONV_W), F32), jax.ShapeDtypeStruct((t, CONV_W), BF16)],
        scratch_shapes=[pltpu.VMEM((t + CONV_PAD, CONV_W), F32),
                        pltpu.VMEM((8, CONV_W // 128, CONV_CH + CONV_PAD, 128), F32),
                        pltpu.VMEM((t, CONV_W), F32), pltpu.VMEM((t, CONV_W), F32), pltpu.VMEM((t, CONV_W), F32),
                        pltpu.VMEM((t, CONV_W), F32), pltpu.VMEM((t, CONV_W), BF16),
                        pltpu.SemaphoreType.DMA((3,)), pltpu.SemaphoreType.DMA((nput,)), pltpu.SemaphoreType.DMA((nput,))],
        compiler_params=_cparams(),
    )(ua, ug, gb, cw, cb, lw, lb)


def _conv_bwd(ua, ug, gb, cz, dmix, cw, lw, lb):
    t = ua.shape[0]

    def body(ua_hbm, ug_hbm, gb_hbm, cz_hbm, dm_hbm, cw_ref, lw_ref, lb_ref,
             dua_hbm, dug_hbm, dgb_hbm, dcw_ref, dvec_ref, zp_ref, dp_ref, sh_ref, wacc_ref,
             ua_ref, ug_ref, gb_ref, cz_ref, dm_ref, dua_ref, dug_ref, dgb_ref, isem, osem0, osem1, osem2):
        loads = _fetch((ua_hbm, ug_hbm, gb_hbm, cz_hbm, dm_hbm), (ua_ref, ug_ref, gb_ref, cz_ref, dm_ref), isem)
        per_put = PUT_ROWS // CONV_CH
        zp_ref[0:CONV_PAD, :] = jnp.zeros((CONV_PAD, CONV_W), F32)
        dp_ref[t:t + CONV_PAD, :] = jnp.zeros((CONV_PAD, CONV_W), F32)
        wacc_ref[...] = jnp.zeros_like(wacc_ref)
        for cp in loads:
            cp.wait()

        def pointwise(i, carry):
            dcb, dlw, dlb = carry
            r0 = pl.multiple_of(i * CONV_CH, CONV_CH)
            rows = pl.ds(r0, CONV_CH)
            zp_ref[pl.ds(CONV_PAD + r0, CONV_CH), :] = ua_ref[rows, :] * _sigmoid(ug_ref[rows, :])
            cz = cz_ref[rows, :]
            mu = jnp.mean(cz, axis=-1, keepdims=True)
            xc = cz - mu
            rs = lax.rsqrt(jnp.mean(xc * xc, axis=-1, keepdims=True) + EPS)
            xh = xc * rs
            ln = xh * lw_ref[...] + lb_ref[...]
            gbv = gb_ref[rows, :]
            dy = dm_ref[rows, :].astype(F32)
            dgb_ref[rows, :] = (dy * _silu(ln) * _dsilu(gbv)).astype(BF16)
            dl = dy * _silu(gbv) * _dsilu(ln)
            dxh = dl * lw_ref[...]
            dcz = rs * (dxh - jnp.mean(dxh, axis=-1, keepdims=True)
                        - xh * jnp.mean(dxh * xh, axis=-1, keepdims=True))
            dp_ref[rows, :] = dcz

            @pl.when(i % per_put == per_put - 1)
            def _():
                _put(dgb_ref, dgb_hbm, osem2, i // per_put).start()

            return (dcb + jnp.sum(dcz, axis=0, keepdims=True),
                    dlw + jnp.sum(dl * xh, axis=0, keepdims=True),
                    dlb + jnp.sum(dl, axis=0, keepdims=True))

        zero = jnp.zeros((1, CONV_W), F32)
        dcb, dlw, dlb = lax.fori_loop(0, t // CONV_CH, pointwise, (zero, zero, zero))
        dvec_ref[...] = jnp.zeros((8, CONV_W), F32)
        dvec_ref[0:1, :] = dcb
        dvec_ref[1:2, :] = dlw
        dvec_ref[2:3, :] = dlb

        def chunk(i, carry):
            r0 = pl.multiple_of(i * CONV_CH, CONV_CH)
            _shifted_windows(dp_ref, r0, sh_ref)
            for c in range(CONV_W // 128):
                lanes = slice(c * 128, (c + 1) * 128)

                def sub(k, carry2):
                    b0 = pl.multiple_of(k * CONV_SUB, CONV_SUB)
                    acc = [None] * CONV_ACCS
                    for j in range(CONV_TAPS):
                        off = CONV_TAPS - 1 - j
                        term = sh_ref[off % 8, c, pl.ds(b0 + 8 * (off // 8), CONV_SUB), :] * cw_ref[j:j + 1, lanes]
                        acc[j % CONV_ACCS] = term if acc[j % CONV_ACCS] is None else acc[j % CONV_ACCS] + term
                    acc = functools.reduce(lambda a, b: a + b, acc)
                    rr = pl.ds(r0 + b0, CONV_SUB)
                    sg = _sigmoid(ug_ref[rr, lanes])
                    dua_ref[rr, lanes] = (acc * sg).astype(BF16)
                    dug_ref[rr, lanes] = (acc * ua_ref[rr, lanes] * sg * (1.0 - sg)).astype(BF16)
                    return carry2

                lax.fori_loop(0, CONV_CH // CONV_SUB, sub, 0)
            _shifted_windows(zp_ref, r0, sh_ref)
            for c in range(CONV_W // 128):
                lanes = slice(c * 128, (c + 1) * 128)

                def subw(k, carry2):
                    b0 = pl.multiple_of(k * CONV_SUB, CONV_SUB)
                    dcz = dp_ref[pl.ds(r0 + b0, CONV_SUB), lanes]
                    for j in range(CONV_TAPS):
                        off = j + CONV_PAD - (CONV_TAPS - 1)
                        pr = dcz * sh_ref[off % 8, c, pl.ds(b0 + 8 * (off // 8), CONV_SUB), :]
                        parts = [pr[8 * q:8 * (q + 1)] for q in range(CONV_SUB // 8)]
                        while len(parts) > 1:
                            parts = [a + b for a, b in zip(parts[0::2], parts[1::2])]
                        wacc_ref[8 * j:8 * (j + 1), lanes] += parts[0]
                    return carry2

                lax.fori_loop(0, CONV_CH // CONV_SUB, subw, 0)

            @pl.when(i % per_put == per_put - 1)
            def _():
                _put_all(((dua_ref, dua_hbm), (dug_ref, dug_hbm)), (osem0, osem1), i // per_put)

            return carry

        lax.fori_loop(0, t // CONV_CH, chunk, 0)
        _put_wait(((dua_ref, dua_hbm), (dug_ref, dug_hbm), (dgb_ref, dgb_hbm)), (osem0, osem1, osem2), t // PUT_ROWS)
        dcw_ref[...] = jnp.zeros((16, 2 * CONV_W), F32)
        for j in range(CONV_TAPS):
            dcw_ref[j // 2:j // 2 + 1, CONV_W * (j % 2):CONV_W * (j % 2 + 1)] = jnp.sum(
                wacc_ref[8 * j:8 * (j + 1), :], axis=0, keepdims=True)

    vm = pl.BlockSpec(memory_space=pltpu.VMEM)
    hbm = pl.BlockSpec(memory_space=pl.ANY)
    return pl.pallas_call(
        body,
        name="conv_bwd",
        in_specs=[hbm] * 5 + [vm] * 3,
        out_specs=[hbm] * 3 + [vm] * 2,
        out_shape=[jax.ShapeDtypeStruct((t, CONV_W), BF16)] * 3
        + [jax.ShapeDtypeStruct((16, 2 * CONV_W), F32), jax.ShapeDtypeStruct((8, CONV_W), F32)],
        scratch_shapes=[pltpu.VMEM((t + CONV_PAD, CONV_W), F32), pltpu.VMEM((t + CONV_PAD, CONV_W), F32),
                        pltpu.VMEM((8, CONV_W // 128, CONV_CH + CONV_PAD, 128), F32), pltpu.VMEM((8 * 32, CONV_W), F32)]
        + [pltpu.VMEM((t, CONV_W), F32)] * 4 + [pltpu.VMEM((t, CONV_W), BF16)] * 4
        + [pltpu.SemaphoreType.DMA((5,))] + [pltpu.SemaphoreType.DMA((t // PUT_ROWS,))] * 3,
        compiler_params=_cparams(),
    )(ua, ug, gb, cz, dmix, cw, lw, lb)


def _out_proj(mix_a, mix_b, x, tgt, gate, w_out):
    t = x.shape[0]
    tm = 512
    nstep = t // tm

    def body(ma_ref, mb_ref, x_ref, t_ref, g_ref, w_ref, dout_ref, dma_ref, dmb_ref, gw_ref, red_ref, acc_ref):
        i = pl.program_id(0)

        @pl.when(i == 0)
        def _():
            acc_ref[...] = jnp.zeros_like(acc_ref)
            red_ref[...] = jnp.zeros_like(red_ref)

        mix = jnp.concatenate([ma_ref[...], mb_ref[...]], axis=1)
        y = jnp.dot(mix, w_ref[...], preferred_element_type=F32)
        gate_v = g_ref[...]
        err = x_ref[...] + gate_v * y - t_ref[...]
        dout = err * (1.0 / D_MODEL)
        dout_ref[...] = dout
        red_ref[0:1, :] += jnp.sum(dout * y, axis=0, keepdims=True)
        red_ref[1:2, :] += jnp.sum(err * err, axis=0, keepdims=True)
        dy = (dout * gate_v).astype(BF16)
        dmix = lax.dot_general(dy, w_ref[...], (((1,), (1,)), ((), ())), preferred_element_type=F32)
        dma_ref[...] = dmix[:, 0:512].astype(BF16)
        dmb_ref[...] = dmix[:, 512:1024].astype(BF16)
        acc_ref[...] += lax.dot_general(mix, dy, (((0,), (0,)), ((), ())), preferred_element_type=F32)

        @pl.when(i == nstep - 1)
        def _():
            gw_ref[...] = acc_ref[...].astype(BF16)

    row = lambda w: pl.BlockSpec((tm, w), lambda i: (i, 0))
    const = lambda s: pl.BlockSpec(s, lambda i: (0, 0))
    return pl.pallas_call(
        body,
        name="out_proj",
        grid=(nstep,),
        in_specs=[row(512), row(512), row(D_MODEL), row(D_MODEL), const((1, D_MODEL)),
                  pl.BlockSpec((D_MODEL, D_MODEL), lambda i: (0, 0), pipeline_mode=pl.Buffered(1))],
        out_specs=[row(D_MODEL), row(512), row(512), const((D_MODEL, D_MODEL)), const((8, D_MODEL))],
        out_shape=[jax.ShapeDtypeStruct((t, D_MODEL), F32), jax.ShapeDtypeStruct((t, 512), BF16),
                   jax.ShapeDtypeStruct((t, 512), BF16), jax.ShapeDtypeStruct((D_MODEL, D_MODEL), BF16),
                   jax.ShapeDtypeStruct((8, D_MODEL), F32)],
        scratch_shapes=[pltpu.VMEM((D_MODEL, D_MODEL), F32)],
        compiler_params=_cparams(dimension_semantics=("arbitrary",)),
    )(mix_a, mix_b, x, tgt, gate, w_out)


DPROJ_WIDTHS = (512, 256, 512, 512, 512, 512)
DPROJ_STARTS = (0, 512, 768, 1280, 1792, 2304)
WIN_W = 768
WIN_START = (0, 640, 1408, 2048)
WIN_OFF = (0, 64, 0, 64)
N_GW = N_CHIPS


def _window_pieces(s):
    lo, hi = WIN_START[s], WIN_START[s] + WIN_W
    out = []
    for p, (st, w) in enumerate(zip(DPROJ_STARTS, DPROJ_WIDTHS)):
        a, b = max(lo, st), min(hi, st + w)
        if a < b:
            out.append((p, a - st, b - a, a - lo))
    return out


def _in_proj_bwd(dparts, h, x, dout, s1, nw, wt_full, dcw, dvec, sm_a, row0):
    t = x.shape[0]
    tm = 256
    nstep = N_GW + t // tm
    n_sem = 20
    rows0 = 32
    hs = rows0 // 2
    npart = len(DPROJ_WIDTHS)

    def body(*refs):
        d_hbm, d_ref = refs[:npart], refs[npart:2 * npart]
        (x_ref, dout_ref, s1_ref, nw_ref, h_ref, wt_hbm, dcw_ref, dvec_ref, sma_ref, row0_ref,
         gx_ref, gw_hbm, ssum_ref, rows_ref,
         stg_ref, wt_ref, gt_ref, sib_ref, out_ref, in_ref, res_ref, sall_ref, red_ref, sm0_ref, ssib_ref, schip_ref, sres_ref,
         wsem, lsem, ssem, rsem) = refs[2 * npart:]
        i = pl.program_id(0)
        x_, y_, c, chips = _place()
        j = 2 * x_ + y_
        dev = 2 * j + c
        sib = (x_, y_, 1 - c)
        rc = functools.partial(_remote, ssem, rsem)
        rel_chip = [2 * cx + cy for cx, cy in chips] + [j]
        peers = [(px, py, pc) for px in (x_, 1 - x_) for py in (y_, 1 - y_) for pc in (c, 1 - c)][1:]
        wt_copy = pltpu.make_async_copy(wt_hbm, wt_ref, lsem.at[0])

        def window(case, slot):
            return [pltpu.make_async_copy(d_hbm[p].at[:, pl.ds(c0, w)], stg_ref.at[slot, :, pl.ds(w0, w)], wsem.at[slot, n])
                    for n, (p, c0, w, w0) in enumerate(_window_pieces(case))]

        def to_sibling(k):
            return rc(k, gt_ref.at[k, 1 - c], sib_ref.at[k], sib)

        def to_chip(k):
            return rc(4 + k, out_ref.at[k], in_ref.at[k], (*chips[k], c))

        def trade(k):
            to_sibling(k).wait_recv()

            def add(n, carry):
                rr = pl.ds(pl.multiple_of(n * RS_CH, RS_CH), RS_CH)
                out_ref[k, rr, :] = (gt_ref[k, c, rr, :].astype(F32) + sib_ref[k, rr, :].astype(F32)).astype(BF16)
                return carry

            lax.fori_loop(0, IN_HALF // RS_CH, add, 0)
            to_chip(k).start()

        mine_s = pl.ds(pl.multiple_of(c * hs, 8), hs)
        other_s = pl.ds(pl.multiple_of((1 - c) * hs, 8), hs)

        def small_to_sibling():
            return rc(15, sm0_ref.at[other_s], ssib_ref, sib)

        def small_to_chip(k):
            return rc(16 + k, schip_ref.at[j], schip_ref.at[j], (*chips[k], c))

        def small_share():
            return rc(19, sres_ref.at[c], sres_ref.at[c], sib)

        for k in range(N_GW):
            @pl.when(i == k)
            def _(k=k):
                slot = k % 2
                if k == 0:
                    red_ref[...] = jnp.zeros_like(red_ref)
                    wt_copy.start()
                    sm0_ref[...] = jnp.zeros_like(sm0_ref)
                    sm0_ref[0:16, :] = dcw_ref[...]
                    sm0_ref[16:17, 0:CONV_W] = dvec_ref[0:1, :]
                    sm0_ref[16:17, CONV_W:2 * CONV_W] = dvec_ref[1:2, :]
                    sm0_ref[17:18, 0:CONV_W] = dvec_ref[2:3, :]
                    for r in range(3):
                        sm0_ref[17:18, CONV_W + 128 * r:CONV_W + 128 * (r + 1)] = sma_ref[r:r + 1, :]
                    sm0_ref[18:19, :] = row0_ref[1:2, :]
                    small_to_sibling().start()
                if k == 1:
                    small_to_sibling().wait_recv()
                    schip_ref[j] = sm0_ref[mine_s, :] + ssib_ref[...]
                    for kk in range(3):
                        small_to_chip(kk).start()
                if k == N_GW - 1:
                    for kk in range(3):
                        jk = rel_chip[kk]
                        rc(16 + kk, schip_ref.at[jk], schip_ref.at[jk], sib).wait_recv()
                    tot = schip_ref[0]
                    for d in range(1, N_CHIPS):
                        tot = tot + schip_ref[d]
                    sres_ref[c] = tot
                    small_share().start()
                for case in range(N_CHIPS):
                    if k == 0:
                        @pl.when(rel_chip[0] == case)
                        def _():
                            for cp in window(case, 0):
                                cp.start()
                    if k + 1 < N_GW:
                        @pl.when(rel_chip[k + 1] == case)
                        def _():
                            for cp in window(case, 1 - slot):
                                cp.start()
                for case in range(N_CHIPS):
                    @pl.when(rel_chip[k] == case)
                    def _():
                        for cp in window(case, slot):
                            cp.wait()
                g = lax.dot_general(stg_ref[slot], h_ref[...], (((0,), (0,)), ((), ())), preferred_element_type=F32)
                for off in sorted(set(WIN_OFF)):
                    @pl.when(rel_chip[k] % 2 == (1 if off else 0))
                    def _():
                        gt_ref[k, 0] = g[off:off + IN_HALF].astype(BF16)
                        gt_ref[k, 1] = g[off + IN_HALF:off + 2 * IN_HALF].astype(BF16)
                to_sibling(k).start()
                if k >= 1:
                    trade(k - 1)

        @pl.when(i == N_GW)
        def _():
            wt_copy.wait()

        @pl.when(i >= N_GW)
        def _():
            xv = x_ref[...]
            r = lax.rsqrt(jnp.mean(xv * xv, axis=-1, keepdims=True) + EPS)
            xh = xv * r
            n = xh * nw_ref[...]
            dproj = jnp.concatenate([ref[...] for ref in d_ref], axis=1)
            dh = jnp.dot(dproj, wt_ref[...], preferred_element_type=F32)
            red_ref[0:1, :] += jnp.sum(dh, axis=0, keepdims=True)
            red_ref[1:2, :] += jnp.sum(dh * n, axis=0, keepdims=True)
            dn = dh * s1_ref[...]
            red_ref[2:3, :] += jnp.sum(dn * xh, axis=0, keepdims=True)
            dxh = dn * nw_ref[...]
            gx_ref[...] = dout_ref[...] + r * (dxh - xh * jnp.mean(dxh * xh, axis=-1, keepdims=True))

        @pl.when(i == nstep - 1)
        def _():
            sall_ref[dev] = row0_ref[...]
            sall_ref[dev, 2:5, :] = red_ref[0:3, :]
            sends = [rc(8 + k, sall_ref.at[dev], sall_ref.at[dev], peer) for k, peer in enumerate(peers)]
            for cp in sends:
                cp.start()
            sends += [to_sibling(k) for k in range(N_GW)] + [to_chip(k) for k in range(3)]
            sends += [small_to_sibling(), small_share()] + [small_to_chip(k) for k in range(3)]
            own = N_GW - 1
            to_sibling(own).wait_recv()
            for k in range(3):
                to_chip(k).wait_recv()

            def total(n, carry):
                rr = pl.ds(pl.multiple_of(n * RS_CH, RS_CH), RS_CH)
                acc = gt_ref[own, c, rr, :].astype(F32) + sib_ref[own, rr, :].astype(F32)
                for k in range(3):
                    acc = acc + in_ref[k, rr, :].astype(F32)
                res_ref[c, rr, :] = acc
                return carry

            lax.fori_loop(0, IN_HALF // RS_CH, total, 0)
            share = rc(7, res_ref.at[c], res_ref.at[c], sib)
            share.start()
            sends.append(share)
            for k, (px, py, pc) in enumerate(peers):
                pdev = 4 * px + 2 * py + pc
                rc(8 + k, sall_ref.at[pdev], sall_ref.at[pdev], (px, py, pc)).wait_recv()
            rows_ref[...] = sall_ref[...]
            rc(19, sres_ref.at[1 - c], sres_ref.at[1 - c], sib).wait_recv()
            ssum_ref[0:hs, :] = sres_ref[0]
            ssum_ref[hs:rows0, :] = sres_ref[1]
            rc(7, res_ref.at[1 - c], res_ref.at[1 - c], sib).wait_recv()
            back = pltpu.make_async_copy(res_ref, gw_hbm, lsem.at[1])
            back.start()
            for cp in sends:
                cp.wait_send()
            back.wait()

    blk = lambda i: jnp.maximum(i - N_GW, 0)
    row = lambda w: pl.BlockSpec((tm, w), lambda i: (blk(i), 0))
    vec = pl.BlockSpec((1, D_MODEL), lambda i: (0, 0))
    const = lambda shape: pl.BlockSpec(shape, lambda i: (0,) * len(shape))
    hbm = pl.BlockSpec(memory_space=pl.ANY)
    return pl.pallas_call(
        body,
        name="in_proj_bwd",
        grid=(nstep,),
        in_specs=[hbm] * npart + [row(w) for w in DPROJ_WIDTHS] + [row(D_MODEL), row(D_MODEL), vec, vec,
                  pl.BlockSpec((t, D_MODEL), lambda i: (0, 0), pipeline_mode=pl.Buffered(1)), hbm, const((16, D_MODEL)),
                  const((8, CONV_W)), const((8, 128)), const((8, D_MODEL))],
        out_specs=[row(D_MODEL), hbm, const((rows0, D_MODEL)), const((N_DEV, 8, D_MODEL))],
        out_shape=[jax.ShapeDtypeStruct((t, D_MODEL), F32), jax.ShapeDtypeStruct((2, IN_HALF, D_MODEL), F32),
                   jax.ShapeDtypeStruct((rows0, D_MODEL), F32), jax.ShapeDtypeStruct((N_DEV, 8, D_MODEL), F32)],
        scratch_shapes=[pltpu.VMEM((2, t, WIN_W), BF16), pltpu.VMEM((IN_W, D_MODEL), BF16),
                        pltpu.VMEM((N_CHIPS, 2, IN_HALF, D_MODEL), BF16), pltpu.VMEM((N_CHIPS, IN_HALF, D_MODEL), BF16),
                        pltpu.VMEM((3, IN_HALF, D_MODEL), BF16), pltpu.VMEM((3, IN_HALF, D_MODEL), BF16),
                        pltpu.VMEM((2, IN_HALF, D_MODEL), F32), pltpu.VMEM((N_DEV, 8, D_MODEL), F32),
                        pltpu.VMEM((8, D_MODEL), F32), pltpu.VMEM((rows0, D_MODEL), F32), pltpu.VMEM((hs, D_MODEL), F32),
                        pltpu.VMEM((N_CHIPS, hs, D_MODEL), F32),
                        pltpu.VMEM((2, hs, D_MODEL), F32), pltpu.SemaphoreType.DMA((2, 3)), pltpu.SemaphoreType.DMA((2,)),
                        pltpu.SemaphoreType.DMA((n_sem,)), pltpu.SemaphoreType.DMA((n_sem,))],
        compiler_params=_cparams(dimension_semantics=("arbitrary",)),
    )(*dparts, *dparts, x, dout, s1, nw, h, wt_full, dcw, dvec, sm_a, row0)


MESH = pl.DeviceIdType.MESH


def _place():
    x, y, c = lax.axis_index("x"), lax.axis_index("y"), lax.axis_index("c")
    chips = [(1 - x, y), (x, 1 - y), (1 - x, 1 - y)]
    return x, y, c, chips


def _remote(sems_s, sems_r, k, src, dst, to):
    return pltpu.make_async_remote_copy(src_ref=src, dst_ref=dst, send_sem=sems_s.at[k], recv_sem=sems_r.at[k],
                                        device_id=to, device_id_type=MESH)


RS_CH = 32
RS_SEMS = 5


def _rs_to_sibling(rc, s0, theirs, sib_ref, sib):
    cp = rc(s0, theirs, sib_ref, sib)
    cp.start()
    return cp


def _rs_trade(rc, s0, theirs, mine, sib_ref, out_ref, in_ref, rows, c, sib, chips):
    rc(s0, theirs, sib_ref, sib).wait_recv()
    cps = []
    for k, (cx, cy) in enumerate(chips):
        jk = 2 * cx + cy

        def add(i, carry, jk=jk, k=k):
            rr = pl.ds(pl.multiple_of(i * RS_CH, RS_CH), RS_CH)
            out_ref[k, rr, :] = (mine[jk, rr, :].astype(F32) + sib_ref[jk, rr, :].astype(F32)).astype(BF16)
            return carry

        lax.fori_loop(0, rows // RS_CH, add, 0)
        cps.append(rc(s0 + 1 + k, out_ref.at[k], in_ref.at[k], (cx, cy, c)))
        cps[-1].start()
    return cps


def _rs_total(rc, s0, mine, sib_ref, out_ref, in_ref, res_ref, rows, j, c, sib):
    for k in range(3):
        rc(s0 + 1 + k, out_ref.at[k], in_ref.at[k], sib).wait_recv()

    def total(i, carry):
        rr = pl.ds(pl.multiple_of(i * RS_CH, RS_CH), RS_CH)
        acc = mine[j, rr, :].astype(F32) + sib_ref[j, rr, :].astype(F32)
        for k in range(3):
            acc = acc + in_ref[k, rr, :].astype(F32)
        res_ref[c, rr, :] = acc
        return carry

    lax.fori_loop(0, rows // RS_CH, total, 0)
    cp = rc(s0 + 4, res_ref.at[c], res_ref.at[c], sib)
    cp.start()
    return cp


def _rs_done(rc, s0, res_ref, c, sib):
    rc(s0 + 4, res_ref.at[1 - c], res_ref.at[1 - c], sib).wait_recv()


def _rs_scratch(rows):
    return [pltpu.VMEM((N_CHIPS, rows, D_MODEL), BF16), pltpu.VMEM((3, rows, D_MODEL), BF16),
            pltpu.VMEM((3, rows, D_MODEL), BF16)]


MAIN_W = 640
MAIN_DST = (((0, 0, 512), (1, 0, 128)), ((2, 0, 512), (3, 0, 128)), ((3, 128, 384), (4, 0, 256)), ((4, 384, 128), (5, 0, 512)))
PAIR_DST = ((1, 128, 128), (4, 256, 128))


def _in_proj_gather(x, wt, c_row, w_ada, b_ada, nw):
    t = x.shape[0]
    ch = 512
    n_sem = 16

    def body(x_hbm, wt_ref, c_ref, wada_ref, bada_ref, nw_ref,
             q_hbm, kv_hbm, ga_hbm, ua_hbm, ug_hbm, gb_hbm, h_hbm, w4_hbm, call_ref, ada_ref,
             x_ref, h_ref, w4_ref, stg_ref, pstg_ref, part_ref, lsem, osem, wsem, ssem, rsem):
        outs = (q_hbm, kv_hbm, ga_hbm, ua_hbm, ug_hbm, gb_hbm)
        x_, y_, c, chips = _place()
        j = 2 * x_ + y_
        dev = 2 * j + c
        sib = (x_, y_, 1 - c)
        idx = [2 * cx + cy for cx, cy in chips]
        rc = functools.partial(_remote, ssem, rsem)
        x_copy = pltpu.make_async_copy(x_hbm, x_ref, lsem.at[0])
        x_copy.start()

        def rows_of(s, cc):
            return pl.ds(pl.multiple_of(2 * IN_HALF * s + IN_HALF * cc, 16), IN_HALF)

        w4_ref[rows_of(j, 0), :] = wt_ref[0].astype(BF16)
        w4_ref[rows_of(j, 1), :] = wt_ref[1].astype(BF16)
        call_ref[dev] = c_ref[...]
        sends = []
        peers = [(px, py, pc) for px in (x_, 1 - x_) for py in (y_, 1 - y_) for pc in (c, 1 - c)][1:]
        for k, peer in enumerate(peers):
            sends.append(rc(k, call_ref.at[dev], call_ref.at[dev], peer))
        for cp in sends:
            cp.start()

        for k, (px, py, pc) in enumerate(peers):
            pdev = 4 * px + 2 * py + pc
            rc(k, call_ref.at[pdev], call_ref.at[pdev], (px, py, pc)).wait_recv()
        rowid = lax.broadcasted_iota(jnp.int32, (N_DEV, D_MODEL), 0)
        call = jnp.zeros((N_DEV, D_MODEL), F32)
        for r in range(N_DEV):
            call = jnp.where(rowid == r, jnp.broadcast_to(call_ref[r], (N_DEV, D_MODEL)), call)
        bsh = bada_ref[:, 0:ADA_SHARD]
        for k in range(1, N_CHIPS):
            bsh = jnp.where(j == k, bada_ref[:, ADA_SHARD * k:ADA_SHARD * (k + 1)], bsh)
        part = jnp.dot(_silu(call).astype(BF16), wada_ref[...].astype(BF16), preferred_element_type=F32) + bsh
        for r in range(N_DEV):
            part_ref[r] = part[r:r + 1, :]
        ada_ref[j] = part_ref[dev]
        for k, chip in enumerate(chips):
            sends.append(rc(13 + k, part_ref.at[2 * idx[k] + c], ada_ref.at[j], (*chip, c)))
            sends[-1].start()
        for k, chip in enumerate(chips):
            sends.append(rc(7 + k, w4_ref.at[rows_of(j, c)], w4_ref.at[rows_of(j, c)], (*chip, c)))
            sends[-1].start()
        for k in range(3):
            rc(13 + k, ada_ref.at[idx[k]], ada_ref.at[idx[k]], sib).wait_recv()

        shift = jnp.concatenate([ada_ref[0], ada_ref[1][:, 0:256]], axis=1)
        s1 = 1.0 + jnp.concatenate([ada_ref[1][:, 256:768], ada_ref[2][:, 0:512]], axis=1)
        x_copy.wait()

        def norm(i, carry):
            rr = pl.ds(pl.multiple_of(i * ch, ch), ch)
            xv = x_ref[rr, :]
            r = lax.rsqrt(jnp.mean(xv * xv, axis=-1, keepdims=True) + EPS)
            h_ref[rr, :] = ((xv * r) * nw_ref[...] * s1 + shift).astype(BF16)
            return carry

        lax.fori_loop(0, t // ch, norm, 0)
        h_copy = pltpu.make_async_copy(h_ref, h_hbm, lsem.at[1])
        h_copy.start()

        def put_main(case, slot):
            cps, col = [], 0
            for n, (a, c0, w) in enumerate(MAIN_DST[case]):
                cps.append(pltpu.make_async_copy(stg_ref.at[slot, :, pl.ds(col, w)], outs[a].at[:, pl.ds(c0, w)], osem.at[slot, n]))
                col += w
            return cps

        def put_pair(case, slot):
            a, c0, w = PAIR_DST[case]
            return pltpu.make_async_copy(pstg_ref.at[slot], outs[a].at[:, pl.ds(c0, w)], osem.at[slot, 2])

        def project(first_row, width, dst, slot):
            wrows = pl.ds(pl.multiple_of(first_row, 128), width)

            def blk(i, carry):
                rr = pl.ds(pl.multiple_of(i * ch, ch), ch)
                dst[slot, rr, :] = lax.dot_general(h_ref[rr, :], w4_ref[wrows, :], (((1,), (1,)), ((), ())),
                                                   preferred_element_type=F32)
                return carry

            lax.fori_loop(0, t // ch, blk, 0)

        def phase(p, s, pair):
            slot = p % 2
            if p >= 2:
                for case in range(N_CHIPS):
                    @pl.when(order[p - 2] == case)
                    def _():
                        for cp in put_main(case, slot):
                            cp.wait()
            if p == 3:
                for case in range(2):
                    @pl.when(j // 2 == case)
                    def _():
                        put_pair(case, 0).wait()
            project(2 * IN_HALF * s + 64 * (s % 2), MAIN_W, stg_ref, slot)
            for case in range(N_CHIPS):
                @pl.when(s == case)
                def _():
                    for cp in put_main(case, slot):
                        cp.start()
            if pair is not None:
                project(MAIN_W + 2 * (2 * IN_HALF) * pair, 128, pstg_ref, slot % 2 if p == 2 else 1)
                for case in range(2):
                    @pl.when(pair == case)
                    def _():
                        put_pair(case, 0 if p == 2 else 1).start()

        order = [j] + idx
        w_out = [pltpu.make_async_copy(w4_ref.at[pl.ds(pl.multiple_of(2 * IN_HALF * s, 32), 2 * IN_HALF)],
                                       w4_hbm.at[pl.ds(pl.multiple_of(2 * IN_HALF * s, 32), 2 * IN_HALF)], wsem.at[p])
                 for p, s in enumerate(order)]
        w_out[0].start()
        phase(0, j, None)
        passed = []
        for k in range(3):
            jk = idx[k]
            rc(7 + k, w4_ref.at[rows_of(jk, c)], w4_ref.at[rows_of(jk, c)], sib).wait_recv()
            passed.append(rc(10 + k, w4_ref.at[rows_of(jk, c)], w4_ref.at[rows_of(jk, c)], sib))
            passed[-1].start()
            rc(10 + k, w4_ref.at[rows_of(jk, 1 - c)], w4_ref.at[rows_of(jk, 1 - c)], sib).wait_recv()
            w_out[1 + k].start()
            if k == 0:
                phase(1, jk, None)
            elif k == 1:
                phase(2, jk, j // 2)
            else:
                phase(3, jk, 1 - j // 2)

        for case in range(N_CHIPS):
            for p in (2, 3):
                @pl.when(order[p] == case)
                def _():
                    for cp in put_main(case, p % 2):
                        cp.wait()
        for case in range(2):
            @pl.when(1 - j // 2 == case)
            def _():
                put_pair(case, 1).wait()
        h_copy.wait()
        for cp in w_out:
            cp.wait()
        for cp in sends + passed:
            cp.wait_send()

    vm = pl.BlockSpec(memory_space=pltpu.VMEM)
    hbm = pl.BlockSpec(memory_space=pl.ANY)
    widths = (512, 256, 512, 512, 512, 512)
    return pl.pallas_call(
        body,
        name="in_proj",
        in_specs=[hbm, vm, vm, vm, vm, vm],
        out_specs=[hbm] * 8 + [vm, vm],
        out_shape=[jax.ShapeDtypeStruct((t, w), F32) for w in widths]
        + [jax.ShapeDtypeStruct((t, D_MODEL), BF16), jax.ShapeDtypeStruct((IN_W, D_MODEL), BF16),
           jax.ShapeDtypeStruct((N_DEV, 1, D_MODEL), F32), jax.ShapeDtypeStruct((N_CHIPS, 1, ADA_SHARD), F32)],
        scratch_shapes=[pltpu.VMEM((t, D_MODEL), F32), pltpu.VMEM((t, D_MODEL), BF16), pltpu.VMEM((IN_W, D_MODEL), BF16),
                        pltpu.VMEM((2, t, MAIN_W), F32), pltpu.VMEM((2, t, 128), F32), pltpu.VMEM((N_DEV, 1, ADA_SHARD), F32),
                        pltpu.SemaphoreType.DMA((2,)), pltpu.SemaphoreType.DMA((2, 3)), pltpu.SemaphoreType.DMA((N_CHIPS,)),
                        pltpu.SemaphoreType.DMA((n_sem,)), pltpu.SemaphoreType.DMA((n_sem,))],
        compiler_params=_cparams(),
    )(x, wt, c_row, w_ada, b_ada, nw)


def _adamw_math(w, g, m, v):
    m2 = ADAM_B1 * m + (1.0 - ADAM_B1) * g
    v2 = ADAM_B2 * v + (1.0 - ADAM_B2) * (g * g)
    m_hat = m2 / (1.0 - ADAM_B1 ** ADAM_STEP)
    v_hat = v2 / (1.0 - ADAM_B2 ** ADAM_STEP)
    delta = -ADAM_LR * (m_hat / (jnp.sqrt(v_hat) + ADAM_EPS) + ADAM_WD * w)
    return delta, m2, v2


def _adamw(name, w, g, m, v, tm, through=None):
    r, cdim = w.shape
    nstep = r // tm
    extra = [] if through is None else [through]

    def body(w_ref, g_ref, m_ref, v_ref, *rest):
        g2_ref, d_ref, m2_ref, v2_ref = rest[len(extra):len(extra) + 4]
        g = g_ref[...]
        g2_ref[...] = g
        d_ref[...], m2_ref[...], v2_ref[...] = _adamw_math(w_ref[...], g, m_ref[...], v_ref[...])
        if extra:
            rest[-1][...] = rest[0][...]

    blk = pl.BlockSpec((tm, cdim), lambda i: (i, 0))
    eblk = [pl.BlockSpec((e.shape[0] // nstep, e.shape[1]), lambda i: (i, 0)) for e in extra]
    return pl.pallas_call(
        body,
        name=name,
        grid=(nstep,),
        in_specs=[blk] * 4 + eblk,
        out_specs=[blk] * 4 + eblk,
        out_shape=[jax.ShapeDtypeStruct((r, cdim), F32)] * 4 + [jax.ShapeDtypeStruct(e.shape, e.dtype) for e in extra],
        compiler_params=_cparams(dimension_semantics=("arbitrary",)),
    )(w, g, m, v, *extra)


def _adamw_ada(w, m, v, cact_t, dcols):
    r, cdim = w.shape
    tm = 256

    def body(w_ref, m_ref, v_ref, ct_ref, dc_ref, g_ref, d_ref, m2_ref, v2_ref):
        g = jnp.dot(ct_ref[...], dc_ref[...], preferred_element_type=F32, precision=lax.Precision.HIGHEST)
        g_ref[...] = g
        d_ref[...], m2_ref[...], v2_ref[...] = _adamw_math(w_ref[...], g, m_ref[...], v_ref[...])

    blk = pl.BlockSpec((tm, cdim), lambda i: (i, 0))
    return pl.pallas_call(
        body,
        name="adamw_w_ada",
        grid=(r // tm,),
        in_specs=[blk] * 3 + [pl.BlockSpec((tm, N_DEV), lambda i: (i, 0)), pl.BlockSpec((N_DEV, cdim), lambda i: (0, 0))],
        out_specs=[blk] * 4,
        out_shape=[jax.ShapeDtypeStruct((r, cdim), F32)] * 4,
        compiler_params=_cparams(dimension_semantics=("arbitrary",)),
    )(w, m, v, cact_t, dcols)


def _adamw_small(ws, ms, vs, ssum, rows):
    n = len(ws)

    def body(*refs):
        w_r, m_r, v_r = refs[0:n], refs[n:2 * n], refs[2 * n:3 * n]
        ss_ref, rows_ref = refs[3 * n], refs[3 * n + 1]
        g_r, d_r, m2_r, v2_r = (refs[3 * n + 2 + k * n:3 * n + 2 + (k + 1) * n] for k in range(4))
        loss_ref = refs[7 * n + 2]
        j = 2 * lax.axis_index("x") + lax.axis_index("y")
        rsum = rows_ref[0]
        for d in range(1, N_DEV):
            rsum = rsum + rows_ref[d]
        taps = []
        for t in range(CONV_TAPS):
            row = ss_ref[t // 2:t // 2 + 1, :]
            c0 = CONV_W * (t % 2)
            pick = row[:, c0:c0 + 128]
            for k in range(1, N_CHIPS):
                pick = jnp.where(j == k, row[:, c0 + 128 * k:c0 + 128 * (k + 1)], pick)
            taps.append(pick)
        grads = [jnp.concatenate([rsum[2:3], rsum[3:4], rsum[0:1]], axis=1), rsum[4:5],
                 ss_ref[17:18, 512:512 + HEAD_DIM], ss_ref[17:18, 640:640 + HEAD_DIM], ss_ref[17:18, 768:776],
                 None, ss_ref[16:17, 0:CONV_W], ss_ref[16:17, CONV_W:2 * CONV_W], ss_ref[17:18, 0:CONV_W]]
        for i in range(n):
            if grads[i] is None:
                for t in range(CONV_TAPS):
                    g_r[i][t:t + 1, :] = taps[t]
                g = g_r[i][...]
            else:
                g = grads[i]
                g_r[i][...] = g
            d_r[i][...], m2_r[i][...], v2_r[i][...] = _adamw_math(w_r[i][...], g, m_r[i][...], v_r[i][...])
        loss_ref[...] = (0.5 / D_MODEL) * jnp.sum(ss_ref[18:19, :], axis=1, keepdims=True)

    vm = pl.BlockSpec(memory_space=pltpu.VMEM)
    shapes = [jax.ShapeDtypeStruct(w.shape, F32) for w in ws]
    out = pl.pallas_call(
        body,
        name="adamw_small",
        in_specs=[vm] * (3 * n + 2),
        out_specs=[vm] * (4 * n + 1),
        out_shape=shapes * 4 + [jax.ShapeDtypeStruct((1, 1), F32)],
        compiler_params=_cparams(),
    )(*ws, *ms, *vs, ssum, rows)
    return out[0:n], out[n:2 * n], out[2 * n:3 * n], out[3 * n:4 * n], out[4 * n]


def _rope_tables(t):
    inv = ROPE_THETA ** (-jnp.arange(0, HEAD_DIM, 2, dtype=F32) / HEAD_DIM)
    ang = jnp.arange(t, dtype=F32)[:, None] * inv[None, :]
    cos, sin = jnp.cos(ang), jnp.sin(ang)
    return jnp.tile(cos, (1, 4)), jnp.tile(jnp.concatenate([-sin, sin], axis=1), (1, 2))


def kernel(x, c, w_ada, b_ada, norm_w, w_in, q_norm_w, k_norm_w, sinks, conv_w, conv_b, ln_w, ln_b, w_out, loss_target, m_w_ada, m_b_ada, m_norm_w, m_w_in, m_q_norm_w, m_k_norm_w, m_sinks, m_conv_w, m_conv_b, m_ln_w, m_ln_b, m_w_out, v_w_ada, v_b_ada, v_norm_w, v_w_in, v_q_norm_w, v_k_norm_w, v_sinks, v_conv_w, v_conv_b, v_ln_w, v_ln_b, v_w_out):
    xi, yi = lax.axis_index("x"), lax.axis_index("y")
    j = 2 * xi + yi
    x2, tgt = x[0], loss_target[0]
    t = x2.shape[0]

    wt_s, mt_s, vt_s = w_in[0].T, m_w_in[0].T, v_w_in[0].T
    cw_pad = jnp.pad(conv_w[0], ((0, 1), (0, 0)))

    q_raw, kv_raw, ga, ua, ug, gb, h, w_full, call, ada4 = _in_proj_gather(
        x2, wt_s.reshape(2, IN_HALF, D_MODEL), c, w_ada[0], b_ada, norm_w)
    ada = ada4.reshape(1, 3 * D_MODEL)
    s1, gate = 1.0 + ada[:, D_MODEL:2 * D_MODEL], ada[:, 2 * D_MODEL:]

    cos_f, sin_s = _rope_tables(t)
    qw2, kw2 = jnp.tile(q_norm_w, (1, 2)), jnp.tile(k_norm_w, (1, 2))

    o, mix_a, wo4, cw4 = _attn_fwd(q_raw, kv_raw, ga, qw2, kw2, sinks, cos_f, sin_s,
                                   w_out[0].reshape(2, OUT_HALF, D_MODEL), cw_pad)
    w_out_full = wo4.reshape(D_MODEL, D_MODEL)
    cw_full = jnp.concatenate([cw4[i] for i in range(N_CHIPS)], axis=1)
    cz, mix_b = _conv_fwd(ua, ug, gb, cw_full, conv_b, ln_w, ln_b)
    dout, dmix_a, dmix_b, gwo_bf, red_o = _out_proj(mix_a, mix_b, x2, tgt, gate, w_out_full)

    dq, dkv, dga, sm_a, gwo = _attn_bwd(q_raw, kv_raw, ga, o, dmix_a, qw2, kw2, sinks, cos_f, sin_s,
                                        gwo_bf.reshape(N_CHIPS, 2, OUT_HALF, D_MODEL))
    dua, dug, dgb, dcw, dvec = _conv_bwd(ua, ug, gb, cz, dmix_b, cw_full, ln_w, ln_b)
    dparts = (dq, dkv, dga, dua, dug, dgb)

    grad_x, gw, ssum, rows = _in_proj_bwd(dparts, h, x2, dout, s1, norm_w, w_full, dcw, dvec, sm_a, red_o)

    gt_w_in = gw.reshape(2 * IN_HALF, D_MODEL)
    g_w_out = gwo.reshape(D_MODEL // N_CHIPS, D_MODEL)
    d_ada_all = jnp.concatenate([rows[:, 2], rows[:, 3], rows[:, 0]], axis=1)
    dcols = lax.dynamic_slice(d_ada_all, (0, ADA_SHARD * j), (N_DEV, ADA_SHARD))
    cact_t = jax.nn.silu(call.reshape(N_DEV, D_MODEL)).T

    g_w_ada, d_w_ada, nm_w_ada, nv_w_ada = _adamw_ada(w_ada[0], m_w_ada[0], v_w_ada[0], cact_t, dcols)
    gt_w_in, dt_w_in, nmt_w_in, nvt_w_in, grad_x = _adamw("adamw_w_in", wt_s, gt_w_in, mt_s, vt_s, 176, through=grad_x)
    g_w_in, d_w_in, nm_w_in, nv_w_in = gt_w_in.T, dt_w_in.T, nmt_w_in.T, nvt_w_in.T
    g_w_out, d_w_out, nm_w_out, nv_w_out = _adamw("adamw_w_out", w_out[0], g_w_out, m_w_out[0], v_w_out[0], 128)
    ws = [b_ada, norm_w, q_norm_w, k_norm_w, sinks, conv_w[0], conv_b, ln_w, ln_b]
    ms = [m_b_ada, m_norm_w, m_q_norm_w, m_k_norm_w, m_sinks, m_conv_w[0], m_conv_b, m_ln_w, m_ln_b]
    vs = [v_b_ada, v_norm_w, v_q_norm_w, v_k_norm_w, v_sinks, v_conv_w[0], v_conv_b, v_ln_w, v_ln_b]
    gs, ds, nms, nvs, loss11 = _adamw_small(ws, ms, vs, ssum, rows)
    loss = loss11[0, 0]

    def order(ada_v, in_v, out_v, sm):
        b, nw_, qw_, kw_, sk_, cw_, cb_, lw_, lb_ = sm
        return [ada_v[None], b, nw_, in_v[None], qw_, kw_, sk_, cw_[None], cb_, lw_, lb_, out_v[None]]

    grads = order(g_w_ada, g_w_in, g_w_out, gs)
    deltas = order(d_w_ada, d_w_in, d_w_out, ds)
    new_m = order(nm_w_ada, nm_w_in, nm_w_out, nms)
    new_v = order(nv_w_ada, nv_w_in, nv_w_out, nvs)
    return (loss, grad_x[None], *grads, *deltas, *new_m, *new_v)
```

```python
import functools

import jax
import jax.numpy as jnp
from jax import lax
from jax.experimental import pallas as pl
from jax.experimental.pallas import tpu as pltpu

F32 = jnp.float32
BF16 = jnp.bfloat16

D_MODEL = 1024
ATTN_W = 512
KV_W = 128
CONV_W = 512
IN_W = 2816
HEAD_DIM = 64
CONV_TAPS = 31
QBLK = 128
EPS = 1e-6
ROPE_THETA = 10000.0

ADAM_LR = 0.001
ADAM_B1 = 0.9
ADAM_B2 = 0.999
ADAM_EPS = 1e-08
ADAM_WD = 0.01
ADAM_STEP = 10

N_CHIPS = 4
N_DEV = 8
IN_HALF = IN_W // N_CHIPS // 2
OUT_HALF = D_MODEL // N_CHIPS // 2
ADA_SHARD = 3 * D_MODEL // N_CHIPS

VMEM_LIMIT = 56 * 1024 * 1024
CONV_PAD = 32


def _cparams(**kw):
    return pltpu.CompilerParams(vmem_limit_bytes=VMEM_LIMIT, **kw)


def _sigmoid(v):
    return 1.0 / (1.0 + jnp.exp(-v))


def _silu(v):
    return v * _sigmoid(v)


def _dsilu(v):
    s = _sigmoid(v)
    return s * (1.0 + v * (1.0 - s))


def _lane(shape):
    return lax.broadcasted_iota(jnp.int32, shape, len(shape) - 1)


PUT_ROWS = 512


def _fetch(hbm_refs, vmem_refs, sem):
    cps = [pltpu.make_async_copy(h, v, sem.at[i]) for i, (h, v) in enumerate(zip(hbm_refs, vmem_refs))]
    for cp in cps:
        cp.start()
    return cps


def _put(vmem_ref, hbm_ref, sem, m):
    r = pl.ds(pl.multiple_of(m * PUT_ROWS, PUT_ROWS), PUT_ROWS)
    return pltpu.make_async_copy(vmem_ref.at[r], hbm_ref.at[r], sem.at[m])


def _put_all(pairs, sems, m):
    for (v, h), sem in zip(pairs, sems):
        _put(v, h, sem, m).start()


def _put_wait(pairs, sems, n):
    for (v, h), sem in zip(pairs, sems):
        for m in range(n):
            _put(v, h, sem, m).wait()


def _head_mean(s, left):
    sl = jnp.sum(jnp.where(left, s, 0.0), axis=-1, keepdims=True)
    sr = jnp.sum(jnp.where(left, 0.0, s), axis=-1, keepdims=True)
    return jnp.where(left, sl, sr) * (1.0 / HEAD_DIM)


def _rot(v, first):
    return jnp.where(first, pltpu.roll(v, 96, 1), pltpu.roll(v, 32, 1))


def _norm_rope(v, w, cos, sin_s, left, first):
    r = lax.rsqrt(_head_mean(v * v, left) + EPS)
    xh = v * r
    n = xh * w
    return n * cos + _rot(n, first) * sin_s, xh, r


def _norm_rope_bwd(d, xh, r, w, cos, sin_s, left, first):
    dn = d * cos - _rot(d, first) * sin_s
    dw = jnp.sum(dn * xh, axis=0, keepdims=True)
    dxh = dn * w
    return r * (dxh - xh * _head_mean(dxh * xh, left)), dw


def _dup_heads(v, left):
    sw = pltpu.roll(v, 64, 1)
    return jnp.where(left, v, sw), jnp.where(left, sw, v)


def _prep_kv(kv_ref, kw_ref, cos_ref, sin_ref, ka_ref, va_ref, t):
    ch = 256
    for g in range(2):
        ka_ref[g, 0:QBLK, :] = jnp.zeros((QBLK, 128), BF16)
        va_ref[g, 0:QBLK, :] = jnp.zeros((QBLK, 128), BF16)

    def chunk(i, carry):
        r0 = pl.multiple_of(i * ch, ch)
        left = _lane((ch, 128)) < 64
        first = (_lane((ch, 128)) % 64) < 32
        k = kv_ref[pl.ds(r0, ch), 0:128]
        v = kv_ref[pl.ds(r0, ch), 128:256]
        kr, _, _ = _norm_rope(k, kw_ref[...], cos_ref[pl.ds(r0, ch), :], sin_ref[pl.ds(r0, ch), :], left, first)
        k0, k1 = _dup_heads(kr, left)
        v0, v1 = _dup_heads(v, left)
        ka_ref[0, pl.ds(QBLK + r0, ch), :] = k0.astype(BF16)
        ka_ref[1, pl.ds(QBLK + r0, ch), :] = k1.astype(BF16)
        va_ref[0, pl.ds(QBLK + r0, ch), :] = v0.astype(BF16)
        va_ref[1, pl.ds(QBLK + r0, ch), :] = v1.astype(BF16)
        return carry

    lax.fori_loop(0, t // ch, chunk, 0)


def _band_mask(n):
    qi = lax.broadcasted_iota(jnp.int32, (2 * QBLK, 2 * QBLK), 0) % QBLK
    kj = lax.broadcasted_iota(jnp.int32, (2 * QBLK, 2 * QBLK), 1)
    local = (kj > qi) & (kj <= qi + QBLK)
    return local & ((n > 0) | (kj >= QBLK))


def _softmax_pair(s, mask, sink0, sink1):
    row = lax.broadcasted_iota(jnp.int32, (2 * QBLK, 1), 0)
    sink = jnp.where(row < QBLK, sink0, sink1)
    s = jnp.where(mask, s, -jnp.inf)
    m = jnp.maximum(jnp.max(s, axis=-1, keepdims=True), sink)
    e = jnp.exp(s - m)
    es = jnp.exp(sink - m)
    inv = 1.0 / (jnp.sum(e, axis=-1, keepdims=True) + es)
    return e * inv, es * inv


def _stack_heads(v, left):
    return jnp.concatenate([jnp.where(left, v, 0.0), jnp.where(left, 0.0, v)], axis=0)


def _attn_fwd(q_raw, kv_raw, ga, qw2, kw2, sinks, cos_f, sin_s, wo, cw):
    t = q_raw.shape[0]
    nblk = t // QBLK
    per_put = PUT_ROWS // QBLK

    def body(q_hbm, kv_ref, ga_hbm, qw_ref, kw_ref, sk_ref, cos_hbm, sin_hbm, wo_ref, cw_ref,
             o_hbm, mix_hbm, wo4_ref, cw4_ref, ka_ref, va_ref, q_ref, ga_ref, o_ref, mix_ref, cos_ref, sin_ref,
             isem, osem0, osem1, ssem, rsem):
        loads = _fetch((cos_hbm, sin_hbm, q_hbm, ga_hbm), (cos_ref, sin_ref, q_ref, ga_ref), isem)
        outs, osems = ((o_ref, o_hbm), (mix_ref, mix_hbm)), (osem0, osem1)
        x, y, c, chips = _place()
        j = 2 * x + y
        sib = (x, y, 1 - c)
        idx = [2 * cx + cy for cx, cy in chips]
        rc = functools.partial(_remote, ssem, rsem)
        wo4_ref[j] = wo_ref[...].astype(BF16)
        cw4_ref[j] = cw_ref[...]
        sends = []
        for k, chip in enumerate(chips):
            sends.append(rc(k, wo4_ref.at[j, c], wo4_ref.at[j, c], (*chip, c)))
            sends.append(rc(6 + k, cw4_ref.at[j], cw4_ref.at[j], (*chip, c)))
        for cp in sends:
            cp.start()

        loads[0].wait()
        loads[1].wait()
        _prep_kv(kv_ref, kw_ref, cos_ref, sin_ref, ka_ref, va_ref, t)
        loads[2].wait()
        loads[3].wait()

        def blk(n, carry):
            r0 = pl.multiple_of(n * QBLK, QBLK)
            left = _lane((QBLK, 128)) < 64
            first = (_lane((QBLK, 128)) % 64) < 32
            cos = cos_ref[pl.ds(r0, QBLK), :]
            sin = sin_ref[pl.ds(r0, QBLK), :]
            mask = _band_mask(n)
            scores = []
            for p in range(4):
                lanes = slice(p * 128, (p + 1) * 128)
                qr, _, _ = _norm_rope(q_ref[pl.ds(r0, QBLK), lanes], qw_ref[...], cos, sin, left, first)
                q2 = _stack_heads(qr * 0.125, left).astype(BF16)
                scores.append(lax.dot_general(q2, ka_ref[p // 2, pl.ds(r0, 2 * QBLK), :], (((1,), (1,)), ((), ())),
                                              preferred_element_type=F32))
            probs = [_softmax_pair(scores[p], mask, sk_ref[0, 2 * p], sk_ref[0, 2 * p + 1])[0].astype(BF16)
                     for p in range(4)]
            for p in range(4):
                lanes = slice(p * 128, (p + 1) * 128)
                o2 = jnp.dot(probs[p], va_ref[p // 2, pl.ds(r0, 2 * QBLK), :], preferred_element_type=F32)
                o = jnp.where(left, o2[0:QBLK], o2[QBLK:2 * QBLK])
                o_ref[pl.ds(r0, QBLK), lanes] = o.astype(BF16)
                mix_ref[pl.ds(r0, QBLK), lanes] = (o * _silu(ga_ref[pl.ds(r0, QBLK), lanes])).astype(BF16)

            @pl.when(n % per_put == per_put - 1)
            def _():
                _put_all(outs, osems, n // per_put)

            return carry

        lax.fori_loop(0, nblk, blk, 0)
        _put_wait(outs, osems, t // PUT_ROWS)

        passed = []
        for k, chip in enumerate(chips):
            jk = idx[k]
            rc(k, wo4_ref.at[jk, c], wo4_ref.at[jk, c], sib).wait_recv()
            passed.append(rc(3 + k, wo4_ref.at[jk, c], wo4_ref.at[jk, c], sib))
            passed[-1].start()
        for k, chip in enumerate(chips):
            jk = idx[k]
            rc(3 + k, wo4_ref.at[jk, 1 - c], wo4_ref.at[jk, 1 - c], sib).wait_recv()
            rc(6 + k, cw4_ref.at[jk], cw4_ref.at[jk], sib).wait_recv()
        for cp in sends + passed:
            cp.wait_send()

    vm = pl.BlockSpec(memory_space=pltpu.VMEM)
    hbm = pl.BlockSpec(memory_space=pl.ANY)
    n_sem = 9
    return pl.pallas_call(
        body,
        name="attn_fwd",
        in_specs=[hbm, vm, hbm, vm, vm, pl.BlockSpec(memory_space=pltpu.SMEM), hbm, hbm, vm, vm],
        out_specs=[hbm, hbm, vm, vm],
        out_shape=[jax.ShapeDtypeStruct((t, ATTN_W), BF16), jax.ShapeDtypeStruct((t, ATTN_W), BF16),
                   jax.ShapeDtypeStruct((N_CHIPS, 2, OUT_HALF, D_MODEL), BF16),
                   jax.ShapeDtypeStruct((N_CHIPS, 32, 128), F32)],
        scratch_shapes=[pltpu.VMEM((2, t + QBLK, 128), BF16), pltpu.VMEM((2, t + QBLK, 128), BF16),
                        pltpu.VMEM((t, ATTN_W), F32), pltpu.VMEM((t, ATTN_W), F32),
                        pltpu.VMEM((t, ATTN_W), BF16), pltpu.VMEM((t, ATTN_W), BF16),
                        pltpu.VMEM((t, 128), F32), pltpu.VMEM((t, 128), F32),
                        pltpu.SemaphoreType.DMA((4,)), pltpu.SemaphoreType.DMA((t // PUT_ROWS,)),
                        pltpu.SemaphoreType.DMA((t // PUT_ROWS,)),
                        pltpu.SemaphoreType.DMA((n_sem,)), pltpu.SemaphoreType.DMA((n_sem,))],
        compiler_params=_cparams(),
    )(q_raw, kv_raw, ga, qw2, kw2, sinks, cos_f, sin_s, wo, cw)


def _attn_bwd(q_raw, kv_raw, ga, o, dmix, qw2, kw2, sinks, cos_f, sin_s, go):
    t = q_raw.shape[0]
    nblk = t // QBLK
    per_put = PUT_ROWS // QBLK

    def body(q_hbm, kv_ref, ga_hbm, o_hbm, dm_hbm, qw_ref, kw_ref, sk_ref, cos_hbm, sin_hbm, go_ref,
             dq_hbm, dkv_ref, dga_hbm, sm_ref, gwo_ref, ka_ref, va_ref, dka_ref, dva_ref,
             sibo_ref, outo_ref, ino_ref, q_ref, ga_ref, o_ref, dm_ref, dq_ref, dga_ref, cos_ref, sin_ref,
             isem, osem0, osem1, ssem, rsem):
        loads = _fetch((cos_hbm, sin_hbm, q_hbm, ga_hbm, o_hbm, dm_hbm), (cos_ref, sin_ref, q_ref, ga_ref, o_ref, dm_ref), isem)
        outs, osems = ((dq_ref, dq_hbm), (dga_ref, dga_hbm)), (osem0, osem1)
        x, y, c, chips = _place()
        sib = (x, y, 1 - c)
        rc = functools.partial(_remote, ssem, rsem)
        theirs, mine = go_ref.at[:, 1 - c], go_ref.at[:, c]
        sends = [_rs_to_sibling(rc, 0, theirs, sibo_ref, sib)]
        loads[0].wait()
        loads[1].wait()
        _prep_kv(kv_ref, kw_ref, cos_ref, sin_ref, ka_ref, va_ref, t)
        dka_ref[...] = jnp.zeros_like(dka_ref)
        dva_ref[...] = jnp.zeros_like(dva_ref)
        sends += _rs_trade(rc, 0, theirs, mine, sibo_ref, outo_ref, ino_ref, OUT_HALF, c, sib, chips)
        for cp in loads[2:]:
            cp.wait()

        def blk(n, carry):
            dqw, dsk = carry
            r0 = pl.multiple_of(n * QBLK, QBLK)
            left = _lane((QBLK, 128)) < 64
            first = (_lane((QBLK, 128)) % 64) < 32
            cos = cos_ref[pl.ds(r0, QBLK), :]
            sin = sin_ref[pl.ds(r0, QBLK), :]
            mask = _band_mask(n)
            row = lax.broadcasted_iota(jnp.int32, (2 * QBLK, 1), 0)
            rows = pl.ds(r0, QBLK)
            win = pl.ds(r0, 2 * QBLK)
            lane_of = [slice(p * 128, (p + 1) * 128) for p in range(4)]
            for grp in ((0, 1), (2, 3)):
                qn = {p: _norm_rope(q_ref[rows, lane_of[p]], qw_ref[...], cos, sin, left, first) for p in grp}
                q2 = {p: _stack_heads(qn[p][0] * 0.125, left).astype(BF16) for p in grp}
                sc = {p: lax.dot_general(q2[p], ka_ref[p // 2, win, :], (((1,), (1,)), ((), ())),
                                         preferred_element_type=F32) for p in grp}
                sm = {p: _softmax_pair(sc[p], mask, sk_ref[0, 2 * p], sk_ref[0, 2 * p + 1]) for p in grp}
                do2 = {}
                for p in grp:
                    gav = ga_ref[rows, lane_of[p]]
                    dmv = dm_ref[rows, lane_of[p]].astype(F32)
                    dga_ref[rows, lane_of[p]] = (dmv * o_ref[rows, lane_of[p]].astype(F32) * _dsilu(gav)).astype(BF16)
                    do2[p] = _stack_heads(dmv * _silu(gav), left).astype(BF16)
                dpm = {p: lax.dot_general(do2[p], va_ref[p // 2, win, :], (((1,), (1,)), ((), ())),
                                          preferred_element_type=F32) for p in grp}
                dsl = {}
                for p in grp:
                    pm, ps = sm[p]
                    delta = jnp.sum(pm * dpm[p], axis=-1, keepdims=True)
                    dsl[p] = (pm * (dpm[p] - delta)).astype(BF16)
                    pd = ps * delta
                    d0 = jnp.sum(jnp.where(row < QBLK, pd, 0.0), axis=0, keepdims=True)
                    d1 = jnp.sum(jnp.where(row < QBLK, 0.0, pd), axis=0, keepdims=True)
                    l8 = _lane((1, 128))
                    dsk = dsk - jnp.where(l8 == 2 * p, d0, 0.0) - jnp.where(l8 == 2 * p + 1, d1, 0.0)
                for p in grp:
                    g = p // 2
                    dva_ref[g, win, :] += lax.dot_general(sm[p][0].astype(BF16), do2[p], (((0,), (0,)), ((), ())),
                                                          preferred_element_type=F32)
                    dka_ref[g, win, :] += lax.dot_general(dsl[p], q2[p], (((0,), (0,)), ((), ())),
                                                          preferred_element_type=F32)
                for p in grp:
                    dq2 = jnp.dot(dsl[p], ka_ref[p // 2, win, :], preferred_element_type=F32)
                    dqr = jnp.where(left, dq2[0:QBLK], dq2[QBLK:2 * QBLK]) * 0.125
                    dq, dw = _norm_rope_bwd(dqr, qn[p][1], qn[p][2], qw_ref[...], cos, sin, left, first)
                    dq_ref[rows, lane_of[p]] = dq.astype(BF16)
                    dqw = dqw + dw

            @pl.when(n % per_put == per_put - 1)
            def _():
                _put_all(outs, osems, n // per_put)

            return dqw, dsk

        zero = jnp.zeros((1, 128), F32)
        dqw, dsk = lax.fori_loop(0, nblk, blk, (zero, zero))

        ch = 256

        def chunk(i, dkw):
            r0 = pl.multiple_of(i * ch, ch)
            left = _lane((ch, 128)) < 64
            first = (_lane((ch, 128)) % 64) < 32
            rows = pl.ds(r0, ch)
            prow = pl.ds(QBLK + r0, ch)

            def fold(ref):
                a0 = ref[0, prow, :]
                a1 = ref[1, prow, :]
                return jnp.where(left, a0 + pltpu.roll(a0, 64, 1), a1 + pltpu.roll(a1, 64, 1))

            cos = cos_ref[rows, :]
            sin = sin_ref[rows, :]
            _, xh, r = _norm_rope(kv_ref[rows, 0:128], kw_ref[...], cos, sin, left, first)
            dk, dw = _norm_rope_bwd(fold(dka_ref), xh, r, kw_ref[...], cos, sin, left, first)
            dkv_ref[rows, 0:128] = dk.astype(BF16)
            dkv_ref[rows, 128:256] = fold(dva_ref).astype(BF16)
            return dkw + dw

        dkw = lax.fori_loop(0, t // ch, chunk, zero)
        sm_ref[...] = jnp.zeros((8, 128), F32)
        sm_ref[0:1, :] = dqw + pltpu.roll(dqw, 64, 1)
        sm_ref[1:2, :] = dkw + pltpu.roll(dkw, 64, 1)
        sm_ref[2:3, :] = dsk

        j = 2 * x + y
        sends.append(_rs_total(rc, 0, mine, sibo_ref, outo_ref, ino_ref, gwo_ref, OUT_HALF, j, c, sib))
        _rs_done(rc, 0, gwo_ref, c, sib)
        for cp in sends:
            cp.wait_send()
        _put_wait(outs, osems, t // PUT_ROWS)

    vm = pl.BlockSpec(memory_space=pltpu.VMEM)
    hbm = pl.BlockSpec(memory_space=pl.ANY)
    return pl.pallas_call(
        body,
        name="attn_bwd",
        in_specs=[hbm, vm, hbm, hbm, hbm, vm, vm, pl.BlockSpec(memory_space=pltpu.SMEM), hbm, hbm, vm],
        out_specs=[hbm, vm, hbm, vm, vm],
        out_shape=[jax.ShapeDtypeStruct((t, ATTN_W), BF16), jax.ShapeDtypeStruct((t, 2 * KV_W), BF16),
                   jax.ShapeDtypeStruct((t, ATTN_W), BF16), jax.ShapeDtypeStruct((8, 128), F32),
                   jax.ShapeDtypeStruct((2, OUT_HALF, D_MODEL), F32)],
        scratch_shapes=[pltpu.VMEM((2, t + QBLK, 128), BF16), pltpu.VMEM((2, t + QBLK, 128), BF16),
                        pltpu.VMEM((2, t + QBLK, 128), F32), pltpu.VMEM((2, t + QBLK, 128), F32)]
        + _rs_scratch(OUT_HALF)
        + [pltpu.VMEM((t, ATTN_W), F32), pltpu.VMEM((t, ATTN_W), F32), pltpu.VMEM((t, ATTN_W), BF16),
           pltpu.VMEM((t, ATTN_W), BF16), pltpu.VMEM((t, ATTN_W), BF16), pltpu.VMEM((t, ATTN_W), BF16),
           pltpu.VMEM((t, 128), F32), pltpu.VMEM((t, 128), F32),
           pltpu.SemaphoreType.DMA((6,)), pltpu.SemaphoreType.DMA((t // PUT_ROWS,)), pltpu.SemaphoreType.DMA((t // PUT_ROWS,)),
           pltpu.SemaphoreType.DMA((RS_SEMS,)), pltpu.SemaphoreType.DMA((RS_SEMS,))],
        compiler_params=_cparams(),
    )(q_raw, kv_raw, ga, o, dmix, qw2, kw2, sinks, cos_f, sin_s, go)


CONV_CH = 256
CONV_SUB = 128
CONV_ACCS = 1


def _shifted_windows(src_ref, r0, sh_ref):
    rows = CONV_CH + CONV_PAD
    win = src_ref[pl.ds(r0, rows), :]
    for b in range(8):
        sh = win if b == 0 else pltpu.roll(win, rows - b, 0)
        for c in range(CONV_W // 128):
            sh_ref[b, c] = sh[:, c * 128:(c + 1) * 128]


def _conv_fwd(ua, ug, gb, cw, cb, lw, lb):
    t = ua.shape[0]

    def body(ua_hbm, ug_hbm, gb_hbm, cw_ref, cb_ref, lw_ref, lb_ref, cz_hbm, mix_hbm, zp_ref, sh_ref,
             ua_ref, ug_ref, gb_ref, cz_ref, mix_ref, isem, osem0, osem1):
        loads = _fetch((ua_hbm, ug_hbm, gb_hbm), (ua_ref, ug_ref, gb_ref), isem)
        outs, osems = ((cz_ref, cz_hbm), (mix_ref, mix_hbm)), (osem0, osem1)
        per_put = PUT_ROWS // CONV_CH
        zp_ref[0:CONV_PAD, :] = jnp.zeros((CONV_PAD, CONV_W), F32)
        loads[0].wait()
        loads[1].wait()

        def glu(i, carry):
            r0 = pl.multiple_of(i * CONV_CH, CONV_CH)
            rows = pl.ds(r0, CONV_CH)
            zp_ref[pl.ds(CONV_PAD + r0, CONV_CH), :] = ua_ref[rows, :] * _sigmoid(ug_ref[rows, :])
            return carry

        lax.fori_loop(0, t // CONV_CH, glu, 0)
        loads[2].wait()

        def chunk(i, carry):
            r0 = pl.multiple_of(i * CONV_CH, CONV_CH)
            _shifted_windows(zp_ref, r0, sh_ref)
            for c in range(CONV_W // 128):
                lanes = slice(c * 128, (c + 1) * 128)

                def sub(k, carry2):
                    b0 = pl.multiple_of(k * CONV_SUB, CONV_SUB)
                    acc = [jnp.broadcast_to(cb_ref[0:1, lanes], (CONV_SUB, 128))] + [None] * (CONV_ACCS - 1)
                    for j in range(CONV_TAPS):
                        off = j + CONV_PAD - (CONV_TAPS - 1)
                        term = sh_ref[off % 8, c, pl.ds(b0 + 8 * (off // 8), CONV_SUB), :] * cw_ref[j:j + 1, lanes]
                        acc[j % CONV_ACCS] = term if acc[j % CONV_ACCS] is None else acc[j % CONV_ACCS] + term
                    cz_ref[pl.ds(r0 + b0, CONV_SUB), lanes] = functools.reduce(lambda a, b: a + b, acc)
                    return carry2

                lax.fori_loop(0, CONV_CH // CONV_SUB, sub, 0)
            rows = pl.ds(r0, CONV_CH)
            cz = cz_ref[rows, :]
            mu = jnp.mean(cz, axis=-1, keepdims=True)
            xc = cz - mu
            rs = lax.rsqrt(jnp.mean(xc * xc, axis=-1, keepdims=True) + EPS)
            ln = xc * rs * lw_ref[...] + lb_ref[...]
            mix_ref[rows, :] = (_silu(ln) * _silu(gb_ref[rows, :])).astype(BF16)

            @pl.when(i % per_put == per_put - 1)
            def _():
                _put_all(outs, osems, i // per_put)

            return carry

        lax.fori_loop(0, t // CONV_CH, chunk, 0)
        _put_wait(outs, osems, t // PUT_ROWS)

    vm = pl.BlockSpec(memory_space=pltpu.VMEM)
    hbm = pl.BlockSpec(memory_space=pl.ANY)
    nput = t // PUT_ROWS
    return pl.pallas_call(
        body,
        name="conv_fwd",
        in_specs=[hbm] * 3 + [vm] * 4,
        out_specs=[hbm, hbm],
        out_shape=[jax.ShapeDtypeStruct((t, CONV_W), F32), jax.ShapeDtypeStruct((t, CONV_W), BF16)],
        scratch_shapes=[pltpu.VMEM((t + CONV_PAD, CONV_W), F32),
                        pltpu.VMEM((8, CONV_W // 128, CONV_CH + CONV_PAD, 128), F32),
                        pltpu.VMEM((t, CONV_W), F32), pltpu.VMEM((t, CONV_W), F32), pltpu.VMEM((t, CONV_W), F32),
                        pltpu.VMEM((t, CONV_W), F32), pltpu.VMEM((t, CONV_W), BF16),
                        pltpu.SemaphoreType.DMA((3,)), pltpu.SemaphoreType.DMA((nput,)), pltpu.SemaphoreType.DMA((nput,))],
        compiler_params=_cparams(),
    )(ua, ug, gb, cw, cb, lw, lb)


def _conv_bwd(ua, ug, gb, cz, dmix, cw, lw, lb):
    t = ua.shape[0]

    def body(ua_hbm, ug_hbm, gb_hbm, cz_hbm, dm_hbm, cw_ref, lw_ref, lb_ref,
             dua_hbm, dug_hbm, dgb_hbm, dcw_ref, dvec_ref, zp_ref, dp_ref, sh_ref, wacc_ref,
             ua_ref, ug_ref, gb_ref, cz_ref, dm_ref, dua_ref, dug_ref, dgb_ref, isem, osem0, osem1, osem2):
        loads = _fetch((ua_hbm, ug_hbm, gb_hbm, cz_hbm, dm_hbm), (ua_ref, ug_ref, gb_ref, cz_ref, dm_ref), isem)
        per_put = PUT_ROWS // CONV_CH
        zp_ref[0:CONV_PAD, :] = jnp.zeros((CONV_PAD, CONV_W), F32)
        dp_ref[t:t + CONV_PAD, :] = jnp.zeros((CONV_PAD, CONV_W), F32)
        wacc_ref[...] = jnp.zeros_like(wacc_ref)
        for cp in loads:
            cp.wait()

        def pointwise(i, carry):
            dcb, dlw, dlb = carry
            r0 = pl.multiple_of(i * CONV_CH, CONV_CH)
            rows = pl.ds(r0, CONV_CH)
            zp_ref[pl.ds(CONV_PAD + r0, CONV_CH), :] = ua_ref[rows, :] * _sigmoid(ug_ref[rows, :])
            cz = cz_ref[rows, :]
            mu = jnp.mean(cz, axis=-1, keepdims=True)
            xc = cz - mu
            rs = lax.rsqrt(jnp.mean(xc * xc, axis=-1, keepdims=True) + EPS)
            xh = xc * rs
            ln = xh * lw_ref[...] + lb_ref[...]
            gbv = gb_ref[rows, :]
            dy = dm_ref[rows, :].astype(F32)
            dgb_ref[rows, :] = (dy * _silu(ln) * _dsilu(gbv)).astype(BF16)
            dl = dy * _silu(gbv) * _dsilu(ln)
            dxh = dl * lw_ref[...]
            dcz = rs * (dxh - jnp.mean(dxh, axis=-1, keepdims=True)
                        - xh * jnp.mean(dxh * xh, axis=-1, keepdims=True))
            dp_ref[rows, :] = dcz

            @pl.when(i % per_put == per_put - 1)
            def _():
                _put(dgb_ref, dgb_hbm, osem2, i // per_put).start()

            return (dcb + jnp.sum(dcz, axis=0, keepdims=True),
                    dlw + jnp.sum(dl * xh, axis=0, keepdims=True),
                    dlb + jnp.sum(dl, axis=0, keepdims=True))

        zero = jnp.zeros((1, CONV_W), F32)
        dcb, dlw, dlb = lax.fori_loop(0, t // CONV_CH, pointwise, (zero, zero, zero))
        dvec_ref[...] = jnp.zeros((8, CONV_W), F32)
        dvec_ref[0:1, :] = dcb
        dvec_ref[1:2, :] = dlw
        dvec_ref[2:3, :] = dlb

        def chunk(i, carry):
            r0 = pl.multiple_of(i * CONV_CH, CONV_CH)
            _shifted_windows(dp_ref, r0, sh_ref)
            for c in range(CONV_W // 128):
                lanes = slice(c * 128, (c + 1) * 128)

                def sub(k, carry2):
                    b0 = pl.multiple_of(k * CONV_SUB, CONV_SUB)
                    acc = [None] * CONV_ACCS
                    for j in range(CONV_TAPS):
                        off = CONV_TAPS - 1 - j
                        term = sh_ref[off % 8, c, pl.ds(b0 + 8 * (off // 8), CONV_SUB), :] * cw_ref[j:j + 1, lanes]
                        acc[j % CONV_ACCS] = term if acc[j % CONV_ACCS] is None else acc[j % CONV_ACCS] + term
                    acc = functools.reduce(lambda a, b: a + b, acc)
                    rr = pl.ds(r0 + b0, CONV_SUB)
                    sg = _sigmoid(ug_ref[rr, lanes])
                    dua_ref[rr, lanes] = (acc * sg).astype(BF16)
                    dug_ref[rr, lanes] = (acc * ua_ref[rr, lanes] * sg * (1.0 - sg)).astype(BF16)
                    return carry2

                lax.fori_loop(0, CONV_CH // CONV_SUB, sub, 0)
            _shifted_windows(zp_ref, r0, sh_ref)
            for c in range(CONV_W // 128):
                lanes = slice(c * 128, (c + 1) * 128)

                def subw(k, carry2):
                    b0 = pl.multiple_of(k * CONV_SUB, CONV_SUB)
                    dcz = dp_ref[pl.ds(r0 + b0, CONV_SUB), lanes]
                    for j in range(CONV_TAPS):
                        off = j + CONV_PAD - (CONV_TAPS - 1)
                        pr = dcz * sh_ref[off % 8, c, pl.ds(b0 + 8 * (off // 8), CONV_SUB), :]
                        parts = [pr[8 * q:8 * (q + 1)] for q in range(CONV_SUB // 8)]
                        while len(parts) > 1:
                            parts = [a + b for a, b in zip(parts[0::2], parts[1::2])]
                        wacc_ref[8 * j:8 * (j + 1), lanes] += parts[0]
                    return carry2

                lax.fori_loop(0, CONV_CH // CONV_SUB, subw, 0)

            @pl.when(i % per_put == per_put - 1)
            def _():
                _put_all(((dua_ref, dua_hbm), (dug_ref, dug_hbm)), (osem0, osem1), i // per_put)

            return carry

        lax.fori_loop(0, t // CONV_CH, chunk, 0)
        _put_wait(((dua_ref, dua_hbm), (dug_ref, dug_hbm), (dgb_ref, dgb_hbm)), (osem0, osem1, osem2), t // PUT_ROWS)
        dcw_ref[...] = jnp.zeros((16, 2 * CONV_W), F32)
        for j in range(CONV_TAPS):
            dcw_ref[j // 2:j // 2 + 1, CONV_W * (j % 2):CONV_W * (j % 2 + 1)] = jnp.sum(
                wacc_ref[8 * j:8 * (j + 1), :], axis=0, keepdims=True)

    vm = pl.BlockSpec(memory_space=pltpu.VMEM)
    hbm = pl.BlockSpec(memory_space=pl.ANY)
    return pl.pallas_call(
        body,
        name="conv_bwd",
        in_specs=[hbm] * 5 + [vm] * 3,
        out_specs=[hbm] * 3 + [vm] * 2,
        out_shape=[jax.ShapeDtypeStruct((t, CONV_W), BF16)] * 3
        + [jax.ShapeDtypeStruct((16, 2 * CONV_W), F32), jax.ShapeDtypeStruct((8, CONV_W), F32)],
        scratch_shapes=[pltpu.VMEM((t + CONV_PAD, CONV_W), F32), pltpu.VMEM((t + CONV_PAD, CONV_W), F32),
                        pltpu.VMEM((8, CONV_W // 128, CONV_CH + CONV_PAD, 128), F32), pltpu.VMEM((8 * 32, CONV_W), F32)]
        + [pltpu.VMEM((t, CONV_W), F32)] * 4 + [pltpu.VMEM((t, CONV_W), BF16)] * 4
        + [pltpu.SemaphoreType.DMA((5,))] + [pltpu.SemaphoreType.DMA((t // PUT_ROWS,))] * 3,
        compiler_params=_cparams(),
    )(ua, ug, gb, cz, dmix, cw, lw, lb)


def _out_proj(mix_a, mix_b, x, tgt, gate, w_out):
    t = x.shape[0]
    tm = 512
    nstep = t // tm

    def body(ma_ref, mb_ref, x_ref, t_ref, g_ref, w_ref, dout_ref, dma_ref, dmb_ref, gw_ref, red_ref, acc_ref):
        i = pl.program_id(0)

        @pl.when(i == 0)
        def _():
            acc_ref[...] = jnp.zeros_like(acc_ref)
            red_ref[...] = jnp.zeros_like(red_ref)

        mix = jnp.concatenate([ma_ref[...], mb_ref[...]], axis=1)
        y = jnp.dot(mix, w_ref[...], preferred_element_type=F32)
        gate_v = g_ref[...]
        err = x_ref[...] + gate_v * y - t_ref[...]
        dout = err * (1.0 / D_MODEL)
        dout_ref[...] = dout
        red_ref[0:1, :] += jnp.sum(dout * y, axis=0, keepdims=True)
        red_ref[1:2, :] += jnp.sum(err * err, axis=0, keepdims=True)
        dy = (dout * gate_v).astype(BF16)
        dmix = lax.dot_general(dy, w_ref[...], (((1,), (1,)), ((), ())), preferred_element_type=F32)
        dma_ref[...] = dmix[:, 0:512].astype(BF16)
        dmb_ref[...] = dmix[:, 512:1024].astype(BF16)
        acc_ref[...] += lax.dot_general(mix, dy, (((0,), (0,)), ((), ())), preferred_element_type=F32)

        @pl.when(i == nstep - 1)
        def _():
            gw_ref[...] = acc_ref[...].astype(BF16)

    row = lambda w: pl.BlockSpec((tm, w), lambda i: (i, 0))
    const = lambda s: pl.BlockSpec(s, lambda i: (0, 0))
    return pl.pallas_call(
        body,
        name="out_proj",
        grid=(nstep,),
        in_specs=[row(512), row(512), row(D_MODEL), row(D_MODEL), const((1, D_MODEL)),
                  pl.BlockSpec((D_MODEL, D_MODEL), lambda i: (0, 0), pipeline_mode=pl.Buffered(1))],
        out_specs=[row(D_MODEL), row(512), row(512), const((D_MODEL, D_MODEL)), const((8, D_MODEL))],
        out_shape=[jax.ShapeDtypeStruct((t, D_MODEL), F32), jax.ShapeDtypeStruct((t, 512), BF16),
                   jax.ShapeDtypeStruct((t, 512), BF16), jax.ShapeDtypeStruct((D_MODEL, D_MODEL), BF16),
                   jax.ShapeDtypeStruct((8, D_MODEL), F32)],
        scratch_shapes=[pltpu.VMEM((D_MODEL, D_MODEL), F32)],
        compiler_params=_cparams(dimension_semantics=("arbitrary",)),
    )(mix_a, mix_b, x, tgt, gate, w_out)


DPROJ_WIDTHS = (512, 256, 512, 512, 512, 512)
DPROJ_STARTS = (0, 512, 768, 1280, 1792, 2304)
WIN_W = 768
WIN_START = (0, 640, 1408, 2048)
WIN_OFF = (0, 64, 0, 64)
N_GW = N_CHIPS


def _window_pieces(s):
    lo, hi = WIN_START[s], WIN_START[s] + WIN_W
    out = []
    for p, (st, w) in enumerate(zip(DPROJ_STARTS, DPROJ_WIDTHS)):
        a, b = max(lo, st), min(hi, st + w)
        if a < b:
            out.append((p, a - st, b - a, a - lo))
    return out


def _in_proj_bwd(dparts, h, x, dout, s1, nw, wt_full, dcw, dvec, sm_a, row0):
    t = x.shape[0]
    tm = 256
    nstep = N_GW + t // tm
    n_sem = 20
    rows0 = 32
    hs = rows0 // 2
    npart = len(DPROJ_WIDTHS)

    def body(*refs):
        d_hbm, d_ref = refs[:npart], refs[npart:2 * npart]
        (x_ref, dout_ref, s1_ref, nw_ref, h_ref, wt_hbm, dcw_ref, dvec_ref, sma_ref, row0_ref,
         gx_ref, gw_hbm, ssum_ref, rows_ref,
         stg_ref, wt_ref, gt_ref, sib_ref, out_ref, in_ref, res_ref, sall_ref, red_ref, sm0_ref, ssib_ref, schip_ref, sres_ref,
         wsem, lsem, ssem, rsem) = refs[2 * npart:]
        i = pl.program_id(0)
        x_, y_, c, chips = _place()
        j = 2 * x_ + y_
        dev = 2 * j + c
        sib = (x_, y_, 1 - c)
        rc = functools.partial(_remote, ssem, rsem)
        rel_chip = [2 * cx + cy for cx, cy in chips] + [j]
        peers = [(px, py, pc) for px in (x_, 1 - x_) for py in (y_, 1 - y_) for pc in (c, 1 - c)][1:]
        wt_copy = pltpu.make_async_copy(wt_hbm, wt_ref, lsem.at[0])

        def window(case, slot):
            return [pltpu.make_async_copy(d_hbm[p].at[:, pl.ds(c0, w)], stg_ref.at[slot, :, pl.ds(w0, w)], wsem.at[slot, n])
                    for n, (p, c0, w, w0) in enumerate(_window_pieces(case))]

        def to_sibling(k):
            return rc(k, gt_ref.at[k, 1 - c], sib_ref.at[k], sib)

        def to_chip(k):
            return rc(4 + k, out_ref.at[k], in_ref.at[k], (*chips[k], c))

        def trade(k):
            to_sibling(k).wait_recv()

            def add(n, carry):
                rr = pl.ds(pl.multiple_of(n * RS_CH, RS_CH), RS_CH)
                out_ref[k, rr, :] = (gt_ref[k, c, rr, :].astype(F32) + sib_ref[k, rr, :].astype(F32)).astype(BF16)
                return carry

            lax.fori_loop(0, IN_HALF // RS_CH, add, 0)
            to_chip(k).start()

        mine_s = pl.ds(pl.multiple_of(c * hs, 8), hs)
        other_s = pl.ds(pl.multiple_of((1 - c) * hs, 8), hs)

        def small_to_sibling():
            return rc(15, sm0_ref.at[other_s], ssib_ref, sib)

        def small_to_chip(k):
            return rc(16 + k, schip_ref.at[j], schip_ref.at[j], (*chips[k], c))

        def small_share():
            return rc(19, sres_ref.at[c], sres_ref.at[c], sib)

        for k in range(N_GW):
            @pl.when(i == k)
            def _(k=k):
                slot = k % 2
                if k == 0:
                    red_ref[...] = jnp.zeros_like(red_ref)
                    wt_copy.start()
                    sm0_ref[...] = jnp.zeros_like(sm0_ref)
                    sm0_ref[0:16, :] = dcw_ref[...]
                    sm0_ref[16:17, 0:CONV_W] = dvec_ref[0:1, :]
                    sm0_ref[16:17, CONV_W:2 * CONV_W] = dvec_ref[1:2, :]
                    sm0_ref[17:18, 0:CONV_W] = dvec_ref[2:3, :]
                    for r in range(3):
                        sm0_ref[17:18, CONV_W + 128 * r:CONV_W + 128 * (r + 1)] = sma_ref[r:r + 1, :]
                    sm0_ref[18:19, :] = row0_ref[1:2, :]
                    small_to_sibling().start()
                if k == 1:
                    small_to_sibling().wait_recv()
                    schip_ref[j] = sm0_ref[mine_s, :] + ssib_ref[...]
                    for kk in range(3):
                        small_to_chip(kk).start()
                if k == N_GW - 1:
                    for kk in range(3):
                        jk = rel_chip[kk]
                        rc(16 + kk, schip_ref.at[jk], schip_ref.at[jk], sib).wait_recv()
                    tot = schip_ref[0]
                    for d in range(1, N_CHIPS):
                        tot = tot + schip_ref[d]
                    sres_ref[c] = tot
                    small_share().start()
                for case in range(N_CHIPS):
                    if k == 0:
                        @pl.when(rel_chip[0] == case)
                        def _():
                            for cp in window(case, 0):
                                cp.start()
                    if k + 1 < N_GW:
                        @pl.when(rel_chip[k + 1] == case)
                        def _():
                            for cp in window(case, 1 - slot):
                                cp.start()
                for case in range(N_CHIPS):
                    @pl.when(rel_chip[k] == case)
                    def _():
                        for cp in window(case, slot):
                            cp.wait()
                g = lax.dot_general(stg_ref[slot], h_ref[...], (((0,), (0,)), ((), ())), preferred_element_type=F32)
                for off in sorted(set(WIN_OFF)):
                    @pl.when(rel_chip[k] % 2 == (1 if off else 0))
                    def _():
                        gt_ref[k, 0] = g[off:off + IN_HALF].astype(BF16)
                        gt_ref[k, 1] = g[off + IN_HALF:off + 2 * IN_HALF].astype(BF16)
                to_sibling(k).start()
                if k >= 1:
                    trade(k - 1)

        @pl.when(i == N_GW)
        def _():
            wt_copy.wait()

        @pl.when(i >= N_GW)
        def _():
            xv = x_ref[...]
            r = lax.rsqrt(jnp.mean(xv * xv, axis=-1, keepdims=True) + EPS)
            xh = xv * r
            n = xh * nw_ref[...]
            dproj = jnp.concatenate([ref[...] for ref in d_ref], axis=1)
            dh = jnp.dot(dproj, wt_ref[...], preferred_element_type=F32)
            red_ref[0:1, :] += jnp.sum(dh, axis=0, keepdims=True)
            red_ref[1:2, :] += jnp.sum(dh * n, axis=0, keepdims=True)
            dn = dh * s1_ref[...]
            red_ref[2:3, :] += jnp.sum(dn * xh, axis=0, keepdims=True)
            dxh = dn * nw_ref[...]
            gx_ref[...] = dout_ref[...] + r * (dxh - xh * jnp.mean(dxh * xh, axis=-1, keepdims=True))

        @pl.when(i == nstep - 1)
        def _():
            sall_ref[dev] = row0_ref[...]
            sall_ref[dev, 2:5, :] = red_ref[0:3, :]
            sends = [rc(8 + k, sall_ref.at[dev], sall_ref.at[dev], peer) for k, peer in enumerate(peers)]
            for cp in sends:
                cp.start()
            sends += [to_sibling(k) for k in range(N_GW)] + [to_chip(k) for k in range(3)]
            sends += [small_to_sibling(), small_share()] + [small_to_chip(k) for k in range(3)]
            own = N_GW - 1
            to_sibling(own).wait_recv()
            for k in range(3):
                to_chip(k).wait_recv()

            def total(n, carry):
                rr = pl.ds(pl.multiple_of(n * RS_CH, RS_CH), RS_CH)
                acc = gt_ref[own, c, rr, :].astype(F32) + sib_ref[own, rr, :].astype(F32)
                for k in range(3):
                    acc = acc + in_ref[k, rr, :].astype(F32)
                res_ref[c, rr, :] = acc
                return carry

            lax.fori_loop(0, IN_HALF // RS_CH, total, 0)
            share = rc(7, res_ref.at[c], res_ref.at[c], sib)
            share.start()
            sends.append(share)
            for k, (px, py, pc) in enumerate(peers):
                pdev = 4 * px + 2 * py + pc
                rc(8 + k, sall_ref.at[pdev], sall_ref.at[pdev], (px, py, pc)).wait_recv()
            rows_ref[...] = sall_ref[...]
            rc(19, sres_ref.at[1 - c], sres_ref.at[1 - c], sib).wait_recv()
            ssum_ref[0:hs, :] = sres_ref[0]
            ssum_ref[hs:rows0, :] = sres_ref[1]
            rc(7, res_ref.at[1 - c], res_ref.at[1 - c], sib).wait_recv()
            back = pltpu.make_async_copy(res_ref, gw_hbm, lsem.at[1])
            back.start()
            for cp in sends:
                cp.wait_send()
            back.wait()

    blk = lambda i: jnp.maximum(i - N_GW, 0)
    row = lambda w: pl.BlockSpec((tm, w), lambda i: (blk(i), 0))
    vec = pl.BlockSpec((1, D_MODEL), lambda i: (0, 0))
    const = lambda shape: pl.BlockSpec(shape, lambda i: (0,) * len(shape))
    hbm = pl.BlockSpec(memory_space=pl.ANY)
    return pl.pallas_call(
        body,
        name="in_proj_bwd",
        grid=(nstep,),
        in_specs=[hbm] * npart + [row(w) for w in DPROJ_WIDTHS] + [row(D_MODEL), row(D_MODEL), vec, vec,
                  pl.BlockSpec((t, D_MODEL), lambda i: (0, 0), pipeline_mode=pl.Buffered(1)), hbm, const((16, D_MODEL)),
                  const((8, CONV_W)), const((8, 128)), const((8, D_MODEL))],
        out_specs=[row(D_MODEL), hbm, const((rows0, D_MODEL)), const((N_DEV, 8, D_MODEL))],
        out_shape=[jax.ShapeDtypeStruct((t, D_MODEL), F32), jax.ShapeDtypeStruct((2, IN_HALF, D_MODEL), F32),
                   jax.ShapeDtypeStruct((rows0, D_MODEL), F32), jax.ShapeDtypeStruct((N_DEV, 8, D_MODEL), F32)],
        scratch_shapes=[pltpu.VMEM((2, t, WIN_W), BF16), pltpu.VMEM((IN_W, D_MODEL), BF16),
                        pltpu.VMEM((N_CHIPS, 2, IN_HALF, D_MODEL), BF16), pltpu.VMEM((N_CHIPS, IN_HALF, D_MODEL), BF16),
                        pltpu.VMEM((3, IN_HALF, D_MODEL), BF16), pltpu.VMEM((3, IN_HALF, D_MODEL), BF16),
                        pltpu.VMEM((2, IN_HALF, D_MODEL), F32), pltpu.VMEM((N_DEV, 8, D_MODEL), F32),
                        pltpu.VMEM((8, D_MODEL), F32), pltpu.VMEM((rows0, D_MODEL), F32), pltpu.VMEM((hs, D_MODEL), F32),
                        pltpu.VMEM((N_CHIPS, hs, D_MODEL), F32),
                        pltpu.VMEM((2, hs, D_MODEL), F32), pltpu.SemaphoreType.DMA((2, 3)), pltpu.SemaphoreType.DMA((2,)),
                        pltpu.SemaphoreType.DMA((n_sem,)), pltpu.SemaphoreType.DMA((n_sem,))],
        compiler_params=_cparams(dimension_semantics=("arbitrary",)),
    )(*dparts, *dparts, x, dout, s1, nw, h, wt_full, dcw, dvec, sm_a, row0)


MESH = pl.DeviceIdType.MESH


def _place():
    x, y, c = lax.axis_index("x"), lax.axis_index("y"), lax.axis_index("c")
    chips = [(1 - x, y), (x, 1 - y), (1 - x, 1 - y)]
    return x, y, c, chips


def _remote(sems_s, sems_r, k, src, dst, to):
    return pltpu.make_async_remote_copy(src_ref=src, dst_ref=dst, send_sem=sems_s.at[k], recv_sem=sems_r.at[k],
                                        device_id=to, device_id_type=MESH)


RS_CH = 32
RS_SEMS = 5


def _rs_to_sibling(rc, s0, theirs, sib_ref, sib):
    cp = rc(s0, theirs, sib_ref, sib)
    cp.start()
    return cp


def _rs_trade(rc, s0, theirs, mine, sib_ref, out_ref, in_ref, rows, c, sib, chips):
    rc(s0, theirs, sib_ref, sib).wait_recv()
    cps = []
    for k, (cx, cy) in enumerate(chips):
        jk = 2 * cx + cy

        def add(i, carry, jk=jk, k=k):
            rr = pl.ds(pl.multiple_of(i * RS_CH, RS_CH), RS_CH)
            out_ref[k, rr, :] = (mine[jk, rr, :].astype(F32) + sib_ref[jk, rr, :].astype(F32)).astype(BF16)
            return carry

        lax.fori_loop(0, rows // RS_CH, add, 0)
        cps.append(rc(s0 + 1 + k, out_ref.at[k], in_ref.at[k], (cx, cy, c)))
        cps[-1].start()
    return cps


def _rs_total(rc, s0, mine, sib_ref, out_ref, in_ref, res_ref, rows, j, c, sib):
    for k in range(3):
        rc(s0 + 1 + k, out_ref.at[k], in_ref.at[k], sib).wait_recv()

    def total(i, carry):
        rr = pl.ds(pl.multiple_of(i * RS_CH, RS_CH), RS_CH)
        acc = mine[j, rr, :].astype(F32) + sib_ref[j, rr, :].astype(F32)
        for k in range(3):
            acc = acc + in_ref[k, rr, :].astype(F32)
        res_ref[c, rr, :] = acc
        return carry

    lax.fori_loop(0, rows // RS_CH, total, 0)
    cp = rc(s0 + 4, res_ref.at[c], res_ref.at[c], sib)
    cp.start()
    return cp


def _rs_done(rc, s0, res_ref, c, sib):
    rc(s0 + 4, res_ref.at[1 - c], res_ref.at[1 - c], sib).wait_recv()


def _rs_scratch(rows):
    return [pltpu.VMEM((N_CHIPS, rows, D_MODEL), BF16), pltpu.VMEM((3, rows, D_MODEL), BF16),
            pltpu.VMEM((3, rows, D_MODEL), BF16)]


MAIN_W = 640
MAIN_DST = (((0, 0, 512), (1, 0, 128)), ((2, 0, 512), (3, 0, 128)), ((3, 128, 384), (4, 0, 256)), ((4, 384, 128), (5, 0, 512)))
PAIR_DST = ((1, 128, 128), (4, 256, 128))


def _in_proj_gather(x, wt, c_row, w_ada, b_ada, nw):
    t = x.shape[0]
    ch = 512
    n_sem = 16

    def body(x_hbm, wt_ref, c_ref, wada_ref, bada_ref, nw_ref,
             q_hbm, kv_hbm, ga_hbm, ua_hbm, ug_hbm, gb_hbm, h_hbm, w4_hbm, call_ref, ada_ref,
             x_ref, h_ref, w4_ref, stg_ref, pstg_ref, part_ref, lsem, osem, wsem, ssem, rsem):
        outs = (q_hbm, kv_hbm, ga_hbm, ua_hbm, ug_hbm, gb_hbm)
        x_, y_, c, chips = _place()
        j = 2 * x_ + y_
        dev = 2 * j + c
        sib = (x_, y_, 1 - c)
        idx = [2 * cx + cy for cx, cy in chips]
        rc = functools.partial(_remote, ssem, rsem)
        x_copy = pltpu.make_async_copy(x_hbm, x_ref, lsem.at[0])
        x_copy.start()

        def rows_of(s, cc):
            return pl.ds(pl.multiple_of(2 * IN_HALF * s + IN_HALF * cc, 16), IN_HALF)

        w4_ref[rows_of(j, 0), :] = wt_ref[0].astype(BF16)
        w4_ref[rows_of(j, 1), :] = wt_ref[1].astype(BF16)
        call_ref[dev] = c_ref[...]
        sends = []
        peers = [(px, py, pc) for px in (x_, 1 - x_) for py in (y_, 1 - y_) for pc in (c, 1 - c)][1:]
        for k, peer in enumerate(peers):
            sends.append(rc(k, call_ref.at[dev], call_ref.at[dev], peer))
        for cp in sends:
            cp.start()

        for k, (px, py, pc) in enumerate(peers):
            pdev = 4 * px + 2 * py + pc
            rc(k, call_ref.at[pdev], call_ref.at[pdev], (px, py, pc)).wait_recv()
        rowid = lax.broadcasted_iota(jnp.int32, (N_DEV, D_MODEL), 0)
        call = jnp.zeros((N_DEV, D_MODEL), F32)
        for r in range(N_DEV):
            call = jnp.where(rowid == r, jnp.broadcast_to(call_ref[r], (N_DEV, D_MODEL)), call)
        bsh = bada_ref[:, 0:ADA_SHARD]
        for k in range(1, N_CHIPS):
            bsh = jnp.where(j == k, bada_ref[:, ADA_SHARD * k:ADA_SHARD * (k + 1)], bsh)
        part = jnp.dot(_silu(call).astype(BF16), wada_ref[...].astype(BF16), preferred_element_type=F32) + bsh
        for r in range(N_DEV):
            part_ref[r] = part[r:r + 1, :]
        ada_ref[j] = part_ref[dev]
        for k, chip in enumerate(chips):
            sends.append(rc(13 + k, part_ref.at[2 * idx[k] + c], ada_ref.at[j], (*chip, c)))
            sends[-1].start()
        for k, chip in enumerate(chips):
            sends.append(rc(7 + k, w4_ref.at[rows_of(j, c)], w4_ref.at[rows_of(j, c)], (*chip, c)))
            sends[-1].start()
        for k in range(3):
            rc(13 + k, ada_ref.at[idx[k]], ada_ref.at[idx[k]], sib).wait_recv()

        shift = jnp.concatenate([ada_ref[0], ada_ref[1][:, 0:256]], axis=1)
        s1 = 1.0 + jnp.concatenate([ada_ref[1][:, 256:768], ada_ref[2][:, 0:512]], axis=1)
        x_copy.wait()

        def norm(i, carry):
            rr = pl.ds(pl.multiple_of(i * ch, ch), ch)
            xv = x_ref[rr, :]
            r = lax.rsqrt(jnp.mean(xv * xv, axis=-1, keepdims=True) + EPS)
            h_ref[rr, :] = ((xv * r) * nw_ref[...] * s1 + shift).astype(BF16)
            return carry

        lax.fori_loop(0, t // ch, norm, 0)
        h_copy = pltpu.make_async_copy(h_ref, h_hbm, lsem.at[1])
        h_copy.start()

        def put_main(case, slot):
            cps, col = [], 0
            for n, (a, c0, w) in enumerate(MAIN_DST[case]):
                cps.append(pltpu.make_async_copy(stg_ref.at[slot, :, pl.ds(col, w)], outs[a].at[:, pl.ds(c0, w)], osem.at[slot, n]))
                col += w
            return cps

        def put_pair(case, slot):
            a, c0, w = PAIR_DST[case]
            return pltpu.make_async_copy(pstg_ref.at[slot], outs[a].at[:, pl.ds(c0, w)], osem.at[slot, 2])

        def project(first_row, width, dst, slot):
            wrows = pl.ds(pl.multiple_of(first_row, 128), width)

            def blk(i, carry):
                rr = pl.ds(pl.multiple_of(i * ch, ch), ch)
                dst[slot, rr, :] = lax.dot_general(h_ref[rr, :], w4_ref[wrows, :], (((1,), (1,)), ((), ())),
                                                   preferred_element_type=F32)
                return carry

            lax.fori_loop(0, t // ch, blk, 0)

        def phase(p, s, pair):
            slot = p % 2
            if p >= 2:
                for case in range(N_CHIPS):
                    @pl.when(order[p - 2] == case)
                    def _():
                        for cp in put_main(case, slot):
                            cp.wait()
            if p == 3:
                for case in range(2):
                    @pl.when(j // 2 == case)
                    def _():
                        put_pair(case, 0).wait()
            project(2 * IN_HALF * s + 64 * (s % 2), MAIN_W, stg_ref, slot)
            for case in range(N_CHIPS):
                @pl.when(s == case)
                def _():
                    for cp in put_main(case, slot):
                        cp.start()
            if pair is not None:
                project(MAIN_W + 2 * (2 * IN_HALF) * pair, 128, pstg_ref, slot % 2 if p == 2 else 1)
                for case in range(2):
                    @pl.when(pair == case)
                    def _():
                        put_pair(case, 0 if p == 2 else 1).start()

        order = [j] + idx
        w_out = [pltpu.make_async_copy(w4_ref.at[pl.ds(pl.multiple_of(2 * IN_HALF * s, 32), 2 * IN_HALF)],
                                       w4_hbm.at[pl.ds(pl.multiple_of(2 * IN_HALF * s, 32), 2 * IN_HALF)], wsem.at[p])
                 for p, s in enumerate(order)]
        w_out[0].start()
        phase(0, j, None)
        passed = []
        for k in range(3):
            jk = idx[k]
            rc(7 + k, w4_ref.at[rows_of(jk, c)], w4_ref.at[rows_of(jk, c)], sib).wait_recv()
            passed.append(rc(10 + k, w4_ref.at[rows_of(jk, c)], w4_ref.at[rows_of(jk, c)], sib))
            passed[-1].start()
            rc(10 + k, w4_ref.at[rows_of(jk, 1 - c)], w4_ref.at[rows_of(jk, 1 - c)], sib).wait_recv()
            w_out[1 + k].start()
            if k == 0:
                phase(1, jk, None)
            elif k == 1:
                phase(2, jk, j // 2)
            else:
                phase(3, jk, 1 - j // 2)

        for case in range(N_CHIPS):
            for p in (2, 3):
                @pl.when(order[p] == case)
                def _():
                    for cp in put_main(case, p % 2):
                        cp.wait()
        for case in range(2):
            @pl.when(1 - j // 2 == case)
            def _():
                put_pair(case, 1).wait()
        h_copy.wait()
        for cp in w_out:
            cp.wait()
        for cp in sends + passed:
            cp.wait_send()

    vm = pl.BlockSpec(memory_space=pltpu.VMEM)
    hbm = pl.BlockSpec(memory_space=pl.ANY)
    widths = (512, 256, 512, 512, 512, 512)
    return pl.pallas_call(
        body,
        name="in_proj",
        in_specs=[hbm, vm, vm, vm, vm, vm],
        out_specs=[hbm] * 8 + [vm, vm],
        out_shape=[jax.ShapeDtypeStruct((t, w), F32) for w in widths]
        + [jax.ShapeDtypeStruct((t, D_MODEL), BF16), jax.ShapeDtypeStruct((IN_W, D_MODEL), BF16),
           jax.ShapeDtypeStruct((N_DEV, 1, D_MODEL), F32), jax.ShapeDtypeStruct((N_CHIPS, 1, ADA_SHARD), F32)],
        scratch_shapes=[pltpu.VMEM((t, D_MODEL), F32), pltpu.VMEM((t, D_MODEL), BF16), pltpu.VMEM((IN_W, D_MODEL), BF16),
                        pltpu.VMEM((2, t, MAIN_W), F32), pltpu.VMEM((2, t, 128), F32), pltpu.VMEM((N_DEV, 1, ADA_SHARD), F32),
                        pltpu.SemaphoreType.DMA((2,)), pltpu.SemaphoreType.DMA((2, 3)), pltpu.SemaphoreType.DMA((N_CHIPS,)),
                        pltpu.SemaphoreType.DMA((n_sem,)), pltpu.SemaphoreType.DMA((n_sem,))],
        compiler_params=_cparams(),
    )(x, wt, c_row, w_ada, b_ada, nw)


def _adamw_math(w, g, m, v):
    m2 = ADAM_B1 * m + (1.0 - ADAM_B1) * g
    v2 = ADAM_B2 * v + (1.0 - ADAM_B2) * (g * g)
    m_hat = m2 / (1.0 - ADAM_B1 ** ADAM_STEP)
    v_hat = v2 / (1.0 - ADAM_B2 ** ADAM_STEP)
    delta = -ADAM_LR * (m_hat / (jnp.sqrt(v_hat) + ADAM_EPS) + ADAM_WD * w)
    return delta, m2, v2


def _adamw(name, w, g, m, v, tm, through=None):
    r, cdim = w.shape
    nstep = r // tm
    extra = [] if through is None else [through]

    def body(w_ref, g_ref, m_ref, v_ref, *rest):
        g2_ref, d_ref, m2_ref, v2_ref = rest[len(extra):len(extra) + 4]
        g = g_ref[...]
        g2_ref[...] = g
        d_ref[...], m2_ref[...], v2_ref[...] = _adamw_math(w_ref[...], g, m_ref[...], v_ref[...])
        if extra:
            rest[-1][...] = rest[0][...]

    blk = pl.BlockSpec((tm, cdim), lambda i: (i, 0))
    eblk = [pl.BlockSpec((e.shape[0] // nstep, e.shape[1]), lambda i: (i, 0)) for e in extra]
    return pl.pallas_call(
        body,
        name=name,
        grid=(nstep,),
        in_specs=[blk] * 4 + eblk,
        out_specs=[blk] * 4 + eblk,
        out_shape=[jax.ShapeDtypeStruct((r, cdim), F32)] * 4 + [jax.ShapeDtypeStruct(e.shape, e.dtype) for e in extra],
        compiler_params=_cparams(dimension_semantics=("arbitrary",)),
    )(w, g, m, v, *extra)


def _adamw_ada(w, m, v, cact_t, dcols):
    r, cdim = w.shape
    tm = 256

    def body(w_ref, m_ref, v_ref, ct_ref, dc_ref, g_ref, d_ref, m2_ref, v2_ref):
        g = jnp.dot(ct_ref[...], dc_ref[...], preferred_element_type=F32, precision=lax.Precision.HIGHEST)
        g_ref[...] = g
        d_ref[...], m2_ref[...], v2_ref[...] = _adamw_math(w_ref[...], g, m_ref[...], v_ref[...])

    blk = pl.BlockSpec((tm, cdim), lambda i: (i, 0))
    return pl.pallas_call(
        body,
        name="adamw_w_ada",
        grid=(r // tm,),
        in_specs=[blk] * 3 + [pl.BlockSpec((tm, N_DEV), lambda i: (i, 0)), pl.BlockSpec((N_DEV, cdim), lambda i: (0, 0))],
        out_specs=[blk] * 4,
        out_shape=[jax.ShapeDtypeStruct((r, cdim), F32)] * 4,
        compiler_params=_cparams(dimension_semantics=("arbitrary",)),
    )(w, m, v, cact_t, dcols)


def _adamw_small(ws, ms, vs, ssum, rows):
    n = len(ws)

    def body(*refs):
        w_r, m_r, v_r = refs[0:n], refs[n:2 * n], refs[2 * n:3 * n]
        ss_ref, rows_ref = refs[3 * n], refs[3 * n + 1]
        g_r, d_r, m2_r, v2_r = (refs[3 * n + 2 + k * n:3 * n + 2 + (k + 1) * n] for k in range(4))
        loss_ref = refs[7 * n + 2]
        j = 2 * lax.axis_index("x") + lax.axis_index("y")
        rsum = rows_ref[0]
        for d in range(1, N_DEV):
            rsum = rsum + rows_ref[d]
        taps = []
        for t in range(CONV_TAPS):
            row = ss_ref[t // 2:t // 2 + 1, :]
            c0 = CONV_W * (t % 2)
            pick = row[:, c0:c0 + 128]
            for k in range(1, N_CHIPS):
                pick = jnp.where(j == k, row[:, c0 + 128 * k:c0 + 128 * (k + 1)], pick)
            taps.append(pick)
        grads = [jnp.concatenate([rsum[2:3], rsum[3:4], rsum[0:1]], axis=1), rsum[4:5],
                 ss_ref[17:18, 512:512 + HEAD_DIM], ss_ref[17:18, 640:640 + HEAD_DIM], ss_ref[17:18, 768:776],
                 None, ss_ref[16:17, 0:CONV_W], ss_ref[16:17, CONV_W:2 * CONV_W], ss_ref[17:18, 0:CONV_W]]
        for i in range(n):
            if grads[i] is None:
                for t in range(CONV_TAPS):
                    g_r[i][t:t + 1, :] = taps[t]
                g = g_r[i][...]
            else:
                g = grads[i]
                g_r[i][...] = g
            d_r[i][...], m2_r[i][...], v2_r[i][...] = _adamw_math(w_r[i][...], g, m_r[i][...], v_r[i][...])
        loss_ref[...] = (0.5 / D_MODEL) * jnp.sum(ss_ref[18:19, :], axis=1, keepdims=True)

    vm = pl.BlockSpec(memory_space=pltpu.VMEM)
    shapes = [jax.ShapeDtypeStruct(w.shape, F32) for w in ws]
    out = pl.pallas_call(
        body,
        name="adamw_small",
        in_specs=[vm] * (3 * n + 2),
        out_specs=[vm] * (4 * n + 1),
        out_shape=shapes * 4 + [jax.ShapeDtypeStruct((1, 1), F32)],
        compiler_params=_cparams(),
    )(*ws, *ms, *vs, ssum, rows)
    return out[0:n], out[n:2 * n], out[2 * n:3 * n], out[3 * n:4 * n], out[4 * n]


def _rope_tables(t):
    inv = ROPE_THETA ** (-jnp.arange(0, HEAD_DIM, 2, dtype=F32) / HEAD_DIM)
    ang = jnp.arange(t, dtype=F32)[:, None] * inv[None, :]
    cos, sin = jnp.cos(ang), jnp.sin(ang)
    return jnp.tile(cos, (1, 4)), jnp.tile(jnp.concatenate([-sin, sin], axis=1), (1, 2))


def kernel(x, c, w_ada, b_ada, norm_w, w_in, q_norm_w, k_norm_w, sinks, conv_w, conv_b, ln_w, ln_b, w_out, loss_target, m_w_ada, m_b_ada, m_norm_w, m_w_in, m_q_norm_w, m_k_norm_w, m_sinks, m_conv_w, m_conv_b, m_ln_w, m_ln_b, m_w_out, v_w_ada, v_b_ada, v_norm_w, v_w_in, v_q_norm_w, v_k_norm_w, v_sinks, v_conv_w, v_conv_b, v_ln_w, v_ln_b, v_w_out):
    xi, yi = lax.axis_index("x"), lax.axis_index("y")
    j = 2 * xi + yi
    x2, tgt = x[0], loss_target[0]
    t = x2.shape[0]

    wt_s, mt_s, vt_s = w_in[0].T, m_w_in[0].T, v_w_in[0].T
    cw_pad = jnp.pad(conv_w[0], ((0, 1), (0, 0)))

    q_raw, kv_raw, ga, ua, ug, gb, h, w_full, call, ada4 = _in_proj_gather(
        x2, wt_s.reshape(2, IN_HALF, D_MODEL), c, w_ada[0], b_ada, norm_w)
    ada = ada4.reshape(1, 3 * D_MODEL)
    s1, gate = 1.0 + ada[:, D_MODEL:2 * D_MODEL], ada[:, 2 * D_MODEL:]

    cos_f, sin_s = _rope_tables(t)
    qw2, kw2 = jnp.tile(q_norm_w, (1, 2)), jnp.tile(k_norm_w, (1, 2))

    o, mix_a, wo4, cw4 = _attn_fwd(q_raw, kv_raw, ga, qw2, kw2, sinks, cos_f, sin_s,
                                   w_out[0].reshape(2, OUT_HALF, D_MODEL), cw_pad)
    w_out_full = wo4.reshape(D_MODEL, D_MODEL)
    cw_full = jnp.concatenate([cw4[i] for i in range(N_CHIPS)], axis=1)
    cz, mix_b = _conv_fwd(ua, ug, gb, cw_full, conv_b, ln_w, ln_b)
    dout, dmix_a, dmix_b, gwo_bf, red_o = _out_proj(mix_a, mix_b, x2, tgt, gate, w_out_full)

    dq, dkv, dga, sm_a, gwo = _attn_bwd(q_raw, kv_raw, ga, o, dmix_a, qw2, kw2, sinks, cos_f, sin_s,
                                        gwo_bf.reshape(N_CHIPS, 2, OUT_HALF, D_MODEL))
    dua, dug, dgb, dcw, dvec = _conv_bwd(ua, ug, gb, cz, dmix_b, cw_full, ln_w, ln_b)
    dparts = (dq, dkv, dga, dua, dug, dgb)

    grad_x, gw, ssum, rows = _in_proj_bwd(dparts, h, x2, dout, s1, norm_w, w_full, dcw, dvec, sm_a, red_o)

    gt_w_in = gw.reshape(2 * IN_HALF, D_MODEL)
    g_w_out = gwo.reshape(D_MODEL // N_CHIPS, D_MODEL)
    d_ada_all = jnp.concatenate([rows[:, 2], rows[:, 3], rows[:, 0]], axis=1)
    dcols = lax.dynamic_slice(d_ada_all, (0, ADA_SHARD * j), (N_DEV, ADA_SHARD))
    cact_t = jax.nn.silu(call.reshape(N_DEV, D_MODEL)).T

    g_w_ada, d_w_ada, nm_w_ada, nv_w_ada = _adamw_ada(w_ada[0], m_w_ada[0], v_w_ada[0], cact_t, dcols)
    gt_w_in, dt_w_in, nmt_w_in, nvt_w_in, grad_x = _adamw("adamw_w_in", wt_s, gt_w_in, mt_s, vt_s, 176, through=grad_x)
    g_w_in, d_w_in, nm_w_in, nv_w_in = gt_w_in.T, dt_w_in.T, nmt_w_in.T, nvt_w_in.T
    g_w_out, d_w_out, nm_w_out, nv_w_out = _adamw("adamw_w_out", w_out[0], g_w_out, m_w_out[0], v_w_out[0], 128)
    ws = [b_ada, norm_w, q_norm_w, k_norm_w, sinks, conv_w[0], conv_b, ln_w, ln_b]
    ms = [m_b_ada, m_norm_w, m_q_norm_w, m_k_norm_w, m_sinks, m_conv_w[0], m_conv_b, m_ln_w, m_ln_b]
    vs = [v_b_ada, v_norm_w, v_q_norm_w, v_k_norm_w, v_sinks, v_conv_w[0], v_conv_b, v_ln_w, v_ln_b]
    gs, ds, nms, nvs, loss11 = _adamw_small(ws, ms, vs, ssum, rows)
    loss = loss11[0, 0]

    def order(ada_v, in_v, out_v, sm):
        b, nw_, qw_, kw_, sk_, cw_, cb_, lw_, lb_ = sm
        return [ada_v[None], b, nw_, in_v[None], qw_, kw_, sk_, cw_[None], cb_, lw_, lb_, out_v[None]]

    grads = order(g_w_ada, g_w_in, g_w_out, gs)
    deltas = order(d_w_ada, d_w_in, d_w_out, ds)
    new_m = order(nm_w_ada, nm_w_in, nm_w_out, nms)
    new_v = order(nv_w_ada, nv_w_in, nv_w_out, nvs)
    return (loss, grad_x[None], *grads, *deltas, *new_m, *new_v)
```

```python
import functools

import jax
import jax.numpy as jnp
from jax import lax
from jax.experimental import pallas as pl
from jax.experimental.pallas import tpu as pltpu

F32 = jnp.float32
BF16 = jnp.bfloat16

D_MODEL = 1024
ATTN_W = 512
KV_W = 128
CONV_W = 512
IN_W = 2816
HEAD_DIM = 64
CONV_TAPS = 31
QBLK = 128
EPS = 1e-6
ROPE_THETA = 10000.0

ADAM_LR = 0.001
ADAM_B1 = 0.9
ADAM_B2 = 0.999
ADAM_EPS = 1e-08
ADAM_WD = 0.01
ADAM_STEP = 10

N_CHIPS = 4
N_DEV = 8
IN_HALF = IN_W // N_CHIPS // 2
OUT_HALF = D_MODEL // N_CHIPS // 2
ADA_SHARD = 3 * D_MODEL // N_CHIPS

VMEM_LIMIT = 56 * 1024 * 1024
CONV_PAD = 32


def _cparams(**kw):
    return pltpu.CompilerParams(vmem_limit_bytes=VMEM_LIMIT, **kw)


def _sigmoid(v):
    return 1.0 / (1.0 + jnp.exp(-v))


def _silu(v):
    return v * _sigmoid(v)


def _dsilu(v):
    s = _sigmoid(v)
    return s * (1.0 + v * (1.0 - s))


def _lane(shape):
    return lax.broadcasted_iota(jnp.int32, shape, len(shape) - 1)


PUT_ROWS = 512


def _fetch(hbm_refs, vmem_refs, sem):
    cps = [pltpu.make_async_copy(h, v, sem.at[i]) for i, (h, v) in enumerate(zip(hbm_refs, vmem_refs))]
    for cp in cps:
        cp.start()
    return cps


def _put(vmem_ref, hbm_ref, sem, m):
    r = pl.ds(pl.multiple_of(m * PUT_ROWS, PUT_ROWS), PUT_ROWS)
    return pltpu.make_async_copy(vmem_ref.at[r], hbm_ref.at[r], sem.at[m])


def _put_all(pairs, sems, m):
    for (v, h), sem in zip(pairs, sems):
        _put(v, h, sem, m).start()


def _put_wait(pairs, sems, n):
    for (v, h), sem in zip(pairs, sems):
        for m in range(n):
            _put(v, h, sem, m).wait()


def _head_mean(s, left):
    sl = jnp.sum(jnp.where(left, s, 0.0), axis=-1, keepdims=True)
    sr = jnp.sum(jnp.where(left, 0.0, s), axis=-1, keepdims=True)
    return jnp.where(left, sl, sr) * (1.0 / HEAD_DIM)


def _rot(v, first):
    return jnp.where(first, pltpu.roll(v, 96, 1), pltpu.roll(v, 32, 1))


def _norm_rope(v, w, cos, sin_s, left, first):
    r = lax.rsqrt(_head_mean(v * v, left) + EPS)
    xh = v * r
    n = xh * w
    return n * cos + _rot(n, first) * sin_s, xh, r


def _norm_rope_bwd(d, xh, r, w, cos, sin_s, left, first):
    dn = d * cos - _rot(d, first) * sin_s
    dw = jnp.sum(dn * xh, axis=0, keepdims=True)
    dxh = dn * w
    return r * (dxh - xh * _head_mean(dxh * xh, left)), dw


def _dup_heads(v, left):
    sw = pltpu.roll(v, 64, 1)
    return jnp.where(left, v, sw), jnp.where(left, sw, v)


def _prep_kv(kv_ref, kw_ref, cos_ref, sin_ref, ka_ref, va_ref, t):
    ch = 256
    for g in range(2):
        ka_ref[g, 0:QBLK, :] = jnp.zeros((QBLK, 128), BF16)
        va_ref[g, 0:QBLK, :] = jnp.zeros((QBLK, 128), BF16)

    def chunk(i, carry):
        r0 = pl.multiple_of(i * ch, ch)
        left = _lane((ch, 128)) < 64
        first = (_lane((ch, 128)) % 64) < 32
        k = kv_ref[pl.ds(r0, ch), 0:128]
        v = kv_ref[pl.ds(r0, ch), 128:256]
        kr, _, _ = _norm_rope(k, kw_ref[...], cos_ref[pl.ds(r0, ch), :], sin_ref[pl.ds(r0, ch), :], left, first)
        k0, k1 = _dup_heads(kr, left)
        v0, v1 = _dup_heads(v, left)
        ka_ref[0, pl.ds(QBLK + r0, ch), :] = k0.astype(BF16)
        ka_ref[1, pl.ds(QBLK + r0, ch), :] = k1.astype(BF16)
        va_ref[0, pl.ds(QBLK + r0, ch), :] = v0.astype(BF16)
        va_ref[1, pl.ds(QBLK + r0, ch), :] = v1.astype(BF16)
        return carry

    lax.fori_loop(0, t // ch, chunk, 0)


def _band_mask(n):
    qi = lax.broadcasted_iota(jnp.int32, (2 * QBLK, 2 * QBLK), 0) % QBLK
    kj = lax.broadcasted_iota(jnp.int32, (2 * QBLK, 2 * QBLK), 1)
    local = (kj > qi) & (kj <= qi + QBLK)
    return local & ((n > 0) | (kj >= QBLK))


def _softmax_pair(s, mask, sink0, sink1):
    row = lax.broadcasted_iota(jnp.int32, (2 * QBLK, 1), 0)
    sink = jnp.where(row < QBLK, sink0, sink1)
    s = jnp.where(mask, s, -jnp.inf)
    m = jnp.maximum(jnp.max(s, axis=-1, keepdims=True), sink)
    e = jnp.exp(s - m)
    es = jnp.exp(sink - m)
    inv = 1.0 / (jnp.sum(e, axis=-1, keepdims=True) + es)
    return e * inv, es * inv


def _stack_heads(v, left):
    return jnp.concatenate([jnp.where(left, v, 0.0), jnp.where(left, 0.0, v)], axis=0)


def _attn_fwd(q_raw, kv_raw, ga, qw2, kw2, sinks, cos_f, sin_s, wo, cw):
    t = q_raw.shape[0]
    nblk = t // QBLK
    per_put = PUT_ROWS // QBLK

    def body(q_hbm, kv_ref, ga_hbm, qw_ref, kw_ref, sk_ref, cos_hbm, sin_hbm, wo_ref, cw_ref,
             o_hbm, mix_hbm, wo4_ref, cw4_ref, ka_ref, va_ref, q_ref, ga_ref, o_ref, mix_ref, cos_ref, sin_ref,
             isem, osem0, osem1, ssem, rsem):
        loads = _fetch((cos_hbm, sin_hbm, q_hbm, ga_hbm), (cos_ref, sin_ref, q_ref, ga_ref), isem)
        outs, osems = ((o_ref, o_hbm), (mix_ref, mix_hbm)), (osem0, osem1)
        x, y, c, chips = _place()
        j = 2 * x + y
        sib = (x, y, 1 - c)
        idx = [2 * cx + cy for cx, cy in chips]
        rc = functools.partial(_remote, ssem, rsem)
        wo4_ref[j] = wo_ref[...].astype(BF16)
        cw4_ref[j] = cw_ref[...]
        sends = []
        for k, chip in enumerate(chips):
            sends.append(rc(k, wo4_ref.at[j, c], wo4_ref.at[j, c], (*chip, c)))
            sends.append(rc(6 + k, cw4_ref.at[j], cw4_ref.at[j], (*chip, c)))
        for cp in sends:
            cp.start()

        loads[0].wait()
        loads[1].wait()
        _prep_kv(kv_ref, kw_ref, cos_ref, sin_ref, ka_ref, va_ref, t)
        loads[2].wait()
        loads[3].wait()

        def blk(n, carry):
            r0 = pl.multiple_of(n * QBLK, QBLK)
            left = _lane((QBLK, 128)) < 64
            first = (_lane((QBLK, 128)) % 64) < 32
            cos = cos_ref[pl.ds(r0, QBLK), :]
            sin = sin_ref[pl.ds(r0, QBLK), :]
            mask = _band_mask(n)
            scores = []
            for p in range(4):
                lanes = slice(p * 128, (p + 1) * 128)
                qr, _, _ = _norm_rope(q_ref[pl.ds(r0, QBLK), lanes], qw_ref[...], cos, sin, left, first)
                q2 = _stack_heads(qr * 0.125, left).astype(BF16)
                scores.append(lax.dot_general(q2, ka_ref[p // 2, pl.ds(r0, 2 * QBLK), :], (((1,), (1,)), ((), ())),
                                              preferred_element_type=F32))
            probs = [_softmax_pair(scores[p], mask, sk_ref[0, 2 * p], sk_ref[0, 2 * p + 1])[0].astype(BF16)
                     for p in range(4)]
            for p in range(4):
                lanes = slice(p * 128, (p + 1) * 128)
                o2 = jnp.dot(probs[p], va_ref[p // 2, pl.ds(r0, 2 * QBLK), :], preferred_element_type=F32)
                o = jnp.where(left, o2[0:QBLK], o2[QBLK:2 * QBLK])
                o_ref[pl.ds(r0, QBLK), lanes] = o.astype(BF16)
                mix_ref[pl.ds(r0, QBLK), lanes] = (o * _silu(ga_ref[pl.ds(r0, QBLK), lanes])).astype(BF16)

            @pl.when(n % per_put == per_put - 1)
            def _():
                _put_all(outs, osems, n // per_put)

            return carry

        lax.fori_loop(0, nblk, blk, 0)
        _put_wait(outs, osems, t // PUT_ROWS)

        passed = []
        for k, chip in enumerate(chips):
            jk = idx[k]
            rc(k, wo4_ref.at[jk, c], wo4_ref.at[jk, c], sib).wait_recv()
            passed.append(rc(3 + k, wo4_ref.at[jk, c], wo4_ref.at[jk, c], sib))
            passed[-1].start()
        for k, chip in enumerate(chips):
            jk = idx[k]
            rc(3 + k, wo4_ref.at[jk, 1 - c], wo4_ref.at[jk, 1 - c], sib).wait_recv()
            rc(6 + k, cw4_ref.at[jk], cw4_ref.at[jk], sib).wait_recv()
        for cp in sends + passed:
            cp.wait_send()

    vm = pl.BlockSpec(memory_space=pltpu.VMEM)
    hbm = pl.BlockSpec(memory_space=pl.ANY)
    n_sem = 9
    return pl.pallas_call(
        body,
        name="attn_fwd",
        in_specs=[hbm, vm, hbm, vm, vm, pl.BlockSpec(memory_space=pltpu.SMEM), hbm, hbm, vm, vm],
        out_specs=[hbm, hbm, vm, vm],
        out_shape=[jax.ShapeDtypeStruct((t, ATTN_W), BF16), jax.ShapeDtypeStruct((t, ATTN_W), BF16),
                   jax.ShapeDtypeStruct((N_CHIPS, 2, OUT_HALF, D_MODEL), BF16),
                   jax.ShapeDtypeStruct((N_CHIPS, 32, 128), F32)],
        scratch_shapes=[pltpu.VMEM((2, t + QBLK, 128), BF16), pltpu.VMEM((2, t + QBLK, 128), BF16),
                        pltpu.VMEM((t, ATTN_W), F32), pltpu.VMEM((t, ATTN_W), F32),
                        pltpu.VMEM((t, ATTN_W), BF16), pltpu.VMEM((t, ATTN_W), BF16),
                        pltpu.VMEM((t, 128), F32), pltpu.VMEM((t, 128), F32),
                        pltpu.SemaphoreType.DMA((4,)), pltpu.SemaphoreType.DMA((t // PUT_ROWS,)),
                        pltpu.SemaphoreType.DMA((t // PUT_ROWS,)),
                        pltpu.SemaphoreType.DMA((n_sem,)), pltpu.SemaphoreType.DMA((n_sem,))],
        compiler_params=_cparams(),
    )(q_raw, kv_raw, ga, qw2, kw2, sinks, cos_f, sin_s, wo, cw)


def _attn_bwd(q_raw, kv_raw, ga, o, dmix, qw2, kw2, sinks, cos_f, sin_s, go):
    t = q_raw.shape[0]
    nblk = t // QBLK
    per_put = PUT_ROWS // QBLK

    def body(q_hbm, kv_ref, ga_hbm, o_hbm, dm_hbm, qw_ref, kw_ref, sk_ref, cos_hbm, sin_hbm, go_ref,
             dq_hbm, dkv_ref, dga_hbm, sm_ref, gwo_ref, ka_ref, va_ref, dka_ref, dva_ref,
             sibo_ref, outo_ref, ino_ref, q_ref, ga_ref, o_ref, dm_ref, dq_ref, dga_ref, cos_ref, sin_ref,
             isem, osem0, osem1, ssem, rsem):
        loads = _fetch((cos_hbm, sin_hbm, q_hbm, ga_hbm, o_hbm, dm_hbm), (cos_ref, sin_ref, q_ref, ga_ref, o_ref, dm_ref), isem)
        outs, osems = ((dq_ref, dq_hbm), (dga_ref, dga_hbm)), (osem0, osem1)
        x, y, c, chips = _place()
        sib = (x, y, 1 - c)
        rc = functools.partial(_remote, ssem, rsem)
        theirs, mine = go_ref.at[:, 1 - c], go_ref.at[:, c]
        sends = [_rs_to_sibling(rc, 0, theirs, sibo_ref, sib)]
        loads[0].wait()
        loads[1].wait()
        _prep_kv(kv_ref, kw_ref, cos_ref, sin_ref, ka_ref, va_ref, t)
        dka_ref[...] = jnp.zeros_like(dka_ref)
        dva_ref[...] = jnp.zeros_like(dva_ref)
        sends += _rs_trade(rc, 0, theirs, mine, sibo_ref, outo_ref, ino_ref, OUT_HALF, c, sib, chips)
        for cp in loads[2:]:
            cp.wait()

        def blk(n, carry):
            dqw, dsk = carry
            r0 = pl.multiple_of(n * QBLK, QBLK)
            left = _lane((QBLK, 128)) < 64
            first = (_lane((QBLK, 128)) % 64) < 32
            cos = cos_ref[pl.ds(r0, QBLK), :]
            sin = sin_ref[pl.ds(r0, QBLK), :]
            mask = _band_mask(n)
            row = lax.broadcasted_iota(jnp.int32, (2 * QBLK, 1), 0)
            rows = pl.ds(r0, QBLK)
            win = pl.ds(r0, 2 * QBLK)
            lane_of = [slice(p * 128, (p + 1) * 128) for p in range(4)]
            for grp in ((0, 1), (2, 3)):
                qn = {p: _norm_rope(q_ref[rows, lane_of[p]], qw_ref[...], cos, sin, left, first) for p in grp}
                q2 = {p: _stack_heads(qn[p][0] * 0.125, left).astype(BF16) for p in grp}
                sc = {p: lax.dot_general(q2[p], ka_ref[p // 2, win, :], (((1,), (1,)), ((), ())),
                                         preferred_element_type=F32) for p in grp}
                do2 = {}
                for p in grp:
                    gav = ga_ref[rows, lane_of[p]]
                    dmv = dm_ref[rows, lane_of[p]].astype(F32)
                    dga_ref[rows, lane_of[p]] = (dmv * o_ref[rows, lane_of[p]].astype(F32) * _dsilu(gav)).astype(BF16)
                    do2[p] = _stack_heads(dmv * _silu(gav), left).astype(BF16)
                dpm = {p: lax.dot_general(do2[p], va_ref[p // 2, win, :], (((1,), (1,)), ((), ())),
                                          preferred_element_type=F32) for p in grp}
                sm = {p: _softmax_pair(sc[p], mask, sk_ref[0, 2 * p], sk_ref[0, 2 * p + 1]) for p in grp}
                dsl = {}
                for p in grp:
                    pm, ps = sm[p]
                    delta = jnp.sum(pm * dpm[p], axis=-1, keepdims=True)
                    dsl[p] = (pm * (dpm[p] - delta)).astype(BF16)
                    pd = ps * delta
                    d0 = jnp.sum(jnp.where(row < QBLK, pd, 0.0), axis=0, keepdims=True)
                    d1 = jnp.sum(jnp.where(row < QBLK, 0.0, pd), axis=0, keepdims=True)
                    l8 = _lane((1, 128))
                    dsk = dsk - jnp.where(l8 == 2 * p, d0, 0.0) - jnp.where(l8 == 2 * p + 1, d1, 0.0)
                for p in grp:
                    g = p // 2
                    dva_ref[g, win, :] += lax.dot_general(sm[p][0].astype(BF16), do2[p], (((0,), (0,)), ((), ())),
                                                          preferred_element_type=F32)
                    dka_ref[g, win, :] += lax.dot_general(dsl[p], q2[p], (((0,), (0,)), ((), ())),
                                                          preferred_element_type=F32)
                for p in grp:
                    dq2 = jnp.dot(dsl[p], ka_ref[p // 2, win, :], preferred_element_type=F32)
                    dqr = jnp.where(left, dq2[0:QBLK], dq2[QBLK:2 * QBLK]) * 0.125
                    dq, dw = _norm_rope_bwd(dqr, qn[p][1], qn[p][2], qw_ref[...], cos, sin, left, first)
                    dq_ref[rows, lane_of[p]] = dq.astype(BF16)
                    dqw = dqw + dw

            @pl.when(n % per_put == per_put - 1)
            def _():
                _put_all(outs, osems, n // per_put)

            return dqw, dsk

        zero = jnp.zeros((1, 128), F32)
        dqw, dsk = lax.fori_loop(0, nblk, blk, (zero, zero))

        ch = 256

        def chunk(i, dkw):
            r0 = pl.multiple_of(i * ch, ch)
            left = _lane((ch, 128)) < 64
            first = (_lane((ch, 128)) % 64) < 32
            rows = pl.ds(r0, ch)
            prow = pl.ds(QBLK + r0, ch)

            def fold(ref):
                a0 = ref[0, prow, :]
                a1 = ref[1, prow, :]
                return jnp.where(left, a0 + pltpu.roll(a0, 64, 1), a1 + pltpu.roll(a1, 64, 1))

            cos = cos_ref[rows, :]
            sin = sin_ref[rows, :]
            _, xh, r = _norm_rope(kv_ref[rows, 0:128], kw_ref[...], cos, sin, left, first)
            dk, dw = _norm_rope_bwd(fold(dka_ref), xh, r, kw_ref[...], cos, sin, left, first)
            dkv_ref[rows, 0:128] = dk.astype(BF16)
            dkv_ref[rows, 128:256] = fold(dva_ref).astype(BF16)
            return dkw + dw

        dkw = lax.fori_loop(0, t // ch, chunk, zero)
        sm_ref[...] = jnp.zeros((8, 128), F32)
        sm_ref[0:1, :] = dqw + pltpu.roll(dqw, 64, 1)
        sm_ref[1:2, :] = dkw + pltpu.roll(dkw, 64, 1)
        sm_ref[2:3, :] = dsk

        j = 2 * x + y
        sends.append(_rs_total(rc, 0, mine, sibo_ref, outo_ref, ino_ref, gwo_ref, OUT_HALF, j, c, sib))
        _rs_done(rc, 0, gwo_ref, c, sib)
        for cp in sends:
            cp.wait_send()
        _put_wait(outs, osems, t // PUT_ROWS)

    vm = pl.BlockSpec(memory_space=pltpu.VMEM)
    hbm = pl.BlockSpec(memory_space=pl.ANY)
    return pl.pallas_call(
        body,
        name="attn_bwd",
        in_specs=[hbm, vm, hbm, hbm, hbm, vm, vm, pl.BlockSpec(memory_space=pltpu.SMEM), hbm, hbm, vm],
        out_specs=[hbm, vm, hbm, vm, vm],
        out_shape=[jax.ShapeDtypeStruct((t, ATTN_W), BF16), jax.ShapeDtypeStruct((t, 2 * KV_W), BF16),
                   jax.ShapeDtypeStruct((t, ATTN_W), BF16), jax.ShapeDtypeStruct((8, 128), F32),
                   jax.ShapeDtypeStruct((2, OUT_HALF, D_MODEL), F32)],
        scratch_shapes=[pltpu.VMEM((2, t + QBLK, 128), BF16), pltpu.VMEM((2, t + QBLK, 128), BF16),
                        pltpu.VMEM((2, t + QBLK, 128), F32), pltpu.VMEM((2, t + QBLK, 128), F32)]
        + _rs_scratch(OUT_HALF)
        + [pltpu.VMEM((t, ATTN_W), F32), pltpu.VMEM((t, ATTN_W), F32), pltpu.VMEM((t, ATTN_W), BF16),
           pltpu.VMEM((t, ATTN_W), BF16), pltpu.VMEM((t, ATTN_W), BF16), pltpu.VMEM((t, ATTN_W), BF16),
           pltpu.VMEM((t, 128), F32), pltpu.VMEM((t, 128), F32),
           pltpu.SemaphoreType.DMA((6,)), pltpu.SemaphoreType.DMA((t // PUT_ROWS,)), pltpu.SemaphoreType.DMA((t // PUT_ROWS,)),
           pltpu.SemaphoreType.DMA((RS_SEMS,)), pltpu.SemaphoreType.DMA((RS_SEMS,))],
        compiler_params=_cparams(),
    )(q_raw, kv_raw, ga, o, dmix, qw2, kw2, sinks, cos_f, sin_s, go)


CONV_CH = 256
CONV_SUB = 128
CONV_ACCS = 1


def _shifted_windows(src_ref, r0, sh_ref):
    rows = CONV_CH + CONV_PAD
    win = src_ref[pl.ds(r0, rows), :]
    for b in range(8):
        sh = win if b == 0 else pltpu.roll(win, rows - b, 0)
        for c in range(CONV_W // 128):
            sh_ref[b, c] = sh[:, c * 128:(c + 1) * 128]


def _conv_fwd(ua, ug, gb, cw, cb, lw, lb):
    t = ua.shape[0]

    def body(ua_hbm, ug_hbm, gb_hbm, cw_ref, cb_ref, lw_ref, lb_ref, cz_hbm, mix_hbm, zp_ref, sh_ref,
             ua_ref, ug_ref, gb_ref, cz_ref, mix_ref, isem, osem0, osem1):
        loads = _fetch((ua_hbm, ug_hbm, gb_hbm), (ua_ref, ug_ref, gb_ref), isem)
        outs, osems = ((cz_ref, cz_hbm), (mix_ref, mix_hbm)), (osem0, osem1)
        per_put = PUT_ROWS // CONV_CH
        zp_ref[0:CONV_PAD, :] = jnp.zeros((CONV_PAD, CONV_W), F32)
        loads[0].wait()
        loads[1].wait()

        def glu(i, carry):
            r0 = pl.multiple_of(i * CONV_CH, CONV_CH)
            rows = pl.ds(r0, CONV_CH)
            zp_ref[pl.ds(CONV_PAD + r0, CONV_CH), :] = ua_ref[rows, :] * _sigmoid(ug_ref[rows, :])
            return carry

        lax.fori_loop(0, t // CONV_CH, glu, 0)
        loads[2].wait()

        def chunk(i, carry):
            r0 = pl.multiple_of(i * CONV_CH, CONV_CH)
            _shifted_windows(zp_ref, r0, sh_ref)
            for c in range(CONV_W // 128):
                lanes = slice(c * 128, (c + 1) * 128)

                def sub(k, carry2):
                    b0 = pl.multiple_of(k * CONV_SUB, CONV_SUB)
                    acc = [jnp.broadcast_to(cb_ref[0:1, lanes], (CONV_SUB, 128))] + [None] * (CONV_ACCS - 1)
                    for j in range(CONV_TAPS):
                        off = j + CONV_PAD - (CONV_TAPS - 1)
                        term = sh_ref[off % 8, c, pl.ds(b0 + 8 * (off // 8), CONV_SUB), :] * cw_ref[j:j + 1, lanes]
                        acc[j % CONV_ACCS] = term if acc[j % CONV_ACCS] is None else acc[j % CONV_ACCS] + term
                    cz_ref[pl.ds(r0 + b0, CONV_SUB), lanes] = functools.reduce(lambda a, b: a + b, acc)
                    return carry2

                lax.fori_loop(0, CONV_CH // CONV_SUB, sub, 0)
            rows = pl.ds(r0, CONV_CH)
            cz = cz_ref[rows, :]
            mu = jnp.mean(cz, axis=-1, keepdims=True)
            xc = cz - mu
            rs = lax.rsqrt(jnp.mean(xc * xc, axis=-1, keepdims=True) + EPS)
            ln = xc * rs * lw_ref[...] + lb_ref[...]
            mix_ref[rows, :] = (_silu(ln) * _silu(gb_ref[rows, :])).astype(BF16)

            @pl.when(i % per_put == per_put - 1)
            def _():
                _put_all(outs, osems, i // per_put)

            return carry

        lax.fori_loop(0, t // CONV_CH, chunk, 0)
        _put_wait(outs, osems, t // PUT_ROWS)

    vm = pl.BlockSpec(memory_space=pltpu.VMEM)
    hbm = pl.BlockSpec(memory_space=pl.ANY)
    nput = t // PUT_ROWS
    return pl.pallas_call(
        body,
        name="conv_fwd",
        in_specs=[hbm] * 3 + [vm] * 4,
        out_specs=[hbm, hbm],
        out_shape=[jax.ShapeDtypeStruct((t, CONV_W), F32), jax.ShapeDtypeStruct((t, CONV_W), BF16)],
        scratch_shapes=[pltpu.VMEM((t + CONV_PAD, CONV_W), F32),
                        pltpu.VMEM((8, CONV_W // 128, CONV_CH + CONV_PAD, 128), F32),
                        pltpu.VMEM((t, CONV_W), F32), pltpu.VMEM((t, CONV_W), F32), pltpu.VMEM((t, CONV_W), F32),
                        pltpu.VMEM((t, CONV_W), F32), pltpu.VMEM((t, CONV_W), BF16),
                        pltpu.SemaphoreType.DMA((3,)), pltpu.SemaphoreType.DMA((nput,)), pltpu.SemaphoreType.DMA((nput,))],
        compiler_params=_cparams(),
    )(ua, ug, gb, cw, cb, lw, lb)


def _conv_bwd(ua, ug, gb, cz, dmix, cw, lw, lb):
    t = ua.shape[0]

    def body(ua_hbm, ug_hbm, gb_hbm, cz_hbm, dm_hbm, cw_ref, lw_ref, lb_ref,
             dua_hbm, dug_hbm, dgb_hbm, dcw_ref, dvec_ref, zp_ref, dp_ref, sh_ref, wacc_ref,
             ua_ref, ug_ref, gb_ref, cz_ref, dm_ref, dua_ref, dug_ref, dgb_ref, isem, osem0, osem1, osem2):
        loads = _fetch((ua_hbm, ug_hbm, gb_hbm, cz_hbm, dm_hbm), (ua_ref, ug_ref, gb_ref, cz_ref, dm_ref), isem)
        per_put = PUT_ROWS // CONV_CH
        zp_ref[0:CONV_PAD, :] = jnp.zeros((CONV_PAD, CONV_W), F32)
        dp_ref[t:t + CONV_PAD, :] = jnp.zeros((CONV_PAD, CONV_W), F32)
        wacc_ref[...] = jnp.zeros_like(wacc_ref)
        for cp in loads:
            cp.wait()

        def pointwise(i, carry):
            dcb, dlw, dlb = carry
            r0 = pl.multiple_of(i * CONV_CH, CONV_CH)
            rows = pl.ds(r0, CONV_CH)
            zp_ref[pl.ds(CONV_PAD + r0, CONV_CH), :] = ua_ref[rows, :] * _sigmoid(ug_ref[rows, :])
            cz = cz_ref[rows, :]
            mu = jnp.mean(cz, axis=-1, keepdims=True)
            xc = cz - mu
            rs = lax.rsqrt(jnp.mean(xc * xc, axis=-1, keepdims=True) + EPS)
            xh = xc * rs
            ln = xh * lw_ref[...] + lb_ref[...]
            gbv = gb_ref[rows, :]
            dy = dm_ref[rows, :].astype(F32)
            dgb_ref[rows, :] = (dy * _silu(ln) * _dsilu(gbv)).astype(BF16)
            dl = dy * _silu(gbv) * _dsilu(ln)
            dxh = dl * lw_ref[...]
            dcz = rs * (dxh - jnp.mean(dxh, axis=-1, keepdims=True)
                        - xh * jnp.mean(dxh * xh, axis=-1, keepdims=True))
            dp_ref[rows, :] = dcz

            @pl.when(i % per_put == per_put - 1)
            def _():
                _put(dgb_ref, dgb_hbm, osem2, i // per_put).start()

            return (dcb + jnp.sum(dcz, axis=0, keepdims=True),
                    dlw + jnp.sum(dl * xh, axis=0, keepdims=True),
                    dlb + jnp.sum(dl, axis=0, keepdims=True))

        zero = jnp.zeros((1, CONV_W), F32)
        dcb, dlw, dlb = lax.fori_loop(0, t // CONV_CH, pointwise, (zero, zero, zero))
        dvec_ref[...] = jnp.zeros((8, CONV_W), F32)
        dvec_ref[0:1, :] = dcb
        dvec_ref[1:2, :] = dlw
        dvec_ref[2:3, :] = dlb

        def chunk(i, carry):
            r0 = pl.multiple_of(i * CONV_CH, CONV_CH)
            _shifted_windows(dp_ref, r0, sh_ref)
            for c in range(CONV_W // 128):
                lanes = slice(c * 128, (c + 1) * 128)

                def sub(k, carry2):
                    b0 = pl.multiple_of(k * CONV_SUB, CONV_SUB)
                    acc = [None] * CONV_ACCS
                    for j in range(CONV_TAPS):
                        off = CONV_TAPS - 1 - j
                        term = sh_ref[off % 8, c, pl.ds(b0 + 8 * (off // 8), CONV_SUB), :] * cw_ref[j:j + 1, lanes]
                        acc[j % CONV_ACCS] = term if acc[j % CONV_ACCS] is None else acc[j % CONV_ACCS] + term
                    acc = functools.reduce(lambda a, b: a + b, acc)
                    rr = pl.ds(r0 + b0, CONV_SUB)
                    sg = _sigmoid(ug_ref[rr, lanes])
                    dua_ref[rr, lanes] = (acc * sg).astype(BF16)
                    dug_ref[rr, lanes] = (acc * ua_ref[rr, lanes] * sg * (1.0 - sg)).astype(BF16)
                    return carry2

                lax.fori_loop(0, CONV_CH // CONV_SUB, sub, 0)
            _shifted_windows(zp_ref, r0, sh_ref)
            for c in range(CONV_W // 128):
                lanes = slice(c * 128, (c + 1) * 128)

                def subw(k, carry2):
                    b0 = pl.multiple_of(k * CONV_SUB, CONV_SUB)
                    dcz = dp_ref[pl.ds(r0 + b0, CONV_SUB), lanes]
                    for j in range(CONV_TAPS):
                        off = j + CONV_PAD - (CONV_TAPS - 1)
                        pr = dcz * sh_ref[off % 8, c, pl.ds(b0 + 8 * (off // 8), CONV_SUB), :]
                        parts = [pr[8 * q:8 * (q + 1)] for q in range(CONV_SUB // 8)]
                        while len(parts) > 1:
                            parts = [a + b for a, b in zip(parts[0::2], parts[1::2])]
                        wacc_ref[8 * j:8 * (j + 1), lanes] += parts[0]
                    return carry2

                lax.fori_loop(0, CONV_CH // CONV_SUB, subw, 0)

            @pl.when(i % per_put == per_put - 1)
            def _():
                _put_all(((dua_ref, dua_hbm), (dug_ref, dug_hbm)), (osem0, osem1), i // per_put)

            return carry

        lax.fori_loop(0, t // CONV_CH, chunk, 0)
        _put_wait(((dua_ref, dua_hbm), (dug_ref, dug_hbm), (dgb_ref, dgb_hbm)), (osem0, osem1, osem2), t // PUT_ROWS)
        dcw_ref[...] = jnp.zeros((16, 2 * CONV_W), F32)
        for j in range(CONV_TAPS):
            dcw_ref[j // 2:j // 2 + 1, CONV_W * (j % 2):CONV_W * (j % 2 + 1)] = jnp.sum(
                wacc_ref[8 * j:8 * (j + 1), :], axis=0, keepdims=True)

    vm = pl.BlockSpec(memory_space=pltpu.VMEM)
    hbm = pl.BlockSpec(memory_space=pl.ANY)
    return pl.pallas_call(
        body,
        name="conv_bwd",
        in_specs=[hbm] * 5 + [vm] * 3,
        out_specs=[hbm] * 3 + [vm] * 2,
        out_shape=[jax.ShapeDtypeStruct((t, CONV_W), BF16)] * 3
        + [jax.ShapeDtypeStruct((16, 2 * CONV_W), F32), jax.ShapeDtypeStruct((8, CONV_W), F32)],
        scratch_shapes=[pltpu.VMEM((t + CONV_PAD, CONV_W), F32), pltpu.VMEM((t + CONV_PAD, CONV_W), F32),
                        pltpu.VMEM((8, CONV_W // 128, CONV_CH + CONV_PAD, 128), F32), pltpu.VMEM((8 * 32, CONV_W), F32)]
        + [pltpu.VMEM((t, CONV_W), F32)] * 4 + [pltpu.VMEM((t, CONV_W), BF16)] * 4
        + [pltpu.SemaphoreType.DMA((5,))] + [pltpu.SemaphoreType.DMA((t // PUT_ROWS,))] * 3,
        compiler_params=_cparams(),
    )(ua, ug, gb, cz, dmix, cw, lw, lb)


def _out_proj(mix_a, mix_b, x, tgt, gate, w_out):
    t = x.shape[0]
    tm = 512
    nstep = t // tm

    def body(ma_ref, mb_ref, x_ref, t_ref, g_ref, w_ref, dout_ref, dma_ref, dmb_ref, gw_ref, red_ref, acc_ref):
        i = pl.program_id(0)

        @pl.when(i == 0)
        def _():
            acc_ref[...] = jnp.zeros_like(acc_ref)
            red_ref[...] = jnp.zeros_like(red_ref)

        mix = jnp.concatenate([ma_ref[...], mb_ref[...]], axis=1)
        y = jnp.dot(mix, w_ref[...], preferred_element_type=F32)
        gate_v = g_ref[...]
        err = x_ref[...] + gate_v * y - t_ref[...]
        dout = err * (1.0 / D_MODEL)
        dout_ref[...] = dout
        red_ref[0:1, :] += jnp.sum(dout * y, axis=0, keepdims=True)
        red_ref[1:2, :] += jnp.sum(err * err, axis=0, keepdims=True)
        dy = (dout * gate_v).astype(BF16)
        dmix = lax.dot_general(dy, w_ref[...], (((1,), (1,)), ((), ())), preferred_element_type=F32)
        dma_ref[...] = dmix[:, 0:512].astype(BF16)
        dmb_ref[...] = dmix[:, 512:1024].astype(BF16)
        acc_ref[...] += lax.dot_general(mix, dy, (((0,), (0,)), ((), ())), preferred_element_type=F32)

        @pl.when(i == nstep - 1)
        def _():
            gw_ref[...] = acc_ref[...].astype(BF16)

    row = lambda w: pl.BlockSpec((tm, w), lambda i: (i, 0))
    const = lambda s: pl.BlockSpec(s, lambda i: (0, 0))
    return pl.pallas_call(
        body,
        name="out_proj",
        grid=(nstep,),
        in_specs=[row(512), row(512), row(D_MODEL), row(D_MODEL), const((1, D_MODEL)),
                  pl.BlockSpec((D_MODEL, D_MODEL), lambda i: (0, 0), pipeline_mode=pl.Buffered(1))],
        out_specs=[row(D_MODEL), row(512), row(512), const((D_MODEL, D_MODEL)), const((8, D_MODEL))],
        out_shape=[jax.ShapeDtypeStruct((t, D_MODEL), F32), jax.ShapeDtypeStruct((t, 512), BF16),
                   jax.ShapeDtypeStruct((t, 512), BF16), jax.ShapeDtypeStruct((D_MODEL, D_MODEL), BF16),
                   jax.ShapeDtypeStruct((8, D_MODEL), F32)],
        scratch_shapes=[pltpu.VMEM((D_MODEL, D_MODEL), F32)],
        compiler_params=_cparams(dimension_semantics=("arbitrary",)),
    )(mix_a, mix_b, x, tgt, gate, w_out)


DPROJ_WIDTHS = (512, 256, 512, 512, 512, 512)
DPROJ_STARTS = (0, 512, 768, 1280, 1792, 2304)
WIN_W = 768
WIN_START = (0, 640, 1408, 2048)
WIN_OFF = (0, 64, 0, 64)
N_GW = N_CHIPS


def _window_pieces(s):
    lo, hi = WIN_START[s], WIN_START[s] + WIN_W
    out = []
    for p, (st, w) in enumerate(zip(DPROJ_STARTS, DPROJ_WIDTHS)):
        a, b = max(lo, st), min(hi, st + w)
        if a < b:
            out.append((p, a - st, b - a, a - lo))
    return out


def _in_proj_bwd(dparts, h, x, dout, s1, nw, wt_full, dcw, dvec, sm_a, row0):
    t = x.shape[0]
    tm = 256
    nstep = N_GW + t // tm
    n_sem = 20
    rows0 = 32
    hs = rows0 // 2
    npart = len(DPROJ_WIDTHS)

    def body(*refs):
        d_hbm, d_ref = refs[:npart], refs[npart:2 * npart]
        (x_ref, dout_ref, s1_ref, nw_ref, h_ref, wt_hbm, dcw_ref, dvec_ref, sma_ref, row0_ref,
         gx_ref, gw_hbm, ssum_ref, rows_ref,
         stg_ref, wt_ref, gt_ref, sib_ref, out_ref, in_ref, res_ref, sall_ref, red_ref, sm0_ref, ssib_ref, schip_ref, sres_ref,
         wsem, lsem, ssem, rsem) = refs[2 * npart:]
        i = pl.program_id(0)
        x_, y_, c, chips = _place()
        j = 2 * x_ + y_
        dev = 2 * j + c
        sib = (x_, y_, 1 - c)
        rc = functools.partial(_remote, ssem, rsem)
        rel_chip = [2 * cx + cy for cx, cy in chips] + [j]
        peers = [(px, py, pc) for px in (x_, 1 - x_) for py in (y_, 1 - y_) for pc in (c, 1 - c)][1:]
        wt_copy = pltpu.make_async_copy(wt_hbm, wt_ref, lsem.at[0])

        def window(case, slot):
            return [pltpu.make_async_copy(d_hbm[p].at[:, pl.ds(c0, w)], stg_ref.at[slot, :, pl.ds(w0, w)], wsem.at[slot, n])
                    for n, (p, c0, w, w0) in enumerate(_window_pieces(case))]

        def to_sibling(k):
            return rc(k, gt_ref.at[k, 1 - c], sib_ref.at[k], sib)

        def to_chip(k):
            return rc(4 + k, out_ref.at[k], in_ref.at[k], (*chips[k], c))

        def trade(k):
            to_sibling(k).wait_recv()

            def add(n, carry):
                rr = pl.ds(pl.multiple_of(n * RS_CH, RS_CH), RS_CH)
                out_ref[k, rr, :] = (gt_ref[k, c, rr, :].astype(F32) + sib_ref[k, rr, :].astype(F32)).astype(BF16)
                return carry

            lax.fori_loop(0, IN_HALF // RS_CH, add, 0)
            to_chip(k).start()

        mine_s = pl.ds(pl.multiple_of(c * hs, 8), hs)
        other_s = pl.ds(pl.multiple_of((1 - c) * hs, 8), hs)

        def small_to_sibling():
            return rc(15, sm0_ref.at[other_s], ssib_ref, sib)

        def small_to_chip(k):
            return rc(16 + k, schip_ref.at[j], schip_ref.at[j], (*chips[k], c))

        def small_share():
            return rc(19, sres_ref.at[c], sres_ref.at[c], sib)

        for k in range(N_GW):
            @pl.when(i == k)
            def _(k=k):
                slot = k % 2
                if k == 0:
                    red_ref[...] = jnp.zeros_like(red_ref)
                    wt_copy.start()
                    sm0_ref[...] = jnp.zeros_like(sm0_ref)
                    sm0_ref[0:16, :] = dcw_ref[...]
                    sm0_ref[16:17, 0:CONV_W] = dvec_ref[0:1, :]
                    sm0_ref[16:17, CONV_W:2 * CONV_W] = dvec_ref[1:2, :]
                    sm0_ref[17:18, 0:CONV_W] = dvec_ref[2:3, :]
                    for r in range(3):
                        sm0_ref[17:18, CONV_W + 128 * r:CONV_W + 128 * (r + 1)] = sma_ref[r:r + 1, :]
                    sm0_ref[18:19, :] = row0_ref[1:2, :]
                    small_to_sibling().start()
                if k == 1:
                    small_to_sibling().wait_recv()
                    schip_ref[j] = sm0_ref[mine_s, :] + ssib_ref[...]
                    for kk in range(3):
                        small_to_chip(kk).start()
                if k == N_GW - 1:
                    for kk in range(3):
                        jk = rel_chip[kk]
                        rc(16 + kk, schip_ref.at[jk], schip_ref.at[jk], sib).wait_recv()
                    tot = schip_ref[0]
                    for d in range(1, N_CHIPS):
                        tot = tot + schip_ref[d]
                    sres_ref[c] = tot
                    small_share().start()
                for case in range(N_CHIPS):
                    if k == 0:
                        @pl.when(rel_chip[0] == case)
                        def _():
                            for cp in window(case, 0):
                                cp.start()
                    if k + 1 < N_GW:
                        @pl.when(rel_chip[k + 1] == case)
                        def _():
                            for cp in window(case, 1 - slot):
                                cp.start()
                for case in range(N_CHIPS):
                    @pl.when(rel_chip[k] == case)
                    def _():
                        for cp in window(case, slot):
                            cp.wait()
                g = lax.dot_general(stg_ref[slot], h_ref[...], (((0,), (0,)), ((), ())), preferred_element_type=F32)
                for off in sorted(set(WIN_OFF)):
                    @pl.when(rel_chip[k] % 2 == (1 if off else 0))
                    def _():
                        gt_ref[k, 0] = g[off:off + IN_HALF].astype(BF16)
                        gt_ref[k, 1] = g[off + IN_HALF:off + 2 * IN_HALF].astype(BF16)
                to_sibling(k).start()
                if k >= 1:
                    trade(k - 1)

        @pl.when(i == N_GW)
        def _():
            wt_copy.wait()

        @pl.when(i >= N_GW)
        def _():
            xv = x_ref[...]
            r = lax.rsqrt(jnp.mean(xv * xv, axis=-1, keepdims=True) + EPS)
            xh = xv * r
            n = xh * nw_ref[...]
            dproj = jnp.concatenate([ref[...] for ref in d_ref], axis=1)
            dh = jnp.dot(dproj, wt_ref[...], preferred_element_type=F32)
            red_ref[0:1, :] += jnp.sum(dh, axis=0, keepdims=True)
            red_ref[1:2, :] += jnp.sum(dh * n, axis=0, keepdims=True)
            dn = dh * s1_ref[...]
            red_ref[2:3, :] += jnp.sum(dn * xh, axis=0, keepdims=True)
            dxh = dn * nw_ref[...]
            gx_ref[...] = dout_ref[...] + r * (dxh - xh * jnp.mean(dxh * xh, axis=-1, keepdims=True))

        @pl.when(i == nstep - 1)
        def _():
            sall_ref[dev] = row0_ref[...]
            sall_ref[dev, 2:5, :] = red_ref[0:3, :]
            sends = [rc(8 + k, sall_ref.at[dev], sall_ref.at[dev], peer) for k, peer in enumerate(peers)]
            for cp in sends:
                cp.start()
            sends += [to_sibling(k) for k in range(N_GW)] + [to_chip(k) for k in range(3)]
            sends += [small_to_sibling(), small_share()] + [small_to_chip(k) for k in range(3)]
            own = N_GW - 1
            to_sibling(own).wait_recv()
            for k in range(3):
                to_chip(k).wait_recv()

            def total(n, carry):
                rr = pl.ds(pl.multiple_of(n * RS_CH, RS_CH), RS_CH)
                acc = gt_ref[own, c, rr, :].astype(F32) + sib_ref[own, rr, :].astype(F32)
                for k in range(3):
                    acc = acc + in_ref[k, rr, :].astype(F32)
                res_ref[c, rr, :] = acc
                return carry

            lax.fori_loop(0, IN_HALF // RS_CH, total, 0)
            share = rc(7, res_ref.at[c], res_ref.at[c], sib)
            share.start()
            sends.append(share)
            for k, (px, py, pc) in enumerate(peers):
                pdev = 4 * px + 2 * py + pc
                rc(8 + k, sall_ref.at[pdev], sall_ref.at[pdev], (px, py, pc)).wait_recv()
            rows_ref[...] = sall_ref[...]
            rc(19, sres_ref.at[1 - c], sres_ref.at[1 - c], sib).wait_recv()
            ssum_ref[0:hs, :] = sres_ref[0]
            ssum_ref[hs:rows0, :] = sres_ref[1]
            rc(7, res_ref.at[1 - c], res_ref.at[1 - c], sib).wait_recv()
            back = pltpu.make_async_copy(res_ref, gw_hbm, lsem.at[1])
            back.start()
            for cp in sends:
                cp.wait_send()
            back.wait()

    blk = lambda i: jnp.maximum(i - N_GW, 0)
    row = lambda w: pl.BlockSpec((tm, w), lambda i: (blk(i), 0))
    vec = pl.BlockSpec((1, D_MODEL), lambda i: (0, 0))
    const = lambda shape: pl.BlockSpec(shape, lambda i: (0,) * len(shape))
    hbm = pl.BlockSpec(memory_space=pl.ANY)
    return pl.pallas_call(
        body,
        name="in_proj_bwd",
        grid=(nstep,),
        in_specs=[hbm] * npart + [row(w) for w in DPROJ_WIDTHS] + [row(D_MODEL), row(D_MODEL), vec, vec,
                  pl.BlockSpec((t, D_MODEL), lambda i: (0, 0), pipeline_mode=pl.Buffered(1)), hbm, const((16, D_MODEL)),
                  const((8, CONV_W)), const((8, 128)), const((8, D_MODEL))],
        out_specs=[row(D_MODEL), hbm, const((rows0, D_MODEL)), const((N_DEV, 8, D_MODEL))],
        out_shape=[jax.ShapeDtypeStruct((t, D_MODEL), F32), jax.ShapeDtypeStruct((2, IN_HALF, D_MODEL), F32),
                   jax.ShapeDtypeStruct((rows0, D_MODEL), F32), jax.ShapeDtypeStruct((N_DEV, 8, D_MODEL), F32)],
        scratch_shapes=[pltpu.VMEM((2, t, WIN_W), BF16), pltpu.VMEM((IN_W, D_MODEL), BF16),
                        pltpu.VMEM((N_CHIPS, 2, IN_HALF, D_MODEL), BF16), pltpu.VMEM((N_CHIPS, IN_HALF, D_MODEL), BF16),
                        pltpu.VMEM((3, IN_HALF, D_MODEL), BF16), pltpu.VMEM((3, IN_HALF, D_MODEL), BF16),
                        pltpu.VMEM((2, IN_HALF, D_MODEL), F32), pltpu.VMEM((N_DEV, 8, D_MODEL), F32),
                        pltpu.VMEM((8, D_MODEL), F32), pltpu.VMEM((rows0, D_MODEL), F32), pltpu.VMEM((hs, D_MODEL), F32),
                        pltpu.VMEM((N_CHIPS, hs, D_MODEL), F32),
                        pltpu.VMEM((2, hs, D_MODEL), F32), pltpu.SemaphoreType.DMA((2, 3)), pltpu.SemaphoreType.DMA((2,)),
                        pltpu.SemaphoreType.DMA((n_sem,)), pltpu.SemaphoreType.DMA((n_sem,))],
        compiler_params=_cparams(dimension_semantics=("arbitrary",)),
    )(*dparts, *dparts, x, dout, s1, nw, h, wt_full, dcw, dvec, sm_a, row0)


MESH = pl.DeviceIdType.MESH


def _place():
    x, y, c = lax.axis_index("x"), lax.axis_index("y"), lax.axis_index("c")
    chips = [(1 - x, y), (x, 1 - y), (1 - x, 1 - y)]
    return x, y, c, chips


def _remote(sems_s, sems_r, k, src, dst, to):
    return pltpu.make_async_remote_copy(src_ref=src, dst_ref=dst, send_sem=sems_s.at[k], recv_sem=sems_r.at[k],
                                        device_id=to, device_id_type=MESH)


RS_CH = 32
RS_SEMS = 5


def _rs_to_sibling(rc, s0, theirs, sib_ref, sib):
    cp = rc(s0, theirs, sib_ref, sib)
    cp.start()
    return cp


def _rs_trade(rc, s0, theirs, mine, sib_ref, out_ref, in_ref, rows, c, sib, chips):
    rc(s0, theirs, sib_ref, sib).wait_recv()
    cps = []
    for k, (cx, cy) in enumerate(chips):
        jk = 2 * cx + cy

        def add(i, carry, jk=jk, k=k):
            rr = pl.ds(pl.multiple_of(i * RS_CH, RS_CH), RS_CH)
            out_ref[k, rr, :] = (mine[jk, rr, :].astype(F32) + sib_ref[jk, rr, :].astype(F32)).astype(BF16)
            return carry

        lax.fori_loop(0, rows // RS_CH, add, 0)
        cps.append(rc(s0 + 1 + k, out_ref.at[k], in_ref.at[k], (cx, cy, c)))
        cps[-1].start()
    return cps


def _rs_total(rc, s0, mine, sib_ref, out_ref, in_ref, res_ref, rows, j, c, sib):
    for k in range(3):
        rc(s0 + 1 + k, out_ref.at[k], in_ref.at[k], sib).wait_recv()

    def total(i, carry):
        rr = pl.ds(pl.multiple_of(i * RS_CH, RS_CH), RS_CH)
        acc = mine[j, rr, :].astype(F32) + sib_ref[j, rr, :].astype(F32)
        for k in range(3):
            acc = acc + in_ref[k, rr, :].astype(F32)
        res_ref[c, rr, :] = acc
        return carry

    lax.fori_loop(0, rows // RS_CH, total, 0)
    cp = rc(s0 + 4, res_ref.at[c], res_ref.at[c], sib)
    cp.start()
    return cp


def _rs_done(rc, s0, res_ref, c, sib):
    rc(s0 + 4, res_ref.at[1 - c], res_ref.at[1 - c], sib).wait_recv()


def _rs_scratch(rows):
    return [pltpu.VMEM((N_CHIPS, rows, D_MODEL), BF16), pltpu.VMEM((3, rows, D_MODEL), BF16),
            pltpu.VMEM((3, rows, D_MODEL), BF16)]


MAIN_W = 640
MAIN_DST = (((0, 0, 512), (1, 0, 128)), ((2, 0, 512), (3, 0, 128)), ((3, 128, 384), (4, 0, 256)), ((4, 384, 128), (5, 0, 512)))
PAIR_DST = ((1, 128, 128), (4, 256, 128))


def _in_proj_gather(x, wt, c_row, w_ada, b_ada, nw):
    t = x.shape[0]
    ch = 512
    n_sem = 16

    def body(x_hbm, wt_ref, c_ref, wada_ref, bada_ref, nw_ref,
             q_hbm, kv_hbm, ga_hbm, ua_hbm, ug_hbm, gb_hbm, h_hbm, w4_hbm, call_ref, ada_ref,
             x_ref, h_ref, w4_ref, stg_ref, pstg_ref, part_ref, lsem, osem, wsem, ssem, rsem):
        outs = (q_hbm, kv_hbm, ga_hbm, ua_hbm, ug_hbm, gb_hbm)
        x_, y_, c, chips = _place()
        j = 2 * x_ + y_
        dev = 2 * j + c
        sib = (x_, y_, 1 - c)
        idx = [2 * cx + cy for cx, cy in chips]
        rc = functools.partial(_remote, ssem, rsem)
        x_copy = pltpu.make_async_copy(x_hbm, x_ref, lsem.at[0])
        x_copy.start()

        def rows_of(s, cc):
            return pl.ds(pl.multiple_of(2 * IN_HALF * s + IN_HALF * cc, 16), IN_HALF)

        w4_ref[rows_of(j, 0), :] = wt_ref[0].astype(BF16)
        w4_ref[rows_of(j, 1), :] = wt_ref[1].astype(BF16)
        call_ref[dev] = c_ref[...]
        sends = []
        peers = [(px, py, pc) for px in (x_, 1 - x_) for py in (y_, 1 - y_) for pc in (c, 1 - c)][1:]
        for k, peer in enumerate(peers):
            sends.append(rc(k, call_ref.at[dev], call_ref.at[dev], peer))
        for cp in sends:
            cp.start()

        x_copy.wait()

        def prenorm(i, carry):
            rr = pl.ds(pl.multiple_of(i * ch, ch), ch)
            xv = x_ref[rr, :]
            r = lax.rsqrt(jnp.mean(xv * xv, axis=-1, keepdims=True) + EPS)
            x_ref[rr, :] = (xv * r) * nw_ref[...]
            return carry

        lax.fori_loop(0, t // ch, prenorm, 0)

        for k, (px, py, pc) in enumerate(peers):
            pdev = 4 * px + 2 * py + pc
            rc(k, call_ref.at[pdev], call_ref.at[pdev], (px, py, pc)).wait_recv()
        rowid = lax.broadcasted_iota(jnp.int32, (N_DEV, D_MODEL), 0)
        call = jnp.zeros((N_DEV, D_MODEL), F32)
        for r in range(N_DEV):
            call = jnp.where(rowid == r, jnp.broadcast_to(call_ref[r], (N_DEV, D_MODEL)), call)
        bsh = bada_ref[:, 0:ADA_SHARD]
        for k in range(1, N_CHIPS):
            bsh = jnp.where(j == k, bada_ref[:, ADA_SHARD * k:ADA_SHARD * (k + 1)], bsh)
        part = jnp.dot(_silu(call).astype(BF16), wada_ref[...].astype(BF16), preferred_element_type=F32) + bsh
        for r in range(N_DEV):
            part_ref[r] = part[r:r + 1, :]
        ada_ref[j] = part_ref[dev]
        for k, chip in enumerate(chips):
            sends.append(rc(13 + k, part_ref.at[2 * idx[k] + c], ada_ref.at[j], (*chip, c)))
            sends[-1].start()
        for k, chip in enumerate(chips):
            sends.append(rc(7 + k, w4_ref.at[rows_of(j, c)], w4_ref.at[rows_of(j, c)], (*chip, c)))
            sends[-1].start()
        for k in range(3):
            rc(13 + k, ada_ref.at[idx[k]], ada_ref.at[idx[k]], sib).wait_recv()

        shift = jnp.concatenate([ada_ref[0], ada_ref[1][:, 0:256]], axis=1)
        s1 = 1.0 + jnp.concatenate([ada_ref[1][:, 256:768], ada_ref[2][:, 0:512]], axis=1)

        def norm(i, carry):
            rr = pl.ds(pl.multiple_of(i * ch, ch), ch)
            h_ref[rr, :] = (x_ref[rr, :] * s1 + shift).astype(BF16)
            return carry

        lax.fori_loop(0, t // ch, norm, 0)
        h_copy = pltpu.make_async_copy(h_ref, h_hbm, lsem.at[1])
        h_copy.start()

        def put_main(case, slot):
            cps, col = [], 0
            for n, (a, c0, w) in enumerate(MAIN_DST[case]):
                cps.append(pltpu.make_async_copy(stg_ref.at[slot, :, pl.ds(col, w)], outs[a].at[:, pl.ds(c0, w)], osem.at[slot, n]))
                col += w
            return cps

        def put_pair(case, slot):
            a, c0, w = PAIR_DST[case]
            return pltpu.make_async_copy(pstg_ref.at[slot], outs[a].at[:, pl.ds(c0, w)], osem.at[slot, 2])

        def project(first_row, width, dst, slot):
            wrows = pl.ds(pl.multiple_of(first_row, 128), width)

            def blk(i, carry):
                rr = pl.ds(pl.multiple_of(i * ch, ch), ch)
                dst[slot, rr, :] = lax.dot_general(h_ref[rr, :], w4_ref[wrows, :], (((1,), (1,)), ((), ())),
                                                   preferred_element_type=F32)
                return carry

            lax.fori_loop(0, t // ch, blk, 0)

        def phase(p, s, pair):
            slot = p % 2
            if p >= 2:
                for case in range(N_CHIPS):
                    @pl.when(order[p - 2] == case)
                    def _():
                        for cp in put_main(case, slot):
                            cp.wait()
            if p == 3:
                for case in range(2):
                    @pl.when(j // 2 == case)
                    def _():
                        put_pair(case, 0).wait()
            project(2 * IN_HALF * s + 64 * (s % 2), MAIN_W, stg_ref, slot)
            for case in range(N_CHIPS):
                @pl.when(s == case)
                def _():
                    for cp in put_main(case, slot):
                        cp.start()
            if pair is not None:
                project(MAIN_W + 2 * (2 * IN_HALF) * pair, 128, pstg_ref, slot % 2 if p == 2 else 1)
                for case in range(2):
                    @pl.when(pair == case)
                    def _():
                        put_pair(case, 0 if p == 2 else 1).start()

        order = [j] + idx
        w_out = [pltpu.make_async_copy(w4_ref.at[pl.ds(pl.multiple_of(2 * IN_HALF * s, 32), 2 * IN_HALF)],
                                       w4_hbm.at[pl.ds(pl.multiple_of(2 * IN_HALF * s, 32), 2 * IN_HALF)], wsem.at[p])
                 for p, s in enumerate(order)]
        w_out[0].start()
        phase(0, j, None)
        passed = []
        for k in range(3):
            jk = idx[k]
            rc(7 + k, w4_ref.at[rows_of(jk, c)], w4_ref.at[rows_of(jk, c)], sib).wait_recv()
            passed.append(rc(10 + k, w4_ref.at[rows_of(jk, c)], w4_ref.at[rows_of(jk, c)], sib))
            passed[-1].start()
            rc(10 + k, w4_ref.at[rows_of(jk, 1 - c)], w4_ref.at[rows_of(jk, 1 - c)], sib).wait_recv()
            w_out[1 + k].start()
            if k == 0:
                phase(1, jk, None)
            elif k == 1:
                phase(2, jk, j // 2)
            else:
                phase(3, jk, 1 - j // 2)

        for case in range(N_CHIPS):
            for p in (2, 3):
                @pl.when(order[p] == case)
                def _():
                    for cp in put_main(case, p % 2):
                        cp.wait()
        for case in range(2):
            @pl.when(1 - j // 2 == case)
            def _():
                put_pair(case, 1).wait()
        h_copy.wait()
        for cp in w_out:
            cp.wait()
        for cp in sends + passed:
            cp.wait_send()

    vm = pl.BlockSpec(memory_space=pltpu.VMEM)
    hbm = pl.BlockSpec(memory_space=pl.ANY)
    widths = (512, 256, 512, 512, 512, 512)
    return pl.pallas_call(
        body,
        name="in_proj",
        in_specs=[hbm, vm, vm, vm, vm, vm],
        out_specs=[hbm] * 8 + [vm, vm],
        out_shape=[jax.ShapeDtypeStruct((t, w), F32) for w in widths]
        + [jax.ShapeDtypeStruct((t, D_MODEL), BF16), jax.ShapeDtypeStruct((IN_W, D_MODEL), BF16),
           jax.ShapeDtypeStruct((N_DEV, 1, D_MODEL), F32), jax.ShapeDtypeStruct((N_CHIPS, 1, ADA_SHARD), F32)],
        scratch_shapes=[pltpu.VMEM((t, D_MODEL), F32), pltpu.VMEM((t, D_MODEL), BF16), pltpu.VMEM((IN_W, D_MODEL), BF16),
                        pltpu.VMEM((2, t, MAIN_W), F32), pltpu.VMEM((2, t, 128), F32), pltpu.VMEM((N_DEV, 1, ADA_SHARD), F32),
                        pltpu.SemaphoreType.DMA((2,)), pltpu.SemaphoreType.DMA((2, 3)), pltpu.SemaphoreType.DMA((N_CHIPS,)),
                        pltpu.SemaphoreType.DMA((n_sem,)), pltpu.SemaphoreType.DMA((n_sem,))],
        compiler_params=_cparams(),
    )(x, wt, c_row, w_ada, b_ada, nw)


def _adamw_math(w, g, m, v):
    m2 = ADAM_B1 * m + (1.0 - ADAM_B1) * g
    v2 = ADAM_B2 * v + (1.0 - ADAM_B2) * (g * g)
    m_hat = m2 / (1.0 - ADAM_B1 ** ADAM_STEP)
    v_hat = v2 / (1.0 - ADAM_B2 ** ADAM_STEP)
    delta = -ADAM_LR * (m_hat / (jnp.sqrt(v_hat) + ADAM_EPS) + ADAM_WD * w)
    return delta, m2, v2


def _adamw(name, w, g, m, v, tm, through=None):
    r, cdim = w.shape
    nstep = r // tm
    extra = [] if through is None else [through]

    def body(w_ref, g_ref, m_ref, v_ref, *rest):
        g2_ref, d_ref, m2_ref, v2_ref = rest[len(extra):len(extra) + 4]
        g = g_ref[...]
        g2_ref[...] = g
        d_ref[...], m2_ref[...], v2_ref[...] = _adamw_math(w_ref[...], g, m_ref[...], v_ref[...])
        if extra:
            rest[-1][...] = rest[0][...]

    blk = pl.BlockSpec((tm, cdim), lambda i: (i, 0))
    eblk = [pl.BlockSpec((e.shape[0] // nstep, e.shape[1]), lambda i: (i, 0)) for e in extra]
    return pl.pallas_call(
        body,
        name=name,
        grid=(nstep,),
        in_specs=[blk] * 4 + eblk,
        out_specs=[blk] * 4 + eblk,
        out_shape=[jax.ShapeDtypeStruct((r, cdim), F32)] * 4 + [jax.ShapeDtypeStruct(e.shape, e.dtype) for e in extra],
        compiler_params=_cparams(dimension_semantics=("arbitrary",)),
    )(w, g, m, v, *extra)


def _adamw_ada(w, m, v, cact_t, dcols):
    r, cdim = w.shape
    tm = 256

    def body(w_ref, m_ref, v_ref, ct_ref, dc_ref, g_ref, d_ref, m2_ref, v2_ref):
        g = jnp.dot(ct_ref[...].astype(BF16), dc_ref[...].astype(BF16), preferred_element_type=F32)
        g_ref[...] = g
        d_ref[...], m2_ref[...], v2_ref[...] = _adamw_math(w_ref[...], g, m_ref[...], v_ref[...])

    blk = pl.BlockSpec((tm, cdim), lambda i: (i, 0))
    return pl.pallas_call(
        body,
        name="adamw_w_ada",
        grid=(r // tm,),
        in_specs=[blk] * 3 + [pl.BlockSpec((tm, N_DEV), lambda i: (i, 0)), pl.BlockSpec((N_DEV, cdim), lambda i: (0, 0))],
        out_specs=[blk] * 4,
        out_shape=[jax.ShapeDtypeStruct((r, cdim), F32)] * 4,
        compiler_params=_cparams(dimension_semantics=("arbitrary",)),
    )(w, m, v, cact_t, dcols)


def _adamw_small(ws, ms, vs, ssum, rows):
    n = len(ws)

    def body(*refs):
        w_r, m_r, v_r = refs[0:n], refs[n:2 * n], refs[2 * n:3 * n]
        ss_ref, rows_ref = refs[3 * n], refs[3 * n + 1]
        g_r, d_r, m2_r, v2_r = (refs[3 * n + 2 + k * n:3 * n + 2 + (k + 1) * n] for k in range(4))
        loss_ref = refs[7 * n + 2]
        j = 2 * lax.axis_index("x") + lax.axis_index("y")
        rsum = rows_ref[0]
        for d in range(1, N_DEV):
            rsum = rsum + rows_ref[d]
        taps = []
        for t in range(CONV_TAPS):
            row = ss_ref[t // 2:t // 2 + 1, :]
            c0 = CONV_W * (t % 2)
            pick = row[:, c0:c0 + 128]
            for k in range(1, N_CHIPS):
                pick = jnp.where(j == k, row[:, c0 + 128 * k:c0 + 128 * (k + 1)], pick)
            taps.append(pick)
        grads = [jnp.concatenate([rsum[2:3], rsum[3:4], rsum[0:1]], axis=1), rsum[4:5],
                 ss_ref[17:18, 512:512 + HEAD_DIM], ss_ref[17:18, 640:640 + HEAD_DIM], ss_ref[17:18, 768:776],
                 None, ss_ref[16:17, 0:CONV_W], ss_ref[16:17, CONV_W:2 * CONV_W], ss_ref[17:18, 0:CONV_W]]
        for i in range(n):
            if grads[i] is None:
                for t in range(CONV_TAPS):
                    g_r[i][t:t + 1, :] = taps[t]
                g = g_r[i][...]
            else:
                g = grads[i]
                g_r[i][...] = g
            d_r[i][...], m2_r[i][...], v2_r[i][...] = _adamw_math(w_r[i][...], g, m_r[i][...], v_r[i][...])
        loss_ref[...] = (0.5 / D_MODEL) * jnp.sum(ss_ref[18:19, :], axis=1, keepdims=True)

    vm = pl.BlockSpec(memory_space=pltpu.VMEM)
    shapes = [jax.ShapeDtypeStruct(w.shape, F32) for w in ws]
    out = pl.pallas_call(
        body,
        name="adamw_small",
        in_specs=[vm] * (3 * n + 2),
        out_specs=[vm] * (4 * n + 1),
        out_shape=shapes * 4 + [jax.ShapeDtypeStruct((1, 1), F32)],
        compiler_params=_cparams(),
    )(*ws, *ms, *vs, ssum, rows)
    return out[0:n], out[n:2 * n], out[2 * n:3 * n], out[3 * n:4 * n], out[4 * n]


def _rope_tables(t):
    inv = ROPE_THETA ** (-jnp.arange(0, HEAD_DIM, 2, dtype=F32) / HEAD_DIM)
    ang = jnp.arange(t, dtype=F32)[:, None] * inv[None, :]
    cos, sin = jnp.cos(ang), jnp.sin(ang)
    return jnp.tile(cos, (1, 4)), jnp.tile(jnp.concatenate([-sin, sin], axis=1), (1, 2))


def kernel(x, c, w_ada, b_ada, norm_w, w_in, q_norm_w, k_norm_w, sinks, conv_w, conv_b, ln_w, ln_b, w_out, loss_target, m_w_ada, m_b_ada, m_norm_w, m_w_in, m_q_norm_w, m_k_norm_w, m_sinks, m_conv_w, m_conv_b, m_ln_w, m_ln_b, m_w_out, v_w_ada, v_b_ada, v_norm_w, v_w_in, v_q_norm_w, v_k_norm_w, v_sinks, v_conv_w, v_conv_b, v_ln_w, v_ln_b, v_w_out):
    xi, yi = lax.axis_index("x"), lax.axis_index("y")
    j = 2 * xi + yi
    x2, tgt = x[0], loss_target[0]
    t = x2.shape[0]

    wt_s, mt_s, vt_s = w_in[0].T, m_w_in[0].T, v_w_in[0].T
    cw_pad = jnp.pad(conv_w[0], ((0, 1), (0, 0)))

    q_raw, kv_raw, ga, ua, ug, gb, h, w_full, call, ada4 = _in_proj_gather(
        x2, wt_s.reshape(2, IN_HALF, D_MODEL), c, w_ada[0], b_ada, norm_w)
    ada = ada4.reshape(1, 3 * D_MODEL)
    s1, gate = 1.0 + ada[:, D_MODEL:2 * D_MODEL], ada[:, 2 * D_MODEL:]

    cos_f, sin_s = _rope_tables(t)
    qw2, kw2 = jnp.tile(q_norm_w, (1, 2)), jnp.tile(k_norm_w, (1, 2))

    o, mix_a, wo4, cw4 = _attn_fwd(q_raw, kv_raw, ga, qw2, kw2, sinks, cos_f, sin_s,
                                   w_out[0].reshape(2, OUT_HALF, D_MODEL), cw_pad)
    w_out_full = wo4.reshape(D_MODEL, D_MODEL)
    cw_full = jnp.concatenate([cw4[i] for i in range(N_CHIPS)], axis=1)
    cz, mix_b = _conv_fwd(ua, ug, gb, cw_full, conv_b, ln_w, ln_b)
    dout, dmix_a, dmix_b, gwo_bf, red_o = _out_proj(mix_a, mix_b, x2, tgt, gate, w_out_full)

    dq, dkv, dga, sm_a, gwo = _attn_bwd(q_raw, kv_raw, ga, o, dmix_a, qw2, kw2, sinks, cos_f, sin_s,
                                        gwo_bf.reshape(N_CHIPS, 2, OUT_HALF, D_MODEL))
    dua, dug, dgb, dcw, dvec = _conv_bwd(ua, ug, gb, cz, dmix_b, cw_full, ln_w, ln_b)
    dparts = (dq, dkv, dga, dua, dug, dgb)

    grad_x, gw, ssum, rows = _in_proj_bwd(dparts, h, x2, dout, s1, norm_w, w_full, dcw, dvec, sm_a, red_o)

    gt_w_in = gw.reshape(2 * IN_HALF, D_MODEL)
    g_w_out = gwo.reshape(D_MODEL // N_CHIPS, D_MODEL)
    d_ada_all = jnp.concatenate([rows[:, 2], rows[:, 3], rows[:, 0]], axis=1)
    dcols = lax.dynamic_slice(d_ada_all, (0, ADA_SHARD * j), (N_DEV, ADA_SHARD))
    cact_t = jax.nn.silu(call.reshape(N_DEV, D_MODEL)).T

    g_w_ada, d_w_ada, nm_w_ada, nv_w_ada = _adamw_ada(w_ada[0], m_w_ada[0], v_w_ada[0], cact_t, dcols)
    gt_w_in, dt_w_in, nmt_w_in, nvt_w_in, grad_x = _adamw("adamw_w_in", wt_s, gt_w_in, mt_s, vt_s, 176, through=grad_x)
    g_w_in, d_w_in, nm_w_in, nv_w_in = gt_w_in.T, dt_w_in.T, nmt_w_in.T, nvt_w_in.T
    g_w_out, d_w_out, nm_w_out, nv_w_out = _adamw("adamw_w_out", w_out[0], g_w_out, m_w_out[0], v_w_out[0], 128)
    ws = [b_ada, norm_w, q_norm_w, k_norm_w, sinks, conv_w[0], conv_b, ln_w, ln_b]
    ms = [m_b_ada, m_norm_w, m_q_norm_w, m_k_norm_w, m_sinks, m_conv_w[0], m_conv_b, m_ln_w, m_ln_b]
    vs = [v_b_ada, v_norm_w, v_q_norm_w, v_k_norm_w, v_sinks, v_conv_w[0], v_conv_b, v_ln_w, v_ln_b]
    gs, ds, nms, nvs, loss11 = _adamw_small(ws, ms, vs, ssum, rows)
    loss = loss11[0, 0]

    def order(ada_v, in_v, out_v, sm):
        b, nw_, qw_, kw_, sk_, cw_, cb_, lw_, lb_ = sm
        return [ada_v[None], b, nw_, in_v[None], qw_, kw_, sk_, cw_[None], cb_, lw_, lb_, out_v[None]]

    grads = order(g_w_ada, g_w_in, g_w_out, gs)
    deltas = order(d_w_ada, d_w_in, d_w_out, ds)
    new_m = order(nm_w_ada, nm_w_in, nm_w_out, nms)
    new_v = order(nv_w_ada, nv_w_in, nv_w_out, nvs)
    return (loss, grad_x[None], *grads, *deltas, *new_m, *new_v)
```

```python
import functools

import jax
import jax.numpy as jnp
from jax import lax
from jax.experimental import pallas as pl
from jax.experimental.pallas import tpu as pltpu

F32 = jnp.float32
BF16 = jnp.bfloat16

D_MODEL = 1024
ATTN_W = 512
KV_W = 128
CONV_W = 512
IN_W = 2816
HEAD_DIM = 64
CONV_TAPS = 31
QBLK = 128
EPS = 1e-6
ROPE_THETA = 10000.0

ADAM_LR = 0.001
ADAM_B1 = 0.9
ADAM_B2 = 0.999
ADAM_EPS = 1e-08
ADAM_WD = 0.01
ADAM_STEP = 10

N_CHIPS = 4
N_DEV = 8
IN_HALF = IN_W // N_CHIPS // 2
OUT_HALF = D_MODEL // N_CHIPS // 2
ADA_SHARD = 3 * D_MODEL // N_CHIPS

VMEM_LIMIT = 56 * 1024 * 1024
CONV_PAD = 32


def _cparams(**kw):
    return pltpu.CompilerParams(vmem_limit_bytes=VMEM_LIMIT, **kw)


def _sigmoid(v):
    return 1.0 / (1.0 + jnp.exp(-v))


def _silu(v):
    return v * _sigmoid(v)


def _dsilu(v):
    s = _sigmoid(v)
    return s * (1.0 + v * (1.0 - s))


def _lane(shape):
    return lax.broadcasted_iota(jnp.int32, shape, len(shape) - 1)


PUT_ROWS = 512


def _fetch(hbm_refs, vmem_refs, sem):
    cps = [pltpu.make_async_copy(h, v, sem.at[i]) for i, (h, v) in enumerate(zip(hbm_refs, vmem_refs))]
    for cp in cps:
        cp.start()
    return cps


def _put(vmem_ref, hbm_ref, sem, m):
    r = pl.ds(pl.multiple_of(m * PUT_ROWS, PUT_ROWS), PUT_ROWS)
    return pltpu.make_async_copy(vmem_ref.at[r], hbm_ref.at[r], sem.at[m])


def _put_all(pairs, sems, m):
    for (v, h), sem in zip(pairs, sems):
        _put(v, h, sem, m).start()


def _put_wait(pairs, sems, n):
    for (v, h), sem in zip(pairs, sems):
        for m in range(n):
            _put(v, h, sem, m).wait()


def _head_mean(s, left):
    sl = jnp.sum(jnp.where(left, s, 0.0), axis=-1, keepdims=True)
    sr = jnp.sum(jnp.where(left, 0.0, s), axis=-1, keepdims=True)
    return jnp.where(left, sl, sr) * (1.0 / HEAD_DIM)


def _rot(v, first):
    return jnp.where(first, pltpu.roll(v, 96, 1), pltpu.roll(v, 32, 1))


def _norm_rope(v, w, cos, sin_s, left, first):
    r = lax.rsqrt(_head_mean(v * v, left) + EPS)
    xh = v * r
    n = xh * w
    return n * cos + _rot(n, first) * sin_s, xh, r


def _norm_rope_bwd(d, xh, r, w, cos, sin_s, left, first):
    dn = d * cos - _rot(d, first) * sin_s
    dw = jnp.sum(dn * xh, axis=0, keepdims=True)
    dxh = dn * w
    return r * (dxh - xh * _head_mean(dxh * xh, left)), dw


def _dup_heads(v, left):
    sw = pltpu.roll(v, 64, 1)
    return jnp.where(left, v, sw), jnp.where(left, sw, v)


def _prep_kv(kv_ref, kw_ref, cos_ref, sin_ref, ka_ref, va_ref, t):
    ch = 256
    for g in range(2):
        ka_ref[g, 0:QBLK, :] = jnp.zeros((QBLK, 128), BF16)
        va_ref[g, 0:QBLK, :] = jnp.zeros((QBLK, 128), BF16)

    def chunk(i, carry):
        r0 = pl.multiple_of(i * ch, ch)
        left = _lane((ch, 128)) < 64
        first = (_lane((ch, 128)) % 64) < 32
        k = kv_ref[pl.ds(r0, ch), 0:128]
        v = kv_ref[pl.ds(r0, ch), 128:256]
        kr, _, _ = _norm_rope(k, kw_ref[...], cos_ref[pl.ds(r0, ch), :], sin_ref[pl.ds(r0, ch), :], left, first)
        k0, k1 = _dup_heads(kr, left)
        v0, v1 = _dup_heads(v, left)
        ka_ref[0, pl.ds(QBLK + r0, ch), :] = k0.astype(BF16)
        ka_ref[1, pl.ds(QBLK + r0, ch), :] = k1.astype(BF16)
        va_ref[0, pl.ds(QBLK + r0, ch), :] = v0.astype(BF16)
        va_ref[1, pl.ds(QBLK + r0, ch), :] = v1.astype(BF16)
        return carry

    lax.fori_loop(0, t // ch, chunk, 0)


def _band_mask(n):
    qi = lax.broadcasted_iota(jnp.int32, (2 * QBLK, 2 * QBLK), 0) % QBLK
    kj = lax.broadcasted_iota(jnp.int32, (2 * QBLK, 2 * QBLK), 1)
    local = (kj > qi) & (kj <= qi + QBLK)
    return local & ((n > 0) | (kj >= QBLK))


def _softmax_pair(s, mask, sink0, sink1):
    row = lax.broadcasted_iota(jnp.int32, (2 * QBLK, 1), 0)
    sink = jnp.where(row < QBLK, sink0, sink1)
    s = jnp.where(mask, s, -jnp.inf)
    m = jnp.maximum(jnp.max(s, axis=-1, keepdims=True), sink)
    e = jnp.exp(s - m)
    es = jnp.exp(sink - m)
    inv = 1.0 / (jnp.sum(e, axis=-1, keepdims=True) + es)
    return e * inv, es * inv


def _stack_heads(v, left):
    return jnp.concatenate([jnp.where(left, v, 0.0), jnp.where(left, 0.0, v)], axis=0)


def _attn_fwd(q_raw, kv_raw, ga, qw2, kw2, sinks, cos_f, sin_s, wo, cw):
    t = q_raw.shape[0]
    nblk = t // QBLK
    per_put = PUT_ROWS // QBLK

    def body(q_hbm, kv_ref, ga_hbm, qw_ref, kw_ref, sk_ref, cos_hbm, sin_hbm, wo_ref, cw_ref,
             o_hbm, mix_hbm, wo4_ref, cw4_ref, ka_ref, va_ref, q_ref, ga_ref, o_ref, mix_ref, cos_ref, sin_ref,
             isem, osem0, osem1, ssem, rsem):
        loads = _fetch((cos_hbm, sin_hbm, q_hbm, ga_hbm), (cos_ref, sin_ref, q_ref, ga_ref), isem)
        outs, osems = ((o_ref, o_hbm), (mix_ref, mix_hbm)), (osem0, osem1)
        x, y, c, chips = _place()
        j = 2 * x + y
        sib = (x, y, 1 - c)
        idx = [2 * cx + cy for cx, cy in chips]
        rc = functools.partial(_remote, ssem, rsem)
        wo4_ref[j] = wo_ref[...].astype(BF16)
        cw4_ref[j] = cw_ref[...]
        sends = []
        for k, chip in enumerate(chips):
            sends.append(rc(k, wo4_ref.at[j, c], wo4_ref.at[j, c], (*chip, c)))
            sends.append(rc(6 + k, cw4_ref.at[j], cw4_ref.at[j], (*chip, c)))
        for cp in sends:
            cp.start()

        loads[0].wait()
        loads[1].wait()
        _prep_kv(kv_ref, kw_ref, cos_ref, sin_ref, ka_ref, va_ref, t)
        loads[2].wait()
        loads[3].wait()

        def blk(n, carry):
            r0 = pl.multiple_of(n * QBLK, QBLK)
            left = _lane((QBLK, 128)) < 64
            first = (_lane((QBLK, 128)) % 64) < 32
            cos = cos_ref[pl.ds(r0, QBLK), :]
            sin = sin_ref[pl.ds(r0, QBLK), :]
            mask = _band_mask(n)
            scores = []
            for p in range(4):
                lanes = slice(p * 128, (p + 1) * 128)
                qr, _, _ = _norm_rope(q_ref[pl.ds(r0, QBLK), lanes], qw_ref[...], cos, sin, left, first)
                q2 = _stack_heads(qr * 0.125, left).astype(BF16)
                scores.append(lax.dot_general(q2, ka_ref[p // 2, pl.ds(r0, 2 * QBLK), :], (((1,), (1,)), ((), ())),
                                              preferred_element_type=F32))
            probs = [_softmax_pair(scores[p], mask, sk_ref[0, 2 * p], sk_ref[0, 2 * p + 1])[0].astype(BF16)
                     for p in range(4)]
            for p in range(4):
                lanes = slice(p * 128, (p + 1) * 128)
                o2 = jnp.dot(probs[p], va_ref[p // 2, pl.ds(r0, 2 * QBLK), :], preferred_element_type=F32)
                o = jnp.where(left, o2[0:QBLK], o2[QBLK:2 * QBLK])
                o_ref[pl.ds(r0, QBLK), lanes] = o.astype(BF16)
                mix_ref[pl.ds(r0, QBLK), lanes] = (o * _silu(ga_ref[pl.ds(r0, QBLK), lanes])).astype(BF16)

            @pl.when(n % per_put == per_put - 1)
            def _():
                _put_all(outs, osems, n // per_put)

            return carry

        lax.fori_loop(0, nblk, blk, 0)
        _put_wait(outs, osems, t // PUT_ROWS)

        passed = []
        for k, chip in enumerate(chips):
            jk = idx[k]
            rc(k, wo4_ref.at[jk, c], wo4_ref.at[jk, c], sib).wait_recv()
            passed.append(rc(3 + k, wo4_ref.at[jk, c], wo4_ref.at[jk, c], sib))
            passed[-1].start()
        for k, chip in enumerate(chips):
            jk = idx[k]
            rc(3 + k, wo4_ref.at[jk, 1 - c], wo4_ref.at[jk, 1 - c], sib).wait_recv()
            rc(6 + k, cw4_ref.at[jk], cw4_ref.at[jk], sib).wait_recv()
        for cp in sends + passed:
            cp.wait_send()

    vm = pl.BlockSpec(memory_space=pltpu.VMEM)
    hbm = pl.BlockSpec(memory_space=pl.ANY)
    n_sem = 9
    return pl.pallas_call(
        body,
        name="attn_fwd",
        in_specs=[hbm, vm, hbm, vm, vm, pl.BlockSpec(memory_space=pltpu.SMEM), hbm, hbm, vm, vm],
        out_specs=[hbm, hbm, vm, vm],
        out_shape=[jax.ShapeDtypeStruct((t, ATTN_W), BF16), jax.ShapeDtypeStruct((t, ATTN_W), BF16),
                   jax.ShapeDtypeStruct((N_CHIPS, 2, OUT_HALF, D_MODEL), BF16),
                   jax.ShapeDtypeStruct((N_CHIPS, 32, 128), F32)],
        scratch_shapes=[pltpu.VMEM((2, t + QBLK, 128), BF16), pltpu.VMEM((2, t + QBLK, 128), BF16),
                        pltpu.VMEM((t, ATTN_W), F32), pltpu.VMEM((t, ATTN_W), F32),
                        pltpu.VMEM((t, ATTN_W), BF16), pltpu.VMEM((t, ATTN_W), BF16),
                        pltpu.VMEM((t, 128), F32), pltpu.VMEM((t, 128), F32),
                        pltpu.SemaphoreType.DMA((4,)), pltpu.SemaphoreType.DMA((t // PUT_ROWS,)),
                        pltpu.SemaphoreType.DMA((t // PUT_ROWS,)),
                        pltpu.SemaphoreType.DMA((n_sem,)), pltpu.SemaphoreType.DMA((n_sem,))],
        compiler_params=_cparams(),
    )(q_raw, kv_raw, ga, qw2, kw2, sinks, cos_f, sin_s, wo, cw)


def _attn_bwd(q_raw, kv_raw, ga, o, dmix, qw2, kw2, sinks, cos_f, sin_s, go):
    t = q_raw.shape[0]
    nblk = t // QBLK
    per_put = PUT_ROWS // QBLK

    def body(q_hbm, kv_ref, ga_hbm, o_hbm, dm_hbm, qw_ref, kw_ref, sk_ref, cos_hbm, sin_hbm, go_ref,
             dq_hbm, dkv_ref, dga_hbm, sm_ref, gwo_ref, ka_ref, va_ref, dka_ref, dva_ref,
             sibo_ref, outo_ref, ino_ref, q_ref, ga_ref, o_ref, dm_ref, dq_ref, dga_ref, cos_ref, sin_ref,
             isem, osem0, osem1, ssem, rsem):
        loads = _fetch((cos_hbm, sin_hbm, q_hbm, ga_hbm, o_hbm, dm_hbm), (cos_ref, sin_ref, q_ref, ga_ref, o_ref, dm_ref), isem)
        outs, osems = ((dq_ref, dq_hbm), (dga_ref, dga_hbm)), (osem0, osem1)
        x, y, c, chips = _place()
        sib = (x, y, 1 - c)
        rc = functools.partial(_remote, ssem, rsem)
        theirs, mine = go_ref.at[:, 1 - c], go_ref.at[:, c]
        sends = [_rs_to_sibling(rc, 0, theirs, sibo_ref, sib)]
        loads[0].wait()
        loads[1].wait()
        _prep_kv(kv_ref, kw_ref, cos_ref, sin_ref, ka_ref, va_ref, t)
        dka_ref[...] = jnp.zeros_like(dka_ref)
        dva_ref[...] = jnp.zeros_like(dva_ref)
        sends += _rs_trade(rc, 0, theirs, mine, sibo_ref, outo_ref, ino_ref, OUT_HALF, c, sib, chips)
        for cp in loads[2:]:
            cp.wait()

        def blk(n, carry):
            dqw, dsk = carry
            r0 = pl.multiple_of(n * QBLK, QBLK)
            left = _lane((QBLK, 128)) < 64
            first = (_lane((QBLK, 128)) % 64) < 32
            cos = cos_ref[pl.ds(r0, QBLK), :]
            sin = sin_ref[pl.ds(r0, QBLK), :]
            mask = _band_mask(n)
            row = lax.broadcasted_iota(jnp.int32, (2 * QBLK, 1), 0)
            rows = pl.ds(r0, QBLK)
            win = pl.ds(r0, 2 * QBLK)
            lane_of = [slice(p * 128, (p + 1) * 128) for p in range(4)]
            for grp in ((0, 1), (2, 3)):
                qn = {p: _norm_rope(q_ref[rows, lane_of[p]], qw_ref[...], cos, sin, left, first) for p in grp}
                q2 = {p: _stack_heads(qn[p][0] * 0.125, left).astype(BF16) for p in grp}
                sc = {p: lax.dot_general(q2[p], ka_ref[p // 2, win, :], (((1,), (1,)), ((), ())),
                                         preferred_element_type=F32) for p in grp}
                do2 = {}
                for p in grp:
                    gav = ga_ref[rows, lane_of[p]]
                    dmv = dm_ref[rows, lane_of[p]].astype(F32)
                    dga_ref[rows, lane_of[p]] = (dmv * o_ref[rows, lane_of[p]].astype(F32) * _dsilu(gav)).astype(BF16)
                    do2[p] = _stack_heads(dmv * _silu(gav), left).astype(BF16)
                dpm = {p: lax.dot_general(do2[p], va_ref[p // 2, win, :], (((1,), (1,)), ((), ())),
                                          preferred_element_type=F32) for p in grp}
                sm = {p: _softmax_pair(sc[p], mask, sk_ref[0, 2 * p], sk_ref[0, 2 * p + 1]) for p in grp}
                dsl = {}
                for p in grp:
                    pm, ps = sm[p]
                    delta = jnp.sum(pm * dpm[p], axis=-1, keepdims=True)
                    dsl[p] = (pm * (dpm[p] - delta)).astype(BF16)
                    pd = ps * delta
                    d0 = jnp.sum(jnp.where(row < QBLK, pd, 0.0), axis=0, keepdims=True)
                    d1 = jnp.sum(jnp.where(row < QBLK, 0.0, pd), axis=0, keepdims=True)
                    l8 = _lane((1, 128))
                    dsk = dsk - jnp.where(l8 == 2 * p, d0, 0.0) - jnp.where(l8 == 2 * p + 1, d1, 0.0)
                for p in grp:
                    g = p // 2
                    dva_ref[g, win, :] += lax.dot_general(sm[p][0].astype(BF16), do2[p], (((0,), (0,)), ((), ())),
                                                          preferred_element_type=F32)
                    dka_ref[g, win, :] += lax.dot_general(dsl[p], q2[p], (((0,), (0,)), ((), ())),
                                                          preferred_element_type=F32)
                for p in grp:
                    dq2 = jnp.dot(dsl[p], ka_ref[p // 2, win, :], preferred_element_type=F32)
                    dqr = jnp.where(left, dq2[0:QBLK], dq2[QBLK:2 * QBLK]) * 0.125
                    dq, dw = _norm_rope_bwd(dqr, qn[p][1], qn[p][2], qw_ref[...], cos, sin, left, first)
                    dq_ref[rows, lane_of[p]] = dq.astype(BF16)
                    dqw = dqw + dw

            @pl.when(n % per_put == per_put - 1)
            def _():
                _put_all(outs, osems, n // per_put)

            return dqw, dsk

        zero = jnp.zeros((1, 128), F32)
        dqw, dsk = lax.fori_loop(0, nblk, blk, (zero, zero))

        ch = 256

        def chunk(i, dkw):
            r0 = pl.multiple_of(i * ch, ch)
            left = _lane((ch, 128)) < 64
            first = (_lane((ch, 128)) % 64) < 32
            rows = pl.ds(r0, ch)
            prow = pl.ds(QBLK + r0, ch)

            def fold(ref):
                a0 = ref[0, prow, :]
                a1 = ref[1, prow, :]
                return jnp.where(left, a0 + pltpu.roll(a0, 64, 1), a1 + pltpu.roll(a1, 64, 1))

            cos = cos_ref[rows, :]
            sin = sin_ref[rows, :]
            _, xh, r = _norm_rope(kv_ref[rows, 0:128], kw_ref[...], cos, sin, left, first)
            dk, dw = _norm_rope_bwd(fold(dka_ref), xh, r, kw_ref[...], cos, sin, left, first)
            dkv_ref[rows, 0:128] = dk.astype(BF16)
            dkv_ref[rows, 128:256] = fold(dva_ref).astype(BF16)
            return dkw + dw

        dkw = lax.fori_loop(0, t // ch, chunk, zero)
        sm_ref[...] = jnp.zeros((8, 128), F32)
        sm_ref[0:1, :] = dqw + pltpu.roll(dqw, 64, 1)
        sm_ref[1:2, :] = dkw + pltpu.roll(dkw, 64, 1)
        sm_ref[2:3, :] = dsk

        j = 2 * x + y
        sends.append(_rs_total(rc, 0, mine, sibo_ref, outo_ref, ino_ref, gwo_ref, OUT_HALF, j, c, sib))
        _rs_done(rc, 0, gwo_ref, c, sib)
        for cp in sends:
            cp.wait_send()
        _put_wait(outs, osems, t // PUT_ROWS)

    vm = pl.BlockSpec(memory_space=pltpu.VMEM)
    hbm = pl.BlockSpec(memory_space=pl.ANY)
    return pl.pallas_call(
        body,
        name="attn_bwd",
        in_specs=[hbm, vm, hbm, hbm, hbm, vm, vm, pl.BlockSpec(memory_space=pltpu.SMEM), hbm, hbm, vm],
        out_specs=[hbm, vm, hbm, vm, vm],
        out_shape=[jax.ShapeDtypeStruct((t, ATTN_W), BF16), jax.ShapeDtypeStruct((t, 2 * KV_W), BF16),
                   jax.ShapeDtypeStruct((t, ATTN_W), BF16), jax.ShapeDtypeStruct((8, 128), F32),
                   jax.ShapeDtypeStruct((2, OUT_HALF, D_MODEL), F32)],
        scratch_shapes=[pltpu.VMEM((2, t + QBLK, 128), BF16), pltpu.VMEM((2, t + QBLK, 128), BF16),
                        pltpu.VMEM((2, t + QBLK, 128), F32), pltpu.VMEM((2, t + QBLK, 128), F32)]
        + _rs_scratch(OUT_HALF)
        + [pltpu.VMEM((t, ATTN_W), F32), pltpu.VMEM((t, ATTN_W), F32), pltpu.VMEM((t, ATTN_W), BF16),
           pltpu.VMEM((t, ATTN_W), BF16), pltpu.VMEM((t, ATTN_W), BF16), pltpu.VMEM((t, ATTN_W), BF16),
           pltpu.VMEM((t, 128), F32), pltpu.VMEM((t, 128), F32),
           pltpu.SemaphoreType.DMA((6,)), pltpu.SemaphoreType.DMA((t // PUT_ROWS,)), pltpu.SemaphoreType.DMA((t // PUT_ROWS,)),
           pltpu.SemaphoreType.DMA((RS_SEMS,)), pltpu.SemaphoreType.DMA((RS_SEMS,))],
        compiler_params=_cparams(),
    )(q_raw, kv_raw, ga, o, dmix, qw2, kw2, sinks, cos_f, sin_s, go)


CONV_CH = 256
CONV_SUB = 128
CONV_ACCS = 1


def _shifted_windows(src_ref, r0, sh_ref):
    rows = CONV_CH + CONV_PAD
    win = src_ref[pl.ds(r0, rows), :]
    for b in range(8):
        sh = win if b == 0 else pltpu.roll(win, rows - b, 0)
        for c in range(CONV_W // 128):
            sh_ref[b, c] = sh[:, c * 128:(c + 1) * 128]


def _conv_fwd(ua, ug, gb, cw, cb, lw, lb):
    t = ua.shape[0]

    def body(ua_hbm, ug_hbm, gb_hbm, cw_ref, cb_ref, lw_ref, lb_ref, cz_hbm, mix_hbm, zp_ref, sh_ref,
             ua_ref, ug_ref, gb_ref, cz_ref, mix_ref, isem, osem0, osem1):
        loads = _fetch((ua_hbm, ug_hbm, gb_hbm), (ua_ref, ug_ref, gb_ref), isem)
        outs, osems = ((cz_ref, cz_hbm), (mix_ref, mix_hbm)), (osem0, osem1)
        per_put = PUT_ROWS // CONV_CH
        zp_ref[0:CONV_PAD, :] = jnp.zeros((CONV_PAD, CONV_W), F32)
        loads[0].wait()
        loads[1].wait()

        def glu(i, carry):
            r0 = pl.multiple_of(i * CONV_CH, CONV_CH)
            rows = pl.ds(r0, CONV_CH)
            zp_ref[pl.ds(CONV_PAD + r0, CONV_CH), :] = ua_ref[rows, :] * _sigmoid(ug_ref[rows, :])
            return carry

        lax.fori_loop(0, t // CONV_CH, glu, 0)
        loads[2].wait()

        def chunk(i, carry):
            r0 = pl.multiple_of(i * CONV_CH, CONV_CH)
            _shifted_windows(zp_ref, r0, sh_ref)
            for c in range(CONV_W // 128):
                lanes = slice(c * 128, (c + 1) * 128)

                def sub(k, carry2):
                    b0 = pl.multiple_of(k * CONV_SUB, CONV_SUB)
                    acc = [jnp.broadcast_to(cb_ref[0:1, lanes], (CONV_SUB, 128))] + [None] * (CONV_ACCS - 1)
                    for j in range(CONV_TAPS):
                        off = j + CONV_PAD - (CONV_TAPS - 1)
                        term = sh_ref[off % 8, c, pl.ds(b0 + 8 * (off // 8), CONV_SUB), :] * cw_ref[j:j + 1, lanes]
                        acc[j % CONV_ACCS] = term if acc[j % CONV_ACCS] is None else acc[j % CONV_ACCS] + term
                    cz_ref[pl.ds(r0 + b0, CONV_SUB), lanes] = functools.reduce(lambda a, b: a + b, acc)
                    return carry2

                lax.fori_loop(0, CONV_CH // CONV_SUB, sub, 0)
            rows = pl.ds(r0, CONV_CH)
            cz = cz_ref[rows, :]
            mu = jnp.mean(cz, axis=-1, keepdims=True)
            xc = cz - mu
            rs = lax.rsqrt(jnp.mean(xc * xc, axis=-1, keepdims=True) + EPS)
            ln = xc * rs * lw_ref[...] + lb_ref[...]
            mix_ref[rows, :] = (_silu(ln) * _silu(gb_ref[rows, :])).astype(BF16)

            @pl.when(i % per_put == per_put - 1)
            def _():
                _put_all(outs, osems, i // per_put)

            return carry

        lax.fori_loop(0, t // CONV_CH, chunk, 0)
        _put_wait(outs, osems, t // PUT_ROWS)

    vm = pl.BlockSpec(memory_space=pltpu.VMEM)
    hbm = pl.BlockSpec(memory_space=pl.ANY)
    nput = t // PUT_ROWS
    return pl.pallas_call(
        body,
        name="conv_fwd",
        in_specs=[hbm] * 3 + [vm] * 4,
        out_specs=[hbm, hbm],
        out_shape=[jax.ShapeDtypeStruct((t, CONV_W), F32), jax.ShapeDtypeStruct((t, CONV_W), BF16)],
        scratch_shapes=[pltpu.VMEM((t + CONV_PAD, CONV_W), F32),
                        pltpu.VMEM((8, CONV_W // 128, CONV_CH + CONV_PAD, 128), F32),
                        pltpu.VMEM((t, CONV_W), F32), pltpu.VMEM((t, CONV_W), F32), pltpu.VMEM((t, CONV_W), F32),
                        pltpu.VMEM((t, CONV_W), F32), pltpu.VMEM((t, CONV_W), BF16),
                        pltpu.SemaphoreType.DMA((3,)), pltpu.SemaphoreType.DMA((nput,)), pltpu.SemaphoreType.DMA((nput,))],
        compiler_params=_cparams(),
    )(ua, ug, gb, cw, cb, lw, lb)


def _conv_bwd(ua, ug, gb, cz, dmix, cw, lw, lb):
    t = ua.shape[0]

    def body(ua_hbm, ug_hbm, gb_hbm, cz_hbm, dm_hbm, cw_ref, lw_ref, lb_ref,
             dua_hbm, dug_hbm, dgb_hbm, dcw_ref, dvec_ref, zp_ref, dp_ref, sh_ref, wacc_ref,
             ua_ref, ug_ref, gb_ref, cz_ref, dm_ref, dua_ref, dug_ref, dgb_ref, isem, osem0, osem1, osem2):
        loads = _fetch((ua_hbm, ug_hbm, gb_hbm, cz_hbm, dm_hbm), (ua_ref, ug_ref, gb_ref, cz_ref, dm_ref), isem)
        per_put = PUT_ROWS // CONV_CH
        zp_ref[0:CONV_PAD, :] = jnp.zeros((CONV_PAD, CONV_W), F32)
        dp_ref[t:t + CONV_PAD, :] = jnp.zeros((CONV_PAD, CONV_W), F32)
        wacc_ref[...] = jnp.zeros_like(wacc_ref)
        for cp in loads:
            cp.wait()

        def pointwise(i, carry):
            dcb, dlw, dlb = carry
            r0 = pl.multiple_of(i * CONV_CH, CONV_CH)
            rows = pl.ds(r0, CONV_CH)
            zp_ref[pl.ds(CONV_PAD + r0, CONV_CH), :] = ua_ref[rows, :] * _sigmoid(ug_ref[rows, :])
            cz = cz_ref[rows, :]
            mu = jnp.mean(cz, axis=-1, keepdims=True)
            xc = cz - mu
            rs = lax.rsqrt(jnp.mean(xc * xc, axis=-1, keepdims=True) + EPS)
            xh = xc * rs
            ln = xh * lw_ref[...] + lb_ref[...]
            gbv = gb_ref[rows, :]
            dy = dm_ref[rows, :].astype(F32)
            dgb_ref[rows, :] = (dy * _silu(ln) * _dsilu(gbv)).astype(BF16)
            dl = dy * _silu(gbv) * _dsilu(ln)
            dxh = dl * lw_ref[...]
            dcz = rs * (dxh - jnp.mean(dxh, axis=-1, keepdims=True)
                        - xh * jnp.mean(dxh * xh, axis=-1, keepdims=True))
            dp_ref[rows, :] = dcz

            @pl.when(i % per_put == per_put - 1)
            def _():
                _put(dgb_ref, dgb_hbm, osem2, i // per_put).start()

            return (dcb + jnp.sum(dcz, axis=0, keepdims=True),
                    dlw + jnp.sum(dl * xh, axis=0, keepdims=True),
                    dlb + jnp.sum(dl, axis=0, keepdims=True))

        zero = jnp.zeros((1, CONV_W), F32)
        dcb, dlw, dlb = lax.fori_loop(0, t // CONV_CH, pointwise, (zero, zero, zero))
        dvec_ref[...] = jnp.zeros((8, CONV_W), F32)
        dvec_ref[0:1, :] = dcb
        dvec_ref[1:2, :] = dlw
        dvec_ref[2:3, :] = dlb

        def chunk(i, carry):
            r0 = pl.multiple_of(i * CONV_CH, CONV_CH)
            _shifted_windows(dp_ref, r0, sh_ref)
            for c in range(CONV_W // 128):
                lanes = slice(c * 128, (c + 1) * 128)

                def sub(k, carry2):
                    b0 = pl.multiple_of(k * CONV_SUB, CONV_SUB)
                    acc = [None] * CONV_ACCS
                    for j in range(CONV_TAPS):
                        off = CONV_TAPS - 1 - j
                        term = sh_ref[off % 8, c, pl.ds(b0 + 8 * (off // 8), CONV_SUB), :] * cw_ref[j:j + 1, lanes]
                        acc[j % CONV_ACCS] = term if acc[j % CONV_ACCS] is None else acc[j % CONV_ACCS] + term
                    acc = functools.reduce(lambda a, b: a + b, acc)
                    rr = pl.ds(r0 + b0, CONV_SUB)
                    sg = _sigmoid(ug_ref[rr, lanes])
                    dua_ref[rr, lanes] = (acc * sg).astype(BF16)
                    dug_ref[rr, lanes] = (acc * ua_ref[rr, lanes] * sg * (1.0 - sg)).astype(BF16)
                    return carry2

                lax.fori_loop(0, CONV_CH // CONV_SUB, sub, 0)
            _shifted_windows(zp_ref, r0, sh_ref)
            for c in range(CONV_W // 128):
                lanes = slice(c * 128, (c + 1) * 128)

                def subw(k, carry2):
                    b0 = pl.multiple_of(k * CONV_SUB, CONV_SUB)
                    dcz = dp_ref[pl.ds(r0 + b0, CONV_SUB), lanes]
                    for j in range(CONV_TAPS):
                        off = j + CONV_PAD - (CONV_TAPS - 1)
                        pr = dcz * sh_ref[off % 8, c, pl.ds(b0 + 8 * (off // 8), CONV_SUB), :]
                        parts = [pr[8 * q:8 * (q + 1)] for q in range(CONV_SUB // 8)]
                        while len(parts) > 1:
                            parts = [a + b for a, b in zip(parts[0::2], parts[1::2])]
                        wacc_ref[8 * j:8 * (j + 1), lanes] += parts[0]
                    return carry2

                lax.fori_loop(0, CONV_CH // CONV_SUB, subw, 0)

            @pl.when(i % per_put == per_put - 1)
            def _():
                _put_all(((dua_ref, dua_hbm), (dug_ref, dug_hbm)), (osem0, osem1), i // per_put)

            return carry

        lax.fori_loop(0, t // CONV_CH, chunk, 0)
        _put_wait(((dua_ref, dua_hbm), (dug_ref, dug_hbm), (dgb_ref, dgb_hbm)), (osem0, osem1, osem2), t // PUT_ROWS)
        dcw_ref[...] = jnp.zeros((16, 2 * CONV_W), F32)
        for j in range(CONV_TAPS):
            dcw_ref[j // 2:j // 2 + 1, CONV_W * (j % 2):CONV_W * (j % 2 + 1)] = jnp.sum(
                wacc_ref[8 * j:8 * (j + 1), :], axis=0, keepdims=True)

    vm = pl.BlockSpec(memory_space=pltpu.VMEM)
    hbm = pl.BlockSpec(memory_space=pl.ANY)
    return pl.pallas_call(
        body,
        name="conv_bwd",
        in_specs=[hbm] * 5 + [vm] * 3,
        out_specs=[hbm] * 3 + [vm] * 2,
        out_shape=[jax.ShapeDtypeStruct((t, CONV_W), BF16)] * 3
        + [jax.ShapeDtypeStruct((16, 2 * CONV_W), F32), jax.ShapeDtypeStruct((8, CONV_W), F32)],
        scratch_shapes=[pltpu.VMEM((t + CONV_PAD, CONV_W), F32), pltpu.VMEM((t + CONV_PAD, CONV_W), F32),
                        pltpu.VMEM((8, CONV_W // 128, CONV_CH + CONV_PAD, 128), F32), pltpu.VMEM((8 * 32, CONV_W), F32)]
        + [pltpu.VMEM((t, CONV_W), F32)] * 4 + [pltpu.VMEM((t, CONV_W), BF16)] * 4
        + [pltpu.SemaphoreType.DMA((5,))] + [pltpu.SemaphoreType.DMA((t // PUT_ROWS,))] * 3,
        compiler_params=_cparams(),
    )(ua, ug, gb, cz, dmix, cw, lw, lb)


def _out_proj(mix_a, mix_b, x, tgt, gate, w_out):
    t = x.shape[0]
    tm = 512
    nstep = t // tm

    def body(ma_ref, mb_ref, x_ref, t_ref, g_ref, w_ref, dout_ref, dma_ref, dmb_ref, gw_ref, red_ref, acc_ref):
        i = pl.program_id(0)

        @pl.when(i == 0)
        def _():
            acc_ref[...] = jnp.zeros_like(acc_ref)
            red_ref[...] = jnp.zeros_like(red_ref)

        mix = jnp.concatenate([ma_ref[...], mb_ref[...]], axis=1)
        y = jnp.dot(mix, w_ref[...], preferred_element_type=F32)
        gate_v = g_ref[...]
        err = x_ref[...] + gate_v * y - t_ref[...]
        dout = err * (1.0 / D_MODEL)
        dout_ref[...] = dout
        red_ref[0:1, :] += jnp.sum(dout * y, axis=0, keepdims=True)
        red_ref[1:2, :] += jnp.sum(err * err, axis=0, keepdims=True)
        dy = (dout * gate_v).astype(BF16)
        dmix = lax.dot_general(dy, w_ref[...], (((1,), (1,)), ((), ())), preferred_element_type=F32)
        dma_ref[...] = dmix[:, 0:512].astype(BF16)
        dmb_ref[...] = dmix[:, 512:1024].astype(BF16)
        acc_ref[...] += lax.dot_general(mix, dy, (((0,), (0,)), ((), ())), preferred_element_type=F32)

        @pl.when(i == nstep - 1)
        def _():
            gw_ref[...] = acc_ref[...].astype(BF16)

    row = lambda w: pl.BlockSpec((tm, w), lambda i: (i, 0))
    const = lambda s: pl.BlockSpec(s, lambda i: (0, 0))
    return pl.pallas_call(
        body,
        name="out_proj",
        grid=(nstep,),
        in_specs=[row(512), row(512), row(D_MODEL), row(D_MODEL), const((1, D_MODEL)),
                  pl.BlockSpec((D_MODEL, D_MODEL), lambda i: (0, 0), pipeline_mode=pl.Buffered(1))],
        out_specs=[row(D_MODEL), row(512), row(512), const((D_MODEL, D_MODEL)), const((8, D_MODEL))],
        out_shape=[jax.ShapeDtypeStruct((t, D_MODEL), F32), jax.ShapeDtypeStruct((t, 512), BF16),
                   jax.ShapeDtypeStruct((t, 512), BF16), jax.ShapeDtypeStruct((D_MODEL, D_MODEL), BF16),
                   jax.ShapeDtypeStruct((8, D_MODEL), F32)],
        scratch_shapes=[pltpu.VMEM((D_MODEL, D_MODEL), F32)],
        compiler_params=_cparams(dimension_semantics=("arbitrary",)),
    )(mix_a, mix_b, x, tgt, gate, w_out)


DPROJ_WIDTHS = (512, 256, 512, 512, 512, 512)
DPROJ_STARTS = (0, 512, 768, 1280, 1792, 2304)
WIN_W = 768
WIN_START = (0, 640, 1408, 2048)
WIN_OFF = (0, 64, 0, 64)
N_GW = N_CHIPS


def _window_pieces(s):
    lo, hi = WIN_START[s], WIN_START[s] + WIN_W
    out = []
    for p, (st, w) in enumerate(zip(DPROJ_STARTS, DPROJ_WIDTHS)):
        a, b = max(lo, st), min(hi, st + w)
        if a < b:
            out.append((p, a - st, b - a, a - lo))
    return out


def _in_proj_bwd(dparts, h, x, dout, s1, nw, wt_full, dcw, dvec, sm_a, row0):
    t = x.shape[0]
    tm = 256
    nstep = N_GW + t // tm
    n_sem = 20
    rows0 = 32
    hs = rows0 // 2
    npart = len(DPROJ_WIDTHS)

    def body(*refs):
        d_hbm, d_ref = refs[:npart], refs[npart:2 * npart]
        (x_ref, dout_ref, s1_ref, nw_ref, h_ref, wt_hbm, dcw_ref, dvec_ref, sma_ref, row0_ref,
         gx_ref, gw_hbm, ssum_ref, rows_ref,
         stg_ref, wt_ref, gt_ref, sib_ref, out_ref, in_ref, res_ref, sall_ref, red_ref, sm0_ref, ssib_ref, schip_ref, sres_ref,
         wsem, lsem, ssem, rsem) = refs[2 * npart:]
        i = pl.program_id(0)
        x_, y_, c, chips = _place()
        j = 2 * x_ + y_
        dev = 2 * j + c
        sib = (x_, y_, 1 - c)
        rc = functools.partial(_remote, ssem, rsem)
        rel_chip = [2 * cx + cy for cx, cy in chips] + [j]
        peers = [(px, py, pc) for px in (x_, 1 - x_) for py in (y_, 1 - y_) for pc in (c, 1 - c)][1:]
        wt_copy = pltpu.make_async_copy(wt_hbm, wt_ref, lsem.at[0])

        def window(case, slot):
            return [pltpu.make_async_copy(d_hbm[p].at[:, pl.ds(c0, w)], stg_ref.at[slot, :, pl.ds(w0, w)], wsem.at[slot, n])
                    for n, (p, c0, w, w0) in enumerate(_window_pieces(case))]

        def to_sibling(k):
            return rc(k, gt_ref.at[k, 1 - c], sib_ref.at[k], sib)

        def to_chip(k):
            return rc(4 + k, out_ref.at[k], in_ref.at[k], (*chips[k], c))

        def trade(k):
            to_sibling(k).wait_recv()

            def add(n, carry):
                rr = pl.ds(pl.multiple_of(n * RS_CH, RS_CH), RS_CH)
                out_ref[k, rr, :] = (gt_ref[k, c, rr, :].astype(F32) + sib_ref[k, rr, :].astype(F32)).astype(BF16)
                return carry

            lax.fori_loop(0, IN_HALF // RS_CH, add, 0)
            to_chip(k).start()

        mine_s = pl.ds(pl.multiple_of(c * hs, 8), hs)
        other_s = pl.ds(pl.multiple_of((1 - c) * hs, 8), hs)

        def small_to_sibling():
            return rc(15, sm0_ref.at[other_s], ssib_ref, sib)

        def small_to_chip(k):
            return rc(16 + k, schip_ref.at[j], schip_ref.at[j], (*chips[k], c))

        def small_share():
            return rc(19, sres_ref.at[c], sres_ref.at[c], sib)

        for k in range(N_GW):
            @pl.when(i == k)
            def _(k=k):
                slot = k % 2
                if k == 0:
                    red_ref[...] = jnp.zeros_like(red_ref)
                    wt_copy.start()
                    sm0_ref[...] = jnp.zeros_like(sm0_ref)
                    sm0_ref[0:16, :] = dcw_ref[...]
                    sm0_ref[16:17, 0:CONV_W] = dvec_ref[0:1, :]
                    sm0_ref[16:17, CONV_W:2 * CONV_W] = dvec_ref[1:2, :]
                    sm0_ref[17:18, 0:CONV_W] = dvec_ref[2:3, :]
                    for r in range(3):
                        sm0_ref[17:18, CONV_W + 128 * r:CONV_W + 128 * (r + 1)] = sma_ref[r:r + 1, :]
                    sm0_ref[18:19, :] = row0_ref[1:2, :]
                    small_to_sibling().start()
                if k == 1:
                    small_to_sibling().wait_recv()
                    schip_ref[j] = sm0_ref[mine_s, :] + ssib_ref[...]
                    for kk in range(3):
                        small_to_chip(kk).start()
                if k == N_GW - 1:
                    for kk in range(3):
                        jk = rel_chip[kk]
                        rc(16 + kk, schip_ref.at[jk], schip_ref.at[jk], sib).wait_recv()
                    tot = schip_ref[0]
                    for d in range(1, N_CHIPS):
                        tot = tot + schip_ref[d]
                    sres_ref[c] = tot
                    small_share().start()
                for case in range(N_CHIPS):
                    if k == 0:
                        @pl.when(rel_chip[0] == case)
                        def _():
                            for cp in window(case, 0):
                                cp.start()
                    if k + 1 < N_GW:
                        @pl.when(rel_chip[k + 1] == case)
                        def _():
                            for cp in window(case, 1 - slot):
                                cp.start()
                for case in range(N_CHIPS):
                    @pl.when(rel_chip[k] == case)
                    def _():
                        for cp in window(case, slot):
                            cp.wait()
                g = lax.dot_general(stg_ref[slot], h_ref[...], (((0,), (0,)), ((), ())), preferred_element_type=F32)
                for off in sorted(set(WIN_OFF)):
                    @pl.when(rel_chip[k] % 2 == (1 if off else 0))
                    def _():
                        gt_ref[k, 0] = g[off:off + IN_HALF].astype(BF16)
                        gt_ref[k, 1] = g[off + IN_HALF:off + 2 * IN_HALF].astype(BF16)
                to_sibling(k).start()
                if k >= 1:
                    trade(k - 1)

        @pl.when(i == N_GW)
        def _():
            wt_copy.wait()

        @pl.when(i >= N_GW)
        def _():
            xv = x_ref[...]
            r = lax.rsqrt(jnp.mean(xv * xv, axis=-1, keepdims=True) + EPS)
            xh = xv * r
            n = xh * nw_ref[...]
            dproj = jnp.concatenate([ref[...] for ref in d_ref], axis=1)
            dh = jnp.dot(dproj, wt_ref[...], preferred_element_type=F32)
            red_ref[0:1, :] += jnp.sum(dh, axis=0, keepdims=True)
            red_ref[1:2, :] += jnp.sum(dh * n, axis=0, keepdims=True)
            dn = dh * s1_ref[...]
            red_ref[2:3, :] += jnp.sum(dn * xh, axis=0, keepdims=True)
            dxh = dn * nw_ref[...]
            gx_ref[...] = dout_ref[...] + r * (dxh - xh * jnp.mean(dxh * xh, axis=-1, keepdims=True))

        @pl.when(i == nstep - 1)
        def _():
            sall_ref[dev] = row0_ref[...]
            sall_ref[dev, 2:5, :] = red_ref[0:3, :]
            sends = [rc(8 + k, sall_ref.at[dev], sall_ref.at[dev], peer) for k, peer in enumerate(peers)]
            for cp in sends:
                cp.start()
            sends += [to_sibling(k) for k in range(N_GW)] + [to_chip(k) for k in range(3)]
            sends += [small_to_sibling(), small_share()] + [small_to_chip(k) for k in range(3)]
            own = N_GW - 1
            to_sibling(own).wait_recv()
            for k in range(3):
                to_chip(k).wait_recv()

            def total(n, carry):
                rr = pl.ds(pl.multiple_of(n * RS_CH, RS_CH), RS_CH)
                acc = gt_ref[own, c, rr, :].astype(F32) + sib_ref[own, rr, :].astype(F32)
                for k in range(3):
                    acc = acc + in_ref[k, rr, :].astype(F32)
                res_ref[c, rr, :] = acc
                return carry

            lax.fori_loop(0, IN_HALF // RS_CH, total, 0)
            share = rc(7, res_ref.at[c], res_ref.at[c], sib)
            share.start()
            sends.append(share)
            for k, (px, py, pc) in enumerate(peers):
                pdev = 4 * px + 2 * py + pc
                rc(8 + k, sall_ref.at[pdev], sall_ref.at[pdev], (px, py, pc)).wait_recv()
            rows_ref[...] = sall_ref[...]
            rc(19, sres_ref.at[1 - c], sres_ref.at[1 - c], sib).wait_recv()
            ssum_ref[0:hs, :] = sres_ref[0]
            ssum_ref[hs:rows0, :] = sres_ref[1]
            rc(7, res_ref.at[1 - c], res_ref.at[1 - c], sib).wait_recv()
            back = pltpu.make_async_copy(res_ref, gw_hbm, lsem.at[1])
            back.start()
            for cp in sends:
                cp.wait_send()
            back.wait()

    blk = lambda i: jnp.maximum(i - N_GW, 0)
    row = lambda w: pl.BlockSpec((tm, w), lambda i: (blk(i), 0))
    vec = pl.BlockSpec((1, D_MODEL), lambda i: (0, 0))
    const = lambda shape: pl.BlockSpec(shape, lambda i: (0,) * len(shape))
    hbm = pl.BlockSpec(memory_space=pl.ANY)
    return pl.pallas_call(
        body,
        name="in_proj_bwd",
        grid=(nstep,),
        in_specs=[hbm] * npart + [row(w) for w in DPROJ_WIDTHS] + [row(D_MODEL), row(D_MODEL), vec, vec,
                  pl.BlockSpec((t, D_MODEL), lambda i: (0, 0), pipeline_mode=pl.Buffered(1)), hbm, const((16, D_MODEL)),
                  const((8, CONV_W)), const((8, 128)), const((8, D_MODEL))],
        out_specs=[row(D_MODEL), hbm, const((rows0, D_MODEL)), const((N_DEV, 8, D_MODEL))],
        out_shape=[jax.ShapeDtypeStruct((t, D_MODEL), F32), jax.ShapeDtypeStruct((2, IN_HALF, D_MODEL), F32),
                   jax.ShapeDtypeStruct((rows0, D_MODEL), F32), jax.ShapeDtypeStruct((N_DEV, 8, D_MODEL), F32)],
        scratch_shapes=[pltpu.VMEM((2, t, WIN_W), BF16), pltpu.VMEM((IN_W, D_MODEL), BF16),
                        pltpu.VMEM((N_CHIPS, 2, IN_HALF, D_MODEL), BF16), pltpu.VMEM((N_CHIPS, IN_HALF, D_MODEL), BF16),
                        pltpu.VMEM((3, IN_HALF, D_MODEL), BF16), pltpu.VMEM((3, IN_HALF, D_MODEL), BF16),
                        pltpu.VMEM((2, IN_HALF, D_MODEL), F32), pltpu.VMEM((N_DEV, 8, D_MODEL), F32),
                        pltpu.VMEM((8, D_MODEL), F32), pltpu.VMEM((rows0, D_MODEL), F32), pltpu.VMEM((hs, D_MODEL), F32),
                        pltpu.VMEM((N_CHIPS, hs, D_MODEL), F32),
                        pltpu.VMEM((2, hs, D_MODEL), F32), pltpu.SemaphoreType.DMA((2, 3)), pltpu.SemaphoreType.DMA((2,)),
                        pltpu.SemaphoreType.DMA((n_sem,)), pltpu.SemaphoreType.DMA((n_sem,))],
        compiler_params=_cparams(dimension_semantics=("arbitrary",)),
    )(*dparts, *dparts, x, dout, s1, nw, h, wt_full, dcw, dvec, sm_a, row0)


MESH = pl.DeviceIdType.MESH


def _place():
    x, y, c = lax.axis_index("x"), lax.axis_index("y"), lax.axis_index("c")
    chips = [(1 - x, y), (x, 1 - y), (1 - x, 1 - y)]
    return x, y, c, chips


def _remote(sems_s, sems_r, k, src, dst, to):
    return pltpu.make_async_remote_copy(src_ref=src, dst_ref=dst, send_sem=sems_s.at[k], recv_sem=sems_r.at[k],
                                        device_id=to, device_id_type=MESH)


RS_CH = 32
RS_SEMS = 5


def _rs_to_sibling(rc, s0, theirs, sib_ref, sib):
    cp = rc(s0, theirs, sib_ref, sib)
    cp.start()
    return cp


def _rs_trade(rc, s0, theirs, mine, sib_ref, out_ref, in_ref, rows, c, sib, chips):
    rc(s0, theirs, sib_ref, sib).wait_recv()
    cps = []
    for k, (cx, cy) in enumerate(chips):
        jk = 2 * cx + cy

        def add(i, carry, jk=jk, k=k):
            rr = pl.ds(pl.multiple_of(i * RS_CH, RS_CH), RS_CH)
            out_ref[k, rr, :] = (mine[jk, rr, :].astype(F32) + sib_ref[jk, rr, :].astype(F32)).astype(BF16)
            return carry

        lax.fori_loop(0, rows // RS_CH, add, 0)
        cps.append(rc(s0 + 1 + k, out_ref.at[k], in_ref.at[k], (cx, cy, c)))
        cps[-1].start()
    return cps


def _rs_total(rc, s0, mine, sib_ref, out_ref, in_ref, res_ref, rows, j, c, sib):
    for k in range(3):
        rc(s0 + 1 + k, out_ref.at[k], in_ref.at[k], sib).wait_recv()

    def total(i, carry):
        rr = pl.ds(pl.multiple_of(i * RS_CH, RS_CH), RS_CH)
        acc = mine[j, rr, :].astype(F32) + sib_ref[j, rr, :].astype(F32)
        for k in range(3):
            acc = acc + in_ref[k, rr, :].astype(F32)
        res_ref[c, rr, :] = acc
        return carry

    lax.fori_loop(0, rows // RS_CH, total, 0)
    cp = rc(s0 + 4, res_ref.at[c], res_ref.at[c], sib)
    cp.start()
    return cp


def _rs_done(rc, s0, res_ref, c, sib):
    rc(s0 + 4, res_ref.at[1 - c], res_ref.at[1 - c], sib).wait_recv()


def _rs_scratch(rows):
    return [pltpu.VMEM((N_CHIPS, rows, D_MODEL), BF16), pltpu.VMEM((3, rows, D_MODEL), BF16),
            pltpu.VMEM((3, rows, D_MODEL), BF16)]


MAIN_W = 640
MAIN_DST = (((0, 0, 512), (1, 0, 128)), ((2, 0, 512), (3, 0, 128)), ((3, 128, 384), (4, 0, 256)), ((4, 384, 128), (5, 0, 512)))
PAIR_DST = ((1, 128, 128), (4, 256, 128))


def _in_proj_gather(x, wt, c_row, w_ada, b_ada, nw):
    t = x.shape[0]
    ch = 512
    n_sem = 16

    def body(x_hbm, wt_ref, c_ref, wada_ref, bada_ref, nw_ref,
             q_hbm, kv_hbm, ga_hbm, ua_hbm, ug_hbm, gb_hbm, h_hbm, w4_hbm, call_ref, ada_ref,
             x_ref, h_ref, w4_ref, stg_ref, pstg_ref, part_ref, lsem, osem, wsem, ssem, rsem):
        outs = (q_hbm, kv_hbm, ga_hbm, ua_hbm, ug_hbm, gb_hbm)
        x_, y_, c, chips = _place()
        j = 2 * x_ + y_
        dev = 2 * j + c
        sib = (x_, y_, 1 - c)
        idx = [2 * cx + cy for cx, cy in chips]
        rc = functools.partial(_remote, ssem, rsem)
        x_copy = pltpu.make_async_copy(x_hbm, x_ref, lsem.at[0])
        x_copy.start()

        def rows_of(s, cc):
            return pl.ds(pl.multiple_of(2 * IN_HALF * s + IN_HALF * cc, 16), IN_HALF)

        w4_ref[rows_of(j, 0), :] = wt_ref[0].astype(BF16)
        w4_ref[rows_of(j, 1), :] = wt_ref[1].astype(BF16)
        call_ref[dev] = c_ref[...]
        sends = []
        peers = [(px, py, pc) for px in (x_, 1 - x_) for py in (y_, 1 - y_) for pc in (c, 1 - c)][1:]
        for k, peer in enumerate(peers):
            sends.append(rc(k, call_ref.at[dev], call_ref.at[dev], peer))
        for cp in sends:
            cp.start()

        for k, (px, py, pc) in enumerate(peers):
            pdev = 4 * px + 2 * py + pc
            rc(k, call_ref.at[pdev], call_ref.at[pdev], (px, py, pc)).wait_recv()
        rowid = lax.broadcasted_iota(jnp.int32, (N_DEV, D_MODEL), 0)
        call = jnp.zeros((N_DEV, D_MODEL), F32)
        for r in range(N_DEV):
            call = jnp.where(rowid == r, jnp.broadcast_to(call_ref[r], (N_DEV, D_MODEL)), call)
        bsh = bada_ref[:, 0:ADA_SHARD]
        for k in range(1, N_CHIPS):
            bsh = jnp.where(j == k, bada_ref[:, ADA_SHARD * k:ADA_SHARD * (k + 1)], bsh)
        part = jnp.dot(_silu(call).astype(BF16), wada_ref[...].astype(BF16), preferred_element_type=F32) + bsh
        for r in range(N_DEV):
            part_ref[r] = part[r:r + 1, :]
        ada_ref[j] = part_ref[dev]
        for k, chip in enumerate(chips):
            sends.append(rc(13 + k, part_ref.at[2 * idx[k] + c], ada_ref.at[j], (*chip, c)))
            sends[-1].start()
        for k, chip in enumerate(chips):
            sends.append(rc(7 + k, w4_ref.at[rows_of(j, c)], w4_ref.at[rows_of(j, c)], (*chip, c)))
            sends[-1].start()

        x_copy.wait()

        def prenorm(i, carry):
            rr = pl.ds(pl.multiple_of(i * ch, ch), ch)
            xv = x_ref[rr, :]
            r = lax.rsqrt(jnp.mean(xv * xv, axis=-1, keepdims=True) + EPS)
            x_ref[rr, :] = (xv * r) * nw_ref[...]
            return carry

        lax.fori_loop(0, t // ch, prenorm, 0)
        for k in range(3):
            rc(13 + k, ada_ref.at[idx[k]], ada_ref.at[idx[k]], sib).wait_recv()

        shift = jnp.concatenate([ada_ref[0], ada_ref[1][:, 0:256]], axis=1)
        s1 = 1.0 + jnp.concatenate([ada_ref[1][:, 256:768], ada_ref[2][:, 0:512]], axis=1)

        def norm(i, carry):
            rr = pl.ds(pl.multiple_of(i * ch, ch), ch)
            h_ref[rr, :] = (x_ref[rr, :] * s1 + shift).astype(BF16)
            return carry

        lax.fori_loop(0, t // ch, norm, 0)
        h_copy = pltpu.make_async_copy(h_ref, h_hbm, lsem.at[1])
        h_copy.start()

        def put_main(case, slot):
            cps, col = [], 0
            for n, (a, c0, w) in enumerate(MAIN_DST[case]):
                cps.append(pltpu.make_async_copy(stg_ref.at[slot, :, pl.ds(col, w)], outs[a].at[:, pl.ds(c0, w)], osem.at[slot, n]))
                col += w
            return cps

        def put_pair(case, slot):
            a, c0, w = PAIR_DST[case]
            return pltpu.make_async_copy(pstg_ref.at[slot], outs[a].at[:, pl.ds(c0, w)], osem.at[slot, 2])

        def project(first_row, width, dst, slot):
            wrows = pl.ds(pl.multiple_of(first_row, 128), width)

            def blk(i, carry):
                rr = pl.ds(pl.multiple_of(i * ch, ch), ch)
                dst[slot, rr, :] = lax.dot_general(h_ref[rr, :], w4_ref[wrows, :], (((1,), (1,)), ((), ())),
                                                   preferred_element_type=F32)
                return carry

            lax.fori_loop(0, t // ch, blk, 0)

        def phase(p, s, pair):
            slot = p % 2
            if p >= 2:
                for case in range(N_CHIPS):
                    @pl.when(order[p - 2] == case)
                    def _():
                        for cp in put_main(case, slot):
                            cp.wait()
            if p == 3:
                for case in range(2):
                    @pl.when(j // 2 == case)
                    def _():
                        put_pair(case, 0).wait()
            project(2 * IN_HALF * s + 64 * (s % 2), MAIN_W, stg_ref, slot)
            for case in range(N_CHIPS):
                @pl.when(s == case)
                def _():
                    for cp in put_main(case, slot):
                        cp.start()
            if pair is not None:
                project(MAIN_W + 2 * (2 * IN_HALF) * pair, 128, pstg_ref, slot % 2 if p == 2 else 1)
                for case in range(2):
                    @pl.when(pair == case)
                    def _():
                        put_pair(case, 0 if p == 2 else 1).start()

        order = [j] + idx
        w_out = [pltpu.make_async_copy(w4_ref.at[pl.ds(pl.multiple_of(2 * IN_HALF * s, 32), 2 * IN_HALF)],
                                       w4_hbm.at[pl.ds(pl.multiple_of(2 * IN_HALF * s, 32), 2 * IN_HALF)], wsem.at[p])
                 for p, s in enumerate(order)]
        w_out[0].start()
        phase(0, j, None)
        passed = []
        for k in range(3):
            jk = idx[k]
            rc(7 + k, w4_ref.at[rows_of(jk, c)], w4_ref.at[rows_of(jk, c)], sib).wait_recv()
            passed.append(rc(10 + k, w4_ref.at[rows_of(jk, c)], w4_ref.at[rows_of(jk, c)], sib))
            passed[-1].start()
            rc(10 + k, w4_ref.at[rows_of(jk, 1 - c)], w4_ref.at[rows_of(jk, 1 - c)], sib).wait_recv()
            w_out[1 + k].start()
            if k == 0:
                phase(1, jk, None)
            elif k == 1:
                phase(2, jk, j // 2)
            else:
                phase(3, jk, 1 - j // 2)

        for case in range(N_CHIPS):
            for p in (2, 3):
                @pl.when(order[p] == case)
                def _():
                    for cp in put_main(case, p % 2):
                        cp.wait()
        for case in range(2):
            @pl.when(1 - j // 2 == case)
            def _():
                put_pair(case, 1).wait()
        h_copy.wait()
        for cp in w_out:
            cp.wait()
        for cp in sends + passed:
            cp.wait_send()

    vm = pl.BlockSpec(memory_space=pltpu.VMEM)
    hbm = pl.BlockSpec(memory_space=pl.ANY)
    widths = (512, 256, 512, 512, 512, 512)
    return pl.pallas_call(
        body,
        name="in_proj",
        in_specs=[hbm, vm, vm, vm, vm, vm],
        out_specs=[hbm] * 8 + [vm, vm],
        out_shape=[jax.ShapeDtypeStruct((t, w), F32) for w in widths]
        + [jax.ShapeDtypeStruct((t, D_MODEL), BF16), jax.ShapeDtypeStruct((IN_W, D_MODEL), BF16),
           jax.ShapeDtypeStruct((N_DEV, 1, D_MODEL), F32), jax.ShapeDtypeStruct((N_CHIPS, 1, ADA_SHARD), F32)],
        scratch_shapes=[pltpu.VMEM((t, D_MODEL), F32), pltpu.VMEM((t, D_MODEL), BF16), pltpu.VMEM((IN_W, D_MODEL), BF16),
                        pltpu.VMEM((2, t, MAIN_W), F32), pltpu.VMEM((2, t, 128), F32), pltpu.VMEM((N_DEV, 1, ADA_SHARD), F32),
                        pltpu.SemaphoreType.DMA((2,)), pltpu.SemaphoreType.DMA((2, 3)), pltpu.SemaphoreType.DMA((N_CHIPS,)),
                        pltpu.SemaphoreType.DMA((n_sem,)), pltpu.SemaphoreType.DMA((n_sem,))],
        compiler_params=_cparams(),
    )(x, wt, c_row, w_ada, b_ada, nw)


def _adamw_math(w, g, m, v):
    m2 = ADAM_B1 * m + (1.0 - ADAM_B1) * g
    v2 = ADAM_B2 * v + (1.0 - ADAM_B2) * (g * g)
    m_hat = m2 / (1.0 - ADAM_B1 ** ADAM_STEP)
    v_hat = v2 / (1.0 - ADAM_B2 ** ADAM_STEP)
    delta = -ADAM_LR * (m_hat / (jnp.sqrt(v_hat) + ADAM_EPS) + ADAM_WD * w)
    return delta, m2, v2


def _adamw(name, w, g, m, v, tm, through=None):
    r, cdim = w.shape
    nstep = r // tm
    extra = [] if through is None else [through]

    def body(w_ref, g_ref, m_ref, v_ref, *rest):
        g2_ref, d_ref, m2_ref, v2_ref = rest[len(extra):len(extra) + 4]
        g = g_ref[...]
        g2_ref[...] = g
        d_ref[...], m2_ref[...], v2_ref[...] = _adamw_math(w_ref[...], g, m_ref[...], v_ref[...])
        if extra:
            rest[-1][...] = rest[0][...]

    blk = pl.BlockSpec((tm, cdim), lambda i: (i, 0))
    eblk = [pl.BlockSpec((e.shape[0] // nstep, e.shape[1]), lambda i: (i, 0)) for e in extra]
    return pl.pallas_call(
        body,
        name=name,
        grid=(nstep,),
        in_specs=[blk] * 4 + eblk,
        out_specs=[blk] * 4 + eblk,
        out_shape=[jax.ShapeDtypeStruct((r, cdim), F32)] * 4 + [jax.ShapeDtypeStruct(e.shape, e.dtype) for e in extra],
        compiler_params=_cparams(dimension_semantics=("arbitrary",)),
    )(w, g, m, v, *extra)


def _adamw_ada(w, m, v, cact_t, dcols):
    r, cdim = w.shape
    tm = 256

    def body(w_ref, m_ref, v_ref, ct_ref, dc_ref, g_ref, d_ref, m2_ref, v2_ref):
        g = jnp.dot(ct_ref[...].astype(BF16), dc_ref[...].astype(BF16), preferred_element_type=F32)
        g_ref[...] = g
        d_ref[...], m2_ref[...], v2_ref[...] = _adamw_math(w_ref[...], g, m_ref[...], v_ref[...])

    blk = pl.BlockSpec((tm, cdim), lambda i: (i, 0))
    return pl.pallas_call(
        body,
        name="adamw_w_ada",
        grid=(r // tm,),
        in_specs=[blk] * 3 + [pl.BlockSpec((tm, N_DEV), lambda i: (i, 0)), pl.BlockSpec((N_DEV, cdim), lambda i: (0, 0))],
        out_specs=[blk] * 4,
        out_shape=[jax.ShapeDtypeStruct((r, cdim), F32)] * 4,
        compiler_params=_cparams(dimension_semantics=("arbitrary",)),
    )(w, m, v, cact_t, dcols)


def _adamw_small(ws, ms, vs, ssum, rows):
    n = len(ws)

    def body(*refs):
        w_r, m_r, v_r = refs[0:n], refs[n:2 * n], refs[2 * n:3 * n]
        ss_ref, rows_ref = refs[3 * n], refs[3 * n + 1]
        g_r, d_r, m2_r, v2_r = (refs[3 * n + 2 + k * n:3 * n + 2 + (k + 1) * n] for k in range(4))
        loss_ref = refs[7 * n + 2]
        j = 2 * lax.axis_index("x") + lax.axis_index("y")
        rsum = rows_ref[0]
        for d in range(1, N_DEV):
            rsum = rsum + rows_ref[d]
        taps = []
        for t in range(CONV_TAPS):
            row = ss_ref[t // 2:t // 2 + 1, :]
            c0 = CONV_W * (t % 2)
            pick = row[:, c0:c0 + 128]
            for k in range(1, N_CHIPS):
                pick = jnp.where(j == k, row[:, c0 + 128 * k:c0 + 128 * (k + 1)], pick)
            taps.append(pick)
        grads = [jnp.concatenate([rsum[2:3], rsum[3:4], rsum[0:1]], axis=1), rsum[4:5],
                 ss_ref[17:18, 512:512 + HEAD_DIM], ss_ref[17:18, 640:640 + HEAD_DIM], ss_ref[17:18, 768:776],
                 None, ss_ref[16:17, 0:CONV_W], ss_ref[16:17, CONV_W:2 * CONV_W], ss_ref[17:18, 0:CONV_W]]
        for i in range(n):
            if grads[i] is None:
                for t in range(CONV_TAPS):
                    g_r[i][t:t + 1, :] = taps[t]
                g = g_r[i][...]
            else:
                g = grads[i]
                g_r[i][...] = g
            d_r[i][...], m2_r[i][...], v2_r[i][...] = _adamw_math(w_r[i][...], g, m_r[i][...], v_r[i][...])
        loss_ref[...] = (0.5 / D_MODEL) * jnp.sum(ss_ref[18:19, :], axis=1, keepdims=True)

    vm = pl.BlockSpec(memory_space=pltpu.VMEM)
    shapes = [jax.ShapeDtypeStruct(w.shape, F32) for w in ws]
    out = pl.pallas_call(
        body,
        name="adamw_small",
        in_specs=[vm] * (3 * n + 2),
        out_specs=[vm] * (4 * n + 1),
        out_shape=shapes * 4 + [jax.ShapeDtypeStruct((1, 1), F32)],
        compiler_params=_cparams(),
    )(*ws, *ms, *vs, ssum, rows)
    return out[0:n], out[n:2 * n], out[2 * n:3 * n], out[3 * n:4 * n], out[4 * n]


def _rope_tables(t):
    inv = ROPE_THETA ** (-jnp.arange(0, HEAD_DIM, 2, dtype=F32) / HEAD_DIM)
    ang = jnp.arange(t, dtype=F32)[:, None] * inv[None, :]
    cos, sin = jnp.cos(ang), jnp.sin(ang)
    return jnp.tile(cos, (1, 4)), jnp.tile(jnp.concatenate([-sin, sin], axis=1), (1, 2))


def kernel(x, c, w_ada, b_ada, norm_w, w_in, q_norm_w, k_norm_w, sinks, conv_w, conv_b, ln_w, ln_b, w_out, loss_target, m_w_ada, m_b_ada, m_norm_w, m_w_in, m_q_norm_w, m_k_norm_w, m_sinks, m_conv_w, m_conv_b, m_ln_w, m_ln_b, m_w_out, v_w_ada, v_b_ada, v_norm_w, v_w_in, v_q_norm_w, v_k_norm_w, v_sinks, v_conv_w, v_conv_b, v_ln_w, v_ln_b, v_w_out):
    xi, yi = lax.axis_index("x"), lax.axis_index("y")
    j = 2 * xi + yi
    x2, tgt = x[0], loss_target[0]
    t = x2.shape[0]

    wt_s, mt_s, vt_s = w_in[0].T, m_w_in[0].T, v_w_in[0].T
    cw_pad = jnp.pad(conv_w[0], ((0, 1), (0, 0)))

    q_raw, kv_raw, ga, ua, ug, gb, h, w_full, call, ada4 = _in_proj_gather(
        x2, wt_s.reshape(2, IN_HALF, D_MODEL), c, w_ada[0], b_ada, norm_w)
    ada = ada4.reshape(1, 3 * D_MODEL)
    s1, gate = 1.0 + ada[:, D_MODEL:2 * D_MODEL], ada[:, 2 * D_MODEL:]

    cos_f, sin_s = _rope_tables(t)
    qw2, kw2 = jnp.tile(q_norm_w, (1, 2)), jnp.tile(k_norm_w, (1, 2))

    o, mix_a, wo4, cw4 = _attn_fwd(q_raw, kv_raw, ga, qw2, kw2, sinks, cos_f, sin_s,
                                   w_out[0].reshape(2, OUT_HALF, D_MODEL), cw_pad)
    w_out_full = wo4.reshape(D_MODEL, D_MODEL)
    cw_full = jnp.concatenate([cw4[i] for i in range(N_CHIPS)], axis=1)
    cz, mix_b = _conv_fwd(ua, ug, gb, cw_full, conv_b, ln_w, ln_b)
    dout, dmix_a, dmix_b, gwo_bf, red_o = _out_proj(mix_a, mix_b, x2, tgt, gate, w_out_full)

    dq, dkv, dga, sm_a, gwo = _attn_bwd(q_raw, kv_raw, ga, o, dmix_a, qw2, kw2, sinks, cos_f, sin_s,
                                        gwo_bf.reshape(N_CHIPS, 2, OUT_HALF, D_MODEL))
    dua, dug, dgb, dcw, dvec = _conv_bwd(ua, ug, gb, cz, dmix_b, cw_full, ln_w, ln_b)
    dparts = (dq, dkv, dga, dua, dug, dgb)

    grad_x, gw, ssum, rows = _in_proj_bwd(dparts, h, x2, dout, s1, norm_w, w_full, dcw, dvec, sm_a, red_o)

    gt_w_in = gw.reshape(2 * IN_HALF, D_MODEL)
    g_w_out = gwo.reshape(D_MODEL // N_CHIPS, D_MODEL)
    d_ada_all = jnp.concatenate([rows[:, 2], rows[:, 3], rows[:, 0]], axis=1)
    dcols = lax.dynamic_slice(d_ada_all, (0, ADA_SHARD * j), (N_DEV, ADA_SHARD))
    cact_t = jax.nn.silu(call.reshape(N_DEV, D_MODEL)).T

    g_w_ada, d_w_ada, nm_w_ada, nv_w_ada = _adamw_ada(w_ada[0], m_w_ada[0], v_w_ada[0], cact_t, dcols)
    gt_w_in, dt_w_in, nmt_w_in, nvt_w_in, grad_x = _adamw("adamw_w_in", wt_s, gt_w_in, mt_s, vt_s, 176, through=grad_x)
    g_w_in, d_w_in, nm_w_in, nv_w_in = gt_w_in.T, dt_w_in.T, nmt_w_in.T, nvt_w_in.T
    g_w_out, d_w_out, nm_w_out, nv_w_out = _adamw("adamw_w_out", w_out[0], g_w_out, m_w_out[0], v_w_out[0], 128)
    ws = [b_ada, norm_w, q_norm_w, k_norm_w, sinks, conv_w[0], conv_b, ln_w, ln_b]
    ms = [m_b_ada, m_norm_w, m_q_norm_w, m_k_norm_w, m_sinks, m_conv_w[0], m_conv_b, m_ln_w, m_ln_b]
    vs = [v_b_ada, v_norm_w, v_q_norm_w, v_k_norm_w, v_sinks, v_conv_w[0], v_conv_b, v_ln_w, v_ln_b]
    gs, ds, nms, nvs, loss11 = _adamw_small(ws, ms, vs, ssum, rows)
    loss = loss11[0, 0]

    def order(ada_v, in_v, out_v, sm):
        b, nw_, qw_, kw_, sk_, cw_, cb_, lw_, lb_ = sm
        return [ada_v[None], b, nw_, in_v[None], qw_, kw_, sk_, cw_[None], cb_, lw_, lb_, out_v[None]]

    grads = order(g_w_ada, g_w_in, g_w_out, gs)
    deltas = order(d_w_ada, d_w_in, d_w_out, ds)
    new_m = order(nm_w_ada, nm_w_in, nm_w_out, nms)
    new_v = order(nv_w_ada, nv_w_in, nv_w_out, nvs)
    return (loss, grad_x[None], *grads, *deltas, *new_m, *new_v)
```

```python
import functools

import jax
import jax.numpy as jnp
from jax import lax
from jax.experimental import pallas as pl
from jax.experimental.pallas import tpu as pltpu

F32 = jnp.float32
BF16 = jnp.bfloat16

D_MODEL = 1024
ATTN_W = 512
KV_W = 128
CONV_W = 512
IN_W = 2816
HEAD_DIM = 64
CONV_TAPS = 31
QBLK = 128
EPS = 1e-6
ROPE_THETA = 10000.0

ADAM_LR = 0.001
ADAM_B1 = 0.9
ADAM_B2 = 0.999
ADAM_EPS = 1e-08
ADAM_WD = 0.01
ADAM_STEP = 10

N_CHIPS = 4
N_DEV = 8
IN_HALF = IN_W // N_CHIPS // 2
OUT_HALF = D_MODEL // N_CHIPS // 2
ADA_SHARD = 3 * D_MODEL // N_CHIPS

VMEM_LIMIT = 56 * 1024 * 1024
CONV_PAD = 32


def _cparams(**kw):
    return pltpu.CompilerParams(vmem_limit_bytes=VMEM_LIMIT, **kw)


def _sigmoid(v):
    return 1.0 / (1.0 + jnp.exp(-v))


def _silu(v):
    return v * _sigmoid(v)


def _dsilu(v):
    s = _sigmoid(v)
    return s * (1.0 + v * (1.0 - s))


def _lane(shape):
    return lax.broadcasted_iota(jnp.int32, shape, len(shape) - 1)


PUT_ROWS = 512


def _fetch(hbm_refs, vmem_refs, sem):
    cps = [pltpu.make_async_copy(h, v, sem.at[i]) for i, (h, v) in enumerate(zip(hbm_refs, vmem_refs))]
    for cp in cps:
        cp.start()
    return cps


def _put(vmem_ref, hbm_ref, sem, m):
    r = pl.ds(pl.multiple_of(m * PUT_ROWS, PUT_ROWS), PUT_ROWS)
    return pltpu.make_async_copy(vmem_ref.at[r], hbm_ref.at[r], sem.at[m])


def _put_all(pairs, sems, m):
    for (v, h), sem in zip(pairs, sems):
        _put(v, h, sem, m).start()


def _put_wait(pairs, sems, n):
    for (v, h), sem in zip(pairs, sems):
        for m in range(n):
            _put(v, h, sem, m).wait()


def _head_mean(s, left):
    sl = jnp.sum(jnp.where(left, s, 0.0), axis=-1, keepdims=True)
    sr = jnp.sum(jnp.where(left, 0.0, s), axis=-1, keepdims=True)
    return jnp.where(left, sl, sr) * (1.0 / HEAD_DIM)


def _rot(v, first):
    return jnp.where(first, pltpu.roll(v, 96, 1), pltpu.roll(v, 32, 1))


def _norm_rope(v, w, cos, sin_s, left, first):
    r = lax.rsqrt(_head_mean(v * v, left) + EPS)
    xh = v * r
    n = xh * w
    return n * cos + _rot(n, first) * sin_s, xh, r


def _norm_rope_bwd(d, xh, r, w, cos, sin_s, left, first):
    dn = d * cos - _rot(d, first) * sin_s
    dw = jnp.sum(dn * xh, axis=0, keepdims=True)
    dxh = dn * w
    return r * (dxh - xh * _head_mean(dxh * xh, left)), dw


def _dup_heads(v, left):
    sw = pltpu.roll(v, 64, 1)
    return jnp.where(left, v, sw), jnp.where(left, sw, v)


def _prep_kv(kv_ref, kw_ref, cos_ref, sin_ref, ka_ref, va_ref, t):
    ch = 256
    for g in range(2):
        ka_ref[g, 0:QBLK, :] = jnp.zeros((QBLK, 128), BF16)
        va_ref[g, 0:QBLK, :] = jnp.zeros((QBLK, 128), BF16)

    def chunk(i, carry):
        r0 = pl.multiple_of(i * ch, ch)
        left = _lane((ch, 128)) < 64
        first = (_lane((ch, 128)) % 64) < 32
        k = kv_ref[pl.ds(r0, ch), 0:128]
        v = kv_ref[pl.ds(r0, ch), 128:256]
        kr, _, _ = _norm_rope(k, kw_ref[...], cos_ref[pl.ds(r0, ch), :], sin_ref[pl.ds(r0, ch), :], left, first)
        k0, k1 = _dup_heads(kr, left)
        v0, v1 = _dup_heads(v, left)
        ka_ref[0, pl.ds(QBLK + r0, ch), :] = k0.astype(BF16)
        ka_ref[1, pl.ds(QBLK + r0, ch), :] = k1.astype(BF16)
        va_ref[0, pl.ds(QBLK + r0, ch), :] = v0.astype(BF16)
        va_ref[1, pl.ds(QBLK + r0, ch), :] = v1.astype(BF16)
        return carry

    lax.fori_loop(0, t // ch, chunk, 0)


def _band_mask(n):
    qi = lax.broadcasted_iota(jnp.int32, (2 * QBLK, 2 * QBLK), 0) % QBLK
    kj = lax.broadcasted_iota(jnp.int32, (2 * QBLK, 2 * QBLK), 1)
    local = (kj > qi) & (kj <= qi + QBLK)
    return local & ((n > 0) | (kj >= QBLK))


def _softmax_pair(s, mask, sink0, sink1):
    row = lax.broadcasted_iota(jnp.int32, (2 * QBLK, 1), 0)
    sink = jnp.where(row < QBLK, sink0, sink1)
    s = jnp.where(mask, s, -jnp.inf)
    m = jnp.maximum(jnp.max(s, axis=-1, keepdims=True), sink)
    e = jnp.exp(s - m)
    es = jnp.exp(sink - m)
    inv = 1.0 / (jnp.sum(e, axis=-1, keepdims=True) + es)
    return e * inv, es * inv


def _stack_heads(v, left):
    return jnp.concatenate([jnp.where(left, v, 0.0), jnp.where(left, 0.0, v)], axis=0)


def _attn_fwd(q_raw, kv_raw, ga, qw2, kw2, sinks, cos_f, sin_s, wo, cw):
    t = q_raw.shape[0]
    nblk = t // QBLK
    per_put = PUT_ROWS // QBLK

    def body(q_hbm, kv_ref, ga_hbm, qw_ref, kw_ref, sk_ref, cos_hbm, sin_hbm, wo_ref, cw_ref,
             o_hbm, mix_hbm, wo4_ref, cw4_ref, ka_ref, va_ref, q_ref, ga_ref, o_ref, mix_ref, cos_ref, sin_ref,
             isem, osem0, osem1, ssem, rsem):
        loads = _fetch((cos_hbm, sin_hbm, q_hbm, ga_hbm), (cos_ref, sin_ref, q_ref, ga_ref), isem)
        outs, osems = ((o_ref, o_hbm), (mix_ref, mix_hbm)), (osem0, osem1)
        x, y, c, chips = _place()
        j = 2 * x + y
        sib = (x, y, 1 - c)
        idx = [2 * cx + cy for cx, cy in chips]
        rc = functools.partial(_remote, ssem, rsem)
        wo4_ref[j] = wo_ref[...].astype(BF16)
        cw4_ref[j] = cw_ref[...]
        sends = []
        for k, chip in enumerate(chips):
            sends.append(rc(k, wo4_ref.at[j, c], wo4_ref.at[j, c], (*chip, c)))
            sends.append(rc(6 + k, cw4_ref.at[j], cw4_ref.at[j], (*chip, c)))
        for cp in sends:
            cp.start()

        loads[0].wait()
        loads[1].wait()
        _prep_kv(kv_ref, kw_ref, cos_ref, sin_ref, ka_ref, va_ref, t)
        loads[2].wait()
        loads[3].wait()

        def blk(n, carry):
            r0 = pl.multiple_of(n * QBLK, QBLK)
            left = _lane((QBLK, 128)) < 64
            first = (_lane((QBLK, 128)) % 64) < 32
            cos = cos_ref[pl.ds(r0, QBLK), :]
            sin = sin_ref[pl.ds(r0, QBLK), :]
            mask = _band_mask(n)
            scores = []
            for p in range(4):
                lanes = slice(p * 128, (p + 1) * 128)
                qr, _, _ = _norm_rope(q_ref[pl.ds(r0, QBLK), lanes], qw_ref[...], cos, sin, left, first)
                q2 = _stack_heads(qr * 0.125, left).astype(BF16)
                scores.append(lax.dot_general(q2, ka_ref[p // 2, pl.ds(r0, 2 * QBLK), :], (((1,), (1,)), ((), ())),
                                              preferred_element_type=F32))
            probs = [_softmax_pair(scores[p], mask, sk_ref[0, 2 * p], sk_ref[0, 2 * p + 1])[0].astype(BF16)
                     for p in range(4)]
            for p in range(4):
                lanes = slice(p * 128, (p + 1) * 128)
                o2 = jnp.dot(probs[p], va_ref[p // 2, pl.ds(r0, 2 * QBLK), :], preferred_element_type=F32)
                o = jnp.where(left, o2[0:QBLK], o2[QBLK:2 * QBLK])
                o_ref[pl.ds(r0, QBLK), lanes] = o.astype(BF16)
                mix_ref[pl.ds(r0, QBLK), lanes] = (o * _silu(ga_ref[pl.ds(r0, QBLK), lanes])).astype(BF16)

            @pl.when(n % per_put == per_put - 1)
            def _():
                _put_all(outs, osems, n // per_put)

            return carry

        lax.fori_loop(0, nblk, blk, 0)
        _put_wait(outs, osems, t // PUT_ROWS)

        passed = []
        for k, chip in enumerate(chips):
            jk = idx[k]
            rc(k, wo4_ref.at[jk, c], wo4_ref.at[jk, c], sib).wait_recv()
            passed.append(rc(3 + k, wo4_ref.at[jk, c], wo4_ref.at[jk, c], sib))
            passed[-1].start()
        for k, chip in enumerate(chips):
            jk = idx[k]
            rc(3 + k, wo4_ref.at[jk, 1 - c], wo4_ref.at[jk, 1 - c], sib).wait_recv()
            rc(6 + k, cw4_ref.at[jk], cw4_ref.at[jk], sib).wait_recv()
        for cp in sends + passed:
            cp.wait_send()

    vm = pl.BlockSpec(memory_space=pltpu.VMEM)
    hbm = pl.BlockSpec(memory_space=pl.ANY)
    n_sem = 9
    return pl.pallas_call(
        body,
        name="attn_fwd",
        in_specs=[hbm, vm, hbm, vm, vm, pl.BlockSpec(memory_space=pltpu.SMEM), hbm, hbm, vm, vm],
        out_specs=[hbm, hbm, vm, vm],
        out_shape=[jax.ShapeDtypeStruct((t, ATTN_W), BF16), jax.ShapeDtypeStruct((t, ATTN_W), BF16),
                   jax.ShapeDtypeStruct((N_CHIPS, 2, OUT_HALF, D_MODEL), BF16),
                   jax.ShapeDtypeStruct((N_CHIPS, 32, 128), F32)],
        scratch_shapes=[pltpu.VMEM((2, t + QBLK, 128), BF16), pltpu.VMEM((2, t + QBLK, 128), BF16),
                        pltpu.VMEM((t, ATTN_W), F32), pltpu.VMEM((t, ATTN_W), F32),
                        pltpu.VMEM((t, ATTN_W), BF16), pltpu.VMEM((t, ATTN_W), BF16),
                        pltpu.VMEM((t, 128), F32), pltpu.VMEM((t, 128), F32),
                        pltpu.SemaphoreType.DMA((4,)), pltpu.SemaphoreType.DMA((t // PUT_ROWS,)),
                        pltpu.SemaphoreType.DMA((t // PUT_ROWS,)),
                        pltpu.SemaphoreType.DMA((n_sem,)), pltpu.SemaphoreType.DMA((n_sem,))],
        compiler_params=_cparams(),
    )(q_raw, kv_raw, ga, qw2, kw2, sinks, cos_f, sin_s, wo, cw)


def _attn_bwd(q_raw, kv_raw, ga, o, dmix, qw2, kw2, sinks, cos_f, sin_s, go):
    t = q_raw.shape[0]
    nblk = t // QBLK
    per_put = PUT_ROWS // QBLK

    def body(q_hbm, kv_ref, ga_hbm, o_hbm, dm_hbm, qw_ref, kw_ref, sk_ref, cos_hbm, sin_hbm, go_ref,
             dq_hbm, dkv_ref, dga_hbm, sm_ref, gwo_ref, ka_ref, va_ref, dka_ref, dva_ref,
             sibo_ref, outo_ref, ino_ref, q_ref, ga_ref, o_ref, dm_ref, dq_ref, dga_ref, cos_ref, sin_ref,
             isem, osem0, osem1, ssem, rsem):
        loads = _fetch((cos_hbm, sin_hbm, q_hbm, ga_hbm, o_hbm, dm_hbm), (cos_ref, sin_ref, q_ref, ga_ref, o_ref, dm_ref), isem)
        outs, osems = ((dq_ref, dq_hbm), (dga_ref, dga_hbm)), (osem0, osem1)
        x, y, c, chips = _place()
        sib = (x, y, 1 - c)
        rc = functools.partial(_remote, ssem, rsem)
        theirs, mine = go_ref.at[:, 1 - c], go_ref.at[:, c]
        sends = [_rs_to_sibling(rc, 0, theirs, sibo_ref, sib)]
        loads[0].wait()
        loads[1].wait()
        _prep_kv(kv_ref, kw_ref, cos_ref, sin_ref, ka_ref, va_ref, t)
        dka_ref[...] = jnp.zeros_like(dka_ref)
        dva_ref[...] = jnp.zeros_like(dva_ref)
        sends += _rs_trade(rc, 0, theirs, mine, sibo_ref, outo_ref, ino_ref, OUT_HALF, c, sib, chips)
        for cp in loads[2:]:
            cp.wait()

        def blk(n, carry):
            dqw, dsk = carry
            r0 = pl.multiple_of(n * QBLK, QBLK)
            left = _lane((QBLK, 128)) < 64
            first = (_lane((QBLK, 128)) % 64) < 32
            cos = cos_ref[pl.ds(r0, QBLK), :]
            sin = sin_ref[pl.ds(r0, QBLK), :]
            mask = _band_mask(n)
            row = lax.broadcasted_iota(jnp.int32, (2 * QBLK, 1), 0)
            rows = pl.ds(r0, QBLK)
            win = pl.ds(r0, 2 * QBLK)
            lane_of = [slice(p * 128, (p + 1) * 128) for p in range(4)]
            for grp in ((0, 1), (2, 3)):
                qn = {p: _norm_rope(q_ref[rows, lane_of[p]], qw_ref[...], cos, sin, left, first) for p in grp}
                q2 = {p: _stack_heads(qn[p][0] * 0.125, left).astype(BF16) for p in grp}
                sc = {p: lax.dot_general(q2[p], ka_ref[p // 2, win, :], (((1,), (1,)), ((), ())),
                                         preferred_element_type=F32) for p in grp}
                do2 = {}
                for p in grp:
                    gav = ga_ref[rows, lane_of[p]]
                    dmv = dm_ref[rows, lane_of[p]].astype(F32)
                    dga_ref[rows, lane_of[p]] = (dmv * o_ref[rows, lane_of[p]].astype(F32) * _dsilu(gav)).astype(BF16)
                    do2[p] = _stack_heads(dmv * _silu(gav), left).astype(BF16)
                dpm = {p: lax.dot_general(do2[p], va_ref[p // 2, win, :], (((1,), (1,)), ((), ())),
                                          preferred_element_type=F32) for p in grp}
                sm = {p: _softmax_pair(sc[p], mask, sk_ref[0, 2 * p], sk_ref[0, 2 * p + 1]) for p in grp}
                dsl = {}
                for p in grp:
                    pm, ps = sm[p]
                    delta = jnp.sum(pm * dpm[p], axis=-1, keepdims=True)
                    dsl[p] = (pm * (dpm[p] - delta)).astype(BF16)
                    pd = ps * delta
                    d0 = jnp.sum(jnp.where(row < QBLK, pd, 0.0), axis=0, keepdims=True)
                    d1 = jnp.sum(jnp.where(row < QBLK, 0.0, pd), axis=0, keepdims=True)
                    l8 = _lane((1, 128))
                    dsk = dsk - jnp.where(l8 == 2 * p, d0, 0.0) - jnp.where(l8 == 2 * p + 1, d1, 0.0)
                for p in grp:
                    g = p // 2
                    dva_ref[g, win, :] += lax.dot_general(sm[p][0].astype(BF16), do2[p], (((0,), (0,)), ((), ())),
                                                          preferred_element_type=F32)
                    dka_ref[g, win, :] += lax.dot_general(dsl[p], q2[p], (((0,), (0,)), ((), ())),
                                                          preferred_element_type=F32)
                for p in grp:
                    dq2 = jnp.dot(dsl[p], ka_ref[p // 2, win, :], preferred_element_type=F32)
                    dqr = jnp.where(left, dq2[0:QBLK], dq2[QBLK:2 * QBLK]) * 0.125
                    dq, dw = _norm_rope_bwd(dqr, qn[p][1], qn[p][2], qw_ref[...], cos, sin, left, first)
                    dq_ref[rows, lane_of[p]] = dq.astype(BF16)
                    dqw = dqw + dw

            @pl.when(n % per_put == per_put - 1)
            def _():
                _put_all(outs, osems, n // per_put)

            return dqw, dsk

        zero = jnp.zeros((1, 128), F32)
        dqw, dsk = lax.fori_loop(0, nblk, blk, (zero, zero))

        ch = 256

        def chunk(i, dkw):
            r0 = pl.multiple_of(i * ch, ch)
            left = _lane((ch, 128)) < 64
            first = (_lane((ch, 128)) % 64) < 32
            rows = pl.ds(r0, ch)
            prow = pl.ds(QBLK + r0, ch)

            def fold(ref):
                a0 = ref[0, prow, :]
                a1 = ref[1, prow, :]
                return jnp.where(left, a0 + pltpu.roll(a0, 64, 1), a1 + pltpu.roll(a1, 64, 1))

            cos = cos_ref[rows, :]
            sin = sin_ref[rows, :]
            _, xh, r = _norm_rope(kv_ref[rows, 0:128], kw_ref[...], cos, sin, left, first)
            dk, dw = _norm_rope_bwd(fold(dka_ref), xh, r, kw_ref[...], cos, sin, left, first)
            dkv_ref[rows, 0:128] = dk.astype(BF16)
            dkv_ref[rows, 128:256] = fold(dva_ref).astype(BF16)
            return dkw + dw

        dkw = lax.fori_loop(0, t // ch, chunk, zero)
        sm_ref[...] = jnp.zeros((8, 128), F32)
        sm_ref[0:1, :] = dqw + pltpu.roll(dqw, 64, 1)
        sm_ref[1:2, :] = dkw + pltpu.roll(dkw, 64, 1)
        sm_ref[2:3, :] = dsk

        j = 2 * x + y
        sends.append(_rs_total(rc, 0, mine, sibo_ref, outo_ref, ino_ref, gwo_ref, OUT_HALF, j, c, sib))
        _rs_done(rc, 0, gwo_ref, c, sib)
        for cp in sends:
            cp.wait_send()
        _put_wait(outs, osems, t // PUT_ROWS)

    vm = pl.BlockSpec(memory_space=pltpu.VMEM)
    hbm = pl.BlockSpec(memory_space=pl.ANY)
    return pl.pallas_call(
        body,
        name="attn_bwd",
        in_specs=[hbm, vm, hbm, hbm, hbm, vm, vm, pl.BlockSpec(memory_space=pltpu.SMEM), hbm, hbm, vm],
        out_specs=[hbm, vm, hbm, vm, vm],
        out_shape=[jax.ShapeDtypeStruct((t, ATTN_W), BF16), jax.ShapeDtypeStruct((t, 2 * KV_W), BF16),
                   jax.ShapeDtypeStruct((t, ATTN_W), BF16), jax.ShapeDtypeStruct((8, 128), F32),
                   jax.ShapeDtypeStruct((2, OUT_HALF, D_MODEL), F32)],
        scratch_shapes=[pltpu.VMEM((2, t + QBLK, 128), BF16), pltpu.VMEM((2, t + QBLK, 128), BF16),
                        pltpu.VMEM((2, t + QBLK, 128), F32), pltpu.VMEM((2, t + QBLK, 128), F32)]
        + _rs_scratch(OUT_HALF)
        + [pltpu.VMEM((t, ATTN_W), F32), pltpu.VMEM((t, ATTN_W), F32), pltpu.VMEM((t, ATTN_W), BF16),
           pltpu.VMEM((t, ATTN_W), BF16), pltpu.VMEM((t, ATTN_W), BF16), pltpu.VMEM((t, ATTN_W), BF16),
           pltpu.VMEM((t, 128), F32), pltpu.VMEM((t, 128), F32),
           pltpu.SemaphoreType.DMA((6,)), pltpu.SemaphoreType.DMA((t // PUT_ROWS,)), pltpu.SemaphoreType.DMA((t // PUT_ROWS,)),
           pltpu.SemaphoreType.DMA((RS_SEMS,)), pltpu.SemaphoreType.DMA((RS_SEMS,))],
        compiler_params=_cparams(),
    )(q_raw, kv_raw, ga, o, dmix, qw2, kw2, sinks, cos_f, sin_s, go)


CONV_CH = 256
CONV_SUB = 128
CONV_ACCS = 1


def _shifted_windows(src_ref, r0, sh_ref):
    rows = CONV_CH + CONV_PAD
    win = src_ref[pl.ds(r0, rows), :]
    for b in range(8):
        sh = win if b == 0 else pltpu.roll(win, rows - b, 0)
        for c in range(CONV_W // 128):
            sh_ref[b, c] = sh[:, c * 128:(c + 1) * 128]


def _conv_fwd(ua, ug, gb, cw, cb, lw, lb):
    t = ua.shape[0]

    def body(ua_hbm, ug_hbm, gb_hbm, cw_ref, cb_ref, lw_ref, lb_ref, cz_hbm, mix_hbm, zp_ref, sh_ref,
             ua_ref, ug_ref, gb_ref, cz_ref, mix_ref, isem, osem0, osem1):
        loads = _fetch((ua_hbm, ug_hbm, gb_hbm), (ua_ref, ug_ref, gb_ref), isem)
        outs, osems = ((cz_ref, cz_hbm), (mix_ref, mix_hbm)), (osem0, osem1)
        per_put = PUT_ROWS // CONV_CH
        zp_ref[0:CONV_PAD, :] = jnp.zeros((CONV_PAD, CONV_W), F32)
        loads[0].wait()
        loads[1].wait()

        def glu(i, carry):
            r0 = pl.multiple_of(i * CONV_CH, CONV_CH)
            rows = pl.ds(r0, CONV_CH)
            zp_ref[pl.ds(CONV_PAD + r0, CONV_CH), :] = ua_ref[rows, :] * _sigmoid(ug_ref[rows, :])
            return carry

        lax.fori_loop(0, t // CONV_CH, glu, 0)
        loads[2].wait()

        def chunk(i, carry):
            r0 = pl.multiple_of(i * CONV_CH, CONV_CH)
            _shifted_windows(zp_ref, r0, sh_ref)
            for c in range(CONV_W // 128):
                lanes = slice(c * 128, (c + 1) * 128)

                def sub(k, carry2):
                    b0 = pl.multiple_of(k * CONV_SUB, CONV_SUB)
                    acc = [jnp.broadcast_to(cb_ref[0:1, lanes], (CONV_SUB, 128))] + [None] * (CONV_ACCS - 1)
                    for j in range(CONV_TAPS):
                        off = j + CONV_PAD - (CONV_TAPS - 1)
                        term = sh_ref[off % 8, c, pl.ds(b0 + 8 * (off // 8), CONV_SUB), :] * cw_ref[j:j + 1, lanes]
                        acc[j % CONV_ACCS] = term if acc[j % CONV_ACCS] is None else acc[j % CONV_ACCS] + term
                    cz_ref[pl.ds(r0 + b0, CONV_SUB), lanes] = functools.reduce(lambda a, b: a + b, acc)
                    return carry2

                lax.fori_loop(0, CONV_CH // CONV_SUB, sub, 0)
            rows = pl.ds(r0, CONV_CH)
            cz = cz_ref[rows, :]
            mu = jnp.mean(cz, axis=-1, keepdims=True)
            xc = cz - mu
            rs = lax.rsqrt(jnp.mean(xc * xc, axis=-1, keepdims=True) + EPS)
            ln = xc * rs * lw_ref[...] + lb_ref[...]
            mix_ref[rows, :] = (_silu(ln) * _silu(gb_ref[rows, :])).astype(BF16)

            @pl.when(i % per_put == per_put - 1)
            def _():
                _put_all(outs, osems, i // per_put)

            return carry

        lax.fori_loop(0, t // CONV_CH, chunk, 0)
        _put_wait(outs, osems, t // PUT_ROWS)

    vm = pl.BlockSpec(memory_space=pltpu.VMEM)
    hbm = pl.BlockSpec(memory_space=pl.ANY)
    nput = t // PUT_ROWS
    return pl.pallas_call(
        body,
        name="conv_fwd",
        in_specs=[hbm] * 3 + [vm] * 4,
        out_specs=[hbm, hbm],
        out_shape=[jax.ShapeDtypeStruct((t, CONV_W), F32), jax.ShapeDtypeStruct((t, CONV_W), BF16)],
        scratch_shapes=[pltpu.VMEM((t + CONV_PAD, CONV_W), F32),
                        pltpu.VMEM((8, CONV_W // 128, CONV_CH + CONV_PAD, 128), F32),
                        pltpu.VMEM((t, CONV_W), F32), pltpu.VMEM((t, CONV_W), F32), pltpu.VMEM((t, CONV_W), F32),
                        pltpu.VMEM((t, CONV_W), F32), pltpu.VMEM((t, CONV_W), BF16),
                        pltpu.SemaphoreType.DMA((3,)), pltpu.SemaphoreType.DMA((nput,)), pltpu.SemaphoreType.DMA((nput,))],
        compiler_params=_cparams(),
    )(ua, ug, gb, cw, cb, lw, lb)


def _conv_bwd(ua, ug, gb, cz, dmix, cw, lw, lb):
    t = ua.shape[0]

    def body(ua_hbm, ug_hbm, gb_hbm, cz_hbm, dm_hbm, cw_ref, lw_ref, lb_ref,
             dua_hbm, dug_hbm, dgb_hbm, dcw_ref, dvec_ref, zp_ref, dp_ref, sh_ref, wacc_ref,
             ua_ref, ug_ref, gb_ref, cz_ref, dm_ref, dua_ref, dug_ref, dgb_ref, isem, osem0, osem1, osem2):
        loads = _fetch((ua_hbm, ug_hbm, gb_hbm, cz_hbm, dm_hbm), (ua_ref, ug_ref, gb_ref, cz_ref, dm_ref), isem)
        per_put = PUT_ROWS // CONV_CH
        zp_ref[0:CONV_PAD, :] = jnp.zeros((CONV_PAD, CONV_W), F32)
        dp_ref[t:t + CONV_PAD, :] = jnp.zeros((CONV_PAD, CONV_W), F32)
        wacc_ref[...] = jnp.zeros_like(wacc_ref)
        for cp in loads:
            cp.wait()

        def pointwise(i, carry):
            dcb, dlw, dlb = carry
            r0 = pl.multiple_of(i * CONV_CH, CONV_CH)
            rows = pl.ds(r0, CONV_CH)
            zp_ref[pl.ds(CONV_PAD + r0, CONV_CH), :] = ua_ref[rows, :] * _sigmoid(ug_ref[rows, :])
            cz = cz_ref[rows, :]
            mu = jnp.mean(cz, axis=-1, keepdims=True)
            xc = cz - mu
            rs = lax.rsqrt(jnp.mean(xc * xc, axis=-1, keepdims=True) + EPS)
            xh = xc * rs
            ln = xh * lw_ref[...] + lb_ref[...]
            gbv = gb_ref[rows, :]
            dy = dm_ref[rows, :].astype(F32)
            dgb_ref[rows, :] = (dy * _silu(ln) * _dsilu(gbv)).astype(BF16)
            dl = dy * _silu(gbv) * _dsilu(ln)
            dxh = dl * lw_ref[...]
            dcz = rs * (dxh - jnp.mean(dxh, axis=-1, keepdims=True)
                        - xh * jnp.mean(dxh * xh, axis=-1, keepdims=True))
            dp_ref[rows, :] = dcz

            @pl.when(i % per_put == per_put - 1)
            def _():
                _put(dgb_ref, dgb_hbm, osem2, i // per_put).start()

            return (dcb + jnp.sum(dcz, axis=0, keepdims=True),
                    dlw + jnp.sum(dl * xh, axis=0, keepdims=True),
                    dlb + jnp.sum(dl, axis=0, keepdims=True))

        zero = jnp.zeros((1, CONV_W), F32)
        dcb, dlw, dlb = lax.fori_loop(0, t // CONV_CH, pointwise, (zero, zero, zero))
        dvec_ref[...] = jnp.zeros((8, CONV_W), F32)
        dvec_ref[0:1, :] = dcb
        dvec_ref[1:2, :] = dlw
        dvec_ref[2:3, :] = dlb

        def chunk(i, carry):
            r0 = pl.multiple_of(i * CONV_CH, CONV_CH)
            _shifted_windows(dp_ref, r0, sh_ref)
            for c in range(CONV_W // 128):
                lanes = slice(c * 128, (c + 1) * 128)

                def sub(k, carry2):
                    b0 = pl.multiple_of(k * CONV_SUB, CONV_SUB)
                    acc = [None] * CONV_ACCS
                    for j in range(CONV_TAPS):
                        off = CONV_TAPS - 1 - j
                        term = sh_ref[off % 8, c, pl.ds(b0 + 8 * (off // 8), CONV_SUB), :] * cw_ref[j:j + 1, lanes]
                        acc[j % CONV_ACCS] = term if acc[j % CONV_ACCS] is None else acc[j % CONV_ACCS] + term
                    acc = functools.reduce(lambda a, b: a + b, acc)
                    rr = pl.ds(r0 + b0, CONV_SUB)
                    sg = _sigmoid(ug_ref[rr, lanes])
                    dua_ref[rr, lanes] = (acc * sg).astype(BF16)
                    dug_ref[rr, lanes] = (acc * ua_ref[rr, lanes] * sg * (1.0 - sg)).astype(BF16)
                    return carry2

                lax.fori_loop(0, CONV_CH // CONV_SUB, sub, 0)
            _shifted_windows(zp_ref, r0, sh_ref)
            for c in range(CONV_W // 128):
                lanes = slice(c * 128, (c + 1) * 128)

                def subw(k, carry2):
                    b0 = pl.multiple_of(k * CONV_SUB, CONV_SUB)
                    dcz = dp_ref[pl.ds(r0 + b0, CONV_SUB), lanes]
                    for j in range(CONV_TAPS):
                        off = j + CONV_PAD - (CONV_TAPS - 1)
                        pr = dcz * sh_ref[off % 8, c, pl.ds(b0 + 8 * (off // 8), CONV_SUB), :]
                        parts = [pr[8 * q:8 * (q + 1)] for q in range(CONV_SUB // 8)]
                        while len(parts) > 1:
                            parts = [a + b for a, b in zip(parts[0::2], parts[1::2])]
                        wacc_ref[8 * j:8 * (j + 1), lanes] += parts[0]
                    return carry2

                lax.fori_loop(0, CONV_CH // CONV_SUB, subw, 0)

            @pl.when(i % per_put == per_put - 1)
            def _():
                _put_all(((dua_ref, dua_hbm), (dug_ref, dug_hbm)), (osem0, osem1), i // per_put)

            return carry

        lax.fori_loop(0, t // CONV_CH, chunk, 0)
        _put_wait(((dua_ref, dua_hbm), (dug_ref, dug_hbm), (dgb_ref, dgb_hbm)), (osem0, osem1, osem2), t // PUT_ROWS)
        dcw_ref[...] = jnp.zeros((16, 2 * CONV_W), F32)
        for j in range(CONV_TAPS):
            dcw_ref[j // 2:j // 2 + 1, CONV_W * (j % 2):CONV_W * (j % 2 + 1)] = jnp.sum(
                wacc_ref[8 * j:8 * (j + 1), :], axis=0, keepdims=True)

    vm = pl.BlockSpec(memory_space=pltpu.VMEM)
    hbm = pl.BlockSpec(memory_space=pl.ANY)
    return pl.pallas_call(
        body,
        name="conv_bwd",
        in_specs=[hbm] * 5 + [vm] * 3,
        out_specs=[hbm] * 3 + [vm] * 2,
        out_shape=[jax.ShapeDtypeStruct((t, CONV_W), BF16)] * 3
        + [jax.ShapeDtypeStruct((16, 2 * CONV_W), F32), jax.ShapeDtypeStruct((8, CONV_W), F32)],
        scratch_shapes=[pltpu.VMEM((t + CONV_PAD, CONV_W), F32), pltpu.VMEM((t + CONV_PAD, CONV_W), F32),
                        pltpu.VMEM((8, CONV_W // 128, CONV_CH + CONV_PAD, 128), F32), pltpu.VMEM((8 * 32, CONV_W), F32)]
        + [pltpu.VMEM((t, CONV_W), F32)] * 4 + [pltpu.VMEM((t, CONV_W), BF16)] * 4
        + [pltpu.SemaphoreType.DMA((5,))] + [pltpu.SemaphoreType.DMA((t // PUT_ROWS,))] * 3,
        compiler_params=_cparams(),
    )(ua, ug, gb, cz, dmix, cw, lw, lb)


def _out_proj(mix_a, mix_b, x, tgt, gate, w_out):
    t = x.shape[0]
    tm = 512
    nstep = t // tm

    def body(ma_ref, mb_ref, x_ref, t_ref, g_ref, w_ref, dout_ref, dma_ref, dmb_ref, gw_ref, red_ref, acc_ref):
        i = pl.program_id(0)

        @pl.when(i == 0)
        def _():
            acc_ref[...] = jnp.zeros_like(acc_ref)
            red_ref[...] = jnp.zeros_like(red_ref)

        mix = jnp.concatenate([ma_ref[...], mb_ref[...]], axis=1)
        y = jnp.dot(mix, w_ref[...], preferred_element_type=F32)
        gate_v = g_ref[...]
        err = x_ref[...] + gate_v * y - t_ref[...]
        dout = err * (1.0 / D_MODEL)
        dout_ref[...] = dout
        red_ref[0:1, :] += jnp.sum(dout * y, axis=0, keepdims=True)
        red_ref[1:2, :] += jnp.sum(err * err, axis=0, keepdims=True)
        dy = (dout * gate_v).astype(BF16)
        dmix = lax.dot_general(dy, w_ref[...], (((1,), (1,)), ((), ())), preferred_element_type=F32)
        dma_ref[...] = dmix[:, 0:512].astype(BF16)
        dmb_ref[...] = dmix[:, 512:1024].astype(BF16)
        acc_ref[...] += lax.dot_general(mix, dy, (((0,), (0,)), ((), ())), preferred_element_type=F32)

        @pl.when(i == nstep - 1)
        def _():
            gw_ref[...] = acc_ref[...].astype(BF16)

    row = lambda w: pl.BlockSpec((tm, w), lambda i: (i, 0))
    const = lambda s: pl.BlockSpec(s, lambda i: (0, 0))
    return pl.pallas_call(
        body,
        name="out_proj",
        grid=(nstep,),
        in_specs=[row(512), row(512), row(D_MODEL), row(D_MODEL), const((1, D_MODEL)),
                  pl.BlockSpec((D_MODEL, D_MODEL), lambda i: (0, 0), pipeline_mode=pl.Buffered(1))],
        out_specs=[row(D_MODEL), row(512), row(512), const((D_MODEL, D_MODEL)), const((8, D_MODEL))],
        out_shape=[jax.ShapeDtypeStruct((t, D_MODEL), F32), jax.ShapeDtypeStruct((t, 512), BF16),
                   jax.ShapeDtypeStruct((t, 512), BF16), jax.ShapeDtypeStruct((D_MODEL, D_MODEL), BF16),
                   jax.ShapeDtypeStruct((8, D_MODEL), F32)],
        scratch_shapes=[pltpu.VMEM((D_MODEL, D_MODEL), F32)],
        compiler_params=_cparams(dimension_semantics=("arbitrary",)),
    )(mix_a, mix_b, x, tgt, gate, w_out)


DPROJ_WIDTHS = (512, 256, 512, 512, 512, 512)
DPROJ_STARTS = (0, 512, 768, 1280, 1792, 2304)
WIN_W = 768
WIN_START = (0, 640, 1408, 2048)
WIN_OFF = (0, 64, 0, 64)
N_GW = N_CHIPS


def _window_pieces(s):
    lo, hi = WIN_START[s], WIN_START[s] + WIN_W
    out = []
    for p, (st, w) in enumerate(zip(DPROJ_STARTS, DPROJ_WIDTHS)):
        a, b = max(lo, st), min(hi, st + w)
        if a < b:
            out.append((p, a - st, b - a, a - lo))
    return out


def _in_proj_bwd(dparts, h, x, dout, s1, nw, wt_full, dcw, dvec, sm_a, row0):
    t = x.shape[0]
    tm = 256
    nstep = N_GW + t // tm
    n_sem = 20
    rows0 = 32
    hs = rows0 // 2
    npart = len(DPROJ_WIDTHS)

    def body(*refs):
        d_hbm, d_ref = refs[:npart], refs[npart:2 * npart]
        (x_ref, dout_ref, s1_ref, nw_ref, h_ref, wt_hbm, dcw_ref, dvec_ref, sma_ref, row0_ref,
         gx_ref, gw_hbm, ssum_ref, rows_ref,
         stg_ref, wt_ref, gt_ref, sib_ref, out_ref, in_ref, res_ref, sall_ref, red_ref, sm0_ref, ssib_ref, schip_ref, sres_ref,
         wsem, lsem, ssem, rsem) = refs[2 * npart:]
        i = pl.program_id(0)
        x_, y_, c, chips = _place()
        j = 2 * x_ + y_
        dev = 2 * j + c
        sib = (x_, y_, 1 - c)
        rc = functools.partial(_remote, ssem, rsem)
        rel_chip = [2 * cx + cy for cx, cy in chips] + [j]
        peers = [(px, py, pc) for px in (x_, 1 - x_) for py in (y_, 1 - y_) for pc in (c, 1 - c)][1:]
        wt_copy = pltpu.make_async_copy(wt_hbm, wt_ref, lsem.at[0])

        def window(case, slot):
            return [pltpu.make_async_copy(d_hbm[p].at[:, pl.ds(c0, w)], stg_ref.at[slot, :, pl.ds(w0, w)], wsem.at[slot, n])
                    for n, (p, c0, w, w0) in enumerate(_window_pieces(case))]

        def to_sibling(k):
            return rc(k, gt_ref.at[k, 1 - c], sib_ref.at[k], sib)

        def to_chip(k):
            return rc(4 + k, out_ref.at[k], in_ref.at[k], (*chips[k], c))

        def trade(k):
            to_sibling(k).wait_recv()

            def add(n, carry):
                rr = pl.ds(pl.multiple_of(n * RS_CH, RS_CH), RS_CH)
                out_ref[k, rr, :] = (gt_ref[k, c, rr, :].astype(F32) + sib_ref[k, rr, :].astype(F32)).astype(BF16)
                return carry

            lax.fori_loop(0, IN_HALF // RS_CH, add, 0)
            to_chip(k).start()

        mine_s = pl.ds(pl.multiple_of(c * hs, 8), hs)
        other_s = pl.ds(pl.multiple_of((1 - c) * hs, 8), hs)

        def small_to_sibling():
            return rc(15, sm0_ref.at[other_s], ssib_ref, sib)

        def small_to_chip(k):
            return rc(16 + k, schip_ref.at[j], schip_ref.at[j], (*chips[k], c))

        def small_share():
            return rc(19, sres_ref.at[c], sres_ref.at[c], sib)

        for k in range(N_GW):
            @pl.when(i == k)
            def _(k=k):
                slot = k % 2
                if k == 0:
                    red_ref[...] = jnp.zeros_like(red_ref)
                    wt_copy.start()
                    sm0_ref[...] = jnp.zeros_like(sm0_ref)
                    sm0_ref[0:16, :] = dcw_ref[...]
                    sm0_ref[16:17, 0:CONV_W] = dvec_ref[0:1, :]
                    sm0_ref[16:17, CONV_W:2 * CONV_W] = dvec_ref[1:2, :]
                    sm0_ref[17:18, 0:CONV_W] = dvec_ref[2:3, :]
                    for r in range(3):
                        sm0_ref[17:18, CONV_W + 128 * r:CONV_W + 128 * (r + 1)] = sma_ref[r:r + 1, :]
                    sm0_ref[18:19, :] = row0_ref[1:2, :]
                    small_to_sibling().start()
                if k == 1:
                    small_to_sibling().wait_recv()
                    schip_ref[j] = sm0_ref[mine_s, :] + ssib_ref[...]
                    for kk in range(3):
                        small_to_chip(kk).start()
                if k == N_GW - 1:
                    for kk in range(3):
                        jk = rel_chip[kk]
                        rc(16 + kk, schip_ref.at[jk], schip_ref.at[jk], sib).wait_recv()
                    tot = schip_ref[0]
                    for d in range(1, N_CHIPS):
                        tot = tot + schip_ref[d]
                    sres_ref[c] = tot
                    small_share().start()
                for case in range(N_CHIPS):
                    if k == 0:
                        @pl.when(rel_chip[0] == case)
                        def _():
                            for cp in window(case, 0):
                                cp.start()
                    if k + 1 < N_GW:
                        @pl.when(rel_chip[k + 1] == case)
                        def _():
                            for cp in window(case, 1 - slot):
                                cp.start()
                for case in range(N_CHIPS):
                    @pl.when(rel_chip[k] == case)
                    def _():
                        for cp in window(case, slot):
                            cp.wait()
                g = lax.dot_general(stg_ref[slot], h_ref[...], (((0,), (0,)), ((), ())), preferred_element_type=F32)
                for off in sorted(set(WIN_OFF)):
                    @pl.when(rel_chip[k] % 2 == (1 if off else 0))
                    def _():
                        gt_ref[k, 0] = g[off:off + IN_HALF].astype(BF16)
                        gt_ref[k, 1] = g[off + IN_HALF:off + 2 * IN_HALF].astype(BF16)
                to_sibling(k).start()
                if k >= 1:
                    trade(k - 1)

        @pl.when(i == N_GW)
        def _():
            wt_copy.wait()

        @pl.when(i >= N_GW)
        def _():
            xv = x_ref[...]
            r = lax.rsqrt(jnp.mean(xv * xv, axis=-1, keepdims=True) + EPS)
            xh = xv * r
            n = xh * nw_ref[...]
            dproj = jnp.concatenate([ref[...] for ref in d_ref], axis=1)
            dh = jnp.dot(dproj, wt_ref[...], preferred_element_type=F32)
            red_ref[0:1, :] += jnp.sum(dh, axis=0, keepdims=True)
            red_ref[1:2, :] += jnp.sum(dh * n, axis=0, keepdims=True)
            dn = dh * s1_ref[...]
            red_ref[2:3, :] += jnp.sum(dn * xh, axis=0, keepdims=True)
            dxh = dn * nw_ref[...]
            gx_ref[...] = dout_ref[...] + r * (dxh - xh * jnp.mean(dxh * xh, axis=-1, keepdims=True))

        @pl.when(i == nstep - 1)
        def _():
            sall_ref[dev] = row0_ref[...]
            sall_ref[dev, 2:5, :] = red_ref[0:3, :]
            sends = [rc(8 + k, sall_ref.at[dev], sall_ref.at[dev], peer) for k, peer in enumerate(peers)]
            for cp in sends:
                cp.start()
            sends += [to_sibling(k) for k in range(N_GW)] + [to_chip(k) for k in range(3)]
            sends += [small_to_sibling(), small_share()] + [small_to_chip(k) for k in range(3)]
            own = N_GW - 1
            to_sibling(own).wait_recv()
            for k in range(3):
                to_chip(k).wait_recv()

            def total(n, carry):
                rr = pl.ds(pl.multiple_of(n * RS_CH, RS_CH), RS_CH)
                acc = gt_ref[own, c, rr, :].astype(F32) + sib_ref[own, rr, :].astype(F32)
                for k in range(3):
                    acc = acc + in_ref[k, rr, :].astype(F32)
                res_ref[c, rr, :] = acc
                return carry

            lax.fori_loop(0, IN_HALF // RS_CH, total, 0)
            share = rc(7, res_ref.at[c], res_ref.at[c], sib)
            share.start()
            sends.append(share)
            for k, (px, py, pc) in enumerate(peers):
                pdev = 4 * px + 2 * py + pc
                rc(8 + k, sall_ref.at[pdev], sall_ref.at[pdev], (px, py, pc)).wait_recv()
            rows_ref[...] = sall_ref[...]
            rc(19, sres_ref.at[1 - c], sres_ref.at[1 - c], sib).wait_recv()
            ssum_ref[0:hs, :] = sres_ref[0]
            ssum_ref[hs:rows0, :] = sres_ref[1]
            rc(7, res_ref.at[1 - c], res_ref.at[1 - c], sib).wait_recv()
            back = pltpu.make_async_copy(res_ref, gw_hbm, lsem.at[1])
            back.start()
            for cp in sends:
                cp.wait_send()
            back.wait()

    blk = lambda i: jnp.maximum(i - N_GW, 0)
    row = lambda w: pl.BlockSpec((tm, w), lambda i: (blk(i), 0))
    vec = pl.BlockSpec((1, D_MODEL), lambda i: (0, 0))
    const = lambda shape: pl.BlockSpec(shape, lambda i: (0,) * len(shape))
    hbm = pl.BlockSpec(memory_space=pl.ANY)
    return pl.pallas_call(
        body,
        name="in_proj_bwd",
        grid=(nstep,),
        in_specs=[hbm] * npart + [row(w) for w in DPROJ_WIDTHS] + [row(D_MODEL), row(D_MODEL), vec, vec,
                  pl.BlockSpec((t, D_MODEL), lambda i: (0, 0), pipeline_mode=pl.Buffered(1)), hbm, const((16, D_MODEL)),
                  const((8, CONV_W)), const((8, 128)), const((8, D_MODEL))],
        out_specs=[row(D_MODEL), hbm, const((rows0, D_MODEL)), const((N_DEV, 8, D_MODEL))],
        out_shape=[jax.ShapeDtypeStruct((t, D_MODEL), F32), jax.ShapeDtypeStruct((2, IN_HALF, D_MODEL), F32),
                   jax.ShapeDtypeStruct((rows0, D_MODEL), F32), jax.ShapeDtypeStruct((N_DEV, 8, D_MODEL), F32)],
        scratch_shapes=[pltpu.VMEM((2, t, WIN_W), BF16), pltpu.VMEM((IN_W, D_MODEL), BF16),
                        pltpu.VMEM((N_CHIPS, 2, IN_HALF, D_MODEL), BF16), pltpu.VMEM((N_CHIPS, IN_HALF, D_MODEL), BF16),
                        pltpu.VMEM((3, IN_HALF, D_MODEL), BF16), pltpu.VMEM((3, IN_HALF, D_MODEL), BF16),
                        pltpu.VMEM((2, IN_HALF, D_MODEL), F32), pltpu.VMEM((N_DEV, 8, D_MODEL), F32),
                        pltpu.VMEM((8, D_MODEL), F32), pltpu.VMEM((rows0, D_MODEL), F32), pltpu.VMEM((hs, D_MODEL), F32),
                        pltpu.VMEM((N_CHIPS, hs, D_MODEL), F32),
                        pltpu.VMEM((2, hs, D_MODEL), F32), pltpu.SemaphoreType.DMA((2, 3)), pltpu.SemaphoreType.DMA((2,)),
                        pltpu.SemaphoreType.DMA((n_sem,)), pltpu.SemaphoreType.DMA((n_sem,))],
        compiler_params=_cparams(dimension_semantics=("arbitrary",)),
    )(*dparts, *dparts, x, dout, s1, nw, h, wt_full, dcw, dvec, sm_a, row0)


MESH = pl.DeviceIdType.MESH


def _place():
    x, y, c = lax.axis_index("x"), lax.axis_index("y"), lax.axis_index("c")
    chips = [(1 - x, y), (x, 1 - y), (1 - x, 1 - y)]
    return x, y, c, chips


def _remote(sems_s, sems_r, k, src, dst, to):
    return pltpu.make_async_remote_copy(src_ref=src, dst_ref=dst, send_sem=sems_s.at[k], recv_sem=sems_r.at[k],
                                        device_id=to, device_id_type=MESH)


RS_CH = 32
RS_SEMS = 5


def _rs_to_sibling(rc, s0, theirs, sib_ref, sib):
    cp = rc(s0, theirs, sib_ref, sib)
    cp.start()
    return cp


def _rs_trade(rc, s0, theirs, mine, sib_ref, out_ref, in_ref, rows, c, sib, chips):
    rc(s0, theirs, sib_ref, sib).wait_recv()
    cps = []
    for k, (cx, cy) in enumerate(chips):
        jk = 2 * cx + cy

        def add(i, carry, jk=jk, k=k):
            rr = pl.ds(pl.multiple_of(i * RS_CH, RS_CH), RS_CH)
            out_ref[k, rr, :] = (mine[jk, rr, :].astype(F32) + sib_ref[jk, rr, :].astype(F32)).astype(BF16)
            return carry

        lax.fori_loop(0, rows // RS_CH, add, 0)
        cps.append(rc(s0 + 1 + k, out_ref.at[k], in_ref.at[k], (cx, cy, c)))
        cps[-1].start()
    return cps


def _rs_total(rc, s0, mine, sib_ref, out_ref, in_ref, res_ref, rows, j, c, sib):
    for k in range(3):
        rc(s0 + 1 + k, out_ref.at[k], in_ref.at[k], sib).wait_recv()

    def total(i, carry):
        rr = pl.ds(pl.multiple_of(i * RS_CH, RS_CH), RS_CH)
        acc = mine[j, rr, :].astype(F32) + sib_ref[j, rr, :].astype(F32)
        for k in range(3):
            acc = acc + in_ref[k, rr, :].astype(F32)
        res_ref[c, rr, :] = acc
        return carry

    lax.fori_loop(0, rows // RS_CH, total, 0)
    cp = rc(s0 + 4, res_ref.at[c], res_ref.at[c], sib)
    cp.start()
    return cp


def _rs_done(rc, s0, res_ref, c, sib):
    rc(s0 + 4, res_ref.at[1 - c], res_ref.at[1 - c], sib).wait_recv()


def _rs_scratch(rows):
    return [pltpu.VMEM((N_CHIPS, rows, D_MODEL), BF16), pltpu.VMEM((3, rows, D_MODEL), BF16),
            pltpu.VMEM((3, rows, D_MODEL), BF16)]


MAIN_W = 640
MAIN_DST = (((0, 0, 512), (1, 0, 128)), ((2, 0, 512), (3, 0, 128)), ((3, 128, 384), (4, 0, 256)), ((4, 384, 128), (5, 0, 512)))
PAIR_DST = ((1, 128, 128), (4, 256, 128))


def _in_proj_gather(x, wt, c_row, w_ada, b_ada, nw):
    t = x.shape[0]
    ch = 512
    n_sem = 16

    def body(x_hbm, wt_ref, c_ref, wada_ref, bada_ref, nw_ref,
             q_hbm, kv_hbm, ga_hbm, ua_hbm, ug_hbm, gb_hbm, h_hbm, w4_hbm, call_ref, ada_ref,
             x_ref, h_ref, w4_ref, stg_ref, pstg_ref, part_ref, lsem, osem, wsem, ssem, rsem):
        outs = (q_hbm, kv_hbm, ga_hbm, ua_hbm, ug_hbm, gb_hbm)
        x_, y_, c, chips = _place()
        j = 2 * x_ + y_
        dev = 2 * j + c
        sib = (x_, y_, 1 - c)
        idx = [2 * cx + cy for cx, cy in chips]
        rc = functools.partial(_remote, ssem, rsem)
        x_copy = pltpu.make_async_copy(x_hbm, x_ref, lsem.at[0])
        x_copy.start()

        def rows_of(s, cc):
            return pl.ds(pl.multiple_of(2 * IN_HALF * s + IN_HALF * cc, 16), IN_HALF)

        w4_ref[rows_of(j, 0), :] = wt_ref[0].astype(BF16)
        w4_ref[rows_of(j, 1), :] = wt_ref[1].astype(BF16)
        call_ref[dev] = c_ref[...]
        sends = []
        peers = [(px, py, pc) for px in (x_, 1 - x_) for py in (y_, 1 - y_) for pc in (c, 1 - c)][1:]
        for k, peer in enumerate(peers):
            sends.append(rc(k, call_ref.at[dev], call_ref.at[dev], peer))
        for cp in sends:
            cp.start()

        for k, (px, py, pc) in enumerate(peers):
            pdev = 4 * px + 2 * py + pc
            rc(k, call_ref.at[pdev], call_ref.at[pdev], (px, py, pc)).wait_recv()
        rowid = lax.broadcasted_iota(jnp.int32, (N_DEV, D_MODEL), 0)
        call = jnp.zeros((N_DEV, D_MODEL), F32)
        for r in range(N_DEV):
            call = jnp.where(rowid == r, jnp.broadcast_to(call_ref[r], (N_DEV, D_MODEL)), call)
        bsh = bada_ref[:, 0:ADA_SHARD]
        for k in range(1, N_CHIPS):
            bsh = jnp.where(j == k, bada_ref[:, ADA_SHARD * k:ADA_SHARD * (k + 1)], bsh)
        part = jnp.dot(_silu(call).astype(BF16), wada_ref[...].astype(BF16), preferred_element_type=F32) + bsh
        for r in range(N_DEV):
            part_ref[r] = part[r:r + 1, :]
        ada_ref[j] = part_ref[dev]
        for k, chip in enumerate(chips):
            sends.append(rc(13 + k, part_ref.at[2 * idx[k] + c], ada_ref.at[j], (*chip, c)))
            sends[-1].start()
        for k, chip in enumerate(chips):
            sends.append(rc(7 + k, w4_ref.at[rows_of(j, c)], w4_ref.at[rows_of(j, c)], (*chip, c)))
            sends[-1].start()

        x_copy.wait()

        def prenorm(i, carry):
            rr = pl.ds(pl.multiple_of(i * ch, ch), ch)
            xv = x_ref[rr, :]
            r = lax.rsqrt(jnp.mean(xv * xv, axis=-1, keepdims=True) + EPS)
            x_ref[rr, :] = (xv * r) * nw_ref[...]
            return carry

        lax.fori_loop(0, t // ch, prenorm, 0)
        for k in range(3):
            rc(13 + k, ada_ref.at[idx[k]], ada_ref.at[idx[k]], sib).wait_recv()

        shift = jnp.concatenate([ada_ref[0], ada_ref[1][:, 0:256]], axis=1)
        s1 = 1.0 + jnp.concatenate([ada_ref[1][:, 256:768], ada_ref[2][:, 0:512]], axis=1)

        def norm(i, carry):
            rr = pl.ds(pl.multiple_of(i * ch, ch), ch)
            h_ref[rr, :] = (x_ref[rr, :] * s1 + shift).astype(BF16)
            return carry

        lax.fori_loop(0, t // ch, norm, 0)
        h_copy = pltpu.make_async_copy(h_ref, h_hbm, lsem.at[1])
        h_copy.start()

        def put_main(case, slot):
            cps, col = [], 0
            for n, (a, c0, w) in enumerate(MAIN_DST[case]):
                cps.append(pltpu.make_async_copy(stg_ref.at[slot, :, pl.ds(col, w)], outs[a].at[:, pl.ds(c0, w)], osem.at[slot, n]))
                col += w
            return cps

        def put_pair(case, slot):
            a, c0, w = PAIR_DST[case]
            return pltpu.make_async_copy(pstg_ref.at[slot], outs[a].at[:, pl.ds(c0, w)], osem.at[slot, 2])

        def project(first_row, width, dst, slot):
            wrows = pl.ds(pl.multiple_of(first_row, 128), width)

            def blk(i, carry):
                rr = pl.ds(pl.multiple_of(i * ch, ch), ch)
                dst[slot, rr, :] = lax.dot_general(h_ref[rr, :], w4_ref[wrows, :], (((1,), (1,)), ((), ())),
                                                   preferred_element_type=F32)
                return carry

            lax.fori_loop(0, t // ch, blk, 0)

        def phase(p, s, pair):
            slot = p % 2
            if p >= 2:
                for case in range(N_CHIPS):
                    @pl.when(order[p - 2] == case)
                    def _():
                        for cp in put_main(case, slot):
                            cp.wait()
            if p == 3:
                for case in range(2):
                    @pl.when(j // 2 == case)
                    def _():
                        put_pair(case, 0).wait()
            project(2 * IN_HALF * s + 64 * (s % 2), MAIN_W, stg_ref, slot)
            for case in range(N_CHIPS):
                @pl.when(s == case)
                def _():
                    for cp in put_main(case, slot):
                        cp.start()
            if pair is not None:
                project(MAIN_W + 2 * (2 * IN_HALF) * pair, 128, pstg_ref, slot % 2 if p == 2 else 1)
                for case in range(2):
                    @pl.when(pair == case)
                    def _():
                        put_pair(case, 0 if p == 2 else 1).start()

        order = [j] + idx
        w_out = [pltpu.make_async_copy(w4_ref.at[pl.ds(pl.multiple_of(2 * IN_HALF * s, 32), 2 * IN_HALF)],
                                       w4_hbm.at[pl.ds(pl.multiple_of(2 * IN_HALF * s, 32), 2 * IN_HALF)], wsem.at[p])
                 for p, s in enumerate(order)]
        w_out[0].start()
        phase(0, j, None)
        passed = []
        for k in range(3):
            jk = idx[k]
            rc(7 + k, w4_ref.at[rows_of(jk, c)], w4_ref.at[rows_of(jk, c)], sib).wait_recv()
            passed.append(rc(10 + k, w4_ref.at[rows_of(jk, c)], w4_ref.at[rows_of(jk, c)], sib))
            passed[-1].start()
            rc(10 + k, w4_ref.at[rows_of(jk, 1 - c)], w4_ref.at[rows_of(jk, 1 - c)], sib).wait_recv()
            w_out[1 + k].start()
            if k == 0:
                phase(1, jk, None)
            elif k == 1:
                phase(2, jk, j // 2)
            else:
                phase(3, jk, 1 - j // 2)

        for case in range(N_CHIPS):
            for p in (2, 3):
                @pl.when(order[p] == case)
                def _():
                    for cp in put_main(case, p % 2):
                        cp.wait()
        for case in range(2):
            @pl.when(1 - j // 2 == case)
            def _():
                put_pair(case, 1).wait()
        h_copy.wait()
        for cp in w_out:
            cp.wait()
        for cp in sends + passed:
            cp.wait_send()

    vm = pl.BlockSpec(memory_space=pltpu.VMEM)
    hbm = pl.BlockSpec(memory_space=pl.ANY)
    widths = (512, 256, 512, 512, 512, 512)
    return pl.pallas_call(
        body,
        name="in_proj",
        in_specs=[hbm, vm, vm, vm, vm, vm],
        out_specs=[hbm] * 8 + [vm, vm],
        out_shape=[jax.ShapeDtypeStruct((t, w), F32) for w in widths]
        + [jax.ShapeDtypeStruct((t, D_MODEL), BF16), jax.ShapeDtypeStruct((IN_W, D_MODEL), BF16),
           jax.ShapeDtypeStruct((N_DEV, 1, D_MODEL), F32), jax.ShapeDtypeStruct((N_CHIPS, 1, ADA_SHARD), F32)],
        scratch_shapes=[pltpu.VMEM((t, D_MODEL), F32), pltpu.VMEM((t, D_MODEL), BF16), pltpu.VMEM((IN_W, D_MODEL), BF16),
                        pltpu.VMEM((2, t, MAIN_W), F32), pltpu.VMEM((2, t, 128), F32), pltpu.VMEM((N_DEV, 1, ADA_SHARD), F32),
                        pltpu.SemaphoreType.DMA((2,)), pltpu.SemaphoreType.DMA((2, 3)), pltpu.SemaphoreType.DMA((N_CHIPS,)),
                        pltpu.SemaphoreType.DMA((n_sem,)), pltpu.SemaphoreType.DMA((n_sem,))],
        compiler_params=_cparams(),
    )(x, wt, c_row, w_ada, b_ada, nw)


def _adamw_math(w, g, m, v):
    m2 = ADAM_B1 * m + (1.0 - ADAM_B1) * g
    v2 = ADAM_B2 * v + (1.0 - ADAM_B2) * (g * g)
    m_hat = m2 / (1.0 - ADAM_B1 ** ADAM_STEP)
    v_hat = v2 / (1.0 - ADAM_B2 ** ADAM_STEP)
    delta = -ADAM_LR * (m_hat / (jnp.sqrt(v_hat) + ADAM_EPS) + ADAM_WD * w)
    return delta, m2, v2


def _adamw(name, w, g, m, v, tm):
    r, cdim = w.shape

    def body(w_ref, g_ref, m_ref, v_ref, g2_ref, d_ref, m2_ref, v2_ref):
        g = g_ref[...]
        g2_ref[...] = g
        d_ref[...], m2_ref[...], v2_ref[...] = _adamw_math(w_ref[...], g, m_ref[...], v_ref[...])

    blk = pl.BlockSpec((tm, cdim), lambda i: (i, 0))
    return pl.pallas_call(
        body,
        name=name,
        grid=(r // tm,),
        in_specs=[blk] * 4,
        out_specs=[blk] * 4,
        out_shape=[jax.ShapeDtypeStruct((r, cdim), F32)] * 4,
        compiler_params=_cparams(dimension_semantics=("arbitrary",)),
    )(w, g, m, v)


def _adamw_ada(w, m, v, cact_t, dcols, through):
    r, cdim = w.shape
    tm = 256
    nstep = r // tm

    def body(w_ref, m_ref, v_ref, ct_ref, dc_ref, th_hbm, g_ref, d_ref, m2_ref, v2_ref, th_out, sem):
        i = pl.program_id(0)
        hand_on = pltpu.make_async_copy(th_hbm, th_out, sem)

        @pl.when(i == 0)
        def _():
            hand_on.start()

        g = jnp.dot(ct_ref[...].astype(BF16), dc_ref[...].astype(BF16), preferred_element_type=F32)
        g_ref[...] = g
        d_ref[...], m2_ref[...], v2_ref[...] = _adamw_math(w_ref[...], g, m_ref[...], v_ref[...])

        @pl.when(i == nstep - 1)
        def _():
            hand_on.wait()

    blk = pl.BlockSpec((tm, cdim), lambda i: (i, 0))
    hbm = pl.BlockSpec(memory_space=pl.ANY)
    return pl.pallas_call(
        body,
        name="adamw_w_ada",
        grid=(nstep,),
        in_specs=[blk] * 3 + [pl.BlockSpec((tm, N_DEV), lambda i: (i, 0)), pl.BlockSpec((N_DEV, cdim), lambda i: (0, 0)), hbm],
        out_specs=[blk] * 4 + [hbm],
        out_shape=[jax.ShapeDtypeStruct((r, cdim), F32)] * 4 + [jax.ShapeDtypeStruct(through.shape, through.dtype)],
        scratch_shapes=[pltpu.SemaphoreType.DMA(())],
        compiler_params=_cparams(dimension_semantics=("arbitrary",)),
    )(w, m, v, cact_t, dcols, through)


def _adamw_small(ws, ms, vs, ssum, rows):
    n = len(ws)

    def body(*refs):
        w_r, m_r, v_r = refs[0:n], refs[n:2 * n], refs[2 * n:3 * n]
        ss_ref, rows_ref = refs[3 * n], refs[3 * n + 1]
        g_r, d_r, m2_r, v2_r = (refs[3 * n + 2 + k * n:3 * n + 2 + (k + 1) * n] for k in range(4))
        loss_ref = refs[7 * n + 2]
        j = 2 * lax.axis_index("x") + lax.axis_index("y")
        rsum = rows_ref[0]
        for d in range(1, N_DEV):
            rsum = rsum + rows_ref[d]
        taps = []
        for t in range(CONV_TAPS):
            row = ss_ref[t // 2:t // 2 + 1, :]
            c0 = CONV_W * (t % 2)
            pick = row[:, c0:c0 + 128]
            for k in range(1, N_CHIPS):
                pick = jnp.where(j == k, row[:, c0 + 128 * k:c0 + 128 * (k + 1)], pick)
            taps.append(pick)
        grads = [jnp.concatenate([rsum[2:3], rsum[3:4], rsum[0:1]], axis=1), rsum[4:5],
                 ss_ref[17:18, 512:512 + HEAD_DIM], ss_ref[17:18, 640:640 + HEAD_DIM], ss_ref[17:18, 768:776],
                 None, ss_ref[16:17, 0:CONV_W], ss_ref[16:17, CONV_W:2 * CONV_W], ss_ref[17:18, 0:CONV_W]]
        for i in range(n):
            if grads[i] is None:
                for t in range(CONV_TAPS):
                    g_r[i][t:t + 1, :] = taps[t]
                g = g_r[i][...]
            else:
                g = grads[i]
                g_r[i][...] = g
            d_r[i][...], m2_r[i][...], v2_r[i][...] = _adamw_math(w_r[i][...], g, m_r[i][...], v_r[i][...])
        loss_ref[...] = (0.5 / D_MODEL) * jnp.sum(ss_ref[18:19, :], axis=1, keepdims=True)

    vm = pl.BlockSpec(memory_space=pltpu.VMEM)
    shapes = [jax.ShapeDtypeStruct(w.shape, F32) for w in ws]
    out = pl.pallas_call(
        body,
        name="adamw_small",
        in_specs=[vm] * (3 * n + 2),
        out_specs=[vm] * (4 * n + 1),
        out_shape=shapes * 4 + [jax.ShapeDtypeStruct((1, 1), F32)],
        compiler_params=_cparams(),
    )(*ws, *ms, *vs, ssum, rows)
    return out[0:n], out[n:2 * n], out[2 * n:3 * n], out[3 * n:4 * n], out[4 * n]


def _rope_tables(t):
    inv = ROPE_THETA ** (-jnp.arange(0, HEAD_DIM, 2, dtype=F32) / HEAD_DIM)
    ang = jnp.arange(t, dtype=F32)[:, None] * inv[None, :]
    cos, sin = jnp.cos(ang), jnp.sin(ang)
    return jnp.tile(cos, (1, 4)), jnp.tile(jnp.concatenate([-sin, sin], axis=1), (1, 2))


def kernel(x, c, w_ada, b_ada, norm_w, w_in, q_norm_w, k_norm_w, sinks, conv_w, conv_b, ln_w, ln_b, w_out, loss_target, m_w_ada, m_b_ada, m_norm_w, m_w_in, m_q_norm_w, m_k_norm_w, m_sinks, m_conv_w, m_conv_b, m_ln_w, m_ln_b, m_w_out, v_w_ada, v_b_ada, v_norm_w, v_w_in, v_q_norm_w, v_k_norm_w, v_sinks, v_conv_w, v_conv_b, v_ln_w, v_ln_b, v_w_out):
    xi, yi = lax.axis_index("x"), lax.axis_index("y")
    j = 2 * xi + yi
    x2, tgt = x[0], loss_target[0]
    t = x2.shape[0]

    wt_s, mt_s, vt_s = w_in[0].T, m_w_in[0].T, v_w_in[0].T
    cw_pad = jnp.pad(conv_w[0], ((0, 1), (0, 0)))

    q_raw, kv_raw, ga, ua, ug, gb, h, w_full, call, ada4 = _in_proj_gather(
        x2, wt_s.reshape(2, IN_HALF, D_MODEL), c, w_ada[0], b_ada, norm_w)
    ada = ada4.reshape(1, 3 * D_MODEL)
    s1, gate = 1.0 + ada[:, D_MODEL:2 * D_MODEL], ada[:, 2 * D_MODEL:]

    cos_f, sin_s = _rope_tables(t)
    qw2, kw2 = jnp.tile(q_norm_w, (1, 2)), jnp.tile(k_norm_w, (1, 2))

    o, mix_a, wo4, cw4 = _attn_fwd(q_raw, kv_raw, ga, qw2, kw2, sinks, cos_f, sin_s,
                                   w_out[0].reshape(2, OUT_HALF, D_MODEL), cw_pad)
    w_out_full = wo4.reshape(D_MODEL, D_MODEL)
    cw_full = jnp.concatenate([cw4[i] for i in range(N_CHIPS)], axis=1)
    cz, mix_b = _conv_fwd(ua, ug, gb, cw_full, conv_b, ln_w, ln_b)
    dout, dmix_a, dmix_b, gwo_bf, red_o = _out_proj(mix_a, mix_b, x2, tgt, gate, w_out_full)

    dq, dkv, dga, sm_a, gwo = _attn_bwd(q_raw, kv_raw, ga, o, dmix_a, qw2, kw2, sinks, cos_f, sin_s,
                                        gwo_bf.reshape(N_CHIPS, 2, OUT_HALF, D_MODEL))
    dua, dug, dgb, dcw, dvec = _conv_bwd(ua, ug, gb, cz, dmix_b, cw_full, ln_w, ln_b)
    dparts = (dq, dkv, dga, dua, dug, dgb)

    grad_x, gw, ssum, rows = _in_proj_bwd(dparts, h, x2, dout, s1, norm_w, w_full, dcw, dvec, sm_a, red_o)

    gt_w_in = gw.reshape(2 * IN_HALF, D_MODEL)
    g_w_out = gwo.reshape(D_MODEL // N_CHIPS, D_MODEL)
    d_ada_all = jnp.concatenate([rows[:, 2], rows[:, 3], rows[:, 0]], axis=1)
    dcols = lax.dynamic_slice(d_ada_all, (0, ADA_SHARD * j), (N_DEV, ADA_SHARD))
    cact_t = jax.nn.silu(call.reshape(N_DEV, D_MODEL)).T

    g_w_ada, d_w_ada, nm_w_ada, nv_w_ada, grad_x = _adamw_ada(w_ada[0], m_w_ada[0], v_w_ada[0], cact_t, dcols, grad_x)
    gt_w_in, dt_w_in, nmt_w_in, nvt_w_in = _adamw("adamw_w_in", wt_s, gt_w_in, mt_s, vt_s, 176)
    g_w_in, d_w_in, nm_w_in, nv_w_in = gt_w_in.T, dt_w_in.T, nmt_w_in.T, nvt_w_in.T
    g_w_out, d_w_out, nm_w_out, nv_w_out = _adamw("adamw_w_out", w_out[0], g_w_out, m_w_out[0], v_w_out[0], 128)
    ws = [b_ada, norm_w, q_norm_w, k_norm_w, sinks, conv_w[0], conv_b, ln_w, ln_b]
    ms = [m_b_ada, m_norm_w, m_q_norm_w, m_k_norm_w, m_sinks, m_conv_w[0], m_conv_b, m_ln_w, m_ln_b]
    vs = [v_b_ada, v_norm_w, v_q_norm_w, v_k_norm_w, v_sinks, v_conv_w[0], v_conv_b, v_ln_w, v_ln_b]
    gs, ds, nms, nvs, loss11 = _adamw_small(ws, ms, vs, ssum, rows)
    loss = loss11[0, 0]

    def order(ada_v, in_v, out_v, sm):
        b, nw_, qw_, kw_, sk_, cw_, cb_, lw_, lb_ = sm
        return [ada_v[None], b, nw_, in_v[None], qw_, kw_, sk_, cw_[None], cb_, lw_, lb_, out_v[None]]

    grads = order(g_w_ada, g_w_in, g_w_out, gs)
    deltas = order(d_w_ada, d_w_in, d_w_out, ds)
    new_m = order(nm_w_ada, nm_w_in, nm_w_out, nms)
    new_v = order(nv_w_ada, nv_w_in, nv_w_out, nvs)
    return (loss, grad_x[None], *grads, *deltas, *new_m, *new_v)
```

```python
import functools

import jax
import jax.numpy as jnp
from jax import lax
from jax.experimental import pallas as pl
from jax.experimental.pallas import tpu as pltpu

F32 = jnp.float32
BF16 = jnp.bfloat16

D_MODEL = 1024
ATTN_W = 512
KV_W = 128
CONV_W = 512
IN_W = 2816
HEAD_DIM = 64
CONV_TAPS = 31
QBLK = 128
EPS = 1e-6
ROPE_THETA = 10000.0

ADAM_LR = 0.001
ADAM_B1 = 0.9
ADAM_B2 = 0.999
ADAM_EPS = 1e-08
ADAM_WD = 0.01
ADAM_STEP = 10

N_CHIPS = 4
N_DEV = 8
IN_HALF = IN_W // N_CHIPS // 2
OUT_HALF = D_MODEL // N_CHIPS // 2
ADA_SHARD = 3 * D_MODEL // N_CHIPS

VMEM_LIMIT = 56 * 1024 * 1024
CONV_PAD = 32


def _cparams(**kw):
    return pltpu.CompilerParams(vmem_limit_bytes=VMEM_LIMIT, **kw)


def _sigmoid(v):
    return 1.0 / (1.0 + jnp.exp(-v))


def _silu(v):
    return v * _sigmoid(v)


def _dsilu(v):
    s = _sigmoid(v)
    return s * (1.0 + v * (1.0 - s))


def _lane(shape):
    return lax.broadcasted_iota(jnp.int32, shape, len(shape) - 1)


PUT_ROWS = 512


def _fetch(hbm_refs, vmem_refs, sem):
    cps = [pltpu.make_async_copy(h, v, sem.at[i]) for i, (h, v) in enumerate(zip(hbm_refs, vmem_refs))]
    for cp in cps:
        cp.start()
    return cps


def _put(vmem_ref, hbm_ref, sem, m):
    r = pl.ds(pl.multiple_of(m * PUT_ROWS, PUT_ROWS), PUT_ROWS)
    return pltpu.make_async_copy(vmem_ref.at[r], hbm_ref.at[r], sem.at[m])


def _put_all(pairs, sems, m):
    for (v, h), sem in zip(pairs, sems):
        _put(v, h, sem, m).start()


def _put_wait(pairs, sems, n):
    for (v, h), sem in zip(pairs, sems):
        for m in range(n):
            _put(v, h, sem, m).wait()


def _head_mean(s, left):
    sl = jnp.sum(jnp.where(left, s, 0.0), axis=-1, keepdims=True)
    sr = jnp.sum(jnp.where(left, 0.0, s), axis=-1, keepdims=True)
    return jnp.where(left, sl, sr) * (1.0 / HEAD_DIM)


def _rot(v, first):
    return jnp.where(first, pltpu.roll(v, 96, 1), pltpu.roll(v, 32, 1))


def _norm_rope(v, w, cos, sin_s, left, first):
    r = lax.rsqrt(_head_mean(v * v, left) + EPS)
    xh = v * r
    n = xh * w
    return n * cos + _rot(n, first) * sin_s, xh, r


def _norm_rope_bwd(d, xh, r, w, cos, sin_s, left, first):
    dn = d * cos - _rot(d, first) * sin_s
    dw = jnp.sum(dn * xh, axis=0, keepdims=True)
    dxh = dn * w
    return r * (dxh - xh * _head_mean(dxh * xh, left)), dw


def _dup_heads(v, left):
    sw = pltpu.roll(v, 64, 1)
    return jnp.where(left, v, sw), jnp.where(left, sw, v)


def _prep_kv(kv_ref, kw_ref, cos_ref, sin_ref, ka_ref, va_ref, t):
    ch = 256
    for g in range(2):
        ka_ref[g, 0:QBLK, :] = jnp.zeros((QBLK, 128), BF16)
        va_ref[g, 0:QBLK, :] = jnp.zeros((QBLK, 128), BF16)

    def chunk(i, carry):
        r0 = pl.multiple_of(i * ch, ch)
        left = _lane((ch, 128)) < 64
        first = (_lane((ch, 128)) % 64) < 32
        k = kv_ref[pl.ds(r0, ch), 0:128]
        v = kv_ref[pl.ds(r0, ch), 128:256]
        kr, _, _ = _norm_rope(k, kw_ref[...], cos_ref[pl.ds(r0, ch), :], sin_ref[pl.ds(r0, ch), :], left, first)
        k0, k1 = _dup_heads(kr, left)
        v0, v1 = _dup_heads(v, left)
        ka_ref[0, pl.ds(QBLK + r0, ch), :] = k0.astype(BF16)
        ka_ref[1, pl.ds(QBLK + r0, ch), :] = k1.astype(BF16)
        va_ref[0, pl.ds(QBLK + r0, ch), :] = v0.astype(BF16)
        va_ref[1, pl.ds(QBLK + r0, ch), :] = v1.astype(BF16)
        return carry

    lax.fori_loop(0, t // ch, chunk, 0)


def _band_mask(n):
    qi = lax.broadcasted_iota(jnp.int32, (2 * QBLK, 2 * QBLK), 0) % QBLK
    kj = lax.broadcasted_iota(jnp.int32, (2 * QBLK, 2 * QBLK), 1)
    local = (kj > qi) & (kj <= qi + QBLK)
    return local & ((n > 0) | (kj >= QBLK))


def _softmax_pair(s, mask, sink0, sink1):
    row = lax.broadcasted_iota(jnp.int32, (2 * QBLK, 1), 0)
    sink = jnp.where(row < QBLK, sink0, sink1)
    s = jnp.where(mask, s, -jnp.inf)
    m = jnp.maximum(jnp.max(s, axis=-1, keepdims=True), sink)
    e = jnp.exp(s - m)
    es = jnp.exp(sink - m)
    inv = 1.0 / (jnp.sum(e, axis=-1, keepdims=True) + es)
    return e * inv, es * inv


def _stack_heads(v, left):
    return jnp.concatenate([jnp.where(left, v, 0.0), jnp.where(left, 0.0, v)], axis=0)


def _attn_fwd(q_raw, kv_raw, ga, qw2, kw2, sinks, cos_f, sin_s, wo, cw):
    t = q_raw.shape[0]
    nblk = t // QBLK
    per_put = PUT_ROWS // QBLK

    def body(q_hbm, kv_ref, ga_hbm, qw_ref, kw_ref, sk_ref, cos_hbm, sin_hbm, wo_ref, cw_ref,
             o_hbm, mix_hbm, wo4_ref, cw4_ref, ka_ref, va_ref, q_ref, ga_ref, o_ref, mix_ref, cos_ref, sin_ref,
             isem, osem0, osem1, ssem, rsem):
        loads = _fetch((cos_hbm, sin_hbm, q_hbm, ga_hbm), (cos_ref, sin_ref, q_ref, ga_ref), isem)
        outs, osems = ((o_ref, o_hbm), (mix_ref, mix_hbm)), (osem0, osem1)
        x, y, c, chips = _place()
        j = 2 * x + y
        sib = (x, y, 1 - c)
        idx = [2 * cx + cy for cx, cy in chips]
        rc = functools.partial(_remote, ssem, rsem)
        wo4_ref[j] = wo_ref[...].astype(BF16)
        cw4_ref[j] = cw_ref[...]
        sends = []
        for k, chip in enumerate(chips):
            sends.append(rc(k, wo4_ref.at[j, c], wo4_ref.at[j, c], (*chip, c)))
            sends.append(rc(6 + k, cw4_ref.at[j], cw4_ref.at[j], (*chip, c)))
        for cp in sends:
            cp.start()

        loads[0].wait()
        loads[1].wait()
        _prep_kv(kv_ref, kw_ref, cos_ref, sin_ref, ka_ref, va_ref, t)
        loads[2].wait()
        loads[3].wait()

        def blk(n, carry):
            r0 = pl.multiple_of(n * QBLK, QBLK)
            left = _lane((QBLK, 128)) < 64
            first = (_lane((QBLK, 128)) % 64) < 32
            cos = cos_ref[pl.ds(r0, QBLK), :]
            sin = sin_ref[pl.ds(r0, QBLK), :]
            mask = _band_mask(n)
            scores = []
            for p in range(4):
                lanes = slice(p * 128, (p + 1) * 128)
                qr, _, _ = _norm_rope(q_ref[pl.ds(r0, QBLK), lanes], qw_ref[...], cos, sin, left, first)
                q2 = _stack_heads(qr * 0.125, left).astype(BF16)
                scores.append(lax.dot_general(q2, ka_ref[p // 2, pl.ds(r0, 2 * QBLK), :], (((1,), (1,)), ((), ())),
                                              preferred_element_type=F32))
            probs = [_softmax_pair(scores[p], mask, sk_ref[0, 2 * p], sk_ref[0, 2 * p + 1])[0].astype(BF16)
                     for p in range(4)]
            for p in range(4):
                lanes = slice(p * 128, (p + 1) * 128)
                o2 = jnp.dot(probs[p], va_ref[p // 2, pl.ds(r0, 2 * QBLK), :], preferred_element_type=F32)
                o = jnp.where(left, o2[0:QBLK], o2[QBLK:2 * QBLK])
                o_ref[pl.ds(r0, QBLK), lanes] = o.astype(BF16)
                mix_ref[pl.ds(r0, QBLK), lanes] = (o * _silu(ga_ref[pl.ds(r0, QBLK), lanes])).astype(BF16)

            @pl.when(n % per_put == per_put - 1)
            def _():
                _put_all(outs, osems, n // per_put)

            return carry

        lax.fori_loop(0, nblk, blk, 0)
        _put_wait(outs, osems, t // PUT_ROWS)

        passed = []
        for k, chip in enumerate(chips):
            jk = idx[k]
            rc(k, wo4_ref.at[jk, c], wo4_ref.at[jk, c], sib).wait_recv()
            passed.append(rc(3 + k, wo4_ref.at[jk, c], wo4_ref.at[jk, c], sib))
            passed[-1].start()
        for k, chip in enumerate(chips):
            jk = idx[k]
            rc(3 + k, wo4_ref.at[jk, 1 - c], wo4_ref.at[jk, 1 - c], sib).wait_recv()
            rc(6 + k, cw4_ref.at[jk], cw4_ref.at[jk], sib).wait_recv()
        for cp in sends + passed:
            cp.wait_send()

    vm = pl.BlockSpec(memory_space=pltpu.VMEM)
    hbm = pl.BlockSpec(memory_space=pl.ANY)
    n_sem = 9
    return pl.pallas_call(
        body,
        name="attn_fwd",
        in_specs=[hbm, vm, hbm, vm, vm, pl.BlockSpec(memory_space=pltpu.SMEM), hbm, hbm, vm, vm],
        out_specs=[hbm, hbm, vm, vm],
        out_shape=[jax.ShapeDtypeStruct((t, ATTN_W), BF16), jax.ShapeDtypeStruct((t, ATTN_W), BF16),
                   jax.ShapeDtypeStruct((N_CHIPS, 2, OUT_HALF, D_MODEL), BF16),
                   jax.ShapeDtypeStruct((N_CHIPS, 32, 128), F32)],
        scratch_shapes=[pltpu.VMEM((2, t + QBLK, 128), BF16), pltpu.VMEM((2, t + QBLK, 128), BF16),
                        pltpu.VMEM((t, ATTN_W), F32), pltpu.VMEM((t, ATTN_W), F32),
                        pltpu.VMEM((t, ATTN_W), BF16), pltpu.VMEM((t, ATTN_W), BF16),
                        pltpu.VMEM((t, 128), F32), pltpu.VMEM((t, 128), F32),
                        pltpu.SemaphoreType.DMA((4,)), pltpu.SemaphoreType.DMA((t // PUT_ROWS,)),
                        pltpu.SemaphoreType.DMA((t // PUT_ROWS,)),
                        pltpu.SemaphoreType.DMA((n_sem,)), pltpu.SemaphoreType.DMA((n_sem,))],
        compiler_params=_cparams(),
    )(q_raw, kv_raw, ga, qw2, kw2, sinks, cos_f, sin_s, wo, cw)


def _attn_bwd(q_raw, kv_raw, ga, o, dmix, qw2, kw2, sinks, cos_f, sin_s, go):
    t = q_raw.shape[0]
    nblk = t // QBLK
    per_put = PUT_ROWS // QBLK

    def body(q_hbm, kv_ref, ga_hbm, o_hbm, dm_hbm, qw_ref, kw_ref, sk_ref, cos_hbm, sin_hbm, go_ref,
             dq_hbm, dkv_ref, dga_hbm, sm_ref, gwo_ref, ka_ref, va_ref, dka_ref, dva_ref,
             sibo_ref, outo_ref, ino_ref, q_ref, ga_ref, o_ref, dm_ref, dq_ref, dga_ref, cos_ref, sin_ref,
             isem, osem0, osem1, ssem, rsem):
        loads = _fetch((cos_hbm, sin_hbm, q_hbm, ga_hbm, o_hbm, dm_hbm), (cos_ref, sin_ref, q_ref, ga_ref, o_ref, dm_ref), isem)
        outs, osems = ((dq_ref, dq_hbm), (dga_ref, dga_hbm)), (osem0, osem1)
        x, y, c, chips = _place()
        sib = (x, y, 1 - c)
        rc = functools.partial(_remote, ssem, rsem)
        theirs, mine = go_ref.at[:, 1 - c], go_ref.at[:, c]
        sends = [_rs_to_sibling(rc, 0, theirs, sibo_ref, sib)]
        loads[0].wait()
        loads[1].wait()
        _prep_kv(kv_ref, kw_ref, cos_ref, sin_ref, ka_ref, va_ref, t)
        dka_ref[...] = jnp.zeros_like(dka_ref)
        dva_ref[...] = jnp.zeros_like(dva_ref)
        sends += _rs_trade(rc, 0, theirs, mine, sibo_ref, outo_ref, ino_ref, OUT_HALF, c, sib, chips)
        for cp in loads[2:]:
            cp.wait()

        def blk(n, carry):
            dqw, dsk = carry
            r0 = pl.multiple_of(n * QBLK, QBLK)
            left = _lane((QBLK, 128)) < 64
            first = (_lane((QBLK, 128)) % 64) < 32
            cos = cos_ref[pl.ds(r0, QBLK), :]
            sin = sin_ref[pl.ds(r0, QBLK), :]
            mask = _band_mask(n)
            row = lax.broadcasted_iota(jnp.int32, (2 * QBLK, 1), 0)
            rows = pl.ds(r0, QBLK)
            win = pl.ds(r0, 2 * QBLK)
            lane_of = [slice(p * 128, (p + 1) * 128) for p in range(4)]
            for grp in ((0, 1), (2, 3)):
                qn = {p: _norm_rope(q_ref[rows, lane_of[p]], qw_ref[...], cos, sin, left, first) for p in grp}
                q2 = {p: _stack_heads(qn[p][0] * 0.125, left).astype(BF16) for p in grp}
                sc = {p: lax.dot_general(q2[p], ka_ref[p // 2, win, :], (((1,), (1,)), ((), ())),
                                         preferred_element_type=F32) for p in grp}
                do2 = {}
                for p in grp:
                    gav = ga_ref[rows, lane_of[p]]
                    dmv = dm_ref[rows, lane_of[p]].astype(F32)
                    dga_ref[rows, lane_of[p]] = (dmv * o_ref[rows, lane_of[p]].astype(F32) * _dsilu(gav)).astype(BF16)
                    do2[p] = _stack_heads(dmv * _silu(gav), left).astype(BF16)
                dpm = {p: lax.dot_general(do2[p], va_ref[p // 2, win, :], (((1,), (1,)), ((), ())),
                                          preferred_element_type=F32) for p in grp}
                sm = {p: _softmax_pair(sc[p], mask, sk_ref[0, 2 * p], sk_ref[0, 2 * p + 1]) for p in grp}
                dsl = {}
                for p in grp:
                    pm, ps = sm[p]
                    delta = jnp.sum(pm * dpm[p], axis=-1, keepdims=True)
                    dsl[p] = (pm * (dpm[p] - delta)).astype(BF16)
                    pd = ps * delta
                    d0 = jnp.sum(jnp.where(row < QBLK, pd, 0.0), axis=0, keepdims=True)
                    d1 = jnp.sum(jnp.where(row < QBLK, 0.0, pd), axis=0, keepdims=True)
                    l8 = _lane((1, 128))
                    dsk = dsk - jnp.where(l8 == 2 * p, d0, 0.0) - jnp.where(l8 == 2 * p + 1, d1, 0.0)
                for p in grp:
                    g = p // 2
                    dva_ref[g, win, :] += lax.dot_general(sm[p][0].astype(BF16), do2[p], (((0,), (0,)), ((), ())),
                                                          preferred_element_type=F32)
                    dka_ref[g, win, :] += lax.dot_general(dsl[p], q2[p], (((0,), (0,)), ((), ())),
                                                          preferred_element_type=F32)
                for p in grp:
                    dq2 = jnp.dot(dsl[p], ka_ref[p // 2, win, :], preferred_element_type=F32)
                    dqr = jnp.where(left, dq2[0:QBLK], dq2[QBLK:2 * QBLK]) * 0.125
                    dq, dw = _norm_rope_bwd(dqr, qn[p][1], qn[p][2], qw_ref[...], cos, sin, left, first)
                    dq_ref[rows, lane_of[p]] = dq.astype(BF16)
                    dqw = dqw + dw

            @pl.when(n % per_put == per_put - 1)
            def _():
                _put_all(outs, osems, n // per_put)

            return dqw, dsk

        zero = jnp.zeros((1, 128), F32)
        dqw, dsk = lax.fori_loop(0, nblk, blk, (zero, zero))

        ch = 256

        def chunk(i, dkw):
            r0 = pl.multiple_of(i * ch, ch)
            left = _lane((ch, 128)) < 64
            first = (_lane((ch, 128)) % 64) < 32
            rows = pl.ds(r0, ch)
            prow = pl.ds(QBLK + r0, ch)

            def fold(ref):
                a0 = ref[0, prow, :]
                a1 = ref[1, prow, :]
                return jnp.where(left, a0 + pltpu.roll(a0, 64, 1), a1 + pltpu.roll(a1, 64, 1))

            cos = cos_ref[rows, :]
            sin = sin_ref[rows, :]
            _, xh, r = _norm_rope(kv_ref[rows, 0:128], kw_ref[...], cos, sin, left, first)
            dk, dw = _norm_rope_bwd(fold(dka_ref), xh, r, kw_ref[...], cos, sin, left, first)
            dkv_ref[rows, 0:128] = dk.astype(BF16)
            dkv_ref[rows, 128:256] = fold(dva_ref).astype(BF16)
            return dkw + dw

        dkw = lax.fori_loop(0, t // ch, chunk, zero)
        sm_ref[...] = jnp.zeros((8, 128), F32)
        sm_ref[0:1, :] = dqw + pltpu.roll(dqw, 64, 1)
        sm_ref[1:2, :] = dkw + pltpu.roll(dkw, 64, 1)
        sm_ref[2:3, :] = dsk

        j = 2 * x + y
        sends.append(_rs_total(rc, 0, mine, sibo_ref, outo_ref, ino_ref, gwo_ref, OUT_HALF, j, c, sib))
        _rs_done(rc, 0, gwo_ref, c, sib)
        for cp in sends:
            cp.wait_send()
        _put_wait(outs, osems, t // PUT_ROWS)

    vm = pl.BlockSpec(memory_space=pltpu.VMEM)
    hbm = pl.BlockSpec(memory_space=pl.ANY)
    return pl.pallas_call(
        body,
        name="attn_bwd",
        in_specs=[hbm, vm, hbm, hbm, hbm, vm, vm, pl.BlockSpec(memory_space=pltpu.SMEM), hbm, hbm, vm],
        out_specs=[hbm, vm, hbm, vm, vm],
        out_shape=[jax.ShapeDtypeStruct((t, ATTN_W), BF16), jax.ShapeDtypeStruct((t, 2 * KV_W), BF16),
                   jax.ShapeDtypeStruct((t, ATTN_W), BF16), jax.ShapeDtypeStruct((8, 128), F32),
                   jax.ShapeDtypeStruct((2, OUT_HALF, D_MODEL), F32)],
        scratch_shapes=[pltpu.VMEM((2, t + QBLK, 128), BF16), pltpu.VMEM((2, t + QBLK, 128), BF16),
                        pltpu.VMEM((2, t + QBLK, 128), F32), pltpu.VMEM((2, t + QBLK, 128), F32)]
        + _rs_scratch(OUT_HALF)
        + [pltpu.VMEM((t, ATTN_W), F32), pltpu.VMEM((t, ATTN_W), F32), pltpu.VMEM((t, ATTN_W), BF16),
           pltpu.VMEM((t, ATTN_W), BF16), pltpu.VMEM((t, ATTN_W), BF16), pltpu.VMEM((t, ATTN_W), BF16),
           pltpu.VMEM((t, 128), F32), pltpu.VMEM((t, 128), F32),
           pltpu.SemaphoreType.DMA((6,)), pltpu.SemaphoreType.DMA((t // PUT_ROWS,)), pltpu.SemaphoreType.DMA((t // PUT_ROWS,)),
           pltpu.SemaphoreType.DMA((RS_SEMS,)), pltpu.SemaphoreType.DMA((RS_SEMS,))],
        compiler_params=_cparams(),
    )(q_raw, kv_raw, ga, o, dmix, qw2, kw2, sinks, cos_f, sin_s, go)


CONV_CH = 256
CONV_SUB = 128
CONV_ACCS = 1


def _shifted_windows(src_ref, r0, sh_ref):
    rows = CONV_CH + CONV_PAD
    win = src_ref[pl.ds(r0, rows), :]
    for b in range(8):
        sh = win if b == 0 else pltpu.roll(win, rows - b, 0)
        for c in range(CONV_W // 128):
            sh_ref[b, c] = sh[:, c * 128:(c + 1) * 128]


def _conv_fwd(ua, ug, gb, cw, cb, lw, lb):
    t = ua.shape[0]

    def body(ua_hbm, ug_hbm, gb_hbm, cw_ref, cb_ref, lw_ref, lb_ref, cz_hbm, mix_hbm, zp_ref, sh_ref,
             ua_ref, ug_ref, gb_ref, cz_ref, mix_ref, isem, osem0, osem1):
        loads = _fetch((ua_hbm, ug_hbm, gb_hbm), (ua_ref, ug_ref, gb_ref), isem)
        outs, osems = ((cz_ref, cz_hbm), (mix_ref, mix_hbm)), (osem0, osem1)
        per_put = PUT_ROWS // CONV_CH
        zp_ref[0:CONV_PAD, :] = jnp.zeros((CONV_PAD, CONV_W), F32)
        loads[0].wait()
        loads[1].wait()

        def glu(i, carry):
            r0 = pl.multiple_of(i * CONV_CH, CONV_CH)
            rows = pl.ds(r0, CONV_CH)
            zp_ref[pl.ds(CONV_PAD + r0, CONV_CH), :] = ua_ref[rows, :] * _sigmoid(ug_ref[rows, :])
            return carry

        lax.fori_loop(0, t // CONV_CH, glu, 0)
        loads[2].wait()

        def chunk(i, carry):
            r0 = pl.multiple_of(i * CONV_CH, CONV_CH)
            _shifted_windows(zp_ref, r0, sh_ref)
            for c in range(CONV_W // 128):
                lanes = slice(c * 128, (c + 1) * 128)

                def sub(k, carry2):
                    b0 = pl.multiple_of(k * CONV_SUB, CONV_SUB)
                    acc = [jnp.broadcast_to(cb_ref[0:1, lanes], (CONV_SUB, 128))] + [None] * (CONV_ACCS - 1)
                    for j in range(CONV_TAPS):
                        off = j + CONV_PAD - (CONV_TAPS - 1)
                        term = sh_ref[off % 8, c, pl.ds(b0 + 8 * (off // 8), CONV_SUB), :] * cw_ref[j:j + 1, lanes]
                        acc[j % CONV_ACCS] = term if acc[j % CONV_ACCS] is None else acc[j % CONV_ACCS] + term
                    cz_ref[pl.ds(r0 + b0, CONV_SUB), lanes] = functools.reduce(lambda a, b: a + b, acc)
                    return carry2

                lax.fori_loop(0, CONV_CH // CONV_SUB, sub, 0)
            rows = pl.ds(r0, CONV_CH)
            cz = cz_ref[rows, :]
            mu = jnp.mean(cz, axis=-1, keepdims=True)
            xc = cz - mu
            rs = lax.rsqrt(jnp.mean(xc * xc, axis=-1, keepdims=True) + EPS)
            ln = xc * rs * lw_ref[...] + lb_ref[...]
            mix_ref[rows, :] = (_silu(ln) * _silu(gb_ref[rows, :])).astype(BF16)

            @pl.when(i % per_put == per_put - 1)
            def _():
                _put_all(outs, osems, i // per_put)

            return carry

        lax.fori_loop(0, t // CONV_CH, chunk, 0)
        _put_wait(outs, osems, t // PUT_ROWS)

    vm = pl.BlockSpec(memory_space=pltpu.VMEM)
    hbm = pl.BlockSpec(memory_space=pl.ANY)
    nput = t // PUT_ROWS
    return pl.pallas_call(
        body,
        name="conv_fwd",
        in_specs=[hbm] * 3 + [vm] * 4,
        out_specs=[hbm, hbm],
        out_shape=[jax.ShapeDtypeStruct((t, CONV_W), F32), jax.ShapeDtypeStruct((t, CONV_W), BF16)],
        scratch_shapes=[pltpu.VMEM((t + CONV_PAD, CONV_W), F32),
                        pltpu.VMEM((8, CONV_W // 128, CONV_CH + CONV_PAD, 128), F32),
                        pltpu.VMEM((t, CONV_W), F32), pltpu.VMEM((t, CONV_W), F32), pltpu.VMEM((t, CONV_W), F32),
                        pltpu.VMEM((t, CONV_W), F32), pltpu.VMEM((t, CONV_W), BF16),
                        pltpu.SemaphoreType.DMA((3,)), pltpu.SemaphoreType.DMA((nput,)), pltpu.SemaphoreType.DMA((nput,))],
        compiler_params=_cparams(),
    )(ua, ug, gb, cw, cb, lw, lb)


def _conv_bwd(ua, ug, gb, cz, dmix, cw, lw, lb):
    t = ua.shape[0]

    def body(ua_hbm, ug_hbm, gb_hbm, cz_hbm, dm_hbm, cw_ref, lw_ref, lb_ref,
             dua_hbm, dug_hbm, dgb_hbm, dcw_ref, dvec_ref, zp_ref, dp_ref, sh_ref, wacc_ref,
             ua_ref, ug_ref, gb_ref, cz_ref, dm_ref, dua_ref, dug_ref, dgb_ref, isem, osem0, osem1, osem2):
        loads = _fetch((ua_hbm, ug_hbm, gb_hbm, cz_hbm, dm_hbm), (ua_ref, ug_ref, gb_ref, cz_ref, dm_ref), isem)
        per_put = PUT_ROWS // CONV_CH
        zp_ref[0:CONV_PAD, :] = jnp.zeros((CONV_PAD, CONV_W), F32)
        dp_ref[t:t + CONV_PAD, :] = jnp.zeros((CONV_PAD, CONV_W), F32)
        wacc_ref[...] = jnp.zeros_like(wacc_ref)
        for cp in loads:
            cp.wait()

        def pointwise(i, carry):
            dcb, dlw, dlb = carry
            r0 = pl.multiple_of(i * CONV_CH, CONV_CH)
            rows = pl.ds(r0, CONV_CH)
            zp_ref[pl.ds(CONV_PAD + r0, CONV_CH), :] = ua_ref[rows, :] * _sigmoid(ug_ref[rows, :])
            cz = cz_ref[rows, :]
            mu = jnp.mean(cz, axis=-1, keepdims=True)
            xc = cz - mu
            rs = lax.rsqrt(jnp.mean(xc * xc, axis=-1, keepdims=True) + EPS)
            xh = xc * rs
            ln = xh * lw_ref[...] + lb_ref[...]
            gbv = gb_ref[rows, :]
            dy = dm_ref[rows, :].astype(F32)
            dgb_ref[rows, :] = (dy * _silu(ln) * _dsilu(gbv)).astype(BF16)
            dl = dy * _silu(gbv) * _dsilu(ln)
            dxh = dl * lw_ref[...]
            dcz = rs * (dxh - jnp.mean(dxh, axis=-1, keepdims=True)
                        - xh * jnp.mean(dxh * xh, axis=-1, keepdims=True))
            dp_ref[rows, :] = dcz

            @pl.when(i % per_put == per_put - 1)
            def _():
                _put(dgb_ref, dgb_hbm, osem2, i // per_put).start()

            return (dcb + jnp.sum(dcz, axis=0, keepdims=True),
                    dlw + jnp.sum(dl * xh, axis=0, keepdims=True),
                    dlb + jnp.sum(dl, axis=0, keepdims=True))

        zero = jnp.zeros((1, CONV_W), F32)
        dcb, dlw, dlb = lax.fori_loop(0, t // CONV_CH, pointwise, (zero, zero, zero))
        dvec_ref[...] = jnp.zeros((8, CONV_W), F32)
        dvec_ref[0:1, :] = dcb
        dvec_ref[1:2, :] = dlw
        dvec_ref[2:3, :] = dlb

        def chunk(i, carry):
            r0 = pl.multiple_of(i * CONV_CH, CONV_CH)
            _shifted_windows(dp_ref, r0, sh_ref)
            for c in range(CONV_W // 128):
                lanes = slice(c * 128, (c + 1) * 128)

                def sub(k, carry2):
                    b0 = pl.multiple_of(k * CONV_SUB, CONV_SUB)
                    acc = [None] * CONV_ACCS
                    for j in range(CONV_TAPS):
                        off = CONV_TAPS - 1 - j
                        term = sh_ref[off % 8, c, pl.ds(b0 + 8 * (off // 8), CONV_SUB), :] * cw_ref[j:j + 1, lanes]
                        acc[j % CONV_ACCS] = term if acc[j % CONV_ACCS] is None else acc[j % CONV_ACCS] + term
                    acc = functools.reduce(lambda a, b: a + b, acc)
                    rr = pl.ds(r0 + b0, CONV_SUB)
                    sg = _sigmoid(ug_ref[rr, lanes])
                    dua_ref[rr, lanes] = (acc * sg).astype(BF16)
                    dug_ref[rr, lanes] = (acc * ua_ref[rr, lanes] * sg * (1.0 - sg)).astype(BF16)
                    return carry2

                lax.fori_loop(0, CONV_CH // CONV_SUB, sub, 0)
            _shifted_windows(zp_ref, r0, sh_ref)
            for c in range(CONV_W // 128):
                lanes = slice(c * 128, (c + 1) * 128)

                def subw(k, carry2):
                    b0 = pl.multiple_of(k * CONV_SUB, CONV_SUB)
                    dcz = dp_ref[pl.ds(r0 + b0, CONV_SUB), lanes]
                    for j in range(CONV_TAPS):
                        off = j + CONV_PAD - (CONV_TAPS - 1)
                        pr = dcz * sh_ref[off % 8, c, pl.ds(b0 + 8 * (off // 8), CONV_SUB), :]
                        parts = [pr[8 * q:8 * (q + 1)] for q in range(CONV_SUB // 8)]
                        while len(parts) > 1:
                            parts = [a + b for a, b in zip(parts[0::2], parts[1::2])]
                        wacc_ref[8 * j:8 * (j + 1), lanes] += parts[0]
                    return carry2

                lax.fori_loop(0, CONV_CH // CONV_SUB, subw, 0)

            @pl.when(i % per_put == per_put - 1)
            def _():
                _put_all(((dua_ref, dua_hbm), (dug_ref, dug_hbm)), (osem0, osem1), i // per_put)

            return carry

        lax.fori_loop(0, t // CONV_CH, chunk, 0)
        _put_wait(((dua_ref, dua_hbm), (dug_ref, dug_hbm), (dgb_ref, dgb_hbm)), (osem0, osem1, osem2), t // PUT_ROWS)
        dcw_ref[...] = jnp.zeros((16, 2 * CONV_W), F32)
        for j in range(CONV_TAPS):
            dcw_ref[j // 2:j // 2 + 1, CONV_W * (j % 2):CONV_W * (j % 2 + 1)] = jnp.sum(
                wacc_ref[8 * j:8 * (j + 1), :], axis=0, keepdims=True)

    vm = pl.BlockSpec(memory_space=pltpu.VMEM)
    hbm = pl.BlockSpec(memory_space=pl.ANY)
    return pl.pallas_call(
        body,
        name="conv_bwd",
        in_specs=[hbm] * 5 + [vm] * 3,
        out_specs=[hbm] * 3 + [vm] * 2,
        out_shape=[jax.ShapeDtypeStruct((t, CONV_W), BF16)] * 3
        + [jax.ShapeDtypeStruct((16, 2 * CONV_W), F32), jax.ShapeDtypeStruct((8, CONV_W), F32)],
        scratch_shapes=[pltpu.VMEM((t + CONV_PAD, CONV_W), F32), pltpu.VMEM((t + CONV_PAD, CONV_W), F32),
                        pltpu.VMEM((8, CONV_W // 128, CONV_CH + CONV_PAD, 128), F32), pltpu.VMEM((8 * 32, CONV_W), F32)]
        + [pltpu.VMEM((t, CONV_W), F32)] * 4 + [pltpu.VMEM((t, CONV_W), BF16)] * 4
        + [pltpu.SemaphoreType.DMA((5,))] + [pltpu.SemaphoreType.DMA((t // PUT_ROWS,))] * 3,
        compiler_params=_cparams(),
    )(ua, ug, gb, cz, dmix, cw, lw, lb)


def _out_proj(mix_a, mix_b, x, tgt, gate, w_out):
    t = x.shape[0]
    tm = 512
    nstep = t // tm

    def body(ma_ref, mb_ref, x_ref, t_ref, g_ref, w_ref, dout_ref, dma_ref, dmb_ref, gw_ref, red_ref, acc_ref):
        i = pl.program_id(0)

        @pl.when(i == 0)
        def _():
            acc_ref[...] = jnp.zeros_like(acc_ref)
            red_ref[...] = jnp.zeros_like(red_ref)

        mix = jnp.concatenate([ma_ref[...], mb_ref[...]], axis=1)
        y = jnp.dot(mix, w_ref[...], preferred_element_type=F32)
        gate_v = g_ref[...]
        err = x_ref[...] + gate_v * y - t_ref[...]
        dout = err * (1.0 / D_MODEL)
        dout_ref[...] = dout
        red_ref[0:1, :] += jnp.sum(dout * y, axis=0, keepdims=True)
        red_ref[1:2, :] += jnp.sum(err * err, axis=0, keepdims=True)
        dy = (dout * gate_v).astype(BF16)
        dmix = lax.dot_general(dy, w_ref[...], (((1,), (1,)), ((), ())), preferred_element_type=F32)
        dma_ref[...] = dmix[:, 0:512].astype(BF16)
        dmb_ref[...] = dmix[:, 512:1024].astype(BF16)
        acc_ref[...] += lax.dot_general(mix, dy, (((0,), (0,)), ((), ())), preferred_element_type=F32)

        @pl.when(i == nstep - 1)
        def _():
            gw_ref[...] = acc_ref[...].astype(BF16)

    row = lambda w: pl.BlockSpec((tm, w), lambda i: (i, 0))
    const = lambda s: pl.BlockSpec(s, lambda i: (0, 0))
    return pl.pallas_call(
        body,
        name="out_proj",
        grid=(nstep,),
        in_specs=[row(512), row(512), row(D_MODEL), row(D_MODEL), const((1, D_MODEL)),
                  pl.BlockSpec((D_MODEL, D_MODEL), lambda i: (0, 0), pipeline_mode=pl.Buffered(1))],
        out_specs=[row(D_MODEL), row(512), row(512), const((D_MODEL, D_MODEL)), const((8, D_MODEL))],
        out_shape=[jax.ShapeDtypeStruct((t, D_MODEL), F32), jax.ShapeDtypeStruct((t, 512), BF16),
                   jax.ShapeDtypeStruct((t, 512), BF16), jax.ShapeDtypeStruct((D_MODEL, D_MODEL), BF16),
                   jax.ShapeDtypeStruct((8, D_MODEL), F32)],
        scratch_shapes=[pltpu.VMEM((D_MODEL, D_MODEL), F32)],
        compiler_params=_cparams(dimension_semantics=("arbitrary",)),
    )(mix_a, mix_b, x, tgt, gate, w_out)


DPROJ_WIDTHS = (512, 256, 512, 512, 512, 512)
DPROJ_STARTS = (0, 512, 768, 1280, 1792, 2304)
WIN_W = 768
WIN_START = (0, 640, 1408, 2048)
WIN_OFF = (0, 64, 0, 64)
N_GW = N_CHIPS


def _window_pieces(s):
    lo, hi = WIN_START[s], WIN_START[s] + WIN_W
    out = []
    for p, (st, w) in enumerate(zip(DPROJ_STARTS, DPROJ_WIDTHS)):
        a, b = max(lo, st), min(hi, st + w)
        if a < b:
            out.append((p, a - st, b - a, a - lo))
    return out


def _in_proj_bwd(dparts, h, x, dout, s1, nw, wt_full, dcw, dvec, sm_a, row0):
    t = x.shape[0]
    tm = 256
    nstep = N_GW + t // tm
    n_sem = 20
    rows0 = 32
    hs = rows0 // 2
    npart = len(DPROJ_WIDTHS)

    def body(*refs):
        d_hbm, d_ref = refs[:npart], refs[npart:2 * npart]
        (x_ref, dout_ref, s1_ref, nw_ref, h_ref, wt_hbm, dcw_ref, dvec_ref, sma_ref, row0_ref,
         gx_ref, gw_hbm, ssum_ref, rows_ref,
         stg_ref, wt_ref, gt_ref, sib_ref, out_ref, in_ref, res_ref, sall_ref, red_ref, sm0_ref, ssib_ref, schip_ref, sres_ref,
         wsem, lsem, ssem, rsem) = refs[2 * npart:]
        i = pl.program_id(0)
        x_, y_, c, chips = _place()
        j = 2 * x_ + y_
        dev = 2 * j + c
        sib = (x_, y_, 1 - c)
        rc = functools.partial(_remote, ssem, rsem)
        rel_chip = [2 * cx + cy for cx, cy in chips] + [j]
        peers = [(px, py, pc) for px in (x_, 1 - x_) for py in (y_, 1 - y_) for pc in (c, 1 - c)][1:]
        wt_copy = pltpu.make_async_copy(wt_hbm, wt_ref, lsem.at[0])

        def window(case, slot):
            return [pltpu.make_async_copy(d_hbm[p].at[:, pl.ds(c0, w)], stg_ref.at[slot, :, pl.ds(w0, w)], wsem.at[slot, n])
                    for n, (p, c0, w, w0) in enumerate(_window_pieces(case))]

        def to_sibling(k):
            return rc(k, gt_ref.at[k, 1 - c], sib_ref.at[k], sib)

        def to_chip(k):
            return rc(4 + k, out_ref.at[k], in_ref.at[k], (*chips[k], c))

        def trade(k):
            to_sibling(k).wait_recv()

            def add(n, carry):
                rr = pl.ds(pl.multiple_of(n * RS_CH, RS_CH), RS_CH)
                out_ref[k, rr, :] = (gt_ref[k, c, rr, :].astype(F32) + sib_ref[k, rr, :].astype(F32)).astype(BF16)
                return carry

            lax.fori_loop(0, IN_HALF // RS_CH, add, 0)
            to_chip(k).start()

        mine_s = pl.ds(pl.multiple_of(c * hs, 8), hs)
        other_s = pl.ds(pl.multiple_of((1 - c) * hs, 8), hs)

        def small_to_sibling():
            return rc(15, sm0_ref.at[other_s], ssib_ref, sib)

        def small_to_chip(k):
            return rc(16 + k, schip_ref.at[j], schip_ref.at[j], (*chips[k], c))

        def small_share():
            return rc(19, sres_ref.at[c], sres_ref.at[c], sib)

        for k in range(N_GW):
            @pl.when(i == k)
            def _(k=k):
                slot = k % 2
                if k == 0:
                    red_ref[...] = jnp.zeros_like(red_ref)
                    wt_copy.start()
                    sm0_ref[...] = jnp.zeros_like(sm0_ref)
                    sm0_ref[0:16, :] = dcw_ref[...]
                    sm0_ref[16:17, 0:CONV_W] = dvec_ref[0:1, :]
                    sm0_ref[16:17, CONV_W:2 * CONV_W] = dvec_ref[1:2, :]
                    sm0_ref[17:18, 0:CONV_W] = dvec_ref[2:3, :]
                    for r in range(3):
                        sm0_ref[17:18, CONV_W + 128 * r:CONV_W + 128 * (r + 1)] = sma_ref[r:r + 1, :]
                    sm0_ref[18:19, :] = row0_ref[1:2, :]
                    small_to_sibling().start()
                if k == 1:
                    small_to_sibling().wait_recv()
                    schip_ref[j] = sm0_ref[mine_s, :] + ssib_ref[...]
                    for kk in range(3):
                        small_to_chip(kk).start()
                if k == N_GW - 1:
                    for kk in range(3):
                        jk = rel_chip[kk]
                        rc(16 + kk, schip_ref.at[jk], schip_ref.at[jk], sib).wait_recv()
                    tot = schip_ref[0]
                    for d in range(1, N_CHIPS):
                        tot = tot + schip_ref[d]
                    sres_ref[c] = tot
                    small_share().start()
                for case in range(N_CHIPS):
                    if k == 0:
                        @pl.when(rel_chip[0] == case)
                        def _():
                            for cp in window(case, 0):
                                cp.start()
                    if k + 1 < N_GW:
                        @pl.when(rel_chip[k + 1] == case)
                        def _():
                            for cp in window(case, 1 - slot):
                                cp.start()
                for case in range(N_CHIPS):
                    @pl.when(rel_chip[k] == case)
                    def _():
                        for cp in window(case, slot):
                            cp.wait()
                g = lax.dot_general(stg_ref[slot], h_ref[...], (((0,), (0,)), ((), ())), preferred_element_type=F32)
                for off in sorted(set(WIN_OFF)):
                    @pl.when(rel_chip[k] % 2 == (1 if off else 0))
                    def _():
                        gt_ref[k, 0] = g[off:off + IN_HALF].astype(BF16)
                        gt_ref[k, 1] = g[off + IN_HALF:off + 2 * IN_HALF].astype(BF16)
                to_sibling(k).start()
                if k >= 1:
                    trade(k - 1)

        @pl.when(i == N_GW)
        def _():
            wt_copy.wait()

        @pl.when(i >= N_GW)
        def _():
            xv = x_ref[...]
            r = lax.rsqrt(jnp.mean(xv * xv, axis=-1, keepdims=True) + EPS)
            xh = xv * r
            n = xh * nw_ref[...]
            dproj = jnp.concatenate([ref[...] for ref in d_ref], axis=1)
            dh = jnp.dot(dproj, wt_ref[...], preferred_element_type=F32)
            red_ref[0:1, :] += jnp.sum(dh, axis=0, keepdims=True)
            red_ref[1:2, :] += jnp.sum(dh * n, axis=0, keepdims=True)
            dn = dh * s1_ref[...]
            red_ref[2:3, :] += jnp.sum(dn * xh, axis=0, keepdims=True)
            dxh = dn * nw_ref[...]
            gx_ref[...] = dout_ref[...] + r * (dxh - xh * jnp.mean(dxh * xh, axis=-1, keepdims=True))

        @pl.when(i == nstep - 1)
        def _():
            sall_ref[dev] = row0_ref[...]
            sall_ref[dev, 2:5, :] = red_ref[0:3, :]
            sends = [rc(8 + k, sall_ref.at[dev], sall_ref.at[dev], peer) for k, peer in enumerate(peers)]
            for cp in sends:
                cp.start()
            sends += [to_sibling(k) for k in range(N_GW)] + [to_chip(k) for k in range(3)]
            sends += [small_to_sibling(), small_share()] + [small_to_chip(k) for k in range(3)]
            own = N_GW - 1
            to_sibling(own).wait_recv()
            for k in range(3):
                to_chip(k).wait_recv()

            def total(n, carry):
                rr = pl.ds(pl.multiple_of(n * RS_CH, RS_CH), RS_CH)
                acc = gt_ref[own, c, rr, :].astype(F32) + sib_ref[own, rr, :].astype(F32)
                for k in range(3):
                    acc = acc + in_ref[k, rr, :].astype(F32)
                res_ref[c, rr, :] = acc
                return carry

            lax.fori_loop(0, IN_HALF // RS_CH, total, 0)
            share = rc(7, res_ref.at[c], res_ref.at[c], sib)
            share.start()
            sends.append(share)
            for k, (px, py, pc) in enumerate(peers):
                pdev = 4 * px + 2 * py + pc
                rc(8 + k, sall_ref.at[pdev], sall_ref.at[pdev], (px, py, pc)).wait_recv()
            rows_ref[...] = sall_ref[...]
            rc(19, sres_ref.at[1 - c], sres_ref.at[1 - c], sib).wait_recv()
            ssum_ref[0:hs, :] = sres_ref[0]
            ssum_ref[hs:rows0, :] = sres_ref[1]
            rc(7, res_ref.at[1 - c], res_ref.at[1 - c], sib).wait_recv()
            back = pltpu.make_async_copy(res_ref, gw_hbm, lsem.at[1])
            back.start()
            for cp in sends:
                cp.wait_send()
            back.wait()

    blk = lambda i: jnp.maximum(i - N_GW, 0)
    row = lambda w: pl.BlockSpec((tm, w), lambda i: (blk(i), 0))
    vec = pl.BlockSpec((1, D_MODEL), lambda i: (0, 0))
    const = lambda shape: pl.BlockSpec(shape, lambda i: (0,) * len(shape))
    hbm = pl.BlockSpec(memory_space=pl.ANY)
    return pl.pallas_call(
        body,
        name="in_proj_bwd",
        grid=(nstep,),
        in_specs=[hbm] * npart + [row(w) for w in DPROJ_WIDTHS] + [row(D_MODEL), row(D_MODEL), vec, vec,
                  pl.BlockSpec((t, D_MODEL), lambda i: (0, 0), pipeline_mode=pl.Buffered(1)), hbm, const((16, D_MODEL)),
                  const((8, CONV_W)), const((8, 128)), const((8, D_MODEL))],
        out_specs=[row(D_MODEL), hbm, const((rows0, D_MODEL)), const((N_DEV, 8, D_MODEL))],
        out_shape=[jax.ShapeDtypeStruct((t, D_MODEL), F32), jax.ShapeDtypeStruct((2, IN_HALF, D_MODEL), F32),
                   jax.ShapeDtypeStruct((rows0, D_MODEL), F32), jax.ShapeDtypeStruct((N_DEV, 8, D_MODEL), F32)],
        scratch_shapes=[pltpu.VMEM((2, t, WIN_W), BF16), pltpu.VMEM((IN_W, D_MODEL), BF16),
                        pltpu.VMEM((N_CHIPS, 2, IN_HALF, D_MODEL), BF16), pltpu.VMEM((N_CHIPS, IN_HALF, D_MODEL), BF16),
                        pltpu.VMEM((3, IN_HALF, D_MODEL), BF16), pltpu.VMEM((3, IN_HALF, D_MODEL), BF16),
                        pltpu.VMEM((2, IN_HALF, D_MODEL), F32), pltpu.VMEM((N_DEV, 8, D_MODEL), F32),
                        pltpu.VMEM((8, D_MODEL), F32), pltpu.VMEM((rows0, D_MODEL), F32), pltpu.VMEM((hs, D_MODEL), F32),
                        pltpu.VMEM((N_CHIPS, hs, D_MODEL), F32),
                        pltpu.VMEM((2, hs, D_MODEL), F32), pltpu.SemaphoreType.DMA((2, 3)), pltpu.SemaphoreType.DMA((2,)),
                        pltpu.SemaphoreType.DMA((n_sem,)), pltpu.SemaphoreType.DMA((n_sem,))],
        compiler_params=_cparams(dimension_semantics=("arbitrary",)),
    )(*dparts, *dparts, x, dout, s1, nw, h, wt_full, dcw, dvec, sm_a, row0)


MESH = pl.DeviceIdType.MESH


def _place():
    x, y, c = lax.axis_index("x"), lax.axis_index("y"), lax.axis_index("c")
    chips = [(1 - x, y), (x, 1 - y), (1 - x, 1 - y)]
    return x, y, c, chips


def _remote(sems_s, sems_r, k, src, dst, to):
    return pltpu.make_async_remote_copy(src_ref=src, dst_ref=dst, send_sem=sems_s.at[k], recv_sem=sems_r.at[k],
                                        device_id=to, device_id_type=MESH)


RS_CH = 32
RS_SEMS = 5


def _rs_to_sibling(rc, s0, theirs, sib_ref, sib):
    cp = rc(s0, theirs, sib_ref, sib)
    cp.start()
    return cp


def _rs_trade(rc, s0, theirs, mine, sib_ref, out_ref, in_ref, rows, c, sib, chips):
    rc(s0, theirs, sib_ref, sib).wait_recv()
    cps = []
    for k, (cx, cy) in enumerate(chips):
        jk = 2 * cx + cy

        def add(i, carry, jk=jk, k=k):
            rr = pl.ds(pl.multiple_of(i * RS_CH, RS_CH), RS_CH)
            out_ref[k, rr, :] = (mine[jk, rr, :].astype(F32) + sib_ref[jk, rr, :].astype(F32)).astype(BF16)
            return carry

        lax.fori_loop(0, rows // RS_CH, add, 0)
        cps.append(rc(s0 + 1 + k, out_ref.at[k], in_ref.at[k], (cx, cy, c)))
        cps[-1].start()
    return cps


def _rs_total(rc, s0, mine, sib_ref, out_ref, in_ref, res_ref, rows, j, c, sib):
    for k in range(3):
        rc(s0 + 1 + k, out_ref.at[k], in_ref.at[k], sib).wait_recv()

    def total(i, carry):
        rr = pl.ds(pl.multiple_of(i * RS_CH, RS_CH), RS_CH)
        acc = mine[j, rr, :].astype(F32) + sib_ref[j, rr, :].astype(F32)
        for k in range(3):
            acc = acc + in_ref[k, rr, :].astype(F32)
        res_ref[c, rr, :] = acc
        return carry

    lax.fori_loop(0, rows // RS_CH, total, 0)
    cp = rc(s0 + 4, res_ref.at[c], res_ref.at[c], sib)
    cp.start()
    return cp


def _rs_done(rc, s0, res_ref, c, sib):
    rc(s0 + 4, res_ref.at[1 - c], res_ref.at[1 - c], sib).wait_recv()


def _rs_scratch(rows):
    return [pltpu.VMEM((N_CHIPS, rows, D_MODEL), BF16), pltpu.VMEM((3, rows, D_MODEL), BF16),
            pltpu.VMEM((3, rows, D_MODEL), BF16)]


MAIN_W = 640
MAIN_DST = (((0, 0, 512), (1, 0, 128)), ((2, 0, 512), (3, 0, 128)), ((3, 128, 384), (4, 0, 256)), ((4, 384, 128), (5, 0, 512)))
PAIR_DST = ((1, 128, 128), (4, 256, 128))


def _in_proj_gather(x, wt, c_row, w_ada, b_ada, nw):
    t = x.shape[0]
    ch = 512
    n_sem = 18
    early = 128

    def body(x_hbm, wt_ref, c_ref, wada_ref, bada_ref, nw_ref,
             q_hbm, kv_hbm, ga_hbm, ua_hbm, ug_hbm, gb_hbm, h_hbm, w4_hbm, call_ref, ada_ref,
             x_ref, h_ref, w4_ref, stg_ref, pstg_ref, part_ref, lsem, osem, wsem, ssem, rsem):
        outs = (q_hbm, kv_hbm, ga_hbm, ua_hbm, ug_hbm, gb_hbm)
        x_, y_, c, chips = _place()
        j = 2 * x_ + y_
        dev = 2 * j + c
        sib = (x_, y_, 1 - c)
        idx = [2 * cx + cy for cx, cy in chips]
        rc = functools.partial(_remote, ssem, rsem)
        x_copy = pltpu.make_async_copy(x_hbm, x_ref, lsem.at[0])
        x_copy.start()

        def rows_of(s, cc):
            return pl.ds(pl.multiple_of(2 * IN_HALF * s + IN_HALF * cc, 16), IN_HALF)

        w4_ref[rows_of(j, 0), :] = wt_ref[0].astype(BF16)
        w4_ref[rows_of(j, 1), :] = wt_ref[1].astype(BF16)
        call_ref[dev] = c_ref[...]
        sends = []
        peers = [(px, py, pc) for px in (x_, 1 - x_) for py in (y_, 1 - y_) for pc in (c, 1 - c)][1:]
        for k, peer in enumerate(peers):
            sends.append(rc(k, call_ref.at[dev], call_ref.at[dev], peer))
        for cp in sends:
            cp.start()

        def piece(s, lo, n):
            return pl.ds(pl.multiple_of(2 * IN_HALF * s + IN_HALF * c + lo, 16), n)

        for k in range(2):
            sends.append(rc(16 + k, w4_ref.at[piece(j, 0, early)], w4_ref.at[piece(j, 0, early)], (*chips[k], c)))
            sends[-1].start()

        for k, (px, py, pc) in enumerate(peers):
            pdev = 4 * px + 2 * py + pc
            rc(k, call_ref.at[pdev], call_ref.at[pdev], (px, py, pc)).wait_recv()
        rowid = lax.broadcasted_iota(jnp.int32, (N_DEV, D_MODEL), 0)
        call = jnp.zeros((N_DEV, D_MODEL), F32)
        for r in range(N_DEV):
            call = jnp.where(rowid == r, jnp.broadcast_to(call_ref[r], (N_DEV, D_MODEL)), call)
        bsh = bada_ref[:, 0:ADA_SHARD]
        for k in range(1, N_CHIPS):
            bsh = jnp.where(j == k, bada_ref[:, ADA_SHARD * k:ADA_SHARD * (k + 1)], bsh)
        part = jnp.dot(_silu(call).astype(BF16), wada_ref[...].astype(BF16), preferred_element_type=F32) + bsh
        for r in range(N_DEV):
            part_ref[r] = part[r:r + 1, :]
        ada_ref[j] = part_ref[dev]
        for k, chip in enumerate(chips):
            sends.append(rc(13 + k, part_ref.at[2 * idx[k] + c], ada_ref.at[j], (*chip, c)))
            sends[-1].start()
        for k, chip in enumerate(chips):
            rest = piece(j, early, IN_HALF - early) if k < 2 else rows_of(j, c)
            sends.append(rc(7 + k, w4_ref.at[rest], w4_ref.at[rest], (*chip, c)))
            sends[-1].start()

        x_copy.wait()

        def prenorm(i, carry):
            rr = pl.ds(pl.multiple_of(i * ch, ch), ch)
            xv = x_ref[rr, :]
            r = lax.rsqrt(jnp.mean(xv * xv, axis=-1, keepdims=True) + EPS)
            x_ref[rr, :] = (xv * r) * nw_ref[...]
            return carry

        lax.fori_loop(0, t // ch, prenorm, 0)
        for k in range(3):
            rc(13 + k, ada_ref.at[idx[k]], ada_ref.at[idx[k]], sib).wait_recv()

        shift = jnp.concatenate([ada_ref[0], ada_ref[1][:, 0:256]], axis=1)
        s1 = 1.0 + jnp.concatenate([ada_ref[1][:, 256:768], ada_ref[2][:, 0:512]], axis=1)

        def norm(i, carry):
            rr = pl.ds(pl.multiple_of(i * ch, ch), ch)
            h_ref[rr, :] = (x_ref[rr, :] * s1 + shift).astype(BF16)
            return carry

        lax.fori_loop(0, t // ch, norm, 0)
        h_copy = pltpu.make_async_copy(h_ref, h_hbm, lsem.at[1])
        h_copy.start()

        def put_main(case, slot):
            cps, col = [], 0
            for n, (a, c0, w) in enumerate(MAIN_DST[case]):
                cps.append(pltpu.make_async_copy(stg_ref.at[slot, :, pl.ds(col, w)], outs[a].at[:, pl.ds(c0, w)], osem.at[slot, n]))
                col += w
            return cps

        def put_pair(case, slot):
            a, c0, w = PAIR_DST[case]
            return pltpu.make_async_copy(pstg_ref.at[slot], outs[a].at[:, pl.ds(c0, w)], osem.at[slot, 2])

        def project(first_row, width, dst, slot):
            wrows = pl.ds(pl.multiple_of(first_row, 128), width)

            def blk(i, carry):
                rr = pl.ds(pl.multiple_of(i * ch, ch), ch)
                dst[slot, rr, :] = lax.dot_general(h_ref[rr, :], w4_ref[wrows, :], (((1,), (1,)), ((), ())),
                                                   preferred_element_type=F32)
                return carry

            lax.fori_loop(0, t // ch, blk, 0)

        def phase(p, s, pair):
            slot = p % 2
            if p >= 2:
                for case in range(N_CHIPS):
                    @pl.when(order[p - 2] == case)
                    def _():
                        for cp in put_main(case, slot):
                            cp.wait()
            if p == 3:
                for case in range(2):
                    @pl.when(j // 2 == case)
                    def _():
                        put_pair(case, 0).wait()
            project(2 * IN_HALF * s + 64 * (s % 2), MAIN_W, stg_ref, slot)
            for case in range(N_CHIPS):
                @pl.when(s == case)
                def _():
                    for cp in put_main(case, slot):
                        cp.start()
            if pair is not None:
                project(MAIN_W + 2 * (2 * IN_HALF) * pair, 128, pstg_ref, slot % 2 if p == 2 else 1)
                for case in range(2):
                    @pl.when(pair == case)
                    def _():
                        put_pair(case, 0 if p == 2 else 1).start()

        order = [j] + idx
        w_out = [pltpu.make_async_copy(w4_ref.at[pl.ds(pl.multiple_of(2 * IN_HALF * s, 32), 2 * IN_HALF)],
                                       w4_hbm.at[pl.ds(pl.multiple_of(2 * IN_HALF * s, 32), 2 * IN_HALF)], wsem.at[p])
                 for p, s in enumerate(order)]
        w_out[0].start()
        phase(0, j, None)
        passed = []
        for k in range(3):
            jk = idx[k]
            if k < 2:
                rc(16 + k, w4_ref.at[piece(jk, 0, early)], w4_ref.at[piece(jk, 0, early)], sib).wait_recv()
                rest = piece(jk, early, IN_HALF - early)
                rc(7 + k, w4_ref.at[rest], w4_ref.at[rest], sib).wait_recv()
            else:
                rc(7 + k, w4_ref.at[rows_of(jk, c)], w4_ref.at[rows_of(jk, c)], sib).wait_recv()
            passed.append(rc(10 + k, w4_ref.at[rows_of(jk, c)], w4_ref.at[rows_of(jk, c)], sib))
            passed[-1].start()
            rc(10 + k, w4_ref.at[rows_of(jk, 1 - c)], w4_ref.at[rows_of(jk, 1 - c)], sib).wait_recv()
            w_out[1 + k].start()
            if k == 0:
                phase(1, jk, None)
            elif k == 1:
                phase(2, jk, j // 2)
            else:
                phase(3, jk, 1 - j // 2)

        for case in range(N_CHIPS):
            for p in (2, 3):
                @pl.when(order[p] == case)
                def _():
                    for cp in put_main(case, p % 2):
                        cp.wait()
        for case in range(2):
            @pl.when(1 - j // 2 == case)
            def _():
                put_pair(case, 1).wait()
        h_copy.wait()
        for cp in w_out:
            cp.wait()
        for cp in sends + passed:
            cp.wait_send()

    vm = pl.BlockSpec(memory_space=pltpu.VMEM)
    hbm = pl.BlockSpec(memory_space=pl.ANY)
    widths = (512, 256, 512, 512, 512, 512)
    return pl.pallas_call(
        body,
        name="in_proj",
        in_specs=[hbm, vm, vm, vm, vm, vm],
        out_specs=[hbm] * 8 + [vm, vm],
        out_shape=[jax.ShapeDtypeStruct((t, w), F32) for w in widths]
        + [jax.ShapeDtypeStruct((t, D_MODEL), BF16), jax.ShapeDtypeStruct((IN_W, D_MODEL), BF16),
           jax.ShapeDtypeStruct((N_DEV, 1, D_MODEL), F32), jax.ShapeDtypeStruct((N_CHIPS, 1, ADA_SHARD), F32)],
        scratch_shapes=[pltpu.VMEM((t, D_MODEL), F32), pltpu.VMEM((t, D_MODEL), BF16), pltpu.VMEM((IN_W, D_MODEL), BF16),
                        pltpu.VMEM((2, t, MAIN_W), F32), pltpu.VMEM((2, t, 128), F32), pltpu.VMEM((N_DEV, 1, ADA_SHARD), F32),
                        pltpu.SemaphoreType.DMA((2,)), pltpu.SemaphoreType.DMA((2, 3)), pltpu.SemaphoreType.DMA((N_CHIPS,)),
                        pltpu.SemaphoreType.DMA((n_sem,)), pltpu.SemaphoreType.DMA((n_sem,))],
        compiler_params=_cparams(),
    )(x, wt, c_row, w_ada, b_ada, nw)


def _adamw_math(w, g, m, v):
    m2 = ADAM_B1 * m + (1.0 - ADAM_B1) * g
    v2 = ADAM_B2 * v + (1.0 - ADAM_B2) * (g * g)
    m_hat = m2 / (1.0 - ADAM_B1 ** ADAM_STEP)
    v_hat = v2 / (1.0 - ADAM_B2 ** ADAM_STEP)
    delta = -ADAM_LR * (m_hat / (jnp.sqrt(v_hat) + ADAM_EPS) + ADAM_WD * w)
    return delta, m2, v2


def _adamw(name, w, g, m, v, tm, through=None):
    r, cdim = w.shape
    nstep = r // tm
    extra = [] if through is None else [through]

    def body(w_ref, g_ref, m_ref, v_ref, *rest):
        g2_ref, d_ref, m2_ref, v2_ref = rest[len(extra):len(extra) + 4]
        g = g_ref[...]
        g2_ref[...] = g
        d_ref[...], m2_ref[...], v2_ref[...] = _adamw_math(w_ref[...], g, m_ref[...], v_ref[...])
        if extra:
            rest[-1][...] = rest[0][...]

    blk = pl.BlockSpec((tm, cdim), lambda i: (i, 0))
    eblk = [pl.BlockSpec((e.shape[0] // nstep, e.shape[1]), lambda i: (i, 0)) for e in extra]
    return pl.pallas_call(
        body,
        name=name,
        grid=(nstep,),
        in_specs=[blk] * 4 + eblk,
        out_specs=[blk] * 4 + eblk,
        out_shape=[jax.ShapeDtypeStruct((r, cdim), F32)] * 4 + [jax.ShapeDtypeStruct(e.shape, e.dtype) for e in extra],
        compiler_params=_cparams(dimension_semantics=("arbitrary",)),
    )(w, g, m, v, *extra)


def _adamw_ada(w, m, v, cact_t, dcols):
    r, cdim = w.shape
    tm = 256

    def body(w_ref, m_ref, v_ref, ct_ref, dc_ref, g_ref, d_ref, m2_ref, v2_ref):
        g = jnp.dot(ct_ref[...].astype(BF16), dc_ref[...].astype(BF16), preferred_element_type=F32)
        g_ref[...] = g
        d_ref[...], m2_ref[...], v2_ref[...] = _adamw_math(w_ref[...], g, m_ref[...], v_ref[...])

    blk = pl.BlockSpec((tm, cdim), lambda i: (i, 0))
    return pl.pallas_call(
        body,
        name="adamw_w_ada",
        grid=(r // tm,),
        in_specs=[blk] * 3 + [pl.BlockSpec((tm, N_DEV), lambda i: (i, 0)), pl.BlockSpec((N_DEV, cdim), lambda i: (0, 0))],
        out_specs=[blk] * 4,
        out_shape=[jax.ShapeDtypeStruct((r, cdim), F32)] * 4,
        compiler_params=_cparams(dimension_semantics=("arbitrary",)),
    )(w, m, v, cact_t, dcols)


def _adamw_small(ws, ms, vs, ssum, rows):
    n = len(ws)

    def body(*refs):
        w_r, m_r, v_r = refs[0:n], refs[n:2 * n], refs[2 * n:3 * n]
        ss_ref, rows_ref = refs[3 * n], refs[3 * n + 1]
        g_r, d_r, m2_r, v2_r = (refs[3 * n + 2 + k * n:3 * n + 2 + (k + 1) * n] for k in range(4))
        loss_ref = refs[7 * n + 2]
        j = 2 * lax.axis_index("x") + lax.axis_index("y")
        rsum = rows_ref[0]
        for d in range(1, N_DEV):
            rsum = rsum + rows_ref[d]
        taps = []
        for t in range(CONV_TAPS):
            row = ss_ref[t // 2:t // 2 + 1, :]
            c0 = CONV_W * (t % 2)
            pick = row[:, c0:c0 + 128]
            for k in range(1, N_CHIPS):
                pick = jnp.where(j == k, row[:, c0 + 128 * k:c0 + 128 * (k + 1)], pick)
            taps.append(pick)
        grads = [jnp.concatenate([rsum[2:3], rsum[3:4], rsum[0:1]], axis=1), rsum[4:5],
                 ss_ref[17:18, 512:512 + HEAD_DIM], ss_ref[17:18, 640:640 + HEAD_DIM], ss_ref[17:18, 768:776],
                 None, ss_ref[16:17, 0:CONV_W], ss_ref[16:17, CONV_W:2 * CONV_W], ss_ref[17:18, 0:CONV_W]]
        for i in range(n):
            if grads[i] is None:
                for t in range(CONV_TAPS):
                    g_r[i][t:t + 1, :] = taps[t]
                g = g_r[i][...]
            else:
                g = grads[i]
                g_r[i][...] = g
            d_r[i][...], m2_r[i][...], v2_r[i][...] = _adamw_math(w_r[i][...], g, m_r[i][...], v_r[i][...])
        loss_ref[...] = (0.5 / D_MODEL) * jnp.sum(ss_ref[18:19, :], axis=1, keepdims=True)

    vm = pl.BlockSpec(memory_space=pltpu.VMEM)
    shapes = [jax.ShapeDtypeStruct(w.shape, F32) for w in ws]
    out = pl.pallas_call(
        body,
        name="adamw_small",
        in_specs=[vm] * (3 * n + 2),
        out_specs=[vm] * (4 * n + 1),
        out_shape=shapes * 4 + [jax.ShapeDtypeStruct((1, 1), F32)],
        compiler_params=_cparams(),
    )(*ws, *ms, *vs, ssum, rows)
    return out[0:n], out[n:2 * n], out[2 * n:3 * n], out[3 * n:4 * n], out[4 * n]


def _rope_tables(t):
    inv = ROPE_THETA ** (-jnp.arange(0, HEAD_DIM, 2, dtype=F32) / HEAD_DIM)
    ang = jnp.arange(t, dtype=F32)[:, None] * inv[None, :]
    cos, sin = jnp.cos(ang), jnp.sin(ang)
    return jnp.tile(cos, (1, 4)), jnp.tile(jnp.concatenate([-sin, sin], axis=1), (1, 2))


def kernel(x, c, w_ada, b_ada, norm_w, w_in, q_norm_w, k_norm_w, sinks, conv_w, conv_b, ln_w, ln_b, w_out, loss_target, m_w_ada, m_b_ada, m_norm_w, m_w_in, m_q_norm_w, m_k_norm_w, m_sinks, m_conv_w, m_conv_b, m_ln_w, m_ln_b, m_w_out, v_w_ada, v_b_ada, v_norm_w, v_w_in, v_q_norm_w, v_k_norm_w, v_sinks, v_conv_w, v_conv_b, v_ln_w, v_ln_b, v_w_out):
    xi, yi = lax.axis_index("x"), lax.axis_index("y")
    j = 2 * xi + yi
    x2, tgt = x[0], loss_target[0]
    t = x2.shape[0]

    wt_s, mt_s, vt_s = w_in[0].T, m_w_in[0].T, v_w_in[0].T
    cw_pad = jnp.pad(conv_w[0], ((0, 1), (0, 0)))

    q_raw, kv_raw, ga, ua, ug, gb, h, w_full, call, ada4 = _in_proj_gather(
        x2, wt_s.reshape(2, IN_HALF, D_MODEL), c, w_ada[0], b_ada, norm_w)
    ada = ada4.reshape(1, 3 * D_MODEL)
    s1, gate = 1.0 + ada[:, D_MODEL:2 * D_MODEL], ada[:, 2 * D_MODEL:]

    cos_f, sin_s = _rope_tables(t)
    qw2, kw2 = jnp.tile(q_norm_w, (1, 2)), jnp.tile(k_norm_w, (1, 2))

    o, mix_a, wo4, cw4 = _attn_fwd(q_raw, kv_raw, ga, qw2, kw2, sinks, cos_f, sin_s,
                                   w_out[0].reshape(2, OUT_HALF, D_MODEL), cw_pad)
    w_out_full = wo4.reshape(D_MODEL, D_MODEL)
    cw_full = jnp.concatenate([cw4[i] for i in range(N_CHIPS)], axis=1)
    cz, mix_b = _conv_fwd(ua, ug, gb, cw_full, conv_b, ln_w, ln_b)
    dout, dmix_a, dmix_b, gwo_bf, red_o = _out_proj(mix_a, mix_b, x2, tgt, gate, w_out_full)

    dq, dkv, dga, sm_a, gwo = _attn_bwd(q_raw, kv_raw, ga, o, dmix_a, qw2, kw2, sinks, cos_f, sin_s,
                                        gwo_bf.reshape(N_CHIPS, 2, OUT_HALF, D_MODEL))
    dua, dug, dgb, dcw, dvec = _conv_bwd(ua, ug, gb, cz, dmix_b, cw_full, ln_w, ln_b)
    dparts = (dq, dkv, dga, dua, dug, dgb)

    grad_x, gw, ssum, rows = _in_proj_bwd(dparts, h, x2, dout, s1, norm_w, w_full, dcw, dvec, sm_a, red_o)

    gt_w_in = gw.reshape(2 * IN_HALF, D_MODEL)
    g_w_out = gwo.reshape(D_MODEL // N_CHIPS, D_MODEL)
    d_ada_all = jnp.concatenate([rows[:, 2], rows[:, 3], rows[:, 0]], axis=1)
    dcols = lax.dynamic_slice(d_ada_all, (0, ADA_SHARD * j), (N_DEV, ADA_SHARD))
    cact_t = jax.nn.silu(call.reshape(N_DEV, D_MODEL)).T

    g_w_ada, d_w_ada, nm_w_ada, nv_w_ada = _adamw_ada(w_ada[0], m_w_ada[0], v_w_ada[0], cact_t, dcols)
    gt_w_in, dt_w_in, nmt_w_in, nvt_w_in, grad_x = _adamw("adamw_w_in", wt_s, gt_w_in, mt_s, vt_s, 176, through=grad_x)
    g_w_in, d_w_in, nm_w_in, nv_w_in = gt_w_in.T, dt_w_in.T, nmt_w_in.T, nvt_w_in.T
    g_w_out, d_w_out, nm_w_out, nv_w_out = _adamw("adamw_w_out", w_out[0], g_w_out, m_w_out[0], v_w_out[0], 128)
    ws = [b_ada, norm_w, q_norm_w, k_norm_w, sinks, conv_w[0], conv_b, ln_w, ln_b]
    ms = [m_b_ada, m_norm_w, m_q_norm_w, m_k_norm_w, m_sinks, m_conv_w[0], m_conv_b, m_ln_w, m_ln_b]
    vs = [v_b_ada, v_norm_w, v_q_norm_w, v_k_norm_w, v_sinks, v_conv_w[0], v_conv_b, v_ln_w, v_ln_b]
    gs, ds, nms, nvs, loss11 = _adamw_small(ws, ms, vs, ssum, rows)
    loss = loss11[0, 0]

    def order(ada_v, in_v, out_v, sm):
        b, nw_, qw_, kw_, sk_, cw_, cb_, lw_, lb_ = sm
        return [ada_v[None], b, nw_, in_v[None], qw_, kw_, sk_, cw_[None], cb_, lw_, lb_, out_v[None]]

    grads = order(g_w_ada, g_w_in, g_w_out, gs)
    deltas = order(d_w_ada, d_w_in, d_w_out, ds)
    new_m = order(nm_w_ada, nm_w_in, nm_w_out, nms)
    new_v = order(nv_w_ada, nv_w_in, nv_w_out, nvs)
    return (loss, grad_x[None], *grads, *deltas, *new_m, *new_v)
```

```python
import functools

import jax
import jax.numpy as jnp
from jax import lax
from jax.experimental import pallas as pl
from jax.experimental.pallas import tpu as pltpu

F32 = jnp.float32
BF16 = jnp.bfloat16

D_MODEL = 1024
ATTN_W = 512
KV_W = 128
CONV_W = 512
IN_W = 2816
HEAD_DIM = 64
CONV_TAPS = 31
QBLK = 128
EPS = 1e-6
ROPE_THETA = 10000.0

ADAM_LR = 0.001
ADAM_B1 = 0.9
ADAM_B2 = 0.999
ADAM_EPS = 1e-08
ADAM_WD = 0.01
ADAM_STEP = 10

N_CHIPS = 4
N_DEV = 8
IN_HALF = IN_W // N_CHIPS // 2
OUT_HALF = D_MODEL // N_CHIPS // 2
ADA_SHARD = 3 * D_MODEL // N_CHIPS

VMEM_LIMIT = 56 * 1024 * 1024
CONV_PAD = 32


def _cparams(**kw):
    return pltpu.CompilerParams(vmem_limit_bytes=VMEM_LIMIT, **kw)


def _sigmoid(v):
    return 1.0 / (1.0 + jnp.exp(-v))


def _silu(v):
    return v * _sigmoid(v)


def _dsilu(v):
    s = _sigmoid(v)
    return s * (1.0 + v * (1.0 - s))


def _lane(shape):
    return lax.broadcasted_iota(jnp.int32, shape, len(shape) - 1)


PUT_ROWS = 512


def _fetch(hbm_refs, vmem_refs, sem):
    cps = [pltpu.make_async_copy(h, v, sem.at[i]) for i, (h, v) in enumerate(zip(hbm_refs, vmem_refs))]
    for cp in cps:
        cp.start()
    return cps


def _put(vmem_ref, hbm_ref, sem, m):
    r = pl.ds(pl.multiple_of(m * PUT_ROWS, PUT_ROWS), PUT_ROWS)
    return pltpu.make_async_copy(vmem_ref.at[r], hbm_ref.at[r], sem.at[m])


def _put_all(pairs, sems, m):
    for (v, h), sem in zip(pairs, sems):
        _put(v, h, sem, m).start()


def _put_wait(pairs, sems, n):
    for (v, h), sem in zip(pairs, sems):
        for m in range(n):
            _put(v, h, sem, m).wait()


def _head_mean(s, left):
    sl = jnp.sum(jnp.where(left, s, 0.0), axis=-1, keepdims=True)
    sr = jnp.sum(jnp.where(left, 0.0, s), axis=-1, keepdims=True)
    return jnp.where(left, sl, sr) * (1.0 / HEAD_DIM)


def _rot(v, first):
    return jnp.where(first, pltpu.roll(v, 96, 1), pltpu.roll(v, 32, 1))


def _norm_rope(v, w, cos, sin_s, left, first):
    r = lax.rsqrt(_head_mean(v * v, left) + EPS)
    xh = v * r
    n = xh * w
    return n * cos + _rot(n, first) * sin_s, xh, r


def _norm_rope_bwd(d, xh, r, w, cos, sin_s, left, first):
    dn = d * cos - _rot(d, first) * sin_s
    dw = jnp.sum(dn * xh, axis=0, keepdims=True)
    dxh = dn * w
    return r * (dxh - xh * _head_mean(dxh * xh, left)), dw


def _dup_heads(v, left):
    sw = pltpu.roll(v, 64, 1)
    return jnp.where(left, v, sw), jnp.where(left, sw, v)


def _prep_kv(kv_ref, kw_ref, cos_ref, sin_ref, ka_ref, va_ref, t):
    ch = 256
    for g in range(2):
        ka_ref[g, 0:QBLK, :] = jnp.zeros((QBLK, 128), BF16)
        va_ref[g, 0:QBLK, :] = jnp.zeros((QBLK, 128), BF16)

    def chunk(i, carry):
        r0 = pl.multiple_of(i * ch, ch)
        left = _lane((ch, 128)) < 64
        first = (_lane((ch, 128)) % 64) < 32
        k = kv_ref[pl.ds(r0, ch), 0:128]
        v = kv_ref[pl.ds(r0, ch), 128:256]
        kr, _, _ = _norm_rope(k, kw_ref[...], cos_ref[pl.ds(r0, ch), :], sin_ref[pl.ds(r0, ch), :], left, first)
        k0, k1 = _dup_heads(kr, left)
        v0, v1 = _dup_heads(v, left)
        ka_ref[0, pl.ds(QBLK + r0, ch), :] = k0.astype(BF16)
        ka_ref[1, pl.ds(QBLK + r0, ch), :] = k1.astype(BF16)
        va_ref[0, pl.ds(QBLK + r0, ch), :] = v0.astype(BF16)
        va_ref[1, pl.ds(QBLK + r0, ch), :] = v1.astype(BF16)
        return carry

    lax.fori_loop(0, t // ch, chunk, 0)


def _band_mask(n):
    qi = lax.broadcasted_iota(jnp.int32, (2 * QBLK, 2 * QBLK), 0) % QBLK
    kj = lax.broadcasted_iota(jnp.int32, (2 * QBLK, 2 * QBLK), 1)
    local = (kj > qi) & (kj <= qi + QBLK)
    return local & ((n > 0) | (kj >= QBLK))


def _softmax_pair(s, mask, sink0, sink1):
    row = lax.broadcasted_iota(jnp.int32, (2 * QBLK, 1), 0)
    sink = jnp.where(row < QBLK, sink0, sink1)
    s = jnp.where(mask, s, -jnp.inf)
    m = jnp.maximum(jnp.max(s, axis=-1, keepdims=True), sink)
    e = jnp.exp(s - m)
    es = jnp.exp(sink - m)
    inv = 1.0 / (jnp.sum(e, axis=-1, keepdims=True) + es)
    return e * inv, es * inv


def _stack_heads(v, left):
    return jnp.concatenate([jnp.where(left, v, 0.0), jnp.where(left, 0.0, v)], axis=0)


def _attn_fwd(q_raw, kv_raw, ga, qw2, kw2, sinks, cos_f, sin_s, wo, cw):
    t = q_raw.shape[0]
    nblk = t // QBLK
    per_put = PUT_ROWS // QBLK

    def body(q_hbm, kv_ref, ga_hbm, qw_ref, kw_ref, sk_ref, cos_hbm, sin_hbm, wo_ref, cw_ref,
             o_hbm, mix_hbm, wo4_ref, cw4_ref, ka_ref, va_ref, q_ref, ga_ref, o_ref, mix_ref, cos_ref, sin_ref,
             isem, osem0, osem1, ssem, rsem):
        loads = _fetch((cos_hbm, sin_hbm, q_hbm, ga_hbm), (cos_ref, sin_ref, q_ref, ga_ref), isem)
        outs, osems = ((o_ref, o_hbm), (mix_ref, mix_hbm)), (osem0, osem1)
        x, y, c, chips = _place()
        j = 2 * x + y
        sib = (x, y, 1 - c)
        idx = [2 * cx + cy for cx, cy in chips]
        rc = functools.partial(_remote, ssem, rsem)
        wo4_ref[j] = wo_ref[...].astype(BF16)
        cw4_ref[j] = cw_ref[...]
        sends = []
        for k, chip in enumerate(chips):
            sends.append(rc(k, wo4_ref.at[j, c], wo4_ref.at[j, c], (*chip, c)))
            sends.append(rc(6 + k, cw4_ref.at[j], cw4_ref.at[j], (*chip, c)))
        for cp in sends:
            cp.start()

        loads[0].wait()
        loads[1].wait()
        _prep_kv(kv_ref, kw_ref, cos_ref, sin_ref, ka_ref, va_ref, t)
        loads[2].wait()
        loads[3].wait()

        def blk(n, carry):
            r0 = pl.multiple_of(n * QBLK, QBLK)
            left = _lane((QBLK, 128)) < 64
            first = (_lane((QBLK, 128)) % 64) < 32
            cos = cos_ref[pl.ds(r0, QBLK), :]
            sin = sin_ref[pl.ds(r0, QBLK), :]
            mask = _band_mask(n)
            scores = []
            for p in range(4):
                lanes = slice(p * 128, (p + 1) * 128)
                qr, _, _ = _norm_rope(q_ref[pl.ds(r0, QBLK), lanes], qw_ref[...], cos, sin, left, first)
                q2 = _stack_heads(qr * 0.125, left).astype(BF16)
                scores.append(lax.dot_general(q2, ka_ref[p // 2, pl.ds(r0, 2 * QBLK), :], (((1,), (1,)), ((), ())),
                                              preferred_element_type=F32))
            probs = [_softmax_pair(scores[p], mask, sk_ref[0, 2 * p], sk_ref[0, 2 * p + 1])[0].astype(BF16)
                     for p in range(4)]
            for p in range(4):
                lanes = slice(p * 128, (p + 1) * 128)
                o2 = jnp.dot(probs[p], va_ref[p // 2, pl.ds(r0, 2 * QBLK), :], preferred_element_type=F32)
                o = jnp.where(left, o2[0:QBLK], o2[QBLK:2 * QBLK])
                o_ref[pl.ds(r0, QBLK), lanes] = o.astype(BF16)
                mix_ref[pl.ds(r0, QBLK), lanes] = (o * _silu(ga_ref[pl.ds(r0, QBLK), lanes])).astype(BF16)

            @pl.when(n % per_put == per_put - 1)
            def _():
                _put_all(outs, osems, n // per_put)

            return carry

        lax.fori_loop(0, nblk, blk, 0)
        _put_wait(outs, osems, t // PUT_ROWS)

        passed = []
        for k, chip in enumerate(chips):
            jk = idx[k]
            rc(k, wo4_ref.at[jk, c], wo4_ref.at[jk, c], sib).wait_recv()
            passed.append(rc(3 + k, wo4_ref.at[jk, c], wo4_ref.at[jk, c], sib))
            passed[-1].start()
        for k, chip in enumerate(chips):
            jk = idx[k]
            rc(3 + k, wo4_ref.at[jk, 1 - c], wo4_ref.at[jk, 1 - c], sib).wait_recv()
            rc(6 + k, cw4_ref.at[jk], cw4_ref.at[jk], sib).wait_recv()
        for cp in sends + passed:
            cp.wait_send()

    vm = pl.BlockSpec(memory_space=pltpu.VMEM)
    hbm = pl.BlockSpec(memory_space=pl.ANY)
    n_sem = 9
    return pl.pallas_call(
        body,
        name="attn_fwd",
        in_specs=[hbm, vm, hbm, vm, vm, pl.BlockSpec(memory_space=pltpu.SMEM), hbm, hbm, vm, vm],
        out_specs=[hbm, hbm, vm, vm],
        out_shape=[jax.ShapeDtypeStruct((t, ATTN_W), BF16), jax.ShapeDtypeStruct((t, ATTN_W), BF16),
                   jax.ShapeDtypeStruct((N_CHIPS, 2, OUT_HALF, D_MODEL), BF16),
                   jax.ShapeDtypeStruct((N_CHIPS, 32, 128), F32)],
        scratch_shapes=[pltpu.VMEM((2, t + QBLK, 128), BF16), pltpu.VMEM((2, t + QBLK, 128), BF16),
                        pltpu.VMEM((t, ATTN_W), F32), pltpu.VMEM((t, ATTN_W), F32),
                        pltpu.VMEM((t, ATTN_W), BF16), pltpu.VMEM((t, ATTN_W), BF16),
                        pltpu.VMEM((t, 128), F32), pltpu.VMEM((t, 128), F32),
                        pltpu.SemaphoreType.DMA((4,)), pltpu.SemaphoreType.DMA((t // PUT_ROWS,)),
                        pltpu.SemaphoreType.DMA((t // PUT_ROWS,)),
                        pltpu.SemaphoreType.DMA((n_sem,)), pltpu.SemaphoreType.DMA((n_sem,))],
        compiler_params=_cparams(),
    )(q_raw, kv_raw, ga, qw2, kw2, sinks, cos_f, sin_s, wo, cw)


def _attn_bwd(q_raw, kv_raw, ga, o, dmix, qw2, kw2, sinks, cos_f, sin_s, go):
    t = q_raw.shape[0]
    nblk = t // QBLK
    per_put = PUT_ROWS // QBLK

    def body(q_hbm, kv_ref, ga_hbm, o_hbm, dm_hbm, qw_ref, kw_ref, sk_ref, cos_hbm, sin_hbm, go_ref,
             dq_hbm, dkv_ref, dga_hbm, sm_ref, gwo_ref, ka_ref, va_ref, dka_ref, dva_ref,
             sibo_ref, outo_ref, ino_ref, q_ref, ga_ref, o_ref, dm_ref, dq_ref, dga_ref, cos_ref, sin_ref,
             isem, osem0, osem1, ssem, rsem):
        loads = _fetch((cos_hbm, sin_hbm, q_hbm, ga_hbm, o_hbm, dm_hbm), (cos_ref, sin_ref, q_ref, ga_ref, o_ref, dm_ref), isem)
        outs, osems = ((dq_ref, dq_hbm), (dga_ref, dga_hbm)), (osem0, osem1)
        x, y, c, chips = _place()
        sib = (x, y, 1 - c)
        rc = functools.partial(_remote, ssem, rsem)
        theirs, mine = go_ref.at[:, 1 - c], go_ref.at[:, c]
        sends = [_rs_to_sibling(rc, 0, theirs, sibo_ref, sib)]
        loads[0].wait()
        loads[1].wait()
        _prep_kv(kv_ref, kw_ref, cos_ref, sin_ref, ka_ref, va_ref, t)
        dka_ref[...] = jnp.zeros_like(dka_ref)
        dva_ref[...] = jnp.zeros_like(dva_ref)
        sends += _rs_trade(rc, 0, theirs, mine, sibo_ref, outo_ref, ino_ref, OUT_HALF, c, sib, chips)
        for cp in loads[2:]:
            cp.wait()

        def blk(n, carry):
            dqw, dsk = carry
            r0 = pl.multiple_of(n * QBLK, QBLK)
            left = _lane((QBLK, 128)) < 64
            first = (_lane((QBLK, 128)) % 64) < 32
            cos = cos_ref[pl.ds(r0, QBLK), :]
            sin = sin_ref[pl.ds(r0, QBLK), :]
            mask = _band_mask(n)
            row = lax.broadcasted_iota(jnp.int32, (2 * QBLK, 1), 0)
            rows = pl.ds(r0, QBLK)
            win = pl.ds(r0, 2 * QBLK)
            lane_of = [slice(p * 128, (p + 1) * 128) for p in range(4)]
            for grp in ((0, 1), (2, 3)):
                qn = {p: _norm_rope(q_ref[rows, lane_of[p]], qw_ref[...], cos, sin, left, first) for p in grp}
                q2 = {p: _stack_heads(qn[p][0] * 0.125, left).astype(BF16) for p in grp}
                sc = {p: lax.dot_general(q2[p], ka_ref[p // 2, win, :], (((1,), (1,)), ((), ())),
                                         preferred_element_type=F32) for p in grp}
                do2 = {}
                for p in grp:
                    gav = ga_ref[rows, lane_of[p]]
                    dmv = dm_ref[rows, lane_of[p]].astype(F32)
                    dga_ref[rows, lane_of[p]] = (dmv * o_ref[rows, lane_of[p]].astype(F32) * _dsilu(gav)).astype(BF16)
                    do2[p] = _stack_heads(dmv * _silu(gav), left).astype(BF16)
                dpm = {p: lax.dot_general(do2[p], va_ref[p // 2, win, :], (((1,), (1,)), ((), ())),
                                          preferred_element_type=F32) for p in grp}
                sm = {p: _softmax_pair(sc[p], mask, sk_ref[0, 2 * p], sk_ref[0, 2 * p + 1]) for p in grp}
                dsl = {}
                for p in grp:
                    pm, ps = sm[p]
                    delta = jnp.sum(pm * dpm[p], axis=-1, keepdims=True)
                    dsl[p] = (pm * (dpm[p] - delta)).astype(BF16)
                    pd = ps * delta
                    d0 = jnp.sum(jnp.where(row < QBLK, pd, 0.0), axis=0, keepdims=True)
                    d1 = jnp.sum(jnp.where(row < QBLK, 0.0, pd), axis=0, keepdims=True)
                    l8 = _lane((1, 128))
                    dsk = dsk - jnp.where(l8 == 2 * p, d0, 0.0) - jnp.where(l8 == 2 * p + 1, d1, 0.0)
                for p in grp:
                    g = p // 2
                    dva_ref[g, win, :] += lax.dot_general(sm[p][0].astype(BF16), do2[p], (((0,), (0,)), ((), ())),
                                                          preferred_element_type=F32)
                    dka_ref[g, win, :] += lax.dot_general(dsl[p], q2[p], (((0,), (0,)), ((), ())),
                                                          preferred_element_type=F32)
                for p in grp:
                    dq2 = jnp.dot(dsl[p], ka_ref[p // 2, win, :], preferred_element_type=F32)
                    dqr = jnp.where(left, dq2[0:QBLK], dq2[QBLK:2 * QBLK]) * 0.125
                    dq, dw = _norm_rope_bwd(dqr, qn[p][1], qn[p][2], qw_ref[...], cos, sin, left, first)
                    dq_ref[rows, lane_of[p]] = dq.astype(BF16)
                    dqw = dqw + dw

            @pl.when(n % per_put == per_put - 1)
            def _():
                _put_all(outs, osems, n // per_put)

            return dqw, dsk

        zero = jnp.zeros((1, 128), F32)
        dqw, dsk = lax.fori_loop(0, nblk, blk, (zero, zero))

        ch = 256

        def chunk(i, dkw):
            r0 = pl.multiple_of(i * ch, ch)
            left = _lane((ch, 128)) < 64
            first = (_lane((ch, 128)) % 64) < 32
            rows = pl.ds(r0, ch)
            prow = pl.ds(QBLK + r0, ch)

            def fold(ref):
                a0 = ref[0, prow, :]
                a1 = ref[1, prow, :]
                return jnp.where(left, a0 + pltpu.roll(a0, 64, 1), a1 + pltpu.roll(a1, 64, 1))

            cos = cos_ref[rows, :]
            sin = sin_ref[rows, :]
            _, xh, r = _norm_rope(kv_ref[rows, 0:128], kw_ref[...], cos, sin, left, first)
            dk, dw = _norm_rope_bwd(fold(dka_ref), xh, r, kw_ref[...], cos, sin, left, first)
            dkv_ref[rows, 0:128] = dk.astype(BF16)
            dkv_ref[rows, 128:256] = fold(dva_ref).astype(BF16)
            return dkw + dw

        dkw = lax.fori_loop(0, t // ch, chunk, zero)
        sm_ref[...] = jnp.zeros((8, 128), F32)
        sm_ref[0:1, :] = dqw + pltpu.roll(dqw, 64, 1)
        sm_ref[1:2, :] = dkw + pltpu.roll(dkw, 64, 1)
        sm_ref[2:3, :] = dsk

        j = 2 * x + y
        sends.append(_rs_total(rc, 0, mine, sibo_ref, outo_ref, ino_ref, gwo_ref, OUT_HALF, j, c, sib))
        _rs_done(rc, 0, gwo_ref, c, sib)
        for cp in sends:
            cp.wait_send()
        _put_wait(outs, osems, t // PUT_ROWS)

    vm = pl.BlockSpec(memory_space=pltpu.VMEM)
    hbm = pl.BlockSpec(memory_space=pl.ANY)
    return pl.pallas_call(
        body,
        name="attn_bwd",
        in_specs=[hbm, vm, hbm, hbm, hbm, vm, vm, pl.BlockSpec(memory_space=pltpu.SMEM), hbm, hbm, vm],
        out_specs=[hbm, vm, hbm, vm, vm],
        out_shape=[jax.ShapeDtypeStruct((t, ATTN_W), BF16), jax.ShapeDtypeStruct((t, 2 * KV_W), BF16),
                   jax.ShapeDtypeStruct((t, ATTN_W), BF16), jax.ShapeDtypeStruct((8, 128), F32),
                   jax.ShapeDtypeStruct((2, OUT_HALF, D_MODEL), F32)],
        scratch_shapes=[pltpu.VMEM((2, t + QBLK, 128), BF16), pltpu.VMEM((2, t + QBLK, 128), BF16),
                        pltpu.VMEM((2, t + QBLK, 128), F32), pltpu.VMEM((2, t + QBLK, 128), F32)]
        + _rs_scratch(OUT_HALF)
        + [pltpu.VMEM((t, ATTN_W), F32), pltpu.VMEM((t, ATTN_W), F32), pltpu.VMEM((t, ATTN_W), BF16),
           pltpu.VMEM((t, ATTN_W), BF16), pltpu.VMEM((t, ATTN_W), BF16), pltpu.VMEM((t, ATTN_W), BF16),
           pltpu.VMEM((t, 128), F32), pltpu.VMEM((t, 128), F32),
           pltpu.SemaphoreType.DMA((6,)), pltpu.SemaphoreType.DMA((t // PUT_ROWS,)), pltpu.SemaphoreType.DMA((t // PUT_ROWS,)),
           pltpu.SemaphoreType.DMA((RS_SEMS,)), pltpu.SemaphoreType.DMA((RS_SEMS,))],
        compiler_params=_cparams(),
    )(q_raw, kv_raw, ga, o, dmix, qw2, kw2, sinks, cos_f, sin_s, go)


CONV_CH = 256
CONV_SUB = 128
CONV_ACCS = 1


def _shifted_windows(src_ref, r0, sh_ref):
    rows = CONV_CH + CONV_PAD
    win = src_ref[pl.ds(r0, rows), :]
    for b in range(8):
        sh = win if b == 0 else pltpu.roll(win, rows - b, 0)
        for c in range(CONV_W // 128):
            sh_ref[b, c] = sh[:, c * 128:(c + 1) * 128]


def _conv_fwd(ua, ug, gb, cw, cb, lw, lb):
    t = ua.shape[0]

    def body(ua_hbm, ug_hbm, gb_hbm, cw_ref, cb_ref, lw_ref, lb_ref, cz_hbm, mix_hbm, zp_ref, sh_ref,
             ua_ref, ug_ref, gb_ref, cz_ref, mix_ref, isem, osem0, osem1):
        loads = _fetch((ua_hbm, ug_hbm, gb_hbm), (ua_ref, ug_ref, gb_ref), isem)
        outs, osems = ((cz_ref, cz_hbm), (mix_ref, mix_hbm)), (osem0, osem1)
        per_put = PUT_ROWS // CONV_CH
        zp_ref[0:CONV_PAD, :] = jnp.zeros((CONV_PAD, CONV_W), F32)
        loads[0].wait()
        loads[1].wait()

        def glu(i, carry):
            r0 = pl.multiple_of(i * CONV_CH, CONV_CH)
            rows = pl.ds(r0, CONV_CH)
            zp_ref[pl.ds(CONV_PAD + r0, CONV_CH), :] = ua_ref[rows, :] * _sigmoid(ug_ref[rows, :])
            return carry

        lax.fori_loop(0, t // CONV_CH, glu, 0)
        loads[2].wait()

        def chunk(i, carry):
            r0 = pl.multiple_of(i * CONV_CH, CONV_CH)
            _shifted_windows(zp_ref, r0, sh_ref)
            for c in range(CONV_W // 128):
                lanes = slice(c * 128, (c + 1) * 128)

                def sub(k, carry2):
                    b0 = pl.multiple_of(k * CONV_SUB, CONV_SUB)
                    acc = [jnp.broadcast_to(cb_ref[0:1, lanes], (CONV_SUB, 128))] + [None] * (CONV_ACCS - 1)
                    for j in range(CONV_TAPS):
                        off = j + CONV_PAD - (CONV_TAPS - 1)
                        term = sh_ref[off % 8, c, pl.ds(b0 + 8 * (off // 8), CONV_SUB), :] * cw_ref[j:j + 1, lanes]
                        acc[j % CONV_ACCS] = term if acc[j % CONV_ACCS] is None else acc[j % CONV_ACCS] + term
                    cz_ref[pl.ds(r0 + b0, CONV_SUB), lanes] = functools.reduce(lambda a, b: a + b, acc)
                    return carry2

                lax.fori_loop(0, CONV_CH // CONV_SUB, sub, 0)
            rows = pl.ds(r0, CONV_CH)
            cz = cz_ref[rows, :]
            mu = jnp.mean(cz, axis=-1, keepdims=True)
            xc = cz - mu
            rs = lax.rsqrt(jnp.mean(xc * xc, axis=-1, keepdims=True) + EPS)
            ln = xc * rs * lw_ref[...] + lb_ref[...]
            mix_ref[rows, :] = (_silu(ln) * _silu(gb_ref[rows, :])).astype(BF16)

            @pl.when(i % per_put == per_put - 1)
            def _():
                _put_all(outs, osems, i // per_put)

            return carry

        lax.fori_loop(0, t // CONV_CH, chunk, 0)
        _put_wait(outs, osems, t // PUT_ROWS)

    vm = pl.BlockSpec(memory_space=pltpu.VMEM)
    hbm = pl.BlockSpec(memory_space=pl.ANY)
    nput = t // PUT_ROWS
    return pl.pallas_call(
        body,
        name="conv_fwd",
        in_specs=[hbm] * 3 + [vm] * 4,
        out_specs=[hbm, hbm],
        out_shape=[jax.ShapeDtypeStruct((t, CONV_W), F32), jax.ShapeDtypeStruct((t, CONV_W), BF16)],
        scratch_shapes=[pltpu.VMEM((t + CONV_PAD, CONV_W), F32),
                        pltpu.VMEM((8, CONV_W // 128, CONV_CH + CONV_PAD, 128), F32),
                        pltpu.VMEM((t, CONV_W), F32), pltpu.VMEM((t, CONV_W), F32), pltpu.VMEM((t, CONV_W), F32),
                        pltpu.VMEM((t, CONV_W), F32), pltpu.VMEM((t, CONV_W), BF16),
                        pltpu.SemaphoreType.DMA((3,)), pltpu.SemaphoreType.DMA((nput,)), pltpu.SemaphoreType.DMA((nput,))],
        compiler_params=_cparams(),
    )(ua, ug, gb, cw, cb, lw, lb)


def _conv_bwd(ua, ug, gb, cz, dmix, cw, lw, lb):
    t = ua.shape[0]

    def body(ua_hbm, ug_hbm, gb_hbm, cz_hbm, dm_hbm, cw_ref, lw_ref, lb_ref,
             dua_hbm, dug_hbm, dgb_hbm, dcw_ref, dvec_ref, zp_ref, dp_ref, sh_ref, wacc_ref,
             ua_ref, ug_ref, gb_ref, cz_ref, dm_ref, dua_ref, dug_ref, dgb_ref, isem, osem0, osem1, osem2):
        loads = _fetch((ua_hbm, ug_hbm, gb_hbm, cz_hbm, dm_hbm), (ua_ref, ug_ref, gb_ref, cz_ref, dm_ref), isem)
        per_put = PUT_ROWS // CONV_CH
        zp_ref[0:CONV_PAD, :] = jnp.zeros((CONV_PAD, CONV_W), F32)
        dp_ref[t:t + CONV_PAD, :] = jnp.zeros((CONV_PAD, CONV_W), F32)
        wacc_ref[...] = jnp.zeros_like(wacc_ref)
        for cp in loads:
            cp.wait()

        def pointwise(i, carry):
            dcb, dlw, dlb = carry
            r0 = pl.multiple_of(i * CONV_CH, CONV_CH)
            rows = pl.ds(r0, CONV_CH)
            zp_ref[pl.ds(CONV_PAD + r0, CONV_CH), :] = ua_ref[rows, :] * _sigmoid(ug_ref[rows, :])
            cz = cz_ref[rows, :]
            mu = jnp.mean(cz, axis=-1, keepdims=True)
            xc = cz - mu
            rs = lax.rsqrt(jnp.mean(xc * xc, axis=-1, keepdims=True) + EPS)
            xh = xc * rs
            ln = xh * lw_ref[...] + lb_ref[...]
            gbv = gb_ref[rows, :]
            dy = dm_ref[rows, :].astype(F32)
            dgb_ref[rows, :] = (dy * _silu(ln) * _dsilu(gbv)).astype(BF16)
            dl = dy * _silu(gbv) * _dsilu(ln)
            dxh = dl * lw_ref[...]
            dcz = rs * (dxh - jnp.mean(dxh, axis=-1, keepdims=True)
                        - xh * jnp.mean(dxh * xh, axis=-1, keepdims=True))
            dp_ref[rows, :] = dcz

            @pl.when(i % per_put == per_put - 1)
            def _():
                _put(dgb_ref, dgb_hbm, osem2, i // per_put).start()

            return (dcb + jnp.sum(dcz, axis=0, keepdims=True),
                    dlw + jnp.sum(dl * xh, axis=0, keepdims=True),
                    dlb + jnp.sum(dl, axis=0, keepdims=True))

        zero = jnp.zeros((1, CONV_W), F32)
        dcb, dlw, dlb = lax.fori_loop(0, t // CONV_CH, pointwise, (zero, zero, zero))
        dvec_ref[...] = jnp.zeros((8, CONV_W), F32)
        dvec_ref[0:1, :] = dcb
        dvec_ref[1:2, :] = dlw
        dvec_ref[2:3, :] = dlb

        def chunk(i, carry):
            r0 = pl.multiple_of(i * CONV_CH, CONV_CH)
            _shifted_windows(dp_ref, r0, sh_ref)
            for c in range(CONV_W // 128):
                lanes = slice(c * 128, (c + 1) * 128)

                def sub(k, carry2):
                    b0 = pl.multiple_of(k * CONV_SUB, CONV_SUB)
                    acc = [None] * CONV_ACCS
                    for j in range(CONV_TAPS):
                        off = CONV_TAPS - 1 - j
                        term = sh_ref[off % 8, c, pl.ds(b0 + 8 * (off // 8), CONV_SUB), :] * cw_ref[j:j + 1, lanes]
                        acc[j % CONV_ACCS] = term if acc[j % CONV_ACCS] is None else acc[j % CONV_ACCS] + term
                    acc = functools.reduce(lambda a, b: a + b, acc)
                    rr = pl.ds(r0 + b0, CONV_SUB)
                    sg = _sigmoid(ug_ref[rr, lanes])
                    dua_ref[rr, lanes] = (acc * sg).astype(BF16)
                    dug_ref[rr, lanes] = (acc * ua_ref[rr, lanes] * sg * (1.0 - sg)).astype(BF16)
                    return carry2

                lax.fori_loop(0, CONV_CH // CONV_SUB, sub, 0)
            _shifted_windows(zp_ref, r0, sh_ref)
            for c in range(CONV_W // 128):
                lanes = slice(c * 128, (c + 1) * 128)

                def subw(k, carry2):
                    b0 = pl.multiple_of(k * CONV_SUB, CONV_SUB)
                    dcz = dp_ref[pl.ds(r0 + b0, CONV_SUB), lanes]
                    for j in range(CONV_TAPS):
                        off = j + CONV_PAD - (CONV_TAPS - 1)
                        pr = dcz * sh_ref[off % 8, c, pl.ds(b0 + 8 * (off // 8), CONV_SUB), :]
                        parts = [pr[8 * q:8 * (q + 1)] for q in range(CONV_SUB // 8)]
                        while len(parts) > 1:
                            parts = [a + b for a, b in zip(parts[0::2], parts[1::2])]
                        wacc_ref[8 * j:8 * (j + 1), lanes] += parts[0]
                    return carry2

                lax.fori_loop(0, CONV_CH // CONV_SUB, subw, 0)

            @pl.when(i % per_put == per_put - 1)
            def _():
                _put_all(((dua_ref, dua_hbm), (dug_ref, dug_hbm)), (osem0, osem1), i // per_put)

            return carry

        lax.fori_loop(0, t // CONV_CH, chunk, 0)
        _put_wait(((dua_ref, dua_hbm), (dug_ref, dug_hbm), (dgb_ref, dgb_hbm)), (osem0, osem1, osem2), t // PUT_ROWS)
        dcw_ref[...] = jnp.zeros((16, 2 * CONV_W), F32)
        for j in range(CONV_TAPS):
            dcw_ref[j // 2:j // 2 + 1, CONV_W * (j % 2):CONV_W * (j % 2 + 1)] = jnp.sum(
                wacc_ref[8 * j:8 * (j + 1), :], axis=0, keepdims=True)

    vm = pl.BlockSpec(memory_space=pltpu.VMEM)
    hbm = pl.BlockSpec(memory_space=pl.ANY)
    return pl.pallas_call(
        body,
        name="conv_bwd",
        in_specs=[hbm] * 5 + [vm] * 3,
        out_specs=[hbm] * 3 + [vm] * 2,
        out_shape=[jax.ShapeDtypeStruct((t, CONV_W), BF16)] * 3
        + [jax.ShapeDtypeStruct((16, 2 * CONV_W), F32), jax.ShapeDtypeStruct((8, CONV_W), F32)],
        scratch_shapes=[pltpu.VMEM((t + CONV_PAD, CONV_W), F32), pltpu.VMEM((t + CONV_PAD, CONV_W), F32),
                        pltpu.VMEM((8, CONV_W // 128, CONV_CH + CONV_PAD, 128), F32), pltpu.VMEM((8 * 32, CONV_W), F32)]
        + [pltpu.VMEM((t, CONV_W), F32)] * 4 + [pltpu.VMEM((t, CONV_W), BF16)] * 4
        + [pltpu.SemaphoreType.DMA((5,))] + [pltpu.SemaphoreType.DMA((t // PUT_ROWS,))] * 3,
        compiler_params=_cparams(),
    )(ua, ug, gb, cz, dmix, cw, lw, lb)


def _out_proj(mix_a, mix_b, x, tgt, gate, w_out):
    t = x.shape[0]
    tm = 512
    nstep = t // tm

    def body(ma_ref, mb_ref, x_ref, t_ref, g_ref, w_ref, dout_ref, dma_ref, dmb_ref, gw_ref, red_ref, acc_ref):
        i = pl.program_id(0)

        @pl.when(i == 0)
        def _():
            acc_ref[...] = jnp.zeros_like(acc_ref)
            red_ref[...] = jnp.zeros_like(red_ref)

        mix = jnp.concatenate([ma_ref[...], mb_ref[...]], axis=1)
        y = jnp.dot(mix, w_ref[...], preferred_element_type=F32)
        gate_v = g_ref[...]
        err = x_ref[...] + gate_v * y - t_ref[...]
        dout = err * (1.0 / D_MODEL)
        dout_ref[...] = dout
        red_ref[0:1, :] += jnp.sum(dout * y, axis=0, keepdims=True)
        red_ref[1:2, :] += jnp.sum(err * err, axis=0, keepdims=True)
        dy = (dout * gate_v).astype(BF16)
        dmix = lax.dot_general(dy, w_ref[...], (((1,), (1,)), ((), ())), preferred_element_type=F32)
        dma_ref[...] = dmix[:, 0:512].astype(BF16)
        dmb_ref[...] = dmix[:, 512:1024].astype(BF16)
        acc_ref[...] += lax.dot_general(mix, dy, (((0,), (0,)), ((), ())), preferred_element_type=F32)

        @pl.when(i == nstep - 1)
        def _():
            gw_ref[...] = acc_ref[...].astype(BF16)

    row = lambda w: pl.BlockSpec((tm, w), lambda i: (i, 0))
    const = lambda s: pl.BlockSpec(s, lambda i: (0, 0))
    return pl.pallas_call(
        body,
        name="out_proj",
        grid=(nstep,),
        in_specs=[row(512), row(512), row(D_MODEL), row(D_MODEL), const((1, D_MODEL)),
                  pl.BlockSpec((D_MODEL, D_MODEL), lambda i: (0, 0), pipeline_mode=pl.Buffered(1))],
        out_specs=[row(D_MODEL), row(512), row(512), const((D_MODEL, D_MODEL)), const((8, D_MODEL))],
        out_shape=[jax.ShapeDtypeStruct((t, D_MODEL), F32), jax.ShapeDtypeStruct((t, 512), BF16),
                   jax.ShapeDtypeStruct((t, 512), BF16), jax.ShapeDtypeStruct((D_MODEL, D_MODEL), BF16),
                   jax.ShapeDtypeStruct((8, D_MODEL), F32)],
        scratch_shapes=[pltpu.VMEM((D_MODEL, D_MODEL), F32)],
        compiler_params=_cparams(dimension_semantics=("arbitrary",)),
    )(mix_a, mix_b, x, tgt, gate, w_out)


DPROJ_WIDTHS = (512, 256, 512, 512, 512, 512)
DPROJ_STARTS = (0, 512, 768, 1280, 1792, 2304)
WIN_W = 768
WIN_START = (0, 640, 1408, 2048)
WIN_OFF = (0, 64, 0, 64)
N_GW = N_CHIPS


def _window_pieces(s):
    lo, hi = WIN_START[s], WIN_START[s] + WIN_W
    out = []
    for p, (st, w) in enumerate(zip(DPROJ_STARTS, DPROJ_WIDTHS)):
        a, b = max(lo, st), min(hi, st + w)
        if a < b:
            out.append((p, a - st, b - a, a - lo))
    return out


def _in_proj_bwd(dparts, h, x, dout, s1, nw, wt_full, dcw, dvec, sm_a, row0):
    t = x.shape[0]
    tm = 256
    nstep = N_GW + t // tm
    n_sem = 20
    rows0 = 32
    hs = rows0 // 2
    npart = len(DPROJ_WIDTHS)

    def body(*refs):
        d_hbm, d_ref = refs[:npart], refs[npart:2 * npart]
        (x_ref, dout_ref, s1_ref, nw_ref, h_ref, wt_hbm, dcw_ref, dvec_ref, sma_ref, row0_ref,
         gx_ref, gw_hbm, ssum_ref, rows_ref,
         stg_ref, wt_ref, gt_ref, sib_ref, out_ref, in_ref, res_ref, sall_ref, red_ref, sm0_ref, ssib_ref, schip_ref, sres_ref,
         wsem, lsem, ssem, rsem) = refs[2 * npart:]
        i = pl.program_id(0)
        x_, y_, c, chips = _place()
        j = 2 * x_ + y_
        dev = 2 * j + c
        sib = (x_, y_, 1 - c)
        rc = functools.partial(_remote, ssem, rsem)
        rel_chip = [2 * cx + cy for cx, cy in chips] + [j]
        peers = [(px, py, pc) for px in (x_, 1 - x_) for py in (y_, 1 - y_) for pc in (c, 1 - c)][1:]
        wt_copy = pltpu.make_async_copy(wt_hbm, wt_ref, lsem.at[0])

        def window(case, slot):
            return [pltpu.make_async_copy(d_hbm[p].at[:, pl.ds(c0, w)], stg_ref.at[slot, :, pl.ds(w0, w)], wsem.at[slot, n])
                    for n, (p, c0, w, w0) in enumerate(_window_pieces(case))]

        def to_sibling(k):
            return rc(k, gt_ref.at[k, 1 - c], sib_ref.at[k], sib)

        def to_chip(k):
            return rc(4 + k, out_ref.at[k], in_ref.at[k], (*chips[k], c))

        def trade(k):
            to_sibling(k).wait_recv()

            def add(n, carry):
                rr = pl.ds(pl.multiple_of(n * RS_CH, RS_CH), RS_CH)
                out_ref[k, rr, :] = (gt_ref[k, c, rr, :].astype(F32) + sib_ref[k, rr, :].astype(F32)).astype(BF16)
                return carry

            lax.fori_loop(0, IN_HALF // RS_CH, add, 0)
            to_chip(k).start()

        def keep(k, first, vals):
            for half in range(2):
                lo, hi = max(first, IN_HALF * half), min(first + vals.shape[0], IN_HALF * (half + 1))
                if lo < hi:
                    gt_ref[k, half, lo - IN_HALF * half:hi - IN_HALF * half, :] = vals[lo - first:hi - first].astype(BF16)

        mine_s = pl.ds(pl.multiple_of(c * hs, 8), hs)
        other_s = pl.ds(pl.multiple_of((1 - c) * hs, 8), hs)

        def small_to_sibling():
            return rc(15, sm0_ref.at[other_s], ssib_ref, sib)

        def small_to_chip(k):
            return rc(16 + k, schip_ref.at[j], schip_ref.at[j], (*chips[k], c))

        def small_share():
            return rc(19, sres_ref.at[c], sres_ref.at[c], sib)

        for k in range(N_GW):
            @pl.when(i == k)
            def _(k=k):
                slot = k % 2
                if k == 0:
                    red_ref[...] = jnp.zeros_like(red_ref)
                    wt_copy.start()
                    sm0_ref[...] = jnp.zeros_like(sm0_ref)
                    sm0_ref[0:16, :] = dcw_ref[...]
                    sm0_ref[16:17, 0:CONV_W] = dvec_ref[0:1, :]
                    sm0_ref[16:17, CONV_W:2 * CONV_W] = dvec_ref[1:2, :]
                    sm0_ref[17:18, 0:CONV_W] = dvec_ref[2:3, :]
                    for r in range(3):
                        sm0_ref[17:18, CONV_W + 128 * r:CONV_W + 128 * (r + 1)] = sma_ref[r:r + 1, :]
                    sm0_ref[18:19, :] = row0_ref[1:2, :]
                    small_to_sibling().start()
                if k == 1:
                    small_to_sibling().wait_recv()
                    schip_ref[j] = sm0_ref[mine_s, :] + ssib_ref[...]
                    for kk in range(3):
                        small_to_chip(kk).start()
                if k == N_GW - 1:
                    for kk in range(3):
                        jk = rel_chip[kk]
                        rc(16 + kk, schip_ref.at[jk], schip_ref.at[jk], sib).wait_recv()
                    tot = schip_ref[0]
                    for d in range(1, N_CHIPS):
                        tot = tot + schip_ref[d]
                    sres_ref[c] = tot
                    small_share().start()
                for case in range(N_CHIPS):
                    if k == 0:
                        @pl.when(rel_chip[0] == case)
                        def _():
                            for cp in window(case, 0):
                                cp.start()
                    if k + 1 < N_GW:
                        @pl.when(rel_chip[k + 1] == case)
                        def _():
                            for cp in window(case, 1 - slot):
                                cp.start()
                for case in range(N_CHIPS):
                    @pl.when(rel_chip[k] == case)
                    def _():
                        for cp in window(case, slot):
                            cp.wait()
                for part in range(2):
                    cols = pl.ds(part * (WIN_W // 2), WIN_W // 2)
                    g = lax.dot_general(stg_ref[slot, :, cols], h_ref[...], (((0,), (0,)), ((), ())),
                                        preferred_element_type=F32)
                    for off in sorted(set(WIN_OFF)):
                        @pl.when(rel_chip[k] % 2 == (1 if off else 0))
                        def _():
                            keep(k, part * (WIN_W // 2) - off, g)
                    if part == 0 and k >= 1:
                        trade(k - 1)
                to_sibling(k).start()

        @pl.when(i == N_GW)
        def _():
            wt_copy.wait()

        @pl.when(i >= N_GW)
        def _():
            xv = x_ref[...]
            r = lax.rsqrt(jnp.mean(xv * xv, axis=-1, keepdims=True) + EPS)
            xh = xv * r
            n = xh * nw_ref[...]
            dproj = jnp.concatenate([ref[...] for ref in d_ref], axis=1)
            dh = jnp.dot(dproj, wt_ref[...], preferred_element_type=F32)
            red_ref[0:1, :] += jnp.sum(dh, axis=0, keepdims=True)
            red_ref[1:2, :] += jnp.sum(dh * n, axis=0, keepdims=True)
            dn = dh * s1_ref[...]
            red_ref[2:3, :] += jnp.sum(dn * xh, axis=0, keepdims=True)
            dxh = dn * nw_ref[...]
            gx_ref[...] = dout_ref[...] + r * (dxh - xh * jnp.mean(dxh * xh, axis=-1, keepdims=True))

        @pl.when(i == nstep - 1)
        def _():
            sall_ref[dev] = row0_ref[...]
            sall_ref[dev, 2:5, :] = red_ref[0:3, :]
            sends = [rc(8 + k, sall_ref.at[dev], sall_ref.at[dev], peer) for k, peer in enumerate(peers)]
            for cp in sends:
                cp.start()
            sends += [to_sibling(k) for k in range(N_GW)] + [to_chip(k) for k in range(3)]
            sends += [small_to_sibling(), small_share()] + [small_to_chip(k) for k in range(3)]
            own = N_GW - 1
            to_sibling(own).wait_recv()
            for k in range(3):
                to_chip(k).wait_recv()

            def total(n, carry):
                rr = pl.ds(pl.multiple_of(n * RS_CH, RS_CH), RS_CH)
                acc = gt_ref[own, c, rr, :].astype(F32) + sib_ref[own, rr, :].astype(F32)
                for k in range(3):
                    acc = acc + in_ref[k, rr, :].astype(F32)
                res_ref[c, rr, :] = acc
                return carry

            lax.fori_loop(0, IN_HALF // RS_CH, total, 0)
            share = rc(7, res_ref.at[c], res_ref.at[c], sib)
            share.start()
            sends.append(share)
            back = [pltpu.make_async_copy(res_ref.at[half], gw_hbm.at[half], lsem.at[1 + half]) for half in range(2)]
            for half in range(2):
                @pl.when(c == half)
                def _():
                    back[half].start()
            for k, (px, py, pc) in enumerate(peers):
                pdev = 4 * px + 2 * py + pc
                rc(8 + k, sall_ref.at[pdev], sall_ref.at[pdev], (px, py, pc)).wait_recv()
            rows_ref[...] = sall_ref[...]
            rc(19, sres_ref.at[1 - c], sres_ref.at[1 - c], sib).wait_recv()
            ssum_ref[0:hs, :] = sres_ref[0]
            ssum_ref[hs:rows0, :] = sres_ref[1]
            rc(7, res_ref.at[1 - c], res_ref.at[1 - c], sib).wait_recv()
            for half in range(2):
                @pl.when(c != half)
                def _():
                    back[half].start()
            for cp in sends:
                cp.wait_send()
            for cp in back:
                cp.wait()

    blk = lambda i: jnp.maximum(i - N_GW, 0)
    row = lambda w: pl.BlockSpec((tm, w), lambda i: (blk(i), 0))
    vec = pl.BlockSpec((1, D_MODEL), lambda i: (0, 0))
    const = lambda shape: pl.BlockSpec(shape, lambda i: (0,) * len(shape))
    hbm = pl.BlockSpec(memory_space=pl.ANY)
    return pl.pallas_call(
        body,
        name="in_proj_bwd",
        grid=(nstep,),
        in_specs=[hbm] * npart + [row(w) for w in DPROJ_WIDTHS] + [row(D_MODEL), row(D_MODEL), vec, vec,
                  pl.BlockSpec((t, D_MODEL), lambda i: (0, 0), pipeline_mode=pl.Buffered(1)), hbm, const((16, D_MODEL)),
                  const((8, CONV_W)), const((8, 128)), const((8, D_MODEL))],
        out_specs=[row(D_MODEL), hbm, const((rows0, D_MODEL)), const((N_DEV, 8, D_MODEL))],
        out_shape=[jax.ShapeDtypeStruct((t, D_MODEL), F32), jax.ShapeDtypeStruct((2, IN_HALF, D_MODEL), F32),
                   jax.ShapeDtypeStruct((rows0, D_MODEL), F32), jax.ShapeDtypeStruct((N_DEV, 8, D_MODEL), F32)],
        scratch_shapes=[pltpu.VMEM((2, t, WIN_W), BF16), pltpu.VMEM((IN_W, D_MODEL), BF16),
                        pltpu.VMEM((N_CHIPS, 2, IN_HALF, D_MODEL), BF16), pltpu.VMEM((N_CHIPS, IN_HALF, D_MODEL), BF16),
                        pltpu.VMEM((3, IN_HALF, D_MODEL), BF16), pltpu.VMEM((3, IN_HALF, D_MODEL), BF16),
                        pltpu.VMEM((2, IN_HALF, D_MODEL), F32), pltpu.VMEM((N_DEV, 8, D_MODEL), F32),
                        pltpu.VMEM((8, D_MODEL), F32), pltpu.VMEM((rows0, D_MODEL), F32), pltpu.VMEM((hs, D_MODEL), F32),
                        pltpu.VMEM((N_CHIPS, hs, D_MODEL), F32),
                        pltpu.VMEM((2, hs, D_MODEL), F32), pltpu.SemaphoreType.DMA((2, 3)), pltpu.SemaphoreType.DMA((3,)),
                        pltpu.SemaphoreType.DMA((n_sem,)), pltpu.SemaphoreType.DMA((n_sem,))],
        compiler_params=_cparams(dimension_semantics=("arbitrary",)),
    )(*dparts, *dparts, x, dout, s1, nw, h, wt_full, dcw, dvec, sm_a, row0)


MESH = pl.DeviceIdType.MESH


def _place():
    x, y, c = lax.axis_index("x"), lax.axis_index("y"), lax.axis_index("c")
    chips = [(1 - x, y), (x, 1 - y), (1 - x, 1 - y)]
    return x, y, c, chips


def _remote(sems_s, sems_r, k, src, dst, to):
    return pltpu.make_async_remote_copy(src_ref=src, dst_ref=dst, send_sem=sems_s.at[k], recv_sem=sems_r.at[k],
                                        device_id=to, device_id_type=MESH)


RS_CH = 32
RS_SEMS = 5


def _rs_to_sibling(rc, s0, theirs, sib_ref, sib):
    cp = rc(s0, theirs, sib_ref, sib)
    cp.start()
    return cp


def _rs_trade(rc, s0, theirs, mine, sib_ref, out_ref, in_ref, rows, c, sib, chips):
    rc(s0, theirs, sib_ref, sib).wait_recv()
    cps = []
    for k, (cx, cy) in enumerate(chips):
        jk = 2 * cx + cy

        def add(i, carry, jk=jk, k=k):
            rr = pl.ds(pl.multiple_of(i * RS_CH, RS_CH), RS_CH)
            out_ref[k, rr, :] = (mine[jk, rr, :].astype(F32) + sib_ref[jk, rr, :].astype(F32)).astype(BF16)
            return carry

        lax.fori_loop(0, rows // RS_CH, add, 0)
        cps.append(rc(s0 + 1 + k, out_ref.at[k], in_ref.at[k], (cx, cy, c)))
        cps[-1].start()
    return cps


def _rs_total(rc, s0, mine, sib_ref, out_ref, in_ref, res_ref, rows, j, c, sib):
    for k in range(3):
        rc(s0 + 1 + k, out_ref.at[k], in_ref.at[k], sib).wait_recv()

    def total(i, carry):
        rr = pl.ds(pl.multiple_of(i * RS_CH, RS_CH), RS_CH)
        acc = mine[j, rr, :].astype(F32) + sib_ref[j, rr, :].astype(F32)
        for k in range(3):
            acc = acc + in_ref[k, rr, :].astype(F32)
        res_ref[c, rr, :] = acc
        return carry

    lax.fori_loop(0, rows // RS_CH, total, 0)
    cp = rc(s0 + 4, res_ref.at[c], res_ref.at[c], sib)
    cp.start()
    return cp


def _rs_done(rc, s0, res_ref, c, sib):
    rc(s0 + 4, res_ref.at[1 - c], res_ref.at[1 - c], sib).wait_recv()


def _rs_scratch(rows):
    return [pltpu.VMEM((N_CHIPS, rows, D_MODEL), BF16), pltpu.VMEM((3, rows, D_MODEL), BF16),
            pltpu.VMEM((3, rows, D_MODEL), BF16)]


MAIN_W = 640
MAIN_DST = (((0, 0, 512), (1, 0, 128)), ((2, 0, 512), (3, 0, 128)), ((3, 128, 384), (4, 0, 256)), ((4, 384, 128), (5, 0, 512)))
PAIR_DST = ((1, 128, 128), (4, 256, 128))


def _in_proj_gather(x, wt, c_row, w_ada, b_ada, nw):
    t = x.shape[0]
    ch = 512
    n_sem = 16

    def body(x_hbm, wt_ref, c_ref, wada_ref, bada_ref, nw_ref,
             q_hbm, kv_hbm, ga_hbm, ua_hbm, ug_hbm, gb_hbm, h_hbm, w4_hbm, call_ref, ada_ref,
             x_ref, h_ref, w4_ref, stg_ref, pstg_ref, part_ref, lsem, osem, wsem, ssem, rsem):
        outs = (q_hbm, kv_hbm, ga_hbm, ua_hbm, ug_hbm, gb_hbm)
        x_, y_, c, chips = _place()
        j = 2 * x_ + y_
        dev = 2 * j + c
        sib = (x_, y_, 1 - c)
        idx = [2 * cx + cy for cx, cy in chips]
        rc = functools.partial(_remote, ssem, rsem)
        x_copy = pltpu.make_async_copy(x_hbm, x_ref, lsem.at[0])
        x_copy.start()

        def rows_of(s, cc):
            return pl.ds(pl.multiple_of(2 * IN_HALF * s + IN_HALF * cc, 16), IN_HALF)

        w4_ref[rows_of(j, 0), :] = wt_ref[0].astype(BF16)
        w4_ref[rows_of(j, 1), :] = wt_ref[1].astype(BF16)
        call_ref[dev] = c_ref[...]
        sends = []
        peers = [(px, py, pc) for px in (x_, 1 - x_) for py in (y_, 1 - y_) for pc in (c, 1 - c)][1:]
        for k, peer in enumerate(peers):
            sends.append(rc(k, call_ref.at[dev], call_ref.at[dev], peer))
        for cp in sends:
            cp.start()

        for k, (px, py, pc) in enumerate(peers):
            pdev = 4 * px + 2 * py + pc
            rc(k, call_ref.at[pdev], call_ref.at[pdev], (px, py, pc)).wait_recv()
        rowid = lax.broadcasted_iota(jnp.int32, (N_DEV, D_MODEL), 0)
        call = jnp.zeros((N_DEV, D_MODEL), F32)
        for r in range(N_DEV):
            call = jnp.where(rowid == r, jnp.broadcast_to(call_ref[r], (N_DEV, D_MODEL)), call)
        bsh = bada_ref[:, 0:ADA_SHARD]
        for k in range(1, N_CHIPS):
            bsh = jnp.where(j == k, bada_ref[:, ADA_SHARD * k:ADA_SHARD * (k + 1)], bsh)
        part = jnp.dot(_silu(call).astype(BF16), wada_ref[...].astype(BF16), preferred_element_type=F32) + bsh
        for r in range(N_DEV):
            part_ref[r] = part[r:r + 1, :]
        ada_ref[j] = part_ref[dev]
        for k, chip in enumerate(chips):
            sends.append(rc(13 + k, part_ref.at[2 * idx[k] + c], ada_ref.at[j], (*chip, c)))
            sends[-1].start()
        for k, chip in enumerate(chips):
            sends.append(rc(7 + k, w4_ref.at[rows_of(j, c)], w4_ref.at[rows_of(j, c)], (*chip, c)))
            sends[-1].start()

        x_copy.wait()

        def prenorm(i, carry):
            rr = pl.ds(pl.multiple_of(i * ch, ch), ch)
            xv = x_ref[rr, :]
            r = lax.rsqrt(jnp.mean(xv * xv, axis=-1, keepdims=True) + EPS)
            x_ref[rr, :] = (xv * r) * nw_ref[...]
            return carry

        lax.fori_loop(0, t // ch, prenorm, 0)
        for k in range(3):
            rc(13 + k, ada_ref.at[idx[k]], ada_ref.at[idx[k]], sib).wait_recv()

        shift = jnp.concatenate([ada_ref[0], ada_ref[1][:, 0:256]], axis=1)
        s1 = 1.0 + jnp.concatenate([ada_ref[1][:, 256:768], ada_ref[2][:, 0:512]], axis=1)

        def norm(i, carry):
            rr = pl.ds(pl.multiple_of(i * ch, ch), ch)
            h_ref[rr, :] = (x_ref[rr, :] * s1 + shift).astype(BF16)
            return carry

        lax.fori_loop(0, t // ch, norm, 0)
        h_copy = pltpu.make_async_copy(h_ref, h_hbm, lsem.at[1])
        h_copy.start()

        def put_main(case, slot):
            cps, col = [], 0
            for n, (a, c0, w) in enumerate(MAIN_DST[case]):
                cps.append(pltpu.make_async_copy(stg_ref.at[slot, :, pl.ds(col, w)], outs[a].at[:, pl.ds(c0, w)], osem.at[slot, n]))
                col += w
            return cps

        def put_pair(case, slot):
            a, c0, w = PAIR_DST[case]
            return pltpu.make_async_copy(pstg_ref.at[slot], outs[a].at[:, pl.ds(c0, w)], osem.at[slot, 2])

        def project(first_row, width, dst, slot):
            wrows = pl.ds(pl.multiple_of(first_row, 128), width)

            def blk(i, carry):
                rr = pl.ds(pl.multiple_of(i * ch, ch), ch)
                dst[slot, rr, :] = lax.dot_general(h_ref[rr, :], w4_ref[wrows, :], (((1,), (1,)), ((), ())),
                                                   preferred_element_type=F32)
                return carry

            lax.fori_loop(0, t // ch, blk, 0)

        def phase(p, s, pair):
            slot = p % 2
            if p >= 2:
                for case in range(N_CHIPS):
                    @pl.when(order[p - 2] == case)
                    def _():
                        for cp in put_main(case, slot):
                            cp.wait()
            if p == 3:
                for case in range(2):
                    @pl.when(j // 2 == case)
                    def _():
                        put_pair(case, 0).wait()
            project(2 * IN_HALF * s + 64 * (s % 2), MAIN_W, stg_ref, slot)
            for case in range(N_CHIPS):
                @pl.when(s == case)
                def _():
                    for cp in put_main(case, slot):
                        cp.start()
            if pair is not None:
                project(MAIN_W + 2 * (2 * IN_HALF) * pair, 128, pstg_ref, slot % 2 if p == 2 else 1)
                for case in range(2):
                    @pl.when(pair == case)
                    def _():
                        put_pair(case, 0 if p == 2 else 1).start()

        order = [j] + idx
        w_out = [pltpu.make_async_copy(w4_ref.at[pl.ds(pl.multiple_of(2 * IN_HALF * s, 32), 2 * IN_HALF)],
                                       w4_hbm.at[pl.ds(pl.multiple_of(2 * IN_HALF * s, 32), 2 * IN_HALF)], wsem.at[p])
                 for p, s in enumerate(order)]
        w_out[0].start()
        phase(0, j, None)
        passed = []
        for k in range(3):
            jk = idx[k]
            rc(7 + k, w4_ref.at[rows_of(jk, c)], w4_ref.at[rows_of(jk, c)], sib).wait_recv()
            passed.append(rc(10 + k, w4_ref.at[rows_of(jk, c)], w4_ref.at[rows_of(jk, c)], sib))
            passed[-1].start()
            rc(10 + k, w4_ref.at[rows_of(jk, 1 - c)], w4_ref.at[rows_of(jk, 1 - c)], sib).wait_recv()
            w_out[1 + k].start()
            if k == 0:
                phase(1, jk, None)
            elif k == 1:
                phase(2, jk, j // 2)
            else:
                phase(3, jk, 1 - j // 2)

        for case in range(N_CHIPS):
            for p in (2, 3):
                @pl.when(order[p] == case)
                def _():
                    for cp in put_main(case, p % 2):
                        cp.wait()
        for case in range(2):
            @pl.when(1 - j // 2 == case)
            def _():
                put_pair(case, 1).wait()
        h_copy.wait()
        for cp in w_out:
            cp.wait()
        for cp in sends + passed:
            cp.wait_send()

    vm = pl.BlockSpec(memory_space=pltpu.VMEM)
    hbm = pl.BlockSpec(memory_space=pl.ANY)
    widths = (512, 256, 512, 512, 512, 512)
    return pl.pallas_call(
        body,
        name="in_proj",
        in_specs=[hbm, vm, vm, vm, vm, vm],
        out_specs=[hbm] * 8 + [vm, vm],
        out_shape=[jax.ShapeDtypeStruct((t, w), F32) for w in widths]
        + [jax.ShapeDtypeStruct((t, D_MODEL), BF16), jax.ShapeDtypeStruct((IN_W, D_MODEL), BF16),
           jax.ShapeDtypeStruct((N_DEV, 1, D_MODEL), F32), jax.ShapeDtypeStruct((N_CHIPS, 1, ADA_SHARD), F32)],
        scratch_shapes=[pltpu.VMEM((t, D_MODEL), F32), pltpu.VMEM((t, D_MODEL), BF16), pltpu.VMEM((IN_W, D_MODEL), BF16),
                        pltpu.VMEM((2, t, MAIN_W), F32), pltpu.VMEM((2, t, 128), F32), pltpu.VMEM((N_DEV, 1, ADA_SHARD), F32),
                        pltpu.SemaphoreType.DMA((2,)), pltpu.SemaphoreType.DMA((2, 3)), pltpu.SemaphoreType.DMA((N_CHIPS,)),
                        pltpu.SemaphoreType.DMA((n_sem,)), pltpu.SemaphoreType.DMA((n_sem,))],
        compiler_params=_cparams(),
    )(x, wt, c_row, w_ada, b_ada, nw)


def _adamw_math(w, g, m, v):
    m2 = ADAM_B1 * m + (1.0 - ADAM_B1) * g
    v2 = ADAM_B2 * v + (1.0 - ADAM_B2) * (g * g)
    m_hat = m2 / (1.0 - ADAM_B1 ** ADAM_STEP)
    v_hat = v2 / (1.0 - ADAM_B2 ** ADAM_STEP)
    delta = -ADAM_LR * (m_hat / (jnp.sqrt(v_hat) + ADAM_EPS) + ADAM_WD * w)
    return delta, m2, v2


def _adamw(name, w, g, m, v, tm, through=None):
    r, cdim = w.shape
    nstep = r // tm
    extra = [] if through is None else [through]

    def body(w_ref, g_ref, m_ref, v_ref, *rest):
        g2_ref, d_ref, m2_ref, v2_ref = rest[len(extra):len(extra) + 4]
        g = g_ref[...]
        g2_ref[...] = g
        d_ref[...], m2_ref[...], v2_ref[...] = _adamw_math(w_ref[...], g, m_ref[...], v_ref[...])
        if extra:
            rest[-1][...] = rest[0][...]

    blk = pl.BlockSpec((tm, cdim), lambda i: (i, 0))
    eblk = [pl.BlockSpec((e.shape[0] // nstep, e.shape[1]), lambda i: (i, 0)) for e in extra]
    return pl.pallas_call(
        body,
        name=name,
        grid=(nstep,),
        in_specs=[blk] * 4 + eblk,
        out_specs=[blk] * 4 + eblk,
        out_shape=[jax.ShapeDtypeStruct((r, cdim), F32)] * 4 + [jax.ShapeDtypeStruct(e.shape, e.dtype) for e in extra],
        compiler_params=_cparams(dimension_semantics=("arbitrary",)),
    )(w, g, m, v, *extra)


def _adamw_ada(w, m, v, cact_t, dcols):
    r, cdim = w.shape
    tm = 256

    def body(w_ref, m_ref, v_ref, ct_ref, dc_ref, g_ref, d_ref, m2_ref, v2_ref):
        g = jnp.dot(ct_ref[...].astype(BF16), dc_ref[...].astype(BF16), preferred_element_type=F32)
        g_ref[...] = g
        d_ref[...], m2_ref[...], v2_ref[...] = _adamw_math(w_ref[...], g, m_ref[...], v_ref[...])

    blk = pl.BlockSpec((tm, cdim), lambda i: (i, 0))
    return pl.pallas_call(
        body,
        name="adamw_w_ada",
        grid=(r // tm,),
        in_specs=[blk] * 3 + [pl.BlockSpec((tm, N_DEV), lambda i: (i, 0)), pl.BlockSpec((N_DEV, cdim), lambda i: (0, 0))],
        out_specs=[blk] * 4,
        out_shape=[jax.ShapeDtypeStruct((r, cdim), F32)] * 4,
        compiler_params=_cparams(dimension_semantics=("arbitrary",)),
    )(w, m, v, cact_t, dcols)


def _adamw_small(ws, ms, vs, ssum, rows):
    n = len(ws)

    def body(*refs):
        w_r, m_r, v_r = refs[0:n], refs[n:2 * n], refs[2 * n:3 * n]
        ss_ref, rows_ref = refs[3 * n], refs[3 * n + 1]
        g_r, d_r, m2_r, v2_r = (refs[3 * n + 2 + k * n:3 * n + 2 + (k + 1) * n] for k in range(4))
        loss_ref = refs[7 * n + 2]
        j = 2 * lax.axis_index("x") + lax.axis_index("y")
        rsum = rows_ref[0]
        for d in range(1, N_DEV):
            rsum = rsum + rows_ref[d]
        taps = []
        for t in range(CONV_TAPS):
            row = ss_ref[t // 2:t // 2 + 1, :]
            c0 = CONV_W * (t % 2)
            pick = row[:, c0:c0 + 128]
            for k in range(1, N_CHIPS):
                pick = jnp.where(j == k, row[:, c0 + 128 * k:c0 + 128 * (k + 1)], pick)
            taps.append(pick)
        grads = [jnp.concatenate([rsum[2:3], rsum[3:4], rsum[0:1]], axis=1), rsum[4:5],
                 ss_ref[17:18, 512:512 + HEAD_DIM], ss_ref[17:18, 640:640 + HEAD_DIM], ss_ref[17:18, 768:776],
                 None, ss_ref[16:17, 0:CONV_W], ss_ref[16:17, CONV_W:2 * CONV_W], ss_ref[17:18, 0:CONV_W]]
        for i in range(n):
            if grads[i] is None:
                for t in range(CONV_TAPS):
                    g_r[i][t:t + 1, :] = taps[t]
                g = g_r[i][...]
            else:
                g = grads[i]
                g_r[i][...] = g
            d_r[i][...], m2_r[i][...], v2_r[i][...] = _adamw_math(w_r[i][...], g, m_r[i][...], v_r[i][...])
        loss_ref[...] = (0.5 / D_MODEL) * jnp.sum(ss_ref[18:19, :], axis=1, keepdims=True)

    vm = pl.BlockSpec(memory_space=pltpu.VMEM)
    shapes = [jax.ShapeDtypeStruct(w.shape, F32) for w in ws]
    out = pl.pallas_call(
        body,
        name="adamw_small",
        in_specs=[vm] * (3 * n + 2),
        out_specs=[vm] * (4 * n + 1),
        out_shape=shapes * 4 + [jax.ShapeDtypeStruct((1, 1), F32)],
        compiler_params=_cparams(),
    )(*ws, *ms, *vs, ssum, rows)
    return out[0:n], out[n:2 * n], out[2 * n:3 * n], out[3 * n:4 * n], out[4 * n]


def _rope_tables(t):
    inv = ROPE_THETA ** (-jnp.arange(0, HEAD_DIM, 2, dtype=F32) / HEAD_DIM)
    ang = jnp.arange(t, dtype=F32)[:, None] * inv[None, :]
    cos, sin = jnp.cos(ang), jnp.sin(ang)
    return jnp.tile(cos, (1, 4)), jnp.tile(jnp.concatenate([-sin, sin], axis=1), (1, 2))


def kernel(x, c, w_ada, b_ada, norm_w, w_in, q_norm_w, k_norm_w, sinks, conv_w, conv_b, ln_w, ln_b, w_out, loss_target, m_w_ada, m_b_ada, m_norm_w, m_w_in, m_q_norm_w, m_k_norm_w, m_sinks, m_conv_w, m_conv_b, m_ln_w, m_ln_b, m_w_out, v_w_ada, v_b_ada, v_norm_w, v_w_in, v_q_norm_w, v_k_norm_w, v_sinks, v_conv_w, v_conv_b, v_ln_w, v_ln_b, v_w_out):
    xi, yi = lax.axis_index("x"), lax.axis_index("y")
    j = 2 * xi + yi
    x2, tgt = x[0], loss_target[0]
    t = x2.shape[0]

    wt_s, mt_s, vt_s = w_in[0].T, m_w_in[0].T, v_w_in[0].T
    cw_pad = jnp.pad(conv_w[0], ((0, 1), (0, 0)))

    q_raw, kv_raw, ga, ua, ug, gb, h, w_full, call, ada4 = _in_proj_gather(
        x2, wt_s.reshape(2, IN_HALF, D_MODEL), c, w_ada[0], b_ada, norm_w)
    ada = ada4.reshape(1, 3 * D_MODEL)
    s1, gate = 1.0 + ada[:, D_MODEL:2 * D_MODEL], ada[:, 2 * D_MODEL:]

    cos_f, sin_s = _rope_tables(t)
    qw2, kw2 = jnp.tile(q_norm_w, (1, 2)), jnp.tile(k_norm_w, (1, 2))

    o, mix_a, wo4, cw4 = _attn_fwd(q_raw, kv_raw, ga, qw2, kw2, sinks, cos_f, sin_s,
                                   w_out[0].reshape(2, OUT_HALF, D_MODEL), cw_pad)
    w_out_full = wo4.reshape(D_MODEL, D_MODEL)
    cw_full = jnp.concatenate([cw4[i] for i in range(N_CHIPS)], axis=1)
    cz, mix_b = _conv_fwd(ua, ug, gb, cw_full, conv_b, ln_w, ln_b)
    dout, dmix_a, dmix_b, gwo_bf, red_o = _out_proj(mix_a, mix_b, x2, tgt, gate, w_out_full)

    dq, dkv, dga, sm_a, gwo = _attn_bwd(q_raw, kv_raw, ga, o, dmix_a, qw2, kw2, sinks, cos_f, sin_s,
                                        gwo_bf.reshape(N_CHIPS, 2, OUT_HALF, D_MODEL))
    dua, dug, dgb, dcw, dvec = _conv_bwd(ua, ug, gb, cz, dmix_b, cw_full, ln_w, ln_b)
    dparts = (dq, dkv, dga, dua, dug, dgb)

    grad_x, gw, ssum, rows = _in_proj_bwd(dparts, h, x2, dout, s1, norm_w, w_full, dcw, dvec, sm_a, red_o)

    gt_w_in = gw.reshape(2 * IN_HALF, D_MODEL)
    g_w_out = gwo.reshape(D_MODEL // N_CHIPS, D_MODEL)
    d_ada_all = jnp.concatenate([rows[:, 2], rows[:, 3], rows[:, 0]], axis=1)
    dcols = lax.dynamic_slice(d_ada_all, (0, ADA_SHARD * j), (N_DEV, ADA_SHARD))
    cact_t = jax.nn.silu(call.reshape(N_DEV, D_MODEL)).T

    g_w_ada, d_w_ada, nm_w_ada, nv_w_ada = _adamw_ada(w_ada[0], m_w_ada[0], v_w_ada[0], cact_t, dcols)
    gt_w_in, dt_w_in, nmt_w_in, nvt_w_in, grad_x = _adamw("adamw_w_in", wt_s, gt_w_in, mt_s, vt_s, 176, through=grad_x)
    g_w_in, d_w_in, nm_w_in, nv_w_in = gt_w_in.T, dt_w_in.T, nmt_w_in.T, nvt_w_in.T
    g_w_out, d_w_out, nm_w_out, nv_w_out = _adamw("adamw_w_out", w_out[0], g_w_out, m_w_out[0], v_w_out[0], 128)
    ws = [b_ada, norm_w, q_norm_w, k_norm_w, sinks, conv_w[0], conv_b, ln_w, ln_b]
    ms = [m_b_ada, m_norm_w, m_q_norm_w, m_k_norm_w, m_sinks, m_conv_w[0], m_conv_b, m_ln_w, m_ln_b]
    vs = [v_b_ada, v_norm_w, v_q_norm_w, v_k_norm_w, v_sinks, v_conv_w[0], v_conv_b, v_ln_w, v_ln_b]
    gs, ds, nms, nvs, loss11 = _adamw_small(ws, ms, vs, ssum, rows)
    loss = loss11[0, 0]

    def order(ada_v, in_v, out_v, sm):
        b, nw_, qw_, kw_, sk_, cw_, cb_, lw_, lb_ = sm
        return [ada_v[None], b, nw_, in_v[None], qw_, kw_, sk_, cw_[None], cb_, lw_, lb_, out_v[None]]

    grads = order(g_w_ada, g_w_in, g_w_out, gs)
    deltas = order(d_w_ada, d_w_in, d_w_out, ds)
    new_m = order(nm_w_ada, nm_w_in, nm_w_out, nms)
    new_v = order(nv_w_ada, nv_w_in, nv_w_out, nvs)
    return (loss, grad_x[None], *grads, *deltas, *new_m, *new_v)
```

```python
import functools

import jax
import jax.numpy as jnp
from jax import lax
from jax.experimental import pallas as pl
from jax.experimental.pallas import tpu as pltpu

F32 = jnp.float32
BF16 = jnp.bfloat16

D_MODEL = 1024
ATTN_W = 512
KV_W = 128
CONV_W = 512
IN_W = 2816
HEAD_DIM = 64
CONV_TAPS = 31
QBLK = 128
EPS = 1e-6
ROPE_THETA = 10000.0

ADAM_LR = 0.001
ADAM_B1 = 0.9
ADAM_B2 = 0.999
ADAM_EPS = 1e-08
ADAM_WD = 0.01
ADAM_STEP = 10

N_CHIPS = 4
N_DEV = 8
IN_HALF = IN_W // N_CHIPS // 2
OUT_HALF = D_MODEL // N_CHIPS // 2
ADA_SHARD = 3 * D_MODEL // N_CHIPS

VMEM_LIMIT = 56 * 1024 * 1024
CONV_PAD = 32


def _cparams(**kw):
    return pltpu.CompilerParams(vmem_limit_bytes=VMEM_LIMIT, **kw)


def _sigmoid(v):
    return 1.0 / (1.0 + jnp.exp(-v))


def _silu(v):
    return v * _sigmoid(v)


def _dsilu(v):
    s = _sigmoid(v)
    return s * (1.0 + v * (1.0 - s))


def _lane(shape):
    return lax.broadcasted_iota(jnp.int32, shape, len(shape) - 1)


PUT_ROWS = 512


def _fetch(hbm_refs, vmem_refs, sem):
    cps = [pltpu.make_async_copy(h, v, sem.at[i]) for i, (h, v) in enumerate(zip(hbm_refs, vmem_refs))]
    for cp in cps:
        cp.start()
    return cps


def _put(vmem_ref, hbm_ref, sem, m):
    r = pl.ds(pl.multiple_of(m * PUT_ROWS, PUT_ROWS), PUT_ROWS)
    return pltpu.make_async_copy(vmem_ref.at[r], hbm_ref.at[r], sem.at[m])


def _put_all(pairs, sems, m):
    for (v, h), sem in zip(pairs, sems):
        _put(v, h, sem, m).start()


def _put_wait(pairs, sems, n):
    for (v, h), sem in zip(pairs, sems):
        for m in range(n):
            _put(v, h, sem, m).wait()


def _head_mean(s, left):
    sl = jnp.sum(jnp.where(left, s, 0.0), axis=-1, keepdims=True)
    sr = jnp.sum(jnp.where(left, 0.0, s), axis=-1, keepdims=True)
    return jnp.where(left, sl, sr) * (1.0 / HEAD_DIM)


def _rot(v, first):
    return jnp.where(first, pltpu.roll(v, 96, 1), pltpu.roll(v, 32, 1))


def _norm_rope(v, w, cos, sin_s, left, first):
    r = lax.rsqrt(_head_mean(v * v, left) + EPS)
    xh = v * r
    n = xh * w
    return n * cos + _rot(n, first) * sin_s, xh, r


def _norm_rope_bwd(d, xh, r, w, cos, sin_s, left, first):
    dn = d * cos - _rot(d, first) * sin_s
    dw = jnp.sum(dn * xh, axis=0, keepdims=True)
    dxh = dn * w
    return r * (dxh - xh * _head_mean(dxh * xh, left)), dw


def _dup_heads(v, left):
    sw = pltpu.roll(v, 64, 1)
    return jnp.where(left, v, sw), jnp.where(left, sw, v)


def _prep_kv(kv_ref, kw_ref, cos_ref, sin_ref, ka_ref, va_ref, t):
    ch = 256
    for g in range(2):
        ka_ref[g, 0:QBLK, :] = jnp.zeros((QBLK, 128), BF16)
        va_ref[g, 0:QBLK, :] = jnp.zeros((QBLK, 128), BF16)

    def chunk(i, carry):
        r0 = pl.multiple_of(i * ch, ch)
        left = _lane((ch, 128)) < 64
        first = (_lane((ch, 128)) % 64) < 32
        k = kv_ref[pl.ds(r0, ch), 0:128]
        v = kv_ref[pl.ds(r0, ch), 128:256]
        kr, _, _ = _norm_rope(k, kw_ref[...], cos_ref[pl.ds(r0, ch), :], sin_ref[pl.ds(r0, ch), :], left, first)
        k0, k1 = _dup_heads(kr, left)
        v0, v1 = _dup_heads(v, left)
        ka_ref[0, pl.ds(QBLK + r0, ch), :] = k0.astype(BF16)
        ka_ref[1, pl.ds(QBLK + r0, ch), :] = k1.astype(BF16)
        va_ref[0, pl.ds(QBLK + r0, ch), :] = v0.astype(BF16)
        va_ref[1, pl.ds(QBLK + r0, ch), :] = v1.astype(BF16)
        return carry

    lax.fori_loop(0, t // ch, chunk, 0)


def _band_mask(n):
    qi = lax.broadcasted_iota(jnp.int32, (2 * QBLK, 2 * QBLK), 0) % QBLK
    kj = lax.broadcasted_iota(jnp.int32, (2 * QBLK, 2 * QBLK), 1)
    local = (kj > qi) & (kj <= qi + QBLK)
    return local & ((n > 0) | (kj >= QBLK))


def _softmax_pair(s, mask, sink0, sink1):
    row = lax.broadcasted_iota(jnp.int32, (2 * QBLK, 1), 0)
    sink = jnp.where(row < QBLK, sink0, sink1)
    s = jnp.where(mask, s, -jnp.inf)
    m = jnp.maximum(jnp.max(s, axis=-1, keepdims=True), sink)
    e = jnp.exp(s - m)
    es = jnp.exp(sink - m)
    inv = 1.0 / (jnp.sum(e, axis=-1, keepdims=True) + es)
    return e * inv, es * inv


def _stack_heads(v, left):
    return jnp.concatenate([jnp.where(left, v, 0.0), jnp.where(left, 0.0, v)], axis=0)


def _attn_fwd(q_raw, kv_raw, ga, qw2, kw2, sinks, cos_f, sin_s, wo, cw):
    t = q_raw.shape[0]
    nblk = t // QBLK
    per_put = PUT_ROWS // QBLK

    def body(q_hbm, kv_ref, ga_hbm, qw_ref, kw_ref, sk_ref, cos_hbm, sin_hbm, wo_ref, cw_ref,
             o_hbm, mix_hbm, wo4_ref, cw4_ref, ka_ref, va_ref, q_ref, ga_ref, o_ref, mix_ref, cos_ref, sin_ref,
             isem, osem0, osem1, ssem, rsem):
        loads = _fetch((cos_hbm, sin_hbm, q_hbm, ga_hbm), (cos_ref, sin_ref, q_ref, ga_ref), isem)
        outs, osems = ((o_ref, o_hbm), (mix_ref, mix_hbm)), (osem0, osem1)
        x, y, c, chips = _place()
        j = 2 * x + y
        sib = (x, y, 1 - c)
        idx = [2 * cx + cy for cx, cy in chips]
        rc = functools.partial(_remote, ssem, rsem)
        wo4_ref[j] = wo_ref[...].astype(BF16)
        for tap in range(CONV_TAPS):
            cw4_ref[j, tap:tap + 1, :] = cw_ref[tap]
        cw4_ref[j, CONV_TAPS:, :] = jnp.zeros((CONV_PAD - CONV_TAPS, 128), F32)
        sends = []
        for k, chip in enumerate(chips):
            sends.append(rc(k, wo4_ref.at[j, c], wo4_ref.at[j, c], (*chip, c)))
            sends.append(rc(6 + k, cw4_ref.at[j], cw4_ref.at[j], (*chip, c)))
        for cp in sends:
            cp.start()

        loads[0].wait()
        loads[1].wait()
        _prep_kv(kv_ref, kw_ref, cos_ref, sin_ref, ka_ref, va_ref, t)
        loads[2].wait()
        loads[3].wait()

        def blk(n, carry):
            r0 = pl.multiple_of(n * QBLK, QBLK)
            left = _lane((QBLK, 128)) < 64
            first = (_lane((QBLK, 128)) % 64) < 32
            cos = cos_ref[pl.ds(r0, QBLK), :]
            sin = sin_ref[pl.ds(r0, QBLK), :]
            mask = _band_mask(n)
            scores = []
            for p in range(4):
                lanes = slice(p * 128, (p + 1) * 128)
                qr, _, _ = _norm_rope(q_ref[pl.ds(r0, QBLK), lanes], qw_ref[...], cos, sin, left, first)
                q2 = _stack_heads(qr * 0.125, left).astype(BF16)
                scores.append(lax.dot_general(q2, ka_ref[p // 2, pl.ds(r0, 2 * QBLK), :], (((1,), (1,)), ((), ())),
                                              preferred_element_type=F32))
            probs = [_softmax_pair(scores[p], mask, sk_ref[0, 2 * p], sk_ref[0, 2 * p + 1])[0].astype(BF16)
                     for p in range(4)]
            for p in range(4):
                lanes = slice(p * 128, (p + 1) * 128)
                o2 = jnp.dot(probs[p], va_ref[p // 2, pl.ds(r0, 2 * QBLK), :], preferred_element_type=F32)
                o = jnp.where(left, o2[0:QBLK], o2[QBLK:2 * QBLK])
                o_ref[pl.ds(r0, QBLK), lanes] = o.astype(BF16)
                mix_ref[pl.ds(r0, QBLK), lanes] = (o * _silu(ga_ref[pl.ds(r0, QBLK), lanes])).astype(BF16)

            @pl.when(n % per_put == per_put - 1)
            def _():
                _put_all(outs, osems, n // per_put)

            return carry

        lax.fori_loop(0, nblk, blk, 0)
        _put_wait(outs, osems, t // PUT_ROWS)

        passed = []
        for k, chip in enumerate(chips):
            jk = idx[k]
            rc(k, wo4_ref.at[jk, c], wo4_ref.at[jk, c], sib).wait_recv()
            passed.append(rc(3 + k, wo4_ref.at[jk, c], wo4_ref.at[jk, c], sib))
            passed[-1].start()
        for k, chip in enumerate(chips):
            jk = idx[k]
            rc(3 + k, wo4_ref.at[jk, 1 - c], wo4_ref.at[jk, 1 - c], sib).wait_recv()
            rc(6 + k, cw4_ref.at[jk], cw4_ref.at[jk], sib).wait_recv()
        for cp in sends + passed:
            cp.wait_send()

    vm = pl.BlockSpec(memory_space=pltpu.VMEM)
    hbm = pl.BlockSpec(memory_space=pl.ANY)
    n_sem = 9
    return pl.pallas_call(
        body,
        name="attn_fwd",
        in_specs=[hbm, vm, hbm, vm, vm, pl.BlockSpec(memory_space=pltpu.SMEM), hbm, hbm, vm, vm],
        out_specs=[hbm, hbm, vm, vm],
        out_shape=[jax.ShapeDtypeStruct((t, ATTN_W), BF16), jax.ShapeDtypeStruct((t, ATTN_W), BF16),
                   jax.ShapeDtypeStruct((N_CHIPS, 2, OUT_HALF, D_MODEL), BF16),
                   jax.ShapeDtypeStruct((N_CHIPS, 32, 128), F32)],
        scratch_shapes=[pltpu.VMEM((2, t + QBLK, 128), BF16), pltpu.VMEM((2, t + QBLK, 128), BF16),
                        pltpu.VMEM((t, ATTN_W), F32), pltpu.VMEM((t, ATTN_W), F32),
                        pltpu.VMEM((t, ATTN_W), BF16), pltpu.VMEM((t, ATTN_W), BF16),
                        pltpu.VMEM((t, 128), F32), pltpu.VMEM((t, 128), F32),
                        pltpu.SemaphoreType.DMA((4,)), pltpu.SemaphoreType.DMA((t // PUT_ROWS,)),
                        pltpu.SemaphoreType.DMA((t // PUT_ROWS,)),
                        pltpu.SemaphoreType.DMA((n_sem,)), pltpu.SemaphoreType.DMA((n_sem,))],
        compiler_params=_cparams(),
    )(q_raw, kv_raw, ga, qw2, kw2, sinks, cos_f, sin_s, wo, cw)


def _attn_bwd(q_raw, kv_raw, ga, o, dmix, qw2, kw2, sinks, cos_f, sin_s, go):
    t = q_raw.shape[0]
    nblk = t // QBLK
    per_put = PUT_ROWS // QBLK

    def body(q_hbm, kv_ref, ga_hbm, o_hbm, dm_hbm, qw_ref, kw_ref, sk_ref, cos_hbm, sin_hbm, go_ref,
             dq_hbm, dkv_ref, dga_hbm, sm_ref, gwo_ref, ka_ref, va_ref, dka_ref, dva_ref,
             sibo_ref, outo_ref, ino_ref, q_ref, ga_ref, o_ref, dm_ref, dq_ref, dga_ref, cos_ref, sin_ref,
             isem, osem0, osem1, ssem, rsem):
        loads = _fetch((cos_hbm, sin_hbm, q_hbm, ga_hbm, o_hbm, dm_hbm), (cos_ref, sin_ref, q_ref, ga_ref, o_ref, dm_ref), isem)
        outs, osems = ((dq_ref, dq_hbm), (dga_ref, dga_hbm)), (osem0, osem1)
        x, y, c, chips = _place()
        sib = (x, y, 1 - c)
        rc = functools.partial(_remote, ssem, rsem)
        theirs, mine = go_ref.at[:, 1 - c], go_ref.at[:, c]
        sends = [_rs_to_sibling(rc, 0, theirs, sibo_ref, sib)]
        loads[0].wait()
        loads[1].wait()
        _prep_kv(kv_ref, kw_ref, cos_ref, sin_ref, ka_ref, va_ref, t)
        dka_ref[...] = jnp.zeros_like(dka_ref)
        dva_ref[...] = jnp.zeros_like(dva_ref)
        sends += _rs_trade(rc, 0, theirs, mine, sibo_ref, outo_ref, ino_ref, OUT_HALF, c, sib, chips)
        for cp in loads[2:]:
            cp.wait()

        def blk(n, carry):
            dqw, dsk = carry
            r0 = pl.multiple_of(n * QBLK, QBLK)
            left = _lane((QBLK, 128)) < 64
            first = (_lane((QBLK, 128)) % 64) < 32
            cos = cos_ref[pl.ds(r0, QBLK), :]
            sin = sin_ref[pl.ds(r0, QBLK), :]
            mask = _band_mask(n)
            row = lax.broadcasted_iota(jnp.int32, (2 * QBLK, 1), 0)
            rows = pl.ds(r0, QBLK)
            win = pl.ds(r0, 2 * QBLK)
            lane_of = [slice(p * 128, (p + 1) * 128) for p in range(4)]
            for grp in ((0, 1), (2, 3)):
                qn = {p: _norm_rope(q_ref[rows, lane_of[p]], qw_ref[...], cos, sin, left, first) for p in grp}
                q2 = {p: _stack_heads(qn[p][0] * 0.125, left).astype(BF16) for p in grp}
                sc = {p: lax.dot_general(q2[p], ka_ref[p // 2, win, :], (((1,), (1,)), ((), ())),
                                         preferred_element_type=F32) for p in grp}
                do2 = {}
                for p in grp:
                    gav = ga_ref[rows, lane_of[p]]
                    dmv = dm_ref[rows, lane_of[p]].astype(F32)
                    dga_ref[rows, lane_of[p]] = (dmv * o_ref[rows, lane_of[p]].astype(F32) * _dsilu(gav)).astype(BF16)
                    do2[p] = _stack_heads(dmv * _silu(gav), left).astype(BF16)
                dpm = {p: lax.dot_general(do2[p], va_ref[p // 2, win, :], (((1,), (1,)), ((), ())),
                                          preferred_element_type=F32) for p in grp}
                sm = {p: _softmax_pair(sc[p], mask, sk_ref[0, 2 * p], sk_ref[0, 2 * p + 1]) for p in grp}
                dsl = {}
                for p in grp:
                    pm, ps = sm[p]
                    delta = jnp.sum(pm * dpm[p], axis=-1, keepdims=True)
                    dsl[p] = (pm * (dpm[p] - delta)).astype(BF16)
                    pd = ps * delta
                    d0 = jnp.sum(jnp.where(row < QBLK, pd, 0.0), axis=0, keepdims=True)
                    d1 = jnp.sum(jnp.where(row < QBLK, 0.0, pd), axis=0, keepdims=True)
                    l8 = _lane((1, 128))
                    dsk = dsk - jnp.where(l8 == 2 * p, d0, 0.0) - jnp.where(l8 == 2 * p + 1, d1, 0.0)
                for p in grp:
                    g = p // 2
                    dva_ref[g, win, :] += lax.dot_general(sm[p][0].astype(BF16), do2[p], (((0,), (0,)), ((), ())),
                                                          preferred_element_type=F32)
                    dka_ref[g, win, :] += lax.dot_general(dsl[p], q2[p], (((0,), (0,)), ((), ())),
                                                          preferred_element_type=F32)
                for p in grp:
                    dq2 = jnp.dot(dsl[p], ka_ref[p // 2, win, :], preferred_element_type=F32)
                    dqr = jnp.where(left, dq2[0:QBLK], dq2[QBLK:2 * QBLK]) * 0.125
                    dq, dw = _norm_rope_bwd(dqr, qn[p][1], qn[p][2], qw_ref[...], cos, sin, left, first)
                    dq_ref[rows, lane_of[p]] = dq.astype(BF16)
                    dqw = dqw + dw

            @pl.when(n % per_put == per_put - 1)
            def _():
                _put_all(outs, osems, n // per_put)

            return dqw, dsk

        zero = jnp.zeros((1, 128), F32)
        dqw, dsk = lax.fori_loop(0, nblk, blk, (zero, zero))

        ch = 256

        def chunk(i, dkw):
            r0 = pl.multiple_of(i * ch, ch)
            left = _lane((ch, 128)) < 64
            first = (_lane((ch, 128)) % 64) < 32
            rows = pl.ds(r0, ch)
            prow = pl.ds(QBLK + r0, ch)

            def fold(ref):
                a0 = ref[0, prow, :]
                a1 = ref[1, prow, :]
                return jnp.where(left, a0 + pltpu.roll(a0, 64, 1), a1 + pltpu.roll(a1, 64, 1))

            cos = cos_ref[rows, :]
            sin = sin_ref[rows, :]
            _, xh, r = _norm_rope(kv_ref[rows, 0:128], kw_ref[...], cos, sin, left, first)
            dk, dw = _norm_rope_bwd(fold(dka_ref), xh, r, kw_ref[...], cos, sin, left, first)
            dkv_ref[rows, 0:128] = dk.astype(BF16)
            dkv_ref[rows, 128:256] = fold(dva_ref).astype(BF16)
            return dkw + dw

        dkw = lax.fori_loop(0, t // ch, chunk, zero)
        sm_ref[...] = jnp.zeros((8, 128), F32)
        sm_ref[0:1, :] = dqw + pltpu.roll(dqw, 64, 1)
        sm_ref[1:2, :] = dkw + pltpu.roll(dkw, 64, 1)
        sm_ref[2:3, :] = dsk

        j = 2 * x + y
        sends.append(_rs_total(rc, 0, mine, sibo_ref, outo_ref, ino_ref, gwo_ref, OUT_HALF, j, c, sib))
        _rs_done(rc, 0, gwo_ref, c, sib)
        for cp in sends:
            cp.wait_send()
        _put_wait(outs, osems, t // PUT_ROWS)

    vm = pl.BlockSpec(memory_space=pltpu.VMEM)
    hbm = pl.BlockSpec(memory_space=pl.ANY)
    return pl.pallas_call(
        body,
        name="attn_bwd",
        in_specs=[hbm, vm, hbm, hbm, hbm, vm, vm, pl.BlockSpec(memory_space=pltpu.SMEM), hbm, hbm, vm],
        out_specs=[hbm, vm, hbm, vm, vm],
        out_shape=[jax.ShapeDtypeStruct((t, ATTN_W), BF16), jax.ShapeDtypeStruct((t, 2 * KV_W), BF16),
                   jax.ShapeDtypeStruct((t, ATTN_W), BF16), jax.ShapeDtypeStruct((8, 128), F32),
                   jax.ShapeDtypeStruct((2, OUT_HALF, D_MODEL), F32)],
        scratch_shapes=[pltpu.VMEM((2, t + QBLK, 128), BF16), pltpu.VMEM((2, t + QBLK, 128), BF16),
                        pltpu.VMEM((2, t + QBLK, 128), F32), pltpu.VMEM((2, t + QBLK, 128), F32)]
        + _rs_scratch(OUT_HALF)
        + [pltpu.VMEM((t, ATTN_W), F32), pltpu.VMEM((t, ATTN_W), F32), pltpu.VMEM((t, ATTN_W), BF16),
           pltpu.VMEM((t, ATTN_W), BF16), pltpu.VMEM((t, ATTN_W), BF16), pltpu.VMEM((t, ATTN_W), BF16),
           pltpu.VMEM((t, 128), F32), pltpu.VMEM((t, 128), F32),
           pltpu.SemaphoreType.DMA((6,)), pltpu.SemaphoreType.DMA((t // PUT_ROWS,)), pltpu.SemaphoreType.DMA((t // PUT_ROWS,)),
           pltpu.SemaphoreType.DMA((RS_SEMS,)), pltpu.SemaphoreType.DMA((RS_SEMS,))],
        compiler_params=_cparams(),
    )(q_raw, kv_raw, ga, o, dmix, qw2, kw2, sinks, cos_f, sin_s, go)


CONV_CH = 256
CONV_SUB = 128
CONV_ACCS = 1


def _shifted_windows(src_ref, r0, sh_ref):
    rows = CONV_CH + CONV_PAD
    win = src_ref[pl.ds(r0, rows), :]
    for b in range(8):
        sh = win if b == 0 else pltpu.roll(win, rows - b, 0)
        for c in range(CONV_W // 128):
            sh_ref[b, c] = sh[:, c * 128:(c + 1) * 128]


def _conv_fwd(ua, ug, gb, cw, cb, lw, lb):
    t = ua.shape[0]

    def body(ua_hbm, ug_hbm, gb_hbm, cw_ref, cb_ref, lw_ref, lb_ref, cz_hbm, mix_hbm, zp_ref, sh_ref,
             ua_ref, ug_ref, gb_ref, cz_ref, mix_ref, isem, osem0, osem1):
        loads = _fetch((ua_hbm, ug_hbm, gb_hbm), (ua_ref, ug_ref, gb_ref), isem)
        outs, osems = ((cz_ref, cz_hbm), (mix_ref, mix_hbm)), (osem0, osem1)
        per_put = PUT_ROWS // CONV_CH
        zp_ref[0:CONV_PAD, :] = jnp.zeros((CONV_PAD, CONV_W), F32)
        loads[0].wait()
        loads[1].wait()

        def glu(i, carry):
            r0 = pl.multiple_of(i * CONV_CH, CONV_CH)
            rows = pl.ds(r0, CONV_CH)
            zp_ref[pl.ds(CONV_PAD + r0, CONV_CH), :] = ua_ref[rows, :] * _sigmoid(ug_ref[rows, :])
            return carry

        lax.fori_loop(0, t // CONV_CH, glu, 0)
        loads[2].wait()

        def chunk(i, carry):
            r0 = pl.multiple_of(i * CONV_CH, CONV_CH)
            _shifted_windows(zp_ref, r0, sh_ref)
            for c in range(CONV_W // 128):
                lanes = slice(c * 128, (c + 1) * 128)

                def sub(k, carry2):
                    b0 = pl.multiple_of(k * CONV_SUB, CONV_SUB)
                    acc = [jnp.broadcast_to(cb_ref[0:1, lanes], (CONV_SUB, 128))] + [None] * (CONV_ACCS - 1)
                    for j in range(CONV_TAPS):
                        off = j + CONV_PAD - (CONV_TAPS - 1)
                        term = sh_ref[off % 8, c, pl.ds(b0 + 8 * (off // 8), CONV_SUB), :] * cw_ref[j:j + 1, lanes]
                        acc[j % CONV_ACCS] = term if acc[j % CONV_ACCS] is None else acc[j % CONV_ACCS] + term
                    cz_ref[pl.ds(r0 + b0, CONV_SUB), lanes] = functools.reduce(lambda a, b: a + b, acc)
                    return carry2

                lax.fori_loop(0, CONV_CH // CONV_SUB, sub, 0)
            rows = pl.ds(r0, CONV_CH)
            cz = cz_ref[rows, :]
            mu = jnp.mean(cz, axis=-1, keepdims=True)
            xc = cz - mu
            rs = lax.rsqrt(jnp.mean(xc * xc, axis=-1, keepdims=True) + EPS)
            ln = xc * rs * lw_ref[...] + lb_ref[...]
            mix_ref[rows, :] = (_silu(ln) * _silu(gb_ref[rows, :])).astype(BF16)

            @pl.when(i % per_put == per_put - 1)
            def _():
                _put_all(outs, osems, i // per_put)

            return carry

        lax.fori_loop(0, t // CONV_CH, chunk, 0)
        _put_wait(outs, osems, t // PUT_ROWS)

    vm = pl.BlockSpec(memory_space=pltpu.VMEM)
    hbm = pl.BlockSpec(memory_space=pl.ANY)
    nput = t // PUT_ROWS
    return pl.pallas_call(
        body,
        name="conv_fwd",
        in_specs=[hbm] * 3 + [vm] * 4,
        out_specs=[hbm, hbm],
        out_shape=[jax.ShapeDtypeStruct((t, CONV_W), F32), jax.ShapeDtypeStruct((t, CONV_W), BF16)],
        scratch_shapes=[pltpu.VMEM((t + CONV_PAD, CONV_W), F32),
                        pltpu.VMEM((8, CONV_W // 128, CONV_CH + CONV_PAD, 128), F32),
                        pltpu.VMEM((t, CONV_W), F32), pltpu.VMEM((t, CONV_W), F32), pltpu.VMEM((t, CONV_W), F32),
                        pltpu.VMEM((t, CONV_W), F32), pltpu.VMEM((t, CONV_W), BF16),
                        pltpu.SemaphoreType.DMA((3,)), pltpu.SemaphoreType.DMA((nput,)), pltpu.SemaphoreType.DMA((nput,))],
        compiler_params=_cparams(),
    )(ua, ug, gb, cw, cb, lw, lb)


def _conv_bwd(ua, ug, gb, cz, dmix, cw, lw, lb):
    t = ua.shape[0]

    def body(ua_hbm, ug_hbm, gb_hbm, cz_hbm, dm_hbm, cw_ref, lw_ref, lb_ref,
             dua_hbm, dug_hbm, dgb_hbm, dcw_ref, dvec_ref, zp_ref, dp_ref, sh_ref, wacc_ref,
             ua_ref, ug_ref, gb_ref, cz_ref, dm_ref, dua_ref, dug_ref, dgb_ref, isem, osem0, osem1, osem2):
        loads = _fetch((ua_hbm, ug_hbm, gb_hbm, cz_hbm, dm_hbm), (ua_ref, ug_ref, gb_ref, cz_ref, dm_ref), isem)
        per_put = PUT_ROWS // CONV_CH
        zp_ref[0:CONV_PAD, :] = jnp.zeros((CONV_PAD, CONV_W), F32)
        dp_ref[t:t + CONV_PAD, :] = jnp.zeros((CONV_PAD, CONV_W), F32)
        wacc_ref[...] = jnp.zeros_like(wacc_ref)
        for cp in loads:
            cp.wait()

        def pointwise(i, carry):
            dcb, dlw, dlb = carry
            r0 = pl.multiple_of(i * CONV_CH, CONV_CH)
            rows = pl.ds(r0, CONV_CH)
            zp_ref[pl.ds(CONV_PAD + r0, CONV_CH), :] = ua_ref[rows, :] * _sigmoid(ug_ref[rows, :])
            cz = cz_ref[rows, :]
            mu = jnp.mean(cz, axis=-1, keepdims=True)
            xc = cz - mu
            rs = lax.rsqrt(jnp.mean(xc * xc, axis=-1, keepdims=True) + EPS)
            xh = xc * rs
            ln = xh * lw_ref[...] + lb_ref[...]
            gbv = gb_ref[rows, :]
            dy = dm_ref[rows, :].astype(F32)
            dgb_ref[rows, :] = (dy * _silu(ln) * _dsilu(gbv)).astype(BF16)
            dl = dy * _silu(gbv) * _dsilu(ln)
            dxh = dl * lw_ref[...]
            dcz = rs * (dxh - jnp.mean(dxh, axis=-1, keepdims=True)
                        - xh * jnp.mean(dxh * xh, axis=-1, keepdims=True))
            dp_ref[rows, :] = dcz

            @pl.when(i % per_put == per_put - 1)
            def _():
                _put(dgb_ref, dgb_hbm, osem2, i // per_put).start()

            return (dcb + jnp.sum(dcz, axis=0, keepdims=True),
                    dlw + jnp.sum(dl * xh, axis=0, keepdims=True),
                    dlb + jnp.sum(dl, axis=0, keepdims=True))

        zero = jnp.zeros((1, CONV_W), F32)
        dcb, dlw, dlb = lax.fori_loop(0, t // CONV_CH, pointwise, (zero, zero, zero))
        dvec_ref[...] = jnp.zeros((8, CONV_W), F32)
        dvec_ref[0:1, :] = dcb
        dvec_ref[1:2, :] = dlw
        dvec_ref[2:3, :] = dlb

        def chunk(i, carry):
            r0 = pl.multiple_of(i * CONV_CH, CONV_CH)
            _shifted_windows(dp_ref, r0, sh_ref)
            for c in range(CONV_W // 128):
                lanes = slice(c * 128, (c + 1) * 128)

                def sub(k, carry2):
                    b0 = pl.multiple_of(k * CONV_SUB, CONV_SUB)
                    acc = [None] * CONV_ACCS
                    for j in range(CONV_TAPS):
                        off = CONV_TAPS - 1 - j
                        term = sh_ref[off % 8, c, pl.ds(b0 + 8 * (off // 8), CONV_SUB), :] * cw_ref[j:j + 1, lanes]
                        acc[j % CONV_ACCS] = term if acc[j % CONV_ACCS] is None else acc[j % CONV_ACCS] + term
                    acc = functools.reduce(lambda a, b: a + b, acc)
                    rr = pl.ds(r0 + b0, CONV_SUB)
                    sg = _sigmoid(ug_ref[rr, lanes])
                    dua_ref[rr, lanes] = (acc * sg).astype(BF16)
                    dug_ref[rr, lanes] = (acc * ua_ref[rr, lanes] * sg * (1.0 - sg)).astype(BF16)
                    return carry2

                lax.fori_loop(0, CONV_CH // CONV_SUB, sub, 0)
            _shifted_windows(zp_ref, r0, sh_ref)
            for c in range(CONV_W // 128):
                lanes = slice(c * 128, (c + 1) * 128)

                def subw(k, carry2):
                    b0 = pl.multiple_of(k * CONV_SUB, CONV_SUB)
                    dcz = dp_ref[pl.ds(r0 + b0, CONV_SUB), lanes]
                    for j in range(CONV_TAPS):
                        off = j + CONV_PAD - (CONV_TAPS - 1)
                        pr = dcz * sh_ref[off % 8, c, pl.ds(b0 + 8 * (off // 8), CONV_SUB), :]
                        parts = [pr[8 * q:8 * (q + 1)] for q in range(CONV_SUB // 8)]
                        while len(parts) > 1:
                            parts = [a + b for a, b in zip(parts[0::2], parts[1::2])]
                        wacc_ref[8 * j:8 * (j + 1), lanes] += parts[0]
                    return carry2

                lax.fori_loop(0, CONV_CH // CONV_SUB, subw, 0)

            @pl.when(i % per_put == per_put - 1)
            def _():
                _put_all(((dua_ref, dua_hbm), (dug_ref, dug_hbm)), (osem0, osem1), i // per_put)

            return carry

        lax.fori_loop(0, t // CONV_CH, chunk, 0)
        _put_wait(((dua_ref, dua_hbm), (dug_ref, dug_hbm), (dgb_ref, dgb_hbm)), (osem0, osem1, osem2), t // PUT_ROWS)
        dcw_ref[...] = jnp.zeros((16, 2 * CONV_W), F32)
        for j in range(CONV_TAPS):
            dcw_ref[j // 2:j // 2 + 1, CONV_W * (j % 2):CONV_W * (j % 2 + 1)] = jnp.sum(
                wacc_ref[8 * j:8 * (j + 1), :], axis=0, keepdims=True)

    vm = pl.BlockSpec(memory_space=pltpu.VMEM)
    hbm = pl.BlockSpec(memory_space=pl.ANY)
    return pl.pallas_call(
        body,
        name="conv_bwd",
        in_specs=[hbm] * 5 + [vm] * 3,
        out_specs=[hbm] * 3 + [vm] * 2,
        out_shape=[jax.ShapeDtypeStruct((t, CONV_W), BF16)] * 3
        + [jax.ShapeDtypeStruct((16, 2 * CONV_W), F32), jax.ShapeDtypeStruct((8, CONV_W), F32)],
        scratch_shapes=[pltpu.VMEM((t + CONV_PAD, CONV_W), F32), pltpu.VMEM((t + CONV_PAD, CONV_W), F32),
                        pltpu.VMEM((8, CONV_W // 128, CONV_CH + CONV_PAD, 128), F32), pltpu.VMEM((8 * 32, CONV_W), F32)]
        + [pltpu.VMEM((t, CONV_W), F32)] * 4 + [pltpu.VMEM((t, CONV_W), BF16)] * 4
        + [pltpu.SemaphoreType.DMA((5,))] + [pltpu.SemaphoreType.DMA((t // PUT_ROWS,))] * 3,
        compiler_params=_cparams(),
    )(ua, ug, gb, cz, dmix, cw, lw, lb)


def _out_proj(mix_a, mix_b, x, tgt, gate, w_out):
    t = x.shape[0]
    tm = 512
    nstep = t // tm

    def body(ma_ref, mb_ref, x_ref, t_ref, g_ref, w_ref, dout_ref, dma_ref, dmb_ref, gw_ref, red_ref, acc_ref):
        i = pl.program_id(0)

        @pl.when(i == 0)
        def _():
            acc_ref[...] = jnp.zeros_like(acc_ref)
            red_ref[...] = jnp.zeros_like(red_ref)

        mix = jnp.concatenate([ma_ref[...], mb_ref[...]], axis=1)
        y = jnp.dot(mix, w_ref[...], preferred_element_type=F32)
        gate_v = g_ref[...]
        err = x_ref[...] + gate_v * y - t_ref[...]
        dout = err * (1.0 / D_MODEL)
        dout_ref[...] = dout
        red_ref[0:1, :] += jnp.sum(dout * y, axis=0, keepdims=True)
        red_ref[1:2, :] += jnp.sum(err * err, axis=0, keepdims=True)
        dy = (dout * gate_v).astype(BF16)
        dmix = lax.dot_general(dy, w_ref[...], (((1,), (1,)), ((), ())), preferred_element_type=F32)
        dma_ref[...] = dmix[:, 0:512].astype(BF16)
        dmb_ref[...] = dmix[:, 512:1024].astype(BF16)
        acc_ref[...] += lax.dot_general(mix, dy, (((0,), (0,)), ((), ())), preferred_element_type=F32)

        @pl.when(i == nstep - 1)
        def _():
            gw_ref[...] = acc_ref[...].astype(BF16)

    row = lambda w: pl.BlockSpec((tm, w), lambda i: (i, 0))
    const = lambda s: pl.BlockSpec(s, lambda i: (0, 0))
    return pl.pallas_call(
        body,
        name="out_proj",
        grid=(nstep,),
        in_specs=[row(512), row(512), row(D_MODEL), row(D_MODEL), const((1, D_MODEL)),
                  pl.BlockSpec((D_MODEL, D_MODEL), lambda i: (0, 0), pipeline_mode=pl.Buffered(1))],
        out_specs=[row(D_MODEL), row(512), row(512), const((D_MODEL, D_MODEL)), const((8, D_MODEL))],
        out_shape=[jax.ShapeDtypeStruct((t, D_MODEL), F32), jax.ShapeDtypeStruct((t, 512), BF16),
                   jax.ShapeDtypeStruct((t, 512), BF16), jax.ShapeDtypeStruct((D_MODEL, D_MODEL), BF16),
                   jax.ShapeDtypeStruct((8, D_MODEL), F32)],
        scratch_shapes=[pltpu.VMEM((D_MODEL, D_MODEL), F32)],
        compiler_params=_cparams(dimension_semantics=("arbitrary",)),
    )(mix_a, mix_b, x, tgt, gate, w_out)


DPROJ_WIDTHS = (512, 256, 512, 512, 512, 512)
DPROJ_STARTS = (0, 512, 768, 1280, 1792, 2304)
WIN_W = 768
WIN_START = (0, 640, 1408, 2048)
WIN_OFF = (0, 64, 0, 64)
N_GW = N_CHIPS


def _window_pieces(s):
    lo, hi = WIN_START[s], WIN_START[s] + WIN_W
    out = []
    for p, (st, w) in enumerate(zip(DPROJ_STARTS, DPROJ_WIDTHS)):
        a, b = max(lo, st), min(hi, st + w)
        if a < b:
            out.append((p, a - st, b - a, a - lo))
    return out


def _in_proj_bwd(dparts, h, x, dout, s1, nw, wt_full, dcw, dvec, sm_a, row0):
    t = x.shape[0]
    tm = 256
    nstep = N_GW + t // tm
    n_sem = 20
    rows0 = 32
    hs = rows0 // 2
    npart = len(DPROJ_WIDTHS)

    def body(*refs):
        d_hbm, d_ref = refs[:npart], refs[npart:2 * npart]
        (x_ref, dout_ref, s1_ref, nw_ref, h_ref, wt_hbm, dcw_ref, dvec_ref, sma_ref, row0_ref,
         gx_ref, gw_hbm, ssum_ref, rows_ref,
         stg_ref, wt_ref, gt_ref, sib_ref, out_ref, in_ref, res_ref, sall_ref, red_ref, sm0_ref, ssib_ref, schip_ref, sres_ref,
         wsem, lsem, ssem, rsem) = refs[2 * npart:]
        i = pl.program_id(0)
        x_, y_, c, chips = _place()
        j = 2 * x_ + y_
        dev = 2 * j + c
        sib = (x_, y_, 1 - c)
        rc = functools.partial(_remote, ssem, rsem)
        rel_chip = [2 * cx + cy for cx, cy in chips] + [j]
        peers = [(px, py, pc) for px in (x_, 1 - x_) for py in (y_, 1 - y_) for pc in (c, 1 - c)][1:]
        wt_copy = pltpu.make_async_copy(wt_hbm, wt_ref, lsem.at[0])

        def window(case, slot):
            return [pltpu.make_async_copy(d_hbm[p].at[:, pl.ds(c0, w)], stg_ref.at[slot, :, pl.ds(w0, w)], wsem.at[slot, n])
                    for n, (p, c0, w, w0) in enumerate(_window_pieces(case))]

        def to_sibling(k):
            return rc(k, gt_ref.at[k, 1 - c], sib_ref.at[k], sib)

        def to_chip(k):
            return rc(4 + k, out_ref.at[k], in_ref.at[k], (*chips[k], c))

        def trade(k):
            to_sibling(k).wait_recv()

            def add(n, carry):
                rr = pl.ds(pl.multiple_of(n * RS_CH, RS_CH), RS_CH)
                out_ref[k, rr, :] = (gt_ref[k, c, rr, :].astype(F32) + sib_ref[k, rr, :].astype(F32)).astype(BF16)
                return carry

            lax.fori_loop(0, IN_HALF // RS_CH, add, 0)
            to_chip(k).start()

        def keep(k, first, vals):
            for half in range(2):
                lo, hi = max(first, IN_HALF * half), min(first + vals.shape[0], IN_HALF * (half + 1))
                if lo < hi:
                    gt_ref[k, half, lo - IN_HALF * half:hi - IN_HALF * half, :] = vals[lo - first:hi - first].astype(BF16)

        mine_s = pl.ds(pl.multiple_of(c * hs, 8), hs)
        other_s = pl.ds(pl.multiple_of((1 - c) * hs, 8), hs)

        def small_to_sibling():
            return rc(15, sm0_ref.at[other_s], ssib_ref, sib)

        def small_to_chip(k):
            return rc(16 + k, schip_ref.at[j], schip_ref.at[j], (*chips[k], c))

        def small_share():
            return rc(19, sres_ref.at[c], sres_ref.at[c], sib)

        for k in range(N_GW):
            @pl.when(i == k)
            def _(k=k):
                slot = k % 2
                if k == 0:
                    red_ref[...] = jnp.zeros_like(red_ref)
                    wt_copy.start()
                    sm0_ref[...] = jnp.zeros_like(sm0_ref)
                    sm0_ref[0:16, :] = dcw_ref[...]
                    sm0_ref[16:17, 0:CONV_W] = dvec_ref[0:1, :]
                    sm0_ref[16:17, CONV_W:2 * CONV_W] = dvec_ref[1:2, :]
                    sm0_ref[17:18, 0:CONV_W] = dvec_ref[2:3, :]
                    for r in range(3):
                        sm0_ref[17:18, CONV_W + 128 * r:CONV_W + 128 * (r + 1)] = sma_ref[r:r + 1, :]
                    sm0_ref[18:19, :] = row0_ref[1:2, :]
                    small_to_sibling().start()
                if k == 1:
                    small_to_sibling().wait_recv()
                    schip_ref[j] = sm0_ref[mine_s, :] + ssib_ref[...]
                    for kk in range(3):
                        small_to_chip(kk).start()
                if k == N_GW - 1:
                    for kk in range(3):
                        jk = rel_chip[kk]
                        rc(16 + kk, schip_ref.at[jk], schip_ref.at[jk], sib).wait_recv()
                    tot = schip_ref[0]
                    for d in range(1, N_CHIPS):
                        tot = tot + schip_ref[d]
                    sres_ref[c] = tot
                    small_share().start()
                for case in range(N_CHIPS):
                    if k == 0:
                        @pl.when(rel_chip[0] == case)
                        def _():
                            for cp in window(case, 0):
                                cp.start()
                    if k + 1 < N_GW:
                        @pl.when(rel_chip[k + 1] == case)
                        def _():
                            for cp in window(case, 1 - slot):
                                cp.start()
                for case in range(N_CHIPS):
                    @pl.when(rel_chip[k] == case)
                    def _():
                        for cp in window(case, slot):
                            cp.wait()
                for part in range(2):
                    cols = pl.ds(part * (WIN_W // 2), WIN_W // 2)
                    g = lax.dot_general(stg_ref[slot, :, cols], h_ref[...], (((0,), (0,)), ((), ())),
                                        preferred_element_type=F32)
                    for off in sorted(set(WIN_OFF)):
                        @pl.when(rel_chip[k] % 2 == (1 if off else 0))
                        def _():
                            keep(k, part * (WIN_W // 2) - off, g)
                    if part == 0 and k >= 1:
                        trade(k - 1)
                to_sibling(k).start()

        @pl.when(i == N_GW)
        def _():
            wt_copy.wait()

        @pl.when(i >= N_GW)
        def _():
            xv = x_ref[...]
            r = lax.rsqrt(jnp.mean(xv * xv, axis=-1, keepdims=True) + EPS)
            xh = xv * r
            n = xh * nw_ref[...]
            dproj = jnp.concatenate([ref[...] for ref in d_ref], axis=1)
            dh = jnp.dot(dproj, wt_ref[...], preferred_element_type=F32)
            red_ref[0:1, :] += jnp.sum(dh, axis=0, keepdims=True)
            red_ref[1:2, :] += jnp.sum(dh * n, axis=0, keepdims=True)
            dn = dh * s1_ref[...]
            red_ref[2:3, :] += jnp.sum(dn * xh, axis=0, keepdims=True)
            dxh = dn * nw_ref[...]
            gx_ref[...] = dout_ref[...] + r * (dxh - xh * jnp.mean(dxh * xh, axis=-1, keepdims=True))

        @pl.when(i == nstep - 1)
        def _():
            sall_ref[dev] = row0_ref[...]
            sall_ref[dev, 2:5, :] = red_ref[0:3, :]
            sends = [rc(8 + k, sall_ref.at[dev], sall_ref.at[dev], peer) for k, peer in enumerate(peers)]
            for cp in sends:
                cp.start()
            sends += [to_sibling(k) for k in range(N_GW)] + [to_chip(k) for k in range(3)]
            sends += [small_to_sibling(), small_share()] + [small_to_chip(k) for k in range(3)]
            own = N_GW - 1
            to_sibling(own).wait_recv()
            for k in range(3):
                to_chip(k).wait_recv()

            def total(n, carry):
                rr = pl.ds(pl.multiple_of(n * RS_CH, RS_CH), RS_CH)
                acc = gt_ref[own, c, rr, :].astype(F32) + sib_ref[own, rr, :].astype(F32)
                for k in range(3):
                    acc = acc + in_ref[k, rr, :].astype(F32)
                res_ref[c, rr, :] = acc
                return carry

            lax.fori_loop(0, IN_HALF // RS_CH, total, 0)
            share = rc(7, res_ref.at[c], res_ref.at[c], sib)
            share.start()
            sends.append(share)
            back = [pltpu.make_async_copy(res_ref.at[half], gw_hbm.at[half], lsem.at[1 + half]) for half in range(2)]
            for half in range(2):
                @pl.when(c == half)
                def _():
                    back[half].start()
            for k, (px, py, pc) in enumerate(peers):
                pdev = 4 * px + 2 * py + pc
                rc(8 + k, sall_ref.at[pdev], sall_ref.at[pdev], (px, py, pc)).wait_recv()
            rows_ref[...] = sall_ref[...]
            rc(19, sres_ref.at[1 - c], sres_ref.at[1 - c], sib).wait_recv()
            ssum_ref[0:hs, :] = sres_ref[0]
            ssum_ref[hs:rows0, :] = sres_ref[1]
            rc(7, res_ref.at[1 - c], res_ref.at[1 - c], sib).wait_recv()
            for half in range(2):
                @pl.when(c != half)
                def _():
                    back[half].start()
            for cp in sends:
                cp.wait_send()
            for cp in back:
                cp.wait()

    blk = lambda i: jnp.maximum(i - N_GW, 0)
    row = lambda w: pl.BlockSpec((tm, w), lambda i: (blk(i), 0))
    vec = pl.BlockSpec((1, D_MODEL), lambda i: (0, 0))
    const = lambda shape: pl.BlockSpec(shape, lambda i: (0,) * len(shape))
    hbm = pl.BlockSpec(memory_space=pl.ANY)
    return pl.pallas_call(
        body,
        name="in_proj_bwd",
        grid=(nstep,),
        in_specs=[hbm] * npart + [row(w) for w in DPROJ_WIDTHS] + [row(D_MODEL), row(D_MODEL), vec, vec,
                  pl.BlockSpec((t, D_MODEL), lambda i: (0, 0), pipeline_mode=pl.Buffered(1)), hbm, const((16, D_MODEL)),
                  const((8, CONV_W)), const((8, 128)), const((8, D_MODEL))],
        out_specs=[row(D_MODEL), hbm, const((rows0, D_MODEL)), const((N_DEV, 8, D_MODEL))],
        out_shape=[jax.ShapeDtypeStruct((t, D_MODEL), F32), jax.ShapeDtypeStruct((2, IN_HALF, D_MODEL), F32),
                   jax.ShapeDtypeStruct((rows0, D_MODEL), F32), jax.ShapeDtypeStruct((N_DEV, 8, D_MODEL), F32)],
        scratch_shapes=[pltpu.VMEM((2, t, WIN_W), BF16), pltpu.VMEM((IN_W, D_MODEL), BF16),
                        pltpu.VMEM((N_CHIPS, 2, IN_HALF, D_MODEL), BF16), pltpu.VMEM((N_CHIPS, IN_HALF, D_MODEL), BF16),
                        pltpu.VMEM((3, IN_HALF, D_MODEL), BF16), pltpu.VMEM((3, IN_HALF, D_MODEL), BF16),
                        pltpu.VMEM((2, IN_HALF, D_MODEL), F32), pltpu.VMEM((N_DEV, 8, D_MODEL), F32),
                        pltpu.VMEM((8, D_MODEL), F32), pltpu.VMEM((rows0, D_MODEL), F32), pltpu.VMEM((hs, D_MODEL), F32),
                        pltpu.VMEM((N_CHIPS, hs, D_MODEL), F32),
                        pltpu.VMEM((2, hs, D_MODEL), F32), pltpu.SemaphoreType.DMA((2, 3)), pltpu.SemaphoreType.DMA((3,)),
                        pltpu.SemaphoreType.DMA((n_sem,)), pltpu.SemaphoreType.DMA((n_sem,))],
        compiler_params=_cparams(dimension_semantics=("arbitrary",)),
    )(*dparts, *dparts, x, dout, s1, nw, h, wt_full, dcw, dvec, sm_a, row0)


MESH = pl.DeviceIdType.MESH


def _place():
    x, y, c = lax.axis_index("x"), lax.axis_index("y"), lax.axis_index("c")
    chips = [(1 - x, y), (x, 1 - y), (1 - x, 1 - y)]
    return x, y, c, chips


def _remote(sems_s, sems_r, k, src, dst, to):
    return pltpu.make_async_remote_copy(src_ref=src, dst_ref=dst, send_sem=sems_s.at[k], recv_sem=sems_r.at[k],
                                        device_id=to, device_id_type=MESH)


RS_CH = 32
RS_SEMS = 5


def _rs_to_sibling(rc, s0, theirs, sib_ref, sib):
    cp = rc(s0, theirs, sib_ref, sib)
    cp.start()
    return cp


def _rs_trade(rc, s0, theirs, mine, sib_ref, out_ref, in_ref, rows, c, sib, chips):
    rc(s0, theirs, sib_ref, sib).wait_recv()
    cps = []
    for k, (cx, cy) in enumerate(chips):
        jk = 2 * cx + cy

        def add(i, carry, jk=jk, k=k):
            rr = pl.ds(pl.multiple_of(i * RS_CH, RS_CH), RS_CH)
            out_ref[k, rr, :] = (mine[jk, rr, :].astype(F32) + sib_ref[jk, rr, :].astype(F32)).astype(BF16)
            return carry

        lax.fori_loop(0, rows // RS_CH, add, 0)
        cps.append(rc(s0 + 1 + k, out_ref.at[k], in_ref.at[k], (cx, cy, c)))
        cps[-1].start()
    return cps


def _rs_total(rc, s0, mine, sib_ref, out_ref, in_ref, res_ref, rows, j, c, sib):
    for k in range(3):
        rc(s0 + 1 + k, out_ref.at[k], in_ref.at[k], sib).wait_recv()

    def total(i, carry):
        rr = pl.ds(pl.multiple_of(i * RS_CH, RS_CH), RS_CH)
        acc = mine[j, rr, :].astype(F32) + sib_ref[j, rr, :].astype(F32)
        for k in range(3):
            acc = acc + in_ref[k, rr, :].astype(F32)
        res_ref[c, rr, :] = acc
        return carry

    lax.fori_loop(0, rows // RS_CH, total, 0)
    cp = rc(s0 + 4, res_ref.at[c], res_ref.at[c], sib)
    cp.start()
    return cp


def _rs_done(rc, s0, res_ref, c, sib):
    rc(s0 + 4, res_ref.at[1 - c], res_ref.at[1 - c], sib).wait_recv()


def _rs_scratch(rows):
    return [pltpu.VMEM((N_CHIPS, rows, D_MODEL), BF16), pltpu.VMEM((3, rows, D_MODEL), BF16),
            pltpu.VMEM((3, rows, D_MODEL), BF16)]


MAIN_W = 640
MAIN_DST = (((0, 0, 512), (1, 0, 128)), ((2, 0, 512), (3, 0, 128)), ((3, 128, 384), (4, 0, 256)), ((4, 384, 128), (5, 0, 512)))
PAIR_DST = ((1, 128, 128), (4, 256, 128))


def _in_proj_gather(x, wt, c_row, w_ada, b_ada, nw):
    t = x.shape[0]
    ch = 512
    n_sem = 16

    def body(x_hbm, wt_ref, c_ref, wada_ref, bada_ref, nw_ref,
             q_hbm, kv_hbm, ga_hbm, ua_hbm, ug_hbm, gb_hbm, h_hbm, w4_hbm, call_ref, ada_ref,
             x_ref, h_ref, w4_ref, stg_ref, pstg_ref, part_ref, lsem, osem, wsem, ssem, rsem):
        outs = (q_hbm, kv_hbm, ga_hbm, ua_hbm, ug_hbm, gb_hbm)
        x_, y_, c, chips = _place()
        j = 2 * x_ + y_
        dev = 2 * j + c
        sib = (x_, y_, 1 - c)
        idx = [2 * cx + cy for cx, cy in chips]
        rc = functools.partial(_remote, ssem, rsem)
        x_copy = pltpu.make_async_copy(x_hbm, x_ref, lsem.at[0])
        x_copy.start()

        def rows_of(s, cc):
            return pl.ds(pl.multiple_of(2 * IN_HALF * s + IN_HALF * cc, 16), IN_HALF)

        w4_ref[rows_of(j, 0), :] = wt_ref[0].astype(BF16)
        w4_ref[rows_of(j, 1), :] = wt_ref[1].astype(BF16)
        call_ref[dev] = c_ref[...]
        sends = []
        peers = [(px, py, pc) for px in (x_, 1 - x_) for py in (y_, 1 - y_) for pc in (c, 1 - c)][1:]
        for k, peer in enumerate(peers):
            sends.append(rc(k, call_ref.at[dev], call_ref.at[dev], peer))
        for cp in sends:
            cp.start()

        for k, (px, py, pc) in enumerate(peers):
            pdev = 4 * px + 2 * py + pc
            rc(k, call_ref.at[pdev], call_ref.at[pdev], (px, py, pc)).wait_recv()
        rowid = lax.broadcasted_iota(jnp.int32, (N_DEV, D_MODEL), 0)
        call = jnp.zeros((N_DEV, D_MODEL), F32)
        for r in range(N_DEV):
            call = jnp.where(rowid == r, jnp.broadcast_to(call_ref[r], (N_DEV, D_MODEL)), call)
        bsh = bada_ref[:, 0:ADA_SHARD]
        for k in range(1, N_CHIPS):
            bsh = jnp.where(j == k, bada_ref[:, ADA_SHARD * k:ADA_SHARD * (k + 1)], bsh)
        part = jnp.dot(_silu(call).astype(BF16), wada_ref[...].astype(BF16), preferred_element_type=F32) + bsh
        for r in range(N_DEV):
            part_ref[r] = part[r:r + 1, :]
        ada_ref[j] = part_ref[dev]
        for k, chip in enumerate(chips):
            sends.append(rc(13 + k, part_ref.at[2 * idx[k] + c], ada_ref.at[j], (*chip, c)))
            sends[-1].start()
        for k, chip in enumerate(chips):
            sends.append(rc(7 + k, w4_ref.at[rows_of(j, c)], w4_ref.at[rows_of(j, c)], (*chip, c)))
            sends[-1].start()

        x_copy.wait()

        def prenorm(i, carry):
            rr = pl.ds(pl.multiple_of(i * ch, ch), ch)
            xv = x_ref[rr, :]
            r = lax.rsqrt(jnp.mean(xv * xv, axis=-1, keepdims=True) + EPS)
            x_ref[rr, :] = (xv * r) * nw_ref[...]
            return carry

        lax.fori_loop(0, t // ch, prenorm, 0)
        for k in range(3):
            rc(13 + k, ada_ref.at[idx[k]], ada_ref.at[idx[k]], sib).wait_recv()

        shift = jnp.concatenate([ada_ref[0], ada_ref[1][:, 0:256]], axis=1)
        s1 = 1.0 + jnp.concatenate([ada_ref[1][:, 256:768], ada_ref[2][:, 0:512]], axis=1)

        def norm(i, carry):
            rr = pl.ds(pl.multiple_of(i * ch, ch), ch)
            h_ref[rr, :] = (x_ref[rr, :] * s1 + shift).astype(BF16)
            return carry

        lax.fori_loop(0, t // ch, norm, 0)
        h_copy = pltpu.make_async_copy(h_ref, h_hbm, lsem.at[1])
        h_copy.start()

        def put_main(case, slot):
            cps, col = [], 0
            for n, (a, c0, w) in enumerate(MAIN_DST[case]):
                cps.append(pltpu.make_async_copy(stg_ref.at[slot, :, pl.ds(col, w)], outs[a].at[:, pl.ds(c0, w)], osem.at[slot, n]))
                col += w
            return cps

        def put_pair(case, slot):
            a, c0, w = PAIR_DST[case]
            return pltpu.make_async_copy(pstg_ref.at[slot], outs[a].at[:, pl.ds(c0, w)], osem.at[slot, 2])

        def project(first_row, width, dst, slot):
            wrows = pl.ds(pl.multiple_of(first_row, 128), width)

            def blk(i, carry):
                rr = pl.ds(pl.multiple_of(i * ch, ch), ch)
                dst[slot, rr, :] = lax.dot_general(h_ref[rr, :], w4_ref[wrows, :], (((1,), (1,)), ((), ())),
                                                   preferred_element_type=F32)
                return carry

            lax.fori_loop(0, t // ch, blk, 0)

        def phase(p, s, pair):
            slot = p % 2
            if p >= 2:
                for case in range(N_CHIPS):
                    @pl.when(order[p - 2] == case)
                    def _():
                        for cp in put_main(case, slot):
                            cp.wait()
            if p == 3:
                for case in range(2):
                    @pl.when(j // 2 == case)
                    def _():
                        put_pair(case, 0).wait()
            project(2 * IN_HALF * s + 64 * (s % 2), MAIN_W, stg_ref, slot)
            for case in range(N_CHIPS):
                @pl.when(s == case)
                def _():
                    for cp in put_main(case, slot):
                        cp.start()
            if pair is not None:
                project(MAIN_W + 2 * (2 * IN_HALF) * pair, 128, pstg_ref, slot % 2 if p == 2 else 1)
                for case in range(2):
                    @pl.when(pair == case)
                    def _():
                        put_pair(case, 0 if p == 2 else 1).start()

        order = [j] + idx
        w_out = [pltpu.make_async_copy(w4_ref.at[pl.ds(pl.multiple_of(2 * IN_HALF * s, 32), 2 * IN_HALF)],
                                       w4_hbm.at[pl.ds(pl.multiple_of(2 * IN_HALF * s, 32), 2 * IN_HALF)], wsem.at[p])
                 for p, s in enumerate(order)]
        w_out[0].start()
        phase(0, j, None)
        passed = []
        for k in range(3):
            jk = idx[k]
            rc(7 + k, w4_ref.at[rows_of(jk, c)], w4_ref.at[rows_of(jk, c)], sib).wait_recv()
            passed.append(rc(10 + k, w4_ref.at[rows_of(jk, c)], w4_ref.at[rows_of(jk, c)], sib))
            passed[-1].start()
            rc(10 + k, w4_ref.at[rows_of(jk, 1 - c)], w4_ref.at[rows_of(jk, 1 - c)], sib).wait_recv()
            w_out[1 + k].start()
            if k == 0:
                phase(1, jk, None)
            elif k == 1:
                phase(2, jk, j // 2)
            else:
                phase(3, jk, 1 - j // 2)

        for case in range(N_CHIPS):
            for p in (2, 3):
                @pl.when(order[p] == case)
                def _():
                    for cp in put_main(case, p % 2):
                        cp.wait()
        for case in range(2):
            @pl.when(1 - j // 2 == case)
            def _():
                put_pair(case, 1).wait()
        h_copy.wait()
        for cp in w_out:
            cp.wait()
        for cp in sends + passed:
            cp.wait_send()

    vm = pl.BlockSpec(memory_space=pltpu.VMEM)
    hbm = pl.BlockSpec(memory_space=pl.ANY)
    widths = (512, 256, 512, 512, 512, 512)
    return pl.pallas_call(
        body,
        name="in_proj",
        in_specs=[hbm, vm, vm, vm, vm, vm],
        out_specs=[hbm] * 8 + [vm, vm],
        out_shape=[jax.ShapeDtypeStruct((t, w), F32) for w in widths]
        + [jax.ShapeDtypeStruct((t, D_MODEL), BF16), jax.ShapeDtypeStruct((IN_W, D_MODEL), BF16),
           jax.ShapeDtypeStruct((N_DEV, 1, D_MODEL), F32), jax.ShapeDtypeStruct((N_CHIPS, 1, ADA_SHARD), F32)],
        scratch_shapes=[pltpu.VMEM((t, D_MODEL), F32), pltpu.VMEM((t, D_MODEL), BF16), pltpu.VMEM((IN_W, D_MODEL), BF16),
                        pltpu.VMEM((2, t, MAIN_W), F32), pltpu.VMEM((2, t, 128), F32), pltpu.VMEM((N_DEV, 1, ADA_SHARD), F32),
                        pltpu.SemaphoreType.DMA((2,)), pltpu.SemaphoreType.DMA((2, 3)), pltpu.SemaphoreType.DMA((N_CHIPS,)),
                        pltpu.SemaphoreType.DMA((n_sem,)), pltpu.SemaphoreType.DMA((n_sem,))],
        compiler_params=_cparams(),
    )(x, wt, c_row, w_ada, b_ada, nw)


def _adamw_math(w, g, m, v):
    m2 = ADAM_B1 * m + (1.0 - ADAM_B1) * g
    v2 = ADAM_B2 * v + (1.0 - ADAM_B2) * (g * g)
    m_hat = m2 / (1.0 - ADAM_B1 ** ADAM_STEP)
    v_hat = v2 / (1.0 - ADAM_B2 ** ADAM_STEP)
    delta = -ADAM_LR * (m_hat / (jnp.sqrt(v_hat) + ADAM_EPS) + ADAM_WD * w)
    return delta, m2, v2


def _adamw(name, w, g, m, v, tm, through=None):
    r, cdim = w.shape
    nstep = r // tm
    extra = [] if through is None else [through]

    def body(w_ref, g_ref, m_ref, v_ref, *rest):
        g2_ref, d_ref, m2_ref, v2_ref = rest[len(extra):len(extra) + 4]
        g = g_ref[...]
        g2_ref[...] = g
        d_ref[...], m2_ref[...], v2_ref[...] = _adamw_math(w_ref[...], g, m_ref[...], v_ref[...])
        if extra:
            rest[-1][...] = rest[0][...]

    blk = pl.BlockSpec((tm, cdim), lambda i: (i, 0))
    eblk = [pl.BlockSpec((e.shape[0] // nstep, e.shape[1]), lambda i: (i, 0)) for e in extra]
    return pl.pallas_call(
        body,
        name=name,
        grid=(nstep,),
        in_specs=[blk] * 4 + eblk,
        out_specs=[blk] * 4 + eblk,
        out_shape=[jax.ShapeDtypeStruct((r, cdim), F32)] * 4 + [jax.ShapeDtypeStruct(e.shape, e.dtype) for e in extra],
        compiler_params=_cparams(dimension_semantics=("arbitrary",)),
    )(w, g, m, v, *extra)


def _adamw_ada(w, m, v, cact_t, dcols):
    r, cdim = w.shape
    tm = 256

    def body(w_ref, m_ref, v_ref, ct_ref, dc_ref, g_ref, d_ref, m2_ref, v2_ref):
        g = jnp.dot(ct_ref[...].astype(BF16), dc_ref[...].astype(BF16), preferred_element_type=F32)
        g_ref[...] = g
        d_ref[...], m2_ref[...], v2_ref[...] = _adamw_math(w_ref[...], g, m_ref[...], v_ref[...])

    blk = pl.BlockSpec((tm, cdim), lambda i: (i, 0))
    return pl.pallas_call(
        body,
        name="adamw_w_ada",
        grid=(r // tm,),
        in_specs=[blk] * 3 + [pl.BlockSpec((tm, N_DEV), lambda i: (i, 0)), pl.BlockSpec((N_DEV, cdim), lambda i: (0, 0))],
        out_specs=[blk] * 4,
        out_shape=[jax.ShapeDtypeStruct((r, cdim), F32)] * 4,
        compiler_params=_cparams(dimension_semantics=("arbitrary",)),
    )(w, m, v, cact_t, dcols)


def _adamw_small(ws, ms, vs, ssum, rows):
    n = len(ws)

    def body(*refs):
        w_r, m_r, v_r = refs[0:n], refs[n:2 * n], refs[2 * n:3 * n]
        ss_ref, rows_ref = refs[3 * n], refs[3 * n + 1]
        g_r, d_r, m2_r, v2_r = (refs[3 * n + 2 + k * n:3 * n + 2 + (k + 1) * n] for k in range(4))
        loss_ref = refs[7 * n + 2]
        j = 2 * lax.axis_index("x") + lax.axis_index("y")
        rsum = rows_ref[0]
        for d in range(1, N_DEV):
            rsum = rsum + rows_ref[d]
        taps = []
        for t in range(CONV_TAPS):
            row = ss_ref[t // 2:t // 2 + 1, :]
            c0 = CONV_W * (t % 2)
            pick = row[:, c0:c0 + 128]
            for k in range(1, N_CHIPS):
                pick = jnp.where(j == k, row[:, c0 + 128 * k:c0 + 128 * (k + 1)], pick)
            taps.append(pick)
        grads = [jnp.concatenate([rsum[2:3], rsum[3:4], rsum[0:1]], axis=1), rsum[4:5],
                 ss_ref[17:18, 512:512 + HEAD_DIM], ss_ref[17:18, 640:640 + HEAD_DIM], ss_ref[17:18, 768:776],
                 None, ss_ref[16:17, 0:CONV_W], ss_ref[16:17, CONV_W:2 * CONV_W], ss_ref[17:18, 0:CONV_W]]
        for i in range(n):
            if grads[i] is None:
                g = jnp.concatenate(taps, axis=0)
                w, m, v = (jnp.concatenate([ref[t] for t in range(CONV_TAPS)], axis=0) for ref in (w_r[i], m_r[i], v_r[i]))
                res = (g,) + _adamw_math(w, g, m, v)
                for ref, val in zip((g_r[i], d_r[i], m2_r[i], v2_r[i]), res):
                    for t in range(CONV_TAPS):
                        ref[t] = val[t:t + 1, :]
                continue
            g = grads[i]
            g_r[i][...] = g
            d_r[i][...], m2_r[i][...], v2_r[i][...] = _adamw_math(w_r[i][...], g, m_r[i][...], v_r[i][...])
        loss_ref[...] = (0.5 / D_MODEL) * jnp.sum(ss_ref[18:19, :], axis=1, keepdims=True)

    vm = pl.BlockSpec(memory_space=pltpu.VMEM)
    shapes = [jax.ShapeDtypeStruct(w.shape, F32) for w in ws]
    out = pl.pallas_call(
        body,
        name="adamw_small",
        in_specs=[vm] * (3 * n + 2),
        out_specs=[vm] * (4 * n + 1),
        out_shape=shapes * 4 + [jax.ShapeDtypeStruct((1, 1), F32)],
        compiler_params=_cparams(),
    )(*ws, *ms, *vs, ssum, rows)
    return out[0:n], out[n:2 * n], out[2 * n:3 * n], out[3 * n:4 * n], out[4 * n]


def _rope_tables(t):
    inv = ROPE_THETA ** (-jnp.arange(0, HEAD_DIM, 2, dtype=F32) / HEAD_DIM)
    ang = jnp.arange(t, dtype=F32)[:, None] * inv[None, :]
    cos, sin = jnp.cos(ang), jnp.sin(ang)
    return jnp.tile(cos, (1, 4)), jnp.tile(jnp.concatenate([-sin, sin], axis=1), (1, 2))


def kernel(x, c, w_ada, b_ada, norm_w, w_in, q_norm_w, k_norm_w, sinks, conv_w, conv_b, ln_w, ln_b, w_out, loss_target, m_w_ada, m_b_ada, m_norm_w, m_w_in, m_q_norm_w, m_k_norm_w, m_sinks, m_conv_w, m_conv_b, m_ln_w, m_ln_b, m_w_out, v_w_ada, v_b_ada, v_norm_w, v_w_in, v_q_norm_w, v_k_norm_w, v_sinks, v_conv_w, v_conv_b, v_ln_w, v_ln_b, v_w_out):
    xi, yi = lax.axis_index("x"), lax.axis_index("y")
    j = 2 * xi + yi
    x2, tgt = x[0], loss_target[0]
    t = x2.shape[0]

    wt_s, mt_s, vt_s = w_in[0].T, m_w_in[0].T, v_w_in[0].T
    by_tap = lambda a: jnp.transpose(a, (1, 0, 2))

    q_raw, kv_raw, ga, ua, ug, gb, h, w_full, call, ada4 = _in_proj_gather(
        x2, wt_s.reshape(2, IN_HALF, D_MODEL), c, w_ada[0], b_ada, norm_w)
    ada = ada4.reshape(1, 3 * D_MODEL)
    s1, gate = 1.0 + ada[:, D_MODEL:2 * D_MODEL], ada[:, 2 * D_MODEL:]

    cos_f, sin_s = _rope_tables(t)
    qw2, kw2 = jnp.tile(q_norm_w, (1, 2)), jnp.tile(k_norm_w, (1, 2))

    o, mix_a, wo4, cw4 = _attn_fwd(q_raw, kv_raw, ga, qw2, kw2, sinks, cos_f, sin_s,
                                   w_out[0].reshape(2, OUT_HALF, D_MODEL), by_tap(conv_w))
    w_out_full = wo4.reshape(D_MODEL, D_MODEL)
    cw_full = jnp.concatenate([cw4[i] for i in range(N_CHIPS)], axis=1)
    cz, mix_b = _conv_fwd(ua, ug, gb, cw_full, conv_b, ln_w, ln_b)
    dout, dmix_a, dmix_b, gwo_bf, red_o = _out_proj(mix_a, mix_b, x2, tgt, gate, w_out_full)

    dq, dkv, dga, sm_a, gwo = _attn_bwd(q_raw, kv_raw, ga, o, dmix_a, qw2, kw2, sinks, cos_f, sin_s,
                                        gwo_bf.reshape(N_CHIPS, 2, OUT_HALF, D_MODEL))
    dua, dug, dgb, dcw, dvec = _conv_bwd(ua, ug, gb, cz, dmix_b, cw_full, ln_w, ln_b)
    dparts = (dq, dkv, dga, dua, dug, dgb)

    grad_x, gw, ssum, rows = _in_proj_bwd(dparts, h, x2, dout, s1, norm_w, w_full, dcw, dvec, sm_a, red_o)

    gt_w_in = gw.reshape(2 * IN_HALF, D_MODEL)
    g_w_out = gwo.reshape(D_MODEL // N_CHIPS, D_MODEL)
    d_ada_all = jnp.concatenate([rows[:, 2], rows[:, 3], rows[:, 0]], axis=1)
    dcols = lax.dynamic_slice(d_ada_all, (0, ADA_SHARD * j), (N_DEV, ADA_SHARD))
    cact_t = jax.nn.silu(call.reshape(N_DEV, D_MODEL)).T

    g_w_ada, d_w_ada, nm_w_ada, nv_w_ada = _adamw_ada(w_ada[0], m_w_ada[0], v_w_ada[0], cact_t, dcols)
    gt_w_in, dt_w_in, nmt_w_in, nvt_w_in, grad_x = _adamw("adamw_w_in", wt_s, gt_w_in, mt_s, vt_s, 176, through=grad_x)
    g_w_in, d_w_in, nm_w_in, nv_w_in = gt_w_in.T, dt_w_in.T, nmt_w_in.T, nvt_w_in.T
    g_w_out, d_w_out, nm_w_out, nv_w_out = _adamw("adamw_w_out", w_out[0], g_w_out, m_w_out[0], v_w_out[0], 128)
    ws = [b_ada, norm_w, q_norm_w, k_norm_w, sinks, by_tap(conv_w), conv_b, ln_w, ln_b]
    ms = [m_b_ada, m_norm_w, m_q_norm_w, m_k_norm_w, m_sinks, by_tap(m_conv_w), m_conv_b, m_ln_w, m_ln_b]
    vs = [v_b_ada, v_norm_w, v_q_norm_w, v_k_norm_w, v_sinks, by_tap(v_conv_w), v_conv_b, v_ln_w, v_ln_b]
    gs, ds, nms, nvs, loss11 = _adamw_small(ws, ms, vs, ssum, rows)
    loss = loss11[0, 0]

    def order(ada_v, in_v, out_v, sm):
        b, nw_, qw_, kw_, sk_, cw_, cb_, lw_, lb_ = sm
        return [ada_v[None], b, nw_, in_v[None], qw_, kw_, sk_, by_tap(cw_), cb_, lw_, lb_, out_v[None]]

    grads = order(g_w_ada, g_w_in, g_w_out, gs)
    deltas = order(d_w_ada, d_w_in, d_w_out, ds)
    new_m = order(nm_w_ada, nm_w_in, nm_w_out, nms)
    new_v = order(nv_w_ada, nv_w_in, nv_w_out, nvs)
    return (loss, grad_x[None], *grads, *deltas, *new_m, *new_v)
```

```python
import functools

import jax
import jax.numpy as jnp
from jax import lax
from jax.experimental import pallas as pl
from jax.experimental.pallas import tpu as pltpu

F32 = jnp.float32
BF16 = jnp.bfloat16

D_MODEL = 1024
ATTN_W = 512
KV_W = 128
CONV_W = 512
IN_W = 2816
HEAD_DIM = 64
CONV_TAPS = 31
QBLK = 128
EPS = 1e-6
ROPE_THETA = 10000.0

ADAM_LR = 0.001
ADAM_B1 = 0.9
ADAM_B2 = 0.999
ADAM_EPS = 1e-08
ADAM_WD = 0.01
ADAM_STEP = 10

N_CHIPS = 4
N_DEV = 8
IN_HALF = IN_W // N_CHIPS // 2
OUT_HALF = D_MODEL // N_CHIPS // 2
ADA_SHARD = 3 * D_MODEL // N_CHIPS

VMEM_LIMIT = 56 * 1024 * 1024
CONV_PAD = 32


def _cparams(**kw):
    return pltpu.CompilerParams(vmem_limit_bytes=VMEM_LIMIT, **kw)


def _sigmoid(v):
    return 1.0 / (1.0 + jnp.exp(-v))


def _silu(v):
    return v * _sigmoid(v)


def _dsilu(v):
    s = _sigmoid(v)
    return s * (1.0 + v * (1.0 - s))


def _lane(shape):
    return lax.broadcasted_iota(jnp.int32, shape, len(shape) - 1)


PUT_ROWS = 512


def _fetch(hbm_refs, vmem_refs, sem):
    cps = [pltpu.make_async_copy(h, v, sem.at[i]) for i, (h, v) in enumerate(zip(hbm_refs, vmem_refs))]
    for cp in cps:
        cp.start()
    return cps


def _put(vmem_ref, hbm_ref, sem, m):
    r = pl.ds(pl.multiple_of(m * PUT_ROWS, PUT_ROWS), PUT_ROWS)
    return pltpu.make_async_copy(vmem_ref.at[r], hbm_ref.at[r], sem.at[m])


def _put_all(pairs, sems, m):
    for (v, h), sem in zip(pairs, sems):
        _put(v, h, sem, m).start()


def _put_wait(pairs, sems, n):
    for (v, h), sem in zip(pairs, sems):
        for m in range(n):
            _put(v, h, sem, m).wait()


def _head_mean(s, left):
    sl = jnp.sum(jnp.where(left, s, 0.0), axis=-1, keepdims=True)
    sr = jnp.sum(jnp.where(left, 0.0, s), axis=-1, keepdims=True)
    return jnp.where(left, sl, sr) * (1.0 / HEAD_DIM)


def _rot(v, first):
    return jnp.where(first, pltpu.roll(v, 96, 1), pltpu.roll(v, 32, 1))


def _norm_rope(v, w, cos, sin_s, left, first):
    r = lax.rsqrt(_head_mean(v * v, left) + EPS)
    xh = v * r
    n = xh * w
    return n * cos + _rot(n, first) * sin_s, xh, r


def _norm_rope_bwd(d, xh, r, w, cos, sin_s, left, first):
    dn = d * cos - _rot(d, first) * sin_s
    dw = jnp.sum(dn * xh, axis=0, keepdims=True)
    dxh = dn * w
    return r * (dxh - xh * _head_mean(dxh * xh, left)), dw


def _dup_heads(v, left):
    sw = pltpu.roll(v, 64, 1)
    return jnp.where(left, v, sw), jnp.where(left, sw, v)


def _prep_kv(kv_ref, kw_ref, cos_ref, sin_ref, ka_ref, va_ref, t):
    ch = 256
    for g in range(2):
        ka_ref[g, 0:QBLK, :] = jnp.zeros((QBLK, 128), BF16)
        va_ref[g, 0:QBLK, :] = jnp.zeros((QBLK, 128), BF16)

    def chunk(i, carry):
        r0 = pl.multiple_of(i * ch, ch)
        left = _lane((ch, 128)) < 64
        first = (_lane((ch, 128)) % 64) < 32
        k = kv_ref[pl.ds(r0, ch), 0:128]
        v = kv_ref[pl.ds(r0, ch), 128:256]
        kr, _, _ = _norm_rope(k, kw_ref[...], cos_ref[pl.ds(r0, ch), :], sin_ref[pl.ds(r0, ch), :], left, first)
        k0, k1 = _dup_heads(kr, left)
        v0, v1 = _dup_heads(v, left)
        ka_ref[0, pl.ds(QBLK + r0, ch), :] = k0.astype(BF16)
        ka_ref[1, pl.ds(QBLK + r0, ch), :] = k1.astype(BF16)
        va_ref[0, pl.ds(QBLK + r0, ch), :] = v0.astype(BF16)
        va_ref[1, pl.ds(QBLK + r0, ch), :] = v1.astype(BF16)
        return carry

    lax.fori_loop(0, t // ch, chunk, 0)


def _band_mask(n):
    qi = lax.broadcasted_iota(jnp.int32, (2 * QBLK, 2 * QBLK), 0) % QBLK
    kj = lax.broadcasted_iota(jnp.int32, (2 * QBLK, 2 * QBLK), 1)
    local = (kj > qi) & (kj <= qi + QBLK)
    return local & ((n > 0) | (kj >= QBLK))


def _softmax_pair(s, mask, sink0, sink1):
    row = lax.broadcasted_iota(jnp.int32, (2 * QBLK, 1), 0)
    sink = jnp.where(row < QBLK, sink0, sink1)
    s = jnp.where(mask, s, -jnp.inf)
    m = jnp.maximum(jnp.max(s, axis=-1, keepdims=True), sink)
    e = jnp.exp(s - m)
    es = jnp.exp(sink - m)
    inv = 1.0 / (jnp.sum(e, axis=-1, keepdims=True) + es)
    return e * inv, es * inv


def _stack_heads(v, left):
    return jnp.concatenate([jnp.where(left, v, 0.0), jnp.where(left, 0.0, v)], axis=0)


def _attn_fwd(q_raw, kv_raw, ga, qkw2, sinks, cos_f, sin_s, wo, cw):
    t = q_raw.shape[0]
    nblk = t // QBLK
    per_put = PUT_ROWS // QBLK

    def body(q_hbm, kv_ref, ga_hbm, qkw_ref, sk_ref, cos_hbm, sin_hbm, wo_ref, cw_ref,
             o_hbm, mix_hbm, wo4_ref, cw4_ref, ka_ref, va_ref, q_ref, ga_ref, o_ref, mix_ref, cos_ref, sin_ref,
             isem, osem0, osem1, ssem, rsem):
        qw_ref, kw_ref = qkw_ref.at[0:1], qkw_ref.at[1:2]
        loads = _fetch((cos_hbm, sin_hbm, q_hbm, ga_hbm), (cos_ref, sin_ref, q_ref, ga_ref), isem)
        outs, osems = ((o_ref, o_hbm), (mix_ref, mix_hbm)), (osem0, osem1)
        x, y, c, chips = _place()
        j = 2 * x + y
        sib = (x, y, 1 - c)
        idx = [2 * cx + cy for cx, cy in chips]
        rc = functools.partial(_remote, ssem, rsem)
        wo4_ref[j] = wo_ref[...].astype(BF16)
        for tap in range(CONV_TAPS):
            cw4_ref[j, tap:tap + 1, :] = cw_ref[tap]
        cw4_ref[j, CONV_TAPS:, :] = jnp.zeros((CONV_PAD - CONV_TAPS, 128), F32)
        sends = []
        for k, chip in enumerate(chips):
            sends.append(rc(k, wo4_ref.at[j, c], wo4_ref.at[j, c], (*chip, c)))
            sends.append(rc(6 + k, cw4_ref.at[j], cw4_ref.at[j], (*chip, c)))
        for cp in sends:
            cp.start()

        loads[0].wait()
        loads[1].wait()
        _prep_kv(kv_ref, kw_ref, cos_ref, sin_ref, ka_ref, va_ref, t)
        loads[2].wait()
        loads[3].wait()

        def blk(n, carry):
            r0 = pl.multiple_of(n * QBLK, QBLK)
            left = _lane((QBLK, 128)) < 64
            first = (_lane((QBLK, 128)) % 64) < 32
            cos = cos_ref[pl.ds(r0, QBLK), :]
            sin = sin_ref[pl.ds(r0, QBLK), :]
            mask = _band_mask(n)
            scores = []
            for p in range(4):
                lanes = slice(p * 128, (p + 1) * 128)
                qr, _, _ = _norm_rope(q_ref[pl.ds(r0, QBLK), lanes], qw_ref[...], cos, sin, left, first)
                q2 = _stack_heads(qr * 0.125, left).astype(BF16)
                scores.append(lax.dot_general(q2, ka_ref[p // 2, pl.ds(r0, 2 * QBLK), :], (((1,), (1,)), ((), ())),
                                              preferred_element_type=F32))
            probs = [_softmax_pair(scores[p], mask, sk_ref[0, 2 * p], sk_ref[0, 2 * p + 1])[0].astype(BF16)
                     for p in range(4)]
            for p in range(4):
                lanes = slice(p * 128, (p + 1) * 128)
                o2 = jnp.dot(probs[p], va_ref[p // 2, pl.ds(r0, 2 * QBLK), :], preferred_element_type=F32)
                o = jnp.where(left, o2[0:QBLK], o2[QBLK:2 * QBLK])
                o_ref[pl.ds(r0, QBLK), lanes] = o.astype(BF16)
                mix_ref[pl.ds(r0, QBLK), lanes] = (o * _silu(ga_ref[pl.ds(r0, QBLK), lanes])).astype(BF16)

            @pl.when(n % per_put == per_put - 1)
            def _():
                _put_all(outs, osems, n // per_put)

            return carry

        lax.fori_loop(0, nblk, blk, 0)
        _put_wait(outs, osems, t // PUT_ROWS)

        passed = []
        for k, chip in enumerate(chips):
            jk = idx[k]
            rc(k, wo4_ref.at[jk, c], wo4_ref.at[jk, c], sib).wait_recv()
            passed.append(rc(3 + k, wo4_ref.at[jk, c], wo4_ref.at[jk, c], sib))
            passed[-1].start()
        for k, chip in enumerate(chips):
            jk = idx[k]
            rc(3 + k, wo4_ref.at[jk, 1 - c], wo4_ref.at[jk, 1 - c], sib).wait_recv()
            rc(6 + k, cw4_ref.at[jk], cw4_ref.at[jk], sib).wait_recv()
        for cp in sends + passed:
            cp.wait_send()

    vm = pl.BlockSpec(memory_space=pltpu.VMEM)
    hbm = pl.BlockSpec(memory_space=pl.ANY)
    n_sem = 9
    return pl.pallas_call(
        body,
        name="attn_fwd",
        in_specs=[hbm, vm, hbm, vm, pl.BlockSpec(memory_space=pltpu.SMEM), hbm, hbm, vm, vm],
        out_specs=[hbm, hbm, vm, vm],
        out_shape=[jax.ShapeDtypeStruct((t, ATTN_W), BF16), jax.ShapeDtypeStruct((t, ATTN_W), BF16),
                   jax.ShapeDtypeStruct((N_CHIPS, 2, OUT_HALF, D_MODEL), BF16),
                   jax.ShapeDtypeStruct((N_CHIPS, 32, 128), F32)],
        scratch_shapes=[pltpu.VMEM((2, t + QBLK, 128), BF16), pltpu.VMEM((2, t + QBLK, 128), BF16),
                        pltpu.VMEM((t, ATTN_W), F32), pltpu.VMEM((t, ATTN_W), F32),
                        pltpu.VMEM((t, ATTN_W), BF16), pltpu.VMEM((t, ATTN_W), BF16),
                        pltpu.VMEM((t, 128), F32), pltpu.VMEM((t, 128), F32),
                        pltpu.SemaphoreType.DMA((4,)), pltpu.SemaphoreType.DMA((t // PUT_ROWS,)),
                        pltpu.SemaphoreType.DMA((t // PUT_ROWS,)),
                        pltpu.SemaphoreType.DMA((n_sem,)), pltpu.SemaphoreType.DMA((n_sem,))],
        compiler_params=_cparams(),
    )(q_raw, kv_raw, ga, qkw2, sinks, cos_f, sin_s, wo, cw)


def _attn_bwd(q_raw, kv_raw, ga, o, dmix, qkw2, sinks, cos_f, sin_s, go):
    t = q_raw.shape[0]
    nblk = t // QBLK
    per_put = PUT_ROWS // QBLK

    def body(q_hbm, kv_ref, ga_hbm, o_hbm, dm_hbm, qkw_ref, sk_ref, cos_hbm, sin_hbm, go_ref,
             dq_hbm, dkv_ref, dga_hbm, sm_ref, gwo_ref, ka_ref, va_ref, dka_ref, dva_ref,
             sibo_ref, outo_ref, ino_ref, q_ref, ga_ref, o_ref, dm_ref, dq_ref, dga_ref, cos_ref, sin_ref,
             isem, osem0, osem1, ssem, rsem):
        qw_ref, kw_ref = qkw_ref.at[0:1], qkw_ref.at[1:2]
        loads = _fetch((cos_hbm, sin_hbm, q_hbm, ga_hbm, o_hbm, dm_hbm), (cos_ref, sin_ref, q_ref, ga_ref, o_ref, dm_ref), isem)
        outs, osems = ((dq_ref, dq_hbm), (dga_ref, dga_hbm)), (osem0, osem1)
        x, y, c, chips = _place()
        sib = (x, y, 1 - c)
        rc = functools.partial(_remote, ssem, rsem)
        theirs, mine = go_ref.at[:, 1 - c], go_ref.at[:, c]
        sends = [_rs_to_sibling(rc, 0, theirs, sibo_ref, sib)]
        loads[0].wait()
        loads[1].wait()
        _prep_kv(kv_ref, kw_ref, cos_ref, sin_ref, ka_ref, va_ref, t)
        dka_ref[...] = jnp.zeros_like(dka_ref)
        dva_ref[...] = jnp.zeros_like(dva_ref)
        sends += _rs_trade(rc, 0, theirs, mine, sibo_ref, outo_ref, ino_ref, OUT_HALF, c, sib, chips)
        for cp in loads[2:]:
            cp.wait()

        def blk(n, carry):
            dqw, dsk = carry
            r0 = pl.multiple_of(n * QBLK, QBLK)
            left = _lane((QBLK, 128)) < 64
            first = (_lane((QBLK, 128)) % 64) < 32
            cos = cos_ref[pl.ds(r0, QBLK), :]
            sin = sin_ref[pl.ds(r0, QBLK), :]
            mask = _band_mask(n)
            row = lax.broadcasted_iota(jnp.int32, (2 * QBLK, 1), 0)
            rows = pl.ds(r0, QBLK)
            win = pl.ds(r0, 2 * QBLK)
            lane_of = [slice(p * 128, (p + 1) * 128) for p in range(4)]
            for grp in ((0, 1), (2, 3)):
                qn = {p: _norm_rope(q_ref[rows, lane_of[p]], qw_ref[...], cos, sin, left, first) for p in grp}
                q2 = {p: _stack_heads(qn[p][0] * 0.125, left).astype(BF16) for p in grp}
                sc = {p: lax.dot_general(q2[p], ka_ref[p // 2, win, :], (((1,), (1,)), ((), ())),
                                         preferred_element_type=F32) for p in grp}
                do2 = {}
                for p in grp:
                    gav = ga_ref[rows, lane_of[p]]
                    dmv = dm_ref[rows, lane_of[p]].astype(F32)
                    dga_ref[rows, lane_of[p]] = (dmv * o_ref[rows, lane_of[p]].astype(F32) * _dsilu(gav)).astype(BF16)
                    do2[p] = _stack_heads(dmv * _silu(gav), left).astype(BF16)
                dpm = {p: lax.dot_general(do2[p], va_ref[p // 2, win, :], (((1,), (1,)), ((), ())),
                                          preferred_element_type=F32) for p in grp}
                sm = {p: _softmax_pair(sc[p], mask, sk_ref[0, 2 * p], sk_ref[0, 2 * p + 1]) for p in grp}
                dsl = {}
                for p in grp:
                    pm, ps = sm[p]
                    delta = jnp.sum(pm * dpm[p], axis=-1, keepdims=True)
                    dsl[p] = (pm * (dpm[p] - delta)).astype(BF16)
                    pd = ps * delta
                    d0 = jnp.sum(jnp.where(row < QBLK, pd, 0.0), axis=0, keepdims=True)
                    d1 = jnp.sum(jnp.where(row < QBLK, 0.0, pd), axis=0, keepdims=True)
                    l8 = _lane((1, 128))
                    dsk = dsk - jnp.where(l8 == 2 * p, d0, 0.0) - jnp.where(l8 == 2 * p + 1, d1, 0.0)
                for p in grp:
                    g = p // 2
                    dva_ref[g, win, :] += lax.dot_general(sm[p][0].astype(BF16), do2[p], (((0,), (0,)), ((), ())),
                                                          preferred_element_type=F32)
                    dka_ref[g, win, :] += lax.dot_general(dsl[p], q2[p], (((0,), (0,)), ((), ())),
                                                          preferred_element_type=F32)
                for p in grp:
                    dq2 = jnp.dot(dsl[p], ka_ref[p // 2, win, :], preferred_element_type=F32)
                    dqr = jnp.where(left, dq2[0:QBLK], dq2[QBLK:2 * QBLK]) * 0.125
                    dq, dw = _norm_rope_bwd(dqr, qn[p][1], qn[p][2], qw_ref[...], cos, sin, left, first)
                    dq_ref[rows, lane_of[p]] = dq.astype(BF16)
                    dqw = dqw + dw

            @pl.when(n % per_put == per_put - 1)
            def _():
                _put_all(outs, osems, n // per_put)

            return dqw, dsk

        zero = jnp.zeros((1, 128), F32)
        dqw, dsk = lax.fori_loop(0, nblk, blk, (zero, zero))

        ch = 256

        def chunk(i, dkw):
            r0 = pl.multiple_of(i * ch, ch)
            left = _lane((ch, 128)) < 64
            first = (_lane((ch, 128)) % 64) < 32
            rows = pl.ds(r0, ch)
            prow = pl.ds(QBLK + r0, ch)

            def fold(ref):
                a0 = ref[0, prow, :]
                a1 = ref[1, prow, :]
                return jnp.where(left, a0 + pltpu.roll(a0, 64, 1), a1 + pltpu.roll(a1, 64, 1))

            cos = cos_ref[rows, :]
            sin = sin_ref[rows, :]
            _, xh, r = _norm_rope(kv_ref[rows, 0:128], kw_ref[...], cos, sin, left, first)
            dk, dw = _norm_rope_bwd(fold(dka_ref), xh, r, kw_ref[...], cos, sin, left, first)
            dkv_ref[rows, 0:128] = dk.astype(BF16)
            dkv_ref[rows, 128:256] = fold(dva_ref).astype(BF16)
            return dkw + dw

        dkw = lax.fori_loop(0, t // ch, chunk, zero)
        sm_ref[...] = jnp.zeros((8, 128), F32)
        sm_ref[0:1, :] = dqw + pltpu.roll(dqw, 64, 1)
        sm_ref[1:2, :] = dkw + pltpu.roll(dkw, 64, 1)
        sm_ref[2:3, :] = dsk

        j = 2 * x + y
        sends.append(_rs_total(rc, 0, mine, sibo_ref, outo_ref, ino_ref, gwo_ref, OUT_HALF, j, c, sib))
        _rs_done(rc, 0, gwo_ref, c, sib)
        for cp in sends:
            cp.wait_send()
        _put_wait(outs, osems, t // PUT_ROWS)

    vm = pl.BlockSpec(memory_space=pltpu.VMEM)
    hbm = pl.BlockSpec(memory_space=pl.ANY)
    return pl.pallas_call(
        body,
        name="attn_bwd",
        in_specs=[hbm, vm, hbm, hbm, hbm, vm, pl.BlockSpec(memory_space=pltpu.SMEM), hbm, hbm, vm],
        out_specs=[hbm, vm, hbm, vm, vm],
        out_shape=[jax.ShapeDtypeStruct((t, ATTN_W), BF16), jax.ShapeDtypeStruct((t, 2 * KV_W), BF16),
                   jax.ShapeDtypeStruct((t, ATTN_W), BF16), jax.ShapeDtypeStruct((8, 128), F32),
                   jax.ShapeDtypeStruct((2, OUT_HALF, D_MODEL), F32)],
        scratch_shapes=[pltpu.VMEM((2, t + QBLK, 128), BF16), pltpu.VMEM((2, t + QBLK, 128), BF16),
                        pltpu.VMEM((2, t + QBLK, 128), F32), pltpu.VMEM((2, t + QBLK, 128), F32)]
        + _rs_scratch(OUT_HALF)
        + [pltpu.VMEM((t, ATTN_W), F32), pltpu.VMEM((t, ATTN_W), F32), pltpu.VMEM((t, ATTN_W), BF16),
           pltpu.VMEM((t, ATTN_W), BF16), pltpu.VMEM((t, ATTN_W), BF16), pltpu.VMEM((t, ATTN_W), BF16),
           pltpu.VMEM((t, 128), F32), pltpu.VMEM((t, 128), F32),
           pltpu.SemaphoreType.DMA((6,)), pltpu.SemaphoreType.DMA((t // PUT_ROWS,)), pltpu.SemaphoreType.DMA((t // PUT_ROWS,)),
           pltpu.SemaphoreType.DMA((RS_SEMS,)), pltpu.SemaphoreType.DMA((RS_SEMS,))],
        compiler_params=_cparams(),
    )(q_raw, kv_raw, ga, o, dmix, qkw2, sinks, cos_f, sin_s, go)


CONV_CH = 256
CONV_SUB = 128
CONV_ACCS = 1


def _shifted_windows(src_ref, r0, sh_ref):
    rows = CONV_CH + CONV_PAD
    win = src_ref[pl.ds(r0, rows), :]
    for b in range(8):
        sh = win if b == 0 else pltpu.roll(win, rows - b, 0)
        for c in range(CONV_W // 128):
            sh_ref[b, c] = sh[:, c * 128:(c + 1) * 128]


def _conv_fwd(ua, ug, gb, cw, cb, lw, lb):
    t = ua.shape[0]

    def body(ua_hbm, ug_hbm, gb_hbm, cw_ref, cb_ref, lw_ref, lb_ref, cz_hbm, mix_hbm, zp_ref, sh_ref,
             ua_ref, ug_ref, gb_ref, cz_ref, mix_ref, isem, osem0, osem1):
        loads = _fetch((ua_hbm, ug_hbm, gb_hbm), (ua_ref, ug_ref, gb_ref), isem)
        outs, osems = ((cz_ref, cz_hbm), (mix_ref, mix_hbm)), (osem0, osem1)
        per_put = PUT_ROWS // CONV_CH
        zp_ref[0:CONV_PAD, :] = jnp.zeros((CONV_PAD, CONV_W), F32)
        loads[0].wait()
        loads[1].wait()

        def glu(i, carry):
            r0 = pl.multiple_of(i * CONV_CH, CONV_CH)
            rows = pl.ds(r0, CONV_CH)
            zp_ref[pl.ds(CONV_PAD + r0, CONV_CH), :] = ua_ref[rows, :] * _sigmoid(ug_ref[rows, :])
            return carry

        lax.fori_loop(0, t // CONV_CH, glu, 0)
        loads[2].wait()

        def chunk(i, carry):
            r0 = pl.multiple_of(i * CONV_CH, CONV_CH)
            _shifted_windows(zp_ref, r0, sh_ref)
            for c in range(CONV_W // 128):
                lanes = slice(c * 128, (c + 1) * 128)

                def sub(k, carry2):
                    b0 = pl.multiple_of(k * CONV_SUB, CONV_SUB)
                    acc = [jnp.broadcast_to(cb_ref[0:1, lanes], (CONV_SUB, 128))] + [None] * (CONV_ACCS - 1)
                    for j in range(CONV_TAPS):
                        off = j + CONV_PAD - (CONV_TAPS - 1)
                        term = sh_ref[off % 8, c, pl.ds(b0 + 8 * (off // 8), CONV_SUB), :] * cw_ref[c, j:j + 1, :]
                        acc[j % CONV_ACCS] = term if acc[j % CONV_ACCS] is None else acc[j % CONV_ACCS] + term
                    cz_ref[pl.ds(r0 + b0, CONV_SUB), lanes] = functools.reduce(lambda a, b: a + b, acc)
                    return carry2

                lax.fori_loop(0, CONV_CH // CONV_SUB, sub, 0)
            rows = pl.ds(r0, CONV_CH)
            cz = cz_ref[rows, :]
            mu = jnp.mean(cz, axis=-1, keepdims=True)
            xc = cz - mu
            rs = lax.rsqrt(jnp.mean(xc * xc, axis=-1, keepdims=True) + EPS)
            ln = xc * rs * lw_ref[...] + lb_ref[...]
            mix_ref[rows, :] = (_silu(ln) * _silu(gb_ref[rows, :])).astype(BF16)

            @pl.when(i % per_put == per_put - 1)
            def _():
                _put_all(outs, osems, i // per_put)

            return carry

        lax.fori_loop(0, t // CONV_CH, chunk, 0)
        _put_wait(outs, osems, t // PUT_ROWS)

    vm = pl.BlockSpec(memory_space=pltpu.VMEM)
    hbm = pl.BlockSpec(memory_space=pl.ANY)
    nput = t // PUT_ROWS
    return pl.pallas_call(
        body,
        name="conv_fwd",
        in_specs=[hbm] * 3 + [vm] * 4,
        out_specs=[hbm, hbm],
        out_shape=[jax.ShapeDtypeStruct((t, CONV_W), F32), jax.ShapeDtypeStruct((t, CONV_W), BF16)],
        scratch_shapes=[pltpu.VMEM((t + CONV_PAD, CONV_W), F32),
                        pltpu.VMEM((8, CONV_W // 128, CONV_CH + CONV_PAD, 128), F32),
                        pltpu.VMEM((t, CONV_W), F32), pltpu.VMEM((t, CONV_W), F32), pltpu.VMEM((t, CONV_W), F32),
                        pltpu.VMEM((t, CONV_W), F32), pltpu.VMEM((t, CONV_W), BF16),
                        pltpu.SemaphoreType.DMA((3,)), pltpu.SemaphoreType.DMA((nput,)), pltpu.SemaphoreType.DMA((nput,))],
        compiler_params=_cparams(),
    )(ua, ug, gb, cw, cb, lw, lb)


def _conv_bwd(ua, ug, gb, cz, dmix, cw, lw, lb):
    t = ua.shape[0]

    def body(ua_hbm, ug_hbm, gb_hbm, cz_hbm, dm_hbm, cw_ref, lw_ref, lb_ref,
             dua_hbm, dug_hbm, dgb_hbm, dcw_ref, dvec_ref, zp_ref, dp_ref, sh_ref, wacc_ref,
             ua_ref, ug_ref, gb_ref, cz_ref, dm_ref, dua_ref, dug_ref, dgb_ref, isem, osem0, osem1, osem2):
        loads = _fetch((ua_hbm, ug_hbm, gb_hbm, cz_hbm, dm_hbm), (ua_ref, ug_ref, gb_ref, cz_ref, dm_ref), isem)
        per_put = PUT_ROWS // CONV_CH
        zp_ref[0:CONV_PAD, :] = jnp.zeros((CONV_PAD, CONV_W), F32)
        dp_ref[t:t + CONV_PAD, :] = jnp.zeros((CONV_PAD, CONV_W), F32)
        wacc_ref[...] = jnp.zeros_like(wacc_ref)
        for cp in loads:
            cp.wait()

        def pointwise(i, carry):
            dcb, dlw, dlb = carry
            r0 = pl.multiple_of(i * CONV_CH, CONV_CH)
            rows = pl.ds(r0, CONV_CH)
            zp_ref[pl.ds(CONV_PAD + r0, CONV_CH), :] = ua_ref[rows, :] * _sigmoid(ug_ref[rows, :])
            cz = cz_ref[rows, :]
            mu = jnp.mean(cz, axis=-1, keepdims=True)
            xc = cz - mu
            rs = lax.rsqrt(jnp.mean(xc * xc, axis=-1, keepdims=True) + EPS)
            xh = xc * rs
            ln = xh * lw_ref[...] + lb_ref[...]
            gbv = gb_ref[rows, :]
            dy = dm_ref[rows, :].astype(F32)
            dgb_ref[rows, :] = (dy * _silu(ln) * _dsilu(gbv)).astype(BF16)
            dl = dy * _silu(gbv) * _dsilu(ln)
            dxh = dl * lw_ref[...]
            dcz = rs * (dxh - jnp.mean(dxh, axis=-1, keepdims=True)
                        - xh * jnp.mean(dxh * xh, axis=-1, keepdims=True))
            dp_ref[rows, :] = dcz

            @pl.when(i % per_put == per_put - 1)
            def _():
                _put(dgb_ref, dgb_hbm, osem2, i // per_put).start()

            return (dcb + jnp.sum(dcz, axis=0, keepdims=True),
                    dlw + jnp.sum(dl * xh, axis=0, keepdims=True),
                    dlb + jnp.sum(dl, axis=0, keepdims=True))

        zero = jnp.zeros((1, CONV_W), F32)
        dcb, dlw, dlb = lax.fori_loop(0, t // CONV_CH, pointwise, (zero, zero, zero))
        dvec_ref[...] = jnp.zeros((8, CONV_W), F32)
        dvec_ref[0:1, :] = dcb
        dvec_ref[1:2, :] = dlw
        dvec_ref[2:3, :] = dlb

        def chunk(i, carry):
            r0 = pl.multiple_of(i * CONV_CH, CONV_CH)
            _shifted_windows(dp_ref, r0, sh_ref)
            for c in range(CONV_W // 128):
                lanes = slice(c * 128, (c + 1) * 128)

                def sub(k, carry2):
                    b0 = pl.multiple_of(k * CONV_SUB, CONV_SUB)
                    acc = [None] * CONV_ACCS
                    for j in range(CONV_TAPS):
                        off = CONV_TAPS - 1 - j
                        term = sh_ref[off % 8, c, pl.ds(b0 + 8 * (off // 8), CONV_SUB), :] * cw_ref[c, j:j + 1, :]
                        acc[j % CONV_ACCS] = term if acc[j % CONV_ACCS] is None else acc[j % CONV_ACCS] + term
                    acc = functools.reduce(lambda a, b: a + b, acc)
                    rr = pl.ds(r0 + b0, CONV_SUB)
                    sg = _sigmoid(ug_ref[rr, lanes])
                    dua_ref[rr, lanes] = (acc * sg).astype(BF16)
                    dug_ref[rr, lanes] = (acc * ua_ref[rr, lanes] * sg * (1.0 - sg)).astype(BF16)
                    return carry2

                lax.fori_loop(0, CONV_CH // CONV_SUB, sub, 0)
            _shifted_windows(zp_ref, r0, sh_ref)
            for c in range(CONV_W // 128):
                lanes = slice(c * 128, (c + 1) * 128)

                def subw(k, carry2):
                    b0 = pl.multiple_of(k * CONV_SUB, CONV_SUB)
                    dcz = dp_ref[pl.ds(r0 + b0, CONV_SUB), lanes]
                    for j in range(CONV_TAPS):
                        off = j + CONV_PAD - (CONV_TAPS - 1)
                        pr = dcz * sh_ref[off % 8, c, pl.ds(b0 + 8 * (off // 8), CONV_SUB), :]
                        parts = [pr[8 * q:8 * (q + 1)] for q in range(CONV_SUB // 8)]
                        while len(parts) > 1:
                            parts = [a + b for a, b in zip(parts[0::2], parts[1::2])]
                        wacc_ref[8 * j:8 * (j + 1), lanes] += parts[0]
                    return carry2

                lax.fori_loop(0, CONV_CH // CONV_SUB, subw, 0)

            @pl.when(i % per_put == per_put - 1)
            def _():
                _put_all(((dua_ref, dua_hbm), (dug_ref, dug_hbm)), (osem0, osem1), i // per_put)

            return carry

        lax.fori_loop(0, t // CONV_CH, chunk, 0)
        _put_wait(((dua_ref, dua_hbm), (dug_ref, dug_hbm), (dgb_ref, dgb_hbm)), (osem0, osem1, osem2), t // PUT_ROWS)
        dcw_ref[...] = jnp.zeros((16, 2 * CONV_W), F32)
        for j in range(CONV_TAPS):
            dcw_ref[j // 2:j // 2 + 1, CONV_W * (j % 2):CONV_W * (j % 2 + 1)] = jnp.sum(
                wacc_ref[8 * j:8 * (j + 1), :], axis=0, keepdims=True)

    vm = pl.BlockSpec(memory_space=pltpu.VMEM)
    hbm = pl.BlockSpec(memory_space=pl.ANY)
    return pl.pallas_call(
        body,
        name="conv_bwd",
        in_specs=[hbm] * 5 + [vm] * 3,
        out_specs=[hbm] * 3 + [vm] * 2,
        out_shape=[jax.ShapeDtypeStruct((t, CONV_W), BF16)] * 3
        + [jax.ShapeDtypeStruct((16, 2 * CONV_W), F32), jax.ShapeDtypeStruct((8, CONV_W), F32)],
        scratch_shapes=[pltpu.VMEM((t + CONV_PAD, CONV_W), F32), pltpu.VMEM((t + CONV_PAD, CONV_W), F32),
                        pltpu.VMEM((8, CONV_W // 128, CONV_CH + CONV_PAD, 128), F32), pltpu.VMEM((8 * 32, CONV_W), F32)]
        + [pltpu.VMEM((t, CONV_W), F32)] * 4 + [pltpu.VMEM((t, CONV_W), BF16)] * 4
        + [pltpu.SemaphoreType.DMA((5,))] + [pltpu.SemaphoreType.DMA((t // PUT_ROWS,))] * 3,
        compiler_params=_cparams(),
    )(ua, ug, gb, cz, dmix, cw, lw, lb)


def _out_proj(mix_a, mix_b, x, tgt, gate, w_out):
    t = x.shape[0]
    tm = 512
    nstep = t // tm

    def body(ma_ref, mb_ref, x_ref, t_ref, g_ref, w_ref, dout_ref, dma_ref, dmb_ref, gw_ref, red_ref, acc_ref):
        i = pl.program_id(0)

        @pl.when(i == 0)
        def _():
            acc_ref[...] = jnp.zeros_like(acc_ref)
            red_ref[...] = jnp.zeros_like(red_ref)

        mix = jnp.concatenate([ma_ref[...], mb_ref[...]], axis=1)
        y = jnp.dot(mix, w_ref[...], preferred_element_type=F32)
        gate_v = g_ref[...]
        err = x_ref[...] + gate_v * y - t_ref[...]
        dout = err * (1.0 / D_MODEL)
        dout_ref[...] = dout
        red_ref[0:1, :] += jnp.sum(dout * y, axis=0, keepdims=True)
        red_ref[1:2, :] += jnp.sum(err * err, axis=0, keepdims=True)
        dy = (dout * gate_v).astype(BF16)
        dmix = lax.dot_general(dy, w_ref[...], (((1,), (1,)), ((), ())), preferred_element_type=F32)
        dma_ref[...] = dmix[:, 0:512].astype(BF16)
        dmb_ref[...] = dmix[:, 512:1024].astype(BF16)
        acc_ref[...] += lax.dot_general(mix, dy, (((0,), (0,)), ((), ())), preferred_element_type=F32)

        @pl.when(i == nstep - 1)
        def _():
            gw_ref[...] = acc_ref[...].astype(BF16)

    row = lambda w: pl.BlockSpec((tm, w), lambda i: (i, 0))
    const = lambda s: pl.BlockSpec(s, lambda i: (0, 0))
    return pl.pallas_call(
        body,
        name="out_proj",
        grid=(nstep,),
        in_specs=[row(512), row(512), row(D_MODEL), row(D_MODEL), const((1, D_MODEL)),
                  pl.BlockSpec((D_MODEL, D_MODEL), lambda i: (0, 0), pipeline_mode=pl.Buffered(1))],
        out_specs=[row(D_MODEL), row(512), row(512), const((D_MODEL, D_MODEL)), const((8, D_MODEL))],
        out_shape=[jax.ShapeDtypeStruct((t, D_MODEL), F32), jax.ShapeDtypeStruct((t, 512), BF16),
                   jax.ShapeDtypeStruct((t, 512), BF16), jax.ShapeDtypeStruct((D_MODEL, D_MODEL), BF16),
                   jax.ShapeDtypeStruct((8, D_MODEL), F32)],
        scratch_shapes=[pltpu.VMEM((D_MODEL, D_MODEL), F32)],
        compiler_params=_cparams(dimension_semantics=("arbitrary",)),
    )(mix_a, mix_b, x, tgt, gate, w_out)


DPROJ_WIDTHS = (512, 256, 512, 512, 512, 512)
DPROJ_STARTS = (0, 512, 768, 1280, 1792, 2304)
WIN_W = 768
WIN_START = (0, 640, 1408, 2048)
WIN_OFF = (0, 64, 0, 64)
N_GW = N_CHIPS


def _window_pieces(s):
    lo, hi = WIN_START[s], WIN_START[s] + WIN_W
    out = []
    for p, (st, w) in enumerate(zip(DPROJ_STARTS, DPROJ_WIDTHS)):
        a, b = max(lo, st), min(hi, st + w)
        if a < b:
            out.append((p, a - st, b - a, a - lo))
    return out


def _in_proj_bwd(dparts, h, x, dout, s1, nw, wt_full, dcw, dvec, sm_a, row0):
    t = x.shape[0]
    tm = 256
    nstep = N_GW + t // tm
    n_sem = 20
    rows0 = 32
    hs = rows0 // 2
    npart = len(DPROJ_WIDTHS)

    def body(*refs):
        d_hbm, d_ref = refs[:npart], refs[npart:2 * npart]
        (x_ref, dout_ref, s1_ref, nw_ref, h_ref, wt_hbm, dcw_ref, dvec_ref, sma_ref, row0_ref,
         gx_ref, gw_hbm, ssum_ref, rows_ref,
         stg_ref, wt_ref, gt_ref, sib_ref, out_ref, in_ref, res_ref, sall_ref, red_ref, sm0_ref, ssib_ref, schip_ref, sres_ref,
         wsem, lsem, ssem, rsem) = refs[2 * npart:]
        i = pl.program_id(0)
        x_, y_, c, chips = _place()
        j = 2 * x_ + y_
        dev = 2 * j + c
        sib = (x_, y_, 1 - c)
        rc = functools.partial(_remote, ssem, rsem)
        rel_chip = [2 * cx + cy for cx, cy in chips] + [j]
        peers = [(px, py, pc) for px in (x_, 1 - x_) for py in (y_, 1 - y_) for pc in (c, 1 - c)][1:]
        wt_copy = pltpu.make_async_copy(wt_hbm, wt_ref, lsem.at[0])

        def window(case, slot):
            return [pltpu.make_async_copy(d_hbm[p].at[:, pl.ds(c0, w)], stg_ref.at[slot, :, pl.ds(w0, w)], wsem.at[slot, n])
                    for n, (p, c0, w, w0) in enumerate(_window_pieces(case))]

        def to_sibling(k):
            return rc(k, gt_ref.at[k, 1 - c], sib_ref.at[k], sib)

        def to_chip(k):
            return rc(4 + k, out_ref.at[k], in_ref.at[k], (*chips[k], c))

        def trade(k):
            to_sibling(k).wait_recv()

            def add(n, carry):
                rr = pl.ds(pl.multiple_of(n * RS_CH, RS_CH), RS_CH)
                out_ref[k, rr, :] = (gt_ref[k, c, rr, :].astype(F32) + sib_ref[k, rr, :].astype(F32)).astype(BF16)
                return carry

            lax.fori_loop(0, IN_HALF // RS_CH, add, 0)
            to_chip(k).start()

        def keep(k, first, vals):
            for half in range(2):
                lo, hi = max(first, IN_HALF * half), min(first + vals.shape[0], IN_HALF * (half + 1))
                if lo < hi:
                    gt_ref[k, half, lo - IN_HALF * half:hi - IN_HALF * half, :] = vals[lo - first:hi - first].astype(BF16)

        mine_s = pl.ds(pl.multiple_of(c * hs, 8), hs)
        other_s = pl.ds(pl.multiple_of((1 - c) * hs, 8), hs)

        def small_to_sibling():
            return rc(15, sm0_ref.at[other_s], ssib_ref, sib)

        def small_to_chip(k):
            return rc(16 + k, schip_ref.at[j], schip_ref.at[j], (*chips[k], c))

        def small_share():
            return rc(19, sres_ref.at[c], sres_ref.at[c], sib)

        for k in range(N_GW):
            @pl.when(i == k)
            def _(k=k):
                slot = k % 2
                if k == 0:
                    red_ref[...] = jnp.zeros_like(red_ref)
                    wt_copy.start()
                    sm0_ref[...] = jnp.zeros_like(sm0_ref)
                    sm0_ref[0:16, :] = dcw_ref[...]
                    sm0_ref[16:17, 0:CONV_W] = dvec_ref[0:1, :]
                    sm0_ref[16:17, CONV_W:2 * CONV_W] = dvec_ref[1:2, :]
                    sm0_ref[17:18, 0:CONV_W] = dvec_ref[2:3, :]
                    for r in range(3):
                        sm0_ref[17:18, CONV_W + 128 * r:CONV_W + 128 * (r + 1)] = sma_ref[r:r + 1, :]
                    sm0_ref[18:19, :] = row0_ref[1:2, :]
                    small_to_sibling().start()
                if k == 1:
                    small_to_sibling().wait_recv()
                    schip_ref[j] = sm0_ref[mine_s, :] + ssib_ref[...]
                    for kk in range(3):
                        small_to_chip(kk).start()
                if k == N_GW - 1:
                    for kk in range(3):
                        jk = rel_chip[kk]
                        rc(16 + kk, schip_ref.at[jk], schip_ref.at[jk], sib).wait_recv()
                    tot = schip_ref[0]
                    for d in range(1, N_CHIPS):
                        tot = tot + schip_ref[d]
                    sres_ref[c] = tot
                    small_share().start()
                for case in range(N_CHIPS):
                    if k == 0:
                        @pl.when(rel_chip[0] == case)
                        def _():
                            for cp in window(case, 0):
                                cp.start()
                    if k + 1 < N_GW:
                        @pl.when(rel_chip[k + 1] == case)
                        def _():
                            for cp in window(case, 1 - slot):
                                cp.start()
                for case in range(N_CHIPS):
                    @pl.when(rel_chip[k] == case)
                    def _():
                        for cp in window(case, slot):
                            cp.wait()
                for part in range(2):
                    cols = pl.ds(part * (WIN_W // 2), WIN_W // 2)
                    g = lax.dot_general(stg_ref[slot, :, cols], h_ref[...], (((0,), (0,)), ((), ())),
                                        preferred_element_type=F32)
                    for off in sorted(set(WIN_OFF)):
                        @pl.when(rel_chip[k] % 2 == (1 if off else 0))
                        def _():
                            keep(k, part * (WIN_W // 2) - off, g)
                    if part == 0 and k >= 1:
                        trade(k - 1)
                to_sibling(k).start()

        @pl.when(i == N_GW)
        def _():
            wt_copy.wait()

        @pl.when(i >= N_GW)
        def _():
            xv = x_ref[...]
            r = lax.rsqrt(jnp.mean(xv * xv, axis=-1, keepdims=True) + EPS)
            xh = xv * r
            n = xh * nw_ref[...]
            dproj = jnp.concatenate([ref[...] for ref in d_ref], axis=1)
            dh = jnp.dot(dproj, wt_ref[...], preferred_element_type=F32)
            red_ref[0:1, :] += jnp.sum(dh, axis=0, keepdims=True)
            red_ref[1:2, :] += jnp.sum(dh * n, axis=0, keepdims=True)
            dn = dh * s1_ref[...]
            red_ref[2:3, :] += jnp.sum(dn * xh, axis=0, keepdims=True)
            dxh = dn * nw_ref[...]
            gx_ref[...] = dout_ref[...] + r * (dxh - xh * jnp.mean(dxh * xh, axis=-1, keepdims=True))

        @pl.when(i == nstep - 1)
        def _():
            sall_ref[dev] = row0_ref[...]
            sall_ref[dev, 2:5, :] = red_ref[0:3, :]
            sends = [rc(8 + k, sall_ref.at[dev], sall_ref.at[dev], peer) for k, peer in enumerate(peers)]
            for cp in sends:
                cp.start()
            sends += [to_sibling(k) for k in range(N_GW)] + [to_chip(k) for k in range(3)]
            sends += [small_to_sibling(), small_share()] + [small_to_chip(k) for k in range(3)]
            own = N_GW - 1
            to_sibling(own).wait_recv()
            for k in range(3):
                to_chip(k).wait_recv()

            def total(n, carry):
                rr = pl.ds(pl.multiple_of(n * RS_CH, RS_CH), RS_CH)
                acc = gt_ref[own, c, rr, :].astype(F32) + sib_ref[own, rr, :].astype(F32)
                for k in range(3):
                    acc = acc + in_ref[k, rr, :].astype(F32)
                res_ref[c, rr, :] = acc
                return carry

            lax.fori_loop(0, IN_HALF // RS_CH, total, 0)
            share = rc(7, res_ref.at[c], res_ref.at[c], sib)
            share.start()
            sends.append(share)
            back = [pltpu.make_async_copy(res_ref.at[half], gw_hbm.at[half], lsem.at[1 + half]) for half in range(2)]
            for half in range(2):
                @pl.when(c == half)
                def _():
                    back[half].start()
            for k, (px, py, pc) in enumerate(peers):
                pdev = 4 * px + 2 * py + pc
                rc(8 + k, sall_ref.at[pdev], sall_ref.at[pdev], (px, py, pc)).wait_recv()
            rows_ref[...] = sall_ref[...]
            rc(19, sres_ref.at[1 - c], sres_ref.at[1 - c], sib).wait_recv()
            ssum_ref[0:hs, :] = sres_ref[0]
            ssum_ref[hs:rows0, :] = sres_ref[1]
            rc(7, res_ref.at[1 - c], res_ref.at[1 - c], sib).wait_recv()
            for half in range(2):
                @pl.when(c != half)
                def _():
                    back[half].start()
            for cp in sends:
                cp.wait_send()
            for cp in back:
                cp.wait()

    blk = lambda i: jnp.maximum(i - N_GW, 0)
    row = lambda w: pl.BlockSpec((tm, w), lambda i: (blk(i), 0))
    vec = pl.BlockSpec((1, D_MODEL), lambda i: (0, 0))
    const = lambda shape: pl.BlockSpec(shape, lambda i: (0,) * len(shape))
    hbm = pl.BlockSpec(memory_space=pl.ANY)
    return pl.pallas_call(
        body,
        name="in_proj_bwd",
        grid=(nstep,),
        in_specs=[hbm] * npart + [row(w) for w in DPROJ_WIDTHS] + [row(D_MODEL), row(D_MODEL), vec, vec,
                  pl.BlockSpec((t, D_MODEL), lambda i: (0, 0), pipeline_mode=pl.Buffered(1)), hbm, const((16, D_MODEL)),
                  const((8, CONV_W)), const((8, 128)), const((8, D_MODEL))],
        out_specs=[row(D_MODEL), hbm, const((rows0, D_MODEL)), const((N_DEV, 8, D_MODEL))],
        out_shape=[jax.ShapeDtypeStruct((t, D_MODEL), F32), jax.ShapeDtypeStruct((2, IN_HALF, D_MODEL), F32),
                   jax.ShapeDtypeStruct((rows0, D_MODEL), F32), jax.ShapeDtypeStruct((N_DEV, 8, D_MODEL), F32)],
        scratch_shapes=[pltpu.VMEM((2, t, WIN_W), BF16), pltpu.VMEM((IN_W, D_MODEL), BF16),
                        pltpu.VMEM((N_CHIPS, 2, IN_HALF, D_MODEL), BF16), pltpu.VMEM((N_CHIPS, IN_HALF, D_MODEL), BF16),
                        pltpu.VMEM((3, IN_HALF, D_MODEL), BF16), pltpu.VMEM((3, IN_HALF, D_MODEL), BF16),
                        pltpu.VMEM((2, IN_HALF, D_MODEL), F32), pltpu.VMEM((N_DEV, 8, D_MODEL), F32),
                        pltpu.VMEM((8, D_MODEL), F32), pltpu.VMEM((rows0, D_MODEL), F32), pltpu.VMEM((hs, D_MODEL), F32),
                        pltpu.VMEM((N_CHIPS, hs, D_MODEL), F32),
                        pltpu.VMEM((2, hs, D_MODEL), F32), pltpu.SemaphoreType.DMA((2, 3)), pltpu.SemaphoreType.DMA((3,)),
                        pltpu.SemaphoreType.DMA((n_sem,)), pltpu.SemaphoreType.DMA((n_sem,))],
        compiler_params=_cparams(dimension_semantics=("arbitrary",)),
    )(*dparts, *dparts, x, dout, s1, nw, h, wt_full, dcw, dvec, sm_a, row0)


MESH = pl.DeviceIdType.MESH


def _place():
    x, y, c = lax.axis_index("x"), lax.axis_index("y"), lax.axis_index("c")
    chips = [(1 - x, y), (x, 1 - y), (1 - x, 1 - y)]
    return x, y, c, chips


def _remote(sems_s, sems_r, k, src, dst, to):
    return pltpu.make_async_remote_copy(src_ref=src, dst_ref=dst, send_sem=sems_s.at[k], recv_sem=sems_r.at[k],
                                        device_id=to, device_id_type=MESH)


RS_CH = 32
RS_SEMS = 5


def _rs_to_sibling(rc, s0, theirs, sib_ref, sib):
    cp = rc(s0, theirs, sib_ref, sib)
    cp.start()
    return cp


def _rs_trade(rc, s0, theirs, mine, sib_ref, out_ref, in_ref, rows, c, sib, chips):
    rc(s0, theirs, sib_ref, sib).wait_recv()
    cps = []
    for k, (cx, cy) in enumerate(chips):
        jk = 2 * cx + cy

        def add(i, carry, jk=jk, k=k):
            rr = pl.ds(pl.multiple_of(i * RS_CH, RS_CH), RS_CH)
            out_ref[k, rr, :] = (mine[jk, rr, :].astype(F32) + sib_ref[jk, rr, :].astype(F32)).astype(BF16)
            return carry

        lax.fori_loop(0, rows // RS_CH, add, 0)
        cps.append(rc(s0 + 1 + k, out_ref.at[k], in_ref.at[k], (cx, cy, c)))
        cps[-1].start()
    return cps


def _rs_total(rc, s0, mine, sib_ref, out_ref, in_ref, res_ref, rows, j, c, sib):
    for k in range(3):
        rc(s0 + 1 + k, out_ref.at[k], in_ref.at[k], sib).wait_recv()

    def total(i, carry):
        rr = pl.ds(pl.multiple_of(i * RS_CH, RS_CH), RS_CH)
        acc = mine[j, rr, :].astype(F32) + sib_ref[j, rr, :].astype(F32)
        for k in range(3):
            acc = acc + in_ref[k, rr, :].astype(F32)
        res_ref[c, rr, :] = acc
        return carry

    lax.fori_loop(0, rows // RS_CH, total, 0)
    cp = rc(s0 + 4, res_ref.at[c], res_ref.at[c], sib)
    cp.start()
    return cp


def _rs_done(rc, s0, res_ref, c, sib):
    rc(s0 + 4, res_ref.at[1 - c], res_ref.at[1 - c], sib).wait_recv()


def _rs_scratch(rows):
    return [pltpu.VMEM((N_CHIPS, rows, D_MODEL), BF16), pltpu.VMEM((3, rows, D_MODEL), BF16),
            pltpu.VMEM((3, rows, D_MODEL), BF16)]


MAIN_W = 640
MAIN_DST = (((0, 0, 512), (1, 0, 128)), ((2, 0, 512), (3, 0, 128)), ((3, 128, 384), (4, 0, 256)), ((4, 384, 128), (5, 0, 512)))
PAIR_DST = ((1, 128, 128), (4, 256, 128))


def _in_proj_gather(x, wt, c_row, w_ada, b_ada, nw):
    t = x.shape[0]
    ch = 512
    n_sem = 16

    def body(x_hbm, wt_ref, c_ref, wada_ref, bada_ref, nw_ref,
             q_hbm, kv_hbm, ga_hbm, ua_hbm, ug_hbm, gb_hbm, h_hbm, w4_hbm, call_ref, ada_ref,
             x_ref, h_ref, w4_ref, stg_ref, pstg_ref, part_ref, lsem, osem, wsem, ssem, rsem):
        outs = (q_hbm, kv_hbm, ga_hbm, ua_hbm, ug_hbm, gb_hbm)
        x_, y_, c, chips = _place()
        j = 2 * x_ + y_
        dev = 2 * j + c
        sib = (x_, y_, 1 - c)
        idx = [2 * cx + cy for cx, cy in chips]
        rc = functools.partial(_remote, ssem, rsem)
        x_copy = pltpu.make_async_copy(x_hbm, x_ref, lsem.at[0])
        x_copy.start()

        def rows_of(s, cc):
            return pl.ds(pl.multiple_of(2 * IN_HALF * s + IN_HALF * cc, 16), IN_HALF)

        w4_ref[rows_of(j, 0), :] = wt_ref[0].astype(BF16)
        w4_ref[rows_of(j, 1), :] = wt_ref[1].astype(BF16)
        call_ref[dev] = c_ref[...]
        sends = []
        peers = [(px, py, pc) for px in (x_, 1 - x_) for py in (y_, 1 - y_) for pc in (c, 1 - c)][1:]
        for k, peer in enumerate(peers):
            sends.append(rc(k, call_ref.at[dev], call_ref.at[dev], peer))
        for cp in sends:
            cp.start()

        for k, (px, py, pc) in enumerate(peers):
            pdev = 4 * px + 2 * py + pc
            rc(k, call_ref.at[pdev], call_ref.at[pdev], (px, py, pc)).wait_recv()
        rowid = lax.broadcasted_iota(jnp.int32, (N_DEV, D_MODEL), 0)
        call = jnp.zeros((N_DEV, D_MODEL), F32)
        for r in range(N_DEV):
            call = jnp.where(rowid == r, jnp.broadcast_to(call_ref[r], (N_DEV, D_MODEL)), call)
        bsh = bada_ref[:, 0:ADA_SHARD]
        for k in range(1, N_CHIPS):
            bsh = jnp.where(j == k, bada_ref[:, ADA_SHARD * k:ADA_SHARD * (k + 1)], bsh)
        part = jnp.dot(_silu(call).astype(BF16), wada_ref[...].astype(BF16), preferred_element_type=F32) + bsh
        for r in range(N_DEV):
            part_ref[r] = part[r:r + 1, :]
        ada_ref[j] = part_ref[dev]
        for k, chip in enumerate(chips):
            sends.append(rc(13 + k, part_ref.at[2 * idx[k] + c], ada_ref.at[j], (*chip, c)))
            sends[-1].start()
        for k, chip in enumerate(chips):
            sends.append(rc(7 + k, w4_ref.at[rows_of(j, c)], w4_ref.at[rows_of(j, c)], (*chip, c)))
            sends[-1].start()

        x_copy.wait()

        def prenorm(i, carry):
            rr = pl.ds(pl.multiple_of(i * ch, ch), ch)
            xv = x_ref[rr, :]
            r = lax.rsqrt(jnp.mean(xv * xv, axis=-1, keepdims=True) + EPS)
            x_ref[rr, :] = (xv * r) * nw_ref[...]
            return carry

        lax.fori_loop(0, t // ch, prenorm, 0)
        for k in range(3):
            rc(13 + k, ada_ref.at[idx[k]], ada_ref.at[idx[k]], sib).wait_recv()

        shift = jnp.concatenate([ada_ref[0], ada_ref[1][:, 0:256]], axis=1)
        s1 = 1.0 + jnp.concatenate([ada_ref[1][:, 256:768], ada_ref[2][:, 0:512]], axis=1)

        def norm(i, carry):
            rr = pl.ds(pl.multiple_of(i * ch, ch), ch)
            h_ref[rr, :] = (x_ref[rr, :] * s1 + shift).astype(BF16)
            return carry

        lax.fori_loop(0, t // ch, norm, 0)
        h_copy = pltpu.make_async_copy(h_ref, h_hbm, lsem.at[1])
        h_copy.start()

        def put_main(case, slot):
            cps, col = [], 0
            for n, (a, c0, w) in enumerate(MAIN_DST[case]):
                cps.append(pltpu.make_async_copy(stg_ref.at[slot, :, pl.ds(col, w)], outs[a].at[:, pl.ds(c0, w)], osem.at[slot, n]))
                col += w
            return cps

        def put_pair(case, slot):
            a, c0, w = PAIR_DST[case]
            return pltpu.make_async_copy(pstg_ref.at[slot], outs[a].at[:, pl.ds(c0, w)], osem.at[slot, 2])

        def project(first_row, width, dst, slot):
            wrows = pl.ds(pl.multiple_of(first_row, 128), width)

            def blk(i, carry):
                rr = pl.ds(pl.multiple_of(i * ch, ch), ch)
                dst[slot, rr, :] = lax.dot_general(h_ref[rr, :], w4_ref[wrows, :], (((1,), (1,)), ((), ())),
                                                   preferred_element_type=F32)
                return carry

            lax.fori_loop(0, t // ch, blk, 0)

        def phase(p, s, pair):
            slot = p % 2
            if p >= 2:
                for case in range(N_CHIPS):
                    @pl.when(order[p - 2] == case)
                    def _():
                        for cp in put_main(case, slot):
                            cp.wait()
            if p == 3:
                for case in range(2):
                    @pl.when(j // 2 == case)
                    def _():
                        put_pair(case, 0).wait()
            project(2 * IN_HALF * s + 64 * (s % 2), MAIN_W, stg_ref, slot)
            for case in range(N_CHIPS):
                @pl.when(s == case)
                def _():
                    for cp in put_main(case, slot):
                        cp.start()
            if pair is not None:
                project(MAIN_W + 2 * (2 * IN_HALF) * pair, 128, pstg_ref, slot % 2 if p == 2 else 1)
                for case in range(2):
                    @pl.when(pair == case)
                    def _():
                        put_pair(case, 0 if p == 2 else 1).start()

        order = [j] + idx
        w_out = [pltpu.make_async_copy(w4_ref.at[pl.ds(pl.multiple_of(2 * IN_HALF * s, 32), 2 * IN_HALF)],
                                       w4_hbm.at[pl.ds(pl.multiple_of(2 * IN_HALF * s, 32), 2 * IN_HALF)], wsem.at[p])
                 for p, s in enumerate(order)]
        w_out[0].start()
        phase(0, j, None)
        passed = []
        for k in range(3):
            jk = idx[k]
            rc(7 + k, w4_ref.at[rows_of(jk, c)], w4_ref.at[rows_of(jk, c)], sib).wait_recv()
            passed.append(rc(10 + k, w4_ref.at[rows_of(jk, c)], w4_ref.at[rows_of(jk, c)], sib))
            passed[-1].start()
            rc(10 + k, w4_ref.at[rows_of(jk, 1 - c)], w4_ref.at[rows_of(jk, 1 - c)], sib).wait_recv()
            w_out[1 + k].start()
            if k == 0:
                phase(1, jk, None)
            elif k == 1:
                phase(2, jk, j // 2)
            else:
                phase(3, jk, 1 - j // 2)

        for case in range(N_CHIPS):
            for p in (2, 3):
                @pl.when(order[p] == case)
                def _():
                    for cp in put_main(case, p % 2):
                        cp.wait()
        for case in range(2):
            @pl.when(1 - j // 2 == case)
            def _():
                put_pair(case, 1).wait()
        h_copy.wait()
        for cp in w_out:
            cp.wait()
        for cp in sends + passed:
            cp.wait_send()

    vm = pl.BlockSpec(memory_space=pltpu.VMEM)
    hbm = pl.BlockSpec(memory_space=pl.ANY)
    widths = (512, 256, 512, 512, 512, 512)
    return pl.pallas_call(
        body,
        name="in_proj",
        in_specs=[hbm, vm, vm, vm, vm, vm],
        out_specs=[hbm] * 8 + [vm, vm],
        out_shape=[jax.ShapeDtypeStruct((t, w), F32) for w in widths]
        + [jax.ShapeDtypeStruct((t, D_MODEL), BF16), jax.ShapeDtypeStruct((IN_W, D_MODEL), BF16),
           jax.ShapeDtypeStruct((N_DEV, 1, D_MODEL), F32), jax.ShapeDtypeStruct((N_CHIPS, 1, ADA_SHARD), F32)],
        scratch_shapes=[pltpu.VMEM((t, D_MODEL), F32), pltpu.VMEM((t, D_MODEL), BF16), pltpu.VMEM((IN_W, D_MODEL), BF16),
                        pltpu.VMEM((2, t, MAIN_W), F32), pltpu.VMEM((2, t, 128), F32), pltpu.VMEM((N_DEV, 1, ADA_SHARD), F32),
                        pltpu.SemaphoreType.DMA((2,)), pltpu.SemaphoreType.DMA((2, 3)), pltpu.SemaphoreType.DMA((N_CHIPS,)),
                        pltpu.SemaphoreType.DMA((n_sem,)), pltpu.SemaphoreType.DMA((n_sem,))],
        compiler_params=_cparams(),
    )(x, wt, c_row, w_ada, b_ada, nw)


def _adamw_math(w, g, m, v):
    m2 = ADAM_B1 * m + (1.0 - ADAM_B1) * g
    v2 = ADAM_B2 * v + (1.0 - ADAM_B2) * (g * g)
    m_hat = m2 / (1.0 - ADAM_B1 ** ADAM_STEP)
    v_hat = v2 / (1.0 - ADAM_B2 ** ADAM_STEP)
    delta = -ADAM_LR * (m_hat / (jnp.sqrt(v_hat) + ADAM_EPS) + ADAM_WD * w)
    return delta, m2, v2


def _adamw(name, w, g, m, v, tm, through=None):
    r, cdim = w.shape
    nstep = r // tm
    extra = [] if through is None else [through]

    def body(w_ref, g_ref, m_ref, v_ref, *rest):
        g2_ref, d_ref, m2_ref, v2_ref = rest[len(extra):len(extra) + 4]
        g = g_ref[...]
        g2_ref[...] = g
        d_ref[...], m2_ref[...], v2_ref[...] = _adamw_math(w_ref[...], g, m_ref[...], v_ref[...])
        if extra:
            rest[-1][...] = rest[0][...]

    blk = pl.BlockSpec((tm, cdim), lambda i: (i, 0))
    eblk = [pl.BlockSpec((e.shape[0] // nstep, e.shape[1]), lambda i: (i, 0)) for e in extra]
    return pl.pallas_call(
        body,
        name=name,
        grid=(nstep,),
        in_specs=[blk] * 4 + eblk,
        out_specs=[blk] * 4 + eblk,
        out_shape=[jax.ShapeDtypeStruct((r, cdim), F32)] * 4 + [jax.ShapeDtypeStruct(e.shape, e.dtype) for e in extra],
        compiler_params=_cparams(dimension_semantics=("arbitrary",)),
    )(w, g, m, v, *extra)


def _adamw_ada(w, m, v, call, rows):
    r, cdim = w.shape
    tm = 256

    def body(w_ref, m_ref, v_ref, c_ref, rows_ref, g_ref, d_ref, m2_ref, v2_ref):
        j = 2 * lax.axis_index("x") + lax.axis_index("y")
        d_ada = jnp.concatenate([jnp.concatenate([rows_ref[d, row:row + 1, :] for d in range(N_DEV)], axis=0)
                                 for row in (2, 3, 0)], axis=1)
        dcols = d_ada[:, 0:cdim]
        for k in range(1, N_CHIPS):
            dcols = jnp.where(j == k, d_ada[:, cdim * k:cdim * (k + 1)], dcols)
        g = lax.dot_general(_silu(c_ref[...]).astype(BF16), dcols.astype(BF16), (((0,), (0,)), ((), ())),
                            preferred_element_type=F32)
        g_ref[...] = g
        d_ref[...], m2_ref[...], v2_ref[...] = _adamw_math(w_ref[...], g, m_ref[...], v_ref[...])

    blk = pl.BlockSpec((tm, cdim), lambda i: (i, 0))
    return pl.pallas_call(
        body,
        name="adamw_w_ada",
        grid=(r // tm,),
        in_specs=[blk] * 3 + [pl.BlockSpec((N_DEV, tm), lambda i: (0, i)),
                              pl.BlockSpec((N_DEV, 8, D_MODEL), lambda i: (0, 0, 0))],
        out_specs=[blk] * 4,
        out_shape=[jax.ShapeDtypeStruct((r, cdim), F32)] * 4,
        compiler_params=_cparams(dimension_semantics=("arbitrary",)),
    )(w, m, v, call, rows)


def _adamw_small(ws, ms, vs, ssum, rows):
    n = len(ws)

    def body(*refs):
        w_r, m_r, v_r = refs[0:n], refs[n:2 * n], refs[2 * n:3 * n]
        ss_ref, rows_ref = refs[3 * n], refs[3 * n + 1]
        g_r, d_r, m2_r, v2_r = (refs[3 * n + 2 + k * n:3 * n + 2 + (k + 1) * n] for k in range(4))
        loss_ref = refs[7 * n + 2]
        j = 2 * lax.axis_index("x") + lax.axis_index("y")
        rsum = rows_ref[0]
        for d in range(1, N_DEV):
            rsum = rsum + rows_ref[d]
        taps = []
        for t in range(CONV_TAPS):
            row = ss_ref[t // 2:t // 2 + 1, :]
            c0 = CONV_W * (t % 2)
            pick = row[:, c0:c0 + 128]
            for k in range(1, N_CHIPS):
                pick = jnp.where(j == k, row[:, c0 + 128 * k:c0 + 128 * (k + 1)], pick)
            taps.append(pick)
        grads = [jnp.concatenate([rsum[2:3], rsum[3:4], rsum[0:1]], axis=1), rsum[4:5],
                 ss_ref[17:18, 512:512 + HEAD_DIM], ss_ref[17:18, 640:640 + HEAD_DIM], ss_ref[17:18, 768:776],
                 None, ss_ref[16:17, 0:CONV_W], ss_ref[16:17, CONV_W:2 * CONV_W], ss_ref[17:18, 0:CONV_W]]
        for i in range(n):
            if grads[i] is None:
                g = jnp.concatenate(taps, axis=0)
                w, m, v = (jnp.concatenate([ref[t] for t in range(CONV_TAPS)], axis=0) for ref in (w_r[i], m_r[i], v_r[i]))
                res = (g,) + _adamw_math(w, g, m, v)
                for ref, val in zip((g_r[i], d_r[i], m2_r[i], v2_r[i]), res):
                    for t in range(CONV_TAPS):
                        ref[t] = val[t:t + 1, :]
                continue
            g = grads[i]
            g_r[i][...] = g
            d_r[i][...], m2_r[i][...], v2_r[i][...] = _adamw_math(w_r[i][...], g, m_r[i][...], v_r[i][...])
        loss_ref[...] = (0.5 / D_MODEL) * jnp.sum(ss_ref[18:19, :], axis=1, keepdims=True)

    vm = pl.BlockSpec(memory_space=pltpu.VMEM)
    shapes = [jax.ShapeDtypeStruct(w.shape, F32) for w in ws]
    out = pl.pallas_call(
        body,
        name="adamw_small",
        in_specs=[vm] * (3 * n + 2),
        out_specs=[vm] * (4 * n + 1),
        out_shape=shapes * 4 + [jax.ShapeDtypeStruct((1, 1), F32)],
        compiler_params=_cparams(),
    )(*ws, *ms, *vs, ssum, rows)
    return out[0:n], out[n:2 * n], out[2 * n:3 * n], out[3 * n:4 * n], out[4 * n]


def _rope_tables(t):
    inv = ROPE_THETA ** (-jnp.arange(0, HEAD_DIM, 2, dtype=F32) / HEAD_DIM)
    ang = jnp.arange(t, dtype=F32)[:, None] * inv[None, :]
    cos, sin = jnp.cos(ang), jnp.sin(ang)
    return jnp.tile(cos, (1, 4)), jnp.tile(jnp.concatenate([-sin, sin], axis=1), (1, 2))


def kernel(x, c, w_ada, b_ada, norm_w, w_in, q_norm_w, k_norm_w, sinks, conv_w, conv_b, ln_w, ln_b, w_out, loss_target, m_w_ada, m_b_ada, m_norm_w, m_w_in, m_q_norm_w, m_k_norm_w, m_sinks, m_conv_w, m_conv_b, m_ln_w, m_ln_b, m_w_out, v_w_ada, v_b_ada, v_norm_w, v_w_in, v_q_norm_w, v_k_norm_w, v_sinks, v_conv_w, v_conv_b, v_ln_w, v_ln_b, v_w_out):
    xi, yi = lax.axis_index("x"), lax.axis_index("y")
    j = 2 * xi + yi
    x2, tgt = x[0], loss_target[0]
    t = x2.shape[0]

    wt_s, mt_s, vt_s = w_in[0].T, m_w_in[0].T, v_w_in[0].T
    by_tap = lambda a: jnp.transpose(a, (1, 0, 2))

    q_raw, kv_raw, ga, ua, ug, gb, h, w_full, call, ada4 = _in_proj_gather(
        x2, wt_s.reshape(2, IN_HALF, D_MODEL), c, w_ada[0], b_ada, norm_w)
    ada = ada4.reshape(1, 3 * D_MODEL)
    s1, gate = 1.0 + ada[:, D_MODEL:2 * D_MODEL], ada[:, 2 * D_MODEL:]

    cos_f, sin_s = _rope_tables(t)
    qkw2 = jnp.tile(jnp.concatenate([q_norm_w, k_norm_w], axis=0), (1, 2))

    o, mix_a, wo4, cw4 = _attn_fwd(q_raw, kv_raw, ga, qkw2, sinks, cos_f, sin_s,
                                   w_out[0].reshape(2, OUT_HALF, D_MODEL), by_tap(conv_w))
    w_out_full = wo4.reshape(D_MODEL, D_MODEL)
    cz, mix_b = _conv_fwd(ua, ug, gb, cw4, conv_b, ln_w, ln_b)
    dout, dmix_a, dmix_b, gwo_bf, red_o = _out_proj(mix_a, mix_b, x2, tgt, gate, w_out_full)

    dq, dkv, dga, sm_a, gwo = _attn_bwd(q_raw, kv_raw, ga, o, dmix_a, qkw2, sinks, cos_f, sin_s,
                                        gwo_bf.reshape(N_CHIPS, 2, OUT_HALF, D_MODEL))
    dua, dug, dgb, dcw, dvec = _conv_bwd(ua, ug, gb, cz, dmix_b, cw4, ln_w, ln_b)
    dparts = (dq, dkv, dga, dua, dug, dgb)

    grad_x, gw, ssum, rows = _in_proj_bwd(dparts, h, x2, dout, s1, norm_w, w_full, dcw, dvec, sm_a, red_o)

    gt_w_in = gw.reshape(2 * IN_HALF, D_MODEL)
    g_w_out = gwo.reshape(D_MODEL // N_CHIPS, D_MODEL)

    g_w_ada, d_w_ada, nm_w_ada, nv_w_ada = _adamw_ada(w_ada[0], m_w_ada[0], v_w_ada[0], call.reshape(N_DEV, D_MODEL), rows)
    gt_w_in, dt_w_in, nmt_w_in, nvt_w_in, grad_x = _adamw("adamw_w_in", wt_s, gt_w_in, mt_s, vt_s, 176, through=grad_x)
    g_w_in, d_w_in, nm_w_in, nv_w_in = gt_w_in.T, dt_w_in.T, nmt_w_in.T, nvt_w_in.T
    g_w_out, d_w_out, nm_w_out, nv_w_out = _adamw("adamw_w_out", w_out[0], g_w_out, m_w_out[0], v_w_out[0], 128)
    ws = [b_ada, norm_w, q_norm_w, k_norm_w, sinks, by_tap(conv_w), conv_b, ln_w, ln_b]
    ms = [m_b_ada, m_norm_w, m_q_norm_w, m_k_norm_w, m_sinks, by_tap(m_conv_w), m_conv_b, m_ln_w, m_ln_b]
    vs = [v_b_ada, v_norm_w, v_q_norm_w, v_k_norm_w, v_sinks, by_tap(v_conv_w), v_conv_b, v_ln_w, v_ln_b]
    gs, ds, nms, nvs, loss11 = _adamw_small(ws, ms, vs, ssum, rows)
    loss = loss11[0, 0]

    def order(ada_v, in_v, out_v, sm):
        b, nw_, qw_, kw_, sk_, cw_, cb_, lw_, lb_ = sm
        return [ada_v[None], b, nw_, in_v[None], qw_, kw_, sk_, by_tap(cw_), cb_, lw_, lb_, out_v[None]]

    grads = order(g_w_ada, g_w_in, g_w_out, gs)
    deltas = order(d_w_ada, d_w_in, d_w_out, ds)
    new_m = order(nm_w_ada, nm_w_in, nm_w_out, nms)
    new_v = order(nv_w_ada, nv_w_in, nv_w_out, nvs)
    return (loss, grad_x[None], *grads, *deltas, *new_m, *new_v)
```

```python
import functools

import jax
import jax.numpy as jnp
from jax import lax
from jax.experimental import pallas as pl
from jax.experimental.pallas import tpu as pltpu

F32 = jnp.float32
BF16 = jnp.bfloat16

D_MODEL = 1024
ATTN_W = 512
KV_W = 128
CONV_W = 512
IN_W = 2816
HEAD_DIM = 64
CONV_TAPS = 31
QBLK = 128
EPS = 1e-6
ROPE_THETA = 10000.0

ADAM_LR = 0.001
ADAM_B1 = 0.9
ADAM_B2 = 0.999
ADAM_EPS = 1e-08
ADAM_WD = 0.01
ADAM_STEP = 10

N_CHIPS = 4
N_DEV = 8
IN_HALF = IN_W // N_CHIPS // 2
OUT_HALF = D_MODEL // N_CHIPS // 2
ADA_SHARD = 3 * D_MODEL // N_CHIPS

VMEM_LIMIT = 56 * 1024 * 1024
CONV_PAD = 32


def _cparams(**kw):
    return pltpu.CompilerParams(vmem_limit_bytes=VMEM_LIMIT, **kw)


def _sigmoid(v):
    return 1.0 / (1.0 + jnp.exp(-v))


def _silu(v):
    return v * _sigmoid(v)


def _dsilu(v):
    s = _sigmoid(v)
    return s * (1.0 + v * (1.0 - s))


def _lane(shape):
    return lax.broadcasted_iota(jnp.int32, shape, len(shape) - 1)


PUT_ROWS = 512


def _fetch(hbm_refs, vmem_refs, sem):
    cps = [pltpu.make_async_copy(h, v, sem.at[i]) for i, (h, v) in enumerate(zip(hbm_refs, vmem_refs))]
    for cp in cps:
        cp.start()
    return cps


def _put(vmem_ref, hbm_ref, sem, m):
    r = pl.ds(pl.multiple_of(m * PUT_ROWS, PUT_ROWS), PUT_ROWS)
    return pltpu.make_async_copy(vmem_ref.at[r], hbm_ref.at[r], sem.at[m])


def _put_all(pairs, sems, m):
    for (v, h), sem in zip(pairs, sems):
        _put(v, h, sem, m).start()


def _put_wait(pairs, sems, n):
    for (v, h), sem in zip(pairs, sems):
        for m in range(n):
            _put(v, h, sem, m).wait()


def _head_mean(s, left):
    sl = jnp.sum(jnp.where(left, s, 0.0), axis=-1, keepdims=True)
    sr = jnp.sum(jnp.where(left, 0.0, s), axis=-1, keepdims=True)
    return jnp.where(left, sl, sr) * (1.0 / HEAD_DIM)


def _rot(v, first):
    return jnp.where(first, pltpu.roll(v, 96, 1), pltpu.roll(v, 32, 1))


def _norm_rope(v, w, cos, sin_s, left, first):
    r = lax.rsqrt(_head_mean(v * v, left) + EPS)
    xh = v * r
    n = xh * w
    return n * cos + _rot(n, first) * sin_s, xh, r


def _norm_rope_bwd(d, xh, r, w, cos, sin_s, left, first):
    dn = d * cos - _rot(d, first) * sin_s
    dw = jnp.sum(dn * xh, axis=0, keepdims=True)
    dxh = dn * w
    return r * (dxh - xh * _head_mean(dxh * xh, left)), dw


def _dup_heads(v, left):
    sw = pltpu.roll(v, 64, 1)
    return jnp.where(left, v, sw), jnp.where(left, sw, v)


def _prep_kv(kv_ref, kw_ref, cos_ref, sin_ref, ka_ref, va_ref, t):
    ch = 256
    for g in range(2):
        ka_ref[g, 0:QBLK, :] = jnp.zeros((QBLK, 128), BF16)
        va_ref[g, 0:QBLK, :] = jnp.zeros((QBLK, 128), BF16)

    def chunk(i, carry):
        r0 = pl.multiple_of(i * ch, ch)
        left = _lane((ch, 128)) < 64
        first = (_lane((ch, 128)) % 64) < 32
        k = kv_ref[pl.ds(r0, ch), 0:128]
        v = kv_ref[pl.ds(r0, ch), 128:256]
        kr, _, _ = _norm_rope(k, kw_ref[...], cos_ref[pl.ds(r0, ch), :], sin_ref[pl.ds(r0, ch), :], left, first)
        k0, k1 = _dup_heads(kr, left)
        v0, v1 = _dup_heads(v, left)
        ka_ref[0, pl.ds(QBLK + r0, ch), :] = k0.astype(BF16)
        ka_ref[1, pl.ds(QBLK + r0, ch), :] = k1.astype(BF16)
        va_ref[0, pl.ds(QBLK + r0, ch), :] = v0.astype(BF16)
        va_ref[1, pl.ds(QBLK + r0, ch), :] = v1.astype(BF16)
        return carry

    lax.fori_loop(0, t // ch, chunk, 0)


def _band_mask(n):
    qi = lax.broadcasted_iota(jnp.int32, (2 * QBLK, 2 * QBLK), 0) % QBLK
    kj = lax.broadcasted_iota(jnp.int32, (2 * QBLK, 2 * QBLK), 1)
    local = (kj > qi) & (kj <= qi + QBLK)
    return local & ((n > 0) | (kj >= QBLK))


def _softmax_pair(s, mask, sink0, sink1):
    row = lax.broadcasted_iota(jnp.int32, (2 * QBLK, 1), 0)
    sink = jnp.where(row < QBLK, sink0, sink1)
    s = jnp.where(mask, s, -jnp.inf)
    m = jnp.maximum(jnp.max(s, axis=-1, keepdims=True), sink)
    e = jnp.exp(s - m)
    es = jnp.exp(sink - m)
    inv = 1.0 / (jnp.sum(e, axis=-1, keepdims=True) + es)
    return e * inv, es * inv


def _stack_heads(v, left):
    return jnp.concatenate([jnp.where(left, v, 0.0), jnp.where(left, 0.0, v)], axis=0)


def _attn_fwd(q_raw, kv_raw, ga, qkw2, sinks, cos_f, sin_s, wo, cw):
    t = q_raw.shape[0]
    nblk = t // QBLK
    per_put = PUT_ROWS // QBLK

    def body(q_hbm, kv_ref, ga_hbm, qkw_ref, sk_ref, cos_hbm, sin_hbm, wo_ref, cw_ref,
             o_hbm, mix_hbm, wo4_ref, cw4_ref, ka_ref, va_ref, q_ref, ga_ref, o_ref, mix_ref, cos_ref, sin_ref,
             isem, osem0, osem1, ssem, rsem):
        qw_ref, kw_ref = qkw_ref.at[0:1], qkw_ref.at[1:2]
        loads = _fetch((cos_hbm, sin_hbm, q_hbm, ga_hbm), (cos_ref, sin_ref, q_ref, ga_ref), isem)
        outs, osems = ((o_ref, o_hbm), (mix_ref, mix_hbm)), (osem0, osem1)
        x, y, c, chips = _place()
        j = 2 * x + y
        sib = (x, y, 1 - c)
        idx = [2 * cx + cy for cx, cy in chips]
        rc = functools.partial(_remote, ssem, rsem)
        wo4_ref[j] = wo_ref[...].astype(BF16)
        for tap in range(CONV_TAPS):
            cw4_ref[j, tap:tap + 1, :] = cw_ref[tap]
        cw4_ref[j, CONV_TAPS:, :] = jnp.zeros((CONV_PAD - CONV_TAPS, 128), F32)
        sends = []
        for k, chip in enumerate(chips):
            sends.append(rc(k, wo4_ref.at[j, c], wo4_ref.at[j, c], (*chip, c)))
            sends.append(rc(6 + k, cw4_ref.at[j], cw4_ref.at[j], (*chip, c)))
        for cp in sends:
            cp.start()

        loads[0].wait()
        loads[1].wait()
        _prep_kv(kv_ref, kw_ref, cos_ref, sin_ref, ka_ref, va_ref, t)
        loads[2].wait()
        loads[3].wait()

        def blk(n, carry):
            r0 = pl.multiple_of(n * QBLK, QBLK)
            left = _lane((QBLK, 128)) < 64
            first = (_lane((QBLK, 128)) % 64) < 32
            cos = cos_ref[pl.ds(r0, QBLK), :]
            sin = sin_ref[pl.ds(r0, QBLK), :]
            mask = _band_mask(n)
            scores = []
            for p in range(4):
                lanes = slice(p * 128, (p + 1) * 128)
                qr, _, _ = _norm_rope(q_ref[pl.ds(r0, QBLK), lanes], qw_ref[...], cos, sin, left, first)
                q2 = _stack_heads(qr * 0.125, left).astype(BF16)
                scores.append(lax.dot_general(q2, ka_ref[p // 2, pl.ds(r0, 2 * QBLK), :], (((1,), (1,)), ((), ())),
                                              preferred_element_type=F32))
            probs = [_softmax_pair(scores[p], mask, sk_ref[0, 2 * p], sk_ref[0, 2 * p + 1])[0].astype(BF16)
                     for p in range(4)]
            for p in range(4):
                lanes = slice(p * 128, (p + 1) * 128)
                o2 = jnp.dot(probs[p], va_ref[p // 2, pl.ds(r0, 2 * QBLK), :], preferred_element_type=F32)
                o = jnp.where(left, o2[0:QBLK], o2[QBLK:2 * QBLK])
                o_ref[pl.ds(r0, QBLK), lanes] = o.astype(BF16)
                mix_ref[pl.ds(r0, QBLK), lanes] = (o * _silu(ga_ref[pl.ds(r0, QBLK), lanes])).astype(BF16)

            @pl.when(n % per_put == per_put - 1)
            def _():
                _put_all(outs, osems, n // per_put)

            return carry

        lax.fori_loop(0, nblk, blk, 0)
        _put_wait(outs, osems, t // PUT_ROWS)

        passed = []
        for k, chip in enumerate(chips):
            jk = idx[k]
            rc(k, wo4_ref.at[jk, c], wo4_ref.at[jk, c], sib).wait_recv()
            passed.append(rc(3 + k, wo4_ref.at[jk, c], wo4_ref.at[jk, c], sib))
            passed[-1].start()
        for k, chip in enumerate(chips):
            jk = idx[k]
            rc(3 + k, wo4_ref.at[jk, 1 - c], wo4_ref.at[jk, 1 - c], sib).wait_recv()
            rc(6 + k, cw4_ref.at[jk], cw4_ref.at[jk], sib).wait_recv()
        for cp in sends + passed:
            cp.wait_send()

    vm = pl.BlockSpec(memory_space=pltpu.VMEM)
    hbm = pl.BlockSpec(memory_space=pl.ANY)
    n_sem = 9
    return pl.pallas_call(
        body,
        name="attn_fwd",
        in_specs=[hbm, vm, hbm, vm, pl.BlockSpec(memory_space=pltpu.SMEM), hbm, hbm, vm, vm],
        out_specs=[hbm, hbm, vm, vm],
        out_shape=[jax.ShapeDtypeStruct((t, ATTN_W), BF16), jax.ShapeDtypeStruct((t, ATTN_W), BF16),
                   jax.ShapeDtypeStruct((N_CHIPS, 2, OUT_HALF, D_MODEL), BF16),
                   jax.ShapeDtypeStruct((N_CHIPS, 32, 128), F32)],
        scratch_shapes=[pltpu.VMEM((2, t + QBLK, 128), BF16), pltpu.VMEM((2, t + QBLK, 128), BF16),
                        pltpu.VMEM((t, ATTN_W), F32), pltpu.VMEM((t, ATTN_W), F32),
                        pltpu.VMEM((t, ATTN_W), BF16), pltpu.VMEM((t, ATTN_W), BF16),
                        pltpu.VMEM((t, 128), F32), pltpu.VMEM((t, 128), F32),
                        pltpu.SemaphoreType.DMA((4,)), pltpu.SemaphoreType.DMA((t // PUT_ROWS,)),
                        pltpu.SemaphoreType.DMA((t // PUT_ROWS,)),
                        pltpu.SemaphoreType.DMA((n_sem,)), pltpu.SemaphoreType.DMA((n_sem,))],
        compiler_params=_cparams(),
    )(q_raw, kv_raw, ga, qkw2, sinks, cos_f, sin_s, wo, cw)


def _attn_bwd(q_raw, kv_raw, ga, o, dmix, qkw2, sinks, cos_f, sin_s, go):
    t = q_raw.shape[0]
    nblk = t // QBLK
    per_put = PUT_ROWS // QBLK

    def body(q_hbm, kv_ref, ga_hbm, o_hbm, dm_hbm, qkw_ref, sk_ref, cos_hbm, sin_hbm, go_ref,
             dq_hbm, dkv_ref, dga_hbm, sm_ref, gwo_ref, ka_ref, va_ref, dka_ref, dva_ref,
             sibo_ref, outo_ref, ino_ref, q_ref, ga_ref, o_ref, dm_ref, dq_ref, dga_ref, cos_ref, sin_ref,
             isem, osem0, osem1, ssem, rsem):
        qw_ref, kw_ref = qkw_ref.at[0:1], qkw_ref.at[1:2]
        loads = _fetch((cos_hbm, sin_hbm, q_hbm, ga_hbm, o_hbm, dm_hbm), (cos_ref, sin_ref, q_ref, ga_ref, o_ref, dm_ref), isem)
        outs, osems = ((dq_ref, dq_hbm), (dga_ref, dga_hbm)), (osem0, osem1)
        x, y, c, chips = _place()
        sib = (x, y, 1 - c)
        rc = functools.partial(_remote, ssem, rsem)
        theirs, mine = go_ref.at[:, 1 - c], go_ref.at[:, c]
        sends = [_rs_to_sibling(rc, 0, theirs, sibo_ref, sib)]
        loads[0].wait()
        loads[1].wait()
        _prep_kv(kv_ref, kw_ref, cos_ref, sin_ref, ka_ref, va_ref, t)
        dka_ref[...] = jnp.zeros_like(dka_ref)
        dva_ref[...] = jnp.zeros_like(dva_ref)
        sends += _rs_trade(rc, 0, theirs, mine, sibo_ref, outo_ref, ino_ref, OUT_HALF, c, sib, chips)
        for cp in loads[2:]:
            cp.wait()

        def blk(n, carry):
            dqw, dsk = carry
            r0 = pl.multiple_of(n * QBLK, QBLK)
            left = _lane((QBLK, 128)) < 64
            first = (_lane((QBLK, 128)) % 64) < 32
            cos = cos_ref[pl.ds(r0, QBLK), :]
            sin = sin_ref[pl.ds(r0, QBLK), :]
            mask = _band_mask(n)
            row = lax.broadcasted_iota(jnp.int32, (2 * QBLK, 1), 0)
            rows = pl.ds(r0, QBLK)
            win = pl.ds(r0, 2 * QBLK)
            lane_of = [slice(p * 128, (p + 1) * 128) for p in range(4)]
            for grp in ((0, 1), (2, 3)):
                qn = {p: _norm_rope(q_ref[rows, lane_of[p]], qw_ref[...], cos, sin, left, first) for p in grp}
                q2 = {p: _stack_heads(qn[p][0] * 0.125, left).astype(BF16) for p in grp}
                sc = {p: lax.dot_general(q2[p], ka_ref[p // 2, win, :], (((1,), (1,)), ((), ())),
                                         preferred_element_type=F32) for p in grp}
                do2 = {}
                for p in grp:
                    gav = ga_ref[rows, lane_of[p]]
                    dmv = dm_ref[rows, lane_of[p]].astype(F32)
                    dga_ref[rows, lane_of[p]] = (dmv * o_ref[rows, lane_of[p]].astype(F32) * _dsilu(gav)).astype(BF16)
                    do2[p] = _stack_heads(dmv * _silu(gav), left).astype(BF16)
                dpm = {p: lax.dot_general(do2[p], va_ref[p // 2, win, :], (((1,), (1,)), ((), ())),
                                          preferred_element_type=F32) for p in grp}
                sm = {p: _softmax_pair(sc[p], mask, sk_ref[0, 2 * p], sk_ref[0, 2 * p + 1]) for p in grp}
                dsl = {}
                for p in grp:
                    pm, ps = sm[p]
                    delta = jnp.sum(pm * dpm[p], axis=-1, keepdims=True)
                    dsl[p] = (pm * (dpm[p] - delta)).astype(BF16)
                    pd = ps * delta
                    d0 = jnp.sum(jnp.where(row < QBLK, pd, 0.0), axis=0, keepdims=True)
                    d1 = jnp.sum(jnp.where(row < QBLK, 0.0, pd), axis=0, keepdims=True)
                    l8 = _lane((1, 128))
                    dsk = dsk - jnp.where(l8 == 2 * p, d0, 0.0) - jnp.where(l8 == 2 * p + 1, d1, 0.0)
                for p in grp:
                    g = p // 2
                    dva_ref[g, win, :] += lax.dot_general(sm[p][0].astype(BF16), do2[p], (((0,), (0,)), ((), ())),
                                                          preferred_element_type=F32)
                    dka_ref[g, win, :] += lax.dot_general(dsl[p], q2[p], (((0,), (0,)), ((), ())),
                                                          preferred_element_type=F32)
                for p in grp:
                    dq2 = jnp.dot(dsl[p], ka_ref[p // 2, win, :], preferred_element_type=F32)
                    dqr = jnp.where(left, dq2[0:QBLK], dq2[QBLK:2 * QBLK]) * 0.125
                    dq, dw = _norm_rope_bwd(dqr, qn[p][1], qn[p][2], qw_ref[...], cos, sin, left, first)
                    dq_ref[rows, lane_of[p]] = dq.astype(BF16)
                    dqw = dqw + dw

            @pl.when(n % per_put == per_put - 1)
            def _():
                _put_all(outs, osems, n // per_put)

            return dqw, dsk

        zero = jnp.zeros((1, 128), F32)
        dqw, dsk = lax.fori_loop(0, nblk, blk, (zero, zero))

        ch = 256

        def chunk(i, dkw):
            r0 = pl.multiple_of(i * ch, ch)
            left = _lane((ch, 128)) < 64
            first = (_lane((ch, 128)) % 64) < 32
            rows = pl.ds(r0, ch)
            prow = pl.ds(QBLK + r0, ch)

            def fold(ref):
                a0 = ref[0, prow, :]
                a1 = ref[1, prow, :]
                return jnp.where(left, a0 + pltpu.roll(a0, 64, 1), a1 + pltpu.roll(a1, 64, 1))

            cos = cos_ref[rows, :]
            sin = sin_ref[rows, :]
            _, xh, r = _norm_rope(kv_ref[rows, 0:128], kw_ref[...], cos, sin, left, first)
            dk, dw = _norm_rope_bwd(fold(dka_ref), xh, r, kw_ref[...], cos, sin, left, first)
            dkv_ref[rows, 0:128] = dk.astype(BF16)
            dkv_ref[rows, 128:256] = fold(dva_ref).astype(BF16)
            return dkw + dw

        dkw = lax.fori_loop(0, t // ch, chunk, zero)
        sm_ref[...] = jnp.zeros((8, 128), F32)
        sm_ref[0:1, :] = dqw + pltpu.roll(dqw, 64, 1)
        sm_ref[1:2, :] = dkw + pltpu.roll(dkw, 64, 1)
        sm_ref[2:3, :] = dsk

        j = 2 * x + y
        sends.append(_rs_total(rc, 0, mine, sibo_ref, outo_ref, ino_ref, gwo_ref, OUT_HALF, j, c, sib))
        _rs_done(rc, 0, gwo_ref, c, sib)
        for cp in sends:
            cp.wait_send()
        _put_wait(outs, osems, t // PUT_ROWS)

    vm = pl.BlockSpec(memory_space=pltpu.VMEM)
    hbm = pl.BlockSpec(memory_space=pl.ANY)
    return pl.pallas_call(
        body,
        name="attn_bwd",
        in_specs=[hbm, vm, hbm, hbm, hbm, vm, pl.BlockSpec(memory_space=pltpu.SMEM), hbm, hbm, vm],
        out_specs=[hbm, vm, hbm, vm, vm],
        out_shape=[jax.ShapeDtypeStruct((t, ATTN_W), BF16), jax.ShapeDtypeStruct((t, 2 * KV_W), BF16),
                   jax.ShapeDtypeStruct((t, ATTN_W), BF16), jax.ShapeDtypeStruct((8, 128), F32),
                   jax.ShapeDtypeStruct((2, OUT_HALF, D_MODEL), F32)],
        scratch_shapes=[pltpu.VMEM((2, t + QBLK, 128), BF16), pltpu.VMEM((2, t + QBLK, 128), BF16),
                        pltpu.VMEM((2, t + QBLK, 128), F32), pltpu.VMEM((2, t + QBLK, 128), F32)]
        + _rs_scratch(OUT_HALF)
        + [pltpu.VMEM((t, ATTN_W), F32), pltpu.VMEM((t, ATTN_W), F32), pltpu.VMEM((t, ATTN_W), BF16),
           pltpu.VMEM((t, ATTN_W), BF16), pltpu.VMEM((t, ATTN_W), BF16), pltpu.VMEM((t, ATTN_W), BF16),
           pltpu.VMEM((t, 128), F32), pltpu.VMEM((t, 128), F32),
           pltpu.SemaphoreType.DMA((6,)), pltpu.SemaphoreType.DMA((t // PUT_ROWS,)), pltpu.SemaphoreType.DMA((t // PUT_ROWS,)),
           pltpu.SemaphoreType.DMA((RS_SEMS,)), pltpu.SemaphoreType.DMA((RS_SEMS,))],
        compiler_params=_cparams(),
    )(q_raw, kv_raw, ga, o, dmix, qkw2, sinks, cos_f, sin_s, go)


CONV_CH = 256
CONV_SUB = 128
CONV_ACCS = 1


def _shifted_windows(src_ref, r0, sh_ref):
    rows = CONV_CH + CONV_PAD
    win = src_ref[pl.ds(r0, rows), :]
    for b in range(8):
        sh = win if b == 0 else pltpu.roll(win, rows - b, 0)
        for c in range(CONV_W // 128):
            sh_ref[b, c] = sh[:, c * 128:(c + 1) * 128]


def _conv_fwd(ua, ug, gb, cw, cb, lw, lb):
    t = ua.shape[0]

    def body(ua_hbm, ug_hbm, gb_hbm, cw_ref, cb_ref, lw_ref, lb_ref, cz_hbm, mix_hbm, zp_ref, sh_ref,
             ua_ref, ug_ref, gb_ref, cz_ref, mix_ref, isem, osem0, osem1):
        loads = _fetch((ua_hbm, ug_hbm, gb_hbm), (ua_ref, ug_ref, gb_ref), isem)
        outs, osems = ((cz_ref, cz_hbm), (mix_ref, mix_hbm)), (osem0, osem1)
        per_put = PUT_ROWS // CONV_CH
        zp_ref[0:CONV_PAD, :] = jnp.zeros((CONV_PAD, CONV_W), F32)
        loads[0].wait()
        loads[1].wait()

        def glu(i, carry):
            r0 = pl.multiple_of(i * CONV_CH, CONV_CH)
            rows = pl.ds(r0, CONV_CH)
            zp_ref[pl.ds(CONV_PAD + r0, CONV_CH), :] = ua_ref[rows, :] * _sigmoid(ug_ref[rows, :])
            return carry

        lax.fori_loop(0, t // CONV_CH, glu, 0)
        loads[2].wait()

        def chunk(i, carry):
            r0 = pl.multiple_of(i * CONV_CH, CONV_CH)
            _shifted_windows(zp_ref, r0, sh_ref)
            for c in range(CONV_W // 128):
                lanes = slice(c * 128, (c + 1) * 128)

                def sub(k, carry2):
                    b0 = pl.multiple_of(k * CONV_SUB, CONV_SUB)
                    acc = [jnp.broadcast_to(cb_ref[0:1, lanes], (CONV_SUB, 128))] + [None] * (CONV_ACCS - 1)
                    for j in range(CONV_TAPS):
                        off = j + CONV_PAD - (CONV_TAPS - 1)
                        term = sh_ref[off % 8, c, pl.ds(b0 + 8 * (off // 8), CONV_SUB), :] * cw_ref[c, j:j + 1, :]
                        acc[j % CONV_ACCS] = term if acc[j % CONV_ACCS] is None else acc[j % CONV_ACCS] + term
                    cz_ref[pl.ds(r0 + b0, CONV_SUB), lanes] = functools.reduce(lambda a, b: a + b, acc)
                    return carry2

                lax.fori_loop(0, CONV_CH // CONV_SUB, sub, 0)
            rows = pl.ds(r0, CONV_CH)
            cz = cz_ref[rows, :]
            mu = jnp.mean(cz, axis=-1, keepdims=True)
            xc = cz - mu
            rs = lax.rsqrt(jnp.mean(xc * xc, axis=-1, keepdims=True) + EPS)
            ln = xc * rs * lw_ref[...] + lb_ref[...]
            mix_ref[rows, :] = (_silu(ln) * _silu(gb_ref[rows, :])).astype(BF16)

            @pl.when(i % per_put == per_put - 1)
            def _():
                _put_all(outs, osems, i // per_put)

            return carry

        lax.fori_loop(0, t // CONV_CH, chunk, 0)
        _put_wait(outs, osems, t // PUT_ROWS)

    vm = pl.BlockSpec(memory_space=pltpu.VMEM)
    hbm = pl.BlockSpec(memory_space=pl.ANY)
    nput = t // PUT_ROWS
    return pl.pallas_call(
        body,
        name="conv_fwd",
        in_specs=[hbm] * 3 + [vm] * 4,
        out_specs=[hbm, hbm],
        out_shape=[jax.ShapeDtypeStruct((t, CONV_W), F32), jax.ShapeDtypeStruct((t, CONV_W), BF16)],
        scratch_shapes=[pltpu.VMEM((t + CONV_PAD, CONV_W), F32),
                        pltpu.VMEM((8, CONV_W // 128, CONV_CH + CONV_PAD, 128), F32),
                        pltpu.VMEM((t, CONV_W), F32), pltpu.VMEM((t, CONV_W), F32), pltpu.VMEM((t, CONV_W), F32),
                        pltpu.VMEM((t, CONV_W), F32), pltpu.VMEM((t, CONV_W), BF16),
                        pltpu.SemaphoreType.DMA((3,)), pltpu.SemaphoreType.DMA((nput,)), pltpu.SemaphoreType.DMA((nput,))],
        compiler_params=_cparams(),
    )(ua, ug, gb, cw, cb, lw, lb)


def _conv_bwd(ua, ug, gb, cz, dmix, cw, lw, lb):
    t = ua.shape[0]

    def body(ua_hbm, ug_hbm, gb_hbm, cz_hbm, dm_hbm, cw_ref, lw_ref, lb_ref,
             dua_hbm, dug_hbm, dgb_hbm, dcw_ref, dvec_ref, zp_ref, dp_ref, sh_ref, wacc_ref,
             ua_ref, ug_ref, gb_ref, cz_ref, dm_ref, dua_ref, dug_ref, dgb_ref, isem, osem0, osem1, osem2):
        loads = _fetch((ua_hbm, ug_hbm, gb_hbm, cz_hbm, dm_hbm), (ua_ref, ug_ref, gb_ref, cz_ref, dm_ref), isem)
        per_put = PUT_ROWS // CONV_CH
        zp_ref[0:CONV_PAD, :] = jnp.zeros((CONV_PAD, CONV_W), F32)
        dp_ref[t:t + CONV_PAD, :] = jnp.zeros((CONV_PAD, CONV_W), F32)
        wacc_ref[...] = jnp.zeros_like(wacc_ref)
        for cp in loads:
            cp.wait()

        def pointwise(i, carry):
            dcb, dlw, dlb = carry
            r0 = pl.multiple_of(i * CONV_CH, CONV_CH)
            rows = pl.ds(r0, CONV_CH)
            zp_ref[pl.ds(CONV_PAD + r0, CONV_CH), :] = ua_ref[rows, :] * _sigmoid(ug_ref[rows, :])
            cz = cz_ref[rows, :]
            mu = jnp.mean(cz, axis=-1, keepdims=True)
            xc = cz - mu
            rs = lax.rsqrt(jnp.mean(xc * xc, axis=-1, keepdims=True) + EPS)
            xh = xc * rs
            ln = xh * lw_ref[...] + lb_ref[...]
            gbv = gb_ref[rows, :]
            dy = dm_ref[rows, :].astype(F32)
            dgb_ref[rows, :] = (dy * _silu(ln) * _dsilu(gbv)).astype(BF16)
            dl = dy * _silu(gbv) * _dsilu(ln)
            dxh = dl * lw_ref[...]
            dcz = rs * (dxh - jnp.mean(dxh, axis=-1, keepdims=True)
                        - xh * jnp.mean(dxh * xh, axis=-1, keepdims=True))
            dp_ref[rows, :] = dcz

            @pl.when(i % per_put == per_put - 1)
            def _():
                _put(dgb_ref, dgb_hbm, osem2, i // per_put).start()

            return (dcb + jnp.sum(dcz, axis=0, keepdims=True),
                    dlw + jnp.sum(dl * xh, axis=0, keepdims=True),
                    dlb + jnp.sum(dl, axis=0, keepdims=True))

        zero = jnp.zeros((1, CONV_W), F32)
        dcb, dlw, dlb = lax.fori_loop(0, t // CONV_CH, pointwise, (zero, zero, zero))
        dvec_ref[...] = jnp.zeros((8, CONV_W), F32)
        dvec_ref[0:1, :] = dcb
        dvec_ref[1:2, :] = dlw
        dvec_ref[2:3, :] = dlb

        def chunk(i, carry):
            r0 = pl.multiple_of(i * CONV_CH, CONV_CH)
            _shifted_windows(dp_ref, r0, sh_ref)
            for c in range(CONV_W // 128):
                lanes = slice(c * 128, (c + 1) * 128)

                def sub(k, carry2):
                    b0 = pl.multiple_of(k * CONV_SUB, CONV_SUB)
                    acc = [None] * CONV_ACCS
                    for j in range(CONV_TAPS):
                        off = CONV_TAPS - 1 - j
                        term = sh_ref[off % 8, c, pl.ds(b0 + 8 * (off // 8), CONV_SUB), :] * cw_ref[c, j:j + 1, :]
                        acc[j % CONV_ACCS] = term if acc[j % CONV_ACCS] is None else acc[j % CONV_ACCS] + term
                    acc = functools.reduce(lambda a, b: a + b, acc)
                    rr = pl.ds(r0 + b0, CONV_SUB)
                    sg = _sigmoid(ug_ref[rr, lanes])
                    dua_ref[rr, lanes] = (acc * sg).astype(BF16)
                    dug_ref[rr, lanes] = (acc * ua_ref[rr, lanes] * sg * (1.0 - sg)).astype(BF16)
                    return carry2

                lax.fori_loop(0, CONV_CH // CONV_SUB, sub, 0)
            _shifted_windows(zp_ref, r0, sh_ref)
            for c in range(CONV_W // 128):
                lanes = slice(c * 128, (c + 1) * 128)

                def subw(k, carry2):
                    b0 = pl.multiple_of(k * CONV_SUB, CONV_SUB)
                    dcz = dp_ref[pl.ds(r0 + b0, CONV_SUB), lanes]
                    for j in range(CONV_TAPS):
                        off = j + CONV_PAD - (CONV_TAPS - 1)
                        pr = dcz * sh_ref[off % 8, c, pl.ds(b0 + 8 * (off // 8), CONV_SUB), :]
                        parts = [pr[8 * q:8 * (q + 1)] for q in range(CONV_SUB // 8)]
                        while len(parts) > 1:
                            parts = [a + b for a, b in zip(parts[0::2], parts[1::2])]
                        wacc_ref[8 * j:8 * (j + 1), lanes] += parts[0]
                    return carry2

                lax.fori_loop(0, CONV_CH // CONV_SUB, subw, 0)

            @pl.when(i % per_put == per_put - 1)
            def _():
                _put_all(((dua_ref, dua_hbm), (dug_ref, dug_hbm)), (osem0, osem1), i // per_put)

            return carry

        lax.fori_loop(0, t // CONV_CH, chunk, 0)
        _put_wait(((dua_ref, dua_hbm), (dug_ref, dug_hbm), (dgb_ref, dgb_hbm)), (osem0, osem1, osem2), t // PUT_ROWS)
        dcw_ref[...] = jnp.zeros((16, 2 * CONV_W), F32)
        for j in range(CONV_TAPS):
            dcw_ref[j // 2:j // 2 + 1, CONV_W * (j % 2):CONV_W * (j % 2 + 1)] = jnp.sum(
                wacc_ref[8 * j:8 * (j + 1), :], axis=0, keepdims=True)

    vm = pl.BlockSpec(memory_space=pltpu.VMEM)
    hbm = pl.BlockSpec(memory_space=pl.ANY)
    return pl.pallas_call(
        body,
        name="conv_bwd",
        in_specs=[hbm] * 5 + [vm] * 3,
        out_specs=[hbm] * 3 + [vm] * 2,
        out_shape=[jax.ShapeDtypeStruct((t, CONV_W), BF16)] * 3
        + [jax.ShapeDtypeStruct((16, 2 * CONV_W), F32), jax.ShapeDtypeStruct((8, CONV_W), F32)],
        scratch_shapes=[pltpu.VMEM((t + CONV_PAD, CONV_W), F32), pltpu.VMEM((t + CONV_PAD, CONV_W), F32),
                        pltpu.VMEM((8, CONV_W // 128, CONV_CH + CONV_PAD, 128), F32), pltpu.VMEM((8 * 32, CONV_W), F32)]
        + [pltpu.VMEM((t, CONV_W), F32)] * 4 + [pltpu.VMEM((t, CONV_W), BF16)] * 4
        + [pltpu.SemaphoreType.DMA((5,))] + [pltpu.SemaphoreType.DMA((t // PUT_ROWS,))] * 3,
        compiler_params=_cparams(),
    )(ua, ug, gb, cz, dmix, cw, lw, lb)


def _out_proj(mix_a, mix_b, x, tgt, gate, w_out):
    t = x.shape[0]
    tm = 512
    nstep = t // tm

    def body(ma_ref, mb_ref, x_ref, t_ref, g_ref, w_ref, dout_ref, dma_ref, dmb_ref, gw_ref, red_ref, acc_ref):
        i = pl.program_id(0)

        @pl.when(i == 0)
        def _():
            acc_ref[...] = jnp.zeros_like(acc_ref)
            red_ref[...] = jnp.zeros_like(red_ref)

        mix = jnp.concatenate([ma_ref[...], mb_ref[...]], axis=1)
        y = jnp.dot(mix, w_ref[...], preferred_element_type=F32)
        gate_v = g_ref[...]
        err = x_ref[...] + gate_v * y - t_ref[...]
        dout = err * (1.0 / D_MODEL)
        dout_ref[...] = dout
        red_ref[0:1, :] += jnp.sum(dout * y, axis=0, keepdims=True)
        red_ref[1:2, :] += jnp.sum(err * err, axis=0, keepdims=True)
        dy = (dout * gate_v).astype(BF16)
        dmix = lax.dot_general(dy, w_ref[...], (((1,), (1,)), ((), ())), preferred_element_type=F32)
        dma_ref[...] = dmix[:, 0:512].astype(BF16)
        dmb_ref[...] = dmix[:, 512:1024].astype(BF16)
        acc_ref[...] += lax.dot_general(mix, dy, (((0,), (0,)), ((), ())), preferred_element_type=F32)

        @pl.when(i == nstep - 1)
        def _():
            gw_ref[...] = acc_ref[...].astype(BF16)

    row = lambda w: pl.BlockSpec((tm, w), lambda i: (i, 0))
    const = lambda s: pl.BlockSpec(s, lambda i: (0, 0))
    return pl.pallas_call(
        body,
        name="out_proj",
        grid=(nstep,),
        in_specs=[row(512), row(512), row(D_MODEL), row(D_MODEL), const((1, D_MODEL)),
                  pl.BlockSpec((D_MODEL, D_MODEL), lambda i: (0, 0), pipeline_mode=pl.Buffered(1))],
        out_specs=[row(D_MODEL), row(512), row(512), const((D_MODEL, D_MODEL)), const((8, D_MODEL))],
        out_shape=[jax.ShapeDtypeStruct((t, D_MODEL), F32), jax.ShapeDtypeStruct((t, 512), BF16),
                   jax.ShapeDtypeStruct((t, 512), BF16), jax.ShapeDtypeStruct((D_MODEL, D_MODEL), BF16),
                   jax.ShapeDtypeStruct((8, D_MODEL), F32)],
        scratch_shapes=[pltpu.VMEM((D_MODEL, D_MODEL), F32)],
        compiler_params=_cparams(dimension_semantics=("arbitrary",)),
    )(mix_a, mix_b, x, tgt, gate, w_out)


DPROJ_WIDTHS = (512, 256, 512, 512, 512, 512)
DPROJ_STARTS = (0, 512, 768, 1280, 1792, 2304)
WIN_W = 768
WIN_START = (0, 640, 1408, 2048)
WIN_OFF = (0, 64, 0, 64)
N_GW = N_CHIPS


def _window_pieces(s):
    lo, hi = WIN_START[s], WIN_START[s] + WIN_W
    out = []
    for p, (st, w) in enumerate(zip(DPROJ_STARTS, DPROJ_WIDTHS)):
        a, b = max(lo, st), min(hi, st + w)
        if a < b:
            out.append((p, a - st, b - a, a - lo))
    return out


def _in_proj_bwd(dparts, h, x, dout, s1, nw, wt_full, dcw, dvec, sm_a, row0):
    t = x.shape[0]
    tm = 256
    nstep = N_GW + t // tm
    n_sem = 20
    rows0 = 32
    hs = rows0 // 2
    npart = len(DPROJ_WIDTHS)

    def body(*refs):
        d_hbm, d_ref = refs[:npart], refs[npart:2 * npart]
        (x_ref, dout_ref, s1_ref, nw_ref, h_ref, wt_hbm, dcw_ref, dvec_ref, sma_ref, row0_ref,
         gx_ref, gw_hbm, ssum_ref, rows_ref,
         stg_ref, wt_ref, gt_ref, sib_ref, out_ref, in_ref, res_ref, sall_ref, red_ref, sm0_ref, ssib_ref, schip_ref, sres_ref,
         wsem, lsem, ssem, rsem) = refs[2 * npart:]
        i = pl.program_id(0)
        x_, y_, c, chips = _place()
        j = 2 * x_ + y_
        dev = 2 * j + c
        sib = (x_, y_, 1 - c)
        rc = functools.partial(_remote, ssem, rsem)
        rel_chip = [2 * cx + cy for cx, cy in chips] + [j]
        peers = [(px, py, pc) for px in (x_, 1 - x_) for py in (y_, 1 - y_) for pc in (c, 1 - c)][1:]
        wt_copy = pltpu.make_async_copy(wt_hbm, wt_ref, lsem.at[0])

        def window(case, slot):
            return [pltpu.make_async_copy(d_hbm[p].at[:, pl.ds(c0, w)], stg_ref.at[slot, :, pl.ds(w0, w)], wsem.at[slot, n])
                    for n, (p, c0, w, w0) in enumerate(_window_pieces(case))]

        def to_sibling(k):
            return rc(k, gt_ref.at[k, 1 - c], sib_ref.at[k], sib)

        def to_chip(k):
            return rc(4 + k, out_ref.at[k], in_ref.at[k], (*chips[k], c))

        def trade(k):
            to_sibling(k).wait_recv()

            def add(n, carry):
                rr = pl.ds(pl.multiple_of(n * RS_CH, RS_CH), RS_CH)
                out_ref[k, rr, :] = (gt_ref[k, c, rr, :].astype(F32) + sib_ref[k, rr, :].astype(F32)).astype(BF16)
                return carry

            lax.fori_loop(0, IN_HALF // RS_CH, add, 0)
            to_chip(k).start()

        def keep(k, first, vals):
            for half in range(2):
                lo, hi = max(first, IN_HALF * half), min(first + vals.shape[0], IN_HALF * (half + 1))
                if lo < hi:
                    gt_ref[k, half, lo - IN_HALF * half:hi - IN_HALF * half, :] = vals[lo - first:hi - first].astype(BF16)

        mine_s = pl.ds(pl.multiple_of(c * hs, 8), hs)
        other_s = pl.ds(pl.multiple_of((1 - c) * hs, 8), hs)

        def small_to_sibling():
            return rc(15, sm0_ref.at[other_s], ssib_ref, sib)

        def small_to_chip(k):
            return rc(16 + k, schip_ref.at[j], schip_ref.at[j], (*chips[k], c))

        def small_share():
            return rc(19, sres_ref.at[c], sres_ref.at[c], sib)

        for k in range(N_GW):
            @pl.when(i == k)
            def _(k=k):
                slot = k % 2
                if k == 0:
                    red_ref[...] = jnp.zeros_like(red_ref)
                    wt_copy.start()
                    sm0_ref[...] = jnp.zeros_like(sm0_ref)
                    sm0_ref[0:16, :] = dcw_ref[...]
                    sm0_ref[16:17, 0:CONV_W] = dvec_ref[0:1, :]
                    sm0_ref[16:17, CONV_W:2 * CONV_W] = dvec_ref[1:2, :]
                    sm0_ref[17:18, 0:CONV_W] = dvec_ref[2:3, :]
                    for r in range(3):
                        sm0_ref[17:18, CONV_W + 128 * r:CONV_W + 128 * (r + 1)] = sma_ref[r:r + 1, :]
                    sm0_ref[18:19, :] = row0_ref[1:2, :]
                    small_to_sibling().start()
                if k == 1:
                    small_to_sibling().wait_recv()
                    schip_ref[j] = sm0_ref[mine_s, :] + ssib_ref[...]
                    for kk in range(3):
                        small_to_chip(kk).start()
                if k == N_GW - 1:
                    for kk in range(3):
                        jk = rel_chip[kk]
                        rc(16 + kk, schip_ref.at[jk], schip_ref.at[jk], sib).wait_recv()
                    tot = schip_ref[0]
                    for d in range(1, N_CHIPS):
                        tot = tot + schip_ref[d]
                    sres_ref[c] = tot
                    small_share().start()
                for case in range(N_CHIPS):
                    if k == 0:
                        @pl.when(rel_chip[0] == case)
                        def _():
                            for cp in window(case, 0):
                                cp.start()
                    if k + 1 < N_GW:
                        @pl.when(rel_chip[k + 1] == case)
                        def _():
                            for cp in window(case, 1 - slot):
                                cp.start()
                for case in range(N_CHIPS):
                    @pl.when(rel_chip[k] == case)
                    def _():
                        for cp in window(case, slot):
                            cp.wait()
                for part in range(2):
                    cols = pl.ds(part * (WIN_W // 2), WIN_W // 2)
                    g = lax.dot_general(stg_ref[slot, :, cols], h_ref[...], (((0,), (0,)), ((), ())),
                                        preferred_element_type=F32)
                    for off in sorted(set(WIN_OFF)):
                        @pl.when(rel_chip[k] % 2 == (1 if off else 0))
                        def _():
                            keep(k, part * (WIN_W // 2) - off, g)
                    if part == 0 and k >= 1:
                        trade(k - 1)
                to_sibling(k).start()

        @pl.when(i == N_GW)
        def _():
            wt_copy.wait()

        @pl.when(i >= N_GW)
        def _():
            xv = x_ref[...]
            r = lax.rsqrt(jnp.mean(xv * xv, axis=-1, keepdims=True) + EPS)
            xh = xv * r
            n = xh * nw_ref[...]
            dproj = jnp.concatenate([ref[...] for ref in d_ref], axis=1)
            dh = jnp.dot(dproj, wt_ref[...], preferred_element_type=F32)
            red_ref[0:1, :] += jnp.sum(dh, axis=0, keepdims=True)
            red_ref[1:2, :] += jnp.sum(dh * n, axis=0, keepdims=True)
            dn = dh * s1_ref[...]
            red_ref[2:3, :] += jnp.sum(dn * xh, axis=0, keepdims=True)
            dxh = dn * nw_ref[...]
            gx_ref[...] = dout_ref[...] + r * (dxh - xh * jnp.mean(dxh * xh, axis=-1, keepdims=True))

        @pl.when(i == nstep - 1)
        def _():
            sall_ref[dev] = row0_ref[...]
            sall_ref[dev, 2:5, :] = red_ref[0:3, :]
            sends = [rc(8 + k, sall_ref.at[dev], sall_ref.at[dev], peer) for k, peer in enumerate(peers)]
            for cp in sends:
                cp.start()
            sends += [to_sibling(k) for k in range(N_GW)] + [to_chip(k) for k in range(3)]
            sends += [small_to_sibling(), small_share()] + [small_to_chip(k) for k in range(3)]
            own = N_GW - 1
            to_sibling(own).wait_recv()
            for k in range(3):
                to_chip(k).wait_recv()

            def total(n, carry):
                rr = pl.ds(pl.multiple_of(n * RS_CH, RS_CH), RS_CH)
                acc = gt_ref[own, c, rr, :].astype(F32) + sib_ref[own, rr, :].astype(F32)
                for k in range(3):
                    acc = acc + in_ref[k, rr, :].astype(F32)
                res_ref[c, rr, :] = acc
                return carry

            lax.fori_loop(0, IN_HALF // RS_CH, total, 0)
            share = rc(7, res_ref.at[c], res_ref.at[c], sib)
            share.start()
            sends.append(share)
            back = [pltpu.make_async_copy(res_ref.at[half], gw_hbm.at[half], lsem.at[1 + half]) for half in range(2)]
            for half in range(2):
                @pl.when(c == half)
                def _():
                    back[half].start()
            for k, (px, py, pc) in enumerate(peers):
                pdev = 4 * px + 2 * py + pc
                rc(8 + k, sall_ref.at[pdev], sall_ref.at[pdev], (px, py, pc)).wait_recv()
            rows_ref[...] = sall_ref[...]
            rc(19, sres_ref.at[1 - c], sres_ref.at[1 - c], sib).wait_recv()
            ssum_ref[0:hs, :] = sres_ref[0]
            ssum_ref[hs:rows0, :] = sres_ref[1]
            rc(7, res_ref.at[1 - c], res_ref.at[1 - c], sib).wait_recv()
            for half in range(2):
                @pl.when(c != half)
                def _():
                    back[half].start()
            for cp in sends:
                cp.wait_send()
            for cp in back:
                cp.wait()

    blk = lambda i: jnp.maximum(i - N_GW, 0)
    row = lambda w: pl.BlockSpec((tm, w), lambda i: (blk(i), 0))
    vec = pl.BlockSpec((1, D_MODEL), lambda i: (0, 0))
    const = lambda shape: pl.BlockSpec(shape, lambda i: (0,) * len(shape))
    hbm = pl.BlockSpec(memory_space=pl.ANY)
    return pl.pallas_call(
        body,
        name="in_proj_bwd",
        grid=(nstep,),
        in_specs=[hbm] * npart + [row(w) for w in DPROJ_WIDTHS] + [row(D_MODEL), row(D_MODEL), vec, vec,
                  pl.BlockSpec((t, D_MODEL), lambda i: (0, 0), pipeline_mode=pl.Buffered(1)), hbm, const((16, D_MODEL)),
                  const((8, CONV_W)), const((8, 128)), const((8, D_MODEL))],
        out_specs=[row(D_MODEL), hbm, const((rows0, D_MODEL)), const((N_DEV, 8, D_MODEL))],
        out_shape=[jax.ShapeDtypeStruct((t, D_MODEL), F32), jax.ShapeDtypeStruct((2, IN_HALF, D_MODEL), F32),
                   jax.ShapeDtypeStruct((rows0, D_MODEL), F32), jax.ShapeDtypeStruct((N_DEV, 8, D_MODEL), F32)],
        scratch_shapes=[pltpu.VMEM((2, t, WIN_W), BF16), pltpu.VMEM((IN_W, D_MODEL), BF16),
                        pltpu.VMEM((N_CHIPS, 2, IN_HALF, D_MODEL), BF16), pltpu.VMEM((N_CHIPS, IN_HALF, D_MODEL), BF16),
                        pltpu.VMEM((3, IN_HALF, D_MODEL), BF16), pltpu.VMEM((3, IN_HALF, D_MODEL), BF16),
                        pltpu.VMEM((2, IN_HALF, D_MODEL), F32), pltpu.VMEM((N_DEV, 8, D_MODEL), F32),
                        pltpu.VMEM((8, D_MODEL), F32), pltpu.VMEM((rows0, D_MODEL), F32), pltpu.VMEM((hs, D_MODEL), F32),
                        pltpu.VMEM((N_CHIPS, hs, D_MODEL), F32),
                        pltpu.VMEM((2, hs, D_MODEL), F32), pltpu.SemaphoreType.DMA((2, 3)), pltpu.SemaphoreType.DMA((3,)),
                        pltpu.SemaphoreType.DMA((n_sem,)), pltpu.SemaphoreType.DMA((n_sem,))],
        compiler_params=_cparams(dimension_semantics=("arbitrary",)),
    )(*dparts, *dparts, x, dout, s1, nw, h, wt_full, dcw, dvec, sm_a, row0)


MESH = pl.DeviceIdType.MESH


def _place():
    x, y, c = lax.axis_index("x"), lax.axis_index("y"), lax.axis_index("c")
    chips = [(1 - x, y), (x, 1 - y), (1 - x, 1 - y)]
    return x, y, c, chips


def _remote(sems_s, sems_r, k, src, dst, to):
    return pltpu.make_async_remote_copy(src_ref=src, dst_ref=dst, send_sem=sems_s.at[k], recv_sem=sems_r.at[k],
                                        device_id=to, device_id_type=MESH)


RS_CH = 32
RS_SEMS = 5


def _rs_to_sibling(rc, s0, theirs, sib_ref, sib):
    cp = rc(s0, theirs, sib_ref, sib)
    cp.start()
    return cp


def _rs_trade(rc, s0, theirs, mine, sib_ref, out_ref, in_ref, rows, c, sib, chips):
    rc(s0, theirs, sib_ref, sib).wait_recv()
    cps = []
    for k, (cx, cy) in enumerate(chips):
        jk = 2 * cx + cy

        def add(i, carry, jk=jk, k=k):
            rr = pl.ds(pl.multiple_of(i * RS_CH, RS_CH), RS_CH)
            out_ref[k, rr, :] = (mine[jk, rr, :].astype(F32) + sib_ref[jk, rr, :].astype(F32)).astype(BF16)
            return carry

        lax.fori_loop(0, rows // RS_CH, add, 0)
        cps.append(rc(s0 + 1 + k, out_ref.at[k], in_ref.at[k], (cx, cy, c)))
        cps[-1].start()
    return cps


def _rs_total(rc, s0, mine, sib_ref, out_ref, in_ref, res_ref, rows, j, c, sib):
    for k in range(3):
        rc(s0 + 1 + k, out_ref.at[k], in_ref.at[k], sib).wait_recv()

    def total(i, carry):
        rr = pl.ds(pl.multiple_of(i * RS_CH, RS_CH), RS_CH)
        acc = mine[j, rr, :].astype(F32) + sib_ref[j, rr, :].astype(F32)
        for k in range(3):
            acc = acc + in_ref[k, rr, :].astype(F32)
        res_ref[c, rr, :] = acc
        return carry

    lax.fori_loop(0, rows // RS_CH, total, 0)
    cp = rc(s0 + 4, res_ref.at[c], res_ref.at[c], sib)
    cp.start()
    return cp


def _rs_done(rc, s0, res_ref, c, sib):
    rc(s0 + 4, res_ref.at[1 - c], res_ref.at[1 - c], sib).wait_recv()


def _rs_scratch(rows):
    return [pltpu.VMEM((N_CHIPS, rows, D_MODEL), BF16), pltpu.VMEM((3, rows, D_MODEL), BF16),
            pltpu.VMEM((3, rows, D_MODEL), BF16)]


MAIN_W = 640
MAIN_DST = (((0, 0, 512), (1, 0, 128)), ((2, 0, 512), (3, 0, 128)), ((3, 128, 384), (4, 0, 256)), ((4, 384, 128), (5, 0, 512)))
PAIR_DST = ((1, 128, 128), (4, 256, 128))


def _in_proj_gather(x, wt, c_row, w_ada, b_ada, nw):
    t = x.shape[0]
    ch = 512
    n_sem = 16

    def body(x_hbm, wt_ref, c_ref, wada_ref, bada_ref, nw_ref,
             q_hbm, kv_hbm, ga_hbm, ua_hbm, ug_hbm, gb_hbm, h_hbm, w4_hbm, call_ref, s1_ref, gate_ref, ada_ref,
             x_ref, h_ref, w4_ref, stg_ref, pstg_ref, part_ref, lsem, osem, wsem, ssem, rsem):
        outs = (q_hbm, kv_hbm, ga_hbm, ua_hbm, ug_hbm, gb_hbm)
        x_, y_, c, chips = _place()
        j = 2 * x_ + y_
        dev = 2 * j + c
        sib = (x_, y_, 1 - c)
        idx = [2 * cx + cy for cx, cy in chips]
        rc = functools.partial(_remote, ssem, rsem)
        x_copy = pltpu.make_async_copy(x_hbm, x_ref, lsem.at[0])
        x_copy.start()

        def rows_of(s, cc):
            return pl.ds(pl.multiple_of(2 * IN_HALF * s + IN_HALF * cc, 16), IN_HALF)

        w4_ref[rows_of(j, 0), :] = wt_ref[0].astype(BF16)
        w4_ref[rows_of(j, 1), :] = wt_ref[1].astype(BF16)
        call_ref[dev] = c_ref[...]
        sends = []
        peers = [(px, py, pc) for px in (x_, 1 - x_) for py in (y_, 1 - y_) for pc in (c, 1 - c)][1:]
        for k, peer in enumerate(peers):
            sends.append(rc(k, call_ref.at[dev], call_ref.at[dev], peer))
        for cp in sends:
            cp.start()

        for k, (px, py, pc) in enumerate(peers):
            pdev = 4 * px + 2 * py + pc
            rc(k, call_ref.at[pdev], call_ref.at[pdev], (px, py, pc)).wait_recv()
        rowid = lax.broadcasted_iota(jnp.int32, (N_DEV, D_MODEL), 0)
        call = jnp.zeros((N_DEV, D_MODEL), F32)
        for r in range(N_DEV):
            call = jnp.where(rowid == r, jnp.broadcast_to(call_ref[r], (N_DEV, D_MODEL)), call)
        bsh = bada_ref[:, 0:ADA_SHARD]
        for k in range(1, N_CHIPS):
            bsh = jnp.where(j == k, bada_ref[:, ADA_SHARD * k:ADA_SHARD * (k + 1)], bsh)
        part = jnp.dot(_silu(call).astype(BF16), wada_ref[...].astype(BF16), preferred_element_type=F32) + bsh
        for r in range(N_DEV):
            part_ref[r] = part[r:r + 1, :]
        ada_ref[j] = part_ref[dev]
        for k, chip in enumerate(chips):
            sends.append(rc(13 + k, part_ref.at[2 * idx[k] + c], ada_ref.at[j], (*chip, c)))
            sends[-1].start()
        for k, chip in enumerate(chips):
            sends.append(rc(7 + k, w4_ref.at[rows_of(j, c)], w4_ref.at[rows_of(j, c)], (*chip, c)))
            sends[-1].start()

        x_copy.wait()

        def prenorm(i, carry):
            rr = pl.ds(pl.multiple_of(i * ch, ch), ch)
            xv = x_ref[rr, :]
            r = lax.rsqrt(jnp.mean(xv * xv, axis=-1, keepdims=True) + EPS)
            x_ref[rr, :] = (xv * r) * nw_ref[...]
            return carry

        lax.fori_loop(0, t // ch, prenorm, 0)
        for k in range(3):
            rc(13 + k, ada_ref.at[idx[k]], ada_ref.at[idx[k]], sib).wait_recv()

        shift = jnp.concatenate([ada_ref[0], ada_ref[1][:, 0:256]], axis=1)
        s1 = 1.0 + jnp.concatenate([ada_ref[1][:, 256:768], ada_ref[2][:, 0:512]], axis=1)
        s1_ref[...] = s1
        gate_ref[...] = jnp.concatenate([ada_ref[2][:, 512:768], ada_ref[3]], axis=1)

        def norm(i, carry):
            rr = pl.ds(pl.multiple_of(i * ch, ch), ch)
            h_ref[rr, :] = (x_ref[rr, :] * s1 + shift).astype(BF16)
            return carry

        lax.fori_loop(0, t // ch, norm, 0)
        h_copy = pltpu.make_async_copy(h_ref, h_hbm, lsem.at[1])
        h_copy.start()

        def put_main(case, slot):
            cps, col = [], 0
            for n, (a, c0, w) in enumerate(MAIN_DST[case]):
                cps.append(pltpu.make_async_copy(stg_ref.at[slot, :, pl.ds(col, w)], outs[a].at[:, pl.ds(c0, w)], osem.at[slot, n]))
                col += w
            return cps

        def put_pair(case, slot):
            a, c0, w = PAIR_DST[case]
            return pltpu.make_async_copy(pstg_ref.at[slot], outs[a].at[:, pl.ds(c0, w)], osem.at[slot, 2])

        def project(first_row, width, dst, slot):
            wrows = pl.ds(pl.multiple_of(first_row, 128), width)

            def blk(i, carry):
                rr = pl.ds(pl.multiple_of(i * ch, ch), ch)
                dst[slot, rr, :] = lax.dot_general(h_ref[rr, :], w4_ref[wrows, :], (((1,), (1,)), ((), ())),
                                                   preferred_element_type=F32)
                return carry

            lax.fori_loop(0, t // ch, blk, 0)

        def phase(p, s, pair):
            slot = p % 2
            if p >= 2:
                for case in range(N_CHIPS):
                    @pl.when(order[p - 2] == case)
                    def _():
                        for cp in put_main(case, slot):
                            cp.wait()
            if p == 3:
                for case in range(2):
                    @pl.when(j // 2 == case)
                    def _():
                        put_pair(case, 0).wait()
            project(2 * IN_HALF * s + 64 * (s % 2), MAIN_W, stg_ref, slot)
            for case in range(N_CHIPS):
                @pl.when(s == case)
                def _():
                    for cp in put_main(case, slot):
                        cp.start()
            if pair is not None:
                project(MAIN_W + 2 * (2 * IN_HALF) * pair, 128, pstg_ref, slot % 2 if p == 2 else 1)
                for case in range(2):
                    @pl.when(pair == case)
                    def _():
                        put_pair(case, 0 if p == 2 else 1).start()

        order = [j] + idx
        w_out = [pltpu.make_async_copy(w4_ref.at[pl.ds(pl.multiple_of(2 * IN_HALF * s, 32), 2 * IN_HALF)],
                                       w4_hbm.at[pl.ds(pl.multiple_of(2 * IN_HALF * s, 32), 2 * IN_HALF)], wsem.at[p])
                 for p, s in enumerate(order)]
        w_out[0].start()
        phase(0, j, None)
        passed = []
        for k in range(3):
            jk = idx[k]
            rc(7 + k, w4_ref.at[rows_of(jk, c)], w4_ref.at[rows_of(jk, c)], sib).wait_recv()
            passed.append(rc(10 + k, w4_ref.at[rows_of(jk, c)], w4_ref.at[rows_of(jk, c)], sib))
            passed[-1].start()
            rc(10 + k, w4_ref.at[rows_of(jk, 1 - c)], w4_ref.at[rows_of(jk, 1 - c)], sib).wait_recv()
            w_out[1 + k].start()
            if k == 0:
                phase(1, jk, None)
            elif k == 1:
                phase(2, jk, j // 2)
            else:
                phase(3, jk, 1 - j // 2)

        for case in range(N_CHIPS):
            for p in (2, 3):
                @pl.when(order[p] == case)
                def _():
                    for cp in put_main(case, p % 2):
                        cp.wait()
        for case in range(2):
            @pl.when(1 - j // 2 == case)
            def _():
                put_pair(case, 1).wait()
        h_copy.wait()
        for cp in w_out:
            cp.wait()
        for cp in sends + passed:
            cp.wait_send()

    vm = pl.BlockSpec(memory_space=pltpu.VMEM)
    hbm = pl.BlockSpec(memory_space=pl.ANY)
    widths = (512, 256, 512, 512, 512, 512)
    return pl.pallas_call(
        body,
        name="in_proj",
        in_specs=[hbm, vm, vm, vm, vm, vm],
        out_specs=[hbm] * 8 + [vm, vm, vm],
        out_shape=[jax.ShapeDtypeStruct((t, w), F32) for w in widths]
        + [jax.ShapeDtypeStruct((t, D_MODEL), BF16), jax.ShapeDtypeStruct((IN_W, D_MODEL), BF16),
           jax.ShapeDtypeStruct((N_DEV, 1, D_MODEL), F32), jax.ShapeDtypeStruct((1, D_MODEL), F32),
           jax.ShapeDtypeStruct((1, D_MODEL), F32)],
        scratch_shapes=[pltpu.VMEM((N_CHIPS, 1, ADA_SHARD), F32), pltpu.VMEM((t, D_MODEL), F32), pltpu.VMEM((t, D_MODEL), BF16), pltpu.VMEM((IN_W, D_MODEL), BF16),
                        pltpu.VMEM((2, t, MAIN_W), F32), pltpu.VMEM((2, t, 128), F32), pltpu.VMEM((N_DEV, 1, ADA_SHARD), F32),
                        pltpu.SemaphoreType.DMA((2,)), pltpu.SemaphoreType.DMA((2, 3)), pltpu.SemaphoreType.DMA((N_CHIPS,)),
                        pltpu.SemaphoreType.DMA((n_sem,)), pltpu.SemaphoreType.DMA((n_sem,))],
        compiler_params=_cparams(),
    )(x, wt, c_row, w_ada, b_ada, nw)


def _adamw_math(w, g, m, v):
    m2 = ADAM_B1 * m + (1.0 - ADAM_B1) * g
    v2 = ADAM_B2 * v + (1.0 - ADAM_B2) * (g * g)
    m_hat = m2 / (1.0 - ADAM_B1 ** ADAM_STEP)
    v_hat = v2 / (1.0 - ADAM_B2 ** ADAM_STEP)
    delta = -ADAM_LR * (m_hat / (jnp.sqrt(v_hat) + ADAM_EPS) + ADAM_WD * w)
    return delta, m2, v2


def _adamw(name, groups, nstep, through):
    flat = [a for grp in groups for a in grp]
    n = len(flat)

    def body(*refs):
        ins, through_in, outs, through_out = refs[:n], refs[n], refs[n + 1:2 * n + 1], refs[2 * n + 1]
        for k in range(0, n, 4):
            w_ref, g_ref, m_ref, v_ref = ins[k:k + 4]
            g2_ref, d_ref, m2_ref, v2_ref = outs[k:k + 4]
            g = g_ref[...]
            g2_ref[...] = g
            d_ref[...], m2_ref[...], v2_ref[...] = _adamw_math(w_ref[...], g, m_ref[...], v_ref[...])
        through_out[...] = through_in[...]

    specs = [pl.BlockSpec((a.shape[0] // nstep, a.shape[1]), lambda i: (i, 0)) for a in flat + [through]]
    out = pl.pallas_call(
        body,
        name=name,
        grid=(nstep,),
        in_specs=specs,
        out_specs=specs,
        out_shape=[jax.ShapeDtypeStruct(a.shape, a.dtype) for a in flat + [through]],
        compiler_params=_cparams(dimension_semantics=("arbitrary",)),
    )(*flat, through)
    return [out[k:k + 4] for k in range(0, n, 4)], out[n]


def _adamw_ada(w, m, v, call, rows):
    r, cdim = w.shape
    tm = 256

    def body(w_ref, m_ref, v_ref, c_ref, rows_ref, g_ref, d_ref, m2_ref, v2_ref):
        j = 2 * lax.axis_index("x") + lax.axis_index("y")
        d_ada = jnp.concatenate([jnp.concatenate([rows_ref[d, row:row + 1, :] for d in range(N_DEV)], axis=0)
                                 for row in (2, 3, 0)], axis=1)
        dcols = d_ada[:, 0:cdim]
        for k in range(1, N_CHIPS):
            dcols = jnp.where(j == k, d_ada[:, cdim * k:cdim * (k + 1)], dcols)
        cvec = jnp.concatenate([c_ref[d] for d in range(N_DEV)], axis=0)
        g = lax.dot_general(_silu(cvec).astype(BF16), dcols.astype(BF16), (((0,), (0,)), ((), ())),
                            preferred_element_type=F32)
        g_ref[...] = g
        d_ref[...], m2_ref[...], v2_ref[...] = _adamw_math(w_ref[...], g, m_ref[...], v_ref[...])

    blk = pl.BlockSpec((tm, cdim), lambda i: (i, 0))
    return pl.pallas_call(
        body,
        name="adamw_w_ada",
        grid=(r // tm,),
        in_specs=[blk] * 3 + [pl.BlockSpec((N_DEV, 1, tm), lambda i: (0, 0, i)),
                              pl.BlockSpec((N_DEV, 8, D_MODEL), lambda i: (0, 0, 0))],
        out_specs=[blk] * 4,
        out_shape=[jax.ShapeDtypeStruct((r, cdim), F32)] * 4,
        compiler_params=_cparams(dimension_semantics=("arbitrary",)),
    )(w, m, v, call, rows)


def _adamw_small(ws, ms, vs, ssum, rows):
    n = len(ws)

    def body(*refs):
        w_r, m_r, v_r = refs[0:n], refs[n:2 * n], refs[2 * n:3 * n]
        ss_ref, rows_ref = refs[3 * n], refs[3 * n + 1]
        g_r, d_r, m2_r, v2_r = (refs[3 * n + 2 + k * n:3 * n + 2 + (k + 1) * n] for k in range(4))
        loss_ref = refs[7 * n + 2]
        j = 2 * lax.axis_index("x") + lax.axis_index("y")
        rsum = rows_ref[0]
        for d in range(1, N_DEV):
            rsum = rsum + rows_ref[d]
        taps = []
        for t in range(CONV_TAPS):
            row = ss_ref[t // 2:t // 2 + 1, :]
            c0 = CONV_W * (t % 2)
            pick = row[:, c0:c0 + 128]
            for k in range(1, N_CHIPS):
                pick = jnp.where(j == k, row[:, c0 + 128 * k:c0 + 128 * (k + 1)], pick)
            taps.append(pick)
        grads = [jnp.concatenate([rsum[2:3], rsum[3:4], rsum[0:1]], axis=1), rsum[4:5],
                 ss_ref[17:18, 512:512 + HEAD_DIM], ss_ref[17:18, 640:640 + HEAD_DIM], ss_ref[17:18, 768:776],
                 None, ss_ref[16:17, 0:CONV_W], ss_ref[16:17, CONV_W:2 * CONV_W], ss_ref[17:18, 0:CONV_W]]
        for i in range(n):
            if grads[i] is None:
                g = jnp.concatenate(taps, axis=0)
                w, m, v = (jnp.concatenate([ref[t] for t in range(CONV_TAPS)], axis=0) for ref in (w_r[i], m_r[i], v_r[i]))
                res = (g,) + _adamw_math(w, g, m, v)
                for ref, val in zip((g_r[i], d_r[i], m2_r[i], v2_r[i]), res):
                    for t in range(CONV_TAPS):
                        ref[t] = val[t:t + 1, :]
                continue
            g = grads[i]
            g_r[i][...] = g
            d_r[i][...], m2_r[i][...], v2_r[i][...] = _adamw_math(w_r[i][...], g, m_r[i][...], v_r[i][...])
        loss_ref[...] = (0.5 / D_MODEL) * jnp.sum(ss_ref[18:19, :], axis=1, keepdims=True)

    vm = pl.BlockSpec(memory_space=pltpu.VMEM)
    shapes = [jax.ShapeDtypeStruct(w.shape, F32) for w in ws]
    out = pl.pallas_call(
        body,
        name="adamw_small",
        in_specs=[vm] * (3 * n + 2),
        out_specs=[vm] * (4 * n + 1),
        out_shape=shapes * 4 + [jax.ShapeDtypeStruct((1, 1), F32)],
        compiler_params=_cparams(),
    )(*ws, *ms, *vs, ssum, rows)
    return out[0:n], out[n:2 * n], out[2 * n:3 * n], out[3 * n:4 * n], out[4 * n]


def _rope_tables(t):
    inv = ROPE_THETA ** (-jnp.arange(0, HEAD_DIM, 2, dtype=F32) / HEAD_DIM)
    ang = jnp.arange(t, dtype=F32)[:, None] * inv[None, :]
    cos, sin = jnp.cos(ang), jnp.sin(ang)
    return jnp.tile(cos, (1, 4)), jnp.tile(jnp.concatenate([-sin, sin], axis=1), (1, 2))


def kernel(x, c, w_ada, b_ada, norm_w, w_in, q_norm_w, k_norm_w, sinks, conv_w, conv_b, ln_w, ln_b, w_out, loss_target, m_w_ada, m_b_ada, m_norm_w, m_w_in, m_q_norm_w, m_k_norm_w, m_sinks, m_conv_w, m_conv_b, m_ln_w, m_ln_b, m_w_out, v_w_ada, v_b_ada, v_norm_w, v_w_in, v_q_norm_w, v_k_norm_w, v_sinks, v_conv_w, v_conv_b, v_ln_w, v_ln_b, v_w_out):
    xi, yi = lax.axis_index("x"), lax.axis_index("y")
    j = 2 * xi + yi
    x2, tgt = x[0], loss_target[0]
    t = x2.shape[0]

    wt_s, mt_s, vt_s = w_in[0].T, m_w_in[0].T, v_w_in[0].T
    by_tap = lambda a: jnp.transpose(a, (1, 0, 2))

    q_raw, kv_raw, ga, ua, ug, gb, h, w_full, call, s1, gate = _in_proj_gather(
        x2, wt_s.reshape(2, IN_HALF, D_MODEL), c, w_ada[0], b_ada, norm_w)

    cos_f, sin_s = _rope_tables(t)
    qkw2 = jnp.tile(jnp.concatenate([q_norm_w, k_norm_w], axis=0), (1, 2))

    o, mix_a, wo4, cw4 = _attn_fwd(q_raw, kv_raw, ga, qkw2, sinks, cos_f, sin_s,
                                   w_out[0].reshape(2, OUT_HALF, D_MODEL), by_tap(conv_w))
    w_out_full = wo4.reshape(D_MODEL, D_MODEL)
    cz, mix_b = _conv_fwd(ua, ug, gb, cw4, conv_b, ln_w, ln_b)
    dout, dmix_a, dmix_b, gwo_bf, red_o = _out_proj(mix_a, mix_b, x2, tgt, gate, w_out_full)

    dq, dkv, dga, sm_a, gwo = _attn_bwd(q_raw, kv_raw, ga, o, dmix_a, qkw2, sinks, cos_f, sin_s,
                                        gwo_bf.reshape(N_CHIPS, 2, OUT_HALF, D_MODEL))
    dua, dug, dgb, dcw, dvec = _conv_bwd(ua, ug, gb, cz, dmix_b, cw4, ln_w, ln_b)
    dparts = (dq, dkv, dga, dua, dug, dgb)

    grad_x, gw, ssum, rows = _in_proj_bwd(dparts, h, x2, dout, s1, norm_w, w_full, dcw, dvec, sm_a, red_o)

    gt_w_in = gw.reshape(2 * IN_HALF, D_MODEL)
    g_w_out = gwo.reshape(D_MODEL // N_CHIPS, D_MODEL)

    g_w_ada, d_w_ada, nm_w_ada, nv_w_ada = _adamw_ada(w_ada[0], m_w_ada[0], v_w_ada[0], call, rows)
    (in_res, out_res), grad_x = _adamw("adamw_w", [(wt_s, gt_w_in, mt_s, vt_s), (w_out[0], g_w_out, m_w_out[0], v_w_out[0])],
                                        4, grad_x)
    gt_w_in, dt_w_in, nmt_w_in, nvt_w_in = in_res
    g_w_out, d_w_out, nm_w_out, nv_w_out = out_res
    g_w_in, d_w_in, nm_w_in, nv_w_in = gt_w_in.T, dt_w_in.T, nmt_w_in.T, nvt_w_in.T
    ws = [b_ada, norm_w, q_norm_w, k_norm_w, sinks, by_tap(conv_w), conv_b, ln_w, ln_b]
    ms = [m_b_ada, m_norm_w, m_q_norm_w, m_k_norm_w, m_sinks, by_tap(m_conv_w), m_conv_b, m_ln_w, m_ln_b]
    vs = [v_b_ada, v_norm_w, v_q_norm_w, v_k_norm_w, v_sinks, by_tap(v_conv_w), v_conv_b, v_ln_w, v_ln_b]
    gs, ds, nms, nvs, loss11 = _adamw_small(ws, ms, vs, ssum, rows)
    loss = loss11[0, 0]

    def order(ada_v, in_v, out_v, sm):
        b, nw_, qw_, kw_, sk_, cw_, cb_, lw_, lb_ = sm
        return [ada_v[None], b, nw_, in_v[None], qw_, kw_, sk_, by_tap(cw_), cb_, lw_, lb_, out_v[None]]

    grads = order(g_w_ada, g_w_in, g_w_out, gs)
    deltas = order(d_w_ada, d_w_in, d_w_out, ds)
    new_m = order(nm_w_ada, nm_w_in, nm_w_out, nms)
    new_v = order(nv_w_ada, nv_w_in, nv_w_out, nvs)
    return (loss, grad_x[None], *grads, *deltas, *new_m, *new_v)
```

```python
import functools

import jax
import jax.numpy as jnp
from jax import lax
from jax.experimental import pallas as pl
from jax.experimental.pallas import tpu as pltpu

F32 = jnp.float32
BF16 = jnp.bfloat16

D_MODEL = 1024
ATTN_W = 512
KV_W = 128
CONV_W = 512
IN_W = 2816
HEAD_DIM = 64
CONV_TAPS = 31
QBLK = 128
EPS = 1e-6
ROPE_THETA = 10000.0

ADAM_LR = 0.001
ADAM_B1 = 0.9
ADAM_B2 = 0.999
ADAM_EPS = 1e-08
ADAM_WD = 0.01
ADAM_STEP = 10

N_CHIPS = 4
N_DEV = 8
IN_HALF = IN_W // N_CHIPS // 2
OUT_HALF = D_MODEL // N_CHIPS // 2
ADA_SHARD = 3 * D_MODEL // N_CHIPS

VMEM_LIMIT = 56 * 1024 * 1024
CONV_PAD = 32


def _cparams(**kw):
    return pltpu.CompilerParams(vmem_limit_bytes=VMEM_LIMIT, **kw)


def _sigmoid(v):
    return 1.0 / (1.0 + jnp.exp(-v))


def _silu(v):
    return v * _sigmoid(v)


def _dsilu(v):
    s = _sigmoid(v)
    return s * (1.0 + v * (1.0 - s))


def _lane(shape):
    return lax.broadcasted_iota(jnp.int32, shape, len(shape) - 1)


PUT_ROWS = 512


def _fetch(hbm_refs, vmem_refs, sem):
    cps = [pltpu.make_async_copy(h, v, sem.at[i]) for i, (h, v) in enumerate(zip(hbm_refs, vmem_refs))]
    for cp in cps:
        cp.start()
    return cps


def _put(vmem_ref, hbm_ref, sem, m):
    r = pl.ds(pl.multiple_of(m * PUT_ROWS, PUT_ROWS), PUT_ROWS)
    return pltpu.make_async_copy(vmem_ref.at[r], hbm_ref.at[r], sem.at[m])


def _put_all(pairs, sems, m):
    for (v, h), sem in zip(pairs, sems):
        _put(v, h, sem, m).start()


def _put_wait(pairs, sems, n):
    for (v, h), sem in zip(pairs, sems):
        for m in range(n):
            _put(v, h, sem, m).wait()


def _head_mean(s, left):
    sl = jnp.sum(jnp.where(left, s, 0.0), axis=-1, keepdims=True)
    sr = jnp.sum(jnp.where(left, 0.0, s), axis=-1, keepdims=True)
    return jnp.where(left, sl, sr) * (1.0 / HEAD_DIM)


def _rot(v, first):
    return jnp.where(first, pltpu.roll(v, 96, 1), pltpu.roll(v, 32, 1))


def _norm_rope(v, w, cos, sin_s, left, first):
    r = lax.rsqrt(_head_mean(v * v, left) + EPS)
    xh = v * r
    n = xh * w
    return n * cos + _rot(n, first) * sin_s, xh, r


def _norm_rope_bwd(d, xh, r, w, cos, sin_s, left, first):
    dn = d * cos - _rot(d, first) * sin_s
    dw = jnp.sum(dn * xh, axis=0, keepdims=True)
    dxh = dn * w
    return r * (dxh - xh * _head_mean(dxh * xh, left)), dw


def _dup_heads(v, left):
    sw = pltpu.roll(v, 64, 1)
    return jnp.where(left, v, sw), jnp.where(left, sw, v)


def _prep_kv(kv_ref, kw_ref, cos_ref, sin_ref, ka_ref, va_ref, t):
    ch = 256
    for g in range(2):
        ka_ref[g, 0:QBLK, :] = jnp.zeros((QBLK, 128), BF16)
        va_ref[g, 0:QBLK, :] = jnp.zeros((QBLK, 128), BF16)

    def chunk(i, carry):
        r0 = pl.multiple_of(i * ch, ch)
        left = _lane((ch, 128)) < 64
        first = (_lane((ch, 128)) % 64) < 32
        k = kv_ref[pl.ds(r0, ch), 0:128]
        v = kv_ref[pl.ds(r0, ch), 128:256]
        kr, _, _ = _norm_rope(k, kw_ref[...], cos_ref[pl.ds(r0, ch), :], sin_ref[pl.ds(r0, ch), :], left, first)
        k0, k1 = _dup_heads(kr, left)
        v0, v1 = _dup_heads(v, left)
        ka_ref[0, pl.ds(QBLK + r0, ch), :] = k0.astype(BF16)
        ka_ref[1, pl.ds(QBLK + r0, ch), :] = k1.astype(BF16)
        va_ref[0, pl.ds(QBLK + r0, ch), :] = v0.astype(BF16)
        va_ref[1, pl.ds(QBLK + r0, ch), :] = v1.astype(BF16)
        return carry

    lax.fori_loop(0, t // ch, chunk, 0)


def _band_mask(n):
    qi = lax.broadcasted_iota(jnp.int32, (2 * QBLK, 2 * QBLK), 0) % QBLK
    kj = lax.broadcasted_iota(jnp.int32, (2 * QBLK, 2 * QBLK), 1)
    local = (kj > qi) & (kj <= qi + QBLK)
    return local & ((n > 0) | (kj >= QBLK))


def _softmax_pair(s, mask, sink0, sink1):
    row = lax.broadcasted_iota(jnp.int32, (2 * QBLK, 1), 0)
    sink = jnp.where(row < QBLK, sink0, sink1)
    s = jnp.where(mask, s, -jnp.inf)
    m = jnp.maximum(jnp.max(s, axis=-1, keepdims=True), sink)
    e = jnp.exp(s - m)
    es = jnp.exp(sink - m)
    inv = 1.0 / (jnp.sum(e, axis=-1, keepdims=True) + es)
    return e * inv, es * inv


def _stack_heads(v, left):
    return jnp.concatenate([jnp.where(left, v, 0.0), jnp.where(left, 0.0, v)], axis=0)


def _attn_fwd(q_raw, kv_raw, ga, qkw2, sinks, cos_f, sin_s, wo, cw):
    t = q_raw.shape[0]
    nblk = t // QBLK
    per_put = PUT_ROWS // QBLK

    def body(q_hbm, kv_ref, ga_hbm, qkw_ref, sk_ref, cos_hbm, sin_hbm, wo_ref, cw_ref,
             o_hbm, mix_hbm, wo4_ref, cw4_ref, ka_ref, va_ref, q_ref, ga_ref, o_ref, mix_ref, cos_ref, sin_ref,
             isem, osem0, osem1, ssem, rsem):
        qw_ref, kw_ref = qkw_ref.at[0:1], qkw_ref.at[1:2]
        loads = _fetch((cos_hbm, sin_hbm, q_hbm, ga_hbm), (cos_ref, sin_ref, q_ref, ga_ref), isem)
        outs, osems = ((o_ref, o_hbm), (mix_ref, mix_hbm)), (osem0, osem1)
        x, y, c, chips = _place()
        j = 2 * x + y
        sib = (x, y, 1 - c)
        idx = [2 * cx + cy for cx, cy in chips]
        rc = functools.partial(_remote, ssem, rsem)
        wo4_ref[j] = wo_ref[...].astype(BF16)
        for tap in range(CONV_TAPS):
            cw4_ref[j, tap:tap + 1, :] = cw_ref[tap]
        cw4_ref[j, CONV_TAPS:, :] = jnp.zeros((CONV_PAD - CONV_TAPS, 128), F32)
        sends = []
        for k, chip in enumerate(chips):
            sends.append(rc(k, wo4_ref.at[j, c], wo4_ref.at[j, c], (*chip, c)))
            sends.append(rc(6 + k, cw4_ref.at[j], cw4_ref.at[j], (*chip, c)))
        for cp in sends:
            cp.start()

        loads[0].wait()
        loads[1].wait()
        _prep_kv(kv_ref, kw_ref, cos_ref, sin_ref, ka_ref, va_ref, t)
        loads[2].wait()
        loads[3].wait()

        def blk(n, carry):
            r0 = pl.multiple_of(n * QBLK, QBLK)
            left = _lane((QBLK, 128)) < 64
            first = (_lane((QBLK, 128)) % 64) < 32
            cos = cos_ref[pl.ds(r0, QBLK), :]
            sin = sin_ref[pl.ds(r0, QBLK), :]
            mask = _band_mask(n)
            scores = []
            for p in range(4):
                lanes = slice(p * 128, (p + 1) * 128)
                qr, _, _ = _norm_rope(q_ref[pl.ds(r0, QBLK), lanes], qw_ref[...], cos, sin, left, first)
                q2 = _stack_heads(qr * 0.125, left).astype(BF16)
                scores.append(lax.dot_general(q2, ka_ref[p // 2, pl.ds(r0, 2 * QBLK), :], (((1,), (1,)), ((), ())),
                                              preferred_element_type=F32))
            probs = [_softmax_pair(scores[p], mask, sk_ref[0, 2 * p], sk_ref[0, 2 * p + 1])[0].astype(BF16)
                     for p in range(4)]
            for p in range(4):
                lanes = slice(p * 128, (p + 1) * 128)
                o2 = jnp.dot(probs[p], va_ref[p // 2, pl.ds(r0, 2 * QBLK), :], preferred_element_type=F32)
                o = jnp.where(left, o2[0:QBLK], o2[QBLK:2 * QBLK])
                o_ref[pl.ds(r0, QBLK), lanes] = o.astype(BF16)
                mix_ref[pl.ds(r0, QBLK), lanes] = (o * _silu(ga_ref[pl.ds(r0, QBLK), lanes])).astype(BF16)

            @pl.when(n % per_put == per_put - 1)
            def _():
                _put_all(outs, osems, n // per_put)

            return carry

        lax.fori_loop(0, nblk, blk, 0)
        _put_wait(outs, osems, t // PUT_ROWS)

        passed = []
        for k, chip in enumerate(chips):
            jk = idx[k]
            rc(k, wo4_ref.at[jk, c], wo4_ref.at[jk, c], sib).wait_recv()
            passed.append(rc(3 + k, wo4_ref.at[jk, c], wo4_ref.at[jk, c], sib))
            passed[-1].start()
        for k, chip in enumerate(chips):
            jk = idx[k]
            rc(3 + k, wo4_ref.at[jk, 1 - c], wo4_ref.at[jk, 1 - c], sib).wait_recv()
            rc(6 + k, cw4_ref.at[jk], cw4_ref.at[jk], sib).wait_recv()
        for cp in sends + passed:
            cp.wait_send()

    vm = pl.BlockSpec(memory_space=pltpu.VMEM)
    hbm = pl.BlockSpec(memory_space=pl.ANY)
    n_sem = 9
    return pl.pallas_call(
        body,
        name="attn_fwd",
        in_specs=[hbm, vm, hbm, vm, pl.BlockSpec(memory_space=pltpu.SMEM), hbm, hbm, vm, vm],
        out_specs=[hbm, hbm, vm, vm],
        out_shape=[jax.ShapeDtypeStruct((t, ATTN_W), BF16), jax.ShapeDtypeStruct((t, ATTN_W), BF16),
                   jax.ShapeDtypeStruct((N_CHIPS, 2, OUT_HALF, D_MODEL), BF16),
                   jax.ShapeDtypeStruct((N_CHIPS, 32, 128), F32)],
        scratch_shapes=[pltpu.VMEM((2, t + QBLK, 128), BF16), pltpu.VMEM((2, t + QBLK, 128), BF16),
                        pltpu.VMEM((t, ATTN_W), F32), pltpu.VMEM((t, ATTN_W), F32),
                        pltpu.VMEM((t, ATTN_W), BF16), pltpu.VMEM((t, ATTN_W), BF16),
                        pltpu.VMEM((t, 128), F32), pltpu.VMEM((t, 128), F32),
                        pltpu.SemaphoreType.DMA((4,)), pltpu.SemaphoreType.DMA((t // PUT_ROWS,)),
                        pltpu.SemaphoreType.DMA((t // PUT_ROWS,)),
                        pltpu.SemaphoreType.DMA((n_sem,)), pltpu.SemaphoreType.DMA((n_sem,))],
        compiler_params=_cparams(),
    )(q_raw, kv_raw, ga, qkw2, sinks, cos_f, sin_s, wo, cw)


def _attn_bwd(q_raw, kv_raw, ga, o, dmix, qkw2, sinks, cos_f, sin_s, go):
    t = q_raw.shape[0]
    nblk = t // QBLK
    per_put = PUT_ROWS // QBLK

    def body(q_hbm, kv_ref, ga_hbm, o_hbm, dm_hbm, qkw_ref, sk_ref, cos_hbm, sin_hbm, go_ref,
             dq_hbm, dkv_ref, dga_hbm, sm_ref, gwo_ref, ka_ref, va_ref, dka_ref, dva_ref,
             sibo_ref, outo_ref, ino_ref, q_ref, ga_ref, o_ref, dm_ref, dq_ref, dga_ref, cos_ref, sin_ref,
             isem, osem0, osem1, ssem, rsem):
        qw_ref, kw_ref = qkw_ref.at[0:1], qkw_ref.at[1:2]
        loads = _fetch((cos_hbm, sin_hbm, q_hbm, ga_hbm, o_hbm, dm_hbm), (cos_ref, sin_ref, q_ref, ga_ref, o_ref, dm_ref), isem)
        outs, osems = ((dq_ref, dq_hbm), (dga_ref, dga_hbm)), (osem0, osem1)
        x, y, c, chips = _place()
        sib = (x, y, 1 - c)
        rc = functools.partial(_remote, ssem, rsem)
        theirs, mine = go_ref.at[:, 1 - c], go_ref.at[:, c]
        sends = [_rs_to_sibling(rc, 0, theirs, sibo_ref, sib)]
        loads[0].wait()
        loads[1].wait()
        _prep_kv(kv_ref, kw_ref, cos_ref, sin_ref, ka_ref, va_ref, t)
        dka_ref[...] = jnp.zeros_like(dka_ref)
        dva_ref[...] = jnp.zeros_like(dva_ref)
        sends += _rs_trade(rc, 0, theirs, mine, sibo_ref, outo_ref, ino_ref, OUT_HALF, c, sib, chips)
        for cp in loads[2:]:
            cp.wait()

        def blk(n, carry):
            dqw, dsk = carry
            r0 = pl.multiple_of(n * QBLK, QBLK)
            left = _lane((QBLK, 128)) < 64
            first = (_lane((QBLK, 128)) % 64) < 32
            cos = cos_ref[pl.ds(r0, QBLK), :]
            sin = sin_ref[pl.ds(r0, QBLK), :]
            mask = _band_mask(n)
            row = lax.broadcasted_iota(jnp.int32, (2 * QBLK, 1), 0)
            rows = pl.ds(r0, QBLK)
            win = pl.ds(r0, 2 * QBLK)
            lane_of = [slice(p * 128, (p + 1) * 128) for p in range(4)]
            for grp in ((0, 1), (2, 3)):
                qn = {p: _norm_rope(q_ref[rows, lane_of[p]], qw_ref[...], cos, sin, left, first) for p in grp}
                q2 = {p: _stack_heads(qn[p][0] * 0.125, left).astype(BF16) for p in grp}
                sc = {p: lax.dot_general(q2[p], ka_ref[p // 2, win, :], (((1,), (1,)), ((), ())),
                                         preferred_element_type=F32) for p in grp}
                do2 = {}
                for p in grp:
                    gav = ga_ref[rows, lane_of[p]]
                    dmv = dm_ref[rows, lane_of[p]].astype(F32)
                    dga_ref[rows, lane_of[p]] = (dmv * o_ref[rows, lane_of[p]].astype(F32) * _dsilu(gav)).astype(BF16)
                    do2[p] = _stack_heads(dmv * _silu(gav), left).astype(BF16)
                dpm = {p: lax.dot_general(do2[p], va_ref[p // 2, win, :], (((1,), (1,)), ((), ())),
                                          preferred_element_type=F32) for p in grp}
                sm = {p: _softmax_pair(sc[p], mask, sk_ref[0, 2 * p], sk_ref[0, 2 * p + 1]) for p in grp}
                dsl = {}
                for p in grp:
                    pm, ps = sm[p]
                    delta = jnp.sum(pm * dpm[p], axis=-1, keepdims=True)
                    dsl[p] = (pm * (dpm[p] - delta)).astype(BF16)
                    pd = ps * delta
                    d0 = jnp.sum(jnp.where(row < QBLK, pd, 0.0), axis=0, keepdims=True)
                    d1 = jnp.sum(jnp.where(row < QBLK, 0.0, pd), axis=0, keepdims=True)
                    l8 = _lane((1, 128))
                    dsk = dsk - jnp.where(l8 == 2 * p, d0, 0.0) - jnp.where(l8 == 2 * p + 1, d1, 0.0)
                for p in grp:
                    g = p // 2
                    dva_ref[g, win, :] += lax.dot_general(sm[p][0].astype(BF16), do2[p], (((0,), (0,)), ((), ())),
                                                          preferred_element_type=F32)
                    dka_ref[g, win, :] += lax.dot_general(dsl[p], q2[p], (((0,), (0,)), ((), ())),
                                                          preferred_element_type=F32)
                for p in grp:
                    dq2 = jnp.dot(dsl[p], ka_ref[p // 2, win, :], preferred_element_type=F32)
                    dqr = jnp.where(left, dq2[0:QBLK], dq2[QBLK:2 * QBLK]) * 0.125
                    dq, dw = _norm_rope_bwd(dqr, qn[p][1], qn[p][2], qw_ref[...], cos, sin, left, first)
                    dq_ref[rows, lane_of[p]] = dq.astype(BF16)
                    dqw = dqw + dw

            @pl.when(n % per_put == per_put - 1)
            def _():
                _put_all(outs, osems, n // per_put)

            return dqw, dsk

        zero = jnp.zeros((1, 128), F32)
        dqw, dsk = lax.fori_loop(0, nblk, blk, (zero, zero))

        ch = 256

        def chunk(i, dkw):
            r0 = pl.multiple_of(i * ch, ch)
            left = _lane((ch, 128)) < 64
            first = (_lane((ch, 128)) % 64) < 32
            rows = pl.ds(r0, ch)
            prow = pl.ds(QBLK + r0, ch)

            def fold(ref):
                a0 = ref[0, prow, :]
                a1 = ref[1, prow, :]
                return jnp.where(left, a0 + pltpu.roll(a0, 64, 1), a1 + pltpu.roll(a1, 64, 1))

            cos = cos_ref[rows, :]
            sin = sin_ref[rows, :]
            _, xh, r = _norm_rope(kv_ref[rows, 0:128], kw_ref[...], cos, sin, left, first)
            dk, dw = _norm_rope_bwd(fold(dka_ref), xh, r, kw_ref[...], cos, sin, left, first)
            dkv_ref[rows, 0:128] = dk.astype(BF16)
            dkv_ref[rows, 128:256] = fold(dva_ref).astype(BF16)
            return dkw + dw

        dkw = lax.fori_loop(0, t // ch, chunk, zero)
        sm_ref[...] = jnp.zeros((8, 128), F32)
        sm_ref[0:1, :] = dqw + pltpu.roll(dqw, 64, 1)
        sm_ref[1:2, :] = dkw + pltpu.roll(dkw, 64, 1)
        sm_ref[2:3, :] = dsk

        j = 2 * x + y
        sends.append(_rs_total(rc, 0, mine, sibo_ref, outo_ref, ino_ref, gwo_ref, OUT_HALF, j, c, sib))
        _rs_done(rc, 0, gwo_ref, c, sib)
        for cp in sends:
            cp.wait_send()
        _put_wait(outs, osems, t // PUT_ROWS)

    vm = pl.BlockSpec(memory_space=pltpu.VMEM)
    hbm = pl.BlockSpec(memory_space=pl.ANY)
    return pl.pallas_call(
        body,
        name="attn_bwd",
        in_specs=[hbm, vm, hbm, hbm, hbm, vm, pl.BlockSpec(memory_space=pltpu.SMEM), hbm, hbm, vm],
        out_specs=[hbm, vm, hbm, vm, vm],
        out_shape=[jax.ShapeDtypeStruct((t, ATTN_W), BF16), jax.ShapeDtypeStruct((t, 2 * KV_W), BF16),
                   jax.ShapeDtypeStruct((t, ATTN_W), BF16), jax.ShapeDtypeStruct((8, 128), F32),
                   jax.ShapeDtypeStruct((2, OUT_HALF, D_MODEL), F32)],
        scratch_shapes=[pltpu.VMEM((2, t + QBLK, 128), BF16), pltpu.VMEM((2, t + QBLK, 128), BF16),
                        pltpu.VMEM((2, t + QBLK, 128), F32), pltpu.VMEM((2, t + QBLK, 128), F32)]
        + _rs_scratch(OUT_HALF)
        + [pltpu.VMEM((t, ATTN_W), F32), pltpu.VMEM((t, ATTN_W), F32), pltpu.VMEM((t, ATTN_W), BF16),
           pltpu.VMEM((t, ATTN_W), BF16), pltpu.VMEM((t, ATTN_W), BF16), pltpu.VMEM((t, ATTN_W), BF16),
           pltpu.VMEM((t, 128), F32), pltpu.VMEM((t, 128), F32),
           pltpu.SemaphoreType.DMA((6,)), pltpu.SemaphoreType.DMA((t // PUT_ROWS,)), pltpu.SemaphoreType.DMA((t // PUT_ROWS,)),
           pltpu.SemaphoreType.DMA((RS_SEMS,)), pltpu.SemaphoreType.DMA((RS_SEMS,))],
        compiler_params=_cparams(),
    )(q_raw, kv_raw, ga, o, dmix, qkw2, sinks, cos_f, sin_s, go)


CONV_CH = 256
CONV_SUB = 128
CONV_ACCS = 1


def _shifted_windows(src_ref, r0, sh_ref):
    rows = CONV_CH + CONV_PAD
    win = src_ref[pl.ds(r0, rows), :]
    for b in range(8):
        sh = win if b == 0 else pltpu.roll(win, rows - b, 0)
        for c in range(CONV_W // 128):
            sh_ref[b, c] = sh[:, c * 128:(c + 1) * 128]


def _conv_fwd(ua, ug, gb, cw, cb, lw, lb):
    t = ua.shape[0]

    def body(ua_hbm, ug_hbm, gb_hbm, cw_ref, cb_ref, lw_ref, lb_ref, cz_hbm, mix_hbm, zp_ref, sh_ref,
             ua_ref, ug_ref, gb_ref, cz_ref, mix_ref, isem, osem0, osem1):
        loads = _fetch((ua_hbm, ug_hbm, gb_hbm), (ua_ref, ug_ref, gb_ref), isem)
        outs, osems = ((cz_ref, cz_hbm), (mix_ref, mix_hbm)), (osem0, osem1)
        per_put = PUT_ROWS // CONV_CH
        zp_ref[0:CONV_PAD, :] = jnp.zeros((CONV_PAD, CONV_W), F32)
        loads[0].wait()
        loads[1].wait()

        def glu(i, carry):
            r0 = pl.multiple_of(i * CONV_CH, CONV_CH)
            rows = pl.ds(r0, CONV_CH)
            zp_ref[pl.ds(CONV_PAD + r0, CONV_CH), :] = ua_ref[rows, :] * _sigmoid(ug_ref[rows, :])
            return carry

        lax.fori_loop(0, t // CONV_CH, glu, 0)
        loads[2].wait()

        def chunk(i, carry):
            r0 = pl.multiple_of(i * CONV_CH, CONV_CH)
            _shifted_windows(zp_ref, r0, sh_ref)
            for c in range(CONV_W // 128):
                lanes = slice(c * 128, (c + 1) * 128)

                def sub(k, carry2):
                    b0 = pl.multiple_of(k * CONV_SUB, CONV_SUB)
                    acc = [jnp.broadcast_to(cb_ref[0:1, lanes], (CONV_SUB, 128))] + [None] * (CONV_ACCS - 1)
                    for j in range(CONV_TAPS):
                        off = j + CONV_PAD - (CONV_TAPS - 1)
                        term = sh_ref[off % 8, c, pl.ds(b0 + 8 * (off // 8), CONV_SUB), :] * cw_ref[c, j:j + 1, :]
                        acc[j % CONV_ACCS] = term if acc[j % CONV_ACCS] is None else acc[j % CONV_ACCS] + term
                    cz_ref[pl.ds(r0 + b0, CONV_SUB), lanes] = functools.reduce(lambda a, b: a + b, acc)
                    return carry2

                lax.fori_loop(0, CONV_CH // CONV_SUB, sub, 0)
            rows = pl.ds(r0, CONV_CH)
            cz = cz_ref[rows, :]
            mu = jnp.mean(cz, axis=-1, keepdims=True)
            xc = cz - mu
            rs = lax.rsqrt(jnp.mean(xc * xc, axis=-1, keepdims=True) + EPS)
            ln = xc * rs * lw_ref[...] + lb_ref[...]
            mix_ref[rows, :] = (_silu(ln) * _silu(gb_ref[rows, :])).astype(BF16)

            @pl.when(i % per_put == per_put - 1)
            def _():
                _put_all(outs, osems, i // per_put)

            return carry

        lax.fori_loop(0, t // CONV_CH, chunk, 0)
        _put_wait(outs, osems, t // PUT_ROWS)

    vm = pl.BlockSpec(memory_space=pltpu.VMEM)
    hbm = pl.BlockSpec(memory_space=pl.ANY)
    nput = t // PUT_ROWS
    return pl.pallas_call(
        body,
        name="conv_fwd",
        in_specs=[hbm] * 3 + [vm] * 4,
        out_specs=[hbm, hbm],
        out_shape=[jax.ShapeDtypeStruct((t, CONV_W), F32), jax.ShapeDtypeStruct((t, CONV_W), BF16)],
        scratch_shapes=[pltpu.VMEM((t + CONV_PAD, CONV_W), F32),
                        pltpu.VMEM((8, CONV_W // 128, CONV_CH + CONV_PAD, 128), F32),
                        pltpu.VMEM((t, CONV_W), F32), pltpu.VMEM((t, CONV_W), F32), pltpu.VMEM((t, CONV_W), F32),
                        pltpu.VMEM((t, CONV_W), F32), pltpu.VMEM((t, CONV_W), BF16),
                        pltpu.SemaphoreType.DMA((3,)), pltpu.SemaphoreType.DMA((nput,)), pltpu.SemaphoreType.DMA((nput,))],
        compiler_params=_cparams(),
    )(ua, ug, gb, cw, cb, lw, lb)


def _conv_bwd(ua, ug, gb, cz, dmix, cw, lw, lb):
    t = ua.shape[0]

    def body(ua_hbm, ug_hbm, gb_hbm, cz_hbm, dm_hbm, cw_ref, lw_ref, lb_ref,
             dua_hbm, dug_hbm, dgb_hbm, dcw_ref, dvec_ref, zp_ref, dp_ref, sh_ref, wacc_ref,
             ua_ref, ug_ref, gb_ref, cz_ref, dm_ref, dua_ref, dug_ref, dgb_ref, isem, osem0, osem1, osem2):
        loads = _fetch((ua_hbm, ug_hbm, gb_hbm, cz_hbm, dm_hbm), (ua_ref, ug_ref, gb_ref, cz_ref, dm_ref), isem)
        per_put = PUT_ROWS // CONV_CH
        zp_ref[0:CONV_PAD, :] = jnp.zeros((CONV_PAD, CONV_W), F32)
        dp_ref[t:t + CONV_PAD, :] = jnp.zeros((CONV_PAD, CONV_W), F32)
        wacc_ref[...] = jnp.zeros_like(wacc_ref)
        for cp in loads:
            cp.wait()

        def pointwise(i, carry):
            dcb, dlw, dlb = carry
            r0 = pl.multiple_of(i * CONV_CH, CONV_CH)
            rows = pl.ds(r0, CONV_CH)
            zp_ref[pl.ds(CONV_PAD + r0, CONV_CH), :] = ua_ref[rows, :] * _sigmoid(ug_ref[rows, :])
            cz = cz_ref[rows, :]
            mu = jnp.mean(cz, axis=-1, keepdims=True)
            xc = cz - mu
            rs = lax.rsqrt(jnp.mean(xc * xc, axis=-1, keepdims=True) + EPS)
            xh = xc * rs
            ln = xh * lw_ref[...] + lb_ref[...]
            gbv = gb_ref[rows, :]
            dy = dm_ref[rows, :].astype(F32)
            dgb_ref[rows, :] = (dy * _silu(ln) * _dsilu(gbv)).astype(BF16)
            dl = dy * _silu(gbv) * _dsilu(ln)
            dxh = dl * lw_ref[...]
            dcz = rs * (dxh - jnp.mean(dxh, axis=-1, keepdims=True)
                        - xh * jnp.mean(dxh * xh, axis=-1, keepdims=True))
            dp_ref[rows, :] = dcz

            @pl.when(i % per_put == per_put - 1)
            def _():
                _put(dgb_ref, dgb_hbm, osem2, i // per_put).start()

            return (dcb + jnp.sum(dcz, axis=0, keepdims=True),
                    dlw + jnp.sum(dl * xh, axis=0, keepdims=True),
                    dlb + jnp.sum(dl, axis=0, keepdims=True))

        zero = jnp.zeros((1, CONV_W), F32)
        dcb, dlw, dlb = lax.fori_loop(0, t // CONV_CH, pointwise, (zero, zero, zero))
        dvec_ref[...] = jnp.zeros((8, CONV_W), F32)
        dvec_ref[0:1, :] = dcb
        dvec_ref[1:2, :] = dlw
        dvec_ref[2:3, :] = dlb

        def chunk(i, carry):
            r0 = pl.multiple_of(i * CONV_CH, CONV_CH)
            _shifted_windows(dp_ref, r0, sh_ref)
            for c in range(CONV_W // 128):
                lanes = slice(c * 128, (c + 1) * 128)

                def sub(k, carry2):
                    b0 = pl.multiple_of(k * CONV_SUB, CONV_SUB)
                    acc = [None] * CONV_ACCS
                    for j in range(CONV_TAPS):
                        off = CONV_TAPS - 1 - j
                        term = sh_ref[off % 8, c, pl.ds(b0 + 8 * (off // 8), CONV_SUB), :] * cw_ref[c, j:j + 1, :]
                        acc[j % CONV_ACCS] = term if acc[j % CONV_ACCS] is None else acc[j % CONV_ACCS] + term
                    acc = functools.reduce(lambda a, b: a + b, acc)
                    rr = pl.ds(r0 + b0, CONV_SUB)
                    sg = _sigmoid(ug_ref[rr, lanes])
                    dua_ref[rr, lanes] = (acc * sg).astype(BF16)
                    dug_ref[rr, lanes] = (acc * ua_ref[rr, lanes] * sg * (1.0 - sg)).astype(BF16)
                    return carry2

                lax.fori_loop(0, CONV_CH // CONV_SUB, sub, 0)
            _shifted_windows(zp_ref, r0, sh_ref)
            for c in range(CONV_W // 128):
                lanes = slice(c * 128, (c + 1) * 128)

                def subw(k, carry2):
                    b0 = pl.multiple_of(k * CONV_SUB, CONV_SUB)
                    dcz = dp_ref[pl.ds(r0 + b0, CONV_SUB), lanes]
                    for j in range(CONV_TAPS):
                        off = j + CONV_PAD - (CONV_TAPS - 1)
                        pr = dcz * sh_ref[off % 8, c, pl.ds(b0 + 8 * (off // 8), CONV_SUB), :]
                        parts = [pr[8 * q:8 * (q + 1)] for q in range(CONV_SUB // 8)]
                        while len(parts) > 1:
                            parts = [a + b for a, b in zip(parts[0::2], parts[1::2])]
                        wacc_ref[8 * j:8 * (j + 1), lanes] += parts[0]
                    return carry2

                lax.fori_loop(0, CONV_CH // CONV_SUB, subw, 0)

            @pl.when(i % per_put == per_put - 1)
            def _():
                _put_all(((dua_ref, dua_hbm), (dug_ref, dug_hbm)), (osem0, osem1), i // per_put)

            return carry

        lax.fori_loop(0, t // CONV_CH, chunk, 0)
        _put_wait(((dua_ref, dua_hbm), (dug_ref, dug_hbm), (dgb_ref, dgb_hbm)), (osem0, osem1, osem2), t // PUT_ROWS)
        dcw_ref[...] = jnp.zeros((16, 2 * CONV_W), F32)
        for j in range(CONV_TAPS):
            dcw_ref[j // 2:j // 2 + 1, CONV_W * (j % 2):CONV_W * (j % 2 + 1)] = jnp.sum(
                wacc_ref[8 * j:8 * (j + 1), :], axis=0, keepdims=True)

    vm = pl.BlockSpec(memory_space=pltpu.VMEM)
    hbm = pl.BlockSpec(memory_space=pl.ANY)
    return pl.pallas_call(
        body,
        name="conv_bwd",
        in_specs=[hbm] * 5 + [vm] * 3,
        out_specs=[hbm] * 3 + [vm] * 2,
        out_shape=[jax.ShapeDtypeStruct((t, CONV_W), BF16)] * 3
        + [jax.ShapeDtypeStruct((16, 2 * CONV_W), F32), jax.ShapeDtypeStruct((8, CONV_W), F32)],
        scratch_shapes=[pltpu.VMEM((t + CONV_PAD, CONV_W), F32), pltpu.VMEM((t + CONV_PAD, CONV_W), F32),
                        pltpu.VMEM((8, CONV_W // 128, CONV_CH + CONV_PAD, 128), F32), pltpu.VMEM((8 * 32, CONV_W), F32)]
        + [pltpu.VMEM((t, CONV_W), F32)] * 4 + [pltpu.VMEM((t, CONV_W), BF16)] * 4
        + [pltpu.SemaphoreType.DMA((5,))] + [pltpu.SemaphoreType.DMA((t // PUT_ROWS,))] * 3,
        compiler_params=_cparams(),
    )(ua, ug, gb, cz, dmix, cw, lw, lb)


def _out_proj(mix_a, mix_b, x, tgt, gate, w_out):
    t = x.shape[0]
    tm = 512
    nstep = t // tm

    def body(ma_ref, mb_ref, x_ref, t_ref, g_ref, w_ref, dout_ref, dma_ref, dmb_ref, gw_ref, red_ref, acc_ref):
        i = pl.program_id(0)

        @pl.when(i == 0)
        def _():
            acc_ref[...] = jnp.zeros_like(acc_ref)
            red_ref[...] = jnp.zeros_like(red_ref)

        mix = jnp.concatenate([ma_ref[...], mb_ref[...]], axis=1)
        y = jnp.dot(mix, w_ref[...], preferred_element_type=F32)
        gate_v = g_ref[...]
        err = x_ref[...] + gate_v * y - t_ref[...]
        dout = err * (1.0 / D_MODEL)
        dout_ref[...] = dout
        red_ref[0:1, :] += jnp.sum(dout * y, axis=0, keepdims=True)
        red_ref[1:2, :] += jnp.sum(err * err, axis=0, keepdims=True)
        dy = (dout * gate_v).astype(BF16)
        dmix = lax.dot_general(dy, w_ref[...], (((1,), (1,)), ((), ())), preferred_element_type=F32)
        dma_ref[...] = dmix[:, 0:512].astype(BF16)
        dmb_ref[...] = dmix[:, 512:1024].astype(BF16)
        acc_ref[...] += lax.dot_general(mix, dy, (((0,), (0,)), ((), ())), preferred_element_type=F32)

        @pl.when(i == nstep - 1)
        def _():
            gw_ref[...] = acc_ref[...].astype(BF16)

    row = lambda w: pl.BlockSpec((tm, w), lambda i: (i, 0))
    const = lambda s: pl.BlockSpec(s, lambda i: (0, 0))
    return pl.pallas_call(
        body,
        name="out_proj",
        grid=(nstep,),
        in_specs=[row(512), row(512), row(D_MODEL), row(D_MODEL), const((1, D_MODEL)),
                  pl.BlockSpec((D_MODEL, D_MODEL), lambda i: (0, 0), pipeline_mode=pl.Buffered(1))],
        out_specs=[row(D_MODEL), row(512), row(512), const((D_MODEL, D_MODEL)), const((8, D_MODEL))],
        out_shape=[jax.ShapeDtypeStruct((t, D_MODEL), F32), jax.ShapeDtypeStruct((t, 512), BF16),
                   jax.ShapeDtypeStruct((t, 512), BF16), jax.ShapeDtypeStruct((D_MODEL, D_MODEL), BF16),
                   jax.ShapeDtypeStruct((8, D_MODEL), F32)],
        scratch_shapes=[pltpu.VMEM((D_MODEL, D_MODEL), F32)],
        compiler_params=_cparams(dimension_semantics=("arbitrary",)),
    )(mix_a, mix_b, x, tgt, gate, w_out)


DPROJ_WIDTHS = (512, 256, 512, 512, 512, 512)
DPROJ_STARTS = (0, 512, 768, 1280, 1792, 2304)
WIN_W = 768
WIN_START = (0, 640, 1408, 2048)
WIN_OFF = (0, 64, 0, 64)
N_GW = N_CHIPS


def _window_pieces(s):
    lo, hi = WIN_START[s], WIN_START[s] + WIN_W
    out = []
    for p, (st, w) in enumerate(zip(DPROJ_STARTS, DPROJ_WIDTHS)):
        a, b = max(lo, st), min(hi, st + w)
        if a < b:
            out.append((p, a - st, b - a, a - lo))
    return out


def _in_proj_bwd(dparts, h, x, dout, s1, nw, wt_full, dcw, dvec, sm_a, row0):
    t = x.shape[0]
    tm = 256
    nstep = N_GW + t // tm
    n_sem = 20
    rows0 = 32
    hs = rows0 // 2
    npart = len(DPROJ_WIDTHS)

    def body(*refs):
        d_hbm, d_ref = refs[:npart], refs[npart:2 * npart]
        (x_ref, dout_ref, s1_ref, nw_ref, h_ref, wt_hbm, dcw_ref, dvec_ref, sma_ref, row0_ref,
         gx_ref, gw_hbm, ssum_ref, rows_ref,
         stg_ref, wt_ref, gt_ref, sib_ref, out_ref, in_ref, res_ref, sall_ref, red_ref, sm0_ref, ssib_ref, schip_ref, sres_ref,
         wsem, lsem, ssem, rsem) = refs[2 * npart:]
        i = pl.program_id(0)
        x_, y_, c, chips = _place()
        j = 2 * x_ + y_
        dev = 2 * j + c
        sib = (x_, y_, 1 - c)
        rc = functools.partial(_remote, ssem, rsem)
        rel_chip = [2 * cx + cy for cx, cy in chips] + [j]
        peers = [(px, py, pc) for px in (x_, 1 - x_) for py in (y_, 1 - y_) for pc in (c, 1 - c)][1:]
        wt_copy = pltpu.make_async_copy(wt_hbm, wt_ref, lsem.at[0])

        def window(case, slot):
            return [pltpu.make_async_copy(d_hbm[p].at[:, pl.ds(c0, w)], stg_ref.at[slot, :, pl.ds(w0, w)], wsem.at[slot, n])
                    for n, (p, c0, w, w0) in enumerate(_window_pieces(case))]

        def to_sibling(k):
            return rc(k, gt_ref.at[k, 1 - c], sib_ref.at[k], sib)

        def to_chip(k):
            return rc(4 + k, out_ref.at[k], in_ref.at[k], (*chips[k], c))

        def trade(k):
            to_sibling(k).wait_recv()

            def add(n, carry):
                rr = pl.ds(pl.multiple_of(n * RS_CH, RS_CH), RS_CH)
                out_ref[k, rr, :] = (gt_ref[k, c, rr, :].astype(F32) + sib_ref[k, rr, :].astype(F32)).astype(BF16)
                return carry

            lax.fori_loop(0, IN_HALF // RS_CH, add, 0)
            to_chip(k).start()

        def keep(k, first, vals):
            for half in range(2):
                lo, hi = max(first, IN_HALF * half), min(first + vals.shape[0], IN_HALF * (half + 1))
                if lo < hi:
                    gt_ref[k, half, lo - IN_HALF * half:hi - IN_HALF * half, :] = vals[lo - first:hi - first].astype(BF16)

        mine_s = pl.ds(pl.multiple_of(c * hs, 8), hs)
        other_s = pl.ds(pl.multiple_of((1 - c) * hs, 8), hs)

        def small_to_sibling():
            return rc(15, sm0_ref.at[other_s], ssib_ref, sib)

        def small_to_chip(k):
            return rc(16 + k, schip_ref.at[j], schip_ref.at[j], (*chips[k], c))

        def small_share():
            return rc(19, sres_ref.at[c], sres_ref.at[c], sib)

        for k in range(N_GW):
            @pl.when(i == k)
            def _(k=k):
                slot = k % 2
                if k == 0:
                    red_ref[...] = jnp.zeros_like(red_ref)
                    wt_copy.start()
                    sm0_ref[...] = jnp.zeros_like(sm0_ref)
                    sm0_ref[0:16, :] = dcw_ref[...]
                    sm0_ref[16:17, 0:CONV_W] = dvec_ref[0:1, :]
                    sm0_ref[16:17, CONV_W:2 * CONV_W] = dvec_ref[1:2, :]
                    sm0_ref[17:18, 0:CONV_W] = dvec_ref[2:3, :]
                    for r in range(3):
                        sm0_ref[17:18, CONV_W + 128 * r:CONV_W + 128 * (r + 1)] = sma_ref[r:r + 1, :]
                    sm0_ref[18:19, :] = row0_ref[1:2, :]
                    small_to_sibling().start()
                if k == 1:
                    small_to_sibling().wait_recv()
                    schip_ref[j] = sm0_ref[mine_s, :] + ssib_ref[...]
                    for kk in range(3):
                        small_to_chip(kk).start()
                if k == N_GW - 1:
                    for kk in range(3):
                        jk = rel_chip[kk]
                        rc(16 + kk, schip_ref.at[jk], schip_ref.at[jk], sib).wait_recv()
                    tot = schip_ref[0]
                    for d in range(1, N_CHIPS):
                        tot = tot + schip_ref[d]
                    sres_ref[c] = tot
                    small_share().start()
                for case in range(N_CHIPS):
                    if k == 0:
                        @pl.when(rel_chip[0] == case)
                        def _():
                            for cp in window(case, 0):
                                cp.start()
                    if k + 1 < N_GW:
                        @pl.when(rel_chip[k + 1] == case)
                        def _():
                            for cp in window(case, 1 - slot):
                                cp.start()
                for case in range(N_CHIPS):
                    @pl.when(rel_chip[k] == case)
                    def _():
                        for cp in window(case, slot):
                            cp.wait()
                for part in range(2):
                    cols = pl.ds(part * (WIN_W // 2), WIN_W // 2)
                    g = lax.dot_general(stg_ref[slot, :, cols], h_ref[...], (((0,), (0,)), ((), ())),
                                        preferred_element_type=F32)
                    for off in sorted(set(WIN_OFF)):
                        @pl.when(rel_chip[k] % 2 == (1 if off else 0))
                        def _():
                            keep(k, part * (WIN_W // 2) - off, g)
                    if part == 0 and k >= 1:
                        trade(k - 1)
                to_sibling(k).start()

        @pl.when(i == N_GW)
        def _():
            wt_copy.wait()

        @pl.when(i >= N_GW)
        def _():
            xv = x_ref[...]
            r = lax.rsqrt(jnp.mean(xv * xv, axis=-1, keepdims=True) + EPS)
            xh = xv * r
            n = xh * nw_ref[...]
            dproj = jnp.concatenate([ref[...] for ref in d_ref], axis=1)
            dh = jnp.dot(dproj, wt_ref[...], preferred_element_type=F32)
            red_ref[0:1, :] += jnp.sum(dh, axis=0, keepdims=True)
            red_ref[1:2, :] += jnp.sum(dh * n, axis=0, keepdims=True)
            dn = dh * s1_ref[...]
            red_ref[2:3, :] += jnp.sum(dn * xh, axis=0, keepdims=True)
            dxh = dn * nw_ref[...]
            gx_ref[...] = dout_ref[...] + r * (dxh - xh * jnp.mean(dxh * xh, axis=-1, keepdims=True))

        @pl.when(i == nstep - 1)
        def _():
            sall_ref[dev] = row0_ref[...]
            sall_ref[dev, 2:5, :] = red_ref[0:3, :]
            sends = [rc(8 + k, sall_ref.at[dev], sall_ref.at[dev], peer) for k, peer in enumerate(peers)]
            for cp in sends:
                cp.start()
            sends += [to_sibling(k) for k in range(N_GW)] + [to_chip(k) for k in range(3)]
            sends += [small_to_sibling(), small_share()] + [small_to_chip(k) for k in range(3)]
            own = N_GW - 1
            to_sibling(own).wait_recv()
            for k in range(3):
                to_chip(k).wait_recv()

            def total(n, carry):
                rr = pl.ds(pl.multiple_of(n * RS_CH, RS_CH), RS_CH)
                acc = gt_ref[own, c, rr, :].astype(F32) + sib_ref[own, rr, :].astype(F32)
                for k in range(3):
                    acc = acc + in_ref[k, rr, :].astype(F32)
                res_ref[c, rr, :] = acc
                return carry

            lax.fori_loop(0, IN_HALF // RS_CH, total, 0)
            share = rc(7, res_ref.at[c], res_ref.at[c], sib)
            share.start()
            sends.append(share)
            back = [pltpu.make_async_copy(res_ref.at[half], gw_hbm.at[half], lsem.at[1 + half]) for half in range(2)]
            for half in range(2):
                @pl.when(c == half)
                def _():
                    back[half].start()
            for k, (px, py, pc) in enumerate(peers):
                pdev = 4 * px + 2 * py + pc
                rc(8 + k, sall_ref.at[pdev], sall_ref.at[pdev], (px, py, pc)).wait_recv()
            rows_ref[...] = sall_ref[...]
            rc(19, sres_ref.at[1 - c], sres_ref.at[1 - c], sib).wait_recv()
            ssum_ref[0:hs, :] = sres_ref[0]
            ssum_ref[hs:rows0, :] = sres_ref[1]
            rc(7, res_ref.at[1 - c], res_ref.at[1 - c], sib).wait_recv()
            for half in range(2):
                @pl.when(c != half)
                def _():
                    back[half].start()
            for cp in sends:
                cp.wait_send()
            for cp in back:
                cp.wait()

    blk = lambda i: jnp.maximum(i - N_GW, 0)
    row = lambda w: pl.BlockSpec((tm, w), lambda i: (blk(i), 0))
    vec = pl.BlockSpec((1, D_MODEL), lambda i: (0, 0))
    const = lambda shape: pl.BlockSpec(shape, lambda i: (0,) * len(shape))
    hbm = pl.BlockSpec(memory_space=pl.ANY)
    return pl.pallas_call(
        body,
        name="in_proj_bwd",
        grid=(nstep,),
        in_specs=[hbm] * npart + [row(w) for w in DPROJ_WIDTHS] + [row(D_MODEL), row(D_MODEL), vec, vec,
                  pl.BlockSpec((t, D_MODEL), lambda i: (0, 0), pipeline_mode=pl.Buffered(1)), hbm, const((16, D_MODEL)),
                  const((8, CONV_W)), const((8, 128)), const((8, D_MODEL))],
        out_specs=[row(D_MODEL), hbm, const((rows0, D_MODEL)), const((N_DEV, 8, D_MODEL))],
        out_shape=[jax.ShapeDtypeStruct((t, D_MODEL), F32), jax.ShapeDtypeStruct((2, IN_HALF, D_MODEL), F32),
                   jax.ShapeDtypeStruct((rows0, D_MODEL), F32), jax.ShapeDtypeStruct((N_DEV, 8, D_MODEL), F32)],
        scratch_shapes=[pltpu.VMEM((2, t, WIN_W), BF16), pltpu.VMEM((IN_W, D_MODEL), BF16),
                        pltpu.VMEM((N_CHIPS, 2, IN_HALF, D_MODEL), BF16), pltpu.VMEM((N_CHIPS, IN_HALF, D_MODEL), BF16),
                        pltpu.VMEM((3, IN_HALF, D_MODEL), BF16), pltpu.VMEM((3, IN_HALF, D_MODEL), BF16),
                        pltpu.VMEM((2, IN_HALF, D_MODEL), F32), pltpu.VMEM((N_DEV, 8, D_MODEL), F32),
                        pltpu.VMEM((8, D_MODEL), F32), pltpu.VMEM((rows0, D_MODEL), F32), pltpu.VMEM((hs, D_MODEL), F32),
                        pltpu.VMEM((N_CHIPS, hs, D_MODEL), F32),
                        pltpu.VMEM((2, hs, D_MODEL), F32), pltpu.SemaphoreType.DMA((2, 3)), pltpu.SemaphoreType.DMA((3,)),
                        pltpu.SemaphoreType.DMA((n_sem,)), pltpu.SemaphoreType.DMA((n_sem,))],
        compiler_params=_cparams(dimension_semantics=("arbitrary",)),
    )(*dparts, *dparts, x, dout, s1, nw, h, wt_full, dcw, dvec, sm_a, row0)


MESH = pl.DeviceIdType.MESH


def _place():
    x, y, c = lax.axis_index("x"), lax.axis_index("y"), lax.axis_index("c")
    chips = [(1 - x, y), (x, 1 - y), (1 - x, 1 - y)]
    return x, y, c, chips


def _remote(sems_s, sems_r, k, src, dst, to):
    return pltpu.make_async_remote_copy(src_ref=src, dst_ref=dst, send_sem=sems_s.at[k], recv_sem=sems_r.at[k],
                                        device_id=to, device_id_type=MESH)


RS_CH = 32
RS_SEMS = 5


def _rs_to_sibling(rc, s0, theirs, sib_ref, sib):
    cp = rc(s0, theirs, sib_ref, sib)
    cp.start()
    return cp


def _rs_trade(rc, s0, theirs, mine, sib_ref, out_ref, in_ref, rows, c, sib, chips):
    rc(s0, theirs, sib_ref, sib).wait_recv()
    cps = []
    for k, (cx, cy) in enumerate(chips):
        jk = 2 * cx + cy

        def add(i, carry, jk=jk, k=k):
            rr = pl.ds(pl.multiple_of(i * RS_CH, RS_CH), RS_CH)
            out_ref[k, rr, :] = (mine[jk, rr, :].astype(F32) + sib_ref[jk, rr, :].astype(F32)).astype(BF16)
            return carry

        lax.fori_loop(0, rows // RS_CH, add, 0)
        cps.append(rc(s0 + 1 + k, out_ref.at[k], in_ref.at[k], (cx, cy, c)))
        cps[-1].start()
    return cps


def _rs_total(rc, s0, mine, sib_ref, out_ref, in_ref, res_ref, rows, j, c, sib):
    for k in range(3):
        rc(s0 + 1 + k, out_ref.at[k], in_ref.at[k], sib).wait_recv()

    def total(i, carry):
        rr = pl.ds(pl.multiple_of(i * RS_CH, RS_CH), RS_CH)
        acc = mine[j, rr, :].astype(F32) + sib_ref[j, rr, :].astype(F32)
        for k in range(3):
            acc = acc + in_ref[k, rr, :].astype(F32)
        res_ref[c, rr, :] = acc
        return carry

    lax.fori_loop(0, rows // RS_CH, total, 0)
    cp = rc(s0 + 4, res_ref.at[c], res_ref.at[c], sib)
    cp.start()
    return cp


def _rs_done(rc, s0, res_ref, c, sib):
    rc(s0 + 4, res_ref.at[1 - c], res_ref.at[1 - c], sib).wait_recv()


def _rs_scratch(rows):
    return [pltpu.VMEM((N_CHIPS, rows, D_MODEL), BF16), pltpu.VMEM((3, rows, D_MODEL), BF16),
            pltpu.VMEM((3, rows, D_MODEL), BF16)]


MAIN_W = 640
MAIN_DST = (((0, 0, 512), (1, 0, 128)), ((2, 0, 512), (3, 0, 128)), ((3, 128, 384), (4, 0, 256)), ((4, 384, 128), (5, 0, 512)))
PAIR_DST = ((1, 128, 128), (4, 256, 128))


def _in_proj_gather(x, wt, c_row, w_ada, b_ada, nw):
    t = x.shape[0]
    ch = 512
    n_sem = 16

    def body(x_hbm, wt_ref, c_ref, wada_ref, bada_ref, nw_ref,
             q_hbm, kv_hbm, ga_hbm, ua_hbm, ug_hbm, gb_hbm, h_hbm, w4_hbm, call_ref, s1_ref, gate_ref, ada_ref,
             x_ref, h_ref, w4_ref, stg_ref, pstg_ref, part_ref, lsem, osem, wsem, ssem, rsem):
        outs = (q_hbm, kv_hbm, ga_hbm, ua_hbm, ug_hbm, gb_hbm)
        x_, y_, c, chips = _place()
        j = 2 * x_ + y_
        dev = 2 * j + c
        sib = (x_, y_, 1 - c)
        idx = [2 * cx + cy for cx, cy in chips]
        rc = functools.partial(_remote, ssem, rsem)
        x_copy = pltpu.make_async_copy(x_hbm, x_ref, lsem.at[0])
        x_copy.start()

        def rows_of(s, cc):
            return pl.ds(pl.multiple_of(2 * IN_HALF * s + IN_HALF * cc, 16), IN_HALF)

        w4_ref[rows_of(j, 0), :] = wt_ref[0].astype(BF16)
        w4_ref[rows_of(j, 1), :] = wt_ref[1].astype(BF16)
        call_ref[dev] = c_ref[...]
        sends = []
        peers = [(px, py, pc) for px in (x_, 1 - x_) for py in (y_, 1 - y_) for pc in (c, 1 - c)][1:]
        for k, peer in enumerate(peers):
            sends.append(rc(k, call_ref.at[dev], call_ref.at[dev], peer))
        for cp in sends:
            cp.start()

        for k, (px, py, pc) in enumerate(peers):
            pdev = 4 * px + 2 * py + pc
            rc(k, call_ref.at[pdev], call_ref.at[pdev], (px, py, pc)).wait_recv()
        rowid = lax.broadcasted_iota(jnp.int32, (N_DEV, D_MODEL), 0)
        call = jnp.zeros((N_DEV, D_MODEL), F32)
        for r in range(N_DEV):
            call = jnp.where(rowid == r, jnp.broadcast_to(call_ref[r], (N_DEV, D_MODEL)), call)
        bsh = bada_ref[:, 0:ADA_SHARD]
        for k in range(1, N_CHIPS):
            bsh = jnp.where(j == k, bada_ref[:, ADA_SHARD * k:ADA_SHARD * (k + 1)], bsh)
        part = jnp.dot(_silu(call).astype(BF16), wada_ref[...].astype(BF16), preferred_element_type=F32) + bsh
        for r in range(N_DEV):
            part_ref[r] = part[r:r + 1, :]
        ada_ref[j] = part_ref[dev]
        for k, chip in enumerate(chips):
            sends.append(rc(13 + k, part_ref.at[2 * idx[k] + c], ada_ref.at[j], (*chip, c)))
            sends[-1].start()
        for k, chip in enumerate(chips):
            sends.append(rc(7 + k, w4_ref.at[rows_of(j, c)], w4_ref.at[rows_of(j, c)], (*chip, c)))
            sends[-1].start()

        x_copy.wait()

        def prenorm(i, carry):
            rr = pl.ds(pl.multiple_of(i * ch, ch), ch)
            xv = x_ref[rr, :]
            r = lax.rsqrt(jnp.mean(xv * xv, axis=-1, keepdims=True) + EPS)
            x_ref[rr, :] = (xv * r) * nw_ref[...]
            return carry

        lax.fori_loop(0, t // ch, prenorm, 0)
        for k in range(3):
            rc(13 + k, ada_ref.at[idx[k]], ada_ref.at[idx[k]], sib).wait_recv()

        shift = jnp.concatenate([ada_ref[0], ada_ref[1][:, 0:256]], axis=1)
        s1 = 1.0 + jnp.concatenate([ada_ref[1][:, 256:768], ada_ref[2][:, 0:512]], axis=1)
        s1_ref[...] = s1
        gate_ref[...] = jnp.concatenate([ada_ref[2][:, 512:768], ada_ref[3]], axis=1)

        def norm(i, carry):
            rr = pl.ds(pl.multiple_of(i * ch, ch), ch)
            h_ref[rr, :] = (x_ref[rr, :] * s1 + shift).astype(BF16)
            return carry

        lax.fori_loop(0, t // ch, norm, 0)
        h_copy = pltpu.make_async_copy(h_ref, h_hbm, lsem.at[1])
        h_copy.start()

        def put_main(case, slot):
            cps, col = [], 0
            for n, (a, c0, w) in enumerate(MAIN_DST[case]):
                cps.append(pltpu.make_async_copy(stg_ref.at[slot, :, pl.ds(col, w)], outs[a].at[:, pl.ds(c0, w)], osem.at[slot, n]))
                col += w
            return cps

        def put_pair(case, slot):
            a, c0, w = PAIR_DST[case]
            return pltpu.make_async_copy(pstg_ref.at[slot], outs[a].at[:, pl.ds(c0, w)], osem.at[slot, 2])

        def project(first_row, width, dst, slot):
            wrows = pl.ds(pl.multiple_of(first_row, 128), width)

            def blk(i, carry):
                rr = pl.ds(pl.multiple_of(i * ch, ch), ch)
                dst[slot, rr, :] = lax.dot_general(h_ref[rr, :], w4_ref[wrows, :], (((1,), (1,)), ((), ())),
                                                   preferred_element_type=F32)
                return carry

            lax.fori_loop(0, t // ch, blk, 0)

        def phase(p, s, pair):
            slot = p % 2
            if p >= 2:
                for case in range(N_CHIPS):
                    @pl.when(order[p - 2] == case)
                    def _():
                        for cp in put_main(case, slot):
                            cp.wait()
            if p == 3:
                for case in range(2):
                    @pl.when(j // 2 == case)
                    def _():
                        put_pair(case, 0).wait()
            project(2 * IN_HALF * s + 64 * (s % 2), MAIN_W, stg_ref, slot)
            for case in range(N_CHIPS):
                @pl.when(s == case)
                def _():
                    for cp in put_main(case, slot):
                        cp.start()
            if pair is not None:
                project(MAIN_W + 2 * (2 * IN_HALF) * pair, 128, pstg_ref, slot % 2 if p == 2 else 1)
                for case in range(2):
                    @pl.when(pair == case)
                    def _():
                        put_pair(case, 0 if p == 2 else 1).start()

        order = [j] + idx
        w_out = [pltpu.make_async_copy(w4_ref.at[pl.ds(pl.multiple_of(2 * IN_HALF * s, 32), 2 * IN_HALF)],
                                       w4_hbm.at[pl.ds(pl.multiple_of(2 * IN_HALF * s, 32), 2 * IN_HALF)], wsem.at[p])
                 for p, s in enumerate(order)]
        w_out[0].start()
        phase(0, j, None)
        passed = []
        for k in range(3):
            jk = idx[k]
            rc(7 + k, w4_ref.at[rows_of(jk, c)], w4_ref.at[rows_of(jk, c)], sib).wait_recv()
            passed.append(rc(10 + k, w4_ref.at[rows_of(jk, c)], w4_ref.at[rows_of(jk, c)], sib))
            passed[-1].start()
            rc(10 + k, w4_ref.at[rows_of(jk, 1 - c)], w4_ref.at[rows_of(jk, 1 - c)], sib).wait_recv()
            w_out[1 + k].start()
            if k == 0:
                phase(1, jk, None)
            elif k == 1:
                phase(2, jk, j // 2)
            else:
                phase(3, jk, 1 - j // 2)

        for case in range(N_CHIPS):
            for p in (2, 3):
                @pl.when(order[p] == case)
                def _():
                    for cp in put_main(case, p % 2):
                        cp.wait()
        for case in range(2):
            @pl.when(1 - j // 2 == case)
            def _():
                put_pair(case, 1).wait()
        h_copy.wait()
        for cp in w_out:
            cp.wait()
        for cp in sends + passed:
            cp.wait_send()

    vm = pl.BlockSpec(memory_space=pltpu.VMEM)
    hbm = pl.BlockSpec(memory_space=pl.ANY)
    widths = (512, 256, 512, 512, 512, 512)
    return pl.pallas_call(
        body,
        name="in_proj",
        in_specs=[hbm, vm, vm, vm, vm, vm],
        out_specs=[hbm] * 8 + [vm, vm, vm],
        out_shape=[jax.ShapeDtypeStruct((t, w), F32) for w in widths]
        + [jax.ShapeDtypeStruct((t, D_MODEL), BF16), jax.ShapeDtypeStruct((IN_W, D_MODEL), BF16),
           jax.ShapeDtypeStruct((N_DEV, 1, D_MODEL), F32), jax.ShapeDtypeStruct((1, D_MODEL), F32),
           jax.ShapeDtypeStruct((1, D_MODEL), F32)],
        scratch_shapes=[pltpu.VMEM((N_CHIPS, 1, ADA_SHARD), F32), pltpu.VMEM((t, D_MODEL), F32), pltpu.VMEM((t, D_MODEL), BF16), pltpu.VMEM((IN_W, D_MODEL), BF16),
                        pltpu.VMEM((2, t, MAIN_W), F32), pltpu.VMEM((2, t, 128), F32), pltpu.VMEM((N_DEV, 1, ADA_SHARD), F32),
                        pltpu.SemaphoreType.DMA((2,)), pltpu.SemaphoreType.DMA((2, 3)), pltpu.SemaphoreType.DMA((N_CHIPS,)),
                        pltpu.SemaphoreType.DMA((n_sem,)), pltpu.SemaphoreType.DMA((n_sem,))],
        compiler_params=_cparams(),
    )(x, wt, c_row, w_ada, b_ada, nw)


def _adamw_math(w, g, m, v):
    m2 = ADAM_B1 * m + (1.0 - ADAM_B1) * g
    v2 = ADAM_B2 * v + (1.0 - ADAM_B2) * (g * g)
    m_hat = m2 / (1.0 - ADAM_B1 ** ADAM_STEP)
    v_hat = v2 / (1.0 - ADAM_B2 ** ADAM_STEP)
    delta = -ADAM_LR * (m_hat / (jnp.sqrt(v_hat) + ADAM_EPS) + ADAM_WD * w)
    return delta, m2, v2


def _adamw(name, groups, nstep, through):
    flat = [a for grp in groups for a in grp]
    n = len(flat)

    def body(*refs):
        ins, through_in, outs, through_out = refs[:n], refs[n], refs[n + 1:2 * n + 1], refs[2 * n + 1]
        for k in range(0, n, 4):
            w_ref, g_ref, m_ref, v_ref = ins[k:k + 4]
            g2_ref, d_ref, m2_ref, v2_ref = outs[k:k + 4]
            g = g_ref[...]
            g2_ref[...] = g
            d_ref[...], m2_ref[...], v2_ref[...] = _adamw_math(w_ref[...], g, m_ref[...], v_ref[...])
        through_out[...] = through_in[...]

    specs = [pl.BlockSpec((a.shape[0] // nstep, a.shape[1]), lambda i: (i, 0)) for a in flat + [through]]
    out = pl.pallas_call(
        body,
        name=name,
        grid=(nstep,),
        in_specs=specs,
        out_specs=specs,
        out_shape=[jax.ShapeDtypeStruct(a.shape, a.dtype) for a in flat + [through]],
        compiler_params=_cparams(dimension_semantics=("arbitrary",)),
    )(*flat, through)
    return [out[k:k + 4] for k in range(0, n, 4)], out[n]


def _adamw_ada(w, m, v, call, rows):
    r, cdim = w.shape
    tm = 256

    def body(w_ref, m_ref, v_ref, c_ref, rows_ref, g_ref, d_ref, m2_ref, v2_ref):
        j = 2 * lax.axis_index("x") + lax.axis_index("y")
        d_ada = jnp.concatenate([jnp.concatenate([rows_ref[d, row:row + 1, :] for d in range(N_DEV)], axis=0)
                                 for row in (2, 3, 0)], axis=1)
        dcols = d_ada[:, 0:cdim]
        for k in range(1, N_CHIPS):
            dcols = jnp.where(j == k, d_ada[:, cdim * k:cdim * (k + 1)], dcols)
        cvec = jnp.concatenate([c_ref[d] for d in range(N_DEV)], axis=0)
        g = lax.dot_general(_silu(cvec).astype(BF16), dcols.astype(BF16), (((0,), (0,)), ((), ())),
                            preferred_element_type=F32)
        g_ref[...] = g
        d_ref[...], m2_ref[...], v2_ref[...] = _adamw_math(w_ref[...], g, m_ref[...], v_ref[...])

    blk = pl.BlockSpec((tm, cdim), lambda i: (i, 0))
    return pl.pallas_call(
        body,
        name="adamw_w_ada",
        grid=(r // tm,),
        in_specs=[blk] * 3 + [pl.BlockSpec((N_DEV, 1, tm), lambda i: (0, 0, i)),
                              pl.BlockSpec((N_DEV, 8, D_MODEL), lambda i: (0, 0, 0))],
        out_specs=[blk] * 4,
        out_shape=[jax.ShapeDtypeStruct((r, cdim), F32)] * 4,
        compiler_params=_cparams(dimension_semantics=("arbitrary",)),
    )(w, m, v, call, rows)


def _adamw_small(ws, ms, vs, ssum, rows):
    n = len(ws)

    def body(*refs):
        w_r, m_r, v_r = refs[0:n], refs[n:2 * n], refs[2 * n:3 * n]
        ss_ref, rows_ref = refs[3 * n], refs[3 * n + 1]
        g_r, d_r, m2_r, v2_r = (refs[3 * n + 2 + k * n:3 * n + 2 + (k + 1) * n] for k in range(4))
        loss_ref = refs[7 * n + 2]
        j = 2 * lax.axis_index("x") + lax.axis_index("y")
        rsum = rows_ref[0]
        for d in range(1, N_DEV):
            rsum = rsum + rows_ref[d]
        taps = []
        for t in range(CONV_TAPS):
            row = ss_ref[t // 2:t // 2 + 1, :]
            c0 = CONV_W * (t % 2)
            pick = row[:, c0:c0 + 128]
            for k in range(1, N_CHIPS):
                pick = jnp.where(j == k, row[:, c0 + 128 * k:c0 + 128 * (k + 1)], pick)
            taps.append(pick)
        grads = [jnp.concatenate([rsum[2:3], rsum[3:4], rsum[0:1]], axis=1), rsum[4:5],
                 ss_ref[17:18, 512:512 + HEAD_DIM], ss_ref[17:18, 640:640 + HEAD_DIM], ss_ref[17:18, 768:776],
                 None, ss_ref[16:17, 0:CONV_W], ss_ref[16:17, CONV_W:2 * CONV_W], ss_ref[17:18, 0:CONV_W]]
        for i in range(n):
            if grads[i] is None:
                g = jnp.concatenate(taps, axis=0)
                w, m, v = (jnp.concatenate([ref[t] for t in range(CONV_TAPS)], axis=0) for ref in (w_r[i], m_r[i], v_r[i]))
                res = (g,) + _adamw_math(w, g, m, v)
                for ref, val in zip((g_r[i], d_r[i], m2_r[i], v2_r[i]), res):
                    for t in range(CONV_TAPS):
                        ref[t] = val[t:t + 1, :]
                continue
            g = grads[i]
            g_r[i][...] = g
            d_r[i][...], m2_r[i][...], v2_r[i][...] = _adamw_math(w_r[i][...], g, m_r[i][...], v_r[i][...])
        loss_ref[...] = (0.5 / D_MODEL) * jnp.sum(ss_ref[18:19, :], axis=1, keepdims=True)

    vm = pl.BlockSpec(memory_space=pltpu.VMEM)
    shapes = [jax.ShapeDtypeStruct(w.shape, F32) for w in ws]
    out = pl.pallas_call(
        body,
        name="adamw_small",
        in_specs=[vm] * (3 * n + 2),
        out_specs=[vm] * (4 * n + 1),
        out_shape=shapes * 4 + [jax.ShapeDtypeStruct((1, 1), F32)],
        compiler_params=_cparams(),
    )(*ws, *ms, *vs, ssum, rows)
    return out[0:n], out[n:2 * n], out[2 * n:3 * n], out[3 * n:4 * n], out[4 * n]


def _rope_tables(t):
    inv = ROPE_THETA ** (-jnp.arange(0, HEAD_DIM, 2, dtype=F32) / HEAD_DIM)
    ang = jnp.arange(t, dtype=F32)[:, None] * inv[None, :]
    cos, sin = jnp.cos(ang), jnp.sin(ang)
    return jnp.tile(cos, (1, 4)), jnp.tile(jnp.concatenate([-sin, sin], axis=1), (1, 2))


def kernel(x, c, w_ada, b_ada, norm_w, w_in, q_norm_w, k_norm_w, sinks, conv_w, conv_b, ln_w, ln_b, w_out, loss_target, m_w_ada, m_b_ada, m_norm_w, m_w_in, m_q_norm_w, m_k_norm_w, m_sinks, m_conv_w, m_conv_b, m_ln_w, m_ln_b, m_w_out, v_w_ada, v_b_ada, v_norm_w, v_w_in, v_q_norm_w, v_k_norm_w, v_sinks, v_conv_w, v_conv_b, v_ln_w, v_ln_b, v_w_out):
    xi, yi = lax.axis_index("x"), lax.axis_index("y")
    j = 2 * xi + yi
    x2, tgt = x[0], loss_target[0]
    t = x2.shape[0]

    wt_s, mt_s, vt_s = w_in[0].T, m_w_in[0].T, v_w_in[0].T
    by_tap = lambda a: jnp.transpose(a, (1, 0, 2))

    q_raw, kv_raw, ga, ua, ug, gb, h, w_full, call, s1, gate = _in_proj_gather(
        x2, wt_s.reshape(2, IN_HALF, D_MODEL), c, w_ada[0], b_ada, norm_w)

    cos_f, sin_s = _rope_tables(t)
    qkw2 = jnp.tile(jnp.concatenate([q_norm_w, k_norm_w], axis=0), (1, 2))

    o, mix_a, wo4, cw4 = _attn_fwd(q_raw, kv_raw, ga, qkw2, sinks, cos_f, sin_s,
                                   w_out[0].reshape(2, OUT_HALF, D_MODEL), by_tap(conv_w))
    w_out_full = wo4.reshape(D_MODEL, D_MODEL)
    cz, mix_b = _conv_fwd(ua, ug, gb, cw4, conv_b, ln_w, ln_b)
    dout, dmix_a, dmix_b, gwo_bf, red_o = _out_proj(mix_a, mix_b, x2, tgt, gate, w_out_full)

    dq, dkv, dga, sm_a, gwo = _attn_bwd(q_raw, kv_raw, ga, o, dmix_a, qkw2, sinks, cos_f, sin_s,
                                        gwo_bf.reshape(N_CHIPS, 2, OUT_HALF, D_MODEL))
    dua, dug, dgb, dcw, dvec = _conv_bwd(ua, ug, gb, cz, dmix_b, cw4, ln_w, ln_b)
    dparts = (dq, dkv, dga, dua, dug, dgb)

    grad_x, gw, ssum, rows = _in_proj_bwd(dparts, h, x2, dout, s1, norm_w, w_full, dcw, dvec, sm_a, red_o)

    gt_w_in = gw.reshape(2 * IN_HALF, D_MODEL)
    g_w_out = gwo.reshape(D_MODEL // N_CHIPS, D_MODEL)

    g_w_ada, d_w_ada, nm_w_ada, nv_w_ada = _adamw_ada(w_ada[0], m_w_ada[0], v_w_ada[0], call, rows)
    (in_res, out_res), grad_x = _adamw("adamw_w", [(wt_s, gt_w_in, mt_s, vt_s), (w_out[0], g_w_out, m_w_out[0], v_w_out[0])],
                                        8, grad_x)
    gt_w_in, dt_w_in, nmt_w_in, nvt_w_in = in_res
    g_w_out, d_w_out, nm_w_out, nv_w_out = out_res
    g_w_in, d_w_in, nm_w_in, nv_w_in = gt_w_in.T, dt_w_in.T, nmt_w_in.T, nvt_w_in.T
    ws = [b_ada, norm_w, q_norm_w, k_norm_w, sinks, by_tap(conv_w), conv_b, ln_w, ln_b]
    ms = [m_b_ada, m_norm_w, m_q_norm_w, m_k_norm_w, m_sinks, by_tap(m_conv_w), m_conv_b, m_ln_w, m_ln_b]
    vs = [v_b_ada, v_norm_w, v_q_norm_w, v_k_norm_w, v_sinks, by_tap(v_conv_w), v_conv_b, v_ln_w, v_ln_b]
    gs, ds, nms, nvs, loss11 = _adamw_small(ws, ms, vs, ssum, rows)
    loss = loss11[0, 0]

    def order(ada_v, in_v, out_v, sm):
        b, nw_, qw_, kw_, sk_, cw_, cb_, lw_, lb_ = sm
        return [ada_v[None], b, nw_, in_v[None], qw_, kw_, sk_, by_tap(cw_), cb_, lw_, lb_, out_v[None]]

    grads = order(g_w_ada, g_w_in, g_w_out, gs)
    deltas = order(d_w_ada, d_w_in, d_w_out, ds)
    new_m = order(nm_w_ada, nm_w_in, nm_w_out, nms)
    new_v = order(nv_w_ada, nv_w_in, nv_w_out, nvs)
    return (loss, grad_x[None], *grads, *deltas, *new_m, *new_v)
```

```python
import functools

import jax
import jax.numpy as jnp
from jax import lax
from jax.experimental import pallas as pl
from jax.experimental.pallas import tpu as pltpu

F32 = jnp.float32
BF16 = jnp.bfloat16

D_MODEL = 1024
ATTN_W = 512
KV_W = 128
CONV_W = 512
IN_W = 2816
HEAD_DIM = 64
CONV_TAPS = 31
QBLK = 128
EPS = 1e-6
ROPE_THETA = 10000.0

ADAM_LR = 0.001
ADAM_B1 = 0.9
ADAM_B2 = 0.999
ADAM_EPS = 1e-08
ADAM_WD = 0.01
ADAM_STEP = 10

N_CHIPS = 4
N_DEV = 8
IN_HALF = IN_W // N_CHIPS // 2
OUT_HALF = D_MODEL // N_CHIPS // 2
ADA_SHARD = 3 * D_MODEL // N_CHIPS

VMEM_LIMIT = 56 * 1024 * 1024
CONV_PAD = 32


def _cparams(**kw):
    return pltpu.CompilerParams(vmem_limit_bytes=VMEM_LIMIT, **kw)


def _sigmoid(v):
    return 1.0 / (1.0 + jnp.exp(-v))


def _silu(v):
    return v * _sigmoid(v)


def _dsilu(v):
    s = _sigmoid(v)
    return s * (1.0 + v * (1.0 - s))


def _lane(shape):
    return lax.broadcasted_iota(jnp.int32, shape, len(shape) - 1)


PUT_ROWS = 512


def _fetch(hbm_refs, vmem_refs, sem):
    cps = [pltpu.make_async_copy(h, v, sem.at[i]) for i, (h, v) in enumerate(zip(hbm_refs, vmem_refs))]
    for cp in cps:
        cp.start()
    return cps


def _put(vmem_ref, hbm_ref, sem, m):
    r = pl.ds(pl.multiple_of(m * PUT_ROWS, PUT_ROWS), PUT_ROWS)
    return pltpu.make_async_copy(vmem_ref.at[r], hbm_ref.at[r], sem.at[m])


def _put_all(pairs, sems, m):
    for (v, h), sem in zip(pairs, sems):
        _put(v, h, sem, m).start()


def _put_wait(pairs, sems, n):
    for (v, h), sem in zip(pairs, sems):
        for m in range(n):
            _put(v, h, sem, m).wait()


def _head_mean(s, left):
    sl = jnp.sum(jnp.where(left, s, 0.0), axis=-1, keepdims=True)
    sr = jnp.sum(jnp.where(left, 0.0, s), axis=-1, keepdims=True)
    return jnp.where(left, sl, sr) * (1.0 / HEAD_DIM)


def _rot(v, first):
    return jnp.where(first, pltpu.roll(v, 96, 1), pltpu.roll(v, 32, 1))


def _norm_rope(v, w, cos, sin_s, left, first):
    r = lax.rsqrt(_head_mean(v * v, left) + EPS)
    xh = v * r
    n = xh * w
    return n * cos + _rot(n, first) * sin_s, xh, r


def _norm_rope_bwd(d, xh, r, w, cos, sin_s, left, first):
    dn = d * cos - _rot(d, first) * sin_s
    dw = jnp.sum(dn * xh, axis=0, keepdims=True)
    dxh = dn * w
    return r * (dxh - xh * _head_mean(dxh * xh, left)), dw


def _dup_heads(v, left):
    sw = pltpu.roll(v, 64, 1)
    return jnp.where(left, v, sw), jnp.where(left, sw, v)


def _prep_kv(kv_ref, kw_ref, cos_ref, sin_ref, ka_ref, va_ref, t):
    ch = 256
    for g in range(2):
        ka_ref[g, 0:QBLK, :] = jnp.zeros((QBLK, 128), BF16)
        va_ref[g, 0:QBLK, :] = jnp.zeros((QBLK, 128), BF16)

    def chunk(i, carry):
        r0 = pl.multiple_of(i * ch, ch)
        left = _lane((ch, 128)) < 64
        first = (_lane((ch, 128)) % 64) < 32
        k = kv_ref[pl.ds(r0, ch), 0:128]
        v = kv_ref[pl.ds(r0, ch), 128:256]
        kr, _, _ = _norm_rope(k, kw_ref[...], cos_ref[pl.ds(r0, ch), :], sin_ref[pl.ds(r0, ch), :], left, first)
        k0, k1 = _dup_heads(kr, left)
        v0, v1 = _dup_heads(v, left)
        ka_ref[0, pl.ds(QBLK + r0, ch), :] = k0.astype(BF16)
        ka_ref[1, pl.ds(QBLK + r0, ch), :] = k1.astype(BF16)
        va_ref[0, pl.ds(QBLK + r0, ch), :] = v0.astype(BF16)
        va_ref[1, pl.ds(QBLK + r0, ch), :] = v1.astype(BF16)
        return carry

    lax.fori_loop(0, t // ch, chunk, 0)


def _band_mask(n):
    qi = lax.broadcasted_iota(jnp.int32, (2 * QBLK, 2 * QBLK), 0) % QBLK
    kj = lax.broadcasted_iota(jnp.int32, (2 * QBLK, 2 * QBLK), 1)
    local = (kj > qi) & (kj <= qi + QBLK)
    return local & ((n > 0) | (kj >= QBLK))


def _softmax_pair(s, mask, sink0, sink1):
    row = lax.broadcasted_iota(jnp.int32, (2 * QBLK, 1), 0)
    sink = jnp.where(row < QBLK, sink0, sink1)
    s = jnp.where(mask, s, -jnp.inf)
    m = jnp.maximum(jnp.max(s, axis=-1, keepdims=True), sink)
    e = jnp.exp(s - m)
    es = jnp.exp(sink - m)
    inv = 1.0 / (jnp.sum(e, axis=-1, keepdims=True) + es)
    return e * inv, es * inv


def _stack_heads(v, left):
    return jnp.concatenate([jnp.where(left, v, 0.0), jnp.where(left, 0.0, v)], axis=0)


def _attn_fwd(q_raw, kv_raw, ga, qkw2, sinks, cos_f, sin_s, wo, cw):
    t = q_raw.shape[0]
    nblk = t // QBLK
    per_put = PUT_ROWS // QBLK

    def body(q_hbm, kv_ref, ga_hbm, qkw_ref, sk_ref, cos_hbm, sin_hbm, wo_ref, cw_ref,
             o_hbm, mix_hbm, wo4_ref, cw4_ref, ka_ref, va_ref, q_ref, ga_ref, o_ref, mix_ref, cos_ref, sin_ref,
             isem, osem0, osem1, ssem, rsem):
        qw_ref, kw_ref = qkw_ref.at[0:1], qkw_ref.at[1:2]
        loads = _fetch((cos_hbm, sin_hbm, q_hbm, ga_hbm), (cos_ref, sin_ref, q_ref, ga_ref), isem)
        outs, osems = ((o_ref, o_hbm), (mix_ref, mix_hbm)), (osem0, osem1)
        x, y, c, chips = _place()
        j = 2 * x + y
        sib = (x, y, 1 - c)
        idx = [2 * cx + cy for cx, cy in chips]
        rc = functools.partial(_remote, ssem, rsem)
        wo4_ref[j] = wo_ref[...].astype(BF16)
        for tap in range(CONV_TAPS):
            cw4_ref[j, tap:tap + 1, :] = cw_ref[tap]
        cw4_ref[j, CONV_TAPS:, :] = jnp.zeros((CONV_PAD - CONV_TAPS, 128), F32)
        sends = []
        for k, chip in enumerate(chips):
            sends.append(rc(k, wo4_ref.at[j, c], wo4_ref.at[j, c], (*chip, c)))
            sends.append(rc(6 + k, cw4_ref.at[j], cw4_ref.at[j], (*chip, c)))
        for cp in sends:
            cp.start()

        loads[0].wait()
        loads[1].wait()
        _prep_kv(kv_ref, kw_ref, cos_ref, sin_ref, ka_ref, va_ref, t)
        loads[2].wait()
        loads[3].wait()

        def blk(n, carry):
            r0 = pl.multiple_of(n * QBLK, QBLK)
            left = _lane((QBLK, 128)) < 64
            first = (_lane((QBLK, 128)) % 64) < 32
            cos = cos_ref[pl.ds(r0, QBLK), :]
            sin = sin_ref[pl.ds(r0, QBLK), :]
            mask = _band_mask(n)
            scores = []
            for p in range(4):
                lanes = slice(p * 128, (p + 1) * 128)
                qr, _, _ = _norm_rope(q_ref[pl.ds(r0, QBLK), lanes], qw_ref[...], cos, sin, left, first)
                q2 = _stack_heads(qr * 0.125, left).astype(BF16)
                scores.append(lax.dot_general(q2, ka_ref[p // 2, pl.ds(r0, 2 * QBLK), :], (((1,), (1,)), ((), ())),
                                              preferred_element_type=F32))
            probs = [_softmax_pair(scores[p], mask, sk_ref[0, 2 * p], sk_ref[0, 2 * p + 1])[0].astype(BF16)
                     for p in range(4)]
            for p in range(4):
                lanes = slice(p * 128, (p + 1) * 128)
                o2 = jnp.dot(probs[p], va_ref[p // 2, pl.ds(r0, 2 * QBLK), :], preferred_element_type=F32)
                o = jnp.where(left, o2[0:QBLK], o2[QBLK:2 * QBLK])
                o_ref[pl.ds(r0, QBLK), lanes] = o.astype(BF16)
                mix_ref[pl.ds(r0, QBLK), lanes] = (o * _silu(ga_ref[pl.ds(r0, QBLK), lanes])).astype(BF16)

            @pl.when(n % per_put == per_put - 1)
            def _():
                _put_all(outs, osems, n // per_put)

            return carry

        lax.fori_loop(0, nblk, blk, 0)
        _put_wait(outs, osems, t // PUT_ROWS)

        passed = []
        for k, chip in enumerate(chips):
            jk = idx[k]
            rc(k, wo4_ref.at[jk, c], wo4_ref.at[jk, c], sib).wait_recv()
            passed.append(rc(3 + k, wo4_ref.at[jk, c], wo4_ref.at[jk, c], sib))
            passed[-1].start()
        for k, chip in enumerate(chips):
            jk = idx[k]
            rc(3 + k, wo4_ref.at[jk, 1 - c], wo4_ref.at[jk, 1 - c], sib).wait_recv()
            rc(6 + k, cw4_ref.at[jk], cw4_ref.at[jk], sib).wait_recv()
        for cp in sends + passed:
            cp.wait_send()

    vm = pl.BlockSpec(memory_space=pltpu.VMEM)
    hbm = pl.BlockSpec(memory_space=pl.ANY)
    n_sem = 9
    return pl.pallas_call(
        body,
        name="attn_fwd",
        in_specs=[hbm, vm, hbm, vm, pl.BlockSpec(memory_space=pltpu.SMEM), hbm, hbm, vm, vm],
        out_specs=[hbm, hbm, vm, vm],
        out_shape=[jax.ShapeDtypeStruct((t, ATTN_W), BF16), jax.ShapeDtypeStruct((t, ATTN_W), BF16),
                   jax.ShapeDtypeStruct((N_CHIPS, 2, OUT_HALF, D_MODEL), BF16),
                   jax.ShapeDtypeStruct((N_CHIPS, 32, 128), F32)],
        scratch_shapes=[pltpu.VMEM((2, t + QBLK, 128), BF16), pltpu.VMEM((2, t + QBLK, 128), BF16),
                        pltpu.VMEM((t, ATTN_W), F32), pltpu.VMEM((t, ATTN_W), F32),
                        pltpu.VMEM((t, ATTN_W), BF16), pltpu.VMEM((t, ATTN_W), BF16),
                        pltpu.VMEM((t, 128), F32), pltpu.VMEM((t, 128), F32),
                        pltpu.SemaphoreType.DMA((4,)), pltpu.SemaphoreType.DMA((t // PUT_ROWS,)),
                        pltpu.SemaphoreType.DMA((t // PUT_ROWS,)),
                        pltpu.SemaphoreType.DMA((n_sem,)), pltpu.SemaphoreType.DMA((n_sem,))],
        compiler_params=_cparams(),
    )(q_raw, kv_raw, ga, qkw2, sinks, cos_f, sin_s, wo, cw)


def _attn_bwd(q_raw, kv_raw, ga, o, dmix, qkw2, sinks, cos_f, sin_s, go):
    t = q_raw.shape[0]
    nblk = t // QBLK
    per_put = PUT_ROWS // QBLK

    def body(q_hbm, kv_ref, ga_hbm, o_hbm, dm_hbm, qkw_ref, sk_ref, cos_hbm, sin_hbm, go_ref,
             dq_hbm, dkv_ref, dga_hbm, sm_ref, gwo_ref, ka_ref, va_ref, dka_ref, dva_ref,
             sibo_ref, outo_ref, ino_ref, q_ref, ga_ref, o_ref, dm_ref, dq_ref, dga_ref, cos_ref, sin_ref,
             isem, osem0, osem1, ssem, rsem):
        qw_ref, kw_ref = qkw_ref.at[0:1], qkw_ref.at[1:2]
        loads = _fetch((cos_hbm, sin_hbm, q_hbm, ga_hbm, o_hbm, dm_hbm), (cos_ref, sin_ref, q_ref, ga_ref, o_ref, dm_ref), isem)
        outs, osems = ((dq_ref, dq_hbm), (dga_ref, dga_hbm)), (osem0, osem1)
        x, y, c, chips = _place()
        sib = (x, y, 1 - c)
        rc = functools.partial(_remote, ssem, rsem)
        theirs, mine = go_ref.at[:, 1 - c], go_ref.at[:, c]
        sends = [_rs_to_sibling(rc, 0, theirs, sibo_ref, sib)]
        loads[0].wait()
        loads[1].wait()
        _prep_kv(kv_ref, kw_ref, cos_ref, sin_ref, ka_ref, va_ref, t)
        dka_ref[...] = jnp.zeros_like(dka_ref)
        dva_ref[...] = jnp.zeros_like(dva_ref)
        sends += _rs_trade(rc, 0, theirs, mine, sibo_ref, outo_ref, ino_ref, OUT_HALF, c, sib, chips)
        for cp in loads[2:]:
            cp.wait()

        def blk(n, carry):
            dqw, dsk = carry
            r0 = pl.multiple_of(n * QBLK, QBLK)
            left = _lane((QBLK, 128)) < 64
            first = (_lane((QBLK, 128)) % 64) < 32
            cos = cos_ref[pl.ds(r0, QBLK), :]
            sin = sin_ref[pl.ds(r0, QBLK), :]
            mask = _band_mask(n)
            row = lax.broadcasted_iota(jnp.int32, (2 * QBLK, 1), 0)
            rows = pl.ds(r0, QBLK)
            win = pl.ds(r0, 2 * QBLK)
            lane_of = [slice(p * 128, (p + 1) * 128) for p in range(4)]
            for grp in ((0, 1), (2, 3)):
                qn = {p: _norm_rope(q_ref[rows, lane_of[p]], qw_ref[...], cos, sin, left, first) for p in grp}
                q2 = {p: _stack_heads(qn[p][0] * 0.125, left).astype(BF16) for p in grp}
                sc = {p: lax.dot_general(q2[p], ka_ref[p // 2, win, :], (((1,), (1,)), ((), ())),
                                         preferred_element_type=F32) for p in grp}
                do2 = {}
                for p in grp:
                    gav = ga_ref[rows, lane_of[p]]
                    dmv = dm_ref[rows, lane_of[p]].astype(F32)
                    dga_ref[rows, lane_of[p]] = (dmv * o_ref[rows, lane_of[p]].astype(F32) * _dsilu(gav)).astype(BF16)
                    do2[p] = _stack_heads(dmv * _silu(gav), left).astype(BF16)
                dpm = {p: lax.dot_general(do2[p], va_ref[p // 2, win, :], (((1,), (1,)), ((), ())),
                                          preferred_element_type=F32) for p in grp}
                sm = {p: _softmax_pair(sc[p], mask, sk_ref[0, 2 * p], sk_ref[0, 2 * p + 1]) for p in grp}
                dsl = {}
                for p in grp:
                    pm, ps = sm[p]
                    delta = jnp.sum(pm * dpm[p], axis=-1, keepdims=True)
                    dsl[p] = (pm * (dpm[p] - delta)).astype(BF16)
                    pd = ps * delta
                    d0 = jnp.sum(jnp.where(row < QBLK, pd, 0.0), axis=0, keepdims=True)
                    d1 = jnp.sum(jnp.where(row < QBLK, 0.0, pd), axis=0, keepdims=True)
                    l8 = _lane((1, 128))
                    dsk = dsk - jnp.where(l8 == 2 * p, d0, 0.0) - jnp.where(l8 == 2 * p + 1, d1, 0.0)
                for p in grp:
                    g = p // 2
                    dva_ref[g, win, :] += lax.dot_general(sm[p][0].astype(BF16), do2[p], (((0,), (0,)), ((), ())),
                                                          preferred_element_type=F32)
                    dka_ref[g, win, :] += lax.dot_general(dsl[p], q2[p], (((0,), (0,)), ((), ())),
                                                          preferred_element_type=F32)
                for p in grp:
                    dq2 = jnp.dot(dsl[p], ka_ref[p // 2, win, :], preferred_element_type=F32)
                    dqr = jnp.where(left, dq2[0:QBLK], dq2[QBLK:2 * QBLK]) * 0.125
                    dq, dw = _norm_rope_bwd(dqr, qn[p][1], qn[p][2], qw_ref[...], cos, sin, left, first)
                    dq_ref[rows, lane_of[p]] = dq.astype(BF16)
                    dqw = dqw + dw

            @pl.when(n % per_put == per_put - 1)
            def _():
                _put_all(outs, osems, n // per_put)

            return dqw, dsk

        zero = jnp.zeros((1, 128), F32)
        dqw, dsk = lax.fori_loop(0, nblk, blk, (zero, zero))

        ch = 256

        def chunk(i, dkw):
            r0 = pl.multiple_of(i * ch, ch)
            left = _lane((ch, 128)) < 64
            first = (_lane((ch, 128)) % 64) < 32
            rows = pl.ds(r0, ch)
            prow = pl.ds(QBLK + r0, ch)

            def fold(ref):
                a0 = ref[0, prow, :]
                a1 = ref[1, prow, :]
                return jnp.where(left, a0 + pltpu.roll(a0, 64, 1), a1 + pltpu.roll(a1, 64, 1))

            cos = cos_ref[rows, :]
            sin = sin_ref[rows, :]
            _, xh, r = _norm_rope(kv_ref[rows, 0:128], kw_ref[...], cos, sin, left, first)
            dk, dw = _norm_rope_bwd(fold(dka_ref), xh, r, kw_ref[...], cos, sin, left, first)
            dkv_ref[rows, 0:128] = dk.astype(BF16)
            dkv_ref[rows, 128:256] = fold(dva_ref).astype(BF16)
            return dkw + dw

        dkw = lax.fori_loop(0, t // ch, chunk, zero)
        sm_ref[...] = jnp.zeros((8, 128), F32)
        sm_ref[0:1, :] = dqw + pltpu.roll(dqw, 64, 1)
        sm_ref[1:2, :] = dkw + pltpu.roll(dkw, 64, 1)
        sm_ref[2:3, :] = dsk

        j = 2 * x + y
        sends.append(_rs_total(rc, 0, mine, sibo_ref, outo_ref, ino_ref, gwo_ref, OUT_HALF, j, c, sib))
        _rs_done(rc, 0, gwo_ref, c, sib)
        for cp in sends:
            cp.wait_send()
        _put_wait(outs, osems, t // PUT_ROWS)

    vm = pl.BlockSpec(memory_space=pltpu.VMEM)
    hbm = pl.BlockSpec(memory_space=pl.ANY)
    return pl.pallas_call(
        body,
        name="attn_bwd",
        in_specs=[hbm, vm, hbm, hbm, hbm, vm, pl.BlockSpec(memory_space=pltpu.SMEM), hbm, hbm, vm],
        out_specs=[hbm, vm, hbm, vm, vm],
        out_shape=[jax.ShapeDtypeStruct((t, ATTN_W), BF16), jax.ShapeDtypeStruct((t, 2 * KV_W), BF16),
                   jax.ShapeDtypeStruct((t, ATTN_W), BF16), jax.ShapeDtypeStruct((8, 128), F32),
                   jax.ShapeDtypeStruct((2, OUT_HALF, D_MODEL), F32)],
        scratch_shapes=[pltpu.VMEM((2, t + QBLK, 128), BF16), pltpu.VMEM((2, t + QBLK, 128), BF16),
                        pltpu.VMEM((2, t + QBLK, 128), F32), pltpu.VMEM((2, t + QBLK, 128), F32)]
        + _rs_scratch(OUT_HALF)
        + [pltpu.VMEM((t, ATTN_W), F32), pltpu.VMEM((t, ATTN_W), F32), pltpu.VMEM((t, ATTN_W), BF16),
           pltpu.VMEM((t, ATTN_W), BF16), pltpu.VMEM((t, ATTN_W), BF16), pltpu.VMEM((t, ATTN_W), BF16),
           pltpu.VMEM((t, 128), F32), pltpu.VMEM((t, 128), F32),
           pltpu.SemaphoreType.DMA((6,)), pltpu.SemaphoreType.DMA((t // PUT_ROWS,)), pltpu.SemaphoreType.DMA((t // PUT_ROWS,)),
           pltpu.SemaphoreType.DMA((RS_SEMS,)), pltpu.SemaphoreType.DMA((RS_SEMS,))],
        compiler_params=_cparams(),
    )(q_raw, kv_raw, ga, o, dmix, qkw2, sinks, cos_f, sin_s, go)


CONV_CH = 256
CONV_SUB = 128
CONV_ACCS = 1


def _shifted_windows(src_ref, r0, sh_ref):
    rows = CONV_CH + CONV_PAD
    win = src_ref[pl.ds(r0, rows), :]
    for b in range(8):
        sh = win if b == 0 else pltpu.roll(win, rows - b, 0)
        for c in range(CONV_W // 128):
            sh_ref[b, c] = sh[:, c * 128:(c + 1) * 128]


def _conv_fwd(ua, ug, gb, cw, cb, lw, lb):
    t = ua.shape[0]

    def body(ua_hbm, ug_hbm, gb_hbm, cw_ref, cb_ref, lw_ref, lb_ref, cz_hbm, mix_hbm, zp_ref, sh_ref,
             ua_ref, ug_ref, gb_ref, cz_ref, mix_ref, isem, osem0, osem1):
        loads = _fetch((ua_hbm, ug_hbm, gb_hbm), (ua_ref, ug_ref, gb_ref), isem)
        outs, osems = ((cz_ref, cz_hbm), (mix_ref, mix_hbm)), (osem0, osem1)
        per_put = PUT_ROWS // CONV_CH
        zp_ref[0:CONV_PAD, :] = jnp.zeros((CONV_PAD, CONV_W), F32)
        loads[0].wait()
        loads[1].wait()

        def glu(i, carry):
            r0 = pl.multiple_of(i * CONV_CH, CONV_CH)
            rows = pl.ds(r0, CONV_CH)
            zp_ref[pl.ds(CONV_PAD + r0, CONV_CH), :] = ua_ref[rows, :] * _sigmoid(ug_ref[rows, :])
            return carry

        lax.fori_loop(0, t // CONV_CH, glu, 0)
        loads[2].wait()

        def chunk(i, carry):
            r0 = pl.multiple_of(i * CONV_CH, CONV_CH)
            _shifted_windows(zp_ref, r0, sh_ref)
            for c in range(CONV_W // 128):
                lanes = slice(c * 128, (c + 1) * 128)

                def sub(k, carry2):
                    b0 = pl.multiple_of(k * CONV_SUB, CONV_SUB)
                    acc = [jnp.broadcast_to(cb_ref[0:1, lanes], (CONV_SUB, 128))] + [None] * (CONV_ACCS - 1)
                    for j in range(CONV_TAPS):
                        off = j + CONV_PAD - (CONV_TAPS - 1)
                        term = sh_ref[off % 8, c, pl.ds(b0 + 8 * (off // 8), CONV_SUB), :] * cw_ref[c, j:j + 1, :]
                        acc[j % CONV_ACCS] = term if acc[j % CONV_ACCS] is None else acc[j % CONV_ACCS] + term
                    cz_ref[pl.ds(r0 + b0, CONV_SUB), lanes] = functools.reduce(lambda a, b: a + b, acc)
                    return carry2

                lax.fori_loop(0, CONV_CH // CONV_SUB, sub, 0)
            rows = pl.ds(r0, CONV_CH)
            cz = cz_ref[rows, :]
            mu = jnp.mean(cz, axis=-1, keepdims=True)
            xc = cz - mu
            rs = lax.rsqrt(jnp.mean(xc * xc, axis=-1, keepdims=True) + EPS)
            ln = xc * rs * lw_ref[...] + lb_ref[...]
            mix_ref[rows, :] = (_silu(ln) * _silu(gb_ref[rows, :])).astype(BF16)

            @pl.when(i % per_put == per_put - 1)
            def _():
                _put_all(outs, osems, i // per_put)

            return carry

        lax.fori_loop(0, t // CONV_CH, chunk, 0)
        _put_wait(outs, osems, t // PUT_ROWS)

    vm = pl.BlockSpec(memory_space=pltpu.VMEM)
    hbm = pl.BlockSpec(memory_space=pl.ANY)
    nput = t // PUT_ROWS
    return pl.pallas_call(
        body,
        name="conv_fwd",
        in_specs=[hbm] * 3 + [vm] * 4,
        out_specs=[hbm, hbm],
        out_shape=[jax.ShapeDtypeStruct((t, CONV_W), F32), jax.ShapeDtypeStruct((t, CONV_W), BF16)],
        scratch_shapes=[pltpu.VMEM((t + CONV_PAD, CONV_W), F32),
                        pltpu.VMEM((8, CONV_W // 128, CONV_CH + CONV_PAD, 128), F32),
                        pltpu.VMEM((t, CONV_W), F32), pltpu.VMEM((t, CONV_W), F32), pltpu.VMEM((t, CONV_W), F32),
                        pltpu.VMEM((t, CONV_W), F32), pltpu.VMEM((t, CONV_W), BF16),
                        pltpu.SemaphoreType.DMA((3,)), pltpu.SemaphoreType.DMA((nput,)), pltpu.SemaphoreType.DMA((nput,))],
        compiler_params=_cparams(),
    )(ua, ug, gb, cw, cb, lw, lb)


def _conv_bwd(ua, ug, gb, cz, dmix, cw, lw, lb):
    t = ua.shape[0]

    def body(ua_hbm, ug_hbm, gb_hbm, cz_hbm, dm_hbm, cw_ref, lw_ref, lb_ref,
             dua_hbm, dug_hbm, dgb_hbm, dcw_ref, dvec_ref, zp_ref, dp_ref, sh_ref, wacc_ref,
             ua_ref, ug_ref, gb_ref, cz_ref, dm_ref, dua_ref, dug_ref, dgb_ref, isem, osem0, osem1, osem2):
        loads = _fetch((ua_hbm, ug_hbm, gb_hbm, cz_hbm, dm_hbm), (ua_ref, ug_ref, gb_ref, cz_ref, dm_ref), isem)
        per_put = PUT_ROWS // CONV_CH
        zp_ref[0:CONV_PAD, :] = jnp.zeros((CONV_PAD, CONV_W), F32)
        dp_ref[t:t + CONV_PAD, :] = jnp.zeros((CONV_PAD, CONV_W), F32)
        wacc_ref[...] = jnp.zeros_like(wacc_ref)
        for cp in loads:
            cp.wait()

        def pointwise(i, carry):
            dcb, dlw, dlb = carry
            r0 = pl.multiple_of(i * CONV_CH, CONV_CH)
            rows = pl.ds(r0, CONV_CH)
            zp_ref[pl.ds(CONV_PAD + r0, CONV_CH), :] = ua_ref[rows, :] * _sigmoid(ug_ref[rows, :])
            cz = cz_ref[rows, :]
            mu = jnp.mean(cz, axis=-1, keepdims=True)
            xc = cz - mu
            rs = lax.rsqrt(jnp.mean(xc * xc, axis=-1, keepdims=True) + EPS)
            xh = xc * rs
            ln = xh * lw_ref[...] + lb_ref[...]
            gbv = gb_ref[rows, :]
            dy = dm_ref[rows, :].astype(F32)
            dgb_ref[rows, :] = (dy * _silu(ln) * _dsilu(gbv)).astype(BF16)
            dl = dy * _silu(gbv) * _dsilu(ln)
            dxh = dl * lw_ref[...]
            dcz = rs * (dxh - jnp.mean(dxh, axis=-1, keepdims=True)
                        - xh * jnp.mean(dxh * xh, axis=-1, keepdims=True))
            dp_ref[rows, :] = dcz

            @pl.when(i % per_put == per_put - 1)
            def _():
                _put(dgb_ref, dgb_hbm, osem2, i // per_put).start()

            return (dcb + jnp.sum(dcz, axis=0, keepdims=True),
                    dlw + jnp.sum(dl * xh, axis=0, keepdims=True),
                    dlb + jnp.sum(dl, axis=0, keepdims=True))

        zero = jnp.zeros((1, CONV_W), F32)
        dcb, dlw, dlb = lax.fori_loop(0, t // CONV_CH, pointwise, (zero, zero, zero))
        dvec_ref[...] = jnp.zeros((8, CONV_W), F32)
        dvec_ref[0:1, :] = dcb
        dvec_ref[1:2, :] = dlw
        dvec_ref[2:3, :] = dlb

        def chunk(i, carry):
            r0 = pl.multiple_of(i * CONV_CH, CONV_CH)
            _shifted_windows(dp_ref, r0, sh_ref)
            for c in range(CONV_W // 128):
                lanes = slice(c * 128, (c + 1) * 128)

                def sub(k, carry2):
                    b0 = pl.multiple_of(k * CONV_SUB, CONV_SUB)
                    acc = [None] * CONV_ACCS
                    for j in range(CONV_TAPS):
                        off = CONV_TAPS - 1 - j
                        term = sh_ref[off % 8, c, pl.ds(b0 + 8 * (off // 8), CONV_SUB), :] * cw_ref[c, j:j + 1, :]
                        acc[j % CONV_ACCS] = term if acc[j % CONV_ACCS] is None else acc[j % CONV_ACCS] + term
                    acc = functools.reduce(lambda a, b: a + b, acc)
                    rr = pl.ds(r0 + b0, CONV_SUB)
                    sg = _sigmoid(ug_ref[rr, lanes])
                    dua_ref[rr, lanes] = (acc * sg).astype(BF16)
                    dug_ref[rr, lanes] = (acc * ua_ref[rr, lanes] * sg * (1.0 - sg)).astype(BF16)
                    return carry2

                lax.fori_loop(0, CONV_CH // CONV_SUB, sub, 0)
            _shifted_windows(zp_ref, r0, sh_ref)
            for c in range(CONV_W // 128):
                lanes = slice(c * 128, (c + 1) * 128)

                def subw(k, carry2):
                    b0 = pl.multiple_of(k * CONV_SUB, CONV_SUB)
                    dcz = dp_ref[pl.ds(r0 + b0, CONV_SUB), lanes]
                    for j in range(CONV_TAPS):
                        off = j + CONV_PAD - (CONV_TAPS - 1)
                        pr = dcz * sh_ref[off % 8, c, pl.ds(b0 + 8 * (off // 8), CONV_SUB), :]
                        parts = [pr[8 * q:8 * (q + 1)] for q in range(CONV_SUB // 8)]
                        while len(parts) > 1:
                            parts = [a + b for a, b in zip(parts[0::2], parts[1::2])]
                        wacc_ref[8 * j:8 * (j + 1), lanes] += parts[0]
                    return carry2

                lax.fori_loop(0, CONV_CH // CONV_SUB, subw, 0)

            @pl.when(i % per_put == per_put - 1)
            def _():
                _put_all(((dua_ref, dua_hbm), (dug_ref, dug_hbm)), (osem0, osem1), i // per_put)

            return carry

        lax.fori_loop(0, t // CONV_CH, chunk, 0)
        _put_wait(((dua_ref, dua_hbm), (dug_ref, dug_hbm), (dgb_ref, dgb_hbm)), (osem0, osem1, osem2), t // PUT_ROWS)
        dcw_ref[...] = jnp.zeros((16, 2 * CONV_W), F32)
        for j in range(CONV_TAPS):
            dcw_ref[j // 2:j // 2 + 1, CONV_W * (j % 2):CONV_W * (j % 2 + 1)] = jnp.sum(
                wacc_ref[8 * j:8 * (j + 1), :], axis=0, keepdims=True)

    vm = pl.BlockSpec(memory_space=pltpu.VMEM)
    hbm = pl.BlockSpec(memory_space=pl.ANY)
    return pl.pallas_call(
        body,
        name="conv_bwd",
        in_specs=[hbm] * 5 + [vm] * 3,
        out_specs=[hbm] * 3 + [vm] * 2,
        out_shape=[jax.ShapeDtypeStruct((t, CONV_W), BF16)] * 3
        + [jax.ShapeDtypeStruct((16, 2 * CONV_W), F32), jax.ShapeDtypeStruct((8, CONV_W), F32)],
        scratch_shapes=[pltpu.VMEM((t + CONV_PAD, CONV_W), F32), pltpu.VMEM((t + CONV_PAD, CONV_W), F32),
                        pltpu.VMEM((8, CONV_W // 128, CONV_CH + CONV_PAD, 128), F32), pltpu.VMEM((8 * 32, CONV_W), F32)]
        + [pltpu.VMEM((t, CONV_W), F32)] * 4 + [pltpu.VMEM((t, CONV_W), BF16)] * 4
        + [pltpu.SemaphoreType.DMA((5,))] + [pltpu.SemaphoreType.DMA((t // PUT_ROWS,))] * 3,
        compiler_params=_cparams(),
    )(ua, ug, gb, cz, dmix, cw, lw, lb)


def _out_proj(mix_a, mix_b, x, tgt, gate, w_out):
    t = x.shape[0]
    tm = 512
    nstep = t // tm

    def body(ma_ref, mb_ref, x_ref, t_ref, g_ref, w_ref, dout_ref, dma_ref, dmb_ref, gw_ref, red_ref, acc_ref):
        i = pl.program_id(0)

        @pl.when(i == 0)
        def _():
            acc_ref[...] = jnp.zeros_like(acc_ref)
            red_ref[...] = jnp.zeros_like(red_ref)

        mix = jnp.concatenate([ma_ref[...], mb_ref[...]], axis=1)
        y = jnp.dot(mix, w_ref[...], preferred_element_type=F32)
        gate_v = g_ref[...]
        err = x_ref[...] + gate_v * y - t_ref[...]
        dout = err * (1.0 / D_MODEL)
        dout_ref[...] = dout
        red_ref[0:1, :] += jnp.sum(dout * y, axis=0, keepdims=True)
        red_ref[1:2, :] += jnp.sum(err * err, axis=0, keepdims=True)
        dy = (dout * gate_v).astype(BF16)
        dmix = lax.dot_general(dy, w_ref[...], (((1,), (1,)), ((), ())), preferred_element_type=F32)
        dma_ref[...] = dmix[:, 0:512].astype(BF16)
        dmb_ref[...] = dmix[:, 512:1024].astype(BF16)
        acc_ref[...] += lax.dot_general(mix, dy, (((0,), (0,)), ((), ())), preferred_element_type=F32)

        @pl.when(i == nstep - 1)
        def _():
            gw_ref[...] = acc_ref[...].astype(BF16)

    row = lambda w: pl.BlockSpec((tm, w), lambda i: (i, 0))
    const = lambda s: pl.BlockSpec(s, lambda i: (0, 0))
    return pl.pallas_call(
        body,
        name="out_proj",
        grid=(nstep,),
        in_specs=[row(512), row(512), row(D_MODEL), row(D_MODEL), const((1, D_MODEL)),
                  pl.BlockSpec((D_MODEL, D_MODEL), lambda i: (0, 0), pipeline_mode=pl.Buffered(1))],
        out_specs=[row(D_MODEL), row(512), row(512), const((D_MODEL, D_MODEL)), const((8, D_MODEL))],
        out_shape=[jax.ShapeDtypeStruct((t, D_MODEL), F32), jax.ShapeDtypeStruct((t, 512), BF16),
                   jax.ShapeDtypeStruct((t, 512), BF16), jax.ShapeDtypeStruct((D_MODEL, D_MODEL), BF16),
                   jax.ShapeDtypeStruct((8, D_MODEL), F32)],
        scratch_shapes=[pltpu.VMEM((D_MODEL, D_MODEL), F32)],
        compiler_params=_cparams(dimension_semantics=("arbitrary",)),
    )(mix_a, mix_b, x, tgt, gate, w_out)


DPROJ_WIDTHS = (512, 256, 512, 512, 512, 512)
DPROJ_STARTS = (0, 512, 768, 1280, 1792, 2304)
WIN_W = 768
WIN_START = (0, 640, 1408, 2048)
WIN_OFF = (0, 64, 0, 64)
N_GW = N_CHIPS


def _window_pieces(s):
    lo, hi = WIN_START[s], WIN_START[s] + WIN_W
    out = []
    for p, (st, w) in enumerate(zip(DPROJ_STARTS, DPROJ_WIDTHS)):
        a, b = max(lo, st), min(hi, st + w)
        if a < b:
            out.append((p, a - st, b - a, a - lo))
    return out


def _in_proj_bwd(dparts, h, x, dout, s1, nw, wt_full, dcw, dvec, sm_a, row0):
    t = x.shape[0]
    tm = 256
    nstep = N_GW + t // tm
    n_sem = 20
    rows0 = 32
    hs = rows0 // 2
    npart = len(DPROJ_WIDTHS)

    def body(*refs):
        d_hbm, d_ref = refs[:npart], refs[npart:2 * npart]
        (x_ref, dout_ref, s1_ref, nw_ref, h_ref, wt_hbm, dcw_ref, dvec_ref, sma_ref, row0_ref,
         gx_ref, gw_hbm, ssum_ref, rows_ref,
         stg_ref, wt_ref, gt_ref, sib_ref, out_ref, in_ref, res_ref, sall_ref, red_ref, sm0_ref, ssib_ref, schip_ref, sres_ref,
         wsem, lsem, ssem, rsem) = refs[2 * npart:]
        i = pl.program_id(0)
        x_, y_, c, chips = _place()
        j = 2 * x_ + y_
        dev = 2 * j + c
        sib = (x_, y_, 1 - c)
        rc = functools.partial(_remote, ssem, rsem)
        rel_chip = [2 * cx + cy for cx, cy in chips] + [j]
        peers = [(px, py, pc) for px in (x_, 1 - x_) for py in (y_, 1 - y_) for pc in (c, 1 - c)][1:]
        wt_copy = pltpu.make_async_copy(wt_hbm, wt_ref, lsem.at[0])

        def window(case, slot):
            return [pltpu.make_async_copy(d_hbm[p].at[:, pl.ds(c0, w)], stg_ref.at[slot, :, pl.ds(w0, w)], wsem.at[slot, n])
                    for n, (p, c0, w, w0) in enumerate(_window_pieces(case))]

        def to_sibling(k):
            return rc(k, gt_ref.at[k, 1 - c], sib_ref.at[k], sib)

        def to_chip(k):
            return rc(4 + k, out_ref.at[k], in_ref.at[k], (*chips[k], c))

        def trade(k):
            to_sibling(k).wait_recv()

            def add(n, carry):
                rr = pl.ds(pl.multiple_of(n * RS_CH, RS_CH), RS_CH)
                out_ref[k, rr, :] = (gt_ref[k, c, rr, :].astype(F32) + sib_ref[k, rr, :].astype(F32)).astype(BF16)
                return carry

            lax.fori_loop(0, IN_HALF // RS_CH, add, 0)
            to_chip(k).start()

        def keep(k, first, vals):
            for half in range(2):
                lo, hi = max(first, IN_HALF * half), min(first + vals.shape[0], IN_HALF * (half + 1))
                if lo < hi:
                    gt_ref[k, half, lo - IN_HALF * half:hi - IN_HALF * half, :] = vals[lo - first:hi - first].astype(BF16)

        mine_s = pl.ds(pl.multiple_of(c * hs, 8), hs)
        other_s = pl.ds(pl.multiple_of((1 - c) * hs, 8), hs)

        def small_to_sibling():
            return rc(15, sm0_ref.at[other_s], ssib_ref, sib)

        def small_to_chip(k):
            return rc(16 + k, schip_ref.at[j], schip_ref.at[j], (*chips[k], c))

        def small_share():
            return rc(19, sres_ref.at[c], sres_ref.at[c], sib)

        for k in range(N_GW):
            @pl.when(i == k)
            def _(k=k):
                slot = k % 2
                if k == 0:
                    red_ref[...] = jnp.zeros_like(red_ref)
                    wt_copy.start()
                    sm0_ref[...] = jnp.zeros_like(sm0_ref)
                    sm0_ref[0:16, :] = dcw_ref[...]
                    sm0_ref[16:17, 0:CONV_W] = dvec_ref[0:1, :]
                    sm0_ref[16:17, CONV_W:2 * CONV_W] = dvec_ref[1:2, :]
                    sm0_ref[17:18, 0:CONV_W] = dvec_ref[2:3, :]
                    for r in range(3):
                        sm0_ref[17:18, CONV_W + 128 * r:CONV_W + 128 * (r + 1)] = sma_ref[r:r + 1, :]
                    sm0_ref[18:19, :] = row0_ref[1:2, :]
                    small_to_sibling().start()
                if k == 1:
                    small_to_sibling().wait_recv()
                    schip_ref[j] = sm0_ref[mine_s, :] + ssib_ref[...]
                    for kk in range(3):
                        small_to_chip(kk).start()
                if k == N_GW - 1:
                    for kk in range(3):
                        jk = rel_chip[kk]
                        rc(16 + kk, schip_ref.at[jk], schip_ref.at[jk], sib).wait_recv()
                    tot = schip_ref[0]
                    for d in range(1, N_CHIPS):
                        tot = tot + schip_ref[d]
                    sres_ref[c] = tot
                    small_share().start()
                for case in range(N_CHIPS):
                    if k == 0:
                        @pl.when(rel_chip[0] == case)
                        def _():
                            for n, cp in enumerate(window(case, 0)):
                                cp.start(priority=n % 2)
                    if k + 1 < N_GW:
                        @pl.when(rel_chip[k + 1] == case)
                        def _():
                            for cp in window(case, 1 - slot):
                                cp.start()
                for case in range(N_CHIPS):
                    @pl.when(rel_chip[k] == case)
                    def _():
                        for cp in window(case, slot):
                            cp.wait()
                for part in range(2):
                    cols = pl.ds(part * (WIN_W // 2), WIN_W // 2)
                    g = lax.dot_general(stg_ref[slot, :, cols], h_ref[...], (((0,), (0,)), ((), ())),
                                        preferred_element_type=F32)
                    for off in sorted(set(WIN_OFF)):
                        @pl.when(rel_chip[k] % 2 == (1 if off else 0))
                        def _():
                            keep(k, part * (WIN_W // 2) - off, g)
                    if part == 0 and k >= 1:
                        trade(k - 1)
                to_sibling(k).start()

        @pl.when(i == N_GW)
        def _():
            wt_copy.wait()

        @pl.when(i >= N_GW)
        def _():
            xv = x_ref[...]
            r = lax.rsqrt(jnp.mean(xv * xv, axis=-1, keepdims=True) + EPS)
            xh = xv * r
            n = xh * nw_ref[...]
            dproj = jnp.concatenate([ref[...] for ref in d_ref], axis=1)
            dh = jnp.dot(dproj, wt_ref[...], preferred_element_type=F32)
            red_ref[0:1, :] += jnp.sum(dh, axis=0, keepdims=True)
            red_ref[1:2, :] += jnp.sum(dh * n, axis=0, keepdims=True)
            dn = dh * s1_ref[...]
            red_ref[2:3, :] += jnp.sum(dn * xh, axis=0, keepdims=True)
            dxh = dn * nw_ref[...]
            gx_ref[...] = dout_ref[...] + r * (dxh - xh * jnp.mean(dxh * xh, axis=-1, keepdims=True))

        @pl.when(i == nstep - 1)
        def _():
            sall_ref[dev] = row0_ref[...]
            sall_ref[dev, 2:5, :] = red_ref[0:3, :]
            sends = [rc(8 + k, sall_ref.at[dev], sall_ref.at[dev], peer) for k, peer in enumerate(peers)]
            for cp in sends:
                cp.start()
            sends += [to_sibling(k) for k in range(N_GW)] + [to_chip(k) for k in range(3)]
            sends += [small_to_sibling(), small_share()] + [small_to_chip(k) for k in range(3)]
            own = N_GW - 1
            to_sibling(own).wait_recv()
            for k in range(3):
                to_chip(k).wait_recv()

            def total(n, carry):
                rr = pl.ds(pl.multiple_of(n * RS_CH, RS_CH), RS_CH)
                acc = gt_ref[own, c, rr, :].astype(F32) + sib_ref[own, rr, :].astype(F32)
                for k in range(3):
                    acc = acc + in_ref[k, rr, :].astype(F32)
                res_ref[c, rr, :] = acc
                return carry

            lax.fori_loop(0, IN_HALF // RS_CH, total, 0)
            share = rc(7, res_ref.at[c], res_ref.at[c], sib)
            share.start()
            sends.append(share)
            back = [pltpu.make_async_copy(res_ref.at[half], gw_hbm.at[half], lsem.at[1 + half]) for half in range(2)]
            for half in range(2):
                @pl.when(c == half)
                def _():
                    back[half].start()
            for k, (px, py, pc) in enumerate(peers):
                pdev = 4 * px + 2 * py + pc
                rc(8 + k, sall_ref.at[pdev], sall_ref.at[pdev], (px, py, pc)).wait_recv()
            rows_ref[...] = sall_ref[...]
            rc(19, sres_ref.at[1 - c], sres_ref.at[1 - c], sib).wait_recv()
            ssum_ref[0:hs, :] = sres_ref[0]
            ssum_ref[hs:rows0, :] = sres_ref[1]
            rc(7, res_ref.at[1 - c], res_ref.at[1 - c], sib).wait_recv()
            for half in range(2):
                @pl.when(c != half)
                def _():
                    back[half].start()
            for cp in sends:
                cp.wait_send()
            for cp in back:
                cp.wait()

    blk = lambda i: jnp.maximum(i - N_GW, 0)
    row = lambda w: pl.BlockSpec((tm, w), lambda i: (blk(i), 0))
    vec = pl.BlockSpec((1, D_MODEL), lambda i: (0, 0))
    const = lambda shape: pl.BlockSpec(shape, lambda i: (0,) * len(shape))
    hbm = pl.BlockSpec(memory_space=pl.ANY)
    return pl.pallas_call(
        body,
        name="in_proj_bwd",
        grid=(nstep,),
        in_specs=[hbm] * npart + [row(w) for w in DPROJ_WIDTHS] + [row(D_MODEL), row(D_MODEL), vec, vec,
                  pl.BlockSpec((t, D_MODEL), lambda i: (0, 0), pipeline_mode=pl.Buffered(1)), hbm, const((16, D_MODEL)),
                  const((8, CONV_W)), const((8, 128)), const((8, D_MODEL))],
        out_specs=[row(D_MODEL), hbm, const((rows0, D_MODEL)), const((N_DEV, 8, D_MODEL))],
        out_shape=[jax.ShapeDtypeStruct((t, D_MODEL), F32), jax.ShapeDtypeStruct((2, IN_HALF, D_MODEL), F32),
                   jax.ShapeDtypeStruct((rows0, D_MODEL), F32), jax.ShapeDtypeStruct((N_DEV, 8, D_MODEL), F32)],
        scratch_shapes=[pltpu.VMEM((2, t, WIN_W), BF16), pltpu.VMEM((IN_W, D_MODEL), BF16),
                        pltpu.VMEM((N_CHIPS, 2, IN_HALF, D_MODEL), BF16), pltpu.VMEM((N_CHIPS, IN_HALF, D_MODEL), BF16),
                        pltpu.VMEM((3, IN_HALF, D_MODEL), BF16), pltpu.VMEM((3, IN_HALF, D_MODEL), BF16),
                        pltpu.VMEM((2, IN_HALF, D_MODEL), F32), pltpu.VMEM((N_DEV, 8, D_MODEL), F32),
                        pltpu.VMEM((8, D_MODEL), F32), pltpu.VMEM((rows0, D_MODEL), F32), pltpu.VMEM((hs, D_MODEL), F32),
                        pltpu.VMEM((N_CHIPS, hs, D_MODEL), F32),
                        pltpu.VMEM((2, hs, D_MODEL), F32), pltpu.SemaphoreType.DMA((2, 3)), pltpu.SemaphoreType.DMA((3,)),
                        pltpu.SemaphoreType.DMA((n_sem,)), pltpu.SemaphoreType.DMA((n_sem,))],
        compiler_params=_cparams(dimension_semantics=("arbitrary",)),
    )(*dparts, *dparts, x, dout, s1, nw, h, wt_full, dcw, dvec, sm_a, row0)


MESH = pl.DeviceIdType.MESH


def _place():
    x, y, c = lax.axis_index("x"), lax.axis_index("y"), lax.axis_index("c")
    chips = [(1 - x, y), (x, 1 - y), (1 - x, 1 - y)]
    return x, y, c, chips


def _remote(sems_s, sems_r, k, src, dst, to):
    return pltpu.make_async_remote_copy(src_ref=src, dst_ref=dst, send_sem=sems_s.at[k], recv_sem=sems_r.at[k],
                                        device_id=to, device_id_type=MESH)


RS_CH = 32
RS_SEMS = 5


def _rs_to_sibling(rc, s0, theirs, sib_ref, sib):
    cp = rc(s0, theirs, sib_ref, sib)
    cp.start()
    return cp


def _rs_trade(rc, s0, theirs, mine, sib_ref, out_ref, in_ref, rows, c, sib, chips):
    rc(s0, theirs, sib_ref, sib).wait_recv()
    cps = []
    for k, (cx, cy) in enumerate(chips):
        jk = 2 * cx + cy

        def add(i, carry, jk=jk, k=k):
            rr = pl.ds(pl.multiple_of(i * RS_CH, RS_CH), RS_CH)
            out_ref[k, rr, :] = (mine[jk, rr, :].astype(F32) + sib_ref[jk, rr, :].astype(F32)).astype(BF16)
            return carry

        lax.fori_loop(0, rows // RS_CH, add, 0)
        cps.append(rc(s0 + 1 + k, out_ref.at[k], in_ref.at[k], (cx, cy, c)))
        cps[-1].start()
    return cps


def _rs_total(rc, s0, mine, sib_ref, out_ref, in_ref, res_ref, rows, j, c, sib):
    for k in range(3):
        rc(s0 + 1 + k, out_ref.at[k], in_ref.at[k], sib).wait_recv()

    def total(i, carry):
        rr = pl.ds(pl.multiple_of(i * RS_CH, RS_CH), RS_CH)
        acc = mine[j, rr, :].astype(F32) + sib_ref[j, rr, :].astype(F32)
        for k in range(3):
            acc = acc + in_ref[k, rr, :].astype(F32)
        res_ref[c, rr, :] = acc
        return carry

    lax.fori_loop(0, rows // RS_CH, total, 0)
    cp = rc(s0 + 4, res_ref.at[c], res_ref.at[c], sib)
    cp.start()
    return cp


def _rs_done(rc, s0, res_ref, c, sib):
    rc(s0 + 4, res_ref.at[1 - c], res_ref.at[1 - c], sib).wait_recv()


def _rs_scratch(rows):
    return [pltpu.VMEM((N_CHIPS, rows, D_MODEL), BF16), pltpu.VMEM((3, rows, D_MODEL), BF16),
            pltpu.VMEM((3, rows, D_MODEL), BF16)]


MAIN_W = 640
MAIN_DST = (((0, 0, 512), (1, 0, 128)), ((2, 0, 512), (3, 0, 128)), ((3, 128, 384), (4, 0, 256)), ((4, 384, 128), (5, 0, 512)))
PAIR_DST = ((1, 128, 128), (4, 256, 128))


def _in_proj_gather(x, wt, c_row, w_ada, b_ada, nw):
    t = x.shape[0]
    ch = 512
    n_sem = 16

    def body(x_hbm, wt_ref, c_ref, wada_ref, bada_ref, nw_ref,
             q_hbm, kv_hbm, ga_hbm, ua_hbm, ug_hbm, gb_hbm, h_hbm, w4_hbm, call_ref, s1_ref, gate_ref, ada_ref,
             x_ref, h_ref, w4_ref, stg_ref, pstg_ref, part_ref, lsem, osem, wsem, ssem, rsem):
        outs = (q_hbm, kv_hbm, ga_hbm, ua_hbm, ug_hbm, gb_hbm)
        x_, y_, c, chips = _place()
        j = 2 * x_ + y_
        dev = 2 * j + c
        sib = (x_, y_, 1 - c)
        idx = [2 * cx + cy for cx, cy in chips]
        rc = functools.partial(_remote, ssem, rsem)
        x_copy = pltpu.make_async_copy(x_hbm, x_ref, lsem.at[0])
        x_copy.start()

        def rows_of(s, cc):
            return pl.ds(pl.multiple_of(2 * IN_HALF * s + IN_HALF * cc, 16), IN_HALF)

        w4_ref[rows_of(j, 0), :] = wt_ref[0].astype(BF16)
        w4_ref[rows_of(j, 1), :] = wt_ref[1].astype(BF16)
        call_ref[dev] = c_ref[...]
        sends = []
        peers = [(px, py, pc) for px in (x_, 1 - x_) for py in (y_, 1 - y_) for pc in (c, 1 - c)][1:]
        for k, peer in enumerate(peers):
            sends.append(rc(k, call_ref.at[dev], call_ref.at[dev], peer))
        for cp in sends:
            cp.start()

        for k, (px, py, pc) in enumerate(peers):
            pdev = 4 * px + 2 * py + pc
            rc(k, call_ref.at[pdev], call_ref.at[pdev], (px, py, pc)).wait_recv()
        rowid = lax.broadcasted_iota(jnp.int32, (N_DEV, D_MODEL), 0)
        call = jnp.zeros((N_DEV, D_MODEL), F32)
        for r in range(N_DEV):
            call = jnp.where(rowid == r, jnp.broadcast_to(call_ref[r], (N_DEV, D_MODEL)), call)
        bsh = bada_ref[:, 0:ADA_SHARD]
        for k in range(1, N_CHIPS):
            bsh = jnp.where(j == k, bada_ref[:, ADA_SHARD * k:ADA_SHARD * (k + 1)], bsh)
        part = jnp.dot(_silu(call).astype(BF16), wada_ref[...].astype(BF16), preferred_element_type=F32) + bsh
        for r in range(N_DEV):
            part_ref[r] = part[r:r + 1, :]
        ada_ref[j] = part_ref[dev]
        for k, chip in enumerate(chips):
            sends.append(rc(13 + k, part_ref.at[2 * idx[k] + c], ada_ref.at[j], (*chip, c)))
            sends[-1].start()
        for k, chip in enumerate(chips):
            sends.append(rc(7 + k, w4_ref.at[rows_of(j, c)], w4_ref.at[rows_of(j, c)], (*chip, c)))
            sends[-1].start()

        x_copy.wait()

        def prenorm(i, carry):
            rr = pl.ds(pl.multiple_of(i * ch, ch), ch)
            xv = x_ref[rr, :]
            r = lax.rsqrt(jnp.mean(xv * xv, axis=-1, keepdims=True) + EPS)
            x_ref[rr, :] = (xv * r) * nw_ref[...]
            return carry

        lax.fori_loop(0, t // ch, prenorm, 0)
        for k in range(3):
            rc(13 + k, ada_ref.at[idx[k]], ada_ref.at[idx[k]], sib).wait_recv()

        shift = jnp.concatenate([ada_ref[0], ada_ref[1][:, 0:256]], axis=1)
        s1 = 1.0 + jnp.concatenate([ada_ref[1][:, 256:768], ada_ref[2][:, 0:512]], axis=1)
        s1_ref[...] = s1
        gate_ref[...] = jnp.concatenate([ada_ref[2][:, 512:768], ada_ref[3]], axis=1)

        def norm(i, carry):
            rr = pl.ds(pl.multiple_of(i * ch, ch), ch)
            h_ref[rr, :] = (x_ref[rr, :] * s1 + shift).astype(BF16)
            return carry

        lax.fori_loop(0, t // ch, norm, 0)
        h_copy = pltpu.make_async_copy(h_ref, h_hbm, lsem.at[1])
        h_copy.start()

        def put_main(case, slot):
            cps, col = [], 0
            for n, (a, c0, w) in enumerate(MAIN_DST[case]):
                cps.append(pltpu.make_async_copy(stg_ref.at[slot, :, pl.ds(col, w)], outs[a].at[:, pl.ds(c0, w)], osem.at[slot, n]))
                col += w
            return cps

        def put_pair(case, slot):
            a, c0, w = PAIR_DST[case]
            return pltpu.make_async_copy(pstg_ref.at[slot], outs[a].at[:, pl.ds(c0, w)], osem.at[slot, 2])

        def project(first_row, width, dst, slot):
            wrows = pl.ds(pl.multiple_of(first_row, 128), width)

            def blk(i, carry):
                rr = pl.ds(pl.multiple_of(i * ch, ch), ch)
                dst[slot, rr, :] = lax.dot_general(h_ref[rr, :], w4_ref[wrows, :], (((1,), (1,)), ((), ())),
                                                   preferred_element_type=F32)
                return carry

            lax.fori_loop(0, t // ch, blk, 0)

        def phase(p, s, pair):
            slot = p % 2
            if p >= 2:
                for case in range(N_CHIPS):
                    @pl.when(order[p - 2] == case)
                    def _():
                        for cp in put_main(case, slot):
                            cp.wait()
            if p == 3:
                for case in range(2):
                    @pl.when(j // 2 == case)
                    def _():
                        put_pair(case, 0).wait()
            project(2 * IN_HALF * s + 64 * (s % 2), MAIN_W, stg_ref, slot)
            for case in range(N_CHIPS):
                @pl.when(s == case)
                def _():
                    for n, cp in enumerate(put_main(case, slot)):
                        cp.start(priority=n % 2)
            if pair is not None:
                project(MAIN_W + 2 * (2 * IN_HALF) * pair, 128, pstg_ref, slot % 2 if p == 2 else 1)
                for case in range(2):
                    @pl.when(pair == case)
                    def _():
                        put_pair(case, 0 if p == 2 else 1).start()

        order = [j] + idx
        w_out = [pltpu.make_async_copy(w4_ref.at[pl.ds(pl.multiple_of(2 * IN_HALF * s, 32), 2 * IN_HALF)],
                                       w4_hbm.at[pl.ds(pl.multiple_of(2 * IN_HALF * s, 32), 2 * IN_HALF)], wsem.at[p])
                 for p, s in enumerate(order)]
        w_out[0].start()
        phase(0, j, None)
        passed = []
        for k in range(3):
            jk = idx[k]
            rc(7 + k, w4_ref.at[rows_of(jk, c)], w4_ref.at[rows_of(jk, c)], sib).wait_recv()
            passed.append(rc(10 + k, w4_ref.at[rows_of(jk, c)], w4_ref.at[rows_of(jk, c)], sib))
            passed[-1].start()
            rc(10 + k, w4_ref.at[rows_of(jk, 1 - c)], w4_ref.at[rows_of(jk, 1 - c)], sib).wait_recv()
            w_out[1 + k].start()
            if k == 0:
                phase(1, jk, None)
            elif k == 1:
                phase(2, jk, j // 2)
            else:
                phase(3, jk, 1 - j // 2)

        for case in range(N_CHIPS):
            for p in (2, 3):
                @pl.when(order[p] == case)
                def _():
                    for cp in put_main(case, p % 2):
                        cp.wait()
        for case in range(2):
            @pl.when(1 - j // 2 == case)
            def _():
                put_pair(case, 1).wait()
        h_copy.wait()
        for cp in w_out:
            cp.wait()
        for cp in sends + passed:
            cp.wait_send()

    vm = pl.BlockSpec(memory_space=pltpu.VMEM)
    hbm = pl.BlockSpec(memory_space=pl.ANY)
    widths = (512, 256, 512, 512, 512, 512)
    return pl.pallas_call(
        body,
        name="in_proj",
        in_specs=[hbm, vm, vm, vm, vm, vm],
        out_specs=[hbm] * 8 + [vm, vm, vm],
        out_shape=[jax.ShapeDtypeStruct((t, w), F32) for w in widths]
        + [jax.ShapeDtypeStruct((t, D_MODEL), BF16), jax.ShapeDtypeStruct((IN_W, D_MODEL), BF16),
           jax.ShapeDtypeStruct((N_DEV, 1, D_MODEL), F32), jax.ShapeDtypeStruct((1, D_MODEL), F32),
           jax.ShapeDtypeStruct((1, D_MODEL), F32)],
        scratch_shapes=[pltpu.VMEM((N_CHIPS, 1, ADA_SHARD), F32), pltpu.VMEM((t, D_MODEL), F32), pltpu.VMEM((t, D_MODEL), BF16), pltpu.VMEM((IN_W, D_MODEL), BF16),
                        pltpu.VMEM((2, t, MAIN_W), F32), pltpu.VMEM((2, t, 128), F32), pltpu.VMEM((N_DEV, 1, ADA_SHARD), F32),
                        pltpu.SemaphoreType.DMA((2,)), pltpu.SemaphoreType.DMA((2, 3)), pltpu.SemaphoreType.DMA((N_CHIPS,)),
                        pltpu.SemaphoreType.DMA((n_sem,)), pltpu.SemaphoreType.DMA((n_sem,))],
        compiler_params=_cparams(),
    )(x, wt, c_row, w_ada, b_ada, nw)


def _adamw_math(w, g, m, v):
    m2 = ADAM_B1 * m + (1.0 - ADAM_B1) * g
    v2 = ADAM_B2 * v + (1.0 - ADAM_B2) * (g * g)
    m_hat = m2 / (1.0 - ADAM_B1 ** ADAM_STEP)
    v_hat = v2 / (1.0 - ADAM_B2 ** ADAM_STEP)
    delta = -ADAM_LR * (m_hat / (jnp.sqrt(v_hat) + ADAM_EPS) + ADAM_WD * w)
    return delta, m2, v2


def _adamw(name, groups, nstep, through):
    flat = [a for grp in groups for a in grp]
    n = len(flat)

    def body(*refs):
        ins, through_in, outs, through_out = refs[:n], refs[n], refs[n + 1:2 * n + 1], refs[2 * n + 1]
        for k in range(0, n, 4):
            w_ref, g_ref, m_ref, v_ref = ins[k:k + 4]
            g2_ref, d_ref, m2_ref, v2_ref = outs[k:k + 4]
            g = g_ref[...]
            g2_ref[...] = g
            d_ref[...], m2_ref[...], v2_ref[...] = _adamw_math(w_ref[...], g, m_ref[...], v_ref[...])
        through_out[...] = through_in[...]

    specs = [pl.BlockSpec((a.shape[0] // nstep, a.shape[1]), lambda i: (i, 0)) for a in flat + [through]]
    out = pl.pallas_call(
        body,
        name=name,
        grid=(nstep,),
        in_specs=specs,
        out_specs=specs,
        out_shape=[jax.ShapeDtypeStruct(a.shape, a.dtype) for a in flat + [through]],
        compiler_params=_cparams(dimension_semantics=("arbitrary",)),
    )(*flat, through)
    return [out[k:k + 4] for k in range(0, n, 4)], out[n]


def _adamw_ada(w, m, v, call, rows):
    r, cdim = w.shape
    tm = 256

    def body(w_ref, m_ref, v_ref, c_ref, rows_ref, g_ref, d_ref, m2_ref, v2_ref):
        j = 2 * lax.axis_index("x") + lax.axis_index("y")
        d_ada = jnp.concatenate([jnp.concatenate([rows_ref[d, row:row + 1, :] for d in range(N_DEV)], axis=0)
                                 for row in (2, 3, 0)], axis=1)
        dcols = d_ada[:, 0:cdim]
        for k in range(1, N_CHIPS):
            dcols = jnp.where(j == k, d_ada[:, cdim * k:cdim * (k + 1)], dcols)
        cvec = jnp.concatenate([c_ref[d] for d in range(N_DEV)], axis=0)
        g = lax.dot_general(_silu(cvec).astype(BF16), dcols.astype(BF16), (((0,), (0,)), ((), ())),
                            preferred_element_type=F32)
        g_ref[...] = g
        d_ref[...], m2_ref[...], v2_ref[...] = _adamw_math(w_ref[...], g, m_ref[...], v_ref[...])

    blk = pl.BlockSpec((tm, cdim), lambda i: (i, 0))
    return pl.pallas_call(
        body,
        name="adamw_w_ada",
        grid=(r // tm,),
        in_specs=[blk] * 3 + [pl.BlockSpec((N_DEV, 1, tm), lambda i: (0, 0, i)),
                              pl.BlockSpec((N_DEV, 8, D_MODEL), lambda i: (0, 0, 0))],
        out_specs=[blk] * 4,
        out_shape=[jax.ShapeDtypeStruct((r, cdim), F32)] * 4,
        compiler_params=_cparams(dimension_semantics=("arbitrary",)),
    )(w, m, v, call, rows)


def _adamw_small(ws, ms, vs, ssum, rows):
    n = len(ws)

    def body(*refs):
        w_r, m_r, v_r = refs[0:n], refs[n:2 * n], refs[2 * n:3 * n]
        ss_ref, rows_ref = refs[3 * n], refs[3 * n + 1]
        g_r, d_r, m2_r, v2_r = (refs[3 * n + 2 + k * n:3 * n + 2 + (k + 1) * n] for k in range(4))
        loss_ref = refs[7 * n + 2]
        j = 2 * lax.axis_index("x") + lax.axis_index("y")
        rsum = rows_ref[0]
        for d in range(1, N_DEV):
            rsum = rsum + rows_ref[d]
        taps = []
        for t in range(CONV_TAPS):
            row = ss_ref[t // 2:t // 2 + 1, :]
            c0 = CONV_W * (t % 2)
            pick = row[:, c0:c0 + 128]
            for k in range(1, N_CHIPS):
                pick = jnp.where(j == k, row[:, c0 + 128 * k:c0 + 128 * (k + 1)], pick)
            taps.append(pick)
        grads = [jnp.concatenate([rsum[2:3], rsum[3:4], rsum[0:1]], axis=1), rsum[4:5],
                 ss_ref[17:18, 512:512 + HEAD_DIM], ss_ref[17:18, 640:640 + HEAD_DIM], ss_ref[17:18, 768:776],
                 None, ss_ref[16:17, 0:CONV_W], ss_ref[16:17, CONV_W:2 * CONV_W], ss_ref[17:18, 0:CONV_W]]
        for i in range(n):
            if grads[i] is None:
                g = jnp.concatenate(taps, axis=0)
                w, m, v = (jnp.concatenate([ref[t] for t in range(CONV_TAPS)], axis=0) for ref in (w_r[i], m_r[i], v_r[i]))
                res = (g,) + _adamw_math(w, g, m, v)
                for ref, val in zip((g_r[i], d_r[i], m2_r[i], v2_r[i]), res):
                    for t in range(CONV_TAPS):
                        ref[t] = val[t:t + 1, :]
                continue
            g = grads[i]
            g_r[i][...] = g
            d_r[i][...], m2_r[i][...], v2_r[i][...] = _adamw_math(w_r[i][...], g, m_r[i][...], v_r[i][...])
        loss_ref[...] = (0.5 / D_MODEL) * jnp.sum(ss_ref[18:19, :], axis=1, keepdims=True)

    vm = pl.BlockSpec(memory_space=pltpu.VMEM)
    shapes = [jax.ShapeDtypeStruct(w.shape, F32) for w in ws]
    out = pl.pallas_call(
        body,
        name="adamw_small",
        in_specs=[vm] * (3 * n + 2),
        out_specs=[vm] * (4 * n + 1),
        out_shape=shapes * 4 + [jax.ShapeDtypeStruct((1, 1), F32)],
        compiler_params=_cparams(),
    )(*ws, *ms, *vs, ssum, rows)
    return out[0:n], out[n:2 * n], out[2 * n:3 * n], out[3 * n:4 * n], out[4 * n]


def _rope_tables(t):
    inv = ROPE_THETA ** (-jnp.arange(0, HEAD_DIM, 2, dtype=F32) / HEAD_DIM)
    ang = jnp.arange(t, dtype=F32)[:, None] * inv[None, :]
    cos, sin = jnp.cos(ang), jnp.sin(ang)
    return jnp.tile(cos, (1, 4)), jnp.tile(jnp.concatenate([-sin, sin], axis=1), (1, 2))


def kernel(x, c, w_ada, b_ada, norm_w, w_in, q_norm_w, k_norm_w, sinks, conv_w, conv_b, ln_w, ln_b, w_out, loss_target, m_w_ada, m_b_ada, m_norm_w, m_w_in, m_q_norm_w, m_k_norm_w, m_sinks, m_conv_w, m_conv_b, m_ln_w, m_ln_b, m_w_out, v_w_ada, v_b_ada, v_norm_w, v_w_in, v_q_norm_w, v_k_norm_w, v_sinks, v_conv_w, v_conv_b, v_ln_w, v_ln_b, v_w_out):
    xi, yi = lax.axis_index("x"), lax.axis_index("y")
    j = 2 * xi + yi
    x2, tgt = x[0], loss_target[0]
    t = x2.shape[0]

    wt_s, mt_s, vt_s = w_in[0].T, m_w_in[0].T, v_w_in[0].T
    by_tap = lambda a: jnp.transpose(a, (1, 0, 2))

    q_raw, kv_raw, ga, ua, ug, gb, h, w_full, call, s1, gate = _in_proj_gather(
        x2, wt_s.reshape(2, IN_HALF, D_MODEL), c, w_ada[0], b_ada, norm_w)

    cos_f, sin_s = _rope_tables(t)
    qkw2 = jnp.tile(jnp.concatenate([q_norm_w, k_norm_w], axis=0), (1, 2))

    o, mix_a, wo4, cw4 = _attn_fwd(q_raw, kv_raw, ga, qkw2, sinks, cos_f, sin_s,
                                   w_out[0].reshape(2, OUT_HALF, D_MODEL), by_tap(conv_w))
    w_out_full = wo4.reshape(D_MODEL, D_MODEL)
    cz, mix_b = _conv_fwd(ua, ug, gb, cw4, conv_b, ln_w, ln_b)
    dout, dmix_a, dmix_b, gwo_bf, red_o = _out_proj(mix_a, mix_b, x2, tgt, gate, w_out_full)

    dq, dkv, dga, sm_a, gwo = _attn_bwd(q_raw, kv_raw, ga, o, dmix_a, qkw2, sinks, cos_f, sin_s,
                                        gwo_bf.reshape(N_CHIPS, 2, OUT_HALF, D_MODEL))
    dua, dug, dgb, dcw, dvec = _conv_bwd(ua, ug, gb, cz, dmix_b, cw4, ln_w, ln_b)
    dparts = (dq, dkv, dga, dua, dug, dgb)

    grad_x, gw, ssum, rows = _in_proj_bwd(dparts, h, x2, dout, s1, norm_w, w_full, dcw, dvec, sm_a, red_o)

    gt_w_in = gw.reshape(2 * IN_HALF, D_MODEL)
    g_w_out = gwo.reshape(D_MODEL // N_CHIPS, D_MODEL)

    g_w_ada, d_w_ada, nm_w_ada, nv_w_ada = _adamw_ada(w_ada[0], m_w_ada[0], v_w_ada[0], call, rows)
    (in_res, out_res), grad_x = _adamw("adamw_w", [(wt_s, gt_w_in, mt_s, vt_s), (w_out[0], g_w_out, m_w_out[0], v_w_out[0])],
                                        4, grad_x)
    gt_w_in, dt_w_in, nmt_w_in, nvt_w_in = in_res
    g_w_out, d_w_out, nm_w_out, nv_w_out = out_res
    g_w_in, d_w_in, nm_w_in, nv_w_in = gt_w_in.T, dt_w_in.T, nmt_w_in.T, nvt_w_in.T
    ws = [b_ada, norm_w, q_norm_w, k_norm_w, sinks, by_tap(conv_w), conv_b, ln_w, ln_b]
    ms = [m_b_ada, m_norm_w, m_q_norm_w, m_k_norm_w, m_sinks, by_tap(m_conv_w), m_conv_b, m_ln_w, m_ln_b]
    vs = [v_b_ada, v_norm_w, v_q_norm_w, v_k_norm_w, v_sinks, by_tap(v_conv_w), v_conv_b, v_ln_w, v_ln_b]
    gs, ds, nms, nvs, loss11 = _adamw_small(ws, ms, vs, ssum, rows)
    loss = loss11[0, 0]

    def order(ada_v, in_v, out_v, sm):
        b, nw_, qw_, kw_, sk_, cw_, cb_, lw_, lb_ = sm
        return [ada_v[None], b, nw_, in_v[None], qw_, kw_, sk_, by_tap(cw_), cb_, lw_, lb_, out_v[None]]

    grads = order(g_w_ada, g_w_in, g_w_out, gs)
    deltas = order(d_w_ada, d_w_in, d_w_out, ds)
    new_m = order(nm_w_ada, nm_w_in, nm_w_out, nms)
    new_v = order(nv_w_ada, nv_w_in, nv_w_out, nvs)
    return (loss, grad_x[None], *grads, *deltas, *new_m, *new_v)
```

```python
import functools

import jax
import jax.numpy as jnp
from jax import lax
from jax.experimental import pallas as pl
from jax.experimental.pallas import tpu as pltpu

F32 = jnp.float32
BF16 = jnp.bfloat16

D_MODEL = 1024
ATTN_W = 512
KV_W = 128
CONV_W = 512
IN_W = 2816
HEAD_DIM = 64
CONV_TAPS = 31
QBLK = 128
EPS = 1e-6
ROPE_THETA = 10000.0

ADAM_LR = 0.001
ADAM_B1 = 0.9
ADAM_B2 = 0.999
ADAM_EPS = 1e-08
ADAM_WD = 0.01
ADAM_STEP = 10

N_CHIPS = 4
N_DEV = 8
IN_HALF = IN_W // N_CHIPS // 2
OUT_HALF = D_MODEL // N_CHIPS // 2
ADA_SHARD = 3 * D_MODEL // N_CHIPS

VMEM_LIMIT = 56 * 1024 * 1024
CONV_PAD = 32


def _cparams(**kw):
    return pltpu.CompilerParams(vmem_limit_bytes=VMEM_LIMIT, **kw)


def _sigmoid(v):
    return 1.0 / (1.0 + jnp.exp(-v))


def _silu(v):
    return v * _sigmoid(v)


def _dsilu(v):
    s = _sigmoid(v)
    return s * (1.0 + v * (1.0 - s))


def _lane(shape):
    return lax.broadcasted_iota(jnp.int32, shape, len(shape) - 1)


PUT_ROWS = 512


def _fetch(hbm_refs, vmem_refs, sem):
    cps = [pltpu.make_async_copy(h, v, sem.at[i]) for i, (h, v) in enumerate(zip(hbm_refs, vmem_refs))]
    for cp in cps:
        cp.start()
    return cps


def _put(vmem_ref, hbm_ref, sem, m):
    r = pl.ds(pl.multiple_of(m * PUT_ROWS, PUT_ROWS), PUT_ROWS)
    return pltpu.make_async_copy(vmem_ref.at[r], hbm_ref.at[r], sem.at[m])


def _put_all(pairs, sems, m):
    for (v, h), sem in zip(pairs, sems):
        _put(v, h, sem, m).start()


def _put_wait(pairs, sems, n):
    for (v, h), sem in zip(pairs, sems):
        for m in range(n):
            _put(v, h, sem, m).wait()


def _head_mean(s, left):
    sl = jnp.sum(jnp.where(left, s, 0.0), axis=-1, keepdims=True)
    sr = jnp.sum(jnp.where(left, 0.0, s), axis=-1, keepdims=True)
    return jnp.where(left, sl, sr) * (1.0 / HEAD_DIM)


def _rot(v, first):
    return jnp.where(first, pltpu.roll(v, 96, 1), pltpu.roll(v, 32, 1))


def _norm_rope(v, w, cos, sin_s, left, first):
    r = lax.rsqrt(_head_mean(v * v, left) + EPS)
    xh = v * r
    n = xh * w
    return n * cos + _rot(n, first) * sin_s, xh, r


def _norm_rope_bwd(d, xh, r, w, cos, sin_s, left, first):
    dn = d * cos - _rot(d, first) * sin_s
    dw = jnp.sum(dn * xh, axis=0, keepdims=True)
    dxh = dn * w
    return r * (dxh - xh * _head_mean(dxh * xh, left)), dw


def _dup_heads(v, left):
    sw = pltpu.roll(v, 64, 1)
    return jnp.where(left, v, sw), jnp.where(left, sw, v)


def _prep_kv(kv_ref, kw_ref, cos_ref, sin_ref, ka_ref, va_ref, t):
    ch = 256
    for g in range(2):
        ka_ref[g, 0:QBLK, :] = jnp.zeros((QBLK, 128), BF16)
        va_ref[g, 0:QBLK, :] = jnp.zeros((QBLK, 128), BF16)

    def chunk(i, carry):
        r0 = pl.multiple_of(i * ch, ch)
        left = _lane((ch, 128)) < 64
        first = (_lane((ch, 128)) % 64) < 32
        k = kv_ref[pl.ds(r0, ch), 0:128]
        v = kv_ref[pl.ds(r0, ch), 128:256]
        kr, _, _ = _norm_rope(k, kw_ref[...], cos_ref[pl.ds(r0, ch), :], sin_ref[pl.ds(r0, ch), :], left, first)
        k0, k1 = _dup_heads(kr, left)
        v0, v1 = _dup_heads(v, left)
        ka_ref[0, pl.ds(QBLK + r0, ch), :] = k0.astype(BF16)
        ka_ref[1, pl.ds(QBLK + r0, ch), :] = k1.astype(BF16)
        va_ref[0, pl.ds(QBLK + r0, ch), :] = v0.astype(BF16)
        va_ref[1, pl.ds(QBLK + r0, ch), :] = v1.astype(BF16)
        return carry

    lax.fori_loop(0, t // ch, chunk, 0)


def _band_mask(n):
    qi = lax.broadcasted_iota(jnp.int32, (2 * QBLK, 2 * QBLK), 0) % QBLK
    kj = lax.broadcasted_iota(jnp.int32, (2 * QBLK, 2 * QBLK), 1)
    local = (kj > qi) & (kj <= qi + QBLK)
    return local & ((n > 0) | (kj >= QBLK))


def _softmax_pair(s, mask, sink0, sink1):
    row = lax.broadcasted_iota(jnp.int32, (2 * QBLK, 1), 0)
    sink = jnp.where(row < QBLK, sink0, sink1)
    s = jnp.where(mask, s, -jnp.inf)
    m = jnp.maximum(jnp.max(s, axis=-1, keepdims=True), sink)
    e = jnp.exp(s - m)
    es = jnp.exp(sink - m)
    inv = 1.0 / (jnp.sum(e, axis=-1, keepdims=True) + es)
    return e * inv, es * inv


def _stack_heads(v, left):
    return jnp.concatenate([jnp.where(left, v, 0.0), jnp.where(left, 0.0, v)], axis=0)


def _attn_fwd(q_raw, kv_raw, ga, qkw2, sinks, cos_f, sin_s, wo, cw):
    t = q_raw.shape[0]
    nblk = t // QBLK
    per_put = PUT_ROWS // QBLK

    def body(q_hbm, kv_ref, ga_hbm, qkw_ref, sk_ref, cos_hbm, sin_hbm, wo_ref, cw_ref,
             o_hbm, mix_hbm, wo4_ref, cw4_ref, ka_ref, va_ref, q_ref, ga_ref, o_ref, mix_ref, cos_ref, sin_ref,
             isem, osem0, osem1, ssem, rsem):
        qw_ref, kw_ref = qkw_ref.at[0:1], qkw_ref.at[1:2]
        loads = _fetch((cos_hbm, sin_hbm, q_hbm, ga_hbm), (cos_ref, sin_ref, q_ref, ga_ref), isem)
        outs, osems = ((o_ref, o_hbm), (mix_ref, mix_hbm)), (osem0, osem1)
        x, y, c, chips = _place()
        j = 2 * x + y
        sib = (x, y, 1 - c)
        idx = [2 * cx + cy for cx, cy in chips]
        rc = functools.partial(_remote, ssem, rsem)
        wo4_ref[j] = wo_ref[...].astype(BF16)
        for tap in range(CONV_TAPS):
            cw4_ref[j, tap:tap + 1, :] = cw_ref[tap]
        cw4_ref[j, CONV_TAPS:, :] = jnp.zeros((CONV_PAD - CONV_TAPS, 128), F32)
        sends = []
        for k, chip in enumerate(chips):
            sends.append(rc(k, wo4_ref.at[j, c], wo4_ref.at[j, c], (*chip, c)))
            sends.append(rc(6 + k, cw4_ref.at[j], cw4_ref.at[j], (*chip, c)))
        for cp in sends:
            cp.start()

        loads[0].wait()
        loads[1].wait()
        _prep_kv(kv_ref, kw_ref, cos_ref, sin_ref, ka_ref, va_ref, t)
        loads[2].wait()
        loads[3].wait()

        def blk(n, carry):
            r0 = pl.multiple_of(n * QBLK, QBLK)
            left = _lane((QBLK, 128)) < 64
            first = (_lane((QBLK, 128)) % 64) < 32
            cos = cos_ref[pl.ds(r0, QBLK), :]
            sin = sin_ref[pl.ds(r0, QBLK), :]
            mask = _band_mask(n)
            scores = []
            for p in range(4):
                lanes = slice(p * 128, (p + 1) * 128)
                qr, _, _ = _norm_rope(q_ref[pl.ds(r0, QBLK), lanes], qw_ref[...], cos, sin, left, first)
                q2 = _stack_heads(qr * 0.125, left).astype(BF16)
                scores.append(lax.dot_general(q2, ka_ref[p // 2, pl.ds(r0, 2 * QBLK), :], (((1,), (1,)), ((), ())),
                                              preferred_element_type=F32))
            probs = [_softmax_pair(scores[p], mask, sk_ref[0, 2 * p], sk_ref[0, 2 * p + 1])[0].astype(BF16)
                     for p in range(4)]
            for p in range(4):
                lanes = slice(p * 128, (p + 1) * 128)
                o2 = jnp.dot(probs[p], va_ref[p // 2, pl.ds(r0, 2 * QBLK), :], preferred_element_type=F32)
                o = jnp.where(left, o2[0:QBLK], o2[QBLK:2 * QBLK])
                o_ref[pl.ds(r0, QBLK), lanes] = o.astype(BF16)
                mix_ref[pl.ds(r0, QBLK), lanes] = (o * _silu(ga_ref[pl.ds(r0, QBLK), lanes])).astype(BF16)

            @pl.when(n % per_put == per_put - 1)
            def _():
                _put_all(outs, osems, n // per_put)

            return carry

        lax.fori_loop(0, nblk, blk, 0)
        _put_wait(outs, osems, t // PUT_ROWS)

        passed = []
        for k, chip in enumerate(chips):
            jk = idx[k]
            rc(k, wo4_ref.at[jk, c], wo4_ref.at[jk, c], sib).wait_recv()
            passed.append(rc(3 + k, wo4_ref.at[jk, c], wo4_ref.at[jk, c], sib))
            passed[-1].start()
        for k, chip in enumerate(chips):
            jk = idx[k]
            rc(3 + k, wo4_ref.at[jk, 1 - c], wo4_ref.at[jk, 1 - c], sib).wait_recv()
            rc(6 + k, cw4_ref.at[jk], cw4_ref.at[jk], sib).wait_recv()
        for cp in sends + passed:
            cp.wait_send()

    vm = pl.BlockSpec(memory_space=pltpu.VMEM)
    hbm = pl.BlockSpec(memory_space=pl.ANY)
    n_sem = 9
    return pl.pallas_call(
        body,
        name="attn_fwd",
        in_specs=[hbm, vm, hbm, vm, pl.BlockSpec(memory_space=pltpu.SMEM), hbm, hbm, vm, vm],
        out_specs=[hbm, hbm, vm, vm],
        out_shape=[jax.ShapeDtypeStruct((t, ATTN_W), BF16), jax.ShapeDtypeStruct((t, ATTN_W), BF16),
                   jax.ShapeDtypeStruct((N_CHIPS, 2, OUT_HALF, D_MODEL), BF16),
                   jax.ShapeDtypeStruct((N_CHIPS, 32, 128), F32)],
        scratch_shapes=[pltpu.VMEM((2, t + QBLK, 128), BF16), pltpu.VMEM((2, t + QBLK, 128), BF16),
                        pltpu.VMEM((t, ATTN_W), F32), pltpu.VMEM((t, ATTN_W), F32),
                        pltpu.VMEM((t, ATTN_W), BF16), pltpu.VMEM((t, ATTN_W), BF16),
                        pltpu.VMEM((t, 128), F32), pltpu.VMEM((t, 128), F32),
                        pltpu.SemaphoreType.DMA((4,)), pltpu.SemaphoreType.DMA((t // PUT_ROWS,)),
                        pltpu.SemaphoreType.DMA((t // PUT_ROWS,)),
                        pltpu.SemaphoreType.DMA((n_sem,)), pltpu.SemaphoreType.DMA((n_sem,))],
        compiler_params=_cparams(),
    )(q_raw, kv_raw, ga, qkw2, sinks, cos_f, sin_s, wo, cw)


def _attn_bwd(q_raw, kv_raw, ga, o, dmix, qkw2, sinks, cos_f, sin_s, go):
    t = q_raw.shape[0]
    nblk = t // QBLK
    per_put = PUT_ROWS // QBLK

    def body(q_hbm, kv_ref, ga_hbm, o_hbm, dm_hbm, qkw_ref, sk_ref, cos_hbm, sin_hbm, go_ref,
             dq_hbm, dkv_ref, dga_hbm, sm_ref, gwo_ref, ka_ref, va_ref, dka_ref, dva_ref,
             sibo_ref, outo_ref, ino_ref, q_ref, ga_ref, o_ref, dm_ref, dq_ref, dga_ref, cos_ref, sin_ref,
             isem, osem0, osem1, ssem, rsem):
        qw_ref, kw_ref = qkw_ref.at[0:1], qkw_ref.at[1:2]
        loads = _fetch((cos_hbm, sin_hbm, q_hbm, ga_hbm, o_hbm, dm_hbm), (cos_ref, sin_ref, q_ref, ga_ref, o_ref, dm_ref), isem)
        outs, osems = ((dq_ref, dq_hbm), (dga_ref, dga_hbm)), (osem0, osem1)
        x, y, c, chips = _place()
        sib = (x, y, 1 - c)
        rc = functools.partial(_remote, ssem, rsem)
        theirs, mine = go_ref.at[:, 1 - c], go_ref.at[:, c]
        sends = [_rs_to_sibling(rc, 0, theirs, sibo_ref, sib)]
        loads[0].wait()
        loads[1].wait()
        _prep_kv(kv_ref, kw_ref, cos_ref, sin_ref, ka_ref, va_ref, t)
        dka_ref[...] = jnp.zeros_like(dka_ref)
        dva_ref[...] = jnp.zeros_like(dva_ref)
        sends += _rs_trade(rc, 0, theirs, mine, sibo_ref, outo_ref, ino_ref, OUT_HALF, c, sib, chips)
        for cp in loads[2:]:
            cp.wait()

        def blk(n, carry):
            dqw, dsk = carry
            r0 = pl.multiple_of(n * QBLK, QBLK)
            left = _lane((QBLK, 128)) < 64
            first = (_lane((QBLK, 128)) % 64) < 32
            cos = cos_ref[pl.ds(r0, QBLK), :]
            sin = sin_ref[pl.ds(r0, QBLK), :]
            mask = _band_mask(n)
            row = lax.broadcasted_iota(jnp.int32, (2 * QBLK, 1), 0)
            rows = pl.ds(r0, QBLK)
            win = pl.ds(r0, 2 * QBLK)
            lane_of = [slice(p * 128, (p + 1) * 128) for p in range(4)]
            for grp in ((0, 1), (2, 3)):
                qn = {p: _norm_rope(q_ref[rows, lane_of[p]], qw_ref[...], cos, sin, left, first) for p in grp}
                q2 = {p: _stack_heads(qn[p][0] * 0.125, left).astype(BF16) for p in grp}
                sc = {p: lax.dot_general(q2[p], ka_ref[p // 2, win, :], (((1,), (1,)), ((), ())),
                                         preferred_element_type=F32) for p in grp}
                do2 = {}
                for p in grp:
                    gav = ga_ref[rows, lane_of[p]]
                    dmv = dm_ref[rows, lane_of[p]].astype(F32)
                    dga_ref[rows, lane_of[p]] = (dmv * o_ref[rows, lane_of[p]].astype(F32) * _dsilu(gav)).astype(BF16)
                    do2[p] = _stack_heads(dmv * _silu(gav), left).astype(BF16)
                dpm = {p: lax.dot_general(do2[p], va_ref[p // 2, win, :], (((1,), (1,)), ((), ())),
                                          preferred_element_type=F32) for p in grp}
                sm = {p: _softmax_pair(sc[p], mask, sk_ref[0, 2 * p], sk_ref[0, 2 * p + 1]) for p in grp}
                dsl = {}
                for p in grp:
                    pm, ps = sm[p]
                    delta = jnp.sum(pm * dpm[p], axis=-1, keepdims=True)
                    dsl[p] = (pm * (dpm[p] - delta)).astype(BF16)
                    pd = ps * delta
                    d0 = jnp.sum(jnp.where(row < QBLK, pd, 0.0), axis=0, keepdims=True)
                    d1 = jnp.sum(jnp.where(row < QBLK, 0.0, pd), axis=0, keepdims=True)
                    l8 = _lane((1, 128))
                    dsk = dsk - jnp.where(l8 == 2 * p, d0, 0.0) - jnp.where(l8 == 2 * p + 1, d1, 0.0)
                for p in grp:
                    g = p // 2
                    dva_ref[g, win, :] += lax.dot_general(sm[p][0].astype(BF16), do2[p], (((0,), (0,)), ((), ())),
                                                          preferred_element_type=F32)
                    dka_ref[g, win, :] += lax.dot_general(dsl[p], q2[p], (((0,), (0,)), ((), ())),
                                                          preferred_element_type=F32)
                for p in grp:
                    dq2 = jnp.dot(dsl[p], ka_ref[p // 2, win, :], preferred_element_type=F32)
                    dqr = jnp.where(left, dq2[0:QBLK], dq2[QBLK:2 * QBLK]) * 0.125
                    dq, dw = _norm_rope_bwd(dqr, qn[p][1], qn[p][2], qw_ref[...], cos, sin, left, first)
                    dq_ref[rows, lane_of[p]] = dq.astype(BF16)
                    dqw = dqw + dw

            @pl.when(n % per_put == per_put - 1)
            def _():
                _put_all(outs, osems, n // per_put)

            return dqw, dsk

        zero = jnp.zeros((1, 128), F32)
        dqw, dsk = lax.fori_loop(0, nblk, blk, (zero, zero))

        ch = 256

        def chunk(i, dkw):
            r0 = pl.multiple_of(i * ch, ch)
            left = _lane((ch, 128)) < 64
            first = (_lane((ch, 128)) % 64) < 32
            rows = pl.ds(r0, ch)
            prow = pl.ds(QBLK + r0, ch)

            def fold(ref):
                a0 = ref[0, prow, :]
                a1 = ref[1, prow, :]
                return jnp.where(left, a0 + pltpu.roll(a0, 64, 1), a1 + pltpu.roll(a1, 64, 1))

            cos = cos_ref[rows, :]
            sin = sin_ref[rows, :]
            _, xh, r = _norm_rope(kv_ref[rows, 0:128], kw_ref[...], cos, sin, left, first)
            dk, dw = _norm_rope_bwd(fold(dka_ref), xh, r, kw_ref[...], cos, sin, left, first)
            dkv_ref[rows, 0:128] = dk.astype(BF16)
            dkv_ref[rows, 128:256] = fold(dva_ref).astype(BF16)
            return dkw + dw

        dkw = lax.fori_loop(0, t // ch, chunk, zero)
        sm_ref[...] = jnp.zeros((8, 128), F32)
        sm_ref[0:1, :] = dqw + pltpu.roll(dqw, 64, 1)
        sm_ref[1:2, :] = dkw + pltpu.roll(dkw, 64, 1)
        sm_ref[2:3, :] = dsk

        j = 2 * x + y
        sends.append(_rs_total(rc, 0, mine, sibo_ref, outo_ref, ino_ref, gwo_ref, OUT_HALF, j, c, sib))
        _rs_done(rc, 0, gwo_ref, c, sib)
        for cp in sends:
            cp.wait_send()
        _put_wait(outs, osems, t // PUT_ROWS)

    vm = pl.BlockSpec(memory_space=pltpu.VMEM)
    hbm = pl.BlockSpec(memory_space=pl.ANY)
    return pl.pallas_call(
        body,
        name="attn_bwd",
        in_specs=[hbm, vm, hbm, hbm, hbm, vm, pl.BlockSpec(memory_space=pltpu.SMEM), hbm, hbm, vm],
        out_specs=[hbm, vm, hbm, vm, vm],
        out_shape=[jax.ShapeDtypeStruct((t, ATTN_W), BF16), jax.ShapeDtypeStruct((t, 2 * KV_W), BF16),
                   jax.ShapeDtypeStruct((t, ATTN_W), BF16), jax.ShapeDtypeStruct((8, 128), F32),
                   jax.ShapeDtypeStruct((2, OUT_HALF, D_MODEL), F32)],
        scratch_shapes=[pltpu.VMEM((2, t + QBLK, 128), BF16), pltpu.VMEM((2, t + QBLK, 128), BF16),
                        pltpu.VMEM((2, t + QBLK, 128), F32), pltpu.VMEM((2, t + QBLK, 128), F32)]
        + _rs_scratch(OUT_HALF)
        + [pltpu.VMEM((t, ATTN_W), F32), pltpu.VMEM((t, ATTN_W), F32), pltpu.VMEM((t, ATTN_W), BF16),
           pltpu.VMEM((t, ATTN_W), BF16), pltpu.VMEM((t, ATTN_W), BF16), pltpu.VMEM((t, ATTN_W), BF16),
           pltpu.VMEM((t, 128), F32), pltpu.VMEM((t, 128), F32),
           pltpu.SemaphoreType.DMA((6,)), pltpu.SemaphoreType.DMA((t // PUT_ROWS,)), pltpu.SemaphoreType.DMA((t // PUT_ROWS,)),
           pltpu.SemaphoreType.DMA((RS_SEMS,)), pltpu.SemaphoreType.DMA((RS_SEMS,))],
        compiler_params=_cparams(),
    )(q_raw, kv_raw, ga, o, dmix, qkw2, sinks, cos_f, sin_s, go)


CONV_CH = 256
CONV_SUB = 128
CONV_ACCS = 1


def _shifted_windows(src_ref, r0, sh_ref):
    rows = CONV_CH + CONV_PAD
    win = src_ref[pl.ds(r0, rows), :]
    for b in range(8):
        sh = win if b == 0 else pltpu.roll(win, rows - b, 0)
        for c in range(CONV_W // 128):
            sh_ref[b, c] = sh[:, c * 128:(c + 1) * 128]


def _conv_fwd(ua, ug, gb, cw, cb, lw, lb):
    t = ua.shape[0]

    def body(ua_hbm, ug_hbm, gb_hbm, cw_ref, cb_ref, lw_ref, lb_ref, cz_hbm, mix_hbm, zp_ref, sh_ref,
             ua_ref, ug_ref, gb_ref, cz_ref, mix_ref, isem, osem0, osem1):
        loads = _fetch((ua_hbm, ug_hbm, gb_hbm), (ua_ref, ug_ref, gb_ref), isem)
        outs, osems = ((cz_ref, cz_hbm), (mix_ref, mix_hbm)), (osem0, osem1)
        per_put = PUT_ROWS // CONV_CH
        zp_ref[0:CONV_PAD, :] = jnp.zeros((CONV_PAD, CONV_W), F32)
        loads[0].wait()
        loads[1].wait()

        def glu(i, carry):
            r0 = pl.multiple_of(i * CONV_CH, CONV_CH)
            rows = pl.ds(r0, CONV_CH)
            zp_ref[pl.ds(CONV_PAD + r0, CONV_CH), :] = ua_ref[rows, :] * _sigmoid(ug_ref[rows, :])
            return carry

        lax.fori_loop(0, t // CONV_CH, glu, 0)
        loads[2].wait()

        def chunk(i, carry):
            r0 = pl.multiple_of(i * CONV_CH, CONV_CH)
            _shifted_windows(zp_ref, r0, sh_ref)
            for c in range(CONV_W // 128):
                lanes = slice(c * 128, (c + 1) * 128)

                def sub(k, carry2):
                    b0 = pl.multiple_of(k * CONV_SUB, CONV_SUB)
                    acc = [jnp.broadcast_to(cb_ref[0:1, lanes], (CONV_SUB, 128))] + [None] * (CONV_ACCS - 1)
                    for j in range(CONV_TAPS):
                        off = j + CONV_PAD - (CONV_TAPS - 1)
                        term = sh_ref[off % 8, c, pl.ds(b0 + 8 * (off // 8), CONV_SUB), :] * cw_ref[c, j:j + 1, :]
                        acc[j % CONV_ACCS] = term if acc[j % CONV_ACCS] is None else acc[j % CONV_ACCS] + term
                    cz_ref[pl.ds(r0 + b0, CONV_SUB), lanes] = functools.reduce(lambda a, b: a + b, acc)
                    return carry2

                lax.fori_loop(0, CONV_CH // CONV_SUB, sub, 0)
            rows = pl.ds(r0, CONV_CH)
            cz = cz_ref[rows, :]
            mu = jnp.mean(cz, axis=-1, keepdims=True)
            xc = cz - mu
            rs = lax.rsqrt(jnp.mean(xc * xc, axis=-1, keepdims=True) + EPS)
            ln = xc * rs * lw_ref[...] + lb_ref[...]
            mix_ref[rows, :] = (_silu(ln) * _silu(gb_ref[rows, :])).astype(BF16)

            @pl.when(i % per_put == per_put - 1)
            def _():
                _put_all(outs, osems, i // per_put)

            return carry

        lax.fori_loop(0, t // CONV_CH, chunk, 0)
        _put_wait(outs, osems, t // PUT_ROWS)

    vm = pl.BlockSpec(memory_space=pltpu.VMEM)
    hbm = pl.BlockSpec(memory_space=pl.ANY)
    nput = t // PUT_ROWS
    return pl.pallas_call(
        body,
        name="conv_fwd",
        in_specs=[hbm] * 3 + [vm] * 4,
        out_specs=[hbm, hbm],
        out_shape=[jax.ShapeDtypeStruct((t, CONV_W), F32), jax.ShapeDtypeStruct((t, CONV_W), BF16)],
        scratch_shapes=[pltpu.VMEM((t + CONV_PAD, CONV_W), F32),
                        pltpu.VMEM((8, CONV_W // 128, CONV_CH + CONV_PAD, 128), F32),
                        pltpu.VMEM((t, CONV_W), F32), pltpu.VMEM((t, CONV_W), F32), pltpu.VMEM((t, CONV_W), F32),
                        pltpu.VMEM((t, CONV_W), F32), pltpu.VMEM((t, CONV_W), BF16),
                        pltpu.SemaphoreType.DMA((3,)), pltpu.SemaphoreType.DMA((nput,)), pltpu.SemaphoreType.DMA((nput,))],
        compiler_params=_cparams(),
    )(ua, ug, gb, cw, cb, lw, lb)


def _conv_bwd(ua, ug, gb, cz, dmix, cw, lw, lb):
    t = ua.shape[0]

    def body(ua_hbm, ug_hbm, gb_hbm, cz_hbm, dm_hbm, cw_ref, lw_ref, lb_ref,
             dua_hbm, dug_hbm, dgb_hbm, dcw_ref, dvec_ref, zp_ref, dp_ref, sh_ref, wacc_ref,
             ua_ref, ug_ref, gb_ref, cz_ref, dm_ref, dua_ref, dug_ref, dgb_ref, isem, osem0, osem1, osem2):
        loads = _fetch((ua_hbm, ug_hbm, gb_hbm, cz_hbm, dm_hbm), (ua_ref, ug_ref, gb_ref, cz_ref, dm_ref), isem)
        per_put = PUT_ROWS // CONV_CH
        zp_ref[0:CONV_PAD, :] = jnp.zeros((CONV_PAD, CONV_W), F32)
        dp_ref[t:t + CONV_PAD, :] = jnp.zeros((CONV_PAD, CONV_W), F32)
        wacc_ref[...] = jnp.zeros_like(wacc_ref)
        for cp in loads:
            cp.wait()

        def pointwise(i, carry):
            dcb, dlw, dlb = carry
            r0 = pl.multiple_of(i * CONV_CH, CONV_CH)
            rows = pl.ds(r0, CONV_CH)
            zp_ref[pl.ds(CONV_PAD + r0, CONV_CH), :] = ua_ref[rows, :] * _sigmoid(ug_ref[rows, :])
            cz = cz_ref[rows, :]
            mu = jnp.mean(cz, axis=-1, keepdims=True)
            xc = cz - mu
            rs = lax.rsqrt(jnp.mean(xc * xc, axis=-1, keepdims=True) + EPS)
            xh = xc * rs
            ln = xh * lw_ref[...] + lb_ref[...]
            gbv = gb_ref[rows, :]
            dy = dm_ref[rows, :].astype(F32)
            dgb_ref[rows, :] = (dy * _silu(ln) * _dsilu(gbv)).astype(BF16)
            dl = dy * _silu(gbv) * _dsilu(ln)
            dxh = dl * lw_ref[...]
            dcz = rs * (dxh - jnp.mean(dxh, axis=-1, keepdims=True)
                        - xh * jnp.mean(dxh * xh, axis=-1, keepdims=True))
            dp_ref[rows, :] = dcz

            @pl.when(i % per_put == per_put - 1)
            def _():
                _put(dgb_ref, dgb_hbm, osem2, i // per_put).start()

            return (dcb + jnp.sum(dcz, axis=0, keepdims=True),
                    dlw + jnp.sum(dl * xh, axis=0, keepdims=True),
                    dlb + jnp.sum(dl, axis=0, keepdims=True))

        zero = jnp.zeros((1, CONV_W), F32)
        dcb, dlw, dlb = lax.fori_loop(0, t // CONV_CH, pointwise, (zero, zero, zero))
        dvec_ref[...] = jnp.zeros((8, CONV_W), F32)
        dvec_ref[0:1, :] = dcb
        dvec_ref[1:2, :] = dlw
        dvec_ref[2:3, :] = dlb

        def chunk(i, carry):
            r0 = pl.multiple_of(i * CONV_CH, CONV_CH)
            _shifted_windows(dp_ref, r0, sh_ref)
            for c in range(CONV_W // 128):
                lanes = slice(c * 128, (c + 1) * 128)

                def sub(k, carry2):
                    b0 = pl.multiple_of(k * CONV_SUB, CONV_SUB)
                    acc = [None] * CONV_ACCS
                    for j in range(CONV_TAPS):
                        off = CONV_TAPS - 1 - j
                        term = sh_ref[off % 8, c, pl.ds(b0 + 8 * (off // 8), CONV_SUB), :] * cw_ref[c, j:j + 1, :]
                        acc[j % CONV_ACCS] = term if acc[j % CONV_ACCS] is None else acc[j % CONV_ACCS] + term
                    acc = functools.reduce(lambda a, b: a + b, acc)
                    rr = pl.ds(r0 + b0, CONV_SUB)
                    sg = _sigmoid(ug_ref[rr, lanes])
                    dua_ref[rr, lanes] = (acc * sg).astype(BF16)
                    dug_ref[rr, lanes] = (acc * ua_ref[rr, lanes] * sg * (1.0 - sg)).astype(BF16)
                    return carry2

                lax.fori_loop(0, CONV_CH // CONV_SUB, sub, 0)
            _shifted_windows(zp_ref, r0, sh_ref)
            for c in range(CONV_W // 128):
                lanes = slice(c * 128, (c + 1) * 128)

                def subw(k, carry2):
                    b0 = pl.multiple_of(k * CONV_SUB, CONV_SUB)
                    dcz = dp_ref[pl.ds(r0 + b0, CONV_SUB), lanes]
                    for j in range(CONV_TAPS):
                        off = j + CONV_PAD - (CONV_TAPS - 1)
                        pr = dcz * sh_ref[off % 8, c, pl.ds(b0 + 8 * (off // 8), CONV_SUB), :]
                        parts = [pr[8 * q:8 * (q + 1)] for q in range(CONV_SUB // 8)]
                        while len(parts) > 1:
                            parts = [a + b for a, b in zip(parts[0::2], parts[1::2])]
                        wacc_ref[8 * j:8 * (j + 1), lanes] += parts[0]
                    return carry2

                lax.fori_loop(0, CONV_CH // CONV_SUB, subw, 0)

            @pl.when(i % per_put == per_put - 1)
            def _():
                _put_all(((dua_ref, dua_hbm), (dug_ref, dug_hbm)), (osem0, osem1), i // per_put)

            return carry

        lax.fori_loop(0, t // CONV_CH, chunk, 0)
        _put_wait(((dua_ref, dua_hbm), (dug_ref, dug_hbm), (dgb_ref, dgb_hbm)), (osem0, osem1, osem2), t // PUT_ROWS)
        dcw_ref[...] = jnp.zeros((16, 2 * CONV_W), F32)
        for j in range(CONV_TAPS):
            dcw_ref[j // 2:j // 2 + 1, CONV_W * (j % 2):CONV_W * (j % 2 + 1)] = jnp.sum(
                wacc_ref[8 * j:8 * (j + 1), :], axis=0, keepdims=True)

    vm = pl.BlockSpec(memory_space=pltpu.VMEM)
    hbm = pl.BlockSpec(memory_space=pl.ANY)
    return pl.pallas_call(
        body,
        name="conv_bwd",
        in_specs=[hbm] * 5 + [vm] * 3,
        out_specs=[hbm] * 3 + [vm] * 2,
        out_shape=[jax.ShapeDtypeStruct((t, CONV_W), BF16)] * 3
        + [jax.ShapeDtypeStruct((16, 2 * CONV_W), F32), jax.ShapeDtypeStruct((8, CONV_W), F32)],
        scratch_shapes=[pltpu.VMEM((t + CONV_PAD, CONV_W), F32), pltpu.VMEM((t + CONV_PAD, CONV_W), F32),
                        pltpu.VMEM((8, CONV_W // 128, CONV_CH + CONV_PAD, 128), F32), pltpu.VMEM((8 * 32, CONV_W), F32)]
        + [pltpu.VMEM((t, CONV_W), F32)] * 4 + [pltpu.VMEM((t, CONV_W), BF16)] * 4
        + [pltpu.SemaphoreType.DMA((5,))] + [pltpu.SemaphoreType.DMA((t // PUT_ROWS,))] * 3,
        compiler_params=_cparams(),
    )(ua, ug, gb, cz, dmix, cw, lw, lb)


def _out_proj(mix_a, mix_b, x, tgt, gate, w_out):
    t = x.shape[0]
    tm = 512
    nstep = t // tm

    def body(ma_ref, mb_ref, x_ref, t_ref, g_ref, w_ref, dout_ref, dma_ref, dmb_ref, gw_ref, red_ref, acc_ref):
        i = pl.program_id(0)

        @pl.when(i == 0)
        def _():
            acc_ref[...] = jnp.zeros_like(acc_ref)
            red_ref[...] = jnp.zeros_like(red_ref)

        mix = jnp.concatenate([ma_ref[...], mb_ref[...]], axis=1)
        y = jnp.dot(mix, w_ref[...], preferred_element_type=F32)
        gate_v = g_ref[...]
        err = x_ref[...] + gate_v * y - t_ref[...]
        dout = err * (1.0 / D_MODEL)
        dout_ref[...] = dout
        red_ref[0:1, :] += jnp.sum(dout * y, axis=0, keepdims=True)
        red_ref[1:2, :] += jnp.sum(err * err, axis=0, keepdims=True)
        dy = (dout * gate_v).astype(BF16)
        dmix = lax.dot_general(dy, w_ref[...], (((1,), (1,)), ((), ())), preferred_element_type=F32)
        dma_ref[...] = dmix[:, 0:512].astype(BF16)
        dmb_ref[...] = dmix[:, 512:1024].astype(BF16)
        acc_ref[...] += lax.dot_general(mix, dy, (((0,), (0,)), ((), ())), preferred_element_type=F32)

        @pl.when(i == nstep - 1)
        def _():
            gw_ref[...] = acc_ref[...].astype(BF16)

    row = lambda w: pl.BlockSpec((tm, w), lambda i: (i, 0))
    const = lambda s: pl.BlockSpec(s, lambda i: (0, 0))
    return pl.pallas_call(
        body,
        name="out_proj",
        grid=(nstep,),
        in_specs=[row(512), row(512), row(D_MODEL), row(D_MODEL), const((1, D_MODEL)),
                  pl.BlockSpec((D_MODEL, D_MODEL), lambda i: (0, 0), pipeline_mode=pl.Buffered(1))],
        out_specs=[row(D_MODEL), row(512), row(512), const((D_MODEL, D_MODEL)), const((8, D_MODEL))],
        out_shape=[jax.ShapeDtypeStruct((t, D_MODEL), F32), jax.ShapeDtypeStruct((t, 512), BF16),
                   jax.ShapeDtypeStruct((t, 512), BF16), jax.ShapeDtypeStruct((D_MODEL, D_MODEL), BF16),
                   jax.ShapeDtypeStruct((8, D_MODEL), F32)],
        scratch_shapes=[pltpu.VMEM((D_MODEL, D_MODEL), F32)],
        compiler_params=_cparams(dimension_semantics=("arbitrary",)),
    )(mix_a, mix_b, x, tgt, gate, w_out)


DPROJ_WIDTHS = (512, 256, 512, 512, 512, 512)
DPROJ_STARTS = (0, 512, 768, 1280, 1792, 2304)
WIN_W = 768
WIN_START = (0, 640, 1408, 2048)
WIN_OFF = (0, 64, 0, 64)
N_GW = N_CHIPS


def _window_pieces(s):
    lo, hi = WIN_START[s], WIN_START[s] + WIN_W
    out = []
    for p, (st, w) in enumerate(zip(DPROJ_STARTS, DPROJ_WIDTHS)):
        a, b = max(lo, st), min(hi, st + w)
        if a < b:
            out.append((p, a - st, b - a, a - lo))
    return out


def _in_proj_bwd(dparts, h, x, dout, s1, nw, wt_full, dcw, dvec, sm_a, row0):
    t = x.shape[0]
    tm = 256
    nstep = N_GW + t // tm
    n_sem = 20
    rows0 = 32
    hs = rows0 // 2
    npart = len(DPROJ_WIDTHS)

    def body(*refs):
        d_hbm, d_ref = refs[:npart], refs[npart:2 * npart]
        (x_ref, dout_ref, s1_ref, nw_ref, h_ref, wt_hbm, dcw_ref, dvec_ref, sma_ref, row0_ref,
         gx_ref, gw_hbm, ssum_ref, rows_ref,
         stg_ref, wt_ref, gt_ref, sib_ref, out_ref, in_ref, res_ref, sall_ref, red_ref, sm0_ref, ssib_ref, schip_ref, sres_ref,
         wsem, lsem, ssem, rsem) = refs[2 * npart:]
        i = pl.program_id(0)
        x_, y_, c, chips = _place()
        j = 2 * x_ + y_
        dev = 2 * j + c
        sib = (x_, y_, 1 - c)
        rc = functools.partial(_remote, ssem, rsem)
        rel_chip = [2 * cx + cy for cx, cy in chips] + [j]
        peers = [(px, py, pc) for px in (x_, 1 - x_) for py in (y_, 1 - y_) for pc in (c, 1 - c)][1:]
        wt_copy = pltpu.make_async_copy(wt_hbm, wt_ref, lsem.at[0])

        def window(case, slot):
            return [pltpu.make_async_copy(d_hbm[p].at[:, pl.ds(c0, w)], stg_ref.at[slot, :, pl.ds(w0, w)], wsem.at[slot, n])
                    for n, (p, c0, w, w0) in enumerate(_window_pieces(case))]

        def to_sibling(k):
            return rc(k, gt_ref.at[k, 1 - c], sib_ref.at[k], sib)

        def to_chip(k):
            return rc(4 + k, out_ref.at[k], in_ref.at[k], (*chips[k], c))

        def trade(k):
            to_sibling(k).wait_recv()

            def add(n, carry):
                rr = pl.ds(pl.multiple_of(n * RS_CH, RS_CH), RS_CH)
                out_ref[k, rr, :] = (gt_ref[k, c, rr, :].astype(F32) + sib_ref[k, rr, :].astype(F32)).astype(BF16)
                return carry

            lax.fori_loop(0, IN_HALF // RS_CH, add, 0)
            to_chip(k).start()

        def keep(k, first, vals):
            for half in range(2):
                lo, hi = max(first, IN_HALF * half), min(first + vals.shape[0], IN_HALF * (half + 1))
                if lo < hi:
                    gt_ref[k, half, lo - IN_HALF * half:hi - IN_HALF * half, :] = vals[lo - first:hi - first].astype(BF16)

        mine_s = pl.ds(pl.multiple_of(c * hs, 8), hs)
        other_s = pl.ds(pl.multiple_of((1 - c) * hs, 8), hs)

        def small_to_sibling():
            return rc(15, sm0_ref.at[other_s], ssib_ref, sib)

        def small_to_chip(k):
            return rc(16 + k, schip_ref.at[j], schip_ref.at[j], (*chips[k], c))

        def small_share():
            return rc(19, sres_ref.at[c], sres_ref.at[c], sib)

        for k in range(N_GW):
            @pl.when(i == k)
            def _(k=k):
                slot = k % 2
                if k == 0:
                    red_ref[...] = jnp.zeros_like(red_ref)
                    wt_copy.start()
                    sm0_ref[...] = jnp.zeros_like(sm0_ref)
                    sm0_ref[0:16, :] = dcw_ref[...]
                    sm0_ref[16:17, 0:CONV_W] = dvec_ref[0:1, :]
                    sm0_ref[16:17, CONV_W:2 * CONV_W] = dvec_ref[1:2, :]
                    sm0_ref[17:18, 0:CONV_W] = dvec_ref[2:3, :]
                    for r in range(3):
                        sm0_ref[17:18, CONV_W + 128 * r:CONV_W + 128 * (r + 1)] = sma_ref[r:r + 1, :]
                    sm0_ref[18:19, :] = row0_ref[1:2, :]
                    small_to_sibling().start()
                if k == 1:
                    small_to_sibling().wait_recv()
                    schip_ref[j] = sm0_ref[mine_s, :] + ssib_ref[...]
                    for kk in range(3):
                        small_to_chip(kk).start()
                if k == N_GW - 1:
                    for kk in range(3):
                        jk = rel_chip[kk]
                        rc(16 + kk, schip_ref.at[jk], schip_ref.at[jk], sib).wait_recv()
                    tot = schip_ref[0]
                    for d in range(1, N_CHIPS):
                        tot = tot + schip_ref[d]
                    sres_ref[c] = tot
                    small_share().start()
                for case in range(N_CHIPS):
                    if k == 0:
                        @pl.when(rel_chip[0] == case)
                        def _():
                            for cp in window(case, 0):
                                cp.start()
                    if k + 1 < N_GW:
                        @pl.when(rel_chip[k + 1] == case)
                        def _():
                            for cp in window(case, 1 - slot):
                                cp.start()
                for case in range(N_CHIPS):
                    @pl.when(rel_chip[k] == case)
                    def _():
                        for cp in window(case, slot):
                            cp.wait()
                for part in range(2):
                    cols = pl.ds(part * (WIN_W // 2), WIN_W // 2)
                    g = lax.dot_general(stg_ref[slot, :, cols], h_ref[...], (((0,), (0,)), ((), ())),
                                        preferred_element_type=F32)
                    for off in sorted(set(WIN_OFF)):
                        @pl.when(rel_chip[k] % 2 == (1 if off else 0))
                        def _():
                            keep(k, part * (WIN_W // 2) - off, g)
                    if part == 0 and k >= 1:
                        trade(k - 1)
                to_sibling(k).start()

        @pl.when(i == N_GW)
        def _():
            wt_copy.wait()

        @pl.when(i >= N_GW)
        def _():
            xv = x_ref[...]
            r = lax.rsqrt(jnp.mean(xv * xv, axis=-1, keepdims=True) + EPS)
            xh = xv * r
            n = xh * nw_ref[...]
            dproj = jnp.concatenate([ref[...] for ref in d_ref], axis=1)
            dh = jnp.dot(dproj, wt_ref[...], preferred_element_type=F32)
            red_ref[0:1, :] += jnp.sum(dh, axis=0, keepdims=True)
            red_ref[1:2, :] += jnp.sum(dh * n, axis=0, keepdims=True)
            dn = dh * s1_ref[...]
            red_ref[2:3, :] += jnp.sum(dn * xh, axis=0, keepdims=True)
            dxh = dn * nw_ref[...]
            gx_ref[...] = dout_ref[...] + r * (dxh - xh * jnp.mean(dxh * xh, axis=-1, keepdims=True))

        @pl.when(i == nstep - 1)
        def _():
            sall_ref[dev] = row0_ref[...]
            sall_ref[dev, 2:5, :] = red_ref[0:3, :]
            sends = [rc(8 + k, sall_ref.at[dev], sall_ref.at[dev], peer) for k, peer in enumerate(peers)]
            for cp in sends:
                cp.start()
            sends += [to_sibling(k) for k in range(N_GW)] + [to_chip(k) for k in range(3)]
            sends += [small_to_sibling(), small_share()] + [small_to_chip(k) for k in range(3)]
            own = N_GW - 1
            to_sibling(own).wait_recv()
            for k in range(3):
                to_chip(k).wait_recv()

            def total(n, carry):
                rr = pl.ds(pl.multiple_of(n * RS_CH, RS_CH), RS_CH)
                acc = gt_ref[own, c, rr, :].astype(F32) + sib_ref[own, rr, :].astype(F32)
                for k in range(3):
                    acc = acc + in_ref[k, rr, :].astype(F32)
                res_ref[c, rr, :] = acc
                return carry

            lax.fori_loop(0, IN_HALF // RS_CH, total, 0)
            share = rc(7, res_ref.at[c], res_ref.at[c], sib)
            share.start()
            sends.append(share)
            back = [pltpu.make_async_copy(res_ref.at[half], gw_hbm.at[half], lsem.at[1 + half]) for half in range(2)]
            for half in range(2):
                @pl.when(c == half)
                def _():
                    back[half].start()
            for k, (px, py, pc) in enumerate(peers):
                pdev = 4 * px + 2 * py + pc
                rc(8 + k, sall_ref.at[pdev], sall_ref.at[pdev], (px, py, pc)).wait_recv()
            rows_ref[...] = sall_ref[...]
            rc(19, sres_ref.at[1 - c], sres_ref.at[1 - c], sib).wait_recv()
            ssum_ref[0:hs, :] = sres_ref[0]
            ssum_ref[hs:rows0, :] = sres_ref[1]
            rc(7, res_ref.at[1 - c], res_ref.at[1 - c], sib).wait_recv()
            for half in range(2):
                @pl.when(c != half)
                def _():
                    back[half].start()
            for cp in sends:
                cp.wait_send()
            for cp in back:
                cp.wait()

    blk = lambda i: jnp.maximum(i - N_GW, 0)
    row = lambda w: pl.BlockSpec((tm, w), lambda i: (blk(i), 0))
    vec = pl.BlockSpec((1, D_MODEL), lambda i: (0, 0))
    const = lambda shape: pl.BlockSpec(shape, lambda i: (0,) * len(shape))
    hbm = pl.BlockSpec(memory_space=pl.ANY)
    return pl.pallas_call(
        body,
        name="in_proj_bwd",
        grid=(nstep,),
        in_specs=[hbm] * npart + [row(w) for w in DPROJ_WIDTHS] + [row(D_MODEL), row(D_MODEL), vec, vec,
                  pl.BlockSpec((t, D_MODEL), lambda i: (0, 0), pipeline_mode=pl.Buffered(1)), hbm, const((16, D_MODEL)),
                  const((8, CONV_W)), const((8, 128)), const((8, D_MODEL))],
        out_specs=[row(D_MODEL), hbm, const((rows0, D_MODEL)), const((N_DEV, 8, D_MODEL))],
        out_shape=[jax.ShapeDtypeStruct((t, D_MODEL), F32), jax.ShapeDtypeStruct((2, IN_HALF, D_MODEL), F32),
                   jax.ShapeDtypeStruct((rows0, D_MODEL), F32), jax.ShapeDtypeStruct((N_DEV, 8, D_MODEL), F32)],
        scratch_shapes=[pltpu.VMEM((2, t, WIN_W), BF16), pltpu.VMEM((IN_W, D_MODEL), BF16),
                        pltpu.VMEM((N_CHIPS, 2, IN_HALF, D_MODEL), BF16), pltpu.VMEM((N_CHIPS, IN_HALF, D_MODEL), BF16),
                        pltpu.VMEM((3, IN_HALF, D_MODEL), BF16), pltpu.VMEM((3, IN_HALF, D_MODEL), BF16),
                        pltpu.VMEM((2, IN_HALF, D_MODEL), F32), pltpu.VMEM((N_DEV, 8, D_MODEL), F32),
                        pltpu.VMEM((8, D_MODEL), F32), pltpu.VMEM((rows0, D_MODEL), F32), pltpu.VMEM((hs, D_MODEL), F32),
                        pltpu.VMEM((N_CHIPS, hs, D_MODEL), F32),
                        pltpu.VMEM((2, hs, D_MODEL), F32), pltpu.SemaphoreType.DMA((2, 3)), pltpu.SemaphoreType.DMA((3,)),
                        pltpu.SemaphoreType.DMA((n_sem,)), pltpu.SemaphoreType.DMA((n_sem,))],
        compiler_params=_cparams(dimension_semantics=("arbitrary",)),
    )(*dparts, *dparts, x, dout, s1, nw, h, wt_full, dcw, dvec, sm_a, row0)


MESH = pl.DeviceIdType.MESH


def _place():
    x, y, c = lax.axis_index("x"), lax.axis_index("y"), lax.axis_index("c")
    chips = [(1 - x, y), (x, 1 - y), (1 - x, 1 - y)]
    return x, y, c, chips


def _remote(sems_s, sems_r, k, src, dst, to):
    return pltpu.make_async_remote_copy(src_ref=src, dst_ref=dst, send_sem=sems_s.at[k], recv_sem=sems_r.at[k],
                                        device_id=to, device_id_type=MESH)


RS_CH = 32
RS_SEMS = 5


def _rs_to_sibling(rc, s0, theirs, sib_ref, sib):
    cp = rc(s0, theirs, sib_ref, sib)
    cp.start()
    return cp


def _rs_trade(rc, s0, theirs, mine, sib_ref, out_ref, in_ref, rows, c, sib, chips):
    rc(s0, theirs, sib_ref, sib).wait_recv()
    cps = []
    for k, (cx, cy) in enumerate(chips):
        jk = 2 * cx + cy

        def add(i, carry, jk=jk, k=k):
            rr = pl.ds(pl.multiple_of(i * RS_CH, RS_CH), RS_CH)
            out_ref[k, rr, :] = (mine[jk, rr, :].astype(F32) + sib_ref[jk, rr, :].astype(F32)).astype(BF16)
            return carry

        lax.fori_loop(0, rows // RS_CH, add, 0)
        cps.append(rc(s0 + 1 + k, out_ref.at[k], in_ref.at[k], (cx, cy, c)))
        cps[-1].start()
    return cps


def _rs_total(rc, s0, mine, sib_ref, out_ref, in_ref, res_ref, rows, j, c, sib):
    for k in range(3):
        rc(s0 + 1 + k, out_ref.at[k], in_ref.at[k], sib).wait_recv()

    def total(i, carry):
        rr = pl.ds(pl.multiple_of(i * RS_CH, RS_CH), RS_CH)
        acc = mine[j, rr, :].astype(F32) + sib_ref[j, rr, :].astype(F32)
        for k in range(3):
            acc = acc + in_ref[k, rr, :].astype(F32)
        res_ref[c, rr, :] = acc
        return carry

    lax.fori_loop(0, rows // RS_CH, total, 0)
    cp = rc(s0 + 4, res_ref.at[c], res_ref.at[c], sib)
    cp.start()
    return cp


def _rs_done(rc, s0, res_ref, c, sib):
    rc(s0 + 4, res_ref.at[1 - c], res_ref.at[1 - c], sib).wait_recv()


def _rs_scratch(rows):
    return [pltpu.VMEM((N_CHIPS, rows, D_MODEL), BF16), pltpu.VMEM((3, rows, D_MODEL), BF16),
            pltpu.VMEM((3, rows, D_MODEL), BF16)]


MAIN_W = 640
MAIN_DST = (((0, 0, 512), (1, 0, 128)), ((2, 0, 512), (3, 0, 128)), ((3, 128, 384), (4, 0, 256)), ((4, 384, 128), (5, 0, 512)))
PAIR_DST = ((1, 128, 128), (4, 256, 128))


def _in_proj_gather(x, wt, c_row, w_ada, b_ada, nw):
    t = x.shape[0]
    ch = 512
    n_sem = 16

    def body(x_hbm, wt_hbm, c_ref, wada_hbm, bada_ref, nw_ref,
             q_hbm, kv_hbm, ga_hbm, ua_hbm, ug_hbm, gb_hbm, h_hbm, w4_hbm, call_ref, s1_ref, gate_ref, ada_ref,
             x_ref, h_ref, w4_ref, stg_ref, pstg_ref, part_ref, wt_ref, wada_ref, lsem, osem, wsem, ssem, rsem):
        outs = (q_hbm, kv_hbm, ga_hbm, ua_hbm, ug_hbm, gb_hbm)
        x_, y_, c, chips = _place()
        j = 2 * x_ + y_
        dev = 2 * j + c
        sib = (x_, y_, 1 - c)
        idx = [2 * cx + cy for cx, cy in chips]
        rc = functools.partial(_remote, ssem, rsem)
        wada_copy = pltpu.make_async_copy(wada_hbm, wada_ref, lsem.at[2])
        wt_copy = pltpu.make_async_copy(wt_hbm, wt_ref, lsem.at[3])
        x_copy = pltpu.make_async_copy(x_hbm, x_ref, lsem.at[0])
        wada_copy.start()
        wt_copy.start()
        x_copy.start()

        def rows_of(s, cc):
            return pl.ds(pl.multiple_of(2 * IN_HALF * s + IN_HALF * cc, 16), IN_HALF)

        call_ref[dev] = c_ref[...]
        sends = []
        peers = [(px, py, pc) for px in (x_, 1 - x_) for py in (y_, 1 - y_) for pc in (c, 1 - c)][1:]
        for k, peer in enumerate(peers):
            sends.append(rc(k, call_ref.at[dev], call_ref.at[dev], peer))
        for cp in sends:
            cp.start()
        wt_copy.wait()
        w4_ref[rows_of(j, 0), :] = wt_ref[0].astype(BF16)
        w4_ref[rows_of(j, 1), :] = wt_ref[1].astype(BF16)

        for k, (px, py, pc) in enumerate(peers):
            pdev = 4 * px + 2 * py + pc
            rc(k, call_ref.at[pdev], call_ref.at[pdev], (px, py, pc)).wait_recv()
        rowid = lax.broadcasted_iota(jnp.int32, (N_DEV, D_MODEL), 0)
        call = jnp.zeros((N_DEV, D_MODEL), F32)
        for r in range(N_DEV):
            call = jnp.where(rowid == r, jnp.broadcast_to(call_ref[r], (N_DEV, D_MODEL)), call)
        bsh = bada_ref[:, 0:ADA_SHARD]
        for k in range(1, N_CHIPS):
            bsh = jnp.where(j == k, bada_ref[:, ADA_SHARD * k:ADA_SHARD * (k + 1)], bsh)
        wada_copy.wait()
        part = jnp.dot(_silu(call).astype(BF16), wada_ref[...].astype(BF16), preferred_element_type=F32) + bsh
        for r in range(N_DEV):
            part_ref[r] = part[r:r + 1, :]
        ada_ref[j] = part_ref[dev]
        for k, chip in enumerate(chips):
            sends.append(rc(13 + k, part_ref.at[2 * idx[k] + c], ada_ref.at[j], (*chip, c)))
            sends[-1].start()
        for k, chip in enumerate(chips):
            sends.append(rc(7 + k, w4_ref.at[rows_of(j, c)], w4_ref.at[rows_of(j, c)], (*chip, c)))
            sends[-1].start()

        x_copy.wait()

        def prenorm(i, carry):
            rr = pl.ds(pl.multiple_of(i * ch, ch), ch)
            xv = x_ref[rr, :]
            r = lax.rsqrt(jnp.mean(xv * xv, axis=-1, keepdims=True) + EPS)
            x_ref[rr, :] = (xv * r) * nw_ref[...]
            return carry

        lax.fori_loop(0, t // ch, prenorm, 0)
        for k in range(3):
            rc(13 + k, ada_ref.at[idx[k]], ada_ref.at[idx[k]], sib).wait_recv()

        shift = jnp.concatenate([ada_ref[0], ada_ref[1][:, 0:256]], axis=1)
        s1 = 1.0 + jnp.concatenate([ada_ref[1][:, 256:768], ada_ref[2][:, 0:512]], axis=1)
        s1_ref[...] = s1
        gate_ref[...] = jnp.concatenate([ada_ref[2][:, 512:768], ada_ref[3]], axis=1)

        def norm(i, carry):
            rr = pl.ds(pl.multiple_of(i * ch, ch), ch)
            h_ref[rr, :] = (x_ref[rr, :] * s1 + shift).astype(BF16)
            return carry

        lax.fori_loop(0, t // ch, norm, 0)
        h_copy = pltpu.make_async_copy(h_ref, h_hbm, lsem.at[1])
        h_copy.start()

        def put_main(case, slot):
            cps, col = [], 0
            for n, (a, c0, w) in enumerate(MAIN_DST[case]):
                cps.append(pltpu.make_async_copy(stg_ref.at[slot, :, pl.ds(col, w)], outs[a].at[:, pl.ds(c0, w)], osem.at[slot, n]))
                col += w
            return cps

        def put_pair(case, slot):
            a, c0, w = PAIR_DST[case]
            return pltpu.make_async_copy(pstg_ref.at[slot], outs[a].at[:, pl.ds(c0, w)], osem.at[slot, 2])

        def project(first_row, width, dst, slot):
            wrows = pl.ds(pl.multiple_of(first_row, 128), width)

            def blk(i, carry):
                rr = pl.ds(pl.multiple_of(i * ch, ch), ch)
                dst[slot, rr, :] = lax.dot_general(h_ref[rr, :], w4_ref[wrows, :], (((1,), (1,)), ((), ())),
                                                   preferred_element_type=F32)
                return carry

            lax.fori_loop(0, t // ch, blk, 0)

        def phase(p, s, pair):
            slot = p % 2
            if p >= 2:
                for case in range(N_CHIPS):
                    @pl.when(order[p - 2] == case)
                    def _():
                        for cp in put_main(case, slot):
                            cp.wait()
            if p == 3:
                for case in range(2):
                    @pl.when(j // 2 == case)
                    def _():
                        put_pair(case, 0).wait()
            project(2 * IN_HALF * s + 64 * (s % 2), MAIN_W, stg_ref, slot)
            for case in range(N_CHIPS):
                @pl.when(s == case)
                def _():
                    for cp in put_main(case, slot):
                        cp.start()
            if pair is not None:
                project(MAIN_W + 2 * (2 * IN_HALF) * pair, 128, pstg_ref, slot % 2 if p == 2 else 1)
                for case in range(2):
                    @pl.when(pair == case)
                    def _():
                        put_pair(case, 0 if p == 2 else 1).start()

        order = [j] + idx
        w_out = [pltpu.make_async_copy(w4_ref.at[pl.ds(pl.multiple_of(2 * IN_HALF * s, 32), 2 * IN_HALF)],
                                       w4_hbm.at[pl.ds(pl.multiple_of(2 * IN_HALF * s, 32), 2 * IN_HALF)], wsem.at[p])
                 for p, s in enumerate(order)]
        w_out[0].start()
        phase(0, j, None)
        passed = []
        for k in range(3):
            jk = idx[k]
            rc(7 + k, w4_ref.at[rows_of(jk, c)], w4_ref.at[rows_of(jk, c)], sib).wait_recv()
            passed.append(rc(10 + k, w4_ref.at[rows_of(jk, c)], w4_ref.at[rows_of(jk, c)], sib))
            passed[-1].start()
            rc(10 + k, w4_ref.at[rows_of(jk, 1 - c)], w4_ref.at[rows_of(jk, 1 - c)], sib).wait_recv()
            w_out[1 + k].start()
            if k == 0:
                phase(1, jk, None)
            elif k == 1:
                phase(2, jk, j // 2)
            else:
                phase(3, jk, 1 - j // 2)

        for case in range(N_CHIPS):
            for p in (2, 3):
                @pl.when(order[p] == case)
                def _():
                    for cp in put_main(case, p % 2):
                        cp.wait()
        for case in range(2):
            @pl.when(1 - j // 2 == case)
            def _():
                put_pair(case, 1).wait()
        h_copy.wait()
        for cp in w_out:
            cp.wait()
        for cp in sends + passed:
            cp.wait_send()

    vm = pl.BlockSpec(memory_space=pltpu.VMEM)
    hbm = pl.BlockSpec(memory_space=pl.ANY)
    widths = (512, 256, 512, 512, 512, 512)
    return pl.pallas_call(
        body,
        name="in_proj",
        in_specs=[hbm, hbm, vm, hbm, vm, vm],
        out_specs=[hbm] * 8 + [vm, vm, vm],
        out_shape=[jax.ShapeDtypeStruct((t, w), F32) for w in widths]
        + [jax.ShapeDtypeStruct((t, D_MODEL), BF16), jax.ShapeDtypeStruct((IN_W, D_MODEL), BF16),
           jax.ShapeDtypeStruct((N_DEV, 1, D_MODEL), F32), jax.ShapeDtypeStruct((1, D_MODEL), F32),
           jax.ShapeDtypeStruct((1, D_MODEL), F32)],
        scratch_shapes=[pltpu.VMEM((N_CHIPS, 1, ADA_SHARD), F32), pltpu.VMEM((t, D_MODEL), F32), pltpu.VMEM((t, D_MODEL), BF16), pltpu.VMEM((IN_W, D_MODEL), BF16),
                        pltpu.VMEM((2, t, MAIN_W), F32), pltpu.VMEM((2, t, 128), F32), pltpu.VMEM((N_DEV, 1, ADA_SHARD), F32),
                        pltpu.VMEM(wt.shape, F32), pltpu.VMEM(w_ada.shape, F32),
                        pltpu.SemaphoreType.DMA((4,)), pltpu.SemaphoreType.DMA((2, 3)), pltpu.SemaphoreType.DMA((N_CHIPS,)),
                        pltpu.SemaphoreType.DMA((n_sem,)), pltpu.SemaphoreType.DMA((n_sem,))],
        compiler_params=_cparams(),
    )(x, wt, c_row, w_ada, b_ada, nw)


def _adamw_math(w, g, m, v):
    m2 = ADAM_B1 * m + (1.0 - ADAM_B1) * g
    v2 = ADAM_B2 * v + (1.0 - ADAM_B2) * (g * g)
    m_hat = m2 / (1.0 - ADAM_B1 ** ADAM_STEP)
    v_hat = v2 / (1.0 - ADAM_B2 ** ADAM_STEP)
    delta = -ADAM_LR * (m_hat / (jnp.sqrt(v_hat) + ADAM_EPS) + ADAM_WD * w)
    return delta, m2, v2


def _adamw(name, groups, nstep, through):
    flat = [a for grp in groups for a in grp]
    n = len(flat)

    def body(*refs):
        ins, through_in, outs, through_out = refs[:n], refs[n], refs[n + 1:2 * n + 1], refs[2 * n + 1]
        for k in range(0, n, 4):
            w_ref, g_ref, m_ref, v_ref = ins[k:k + 4]
            g2_ref, d_ref, m2_ref, v2_ref = outs[k:k + 4]
            g = g_ref[...]
            g2_ref[...] = g
            d_ref[...], m2_ref[...], v2_ref[...] = _adamw_math(w_ref[...], g, m_ref[...], v_ref[...])
        through_out[...] = through_in[...]

    specs = [pl.BlockSpec((a.shape[0] // nstep, a.shape[1]), lambda i: (i, 0)) for a in flat + [through]]
    out = pl.pallas_call(
        body,
        name=name,
        grid=(nstep,),
        in_specs=specs,
        out_specs=specs,
        out_shape=[jax.ShapeDtypeStruct(a.shape, a.dtype) for a in flat + [through]],
        compiler_params=_cparams(dimension_semantics=("arbitrary",)),
    )(*flat, through)
    return [out[k:k + 4] for k in range(0, n, 4)], out[n]


def _adamw_ada(w, m, v, call, rows):
    r, cdim = w.shape
    tm = 256

    def body(w_ref, m_ref, v_ref, c_ref, rows_ref, g_ref, d_ref, m2_ref, v2_ref):
        j = 2 * lax.axis_index("x") + lax.axis_index("y")
        d_ada = jnp.concatenate([jnp.concatenate([rows_ref[d, row:row + 1, :] for d in range(N_DEV)], axis=0)
                                 for row in (2, 3, 0)], axis=1)
        dcols = d_ada[:, 0:cdim]
        for k in range(1, N_CHIPS):
            dcols = jnp.where(j == k, d_ada[:, cdim * k:cdim * (k + 1)], dcols)
        cvec = jnp.concatenate([c_ref[d] for d in range(N_DEV)], axis=0)
        g = lax.dot_general(_silu(cvec).astype(BF16), dcols.astype(BF16), (((0,), (0,)), ((), ())),
                            preferred_element_type=F32)
        g_ref[...] = g
        d_ref[...], m2_ref[...], v2_ref[...] = _adamw_math(w_ref[...], g, m_ref[...], v_ref[...])

    blk = pl.BlockSpec((tm, cdim), lambda i: (i, 0))
    return pl.pallas_call(
        body,
        name="adamw_w_ada",
        grid=(r // tm,),
        in_specs=[blk] * 3 + [pl.BlockSpec((N_DEV, 1, tm), lambda i: (0, 0, i)),
                              pl.BlockSpec((N_DEV, 8, D_MODEL), lambda i: (0, 0, 0))],
        out_specs=[blk] * 4,
        out_shape=[jax.ShapeDtypeStruct((r, cdim), F32)] * 4,
        compiler_params=_cparams(dimension_semantics=("arbitrary",)),
    )(w, m, v, call, rows)


def _adamw_small(ws, ms, vs, ssum, rows):
    n = len(ws)

    def body(*refs):
        w_r, m_r, v_r = refs[0:n], refs[n:2 * n], refs[2 * n:3 * n]
        ss_ref, rows_ref = refs[3 * n], refs[3 * n + 1]
        g_r, d_r, m2_r, v2_r = (refs[3 * n + 2 + k * n:3 * n + 2 + (k + 1) * n] for k in range(4))
        loss_ref = refs[7 * n + 2]
        j = 2 * lax.axis_index("x") + lax.axis_index("y")
        rsum = rows_ref[0]
        for d in range(1, N_DEV):
            rsum = rsum + rows_ref[d]
        taps = []
        for t in range(CONV_TAPS):
            row = ss_ref[t // 2:t // 2 + 1, :]
            c0 = CONV_W * (t % 2)
            pick = row[:, c0:c0 + 128]
            for k in range(1, N_CHIPS):
                pick = jnp.where(j == k, row[:, c0 + 128 * k:c0 + 128 * (k + 1)], pick)
            taps.append(pick)
        grads = [jnp.concatenate([rsum[2:3], rsum[3:4], rsum[0:1]], axis=1), rsum[4:5],
                 ss_ref[17:18, 512:512 + HEAD_DIM], ss_ref[17:18, 640:640 + HEAD_DIM], ss_ref[17:18, 768:776],
                 None, ss_ref[16:17, 0:CONV_W], ss_ref[16:17, CONV_W:2 * CONV_W], ss_ref[17:18, 0:CONV_W]]
        for i in range(n):
            if grads[i] is None:
                g = jnp.concatenate(taps, axis=0)
                w, m, v = (jnp.concatenate([ref[t] for t in range(CONV_TAPS)], axis=0) for ref in (w_r[i], m_r[i], v_r[i]))
                res = (g,) + _adamw_math(w, g, m, v)
                for ref, val in zip((g_r[i], d_r[i], m2_r[i], v2_r[i]), res):
                    for t in range(CONV_TAPS):
                        ref[t] = val[t:t + 1, :]
                continue
            g = grads[i]
            g_r[i][...] = g
            d_r[i][...], m2_r[i][...], v2_r[i][...] = _adamw_math(w_r[i][...], g, m_r[i][...], v_r[i][...])
        loss_ref[...] = (0.5 / D_MODEL) * jnp.sum(ss_ref[18:19, :], axis=1, keepdims=True)

    vm = pl.BlockSpec(memory_space=pltpu.VMEM)
    shapes = [jax.ShapeDtypeStruct(w.shape, F32) for w in ws]
    out = pl.pallas_call(
        body,
        name="adamw_small",
        in_specs=[vm] * (3 * n + 2),
        out_specs=[vm] * (4 * n + 1),
        out_shape=shapes * 4 + [jax.ShapeDtypeStruct((1, 1), F32)],
        compiler_params=_cparams(),
    )(*ws, *ms, *vs, ssum, rows)
    return out[0:n], out[n:2 * n], out[2 * n:3 * n], out[3 * n:4 * n], out[4 * n]


def _rope_tables(t):
    inv = ROPE_THETA ** (-jnp.arange(0, HEAD_DIM, 2, dtype=F32) / HEAD_DIM)
    ang = jnp.arange(t, dtype=F32)[:, None] * inv[None, :]
    cos, sin = jnp.cos(ang), jnp.sin(ang)
    return jnp.tile(cos, (1, 4)), jnp.tile(jnp.concatenate([-sin, sin], axis=1), (1, 2))


def kernel(x, c, w_ada, b_ada, norm_w, w_in, q_norm_w, k_norm_w, sinks, conv_w, conv_b, ln_w, ln_b, w_out, loss_target, m_w_ada, m_b_ada, m_norm_w, m_w_in, m_q_norm_w, m_k_norm_w, m_sinks, m_conv_w, m_conv_b, m_ln_w, m_ln_b, m_w_out, v_w_ada, v_b_ada, v_norm_w, v_w_in, v_q_norm_w, v_k_norm_w, v_sinks, v_conv_w, v_conv_b, v_ln_w, v_ln_b, v_w_out):
    xi, yi = lax.axis_index("x"), lax.axis_index("y")
    j = 2 * xi + yi
    x2, tgt = x[0], loss_target[0]
    t = x2.shape[0]

    wt_s, mt_s, vt_s = w_in[0].T, m_w_in[0].T, v_w_in[0].T
    by_tap = lambda a: jnp.transpose(a, (1, 0, 2))

    q_raw, kv_raw, ga, ua, ug, gb, h, w_full, call, s1, gate = _in_proj_gather(
        x2, wt_s.reshape(2, IN_HALF, D_MODEL), c, w_ada[0], b_ada, norm_w)

    cos_f, sin_s = _rope_tables(t)
    qkw2 = jnp.tile(jnp.concatenate([q_norm_w, k_norm_w], axis=0), (1, 2))

    o, mix_a, wo4, cw4 = _attn_fwd(q_raw, kv_raw, ga, qkw2, sinks, cos_f, sin_s,
                                   w_out[0].reshape(2, OUT_HALF, D_MODEL), by_tap(conv_w))
    w_out_full = wo4.reshape(D_MODEL, D_MODEL)
    cz, mix_b = _conv_fwd(ua, ug, gb, cw4, conv_b, ln_w, ln_b)
    dout, dmix_a, dmix_b, gwo_bf, red_o = _out_proj(mix_a, mix_b, x2, tgt, gate, w_out_full)

    dq, dkv, dga, sm_a, gwo = _attn_bwd(q_raw, kv_raw, ga, o, dmix_a, qkw2, sinks, cos_f, sin_s,
                                        gwo_bf.reshape(N_CHIPS, 2, OUT_HALF, D_MODEL))
    dua, dug, dgb, dcw, dvec = _conv_bwd(ua, ug, gb, cz, dmix_b, cw4, ln_w, ln_b)
    dparts = (dq, dkv, dga, dua, dug, dgb)

    grad_x, gw, ssum, rows = _in_proj_bwd(dparts, h, x2, dout, s1, norm_w, w_full, dcw, dvec, sm_a, red_o)

    gt_w_in = gw.reshape(2 * IN_HALF, D_MODEL)
    g_w_out = gwo.reshape(D_MODEL // N_CHIPS, D_MODEL)

    g_w_ada, d_w_ada, nm_w_ada, nv_w_ada = _adamw_ada(w_ada[0], m_w_ada[0], v_w_ada[0], call, rows)
    (in_res, out_res), grad_x = _adamw("adamw_w", [(wt_s, gt_w_in, mt_s, vt_s), (w_out[0], g_w_out, m_w_out[0], v_w_out[0])],
                                        4, grad_x)
    gt_w_in, dt_w_in, nmt_w_in, nvt_w_in = in_res
    g_w_out, d_w_out, nm_w_out, nv_w_out = out_res
    g_w_in, d_w_in, nm_w_in, nv_w_in = gt_w_in.T, dt_w_in.T, nmt_w_in.T, nvt_w_in.T
    ws = [b_ada, norm_w, q_norm_w, k_norm_w, sinks, by_tap(conv_w), conv_b, ln_w, ln_b]
    ms = [m_b_ada, m_norm_w, m_q_norm_w, m_k_norm_w, m_sinks, by_tap(m_conv_w), m_conv_b, m_ln_w, m_ln_b]
    vs = [v_b_ada, v_norm_w, v_q_norm_w, v_k_norm_w, v_sinks, by_tap(v_conv_w), v_conv_b, v_ln_w, v_ln_b]
    gs, ds, nms, nvs, loss11 = _adamw_small(ws, ms, vs, ssum, rows)
    loss = loss11[0, 0]

    def order(ada_v, in_v, out_v, sm):
        b, nw_, qw_, kw_, sk_, cw_, cb_, lw_, lb_ = sm
        return [ada_v[None], b, nw_, in_v[None], qw_, kw_, sk_, by_tap(cw_), cb_, lw_, lb_, out_v[None]]

    grads = order(g_w_ada, g_w_in, g_w_out, gs)
    deltas = order(d_w_ada, d_w_in, d_w_out, ds)
    new_m = order(nm_w_ada, nm_w_in, nm_w_out, nms)
    new_v = order(nv_w_ada, nv_w_in, nv_w_out, nvs)
    return (loss, grad_x[None], *grads, *deltas, *new_m, *new_v)
```
